```python
import math
import jax, jax.numpy as jnp
from jax import lax
import numpy as np

D_MODEL = 1024
BATCH = 8
SEQ = 4096
DEPTH = 1

HEAD_DIM = 64
N_Q_HEADS = 8
N_KV_HEADS = 2
GQA_GROUP = N_Q_HEADS // N_KV_HEADS
ATTN_WIDTH = N_Q_HEADS * HEAD_DIM
KV_WIDTH = N_KV_HEADS * HEAD_DIM
WINDOW = 128
BLOCK = 128
POOL_WIDTH = D_MODEL - ATTN_WIDTH
POOL_SIZES = (2, 4, 8, 16)
N_POOL_GROUPS = len(POOL_SIZES)
POOL_GROUP_DIM = POOL_WIDTH // N_POOL_GROUPS
IN_WIDTH = ATTN_WIDTH + 2 * KV_WIDTH + POOL_WIDTH
MIX_WIDTH = ATTN_WIDTH + POOL_WIDTH
N_BUCKETS = 32
MAX_DISTANCE = 128
D_FF = -(-8 * D_MODEL // (3 * 256)) * 256
PLE_DIM = 256
EPS = 1e-6

kernel_name = "hymba_swa_sink_pool_hybrid"


def rmsnorm(x, g):
    xf = x.astype(jnp.float32)
    y = xf * lax.rsqrt(jnp.mean(xf * xf, axis=-1, keepdims=True) + EPS)
    return (y * g.astype(jnp.float32)).astype(x.dtype)


def t5_causal_bucket(dist):
    n = np.maximum(dist, 0)
    max_exact = N_BUCKETS // 2
    nf = np.maximum(n, 1).astype(np.float64)
    large = max_exact + (np.log(nf / max_exact) / math.log(MAX_DISTANCE / max_exact)
                         * (N_BUCKETS - max_exact)).astype(np.int64)
    large = np.minimum(large, N_BUCKETS - 1)
    return np.where(n < max_exact, n, large).astype(np.int32)


def sliding_window_sink_attention(q, k, v, sinks, rel_bias):
    B, S = q.shape[0], q.shape[1]
    nb = S // BLOCK
    qb = q.reshape(B, nb, BLOCK, N_KV_HEADS, GQA_GROUP, HEAD_DIM)
    kb = k.reshape(B, nb, BLOCK, N_KV_HEADS, HEAD_DIM)
    vb = v.reshape(B, nb, BLOCK, N_KV_HEADS, HEAD_DIM)
    pad = ((0, 0), (1, 0), (0, 0), (0, 0), (0, 0))
    k2 = jnp.concatenate([jnp.pad(kb, pad)[:, :-1], kb], axis=2)
    v2 = jnp.concatenate([jnp.pad(vb, pad)[:, :-1], vb], axis=2)

    logits = jnp.einsum('bnqhgd,bnkhd->bnhgqk', qb, k2).astype(jnp.float32) * (HEAD_DIM ** -0.5)

    i_idx = np.arange(BLOCK)[:, None]
    j_idx = np.arange(2 * BLOCK)[None, :]
    d = BLOCK + i_idx - j_idx
    bucket = t5_causal_bucket(d)
    bias = rel_bias.astype(jnp.float32)[bucket]
    bias = jnp.transpose(bias, (2, 0, 1)).reshape(N_KV_HEADS, GQA_GROUP, BLOCK, 2 * BLOCK)

    rel_ok = (d >= 0) & (d < WINDOW)
    blk_ok = (np.arange(nb)[:, None, None] > 0) | (j_idx[None] >= BLOCK)
    mask = jnp.asarray(rel_ok[None] & blk_ok)[None, :, None, None]

    logits = jnp.where(mask, logits + bias, -jnp.inf)
    sink = sinks.astype(jnp.float32).reshape(N_KV_HEADS, GQA_GROUP)[:, :, None, None]
    m = jnp.maximum(jnp.max(logits, axis=-1, keepdims=True), sink)
    pexp = jnp.exp(logits - m)
    probs = pexp / (jnp.sum(pexp, axis=-1, keepdims=True) + jnp.exp(sink - m))
    out = jnp.einsum('bnhgqk,bnkhd->bnqhgd', probs.astype(v.dtype), v2)
    return out.reshape(B, S, N_Q_HEADS * HEAD_DIM)


def multiscale_pool_mixer(u, w_pool, pool_scale):
    B, S = u.shape[0], u.shape[1]
    ug = u.reshape(B, S, N_POOL_GROUPS, POOL_GROUP_DIM).astype(jnp.float32)
    cs = jnp.cumsum(ug, axis=1)
    t1 = jnp.arange(1, S + 1, dtype=jnp.float32)
    outs = []
    for g, w in enumerate(POOL_SIZES):
        c = cs[:, :, g]
        lag = jnp.pad(c, ((0, 0), (w, 0), (0, 0)))[:, :S]
        cnt = jnp.minimum(t1, float(w))[None, :, None]
        outs.append((c - lag) / cnt - ug[:, :, g])
    pooled = jnp.stack(outs, axis=2)
    y = jnp.einsum('bsgc,gcd->bsgd', pooled, w_pool.astype(jnp.float32))
    y = y.reshape(B, S, POOL_WIDTH) * pool_scale.astype(jnp.float32)
    return y.astype(u.dtype)


def _fwd_setup_inputs(seed: int = 0) -> dict:
    key = jax.random.key(seed)
    ks = jax.random.split(key, 20)
    f32 = jnp.float32
    nrm = lambda k, shape, s: jax.random.normal(k, shape, f32) * s
    return {
        "x": nrm(ks[0], (BATCH, SEQ, D_MODEL), 1.0),
        "p": nrm(ks[1], (DEPTH, BATCH, SEQ, PLE_DIM), 1.0),
        "w_in": nrm(ks[2], (DEPTH, D_MODEL, IN_WIDTH), D_MODEL ** -0.5),
        "w_out": nrm(ks[3], (DEPTH, MIX_WIDTH, D_MODEL), MIX_WIDTH ** -0.5),
        "g_attn_norm": 1.0 + nrm(ks[4], (DEPTH, D_MODEL), 0.05),
        "g_q": 1.0 + nrm(ks[5], (DEPTH, HEAD_DIM), 0.05),
        "g_k": 1.0 + nrm(ks[6], (DEPTH, HEAD_DIM), 0.05),
        "attn_sinks": nrm(ks[7], (DEPTH, N_Q_HEADS), 1.0),
        "rel_bias": nrm(ks[8], (N_BUCKETS, N_Q_HEADS), 0.3),
        "w_pool": nrm(ks[9], (DEPTH, N_POOL_GROUPS, POOL_GROUP_DIM, POOL_GROUP_DIM), POOL_GROUP_DIM ** -0.5),
        "pool_scale": 1.0 + nrm(ks[10], (DEPTH, POOL_WIDTH), 0.1),
        "g_ffn_norm": 1.0 + nrm(ks[11], (DEPTH, D_MODEL), 0.05),
        "w_gate": nrm(ks[12], (DEPTH, D_MODEL, D_FF), D_MODEL ** -0.5),
        "w_up": nrm(ks[13], (DEPTH, D_MODEL, D_FF), D_MODEL ** -0.5),
        "w_down": nrm(ks[14], (DEPTH, D_FF, D_MODEL), D_FF ** -0.5),
        "g_ple_norm": 1.0 + nrm(ks[15], (DEPTH, D_MODEL), 0.05),
        "w_ple_gate": nrm(ks[16], (DEPTH, D_MODEL, D_MODEL), D_MODEL ** -0.5),
        "w_ple_proj": nrm(ks[17], (DEPTH, PLE_DIM, D_MODEL), PLE_DIM ** -0.5),
    }


def _fwd_reference(x, p, w_in, w_out, g_attn_norm, g_q, g_k, attn_sinks, rel_bias, w_pool, pool_scale,
              g_ffn_norm, w_gate, w_up, w_down, g_ple_norm, w_ple_gate, w_ple_proj):
    B, S = x.shape[0], x.shape[1]
    h = x
    for i in range(DEPTH):
        hn = rmsnorm(h, g_attn_norm[i])
        z = hn @ w_in[i]
        q = z[..., :ATTN_WIDTH]
        k = z[..., ATTN_WIDTH:ATTN_WIDTH + KV_WIDTH]
        v = z[..., ATTN_WIDTH + KV_WIDTH:ATTN_WIDTH + 2 * KV_WIDTH]
        u = z[..., ATTN_WIDTH + 2 * KV_WIDTH:]
        q = rmsnorm(q.reshape(B, S, N_Q_HEADS, HEAD_DIM), g_q[i])
        k = rmsnorm(k.reshape(B, S, N_KV_HEADS, HEAD_DIM), g_k[i])
        v = v.reshape(B, S, N_KV_HEADS, HEAD_DIM)
        a = sliding_window_sink_attention(q, k, v, attn_sinks[i], rel_bias)
        m = multiscale_pool_mixer(u, w_pool[i], pool_scale[i])
        h = h + jnp.concatenate([a, m], axis=-1) @ w_out[i]
        hn = rmsnorm(h, g_ffn_norm[i])
        h = h + (jax.nn.silu(hn @ w_gate[i]) * (hn @ w_up[i])) @ w_down[i]
        gate = jax.nn.sigmoid(rmsnorm(h, g_ple_norm[i]) @ w_ple_gate[i])
        h = h + gate * (p[i] @ w_ple_proj[i])
    return h


import jax as _jax
import jax.numpy as _jnp

TWIN_FORMAT = 'train_step'
FWD_PARAMS = ['x', 'p', 'w_in', 'w_out', 'g_attn_norm', 'g_q', 'g_k', 'attn_sinks', 'rel_bias', 'w_pool', 'pool_scale', 'g_ffn_norm', 'w_gate', 'w_up', 'w_down', 'g_ple_norm', 'w_ple_gate', 'w_ple_proj']
TWIN_WEIGHTS = ['w_in', 'w_out', 'g_attn_norm', 'g_q', 'g_k', 'attn_sinks', 'rel_bias', 'w_pool', 'pool_scale', 'g_ffn_norm', 'w_gate', 'w_up', 'w_down', 'g_ple_norm', 'w_ple_gate', 'w_ple_proj']
TWIN_DIFF_INPUT = 'x'
TWIN_INPUTS = ['x', 'p', 'w_in', 'w_out', 'g_attn_norm', 'g_q', 'g_k', 'attn_sinks', 'rel_bias', 'w_pool', 'pool_scale', 'g_ffn_norm', 'w_gate', 'w_up', 'w_down', 'g_ple_norm', 'w_ple_gate', 'w_ple_proj', 'loss_target', 'm_w_in', 'm_w_out', 'm_g_attn_norm', 'm_g_q', 'm_g_k', 'm_attn_sinks', 'm_rel_bias', 'm_w_pool', 'm_pool_scale', 'm_g_ffn_norm', 'm_w_gate', 'm_w_up', 'm_w_down', 'm_g_ple_norm', 'm_w_ple_gate', 'm_w_ple_proj', 'v_w_in', 'v_w_out', 'v_g_attn_norm', 'v_g_q', 'v_g_k', 'v_attn_sinks', 'v_rel_bias', 'v_w_pool', 'v_pool_scale', 'v_g_ffn_norm', 'v_w_gate', 'v_w_up', 'v_w_down', 'v_g_ple_norm', 'v_w_ple_gate', 'v_w_ple_proj']
TWIN_OUTPUTS = ['loss', 'grad_x', 'grad_w_in', 'grad_w_out', 'grad_g_attn_norm', 'grad_g_q', 'grad_g_k', 'grad_attn_sinks', 'grad_rel_bias', 'grad_w_pool', 'grad_pool_scale', 'grad_g_ffn_norm', 'grad_w_gate', 'grad_w_up', 'grad_w_down', 'grad_g_ple_norm', 'grad_w_ple_gate', 'grad_w_ple_proj', 'delta_w_in', 'delta_w_out', 'delta_g_attn_norm', 'delta_g_q', 'delta_g_k', 'delta_attn_sinks', 'delta_rel_bias', 'delta_w_pool', 'delta_pool_scale', 'delta_g_ffn_norm', 'delta_w_gate', 'delta_w_up', 'delta_w_down', 'delta_g_ple_norm', 'delta_w_ple_gate', 'delta_w_ple_proj', 'new_m_w_in', 'new_m_w_out', 'new_m_g_attn_norm', 'new_m_g_q', 'new_m_g_k', 'new_m_attn_sinks', 'new_m_rel_bias', 'new_m_w_pool', 'new_m_pool_scale', 'new_m_g_ffn_norm', 'new_m_w_gate', 'new_m_w_up', 'new_m_w_down', 'new_m_g_ple_norm', 'new_m_w_ple_gate', 'new_m_w_ple_proj', 'new_v_w_in', 'new_v_w_out', 'new_v_g_attn_norm', 'new_v_g_q', 'new_v_g_k', 'new_v_attn_sinks', 'new_v_rel_bias', 'new_v_w_pool', 'new_v_pool_scale', 'new_v_g_ffn_norm', 'new_v_w_gate', 'new_v_w_up', 'new_v_w_down', 'new_v_g_ple_norm', 'new_v_w_ple_gate', 'new_v_w_ple_proj']
TWIN_LEAF_KINDS = {'loss': 'loss', 'grad_x': 'grad_x', 'grad_w_in': 'grad_w', 'grad_w_out': 'grad_w', 'grad_g_attn_norm': 'grad_w', 'grad_g_q': 'grad_w', 'grad_g_k': 'grad_w', 'grad_attn_sinks': 'grad_w', 'grad_rel_bias': 'grad_w', 'grad_w_pool': 'grad_w', 'grad_pool_scale': 'grad_w', 'grad_g_ffn_norm': 'grad_w', 'grad_w_gate': 'grad_w', 'grad_w_up': 'grad_w', 'grad_w_down': 'grad_w', 'grad_g_ple_norm': 'grad_w', 'grad_w_ple_gate': 'grad_w', 'grad_w_ple_proj': 'grad_w', 'delta_w_in': 'delta_w', 'delta_w_out': 'delta_w', 'delta_g_attn_norm': 'delta_w', 'delta_g_q': 'delta_w', 'delta_g_k': 'delta_w', 'delta_attn_sinks': 'delta_w', 'delta_rel_bias': 'delta_w', 'delta_w_pool': 'delta_w', 'delta_pool_scale': 'delta_w', 'delta_g_ffn_norm': 'delta_w', 'delta_w_gate': 'delta_w', 'delta_w_up': 'delta_w', 'delta_w_down': 'delta_w', 'delta_g_ple_norm': 'delta_w', 'delta_w_ple_gate': 'delta_w', 'delta_w_ple_proj': 'delta_w', 'new_m_w_in': 'new_m', 'new_m_w_out': 'new_m', 'new_m_g_attn_norm': 'new_m', 'new_m_g_q': 'new_m', 'new_m_g_k': 'new_m', 'new_m_attn_sinks': 'new_m', 'new_m_rel_bias': 'new_m', 'new_m_w_pool': 'new_m', 'new_m_pool_scale': 'new_m', 'new_m_g_ffn_norm': 'new_m', 'new_m_w_gate': 'new_m', 'new_m_w_up': 'new_m', 'new_m_w_down': 'new_m', 'new_m_g_ple_norm': 'new_m', 'new_m_w_ple_gate': 'new_m', 'new_m_w_ple_proj': 'new_m', 'new_v_w_in': 'new_v', 'new_v_w_out': 'new_v', 'new_v_g_attn_norm': 'new_v', 'new_v_g_q': 'new_v', 'new_v_g_k': 'new_v', 'new_v_attn_sinks': 'new_v', 'new_v_rel_bias': 'new_v', 'new_v_w_pool': 'new_v', 'new_v_pool_scale': 'new_v', 'new_v_g_ffn_norm': 'new_v', 'new_v_w_gate': 'new_v', 'new_v_w_up': 'new_v', 'new_v_w_down': 'new_v', 'new_v_g_ple_norm': 'new_v', 'new_v_w_ple_gate': 'new_v', 'new_v_w_ple_proj': 'new_v'}


def _forward(args):
    return _fwd_reference(*[args[k] for k in FWD_PARAMS])


def _output_shape():
    def fwd():
        inp = _fwd_setup_inputs(0)
        return _fwd_reference(*[inp[k] for k in FWD_PARAMS])
    out = _jax.eval_shape(fwd)
    return out.shape, out.dtype

N_MICROBATCH = 1
ADAM_LR = 0.001
ADAM_B1 = 0.9
ADAM_B2 = 0.999
ADAM_EPS = 1e-08
ADAM_WD = 0.01
ADAM_STEP = 10
PER_EXAMPLE_BATCH_AXIS = {'x': 0, 'p': 1, 'loss_target': 0}
SHARED_INPUTS = []
_WEIGHT_DTYPES = {'w_in': _jnp.float32, 'w_out': _jnp.float32, 'g_attn_norm': _jnp.float32, 'g_q': _jnp.float32, 'g_k': _jnp.float32, 'attn_sinks': _jnp.float32, 'rel_bias': _jnp.float32, 'w_pool': _jnp.float32, 'pool_scale': _jnp.float32, 'g_ffn_norm': _jnp.float32, 'w_gate': _jnp.float32, 'w_up': _jnp.float32, 'w_down': _jnp.float32, 'g_ple_norm': _jnp.float32, 'w_ple_gate': _jnp.float32, 'w_ple_proj': _jnp.float32}
MOMENT_SCALE = {'w_in': 8.240830e-01, 'w_out': 9.703458e-01, 'g_attn_norm': 1.259949e+01, 'g_q': 3.754239e+00, 'g_k': 3.797104e+00, 'attn_sinks': 8.725787e-01, 'rel_bias': 4.067899e-01, 'w_pool': 2.588794e+00, 'pool_scale': 2.522482e+01, 'g_ffn_norm': 2.493619e+01, 'w_gate': 1.993139e-01, 'w_up': 2.355658e-01, 'w_down': 3.511054e-01, 'g_ple_norm': 9.664116e-01, 'w_ple_gate': 8.341017e-02, 'w_ple_proj': 5.116382e-01}


def _to_microbatches(a, axis):
    t = _jnp.moveaxis(a, axis, 0)
    t = t.reshape((N_MICROBATCH, t.shape[0] // N_MICROBATCH) + t.shape[1:])
    return _jnp.moveaxis(t, 1, axis + 1)


def setup_inputs(seed: int = 0) -> dict:
    inp = _fwd_setup_inputs(seed)
    key = _jax.random.fold_in(_jax.random.key(seed), 7919)
    shape, _ = _output_shape()
    out = dict(inp)
    out["loss_target"] = _jax.random.normal(_jax.random.fold_in(key, 0), shape, _jnp.float32)
    for i, name in enumerate(TWIN_WEIGHTS):
        w = inp[name].astype(_jnp.float32)
        if MOMENT_SCALE is None:
            s = _jnp.sqrt(_jnp.mean(_jnp.square(w)) + 1e-30)
        else:
            s = MOMENT_SCALE[name]
        km, kv = _jax.random.split(_jax.random.fold_in(key, i + 1))
        out[name] = w
        out["m_" + name] = s * _jax.random.normal(km, w.shape, _jnp.float32)
        out["v_" + name] = (s * s) * _jax.random.uniform(kv, w.shape, _jnp.float32, 0.5, 1.5)
    if N_MICROBATCH > 1:
        for name, axis in PER_EXAMPLE_BATCH_AXIS.items():
            out[name] = _to_microbatches(out[name], axis)
    return {'x': out['x'], 'p': out['p'], 'w_in': out['w_in'], 'w_out': out['w_out'], 'g_attn_norm': out['g_attn_norm'], 'g_q': out['g_q'], 'g_k': out['g_k'], 'attn_sinks': out['attn_sinks'], 'rel_bias': out['rel_bias'], 'w_pool': out['w_pool'], 'pool_scale': out['pool_scale'], 'g_ffn_norm': out['g_ffn_norm'], 'w_gate': out['w_gate'], 'w_up': out['w_up'], 'w_down': out['w_down'], 'g_ple_norm': out['g_ple_norm'], 'w_ple_gate': out['w_ple_gate'], 'w_ple_proj': out['w_ple_proj'], 'loss_target': out['loss_target'], 'm_w_in': out['m_w_in'], 'm_w_out': out['m_w_out'], 'm_g_attn_norm': out['m_g_attn_norm'], 'm_g_q': out['m_g_q'], 'm_g_k': out['m_g_k'], 'm_attn_sinks': out['m_attn_sinks'], 'm_rel_bias': out['m_rel_bias'], 'm_w_pool': out['m_w_pool'], 'm_pool_scale': out['m_pool_scale'], 'm_g_ffn_norm': out['m_g_ffn_norm'], 'm_w_gate': out['m_w_gate'], 'm_w_up': out['m_w_up'], 'm_w_down': out['m_w_down'], 'm_g_ple_norm': out['m_g_ple_norm'], 'm_w_ple_gate': out['m_w_ple_gate'], 'm_w_ple_proj': out['m_w_ple_proj'], 'v_w_in': out['v_w_in'], 'v_w_out': out['v_w_out'], 'v_g_attn_norm': out['v_g_attn_norm'], 'v_g_q': out['v_g_q'], 'v_g_k': out['v_g_k'], 'v_attn_sinks': out['v_attn_sinks'], 'v_rel_bias': out['v_rel_bias'], 'v_w_pool': out['v_w_pool'], 'v_pool_scale': out['v_pool_scale'], 'v_g_ffn_norm': out['v_g_ffn_norm'], 'v_w_gate': out['v_w_gate'], 'v_w_up': out['v_w_up'], 'v_w_down': out['v_w_down'], 'v_g_ple_norm': out['v_g_ple_norm'], 'v_w_ple_gate': out['v_w_ple_gate'], 'v_w_ple_proj': out['v_w_ple_proj']}


def _loss(weights, diff, rest, loss_target):
    with _jax.named_scope("forward"):
        args = {**rest, TWIN_DIFF_INPUT: diff, **{k: w.astype(_WEIGHT_DTYPES[k]) for k, w in weights.items()}}
        y = _forward(args)
    with _jax.named_scope("loss_head"):
        err = _jnp.square(y.astype(_jnp.float32) - loss_target)
        return 0.5 * _jnp.sum(_jnp.mean(err, axis=-1)) if err.ndim else 0.5 * err


def _adamw(w, g, m, v):
    m = ADAM_B1 * m + (1.0 - ADAM_B1) * g
    v = ADAM_B2 * v + (1.0 - ADAM_B2) * _jnp.square(g)
    m_hat = m / (1.0 - ADAM_B1 ** ADAM_STEP)
    v_hat = v / (1.0 - ADAM_B2 ** ADAM_STEP)
    delta = -ADAM_LR * (m_hat / (_jnp.sqrt(v_hat) + ADAM_EPS) + ADAM_WD * w)
    return delta, m, v


def reference(x, p, w_in, w_out, g_attn_norm, g_q, g_k, attn_sinks, rel_bias, w_pool, pool_scale, g_ffn_norm, w_gate, w_up, w_down, g_ple_norm, w_ple_gate, w_ple_proj, loss_target, m_w_in, m_w_out, m_g_attn_norm, m_g_q, m_g_k, m_attn_sinks, m_rel_bias, m_w_pool, m_pool_scale, m_g_ffn_norm, m_w_gate, m_w_up, m_w_down, m_g_ple_norm, m_w_ple_gate, m_w_ple_proj, v_w_in, v_w_out, v_g_attn_norm, v_g_q, v_g_k, v_attn_sinks, v_rel_bias, v_w_pool, v_pool_scale, v_g_ffn_norm, v_w_gate, v_w_up, v_w_down, v_g_ple_norm, v_w_ple_gate, v_w_ple_proj):
    given = dict(x=x, p=p, w_in=w_in, w_out=w_out, g_attn_norm=g_attn_norm, g_q=g_q, g_k=g_k, attn_sinks=attn_sinks, rel_bias=rel_bias, w_pool=w_pool, pool_scale=pool_scale, g_ffn_norm=g_ffn_norm, w_gate=w_gate, w_up=w_up, w_down=w_down, g_ple_norm=g_ple_norm, w_ple_gate=w_ple_gate, w_ple_proj=w_ple_proj, loss_target=loss_target, m_w_in=m_w_in, m_w_out=m_w_out, m_g_attn_norm=m_g_attn_norm, m_g_q=m_g_q, m_g_k=m_g_k, m_attn_sinks=m_attn_sinks, m_rel_bias=m_rel_bias, m_w_pool=m_w_pool, m_pool_scale=m_pool_scale, m_g_ffn_norm=m_g_ffn_norm, m_w_gate=m_w_gate, m_w_up=m_w_up, m_w_down=m_w_down, m_g_ple_norm=m_g_ple_norm, m_w_ple_gate=m_w_ple_gate, m_w_ple_proj=m_w_ple_proj, v_w_in=v_w_in, v_w_out=v_w_out, v_g_attn_norm=v_g_attn_norm, v_g_q=v_g_q, v_g_k=v_g_k, v_attn_sinks=v_attn_sinks, v_rel_bias=v_rel_bias, v_w_pool=v_w_pool, v_pool_scale=v_pool_scale, v_g_ffn_norm=v_g_ffn_norm, v_w_gate=v_w_gate, v_w_up=v_w_up, v_w_down=v_w_down, v_g_ple_norm=v_g_ple_norm, v_w_ple_gate=v_w_ple_gate, v_w_ple_proj=v_w_ple_proj)
    weights = {n: given[n] for n in TWIN_WEIGHTS}
    shared = {n: given[n] for n in SHARED_INPUTS}
    per_example = {n: given[n] for n in ['x', 'p']}
    grad_fn = _jax.value_and_grad(_loss, argnums=(0, 1))

    def one_microbatch(ex, loss_target):
        ex = dict(ex)
        diff = ex.pop(TWIN_DIFF_INPUT)
        return grad_fn(weights, diff, {**shared, **ex}, loss_target)

    if N_MICROBATCH == 1:
        loss, (grad_w, grad_x) = one_microbatch(per_example, given["loss_target"])
    else:
        def body(carry, xs):
            loss_sum, grad_sum = carry
            l_k, (gw_k, gx_k) = one_microbatch(xs[0], xs[1])
            with _jax.named_scope("update"):
                return (loss_sum + l_k, _jax.tree.map(_jnp.add, grad_sum, gw_k)), gx_k

        init = (_jnp.zeros((), _jnp.float32), _jax.tree.map(_jnp.zeros_like, weights))
        (loss, grad_w), grad_x = _jax.lax.scan(body, init, (per_example, given["loss_target"]))
    with _jax.named_scope("update"):
        delta_w, new_m, new_v = {}, {}, {}
        for n in TWIN_WEIGHTS:
            delta_w[n], new_m[n], new_v[n] = _adamw(weights[n], grad_w[n], given["m_" + n], given["v_" + n])
    return (loss, grad_x, *[grad_w[n] for n in TWIN_WEIGHTS], *[delta_w[n] for n in TWIN_WEIGHTS],
            *[new_m[n] for n in TWIN_WEIGHTS], *[new_v[n] for n in TWIN_WEIGHTS])
```

```python
import functools

import numpy as np
import jax
import jax.numpy as jnp
from jax import lax
from jax.experimental import pallas as pl
from jax.experimental.pallas import tpu as pltpu

F32 = jnp.float32
BF16 = jnp.bfloat16
MESH = pl.DeviceIdType.MESH

D_MODEL = 1024
HEAD_DIM = 64
N_Q_HEADS = 8
ATTN_WIDTH = 512
KV_WIDTH = 128
POOL_WIDTH = 512
IN_WIDTH = 1280
D_FF = 2816
PLE_DIM = 256
FF_CHUNK = 1408
BLOCK = 128
N_BUCKETS = 32
MAX_DISTANCE = 128
POOL_SIZES = (2, 4, 8, 16)
EPS = 1e-6
NEG = -1e30
N_CHIPS = 4
N_DEV = 8

ADAM_LR = 0.001
ADAM_B1 = 0.9
ADAM_B2 = 0.999
ADAM_EPS = 1e-08
ADAM_WD = 0.01
ADAM_STEP = 10

SLAB = {"inT": (0, 320), "out": (320, 256), "gateT": (576, 704), "upT": (1280, 704), "down": (1984, 704),
        "plg": (2688, 256), "plpT": (2944, 64)}
SLAB_ROWS = 3008
HALF_ROWS = SLAB_ROWS // 2
POOL_HALO = 24

SMALL = {"g_attn": (0, 8), "g_ffn": (8, 8), "g_ple": (16, 8), "pool_scale": (24, 4), "g_q": (28, 1), "g_k": (29, 1),
         "sinks": (30, 1), "rel_bias": (31, 2), "loss": (33, 1), "w_pool": (40, 512)}
SMALL_ROWS = 552

VMEM_LIMIT_BIG = 60 * 1024 * 1024
VMEM_LIMIT = 48 * 1024 * 1024


def _params(vmem=VMEM_LIMIT, n_axes=1):
    return pltpu.CompilerParams(dimension_semantics=("arbitrary",) * n_axes, vmem_limit_bytes=vmem)


def _dot(a, b, ca, cb):
    return lax.dot_general(a, b, (((ca,), (cb,)), ((), ())), preferred_element_type=F32)


def _full(shape):
    return pl.BlockSpec(shape, lambda i: (0,) * len(shape))


ANY = pl.BlockSpec(memory_space=pl.ANY)


def _load_rows(slab_ref, name, dst_ref, sems):
    off, rows = SLAB[name]
    copies = [pltpu.make_async_copy(slab_ref.at[j, pl.ds(off, rows), :], dst_ref.at[pl.ds(j * rows, rows), :], sems.at[j])
              for j in range(N_CHIPS)]
    for cp in copies:
        cp.start()
    for cp in copies:
        cp.wait()


def _rms_fwd(x, g):
    r = lax.rsqrt(jnp.mean(x * x, axis=-1, keepdims=True) + EPS)
    return x * r * g


def _rms_bwd(x, g, dy):
    r = lax.rsqrt(jnp.mean(x * x, axis=-1, keepdims=True) + EPS)
    xn = x * r
    dyg = dy * g
    dx = r * (dyg - xn * jnp.mean(dyg * xn, axis=-1, keepdims=True))
    return dx, jnp.sum(dy * xn, axis=0, keepdims=True)


def _half_sum(v, lo):
    s_lo = jnp.sum(jnp.where(lo, v, 0.0), axis=-1, keepdims=True)
    s_hi = jnp.sum(jnp.where(lo, 0.0, v), axis=-1, keepdims=True)
    return jnp.where(lo, s_lo, s_hi)


def _pair_norm(zp, g, lo):
    r = lax.rsqrt(_half_sum(zp * zp, lo) * (1.0 / HEAD_DIM) + EPS)
    return zp * r * g


def _pair_norm_bwd(zp, g, dy, lo):
    r = lax.rsqrt(_half_sum(zp * zp, lo) * (1.0 / HEAD_DIM) + EPS)
    xn = zp * r
    dyg = dy * g
    dx = r * (dyg - xn * (_half_sum(dyg * xn, lo) * (1.0 / HEAD_DIM)))
    return dx, jnp.sum(dy * xn, axis=0, keepdims=True)


def _to_stacked(pair, group, lo):
    rolled = pltpu.roll(pair, 64, axis=1)
    if group == 0:
        return jnp.where(lo, pair, 0.0), jnp.where(lo, rolled, 0.0)
    return jnp.where(lo, 0.0, rolled), jnp.where(lo, 0.0, pair)


def _from_stacked(even, odd, group, lo):
    if group == 0:
        return jnp.where(lo, even, pltpu.roll(odd, 64, axis=1))
    return jnp.where(lo, pltpu.roll(even, 64, axis=1), odd)


def _sigmoid(v):
    return 1.0 / (1.0 + jnp.exp(-v))


def _pool_counts(tile, n_rows):
    t1 = tile * n_rows + lax.broadcasted_iota(jnp.int32, (n_rows, POOL_WIDTH), 0) + 1
    lane = lax.broadcasted_iota(jnp.int32, (n_rows, POOL_WIDTH), 1)
    win = jnp.where(lane < 128, 2, jnp.where(lane < 256, 4, jnp.where(lane < 384, 8, 16)))
    return jnp.minimum(t1, win).astype(F32)


def _attn_in(x2, g_attn, gq, gk, slab):
    s_len = x2.shape[0]
    t = 512

    def body(x_ref, g_ref, gq_ref, gk_ref, slab_ref, hn_ref, zqk_ref, u_ref, kn_ref, v_ref, qst_ref, w_ref, sems):
        @pl.when(pl.program_id(0) == 0)
        def _():
            _load_rows(slab_ref, "inT", w_ref, sems)

        hn = _rms_fwd(x_ref[...], g_ref[...]).astype(BF16)
        hn_ref[...] = hn
        z = _dot(hn, w_ref[...], 1, 1)
        zqk_ref[...] = z[:, :640]
        u_ref[...] = z[:, 768:]
        v_ref[...] = z[:, 640:768].astype(BF16)
        lo = lax.broadcasted_iota(jnp.int32, (t, 128), 1) < 64
        kn_ref[...] = _pair_norm(z[:, 512:640], gk_ref[...], lo).astype(BF16)
        for p in range(4):
            qn = _pair_norm(z[:, 128 * p:128 * p + 128], gq_ref[...], lo)
            even, odd = _to_stacked(qn, p // 2, lo)
            qst_ref[2 * p] = even.astype(BF16)
            qst_ref[2 * p + 1] = odd.astype(BF16)

    row = lambda w: pl.BlockSpec((t, w), lambda i: (i, 0))
    return pl.pallas_call(
        body, name="attn_in", grid=(s_len // t,),
        in_specs=[row(D_MODEL), _full((1, D_MODEL)), _full((1, 128)), _full((1, 128)), ANY],
        out_specs=[row(D_MODEL), row(640), row(POOL_WIDTH), row(128), row(128),
                   pl.BlockSpec((N_Q_HEADS, t, 128), lambda i: (0, i, 0))],
        out_shape=[jax.ShapeDtypeStruct((s_len, D_MODEL), BF16), jax.ShapeDtypeStruct((s_len, 640), F32),
                   jax.ShapeDtypeStruct((s_len, POOL_WIDTH), F32), jax.ShapeDtypeStruct((s_len, 128), BF16),
                   jax.ShapeDtypeStruct((s_len, 128), BF16), jax.ShapeDtypeStruct((N_Q_HEADS, s_len, 128), BF16)],
        scratch_shapes=[pltpu.VMEM((IN_WIDTH, D_MODEL), BF16), pltpu.SemaphoreType.DMA((N_CHIPS,))],
        compiler_params=_params(),
    )(x2, g_attn, gq, gk, slab)


def _bucket_table():
    i_idx = np.arange(BLOCK)[:, None]
    j_idx = np.arange(2 * BLOCK)[None, :]
    d = BLOCK + i_idx - j_idx
    n = np.maximum(d, 0)
    max_exact = N_BUCKETS // 2
    nf = np.maximum(n, 1).astype(np.float64)
    large = max_exact + (np.log(nf / max_exact) / np.log(MAX_DISTANCE / max_exact) * (N_BUCKETS - max_exact)).astype(np.int64)
    large = np.minimum(large, N_BUCKETS - 1)
    bucket = np.where(n < max_exact, n, large)
    return np.where((d >= 0) & (d < BLOCK), bucket, -1).astype(np.int32)


def _bias_build(rel_bias_t, bucket):
    def body(rb_ref, bucket_ref, out_ref):
        bk = bucket_ref[...]
        for h in range(N_Q_HEADS):
            acc = jnp.full((BLOCK, 2 * BLOCK), NEG, F32)
            for b in range(N_BUCKETS):
                acc = jnp.where(bk == b, rb_ref[h, b], acc)
            out_ref[pl.ds(h * BLOCK, BLOCK), :] = acc

    return pl.pallas_call(
        body, name="bias_build",
        in_specs=[pl.BlockSpec(memory_space=pltpu.SMEM), pl.BlockSpec(memory_space=pltpu.VMEM)],
        out_specs=pl.BlockSpec(memory_space=pltpu.VMEM),
        out_shape=jax.ShapeDtypeStruct((N_Q_HEADS * BLOCK, 2 * BLOCK), F32),
    )(rel_bias_t, bucket)


def _band_softmax(q_ref, kp_ref, kc_ref, bias_ref, sink_ref, block):
    q = q_ref[...].reshape(N_Q_HEADS * BLOCK, 128)
    k2 = jnp.concatenate([kp_ref[...], kc_ref[...]], axis=0)
    s = _dot(q, k2, 1, 1) * (HEAD_DIM ** -0.5) + bias_ref[...]
    col = lax.broadcasted_iota(jnp.int32, s.shape, 1)
    s = jnp.where(col < jnp.where(block == 0, BLOCK, 0), NEG, s)
    sink = sink_ref[...]
    m = jnp.maximum(jnp.max(s, axis=-1, keepdims=True), sink)
    p = jnp.exp(s - m)
    e_sink = jnp.exp(sink - m)
    inv = 1.0 / (jnp.sum(p, axis=-1, keepdims=True) + e_sink)
    return q, k2, p * inv, e_sink * inv


def _attn_specs():
    prev = lambda i: (jnp.maximum(i - 1, 0), 0)
    cur = lambda i: (i, 0)
    stacked = pl.BlockSpec((N_Q_HEADS, BLOCK, 128), lambda i: (0, i, 0))
    kv = [pl.BlockSpec((BLOCK, 128), prev), pl.BlockSpec((BLOCK, 128), cur)]
    consts = [_full((N_Q_HEADS * BLOCK, 2 * BLOCK)), _full((N_Q_HEADS * BLOCK, 1))]
    return stacked, kv, consts


def _head_lane_mask():
    rows = lax.broadcasted_iota(jnp.int32, (N_Q_HEADS * BLOCK, 128), 0)
    lanes = lax.broadcasted_iota(jnp.int32, (N_Q_HEADS * BLOCK, 128), 1)
    return (rows < 4 * BLOCK) == (lanes < 64)


def _attn_fwd(qst, kn, vb, bias_st, sink_st):
    s_len = kn.shape[0]

    def body(q_ref, kp_ref, kc_ref, vp_ref, vc_ref, bias_ref, sink_ref, o_ref):
        _, _, probs, _ = _band_softmax(q_ref, kp_ref, kc_ref, bias_ref, sink_ref, pl.program_id(0))
        v2 = jnp.concatenate([vp_ref[...], vc_ref[...]], axis=0)
        o = _dot(probs.astype(BF16), v2, 1, 0)
        o_ref[...] = jnp.where(_head_lane_mask(), o, 0.0).astype(BF16).reshape(N_Q_HEADS, BLOCK, 128)

    stacked, kv, consts = _attn_specs()
    return pl.pallas_call(
        body, name="attn_fwd", grid=(s_len // BLOCK,),
        in_specs=[stacked] + kv + kv + consts, out_specs=stacked,
        out_shape=jax.ShapeDtypeStruct((N_Q_HEADS, s_len, 128), BF16),
        compiler_params=_params(),
    )(qst, kn, kn, vb, vb, bias_st, sink_st)


def _mix_out(u, ost, x2, slab, wpool, pool_scale, g_ffn):
    s_len = x2.shape[0]
    t = 512
    n = t + 16

    def body(u_ref, o_ref, x_ref, slab_ref, wp_ref, sc_ref, g_ref, pooled_ref, mix_ref, h1_ref, hn_ref,
             w_ref, ext_ref, st_ref, sems):
        i = pl.program_id(0)

        @pl.when(i == 0)
        def _():
            _load_rows(slab_ref, "out", w_ref, sems)
            ext_ref[...] = jnp.zeros_like(ext_ref)
            st_ref[...] = jnp.zeros_like(st_ref)

        u_tile = u_ref[...]
        ext_ref[pl.ds(POOL_HALO, t), :] = u_tile
        st_ref[pl.ds(8, n), :] = ext_ref[pl.ds(8, n), :] + ext_ref[pl.ds(7, n), :]
        st_ref[pl.ds(8, n), 128:] = st_ref[pl.ds(8, n), 128:] + st_ref[pl.ds(6, n), 128:]
        st_ref[pl.ds(8, n), 256:] = st_ref[pl.ds(8, n), 256:] + st_ref[pl.ds(4, n), 256:]
        st_ref[pl.ds(8, n), 384:] = st_ref[pl.ds(8, n), 384:] + st_ref[pl.ds(0, n), 384:]
        ext_ref[pl.ds(0, POOL_HALO), :] = ext_ref[pl.ds(t, POOL_HALO), :]
        pooled = (st_ref[pl.ds(POOL_HALO, t), :] / _pool_counts(i, t) - u_tile).astype(BF16)
        pooled_ref[...] = pooled
        for g in range(4):
            cols = slice(128 * g, 128 * g + 128)
            y = _dot(pooled[:, cols], wp_ref[g], 1, 0) * sc_ref[:, cols]
            mix_ref[:, ATTN_WIDTH + 128 * g:ATTN_WIDTH + 128 * g + 128] = y.astype(BF16)
        lo = lax.broadcasted_iota(jnp.int32, (t, 128), 1) < 64
        for p in range(4):
            a = _from_stacked(o_ref[2 * p].astype(F32), o_ref[2 * p + 1].astype(F32), p // 2, lo)
            mix_ref[:, 128 * p:128 * p + 128] = a.astype(BF16)
        h1 = x_ref[...] + _dot(mix_ref[...], w_ref[...], 1, 0)
        h1_ref[...] = h1
        hn_ref[...] = _rms_fwd(h1, g_ref[...]).astype(BF16)

    row = lambda w: pl.BlockSpec((t, w), lambda i: (i, 0))
    return pl.pallas_call(
        body, name="mix_out", grid=(s_len // t,),
        in_specs=[row(POOL_WIDTH), pl.BlockSpec((N_Q_HEADS, t, 128), lambda i: (0, i, 0)), row(D_MODEL), ANY,
                  _full((4, 128, 128)), _full((1, POOL_WIDTH)), _full((1, D_MODEL))],
        out_specs=[row(POOL_WIDTH), row(D_MODEL), row(D_MODEL), row(D_MODEL)],
        out_shape=[jax.ShapeDtypeStruct((s_len, POOL_WIDTH), BF16), jax.ShapeDtypeStruct((s_len, D_MODEL), BF16),
                   jax.ShapeDtypeStruct((s_len, D_MODEL), F32), jax.ShapeDtypeStruct((s_len, D_MODEL), BF16)],
        scratch_shapes=[pltpu.VMEM((D_MODEL, D_MODEL), BF16), pltpu.VMEM((t + POOL_HALO, POOL_WIDTH), F32),
                        pltpu.VMEM((t + POOL_HALO, POOL_WIDTH), F32), pltpu.SemaphoreType.DMA((N_CHIPS,))],
        compiler_params=_params(),
    )(u, ost, x2, slab, wpool, pool_scale, g_ffn)


def _ffn_fwd(hn2, h1, slab):
    s_len = h1.shape[0]
    t = 256

    def body(hn_ref, h1_ref, slab_ref, gate_ref, up_ref, h2_ref, wg_ref, wu_ref, wd_ref, sems):
        @pl.when(pl.program_id(0) == 0)
        def _():
            _load_rows(slab_ref, "gateT", wg_ref, sems)
            _load_rows(slab_ref, "upT", wu_ref, sems)
            _load_rows(slab_ref, "down", wd_ref, sems)

        hn = hn_ref[...]
        h2 = h1_ref[...]
        for ch in range(D_FF // FF_CHUNK):
            rows = pl.ds(ch * FF_CHUNK, FF_CHUNK)
            cols = slice(ch * FF_CHUNK, (ch + 1) * FF_CHUNK)
            gate = _dot(hn, wg_ref[rows, :], 1, 1)
            up = _dot(hn, wu_ref[rows, :], 1, 1)
            gate_ref[:, cols] = gate
            up_ref[:, cols] = up
            act = (gate * _sigmoid(gate) * up).astype(BF16)
            h2 = h2 + _dot(act, wd_ref[rows, :], 1, 0)
        h2_ref[...] = h2

    row = lambda w: pl.BlockSpec((t, w), lambda i: (i, 0))
    return pl.pallas_call(
        body, name="ffn_fwd", grid=(s_len // t,),
        in_specs=[row(D_MODEL), row(D_MODEL), ANY],
        out_specs=[row(D_FF), row(D_FF), row(D_MODEL)],
        out_shape=[jax.ShapeDtypeStruct((s_len, D_FF), F32), jax.ShapeDtypeStruct((s_len, D_FF), F32),
                   jax.ShapeDtypeStruct((s_len, D_MODEL), F32)],
        scratch_shapes=[pltpu.VMEM((D_FF, D_MODEL), BF16)] * 3 + [pltpu.SemaphoreType.DMA((N_CHIPS,))],
        compiler_params=_params(VMEM_LIMIT_BIG),
    )(hn2, h1, slab)


def _ple_loss(h2, p2, tgt, slab, wplp_t, g_ple):
    s_len = h2.shape[0]
    t = 512
    n_tiles = s_len // t

    def body(h2_ref, p_ref, tgt_ref, slab_ref, wp_ref, g_ref, loss_ref, dh2_ref, dgl_ref, dpp_ref, hn_ref, dg_ref,
             w_ref, loss_acc, sems):
        i = pl.program_id(0)

        @pl.when(i == 0)
        def _():
            _load_rows(slab_ref, "plg", w_ref, sems)
            loss_acc[...] = jnp.zeros_like(loss_acc)
            dg_ref[...] = jnp.zeros_like(dg_ref)

        h2v = h2_ref[...]
        g = g_ref[...]
        hn = _rms_fwd(h2v, g).astype(BF16)
        hn_ref[...] = hn
        gate = _sigmoid(_dot(hn, w_ref[...], 1, 0))
        pp = _dot(p_ref[...].astype(BF16), wp_ref[...], 1, 1)
        err = h2v + gate * pp - tgt_ref[...]
        loss_acc[...] += jnp.sum(err * err, axis=0, keepdims=True)
        dy = err * (1.0 / D_MODEL)
        dpp_ref[...] = (dy * gate).astype(BF16)
        dgl = (dy * pp * gate * (1.0 - gate)).astype(BF16)
        dgl_ref[...] = dgl
        dx, dg = _rms_bwd(h2v, g, _dot(dgl, w_ref[...], 1, 1))
        dh2_ref[...] = dy + dx
        dg_ref[...] += dg

        @pl.when(i == n_tiles - 1)
        def _():
            total = jnp.sum(loss_acc[...], axis=-1, keepdims=True) * (0.5 / D_MODEL)
            loss_ref[...] = jnp.broadcast_to(total, loss_ref.shape)

    row = lambda w: pl.BlockSpec((t, w), lambda i: (i, 0))
    return pl.pallas_call(
        body, name="ple_loss", grid=(n_tiles,),
        in_specs=[row(D_MODEL), row(PLE_DIM), row(D_MODEL), ANY, _full((D_MODEL, PLE_DIM)), _full((1, D_MODEL))],
        out_specs=[_full((1, 128)), row(D_MODEL), row(D_MODEL), row(D_MODEL), row(D_MODEL), _full((1, D_MODEL))],
        out_shape=[jax.ShapeDtypeStruct((1, 128), F32), jax.ShapeDtypeStruct((s_len, D_MODEL), F32),
                   jax.ShapeDtypeStruct((s_len, D_MODEL), BF16), jax.ShapeDtypeStruct((s_len, D_MODEL), BF16),
                   jax.ShapeDtypeStruct((s_len, D_MODEL), BF16), jax.ShapeDtypeStruct((1, D_MODEL), F32)],
        scratch_shapes=[pltpu.VMEM((D_MODEL, D_MODEL), BF16), pltpu.VMEM((1, D_MODEL), F32),
                        pltpu.SemaphoreType.DMA((N_CHIPS,))],
        compiler_params=_params(),
    )(h2, p2, tgt, slab, wplp_t, g_ple)


def _ffn_bwd(dh2, gate, up, h1, slab, g_ffn):
    s_len = h1.shape[0]
    t = 256

    def body(dh2_ref, gate_ref, up_ref, h1_ref, slab_ref, g_ref, dgate_ref, dup_ref, act_ref, dh1_ref, dg_ref,
             wg_ref, wu_ref, wd_ref, sems):
        @pl.when(pl.program_id(0) == 0)
        def _():
            _load_rows(slab_ref, "gateT", wg_ref, sems)
            _load_rows(slab_ref, "upT", wu_ref, sems)
            _load_rows(slab_ref, "down", wd_ref, sems)
            dg_ref[...] = jnp.zeros_like(dg_ref)

        dh2v = dh2_ref[...]
        dh2b = dh2v.astype(BF16)
        dhn = jnp.zeros((t, D_MODEL), F32)
        for ch in range(D_FF // FF_CHUNK):
            rows = pl.ds(ch * FF_CHUNK, FF_CHUNK)
            cols = slice(ch * FF_CHUNK, (ch + 1) * FF_CHUNK)
            dact = _dot(dh2b, wd_ref[rows, :], 1, 1)
            gate_v = gate_ref[:, cols]
            up_v = up_ref[:, cols]
            sg = _sigmoid(gate_v)
            silu = gate_v * sg
            act_ref[:, cols] = (silu * up_v).astype(BF16)
            dup = (dact * silu).astype(BF16)
            dgate = (dact * up_v * (sg * (1.0 + gate_v * (1.0 - sg)))).astype(BF16)
            dup_ref[:, cols] = dup
            dgate_ref[:, cols] = dgate
            dhn = dhn + _dot(dgate, wg_ref[rows, :], 1, 0) + _dot(dup, wu_ref[rows, :], 1, 0)
        dx, dg = _rms_bwd(h1_ref[...], g_ref[...], dhn)
        dh1_ref[...] = dh2v + dx
        dg_ref[...] += dg

    row = lambda w: pl.BlockSpec((t, w), lambda i: (i, 0))
    return pl.pallas_call(
        body, name="ffn_bwd", grid=(s_len // t,),
        in_specs=[row(D_MODEL), row(D_FF), row(D_FF), row(D_MODEL), ANY, _full((1, D_MODEL))],
        out_specs=[row(D_FF), row(D_FF), row(D_FF), row(D_MODEL), _full((1, D_MODEL))],
        out_shape=[jax.ShapeDtypeStruct((s_len, D_FF), BF16), jax.ShapeDtypeStruct((s_len, D_FF), BF16),
                   jax.ShapeDtypeStruct((s_len, D_FF), BF16), jax.ShapeDtypeStruct((s_len, D_MODEL), F32),
                   jax.ShapeDtypeStruct((1, D_MODEL), F32)],
        scratch_shapes=[pltpu.VMEM((D_FF, D_MODEL), BF16)] * 3 + [pltpu.SemaphoreType.DMA((N_CHIPS,))],
        compiler_params=_params(VMEM_LIMIT_BIG),
    )(dh2, gate, up, h1, slab, g_ffn)


def _mix_out_bwd(dh1, slab, pooled, wpool, pool_scale):
    s_len = dh1.shape[0]
    t = 512
    n = t + 16
    n_tiles = s_len // t

    def body(dh1_ref, slab_ref, pooled_ref, wp_ref, sc_ref, dost_ref, du_ref, dyp_ref, dsc_ref,
             w_ref, ext_ref, st_ref, sems):
        i = pl.program_id(0)

        @pl.when(i == 0)
        def _():
            _load_rows(slab_ref, "out", w_ref, sems)
            ext_ref[...] = jnp.zeros_like(ext_ref)
            st_ref[...] = jnp.zeros_like(st_ref)
            dsc_ref[...] = jnp.zeros_like(dsc_ref)

        dmix = _dot(dh1_ref[...].astype(BF16), w_ref[...], 1, 1)
        lo = lax.broadcasted_iota(jnp.int32, (t, 128), 1) < 64
        for p in range(4):
            even, odd = _to_stacked(dmix[:, 128 * p:128 * p + 128], p // 2, lo)
            dost_ref[2 * p] = even.astype(BF16)
            dost_ref[2 * p + 1] = odd.astype(BF16)
        pooled_v = pooled_ref[...]
        counts = _pool_counts(n_tiles - 1 - i, t)
        for g in range(4):
            cols = slice(128 * g, 128 * g + 128)
            dm = dmix[:, ATTN_WIDTH + 128 * g:ATTN_WIDTH + 128 * g + 128]
            ypre = _dot(pooled_v[:, cols], wp_ref[g], 1, 0)
            dsc_ref[:, cols] += jnp.sum(ypre * dm, axis=0, keepdims=True)
            dyp = (dm * sc_ref[:, cols]).astype(BF16)
            dyp_ref[:, cols] = dyp
            dpooled = _dot(dyp, wp_ref[g], 1, 1)
            du_ref[:, cols] = -dpooled
            ext_ref[pl.ds(0, t), cols] = dpooled / counts[:, cols]
        st_ref[pl.ds(0, n), :] = ext_ref[pl.ds(0, n), :] + ext_ref[pl.ds(1, n), :]
        st_ref[pl.ds(0, n), 128:] = st_ref[pl.ds(0, n), 128:] + st_ref[pl.ds(2, n), 128:]
        st_ref[pl.ds(0, n), 256:] = st_ref[pl.ds(0, n), 256:] + st_ref[pl.ds(4, n), 256:]
        st_ref[pl.ds(0, n), 384:] = st_ref[pl.ds(0, n), 384:] + st_ref[pl.ds(8, n), 384:]
        ext_ref[pl.ds(t, POOL_HALO), :] = ext_ref[pl.ds(0, POOL_HALO), :]
        du_ref[...] += st_ref[pl.ds(0, t), :]

    rev = lambda w: pl.BlockSpec((t, w), lambda i: (n_tiles - 1 - i, 0))
    return pl.pallas_call(
        body, name="mix_out_bwd", grid=(n_tiles,),
        in_specs=[rev(D_MODEL), ANY, rev(POOL_WIDTH), _full((4, 128, 128)), _full((1, POOL_WIDTH))],
        out_specs=[pl.BlockSpec((N_Q_HEADS, t, 128), lambda i: (0, n_tiles - 1 - i, 0)), rev(POOL_WIDTH), rev(POOL_WIDTH),
                   _full((1, POOL_WIDTH))],
        out_shape=[jax.ShapeDtypeStruct((N_Q_HEADS, s_len, 128), BF16), jax.ShapeDtypeStruct((s_len, POOL_WIDTH), F32),
                   jax.ShapeDtypeStruct((s_len, POOL_WIDTH), BF16), jax.ShapeDtypeStruct((1, POOL_WIDTH), F32)],
        scratch_shapes=[pltpu.VMEM((D_MODEL, D_MODEL), BF16), pltpu.VMEM((t + POOL_HALO, POOL_WIDTH), F32),
                        pltpu.VMEM((t + POOL_HALO, POOL_WIDTH), F32), pltpu.SemaphoreType.DMA((N_CHIPS,))],
        compiler_params=_params(),
    )(dh1, slab, pooled, wpool, pool_scale)


def _attn_bwd(qst, kn, vb, dost, bias_st, sink_st):
    s_len = kn.shape[0]

    def body(q_ref, kp_ref, kc_ref, vp_ref, vc_ref, do_ref, bias_ref, sink_ref, dq_ref, dk_ref, dv_ref, dbias_ref, dsink_ref):
        i = pl.program_id(0)

        @pl.when(i == 0)
        def _():
            dk_ref[...] = jnp.zeros_like(dk_ref)
            dv_ref[...] = jnp.zeros_like(dv_ref)
            dbias_ref[...] = jnp.zeros_like(dbias_ref)
            dsink_ref[...] = jnp.zeros_like(dsink_ref)

        q, k2, probs, p_sink = _band_softmax(q_ref, kp_ref, kc_ref, bias_ref, sink_ref, i)
        v2 = jnp.concatenate([vp_ref[...], vc_ref[...]], axis=0)
        do = do_ref[...].reshape(N_Q_HEADS * BLOCK, 128)
        dp = _dot(do, v2, 1, 1)
        dsum = jnp.sum(probs * dp, axis=-1, keepdims=True)
        dlog = probs * (dp - dsum)
        dsink_ref[...] -= p_sink * dsum
        dbias_ref[...] += dlog
        dlog_s = (dlog * (HEAD_DIM ** -0.5)).astype(BF16)
        dq = _dot(dlog_s, k2, 1, 0)
        dq_ref[...] = jnp.where(_head_lane_mask(), dq, 0.0).reshape(N_Q_HEADS, BLOCK, 128)
        dk2 = _dot(dlog_s, q, 0, 0)
        dv2 = _dot(probs.astype(BF16), do, 0, 0)
        prev_rows = pl.ds(pl.multiple_of(jnp.maximum(i - 1, 0) * BLOCK, BLOCK), BLOCK)
        cur_rows = pl.ds(pl.multiple_of(i * BLOCK, BLOCK), BLOCK)
        dk_ref[prev_rows, :] += dk2[:BLOCK]
        dk_ref[cur_rows, :] += dk2[BLOCK:]
        dv_ref[prev_rows, :] += dv2[:BLOCK]
        dv_ref[cur_rows, :] += dv2[BLOCK:]

    stacked, kv, consts = _attn_specs()
    return pl.pallas_call(
        body, name="attn_bwd", grid=(s_len // BLOCK,),
        in_specs=[stacked] + kv + kv + [stacked] + consts,
        out_specs=[stacked, _full((s_len, 128)), _full((s_len, 128)), _full((N_Q_HEADS * BLOCK, 2 * BLOCK)),
                   _full((N_Q_HEADS * BLOCK, 1))],
        out_shape=[jax.ShapeDtypeStruct((N_Q_HEADS, s_len, 128), F32), jax.ShapeDtypeStruct((s_len, 128), F32),
                   jax.ShapeDtypeStruct((s_len, 128), F32), jax.ShapeDtypeStruct((N_Q_HEADS * BLOCK, 2 * BLOCK), F32),
                   jax.ShapeDtypeStruct((N_Q_HEADS * BLOCK, 1), F32)],
        compiler_params=_params(),
    )(qst, kn, kn, vb, vb, dost, bias_st, sink_st)


def _bias_reduce(dbias, dsink_rows, bucket):
    def body(db_ref, ds_ref, bucket_ref, rb_ref, sk_ref):
        bk = bucket_ref[...]
        rows = lax.broadcasted_iota(jnp.int32, (N_Q_HEADS, 128), 0)
        lanes = lax.broadcasted_iota(jnp.int32, (N_Q_HEADS, 128), 1)
        rb = jnp.zeros((N_Q_HEADS, 128), F32)
        sk = jnp.zeros((N_Q_HEADS, 128), F32)
        for h in range(N_Q_HEADS):
            band = db_ref[pl.ds(h * BLOCK, BLOCK), :]
            for b in range(N_BUCKETS):
                tot = jnp.sum(jnp.where(bk == b, band, 0.0))
                rb = jnp.where((rows == h) & (lanes == b), tot, rb)
            sk = jnp.where((rows == h) & (lanes == 0), jnp.sum(ds_ref[pl.ds(h * BLOCK, BLOCK), :]), sk)
        rb_ref[...] = rb
        sk_ref[...] = sk

    vm = pl.BlockSpec(memory_space=pltpu.VMEM)
    return pl.pallas_call(
        body, name="bias_reduce", in_specs=[vm, vm, vm], out_specs=[vm, vm],
        out_shape=[jax.ShapeDtypeStruct((N_Q_HEADS, 128), F32), jax.ShapeDtypeStruct((N_Q_HEADS, 128), F32)],
    )(dbias, dsink_rows, bucket)


def _attn_in_bwd(dqst, zqk, dk, dv, du, x2, dh1, slab, g_attn, gq, gk):
    s_len = x2.shape[0]
    t = 512

    def body(dq_ref, zqk_ref, dk_ref, dv_ref, du_ref, x_ref, dh1_ref, slab_ref, g_ref, gq_ref, gk_ref,
             dz_ref, dx_ref, dg_ref, dgq_ref, dgk_ref, w_ref, sems):
        @pl.when(pl.program_id(0) == 0)
        def _():
            _load_rows(slab_ref, "inT", w_ref, sems)
            dg_ref[...] = jnp.zeros_like(dg_ref)
            dgq_ref[...] = jnp.zeros_like(dgq_ref)
            dgk_ref[...] = jnp.zeros_like(dgk_ref)

        lo = lax.broadcasted_iota(jnp.int32, (t, 128), 1) < 64
        for p in range(4):
            dqn = _from_stacked(dq_ref[2 * p], dq_ref[2 * p + 1], p // 2, lo)
            dq_raw, dgq = _pair_norm_bwd(zqk_ref[:, 128 * p:128 * p + 128], gq_ref[...], dqn, lo)
            dz_ref[:, 128 * p:128 * p + 128] = dq_raw.astype(BF16)
            dgq_ref[...] += dgq
        dk_raw, dgk = _pair_norm_bwd(zqk_ref[:, 512:640], gk_ref[...], dk_ref[...], lo)
        dgk_ref[...] += dgk
        dz_ref[:, 512:640] = dk_raw.astype(BF16)
        dz_ref[:, 640:768] = dv_ref[...].astype(BF16)
        dz_ref[:, 768:] = du_ref[...].astype(BF16)
        dx, dg = _rms_bwd(x_ref[...], g_ref[...], _dot(dz_ref[...], w_ref[...], 1, 0))
        dx_ref[...] = dh1_ref[...] + dx
        dg_ref[...] += dg

    row = lambda w: pl.BlockSpec((t, w), lambda i: (i, 0))
    return pl.pallas_call(
        body, name="attn_in_bwd", grid=(s_len // t,),
        in_specs=[pl.BlockSpec((N_Q_HEADS, t, 128), lambda i: (0, i, 0)), row(640), row(128), row(128), row(POOL_WIDTH),
                  row(D_MODEL), row(D_MODEL), ANY, _full((1, D_MODEL)), _full((1, 128)), _full((1, 128))],
        out_specs=[row(IN_WIDTH), row(D_MODEL), _full((1, D_MODEL)), _full((1, 128)), _full((1, 128))],
        out_shape=[jax.ShapeDtypeStruct((s_len, IN_WIDTH), BF16), jax.ShapeDtypeStruct((s_len, D_MODEL), F32),
                   jax.ShapeDtypeStruct((1, D_MODEL), F32), jax.ShapeDtypeStruct((1, 128), F32),
                   jax.ShapeDtypeStruct((1, 128), F32)],
        scratch_shapes=[pltpu.VMEM((IN_WIDTH, D_MODEL), BF16), pltpu.SemaphoreType.DMA((N_CHIPS,))],
        compiler_params=_params(),
    )(dqst, zqk, dk, dv, du, x2, dh1, slab, g_attn, gq, gk)


def _dw(a, b, name):
    s_len, m = a.shape
    n_out = b.shape[1]
    tk = 512
    n_steps = s_len // tk
    tm = m // 2 if m > 1408 else m

    def body(a_ref, b_ref, o_ref, acc_ref):
        k = pl.program_id(1)

        @pl.when(k == 0)
        def _():
            acc_ref[...] = jnp.zeros_like(acc_ref)

        acc_ref[...] += _dot(a_ref[...].astype(BF16), b_ref[...].astype(BF16), 0, 0)

        @pl.when(k == n_steps - 1)
        def _():
            o_ref[...] = acc_ref[...].astype(BF16)

    return pl.pallas_call(
        body, name=name, grid=(m // tm, n_steps),
        in_specs=[pl.BlockSpec((tk, tm), lambda i, k: (k, i)), pl.BlockSpec((tk, n_out), lambda i, k: (k, 0))],
        out_specs=pl.BlockSpec((tm, n_out), lambda i, k: (i, 0)),
        out_shape=jax.ShapeDtypeStruct((m, n_out), BF16),
        scratch_shapes=[pltpu.VMEM((tm, n_out), F32)],
        compiler_params=_params(n_axes=2),
    )(a, b)


def _dw_pool(pooled, dyp):
    s_len = pooled.shape[0]
    tk = 512

    def body(a_ref, b_ref, o_ref):
        @pl.when(pl.program_id(0) == 0)
        def _():
            o_ref[...] = jnp.zeros_like(o_ref)

        for g in range(4):
            cols = slice(128 * g, 128 * g + 128)
            o_ref[g] += _dot(a_ref[:, cols], b_ref[:, cols], 0, 0)

    blk = pl.BlockSpec((tk, POOL_WIDTH), lambda k: (k, 0))
    return pl.pallas_call(
        body, name="dw_pool", grid=(s_len // tk,), in_specs=[blk, blk], out_specs=_full((4, 128, 128)),
        out_shape=jax.ShapeDtypeStruct((4, 128, 128), F32), compiler_params=_params(),
    )(pooled, dyp)


def _position():
    x, y, c = lax.axis_index("x"), lax.axis_index("y"), lax.axis_index("c")
    other_chips = [(1 - x, y), (x, 1 - y), (1 - x, 1 - y)]
    return x, y, c, other_chips


def _half(c):
    return pl.ds(pl.multiple_of(c * HALF_ROWS, 16), HALF_ROWS)


def _ag_weights(local_slab):
    def body(l_ref, g_ref, send, recv, local_sem):
        x, y, c, chips = _position()
        me = 2 * x + y
        sibling = (x, y, 1 - c)
        mine, theirs = _half(c), _half(1 - c)

        def copy(k, chip_idx, rows, to, src=None):
            dst = g_ref.at[chip_idx, rows, :]
            return pltpu.make_async_remote_copy(src_ref=dst if src is None else src, dst_ref=dst, send_sem=send.at[k],
                                                recv_sem=recv.at[k], device_id=to, device_id_type=MESH)

        own = pltpu.make_async_copy(l_ref, g_ref.at[me], local_sem)
        own.start()
        first = [copy(k, me, mine, (*chip, c), src=l_ref.at[mine, :]) for k, chip in enumerate(chips)]
        for cp in first:
            cp.start()
        passed = []
        for k, chip in enumerate(chips):
            idx = 2 * chip[0] + chip[1]
            copy(k, idx, mine, (x, y, c)).wait_recv()
            fwd = copy(3 + k, idx, mine, sibling)
            fwd.start()
            passed.append(fwd)
        for k, chip in enumerate(chips):
            copy(3 + k, 2 * chip[0] + chip[1], theirs, (x, y, c)).wait_recv()
        for cp in first + passed:
            cp.wait_send()
        own.wait()

    return pl.pallas_call(
        body, name="ag_weights", in_specs=[ANY], out_specs=ANY,
        out_shape=jax.ShapeDtypeStruct((N_CHIPS, SLAB_ROWS, D_MODEL), BF16),
        scratch_shapes=[pltpu.SemaphoreType.DMA((6,)), pltpu.SemaphoreType.DMA((6,)), pltpu.SemaphoreType.DMA],
    )(local_slab)


def _rs_swap_halves(partial):
    def body(p_ref, r_ref, send, recv):
        x, y, c, _ = _position()
        cp = pltpu.make_async_remote_copy(src_ref=p_ref.at[:, _half(1 - c), :], dst_ref=r_ref, send_sem=send, recv_sem=recv,
                                          device_id=(x, y, 1 - c), device_id_type=MESH)
        cp.start()
        cp.wait()

    return pl.pallas_call(
        body, name="rs_swap_halves", in_specs=[ANY], out_specs=ANY,
        out_shape=jax.ShapeDtypeStruct((N_CHIPS, HALF_ROWS, D_MODEL), BF16),
        scratch_shapes=[pltpu.SemaphoreType.DMA, pltpu.SemaphoreType.DMA],
    )(partial)


def _rs_add_halves(partial, other, core):
    t = HALF_ROWS // 2

    def body(core_ref, a_ref, b_ref, o_ref):
        o_ref[...] = (a_ref[...].astype(F32) + b_ref[...].astype(F32)).astype(BF16)

    steps = HALF_ROWS // t
    return pl.pallas_call(
        body, name="rs_add_halves",
        grid_spec=pltpu.PrefetchScalarGridSpec(
            num_scalar_prefetch=1, grid=(N_CHIPS, steps),
            in_specs=[pl.BlockSpec((1, t, D_MODEL), lambda j, i, core_ref: (j, core_ref[0] * steps + i, 0)),
                      pl.BlockSpec((1, t, D_MODEL), lambda j, i, core_ref: (j, i, 0))],
            out_specs=pl.BlockSpec((1, t, D_MODEL), lambda j, i, core_ref: (j, i, 0))),
        out_shape=jax.ShapeDtypeStruct((N_CHIPS, HALF_ROWS, D_MODEL), BF16),
        compiler_params=_params(n_axes=2),
    )(core, partial, other)


def _rs_exchange_chips(pre):
    def body(s_ref, r_ref, send, recv, local_sem):
        x, y, c, chips = _position()
        me = 2 * x + y

        def copy(k, src_idx, dst_idx, to):
            return pltpu.make_async_remote_copy(src_ref=s_ref.at[src_idx], dst_ref=r_ref.at[dst_idx], send_sem=send.at[k],
                                                recv_sem=recv.at[k], device_id=to, device_id_type=MESH)

        own = pltpu.make_async_copy(s_ref.at[me], r_ref.at[me], local_sem)
        own.start()
        sends = [copy(k, 2 * chip[0] + chip[1], me, (*chip, c)) for k, chip in enumerate(chips)]
        for cp in sends:
            cp.start()
        for k, chip in enumerate(chips):
            copy(k, me, 2 * chip[0] + chip[1], (x, y, c)).wait_recv()
        for cp in sends:
            cp.wait_send()
        own.wait()

    return pl.pallas_call(
        body, name="rs_exchange_chips", in_specs=[ANY], out_specs=ANY,
        out_shape=jax.ShapeDtypeStruct((N_CHIPS, HALF_ROWS, D_MODEL), BF16),
        scratch_shapes=[pltpu.SemaphoreType.DMA((3,)), pltpu.SemaphoreType.DMA((3,)), pltpu.SemaphoreType.DMA],
    )(pre)


def _sum_leading(parts, name, rows_per_step):
    n, rows, cols = parts.shape

    def body(p_ref, o_ref):
        acc = p_ref[0].astype(F32)
        for k in range(1, n):
            acc = acc + p_ref[k].astype(F32)
        o_ref[...] = acc

    return pl.pallas_call(
        body, name=name, grid=(rows // rows_per_step,),
        in_specs=[pl.BlockSpec((n, rows_per_step, cols), lambda i: (0, i, 0))],
        out_specs=pl.BlockSpec((rows_per_step, cols), lambda i: (i, 0)),
        out_shape=jax.ShapeDtypeStruct((rows, cols), F32), compiler_params=_params(),
    )(parts)


def _rs_finish(final_half, small):
    def body(f_ref, s_ref, g_ref, t_ref, send, recv, local_sems):
        x, y, c, chips = _position()
        sibling = (x, y, 1 - c)

        def slot(px, py, pc):
            return t_ref.at[4 * px + 2 * py + pc]

        def copy(k, block, to, src=None):
            return pltpu.make_async_remote_copy(src_ref=slot(*block) if src is None else src, dst_ref=slot(*block),
                                                send_sem=send.at[k], recv_sem=recv.at[k], device_id=to, device_id_type=MESH)

        def half_copy(rows, to):
            return pltpu.make_async_remote_copy(src_ref=f_ref, dst_ref=g_ref.at[rows, :], send_sem=send.at[7],
                                                recv_sem=recv.at[7], device_id=to, device_id_type=MESH)

        own_half = pltpu.make_async_copy(f_ref, g_ref.at[_half(c), :], local_sems.at[0])
        own_small = pltpu.make_async_copy(s_ref, slot(x, y, c), local_sems.at[1])
        own_half.start()
        own_small.start()
        to_sibling = half_copy(_half(c), sibling)
        to_sibling.start()
        first = [copy(0, (x, y, c), sibling, src=s_ref)]
        first += [copy(1 + k, (x, y, c), (*chip, c), src=s_ref) for k, chip in enumerate(chips)]
        for cp in first:
            cp.start()
        passed = []
        for k, chip in enumerate(chips):
            copy(1 + k, (*chip, c), (x, y, c)).wait_recv()
            fwd = copy(4 + k, (*chip, c), sibling)
            fwd.start()
            passed.append(fwd)
        copy(0, sibling, (x, y, c)).wait_recv()
        for k, chip in enumerate(chips):
            copy(4 + k, (*chip, 1 - c), (x, y, c)).wait_recv()
        half_copy(_half(1 - c), (x, y, c)).wait_recv()
        for cp in first + passed + [to_sibling]:
            cp.wait_send()
        own_half.wait()
        own_small.wait()

    return pl.pallas_call(
        body, name="rs_finish", in_specs=[ANY, ANY], out_specs=[ANY, ANY],
        out_shape=[jax.ShapeDtypeStruct((SLAB_ROWS, D_MODEL), F32), jax.ShapeDtypeStruct((N_DEV, SMALL_ROWS, 128), F32)],
        scratch_shapes=[pltpu.SemaphoreType.DMA((8,)), pltpu.SemaphoreType.DMA((8,)), pltpu.SemaphoreType.DMA((2,))],
    )(final_half, small)


def _adamw(w, g, m, v, name):
    rows, cols = w.shape
    t = rows if rows % 256 else 256

    def body(w_ref, g_ref, m_ref, v_ref, d_ref, nm_ref, nv_ref):
        gv = g_ref[...]
        m_new = ADAM_B1 * m_ref[...] + (1.0 - ADAM_B1) * gv
        v_new = ADAM_B2 * v_ref[...] + (1.0 - ADAM_B2) * (gv * gv)
        m_hat = m_new / (1.0 - ADAM_B1 ** ADAM_STEP)
        v_hat = v_new / (1.0 - ADAM_B2 ** ADAM_STEP)
        d_ref[...] = -ADAM_LR * (m_hat / (jnp.sqrt(v_hat) + ADAM_EPS) + ADAM_WD * w_ref[...])
        nm_ref[...] = m_new
        nv_ref[...] = v_new

    blk = pl.BlockSpec((t, cols), lambda i: (i, 0))
    shape = jax.ShapeDtypeStruct((rows, cols), F32)
    return pl.pallas_call(
        body, name=name, grid=(rows // t,), in_specs=[blk] * 4, out_specs=[blk] * 3, out_shape=[shape] * 3,
        compiler_params=_params(),
    )(w, g, m, v)


def _pad_rows(vec, n_rows):
    flat = vec.reshape(-1)
    return jnp.pad(flat, (0, n_rows * 128 - flat.shape[0])).reshape(n_rows, 128)


def _small_table(parts):
    table = jnp.zeros((SMALL_ROWS, 128), F32)
    for name, arr in parts.items():
        off, n_rows = SMALL[name]
        table = lax.dynamic_update_slice(table, _pad_rows(arr.astype(F32), n_rows), (off, 0))
    return table


def _small_entry(table, name, shape):
    off, n_rows = SMALL[name]
    size = int(np.prod(shape))
    return table[off:off + n_rows].reshape(-1)[:size].reshape(shape)


def _local_grads(x2, p2, tgt, slab, g_attn_norm, g_q, g_k, attn_sinks, rel_bias, w_pool, pool_scale, g_ffn_norm, g_ple_norm):
    bucket = jnp.asarray(_bucket_table())
    gq = jnp.tile(g_q, (1, 2))
    gk = jnp.tile(g_k, (1, 2))
    wpool = w_pool[0].astype(BF16)
    wplp_t = slab[:, SLAB["plpT"][0]:, :].reshape(D_MODEL, PLE_DIM)
    sink_st = jnp.repeat(attn_sinks[0], BLOCK)[:, None]
    bias_st = _bias_build(rel_bias.T, bucket)

    hn1, zqk, u, kn, vb, qst = _attn_in(x2, g_attn_norm, gq, gk, slab)
    ost = _attn_fwd(qst, kn, vb, bias_st, sink_st)
    pooled, mix, h1, hn2 = _mix_out(u, ost, x2, slab, wpool, pool_scale, g_ffn_norm)
    gate, up, h2 = _ffn_fwd(hn2, h1, slab)
    loss_v, dh2, dgl, dpp, hn3, dg_ple = _ple_loss(h2, p2, tgt, slab, wplp_t, g_ple_norm)

    dgate, dup, act, dh1, dg_ffn = _ffn_bwd(dh2, gate, up, h1, slab, g_ffn_norm)
    dost, du, dyp, dscale = _mix_out_bwd(dh1, slab, pooled, wpool, pool_scale)
    dqst, dk, dv, dbias, dsink_rows = _attn_bwd(qst, kn, vb, dost, bias_st, sink_st)
    drb, dsk = _bias_reduce(dbias, dsink_rows, bucket)
    dz, dx, dg_attn, dgq, dgk = _attn_in_bwd(dqst, zqk, dk, dv, du, x2, dh1, slab, g_attn_norm, gq, gk)

    chunks = [
        _dw(dz, hn1, "dw_in").reshape(N_CHIPS, -1, D_MODEL),
        _dw(mix, dh1, "dw_out").reshape(N_CHIPS, -1, D_MODEL),
        _dw(dgate, hn2, "dw_gate").reshape(N_CHIPS, -1, D_MODEL),
        _dw(dup, hn2, "dw_up").reshape(N_CHIPS, -1, D_MODEL),
        _dw(act, dh2, "dw_down").reshape(N_CHIPS, -1, D_MODEL),
        _dw(hn3, dgl, "dw_ple_gate").reshape(N_CHIPS, -1, D_MODEL),
        _dw(dpp, p2, "dw_ple_proj").reshape(N_CHIPS, -1, D_MODEL),
    ]
    partial = jnp.concatenate(chunks, axis=1)
    small = _small_table({
        "g_attn": dg_attn, "g_ffn": dg_ffn, "g_ple": dg_ple, "pool_scale": dscale,
        "g_q": dgq[:, :HEAD_DIM] + dgq[:, HEAD_DIM:], "g_k": dgk[:, :HEAD_DIM] + dgk[:, HEAD_DIM:],
        "sinks": dsk[:, 0], "rel_bias": drb[:, :N_BUCKETS].T, "loss": loss_v[:, :1], "w_pool": _dw_pool(pooled, dyp),
    })
    return dx, partial, small


def kernel(x, p, w_in, w_out, g_attn_norm, g_q, g_k, attn_sinks, rel_bias, w_pool, pool_scale, g_ffn_norm, w_gate, w_up, w_down, g_ple_norm, w_ple_gate, w_ple_proj, loss_target, m_w_in, m_w_out, m_g_attn_norm, m_g_q, m_g_k, m_attn_sinks, m_rel_bias, m_w_pool, m_pool_scale, m_g_ffn_norm, m_w_gate, m_w_up, m_w_down, m_g_ple_norm, m_w_ple_gate, m_w_ple_proj, v_w_in, v_w_out, v_g_attn_norm, v_g_q, v_g_k, v_attn_sinks, v_rel_bias, v_w_pool, v_pool_scale, v_g_ffn_norm, v_w_gate, v_w_up, v_w_down, v_g_ple_norm, v_w_ple_gate, v_w_ple_proj):
    core = lax.axis_index("c").astype(jnp.int32).reshape(1)

    local_slab = jnp.concatenate(
        [w_in[0].T, w_out[0], w_gate[0].T, w_up[0].T, w_down[0], w_ple_gate[0], w_ple_proj[0].T.reshape(64, D_MODEL)],
        axis=0).astype(BF16)
    slab = _ag_weights(local_slab)

    dx, partial, small = _local_grads(x[0], p[0, 0], loss_target[0], slab, g_attn_norm, g_q, g_k, attn_sinks, rel_bias,
                                      w_pool, pool_scale, g_ffn_norm, g_ple_norm)

    pre = _rs_add_halves(partial, _rs_swap_halves(partial), core)
    final_half = _sum_leading(_rs_exchange_chips(pre), "rs_sum_chips", HALF_ROWS // 2)
    grad_slab, small_all = _rs_finish(final_half, small)
    small_sum = _sum_leading(small_all, "small_sum", SMALL_ROWS)

    def rows(name):
        off, n_rows = SLAB[name]
        return grad_slab[off:off + n_rows]

    big = {
        "w_in": (w_in, m_w_in, v_w_in, rows("inT").T),
        "w_out": (w_out, m_w_out, v_w_out, rows("out")),
        "w_gate": (w_gate, m_w_gate, v_w_gate, rows("gateT").T),
        "w_up": (w_up, m_w_up, v_w_up, rows("upT").T),
        "w_down": (w_down, m_w_down, v_w_down, rows("down")),
        "w_ple_gate": (w_ple_gate, m_w_ple_gate, v_w_ple_gate, rows("plg")),
        "w_ple_proj": (w_ple_proj, m_w_ple_proj, v_w_ple_proj, rows("plpT").reshape(PLE_DIM, PLE_DIM).T),
    }
    small_params = {
        "g_attn_norm": (g_attn_norm, m_g_attn_norm, v_g_attn_norm, "g_attn"), "g_q": (g_q, m_g_q, v_g_q, "g_q"),
        "g_k": (g_k, m_g_k, v_g_k, "g_k"), "attn_sinks": (attn_sinks, m_attn_sinks, v_attn_sinks, "sinks"),
        "rel_bias": (rel_bias, m_rel_bias, v_rel_bias, "rel_bias"), "w_pool": (w_pool, m_w_pool, v_w_pool, "w_pool"),
        "pool_scale": (pool_scale, m_pool_scale, v_pool_scale, "pool_scale"),
        "g_ffn_norm": (g_ffn_norm, m_g_ffn_norm, v_g_ffn_norm, "g_ffn"),
        "g_ple_norm": (g_ple_norm, m_g_ple_norm, v_g_ple_norm, "g_ple"),
    }

    grads, deltas, new_ms, new_vs = {}, {}, {}, {}
    for name, (w, m, v, g2) in big.items():
        d, nm, nv = _adamw(w[0], g2, m[0], v[0], "adamw_" + name)
        grads[name], deltas[name], new_ms[name], new_vs[name] = g2[None], d[None], nm[None], nv[None]

    names = list(small_params)
    pack = lambda idx: _small_table({small_params[n][3]: small_params[n][idx] for n in names})
    d_t, nm_t, nv_t = _adamw(pack(0), small_sum, pack(1), pack(2), "adamw_small")
    for n in names:
        w, _, _, key = small_params[n]
        grads[n] = _small_entry(small_sum, key, w.shape)
        deltas[n] = _small_entry(d_t, key, w.shape)
        new_ms[n] = _small_entry(nm_t, key, w.shape)
        new_vs[n] = _small_entry(nv_t, key, w.shape)

    order = ["w_in", "w_out", "g_attn_norm", "g_q", "g_k", "attn_sinks", "rel_bias", "w_pool", "pool_scale", "g_ffn_norm",
             "w_gate", "w_up", "w_down", "g_ple_norm", "w_ple_gate", "w_ple_proj"]
    loss = _small_entry(small_sum, "loss", ())
    return (loss, dx[None], *[grads[n] for n in order], *[deltas[n] for n in order], *[new_ms[n] for n in order],
            *[new_vs[n] for n in order])
```

```python
import functools

import numpy as np
import jax
import jax.numpy as jnp
from jax import lax
from jax.experimental import pallas as pl
from jax.experimental.pallas import tpu as pltpu

F32 = jnp.float32
BF16 = jnp.bfloat16
MESH = pl.DeviceIdType.MESH

D_MODEL = 1024
HEAD_DIM = 64
N_Q_HEADS = 8
ATTN_WIDTH = 512
KV_WIDTH = 128
POOL_WIDTH = 512
IN_WIDTH = 1280
D_FF = 2816
PLE_DIM = 256
FF_CHUNK = 1408
BLOCK = 128
N_BUCKETS = 32
MAX_DISTANCE = 128
POOL_SIZES = (2, 4, 8, 16)
EPS = 1e-6
NEG = -1e30
N_CHIPS = 4
N_DEV = 8

ADAM_LR = 0.001
ADAM_B1 = 0.9
ADAM_B2 = 0.999
ADAM_EPS = 1e-08
ADAM_WD = 0.01
ADAM_STEP = 10

SLAB = {"inT": (0, 320), "out": (320, 256), "gateT": (576, 704), "upT": (1280, 704), "down": (1984, 704),
        "plg": (2688, 256), "plpT": (2944, 64)}
SLAB_ROWS = 3008
HALF_ROWS = SLAB_ROWS // 2
POOL_HALO = 24

SMALL = {"g_attn": 0, "g_ffn": 8, "g_ple": 16, "pool_scale": 24, "g_q": 28, "g_k": 29, "sinks": 30, "loss": 31,
         "rel_bias": 32, "w_pool": 64}
SMALL_ROWS = 576

VMEM_LIMIT_BIG = 60 * 1024 * 1024
VMEM_LIMIT = 48 * 1024 * 1024


def _params(vmem=VMEM_LIMIT, n_axes=1):
    return pltpu.CompilerParams(dimension_semantics=("arbitrary",) * n_axes, vmem_limit_bytes=vmem)


def _dot(a, b, ca, cb):
    return lax.dot_general(a, b, (((ca,), (cb,)), ((), ())), preferred_element_type=F32)


def _full(shape):
    return pl.BlockSpec(shape, lambda i: (0,) * len(shape))


ANY = pl.BlockSpec(memory_space=pl.ANY)
VMEM_WHOLE = pl.BlockSpec(memory_space=pltpu.VMEM)


W_SPECS = [ANY, ANY, pl.BlockSpec(memory_space=pltpu.SMEM)]


def _load_rows(w_refs, name, dst_ref, sems):
    slab_ref, local_ref, me_ref = w_refs
    off, rows = SLAB[name]
    me = me_ref[0]
    for phase in ("start", "wait"):
        for j in range(N_CHIPS):
            dst = dst_ref.at[pl.ds(j * rows, rows), :]
            theirs = pltpu.make_async_copy(slab_ref.at[j, pl.ds(off, rows), :], dst, sems.at[j])
            own = pltpu.make_async_copy(local_ref.at[pl.ds(off, rows), :], dst, sems.at[j])

            @pl.when(me == j)
            def _():
                getattr(own, phase)()

            @pl.when(me != j)
            def _():
                getattr(theirs, phase)()


def _rms_fwd(x, g):
    r = lax.rsqrt(jnp.mean(x * x, axis=-1, keepdims=True) + EPS)
    return x * r * g


def _rms_bwd(x, g, dy):
    r = lax.rsqrt(jnp.mean(x * x, axis=-1, keepdims=True) + EPS)
    xn = x * r
    dyg = dy * g
    dx = r * (dyg - xn * jnp.mean(dyg * xn, axis=-1, keepdims=True))
    return dx, jnp.sum(dy * xn, axis=0, keepdims=True)


def _half_sum(v, lo):
    s_lo = jnp.sum(jnp.where(lo, v, 0.0), axis=-1, keepdims=True)
    s_hi = jnp.sum(jnp.where(lo, 0.0, v), axis=-1, keepdims=True)
    return jnp.where(lo, s_lo, s_hi)


def _pair_norm(zp, g, lo):
    r = lax.rsqrt(_half_sum(zp * zp, lo) * (1.0 / HEAD_DIM) + EPS)
    return zp * r * g


def _pair_norm_bwd(zp, g, dy, lo):
    r = lax.rsqrt(_half_sum(zp * zp, lo) * (1.0 / HEAD_DIM) + EPS)
    xn = zp * r
    dyg = dy * g
    dx = r * (dyg - xn * (_half_sum(dyg * xn, lo) * (1.0 / HEAD_DIM)))
    return dx, jnp.sum(dy * xn, axis=0, keepdims=True)


def _to_stacked(pair, group, lo):
    rolled = pltpu.roll(pair, 64, axis=1)
    if group == 0:
        return jnp.where(lo, pair, 0.0), jnp.where(lo, rolled, 0.0)
    return jnp.where(lo, 0.0, rolled), jnp.where(lo, 0.0, pair)


def _from_stacked(even, odd, group, lo):
    if group == 0:
        return jnp.where(lo, even, pltpu.roll(odd, 64, axis=1))
    return jnp.where(lo, pltpu.roll(even, 64, axis=1), odd)


def _sigmoid(v):
    return 1.0 / (1.0 + jnp.exp(-v))


def _pool_counts(tile, n_rows):
    t1 = tile * n_rows + lax.broadcasted_iota(jnp.int32, (n_rows, POOL_WIDTH), 0) + 1
    lane = lax.broadcasted_iota(jnp.int32, (n_rows, POOL_WIDTH), 1)
    win = jnp.where(lane < 128, 2, jnp.where(lane < 256, 4, jnp.where(lane < 384, 8, 16)))
    return jnp.minimum(t1, win).astype(F32)


def _attn_in(x2, g_attn, gq, gk, wts):
    s_len = x2.shape[0]
    t = 512

    def body(x_ref, g_ref, gq_ref, gk_ref, sl_ref, lo_ref, me_ref, hn_ref, zqk_ref, u_ref, kn_ref, v_ref, qst_ref, w_ref, sems):
        @pl.when(pl.program_id(0) == 0)
        def _():
            _load_rows((sl_ref, lo_ref, me_ref), "inT", w_ref, sems)

        hn = _rms_fwd(x_ref[...], g_ref[...]).astype(BF16)
        hn_ref[...] = hn
        z = _dot(hn, w_ref[...], 1, 1)
        zqk_ref[...] = z[:, :640]
        u_ref[...] = z[:, 768:]
        v_ref[...] = z[:, 640:768].astype(BF16)
        lo = lax.broadcasted_iota(jnp.int32, (t, 128), 1) < 64
        kn_ref[...] = _pair_norm(z[:, 512:640], gk_ref[...], lo).astype(BF16)
        for p in range(4):
            qn = _pair_norm(z[:, 128 * p:128 * p + 128], gq_ref[...], lo)
            even, odd = _to_stacked(qn, p // 2, lo)
            qst_ref[2 * p] = even.astype(BF16)
            qst_ref[2 * p + 1] = odd.astype(BF16)

    row = lambda w: pl.BlockSpec((t, w), lambda i: (i, 0))
    return pl.pallas_call(
        body, name="attn_in", grid=(s_len // t,),
        in_specs=[row(D_MODEL), _full((1, D_MODEL)), _full((1, 128)), _full((1, 128))] + W_SPECS,
        out_specs=[row(D_MODEL), row(640), row(POOL_WIDTH), row(128), row(128),
                   pl.BlockSpec((N_Q_HEADS, t, 128), lambda i: (0, i, 0))],
        out_shape=[jax.ShapeDtypeStruct((s_len, D_MODEL), BF16), jax.ShapeDtypeStruct((s_len, 640), F32),
                   jax.ShapeDtypeStruct((s_len, POOL_WIDTH), F32), jax.ShapeDtypeStruct((s_len, 128), BF16),
                   jax.ShapeDtypeStruct((s_len, 128), BF16), jax.ShapeDtypeStruct((N_Q_HEADS, s_len, 128), BF16)],
        scratch_shapes=[pltpu.VMEM((IN_WIDTH, D_MODEL), BF16), pltpu.SemaphoreType.DMA((N_CHIPS,))],
        compiler_params=_params(),
    )(x2, g_attn, gq, gk, *wts)


def _bucket_table():
    i_idx = np.arange(BLOCK)[:, None]
    j_idx = np.arange(2 * BLOCK)[None, :]
    d = BLOCK + i_idx - j_idx
    n = np.maximum(d, 0)
    max_exact = N_BUCKETS // 2
    nf = np.maximum(n, 1).astype(np.float64)
    large = max_exact + (np.log(nf / max_exact) / np.log(MAX_DISTANCE / max_exact) * (N_BUCKETS - max_exact)).astype(np.int64)
    large = np.minimum(large, N_BUCKETS - 1)
    bucket = np.where(n < max_exact, n, large)
    return np.where((d >= 0) & (d < BLOCK), bucket, -1).astype(np.int32)


def _bias_build(rel_bias_t, bucket):
    def body(rb_ref, bucket_ref, out_ref):
        bk = bucket_ref[...]
        for h in range(N_Q_HEADS):
            acc = jnp.full((BLOCK, 2 * BLOCK), NEG, F32)
            for b in range(N_BUCKETS):
                acc = jnp.where(bk == b, rb_ref[h, b], acc)
            out_ref[pl.ds(h * BLOCK, BLOCK), :] = acc

    return pl.pallas_call(
        body, name="bias_build",
        in_specs=[pl.BlockSpec(memory_space=pltpu.SMEM), pl.BlockSpec(memory_space=pltpu.VMEM)],
        out_specs=pl.BlockSpec(memory_space=pltpu.VMEM),
        out_shape=jax.ShapeDtypeStruct((N_Q_HEADS * BLOCK, 2 * BLOCK), F32),
    )(rel_bias_t, bucket)


def _band_softmax(q_ref, kp_ref, kc_ref, bias_ref, sink_ref, block):
    q = q_ref[...].reshape(N_Q_HEADS * BLOCK, 128)
    k2 = jnp.concatenate([kp_ref[...], kc_ref[...]], axis=0)
    s = _dot(q, k2, 1, 1) * (HEAD_DIM ** -0.5) + bias_ref[...]
    col = lax.broadcasted_iota(jnp.int32, s.shape, 1)
    s = jnp.where(col < jnp.where(block == 0, BLOCK, 0), NEG, s)
    sink = sink_ref[...]
    m = jnp.maximum(jnp.max(s, axis=-1, keepdims=True), sink)
    p = jnp.exp(s - m)
    e_sink = jnp.exp(sink - m)
    inv = 1.0 / (jnp.sum(p, axis=-1, keepdims=True) + e_sink)
    return q, k2, p * inv, e_sink * inv


def _attn_specs():
    prev = lambda i: (jnp.maximum(i - 1, 0), 0)
    cur = lambda i: (i, 0)
    stacked = pl.BlockSpec((N_Q_HEADS, BLOCK, 128), lambda i: (0, i, 0))
    kv = [pl.BlockSpec((BLOCK, 128), prev), pl.BlockSpec((BLOCK, 128), cur)]
    consts = [_full((N_Q_HEADS * BLOCK, 2 * BLOCK)), _full((N_Q_HEADS * BLOCK, 1))]
    return stacked, kv, consts


def _head_lane_mask():
    rows = lax.broadcasted_iota(jnp.int32, (N_Q_HEADS * BLOCK, 128), 0)
    lanes = lax.broadcasted_iota(jnp.int32, (N_Q_HEADS * BLOCK, 128), 1)
    return (rows < 4 * BLOCK) == (lanes < 64)


def _attn_fwd(qst, kn, vb, bias_st, sink_st):
    s_len = kn.shape[0]

    def body(q_ref, kp_ref, kc_ref, vp_ref, vc_ref, bias_ref, sink_ref, o_ref):
        _, _, probs, _ = _band_softmax(q_ref, kp_ref, kc_ref, bias_ref, sink_ref, pl.program_id(0))
        v2 = jnp.concatenate([vp_ref[...], vc_ref[...]], axis=0)
        o = _dot(probs.astype(BF16), v2, 1, 0)
        o_ref[...] = jnp.where(_head_lane_mask(), o, 0.0).astype(BF16).reshape(N_Q_HEADS, BLOCK, 128)

    stacked, kv, consts = _attn_specs()
    return pl.pallas_call(
        body, name="attn_fwd", grid=(s_len // BLOCK,),
        in_specs=[stacked] + kv + kv + consts, out_specs=stacked,
        out_shape=jax.ShapeDtypeStruct((N_Q_HEADS, s_len, 128), BF16),
        compiler_params=_params(),
    )(qst, kn, kn, vb, vb, bias_st, sink_st)


def _mix_out(u, ost, x2, wts, wpool, pool_scale, g_ffn):
    s_len = x2.shape[0]
    t = 512
    n = t + 16

    def body(u_ref, o_ref, x_ref, sl_ref, lo_ref, me_ref, wp_ref, sc_ref, g_ref, pooled_ref, mix_ref, h1_ref, hn_ref,
             w_ref, ext_ref, st_ref, sems):
        i = pl.program_id(0)

        @pl.when(i == 0)
        def _():
            _load_rows((sl_ref, lo_ref, me_ref), "out", w_ref, sems)
            ext_ref[...] = jnp.zeros_like(ext_ref)
            st_ref[...] = jnp.zeros_like(st_ref)

        u_tile = u_ref[...]
        ext_ref[pl.ds(POOL_HALO, t), :] = u_tile
        st_ref[pl.ds(8, n), :] = ext_ref[pl.ds(8, n), :] + ext_ref[pl.ds(7, n), :]
        st_ref[pl.ds(8, n), 128:] = st_ref[pl.ds(8, n), 128:] + st_ref[pl.ds(6, n), 128:]
        st_ref[pl.ds(8, n), 256:] = st_ref[pl.ds(8, n), 256:] + st_ref[pl.ds(4, n), 256:]
        st_ref[pl.ds(8, n), 384:] = st_ref[pl.ds(8, n), 384:] + st_ref[pl.ds(0, n), 384:]
        ext_ref[pl.ds(0, POOL_HALO), :] = ext_ref[pl.ds(t, POOL_HALO), :]
        pooled = (st_ref[pl.ds(POOL_HALO, t), :] / _pool_counts(i, t) - u_tile).astype(BF16)
        pooled_ref[...] = pooled
        for g in range(4):
            cols = slice(128 * g, 128 * g + 128)
            y = _dot(pooled[:, cols], wp_ref[g], 1, 0) * sc_ref[:, cols]
            mix_ref[:, ATTN_WIDTH + 128 * g:ATTN_WIDTH + 128 * g + 128] = y.astype(BF16)
        lo = lax.broadcasted_iota(jnp.int32, (t, 128), 1) < 64
        for p in range(4):
            a = _from_stacked(o_ref[2 * p].astype(F32), o_ref[2 * p + 1].astype(F32), p // 2, lo)
            mix_ref[:, 128 * p:128 * p + 128] = a.astype(BF16)
        h1 = x_ref[...] + _dot(mix_ref[...], w_ref[...], 1, 0)
        h1_ref[...] = h1
        hn_ref[...] = _rms_fwd(h1, g_ref[...]).astype(BF16)

    row = lambda w: pl.BlockSpec((t, w), lambda i: (i, 0))
    return pl.pallas_call(
        body, name="mix_out", grid=(s_len // t,),
        in_specs=[row(POOL_WIDTH), pl.BlockSpec((N_Q_HEADS, t, 128), lambda i: (0, i, 0)), row(D_MODEL)] + W_SPECS
        + [_full((4, 128, 128)), _full((1, POOL_WIDTH)), _full((1, D_MODEL))],
        out_specs=[row(POOL_WIDTH), row(D_MODEL), row(D_MODEL), row(D_MODEL)],
        out_shape=[jax.ShapeDtypeStruct((s_len, POOL_WIDTH), BF16), jax.ShapeDtypeStruct((s_len, D_MODEL), BF16),
                   jax.ShapeDtypeStruct((s_len, D_MODEL), F32), jax.ShapeDtypeStruct((s_len, D_MODEL), BF16)],
        scratch_shapes=[pltpu.VMEM((D_MODEL, D_MODEL), BF16), pltpu.VMEM((t + POOL_HALO, POOL_WIDTH), F32),
                        pltpu.VMEM((t + POOL_HALO, POOL_WIDTH), F32), pltpu.SemaphoreType.DMA((N_CHIPS,))],
        compiler_params=_params(),
    )(u, ost, x2, *wts, wpool, pool_scale, g_ffn)


def _ffn_fwd(hn2, h1, wts):
    s_len = h1.shape[0]
    t = 256

    def body(hn_ref, h1_ref, sl_ref, lo_ref, me_ref, gate_ref, up_ref, h2_ref, wg_ref, wu_ref, wd_ref, sems):
        @pl.when(pl.program_id(0) == 0)
        def _():
            w_refs = (sl_ref, lo_ref, me_ref)
            _load_rows(w_refs, "gateT", wg_ref, sems)
            _load_rows(w_refs, "upT", wu_ref, sems)
            _load_rows(w_refs, "down", wd_ref, sems)

        hn = hn_ref[...]
        h2 = h1_ref[...]
        for ch in range(D_FF // FF_CHUNK):
            rows = pl.ds(ch * FF_CHUNK, FF_CHUNK)
            cols = slice(ch * FF_CHUNK, (ch + 1) * FF_CHUNK)
            gate = _dot(hn, wg_ref[rows, :], 1, 1)
            up = _dot(hn, wu_ref[rows, :], 1, 1)
            gate_ref[:, cols] = gate
            up_ref[:, cols] = up
            act = (gate * _sigmoid(gate) * up).astype(BF16)
            h2 = h2 + _dot(act, wd_ref[rows, :], 1, 0)
        h2_ref[...] = h2

    row = lambda w: pl.BlockSpec((t, w), lambda i: (i, 0))
    return pl.pallas_call(
        body, name="ffn_fwd", grid=(s_len // t,),
        in_specs=[row(D_MODEL), row(D_MODEL)] + W_SPECS,
        out_specs=[row(D_FF), row(D_FF), row(D_MODEL)],
        out_shape=[jax.ShapeDtypeStruct((s_len, D_FF), F32), jax.ShapeDtypeStruct((s_len, D_FF), F32),
                   jax.ShapeDtypeStruct((s_len, D_MODEL), F32)],
        scratch_shapes=[pltpu.VMEM((D_FF, D_MODEL), BF16)] * 3 + [pltpu.SemaphoreType.DMA((N_CHIPS,))],
        compiler_params=_params(VMEM_LIMIT_BIG),
    )(hn2, h1, *wts)


def _ple_loss(h2, p2, tgt, wts, wplp_t, g_ple):
    s_len = h2.shape[0]
    t = 512
    n_tiles = s_len // t

    def body(h2_ref, p_ref, tgt_ref, sl_ref, lo_ref, me_ref, wp_ref, g_ref, loss_ref, dh2_ref, dgl_ref, dpp_ref, hn_ref,
             dg_ref, w_ref, loss_acc, sems):
        i = pl.program_id(0)

        @pl.when(i == 0)
        def _():
            _load_rows((sl_ref, lo_ref, me_ref), "plg", w_ref, sems)
            loss_acc[...] = jnp.zeros_like(loss_acc)
            dg_ref[...] = jnp.zeros_like(dg_ref)

        h2v = h2_ref[...]
        g = g_ref[...]
        hn = _rms_fwd(h2v, g).astype(BF16)
        hn_ref[...] = hn
        gate = _sigmoid(_dot(hn, w_ref[...], 1, 0))
        pp = _dot(p_ref[...].astype(BF16), wp_ref[...], 1, 1)
        err = h2v + gate * pp - tgt_ref[...]
        loss_acc[...] += jnp.sum(err * err, axis=0, keepdims=True)
        dy = err * (1.0 / D_MODEL)
        dpp_ref[...] = (dy * gate).astype(BF16)
        dgl = (dy * pp * gate * (1.0 - gate)).astype(BF16)
        dgl_ref[...] = dgl
        dx, dg = _rms_bwd(h2v, g, _dot(dgl, w_ref[...], 1, 1))
        dh2_ref[...] = dy + dx
        dg_ref[...] += dg

        @pl.when(i == n_tiles - 1)
        def _():
            total = jnp.sum(loss_acc[...], axis=-1, keepdims=True) * (0.5 / D_MODEL)
            loss_ref[...] = jnp.broadcast_to(total, loss_ref.shape)

    row = lambda w: pl.BlockSpec((t, w), lambda i: (i, 0))
    return pl.pallas_call(
        body, name="ple_loss", grid=(n_tiles,),
        in_specs=[row(D_MODEL), row(PLE_DIM), row(D_MODEL)] + W_SPECS + [_full((D_MODEL, PLE_DIM)), _full((1, D_MODEL))],
        out_specs=[_full((1, 128)), row(D_MODEL), row(D_MODEL), row(D_MODEL), row(D_MODEL), _full((1, D_MODEL))],
        out_shape=[jax.ShapeDtypeStruct((1, 128), F32), jax.ShapeDtypeStruct((s_len, D_MODEL), F32),
                   jax.ShapeDtypeStruct((s_len, D_MODEL), BF16), jax.ShapeDtypeStruct((s_len, D_MODEL), BF16),
                   jax.ShapeDtypeStruct((s_len, D_MODEL), BF16), jax.ShapeDtypeStruct((1, D_MODEL), F32)],
        scratch_shapes=[pltpu.VMEM((D_MODEL, D_MODEL), BF16), pltpu.VMEM((1, D_MODEL), F32),
                        pltpu.SemaphoreType.DMA((N_CHIPS,))],
        compiler_params=_params(),
    )(h2, p2, tgt, *wts, wplp_t, g_ple)


def _ffn_bwd(dh2, gate, up, h1, wts, g_ffn):
    s_len = h1.shape[0]
    t = 256

    def body(dh2_ref, gate_ref, up_ref, h1_ref, sl_ref, lo_ref, me_ref, g_ref, dgate_ref, dup_ref, act_ref, dh1_ref, dg_ref,
             wg_ref, wu_ref, wd_ref, sems):
        @pl.when(pl.program_id(0) == 0)
        def _():
            w_refs = (sl_ref, lo_ref, me_ref)
            _load_rows(w_refs, "gateT", wg_ref, sems)
            _load_rows(w_refs, "upT", wu_ref, sems)
            _load_rows(w_refs, "down", wd_ref, sems)
            dg_ref[...] = jnp.zeros_like(dg_ref)

        dh2v = dh2_ref[...]
        dh2b = dh2v.astype(BF16)
        dhn = jnp.zeros((t, D_MODEL), F32)
        for ch in range(D_FF // FF_CHUNK):
            rows = pl.ds(ch * FF_CHUNK, FF_CHUNK)
            cols = slice(ch * FF_CHUNK, (ch + 1) * FF_CHUNK)
            dact = _dot(dh2b, wd_ref[rows, :], 1, 1)
            gate_v = gate_ref[:, cols]
            up_v = up_ref[:, cols]
            sg = _sigmoid(gate_v)
            silu = gate_v * sg
            act_ref[:, cols] = (silu * up_v).astype(BF16)
            dup = (dact * silu).astype(BF16)
            dgate = (dact * up_v * (sg * (1.0 + gate_v * (1.0 - sg)))).astype(BF16)
            dup_ref[:, cols] = dup
            dgate_ref[:, cols] = dgate
            dhn = dhn + _dot(dgate, wg_ref[rows, :], 1, 0) + _dot(dup, wu_ref[rows, :], 1, 0)
        dx, dg = _rms_bwd(h1_ref[...], g_ref[...], dhn)
        dh1_ref[...] = dh2v + dx
        dg_ref[...] += dg

    row = lambda w: pl.BlockSpec((t, w), lambda i: (i, 0))
    return pl.pallas_call(
        body, name="ffn_bwd", grid=(s_len // t,),
        in_specs=[row(D_MODEL), row(D_FF), row(D_FF), row(D_MODEL)] + W_SPECS + [_full((1, D_MODEL))],
        out_specs=[row(D_FF), row(D_FF), row(D_FF), row(D_MODEL), _full((1, D_MODEL))],
        out_shape=[jax.ShapeDtypeStruct((s_len, D_FF), BF16), jax.ShapeDtypeStruct((s_len, D_FF), BF16),
                   jax.ShapeDtypeStruct((s_len, D_FF), BF16), jax.ShapeDtypeStruct((s_len, D_MODEL), F32),
                   jax.ShapeDtypeStruct((1, D_MODEL), F32)],
        scratch_shapes=[pltpu.VMEM((D_FF, D_MODEL), BF16)] * 3 + [pltpu.SemaphoreType.DMA((N_CHIPS,))],
        compiler_params=_params(VMEM_LIMIT_BIG),
    )(dh2, gate, up, h1, *wts, g_ffn)


def _mix_out_bwd(dh1, wts, pooled, wpool, pool_scale):
    s_len = dh1.shape[0]
    t = 512
    n = t + 16
    n_tiles = s_len // t

    def body(dh1_ref, sl_ref, lo_ref, me_ref, pooled_ref, wp_ref, sc_ref, dost_ref, du_ref, dyp_ref, dsc_ref,
             w_ref, ext_ref, st_ref, sems):
        i = pl.program_id(0)

        @pl.when(i == 0)
        def _():
            _load_rows((sl_ref, lo_ref, me_ref), "out", w_ref, sems)
            ext_ref[...] = jnp.zeros_like(ext_ref)
            st_ref[...] = jnp.zeros_like(st_ref)
            dsc_ref[...] = jnp.zeros_like(dsc_ref)

        dmix = _dot(dh1_ref[...].astype(BF16), w_ref[...], 1, 1)
        lo = lax.broadcasted_iota(jnp.int32, (t, 128), 1) < 64
        for p in range(4):
            even, odd = _to_stacked(dmix[:, 128 * p:128 * p + 128], p // 2, lo)
            dost_ref[2 * p] = even.astype(BF16)
            dost_ref[2 * p + 1] = odd.astype(BF16)
        pooled_v = pooled_ref[...]
        counts = _pool_counts(n_tiles - 1 - i, t)
        for g in range(4):
            cols = slice(128 * g, 128 * g + 128)
            dm = dmix[:, ATTN_WIDTH + 128 * g:ATTN_WIDTH + 128 * g + 128]
            ypre = _dot(pooled_v[:, cols], wp_ref[g], 1, 0)
            dsc_ref[:, cols] += jnp.sum(ypre * dm, axis=0, keepdims=True)
            dyp = (dm * sc_ref[:, cols]).astype(BF16)
            dyp_ref[:, cols] = dyp
            dpooled = _dot(dyp, wp_ref[g], 1, 1)
            du_ref[:, cols] = -dpooled
            ext_ref[pl.ds(0, t), cols] = dpooled / counts[:, cols]
        st_ref[pl.ds(0, n), :] = ext_ref[pl.ds(0, n), :] + ext_ref[pl.ds(1, n), :]
        st_ref[pl.ds(0, n), 128:] = st_ref[pl.ds(0, n), 128:] + st_ref[pl.ds(2, n), 128:]
        st_ref[pl.ds(0, n), 256:] = st_ref[pl.ds(0, n), 256:] + st_ref[pl.ds(4, n), 256:]
        st_ref[pl.ds(0, n), 384:] = st_ref[pl.ds(0, n), 384:] + st_ref[pl.ds(8, n), 384:]
        ext_ref[pl.ds(t, POOL_HALO), :] = ext_ref[pl.ds(0, POOL_HALO), :]
        du_ref[...] += st_ref[pl.ds(0, t), :]

    rev = lambda w: pl.BlockSpec((t, w), lambda i: (n_tiles - 1 - i, 0))
    return pl.pallas_call(
        body, name="mix_out_bwd", grid=(n_tiles,),
        in_specs=[rev(D_MODEL)] + W_SPECS + [rev(POOL_WIDTH), _full((4, 128, 128)), _full((1, POOL_WIDTH))],
        out_specs=[pl.BlockSpec((N_Q_HEADS, t, 128), lambda i: (0, n_tiles - 1 - i, 0)), rev(POOL_WIDTH), rev(POOL_WIDTH),
                   _full((1, POOL_WIDTH))],
        out_shape=[jax.ShapeDtypeStruct((N_Q_HEADS, s_len, 128), BF16), jax.ShapeDtypeStruct((s_len, POOL_WIDTH), F32),
                   jax.ShapeDtypeStruct((s_len, POOL_WIDTH), BF16), jax.ShapeDtypeStruct((1, POOL_WIDTH), F32)],
        scratch_shapes=[pltpu.VMEM((D_MODEL, D_MODEL), BF16), pltpu.VMEM((t + POOL_HALO, POOL_WIDTH), F32),
                        pltpu.VMEM((t + POOL_HALO, POOL_WIDTH), F32), pltpu.SemaphoreType.DMA((N_CHIPS,))],
        compiler_params=_params(),
    )(dh1, *wts, pooled, wpool, pool_scale)


def _attn_bwd(qst, kn, vb, dost, bias_st, sink_st):
    s_len = kn.shape[0]

    def body(q_ref, kp_ref, kc_ref, vp_ref, vc_ref, do_ref, bias_ref, sink_ref, dq_ref, dk_ref, dv_ref, dbias_ref, dsink_ref):
        i = pl.program_id(0)

        @pl.when(i == 0)
        def _():
            dk_ref[...] = jnp.zeros_like(dk_ref)
            dv_ref[...] = jnp.zeros_like(dv_ref)
            dbias_ref[...] = jnp.zeros_like(dbias_ref)
            dsink_ref[...] = jnp.zeros_like(dsink_ref)

        q, k2, probs, p_sink = _band_softmax(q_ref, kp_ref, kc_ref, bias_ref, sink_ref, i)
        v2 = jnp.concatenate([vp_ref[...], vc_ref[...]], axis=0)
        do = do_ref[...].reshape(N_Q_HEADS * BLOCK, 128)
        dp = _dot(do, v2, 1, 1)
        dsum = jnp.sum(probs * dp, axis=-1, keepdims=True)
        dlog = probs * (dp - dsum)
        dsink_ref[...] -= p_sink * dsum
        dbias_ref[...] += dlog
        dlog_s = (dlog * (HEAD_DIM ** -0.5)).astype(BF16)
        dq = _dot(dlog_s, k2, 1, 0)
        dq_ref[...] = jnp.where(_head_lane_mask(), dq, 0.0).reshape(N_Q_HEADS, BLOCK, 128)
        dk2 = _dot(dlog_s, q, 0, 0)
        dv2 = _dot(probs.astype(BF16), do, 0, 0)
        prev_rows = pl.ds(pl.multiple_of(jnp.maximum(i - 1, 0) * BLOCK, BLOCK), BLOCK)
        cur_rows = pl.ds(pl.multiple_of(i * BLOCK, BLOCK), BLOCK)
        dk_ref[prev_rows, :] += dk2[:BLOCK]
        dk_ref[cur_rows, :] += dk2[BLOCK:]
        dv_ref[prev_rows, :] += dv2[:BLOCK]
        dv_ref[cur_rows, :] += dv2[BLOCK:]

    stacked, kv, consts = _attn_specs()
    return pl.pallas_call(
        body, name="attn_bwd", grid=(s_len // BLOCK,),
        in_specs=[stacked] + kv + kv + [stacked] + consts,
        out_specs=[stacked, _full((s_len, 128)), _full((s_len, 128)), _full((N_Q_HEADS * BLOCK, 2 * BLOCK)),
                   _full((N_Q_HEADS * BLOCK, 1))],
        out_shape=[jax.ShapeDtypeStruct((N_Q_HEADS, s_len, 128), F32), jax.ShapeDtypeStruct((s_len, 128), F32),
                   jax.ShapeDtypeStruct((s_len, 128), F32), jax.ShapeDtypeStruct((N_Q_HEADS * BLOCK, 2 * BLOCK), F32),
                   jax.ShapeDtypeStruct((N_Q_HEADS * BLOCK, 1), F32)],
        compiler_params=_params(),
    )(qst, kn, kn, vb, vb, dost, bias_st, sink_st)


def _small_pack(dg_attn, dg_ffn, dg_ple, dscale, dgq, dgk, dbias, dsink_rows, bucket, loss_v, dwpool):
    def body(ga_ref, gf_ref, gp_ref, sc_ref, gq_ref, gk_ref, db_ref, ds_ref, bucket_ref, loss_ref, wp_ref, out_ref):
        out_ref[pl.ds(0, SMALL["w_pool"]), :] = jnp.zeros((SMALL["w_pool"], 128), F32)
        for name, ref, n in (("g_attn", ga_ref, 8), ("g_ffn", gf_ref, 8), ("g_ple", gp_ref, 8), ("pool_scale", sc_ref, 4)):
            for k in range(n):
                out_ref[pl.ds(SMALL[name] + k, 1), :] = ref[:, 128 * k:128 * k + 128]
        for name, ref in (("g_q", gq_ref), ("g_k", gk_ref)):
            both = ref[...]
            out_ref[pl.ds(SMALL[name], 1), :] = both + pltpu.roll(both, 64, axis=1)
        out_ref[pl.ds(SMALL["loss"], 1), :] = loss_ref[...]
        bk = bucket_ref[...]
        rows = lax.broadcasted_iota(jnp.int32, (N_BUCKETS, 128), 0)
        lanes = lax.broadcasted_iota(jnp.int32, (N_BUCKETS, 128), 1)
        lane1 = lax.broadcasted_iota(jnp.int32, (1, 128), 1)
        rb = jnp.zeros((N_BUCKETS, 128), F32)
        sk = jnp.zeros((1, 128), F32)
        for h in range(N_Q_HEADS):
            band = db_ref[pl.ds(h * BLOCK, BLOCK), :]
            for b in range(N_BUCKETS):
                rb = jnp.where((rows == b) & (lanes == h), jnp.sum(jnp.where(bk == b, band, 0.0)), rb)
            sk = jnp.where(lane1 == h, jnp.sum(ds_ref[pl.ds(h * BLOCK, BLOCK), :]), sk)
        out_ref[pl.ds(SMALL["rel_bias"], N_BUCKETS), :] = rb
        out_ref[pl.ds(SMALL["sinks"], 1), :] = sk
        out_ref[pl.ds(SMALL["w_pool"], 512), :] = wp_ref[...].reshape(512, 128)

    return pl.pallas_call(
        body, name="small_pack", in_specs=[VMEM_WHOLE] * 11, out_specs=VMEM_WHOLE,
        out_shape=jax.ShapeDtypeStruct((SMALL_ROWS, 128), F32),
    )(dg_attn, dg_ffn, dg_ple, dscale, dgq, dgk, dbias, dsink_rows, bucket, loss_v, dwpool)


def _attn_in_bwd(dqst, zqk, dk, dv, du, x2, dh1, wts, g_attn, gq, gk):
    s_len = x2.shape[0]
    t = 512

    def body(dq_ref, zqk_ref, dk_ref, dv_ref, du_ref, x_ref, dh1_ref, sl_ref, lo_ref, me_ref, g_ref, gq_ref, gk_ref,
             dz_ref, dx_ref, dg_ref, dgq_ref, dgk_ref, w_ref, sems):
        @pl.when(pl.program_id(0) == 0)
        def _():
            _load_rows((sl_ref, lo_ref, me_ref), "inT", w_ref, sems)
            dg_ref[...] = jnp.zeros_like(dg_ref)
            dgq_ref[...] = jnp.zeros_like(dgq_ref)
            dgk_ref[...] = jnp.zeros_like(dgk_ref)

        lo = lax.broadcasted_iota(jnp.int32, (t, 128), 1) < 64
        for p in range(4):
            dqn = _from_stacked(dq_ref[2 * p], dq_ref[2 * p + 1], p // 2, lo)
            dq_raw, dgq = _pair_norm_bwd(zqk_ref[:, 128 * p:128 * p + 128], gq_ref[...], dqn, lo)
            dz_ref[:, 128 * p:128 * p + 128] = dq_raw.astype(BF16)
            dgq_ref[...] += dgq
        dk_raw, dgk = _pair_norm_bwd(zqk_ref[:, 512:640], gk_ref[...], dk_ref[...], lo)
        dgk_ref[...] += dgk
        dz_ref[:, 512:640] = dk_raw.astype(BF16)
        dz_ref[:, 640:768] = dv_ref[...].astype(BF16)
        dz_ref[:, 768:] = du_ref[...].astype(BF16)
        dx, dg = _rms_bwd(x_ref[...], g_ref[...], _dot(dz_ref[...], w_ref[...], 1, 0))
        dx_ref[...] = dh1_ref[...] + dx
        dg_ref[...] += dg

    row = lambda w: pl.BlockSpec((t, w), lambda i: (i, 0))
    return pl.pallas_call(
        body, name="attn_in_bwd", grid=(s_len // t,),
        in_specs=[pl.BlockSpec((N_Q_HEADS, t, 128), lambda i: (0, i, 0)), row(640), row(128), row(128), row(POOL_WIDTH),
                  row(D_MODEL), row(D_MODEL)] + W_SPECS + [_full((1, D_MODEL)), _full((1, 128)), _full((1, 128))],
        out_specs=[row(IN_WIDTH), row(D_MODEL), _full((1, D_MODEL)), _full((1, 128)), _full((1, 128))],
        out_shape=[jax.ShapeDtypeStruct((s_len, IN_WIDTH), BF16), jax.ShapeDtypeStruct((s_len, D_MODEL), F32),
                   jax.ShapeDtypeStruct((1, D_MODEL), F32), jax.ShapeDtypeStruct((1, 128), F32),
                   jax.ShapeDtypeStruct((1, 128), F32)],
        scratch_shapes=[pltpu.VMEM((IN_WIDTH, D_MODEL), BF16), pltpu.SemaphoreType.DMA((N_CHIPS,))],
        compiler_params=_params(),
    )(dqst, zqk, dk, dv, du, x2, dh1, *wts, g_attn, gq, gk)


def _dw(a, b, name):
    s_len, m = a.shape
    n_out = b.shape[1]
    tk = 512
    n_steps = s_len // tk
    tm = m // 2 if m > 1408 else m

    def body(a_ref, b_ref, o_ref, acc_ref):
        k = pl.program_id(1)

        @pl.when(k == 0)
        def _():
            acc_ref[...] = jnp.zeros_like(acc_ref)

        acc_ref[...] += _dot(a_ref[...].astype(BF16), b_ref[...].astype(BF16), 0, 0)

        @pl.when(k == n_steps - 1)
        def _():
            o_ref[...] = acc_ref[...].astype(BF16)

    return pl.pallas_call(
        body, name=name, grid=(m // tm, n_steps),
        in_specs=[pl.BlockSpec((tk, tm), lambda i, k: (k, i)), pl.BlockSpec((tk, n_out), lambda i, k: (k, 0))],
        out_specs=pl.BlockSpec((tm, n_out), lambda i, k: (i, 0)),
        out_shape=jax.ShapeDtypeStruct((m, n_out), BF16),
        scratch_shapes=[pltpu.VMEM((tm, n_out), F32)],
        compiler_params=_params(n_axes=2),
    )(a, b)


def _dw_pool(pooled, dyp):
    s_len = pooled.shape[0]
    tk = 512

    def body(a_ref, b_ref, o_ref):
        @pl.when(pl.program_id(0) == 0)
        def _():
            o_ref[...] = jnp.zeros_like(o_ref)

        for g in range(4):
            cols = slice(128 * g, 128 * g + 128)
            o_ref[g] += _dot(a_ref[:, cols], b_ref[:, cols], 0, 0)

    blk = pl.BlockSpec((tk, POOL_WIDTH), lambda k: (k, 0))
    return pl.pallas_call(
        body, name="dw_pool", grid=(s_len // tk,), in_specs=[blk, blk], out_specs=_full((4, 128, 128)),
        out_shape=jax.ShapeDtypeStruct((4, 128, 128), F32), compiler_params=_params(),
    )(pooled, dyp)


def _position():
    x, y, c = lax.axis_index("x"), lax.axis_index("y"), lax.axis_index("c")
    other_chips = [(1 - x, y), (x, 1 - y), (1 - x, 1 - y)]
    return x, y, c, other_chips


def _half(c):
    return pl.ds(pl.multiple_of(c * HALF_ROWS, 16), HALF_ROWS)


def _ag_weights(local_slab):
    def body(l_ref, g_ref, send, recv):
        x, y, c, chips = _position()
        me = 2 * x + y
        sibling = (x, y, 1 - c)
        mine, theirs = _half(c), _half(1 - c)

        def copy(k, chip_idx, rows, to, src=None):
            dst = g_ref.at[chip_idx, rows, :]
            return pltpu.make_async_remote_copy(src_ref=dst if src is None else src, dst_ref=dst, send_sem=send.at[k],
                                                recv_sem=recv.at[k], device_id=to, device_id_type=MESH)

        first = [copy(k, me, mine, (*chip, c), src=l_ref.at[mine, :]) for k, chip in enumerate(chips)]
        for cp in first:
            cp.start()
        passed = []
        for k, chip in enumerate(chips):
            idx = 2 * chip[0] + chip[1]
            copy(k, idx, mine, (x, y, c)).wait_recv()
            fwd = copy(3 + k, idx, mine, sibling)
            fwd.start()
            passed.append(fwd)
        for k, chip in enumerate(chips):
            copy(3 + k, 2 * chip[0] + chip[1], theirs, (x, y, c)).wait_recv()
        for cp in first + passed:
            cp.wait_send()

    return pl.pallas_call(
        body, name="ag_weights", in_specs=[ANY], out_specs=ANY,
        out_shape=jax.ShapeDtypeStruct((N_CHIPS, SLAB_ROWS, D_MODEL), BF16),
        scratch_shapes=[pltpu.SemaphoreType.DMA((6,)), pltpu.SemaphoreType.DMA((6,))],
    )(local_slab)


def _rs_swap_halves(partial):
    def body(p_ref, r_ref, send, recv):
        x, y, c, _ = _position()
        cp = pltpu.make_async_remote_copy(src_ref=p_ref.at[:, _half(1 - c), :], dst_ref=r_ref, send_sem=send, recv_sem=recv,
                                          device_id=(x, y, 1 - c), device_id_type=MESH)
        cp.start()
        cp.wait()

    return pl.pallas_call(
        body, name="rs_swap_halves", in_specs=[ANY], out_specs=ANY,
        out_shape=jax.ShapeDtypeStruct((N_CHIPS, HALF_ROWS, D_MODEL), BF16),
        scratch_shapes=[pltpu.SemaphoreType.DMA, pltpu.SemaphoreType.DMA],
    )(partial)


def _rs_add_halves(partial, other, core):
    t = HALF_ROWS // 2

    def body(core_ref, a_ref, b_ref, o_ref):
        o_ref[...] = (a_ref[...].astype(F32) + b_ref[...].astype(F32)).astype(BF16)

    steps = HALF_ROWS // t
    return pl.pallas_call(
        body, name="rs_add_halves",
        grid_spec=pltpu.PrefetchScalarGridSpec(
            num_scalar_prefetch=1, grid=(N_CHIPS, steps),
            in_specs=[pl.BlockSpec((1, t, D_MODEL), lambda j, i, core_ref: (j, core_ref[0] * steps + i, 0)),
                      pl.BlockSpec((1, t, D_MODEL), lambda j, i, core_ref: (j, i, 0))],
            out_specs=pl.BlockSpec((1, t, D_MODEL), lambda j, i, core_ref: (j, i, 0))),
        out_shape=jax.ShapeDtypeStruct((N_CHIPS, HALF_ROWS, D_MODEL), BF16),
        compiler_params=_params(n_axes=2),
    )(core, partial, other)


def _rs_exchange_chips(pre):
    def body(s_ref, r_ref, send, recv):
        x, y, c, chips = _position()

        def copy(k, chunk, to):
            return pltpu.make_async_remote_copy(src_ref=s_ref.at[chunk], dst_ref=r_ref.at[k], send_sem=send.at[k],
                                                recv_sem=recv.at[k], device_id=to, device_id_type=MESH)

        sends = [copy(k, 2 * chip[0] + chip[1], (*chip, c)) for k, chip in enumerate(chips)]
        for cp in sends:
            cp.start()
        for cp in sends:
            cp.wait()

    return pl.pallas_call(
        body, name="rs_exchange_chips", in_specs=[ANY], out_specs=ANY,
        out_shape=jax.ShapeDtypeStruct((3, HALF_ROWS, D_MODEL), BF16),
        scratch_shapes=[pltpu.SemaphoreType.DMA((3,)), pltpu.SemaphoreType.DMA((3,))],
    )(pre)


def _rs_sum_chips(pre, received, place):
    t = HALF_ROWS // 2
    steps = HALF_ROWS // t

    def body(place_ref, own_ref, r_ref, o_ref):
        acc = own_ref[0].astype(F32)
        for k in range(3):
            acc = acc + r_ref[k].astype(F32)
        o_ref[...] = acc

    return pl.pallas_call(
        body, name="rs_sum_chips",
        grid_spec=pltpu.PrefetchScalarGridSpec(
            num_scalar_prefetch=1, grid=(steps,),
            in_specs=[pl.BlockSpec((1, t, D_MODEL), lambda i, place_ref: (place_ref[0], i, 0)),
                      pl.BlockSpec((3, t, D_MODEL), lambda i, place_ref: (0, i, 0))],
            out_specs=pl.BlockSpec((t, D_MODEL), lambda i, place_ref: (place_ref[1] * steps + i, 0))),
        out_shape=jax.ShapeDtypeStruct((SLAB_ROWS, D_MODEL), F32),
        compiler_params=_params(),
    )(place, pre, received)


def _rs_finish(grad_slab, small):
    def body(f_ref, s_ref, g_ref, t_ref, send, recv, local_sem):
        del f_ref
        x, y, c, chips = _position()
        sibling = (x, y, 1 - c)

        def slot(px, py, pc):
            return t_ref.at[4 * px + 2 * py + pc]

        def copy(k, block, to, src=None):
            return pltpu.make_async_remote_copy(src_ref=slot(*block) if src is None else src, dst_ref=slot(*block),
                                                send_sem=send.at[k], recv_sem=recv.at[k], device_id=to, device_id_type=MESH)

        def half_copy(rows, to):
            return pltpu.make_async_remote_copy(src_ref=g_ref.at[rows, :], dst_ref=g_ref.at[rows, :], send_sem=send.at[7],
                                                recv_sem=recv.at[7], device_id=to, device_id_type=MESH)

        own_small = pltpu.make_async_copy(s_ref, slot(x, y, c), local_sem)
        own_small.start()
        to_sibling = half_copy(_half(c), sibling)
        to_sibling.start()
        first = [copy(0, (x, y, c), sibling, src=s_ref)]
        first += [copy(1 + k, (x, y, c), (*chip, c), src=s_ref) for k, chip in enumerate(chips)]
        for cp in first:
            cp.start()
        passed = []
        for k, chip in enumerate(chips):
            copy(1 + k, (*chip, c), (x, y, c)).wait_recv()
            fwd = copy(4 + k, (*chip, c), sibling)
            fwd.start()
            passed.append(fwd)
        copy(0, sibling, (x, y, c)).wait_recv()
        for k, chip in enumerate(chips):
            copy(4 + k, (*chip, 1 - c), (x, y, c)).wait_recv()
        half_copy(_half(1 - c), (x, y, c)).wait_recv()
        for cp in first + passed + [to_sibling]:
            cp.wait_send()
        own_small.wait()

    return pl.pallas_call(
        body, name="rs_finish", in_specs=[ANY, ANY], out_specs=[ANY, ANY], input_output_aliases={0: 0},
        out_shape=[jax.ShapeDtypeStruct((SLAB_ROWS, D_MODEL), F32), jax.ShapeDtypeStruct((N_DEV, SMALL_ROWS, 128), F32)],
        scratch_shapes=[pltpu.SemaphoreType.DMA((8,)), pltpu.SemaphoreType.DMA((8,)), pltpu.SemaphoreType.DMA],
    )(grad_slab, small)


def _adam_update(w, g, m, v):
    m_new = ADAM_B1 * m + (1.0 - ADAM_B1) * g
    v_new = ADAM_B2 * v + (1.0 - ADAM_B2) * (g * g)
    m_hat = m_new / (1.0 - ADAM_B1 ** ADAM_STEP)
    v_hat = v_new / (1.0 - ADAM_B2 ** ADAM_STEP)
    return -ADAM_LR * (m_hat / (jnp.sqrt(v_hat) + ADAM_EPS) + ADAM_WD * w), m_new, v_new


def _adamw(w, g, m, v, name):
    rows, cols = w.shape
    t = rows if rows % 256 else 256

    def body(w_ref, g_ref, m_ref, v_ref, d_ref, nm_ref, nv_ref):
        d_ref[...], nm_ref[...], nv_ref[...] = _adam_update(w_ref[...], g_ref[...], m_ref[...], v_ref[...])

    blk = pl.BlockSpec((t, cols), lambda i: (i, 0))
    shape = jax.ShapeDtypeStruct((rows, cols), F32)
    return pl.pallas_call(
        body, name=name, grid=(rows // t,), in_specs=[blk] * 4, out_specs=[blk] * 3, out_shape=[shape] * 3,
        compiler_params=_params(),
    )(w, g, m, v)


SMALL_PARAMS = [("g_attn", (1, D_MODEL), 8), ("g_q", (1, HEAD_DIM), None), ("g_k", (1, HEAD_DIM), None),
                ("sinks", (1, N_Q_HEADS), None), ("rel_bias", (N_BUCKETS, N_Q_HEADS), None), ("w_pool", (512, 128), None),
                ("pool_scale", (1, POOL_WIDTH), 4), ("g_ffn", (1, D_MODEL), 8), ("g_ple", (1, D_MODEL), 8)]


def _adamw_small(tables, wmv):
    n_par = len(SMALL_PARAMS)

    def body(*refs):
        t_ref = refs[0]
        ins = refs[1:1 + 3 * n_par]
        loss_ref = refs[1 + 3 * n_par]
        outs = refs[2 + 3 * n_par:-1]
        tot_ref = refs[-1]
        total = t_ref[0]
        for d in range(1, N_DEV):
            total = total + t_ref[d]
        tot_ref[...] = total
        loss_ref[...] = tot_ref[pl.ds(SMALL["loss"], 1), 0:1]
        for i, (name, shape, split) in enumerate(SMALL_PARAMS):
            g_ref, d_ref, nm_ref, nv_ref = outs[4 * i:4 * i + 4]
            row = SMALL[name]
            if split:
                for k in range(split):
                    g_ref[:, 128 * k:128 * k + 128] = tot_ref[pl.ds(row + k, 1), :]
            else:
                g_ref[...] = tot_ref[pl.ds(row, shape[0]), 0:shape[1]]
            w_ref, m_ref, v_ref = ins[3 * i:3 * i + 3]
            d_ref[...], nm_ref[...], nv_ref[...] = _adam_update(w_ref[...], g_ref[...], m_ref[...], v_ref[...])

    shapes = [jax.ShapeDtypeStruct((1, 1), F32)]
    for _, shape, _ in SMALL_PARAMS:
        shapes += [jax.ShapeDtypeStruct(shape, F32)] * 4
    flat = [a for triple in wmv for a in triple]
    res = pl.pallas_call(
        body, name="adamw_small", in_specs=[VMEM_WHOLE] * (1 + 3 * n_par), out_specs=[VMEM_WHOLE] * len(shapes),
        out_shape=shapes, scratch_shapes=[pltpu.VMEM((SMALL_ROWS, 128), F32)],
    )(tables, *flat)
    return res[0], [res[1 + 4 * i:5 + 4 * i] for i in range(n_par)]


def _local_grads(x2, p2, tgt, wts, g_attn_norm, g_q, g_k, attn_sinks, rel_bias, w_pool, pool_scale, g_ffn_norm, g_ple_norm):
    slab, local_slab, me = wts
    bucket = jnp.asarray(_bucket_table())
    gq = jnp.tile(g_q, (1, 2))
    gk = jnp.tile(g_k, (1, 2))
    wpool = w_pool[0].astype(BF16)
    plp = SLAB["plpT"][0]
    wplp_t = lax.dynamic_update_slice(slab[:, plp:, :], local_slab[None, plp:, :], (me[0], 0, 0)).reshape(D_MODEL, PLE_DIM)
    sink_st = jnp.repeat(attn_sinks[0], BLOCK)[:, None]
    bias_st = _bias_build(rel_bias.T, bucket)

    hn1, zqk, u, kn, vb, qst = _attn_in(x2, g_attn_norm, gq, gk, wts)
    ost = _attn_fwd(qst, kn, vb, bias_st, sink_st)
    pooled, mix, h1, hn2 = _mix_out(u, ost, x2, wts, wpool, pool_scale, g_ffn_norm)
    gate, up, h2 = _ffn_fwd(hn2, h1, wts)
    loss_v, dh2, dgl, dpp, hn3, dg_ple = _ple_loss(h2, p2, tgt, wts, wplp_t, g_ple_norm)

    dgate, dup, act, dh1, dg_ffn = _ffn_bwd(dh2, gate, up, h1, wts, g_ffn_norm)
    dost, du, dyp, dscale = _mix_out_bwd(dh1, wts, pooled, wpool, pool_scale)
    dqst, dk, dv, dbias, dsink_rows = _attn_bwd(qst, kn, vb, dost, bias_st, sink_st)
    dz, dx, dg_attn, dgq, dgk = _attn_in_bwd(dqst, zqk, dk, dv, du, x2, dh1, wts, g_attn_norm, gq, gk)

    chunks = [
        _dw(dz, hn1, "dw_in").reshape(N_CHIPS, -1, D_MODEL),
        _dw(mix, dh1, "dw_out").reshape(N_CHIPS, -1, D_MODEL),
        _dw(dgate, hn2, "dw_gate").reshape(N_CHIPS, -1, D_MODEL),
        _dw(dup, hn2, "dw_up").reshape(N_CHIPS, -1, D_MODEL),
        _dw(act, dh2, "dw_down").reshape(N_CHIPS, -1, D_MODEL),
        _dw(hn3, dgl, "dw_ple_gate").reshape(N_CHIPS, -1, D_MODEL),
        _dw(dpp, p2, "dw_ple_proj").reshape(N_CHIPS, -1, D_MODEL),
    ]
    partial = jnp.concatenate(chunks, axis=1)
    small = _small_pack(dg_attn, dg_ffn, dg_ple, dscale, dgq, dgk, dbias, dsink_rows, bucket, loss_v, _dw_pool(pooled, dyp))
    return dx, partial, small


def kernel(x, p, w_in, w_out, g_attn_norm, g_q, g_k, attn_sinks, rel_bias, w_pool, pool_scale, g_ffn_norm, w_gate, w_up, w_down, g_ple_norm, w_ple_gate, w_ple_proj, loss_target, m_w_in, m_w_out, m_g_attn_norm, m_g_q, m_g_k, m_attn_sinks, m_rel_bias, m_w_pool, m_pool_scale, m_g_ffn_norm, m_w_gate, m_w_up, m_w_down, m_g_ple_norm, m_w_ple_gate, m_w_ple_proj, v_w_in, v_w_out, v_g_attn_norm, v_g_q, v_g_k, v_attn_sinks, v_rel_bias, v_w_pool, v_pool_scale, v_g_ffn_norm, v_w_gate, v_w_up, v_w_down, v_g_ple_norm, v_w_ple_gate, v_w_ple_proj):
    core = lax.axis_index("c").astype(jnp.int32).reshape(1)
    me = (2 * lax.axis_index("x") + lax.axis_index("y")).astype(jnp.int32).reshape(1)

    local_slab = jnp.concatenate(
        [w_in[0].T, w_out[0], w_gate[0].T, w_up[0].T, w_down[0], w_ple_gate[0], w_ple_proj[0].T.reshape(64, D_MODEL)],
        axis=0).astype(BF16)
    wts = (_ag_weights(local_slab), local_slab, me)

    dx, partial, small = _local_grads(x[0], p[0, 0], loss_target[0], wts, g_attn_norm, g_q, g_k, attn_sinks, rel_bias,
                                      w_pool, pool_scale, g_ffn_norm, g_ple_norm)

    pre = _rs_add_halves(partial, _rs_swap_halves(partial), core)
    half_summed = _rs_sum_chips(pre, _rs_exchange_chips(pre), jnp.concatenate([me, core]))
    grad_slab, small_all = _rs_finish(half_summed, small)

    def rows(name):
        off, n_rows = SLAB[name]
        return grad_slab[off:off + n_rows]

    big = {
        "w_in": (w_in, m_w_in, v_w_in, rows("inT").T),
        "w_out": (w_out, m_w_out, v_w_out, rows("out")),
        "w_gate": (w_gate, m_w_gate, v_w_gate, rows("gateT").T),
        "w_up": (w_up, m_w_up, v_w_up, rows("upT").T),
        "w_down": (w_down, m_w_down, v_w_down, rows("down")),
        "w_ple_gate": (w_ple_gate, m_w_ple_gate, v_w_ple_gate, rows("plg")),
        "w_ple_proj": (w_ple_proj, m_w_ple_proj, v_w_ple_proj, rows("plpT").reshape(PLE_DIM, PLE_DIM).T),
    }
    small_params = {
        "g_attn_norm": (g_attn_norm, m_g_attn_norm, v_g_attn_norm), "g_q": (g_q, m_g_q, v_g_q), "g_k": (g_k, m_g_k, v_g_k),
        "attn_sinks": (attn_sinks, m_attn_sinks, v_attn_sinks), "rel_bias": (rel_bias, m_rel_bias, v_rel_bias),
        "w_pool": tuple(a.reshape(512, 128) for a in (w_pool, m_w_pool, v_w_pool)),
        "pool_scale": (pool_scale, m_pool_scale, v_pool_scale), "g_ffn_norm": (g_ffn_norm, m_g_ffn_norm, v_g_ffn_norm),
        "g_ple_norm": (g_ple_norm, m_g_ple_norm, v_g_ple_norm),
    }

    grads, deltas, new_ms, new_vs = {}, {}, {}, {}
    for name, (w, m, v, g2) in big.items():
        d, nm, nv = _adamw(w[0], g2, m[0], v[0], "adamw_" + name)
        grads[name], deltas[name], new_ms[name], new_vs[name] = g2[None], d[None], nm[None], nv[None]

    loss, small_out = _adamw_small(small_all, list(small_params.values()))
    for name, (g2, d, nm, nv) in zip(small_params, small_out):
        shape = w_pool.shape if name == "w_pool" else g2.shape
        grads[name], deltas[name], new_ms[name], new_vs[name] = (a.reshape(shape) for a in (g2, d, nm, nv))

    order = ["w_in", "w_out", "g_attn_norm", "g_q", "g_k", "attn_sinks", "rel_bias", "w_pool", "pool_scale", "g_ffn_norm",
             "w_gate", "w_up", "w_down", "g_ple_norm", "w_ple_gate", "w_ple_proj"]
    return (loss.reshape(()), dx[None], *[grads[n] for n in order], *[deltas[n] for n in order],
            *[new_ms[n] for n in order], *[new_vs[n] for n in order])
```

```python
import functools

import numpy as np
import jax
import jax.numpy as jnp
from jax import lax
from jax.experimental import pallas as pl
from jax.experimental.pallas import tpu as pltpu
from jax.experimental.pallas import tpu_sc as plsc

F32 = jnp.float32
BF16 = jnp.bfloat16
MESH = pl.DeviceIdType.MESH

D_MODEL = 1024
HEAD_DIM = 64
N_Q_HEADS = 8
ATTN_WIDTH = 512
KV_WIDTH = 128
POOL_WIDTH = 512
IN_WIDTH = 1280
D_FF = 2816
PLE_DIM = 256
FF_CHUNK = 1408
BLOCK = 128
N_BUCKETS = 32
MAX_DISTANCE = 128
POOL_SIZES = (2, 4, 8, 16)
EPS = 1e-6
NEG = -1e30
N_CHIPS = 4
N_DEV = 8

ADAM_LR = 0.001
ADAM_B1 = 0.9
ADAM_B2 = 0.999
ADAM_EPS = 1e-08
ADAM_WD = 0.01
ADAM_STEP = 10

SLAB = {"inT": (0, 320), "out": (320, 256), "gateT": (576, 704), "upT": (1280, 704), "down": (1984, 704),
        "plg": (2688, 256), "plpT": (2944, 64)}
SLAB_ROWS = 3008
HALF_ROWS = SLAB_ROWS // 2
EARLY_ROWS = 576
POOL_HALO = 24

SMALL = {"g_attn": 0, "g_ffn": 8, "g_ple": 16, "pool_scale": 24, "g_q": 28, "g_k": 29, "sinks": 30, "loss": 31,
         "rel_bias": 32, "w_pool": 64}
SMALL_ROWS = 576

VMEM_LIMIT_BIG = 60 * 1024 * 1024
VMEM_LIMIT = 48 * 1024 * 1024


def _params(vmem=VMEM_LIMIT, n_axes=1):
    return pltpu.CompilerParams(dimension_semantics=("arbitrary",) * n_axes, vmem_limit_bytes=vmem)


def _dot(a, b, ca, cb):
    return lax.dot_general(a, b, (((ca,), (cb,)), ((), ())), preferred_element_type=F32)


def _full(shape):
    return pl.BlockSpec(shape, lambda i: (0,) * len(shape))


ANY = pl.BlockSpec(memory_space=pl.ANY)
VMEM_WHOLE = pl.BlockSpec(memory_space=pltpu.VMEM)


W_SPECS = [ANY, ANY, ANY, pl.BlockSpec(memory_space=pltpu.SMEM)]


def _load_rows(w_refs, name, dst_ref, sems):
    early_ref, late_ref, local_ref, me_ref = w_refs
    off, rows = SLAB[name]
    slab_ref, slab_off = (early_ref, off) if off < EARLY_ROWS else (late_ref, off - EARLY_ROWS)
    me = me_ref[0]
    for phase in ("start", "wait"):
        for j in range(N_CHIPS):
            dst = dst_ref.at[pl.ds(j * rows, rows), :]
            theirs = pltpu.make_async_copy(slab_ref.at[j, pl.ds(slab_off, rows), :], dst, sems.at[j])
            own = pltpu.make_async_copy(local_ref.at[pl.ds(off, rows), :], dst, sems.at[j])

            @pl.when(me == j)
            def _():
                getattr(own, phase)()

            @pl.when(me != j)
            def _():
                getattr(theirs, phase)()


def _rms_fwd(x, g):
    r = lax.rsqrt(jnp.mean(x * x, axis=-1, keepdims=True) + EPS)
    return x * r * g


def _rms_bwd(x, g, dy):
    r = lax.rsqrt(jnp.mean(x * x, axis=-1, keepdims=True) + EPS)
    xn = x * r
    dyg = dy * g
    dx = r * (dyg - xn * jnp.mean(dyg * xn, axis=-1, keepdims=True))
    return dx, jnp.sum(dy * xn, axis=0, keepdims=True)


def _half_sum(v, lo):
    s_lo = jnp.sum(jnp.where(lo, v, 0.0), axis=-1, keepdims=True)
    s_hi = jnp.sum(jnp.where(lo, 0.0, v), axis=-1, keepdims=True)
    return jnp.where(lo, s_lo, s_hi)


def _pair_norm(zp, g, lo):
    r = lax.rsqrt(_half_sum(zp * zp, lo) * (1.0 / HEAD_DIM) + EPS)
    return zp * r * g


def _pair_norm_bwd(zp, g, dy, lo):
    r = lax.rsqrt(_half_sum(zp * zp, lo) * (1.0 / HEAD_DIM) + EPS)
    xn = zp * r
    dyg = dy * g
    dx = r * (dyg - xn * (_half_sum(dyg * xn, lo) * (1.0 / HEAD_DIM)))
    return dx, jnp.sum(dy * xn, axis=0, keepdims=True)


def _to_stacked(pair, group, lo):
    rolled = pltpu.roll(pair, 64, axis=1)
    if group == 0:
        return jnp.where(lo, pair, 0.0), jnp.where(lo, rolled, 0.0)
    return jnp.where(lo, 0.0, rolled), jnp.where(lo, 0.0, pair)


def _from_stacked(even, odd, group, lo):
    if group == 0:
        return jnp.where(lo, even, pltpu.roll(odd, 64, axis=1))
    return jnp.where(lo, pltpu.roll(even, 64, axis=1), odd)


def _sigmoid(v):
    return 1.0 / (1.0 + jnp.exp(-v))


def _pool_counts(tile, n_rows):
    t1 = tile * n_rows + lax.broadcasted_iota(jnp.int32, (n_rows, POOL_WIDTH), 0) + 1
    lane = lax.broadcasted_iota(jnp.int32, (n_rows, POOL_WIDTH), 1)
    win = jnp.where(lane < 128, 2, jnp.where(lane < 256, 4, jnp.where(lane < 384, 8, 16)))
    return jnp.minimum(t1, win).astype(F32)


def _attn_in(x2, g_attn, gq, gk, wts):
    s_len = x2.shape[0]
    t = 512

    def body(x_ref, g_ref, gq_ref, gk_ref, ge_ref, gl_ref, lo_ref, me_ref, hn_ref, zqk_ref, u_ref, kn_ref, v_ref, qst_ref, w_ref, sems):
        @pl.when(pl.program_id(0) == 0)
        def _():
            _load_rows((ge_ref, gl_ref, lo_ref, me_ref), "inT", w_ref, sems)

        hn = _rms_fwd(x_ref[...], g_ref[...]).astype(BF16)
        hn_ref[...] = hn
        z = _dot(hn, w_ref[...], 1, 1)
        zqk_ref[...] = z[:, :640]
        u_ref[...] = z[:, 768:]
        v_ref[...] = z[:, 640:768].astype(BF16)
        lo = lax.broadcasted_iota(jnp.int32, (t, 128), 1) < 64
        kn_ref[...] = _pair_norm(z[:, 512:640], gk_ref[...], lo).astype(BF16)
        for p in range(4):
            qn = _pair_norm(z[:, 128 * p:128 * p + 128], gq_ref[...], lo)
            even, odd = _to_stacked(qn, p // 2, lo)
            qst_ref[2 * p] = even.astype(BF16)
            qst_ref[2 * p + 1] = odd.astype(BF16)

    row = lambda w: pl.BlockSpec((t, w), lambda i: (i, 0))
    return pl.pallas_call(
        body, name="attn_in", grid=(s_len // t,),
        in_specs=[row(D_MODEL), _full((1, D_MODEL)), _full((1, 128)), _full((1, 128))] + W_SPECS,
        out_specs=[row(D_MODEL), row(640), row(POOL_WIDTH), row(128), row(128),
                   pl.BlockSpec((N_Q_HEADS, t, 128), lambda i: (0, i, 0))],
        out_shape=[jax.ShapeDtypeStruct((s_len, D_MODEL), BF16), jax.ShapeDtypeStruct((s_len, 640), F32),
                   jax.ShapeDtypeStruct((s_len, POOL_WIDTH), F32), jax.ShapeDtypeStruct((s_len, 128), BF16),
                   jax.ShapeDtypeStruct((s_len, 128), BF16), jax.ShapeDtypeStruct((N_Q_HEADS, s_len, 128), BF16)],
        scratch_shapes=[pltpu.VMEM((IN_WIDTH, D_MODEL), BF16), pltpu.SemaphoreType.DMA((N_CHIPS,))],
        compiler_params=_params(),
    )(x2, g_attn, gq, gk, *wts)


def _bucket_table():
    i_idx = np.arange(BLOCK)[:, None]
    j_idx = np.arange(2 * BLOCK)[None, :]
    d = BLOCK + i_idx - j_idx
    n = np.maximum(d, 0)
    max_exact = N_BUCKETS // 2
    nf = np.maximum(n, 1).astype(np.float64)
    large = max_exact + (np.log(nf / max_exact) / np.log(MAX_DISTANCE / max_exact) * (N_BUCKETS - max_exact)).astype(np.int64)
    large = np.minimum(large, N_BUCKETS - 1)
    bucket = np.where(n < max_exact, n, large)
    return np.where((d >= 0) & (d < BLOCK), bucket, -1).astype(np.int32)


def _bias_build(rel_bias_t, bucket):
    def body(rb_ref, bucket_ref, out_ref):
        bk = bucket_ref[...]
        for h in range(N_Q_HEADS):
            acc = jnp.full((BLOCK, 2 * BLOCK), NEG, F32)
            for b in range(N_BUCKETS):
                acc = jnp.where(bk == b, rb_ref[h, b], acc)
            out_ref[pl.ds(h * BLOCK, BLOCK), :] = acc

    return pl.pallas_call(
        body, name="bias_build",
        in_specs=[pl.BlockSpec(memory_space=pltpu.SMEM), pl.BlockSpec(memory_space=pltpu.VMEM)],
        out_specs=pl.BlockSpec(memory_space=pltpu.VMEM),
        out_shape=jax.ShapeDtypeStruct((N_Q_HEADS * BLOCK, 2 * BLOCK), F32),
    )(rel_bias_t, bucket)


def _band_softmax(q_ref, kp_ref, kc_ref, bias_ref, sink_ref, block):
    q = q_ref[...].reshape(N_Q_HEADS * BLOCK, 128)
    k2 = jnp.concatenate([kp_ref[...], kc_ref[...]], axis=0)
    s = _dot(q, k2, 1, 1) * (HEAD_DIM ** -0.5) + bias_ref[...]
    col = lax.broadcasted_iota(jnp.int32, s.shape, 1)
    s = jnp.where(col < jnp.where(block == 0, BLOCK, 0), NEG, s)
    sink = sink_ref[...]
    m = jnp.maximum(jnp.max(s, axis=-1, keepdims=True), sink)
    p = jnp.exp(s - m)
    e_sink = jnp.exp(sink - m)
    inv = 1.0 / (jnp.sum(p, axis=-1, keepdims=True) + e_sink)
    return q, k2, p * inv, e_sink * inv


def _attn_specs():
    prev = lambda i: (jnp.maximum(i - 1, 0), 0)
    cur = lambda i: (i, 0)
    stacked = pl.BlockSpec((N_Q_HEADS, BLOCK, 128), lambda i: (0, i, 0))
    kv = [pl.BlockSpec((BLOCK, 128), prev), pl.BlockSpec((BLOCK, 128), cur)]
    consts = [_full((N_Q_HEADS * BLOCK, 2 * BLOCK)), _full((N_Q_HEADS * BLOCK, 1))]
    return stacked, kv, consts


def _head_lane_mask():
    rows = lax.broadcasted_iota(jnp.int32, (N_Q_HEADS * BLOCK, 128), 0)
    lanes = lax.broadcasted_iota(jnp.int32, (N_Q_HEADS * BLOCK, 128), 1)
    return (rows < 4 * BLOCK) == (lanes < 64)


def _attn_fwd(qst, kn, vb, bias_st, sink_st):
    s_len = kn.shape[0]

    def body(q_ref, kp_ref, kc_ref, vp_ref, vc_ref, bias_ref, sink_ref, o_ref):
        _, _, probs, _ = _band_softmax(q_ref, kp_ref, kc_ref, bias_ref, sink_ref, pl.program_id(0))
        v2 = jnp.concatenate([vp_ref[...], vc_ref[...]], axis=0)
        o = _dot(probs.astype(BF16), v2, 1, 0)
        o_ref[...] = jnp.where(_head_lane_mask(), o, 0.0).astype(BF16).reshape(N_Q_HEADS, BLOCK, 128)

    stacked, kv, consts = _attn_specs()
    return pl.pallas_call(
        body, name="attn_fwd", grid=(s_len // BLOCK,),
        in_specs=[stacked] + kv + kv + consts, out_specs=stacked,
        out_shape=jax.ShapeDtypeStruct((N_Q_HEADS, s_len, 128), BF16),
        compiler_params=_params(),
    )(qst, kn, kn, vb, vb, bias_st, sink_st)


def _mix_out(u, ost, x2, wts, wpool, pool_scale, g_ffn):
    s_len = x2.shape[0]
    t = 512
    n = t + 16

    def body(u_ref, o_ref, x_ref, ge_ref, gl_ref, lo_ref, me_ref, wp_ref, sc_ref, g_ref, pooled_ref, mix_ref, h1_ref, hn_ref,
             w_ref, ext_ref, st_ref, sems):
        i = pl.program_id(0)

        @pl.when(i == 0)
        def _():
            _load_rows((ge_ref, gl_ref, lo_ref, me_ref), "out", w_ref, sems)
            ext_ref[...] = jnp.zeros_like(ext_ref)
            st_ref[...] = jnp.zeros_like(st_ref)

        u_tile = u_ref[...]
        ext_ref[pl.ds(POOL_HALO, t), :] = u_tile
        st_ref[pl.ds(8, n), :] = ext_ref[pl.ds(8, n), :] + ext_ref[pl.ds(7, n), :]
        st_ref[pl.ds(8, n), 128:] = st_ref[pl.ds(8, n), 128:] + st_ref[pl.ds(6, n), 128:]
        st_ref[pl.ds(8, n), 256:] = st_ref[pl.ds(8, n), 256:] + st_ref[pl.ds(4, n), 256:]
        st_ref[pl.ds(8, n), 384:] = st_ref[pl.ds(8, n), 384:] + st_ref[pl.ds(0, n), 384:]
        ext_ref[pl.ds(0, POOL_HALO), :] = ext_ref[pl.ds(t, POOL_HALO), :]
        pooled = (st_ref[pl.ds(POOL_HALO, t), :] / _pool_counts(i, t) - u_tile).astype(BF16)
        pooled_ref[...] = pooled
        for g in range(4):
            cols = slice(128 * g, 128 * g + 128)
            y = _dot(pooled[:, cols], wp_ref[g], 1, 0) * sc_ref[:, cols]
            mix_ref[:, ATTN_WIDTH + 128 * g:ATTN_WIDTH + 128 * g + 128] = y.astype(BF16)
        lo = lax.broadcasted_iota(jnp.int32, (t, 128), 1) < 64
        for p in range(4):
            a = _from_stacked(o_ref[2 * p].astype(F32), o_ref[2 * p + 1].astype(F32), p // 2, lo)
            mix_ref[:, 128 * p:128 * p + 128] = a.astype(BF16)
        h1 = x_ref[...] + _dot(mix_ref[...], w_ref[...], 1, 0)
        h1_ref[...] = h1
        hn_ref[...] = _rms_fwd(h1, g_ref[...]).astype(BF16)

    row = lambda w: pl.BlockSpec((t, w), lambda i: (i, 0))
    return pl.pallas_call(
        body, name="mix_out", grid=(s_len // t,),
        in_specs=[row(POOL_WIDTH), pl.BlockSpec((N_Q_HEADS, t, 128), lambda i: (0, i, 0)), row(D_MODEL)] + W_SPECS
        + [_full((4, 128, 128)), _full((1, POOL_WIDTH)), _full((1, D_MODEL))],
        out_specs=[row(POOL_WIDTH), row(D_MODEL), row(D_MODEL), row(D_MODEL)],
        out_shape=[jax.ShapeDtypeStruct((s_len, POOL_WIDTH), BF16), jax.ShapeDtypeStruct((s_len, D_MODEL), BF16),
                   jax.ShapeDtypeStruct((s_len, D_MODEL), F32), jax.ShapeDtypeStruct((s_len, D_MODEL), BF16)],
        scratch_shapes=[pltpu.VMEM((D_MODEL, D_MODEL), BF16), pltpu.VMEM((t + POOL_HALO, POOL_WIDTH), F32),
                        pltpu.VMEM((t + POOL_HALO, POOL_WIDTH), F32), pltpu.SemaphoreType.DMA((N_CHIPS,))],
        compiler_params=_params(),
    )(u, ost, x2, *wts, wpool, pool_scale, g_ffn)


def _ffn_fwd(hn2, h1, wts):
    s_len = h1.shape[0]
    t = 256

    def body(hn_ref, h1_ref, ge_ref, gl_ref, lo_ref, me_ref, gate_ref, up_ref, h2_ref, wg_ref, wu_ref, wd_ref, sems):
        @pl.when(pl.program_id(0) == 0)
        def _():
            w_refs = (ge_ref, gl_ref, lo_ref, me_ref)
            _load_rows(w_refs, "gateT", wg_ref, sems)
            _load_rows(w_refs, "upT", wu_ref, sems)
            _load_rows(w_refs, "down", wd_ref, sems)

        hn = hn_ref[...]
        h2 = h1_ref[...]
        for ch in range(D_FF // FF_CHUNK):
            rows = pl.ds(ch * FF_CHUNK, FF_CHUNK)
            cols = slice(ch * FF_CHUNK, (ch + 1) * FF_CHUNK)
            gate = _dot(hn, wg_ref[rows, :], 1, 1)
            up = _dot(hn, wu_ref[rows, :], 1, 1)
            gate_ref[:, cols] = gate
            up_ref[:, cols] = up
            act = (gate * _sigmoid(gate) * up).astype(BF16)
            h2 = h2 + _dot(act, wd_ref[rows, :], 1, 0)
        h2_ref[...] = h2

    row = lambda w: pl.BlockSpec((t, w), lambda i: (i, 0))
    return pl.pallas_call(
        body, name="ffn_fwd", grid=(s_len // t,),
        in_specs=[row(D_MODEL), row(D_MODEL)] + W_SPECS,
        out_specs=[row(D_FF), row(D_FF), row(D_MODEL)],
        out_shape=[jax.ShapeDtypeStruct((s_len, D_FF), F32), jax.ShapeDtypeStruct((s_len, D_FF), F32),
                   jax.ShapeDtypeStruct((s_len, D_MODEL), F32)],
        scratch_shapes=[pltpu.VMEM((D_FF, D_MODEL), BF16)] * 3 + [pltpu.SemaphoreType.DMA((N_CHIPS,))],
        compiler_params=_params(VMEM_LIMIT_BIG),
    )(hn2, h1, *wts)


def _ple_loss(h2, p2, tgt, wts, wplp_t, g_ple):
    s_len = h2.shape[0]
    t = 512
    n_tiles = s_len // t

    def body(h2_ref, p_ref, tgt_ref, ge_ref, gl_ref, lo_ref, me_ref, wp_ref, g_ref, loss_ref, dh2_ref, dgl_ref, dpp_ref, hn_ref,
             dg_ref, w_ref, loss_acc, sems):
        i = pl.program_id(0)

        @pl.when(i == 0)
        def _():
            _load_rows((ge_ref, gl_ref, lo_ref, me_ref), "plg", w_ref, sems)
            loss_acc[...] = jnp.zeros_like(loss_acc)
            dg_ref[...] = jnp.zeros_like(dg_ref)

        h2v = h2_ref[...]
        g = g_ref[...]
        hn = _rms_fwd(h2v, g).astype(BF16)
        hn_ref[...] = hn
        gate = _sigmoid(_dot(hn, w_ref[...], 1, 0))
        pp = _dot(p_ref[...].astype(BF16), wp_ref[...], 1, 1)
        err = h2v + gate * pp - tgt_ref[...]
        loss_acc[...] += jnp.sum(err * err, axis=0, keepdims=True)
        dy = err * (1.0 / D_MODEL)
        dpp_ref[...] = (dy * gate).astype(BF16)
        dgl = (dy * pp * gate * (1.0 - gate)).astype(BF16)
        dgl_ref[...] = dgl
        dx, dg = _rms_bwd(h2v, g, _dot(dgl, w_ref[...], 1, 1))
        dh2_ref[...] = dy + dx
        dg_ref[...] += dg

        @pl.when(i == n_tiles - 1)
        def _():
            total = jnp.sum(loss_acc[...], axis=-1, keepdims=True) * (0.5 / D_MODEL)
            loss_ref[...] = jnp.broadcast_to(total, loss_ref.shape)

    row = lambda w: pl.BlockSpec((t, w), lambda i: (i, 0))
    return pl.pallas_call(
        body, name="ple_loss", grid=(n_tiles,),
        in_specs=[row(D_MODEL), row(PLE_DIM), row(D_MODEL)] + W_SPECS + [_full((D_MODEL, PLE_DIM)), _full((1, D_MODEL))],
        out_specs=[_full((1, 128)), row(D_MODEL), row(D_MODEL), row(D_MODEL), row(D_MODEL), _full((1, D_MODEL))],
        out_shape=[jax.ShapeDtypeStruct((1, 128), F32), jax.ShapeDtypeStruct((s_len, D_MODEL), F32),
                   jax.ShapeDtypeStruct((s_len, D_MODEL), BF16), jax.ShapeDtypeStruct((s_len, D_MODEL), BF16),
                   jax.ShapeDtypeStruct((s_len, D_MODEL), BF16), jax.ShapeDtypeStruct((1, D_MODEL), F32)],
        scratch_shapes=[pltpu.VMEM((D_MODEL, D_MODEL), BF16), pltpu.VMEM((1, D_MODEL), F32),
                        pltpu.SemaphoreType.DMA((N_CHIPS,))],
        compiler_params=_params(),
    )(h2, p2, tgt, *wts, wplp_t, g_ple)


def _ffn_bwd(dh2, gate, up, h1, wts, g_ffn):
    s_len = h1.shape[0]
    t = 256

    def body(dh2_ref, gate_ref, up_ref, h1_ref, ge_ref, gl_ref, lo_ref, me_ref, g_ref, dgate_ref, dup_ref, act_ref, dh1_ref, dg_ref,
             wg_ref, wu_ref, wd_ref, sems):
        @pl.when(pl.program_id(0) == 0)
        def _():
            w_refs = (ge_ref, gl_ref, lo_ref, me_ref)
            _load_rows(w_refs, "gateT", wg_ref, sems)
            _load_rows(w_refs, "upT", wu_ref, sems)
            _load_rows(w_refs, "down", wd_ref, sems)
            dg_ref[...] = jnp.zeros_like(dg_ref)

        dh2v = dh2_ref[...]
        dh2b = dh2v.astype(BF16)
        dhn = jnp.zeros((t, D_MODEL), F32)
        for ch in range(D_FF // FF_CHUNK):
            rows = pl.ds(ch * FF_CHUNK, FF_CHUNK)
            cols = slice(ch * FF_CHUNK, (ch + 1) * FF_CHUNK)
            dact = _dot(dh2b, wd_ref[rows, :], 1, 1)
            gate_v = gate_ref[:, cols]
            up_v = up_ref[:, cols]
            sg = _sigmoid(gate_v)
            silu = gate_v * sg
            act_ref[:, cols] = (silu * up_v).astype(BF16)
            dup = (dact * silu).astype(BF16)
            dgate = (dact * up_v * (sg * (1.0 + gate_v * (1.0 - sg)))).astype(BF16)
            dup_ref[:, cols] = dup
            dgate_ref[:, cols] = dgate
            dhn = dhn + _dot(dgate, wg_ref[rows, :], 1, 0) + _dot(dup, wu_ref[rows, :], 1, 0)
        dx, dg = _rms_bwd(h1_ref[...], g_ref[...], dhn)
        dh1_ref[...] = dh2v + dx
        dg_ref[...] += dg

    row = lambda w: pl.BlockSpec((t, w), lambda i: (i, 0))
    return pl.pallas_call(
        body, name="ffn_bwd", grid=(s_len // t,),
        in_specs=[row(D_MODEL), row(D_FF), row(D_FF), row(D_MODEL)] + W_SPECS + [_full((1, D_MODEL))],
        out_specs=[row(D_FF), row(D_FF), row(D_FF), row(D_MODEL), _full((1, D_MODEL))],
        out_shape=[jax.ShapeDtypeStruct((s_len, D_FF), BF16), jax.ShapeDtypeStruct((s_len, D_FF), BF16),
                   jax.ShapeDtypeStruct((s_len, D_FF), BF16), jax.ShapeDtypeStruct((s_len, D_MODEL), F32),
                   jax.ShapeDtypeStruct((1, D_MODEL), F32)],
        scratch_shapes=[pltpu.VMEM((D_FF, D_MODEL), BF16)] * 3 + [pltpu.SemaphoreType.DMA((N_CHIPS,))],
        compiler_params=_params(VMEM_LIMIT_BIG),
    )(dh2, gate, up, h1, *wts, g_ffn)


def _mix_out_bwd(dh1, wts, pooled, wpool, pool_scale):
    s_len = dh1.shape[0]
    t = 512
    n = t + 16
    n_tiles = s_len // t

    def body(dh1_ref, ge_ref, gl_ref, lo_ref, me_ref, pooled_ref, wp_ref, sc_ref, dost_ref, du_ref, dyp_ref, dsc_ref,
             w_ref, ext_ref, st_ref, sems):
        i = pl.program_id(0)

        @pl.when(i == 0)
        def _():
            _load_rows((ge_ref, gl_ref, lo_ref, me_ref), "out", w_ref, sems)
            ext_ref[...] = jnp.zeros_like(ext_ref)
            st_ref[...] = jnp.zeros_like(st_ref)
            dsc_ref[...] = jnp.zeros_like(dsc_ref)

        dmix = _dot(dh1_ref[...].astype(BF16), w_ref[...], 1, 1)
        lo = lax.broadcasted_iota(jnp.int32, (t, 128), 1) < 64
        for p in range(4):
            even, odd = _to_stacked(dmix[:, 128 * p:128 * p + 128], p // 2, lo)
            dost_ref[2 * p] = even.astype(BF16)
            dost_ref[2 * p + 1] = odd.astype(BF16)
        pooled_v = pooled_ref[...]
        counts = _pool_counts(n_tiles - 1 - i, t)
        for g in range(4):
            cols = slice(128 * g, 128 * g + 128)
            dm = dmix[:, ATTN_WIDTH + 128 * g:ATTN_WIDTH + 128 * g + 128]
            ypre = _dot(pooled_v[:, cols], wp_ref[g], 1, 0)
            dsc_ref[:, cols] += jnp.sum(ypre * dm, axis=0, keepdims=True)
            dyp = (dm * sc_ref[:, cols]).astype(BF16)
            dyp_ref[:, cols] = dyp
            dpooled = _dot(dyp, wp_ref[g], 1, 1)
            du_ref[:, cols] = -dpooled
            ext_ref[pl.ds(0, t), cols] = dpooled / counts[:, cols]
        st_ref[pl.ds(0, n), :] = ext_ref[pl.ds(0, n), :] + ext_ref[pl.ds(1, n), :]
        st_ref[pl.ds(0, n), 128:] = st_ref[pl.ds(0, n), 128:] + st_ref[pl.ds(2, n), 128:]
        st_ref[pl.ds(0, n), 256:] = st_ref[pl.ds(0, n), 256:] + st_ref[pl.ds(4, n), 256:]
        st_ref[pl.ds(0, n), 384:] = st_ref[pl.ds(0, n), 384:] + st_ref[pl.ds(8, n), 384:]
        ext_ref[pl.ds(t, POOL_HALO), :] = ext_ref[pl.ds(0, POOL_HALO), :]
        du_ref[...] += st_ref[pl.ds(0, t), :]

    rev = lambda w: pl.BlockSpec((t, w), lambda i: (n_tiles - 1 - i, 0))
    return pl.pallas_call(
        body, name="mix_out_bwd", grid=(n_tiles,),
        in_specs=[rev(D_MODEL)] + W_SPECS + [rev(POOL_WIDTH), _full((4, 128, 128)), _full((1, POOL_WIDTH))],
        out_specs=[pl.BlockSpec((N_Q_HEADS, t, 128), lambda i: (0, n_tiles - 1 - i, 0)), rev(POOL_WIDTH), rev(POOL_WIDTH),
                   _full((1, POOL_WIDTH))],
        out_shape=[jax.ShapeDtypeStruct((N_Q_HEADS, s_len, 128), BF16), jax.ShapeDtypeStruct((s_len, POOL_WIDTH), F32),
                   jax.ShapeDtypeStruct((s_len, POOL_WIDTH), BF16), jax.ShapeDtypeStruct((1, POOL_WIDTH), F32)],
        scratch_shapes=[pltpu.VMEM((D_MODEL, D_MODEL), BF16), pltpu.VMEM((t + POOL_HALO, POOL_WIDTH), F32),
                        pltpu.VMEM((t + POOL_HALO, POOL_WIDTH), F32), pltpu.SemaphoreType.DMA((N_CHIPS,))],
        compiler_params=_params(),
    )(dh1, *wts, pooled, wpool, pool_scale)


def _attn_bwd(qst, kn, vb, dost, bias_st, sink_st):
    s_len = kn.shape[0]

    def body(q_ref, kp_ref, kc_ref, vp_ref, vc_ref, do_ref, bias_ref, sink_ref, dq_ref, dk_ref, dv_ref, dbias_ref, dsink_ref):
        i = pl.program_id(0)

        @pl.when(i == 0)
        def _():
            dk_ref[...] = jnp.zeros_like(dk_ref)
            dv_ref[...] = jnp.zeros_like(dv_ref)
            dbias_ref[...] = jnp.zeros_like(dbias_ref)
            dsink_ref[...] = jnp.zeros_like(dsink_ref)

        q, k2, probs, p_sink = _band_softmax(q_ref, kp_ref, kc_ref, bias_ref, sink_ref, i)
        v2 = jnp.concatenate([vp_ref[...], vc_ref[...]], axis=0)
        do = do_ref[...].reshape(N_Q_HEADS * BLOCK, 128)
        dp = _dot(do, v2, 1, 1)
        dsum = jnp.sum(probs * dp, axis=-1, keepdims=True)
        dlog = probs * (dp - dsum)
        dsink_ref[...] -= p_sink * dsum
        dbias_ref[...] += dlog
        dlog_s = (dlog * (HEAD_DIM ** -0.5)).astype(BF16)
        dq = _dot(dlog_s, k2, 1, 0)
        dq_ref[...] = jnp.where(_head_lane_mask(), dq, 0.0).reshape(N_Q_HEADS, BLOCK, 128)
        dk2 = _dot(dlog_s, q, 0, 0)
        dv2 = _dot(probs.astype(BF16), do, 0, 0)
        prev_rows = pl.ds(pl.multiple_of(jnp.maximum(i - 1, 0) * BLOCK, BLOCK), BLOCK)
        cur_rows = pl.ds(pl.multiple_of(i * BLOCK, BLOCK), BLOCK)
        dk_ref[prev_rows, :] += dk2[:BLOCK]
        dk_ref[cur_rows, :] += dk2[BLOCK:]
        dv_ref[prev_rows, :] += dv2[:BLOCK]
        dv_ref[cur_rows, :] += dv2[BLOCK:]

    stacked, kv, consts = _attn_specs()
    return pl.pallas_call(
        body, name="attn_bwd", grid=(s_len // BLOCK,),
        in_specs=[stacked] + kv + kv + [stacked] + consts,
        out_specs=[stacked, _full((s_len, 128)), _full((s_len, 128)), _full((N_Q_HEADS * BLOCK, 2 * BLOCK)),
                   _full((N_Q_HEADS * BLOCK, 1))],
        out_shape=[jax.ShapeDtypeStruct((N_Q_HEADS, s_len, 128), F32), jax.ShapeDtypeStruct((s_len, 128), F32),
                   jax.ShapeDtypeStruct((s_len, 128), F32), jax.ShapeDtypeStruct((N_Q_HEADS * BLOCK, 2 * BLOCK), F32),
                   jax.ShapeDtypeStruct((N_Q_HEADS * BLOCK, 1), F32)],
        compiler_params=_params(),
    )(qst, kn, kn, vb, vb, dost, bias_st, sink_st)


def _small_pack(dg_attn, dg_ffn, dg_ple, dscale, dgq, dgk, dbias, dsink_rows, bucket, loss_v, dwpool):
    def body(ga_ref, gf_ref, gp_ref, sc_ref, gq_ref, gk_ref, db_ref, ds_ref, bucket_ref, loss_ref, wp_ref, out_ref):
        out_ref[pl.ds(0, SMALL["w_pool"]), :] = jnp.zeros((SMALL["w_pool"], 128), F32)
        for name, ref, n in (("g_attn", ga_ref, 8), ("g_ffn", gf_ref, 8), ("g_ple", gp_ref, 8), ("pool_scale", sc_ref, 4)):
            for k in range(n):
                out_ref[pl.ds(SMALL[name] + k, 1), :] = ref[:, 128 * k:128 * k + 128]
        for name, ref in (("g_q", gq_ref), ("g_k", gk_ref)):
            both = ref[...]
            out_ref[pl.ds(SMALL[name], 1), :] = both + pltpu.roll(both, 64, axis=1)
        out_ref[pl.ds(SMALL["loss"], 1), :] = loss_ref[...]
        bk = bucket_ref[...]
        rows = lax.broadcasted_iota(jnp.int32, (N_BUCKETS, 128), 0)
        lanes = lax.broadcasted_iota(jnp.int32, (N_BUCKETS, 128), 1)
        lane1 = lax.broadcasted_iota(jnp.int32, (1, 128), 1)
        rb = jnp.zeros((N_BUCKETS, 128), F32)
        sk = jnp.zeros((1, 128), F32)
        for h in range(N_Q_HEADS):
            band = db_ref[pl.ds(h * BLOCK, BLOCK), :]
            for b in range(N_BUCKETS):
                rb = jnp.where((rows == b) & (lanes == h), jnp.sum(jnp.where(bk == b, band, 0.0)), rb)
            sk = jnp.where(lane1 == h, jnp.sum(ds_ref[pl.ds(h * BLOCK, BLOCK), :]), sk)
        out_ref[pl.ds(SMALL["rel_bias"], N_BUCKETS), :] = rb
        out_ref[pl.ds(SMALL["sinks"], 1), :] = sk
        out_ref[pl.ds(SMALL["w_pool"], 512), :] = wp_ref[...].reshape(512, 128)

    return pl.pallas_call(
        body, name="small_pack", in_specs=[VMEM_WHOLE] * 11, out_specs=VMEM_WHOLE,
        out_shape=jax.ShapeDtypeStruct((SMALL_ROWS, 128), F32),
    )(dg_attn, dg_ffn, dg_ple, dscale, dgq, dgk, dbias, dsink_rows, bucket, loss_v, dwpool)


def _attn_in_bwd(dqst, zqk, dk, dv, du, x2, dh1, wts, g_attn, gq, gk):
    s_len = x2.shape[0]
    t = 512

    def body(dq_ref, zqk_ref, dk_ref, dv_ref, du_ref, x_ref, dh1_ref, ge_ref, gl_ref, lo_ref, me_ref, g_ref, gq_ref, gk_ref,
             dz_ref, dx_ref, dg_ref, dgq_ref, dgk_ref, w_ref, sems):
        @pl.when(pl.program_id(0) == 0)
        def _():
            _load_rows((ge_ref, gl_ref, lo_ref, me_ref), "inT", w_ref, sems)
            dg_ref[...] = jnp.zeros_like(dg_ref)
            dgq_ref[...] = jnp.zeros_like(dgq_ref)
            dgk_ref[...] = jnp.zeros_like(dgk_ref)

        lo = lax.broadcasted_iota(jnp.int32, (t, 128), 1) < 64
        for p in range(4):
            dqn = _from_stacked(dq_ref[2 * p], dq_ref[2 * p + 1], p // 2, lo)
            dq_raw, dgq = _pair_norm_bwd(zqk_ref[:, 128 * p:128 * p + 128], gq_ref[...], dqn, lo)
            dz_ref[:, 128 * p:128 * p + 128] = dq_raw.astype(BF16)
            dgq_ref[...] += dgq
        dk_raw, dgk = _pair_norm_bwd(zqk_ref[:, 512:640], gk_ref[...], dk_ref[...], lo)
        dgk_ref[...] += dgk
        dz_ref[:, 512:640] = dk_raw.astype(BF16)
        dz_ref[:, 640:768] = dv_ref[...].astype(BF16)
        dz_ref[:, 768:] = du_ref[...].astype(BF16)
        dx, dg = _rms_bwd(x_ref[...], g_ref[...], _dot(dz_ref[...], w_ref[...], 1, 0))
        dx_ref[...] = dh1_ref[...] + dx
        dg_ref[...] += dg

    row = lambda w: pl.BlockSpec((t, w), lambda i: (i, 0))
    return pl.pallas_call(
        body, name="attn_in_bwd", grid=(s_len // t,),
        in_specs=[pl.BlockSpec((N_Q_HEADS, t, 128), lambda i: (0, i, 0)), row(640), row(128), row(128), row(POOL_WIDTH),
                  row(D_MODEL), row(D_MODEL)] + W_SPECS + [_full((1, D_MODEL)), _full((1, 128)), _full((1, 128))],
        out_specs=[row(IN_WIDTH), row(D_MODEL), _full((1, D_MODEL)), _full((1, 128)), _full((1, 128))],
        out_shape=[jax.ShapeDtypeStruct((s_len, IN_WIDTH), BF16), jax.ShapeDtypeStruct((s_len, D_MODEL), F32),
                   jax.ShapeDtypeStruct((1, D_MODEL), F32), jax.ShapeDtypeStruct((1, 128), F32),
                   jax.ShapeDtypeStruct((1, 128), F32)],
        scratch_shapes=[pltpu.VMEM((IN_WIDTH, D_MODEL), BF16), pltpu.SemaphoreType.DMA((N_CHIPS,))],
        compiler_params=_params(),
    )(dqst, zqk, dk, dv, du, x2, dh1, *wts, g_attn, gq, gk)


def _dw(a, b, name):
    s_len, m = a.shape
    n_out = b.shape[1]
    tk = 512
    n_steps = s_len // tk
    tm = m // 2 if m > 1408 else m

    def body(a_ref, b_ref, o_ref, acc_ref):
        k = pl.program_id(1)

        @pl.when(k == 0)
        def _():
            acc_ref[...] = jnp.zeros_like(acc_ref)

        acc_ref[...] += _dot(a_ref[...].astype(BF16), b_ref[...].astype(BF16), 0, 0)

        @pl.when(k == n_steps - 1)
        def _():
            o_ref[...] = acc_ref[...].astype(BF16)

    return pl.pallas_call(
        body, name=name, grid=(m // tm, n_steps),
        in_specs=[pl.BlockSpec((tk, tm), lambda i, k: (k, i)), pl.BlockSpec((tk, n_out), lambda i, k: (k, 0))],
        out_specs=pl.BlockSpec((tm, n_out), lambda i, k: (i, 0)),
        out_shape=jax.ShapeDtypeStruct((m, n_out), BF16),
        scratch_shapes=[pltpu.VMEM((tm, n_out), F32)],
        compiler_params=_params(n_axes=2),
    )(a, b)


def _dw_pool(pooled, dyp):
    s_len = pooled.shape[0]
    tk = 512

    def body(a_ref, b_ref, o_ref):
        @pl.when(pl.program_id(0) == 0)
        def _():
            o_ref[...] = jnp.zeros_like(o_ref)

        for g in range(4):
            cols = slice(128 * g, 128 * g + 128)
            o_ref[g] += _dot(a_ref[:, cols], b_ref[:, cols], 0, 0)

    blk = pl.BlockSpec((tk, POOL_WIDTH), lambda k: (k, 0))
    return pl.pallas_call(
        body, name="dw_pool", grid=(s_len // tk,), in_specs=[blk, blk], out_specs=_full((4, 128, 128)),
        out_shape=jax.ShapeDtypeStruct((4, 128, 128), F32), compiler_params=_params(),
    )(pooled, dyp)


def _position():
    x, y, c = lax.axis_index("x"), lax.axis_index("y"), lax.axis_index("c")
    other_chips = [(1 - x, y), (x, 1 - y), (1 - x, 1 - y)]
    return x, y, c, other_chips


def _half(c):
    return pl.ds(pl.multiple_of(c * HALF_ROWS, 16), HALF_ROWS)


def _ag_weights(local_slab, row0, n_rows, name, collective_id):
    half = n_rows // 2

    def body(l_ref, g_ref, send, recv):
        x, y, c, chips = _position()
        me = 2 * x + y
        sibling = (x, y, 1 - c)
        peers = [sibling] + [(*chip, c) for chip in chips]
        barrier = pltpu.get_barrier_semaphore()
        for peer in peers:
            pl.semaphore_signal(barrier, inc=1, device_id=peer, device_id_type=MESH)
        pl.semaphore_wait(barrier, len(peers))
        mine = pl.ds(pl.multiple_of(c * half, 16), half)
        theirs = pl.ds(pl.multiple_of((1 - c) * half, 16), half)

        def copy(k, chip_idx, rows, to, src=None):
            dst = g_ref.at[chip_idx, rows, :]
            return pltpu.make_async_remote_copy(src_ref=dst if src is None else src, dst_ref=dst, send_sem=send.at[k],
                                                recv_sem=recv.at[k], device_id=to, device_id_type=MESH)

        own_rows = l_ref.at[pl.ds(pl.multiple_of(row0 + c * half, 16), half), :]
        first = [copy(k, me, mine, (*chip, c), src=own_rows) for k, chip in enumerate(chips)]
        for cp in first:
            cp.start()
        passed = []
        for k, chip in enumerate(chips):
            idx = 2 * chip[0] + chip[1]
            copy(k, idx, mine, (x, y, c)).wait_recv()
            fwd = copy(3 + k, idx, mine, sibling)
            fwd.start()
            passed.append(fwd)
        for k, chip in enumerate(chips):
            copy(3 + k, 2 * chip[0] + chip[1], theirs, (x, y, c)).wait_recv()
        for cp in first + passed:
            cp.wait_send()

    return pl.kernel(
        body, out_type=jax.ShapeDtypeStruct((N_CHIPS, n_rows, D_MODEL), BF16),
        mesh=plsc.ScalarSubcoreMesh(axis_name="sequencer", num_cores=1), name=name,
        scratch_types=[pltpu.SemaphoreType.DMA((6,)), pltpu.SemaphoreType.DMA((6,))],
        compiler_params=pltpu.CompilerParams(collective_id=collective_id),
    )(local_slab)


def _rs_swap_halves(partial):
    def body(p_ref, r_ref, send, recv):
        x, y, c, _ = _position()
        cp = pltpu.make_async_remote_copy(src_ref=p_ref.at[:, _half(1 - c), :], dst_ref=r_ref, send_sem=send, recv_sem=recv,
                                          device_id=(x, y, 1 - c), device_id_type=MESH)
        cp.start()
        cp.wait()

    return pl.pallas_call(
        body, name="rs_swap_halves", in_specs=[ANY], out_specs=ANY,
        out_shape=jax.ShapeDtypeStruct((N_CHIPS, HALF_ROWS, D_MODEL), BF16),
        scratch_shapes=[pltpu.SemaphoreType.DMA, pltpu.SemaphoreType.DMA],
    )(partial)


def _rs_add_halves(partial, other, core):
    t = HALF_ROWS // 2

    def body(core_ref, a_ref, b_ref, o_ref):
        o_ref[...] = (a_ref[...].astype(F32) + b_ref[...].astype(F32)).astype(BF16)

    steps = HALF_ROWS // t
    return pl.pallas_call(
        body, name="rs_add_halves",
        grid_spec=pltpu.PrefetchScalarGridSpec(
            num_scalar_prefetch=1, grid=(N_CHIPS, steps),
            in_specs=[pl.BlockSpec((1, t, D_MODEL), lambda j, i, core_ref: (j, core_ref[0] * steps + i, 0)),
                      pl.BlockSpec((1, t, D_MODEL), lambda j, i, core_ref: (j, i, 0))],
            out_specs=pl.BlockSpec((1, t, D_MODEL), lambda j, i, core_ref: (j, i, 0))),
        out_shape=jax.ShapeDtypeStruct((N_CHIPS, HALF_ROWS, D_MODEL), BF16),
        compiler_params=_params(n_axes=2),
    )(core, partial, other)


def _rs_exchange_chips(pre):
    def body(s_ref, r_ref, send, recv):
        x, y, c, chips = _position()

        def copy(k, chunk, to):
            return pltpu.make_async_remote_copy(src_ref=s_ref.at[chunk], dst_ref=r_ref.at[k], send_sem=send.at[k],
                                                recv_sem=recv.at[k], device_id=to, device_id_type=MESH)

        sends = [copy(k, 2 * chip[0] + chip[1], (*chip, c)) for k, chip in enumerate(chips)]
        for cp in sends:
            cp.start()
        for cp in sends:
            cp.wait()

    return pl.pallas_call(
        body, name="rs_exchange_chips", in_specs=[ANY], out_specs=ANY,
        out_shape=jax.ShapeDtypeStruct((3, HALF_ROWS, D_MODEL), BF16),
        scratch_shapes=[pltpu.SemaphoreType.DMA((3,)), pltpu.SemaphoreType.DMA((3,))],
    )(pre)


def _rs_sum_chips(pre, received, place):
    t = HALF_ROWS // 2
    steps = HALF_ROWS // t

    def body(place_ref, own_ref, r_ref, o_ref):
        acc = own_ref[0].astype(F32)
        for k in range(3):
            acc = acc + r_ref[k].astype(F32)
        o_ref[...] = acc

    return pl.pallas_call(
        body, name="rs_sum_chips",
        grid_spec=pltpu.PrefetchScalarGridSpec(
            num_scalar_prefetch=1, grid=(steps,),
            in_specs=[pl.BlockSpec((1, t, D_MODEL), lambda i, place_ref: (place_ref[0], i, 0)),
                      pl.BlockSpec((3, t, D_MODEL), lambda i, place_ref: (0, i, 0))],
            out_specs=pl.BlockSpec((t, D_MODEL), lambda i, place_ref: (place_ref[1] * steps + i, 0))),
        out_shape=jax.ShapeDtypeStruct((SLAB_ROWS, D_MODEL), F32),
        compiler_params=_params(),
    )(place, pre, received)


def _rs_finish(grad_slab, small):
    def body(f_ref, s_ref, g_ref, t_ref, send, recv, local_sem):
        del f_ref
        x, y, c, chips = _position()
        sibling = (x, y, 1 - c)

        def slot(px, py, pc):
            return t_ref.at[4 * px + 2 * py + pc]

        def copy(k, block, to, src=None):
            return pltpu.make_async_remote_copy(src_ref=slot(*block) if src is None else src, dst_ref=slot(*block),
                                                send_sem=send.at[k], recv_sem=recv.at[k], device_id=to, device_id_type=MESH)

        def half_copy(rows, to):
            return pltpu.make_async_remote_copy(src_ref=g_ref.at[rows, :], dst_ref=g_ref.at[rows, :], send_sem=send.at[7],
                                                recv_sem=recv.at[7], device_id=to, device_id_type=MESH)

        own_small = pltpu.make_async_copy(s_ref, slot(x, y, c), local_sem)
        own_small.start()
        to_sibling = half_copy(_half(c), sibling)
        to_sibling.start()
        first = [copy(0, (x, y, c), sibling, src=s_ref)]
        first += [copy(1 + k, (x, y, c), (*chip, c), src=s_ref) for k, chip in enumerate(chips)]
        for cp in first:
            cp.start()
        passed = []
        for k, chip in enumerate(chips):
            copy(1 + k, (*chip, c), (x, y, c)).wait_recv()
            fwd = copy(4 + k, (*chip, c), sibling)
            fwd.start()
            passed.append(fwd)
        copy(0, sibling, (x, y, c)).wait_recv()
        for k, chip in enumerate(chips):
            copy(4 + k, (*chip, 1 - c), (x, y, c)).wait_recv()
        half_copy(_half(1 - c), (x, y, c)).wait_recv()
        for cp in first + passed + [to_sibling]:
            cp.wait_send()
        own_small.wait()

    return pl.pallas_call(
        body, name="rs_finish", in_specs=[ANY, ANY], out_specs=[ANY, ANY], input_output_aliases={0: 0},
        out_shape=[jax.ShapeDtypeStruct((SLAB_ROWS, D_MODEL), F32), jax.ShapeDtypeStruct((N_DEV, SMALL_ROWS, 128), F32)],
        scratch_shapes=[pltpu.SemaphoreType.DMA((8,)), pltpu.SemaphoreType.DMA((8,)), pltpu.SemaphoreType.DMA],
    )(grad_slab, small)


def _adam_update(w, g, m, v):
    m_new = ADAM_B1 * m + (1.0 - ADAM_B1) * g
    v_new = ADAM_B2 * v + (1.0 - ADAM_B2) * (g * g)
    m_hat = m_new / (1.0 - ADAM_B1 ** ADAM_STEP)
    v_hat = v_new / (1.0 - ADAM_B2 ** ADAM_STEP)
    return -ADAM_LR * (m_hat / (jnp.sqrt(v_hat) + ADAM_EPS) + ADAM_WD * w), m_new, v_new


def _adamw(w, g, m, v, name):
    rows, cols = w.shape
    t = rows if rows % 256 else 256

    def body(w_ref, g_ref, m_ref, v_ref, d_ref, nm_ref, nv_ref):
        d_ref[...], nm_ref[...], nv_ref[...] = _adam_update(w_ref[...], g_ref[...], m_ref[...], v_ref[...])

    blk = pl.BlockSpec((t, cols), lambda i: (i, 0))
    shape = jax.ShapeDtypeStruct((rows, cols), F32)
    return pl.pallas_call(
        body, name=name, grid=(rows // t,), in_specs=[blk] * 4, out_specs=[blk] * 3, out_shape=[shape] * 3,
        compiler_params=_params(),
    )(w, g, m, v)


SMALL_PARAMS = [("g_attn", (1, D_MODEL), 8), ("g_q", (1, HEAD_DIM), None), ("g_k", (1, HEAD_DIM), None),
                ("sinks", (1, N_Q_HEADS), None), ("rel_bias", (N_BUCKETS, N_Q_HEADS), None), ("w_pool", (512, 128), None),
                ("pool_scale", (1, POOL_WIDTH), 4), ("g_ffn", (1, D_MODEL), 8), ("g_ple", (1, D_MODEL), 8)]


def _adamw_small(tables, wmv):
    n_par = len(SMALL_PARAMS)

    def body(*refs):
        t_ref = refs[0]
        ins = refs[1:1 + 3 * n_par]
        loss_ref = refs[1 + 3 * n_par]
        outs = refs[2 + 3 * n_par:-1]
        tot_ref = refs[-1]
        total = t_ref[0]
        for d in range(1, N_DEV):
            total = total + t_ref[d]
        tot_ref[...] = total
        loss_ref[...] = tot_ref[pl.ds(SMALL["loss"], 1), 0:1]
        for i, (name, shape, split) in enumerate(SMALL_PARAMS):
            g_ref, d_ref, nm_ref, nv_ref = outs[4 * i:4 * i + 4]
            row = SMALL[name]
            if split:
                for k in range(split):
                    g_ref[:, 128 * k:128 * k + 128] = tot_ref[pl.ds(row + k, 1), :]
            else:
                g_ref[...] = tot_ref[pl.ds(row, shape[0]), 0:shape[1]]
            w_ref, m_ref, v_ref = ins[3 * i:3 * i + 3]
            d_ref[...], nm_ref[...], nv_ref[...] = _adam_update(w_ref[...], g_ref[...], m_ref[...], v_ref[...])

    shapes = [jax.ShapeDtypeStruct((1, 1), F32)]
    for _, shape, _ in SMALL_PARAMS:
        shapes += [jax.ShapeDtypeStruct(shape, F32)] * 4
    flat = [a for triple in wmv for a in triple]
    res = pl.pallas_call(
        body, name="adamw_small", in_specs=[VMEM_WHOLE] * (1 + 3 * n_par), out_specs=[VMEM_WHOLE] * len(shapes),
        out_shape=shapes, scratch_shapes=[pltpu.VMEM((SMALL_ROWS, 128), F32)],
    )(tables, *flat)
    return res[0], [res[1 + 4 * i:5 + 4 * i] for i in range(n_par)]


def _local_grads(x2, p2, tgt, wts, g_attn_norm, g_q, g_k, attn_sinks, rel_bias, w_pool, pool_scale, g_ffn_norm, g_ple_norm):
    _, late, local_slab, me = wts
    bucket = jnp.asarray(_bucket_table())
    gq = jnp.tile(g_q, (1, 2))
    gk = jnp.tile(g_k, (1, 2))
    wpool = w_pool[0].astype(BF16)
    plp = SLAB["plpT"][0]
    wplp_t = lax.dynamic_update_slice(late[:, plp - EARLY_ROWS:, :], local_slab[None, plp:, :],
                                      (me[0], 0, 0)).reshape(D_MODEL, PLE_DIM)
    sink_st = jnp.repeat(attn_sinks[0], BLOCK)[:, None]
    bias_st = _bias_build(rel_bias.T, bucket)

    hn1, zqk, u, kn, vb, qst = _attn_in(x2, g_attn_norm, gq, gk, wts)
    ost = _attn_fwd(qst, kn, vb, bias_st, sink_st)
    pooled, mix, h1, hn2 = _mix_out(u, ost, x2, wts, wpool, pool_scale, g_ffn_norm)
    gate, up, h2 = _ffn_fwd(hn2, h1, wts)
    loss_v, dh2, dgl, dpp, hn3, dg_ple = _ple_loss(h2, p2, tgt, wts, wplp_t, g_ple_norm)

    dgate, dup, act, dh1, dg_ffn = _ffn_bwd(dh2, gate, up, h1, wts, g_ffn_norm)
    dost, du, dyp, dscale = _mix_out_bwd(dh1, wts, pooled, wpool, pool_scale)
    dqst, dk, dv, dbias, dsink_rows = _attn_bwd(qst, kn, vb, dost, bias_st, sink_st)
    dz, dx, dg_attn, dgq, dgk = _attn_in_bwd(dqst, zqk, dk, dv, du, x2, dh1, wts, g_attn_norm, gq, gk)

    chunks = [
        _dw(dz, hn1, "dw_in").reshape(N_CHIPS, -1, D_MODEL),
        _dw(mix, dh1, "dw_out").reshape(N_CHIPS, -1, D_MODEL),
        _dw(dgate, hn2, "dw_gate").reshape(N_CHIPS, -1, D_MODEL),
        _dw(dup, hn2, "dw_up").reshape(N_CHIPS, -1, D_MODEL),
        _dw(act, dh2, "dw_down").reshape(N_CHIPS, -1, D_MODEL),
        _dw(hn3, dgl, "dw_ple_gate").reshape(N_CHIPS, -1, D_MODEL),
        _dw(dpp, p2, "dw_ple_proj").reshape(N_CHIPS, -1, D_MODEL),
    ]
    partial = jnp.concatenate(chunks, axis=1)
    small = _small_pack(dg_attn, dg_ffn, dg_ple, dscale, dgq, dgk, dbias, dsink_rows, bucket, loss_v, _dw_pool(pooled, dyp))
    return dx, partial, small


def kernel(x, p, w_in, w_out, g_attn_norm, g_q, g_k, attn_sinks, rel_bias, w_pool, pool_scale, g_ffn_norm, w_gate, w_up, w_down, g_ple_norm, w_ple_gate, w_ple_proj, loss_target, m_w_in, m_w_out, m_g_attn_norm, m_g_q, m_g_k, m_attn_sinks, m_rel_bias, m_w_pool, m_pool_scale, m_g_ffn_norm, m_w_gate, m_w_up, m_w_down, m_g_ple_norm, m_w_ple_gate, m_w_ple_proj, v_w_in, v_w_out, v_g_attn_norm, v_g_q, v_g_k, v_attn_sinks, v_rel_bias, v_w_pool, v_pool_scale, v_g_ffn_norm, v_w_gate, v_w_up, v_w_down, v_g_ple_norm, v_w_ple_gate, v_w_ple_proj):
    core = lax.axis_index("c").astype(jnp.int32).reshape(1)
    me = (2 * lax.axis_index("x") + lax.axis_index("y")).astype(jnp.int32).reshape(1)

    local_slab = jnp.concatenate(
        [w_in[0].T, w_out[0], w_gate[0].T, w_up[0].T, w_down[0], w_ple_gate[0], w_ple_proj[0].T.reshape(64, D_MODEL)],
        axis=0).astype(BF16)
    wts = (_ag_weights(local_slab, 0, EARLY_ROWS, "ag_early", 1),
           _ag_weights(local_slab, EARLY_ROWS, SLAB_ROWS - EARLY_ROWS, "ag_late", 2), local_slab, me)

    dx, partial, small = _local_grads(x[0], p[0, 0], loss_target[0], wts, g_attn_norm, g_q, g_k, attn_sinks, rel_bias,
                                      w_pool, pool_scale, g_ffn_norm, g_ple_norm)

    pre = _rs_add_halves(partial, _rs_swap_halves(partial), core)
    half_summed = _rs_sum_chips(pre, _rs_exchange_chips(pre), jnp.concatenate([me, core]))
    grad_slab, small_all = _rs_finish(half_summed, small)

    def rows(name):
        off, n_rows = SLAB[name]
        return grad_slab[off:off + n_rows]

    big = {
        "w_in": (w_in, m_w_in, v_w_in, rows("inT").T),
        "w_out": (w_out, m_w_out, v_w_out, rows("out")),
        "w_gate": (w_gate, m_w_gate, v_w_gate, rows("gateT").T),
        "w_up": (w_up, m_w_up, v_w_up, rows("upT").T),
        "w_down": (w_down, m_w_down, v_w_down, rows("down")),
        "w_ple_gate": (w_ple_gate, m_w_ple_gate, v_w_ple_gate, rows("plg")),
        "w_ple_proj": (w_ple_proj, m_w_ple_proj, v_w_ple_proj, rows("plpT").reshape(PLE_DIM, PLE_DIM).T),
    }
    small_params = {
        "g_attn_norm": (g_attn_norm, m_g_attn_norm, v_g_attn_norm), "g_q": (g_q, m_g_q, v_g_q), "g_k": (g_k, m_g_k, v_g_k),
        "attn_sinks": (attn_sinks, m_attn_sinks, v_attn_sinks), "rel_bias": (rel_bias, m_rel_bias, v_rel_bias),
        "w_pool": tuple(a.reshape(512, 128) for a in (w_pool, m_w_pool, v_w_pool)),
        "pool_scale": (pool_scale, m_pool_scale, v_pool_scale), "g_ffn_norm": (g_ffn_norm, m_g_ffn_norm, v_g_ffn_norm),
        "g_ple_norm": (g_ple_norm, m_g_ple_norm, v_g_ple_norm),
    }

    grads, deltas, new_ms, new_vs = {}, {}, {}, {}
    for name, (w, m, v, g2) in big.items():
        d, nm, nv = _adamw(w[0], g2, m[0], v[0], "adamw_" + name)
        grads[name], deltas[name], new_ms[name], new_vs[name] = g2[None], d[None], nm[None], nv[None]

    loss, small_out = _adamw_small(small_all, list(small_params.values()))
    for name, (g2, d, nm, nv) in zip(small_params, small_out):
        shape = w_pool.shape if name == "w_pool" else g2.shape
        grads[name], deltas[name], new_ms[name], new_vs[name] = (a.reshape(shape) for a in (g2, d, nm, nv))

    order = ["w_in", "w_out", "g_attn_norm", "g_q", "g_k", "attn_sinks", "rel_bias", "w_pool", "pool_scale", "g_ffn_norm",
             "w_gate", "w_up", "w_down", "g_ple_norm", "w_ple_gate", "w_ple_proj"]
    return (loss.reshape(()), dx[None], *[grads[n] for n in order], *[deltas[n] for n in order],
            *[new_ms[n] for n in order], *[new_vs[n] for n in order])
```

```python
import functools

import numpy as np
import jax
import jax.numpy as jnp
from jax import lax
from jax.experimental import pallas as pl
from jax.experimental.pallas import tpu as pltpu
from jax.experimental.pallas import tpu_sc as plsc

F32 = jnp.float32
BF16 = jnp.bfloat16
MESH = pl.DeviceIdType.MESH

D_MODEL = 1024
HEAD_DIM = 64
N_Q_HEADS = 8
ATTN_WIDTH = 512
KV_WIDTH = 128
POOL_WIDTH = 512
IN_WIDTH = 1280
D_FF = 2816
PLE_DIM = 256
FF_CHUNK = 1408
BLOCK = 128
N_BUCKETS = 32
MAX_DISTANCE = 128
POOL_SIZES = (2, 4, 8, 16)
EPS = 1e-6
NEG = -1e30
N_CHIPS = 4
N_DEV = 8

ADAM_LR = 0.001
ADAM_B1 = 0.9
ADAM_B2 = 0.999
ADAM_EPS = 1e-08
ADAM_WD = 0.01
ADAM_STEP = 10

SLAB = {"inT": (0, 320), "out": (320, 256), "gateT": (576, 704), "upT": (1280, 704), "down": (1984, 704),
        "plg": (2688, 256), "plp": (2944, 64)}
SLAB_ROWS = 3008
HALF_ROWS = SLAB_ROWS // 2
EARLY_ROWS = 576
POOL_HALO = 24

SMALL = {"g_attn": 0, "g_ffn": 8, "g_ple": 16, "pool_scale": 24, "g_q": 28, "g_k": 29, "sinks": 30, "loss": 31,
         "rel_bias": 32, "w_pool": 64}
SMALL_ROWS = 576

VMEM_LIMIT_BIG = 60 * 1024 * 1024
VMEM_LIMIT = 48 * 1024 * 1024


def _params(vmem=VMEM_LIMIT, n_axes=1):
    return pltpu.CompilerParams(dimension_semantics=("arbitrary",) * n_axes, vmem_limit_bytes=vmem)


def _dot(a, b, ca, cb):
    return lax.dot_general(a, b, (((ca,), (cb,)), ((), ())), preferred_element_type=F32)


def _full(shape):
    return pl.BlockSpec(shape, lambda i: (0,) * len(shape))


ANY = pl.BlockSpec(memory_space=pl.ANY)
VMEM_WHOLE = pl.BlockSpec(memory_space=pltpu.VMEM)


W_SPECS = [ANY, ANY, pl.BlockSpec(memory_space=pltpu.SMEM)]


def _load_rows(w_refs, name, dst_ref, sems):
    slab_ref, local_ref, me_ref = w_refs
    off, rows = SLAB[name]
    slab_off = off if off < EARLY_ROWS else off - EARLY_ROWS
    me = me_ref[0]
    for phase in ("start", "wait"):
        for j in range(N_CHIPS):
            dst = dst_ref.at[pl.ds(j * rows, rows), :]
            theirs = pltpu.make_async_copy(slab_ref.at[j, pl.ds(slab_off, rows), :], dst, sems.at[j])
            own = pltpu.make_async_copy(local_ref.at[pl.ds(off, rows), :], dst, sems.at[j])

            @pl.when(me == j)
            def _():
                getattr(own, phase)()

            @pl.when(me != j)
            def _():
                getattr(theirs, phase)()


def _rms_fwd(x, g):
    r = lax.rsqrt(jnp.mean(x * x, axis=-1, keepdims=True) + EPS)
    return x * r * g


def _rms_bwd(x, g, dy):
    r = lax.rsqrt(jnp.mean(x * x, axis=-1, keepdims=True) + EPS)
    xn = x * r
    dyg = dy * g
    dx = r * (dyg - xn * jnp.mean(dyg * xn, axis=-1, keepdims=True))
    return dx, jnp.sum(dy * xn, axis=0, keepdims=True)


def _half_sum(v, lo):
    s_lo = jnp.sum(jnp.where(lo, v, 0.0), axis=-1, keepdims=True)
    s_hi = jnp.sum(jnp.where(lo, 0.0, v), axis=-1, keepdims=True)
    return jnp.where(lo, s_lo, s_hi)


def _pair_norm(zp, g, lo):
    r = lax.rsqrt(_half_sum(zp * zp, lo) * (1.0 / HEAD_DIM) + EPS)
    return zp * r * g


def _pair_norm_bwd(zp, g, dy, lo):
    r = lax.rsqrt(_half_sum(zp * zp, lo) * (1.0 / HEAD_DIM) + EPS)
    xn = zp * r
    dyg = dy * g
    dx = r * (dyg - xn * (_half_sum(dyg * xn, lo) * (1.0 / HEAD_DIM)))
    return dx, jnp.sum(dy * xn, axis=0, keepdims=True)


def _to_stacked(pair, group, lo):
    rolled = pltpu.roll(pair, 64, axis=1)
    if group == 0:
        return jnp.where(lo, pair, 0.0), jnp.where(lo, rolled, 0.0)
    return jnp.where(lo, 0.0, rolled), jnp.where(lo, 0.0, pair)


def _from_stacked(even, odd, group, lo):
    if group == 0:
        return jnp.where(lo, even, pltpu.roll(odd, 64, axis=1))
    return jnp.where(lo, pltpu.roll(even, 64, axis=1), odd)


def _sigmoid(v):
    return 1.0 / (1.0 + jnp.exp(-v))


def _pool_counts(tile, n_rows):
    t1 = tile * n_rows + lax.broadcasted_iota(jnp.int32, (n_rows, POOL_WIDTH), 0) + 1
    lane = lax.broadcasted_iota(jnp.int32, (n_rows, POOL_WIDTH), 1)
    win = jnp.where(lane < 128, 2, jnp.where(lane < 256, 4, jnp.where(lane < 384, 8, 16)))
    return jnp.minimum(t1, win).astype(F32)


def _attn_in(x2, g_attn, gq, gk, wts):
    s_len = x2.shape[0]
    t = 512

    def body(x_ref, g_ref, gq_ref, gk_ref, sl_ref, lo_ref, me_ref, hn_ref, zqk_ref, u_ref, kn_ref, v_ref, qst_ref, w_ref, sems):
        @pl.when(pl.program_id(0) == 0)
        def _():
            _load_rows((sl_ref, lo_ref, me_ref), "inT", w_ref, sems)

        hn = _rms_fwd(x_ref[...], g_ref[...]).astype(BF16)
        hn_ref[...] = hn
        z = _dot(hn, w_ref[...], 1, 1)
        zqk_ref[...] = z[:, :640]
        u_ref[...] = z[:, 768:]
        v_ref[...] = z[:, 640:768].astype(BF16)
        lo = lax.broadcasted_iota(jnp.int32, (t, 128), 1) < 64
        kn_ref[...] = _pair_norm(z[:, 512:640], gk_ref[...], lo).astype(BF16)
        for p in range(4):
            qn = _pair_norm(z[:, 128 * p:128 * p + 128], gq_ref[...], lo)
            even, odd = _to_stacked(qn, p // 2, lo)
            qst_ref[2 * p] = even.astype(BF16)
            qst_ref[2 * p + 1] = odd.astype(BF16)

    row = lambda w: pl.BlockSpec((t, w), lambda i: (i, 0))
    return pl.pallas_call(
        body, name="attn_in", grid=(s_len // t,),
        in_specs=[row(D_MODEL), _full((1, D_MODEL)), _full((1, 128)), _full((1, 128))] + W_SPECS,
        out_specs=[row(D_MODEL), row(640), row(POOL_WIDTH), row(128), row(128),
                   pl.BlockSpec((N_Q_HEADS, t, 128), lambda i: (0, i, 0))],
        out_shape=[jax.ShapeDtypeStruct((s_len, D_MODEL), BF16), jax.ShapeDtypeStruct((s_len, 640), F32),
                   jax.ShapeDtypeStruct((s_len, POOL_WIDTH), F32), jax.ShapeDtypeStruct((s_len, 128), BF16),
                   jax.ShapeDtypeStruct((s_len, 128), BF16), jax.ShapeDtypeStruct((N_Q_HEADS, s_len, 128), BF16)],
        scratch_shapes=[pltpu.VMEM((IN_WIDTH, D_MODEL), BF16), pltpu.SemaphoreType.DMA((N_CHIPS,))],
        compiler_params=_params(),
    )(x2, g_attn, gq, gk, *wts)


def _bucket_table():
    i_idx = np.arange(BLOCK)[:, None]
    j_idx = np.arange(2 * BLOCK)[None, :]
    d = BLOCK + i_idx - j_idx
    n = np.maximum(d, 0)
    max_exact = N_BUCKETS // 2
    nf = np.maximum(n, 1).astype(np.float64)
    large = max_exact + (np.log(nf / max_exact) / np.log(MAX_DISTANCE / max_exact) * (N_BUCKETS - max_exact)).astype(np.int64)
    large = np.minimum(large, N_BUCKETS - 1)
    bucket = np.where(n < max_exact, n, large)
    return np.where((d >= 0) & (d < BLOCK), bucket, -1).astype(np.int32)


def _bias_build(rel_bias_t, bucket):
    def body(rb_ref, bucket_ref, out_ref):
        bk = bucket_ref[...]
        for h in range(N_Q_HEADS):
            acc = jnp.full((BLOCK, 2 * BLOCK), NEG, F32)
            for b in range(N_BUCKETS):
                acc = jnp.where(bk == b, rb_ref[h, b], acc)
            out_ref[pl.ds(h * BLOCK, BLOCK), :] = acc

    return pl.pallas_call(
        body, name="bias_build",
        in_specs=[pl.BlockSpec(memory_space=pltpu.SMEM), pl.BlockSpec(memory_space=pltpu.VMEM)],
        out_specs=pl.BlockSpec(memory_space=pltpu.VMEM),
        out_shape=jax.ShapeDtypeStruct((N_Q_HEADS * BLOCK, 2 * BLOCK), F32),
    )(rel_bias_t, bucket)


def _band_softmax(q_ref, kp_ref, kc_ref, bias_ref, sink_ref, block):
    q = q_ref[...].reshape(N_Q_HEADS * BLOCK, 128)
    k2 = jnp.concatenate([kp_ref[...], kc_ref[...]], axis=0)
    s = _dot(q, k2, 1, 1) * (HEAD_DIM ** -0.5) + bias_ref[...]
    col = lax.broadcasted_iota(jnp.int32, s.shape, 1)
    s = jnp.where(col < jnp.where(block == 0, BLOCK, 0), NEG, s)
    sink = sink_ref[...]
    m = jnp.maximum(jnp.max(s, axis=-1, keepdims=True), sink)
    p = jnp.exp(s - m)
    e_sink = jnp.exp(sink - m)
    inv = 1.0 / (jnp.sum(p, axis=-1, keepdims=True) + e_sink)
    return q, k2, p * inv, e_sink * inv


def _attn_specs():
    prev = lambda i: (jnp.maximum(i - 1, 0), 0)
    cur = lambda i: (i, 0)
    stacked = pl.BlockSpec((N_Q_HEADS, BLOCK, 128), lambda i: (0, i, 0))
    kv = [pl.BlockSpec((BLOCK, 128), prev), pl.BlockSpec((BLOCK, 128), cur)]
    consts = [_full((N_Q_HEADS * BLOCK, 2 * BLOCK)), _full((N_Q_HEADS * BLOCK, 1))]
    return stacked, kv, consts


def _head_lane_mask():
    rows = lax.broadcasted_iota(jnp.int32, (N_Q_HEADS * BLOCK, 128), 0)
    lanes = lax.broadcasted_iota(jnp.int32, (N_Q_HEADS * BLOCK, 128), 1)
    return (rows < 4 * BLOCK) == (lanes < 64)


def _attn_fwd(qst, kn, vb, bias_st, sink_st):
    s_len = kn.shape[0]

    def body(q_ref, kp_ref, kc_ref, vp_ref, vc_ref, bias_ref, sink_ref, o_ref):
        _, _, probs, _ = _band_softmax(q_ref, kp_ref, kc_ref, bias_ref, sink_ref, pl.program_id(0))
        v2 = jnp.concatenate([vp_ref[...], vc_ref[...]], axis=0)
        o = _dot(probs.astype(BF16), v2, 1, 0)
        o_ref[...] = jnp.where(_head_lane_mask(), o, 0.0).astype(BF16).reshape(N_Q_HEADS, BLOCK, 128)

    stacked, kv, consts = _attn_specs()
    return pl.pallas_call(
        body, name="attn_fwd", grid=(s_len // BLOCK,),
        in_specs=[stacked] + kv + kv + consts, out_specs=stacked,
        out_shape=jax.ShapeDtypeStruct((N_Q_HEADS, s_len, 128), BF16),
        compiler_params=_params(),
    )(qst, kn, kn, vb, vb, bias_st, sink_st)


def _mix_out(u, ost, x2, wts, wpool, pool_scale, g_ffn):
    s_len = x2.shape[0]
    t = 512
    n = t + 16

    def body(u_ref, o_ref, x_ref, sl_ref, lo_ref, me_ref, wp_ref, sc_ref, g_ref, pooled_ref, mix_ref, h1_ref, hn_ref,
             w_ref, ext_ref, st_ref, sems):
        i = pl.program_id(0)

        @pl.when(i == 0)
        def _():
            _load_rows((sl_ref, lo_ref, me_ref), "out", w_ref, sems)
            ext_ref[...] = jnp.zeros_like(ext_ref)
            st_ref[...] = jnp.zeros_like(st_ref)

        u_tile = u_ref[...]
        ext_ref[pl.ds(POOL_HALO, t), :] = u_tile
        st_ref[pl.ds(8, n), :] = ext_ref[pl.ds(8, n), :] + ext_ref[pl.ds(7, n), :]
        st_ref[pl.ds(8, n), 128:] = st_ref[pl.ds(8, n), 128:] + st_ref[pl.ds(6, n), 128:]
        st_ref[pl.ds(8, n), 256:] = st_ref[pl.ds(8, n), 256:] + st_ref[pl.ds(4, n), 256:]
        st_ref[pl.ds(8, n), 384:] = st_ref[pl.ds(8, n), 384:] + st_ref[pl.ds(0, n), 384:]
        ext_ref[pl.ds(0, POOL_HALO), :] = ext_ref[pl.ds(t, POOL_HALO), :]
        pooled = (st_ref[pl.ds(POOL_HALO, t), :] / _pool_counts(i, t) - u_tile).astype(BF16)
        pooled_ref[...] = pooled
        for g in range(4):
            cols = slice(128 * g, 128 * g + 128)
            y = _dot(pooled[:, cols], wp_ref[g], 1, 0) * sc_ref[:, cols]
            mix_ref[:, ATTN_WIDTH + 128 * g:ATTN_WIDTH + 128 * g + 128] = y.astype(BF16)
        lo = lax.broadcasted_iota(jnp.int32, (t, 128), 1) < 64
        for p in range(4):
            a = _from_stacked(o_ref[2 * p].astype(F32), o_ref[2 * p + 1].astype(F32), p // 2, lo)
            mix_ref[:, 128 * p:128 * p + 128] = a.astype(BF16)
        h1 = x_ref[...] + _dot(mix_ref[...], w_ref[...], 1, 0)
        h1_ref[...] = h1
        hn_ref[...] = _rms_fwd(h1, g_ref[...]).astype(BF16)

    row = lambda w: pl.BlockSpec((t, w), lambda i: (i, 0))
    return pl.pallas_call(
        body, name="mix_out", grid=(s_len // t,),
        in_specs=[row(POOL_WIDTH), pl.BlockSpec((N_Q_HEADS, t, 128), lambda i: (0, i, 0)), row(D_MODEL)] + W_SPECS
        + [_full((4, 128, 128)), _full((1, POOL_WIDTH)), _full((1, D_MODEL))],
        out_specs=[row(POOL_WIDTH), row(D_MODEL), row(D_MODEL), row(D_MODEL)],
        out_shape=[jax.ShapeDtypeStruct((s_len, POOL_WIDTH), BF16), jax.ShapeDtypeStruct((s_len, D_MODEL), BF16),
                   jax.ShapeDtypeStruct((s_len, D_MODEL), F32), jax.ShapeDtypeStruct((s_len, D_MODEL), BF16)],
        scratch_shapes=[pltpu.VMEM((D_MODEL, D_MODEL), BF16), pltpu.VMEM((t + POOL_HALO, POOL_WIDTH), F32),
                        pltpu.VMEM((t + POOL_HALO, POOL_WIDTH), F32), pltpu.SemaphoreType.DMA((N_CHIPS,))],
        compiler_params=_params(),
    )(u, ost, x2, *wts, wpool, pool_scale, g_ffn)


def _ffn_fwd(hn2, h1, wts):
    s_len = h1.shape[0]
    t = 256

    def body(hn_ref, h1_ref, sl_ref, lo_ref, me_ref, gate_ref, up_ref, h2_ref, wg_ref, wu_ref, wd_ref, sems):
        @pl.when(pl.program_id(0) == 0)
        def _():
            w_refs = (sl_ref, lo_ref, me_ref)
            _load_rows(w_refs, "gateT", wg_ref, sems)
            _load_rows(w_refs, "upT", wu_ref, sems)
            _load_rows(w_refs, "down", wd_ref, sems)

        hn = hn_ref[...]
        h2 = h1_ref[...]
        for ch in range(D_FF // FF_CHUNK):
            rows = pl.ds(ch * FF_CHUNK, FF_CHUNK)
            cols = slice(ch * FF_CHUNK, (ch + 1) * FF_CHUNK)
            gate = _dot(hn, wg_ref[rows, :], 1, 1)
            up = _dot(hn, wu_ref[rows, :], 1, 1)
            gate_ref[:, cols] = gate
            up_ref[:, cols] = up
            act = (gate * _sigmoid(gate) * up).astype(BF16)
            h2 = h2 + _dot(act, wd_ref[rows, :], 1, 0)
        h2_ref[...] = h2

    row = lambda w: pl.BlockSpec((t, w), lambda i: (i, 0))
    return pl.pallas_call(
        body, name="ffn_fwd", grid=(s_len // t,),
        in_specs=[row(D_MODEL), row(D_MODEL)] + W_SPECS,
        out_specs=[row(D_FF), row(D_FF), row(D_MODEL)],
        out_shape=[jax.ShapeDtypeStruct((s_len, D_FF), F32), jax.ShapeDtypeStruct((s_len, D_FF), F32),
                   jax.ShapeDtypeStruct((s_len, D_MODEL), F32)],
        scratch_shapes=[pltpu.VMEM((D_FF, D_MODEL), BF16)] * 3 + [pltpu.SemaphoreType.DMA((N_CHIPS,))],
        compiler_params=_params(VMEM_LIMIT_BIG),
    )(hn2, h1, *wts)


def _ple_loss(h2, p2, tgt, wts, g_ple):
    s_len = h2.shape[0]
    t = 512
    n_tiles = s_len // t

    def body(h2_ref, p_ref, tgt_ref, sl_ref, lo_ref, me_ref, g_ref, loss_ref, dh2_ref, dgl_ref, dpp_ref, hn_ref,
             dg_ref, w_ref, wp_ref, packed_ref, loss_acc, sems):
        i = pl.program_id(0)

        @pl.when(i == 0)
        def _():
            w_refs = (sl_ref, lo_ref, me_ref)
            _load_rows(w_refs, "plg", w_ref, sems)
            _load_rows(w_refs, "plp", packed_ref, sems)
            for j in range(N_CHIPS):
                for q in range(4):
                    wp_ref[pl.ds(64 * q, 64), 256 * j:256 * j + 256] = packed_ref[pl.ds(64 * j, 64), 256 * q:256 * q + 256]
            loss_acc[...] = jnp.zeros_like(loss_acc)
            dg_ref[...] = jnp.zeros_like(dg_ref)

        h2v = h2_ref[...]
        g = g_ref[...]
        hn = _rms_fwd(h2v, g).astype(BF16)
        hn_ref[...] = hn
        gate = _sigmoid(_dot(hn, w_ref[...], 1, 0))
        pp = _dot(p_ref[...].astype(BF16), wp_ref[...], 1, 0)
        err = h2v + gate * pp - tgt_ref[...]
        loss_acc[...] += jnp.sum(err * err, axis=0, keepdims=True)
        dy = err * (1.0 / D_MODEL)
        dpp_ref[...] = (dy * gate).astype(BF16)
        dgl = (dy * pp * gate * (1.0 - gate)).astype(BF16)
        dgl_ref[...] = dgl
        dx, dg = _rms_bwd(h2v, g, _dot(dgl, w_ref[...], 1, 1))
        dh2_ref[...] = dy + dx
        dg_ref[...] += dg

        @pl.when(i == n_tiles - 1)
        def _():
            total = jnp.sum(loss_acc[...], axis=-1, keepdims=True) * (0.5 / D_MODEL)
            loss_ref[...] = jnp.broadcast_to(total, loss_ref.shape)

    row = lambda w: pl.BlockSpec((t, w), lambda i: (i, 0))
    return pl.pallas_call(
        body, name="ple_loss", grid=(n_tiles,),
        in_specs=[row(D_MODEL), row(PLE_DIM), row(D_MODEL)] + W_SPECS + [_full((1, D_MODEL))],
        out_specs=[_full((1, 128)), row(D_MODEL), row(D_MODEL), row(D_MODEL), row(D_MODEL), _full((1, D_MODEL))],
        out_shape=[jax.ShapeDtypeStruct((1, 128), F32), jax.ShapeDtypeStruct((s_len, D_MODEL), F32),
                   jax.ShapeDtypeStruct((s_len, D_MODEL), BF16), jax.ShapeDtypeStruct((s_len, D_MODEL), BF16),
                   jax.ShapeDtypeStruct((s_len, D_MODEL), BF16), jax.ShapeDtypeStruct((1, D_MODEL), F32)],
        scratch_shapes=[pltpu.VMEM((D_MODEL, D_MODEL), BF16), pltpu.VMEM((PLE_DIM, D_MODEL), BF16),
                        pltpu.VMEM((PLE_DIM, D_MODEL), BF16), pltpu.VMEM((1, D_MODEL), F32),
                        pltpu.SemaphoreType.DMA((N_CHIPS,))],
        compiler_params=_params(),
    )(h2, p2, tgt, *wts, g_ple)


def _ffn_bwd(dh2, gate, up, h1, wts, g_ffn):
    s_len = h1.shape[0]
    t = 256

    def body(dh2_ref, gate_ref, up_ref, h1_ref, sl_ref, lo_ref, me_ref, g_ref, dgate_ref, dup_ref, act_ref, dh1_ref, dg_ref,
             wg_ref, wu_ref, wd_ref, sems):
        @pl.when(pl.program_id(0) == 0)
        def _():
            w_refs = (sl_ref, lo_ref, me_ref)
            _load_rows(w_refs, "gateT", wg_ref, sems)
            _load_rows(w_refs, "upT", wu_ref, sems)
            _load_rows(w_refs, "down", wd_ref, sems)
            dg_ref[...] = jnp.zeros_like(dg_ref)

        dh2v = dh2_ref[...]
        dh2b = dh2v.astype(BF16)
        dhn = jnp.zeros((t, D_MODEL), F32)
        for ch in range(D_FF // FF_CHUNK):
            rows = pl.ds(ch * FF_CHUNK, FF_CHUNK)
            cols = slice(ch * FF_CHUNK, (ch + 1) * FF_CHUNK)
            dact = _dot(dh2b, wd_ref[rows, :], 1, 1)
            gate_v = gate_ref[:, cols]
            up_v = up_ref[:, cols]
            sg = _sigmoid(gate_v)
            silu = gate_v * sg
            act_ref[:, cols] = (silu * up_v).astype(BF16)
            dup = (dact * silu).astype(BF16)
            dgate = (dact * up_v * (sg * (1.0 + gate_v * (1.0 - sg)))).astype(BF16)
            dup_ref[:, cols] = dup
            dgate_ref[:, cols] = dgate
            dhn = dhn + _dot(dgate, wg_ref[rows, :], 1, 0) + _dot(dup, wu_ref[rows, :], 1, 0)
        dx, dg = _rms_bwd(h1_ref[...], g_ref[...], dhn)
        dh1_ref[...] = dh2v + dx
        dg_ref[...] += dg

    row = lambda w: pl.BlockSpec((t, w), lambda i: (i, 0))
    return pl.pallas_call(
        body, name="ffn_bwd", grid=(s_len // t,),
        in_specs=[row(D_MODEL), row(D_FF), row(D_FF), row(D_MODEL)] + W_SPECS + [_full((1, D_MODEL))],
        out_specs=[row(D_FF), row(D_FF), row(D_FF), row(D_MODEL), _full((1, D_MODEL))],
        out_shape=[jax.ShapeDtypeStruct((s_len, D_FF), BF16), jax.ShapeDtypeStruct((s_len, D_FF), BF16),
                   jax.ShapeDtypeStruct((s_len, D_FF), BF16), jax.ShapeDtypeStruct((s_len, D_MODEL), F32),
                   jax.ShapeDtypeStruct((1, D_MODEL), F32)],
        scratch_shapes=[pltpu.VMEM((D_FF, D_MODEL), BF16)] * 3 + [pltpu.SemaphoreType.DMA((N_CHIPS,))],
        compiler_params=_params(VMEM_LIMIT_BIG),
    )(dh2, gate, up, h1, *wts, g_ffn)


def _mix_out_bwd(dh1, wts, pooled, wpool, pool_scale):
    s_len = dh1.shape[0]
    t = 512
    n = t + 16
    n_tiles = s_len // t

    def body(dh1_ref, sl_ref, lo_ref, me_ref, pooled_ref, wp_ref, sc_ref, dost_ref, du_ref, dyp_ref, dsc_ref,
             w_ref, ext_ref, st_ref, sems):
        i = pl.program_id(0)

        @pl.when(i == 0)
        def _():
            _load_rows((sl_ref, lo_ref, me_ref), "out", w_ref, sems)
            ext_ref[...] = jnp.zeros_like(ext_ref)
            st_ref[...] = jnp.zeros_like(st_ref)
            dsc_ref[...] = jnp.zeros_like(dsc_ref)

        dmix = _dot(dh1_ref[...].astype(BF16), w_ref[...], 1, 1)
        lo = lax.broadcasted_iota(jnp.int32, (t, 128), 1) < 64
        for p in range(4):
            even, odd = _to_stacked(dmix[:, 128 * p:128 * p + 128], p // 2, lo)
            dost_ref[2 * p] = even.astype(BF16)
            dost_ref[2 * p + 1] = odd.astype(BF16)
        pooled_v = pooled_ref[...]
        counts = _pool_counts(n_tiles - 1 - i, t)
        for g in range(4):
            cols = slice(128 * g, 128 * g + 128)
            dm = dmix[:, ATTN_WIDTH + 128 * g:ATTN_WIDTH + 128 * g + 128]
            ypre = _dot(pooled_v[:, cols], wp_ref[g], 1, 0)
            dsc_ref[:, cols] += jnp.sum(ypre * dm, axis=0, keepdims=True)
            dyp = (dm * sc_ref[:, cols]).astype(BF16)
            dyp_ref[:, cols] = dyp
            dpooled = _dot(dyp, wp_ref[g], 1, 1)
            du_ref[:, cols] = -dpooled
            ext_ref[pl.ds(0, t), cols] = dpooled / counts[:, cols]
        st_ref[pl.ds(0, n), :] = ext_ref[pl.ds(0, n), :] + ext_ref[pl.ds(1, n), :]
        st_ref[pl.ds(0, n), 128:] = st_ref[pl.ds(0, n), 128:] + st_ref[pl.ds(2, n), 128:]
        st_ref[pl.ds(0, n), 256:] = st_ref[pl.ds(0, n), 256:] + st_ref[pl.ds(4, n), 256:]
        st_ref[pl.ds(0, n), 384:] = st_ref[pl.ds(0, n), 384:] + st_ref[pl.ds(8, n), 384:]
        ext_ref[pl.ds(t, POOL_HALO), :] = ext_ref[pl.ds(0, POOL_HALO), :]
        du_ref[...] += st_ref[pl.ds(0, t), :]

    rev = lambda w: pl.BlockSpec((t, w), lambda i: (n_tiles - 1 - i, 0))
    return pl.pallas_call(
        body, name="mix_out_bwd", grid=(n_tiles,),
        in_specs=[rev(D_MODEL)] + W_SPECS + [rev(POOL_WIDTH), _full((4, 128, 128)), _full((1, POOL_WIDTH))],
        out_specs=[pl.BlockSpec((N_Q_HEADS, t, 128), lambda i: (0, n_tiles - 1 - i, 0)), rev(POOL_WIDTH), rev(POOL_WIDTH),
                   _full((1, POOL_WIDTH))],
        out_shape=[jax.ShapeDtypeStruct((N_Q_HEADS, s_len, 128), BF16), jax.ShapeDtypeStruct((s_len, POOL_WIDTH), F32),
                   jax.ShapeDtypeStruct((s_len, POOL_WIDTH), BF16), jax.ShapeDtypeStruct((1, POOL_WIDTH), F32)],
        scratch_shapes=[pltpu.VMEM((D_MODEL, D_MODEL), BF16), pltpu.VMEM((t + POOL_HALO, POOL_WIDTH), F32),
                        pltpu.VMEM((t + POOL_HALO, POOL_WIDTH), F32), pltpu.SemaphoreType.DMA((N_CHIPS,))],
        compiler_params=_params(),
    )(dh1, *wts, pooled, wpool, pool_scale)


def _attn_bwd(qst, kn, vb, dost, bias_st, sink_st):
    s_len = kn.shape[0]

    def body(q_ref, kp_ref, kc_ref, vp_ref, vc_ref, do_ref, bias_ref, sink_ref, dq_ref, dk_ref, dv_ref, dbias_ref, dsink_ref):
        i = pl.program_id(0)

        @pl.when(i == 0)
        def _():
            dk_ref[...] = jnp.zeros_like(dk_ref)
            dv_ref[...] = jnp.zeros_like(dv_ref)
            dbias_ref[...] = jnp.zeros_like(dbias_ref)
            dsink_ref[...] = jnp.zeros_like(dsink_ref)

        q, k2, probs, p_sink = _band_softmax(q_ref, kp_ref, kc_ref, bias_ref, sink_ref, i)
        v2 = jnp.concatenate([vp_ref[...], vc_ref[...]], axis=0)
        do = do_ref[...].reshape(N_Q_HEADS * BLOCK, 128)
        dp = _dot(do, v2, 1, 1)
        dsum = jnp.sum(probs * dp, axis=-1, keepdims=True)
        dlog = probs * (dp - dsum)
        dsink_ref[...] -= p_sink * dsum
        dbias_ref[...] += dlog
        dlog_s = (dlog * (HEAD_DIM ** -0.5)).astype(BF16)
        dq = _dot(dlog_s, k2, 1, 0)
        dq_ref[...] = jnp.where(_head_lane_mask(), dq, 0.0).reshape(N_Q_HEADS, BLOCK, 128)
        dk2 = _dot(dlog_s, q, 0, 0)
        dv2 = _dot(probs.astype(BF16), do, 0, 0)
        prev_rows = pl.ds(pl.multiple_of(jnp.maximum(i - 1, 0) * BLOCK, BLOCK), BLOCK)
        cur_rows = pl.ds(pl.multiple_of(i * BLOCK, BLOCK), BLOCK)
        dk_ref[prev_rows, :] += dk2[:BLOCK]
        dk_ref[cur_rows, :] += dk2[BLOCK:]
        dv_ref[prev_rows, :] += dv2[:BLOCK]
        dv_ref[cur_rows, :] += dv2[BLOCK:]

    stacked, kv, consts = _attn_specs()
    return pl.pallas_call(
        body, name="attn_bwd", grid=(s_len // BLOCK,),
        in_specs=[stacked] + kv + kv + [stacked] + consts,
        out_specs=[stacked, _full((s_len, 128)), _full((s_len, 128)), _full((N_Q_HEADS * BLOCK, 2 * BLOCK)),
                   _full((N_Q_HEADS * BLOCK, 1))],
        out_shape=[jax.ShapeDtypeStruct((N_Q_HEADS, s_len, 128), F32), jax.ShapeDtypeStruct((s_len, 128), F32),
                   jax.ShapeDtypeStruct((s_len, 128), F32), jax.ShapeDtypeStruct((N_Q_HEADS * BLOCK, 2 * BLOCK), F32),
                   jax.ShapeDtypeStruct((N_Q_HEADS * BLOCK, 1), F32)],
        compiler_params=_params(),
    )(qst, kn, kn, vb, vb, dost, bias_st, sink_st)


def _small_pack(dg_attn, dg_ffn, dg_ple, dscale, dgq, dgk, dbias, dsink_rows, bucket, loss_v, dwpool):
    def body(ga_ref, gf_ref, gp_ref, sc_ref, gq_ref, gk_ref, db_ref, ds_ref, bucket_ref, loss_ref, wp_ref, out_ref):
        out_ref[pl.ds(0, SMALL["w_pool"]), :] = jnp.zeros((SMALL["w_pool"], 128), F32)
        for name, ref, n in (("g_attn", ga_ref, 8), ("g_ffn", gf_ref, 8), ("g_ple", gp_ref, 8), ("pool_scale", sc_ref, 4)):
            for k in range(n):
                out_ref[pl.ds(SMALL[name] + k, 1), :] = ref[:, 128 * k:128 * k + 128]
        for name, ref in (("g_q", gq_ref), ("g_k", gk_ref)):
            both = ref[...]
            out_ref[pl.ds(SMALL[name], 1), :] = both + pltpu.roll(both, 64, axis=1)
        out_ref[pl.ds(SMALL["loss"], 1), :] = loss_ref[...]
        bk = bucket_ref[...]
        rows = lax.broadcasted_iota(jnp.int32, (N_BUCKETS, 128), 0)
        lanes = lax.broadcasted_iota(jnp.int32, (N_BUCKETS, 128), 1)
        lane1 = lax.broadcasted_iota(jnp.int32, (1, 128), 1)
        rb = jnp.zeros((N_BUCKETS, 128), F32)
        sk = jnp.zeros((1, 128), F32)
        for h in range(N_Q_HEADS):
            band = db_ref[pl.ds(h * BLOCK, BLOCK), :]
            for b in range(N_BUCKETS):
                rb = jnp.where((rows == b) & (lanes == h), jnp.sum(jnp.where(bk == b, band, 0.0)), rb)
            sk = jnp.where(lane1 == h, jnp.sum(ds_ref[pl.ds(h * BLOCK, BLOCK), :]), sk)
        out_ref[pl.ds(SMALL["rel_bias"], N_BUCKETS), :] = rb
        out_ref[pl.ds(SMALL["sinks"], 1), :] = sk
        out_ref[pl.ds(SMALL["w_pool"], 512), :] = wp_ref[...].reshape(512, 128)

    return pl.pallas_call(
        body, name="small_pack", in_specs=[VMEM_WHOLE] * 11, out_specs=VMEM_WHOLE,
        out_shape=jax.ShapeDtypeStruct((SMALL_ROWS, 128), F32),
    )(dg_attn, dg_ffn, dg_ple, dscale, dgq, dgk, dbias, dsink_rows, bucket, loss_v, dwpool)


def _attn_in_bwd(dqst, zqk, dk, dv, du, x2, dh1, wts, g_attn, gq, gk):
    s_len = x2.shape[0]
    t = 512

    def body(dq_ref, zqk_ref, dk_ref, dv_ref, du_ref, x_ref, dh1_ref, sl_ref, lo_ref, me_ref, g_ref, gq_ref, gk_ref,
             dz_ref, dx_ref, dg_ref, dgq_ref, dgk_ref, w_ref, sems):
        @pl.when(pl.program_id(0) == 0)
        def _():
            _load_rows((sl_ref, lo_ref, me_ref), "inT", w_ref, sems)
            dg_ref[...] = jnp.zeros_like(dg_ref)
            dgq_ref[...] = jnp.zeros_like(dgq_ref)
            dgk_ref[...] = jnp.zeros_like(dgk_ref)

        lo = lax.broadcasted_iota(jnp.int32, (t, 128), 1) < 64
        for p in range(4):
            dqn = _from_stacked(dq_ref[2 * p], dq_ref[2 * p + 1], p // 2, lo)
            dq_raw, dgq = _pair_norm_bwd(zqk_ref[:, 128 * p:128 * p + 128], gq_ref[...], dqn, lo)
            dz_ref[:, 128 * p:128 * p + 128] = dq_raw.astype(BF16)
            dgq_ref[...] += dgq
        dk_raw, dgk = _pair_norm_bwd(zqk_ref[:, 512:640], gk_ref[...], dk_ref[...], lo)
        dgk_ref[...] += dgk
        dz_ref[:, 512:640] = dk_raw.astype(BF16)
        dz_ref[:, 640:768] = dv_ref[...].astype(BF16)
        dz_ref[:, 768:] = du_ref[...].astype(BF16)
        dx, dg = _rms_bwd(x_ref[...], g_ref[...], _dot(dz_ref[...], w_ref[...], 1, 0))
        dx_ref[...] = dh1_ref[...] + dx
        dg_ref[...] += dg

    row = lambda w: pl.BlockSpec((t, w), lambda i: (i, 0))
    return pl.pallas_call(
        body, name="attn_in_bwd", grid=(s_len // t,),
        in_specs=[pl.BlockSpec((N_Q_HEADS, t, 128), lambda i: (0, i, 0)), row(640), row(128), row(128), row(POOL_WIDTH),
                  row(D_MODEL), row(D_MODEL)] + W_SPECS + [_full((1, D_MODEL)), _full((1, 128)), _full((1, 128))],
        out_specs=[row(IN_WIDTH), row(D_MODEL), _full((1, D_MODEL)), _full((1, 128)), _full((1, 128))],
        out_shape=[jax.ShapeDtypeStruct((s_len, IN_WIDTH), BF16), jax.ShapeDtypeStruct((s_len, D_MODEL), F32),
                   jax.ShapeDtypeStruct((1, D_MODEL), F32), jax.ShapeDtypeStruct((1, 128), F32),
                   jax.ShapeDtypeStruct((1, 128), F32)],
        scratch_shapes=[pltpu.VMEM((IN_WIDTH, D_MODEL), BF16), pltpu.SemaphoreType.DMA((N_CHIPS,))],
        compiler_params=_params(),
    )(dqst, zqk, dk, dv, du, x2, dh1, *wts, g_attn, gq, gk)


def _dw(a, b, name):
    s_len, m = a.shape
    n_out = b.shape[1]
    tk = 512
    n_steps = s_len // tk
    tm = m // 2 if m > 1408 else m

    def body(a_ref, b_ref, o_ref, acc_ref):
        k = pl.program_id(1)

        @pl.when(k == 0)
        def _():
            acc_ref[...] = jnp.zeros_like(acc_ref)

        acc_ref[...] += _dot(a_ref[...].astype(BF16), b_ref[...].astype(BF16), 0, 0)

        @pl.when(k == n_steps - 1)
        def _():
            o_ref[...] = acc_ref[...].astype(BF16)

    return pl.pallas_call(
        body, name=name, grid=(m // tm, n_steps),
        in_specs=[pl.BlockSpec((tk, tm), lambda i, k: (k, i)), pl.BlockSpec((tk, n_out), lambda i, k: (k, 0))],
        out_specs=pl.BlockSpec((tm, n_out), lambda i, k: (i, 0)),
        out_shape=jax.ShapeDtypeStruct((m, n_out), BF16),
        scratch_shapes=[pltpu.VMEM((tm, n_out), F32)],
        compiler_params=_params(n_axes=2),
    )(a, b)


def _dw_pool(pooled, dyp):
    s_len = pooled.shape[0]
    tk = 512

    def body(a_ref, b_ref, o_ref):
        @pl.when(pl.program_id(0) == 0)
        def _():
            o_ref[...] = jnp.zeros_like(o_ref)

        for g in range(4):
            cols = slice(128 * g, 128 * g + 128)
            o_ref[g] += _dot(a_ref[:, cols], b_ref[:, cols], 0, 0)

    blk = pl.BlockSpec((tk, POOL_WIDTH), lambda k: (k, 0))
    return pl.pallas_call(
        body, name="dw_pool", grid=(s_len // tk,), in_specs=[blk, blk], out_specs=_full((4, 128, 128)),
        out_shape=jax.ShapeDtypeStruct((4, 128, 128), F32), compiler_params=_params(),
    )(pooled, dyp)


def _position():
    x, y, c = lax.axis_index("x"), lax.axis_index("y"), lax.axis_index("c")
    other_chips = [(1 - x, y), (x, 1 - y), (1 - x, 1 - y)]
    return x, y, c, other_chips


def _half(c):
    return pl.ds(pl.multiple_of(c * HALF_ROWS, 16), HALF_ROWS)


def _ag_weights(local_slab, row0, n_rows, name, collective_id):
    half = n_rows // 2

    def body(l_ref, g_ref, send, recv):
        x, y, c, chips = _position()
        me = 2 * x + y
        sibling = (x, y, 1 - c)
        peers = [sibling] + [(*chip, c) for chip in chips]
        barrier = pltpu.get_barrier_semaphore()
        for peer in peers:
            pl.semaphore_signal(barrier, inc=1, device_id=peer, device_id_type=MESH)
        pl.semaphore_wait(barrier, len(peers))
        mine = pl.ds(pl.multiple_of(c * half, 16), half)
        theirs = pl.ds(pl.multiple_of((1 - c) * half, 16), half)

        def copy(k, chip_idx, rows, to, src=None):
            dst = g_ref.at[chip_idx, rows, :]
            return pltpu.make_async_remote_copy(src_ref=dst if src is None else src, dst_ref=dst, send_sem=send.at[k],
                                                recv_sem=recv.at[k], device_id=to, device_id_type=MESH)

        own_rows = l_ref.at[pl.ds(pl.multiple_of(row0 + c * half, 16), half), :]
        first = [copy(k, me, mine, (*chip, c), src=own_rows) for k, chip in enumerate(chips)]
        for cp in first:
            cp.start()
        passed = []
        for k, chip in enumerate(chips):
            idx = 2 * chip[0] + chip[1]
            copy(k, idx, mine, (x, y, c)).wait_recv()
            fwd = copy(3 + k, idx, mine, sibling)
            fwd.start()
            passed.append(fwd)
        for k, chip in enumerate(chips):
            copy(3 + k, 2 * chip[0] + chip[1], theirs, (x, y, c)).wait_recv()
        for cp in first + passed:
            cp.wait_send()

    return pl.kernel(
        body, out_type=jax.ShapeDtypeStruct((N_CHIPS, n_rows, D_MODEL), BF16),
        mesh=plsc.ScalarSubcoreMesh(axis_name="sequencer", num_cores=1), name=name,
        scratch_types=[pltpu.SemaphoreType.DMA((6,)), pltpu.SemaphoreType.DMA((6,))],
        compiler_params=pltpu.CompilerParams(collective_id=collective_id),
    )(local_slab)


def _rs_swap_halves(partial):
    def body(p_ref, r_ref, send, recv):
        x, y, c, _ = _position()
        cp = pltpu.make_async_remote_copy(src_ref=p_ref.at[:, _half(1 - c), :], dst_ref=r_ref, send_sem=send, recv_sem=recv,
                                          device_id=(x, y, 1 - c), device_id_type=MESH)
        cp.start()
        cp.wait()

    return pl.pallas_call(
        body, name="rs_swap_halves", in_specs=[ANY], out_specs=ANY,
        out_shape=jax.ShapeDtypeStruct((N_CHIPS, HALF_ROWS, D_MODEL), BF16),
        scratch_shapes=[pltpu.SemaphoreType.DMA, pltpu.SemaphoreType.DMA],
    )(partial)


def _rs_add_halves(partial, other, core):
    t = HALF_ROWS // 2

    def body(core_ref, a_ref, b_ref, o_ref):
        o_ref[...] = (a_ref[...].astype(F32) + b_ref[...].astype(F32)).astype(BF16)

    steps = HALF_ROWS // t
    return pl.pallas_call(
        body, name="rs_add_halves",
        grid_spec=pltpu.PrefetchScalarGridSpec(
            num_scalar_prefetch=1, grid=(N_CHIPS, steps),
            in_specs=[pl.BlockSpec((1, t, D_MODEL), lambda j, i, core_ref: (j, core_ref[0] * steps + i, 0)),
                      pl.BlockSpec((1, t, D_MODEL), lambda j, i, core_ref: (j, i, 0))],
            out_specs=pl.BlockSpec((1, t, D_MODEL), lambda j, i, core_ref: (j, i, 0))),
        out_shape=jax.ShapeDtypeStruct((N_CHIPS, HALF_ROWS, D_MODEL), BF16),
        compiler_params=_params(n_axes=2),
    )(core, partial, other)


def _rs_exchange_chips(pre):
    def body(s_ref, r_ref, send, recv):
        x, y, c, chips = _position()

        def copy(k, chunk, to):
            return pltpu.make_async_remote_copy(src_ref=s_ref.at[chunk], dst_ref=r_ref.at[k], send_sem=send.at[k],
                                                recv_sem=recv.at[k], device_id=to, device_id_type=MESH)

        sends = [copy(k, 2 * chip[0] + chip[1], (*chip, c)) for k, chip in enumerate(chips)]
        for cp in sends:
            cp.start()
        for cp in sends:
            cp.wait()

    return pl.pallas_call(
        body, name="rs_exchange_chips", in_specs=[ANY], out_specs=ANY,
        out_shape=jax.ShapeDtypeStruct((3, HALF_ROWS, D_MODEL), BF16),
        scratch_shapes=[pltpu.SemaphoreType.DMA((3,)), pltpu.SemaphoreType.DMA((3,))],
    )(pre)


def _rs_sum_chips(pre, received, place):
    t = HALF_ROWS // 2
    steps = HALF_ROWS // t

    def body(place_ref, own_ref, r_ref, o_ref):
        acc = own_ref[0].astype(F32)
        for k in range(3):
            acc = acc + r_ref[k].astype(F32)
        o_ref[...] = acc

    return pl.pallas_call(
        body, name="rs_sum_chips",
        grid_spec=pltpu.PrefetchScalarGridSpec(
            num_scalar_prefetch=1, grid=(steps,),
            in_specs=[pl.BlockSpec((1, t, D_MODEL), lambda i, place_ref: (place_ref[0], i, 0)),
                      pl.BlockSpec((3, t, D_MODEL), lambda i, place_ref: (0, i, 0))],
            out_specs=pl.BlockSpec((t, D_MODEL), lambda i, place_ref: (place_ref[1] * steps + i, 0))),
        out_shape=jax.ShapeDtypeStruct((SLAB_ROWS, D_MODEL), F32),
        compiler_params=_params(),
    )(place, pre, received)


def _rs_finish(grad_slab, small):
    def body(f_ref, s_ref, g_ref, t_ref, send, recv, local_sem):
        del f_ref
        x, y, c, chips = _position()
        sibling = (x, y, 1 - c)

        def slot(px, py, pc):
            return t_ref.at[4 * px + 2 * py + pc]

        def copy(k, block, to, src=None):
            return pltpu.make_async_remote_copy(src_ref=slot(*block) if src is None else src, dst_ref=slot(*block),
                                                send_sem=send.at[k], recv_sem=recv.at[k], device_id=to, device_id_type=MESH)

        def half_copy(rows, to):
            return pltpu.make_async_remote_copy(src_ref=g_ref.at[rows, :], dst_ref=g_ref.at[rows, :], send_sem=send.at[7],
                                                recv_sem=recv.at[7], device_id=to, device_id_type=MESH)

        own_small = pltpu.make_async_copy(s_ref, slot(x, y, c), local_sem)
        own_small.start()
        to_sibling = half_copy(_half(c), sibling)
        to_sibling.start()
        first = [copy(0, (x, y, c), sibling, src=s_ref)]
        first += [copy(1 + k, (x, y, c), (*chip, c), src=s_ref) for k, chip in enumerate(chips)]
        for cp in first:
            cp.start()
        passed = []
        for k, chip in enumerate(chips):
            copy(1 + k, (*chip, c), (x, y, c)).wait_recv()
            fwd = copy(4 + k, (*chip, c), sibling)
            fwd.start()
            passed.append(fwd)
        copy(0, sibling, (x, y, c)).wait_recv()
        for k, chip in enumerate(chips):
            copy(4 + k, (*chip, 1 - c), (x, y, c)).wait_recv()
        half_copy(_half(1 - c), (x, y, c)).wait_recv()
        for cp in first + passed + [to_sibling]:
            cp.wait_send()
        own_small.wait()

    return pl.pallas_call(
        body, name="rs_finish", in_specs=[ANY, ANY], out_specs=[ANY, ANY], input_output_aliases={0: 0},
        out_shape=[jax.ShapeDtypeStruct((SLAB_ROWS, D_MODEL), F32), jax.ShapeDtypeStruct((N_DEV, SMALL_ROWS, 128), F32)],
        scratch_shapes=[pltpu.SemaphoreType.DMA((8,)), pltpu.SemaphoreType.DMA((8,)), pltpu.SemaphoreType.DMA],
    )(grad_slab, small)


def _adam_update(w, g, m, v):
    m_new = ADAM_B1 * m + (1.0 - ADAM_B1) * g
    v_new = ADAM_B2 * v + (1.0 - ADAM_B2) * (g * g)
    m_hat = m_new / (1.0 - ADAM_B1 ** ADAM_STEP)
    v_hat = v_new / (1.0 - ADAM_B2 ** ADAM_STEP)
    return -ADAM_LR * (m_hat / (jnp.sqrt(v_hat) + ADAM_EPS) + ADAM_WD * w), m_new, v_new


def _adamw(w, g, m, v, name):
    rows, cols = w.shape
    t = rows if rows % 256 else 256

    def body(w_ref, g_ref, m_ref, v_ref, d_ref, nm_ref, nv_ref):
        d_ref[...], nm_ref[...], nv_ref[...] = _adam_update(w_ref[...], g_ref[...], m_ref[...], v_ref[...])

    blk = pl.BlockSpec((t, cols), lambda i: (i, 0))
    shape = jax.ShapeDtypeStruct((rows, cols), F32)
    return pl.pallas_call(
        body, name=name, grid=(rows // t,), in_specs=[blk] * 4, out_specs=[blk] * 3, out_shape=[shape] * 3,
        compiler_params=_params(),
    )(w, g, m, v)


SMALL_PARAMS = [("g_attn", (1, D_MODEL), 8), ("g_q", (1, HEAD_DIM), None), ("g_k", (1, HEAD_DIM), None),
                ("sinks", (1, N_Q_HEADS), None), ("rel_bias", (N_BUCKETS, N_Q_HEADS), None), ("w_pool", (512, 128), None),
                ("pool_scale", (1, POOL_WIDTH), 4), ("g_ffn", (1, D_MODEL), 8), ("g_ple", (1, D_MODEL), 8)]


def _adamw_small(tables, wmv):
    n_par = len(SMALL_PARAMS)

    def body(*refs):
        t_ref = refs[0]
        ins = refs[1:1 + 3 * n_par]
        loss_ref = refs[1 + 3 * n_par]
        outs = refs[2 + 3 * n_par:-1]
        tot_ref = refs[-1]
        total = t_ref[0]
        for d in range(1, N_DEV):
            total = total + t_ref[d]
        tot_ref[...] = total
        loss_ref[...] = tot_ref[pl.ds(SMALL["loss"], 1), 0:1]
        for i, (name, shape, split) in enumerate(SMALL_PARAMS):
            g_ref, d_ref, nm_ref, nv_ref = outs[4 * i:4 * i + 4]
            row = SMALL[name]
            if split:
                for k in range(split):
                    g_ref[:, 128 * k:128 * k + 128] = tot_ref[pl.ds(row + k, 1), :]
            else:
                g_ref[...] = tot_ref[pl.ds(row, shape[0]), 0:shape[1]]
            w_ref, m_ref, v_ref = ins[3 * i:3 * i + 3]
            d_ref[...], nm_ref[...], nv_ref[...] = _adam_update(w_ref[...], g_ref[...], m_ref[...], v_ref[...])

    shapes = [jax.ShapeDtypeStruct((1, 1), F32)]
    for _, shape, _ in SMALL_PARAMS:
        shapes += [jax.ShapeDtypeStruct(shape, F32)] * 4
    flat = [a for triple in wmv for a in triple]
    res = pl.pallas_call(
        body, name="adamw_small", in_specs=[VMEM_WHOLE] * (1 + 3 * n_par), out_specs=[VMEM_WHOLE] * len(shapes),
        out_shape=shapes, scratch_shapes=[pltpu.VMEM((SMALL_ROWS, 128), F32)],
    )(tables, *flat)
    return res[0], [res[1 + 4 * i:5 + 4 * i] for i in range(n_par)]


def _pack_ple_proj(shard):
    return shard.reshape(4, 64, 256).transpose(1, 0, 2).reshape(64, D_MODEL)


def _local_grads(x2, p2, tgt, wts, g_attn_norm, g_q, g_k, attn_sinks, rel_bias, w_pool, pool_scale, g_ffn_norm, g_ple_norm):
    early, late, local_slab, me = wts
    w_early, w_late = (early, local_slab, me), (late, local_slab, me)
    bucket = jnp.asarray(_bucket_table())
    gq = jnp.tile(g_q, (1, 2))
    gk = jnp.tile(g_k, (1, 2))
    wpool = w_pool[0].astype(BF16)
    sink_st = jnp.repeat(attn_sinks[0], BLOCK)[:, None]
    bias_st = _bias_build(rel_bias.T, bucket)

    hn1, zqk, u, kn, vb, qst = _attn_in(x2, g_attn_norm, gq, gk, w_early)
    ost = _attn_fwd(qst, kn, vb, bias_st, sink_st)
    pooled, mix, h1, hn2 = _mix_out(u, ost, x2, w_early, wpool, pool_scale, g_ffn_norm)
    gate, up, h2 = _ffn_fwd(hn2, h1, w_late)
    loss_v, dh2, dgl, dpp, hn3, dg_ple = _ple_loss(h2, p2, tgt, w_late, g_ple_norm)

    dgate, dup, act, dh1, dg_ffn = _ffn_bwd(dh2, gate, up, h1, w_late, g_ffn_norm)
    dost, du, dyp, dscale = _mix_out_bwd(dh1, w_early, pooled, wpool, pool_scale)
    dqst, dk, dv, dbias, dsink_rows = _attn_bwd(qst, kn, vb, dost, bias_st, sink_st)
    dz, dx, dg_attn, dgq, dgk = _attn_in_bwd(dqst, zqk, dk, dv, du, x2, dh1, w_early, g_attn_norm, gq, gk)

    chunks = [
        _dw(dz, hn1, "dw_in").reshape(N_CHIPS, -1, D_MODEL),
        _dw(mix, dh1, "dw_out").reshape(N_CHIPS, -1, D_MODEL),
        _dw(dgate, hn2, "dw_gate").reshape(N_CHIPS, -1, D_MODEL),
        _dw(dup, hn2, "dw_up").reshape(N_CHIPS, -1, D_MODEL),
        _dw(act, dh2, "dw_down").reshape(N_CHIPS, -1, D_MODEL),
        _dw(hn3, dgl, "dw_ple_gate").reshape(N_CHIPS, -1, D_MODEL),
        _dw(p2, dpp, "dw_ple_proj").reshape(4, 64, N_CHIPS, 256).transpose(2, 1, 0, 3).reshape(N_CHIPS, 64, D_MODEL),
    ]
    partial = jnp.concatenate(chunks, axis=1)
    small = _small_pack(dg_attn, dg_ffn, dg_ple, dscale, dgq, dgk, dbias, dsink_rows, bucket, loss_v, _dw_pool(pooled, dyp))
    return dx, partial, small


def kernel(x, p, w_in, w_out, g_attn_norm, g_q, g_k, attn_sinks, rel_bias, w_pool, pool_scale, g_ffn_norm, w_gate, w_up, w_down, g_ple_norm, w_ple_gate, w_ple_proj, loss_target, m_w_in, m_w_out, m_g_attn_norm, m_g_q, m_g_k, m_attn_sinks, m_rel_bias, m_w_pool, m_pool_scale, m_g_ffn_norm, m_w_gate, m_w_up, m_w_down, m_g_ple_norm, m_w_ple_gate, m_w_ple_proj, v_w_in, v_w_out, v_g_attn_norm, v_g_q, v_g_k, v_attn_sinks, v_rel_bias, v_w_pool, v_pool_scale, v_g_ffn_norm, v_w_gate, v_w_up, v_w_down, v_g_ple_norm, v_w_ple_gate, v_w_ple_proj):
    core = lax.axis_index("c").astype(jnp.int32).reshape(1)
    me = (2 * lax.axis_index("x") + lax.axis_index("y")).astype(jnp.int32).reshape(1)

    local_slab = jnp.concatenate(
        [w_in[0].T, w_out[0], w_gate[0].T, w_up[0].T, w_down[0], w_ple_gate[0], _pack_ple_proj(w_ple_proj[0])],
        axis=0).astype(BF16)
    wts = (_ag_weights(local_slab, 0, EARLY_ROWS, "ag_early", 1),
           _ag_weights(local_slab, EARLY_ROWS, SLAB_ROWS - EARLY_ROWS, "ag_late", 2), local_slab, me)

    dx, partial, small = _local_grads(x[0], p[0, 0], loss_target[0], wts, g_attn_norm, g_q, g_k, attn_sinks, rel_bias,
                                      w_pool, pool_scale, g_ffn_norm, g_ple_norm)

    pre = _rs_add_halves(partial, _rs_swap_halves(partial), core)
    half_summed = _rs_sum_chips(pre, _rs_exchange_chips(pre), jnp.concatenate([me, core]))
    grad_slab, small_all = _rs_finish(half_summed, small)

    def rows(name):
        off, n_rows = SLAB[name]
        return grad_slab[off:off + n_rows]

    big = {
        "w_in": (w_in, m_w_in, v_w_in, rows("inT").T),
        "w_out": (w_out, m_w_out, v_w_out, rows("out")),
        "w_gate": (w_gate, m_w_gate, v_w_gate, rows("gateT").T),
        "w_up": (w_up, m_w_up, v_w_up, rows("upT").T),
        "w_down": (w_down, m_w_down, v_w_down, rows("down")),
        "w_ple_gate": (w_ple_gate, m_w_ple_gate, v_w_ple_gate, rows("plg")),
        "w_ple_proj": (w_ple_proj, m_w_ple_proj, v_w_ple_proj,
                       rows("plp").reshape(64, 4, 256).transpose(1, 0, 2).reshape(PLE_DIM, PLE_DIM)),
    }
    small_params = {
        "g_attn_norm": (g_attn_norm, m_g_attn_norm, v_g_attn_norm), "g_q": (g_q, m_g_q, v_g_q), "g_k": (g_k, m_g_k, v_g_k),
        "attn_sinks": (attn_sinks, m_attn_sinks, v_attn_sinks), "rel_bias": (rel_bias, m_rel_bias, v_rel_bias),
        "w_pool": tuple(a.reshape(512, 128) for a in (w_pool, m_w_pool, v_w_pool)),
        "pool_scale": (pool_scale, m_pool_scale, v_pool_scale), "g_ffn_norm": (g_ffn_norm, m_g_ffn_norm, v_g_ffn_norm),
        "g_ple_norm": (g_ple_norm, m_g_ple_norm, v_g_ple_norm),
    }

    grads, deltas, new_ms, new_vs = {}, {}, {}, {}
    for name, (w, m, v, g2) in big.items():
        d, nm, nv = _adamw(w[0], g2, m[0], v[0], "adamw_" + name)
        grads[name], deltas[name], new_ms[name], new_vs[name] = g2[None], d[None], nm[None], nv[None]

    loss, small_out = _adamw_small(small_all, list(small_params.values()))
    for name, (g2, d, nm, nv) in zip(small_params, small_out):
        shape = w_pool.shape if name == "w_pool" else g2.shape
        grads[name], deltas[name], new_ms[name], new_vs[name] = (a.reshape(shape) for a in (g2, d, nm, nv))

    order = ["w_in", "w_out", "g_attn_norm", "g_q", "g_k", "attn_sinks", "rel_bias", "w_pool", "pool_scale", "g_ffn_norm",
             "w_gate", "w_up", "w_down", "g_ple_norm", "w_ple_gate", "w_ple_proj"]
    return (loss.reshape(()), dx[None], *[grads[n] for n in order], *[deltas[n] for n in order],
            *[new_ms[n] for n in order], *[new_vs[n] for n in order])
```

```python
import functools

import numpy as np
import jax
import jax.numpy as jnp
from jax import lax
from jax.experimental import pallas as pl
from jax.experimental.pallas import tpu as pltpu
from jax.experimental.pallas import tpu_sc as plsc

F32 = jnp.float32
BF16 = jnp.bfloat16
MESH = pl.DeviceIdType.MESH

D_MODEL = 1024
HEAD_DIM = 64
N_Q_HEADS = 8
ATTN_WIDTH = 512
KV_WIDTH = 128
POOL_WIDTH = 512
IN_WIDTH = 1280
D_FF = 2816
PLE_DIM = 256
FF_CHUNK = 1408
BLOCK = 128
N_BUCKETS = 32
MAX_DISTANCE = 128
POOL_SIZES = (2, 4, 8, 16)
EPS = 1e-6
NEG = -1e30
N_CHIPS = 4
N_DEV = 8

ADAM_LR = 0.001
ADAM_B1 = 0.9
ADAM_B2 = 0.999
ADAM_EPS = 1e-08
ADAM_WD = 0.01
ADAM_STEP = 10

SLAB = {"inT": (0, 320), "out": (320, 256), "gateT": (576, 704), "upT": (1280, 704), "down": (1984, 704),
        "plg": (2688, 256), "plp": (2944, 64)}
SLAB_ROWS = 3008
HALF_ROWS = SLAB_ROWS // 2
EARLY_ROWS = 576
POOL_HALO = 24

SMALL = {"g_attn": 0, "g_ffn": 8, "g_ple": 16, "pool_scale": 24, "g_q": 28, "g_k": 29, "sinks": 30, "loss": 31,
         "rel_bias": 32, "w_pool": 64}
SMALL_ROWS = 576

VMEM_LIMIT_BIG = 60 * 1024 * 1024
VMEM_LIMIT = 48 * 1024 * 1024


def _params(vmem=VMEM_LIMIT, n_axes=1):
    return pltpu.CompilerParams(dimension_semantics=("arbitrary",) * n_axes, vmem_limit_bytes=vmem)


def _dot(a, b, ca, cb):
    return lax.dot_general(a, b, (((ca,), (cb,)), ((), ())), preferred_element_type=F32)


def _full(shape):
    return pl.BlockSpec(shape, lambda i: (0,) * len(shape))


ANY = pl.BlockSpec(memory_space=pl.ANY)
VMEM_WHOLE = pl.BlockSpec(memory_space=pltpu.VMEM)


W_SPECS = [ANY, ANY, pl.BlockSpec(memory_space=pltpu.SMEM)]


def _load_rows(w_refs, name, dst_ref, sems):
    slab_ref, local_ref, me_ref = w_refs
    off, rows = SLAB[name]
    slab_off = off if off < EARLY_ROWS else off - EARLY_ROWS
    me = me_ref[0]
    for phase in ("start", "wait"):
        for j in range(N_CHIPS):
            dst = dst_ref.at[pl.ds(j * rows, rows), :]
            theirs = pltpu.make_async_copy(slab_ref.at[j, pl.ds(slab_off, rows), :], dst, sems.at[j])
            own = pltpu.make_async_copy(local_ref.at[pl.ds(off, rows), :], dst, sems.at[j])

            @pl.when(me == j)
            def _():
                getattr(own, phase)()

            @pl.when(me != j)
            def _():
                getattr(theirs, phase)()


def _rms_fwd(x, g):
    r = lax.rsqrt(jnp.mean(x * x, axis=-1, keepdims=True) + EPS)
    return x * r * g


def _rms_bwd(x, g, dy):
    r = lax.rsqrt(jnp.mean(x * x, axis=-1, keepdims=True) + EPS)
    xn = x * r
    dyg = dy * g
    dx = r * (dyg - xn * jnp.mean(dyg * xn, axis=-1, keepdims=True))
    return dx, jnp.sum(dy * xn, axis=0, keepdims=True)


def _half_sum(v, lo):
    s_lo = jnp.sum(jnp.where(lo, v, 0.0), axis=-1, keepdims=True)
    s_hi = jnp.sum(jnp.where(lo, 0.0, v), axis=-1, keepdims=True)
    return jnp.where(lo, s_lo, s_hi)


def _pair_norm(zp, g, lo):
    r = lax.rsqrt(_half_sum(zp * zp, lo) * (1.0 / HEAD_DIM) + EPS)
    return zp * r * g


def _pair_norm_bwd(zp, g, dy, lo):
    r = lax.rsqrt(_half_sum(zp * zp, lo) * (1.0 / HEAD_DIM) + EPS)
    xn = zp * r
    dyg = dy * g
    dx = r * (dyg - xn * (_half_sum(dyg * xn, lo) * (1.0 / HEAD_DIM)))
    return dx, jnp.sum(dy * xn, axis=0, keepdims=True)


def _to_stacked(pair, group, lo):
    rolled = pltpu.roll(pair, 64, axis=1)
    if group == 0:
        return jnp.where(lo, pair, 0.0), jnp.where(lo, rolled, 0.0)
    return jnp.where(lo, 0.0, rolled), jnp.where(lo, 0.0, pair)


def _from_stacked(even, odd, group, lo):
    if group == 0:
        return jnp.where(lo, even, pltpu.roll(odd, 64, axis=1))
    return jnp.where(lo, pltpu.roll(even, 64, axis=1), odd)


def _sigmoid(v):
    return 1.0 / (1.0 + jnp.exp(-v))


def _pool_counts(tile, n_rows):
    t1 = tile * n_rows + lax.broadcasted_iota(jnp.int32, (n_rows, POOL_WIDTH), 0) + 1
    lane = lax.broadcasted_iota(jnp.int32, (n_rows, POOL_WIDTH), 1)
    win = jnp.where(lane < 128, 2, jnp.where(lane < 256, 4, jnp.where(lane < 384, 8, 16)))
    return jnp.minimum(t1, win).astype(F32)


def _attn_in(x2, g_attn, gq, gk, wts):
    s_len = x2.shape[0]
    t = 512

    def body(x_ref, g_ref, gq_ref, gk_ref, sl_ref, lo_ref, me_ref, hn_ref, zqk_ref, u_ref, kn_ref, v_ref, qst_ref, w_ref, sems):
        @pl.when(pl.program_id(0) == 0)
        def _():
            _load_rows((sl_ref, lo_ref, me_ref), "inT", w_ref, sems)

        hn = _rms_fwd(x_ref[...], g_ref[...]).astype(BF16)
        hn_ref[...] = hn
        z = _dot(hn, w_ref[...], 1, 1)
        zqk_ref[...] = z[:, :640]
        u_ref[...] = z[:, 768:]
        v_ref[...] = z[:, 640:768].astype(BF16)
        lo = lax.broadcasted_iota(jnp.int32, (t, 128), 1) < 64
        kn_ref[...] = _pair_norm(z[:, 512:640], gk_ref[...], lo).astype(BF16)
        for p in range(4):
            qn = _pair_norm(z[:, 128 * p:128 * p + 128], gq_ref[...], lo)
            even, odd = _to_stacked(qn, p // 2, lo)
            qst_ref[2 * p] = even.astype(BF16)
            qst_ref[2 * p + 1] = odd.astype(BF16)

    row = lambda w: pl.BlockSpec((t, w), lambda i: (i, 0))
    return pl.pallas_call(
        body, name="attn_in", grid=(s_len // t,),
        in_specs=[row(D_MODEL), _full((1, D_MODEL)), _full((1, 128)), _full((1, 128))] + W_SPECS,
        out_specs=[row(D_MODEL), row(640), row(POOL_WIDTH), row(128), row(128),
                   pl.BlockSpec((N_Q_HEADS, t, 128), lambda i: (0, i, 0))],
        out_shape=[jax.ShapeDtypeStruct((s_len, D_MODEL), BF16), jax.ShapeDtypeStruct((s_len, 640), F32),
                   jax.ShapeDtypeStruct((s_len, POOL_WIDTH), F32), jax.ShapeDtypeStruct((s_len, 128), BF16),
                   jax.ShapeDtypeStruct((s_len, 128), BF16), jax.ShapeDtypeStruct((N_Q_HEADS, s_len, 128), BF16)],
        scratch_shapes=[pltpu.VMEM((IN_WIDTH, D_MODEL), BF16), pltpu.SemaphoreType.DMA((N_CHIPS,))],
        compiler_params=_params(),
    )(x2, g_attn, gq, gk, *wts)


def _bucket_table():
    i_idx = np.arange(BLOCK)[:, None]
    j_idx = np.arange(2 * BLOCK)[None, :]
    d = BLOCK + i_idx - j_idx
    n = np.maximum(d, 0)
    max_exact = N_BUCKETS // 2
    nf = np.maximum(n, 1).astype(np.float64)
    large = max_exact + (np.log(nf / max_exact) / np.log(MAX_DISTANCE / max_exact) * (N_BUCKETS - max_exact)).astype(np.int64)
    large = np.minimum(large, N_BUCKETS - 1)
    bucket = np.where(n < max_exact, n, large)
    return np.where((d >= 0) & (d < BLOCK), bucket, -1).astype(np.int32)


def _bias_build(rel_bias_t, bucket):
    def body(rb_ref, bucket_ref, out_ref):
        bk = bucket_ref[...]
        for h in range(N_Q_HEADS):
            acc = jnp.full((BLOCK, 2 * BLOCK), NEG, F32)
            for b in range(N_BUCKETS):
                acc = jnp.where(bk == b, rb_ref[h, b], acc)
            out_ref[pl.ds(h * BLOCK, BLOCK), :] = acc

    return pl.pallas_call(
        body, name="bias_build",
        in_specs=[pl.BlockSpec(memory_space=pltpu.SMEM), pl.BlockSpec(memory_space=pltpu.VMEM)],
        out_specs=pl.BlockSpec(memory_space=pltpu.VMEM),
        out_shape=jax.ShapeDtypeStruct((N_Q_HEADS * BLOCK, 2 * BLOCK), F32),
    )(rel_bias_t, bucket)


def _band_softmax(q_ref, kp_ref, kc_ref, bias_ref, sink_ref, block):
    q = q_ref[...].reshape(N_Q_HEADS * BLOCK, 128)
    k2 = jnp.concatenate([kp_ref[...], kc_ref[...]], axis=0)
    s = _dot(q, k2, 1, 1) * (HEAD_DIM ** -0.5) + bias_ref[...]
    col = lax.broadcasted_iota(jnp.int32, s.shape, 1)
    s = jnp.where(col < jnp.where(block == 0, BLOCK, 0), NEG, s)
    sink = sink_ref[...]
    m = jnp.maximum(jnp.max(s, axis=-1, keepdims=True), sink)
    p = jnp.exp(s - m)
    e_sink = jnp.exp(sink - m)
    inv = 1.0 / (jnp.sum(p, axis=-1, keepdims=True) + e_sink)
    return q, k2, p * inv, e_sink * inv


def _attn_specs():
    prev = lambda i: (jnp.maximum(i - 1, 0), 0)
    cur = lambda i: (i, 0)
    stacked = pl.BlockSpec((N_Q_HEADS, BLOCK, 128), lambda i: (0, i, 0))
    kv = [pl.BlockSpec((BLOCK, 128), prev), pl.BlockSpec((BLOCK, 128), cur)]
    consts = [_full((N_Q_HEADS * BLOCK, 2 * BLOCK)), _full((N_Q_HEADS * BLOCK, 1))]
    return stacked, kv, consts


def _head_lane_mask():
    rows = lax.broadcasted_iota(jnp.int32, (N_Q_HEADS * BLOCK, 128), 0)
    lanes = lax.broadcasted_iota(jnp.int32, (N_Q_HEADS * BLOCK, 128), 1)
    return (rows < 4 * BLOCK) == (lanes < 64)


def _attn_fwd(qst, kn, vb, bias_st, sink_st):
    s_len = kn.shape[0]

    def body(q_ref, kp_ref, kc_ref, vp_ref, vc_ref, bias_ref, sink_ref, o_ref):
        _, _, probs, _ = _band_softmax(q_ref, kp_ref, kc_ref, bias_ref, sink_ref, pl.program_id(0))
        v2 = jnp.concatenate([vp_ref[...], vc_ref[...]], axis=0)
        o = _dot(probs.astype(BF16), v2, 1, 0)
        o_ref[...] = jnp.where(_head_lane_mask(), o, 0.0).astype(BF16).reshape(N_Q_HEADS, BLOCK, 128)

    stacked, kv, consts = _attn_specs()
    return pl.pallas_call(
        body, name="attn_fwd", grid=(s_len // BLOCK,),
        in_specs=[stacked] + kv + kv + consts, out_specs=stacked,
        out_shape=jax.ShapeDtypeStruct((N_Q_HEADS, s_len, 128), BF16),
        compiler_params=_params(),
    )(qst, kn, kn, vb, vb, bias_st, sink_st)


def _mix_out(u, ost, x2, wts, wpool, pool_scale, g_ffn):
    s_len = x2.shape[0]
    t = 512
    n = t + 16

    def body(u_ref, o_ref, x_ref, sl_ref, lo_ref, me_ref, wp_ref, sc_ref, g_ref, pooled_ref, mix_ref, h1_ref, hn_ref,
             w_ref, ext_ref, st_ref, sems):
        i = pl.program_id(0)

        @pl.when(i == 0)
        def _():
            _load_rows((sl_ref, lo_ref, me_ref), "out", w_ref, sems)
            ext_ref[...] = jnp.zeros_like(ext_ref)
            st_ref[...] = jnp.zeros_like(st_ref)

        u_tile = u_ref[...]
        ext_ref[pl.ds(POOL_HALO, t), :] = u_tile
        st_ref[pl.ds(8, n), :] = ext_ref[pl.ds(8, n), :] + ext_ref[pl.ds(7, n), :]
        st_ref[pl.ds(8, n), 128:] = st_ref[pl.ds(8, n), 128:] + st_ref[pl.ds(6, n), 128:]
        st_ref[pl.ds(8, n), 256:] = st_ref[pl.ds(8, n), 256:] + st_ref[pl.ds(4, n), 256:]
        st_ref[pl.ds(8, n), 384:] = st_ref[pl.ds(8, n), 384:] + st_ref[pl.ds(0, n), 384:]
        ext_ref[pl.ds(0, POOL_HALO), :] = ext_ref[pl.ds(t, POOL_HALO), :]
        pooled = (st_ref[pl.ds(POOL_HALO, t), :] / _pool_counts(i, t) - u_tile).astype(BF16)
        pooled_ref[...] = pooled
        for g in range(4):
            cols = slice(128 * g, 128 * g + 128)
            y = _dot(pooled[:, cols], wp_ref[g], 1, 0) * sc_ref[:, cols]
            mix_ref[:, ATTN_WIDTH + 128 * g:ATTN_WIDTH + 128 * g + 128] = y.astype(BF16)
        lo = lax.broadcasted_iota(jnp.int32, (t, 128), 1) < 64
        for p in range(4):
            a = _from_stacked(o_ref[2 * p].astype(F32), o_ref[2 * p + 1].astype(F32), p // 2, lo)
            mix_ref[:, 128 * p:128 * p + 128] = a.astype(BF16)
        h1 = x_ref[...] + _dot(mix_ref[...], w_ref[...], 1, 0)
        h1_ref[...] = h1
        hn_ref[...] = _rms_fwd(h1, g_ref[...]).astype(BF16)

    row = lambda w: pl.BlockSpec((t, w), lambda i: (i, 0))
    return pl.pallas_call(
        body, name="mix_out", grid=(s_len // t,),
        in_specs=[row(POOL_WIDTH), pl.BlockSpec((N_Q_HEADS, t, 128), lambda i: (0, i, 0)), row(D_MODEL)] + W_SPECS
        + [_full((4, 128, 128)), _full((1, POOL_WIDTH)), _full((1, D_MODEL))],
        out_specs=[row(POOL_WIDTH), row(D_MODEL), row(D_MODEL), row(D_MODEL)],
        out_shape=[jax.ShapeDtypeStruct((s_len, POOL_WIDTH), BF16), jax.ShapeDtypeStruct((s_len, D_MODEL), BF16),
                   jax.ShapeDtypeStruct((s_len, D_MODEL), F32), jax.ShapeDtypeStruct((s_len, D_MODEL), BF16)],
        scratch_shapes=[pltpu.VMEM((D_MODEL, D_MODEL), BF16), pltpu.VMEM((t + POOL_HALO, POOL_WIDTH), F32),
                        pltpu.VMEM((t + POOL_HALO, POOL_WIDTH), F32), pltpu.SemaphoreType.DMA((N_CHIPS,))],
        compiler_params=_params(),
    )(u, ost, x2, *wts, wpool, pool_scale, g_ffn)


def _ffn_fwd(hn2, h1, wts):
    s_len = h1.shape[0]
    t = 256

    def body(hn_ref, h1_ref, sl_ref, lo_ref, me_ref, gate_ref, up_ref, h2_ref, wg_ref, wu_ref, wd_ref, sems):
        @pl.when(pl.program_id(0) == 0)
        def _():
            w_refs = (sl_ref, lo_ref, me_ref)
            _load_rows(w_refs, "gateT", wg_ref, sems)
            _load_rows(w_refs, "upT", wu_ref, sems)
            _load_rows(w_refs, "down", wd_ref, sems)

        hn = hn_ref[...]
        h2 = h1_ref[...]
        for ch in range(D_FF // FF_CHUNK):
            rows = pl.ds(ch * FF_CHUNK, FF_CHUNK)
            cols = slice(ch * FF_CHUNK, (ch + 1) * FF_CHUNK)
            gate = _dot(hn, wg_ref[rows, :], 1, 1)
            up = _dot(hn, wu_ref[rows, :], 1, 1)
            gate_ref[:, cols] = gate
            up_ref[:, cols] = up
            act = (gate * _sigmoid(gate) * up).astype(BF16)
            h2 = h2 + _dot(act, wd_ref[rows, :], 1, 0)
        h2_ref[...] = h2

    row = lambda w: pl.BlockSpec((t, w), lambda i: (i, 0))
    return pl.pallas_call(
        body, name="ffn_fwd", grid=(s_len // t,),
        in_specs=[row(D_MODEL), row(D_MODEL)] + W_SPECS,
        out_specs=[row(D_FF), row(D_FF), row(D_MODEL)],
        out_shape=[jax.ShapeDtypeStruct((s_len, D_FF), F32), jax.ShapeDtypeStruct((s_len, D_FF), F32),
                   jax.ShapeDtypeStruct((s_len, D_MODEL), F32)],
        scratch_shapes=[pltpu.VMEM((D_FF, D_MODEL), BF16)] * 3 + [pltpu.SemaphoreType.DMA((N_CHIPS,))],
        compiler_params=_params(VMEM_LIMIT_BIG),
    )(hn2, h1, *wts)


def _ple_loss(h2, p2, tgt, wts, g_ple):
    s_len = h2.shape[0]
    t = 512
    n_tiles = s_len // t

    def body(h2_ref, p_ref, tgt_ref, sl_ref, lo_ref, me_ref, g_ref, loss_ref, dh2_ref, dgl_ref, dpp_ref, hn_ref,
             dg_ref, w_ref, wp_ref, packed_ref, loss_acc, sems):
        i = pl.program_id(0)

        @pl.when(i == 0)
        def _():
            w_refs = (sl_ref, lo_ref, me_ref)
            _load_rows(w_refs, "plg", w_ref, sems)
            _load_rows(w_refs, "plp", packed_ref, sems)
            for j in range(N_CHIPS):
                for q in range(4):
                    wp_ref[pl.ds(64 * q, 64), 256 * j:256 * j + 256] = packed_ref[pl.ds(64 * j, 64), 256 * q:256 * q + 256]
            loss_acc[...] = jnp.zeros_like(loss_acc)
            dg_ref[...] = jnp.zeros_like(dg_ref)

        h2v = h2_ref[...]
        g = g_ref[...]
        hn = _rms_fwd(h2v, g).astype(BF16)
        hn_ref[...] = hn
        gate = _sigmoid(_dot(hn, w_ref[...], 1, 0))
        pp = _dot(p_ref[...].astype(BF16), wp_ref[...], 1, 0)
        err = h2v + gate * pp - tgt_ref[...]
        loss_acc[...] += jnp.sum(err * err, axis=0, keepdims=True)
        dy = err * (1.0 / D_MODEL)
        dpp_ref[...] = (dy * gate).astype(BF16)
        dgl = (dy * pp * gate * (1.0 - gate)).astype(BF16)
        dgl_ref[...] = dgl
        dx, dg = _rms_bwd(h2v, g, _dot(dgl, w_ref[...], 1, 1))
        dh2_ref[...] = dy + dx
        dg_ref[...] += dg

        @pl.when(i == n_tiles - 1)
        def _():
            total = jnp.sum(loss_acc[...], axis=-1, keepdims=True) * (0.5 / D_MODEL)
            loss_ref[...] = jnp.broadcast_to(total, loss_ref.shape)

    row = lambda w: pl.BlockSpec((t, w), lambda i: (i, 0))
    return pl.pallas_call(
        body, name="ple_loss", grid=(n_tiles,),
        in_specs=[row(D_MODEL), row(PLE_DIM), row(D_MODEL)] + W_SPECS + [_full((1, D_MODEL))],
        out_specs=[_full((1, 128)), row(D_MODEL), row(D_MODEL), row(D_MODEL), row(D_MODEL), _full((1, D_MODEL))],
        out_shape=[jax.ShapeDtypeStruct((1, 128), F32), jax.ShapeDtypeStruct((s_len, D_MODEL), F32),
                   jax.ShapeDtypeStruct((s_len, D_MODEL), BF16), jax.ShapeDtypeStruct((s_len, D_MODEL), BF16),
                   jax.ShapeDtypeStruct((s_len, D_MODEL), BF16), jax.ShapeDtypeStruct((1, D_MODEL), F32)],
        scratch_shapes=[pltpu.VMEM((D_MODEL, D_MODEL), BF16), pltpu.VMEM((PLE_DIM, D_MODEL), BF16),
                        pltpu.VMEM((PLE_DIM, D_MODEL), BF16), pltpu.VMEM((1, D_MODEL), F32),
                        pltpu.SemaphoreType.DMA((N_CHIPS,))],
        compiler_params=_params(),
    )(h2, p2, tgt, *wts, g_ple)


def _ffn_bwd(dh2, gate, up, h1, wts, g_ffn):
    s_len = h1.shape[0]
    t = 256

    def body(dh2_ref, gate_ref, up_ref, h1_ref, sl_ref, lo_ref, me_ref, g_ref, dgate_ref, dup_ref, act_ref, dh1_ref, dg_ref,
             wg_ref, wu_ref, wd_ref, sems):
        @pl.when(pl.program_id(0) == 0)
        def _():
            w_refs = (sl_ref, lo_ref, me_ref)
            _load_rows(w_refs, "gateT", wg_ref, sems)
            _load_rows(w_refs, "upT", wu_ref, sems)
            _load_rows(w_refs, "down", wd_ref, sems)
            dg_ref[...] = jnp.zeros_like(dg_ref)

        dh2v = dh2_ref[...]
        dh2b = dh2v.astype(BF16)
        dhn = jnp.zeros((t, D_MODEL), F32)
        for ch in range(D_FF // FF_CHUNK):
            rows = pl.ds(ch * FF_CHUNK, FF_CHUNK)
            cols = slice(ch * FF_CHUNK, (ch + 1) * FF_CHUNK)
            dact = _dot(dh2b, wd_ref[rows, :], 1, 1)
            gate_v = gate_ref[:, cols]
            up_v = up_ref[:, cols]
            sg = _sigmoid(gate_v)
            silu = gate_v * sg
            act_ref[:, cols] = (silu * up_v).astype(BF16)
            dup = (dact * silu).astype(BF16)
            dgate = (dact * up_v * (sg * (1.0 + gate_v * (1.0 - sg)))).astype(BF16)
            dup_ref[:, cols] = dup
            dgate_ref[:, cols] = dgate
            dhn = dhn + _dot(dgate, wg_ref[rows, :], 1, 0) + _dot(dup, wu_ref[rows, :], 1, 0)
        dx, dg = _rms_bwd(h1_ref[...], g_ref[...], dhn)
        dh1_ref[...] = dh2v + dx
        dg_ref[...] += dg

    row = lambda w: pl.BlockSpec((t, w), lambda i: (i, 0))
    return pl.pallas_call(
        body, name="ffn_bwd", grid=(s_len // t,),
        in_specs=[row(D_MODEL), row(D_FF), row(D_FF), row(D_MODEL)] + W_SPECS + [_full((1, D_MODEL))],
        out_specs=[row(D_FF), row(D_FF), row(D_FF), row(D_MODEL), _full((1, D_MODEL))],
        out_shape=[jax.ShapeDtypeStruct((s_len, D_FF), BF16), jax.ShapeDtypeStruct((s_len, D_FF), BF16),
                   jax.ShapeDtypeStruct((s_len, D_FF), BF16), jax.ShapeDtypeStruct((s_len, D_MODEL), F32),
                   jax.ShapeDtypeStruct((1, D_MODEL), F32)],
        scratch_shapes=[pltpu.VMEM((D_FF, D_MODEL), BF16)] * 3 + [pltpu.SemaphoreType.DMA((N_CHIPS,))],
        compiler_params=_params(VMEM_LIMIT_BIG),
    )(dh2, gate, up, h1, *wts, g_ffn)


def _mix_out_bwd(dh1, wts, pooled, wpool, pool_scale, after):
    s_len = dh1.shape[0]
    t = 512
    n = t + 16
    n_tiles = s_len // t

    def body(dh1_ref, sl_ref, lo_ref, me_ref, pooled_ref, wp_ref, sc_ref, after_ref, dost_ref, du_ref, dyp_ref, dsc_ref,
             w_ref, ext_ref, st_ref, sems):
        del after_ref
        i = pl.program_id(0)

        @pl.when(i == 0)
        def _():
            _load_rows((sl_ref, lo_ref, me_ref), "out", w_ref, sems)
            ext_ref[...] = jnp.zeros_like(ext_ref)
            st_ref[...] = jnp.zeros_like(st_ref)
            dsc_ref[...] = jnp.zeros_like(dsc_ref)

        dmix = _dot(dh1_ref[...].astype(BF16), w_ref[...], 1, 1)
        lo = lax.broadcasted_iota(jnp.int32, (t, 128), 1) < 64
        for p in range(4):
            even, odd = _to_stacked(dmix[:, 128 * p:128 * p + 128], p // 2, lo)
            dost_ref[2 * p] = even.astype(BF16)
            dost_ref[2 * p + 1] = odd.astype(BF16)
        pooled_v = pooled_ref[...]
        counts = _pool_counts(n_tiles - 1 - i, t)
        for g in range(4):
            cols = slice(128 * g, 128 * g + 128)
            dm = dmix[:, ATTN_WIDTH + 128 * g:ATTN_WIDTH + 128 * g + 128]
            ypre = _dot(pooled_v[:, cols], wp_ref[g], 1, 0)
            dsc_ref[:, cols] += jnp.sum(ypre * dm, axis=0, keepdims=True)
            dyp = (dm * sc_ref[:, cols]).astype(BF16)
            dyp_ref[:, cols] = dyp
            dpooled = _dot(dyp, wp_ref[g], 1, 1)
            du_ref[:, cols] = -dpooled
            ext_ref[pl.ds(0, t), cols] = dpooled / counts[:, cols]
        st_ref[pl.ds(0, n), :] = ext_ref[pl.ds(0, n), :] + ext_ref[pl.ds(1, n), :]
        st_ref[pl.ds(0, n), 128:] = st_ref[pl.ds(0, n), 128:] + st_ref[pl.ds(2, n), 128:]
        st_ref[pl.ds(0, n), 256:] = st_ref[pl.ds(0, n), 256:] + st_ref[pl.ds(4, n), 256:]
        st_ref[pl.ds(0, n), 384:] = st_ref[pl.ds(0, n), 384:] + st_ref[pl.ds(8, n), 384:]
        ext_ref[pl.ds(t, POOL_HALO), :] = ext_ref[pl.ds(0, POOL_HALO), :]
        du_ref[...] += st_ref[pl.ds(0, t), :]

    rev = lambda w: pl.BlockSpec((t, w), lambda i: (n_tiles - 1 - i, 0))
    return pl.pallas_call(
        body, name="mix_out_bwd", grid=(n_tiles,),
        in_specs=[rev(D_MODEL)] + W_SPECS + [rev(POOL_WIDTH), _full((4, 128, 128)), _full((1, POOL_WIDTH)), ANY],
        out_specs=[pl.BlockSpec((N_Q_HEADS, t, 128), lambda i: (0, n_tiles - 1 - i, 0)), rev(POOL_WIDTH), rev(POOL_WIDTH),
                   _full((1, POOL_WIDTH))],
        out_shape=[jax.ShapeDtypeStruct((N_Q_HEADS, s_len, 128), BF16), jax.ShapeDtypeStruct((s_len, POOL_WIDTH), F32),
                   jax.ShapeDtypeStruct((s_len, POOL_WIDTH), BF16), jax.ShapeDtypeStruct((1, POOL_WIDTH), F32)],
        scratch_shapes=[pltpu.VMEM((D_MODEL, D_MODEL), BF16), pltpu.VMEM((t + POOL_HALO, POOL_WIDTH), F32),
                        pltpu.VMEM((t + POOL_HALO, POOL_WIDTH), F32), pltpu.SemaphoreType.DMA((N_CHIPS,))],
        compiler_params=_params(),
    )(dh1, *wts, pooled, wpool, pool_scale, after)


def _attn_bwd(qst, kn, vb, dost, bias_st, sink_st, after):
    s_len = kn.shape[0]

    def body(q_ref, kp_ref, kc_ref, vp_ref, vc_ref, do_ref, bias_ref, sink_ref, after_ref, dq_ref, dk_ref, dv_ref, dbias_ref,
             dsink_ref):
        del after_ref
        i = pl.program_id(0)

        @pl.when(i == 0)
        def _():
            dk_ref[...] = jnp.zeros_like(dk_ref)
            dv_ref[...] = jnp.zeros_like(dv_ref)
            dbias_ref[...] = jnp.zeros_like(dbias_ref)
            dsink_ref[...] = jnp.zeros_like(dsink_ref)

        q, k2, probs, p_sink = _band_softmax(q_ref, kp_ref, kc_ref, bias_ref, sink_ref, i)
        v2 = jnp.concatenate([vp_ref[...], vc_ref[...]], axis=0)
        do = do_ref[...].reshape(N_Q_HEADS * BLOCK, 128)
        dp = _dot(do, v2, 1, 1)
        dsum = jnp.sum(probs * dp, axis=-1, keepdims=True)
        dlog = probs * (dp - dsum)
        dsink_ref[...] -= p_sink * dsum
        dbias_ref[...] += dlog
        dlog_s = (dlog * (HEAD_DIM ** -0.5)).astype(BF16)
        dq = _dot(dlog_s, k2, 1, 0)
        dq_ref[...] = jnp.where(_head_lane_mask(), dq, 0.0).reshape(N_Q_HEADS, BLOCK, 128)
        dk2 = _dot(dlog_s, q, 0, 0)
        dv2 = _dot(probs.astype(BF16), do, 0, 0)
        prev_rows = pl.ds(pl.multiple_of(jnp.maximum(i - 1, 0) * BLOCK, BLOCK), BLOCK)
        cur_rows = pl.ds(pl.multiple_of(i * BLOCK, BLOCK), BLOCK)
        dk_ref[prev_rows, :] += dk2[:BLOCK]
        dk_ref[cur_rows, :] += dk2[BLOCK:]
        dv_ref[prev_rows, :] += dv2[:BLOCK]
        dv_ref[cur_rows, :] += dv2[BLOCK:]

    stacked, kv, consts = _attn_specs()
    return pl.pallas_call(
        body, name="attn_bwd", grid=(s_len // BLOCK,),
        in_specs=[stacked] + kv + kv + [stacked] + consts + [ANY],
        out_specs=[stacked, _full((s_len, 128)), _full((s_len, 128)), _full((N_Q_HEADS * BLOCK, 2 * BLOCK)),
                   _full((N_Q_HEADS * BLOCK, 1))],
        out_shape=[jax.ShapeDtypeStruct((N_Q_HEADS, s_len, 128), F32), jax.ShapeDtypeStruct((s_len, 128), F32),
                   jax.ShapeDtypeStruct((s_len, 128), F32), jax.ShapeDtypeStruct((N_Q_HEADS * BLOCK, 2 * BLOCK), F32),
                   jax.ShapeDtypeStruct((N_Q_HEADS * BLOCK, 1), F32)],
        compiler_params=_params(),
    )(qst, kn, kn, vb, vb, dost, bias_st, sink_st, after)


def _small_pack(dg_attn, dg_ffn, dg_ple, dscale, dgq, dgk, dbias, dsink_rows, bucket, loss_v, dwpool):
    def body(ga_ref, gf_ref, gp_ref, sc_ref, gq_ref, gk_ref, db_ref, ds_ref, bucket_ref, loss_ref, wp_ref, out_ref):
        out_ref[pl.ds(0, SMALL["w_pool"]), :] = jnp.zeros((SMALL["w_pool"], 128), F32)
        for name, ref, n in (("g_attn", ga_ref, 8), ("g_ffn", gf_ref, 8), ("g_ple", gp_ref, 8), ("pool_scale", sc_ref, 4)):
            for k in range(n):
                out_ref[pl.ds(SMALL[name] + k, 1), :] = ref[:, 128 * k:128 * k + 128]
        for name, ref in (("g_q", gq_ref), ("g_k", gk_ref)):
            both = ref[...]
            out_ref[pl.ds(SMALL[name], 1), :] = both + pltpu.roll(both, 64, axis=1)
        out_ref[pl.ds(SMALL["loss"], 1), :] = loss_ref[...]
        bk = bucket_ref[...]
        rows = lax.broadcasted_iota(jnp.int32, (N_BUCKETS, 128), 0)
        lanes = lax.broadcasted_iota(jnp.int32, (N_BUCKETS, 128), 1)
        lane1 = lax.broadcasted_iota(jnp.int32, (1, 128), 1)
        rb = jnp.zeros((N_BUCKETS, 128), F32)
        sk = jnp.zeros((1, 128), F32)
        for h in range(N_Q_HEADS):
            band = db_ref[pl.ds(h * BLOCK, BLOCK), :]
            for b in range(N_BUCKETS):
                rb = jnp.where((rows == b) & (lanes == h), jnp.sum(jnp.where(bk == b, band, 0.0)), rb)
            sk = jnp.where(lane1 == h, jnp.sum(ds_ref[pl.ds(h * BLOCK, BLOCK), :]), sk)
        out_ref[pl.ds(SMALL["rel_bias"], N_BUCKETS), :] = rb
        out_ref[pl.ds(SMALL["sinks"], 1), :] = sk
        out_ref[pl.ds(SMALL["w_pool"], 512), :] = wp_ref[...].reshape(512, 128)

    return pl.pallas_call(
        body, name="small_pack", in_specs=[VMEM_WHOLE] * 11, out_specs=VMEM_WHOLE,
        out_shape=jax.ShapeDtypeStruct((SMALL_ROWS, 128), F32),
    )(dg_attn, dg_ffn, dg_ple, dscale, dgq, dgk, dbias, dsink_rows, bucket, loss_v, dwpool)


def _attn_in_bwd(dqst, zqk, dk, dv, du, x2, dh1, wts, g_attn, gq, gk):
    s_len = x2.shape[0]
    t = 512

    def body(dq_ref, zqk_ref, dk_ref, dv_ref, du_ref, x_ref, dh1_ref, sl_ref, lo_ref, me_ref, g_ref, gq_ref, gk_ref,
             dz_ref, dx_ref, dg_ref, dgq_ref, dgk_ref, w_ref, sems):
        @pl.when(pl.program_id(0) == 0)
        def _():
            _load_rows((sl_ref, lo_ref, me_ref), "inT", w_ref, sems)
            dg_ref[...] = jnp.zeros_like(dg_ref)
            dgq_ref[...] = jnp.zeros_like(dgq_ref)
            dgk_ref[...] = jnp.zeros_like(dgk_ref)

        lo = lax.broadcasted_iota(jnp.int32, (t, 128), 1) < 64
        for p in range(4):
            dqn = _from_stacked(dq_ref[2 * p], dq_ref[2 * p + 1], p // 2, lo)
            dq_raw, dgq = _pair_norm_bwd(zqk_ref[:, 128 * p:128 * p + 128], gq_ref[...], dqn, lo)
            dz_ref[:, 128 * p:128 * p + 128] = dq_raw.astype(BF16)
            dgq_ref[...] += dgq
        dk_raw, dgk = _pair_norm_bwd(zqk_ref[:, 512:640], gk_ref[...], dk_ref[...], lo)
        dgk_ref[...] += dgk
        dz_ref[:, 512:640] = dk_raw.astype(BF16)
        dz_ref[:, 640:768] = dv_ref[...].astype(BF16)
        dz_ref[:, 768:] = du_ref[...].astype(BF16)
        dx, dg = _rms_bwd(x_ref[...], g_ref[...], _dot(dz_ref[...], w_ref[...], 1, 0))
        dx_ref[...] = dh1_ref[...] + dx
        dg_ref[...] += dg

    row = lambda w: pl.BlockSpec((t, w), lambda i: (i, 0))
    return pl.pallas_call(
        body, name="attn_in_bwd", grid=(s_len // t,),
        in_specs=[pl.BlockSpec((N_Q_HEADS, t, 128), lambda i: (0, i, 0)), row(640), row(128), row(128), row(POOL_WIDTH),
                  row(D_MODEL), row(D_MODEL)] + W_SPECS + [_full((1, D_MODEL)), _full((1, 128)), _full((1, 128))],
        out_specs=[row(IN_WIDTH), row(D_MODEL), _full((1, D_MODEL)), _full((1, 128)), _full((1, 128))],
        out_shape=[jax.ShapeDtypeStruct((s_len, IN_WIDTH), BF16), jax.ShapeDtypeStruct((s_len, D_MODEL), F32),
                   jax.ShapeDtypeStruct((1, D_MODEL), F32), jax.ShapeDtypeStruct((1, 128), F32),
                   jax.ShapeDtypeStruct((1, 128), F32)],
        scratch_shapes=[pltpu.VMEM((IN_WIDTH, D_MODEL), BF16), pltpu.SemaphoreType.DMA((N_CHIPS,))],
        compiler_params=_params(),
    )(dqst, zqk, dk, dv, du, x2, dh1, *wts, g_attn, gq, gk)


def _dw(a, b, name):
    s_len, m = a.shape
    n_out = b.shape[1]
    tk = 512
    n_steps = s_len // tk
    tm = m // 2 if m > 1408 else m

    def body(a_ref, b_ref, o_ref, acc_ref):
        k = pl.program_id(1)

        @pl.when(k == 0)
        def _():
            acc_ref[...] = jnp.zeros_like(acc_ref)

        acc_ref[...] += _dot(a_ref[...].astype(BF16), b_ref[...].astype(BF16), 0, 0)

        @pl.when(k == n_steps - 1)
        def _():
            o_ref[...] = acc_ref[...].astype(BF16)

    return pl.pallas_call(
        body, name=name, grid=(m // tm, n_steps),
        in_specs=[pl.BlockSpec((tk, tm), lambda i, k: (k, i)), pl.BlockSpec((tk, n_out), lambda i, k: (k, 0))],
        out_specs=pl.BlockSpec((tm, n_out), lambda i, k: (i, 0)),
        out_shape=jax.ShapeDtypeStruct((m, n_out), BF16),
        scratch_shapes=[pltpu.VMEM((tm, n_out), F32)],
        compiler_params=_params(n_axes=2),
    )(a, b)


def _dw_pool(pooled, dyp):
    s_len = pooled.shape[0]
    tk = 512

    def body(a_ref, b_ref, o_ref):
        @pl.when(pl.program_id(0) == 0)
        def _():
            o_ref[...] = jnp.zeros_like(o_ref)

        for g in range(4):
            cols = slice(128 * g, 128 * g + 128)
            o_ref[g] += _dot(a_ref[:, cols], b_ref[:, cols], 0, 0)

    blk = pl.BlockSpec((tk, POOL_WIDTH), lambda k: (k, 0))
    return pl.pallas_call(
        body, name="dw_pool", grid=(s_len // tk,), in_specs=[blk, blk], out_specs=_full((4, 128, 128)),
        out_shape=jax.ShapeDtypeStruct((4, 128, 128), F32), compiler_params=_params(),
    )(pooled, dyp)


def _position():
    x, y, c = lax.axis_index("x"), lax.axis_index("y"), lax.axis_index("c")
    other_chips = [(1 - x, y), (x, 1 - y), (1 - x, 1 - y)]
    return x, y, c, other_chips


def _ag_weights(local_slab, row0, n_rows, name, collective_id):
    half = n_rows // 2

    def body(l_ref, g_ref, send, recv):
        x, y, c, chips = _position()
        me = 2 * x + y
        sibling = (x, y, 1 - c)
        peers = [sibling] + [(*chip, c) for chip in chips]
        barrier = pltpu.get_barrier_semaphore()
        for peer in peers:
            pl.semaphore_signal(barrier, inc=1, device_id=peer, device_id_type=MESH)
        pl.semaphore_wait(barrier, len(peers))
        mine = pl.ds(pl.multiple_of(c * half, 16), half)
        theirs = pl.ds(pl.multiple_of((1 - c) * half, 16), half)

        def copy(k, chip_idx, rows, to, src=None):
            dst = g_ref.at[chip_idx, rows, :]
            return pltpu.make_async_remote_copy(src_ref=dst if src is None else src, dst_ref=dst, send_sem=send.at[k],
                                                recv_sem=recv.at[k], device_id=to, device_id_type=MESH)

        own_rows = l_ref.at[pl.ds(pl.multiple_of(row0 + c * half, 16), half), :]
        first = [copy(k, me, mine, (*chip, c), src=own_rows) for k, chip in enumerate(chips)]
        for cp in first:
            cp.start()
        passed = []
        for k, chip in enumerate(chips):
            idx = 2 * chip[0] + chip[1]
            copy(k, idx, mine, (x, y, c)).wait_recv()
            fwd = copy(3 + k, idx, mine, sibling)
            fwd.start()
            passed.append(fwd)
        for k, chip in enumerate(chips):
            copy(3 + k, 2 * chip[0] + chip[1], theirs, (x, y, c)).wait_recv()
        for cp in first + passed:
            cp.wait_send()

    return pl.kernel(
        body, out_type=jax.ShapeDtypeStruct((N_CHIPS, n_rows, D_MODEL), BF16),
        mesh=plsc.ScalarSubcoreMesh(axis_name="sequencer", num_cores=1), name=name,
        scratch_types=[pltpu.SemaphoreType.DMA((6,)), pltpu.SemaphoreType.DMA((6,))],
        compiler_params=pltpu.CompilerParams(collective_id=collective_id),
    )(local_slab)


def _comm_call(body, peers_of, out_shape, n_sems, operand, name, collective_id):
    sems = [pltpu.SemaphoreType.DMA((n_sems,)), pltpu.SemaphoreType.DMA((n_sems,))]
    if collective_id is None:
        return pl.pallas_call(body, name=name, in_specs=[ANY], out_specs=ANY, out_shape=out_shape, scratch_shapes=sems)(operand)

    def with_handshake(in_ref, out_ref, send, recv):
        x, y, c, _ = _position()
        peers = peers_of(x, y, c)
        barrier = pltpu.get_barrier_semaphore()
        for peer in peers:
            pl.semaphore_signal(barrier, inc=1, device_id=peer, device_id_type=MESH)
        pl.semaphore_wait(barrier, len(peers))
        body(in_ref, out_ref, send, recv)

    return pl.kernel(with_handshake, out_type=out_shape, mesh=plsc.ScalarSubcoreMesh(axis_name="sequencer", num_cores=1),
                     name=name, scratch_types=sems, compiler_params=pltpu.CompilerParams(collective_id=collective_id))(operand)


def _rs_swap_halves(partial, name, collective_id=None):
    half = partial.shape[1] // 2

    def body(p_ref, r_ref, send, recv):
        x, y, c, _ = _position()
        theirs = pl.ds(pl.multiple_of((1 - c) * half, 16), half)
        cp = pltpu.make_async_remote_copy(src_ref=p_ref.at[:, theirs, :], dst_ref=r_ref, send_sem=send.at[0],
                                          recv_sem=recv.at[0], device_id=(x, y, 1 - c), device_id_type=MESH)
        cp.start()
        cp.wait()

    return _comm_call(body, lambda x, y, c: [(x, y, 1 - c)], jax.ShapeDtypeStruct((N_CHIPS, half, D_MODEL), BF16), 1,
                      partial, name, collective_id)


def _rs_add_halves(partial, other, core, name):
    half = other.shape[1]
    t = half // 2
    steps = half // t

    def body(core_ref, a_ref, b_ref, o_ref):
        o_ref[...] = (a_ref[...].astype(F32) + b_ref[...].astype(F32)).astype(BF16)

    return pl.pallas_call(
        body, name=name,
        grid_spec=pltpu.PrefetchScalarGridSpec(
            num_scalar_prefetch=1, grid=(N_CHIPS, steps),
            in_specs=[pl.BlockSpec((1, t, D_MODEL), lambda j, i, core_ref: (j, core_ref[0] * steps + i, 0)),
                      pl.BlockSpec((1, t, D_MODEL), lambda j, i, core_ref: (j, i, 0))],
            out_specs=pl.BlockSpec((1, t, D_MODEL), lambda j, i, core_ref: (j, i, 0))),
        out_shape=jax.ShapeDtypeStruct((N_CHIPS, half, D_MODEL), BF16),
        compiler_params=_params(n_axes=2),
    )(core, partial, other)


def _rs_exchange_chips(pre, name, collective_id=None):
    def body(s_ref, r_ref, send, recv):
        x, y, c, chips = _position()

        def copy(k, chunk, to):
            return pltpu.make_async_remote_copy(src_ref=s_ref.at[chunk], dst_ref=r_ref.at[k], send_sem=send.at[k],
                                                recv_sem=recv.at[k], device_id=to, device_id_type=MESH)

        sends = [copy(k, 2 * chip[0] + chip[1], (*chip, c)) for k, chip in enumerate(chips)]
        for cp in sends:
            cp.start()
        for cp in sends:
            cp.wait()

    return _comm_call(body, lambda x, y, c: [(1 - x, y, c), (x, 1 - y, c), (1 - x, 1 - y, c)],
                      jax.ShapeDtypeStruct((3, pre.shape[1], D_MODEL), BF16), 3, pre, name, collective_id)


def _rs_sum_chips(pre, received, place, name):
    half = pre.shape[1]
    t = half // 2 if half > 512 else half
    steps = half // t

    def body(place_ref, own_ref, r_ref, o_ref):
        acc = own_ref[0].astype(F32)
        for k in range(3):
            acc = acc + r_ref[k].astype(F32)
        o_ref[...] = acc

    return pl.pallas_call(
        body, name=name,
        grid_spec=pltpu.PrefetchScalarGridSpec(
            num_scalar_prefetch=1, grid=(steps,),
            in_specs=[pl.BlockSpec((1, t, D_MODEL), lambda i, place_ref: (place_ref[0], i, 0)),
                      pl.BlockSpec((3, t, D_MODEL), lambda i, place_ref: (0, i, 0))],
            out_specs=pl.BlockSpec((t, D_MODEL), lambda i, place_ref: (place_ref[1] * steps + i, 0))),
        out_shape=jax.ShapeDtypeStruct((2 * half, D_MODEL), F32),
        compiler_params=_params(),
    )(place, pre, received)


def _rs_finish(grads_a, grads_b, small):
    def body(fa_ref, fb_ref, s_ref, ga_ref, gb_ref, t_ref, send, recv, local_sem):
        del fa_ref, fb_ref
        x, y, c, chips = _position()
        sibling = (x, y, 1 - c)

        def slot(px, py, pc):
            return t_ref.at[4 * px + 2 * py + pc]

        def copy(k, block, to, src=None):
            return pltpu.make_async_remote_copy(src_ref=slot(*block) if src is None else src, dst_ref=slot(*block),
                                                send_sem=send.at[k], recv_sem=recv.at[k], device_id=to, device_id_type=MESH)

        def half_copies(core, to):
            out = []
            for k, g_ref in ((7, ga_ref), (8, gb_ref)):
                half = g_ref.shape[0] // 2
                rows = g_ref.at[pl.ds(pl.multiple_of(core * half, 8), half), :]
                out.append(pltpu.make_async_remote_copy(src_ref=rows, dst_ref=rows, send_sem=send.at[k], recv_sem=recv.at[k],
                                                        device_id=to, device_id_type=MESH))
            return out

        own_small = pltpu.make_async_copy(s_ref, slot(x, y, c), local_sem)
        own_small.start()
        to_sibling = half_copies(c, sibling)
        for cp in to_sibling:
            cp.start()
        first = [copy(0, (x, y, c), sibling, src=s_ref)]
        first += [copy(1 + k, (x, y, c), (*chip, c), src=s_ref) for k, chip in enumerate(chips)]
        for cp in first:
            cp.start()
        passed = []
        for k, chip in enumerate(chips):
            copy(1 + k, (*chip, c), (x, y, c)).wait_recv()
            fwd = copy(4 + k, (*chip, c), sibling)
            fwd.start()
            passed.append(fwd)
        copy(0, sibling, (x, y, c)).wait_recv()
        for k, chip in enumerate(chips):
            copy(4 + k, (*chip, 1 - c), (x, y, c)).wait_recv()
        for cp in half_copies(1 - c, (x, y, c)):
            cp.wait_recv()
        for cp in first + passed + to_sibling:
            cp.wait_send()
        own_small.wait()

    return pl.pallas_call(
        body, name="rs_finish", in_specs=[ANY, ANY, ANY], out_specs=[ANY, ANY, ANY], input_output_aliases={0: 0, 1: 1},
        out_shape=[jax.ShapeDtypeStruct(grads_a.shape, F32), jax.ShapeDtypeStruct(grads_b.shape, F32),
                   jax.ShapeDtypeStruct((N_DEV, SMALL_ROWS, 128), F32)],
        scratch_shapes=[pltpu.SemaphoreType.DMA((9,)), pltpu.SemaphoreType.DMA((9,)), pltpu.SemaphoreType.DMA],
    )(grads_a, grads_b, small)


def _adam_update(w, g, m, v):
    m_new = ADAM_B1 * m + (1.0 - ADAM_B1) * g
    v_new = ADAM_B2 * v + (1.0 - ADAM_B2) * (g * g)
    m_hat = m_new / (1.0 - ADAM_B1 ** ADAM_STEP)
    v_hat = v_new / (1.0 - ADAM_B2 ** ADAM_STEP)
    return -ADAM_LR * (m_hat / (jnp.sqrt(v_hat) + ADAM_EPS) + ADAM_WD * w), m_new, v_new


def _adamw(w, g, m, v, name):
    rows, cols = w.shape
    t = rows if rows % 256 else 256

    def body(w_ref, g_ref, m_ref, v_ref, d_ref, nm_ref, nv_ref):
        d_ref[...], nm_ref[...], nv_ref[...] = _adam_update(w_ref[...], g_ref[...], m_ref[...], v_ref[...])

    blk = pl.BlockSpec((t, cols), lambda i: (i, 0))
    shape = jax.ShapeDtypeStruct((rows, cols), F32)
    return pl.pallas_call(
        body, name=name, grid=(rows // t,), in_specs=[blk] * 4, out_specs=[blk] * 3, out_shape=[shape] * 3,
        compiler_params=_params(),
    )(w, g, m, v)


SMALL_PARAMS = [("g_attn", (1, D_MODEL), 8), ("g_q", (1, HEAD_DIM), None), ("g_k", (1, HEAD_DIM), None),
                ("sinks", (1, N_Q_HEADS), None), ("rel_bias", (N_BUCKETS, N_Q_HEADS), None), ("w_pool", (512, 128), None),
                ("pool_scale", (1, POOL_WIDTH), 4), ("g_ffn", (1, D_MODEL), 8), ("g_ple", (1, D_MODEL), 8)]


def _adamw_small(tables, wmv):
    n_par = len(SMALL_PARAMS)

    def body(*refs):
        t_ref = refs[0]
        ins = refs[1:1 + 3 * n_par]
        loss_ref = refs[1 + 3 * n_par]
        outs = refs[2 + 3 * n_par:-1]
        tot_ref = refs[-1]
        total = t_ref[0]
        for d in range(1, N_DEV):
            total = total + t_ref[d]
        tot_ref[...] = total
        loss_ref[...] = tot_ref[pl.ds(SMALL["loss"], 1), 0:1]
        for i, (name, shape, split) in enumerate(SMALL_PARAMS):
            g_ref, d_ref, nm_ref, nv_ref = outs[4 * i:4 * i + 4]
            row = SMALL[name]
            if split:
                for k in range(split):
                    g_ref[:, 128 * k:128 * k + 128] = tot_ref[pl.ds(row + k, 1), :]
            else:
                g_ref[...] = tot_ref[pl.ds(row, shape[0]), 0:shape[1]]
            w_ref, m_ref, v_ref = ins[3 * i:3 * i + 3]
            d_ref[...], nm_ref[...], nv_ref[...] = _adam_update(w_ref[...], g_ref[...], m_ref[...], v_ref[...])

    shapes = [jax.ShapeDtypeStruct((1, 1), F32)]
    for _, shape, _ in SMALL_PARAMS:
        shapes += [jax.ShapeDtypeStruct(shape, F32)] * 4
    flat = [a for triple in wmv for a in triple]
    res = pl.pallas_call(
        body, name="adamw_small", in_specs=[VMEM_WHOLE] * (1 + 3 * n_par), out_specs=[VMEM_WHOLE] * len(shapes),
        out_shape=shapes, scratch_shapes=[pltpu.VMEM((SMALL_ROWS, 128), F32)],
    )(tables, *flat)
    return res[0], [res[1 + 4 * i:5 + 4 * i] for i in range(n_par)]


def _pack_ple_proj(shard):
    return shard.reshape(4, 64, 256).transpose(1, 0, 2).reshape(64, D_MODEL)


class _Reduction:
    def __init__(self, tag, place, ids=(None, None)):
        self.tag, self.place, self.ids = tag, place, ids

    def start(self, partial):
        self.partial = partial
        self.other = _rs_swap_halves(partial, "rs_swap_" + self.tag, self.ids[0])
        return partial

    def middle(self):
        self.pre = _rs_add_halves(self.partial, self.other, self.place[1:], "rs_add_" + self.tag)
        self.received = _rs_exchange_chips(self.pre, "rs_exchange_" + self.tag, self.ids[1])
        return self.pre

    def finish(self):
        return _rs_sum_chips(self.pre, self.received, self.place, "rs_sum_" + self.tag)


def _local_grads(x2, p2, tgt, wts, g_attn_norm, g_q, g_k, attn_sinks, rel_bias, w_pool, pool_scale, g_ffn_norm, g_ple_norm,
                 reduce_a):
    early, late, local_slab, me = wts
    w_early, w_late = (early, local_slab, me), (late, local_slab, me)
    bucket = jnp.asarray(_bucket_table())
    gq = jnp.tile(g_q, (1, 2))
    gk = jnp.tile(g_k, (1, 2))
    wpool = w_pool[0].astype(BF16)
    sink_st = jnp.repeat(attn_sinks[0], BLOCK)[:, None]
    bias_st = _bias_build(rel_bias.T, bucket)

    hn1, zqk, u, kn, vb, qst = _attn_in(x2, g_attn_norm, gq, gk, w_early)
    ost = _attn_fwd(qst, kn, vb, bias_st, sink_st)
    pooled, mix, h1, hn2 = _mix_out(u, ost, x2, w_early, wpool, pool_scale, g_ffn_norm)
    gate, up, h2 = _ffn_fwd(hn2, h1, w_late)
    loss_v, dh2, dgl, dpp, hn3, dg_ple = _ple_loss(h2, p2, tgt, w_late, g_ple_norm)

    dgate, dup, act, dh1, dg_ffn = _ffn_bwd(dh2, gate, up, h1, w_late, g_ffn_norm)
    partial_a = reduce_a.start(jnp.concatenate([
        _dw(mix, dh1, "dw_out").reshape(N_CHIPS, -1, D_MODEL),
        _dw(dgate, hn2, "dw_gate").reshape(N_CHIPS, -1, D_MODEL),
        _dw(dup, hn2, "dw_up").reshape(N_CHIPS, -1, D_MODEL),
        _dw(act, dh2, "dw_down").reshape(N_CHIPS, -1, D_MODEL),
        _dw(hn3, dgl, "dw_ple_gate").reshape(N_CHIPS, -1, D_MODEL),
        _dw(p2, dpp, "dw_ple_proj").reshape(4, 64, N_CHIPS, 256).transpose(2, 1, 0, 3).reshape(N_CHIPS, 64, D_MODEL),
    ], axis=1))
    dost, du, dyp, dscale = _mix_out_bwd(dh1, w_early, pooled, wpool, pool_scale, partial_a)
    pre_a = reduce_a.middle()
    dqst, dk, dv, dbias, dsink_rows = _attn_bwd(qst, kn, vb, dost, bias_st, sink_st, pre_a)
    dz, dx, dg_attn, dgq, dgk = _attn_in_bwd(dqst, zqk, dk, dv, du, x2, dh1, w_early, g_attn_norm, gq, gk)

    partial_b = _dw(dz, hn1, "dw_in").reshape(N_CHIPS, -1, D_MODEL)
    small = _small_pack(dg_attn, dg_ffn, dg_ple, dscale, dgq, dgk, dbias, dsink_rows, bucket, loss_v, _dw_pool(pooled, dyp))
    return dx, partial_b, small


def kernel(x, p, w_in, w_out, g_attn_norm, g_q, g_k, attn_sinks, rel_bias, w_pool, pool_scale, g_ffn_norm, w_gate, w_up, w_down, g_ple_norm, w_ple_gate, w_ple_proj, loss_target, m_w_in, m_w_out, m_g_attn_norm, m_g_q, m_g_k, m_attn_sinks, m_rel_bias, m_w_pool, m_pool_scale, m_g_ffn_norm, m_w_gate, m_w_up, m_w_down, m_g_ple_norm, m_w_ple_gate, m_w_ple_proj, v_w_in, v_w_out, v_g_attn_norm, v_g_q, v_g_k, v_attn_sinks, v_rel_bias, v_w_pool, v_pool_scale, v_g_ffn_norm, v_w_gate, v_w_up, v_w_down, v_g_ple_norm, v_w_ple_gate, v_w_ple_proj):
    core = lax.axis_index("c").astype(jnp.int32).reshape(1)
    me = (2 * lax.axis_index("x") + lax.axis_index("y")).astype(jnp.int32).reshape(1)

    local_slab = jnp.concatenate(
        [w_in[0].T, w_out[0], w_gate[0].T, w_up[0].T, w_down[0], w_ple_gate[0], _pack_ple_proj(w_ple_proj[0])],
        axis=0).astype(BF16)
    wts = (_ag_weights(local_slab, 0, EARLY_ROWS, "ag_early", 1),
           _ag_weights(local_slab, EARLY_ROWS, SLAB_ROWS - EARLY_ROWS, "ag_late", 2), local_slab, me)

    place = jnp.concatenate([me, core])
    reduce_a = _Reduction("a", place, ids=(3, 4))
    dx, partial_b, small = _local_grads(x[0], p[0, 0], loss_target[0], wts, g_attn_norm, g_q, g_k, attn_sinks, rel_bias,
                                        w_pool, pool_scale, g_ffn_norm, g_ple_norm, reduce_a)
    reduce_b = _Reduction("b", place)
    reduce_b.start(partial_b)
    reduce_b.middle()
    grads_a, grads_b, small_all = _rs_finish(reduce_a.finish(), reduce_b.finish(), small)

    def rows(name):
        off, n_rows = SLAB[name]
        if name == "inT":
            return grads_b
        return grads_a[off - SLAB["inT"][1]:off - SLAB["inT"][1] + n_rows]

    big = {
        "w_in": (w_in, m_w_in, v_w_in, rows("inT").T),
        "w_out": (w_out, m_w_out, v_w_out, rows("out")),
        "w_gate": (w_gate, m_w_gate, v_w_gate, rows("gateT").T),
        "w_up": (w_up, m_w_up, v_w_up, rows("upT").T),
        "w_down": (w_down, m_w_down, v_w_down, rows("down")),
        "w_ple_gate": (w_ple_gate, m_w_ple_gate, v_w_ple_gate, rows("plg")),
        "w_ple_proj": (w_ple_proj, m_w_ple_proj, v_w_ple_proj,
                       rows("plp").reshape(64, 4, 256).transpose(1, 0, 2).reshape(PLE_DIM, PLE_DIM)),
    }
    small_params = {
        "g_attn_norm": (g_attn_norm, m_g_attn_norm, v_g_attn_norm), "g_q": (g_q, m_g_q, v_g_q), "g_k": (g_k, m_g_k, v_g_k),
        "attn_sinks": (attn_sinks, m_attn_sinks, v_attn_sinks), "rel_bias": (rel_bias, m_rel_bias, v_rel_bias),
        "w_pool": tuple(a.reshape(512, 128) for a in (w_pool, m_w_pool, v_w_pool)),
        "pool_scale": (pool_scale, m_pool_scale, v_pool_scale), "g_ffn_norm": (g_ffn_norm, m_g_ffn_norm, v_g_ffn_norm),
        "g_ple_norm": (g_ple_norm, m_g_ple_norm, v_g_ple_norm),
    }

    grads, deltas, new_ms, new_vs = {}, {}, {}, {}
    for name, (w, m, v, g2) in big.items():
        d, nm, nv = _adamw(w[0], g2, m[0], v[0], "adamw_" + name)
        grads[name], deltas[name], new_ms[name], new_vs[name] = g2[None], d[None], nm[None], nv[None]

    loss, small_out = _adamw_small(small_all, list(small_params.values()))
    for name, (g2, d, nm, nv) in zip(small_params, small_out):
        shape = w_pool.shape if name == "w_pool" else g2.shape
        grads[name], deltas[name], new_ms[name], new_vs[name] = (a.reshape(shape) for a in (g2, d, nm, nv))

    order = ["w_in", "w_out", "g_attn_norm", "g_q", "g_k", "attn_sinks", "rel_bias", "w_pool", "pool_scale", "g_ffn_norm",
             "w_gate", "w_up", "w_down", "g_ple_norm", "w_ple_gate", "w_ple_proj"]
    return (loss.reshape(()), dx[None], *[grads[n] for n in order], *[deltas[n] for n in order],
            *[new_ms[n] for n in order], *[new_vs[n] for n in order])
```

```python
import functools

import numpy as np
import jax
import jax.numpy as jnp
from jax import lax
from jax.experimental import pallas as pl
from jax.experimental.pallas import tpu as pltpu
from jax.experimental.pallas import tpu_sc as plsc

F32 = jnp.float32
BF16 = jnp.bfloat16
MESH = pl.DeviceIdType.MESH

D_MODEL = 1024
HEAD_DIM = 64
N_Q_HEADS = 8
ATTN_WIDTH = 512
KV_WIDTH = 128
POOL_WIDTH = 512
IN_WIDTH = 1280
D_FF = 2816
PLE_DIM = 256
FF_CHUNK = 1408
BLOCK = 128
N_BUCKETS = 32
MAX_DISTANCE = 128
POOL_SIZES = (2, 4, 8, 16)
EPS = 1e-6
NEG = -1e30
N_CHIPS = 4
N_DEV = 8

ADAM_LR = 0.001
ADAM_B1 = 0.9
ADAM_B2 = 0.999
ADAM_EPS = 1e-08
ADAM_WD = 0.01
ADAM_STEP = 10

SLAB = {"inT": (0, 320), "out": (320, 256), "gateT": (576, 704), "upT": (1280, 704), "down": (1984, 704),
        "plg": (2688, 256), "plp": (2944, 64)}
SLAB_ROWS = 3008
HALF_ROWS = SLAB_ROWS // 2
EARLY_ROWS = 576
POOL_HALO = 24

SMALL = {"g_attn": 0, "g_ffn": 8, "g_ple": 16, "pool_scale": 24, "g_q": 28, "g_k": 29, "sinks": 30, "loss": 31,
         "rel_bias": 32, "w_pool": 64}
SMALL_ROWS = 576

VMEM_LIMIT_BIG = 60 * 1024 * 1024
VMEM_LIMIT = 48 * 1024 * 1024


def _params(vmem=VMEM_LIMIT, n_axes=1):
    return pltpu.CompilerParams(dimension_semantics=("arbitrary",) * n_axes, vmem_limit_bytes=vmem)


def _dot(a, b, ca, cb):
    return lax.dot_general(a, b, (((ca,), (cb,)), ((), ())), preferred_element_type=F32)


def _full(shape):
    return pl.BlockSpec(shape, lambda i: (0,) * len(shape))


ANY = pl.BlockSpec(memory_space=pl.ANY)
VMEM_WHOLE = pl.BlockSpec(memory_space=pltpu.VMEM)


W_SPECS = [ANY, ANY, pl.BlockSpec(memory_space=pltpu.SMEM)]


def _load_rows(w_refs, name, dst_ref, sems):
    slab_ref, local_ref, me_ref = w_refs
    off, rows = SLAB[name]
    slab_off = off if off < EARLY_ROWS else off - EARLY_ROWS
    me = me_ref[0]
    for phase in ("start", "wait"):
        for j in range(N_CHIPS):
            dst = dst_ref.at[pl.ds(j * rows, rows), :]
            theirs = pltpu.make_async_copy(slab_ref.at[j, pl.ds(slab_off, rows), :], dst, sems.at[j])
            own = pltpu.make_async_copy(local_ref.at[pl.ds(off, rows), :], dst, sems.at[j])

            @pl.when(me == j)
            def _():
                getattr(own, phase)()

            @pl.when(me != j)
            def _():
                getattr(theirs, phase)()


def _rms_fwd(x, g):
    r = lax.rsqrt(jnp.mean(x * x, axis=-1, keepdims=True) + EPS)
    return x * r * g


def _rms_bwd(x, g, dy):
    r = lax.rsqrt(jnp.mean(x * x, axis=-1, keepdims=True) + EPS)
    xn = x * r
    dyg = dy * g
    dx = r * (dyg - xn * jnp.mean(dyg * xn, axis=-1, keepdims=True))
    return dx, jnp.sum(dy * xn, axis=0, keepdims=True)


def _half_sum(v, lo):
    s_lo = jnp.sum(jnp.where(lo, v, 0.0), axis=-1, keepdims=True)
    s_hi = jnp.sum(jnp.where(lo, 0.0, v), axis=-1, keepdims=True)
    return jnp.where(lo, s_lo, s_hi)


def _pair_norm(zp, g, lo):
    r = lax.rsqrt(_half_sum(zp * zp, lo) * (1.0 / HEAD_DIM) + EPS)
    return zp * r * g


def _pair_norm_bwd(zp, g, dy, lo):
    r = lax.rsqrt(_half_sum(zp * zp, lo) * (1.0 / HEAD_DIM) + EPS)
    xn = zp * r
    dyg = dy * g
    dx = r * (dyg - xn * (_half_sum(dyg * xn, lo) * (1.0 / HEAD_DIM)))
    return dx, jnp.sum(dy * xn, axis=0, keepdims=True)


def _to_stacked(pair, group, lo):
    rolled = pltpu.roll(pair, 64, axis=1)
    if group == 0:
        return jnp.where(lo, pair, 0.0), jnp.where(lo, rolled, 0.0)
    return jnp.where(lo, 0.0, rolled), jnp.where(lo, 0.0, pair)


def _from_stacked(even, odd, group, lo):
    if group == 0:
        return jnp.where(lo, even, pltpu.roll(odd, 64, axis=1))
    return jnp.where(lo, pltpu.roll(even, 64, axis=1), odd)


def _sigmoid(v):
    return 1.0 / (1.0 + jnp.exp(-v))


def _pool_counts(tile, n_rows):
    t1 = tile * n_rows + lax.broadcasted_iota(jnp.int32, (n_rows, POOL_WIDTH), 0) + 1
    lane = lax.broadcasted_iota(jnp.int32, (n_rows, POOL_WIDTH), 1)
    win = jnp.where(lane < 128, 2, jnp.where(lane < 256, 4, jnp.where(lane < 384, 8, 16)))
    return jnp.minimum(t1, win).astype(F32)


def _attn_in(x2, g_attn, gq, gk, wts):
    s_len = x2.shape[0]
    t = 512

    def body(x_ref, g_ref, gq_ref, gk_ref, sl_ref, lo_ref, me_ref, hn_ref, zqk_ref, u_ref, kn_ref, v_ref, qst_ref, w_ref, sems):
        @pl.when(pl.program_id(0) == 0)
        def _():
            _load_rows((sl_ref, lo_ref, me_ref), "inT", w_ref, sems)

        hn = _rms_fwd(x_ref[...], g_ref[...]).astype(BF16)
        hn_ref[...] = hn
        z = _dot(hn, w_ref[...], 1, 1)
        zqk_ref[...] = z[:, :640]
        u_ref[...] = z[:, 768:]
        v_ref[...] = z[:, 640:768].astype(BF16)
        lo = lax.broadcasted_iota(jnp.int32, (t, 128), 1) < 64
        kn_ref[...] = _pair_norm(z[:, 512:640], gk_ref[...], lo).astype(BF16)
        for p in range(4):
            qn = _pair_norm(z[:, 128 * p:128 * p + 128], gq_ref[...], lo)
            even, odd = _to_stacked(qn, p // 2, lo)
            qst_ref[2 * p] = even.astype(BF16)
            qst_ref[2 * p + 1] = odd.astype(BF16)

    row = lambda w: pl.BlockSpec((t, w), lambda i: (i, 0))
    return pl.pallas_call(
        body, name="attn_in", grid=(s_len // t,),
        in_specs=[row(D_MODEL), _full((1, D_MODEL)), _full((1, 128)), _full((1, 128))] + W_SPECS,
        out_specs=[row(D_MODEL), row(640), row(POOL_WIDTH), row(128), row(128),
                   pl.BlockSpec((N_Q_HEADS, t, 128), lambda i: (0, i, 0))],
        out_shape=[jax.ShapeDtypeStruct((s_len, D_MODEL), BF16), jax.ShapeDtypeStruct((s_len, 640), F32),
                   jax.ShapeDtypeStruct((s_len, POOL_WIDTH), F32), jax.ShapeDtypeStruct((s_len, 128), BF16),
                   jax.ShapeDtypeStruct((s_len, 128), BF16), jax.ShapeDtypeStruct((N_Q_HEADS, s_len, 128), BF16)],
        scratch_shapes=[pltpu.VMEM((IN_WIDTH, D_MODEL), BF16), pltpu.SemaphoreType.DMA((N_CHIPS,))],
        compiler_params=_params(),
    )(x2, g_attn, gq, gk, *wts)


def _bucket_table():
    i_idx = np.arange(BLOCK)[:, None]
    j_idx = np.arange(2 * BLOCK)[None, :]
    d = BLOCK + i_idx - j_idx
    n = np.maximum(d, 0)
    max_exact = N_BUCKETS // 2
    nf = np.maximum(n, 1).astype(np.float64)
    large = max_exact + (np.log(nf / max_exact) / np.log(MAX_DISTANCE / max_exact) * (N_BUCKETS - max_exact)).astype(np.int64)
    large = np.minimum(large, N_BUCKETS - 1)
    bucket = np.where(n < max_exact, n, large)
    return np.where((d >= 0) & (d < BLOCK), bucket, -1).astype(np.int32)


def _bias_build(rel_bias_t, bucket):
    def body(rb_ref, bucket_ref, out_ref):
        bk = bucket_ref[...]
        for h in range(N_Q_HEADS):
            acc = jnp.full((BLOCK, 2 * BLOCK), NEG, F32)
            for b in range(N_BUCKETS):
                acc = jnp.where(bk == b, rb_ref[h, b], acc)
            out_ref[pl.ds(h * BLOCK, BLOCK), :] = acc

    return pl.pallas_call(
        body, name="bias_build",
        in_specs=[pl.BlockSpec(memory_space=pltpu.SMEM), pl.BlockSpec(memory_space=pltpu.VMEM)],
        out_specs=pl.BlockSpec(memory_space=pltpu.VMEM),
        out_shape=jax.ShapeDtypeStruct((N_Q_HEADS * BLOCK, 2 * BLOCK), F32),
    )(rel_bias_t, bucket)


def _band_softmax(q_ref, kp_ref, kc_ref, bias_ref, sink_ref, block):
    q = q_ref[...].reshape(N_Q_HEADS * BLOCK, 128)
    k2 = jnp.concatenate([kp_ref[...], kc_ref[...]], axis=0)
    s = _dot(q, k2, 1, 1) * (HEAD_DIM ** -0.5) + bias_ref[...]
    col = lax.broadcasted_iota(jnp.int32, s.shape, 1)
    s = jnp.where(col < jnp.where(block == 0, BLOCK, 0), NEG, s)
    sink = sink_ref[...]
    m = jnp.maximum(jnp.max(s, axis=-1, keepdims=True), sink)
    p = jnp.exp(s - m)
    e_sink = jnp.exp(sink - m)
    inv = 1.0 / (jnp.sum(p, axis=-1, keepdims=True) + e_sink)
    return q, k2, p * inv, e_sink * inv


def _attn_specs():
    prev = lambda i: (jnp.maximum(i - 1, 0), 0)
    cur = lambda i: (i, 0)
    stacked = pl.BlockSpec((N_Q_HEADS, BLOCK, 128), lambda i: (0, i, 0))
    kv = [pl.BlockSpec((BLOCK, 128), prev), pl.BlockSpec((BLOCK, 128), cur)]
    consts = [_full((N_Q_HEADS * BLOCK, 2 * BLOCK)), _full((N_Q_HEADS * BLOCK, 1))]
    return stacked, kv, consts


def _head_lane_mask():
    rows = lax.broadcasted_iota(jnp.int32, (N_Q_HEADS * BLOCK, 128), 0)
    lanes = lax.broadcasted_iota(jnp.int32, (N_Q_HEADS * BLOCK, 128), 1)
    return (rows < 4 * BLOCK) == (lanes < 64)


def _attn_fwd(qst, kn, vb, bias_st, sink_st):
    s_len = kn.shape[0]

    def body(q_ref, kp_ref, kc_ref, vp_ref, vc_ref, bias_ref, sink_ref, o_ref):
        _, _, probs, _ = _band_softmax(q_ref, kp_ref, kc_ref, bias_ref, sink_ref, pl.program_id(0))
        v2 = jnp.concatenate([vp_ref[...], vc_ref[...]], axis=0)
        o = _dot(probs.astype(BF16), v2, 1, 0)
        o_ref[...] = jnp.where(_head_lane_mask(), o, 0.0).astype(BF16).reshape(N_Q_HEADS, BLOCK, 128)

    stacked, kv, consts = _attn_specs()
    return pl.pallas_call(
        body, name="attn_fwd", grid=(s_len // BLOCK,),
        in_specs=[stacked] + kv + kv + consts, out_specs=stacked,
        out_shape=jax.ShapeDtypeStruct((N_Q_HEADS, s_len, 128), BF16),
        compiler_params=_params(),
    )(qst, kn, kn, vb, vb, bias_st, sink_st)


def _mix_out(u, ost, x2, wts, wpool, pool_scale, g_ffn):
    s_len = x2.shape[0]
    t = 512
    n = t + 16

    def body(u_ref, o_ref, x_ref, sl_ref, lo_ref, me_ref, wp_ref, sc_ref, g_ref, pooled_ref, mix_ref, h1_ref, hn_ref,
             w_ref, ext_ref, st_ref, sems):
        i = pl.program_id(0)

        @pl.when(i == 0)
        def _():
            _load_rows((sl_ref, lo_ref, me_ref), "out", w_ref, sems)
            ext_ref[...] = jnp.zeros_like(ext_ref)
            st_ref[...] = jnp.zeros_like(st_ref)

        u_tile = u_ref[...]
        ext_ref[pl.ds(POOL_HALO, t), :] = u_tile
        st_ref[pl.ds(8, n), :] = ext_ref[pl.ds(8, n), :] + ext_ref[pl.ds(7, n), :]
        st_ref[pl.ds(8, n), 128:] = st_ref[pl.ds(8, n), 128:] + st_ref[pl.ds(6, n), 128:]
        st_ref[pl.ds(8, n), 256:] = st_ref[pl.ds(8, n), 256:] + st_ref[pl.ds(4, n), 256:]
        st_ref[pl.ds(8, n), 384:] = st_ref[pl.ds(8, n), 384:] + st_ref[pl.ds(0, n), 384:]
        ext_ref[pl.ds(0, POOL_HALO), :] = ext_ref[pl.ds(t, POOL_HALO), :]
        pooled = (st_ref[pl.ds(POOL_HALO, t), :] / _pool_counts(i, t) - u_tile).astype(BF16)
        pooled_ref[...] = pooled
        for g in range(4):
            cols = slice(128 * g, 128 * g + 128)
            y = _dot(pooled[:, cols], wp_ref[g], 1, 0) * sc_ref[:, cols]
            mix_ref[:, ATTN_WIDTH + 128 * g:ATTN_WIDTH + 128 * g + 128] = y.astype(BF16)
        lo = lax.broadcasted_iota(jnp.int32, (t, 128), 1) < 64
        for p in range(4):
            a = _from_stacked(o_ref[2 * p].astype(F32), o_ref[2 * p + 1].astype(F32), p // 2, lo)
            mix_ref[:, 128 * p:128 * p + 128] = a.astype(BF16)
        h1 = x_ref[...] + _dot(mix_ref[...], w_ref[...], 1, 0)
        h1_ref[...] = h1
        hn_ref[...] = _rms_fwd(h1, g_ref[...]).astype(BF16)

    row = lambda w: pl.BlockSpec((t, w), lambda i: (i, 0))
    return pl.pallas_call(
        body, name="mix_out", grid=(s_len // t,),
        in_specs=[row(POOL_WIDTH), pl.BlockSpec((N_Q_HEADS, t, 128), lambda i: (0, i, 0)), row(D_MODEL)] + W_SPECS
        + [_full((4, 128, 128)), _full((1, POOL_WIDTH)), _full((1, D_MODEL))],
        out_specs=[row(POOL_WIDTH), row(D_MODEL), row(D_MODEL), row(D_MODEL)],
        out_shape=[jax.ShapeDtypeStruct((s_len, POOL_WIDTH), BF16), jax.ShapeDtypeStruct((s_len, D_MODEL), BF16),
                   jax.ShapeDtypeStruct((s_len, D_MODEL), F32), jax.ShapeDtypeStruct((s_len, D_MODEL), BF16)],
        scratch_shapes=[pltpu.VMEM((D_MODEL, D_MODEL), BF16), pltpu.VMEM((t + POOL_HALO, POOL_WIDTH), F32),
                        pltpu.VMEM((t + POOL_HALO, POOL_WIDTH), F32), pltpu.SemaphoreType.DMA((N_CHIPS,))],
        compiler_params=_params(),
    )(u, ost, x2, *wts, wpool, pool_scale, g_ffn)


def _ffn_fwd(hn2, h1, wts):
    s_len = h1.shape[0]
    t = 256

    def body(hn_ref, h1_ref, sl_ref, lo_ref, me_ref, gate_ref, up_ref, h2_ref, wg_ref, wu_ref, wd_ref, sems):
        @pl.when(pl.program_id(0) == 0)
        def _():
            w_refs = (sl_ref, lo_ref, me_ref)
            _load_rows(w_refs, "gateT", wg_ref, sems)
            _load_rows(w_refs, "upT", wu_ref, sems)
            _load_rows(w_refs, "down", wd_ref, sems)

        hn = hn_ref[...]
        h2 = h1_ref[...]
        for ch in range(D_FF // FF_CHUNK):
            rows = pl.ds(ch * FF_CHUNK, FF_CHUNK)
            cols = slice(ch * FF_CHUNK, (ch + 1) * FF_CHUNK)
            gate = _dot(hn, wg_ref[rows, :], 1, 1)
            up = _dot(hn, wu_ref[rows, :], 1, 1)
            gate_ref[:, cols] = gate
            up_ref[:, cols] = up
            act = (gate * _sigmoid(gate) * up).astype(BF16)
            h2 = h2 + _dot(act, wd_ref[rows, :], 1, 0)
        h2_ref[...] = h2

    row = lambda w: pl.BlockSpec((t, w), lambda i: (i, 0))
    return pl.pallas_call(
        body, name="ffn_fwd", grid=(s_len // t,),
        in_specs=[row(D_MODEL), row(D_MODEL)] + W_SPECS,
        out_specs=[row(D_FF), row(D_FF), row(D_MODEL)],
        out_shape=[jax.ShapeDtypeStruct((s_len, D_FF), F32), jax.ShapeDtypeStruct((s_len, D_FF), F32),
                   jax.ShapeDtypeStruct((s_len, D_MODEL), F32)],
        scratch_shapes=[pltpu.VMEM((D_FF, D_MODEL), BF16)] * 3 + [pltpu.SemaphoreType.DMA((N_CHIPS,))],
        compiler_params=_params(VMEM_LIMIT_BIG),
    )(hn2, h1, *wts)


def _ple_loss(h2, p2, tgt, wts, g_ple):
    s_len = h2.shape[0]
    t = 512
    n_tiles = s_len // t

    def body(h2_ref, p_ref, tgt_ref, sl_ref, lo_ref, me_ref, g_ref, loss_ref, dh2_ref, dgl_ref, dpp_ref, hn_ref,
             dg_ref, w_ref, wp_ref, packed_ref, loss_acc, sems):
        i = pl.program_id(0)

        @pl.when(i == 0)
        def _():
            w_refs = (sl_ref, lo_ref, me_ref)
            _load_rows(w_refs, "plg", w_ref, sems)
            _load_rows(w_refs, "plp", packed_ref, sems)
            for j in range(N_CHIPS):
                for q in range(4):
                    wp_ref[pl.ds(64 * q, 64), 256 * j:256 * j + 256] = packed_ref[pl.ds(64 * j, 64), 256 * q:256 * q + 256]
            loss_acc[...] = jnp.zeros_like(loss_acc)
            dg_ref[...] = jnp.zeros_like(dg_ref)

        h2v = h2_ref[...]
        g = g_ref[...]
        hn = _rms_fwd(h2v, g).astype(BF16)
        hn_ref[...] = hn
        gate = _sigmoid(_dot(hn, w_ref[...], 1, 0))
        pp = _dot(p_ref[...].astype(BF16), wp_ref[...], 1, 0)
        err = h2v + gate * pp - tgt_ref[...]
        loss_acc[...] += jnp.sum(err * err, axis=0, keepdims=True)
        dy = err * (1.0 / D_MODEL)
        dpp_ref[...] = (dy * gate).astype(BF16)
        dgl = (dy * pp * gate * (1.0 - gate)).astype(BF16)
        dgl_ref[...] = dgl
        dx, dg = _rms_bwd(h2v, g, _dot(dgl, w_ref[...], 1, 1))
        dh2_ref[...] = dy + dx
        dg_ref[...] += dg

        @pl.when(i == n_tiles - 1)
        def _():
            total = jnp.sum(loss_acc[...], axis=-1, keepdims=True) * (0.5 / D_MODEL)
            loss_ref[...] = jnp.broadcast_to(total, loss_ref.shape)

    row = lambda w: pl.BlockSpec((t, w), lambda i: (i, 0))
    return pl.pallas_call(
        body, name="ple_loss", grid=(n_tiles,),
        in_specs=[row(D_MODEL), row(PLE_DIM), row(D_MODEL)] + W_SPECS + [_full((1, D_MODEL))],
        out_specs=[_full((1, 128)), row(D_MODEL), row(D_MODEL), row(D_MODEL), row(D_MODEL), _full((1, D_MODEL))],
        out_shape=[jax.ShapeDtypeStruct((1, 128), F32), jax.ShapeDtypeStruct((s_len, D_MODEL), F32),
                   jax.ShapeDtypeStruct((s_len, D_MODEL), BF16), jax.ShapeDtypeStruct((s_len, D_MODEL), BF16),
                   jax.ShapeDtypeStruct((s_len, D_MODEL), BF16), jax.ShapeDtypeStruct((1, D_MODEL), F32)],
        scratch_shapes=[pltpu.VMEM((D_MODEL, D_MODEL), BF16), pltpu.VMEM((PLE_DIM, D_MODEL), BF16),
                        pltpu.VMEM((PLE_DIM, D_MODEL), BF16), pltpu.VMEM((1, D_MODEL), F32),
                        pltpu.SemaphoreType.DMA((N_CHIPS,))],
        compiler_params=_params(),
    )(h2, p2, tgt, *wts, g_ple)


def _ffn_bwd(dh2, gate, up, h1, wts, g_ffn):
    s_len = h1.shape[0]
    t = 256

    def body(dh2_ref, gate_ref, up_ref, h1_ref, sl_ref, lo_ref, me_ref, g_ref, dgate_ref, dup_ref, act_ref, dh1_ref, dg_ref,
             wg_ref, wu_ref, wd_ref, sems):
        @pl.when(pl.program_id(0) == 0)
        def _():
            w_refs = (sl_ref, lo_ref, me_ref)
            _load_rows(w_refs, "gateT", wg_ref, sems)
            _load_rows(w_refs, "upT", wu_ref, sems)
            _load_rows(w_refs, "down", wd_ref, sems)
            dg_ref[...] = jnp.zeros_like(dg_ref)

        dh2v = dh2_ref[...]
        dh2b = dh2v.astype(BF16)
        dhn = jnp.zeros((t, D_MODEL), F32)
        for ch in range(D_FF // FF_CHUNK):
            rows = pl.ds(ch * FF_CHUNK, FF_CHUNK)
            cols = slice(ch * FF_CHUNK, (ch + 1) * FF_CHUNK)
            dact = _dot(dh2b, wd_ref[rows, :], 1, 1)
            gate_v = gate_ref[:, cols]
            up_v = up_ref[:, cols]
            sg = _sigmoid(gate_v)
            silu = gate_v * sg
            act_ref[:, cols] = (silu * up_v).astype(BF16)
            dup = (dact * silu).astype(BF16)
            dgate = (dact * up_v * (sg * (1.0 + gate_v * (1.0 - sg)))).astype(BF16)
            dup_ref[:, cols] = dup
            dgate_ref[:, cols] = dgate
            dhn = dhn + _dot(dgate, wg_ref[rows, :], 1, 0) + _dot(dup, wu_ref[rows, :], 1, 0)
        dx, dg = _rms_bwd(h1_ref[...], g_ref[...], dhn)
        dh1_ref[...] = dh2v + dx
        dg_ref[...] += dg

    row = lambda w: pl.BlockSpec((t, w), lambda i: (i, 0))
    return pl.pallas_call(
        body, name="ffn_bwd", grid=(s_len // t,),
        in_specs=[row(D_MODEL), row(D_FF), row(D_FF), row(D_MODEL)] + W_SPECS + [_full((1, D_MODEL))],
        out_specs=[row(D_FF), row(D_FF), row(D_FF), row(D_MODEL), _full((1, D_MODEL))],
        out_shape=[jax.ShapeDtypeStruct((s_len, D_FF), BF16), jax.ShapeDtypeStruct((s_len, D_FF), BF16),
                   jax.ShapeDtypeStruct((s_len, D_FF), BF16), jax.ShapeDtypeStruct((s_len, D_MODEL), F32),
                   jax.ShapeDtypeStruct((1, D_MODEL), F32)],
        scratch_shapes=[pltpu.VMEM((D_FF, D_MODEL), BF16)] * 3 + [pltpu.SemaphoreType.DMA((N_CHIPS,))],
        compiler_params=_params(VMEM_LIMIT_BIG),
    )(dh2, gate, up, h1, *wts, g_ffn)


def _mix_out_bwd(dh1, wts, pooled, wpool, pool_scale, after):
    s_len = dh1.shape[0]
    t = 512
    n = t + 16
    n_tiles = s_len // t

    def body(dh1_ref, sl_ref, lo_ref, me_ref, pooled_ref, wp_ref, sc_ref, after_ref, dost_ref, du_ref, dyp_ref, dsc_ref,
             w_ref, ext_ref, st_ref, sems):
        del after_ref
        i = pl.program_id(0)

        @pl.when(i == 0)
        def _():
            _load_rows((sl_ref, lo_ref, me_ref), "out", w_ref, sems)
            ext_ref[...] = jnp.zeros_like(ext_ref)
            st_ref[...] = jnp.zeros_like(st_ref)
            dsc_ref[...] = jnp.zeros_like(dsc_ref)

        dmix = _dot(dh1_ref[...].astype(BF16), w_ref[...], 1, 1)
        lo = lax.broadcasted_iota(jnp.int32, (t, 128), 1) < 64
        for p in range(4):
            even, odd = _to_stacked(dmix[:, 128 * p:128 * p + 128], p // 2, lo)
            dost_ref[2 * p] = even.astype(BF16)
            dost_ref[2 * p + 1] = odd.astype(BF16)
        pooled_v = pooled_ref[...]
        counts = _pool_counts(n_tiles - 1 - i, t)
        for g in range(4):
            cols = slice(128 * g, 128 * g + 128)
            dm = dmix[:, ATTN_WIDTH + 128 * g:ATTN_WIDTH + 128 * g + 128]
            ypre = _dot(pooled_v[:, cols], wp_ref[g], 1, 0)
            dsc_ref[:, cols] += jnp.sum(ypre * dm, axis=0, keepdims=True)
            dyp = (dm * sc_ref[:, cols]).astype(BF16)
            dyp_ref[:, cols] = dyp
            dpooled = _dot(dyp, wp_ref[g], 1, 1)
            du_ref[:, cols] = -dpooled
            ext_ref[pl.ds(0, t), cols] = dpooled / counts[:, cols]
        st_ref[pl.ds(0, n), :] = ext_ref[pl.ds(0, n), :] + ext_ref[pl.ds(1, n), :]
        st_ref[pl.ds(0, n), 128:] = st_ref[pl.ds(0, n), 128:] + st_ref[pl.ds(2, n), 128:]
        st_ref[pl.ds(0, n), 256:] = st_ref[pl.ds(0, n), 256:] + st_ref[pl.ds(4, n), 256:]
        st_ref[pl.ds(0, n), 384:] = st_ref[pl.ds(0, n), 384:] + st_ref[pl.ds(8, n), 384:]
        ext_ref[pl.ds(t, POOL_HALO), :] = ext_ref[pl.ds(0, POOL_HALO), :]
        du_ref[...] += st_ref[pl.ds(0, t), :]

    rev = lambda w: pl.BlockSpec((t, w), lambda i: (n_tiles - 1 - i, 0))
    return pl.pallas_call(
        body, name="mix_out_bwd", grid=(n_tiles,),
        in_specs=[rev(D_MODEL)] + W_SPECS + [rev(POOL_WIDTH), _full((4, 128, 128)), _full((1, POOL_WIDTH)), ANY],
        out_specs=[pl.BlockSpec((N_Q_HEADS, t, 128), lambda i: (0, n_tiles - 1 - i, 0)), rev(POOL_WIDTH), rev(POOL_WIDTH),
                   _full((1, POOL_WIDTH))],
        out_shape=[jax.ShapeDtypeStruct((N_Q_HEADS, s_len, 128), BF16), jax.ShapeDtypeStruct((s_len, POOL_WIDTH), F32),
                   jax.ShapeDtypeStruct((s_len, POOL_WIDTH), BF16), jax.ShapeDtypeStruct((1, POOL_WIDTH), F32)],
        scratch_shapes=[pltpu.VMEM((D_MODEL, D_MODEL), BF16), pltpu.VMEM((t + POOL_HALO, POOL_WIDTH), F32),
                        pltpu.VMEM((t + POOL_HALO, POOL_WIDTH), F32), pltpu.SemaphoreType.DMA((N_CHIPS,))],
        compiler_params=_params(),
    )(dh1, *wts, pooled, wpool, pool_scale, after)


def _attn_bwd(qst, kn, vb, dost, bias_st, sink_st, after):
    s_len = kn.shape[0]

    def body(q_ref, kp_ref, kc_ref, vp_ref, vc_ref, do_ref, bias_ref, sink_ref, after_ref, dq_ref, dk_ref, dv_ref, dbias_ref,
             dsink_ref):
        del after_ref
        i = pl.program_id(0)

        @pl.when(i == 0)
        def _():
            dk_ref[...] = jnp.zeros_like(dk_ref)
            dv_ref[...] = jnp.zeros_like(dv_ref)
            dbias_ref[...] = jnp.zeros_like(dbias_ref)
            dsink_ref[...] = jnp.zeros_like(dsink_ref)

        q, k2, probs, p_sink = _band_softmax(q_ref, kp_ref, kc_ref, bias_ref, sink_ref, i)
        v2 = jnp.concatenate([vp_ref[...], vc_ref[...]], axis=0)
        do = do_ref[...].reshape(N_Q_HEADS * BLOCK, 128)
        dp = _dot(do, v2, 1, 1)
        dsum = jnp.sum(probs * dp, axis=-1, keepdims=True)
        dlog = probs * (dp - dsum)
        dsink_ref[...] -= p_sink * dsum
        dbias_ref[...] += dlog
        dlog_s = (dlog * (HEAD_DIM ** -0.5)).astype(BF16)
        dq = _dot(dlog_s, k2, 1, 0)
        dq_ref[...] = jnp.where(_head_lane_mask(), dq, 0.0).reshape(N_Q_HEADS, BLOCK, 128)
        dk2 = _dot(dlog_s, q, 0, 0)
        dv2 = _dot(probs.astype(BF16), do, 0, 0)
        prev_rows = pl.ds(pl.multiple_of(jnp.maximum(i - 1, 0) * BLOCK, BLOCK), BLOCK)
        cur_rows = pl.ds(pl.multiple_of(i * BLOCK, BLOCK), BLOCK)
        dk_ref[prev_rows, :] += dk2[:BLOCK]
        dk_ref[cur_rows, :] += dk2[BLOCK:]
        dv_ref[prev_rows, :] += dv2[:BLOCK]
        dv_ref[cur_rows, :] += dv2[BLOCK:]

    stacked, kv, consts = _attn_specs()
    return pl.pallas_call(
        body, name="attn_bwd", grid=(s_len // BLOCK,),
        in_specs=[stacked] + kv + kv + [stacked] + consts + [ANY],
        out_specs=[stacked, _full((s_len, 128)), _full((s_len, 128)), _full((N_Q_HEADS * BLOCK, 2 * BLOCK)),
                   _full((N_Q_HEADS * BLOCK, 1))],
        out_shape=[jax.ShapeDtypeStruct((N_Q_HEADS, s_len, 128), F32), jax.ShapeDtypeStruct((s_len, 128), F32),
                   jax.ShapeDtypeStruct((s_len, 128), F32), jax.ShapeDtypeStruct((N_Q_HEADS * BLOCK, 2 * BLOCK), F32),
                   jax.ShapeDtypeStruct((N_Q_HEADS * BLOCK, 1), F32)],
        compiler_params=_params(),
    )(qst, kn, kn, vb, vb, dost, bias_st, sink_st, after)


def _small_pack(dg_attn, dg_ffn, dg_ple, dscale, dgq, dgk, dbias, dsink_rows, bucket, loss_v, dwpool):
    def body(ga_ref, gf_ref, gp_ref, sc_ref, gq_ref, gk_ref, db_ref, ds_ref, bucket_ref, loss_ref, wp_ref, out_ref):
        out_ref[pl.ds(0, SMALL["w_pool"]), :] = jnp.zeros((SMALL["w_pool"], 128), F32)
        for name, ref, n in (("g_attn", ga_ref, 8), ("g_ffn", gf_ref, 8), ("g_ple", gp_ref, 8), ("pool_scale", sc_ref, 4)):
            for k in range(n):
                out_ref[pl.ds(SMALL[name] + k, 1), :] = ref[:, 128 * k:128 * k + 128]
        for name, ref in (("g_q", gq_ref), ("g_k", gk_ref)):
            both = ref[...]
            out_ref[pl.ds(SMALL[name], 1), :] = both + pltpu.roll(both, 64, axis=1)
        out_ref[pl.ds(SMALL["loss"], 1), :] = loss_ref[...]
        bk = bucket_ref[...]
        rows = lax.broadcasted_iota(jnp.int32, (N_BUCKETS, 128), 0)
        lanes = lax.broadcasted_iota(jnp.int32, (N_BUCKETS, 128), 1)
        lane1 = lax.broadcasted_iota(jnp.int32, (1, 128), 1)
        rb = jnp.zeros((N_BUCKETS, 128), F32)
        sk = jnp.zeros((1, 128), F32)
        for h in range(N_Q_HEADS):
            band = db_ref[pl.ds(h * BLOCK, BLOCK), :]
            for b in range(N_BUCKETS):
                rb = jnp.where((rows == b) & (lanes == h), jnp.sum(jnp.where(bk == b, band, 0.0)), rb)
            sk = jnp.where(lane1 == h, jnp.sum(ds_ref[pl.ds(h * BLOCK, BLOCK), :]), sk)
        out_ref[pl.ds(SMALL["rel_bias"], N_BUCKETS), :] = rb
        out_ref[pl.ds(SMALL["sinks"], 1), :] = sk
        out_ref[pl.ds(SMALL["w_pool"], 512), :] = wp_ref[...].reshape(512, 128)

    return pl.pallas_call(
        body, name="small_pack", in_specs=[VMEM_WHOLE] * 11, out_specs=VMEM_WHOLE,
        out_shape=jax.ShapeDtypeStruct((SMALL_ROWS, 128), F32),
    )(dg_attn, dg_ffn, dg_ple, dscale, dgq, dgk, dbias, dsink_rows, bucket, loss_v, dwpool)


def _attn_in_bwd(dqst, zqk, dk, dv, du, x2, dh1, wts, g_attn, gq, gk):
    s_len = x2.shape[0]
    t = 512

    def body(dq_ref, zqk_ref, dk_ref, dv_ref, du_ref, x_ref, dh1_ref, sl_ref, lo_ref, me_ref, g_ref, gq_ref, gk_ref,
             dz_ref, dx_ref, dg_ref, dgq_ref, dgk_ref, w_ref, sems):
        @pl.when(pl.program_id(0) == 0)
        def _():
            _load_rows((sl_ref, lo_ref, me_ref), "inT", w_ref, sems)
            dg_ref[...] = jnp.zeros_like(dg_ref)
            dgq_ref[...] = jnp.zeros_like(dgq_ref)
            dgk_ref[...] = jnp.zeros_like(dgk_ref)

        lo = lax.broadcasted_iota(jnp.int32, (t, 128), 1) < 64
        for p in range(4):
            dqn = _from_stacked(dq_ref[2 * p], dq_ref[2 * p + 1], p // 2, lo)
            dq_raw, dgq = _pair_norm_bwd(zqk_ref[:, 128 * p:128 * p + 128], gq_ref[...], dqn, lo)
            dz_ref[:, 128 * p:128 * p + 128] = dq_raw.astype(BF16)
            dgq_ref[...] += dgq
        dk_raw, dgk = _pair_norm_bwd(zqk_ref[:, 512:640], gk_ref[...], dk_ref[...], lo)
        dgk_ref[...] += dgk
        dz_ref[:, 512:640] = dk_raw.astype(BF16)
        dz_ref[:, 640:768] = dv_ref[...].astype(BF16)
        dz_ref[:, 768:] = du_ref[...].astype(BF16)
        dx, dg = _rms_bwd(x_ref[...], g_ref[...], _dot(dz_ref[...], w_ref[...], 1, 0))
        dx_ref[...] = dh1_ref[...] + dx
        dg_ref[...] += dg

    row = lambda w: pl.BlockSpec((t, w), lambda i: (i, 0))
    return pl.pallas_call(
        body, name="attn_in_bwd", grid=(s_len // t,),
        in_specs=[pl.BlockSpec((N_Q_HEADS, t, 128), lambda i: (0, i, 0)), row(640), row(128), row(128), row(POOL_WIDTH),
                  row(D_MODEL), row(D_MODEL)] + W_SPECS + [_full((1, D_MODEL)), _full((1, 128)), _full((1, 128))],
        out_specs=[row(IN_WIDTH), row(D_MODEL), _full((1, D_MODEL)), _full((1, 128)), _full((1, 128))],
        out_shape=[jax.ShapeDtypeStruct((s_len, IN_WIDTH), BF16), jax.ShapeDtypeStruct((s_len, D_MODEL), F32),
                   jax.ShapeDtypeStruct((1, D_MODEL), F32), jax.ShapeDtypeStruct((1, 128), F32),
                   jax.ShapeDtypeStruct((1, 128), F32)],
        scratch_shapes=[pltpu.VMEM((IN_WIDTH, D_MODEL), BF16), pltpu.SemaphoreType.DMA((N_CHIPS,))],
        compiler_params=_params(),
    )(dqst, zqk, dk, dv, du, x2, dh1, *wts, g_attn, gq, gk)


def _dw(a, b, name, into=None):
    s_len, m = a.shape
    n_out = b.shape[1]
    tk = 512
    n_steps = s_len // tk
    tm = m // 2 if m > 1408 else m
    chunk = m // N_CHIPS
    per_tile = tm // chunk

    def accumulate(a_ref, b_ref, acc_ref, k):
        @pl.when(k == 0)
        def _():
            acc_ref[...] = jnp.zeros_like(acc_ref)

        acc_ref[...] += _dot(a_ref[...].astype(BF16), b_ref[...].astype(BF16), 0, 0)

    in_specs = [pl.BlockSpec((tk, tm), lambda i, k: (k, i)), pl.BlockSpec((tk, n_out), lambda i, k: (k, 0))]
    if into is None:
        def body(a_ref, b_ref, o_ref, acc_ref):
            k = pl.program_id(1)
            accumulate(a_ref, b_ref, acc_ref, k)

            @pl.when(k == n_steps - 1)
            def _():
                o_ref[...] = acc_ref[...].astype(BF16)

        return pl.pallas_call(
            body, name=name, grid=(m // tm, n_steps), in_specs=in_specs,
            out_specs=pl.BlockSpec((tm, n_out), lambda i, k: (i, 0)), out_shape=jax.ShapeDtypeStruct((m, n_out), BF16),
            scratch_shapes=[pltpu.VMEM((tm, n_out), F32)], compiler_params=_params(n_axes=2),
        )(a, b)

    slab, slab_rows, row_off = into
    assert n_out == D_MODEL

    def body_into(a_ref, b_ref, *rest):
        o_ref, acc_ref, stage_ref, sems = rest[-4:]
        i, k = pl.program_id(0), pl.program_id(1)
        accumulate(a_ref, b_ref, acc_ref, k)

        @pl.when(k == n_steps - 1)
        def _():
            stage_ref[...] = acc_ref[...].astype(BF16)
            copies = [pltpu.make_async_copy(stage_ref.at[pl.ds(jj * chunk, chunk), :],
                                            o_ref.at[i * per_tile + jj, pl.ds(row_off, chunk), :], sems.at[jj])
                      for jj in range(per_tile)]
            for cp in copies:
                cp.start()
            for cp in copies:
                cp.wait()

    operands, aliases = [a, b], {}
    if slab is not None:
        in_specs = in_specs + [ANY]
        operands.append(slab)
        aliases = {2: 0}
    return pl.pallas_call(
        body_into, name=name, grid=(m // tm, n_steps), in_specs=in_specs, out_specs=ANY,
        out_shape=jax.ShapeDtypeStruct((N_CHIPS, slab_rows, D_MODEL), BF16), input_output_aliases=aliases,
        scratch_shapes=[pltpu.VMEM((tm, n_out), F32), pltpu.VMEM((tm, n_out), BF16), pltpu.SemaphoreType.DMA((per_tile,))],
        compiler_params=_params(n_axes=2),
    )(*operands)


def _dw_pool(pooled, dyp):
    s_len = pooled.shape[0]
    tk = 512

    def body(a_ref, b_ref, o_ref):
        @pl.when(pl.program_id(0) == 0)
        def _():
            o_ref[...] = jnp.zeros_like(o_ref)

        for g in range(4):
            cols = slice(128 * g, 128 * g + 128)
            o_ref[g] += _dot(a_ref[:, cols], b_ref[:, cols], 0, 0)

    blk = pl.BlockSpec((tk, POOL_WIDTH), lambda k: (k, 0))
    return pl.pallas_call(
        body, name="dw_pool", grid=(s_len // tk,), in_specs=[blk, blk], out_specs=_full((4, 128, 128)),
        out_shape=jax.ShapeDtypeStruct((4, 128, 128), F32), compiler_params=_params(),
    )(pooled, dyp)


def _position():
    x, y, c = lax.axis_index("x"), lax.axis_index("y"), lax.axis_index("c")
    other_chips = [(1 - x, y), (x, 1 - y), (1 - x, 1 - y)]
    return x, y, c, other_chips


def _ag_weights(local_slab, row0, n_rows, name, collective_id):
    half = n_rows // 2

    def body(l_ref, g_ref, send, recv):
        x, y, c, chips = _position()
        me = 2 * x + y
        sibling = (x, y, 1 - c)
        peers = [sibling] + [(*chip, c) for chip in chips]
        barrier = pltpu.get_barrier_semaphore()
        for peer in peers:
            pl.semaphore_signal(barrier, inc=1, device_id=peer, device_id_type=MESH)
        pl.semaphore_wait(barrier, len(peers))
        mine = pl.ds(pl.multiple_of(c * half, 16), half)
        theirs = pl.ds(pl.multiple_of((1 - c) * half, 16), half)

        def copy(k, chip_idx, rows, to, src=None):
            dst = g_ref.at[chip_idx, rows, :]
            return pltpu.make_async_remote_copy(src_ref=dst if src is None else src, dst_ref=dst, send_sem=send.at[k],
                                                recv_sem=recv.at[k], device_id=to, device_id_type=MESH)

        own_rows = l_ref.at[pl.ds(pl.multiple_of(row0 + c * half, 16), half), :]
        first = [copy(k, me, mine, (*chip, c), src=own_rows) for k, chip in enumerate(chips)]
        for cp in first:
            cp.start()
        passed = []
        for k, chip in enumerate(chips):
            idx = 2 * chip[0] + chip[1]
            copy(k, idx, mine, (x, y, c)).wait_recv()
            fwd = copy(3 + k, idx, mine, sibling)
            fwd.start()
            passed.append(fwd)
        for k, chip in enumerate(chips):
            copy(3 + k, 2 * chip[0] + chip[1], theirs, (x, y, c)).wait_recv()
        for cp in first + passed:
            cp.wait_send()

    return pl.kernel(
        body, out_type=jax.ShapeDtypeStruct((N_CHIPS, n_rows, D_MODEL), BF16),
        mesh=plsc.ScalarSubcoreMesh(axis_name="sequencer", num_cores=1), name=name,
        scratch_types=[pltpu.SemaphoreType.DMA((6,)), pltpu.SemaphoreType.DMA((6,))],
        compiler_params=pltpu.CompilerParams(collective_id=collective_id),
    )(local_slab)


def _comm_call(body, peers_of, out_shape, n_sems, operand, name, collective_id):
    sems = [pltpu.SemaphoreType.DMA((n_sems,)), pltpu.SemaphoreType.DMA((n_sems,))]
    if collective_id is None:
        return pl.pallas_call(body, name=name, in_specs=[ANY], out_specs=ANY, out_shape=out_shape, scratch_shapes=sems)(operand)

    def with_handshake(in_ref, out_ref, send, recv):
        x, y, c, _ = _position()
        peers = peers_of(x, y, c)
        barrier = pltpu.get_barrier_semaphore()
        for peer in peers:
            pl.semaphore_signal(barrier, inc=1, device_id=peer, device_id_type=MESH)
        pl.semaphore_wait(barrier, len(peers))
        body(in_ref, out_ref, send, recv)

    return pl.kernel(with_handshake, out_type=out_shape, mesh=plsc.ScalarSubcoreMesh(axis_name="sequencer", num_cores=1),
                     name=name, scratch_types=sems, compiler_params=pltpu.CompilerParams(collective_id=collective_id))(operand)


def _rs_swap_halves(partial, name, collective_id=None):
    half = partial.shape[1] // 2

    def body(p_ref, r_ref, send, recv):
        x, y, c, _ = _position()
        theirs = pl.ds(pl.multiple_of((1 - c) * half, 16), half)
        cp = pltpu.make_async_remote_copy(src_ref=p_ref.at[:, theirs, :], dst_ref=r_ref, send_sem=send.at[0],
                                          recv_sem=recv.at[0], device_id=(x, y, 1 - c), device_id_type=MESH)
        cp.start()
        cp.wait()

    return _comm_call(body, lambda x, y, c: [(x, y, 1 - c)], jax.ShapeDtypeStruct((N_CHIPS, half, D_MODEL), BF16), 1,
                      partial, name, collective_id)


def _rs_add_halves(partial, other, core, name, after):
    half = other.shape[1]
    t = half // 2
    steps = half // t

    def body(core_ref, a_ref, b_ref, after_ref, o_ref):
        del after_ref
        o_ref[...] = (a_ref[...].astype(F32) + b_ref[...].astype(F32)).astype(BF16)

    return pl.pallas_call(
        body, name=name,
        grid_spec=pltpu.PrefetchScalarGridSpec(
            num_scalar_prefetch=1, grid=(N_CHIPS, steps),
            in_specs=[pl.BlockSpec((1, t, D_MODEL), lambda j, i, core_ref: (j, core_ref[0] * steps + i, 0)),
                      pl.BlockSpec((1, t, D_MODEL), lambda j, i, core_ref: (j, i, 0)), ANY],
            out_specs=pl.BlockSpec((1, t, D_MODEL), lambda j, i, core_ref: (j, i, 0))),
        out_shape=jax.ShapeDtypeStruct((N_CHIPS, half, D_MODEL), BF16),
        compiler_params=_params(n_axes=2),
    )(core, partial, other, after)


def _rs_exchange_chips(pre, name, collective_id=None):
    def body(s_ref, r_ref, send, recv):
        x, y, c, chips = _position()

        def copy(k, chunk, to):
            return pltpu.make_async_remote_copy(src_ref=s_ref.at[chunk], dst_ref=r_ref.at[k], send_sem=send.at[k],
                                                recv_sem=recv.at[k], device_id=to, device_id_type=MESH)

        sends = [copy(k, 2 * chip[0] + chip[1], (*chip, c)) for k, chip in enumerate(chips)]
        for cp in sends:
            cp.start()
        for cp in sends:
            cp.wait()

    return _comm_call(body, lambda x, y, c: [(1 - x, y, c), (x, 1 - y, c), (1 - x, 1 - y, c)],
                      jax.ShapeDtypeStruct((3, pre.shape[1], D_MODEL), BF16), 3, pre, name, collective_id)


def _rs_sum_chips(pre, received, place, name):
    half = pre.shape[1]
    t = half // 2 if half > 512 else half
    steps = half // t

    def body(place_ref, own_ref, r_ref, o_ref):
        acc = own_ref[0].astype(F32)
        for k in range(3):
            acc = acc + r_ref[k].astype(F32)
        o_ref[...] = acc

    return pl.pallas_call(
        body, name=name,
        grid_spec=pltpu.PrefetchScalarGridSpec(
            num_scalar_prefetch=1, grid=(steps,),
            in_specs=[pl.BlockSpec((1, t, D_MODEL), lambda i, place_ref: (place_ref[0], i, 0)),
                      pl.BlockSpec((3, t, D_MODEL), lambda i, place_ref: (0, i, 0))],
            out_specs=pl.BlockSpec((t, D_MODEL), lambda i, place_ref: (place_ref[1] * steps + i, 0))),
        out_shape=jax.ShapeDtypeStruct((2 * half, D_MODEL), F32),
        compiler_params=_params(),
    )(place, pre, received)


def _rs_finish(grads_a, grads_b, small):
    def body(fa_ref, fb_ref, s_ref, ga_ref, gb_ref, t_ref, send, recv, local_sem):
        del fa_ref, fb_ref
        x, y, c, chips = _position()
        sibling = (x, y, 1 - c)

        def slot(px, py, pc):
            return t_ref.at[4 * px + 2 * py + pc]

        def copy(k, block, to, src=None):
            return pltpu.make_async_remote_copy(src_ref=slot(*block) if src is None else src, dst_ref=slot(*block),
                                                send_sem=send.at[k], recv_sem=recv.at[k], device_id=to, device_id_type=MESH)

        def half_copies(core, to):
            out = []
            for k, g_ref in ((7, ga_ref), (8, gb_ref)):
                half = g_ref.shape[0] // 2
                rows = g_ref.at[pl.ds(pl.multiple_of(core * half, 8), half), :]
                out.append(pltpu.make_async_remote_copy(src_ref=rows, dst_ref=rows, send_sem=send.at[k], recv_sem=recv.at[k],
                                                        device_id=to, device_id_type=MESH))
            return out

        own_small = pltpu.make_async_copy(s_ref, slot(x, y, c), local_sem)
        own_small.start()
        to_sibling = half_copies(c, sibling)
        for cp in to_sibling:
            cp.start()
        first = [copy(0, (x, y, c), sibling, src=s_ref)]
        first += [copy(1 + k, (x, y, c), (*chip, c), src=s_ref) for k, chip in enumerate(chips)]
        for cp in first:
            cp.start()
        passed = []
        for k, chip in enumerate(chips):
            copy(1 + k, (*chip, c), (x, y, c)).wait_recv()
            fwd = copy(4 + k, (*chip, c), sibling)
            fwd.start()
            passed.append(fwd)
        copy(0, sibling, (x, y, c)).wait_recv()
        for k, chip in enumerate(chips):
            copy(4 + k, (*chip, 1 - c), (x, y, c)).wait_recv()
        for cp in half_copies(1 - c, (x, y, c)):
            cp.wait_recv()
        for cp in first + passed + to_sibling:
            cp.wait_send()
        own_small.wait()

    return pl.pallas_call(
        body, name="rs_finish", in_specs=[ANY, ANY, ANY], out_specs=[ANY, ANY, ANY], input_output_aliases={0: 0, 1: 1},
        out_shape=[jax.ShapeDtypeStruct(grads_a.shape, F32), jax.ShapeDtypeStruct(grads_b.shape, F32),
                   jax.ShapeDtypeStruct((N_DEV, SMALL_ROWS, 128), F32)],
        scratch_shapes=[pltpu.SemaphoreType.DMA((9,)), pltpu.SemaphoreType.DMA((9,)), pltpu.SemaphoreType.DMA],
    )(grads_a, grads_b, small)


def _adam_update(w, g, m, v):
    m_new = ADAM_B1 * m + (1.0 - ADAM_B1) * g
    v_new = ADAM_B2 * v + (1.0 - ADAM_B2) * (g * g)
    m_hat = m_new / (1.0 - ADAM_B1 ** ADAM_STEP)
    v_hat = v_new / (1.0 - ADAM_B2 ** ADAM_STEP)
    return -ADAM_LR * (m_hat / (jnp.sqrt(v_hat) + ADAM_EPS) + ADAM_WD * w), m_new, v_new


def _adamw(w, g, m, v, name):
    rows, cols = w.shape
    t = rows if rows <= 320 else (rows // 2 if rows % 256 else 256)

    def body(w_ref, g_ref, m_ref, v_ref, d_ref, nm_ref, nv_ref):
        d_ref[...], nm_ref[...], nv_ref[...] = _adam_update(w_ref[...], g_ref[...], m_ref[...], v_ref[...])

    blk = pl.BlockSpec((t, cols), lambda i: (i, 0))
    shape = jax.ShapeDtypeStruct((rows, cols), F32)
    return pl.pallas_call(
        body, name=name, grid=(rows // t,), in_specs=[blk] * 4, out_specs=[blk] * 3, out_shape=[shape] * 3,
        compiler_params=_params(),
    )(w, g, m, v)


SMALL_PARAMS = [("g_attn", (1, D_MODEL), 8), ("g_q", (1, HEAD_DIM), None), ("g_k", (1, HEAD_DIM), None),
                ("sinks", (1, N_Q_HEADS), None), ("rel_bias", (N_BUCKETS, N_Q_HEADS), None), ("w_pool", (512, 128), None),
                ("pool_scale", (1, POOL_WIDTH), 4), ("g_ffn", (1, D_MODEL), 8), ("g_ple", (1, D_MODEL), 8)]


def _adamw_small(tables, wmv):
    n_par = len(SMALL_PARAMS)

    def body(*refs):
        t_ref = refs[0]
        ins = refs[1:1 + 3 * n_par]
        loss_ref = refs[1 + 3 * n_par]
        outs = refs[2 + 3 * n_par:-1]
        tot_ref = refs[-1]
        total = t_ref[0]
        for d in range(1, N_DEV):
            total = total + t_ref[d]
        tot_ref[...] = total
        loss_ref[...] = tot_ref[pl.ds(SMALL["loss"], 1), 0:1]
        for i, (name, shape, split) in enumerate(SMALL_PARAMS):
            g_ref, d_ref, nm_ref, nv_ref = outs[4 * i:4 * i + 4]
            row = SMALL[name]
            if split:
                for k in range(split):
                    g_ref[:, 128 * k:128 * k + 128] = tot_ref[pl.ds(row + k, 1), :]
            else:
                g_ref[...] = tot_ref[pl.ds(row, shape[0]), 0:shape[1]]
            w_ref, m_ref, v_ref = ins[3 * i:3 * i + 3]
            d_ref[...], nm_ref[...], nv_ref[...] = _adam_update(w_ref[...], g_ref[...], m_ref[...], v_ref[...])

    shapes = [jax.ShapeDtypeStruct((1, 1), F32)]
    for _, shape, _ in SMALL_PARAMS:
        shapes += [jax.ShapeDtypeStruct(shape, F32)] * 4
    flat = [a for triple in wmv for a in triple]
    res = pl.pallas_call(
        body, name="adamw_small", in_specs=[VMEM_WHOLE] * (1 + 3 * n_par), out_specs=[VMEM_WHOLE] * len(shapes),
        out_shape=shapes, scratch_shapes=[pltpu.VMEM((SMALL_ROWS, 128), F32)],
    )(tables, *flat)
    return res[0], [res[1 + 4 * i:5 + 4 * i] for i in range(n_par)]


def _pack_ple_proj(shard):
    return shard.reshape(4, 64, 256).transpose(1, 0, 2).reshape(64, D_MODEL)


class _Reduction:
    def __init__(self, tag, place, ids=(None, None)):
        self.tag, self.place, self.ids = tag, place, ids

    def start(self, partial):
        self.partial = partial
        self.other = _rs_swap_halves(partial, "rs_swap_" + self.tag, self.ids[0])
        return partial

    def middle(self, after):
        self.pre = _rs_add_halves(self.partial, self.other, self.place[1:], "rs_add_" + self.tag, after)
        self.received = _rs_exchange_chips(self.pre, "rs_exchange_" + self.tag, self.ids[1])
        return self.pre

    def finish(self):
        return _rs_sum_chips(self.pre, self.received, self.place, "rs_sum_" + self.tag)


def _local_grads(x2, p2, tgt, wts, g_attn_norm, g_q, g_k, attn_sinks, rel_bias, w_pool, pool_scale, g_ffn_norm, g_ple_norm,
                 reduce_a):
    early, late, local_slab, me = wts
    w_early, w_late = (early, local_slab, me), (late, local_slab, me)
    bucket = jnp.asarray(_bucket_table())
    gq = jnp.tile(g_q, (1, 2))
    gk = jnp.tile(g_k, (1, 2))
    wpool = w_pool[0].astype(BF16)
    sink_st = jnp.repeat(attn_sinks[0], BLOCK)[:, None]
    bias_st = _bias_build(rel_bias.T, bucket)

    hn1, zqk, u, kn, vb, qst = _attn_in(x2, g_attn_norm, gq, gk, w_early)
    ost = _attn_fwd(qst, kn, vb, bias_st, sink_st)
    pooled, mix, h1, hn2 = _mix_out(u, ost, x2, w_early, wpool, pool_scale, g_ffn_norm)
    gate, up, h2 = _ffn_fwd(hn2, h1, w_late)
    loss_v, dh2, dgl, dpp, hn3, dg_ple = _ple_loss(h2, p2, tgt, w_late, g_ple_norm)

    dgate, dup, act, dh1, dg_ffn = _ffn_bwd(dh2, gate, up, h1, w_late, g_ffn_norm)
    rows_a = SLAB_ROWS - SLAB["inT"][1]
    partial_a = None
    for name, lhs, rhs in (("out", mix, dh1), ("gateT", dgate, hn2), ("upT", dup, hn2), ("down", act, dh2), ("plg", hn3, dgl)):
        partial_a = _dw(lhs, rhs, "dw_" + name, into=(partial_a, rows_a, SLAB[name][0] - SLAB["inT"][1]))
    dw_plp = _dw(p2, dpp, "dw_plp").reshape(4, 64, N_CHIPS, 256).transpose(2, 1, 0, 3).reshape(N_CHIPS, 64, D_MODEL)
    partial_a = reduce_a.start(lax.dynamic_update_slice(partial_a, dw_plp, (0, SLAB["plp"][0] - SLAB["inT"][1], 0)))
    dost, du, dyp, dscale = _mix_out_bwd(dh1, w_early, pooled, wpool, pool_scale, partial_a)
    pre_a = reduce_a.middle(du)
    dqst, dk, dv, dbias, dsink_rows = _attn_bwd(qst, kn, vb, dost, bias_st, sink_st, pre_a)
    dz, dx, dg_attn, dgq, dgk = _attn_in_bwd(dqst, zqk, dk, dv, du, x2, dh1, w_early, g_attn_norm, gq, gk)

    partial_b = _dw(dz, hn1, "dw_in").reshape(N_CHIPS, -1, D_MODEL)
    small = _small_pack(dg_attn, dg_ffn, dg_ple, dscale, dgq, dgk, dbias, dsink_rows, bucket, loss_v, _dw_pool(pooled, dyp))
    return dx, partial_b, small


def kernel(x, p, w_in, w_out, g_attn_norm, g_q, g_k, attn_sinks, rel_bias, w_pool, pool_scale, g_ffn_norm, w_gate, w_up, w_down, g_ple_norm, w_ple_gate, w_ple_proj, loss_target, m_w_in, m_w_out, m_g_attn_norm, m_g_q, m_g_k, m_attn_sinks, m_rel_bias, m_w_pool, m_pool_scale, m_g_ffn_norm, m_w_gate, m_w_up, m_w_down, m_g_ple_norm, m_w_ple_gate, m_w_ple_proj, v_w_in, v_w_out, v_g_attn_norm, v_g_q, v_g_k, v_attn_sinks, v_rel_bias, v_w_pool, v_pool_scale, v_g_ffn_norm, v_w_gate, v_w_up, v_w_down, v_g_ple_norm, v_w_ple_gate, v_w_ple_proj):
    core = lax.axis_index("c").astype(jnp.int32).reshape(1)
    me = (2 * lax.axis_index("x") + lax.axis_index("y")).astype(jnp.int32).reshape(1)

    local_slab = jnp.concatenate(
        [w_in[0].T, w_out[0], w_gate[0].T, w_up[0].T, w_down[0], w_ple_gate[0], _pack_ple_proj(w_ple_proj[0])],
        axis=0).astype(BF16)
    wts = (_ag_weights(local_slab, 0, EARLY_ROWS, "ag_early", 1),
           _ag_weights(local_slab, EARLY_ROWS, SLAB_ROWS - EARLY_ROWS, "ag_late", 2), local_slab, me)

    place = jnp.concatenate([me, core])
    reduce_a = _Reduction("a", place, ids=(3, 4))
    dx, partial_b, small = _local_grads(x[0], p[0, 0], loss_target[0], wts, g_attn_norm, g_q, g_k, attn_sinks, rel_bias,
                                        w_pool, pool_scale, g_ffn_norm, g_ple_norm, reduce_a)
    reduce_b = _Reduction("b", place)
    reduce_b.start(partial_b)
    reduce_b.middle(partial_b)
    grads_a, grads_b, small_all = _rs_finish(reduce_a.finish(), reduce_b.finish(), small)

    def rows(name):
        off, n_rows = SLAB[name]
        if name == "inT":
            return grads_b
        return grads_a[off - SLAB["inT"][1]:off - SLAB["inT"][1] + n_rows]

    big = {
        "w_in": (w_in, m_w_in, v_w_in, rows("inT"), True),
        "w_out": (w_out, m_w_out, v_w_out, rows("out"), False),
        "w_gate": (w_gate, m_w_gate, v_w_gate, rows("gateT"), True),
        "w_up": (w_up, m_w_up, v_w_up, rows("upT"), True),
        "w_down": (w_down, m_w_down, v_w_down, rows("down"), False),
        "w_ple_gate": (w_ple_gate, m_w_ple_gate, v_w_ple_gate, rows("plg"), False),
        "w_ple_proj": (w_ple_proj, m_w_ple_proj, v_w_ple_proj,
                       rows("plp").reshape(64, 4, 256).transpose(1, 0, 2).reshape(PLE_DIM, PLE_DIM), False),
    }
    small_params = {
        "g_attn_norm": (g_attn_norm, m_g_attn_norm, v_g_attn_norm), "g_q": (g_q, m_g_q, v_g_q), "g_k": (g_k, m_g_k, v_g_k),
        "attn_sinks": (attn_sinks, m_attn_sinks, v_attn_sinks), "rel_bias": (rel_bias, m_rel_bias, v_rel_bias),
        "w_pool": tuple(a.reshape(512, 128) for a in (w_pool, m_w_pool, v_w_pool)),
        "pool_scale": (pool_scale, m_pool_scale, v_pool_scale), "g_ffn_norm": (g_ffn_norm, m_g_ffn_norm, v_g_ffn_norm),
        "g_ple_norm": (g_ple_norm, m_g_ple_norm, v_g_ple_norm),
    }

    grads, deltas, new_ms, new_vs = {}, {}, {}, {}
    for name, (w, m, v, g2, transposed) in big.items():
        view = (lambda a: a.T) if transposed else (lambda a: a)
        d, nm, nv = _adamw(view(w[0]), g2, view(m[0]), view(v[0]), "adamw_" + name)
        grads[name], deltas[name], new_ms[name], new_vs[name] = (view(a)[None] for a in (g2, d, nm, nv))

    loss, small_out = _adamw_small(small_all, list(small_params.values()))
    for name, (g2, d, nm, nv) in zip(small_params, small_out):
        shape = w_pool.shape if name == "w_pool" else g2.shape
        grads[name], deltas[name], new_ms[name], new_vs[name] = (a.reshape(shape) for a in (g2, d, nm, nv))

    order = ["w_in", "w_out", "g_attn_norm", "g_q", "g_k", "attn_sinks", "rel_bias", "w_pool", "pool_scale", "g_ffn_norm",
             "w_gate", "w_up", "w_down", "g_ple_norm", "w_ple_gate", "w_ple_proj"]
    return (loss.reshape(()), dx[None], *[grads[n] for n in order], *[deltas[n] for n in order],
            *[new_ms[n] for n in order], *[new_vs[n] for n in order])
```

```python
import functools

import numpy as np
import jax
import jax.numpy as jnp
from jax import lax
from jax.experimental import pallas as pl
from jax.experimental.pallas import tpu as pltpu
from jax.experimental.pallas import tpu_sc as plsc

F32 = jnp.float32
BF16 = jnp.bfloat16
MESH = pl.DeviceIdType.MESH

D_MODEL = 1024
HEAD_DIM = 64
N_Q_HEADS = 8
ATTN_WIDTH = 512
KV_WIDTH = 128
POOL_WIDTH = 512
IN_WIDTH = 1280
D_FF = 2816
PLE_DIM = 256
FF_CHUNK = 1408
BLOCK = 128
N_BUCKETS = 32
MAX_DISTANCE = 128
POOL_SIZES = (2, 4, 8, 16)
EPS = 1e-6
NEG = -1e30
N_CHIPS = 4
N_DEV = 8

ADAM_LR = 0.001
ADAM_B1 = 0.9
ADAM_B2 = 0.999
ADAM_EPS = 1e-08
ADAM_WD = 0.01
ADAM_STEP = 10

SLAB = {"inT": (0, 320), "out": (320, 256), "gateT": (576, 704), "upT": (1280, 704), "down": (1984, 704),
        "plg": (2688, 256), "plp": (2944, 64)}
SLAB_ROWS = 3008
HALF_ROWS = SLAB_ROWS // 2
EARLY_ROWS = 576
POOL_HALO = 24

SMALL = {"g_attn": 0, "g_ffn": 8, "g_ple": 16, "pool_scale": 24, "g_q": 28, "g_k": 29, "sinks": 30, "loss": 31,
         "rel_bias": 32, "w_pool": 64}
SMALL_ROWS = 576

VMEM_LIMIT_BIG = 60 * 1024 * 1024
VMEM_LIMIT = 48 * 1024 * 1024


def _params(vmem=VMEM_LIMIT, n_axes=1):
    return pltpu.CompilerParams(dimension_semantics=("arbitrary",) * n_axes, vmem_limit_bytes=vmem)


def _dot(a, b, ca, cb):
    return lax.dot_general(a, b, (((ca,), (cb,)), ((), ())), preferred_element_type=F32)


def _full(shape):
    return pl.BlockSpec(shape, lambda i: (0,) * len(shape))


ANY = pl.BlockSpec(memory_space=pl.ANY)
VMEM_WHOLE = pl.BlockSpec(memory_space=pltpu.VMEM)


W_SPECS = [ANY, ANY, pl.BlockSpec(memory_space=pltpu.SMEM)]


def _load_rows(w_refs, name, dst_ref, sems):
    slab_ref, local_ref, me_ref = w_refs
    off, rows = SLAB[name]
    slab_off = off if off < EARLY_ROWS else off - EARLY_ROWS
    me = me_ref[0]
    for phase in ("start", "wait"):
        for j in range(N_CHIPS):
            dst = dst_ref.at[pl.ds(j * rows, rows), :]
            theirs = pltpu.make_async_copy(slab_ref.at[j, pl.ds(slab_off, rows), :], dst, sems.at[j])
            own = pltpu.make_async_copy(local_ref.at[pl.ds(off, rows), :], dst, sems.at[j])

            @pl.when(me == j)
            def _():
                getattr(own, phase)()

            @pl.when(me != j)
            def _():
                getattr(theirs, phase)()


def _rms_fwd(x, g):
    r = lax.rsqrt(jnp.mean(x * x, axis=-1, keepdims=True) + EPS)
    return x * r * g


def _rms_bwd(x, g, dy):
    r = lax.rsqrt(jnp.mean(x * x, axis=-1, keepdims=True) + EPS)
    xn = x * r
    dyg = dy * g
    dx = r * (dyg - xn * jnp.mean(dyg * xn, axis=-1, keepdims=True))
    return dx, jnp.sum(dy * xn, axis=0, keepdims=True)


def _half_sum(v, lo):
    s_lo = jnp.sum(jnp.where(lo, v, 0.0), axis=-1, keepdims=True)
    s_hi = jnp.sum(jnp.where(lo, 0.0, v), axis=-1, keepdims=True)
    return jnp.where(lo, s_lo, s_hi)


def _pair_norm(zp, g, lo):
    r = lax.rsqrt(_half_sum(zp * zp, lo) * (1.0 / HEAD_DIM) + EPS)
    return zp * r * g


def _pair_norm_bwd(zp, g, dy, lo):
    r = lax.rsqrt(_half_sum(zp * zp, lo) * (1.0 / HEAD_DIM) + EPS)
    xn = zp * r
    dyg = dy * g
    dx = r * (dyg - xn * (_half_sum(dyg * xn, lo) * (1.0 / HEAD_DIM)))
    return dx, jnp.sum(dy * xn, axis=0, keepdims=True)


def _to_stacked(pair, group, lo):
    rolled = pltpu.roll(pair, 64, axis=1)
    if group == 0:
        return jnp.where(lo, pair, 0.0), jnp.where(lo, rolled, 0.0)
    return jnp.where(lo, 0.0, rolled), jnp.where(lo, 0.0, pair)


def _from_stacked(even, odd, group, lo):
    if group == 0:
        return jnp.where(lo, even, pltpu.roll(odd, 64, axis=1))
    return jnp.where(lo, pltpu.roll(even, 64, axis=1), odd)


def _sigmoid(v):
    return 1.0 / (1.0 + jnp.exp(-v))


def _pool_counts(tile, n_rows):
    t1 = tile * n_rows + lax.broadcasted_iota(jnp.int32, (n_rows, POOL_WIDTH), 0) + 1
    lane = lax.broadcasted_iota(jnp.int32, (n_rows, POOL_WIDTH), 1)
    win = jnp.where(lane < 128, 2, jnp.where(lane < 256, 4, jnp.where(lane < 384, 8, 16)))
    return jnp.minimum(t1, win).astype(F32)


def _attn_in(x2, g_attn, gq, gk, wts):
    s_len = x2.shape[0]
    t = 512

    def body(x_ref, g_ref, gq_ref, gk_ref, sl_ref, lo_ref, me_ref, hn_ref, zqk_ref, u_ref, kn_ref, v_ref, qst_ref, w_ref, sems):
        @pl.when(pl.program_id(0) == 0)
        def _():
            _load_rows((sl_ref, lo_ref, me_ref), "inT", w_ref, sems)

        hn = _rms_fwd(x_ref[...], g_ref[...]).astype(BF16)
        hn_ref[...] = hn
        z = _dot(hn, w_ref[...], 1, 1)
        zqk_ref[...] = z[:, :640]
        u_ref[...] = z[:, 768:]
        v_ref[...] = z[:, 640:768].astype(BF16)
        lo = lax.broadcasted_iota(jnp.int32, (t, 128), 1) < 64
        kn_ref[...] = _pair_norm(z[:, 512:640], gk_ref[...], lo).astype(BF16)
        for p in range(4):
            qn = _pair_norm(z[:, 128 * p:128 * p + 128], gq_ref[...], lo)
            even, odd = _to_stacked(qn, p // 2, lo)
            qst_ref[2 * p] = even.astype(BF16)
            qst_ref[2 * p + 1] = odd.astype(BF16)

    row = lambda w: pl.BlockSpec((t, w), lambda i: (i, 0))
    return pl.pallas_call(
        body, name="attn_in", grid=(s_len // t,),
        in_specs=[row(D_MODEL), _full((1, D_MODEL)), _full((1, 128)), _full((1, 128))] + W_SPECS,
        out_specs=[row(D_MODEL), row(640), row(POOL_WIDTH), row(128), row(128),
                   pl.BlockSpec((N_Q_HEADS, t, 128), lambda i: (0, i, 0))],
        out_shape=[jax.ShapeDtypeStruct((s_len, D_MODEL), BF16), jax.ShapeDtypeStruct((s_len, 640), F32),
                   jax.ShapeDtypeStruct((s_len, POOL_WIDTH), F32), jax.ShapeDtypeStruct((s_len, 128), BF16),
                   jax.ShapeDtypeStruct((s_len, 128), BF16), jax.ShapeDtypeStruct((N_Q_HEADS, s_len, 128), BF16)],
        scratch_shapes=[pltpu.VMEM((IN_WIDTH, D_MODEL), BF16), pltpu.SemaphoreType.DMA((N_CHIPS,))],
        compiler_params=_params(),
    )(x2, g_attn, gq, gk, *wts)


def _bucket_table():
    i_idx = np.arange(BLOCK)[:, None]
    j_idx = np.arange(2 * BLOCK)[None, :]
    d = BLOCK + i_idx - j_idx
    n = np.maximum(d, 0)
    max_exact = N_BUCKETS // 2
    nf = np.maximum(n, 1).astype(np.float64)
    large = max_exact + (np.log(nf / max_exact) / np.log(MAX_DISTANCE / max_exact) * (N_BUCKETS - max_exact)).astype(np.int64)
    large = np.minimum(large, N_BUCKETS - 1)
    bucket = np.where(n < max_exact, n, large)
    return np.where((d >= 0) & (d < BLOCK), bucket, -1).astype(np.int32)


def _bias_build(rel_bias_t, bucket):
    def body(rb_ref, bucket_ref, out_ref):
        bk = bucket_ref[...]
        for h in range(N_Q_HEADS):
            acc = jnp.full((BLOCK, 2 * BLOCK), NEG, F32)
            for b in range(N_BUCKETS):
                acc = jnp.where(bk == b, rb_ref[h, b], acc)
            out_ref[0, pl.ds(h * BLOCK, BLOCK), :] = acc
            out_ref[1, pl.ds(h * BLOCK, BLOCK), :] = acc
            out_ref[1, pl.ds(h * BLOCK, BLOCK), 0:BLOCK] = jnp.full((BLOCK, BLOCK), NEG, F32)

    return pl.pallas_call(
        body, name="bias_build",
        in_specs=[pl.BlockSpec(memory_space=pltpu.SMEM), VMEM_WHOLE], out_specs=VMEM_WHOLE,
        out_shape=jax.ShapeDtypeStruct((2, N_Q_HEADS * BLOCK, 2 * BLOCK), F32),
    )(rel_bias_t, bucket)


def _head_softmax(s_ref, bias_ref, sink_ref, h):
    rows = pl.ds(pl.multiple_of(h * BLOCK, BLOCK), BLOCK)
    s = s_ref[rows, :] * (HEAD_DIM ** -0.5) + bias_ref[rows, :]
    sink = sink_ref[h]
    m = jnp.maximum(jnp.max(s, axis=-1, keepdims=True), sink)
    p = jnp.exp(s - m)
    e_sink = jnp.exp(sink - m)
    inv = 1.0 / (jnp.sum(p, axis=-1, keepdims=True) + e_sink)
    return rows, p * inv, e_sink * inv


def _attn_specs():
    prev = lambda i: (jnp.maximum(i - 1, 0), 0)
    cur = lambda i: (i, 0)
    stacked = pl.BlockSpec((N_Q_HEADS, BLOCK, 128), lambda i: (0, i, 0))
    kv = [pl.BlockSpec((BLOCK, 128), prev), pl.BlockSpec((BLOCK, 128), cur)]
    consts = [pl.BlockSpec((None, N_Q_HEADS * BLOCK, 2 * BLOCK), lambda i: (jnp.where(i == 0, 1, 0), 0, 0)),
              pl.BlockSpec(memory_space=pltpu.SMEM)]
    return stacked, kv, consts


def _head_lane_mask():
    rows = lax.broadcasted_iota(jnp.int32, (N_Q_HEADS * BLOCK, 128), 0)
    lanes = lax.broadcasted_iota(jnp.int32, (N_Q_HEADS * BLOCK, 128), 1)
    return (rows < 4 * BLOCK) == (lanes < 64)


def _attn_fwd(qst, kn, vb, bias_st, sinks):
    s_len = kn.shape[0]

    def body(q_ref, kp_ref, kc_ref, vp_ref, vc_ref, bias_ref, sink_ref, o_ref, s_ref, p_ref):
        q = q_ref[...].reshape(N_Q_HEADS * BLOCK, 128)
        s_ref[...] = _dot(q, jnp.concatenate([kp_ref[...], kc_ref[...]], axis=0), 1, 1)

        def head(h, carry):
            rows, probs, _ = _head_softmax(s_ref, bias_ref, sink_ref, h)
            p_ref[rows, :] = probs.astype(BF16)
            return carry

        lax.fori_loop(0, N_Q_HEADS, head, 0, unroll=True)
        o = _dot(p_ref[...], jnp.concatenate([vp_ref[...], vc_ref[...]], axis=0), 1, 0)
        o_ref[...] = jnp.where(_head_lane_mask(), o, 0.0).astype(BF16).reshape(N_Q_HEADS, BLOCK, 128)

    stacked, kv, consts = _attn_specs()
    return pl.pallas_call(
        body, name="attn_fwd", grid=(s_len // BLOCK,),
        in_specs=[stacked] + kv + kv + consts, out_specs=stacked,
        out_shape=jax.ShapeDtypeStruct((N_Q_HEADS, s_len, 128), BF16),
        scratch_shapes=[pltpu.VMEM((N_Q_HEADS * BLOCK, 2 * BLOCK), F32), pltpu.VMEM((N_Q_HEADS * BLOCK, 2 * BLOCK), BF16)],
        compiler_params=_params(),
    )(qst, kn, kn, vb, vb, bias_st, sinks)


def _mix_out(u, ost, x2, wts, wpool, pool_scale, g_ffn):
    s_len = x2.shape[0]
    t = 512
    n = t + 16

    def body(u_ref, o_ref, x_ref, sl_ref, lo_ref, me_ref, wp_ref, sc_ref, g_ref, pooled_ref, mix_ref, h1_ref, hn_ref,
             w_ref, ext_ref, st_ref, sems):
        i = pl.program_id(0)

        @pl.when(i == 0)
        def _():
            _load_rows((sl_ref, lo_ref, me_ref), "out", w_ref, sems)
            ext_ref[...] = jnp.zeros_like(ext_ref)
            st_ref[...] = jnp.zeros_like(st_ref)

        u_tile = u_ref[...]
        ext_ref[pl.ds(POOL_HALO, t), :] = u_tile
        st_ref[pl.ds(8, n), :] = ext_ref[pl.ds(8, n), :] + ext_ref[pl.ds(7, n), :]
        st_ref[pl.ds(8, n), 128:] = st_ref[pl.ds(8, n), 128:] + st_ref[pl.ds(6, n), 128:]
        st_ref[pl.ds(8, n), 256:] = st_ref[pl.ds(8, n), 256:] + st_ref[pl.ds(4, n), 256:]
        st_ref[pl.ds(8, n), 384:] = st_ref[pl.ds(8, n), 384:] + st_ref[pl.ds(0, n), 384:]
        ext_ref[pl.ds(0, POOL_HALO), :] = ext_ref[pl.ds(t, POOL_HALO), :]
        pooled = (st_ref[pl.ds(POOL_HALO, t), :] / _pool_counts(i, t) - u_tile).astype(BF16)
        pooled_ref[...] = pooled
        for g in range(4):
            cols = slice(128 * g, 128 * g + 128)
            y = _dot(pooled[:, cols], wp_ref[g], 1, 0) * sc_ref[:, cols]
            mix_ref[:, ATTN_WIDTH + 128 * g:ATTN_WIDTH + 128 * g + 128] = y.astype(BF16)
        lo = lax.broadcasted_iota(jnp.int32, (t, 128), 1) < 64
        for p in range(4):
            a = _from_stacked(o_ref[2 * p].astype(F32), o_ref[2 * p + 1].astype(F32), p // 2, lo)
            mix_ref[:, 128 * p:128 * p + 128] = a.astype(BF16)
        h1 = x_ref[...] + _dot(mix_ref[...], w_ref[...], 1, 0)
        h1_ref[...] = h1
        hn_ref[...] = _rms_fwd(h1, g_ref[...]).astype(BF16)

    row = lambda w: pl.BlockSpec((t, w), lambda i: (i, 0))
    return pl.pallas_call(
        body, name="mix_out", grid=(s_len // t,),
        in_specs=[row(POOL_WIDTH), pl.BlockSpec((N_Q_HEADS, t, 128), lambda i: (0, i, 0)), row(D_MODEL)] + W_SPECS
        + [_full((4, 128, 128)), _full((1, POOL_WIDTH)), _full((1, D_MODEL))],
        out_specs=[row(POOL_WIDTH), row(D_MODEL), row(D_MODEL), row(D_MODEL)],
        out_shape=[jax.ShapeDtypeStruct((s_len, POOL_WIDTH), BF16), jax.ShapeDtypeStruct((s_len, D_MODEL), BF16),
                   jax.ShapeDtypeStruct((s_len, D_MODEL), F32), jax.ShapeDtypeStruct((s_len, D_MODEL), BF16)],
        scratch_shapes=[pltpu.VMEM((D_MODEL, D_MODEL), BF16), pltpu.VMEM((t + POOL_HALO, POOL_WIDTH), F32),
                        pltpu.VMEM((t + POOL_HALO, POOL_WIDTH), F32), pltpu.SemaphoreType.DMA((N_CHIPS,))],
        compiler_params=_params(),
    )(u, ost, x2, *wts, wpool, pool_scale, g_ffn)


def _ffn_fwd(hn2, h1, wts):
    s_len = h1.shape[0]
    t = 256

    def body(hn_ref, h1_ref, sl_ref, lo_ref, me_ref, gate_ref, up_ref, h2_ref, wg_ref, wu_ref, wd_ref, sems):
        @pl.when(pl.program_id(0) == 0)
        def _():
            w_refs = (sl_ref, lo_ref, me_ref)
            _load_rows(w_refs, "gateT", wg_ref, sems)
            _load_rows(w_refs, "upT", wu_ref, sems)
            _load_rows(w_refs, "down", wd_ref, sems)

        hn = hn_ref[...]
        h2 = h1_ref[...]
        for ch in range(D_FF // FF_CHUNK):
            rows = pl.ds(ch * FF_CHUNK, FF_CHUNK)
            cols = slice(ch * FF_CHUNK, (ch + 1) * FF_CHUNK)
            gate = _dot(hn, wg_ref[rows, :], 1, 1)
            up = _dot(hn, wu_ref[rows, :], 1, 1)
            gate_ref[:, cols] = gate
            up_ref[:, cols] = up
            act = (gate * _sigmoid(gate) * up).astype(BF16)
            h2 = h2 + _dot(act, wd_ref[rows, :], 1, 0)
        h2_ref[...] = h2

    row = lambda w: pl.BlockSpec((t, w), lambda i: (i, 0))
    return pl.pallas_call(
        body, name="ffn_fwd", grid=(s_len // t,),
        in_specs=[row(D_MODEL), row(D_MODEL)] + W_SPECS,
        out_specs=[row(D_FF), row(D_FF), row(D_MODEL)],
        out_shape=[jax.ShapeDtypeStruct((s_len, D_FF), F32), jax.ShapeDtypeStruct((s_len, D_FF), F32),
                   jax.ShapeDtypeStruct((s_len, D_MODEL), F32)],
        scratch_shapes=[pltpu.VMEM((D_FF, D_MODEL), BF16)] * 3 + [pltpu.SemaphoreType.DMA((N_CHIPS,))],
        compiler_params=_params(VMEM_LIMIT_BIG),
    )(hn2, h1, *wts)


def _ple_loss(h2, p2, tgt, wts, g_ple):
    s_len = h2.shape[0]
    t = 512
    n_tiles = s_len // t

    def body(h2_ref, p_ref, tgt_ref, sl_ref, lo_ref, me_ref, g_ref, loss_ref, dh2_ref, dgl_ref, dpp_ref, hn_ref,
             dg_ref, w_ref, wp_ref, packed_ref, loss_acc, sems):
        i = pl.program_id(0)

        @pl.when(i == 0)
        def _():
            w_refs = (sl_ref, lo_ref, me_ref)
            _load_rows(w_refs, "plg", w_ref, sems)
            _load_rows(w_refs, "plp", packed_ref, sems)
            for j in range(N_CHIPS):
                for q in range(4):
                    wp_ref[pl.ds(64 * q, 64), 256 * j:256 * j + 256] = packed_ref[pl.ds(64 * j, 64), 256 * q:256 * q + 256]
            loss_acc[...] = jnp.zeros_like(loss_acc)
            dg_ref[...] = jnp.zeros_like(dg_ref)

        h2v = h2_ref[...]
        g = g_ref[...]
        hn = _rms_fwd(h2v, g).astype(BF16)
        hn_ref[...] = hn
        gate = _sigmoid(_dot(hn, w_ref[...], 1, 0))
        pp = _dot(p_ref[...].astype(BF16), wp_ref[...], 1, 0)
        err = h2v + gate * pp - tgt_ref[...]
        loss_acc[...] += jnp.sum(err * err, axis=0, keepdims=True)
        dy = err * (1.0 / D_MODEL)
        dpp_ref[...] = (dy * gate).astype(BF16)
        dgl = (dy * pp * gate * (1.0 - gate)).astype(BF16)
        dgl_ref[...] = dgl
        dx, dg = _rms_bwd(h2v, g, _dot(dgl, w_ref[...], 1, 1))
        dh2_ref[...] = dy + dx
        dg_ref[...] += dg

        @pl.when(i == n_tiles - 1)
        def _():
            total = jnp.sum(loss_acc[...], axis=-1, keepdims=True) * (0.5 / D_MODEL)
            loss_ref[...] = jnp.broadcast_to(total, loss_ref.shape)

    row = lambda w: pl.BlockSpec((t, w), lambda i: (i, 0))
    return pl.pallas_call(
        body, name="ple_loss", grid=(n_tiles,),
        in_specs=[row(D_MODEL), row(PLE_DIM), row(D_MODEL)] + W_SPECS + [_full((1, D_MODEL))],
        out_specs=[_full((1, 128)), row(D_MODEL), row(D_MODEL), row(D_MODEL), row(D_MODEL), _full((1, D_MODEL))],
        out_shape=[jax.ShapeDtypeStruct((1, 128), F32), jax.ShapeDtypeStruct((s_len, D_MODEL), F32),
                   jax.ShapeDtypeStruct((s_len, D_MODEL), BF16), jax.ShapeDtypeStruct((s_len, D_MODEL), BF16),
                   jax.ShapeDtypeStruct((s_len, D_MODEL), BF16), jax.ShapeDtypeStruct((1, D_MODEL), F32)],
        scratch_shapes=[pltpu.VMEM((D_MODEL, D_MODEL), BF16), pltpu.VMEM((PLE_DIM, D_MODEL), BF16),
                        pltpu.VMEM((PLE_DIM, D_MODEL), BF16), pltpu.VMEM((1, D_MODEL), F32),
                        pltpu.SemaphoreType.DMA((N_CHIPS,))],
        compiler_params=_params(),
    )(h2, p2, tgt, *wts, g_ple)


def _ffn_bwd(dh2, gate, up, h1, wts, g_ffn):
    s_len = h1.shape[0]
    t = 256

    def body(dh2_ref, gate_ref, up_ref, h1_ref, sl_ref, lo_ref, me_ref, g_ref, dgate_ref, dup_ref, act_ref, dh1_ref, dg_ref,
             wg_ref, wu_ref, wd_ref, sems):
        @pl.when(pl.program_id(0) == 0)
        def _():
            w_refs = (sl_ref, lo_ref, me_ref)
            _load_rows(w_refs, "gateT", wg_ref, sems)
            _load_rows(w_refs, "upT", wu_ref, sems)
            _load_rows(w_refs, "down", wd_ref, sems)
            dg_ref[...] = jnp.zeros_like(dg_ref)

        dh2v = dh2_ref[...]
        dh2b = dh2v.astype(BF16)
        dhn = jnp.zeros((t, D_MODEL), F32)
        for ch in range(D_FF // FF_CHUNK):
            rows = pl.ds(ch * FF_CHUNK, FF_CHUNK)
            cols = slice(ch * FF_CHUNK, (ch + 1) * FF_CHUNK)
            dact = _dot(dh2b, wd_ref[rows, :], 1, 1)
            gate_v = gate_ref[:, cols]
            up_v = up_ref[:, cols]
            sg = _sigmoid(gate_v)
            silu = gate_v * sg
            act_ref[:, cols] = (silu * up_v).astype(BF16)
            dup = (dact * silu).astype(BF16)
            dgate = (dact * up_v * (sg * (1.0 + gate_v * (1.0 - sg)))).astype(BF16)
            dup_ref[:, cols] = dup
            dgate_ref[:, cols] = dgate
            dhn = dhn + _dot(dgate, wg_ref[rows, :], 1, 0) + _dot(dup, wu_ref[rows, :], 1, 0)
        dx, dg = _rms_bwd(h1_ref[...], g_ref[...], dhn)
        dh1_ref[...] = dh2v + dx
        dg_ref[...] += dg

    row = lambda w: pl.BlockSpec((t, w), lambda i: (i, 0))
    return pl.pallas_call(
        body, name="ffn_bwd", grid=(s_len // t,),
        in_specs=[row(D_MODEL), row(D_FF), row(D_FF), row(D_MODEL)] + W_SPECS + [_full((1, D_MODEL))],
        out_specs=[row(D_FF), row(D_FF), row(D_FF), row(D_MODEL), _full((1, D_MODEL))],
        out_shape=[jax.ShapeDtypeStruct((s_len, D_FF), BF16), jax.ShapeDtypeStruct((s_len, D_FF), BF16),
                   jax.ShapeDtypeStruct((s_len, D_FF), BF16), jax.ShapeDtypeStruct((s_len, D_MODEL), F32),
                   jax.ShapeDtypeStruct((1, D_MODEL), F32)],
        scratch_shapes=[pltpu.VMEM((D_FF, D_MODEL), BF16)] * 3 + [pltpu.SemaphoreType.DMA((N_CHIPS,))],
        compiler_params=_params(VMEM_LIMIT_BIG),
    )(dh2, gate, up, h1, *wts, g_ffn)


def _mix_out_bwd(dh1, wts, pooled, wpool, pool_scale, after):
    s_len = dh1.shape[0]
    t = 512
    n = t + 16
    n_tiles = s_len // t

    def body(dh1_ref, sl_ref, lo_ref, me_ref, pooled_ref, wp_ref, sc_ref, after_ref, dost_ref, du_ref, dyp_ref, dsc_ref,
             w_ref, ext_ref, st_ref, sems):
        del after_ref
        i = pl.program_id(0)

        @pl.when(i == 0)
        def _():
            _load_rows((sl_ref, lo_ref, me_ref), "out", w_ref, sems)
            ext_ref[...] = jnp.zeros_like(ext_ref)
            st_ref[...] = jnp.zeros_like(st_ref)
            dsc_ref[...] = jnp.zeros_like(dsc_ref)

        dmix = _dot(dh1_ref[...].astype(BF16), w_ref[...], 1, 1)
        lo = lax.broadcasted_iota(jnp.int32, (t, 128), 1) < 64
        for p in range(4):
            even, odd = _to_stacked(dmix[:, 128 * p:128 * p + 128], p // 2, lo)
            dost_ref[2 * p] = even.astype(BF16)
            dost_ref[2 * p + 1] = odd.astype(BF16)
        pooled_v = pooled_ref[...]
        counts = _pool_counts(n_tiles - 1 - i, t)
        for g in range(4):
            cols = slice(128 * g, 128 * g + 128)
            dm = dmix[:, ATTN_WIDTH + 128 * g:ATTN_WIDTH + 128 * g + 128]
            ypre = _dot(pooled_v[:, cols], wp_ref[g], 1, 0)
            dsc_ref[:, cols] += jnp.sum(ypre * dm, axis=0, keepdims=True)
            dyp = (dm * sc_ref[:, cols]).astype(BF16)
            dyp_ref[:, cols] = dyp
            dpooled = _dot(dyp, wp_ref[g], 1, 1)
            du_ref[:, cols] = -dpooled
            ext_ref[pl.ds(0, t), cols] = dpooled / counts[:, cols]
        st_ref[pl.ds(0, n), :] = ext_ref[pl.ds(0, n), :] + ext_ref[pl.ds(1, n), :]
        st_ref[pl.ds(0, n), 128:] = st_ref[pl.ds(0, n), 128:] + st_ref[pl.ds(2, n), 128:]
        st_ref[pl.ds(0, n), 256:] = st_ref[pl.ds(0, n), 256:] + st_ref[pl.ds(4, n), 256:]
        st_ref[pl.ds(0, n), 384:] = st_ref[pl.ds(0, n), 384:] + st_ref[pl.ds(8, n), 384:]
        ext_ref[pl.ds(t, POOL_HALO), :] = ext_ref[pl.ds(0, POOL_HALO), :]
        du_ref[...] += st_ref[pl.ds(0, t), :]

    rev = lambda w: pl.BlockSpec((t, w), lambda i: (n_tiles - 1 - i, 0))
    return pl.pallas_call(
        body, name="mix_out_bwd", grid=(n_tiles,),
        in_specs=[rev(D_MODEL)] + W_SPECS + [rev(POOL_WIDTH), _full((4, 128, 128)), _full((1, POOL_WIDTH)), ANY],
        out_specs=[pl.BlockSpec((N_Q_HEADS, t, 128), lambda i: (0, n_tiles - 1 - i, 0)), rev(POOL_WIDTH), rev(POOL_WIDTH),
                   _full((1, POOL_WIDTH))],
        out_shape=[jax.ShapeDtypeStruct((N_Q_HEADS, s_len, 128), BF16), jax.ShapeDtypeStruct((s_len, POOL_WIDTH), F32),
                   jax.ShapeDtypeStruct((s_len, POOL_WIDTH), BF16), jax.ShapeDtypeStruct((1, POOL_WIDTH), F32)],
        scratch_shapes=[pltpu.VMEM((D_MODEL, D_MODEL), BF16), pltpu.VMEM((t + POOL_HALO, POOL_WIDTH), F32),
                        pltpu.VMEM((t + POOL_HALO, POOL_WIDTH), F32), pltpu.SemaphoreType.DMA((N_CHIPS,))],
        compiler_params=_params(),
    )(dh1, *wts, pooled, wpool, pool_scale, after)


def _attn_bwd(qst, kn, vb, dost, bias_st, sinks, after):
    s_len = kn.shape[0]

    def body(q_ref, kp_ref, kc_ref, vp_ref, vc_ref, do_ref, bias_ref, sink_ref, after_ref, dq_ref, dk_ref, dv_ref, dbias_ref,
             dsink_ref, s_ref, dp_ref, p_ref, dl_ref):
        del after_ref
        i = pl.program_id(0)

        @pl.when(i == 0)
        def _():
            dk_ref[...] = jnp.zeros_like(dk_ref)
            dv_ref[...] = jnp.zeros_like(dv_ref)
            dbias_ref[...] = jnp.zeros_like(dbias_ref)
            dsink_ref[...] = jnp.zeros_like(dsink_ref)

        q = q_ref[...].reshape(N_Q_HEADS * BLOCK, 128)
        do = do_ref[...].reshape(N_Q_HEADS * BLOCK, 128)
        k2 = jnp.concatenate([kp_ref[...], kc_ref[...]], axis=0)
        s_ref[...] = _dot(q, k2, 1, 1)
        dp_ref[...] = _dot(do, jnp.concatenate([vp_ref[...], vc_ref[...]], axis=0), 1, 1)

        def head(h, carry):
            rows, probs, p_sink = _head_softmax(s_ref, bias_ref, sink_ref, h)
            dp = dp_ref[rows, :]
            dsum = jnp.sum(probs * dp, axis=-1, keepdims=True)
            dlog = probs * (dp - dsum)
            dsink_ref[rows, :] -= p_sink * dsum
            dbias_ref[rows, :] += dlog
            p_ref[rows, :] = probs.astype(BF16)
            dl_ref[rows, :] = (dlog * (HEAD_DIM ** -0.5)).astype(BF16)
            return carry

        lax.fori_loop(0, N_Q_HEADS, head, 0, unroll=True)
        dlog_s = dl_ref[...]
        dq_ref[...] = jnp.where(_head_lane_mask(), _dot(dlog_s, k2, 1, 0), 0.0).reshape(N_Q_HEADS, BLOCK, 128)
        dk2 = _dot(dlog_s, q, 0, 0)
        dv2 = _dot(p_ref[...], do, 0, 0)
        prev_rows = pl.ds(pl.multiple_of(jnp.maximum(i - 1, 0) * BLOCK, BLOCK), BLOCK)
        cur_rows = pl.ds(pl.multiple_of(i * BLOCK, BLOCK), BLOCK)
        dk_ref[prev_rows, :] += dk2[:BLOCK]
        dk_ref[cur_rows, :] += dk2[BLOCK:]
        dv_ref[prev_rows, :] += dv2[:BLOCK]
        dv_ref[cur_rows, :] += dv2[BLOCK:]

    stacked, kv, consts = _attn_specs()
    band = (N_Q_HEADS * BLOCK, 2 * BLOCK)
    return pl.pallas_call(
        body, name="attn_bwd", grid=(s_len // BLOCK,),
        in_specs=[stacked] + kv + kv + [stacked] + consts + [ANY],
        out_specs=[stacked, _full((s_len, 128)), _full((s_len, 128)), _full(band), _full((N_Q_HEADS * BLOCK, 1))],
        out_shape=[jax.ShapeDtypeStruct((N_Q_HEADS, s_len, 128), F32), jax.ShapeDtypeStruct((s_len, 128), F32),
                   jax.ShapeDtypeStruct((s_len, 128), F32), jax.ShapeDtypeStruct(band, F32),
                   jax.ShapeDtypeStruct((N_Q_HEADS * BLOCK, 1), F32)],
        scratch_shapes=[pltpu.VMEM(band, F32), pltpu.VMEM(band, F32), pltpu.VMEM(band, BF16), pltpu.VMEM(band, BF16)],
        compiler_params=_params(),
    )(qst, kn, kn, vb, vb, dost, bias_st, sinks, after)


def _small_pack(dg_attn, dg_ffn, dg_ple, dscale, dgq, dgk, dbias, dsink_rows, bucket, loss_v, dwpool):
    def body(ga_ref, gf_ref, gp_ref, sc_ref, gq_ref, gk_ref, db_ref, ds_ref, bucket_ref, loss_ref, wp_ref, out_ref):
        out_ref[pl.ds(0, SMALL["w_pool"]), :] = jnp.zeros((SMALL["w_pool"], 128), F32)
        for name, ref, n in (("g_attn", ga_ref, 8), ("g_ffn", gf_ref, 8), ("g_ple", gp_ref, 8), ("pool_scale", sc_ref, 4)):
            for k in range(n):
                out_ref[pl.ds(SMALL[name] + k, 1), :] = ref[:, 128 * k:128 * k + 128]
        for name, ref in (("g_q", gq_ref), ("g_k", gk_ref)):
            both = ref[...]
            out_ref[pl.ds(SMALL[name], 1), :] = both + pltpu.roll(both, 64, axis=1)
        out_ref[pl.ds(SMALL["loss"], 1), :] = loss_ref[...]
        bk = bucket_ref[...]
        rows = lax.broadcasted_iota(jnp.int32, (N_BUCKETS, 128), 0)
        lanes = lax.broadcasted_iota(jnp.int32, (N_BUCKETS, 128), 1)
        lane1 = lax.broadcasted_iota(jnp.int32, (1, 128), 1)
        rb = jnp.zeros((N_BUCKETS, 128), F32)
        sk = jnp.zeros((1, 128), F32)
        for h in range(N_Q_HEADS):
            band = db_ref[pl.ds(h * BLOCK, BLOCK), :]
            for b in range(N_BUCKETS):
                rb = jnp.where((rows == b) & (lanes == h), jnp.sum(jnp.where(bk == b, band, 0.0)), rb)
            sk = jnp.where(lane1 == h, jnp.sum(ds_ref[pl.ds(h * BLOCK, BLOCK), :]), sk)
        out_ref[pl.ds(SMALL["rel_bias"], N_BUCKETS), :] = rb
        out_ref[pl.ds(SMALL["sinks"], 1), :] = sk
        out_ref[pl.ds(SMALL["w_pool"], 512), :] = wp_ref[...].reshape(512, 128)

    return pl.pallas_call(
        body, name="small_pack", in_specs=[VMEM_WHOLE] * 11, out_specs=VMEM_WHOLE,
        out_shape=jax.ShapeDtypeStruct((SMALL_ROWS, 128), F32),
    )(dg_attn, dg_ffn, dg_ple, dscale, dgq, dgk, dbias, dsink_rows, bucket, loss_v, dwpool)


def _attn_in_bwd(dqst, zqk, dk, dv, du, x2, dh1, wts, g_attn, gq, gk):
    s_len = x2.shape[0]
    t = 512

    def body(dq_ref, zqk_ref, dk_ref, dv_ref, du_ref, x_ref, dh1_ref, sl_ref, lo_ref, me_ref, g_ref, gq_ref, gk_ref,
             dz_ref, dx_ref, dg_ref, dgq_ref, dgk_ref, w_ref, sems):
        @pl.when(pl.program_id(0) == 0)
        def _():
            _load_rows((sl_ref, lo_ref, me_ref), "inT", w_ref, sems)
            dg_ref[...] = jnp.zeros_like(dg_ref)
            dgq_ref[...] = jnp.zeros_like(dgq_ref)
            dgk_ref[...] = jnp.zeros_like(dgk_ref)

        lo = lax.broadcasted_iota(jnp.int32, (t, 128), 1) < 64
        for p in range(4):
            dqn = _from_stacked(dq_ref[2 * p], dq_ref[2 * p + 1], p // 2, lo)
            dq_raw, dgq = _pair_norm_bwd(zqk_ref[:, 128 * p:128 * p + 128], gq_ref[...], dqn, lo)
            dz_ref[:, 128 * p:128 * p + 128] = dq_raw.astype(BF16)
            dgq_ref[...] += dgq
        dk_raw, dgk = _pair_norm_bwd(zqk_ref[:, 512:640], gk_ref[...], dk_ref[...], lo)
        dgk_ref[...] += dgk
        dz_ref[:, 512:640] = dk_raw.astype(BF16)
        dz_ref[:, 640:768] = dv_ref[...].astype(BF16)
        dz_ref[:, 768:] = du_ref[...].astype(BF16)
        dx, dg = _rms_bwd(x_ref[...], g_ref[...], _dot(dz_ref[...], w_ref[...], 1, 0))
        dx_ref[...] = dh1_ref[...] + dx
        dg_ref[...] += dg

    row = lambda w: pl.BlockSpec((t, w), lambda i: (i, 0))
    return pl.pallas_call(
        body, name="attn_in_bwd", grid=(s_len // t,),
        in_specs=[pl.BlockSpec((N_Q_HEADS, t, 128), lambda i: (0, i, 0)), row(640), row(128), row(128), row(POOL_WIDTH),
                  row(D_MODEL), row(D_MODEL)] + W_SPECS + [_full((1, D_MODEL)), _full((1, 128)), _full((1, 128))],
        out_specs=[row(IN_WIDTH), row(D_MODEL), _full((1, D_MODEL)), _full((1, 128)), _full((1, 128))],
        out_shape=[jax.ShapeDtypeStruct((s_len, IN_WIDTH), BF16), jax.ShapeDtypeStruct((s_len, D_MODEL), F32),
                   jax.ShapeDtypeStruct((1, D_MODEL), F32), jax.ShapeDtypeStruct((1, 128), F32),
                   jax.ShapeDtypeStruct((1, 128), F32)],
        scratch_shapes=[pltpu.VMEM((IN_WIDTH, D_MODEL), BF16), pltpu.SemaphoreType.DMA((N_CHIPS,))],
        compiler_params=_params(),
    )(dqst, zqk, dk, dv, du, x2, dh1, *wts, g_attn, gq, gk)


def _dw(a, b, name, into=None):
    s_len, m = a.shape
    n_out = b.shape[1]
    tk = 512
    n_steps = s_len // tk
    tm = m // 2 if m > 1408 else m
    chunk = m // N_CHIPS
    per_tile = tm // chunk

    def accumulate(a_ref, b_ref, acc_ref, k):
        @pl.when(k == 0)
        def _():
            acc_ref[...] = jnp.zeros_like(acc_ref)

        acc_ref[...] += _dot(a_ref[...].astype(BF16), b_ref[...].astype(BF16), 0, 0)

    in_specs = [pl.BlockSpec((tk, tm), lambda i, k: (k, i)), pl.BlockSpec((tk, n_out), lambda i, k: (k, 0))]
    if into is None:
        def body(a_ref, b_ref, o_ref, acc_ref):
            k = pl.program_id(1)
            accumulate(a_ref, b_ref, acc_ref, k)

            @pl.when(k == n_steps - 1)
            def _():
                o_ref[...] = acc_ref[...].astype(BF16)

        return pl.pallas_call(
            body, name=name, grid=(m // tm, n_steps), in_specs=in_specs,
            out_specs=pl.BlockSpec((tm, n_out), lambda i, k: (i, 0)), out_shape=jax.ShapeDtypeStruct((m, n_out), BF16),
            scratch_shapes=[pltpu.VMEM((tm, n_out), F32)], compiler_params=_params(n_axes=2),
        )(a, b)

    slab, slab_rows, row_off = into
    assert n_out == D_MODEL

    def body_into(a_ref, b_ref, *rest):
        o_ref, acc_ref, stage_ref, sems = rest[-4:]
        i, k = pl.program_id(0), pl.program_id(1)
        accumulate(a_ref, b_ref, acc_ref, k)

        @pl.when(k == n_steps - 1)
        def _():
            stage_ref[...] = acc_ref[...].astype(BF16)
            copies = [pltpu.make_async_copy(stage_ref.at[pl.ds(jj * chunk, chunk), :],
                                            o_ref.at[i * per_tile + jj, pl.ds(row_off, chunk), :], sems.at[jj])
                      for jj in range(per_tile)]
            for cp in copies:
                cp.start()
            for cp in copies:
                cp.wait()

    operands, aliases = [a, b], {}
    if slab is not None:
        in_specs = in_specs + [ANY]
        operands.append(slab)
        aliases = {2: 0}
    return pl.pallas_call(
        body_into, name=name, grid=(m // tm, n_steps), in_specs=in_specs, out_specs=ANY,
        out_shape=jax.ShapeDtypeStruct((N_CHIPS, slab_rows, D_MODEL), BF16), input_output_aliases=aliases,
        scratch_shapes=[pltpu.VMEM((tm, n_out), F32), pltpu.VMEM((tm, n_out), BF16), pltpu.SemaphoreType.DMA((per_tile,))],
        compiler_params=_params(n_axes=2),
    )(*operands)


def _dw_pool(pooled, dyp):
    s_len = pooled.shape[0]
    tk = 512

    def body(a_ref, b_ref, o_ref):
        @pl.when(pl.program_id(0) == 0)
        def _():
            o_ref[...] = jnp.zeros_like(o_ref)

        for g in range(4):
            cols = slice(128 * g, 128 * g + 128)
            o_ref[g] += _dot(a_ref[:, cols], b_ref[:, cols], 0, 0)

    blk = pl.BlockSpec((tk, POOL_WIDTH), lambda k: (k, 0))
    return pl.pallas_call(
        body, name="dw_pool", grid=(s_len // tk,), in_specs=[blk, blk], out_specs=_full((4, 128, 128)),
        out_shape=jax.ShapeDtypeStruct((4, 128, 128), F32), compiler_params=_params(),
    )(pooled, dyp)


def _position():
    x, y, c = lax.axis_index("x"), lax.axis_index("y"), lax.axis_index("c")
    other_chips = [(1 - x, y), (x, 1 - y), (1 - x, 1 - y)]
    return x, y, c, other_chips


def _ag_weights(local_slab, row0, n_rows, name, collective_id):
    half = n_rows // 2

    def body(l_ref, g_ref, send, recv):
        x, y, c, chips = _position()
        me = 2 * x + y
        sibling = (x, y, 1 - c)
        peers = [sibling] + [(*chip, c) for chip in chips]
        barrier = pltpu.get_barrier_semaphore()
        for peer in peers:
            pl.semaphore_signal(barrier, inc=1, device_id=peer, device_id_type=MESH)
        pl.semaphore_wait(barrier, len(peers))
        mine = pl.ds(pl.multiple_of(c * half, 16), half)
        theirs = pl.ds(pl.multiple_of((1 - c) * half, 16), half)

        def copy(k, chip_idx, rows, to, src=None):
            dst = g_ref.at[chip_idx, rows, :]
            return pltpu.make_async_remote_copy(src_ref=dst if src is None else src, dst_ref=dst, send_sem=send.at[k],
                                                recv_sem=recv.at[k], device_id=to, device_id_type=MESH)

        own_rows = l_ref.at[pl.ds(pl.multiple_of(row0 + c * half, 16), half), :]
        first = [copy(k, me, mine, (*chip, c), src=own_rows) for k, chip in enumerate(chips)]
        for cp in first:
            cp.start()
        passed = []
        for k, chip in enumerate(chips):
            idx = 2 * chip[0] + chip[1]
            copy(k, idx, mine, (x, y, c)).wait_recv()
            fwd = copy(3 + k, idx, mine, sibling)
            fwd.start()
            passed.append(fwd)
        for k, chip in enumerate(chips):
            copy(3 + k, 2 * chip[0] + chip[1], theirs, (x, y, c)).wait_recv()
        for cp in first + passed:
            cp.wait_send()

    return pl.kernel(
        body, out_type=jax.ShapeDtypeStruct((N_CHIPS, n_rows, D_MODEL), BF16),
        mesh=plsc.ScalarSubcoreMesh(axis_name="sequencer", num_cores=1), name=name,
        scratch_types=[pltpu.SemaphoreType.DMA((6,)), pltpu.SemaphoreType.DMA((6,))],
        compiler_params=pltpu.CompilerParams(collective_id=collective_id),
    )(local_slab)


def _comm_call(body, peers_of, out_shape, n_sems, operand, name, collective_id):
    sems = [pltpu.SemaphoreType.DMA((n_sems,)), pltpu.SemaphoreType.DMA((n_sems,))]
    if collective_id is None:
        return pl.pallas_call(body, name=name, in_specs=[ANY], out_specs=ANY, out_shape=out_shape, scratch_shapes=sems)(operand)

    def with_handshake(in_ref, out_ref, send, recv):
        x, y, c, _ = _position()
        peers = peers_of(x, y, c)
        barrier = pltpu.get_barrier_semaphore()
        for peer in peers:
            pl.semaphore_signal(barrier, inc=1, device_id=peer, device_id_type=MESH)
        pl.semaphore_wait(barrier, len(peers))
        body(in_ref, out_ref, send, recv)

    return pl.kernel(with_handshake, out_type=out_shape, mesh=plsc.ScalarSubcoreMesh(axis_name="sequencer", num_cores=1),
                     name=name, scratch_types=sems, compiler_params=pltpu.CompilerParams(collective_id=collective_id))(operand)


def _rs_swap_halves(partial, name, collective_id=None):
    half = partial.shape[1] // 2

    def body(p_ref, r_ref, send, recv):
        x, y, c, _ = _position()
        theirs = pl.ds(pl.multiple_of((1 - c) * half, 16), half)
        cp = pltpu.make_async_remote_copy(src_ref=p_ref.at[:, theirs, :], dst_ref=r_ref, send_sem=send.at[0],
                                          recv_sem=recv.at[0], device_id=(x, y, 1 - c), device_id_type=MESH)
        cp.start()
        cp.wait()

    return _comm_call(body, lambda x, y, c: [(x, y, 1 - c)], jax.ShapeDtypeStruct((N_CHIPS, half, D_MODEL), BF16), 1,
                      partial, name, collective_id)


def _rs_add_halves(partial, other, core, name, after):
    half = other.shape[1]
    t = half // 2
    steps = half // t

    def body(core_ref, a_ref, b_ref, after_ref, o_ref):
        del after_ref
        o_ref[...] = (a_ref[...].astype(F32) + b_ref[...].astype(F32)).astype(BF16)

    return pl.pallas_call(
        body, name=name,
        grid_spec=pltpu.PrefetchScalarGridSpec(
            num_scalar_prefetch=1, grid=(N_CHIPS, steps),
            in_specs=[pl.BlockSpec((1, t, D_MODEL), lambda j, i, core_ref: (j, core_ref[0] * steps + i, 0)),
                      pl.BlockSpec((1, t, D_MODEL), lambda j, i, core_ref: (j, i, 0)), ANY],
            out_specs=pl.BlockSpec((1, t, D_MODEL), lambda j, i, core_ref: (j, i, 0))),
        out_shape=jax.ShapeDtypeStruct((N_CHIPS, half, D_MODEL), BF16),
        compiler_params=_params(n_axes=2),
    )(core, partial, other, after)


def _rs_exchange_chips(pre, name, collective_id=None):
    def body(s_ref, r_ref, send, recv):
        x, y, c, chips = _position()

        def copy(k, chunk, to):
            return pltpu.make_async_remote_copy(src_ref=s_ref.at[chunk], dst_ref=r_ref.at[k], send_sem=send.at[k],
                                                recv_sem=recv.at[k], device_id=to, device_id_type=MESH)

        sends = [copy(k, 2 * chip[0] + chip[1], (*chip, c)) for k, chip in enumerate(chips)]
        for cp in sends:
            cp.start()
        for cp in sends:
            cp.wait()

    return _comm_call(body, lambda x, y, c: [(1 - x, y, c), (x, 1 - y, c), (1 - x, 1 - y, c)],
                      jax.ShapeDtypeStruct((3, pre.shape[1], D_MODEL), BF16), 3, pre, name, collective_id)


def _rs_sum_chips(pre, received, place, name):
    half = pre.shape[1]
    t = half // 2 if half > 512 else half
    steps = half // t

    def body(place_ref, own_ref, r_ref, o_ref):
        acc = own_ref[0].astype(F32)
        for k in range(3):
            acc = acc + r_ref[k].astype(F32)
        o_ref[...] = acc

    return pl.pallas_call(
        body, name=name,
        grid_spec=pltpu.PrefetchScalarGridSpec(
            num_scalar_prefetch=1, grid=(steps,),
            in_specs=[pl.BlockSpec((1, t, D_MODEL), lambda i, place_ref: (place_ref[0], i, 0)),
                      pl.BlockSpec((3, t, D_MODEL), lambda i, place_ref: (0, i, 0))],
            out_specs=pl.BlockSpec((t, D_MODEL), lambda i, place_ref: (place_ref[1] * steps + i, 0))),
        out_shape=jax.ShapeDtypeStruct((2 * half, D_MODEL), F32),
        compiler_params=_params(),
    )(place, pre, received)


def _rs_finish(grads_a, grads_b, small):
    def body(fa_ref, fb_ref, s_ref, ga_ref, gb_ref, t_ref, send, recv, local_sem):
        del fa_ref, fb_ref
        x, y, c, chips = _position()
        sibling = (x, y, 1 - c)

        def slot(px, py, pc):
            return t_ref.at[4 * px + 2 * py + pc]

        def copy(k, block, to, src=None):
            return pltpu.make_async_remote_copy(src_ref=slot(*block) if src is None else src, dst_ref=slot(*block),
                                                send_sem=send.at[k], recv_sem=recv.at[k], device_id=to, device_id_type=MESH)

        def half_copies(core, to):
            out = []
            for k, g_ref in ((7, ga_ref), (8, gb_ref)):
                half = g_ref.shape[0] // 2
                rows = g_ref.at[pl.ds(pl.multiple_of(core * half, 8), half), :]
                out.append(pltpu.make_async_remote_copy(src_ref=rows, dst_ref=rows, send_sem=send.at[k], recv_sem=recv.at[k],
                                                        device_id=to, device_id_type=MESH))
            return out

        own_small = pltpu.make_async_copy(s_ref, slot(x, y, c), local_sem)
        own_small.start()
        to_sibling = half_copies(c, sibling)
        for cp in to_sibling:
            cp.start()
        first = [copy(0, (x, y, c), sibling, src=s_ref)]
        first += [copy(1 + k, (x, y, c), (*chip, c), src=s_ref) for k, chip in enumerate(chips)]
        for cp in first:
            cp.start()
        passed = []
        for k, chip in enumerate(chips):
            copy(1 + k, (*chip, c), (x, y, c)).wait_recv()
            fwd = copy(4 + k, (*chip, c), sibling)
            fwd.start()
            passed.append(fwd)
        copy(0, sibling, (x, y, c)).wait_recv()
        for k, chip in enumerate(chips):
            copy(4 + k, (*chip, 1 - c), (x, y, c)).wait_recv()
        for cp in half_copies(1 - c, (x, y, c)):
            cp.wait_recv()
        for cp in first + passed + to_sibling:
            cp.wait_send()
        own_small.wait()

    return pl.pallas_call(
        body, name="rs_finish", in_specs=[ANY, ANY, ANY], out_specs=[ANY, ANY, ANY], input_output_aliases={0: 0, 1: 1},
        out_shape=[jax.ShapeDtypeStruct(grads_a.shape, F32), jax.ShapeDtypeStruct(grads_b.shape, F32),
                   jax.ShapeDtypeStruct((N_DEV, SMALL_ROWS, 128), F32)],
        scratch_shapes=[pltpu.SemaphoreType.DMA((9,)), pltpu.SemaphoreType.DMA((9,)), pltpu.SemaphoreType.DMA],
    )(grads_a, grads_b, small)


def _adam_update(w, g, m, v):
    m_new = ADAM_B1 * m + (1.0 - ADAM_B1) * g
    v_new = ADAM_B2 * v + (1.0 - ADAM_B2) * (g * g)
    m_hat = m_new / (1.0 - ADAM_B1 ** ADAM_STEP)
    v_hat = v_new / (1.0 - ADAM_B2 ** ADAM_STEP)
    return -ADAM_LR * (m_hat / (jnp.sqrt(v_hat) + ADAM_EPS) + ADAM_WD * w), m_new, v_new


def _adamw(w, g, m, v, name):
    rows, cols = w.shape
    t = rows if rows <= 320 else (rows // 2 if rows % 256 else 256)

    def body(w_ref, g_ref, m_ref, v_ref, d_ref, nm_ref, nv_ref):
        d_ref[...], nm_ref[...], nv_ref[...] = _adam_update(w_ref[...], g_ref[...], m_ref[...], v_ref[...])

    blk = pl.BlockSpec((t, cols), lambda i: (i, 0))
    shape = jax.ShapeDtypeStruct((rows, cols), F32)
    return pl.pallas_call(
        body, name=name, grid=(rows // t,), in_specs=[blk] * 4, out_specs=[blk] * 3, out_shape=[shape] * 3,
        compiler_params=_params(),
    )(w, g, m, v)


SMALL_PARAMS = [("g_attn", (1, D_MODEL), 8), ("g_q", (1, HEAD_DIM), None), ("g_k", (1, HEAD_DIM), None),
                ("sinks", (1, N_Q_HEADS), None), ("rel_bias", (N_BUCKETS, N_Q_HEADS), None), ("w_pool", (512, 128), None),
                ("pool_scale", (1, POOL_WIDTH), 4), ("g_ffn", (1, D_MODEL), 8), ("g_ple", (1, D_MODEL), 8)]


def _adamw_small(tables, wmv):
    n_par = len(SMALL_PARAMS)

    def body(*refs):
        t_ref = refs[0]
        ins = refs[1:1 + 3 * n_par]
        loss_ref = refs[1 + 3 * n_par]
        outs = refs[2 + 3 * n_par:-1]
        tot_ref = refs[-1]
        total = t_ref[0]
        for d in range(1, N_DEV):
            total = total + t_ref[d]
        tot_ref[...] = total
        loss_ref[...] = tot_ref[pl.ds(SMALL["loss"], 1), 0:1]
        for i, (name, shape, split) in enumerate(SMALL_PARAMS):
            g_ref, d_ref, nm_ref, nv_ref = outs[4 * i:4 * i + 4]
            row = SMALL[name]
            if split:
                for k in range(split):
                    g_ref[:, 128 * k:128 * k + 128] = tot_ref[pl.ds(row + k, 1), :]
            else:
                g_ref[...] = tot_ref[pl.ds(row, shape[0]), 0:shape[1]]
            w_ref, m_ref, v_ref = ins[3 * i:3 * i + 3]
            d_ref[...], nm_ref[...], nv_ref[...] = _adam_update(w_ref[...], g_ref[...], m_ref[...], v_ref[...])

    shapes = [jax.ShapeDtypeStruct((1, 1), F32)]
    for _, shape, _ in SMALL_PARAMS:
        shapes += [jax.ShapeDtypeStruct(shape, F32)] * 4
    flat = [a for triple in wmv for a in triple]
    res = pl.pallas_call(
        body, name="adamw_small", in_specs=[VMEM_WHOLE] * (1 + 3 * n_par), out_specs=[VMEM_WHOLE] * len(shapes),
        out_shape=shapes, scratch_shapes=[pltpu.VMEM((SMALL_ROWS, 128), F32)],
    )(tables, *flat)
    return res[0], [res[1 + 4 * i:5 + 4 * i] for i in range(n_par)]


def _pack_ple_proj(shard):
    return shard.reshape(4, 64, 256).transpose(1, 0, 2).reshape(64, D_MODEL)


class _Reduction:
    def __init__(self, tag, place, ids=(None, None)):
        self.tag, self.place, self.ids = tag, place, ids

    def start(self, partial):
        self.partial = partial
        self.other = _rs_swap_halves(partial, "rs_swap_" + self.tag, self.ids[0])
        return partial

    def middle(self, after):
        self.pre = _rs_add_halves(self.partial, self.other, self.place[1:], "rs_add_" + self.tag, after)
        self.received = _rs_exchange_chips(self.pre, "rs_exchange_" + self.tag, self.ids[1])
        return self.pre

    def finish(self):
        return _rs_sum_chips(self.pre, self.received, self.place, "rs_sum_" + self.tag)


def _local_grads(x2, p2, tgt, wts, g_attn_norm, g_q, g_k, attn_sinks, rel_bias, w_pool, pool_scale, g_ffn_norm, g_ple_norm,
                 reduce_a):
    early, late, local_slab, me = wts
    w_early, w_late = (early, local_slab, me), (late, local_slab, me)
    bucket = jnp.asarray(_bucket_table())
    gq = jnp.tile(g_q, (1, 2))
    gk = jnp.tile(g_k, (1, 2))
    wpool = w_pool[0].astype(BF16)
    sinks = attn_sinks[0]
    bias_st = _bias_build(rel_bias.T, bucket)

    hn1, zqk, u, kn, vb, qst = _attn_in(x2, g_attn_norm, gq, gk, w_early)
    ost = _attn_fwd(qst, kn, vb, bias_st, sinks)
    pooled, mix, h1, hn2 = _mix_out(u, ost, x2, w_early, wpool, pool_scale, g_ffn_norm)
    gate, up, h2 = _ffn_fwd(hn2, h1, w_late)
    loss_v, dh2, dgl, dpp, hn3, dg_ple = _ple_loss(h2, p2, tgt, w_late, g_ple_norm)

    dgate, dup, act, dh1, dg_ffn = _ffn_bwd(dh2, gate, up, h1, w_late, g_ffn_norm)
    rows_a = SLAB_ROWS - SLAB["inT"][1]
    partial_a = None
    for name, lhs, rhs in (("out", mix, dh1), ("gateT", dgate, hn2), ("upT", dup, hn2), ("down", act, dh2), ("plg", hn3, dgl)):
        partial_a = _dw(lhs, rhs, "dw_" + name, into=(partial_a, rows_a, SLAB[name][0] - SLAB["inT"][1]))
    dw_plp = _dw(p2, dpp, "dw_plp").reshape(4, 64, N_CHIPS, 256).transpose(2, 1, 0, 3).reshape(N_CHIPS, 64, D_MODEL)
    partial_a = reduce_a.start(lax.dynamic_update_slice(partial_a, dw_plp, (0, SLAB["plp"][0] - SLAB["inT"][1], 0)))
    dost, du, dyp, dscale = _mix_out_bwd(dh1, w_early, pooled, wpool, pool_scale, partial_a)
    pre_a = reduce_a.middle(du)
    dqst, dk, dv, dbias, dsink_rows = _attn_bwd(qst, kn, vb, dost, bias_st, sinks, pre_a)
    dz, dx, dg_attn, dgq, dgk = _attn_in_bwd(dqst, zqk, dk, dv, du, x2, dh1, w_early, g_attn_norm, gq, gk)

    partial_b = _dw(dz, hn1, "dw_in").reshape(N_CHIPS, -1, D_MODEL)
    small = _small_pack(dg_attn, dg_ffn, dg_ple, dscale, dgq, dgk, dbias, dsink_rows, bucket, loss_v, _dw_pool(pooled, dyp))
    return dx, partial_b, small


def kernel(x, p, w_in, w_out, g_attn_norm, g_q, g_k, attn_sinks, rel_bias, w_pool, pool_scale, g_ffn_norm, w_gate, w_up, w_down, g_ple_norm, w_ple_gate, w_ple_proj, loss_target, m_w_in, m_w_out, m_g_attn_norm, m_g_q, m_g_k, m_attn_sinks, m_rel_bias, m_w_pool, m_pool_scale, m_g_ffn_norm, m_w_gate, m_w_up, m_w_down, m_g_ple_norm, m_w_ple_gate, m_w_ple_proj, v_w_in, v_w_out, v_g_attn_norm, v_g_q, v_g_k, v_attn_sinks, v_rel_bias, v_w_pool, v_pool_scale, v_g_ffn_norm, v_w_gate, v_w_up, v_w_down, v_g_ple_norm, v_w_ple_gate, v_w_ple_proj):
    core = lax.axis_index("c").astype(jnp.int32).reshape(1)
    me = (2 * lax.axis_index("x") + lax.axis_index("y")).astype(jnp.int32).reshape(1)

    local_slab = jnp.concatenate(
        [w_in[0].T, w_out[0], w_gate[0].T, w_up[0].T, w_down[0], w_ple_gate[0], _pack_ple_proj(w_ple_proj[0])],
        axis=0).astype(BF16)
    wts = (_ag_weights(local_slab, 0, EARLY_ROWS, "ag_early", 1),
           _ag_weights(local_slab, EARLY_ROWS, SLAB_ROWS - EARLY_ROWS, "ag_late", 2), local_slab, me)

    place = jnp.concatenate([me, core])
    reduce_a = _Reduction("a", place, ids=(3, 4))
    dx, partial_b, small = _local_grads(x[0], p[0, 0], loss_target[0], wts, g_attn_norm, g_q, g_k, attn_sinks, rel_bias,
                                        w_pool, pool_scale, g_ffn_norm, g_ple_norm, reduce_a)
    reduce_b = _Reduction("b", place)
    reduce_b.start(partial_b)
    reduce_b.middle(partial_b)
    grads_a, grads_b, small_all = _rs_finish(reduce_a.finish(), reduce_b.finish(), small)

    def rows(name):
        off, n_rows = SLAB[name]
        if name == "inT":
            return grads_b
        return grads_a[off - SLAB["inT"][1]:off - SLAB["inT"][1] + n_rows]

    big = {
        "w_in": (w_in, m_w_in, v_w_in, rows("inT"), True),
        "w_out": (w_out, m_w_out, v_w_out, rows("out"), False),
        "w_gate": (w_gate, m_w_gate, v_w_gate, rows("gateT"), True),
        "w_up": (w_up, m_w_up, v_w_up, rows("upT"), True),
        "w_down": (w_down, m_w_down, v_w_down, rows("down"), False),
        "w_ple_gate": (w_ple_gate, m_w_ple_gate, v_w_ple_gate, rows("plg"), False),
        "w_ple_proj": (w_ple_proj, m_w_ple_proj, v_w_ple_proj,
                       rows("plp").reshape(64, 4, 256).transpose(1, 0, 2).reshape(PLE_DIM, PLE_DIM), False),
    }
    small_params = {
        "g_attn_norm": (g_attn_norm, m_g_attn_norm, v_g_attn_norm), "g_q": (g_q, m_g_q, v_g_q), "g_k": (g_k, m_g_k, v_g_k),
        "attn_sinks": (attn_sinks, m_attn_sinks, v_attn_sinks), "rel_bias": (rel_bias, m_rel_bias, v_rel_bias),
        "w_pool": tuple(a.reshape(512, 128) for a in (w_pool, m_w_pool, v_w_pool)),
        "pool_scale": (pool_scale, m_pool_scale, v_pool_scale), "g_ffn_norm": (g_ffn_norm, m_g_ffn_norm, v_g_ffn_norm),
        "g_ple_norm": (g_ple_norm, m_g_ple_norm, v_g_ple_norm),
    }

    grads, deltas, new_ms, new_vs = {}, {}, {}, {}
    for name, (w, m, v, g2, transposed) in big.items():
        view = (lambda a: a.T) if transposed else (lambda a: a)
        d, nm, nv = _adamw(view(w[0]), g2, view(m[0]), view(v[0]), "adamw_" + name)
        grads[name], deltas[name], new_ms[name], new_vs[name] = (view(a)[None] for a in (g2, d, nm, nv))

    loss, small_out = _adamw_small(small_all, list(small_params.values()))
    for name, (g2, d, nm, nv) in zip(small_params, small_out):
        shape = w_pool.shape if name == "w_pool" else g2.shape
        grads[name], deltas[name], new_ms[name], new_vs[name] = (a.reshape(shape) for a in (g2, d, nm, nv))

    order = ["w_in", "w_out", "g_attn_norm", "g_q", "g_k", "attn_sinks", "rel_bias", "w_pool", "pool_scale", "g_ffn_norm",
             "w_gate", "w_up", "w_down", "g_ple_norm", "w_ple_gate", "w_ple_proj"]
    return (loss.reshape(()), dx[None], *[grads[n] for n in order], *[deltas[n] for n in order],
            *[new_ms[n] for n in order], *[new_vs[n] for n in order])
```

```python
import functools

import numpy as np
import jax
import jax.numpy as jnp
from jax import lax
from jax.experimental import pallas as pl
from jax.experimental.pallas import tpu as pltpu
from jax.experimental.pallas import tpu_sc as plsc

F32 = jnp.float32
BF16 = jnp.bfloat16
MESH = pl.DeviceIdType.MESH

D_MODEL = 1024
HEAD_DIM = 64
N_Q_HEADS = 8
ATTN_WIDTH = 512
KV_WIDTH = 128
POOL_WIDTH = 512
IN_WIDTH = 1280
D_FF = 2816
PLE_DIM = 256
FF_CHUNK = 1408
BLOCK = 128
N_BUCKETS = 32
MAX_DISTANCE = 128
POOL_SIZES = (2, 4, 8, 16)
EPS = 1e-6
NEG = -1e30
N_CHIPS = 4
N_DEV = 8

ADAM_LR = 0.001
ADAM_B1 = 0.9
ADAM_B2 = 0.999
ADAM_EPS = 1e-08
ADAM_WD = 0.01
ADAM_STEP = 10

SLAB = {"inT": (0, 320), "out": (320, 256), "gateT": (576, 704), "upT": (1280, 704), "down": (1984, 704),
        "plg": (2688, 256), "plp": (2944, 64)}
SLAB_ROWS = 3008
HALF_ROWS = SLAB_ROWS // 2
EARLY_ROWS = 576
POOL_HALO = 24

SMALL = {"g_attn": 0, "g_ffn": 8, "g_ple": 16, "pool_scale": 24, "g_q": 28, "g_k": 29, "sinks": 30, "loss": 31,
         "rel_bias": 32, "w_pool": 64}
SMALL_ROWS = 576

VMEM_LIMIT_BIG = 60 * 1024 * 1024
VMEM_LIMIT = 48 * 1024 * 1024


def _params(vmem=VMEM_LIMIT, n_axes=1):
    return pltpu.CompilerParams(dimension_semantics=("arbitrary",) * n_axes, vmem_limit_bytes=vmem)


def _dot(a, b, ca, cb):
    return lax.dot_general(a, b, (((ca,), (cb,)), ((), ())), preferred_element_type=F32)


def _full(shape):
    return pl.BlockSpec(shape, lambda i: (0,) * len(shape))


ANY = pl.BlockSpec(memory_space=pl.ANY)
VMEM_WHOLE = pl.BlockSpec(memory_space=pltpu.VMEM)


W_SPECS = [ANY, ANY, pl.BlockSpec(memory_space=pltpu.SMEM)]


def _load_rows(w_refs, name, dst_ref, sems):
    slab_ref, local_ref, me_ref = w_refs
    off, rows = SLAB[name]
    slab_off = off if off < EARLY_ROWS else off - EARLY_ROWS
    me = me_ref[0]
    for phase in ("start", "wait"):
        for j in range(N_CHIPS):
            dst = dst_ref.at[pl.ds(j * rows, rows), :]
            theirs = pltpu.make_async_copy(slab_ref.at[j, pl.ds(slab_off, rows), :], dst, sems.at[j])
            own = pltpu.make_async_copy(local_ref.at[pl.ds(off, rows), :], dst, sems.at[j])

            @pl.when(me == j)
            def _():
                getattr(own, phase)()

            @pl.when(me != j)
            def _():
                getattr(theirs, phase)()


def _rms_fwd(x, g):
    r = lax.rsqrt(jnp.mean(x * x, axis=-1, keepdims=True) + EPS)
    return x * r * g


def _rms_bwd(x, g, dy):
    r = lax.rsqrt(jnp.mean(x * x, axis=-1, keepdims=True) + EPS)
    xn = x * r
    dyg = dy * g
    dx = r * (dyg - xn * jnp.mean(dyg * xn, axis=-1, keepdims=True))
    return dx, jnp.sum(dy * xn, axis=0, keepdims=True)


def _half_sum(v, lo):
    s_lo = jnp.sum(jnp.where(lo, v, 0.0), axis=-1, keepdims=True)
    s_hi = jnp.sum(jnp.where(lo, 0.0, v), axis=-1, keepdims=True)
    return jnp.where(lo, s_lo, s_hi)


def _pair_norm(zp, g, lo):
    r = lax.rsqrt(_half_sum(zp * zp, lo) * (1.0 / HEAD_DIM) + EPS)
    return zp * r * g


def _pair_norm_bwd(zp, g, dy, lo):
    r = lax.rsqrt(_half_sum(zp * zp, lo) * (1.0 / HEAD_DIM) + EPS)
    xn = zp * r
    dyg = dy * g
    dx = r * (dyg - xn * (_half_sum(dyg * xn, lo) * (1.0 / HEAD_DIM)))
    return dx, jnp.sum(dy * xn, axis=0, keepdims=True)


def _to_stacked(pair, group, lo):
    rolled = pltpu.roll(pair, 64, axis=1)
    if group == 0:
        return jnp.where(lo, pair, 0.0), jnp.where(lo, rolled, 0.0)
    return jnp.where(lo, 0.0, rolled), jnp.where(lo, 0.0, pair)


def _from_stacked(even, odd, group, lo):
    if group == 0:
        return jnp.where(lo, even, pltpu.roll(odd, 64, axis=1))
    return jnp.where(lo, pltpu.roll(even, 64, axis=1), odd)


def _sigmoid(v):
    return 1.0 / (1.0 + jnp.exp(-v))


def _pool_counts(tile, n_rows):
    t1 = tile * n_rows + lax.broadcasted_iota(jnp.int32, (n_rows, POOL_WIDTH), 0) + 1
    lane = lax.broadcasted_iota(jnp.int32, (n_rows, POOL_WIDTH), 1)
    win = jnp.where(lane < 128, 2, jnp.where(lane < 256, 4, jnp.where(lane < 384, 8, 16)))
    return jnp.minimum(t1, win).astype(F32)


def _attn_in(x2, g_attn, gq, gk, wts):
    s_len = x2.shape[0]
    t = 512

    def body(x_ref, g_ref, gq_ref, gk_ref, sl_ref, lo_ref, me_ref, hn_ref, zqk_ref, u_ref, kn_ref, v_ref, qst_ref, w_ref, sems):
        @pl.when(pl.program_id(0) == 0)
        def _():
            _load_rows((sl_ref, lo_ref, me_ref), "inT", w_ref, sems)

        hn = _rms_fwd(x_ref[...], g_ref[...]).astype(BF16)
        hn_ref[...] = hn
        z = _dot(hn, w_ref[...], 1, 1)
        zqk_ref[...] = z[:, :640]
        u_ref[...] = z[:, 768:]
        v_ref[...] = z[:, 640:768].astype(BF16)
        lo = lax.broadcasted_iota(jnp.int32, (t, 128), 1) < 64
        kn_ref[...] = _pair_norm(z[:, 512:640], gk_ref[...], lo).astype(BF16)
        for p in range(4):
            qn = _pair_norm(z[:, 128 * p:128 * p + 128], gq_ref[...], lo)
            even, odd = _to_stacked(qn, p // 2, lo)
            qst_ref[2 * p] = even.astype(BF16)
            qst_ref[2 * p + 1] = odd.astype(BF16)

    row = lambda w: pl.BlockSpec((t, w), lambda i: (i, 0))
    return pl.pallas_call(
        body, name="attn_in", grid=(s_len // t,),
        in_specs=[row(D_MODEL), _full((1, D_MODEL)), _full((1, 128)), _full((1, 128))] + W_SPECS,
        out_specs=[row(D_MODEL), row(640), row(POOL_WIDTH), row(128), row(128),
                   pl.BlockSpec((N_Q_HEADS, t, 128), lambda i: (0, i, 0))],
        out_shape=[jax.ShapeDtypeStruct((s_len, D_MODEL), BF16), jax.ShapeDtypeStruct((s_len, 640), F32),
                   jax.ShapeDtypeStruct((s_len, POOL_WIDTH), F32), jax.ShapeDtypeStruct((s_len, 128), BF16),
                   jax.ShapeDtypeStruct((s_len, 128), BF16), jax.ShapeDtypeStruct((N_Q_HEADS, s_len, 128), BF16)],
        scratch_shapes=[pltpu.VMEM((IN_WIDTH, D_MODEL), BF16), pltpu.SemaphoreType.DMA((N_CHIPS,))],
        compiler_params=_params(),
    )(x2, g_attn, gq, gk, *wts)


def _bucket_table():
    i_idx = np.arange(BLOCK)[:, None]
    j_idx = np.arange(2 * BLOCK)[None, :]
    d = BLOCK + i_idx - j_idx
    n = np.maximum(d, 0)
    max_exact = N_BUCKETS // 2
    nf = np.maximum(n, 1).astype(np.float64)
    large = max_exact + (np.log(nf / max_exact) / np.log(MAX_DISTANCE / max_exact) * (N_BUCKETS - max_exact)).astype(np.int64)
    large = np.minimum(large, N_BUCKETS - 1)
    bucket = np.where(n < max_exact, n, large)
    return np.where((d >= 0) & (d < BLOCK), bucket, -1).astype(np.int32)


def _bias_build(rel_bias_t, bucket):
    def body(rb_ref, bucket_ref, out_ref):
        bk = bucket_ref[...]
        for h in range(N_Q_HEADS):
            acc = jnp.full((BLOCK, 2 * BLOCK), NEG, F32)
            for b in range(N_BUCKETS):
                acc = jnp.where(bk == b, rb_ref[h, b], acc)
            out_ref[0, pl.ds(h * BLOCK, BLOCK), :] = acc
            out_ref[1, pl.ds(h * BLOCK, BLOCK), :] = acc
            out_ref[1, pl.ds(h * BLOCK, BLOCK), 0:BLOCK] = jnp.full((BLOCK, BLOCK), NEG, F32)

    return pl.pallas_call(
        body, name="bias_build",
        in_specs=[pl.BlockSpec(memory_space=pltpu.SMEM), VMEM_WHOLE], out_specs=VMEM_WHOLE,
        out_shape=jax.ShapeDtypeStruct((2, N_Q_HEADS * BLOCK, 2 * BLOCK), F32),
    )(rel_bias_t, bucket)


def _head_softmax(s_ref, bias_ref, sink_ref, h):
    rows = pl.ds(pl.multiple_of(h * BLOCK, BLOCK), BLOCK)
    s = s_ref[rows, :] * (HEAD_DIM ** -0.5) + bias_ref[rows, :]
    sink = sink_ref[h]
    m = jnp.maximum(jnp.max(s, axis=-1, keepdims=True), sink)
    p = jnp.exp(s - m)
    e_sink = jnp.exp(sink - m)
    inv = 1.0 / (jnp.sum(p, axis=-1, keepdims=True) + e_sink)
    return rows, p * inv, e_sink * inv


def _attn_specs():
    prev = lambda i: (jnp.maximum(i - 1, 0), 0)
    cur = lambda i: (i, 0)
    stacked = pl.BlockSpec((N_Q_HEADS, BLOCK, 128), lambda i: (0, i, 0))
    kv = [pl.BlockSpec((BLOCK, 128), prev), pl.BlockSpec((BLOCK, 128), cur)]
    consts = [pl.BlockSpec((None, N_Q_HEADS * BLOCK, 2 * BLOCK), lambda i: (jnp.where(i == 0, 1, 0), 0, 0)),
              pl.BlockSpec(memory_space=pltpu.SMEM)]
    return stacked, kv, consts


def _head_lane_mask():
    rows = lax.broadcasted_iota(jnp.int32, (N_Q_HEADS * BLOCK, 128), 0)
    lanes = lax.broadcasted_iota(jnp.int32, (N_Q_HEADS * BLOCK, 128), 1)
    return (rows < 4 * BLOCK) == (lanes < 64)


def _attn_fwd(qst, kn, vb, bias_st, sinks):
    s_len = kn.shape[0]

    def body(q_ref, kp_ref, kc_ref, vp_ref, vc_ref, bias_ref, sink_ref, o_ref, s_ref, p_ref):
        q = q_ref[...].reshape(N_Q_HEADS * BLOCK, 128)
        s_ref[...] = _dot(q, jnp.concatenate([kp_ref[...], kc_ref[...]], axis=0), 1, 1)

        def head(h, carry):
            rows, probs, _ = _head_softmax(s_ref, bias_ref, sink_ref, h)
            p_ref[rows, :] = probs.astype(BF16)
            return carry

        lax.fori_loop(0, N_Q_HEADS, head, 0, unroll=True)
        o = _dot(p_ref[...], jnp.concatenate([vp_ref[...], vc_ref[...]], axis=0), 1, 0)
        o_ref[...] = jnp.where(_head_lane_mask(), o, 0.0).astype(BF16).reshape(N_Q_HEADS, BLOCK, 128)

    stacked, kv, consts = _attn_specs()
    return pl.pallas_call(
        body, name="attn_fwd", grid=(s_len // BLOCK,),
        in_specs=[stacked] + kv + kv + consts, out_specs=stacked,
        out_shape=jax.ShapeDtypeStruct((N_Q_HEADS, s_len, 128), BF16),
        scratch_shapes=[pltpu.VMEM((N_Q_HEADS * BLOCK, 2 * BLOCK), F32), pltpu.VMEM((N_Q_HEADS * BLOCK, 2 * BLOCK), BF16)],
        compiler_params=_params(),
    )(qst, kn, kn, vb, vb, bias_st, sinks)


def _mix_out(u, ost, x2, wts, wpool, pool_scale, g_ffn):
    s_len = x2.shape[0]
    t = 512
    n = t + 16

    def body(u_ref, o_ref, x_ref, sl_ref, lo_ref, me_ref, wp_ref, sc_ref, g_ref, pooled_ref, mix_ref, h1_ref, hn_ref,
             w_ref, ext_ref, st_ref, sems):
        i = pl.program_id(0)

        @pl.when(i == 0)
        def _():
            _load_rows((sl_ref, lo_ref, me_ref), "out", w_ref, sems)
            ext_ref[...] = jnp.zeros_like(ext_ref)
            st_ref[...] = jnp.zeros_like(st_ref)

        u_tile = u_ref[...]
        ext_ref[pl.ds(POOL_HALO, t), :] = u_tile
        st_ref[pl.ds(8, n), :] = ext_ref[pl.ds(8, n), :] + ext_ref[pl.ds(7, n), :]
        st_ref[pl.ds(8, n), 128:] = st_ref[pl.ds(8, n), 128:] + st_ref[pl.ds(6, n), 128:]
        st_ref[pl.ds(8, n), 256:] = st_ref[pl.ds(8, n), 256:] + st_ref[pl.ds(4, n), 256:]
        st_ref[pl.ds(8, n), 384:] = st_ref[pl.ds(8, n), 384:] + st_ref[pl.ds(0, n), 384:]
        ext_ref[pl.ds(0, POOL_HALO), :] = ext_ref[pl.ds(t, POOL_HALO), :]
        pooled = (st_ref[pl.ds(POOL_HALO, t), :] / _pool_counts(i, t) - u_tile).astype(BF16)
        pooled_ref[...] = pooled
        for g in range(4):
            cols = slice(128 * g, 128 * g + 128)
            y = _dot(pooled[:, cols], wp_ref[g], 1, 0) * sc_ref[:, cols]
            mix_ref[:, ATTN_WIDTH + 128 * g:ATTN_WIDTH + 128 * g + 128] = y.astype(BF16)
        lo = lax.broadcasted_iota(jnp.int32, (t, 128), 1) < 64
        for p in range(4):
            a = _from_stacked(o_ref[2 * p].astype(F32), o_ref[2 * p + 1].astype(F32), p // 2, lo)
            mix_ref[:, 128 * p:128 * p + 128] = a.astype(BF16)
        h1 = x_ref[...] + _dot(mix_ref[...], w_ref[...], 1, 0)
        h1_ref[...] = h1
        hn_ref[...] = _rms_fwd(h1, g_ref[...]).astype(BF16)

    row = lambda w: pl.BlockSpec((t, w), lambda i: (i, 0))
    return pl.pallas_call(
        body, name="mix_out", grid=(s_len // t,),
        in_specs=[row(POOL_WIDTH), pl.BlockSpec((N_Q_HEADS, t, 128), lambda i: (0, i, 0)), row(D_MODEL)] + W_SPECS
        + [_full((4, 128, 128)), _full((1, POOL_WIDTH)), _full((1, D_MODEL))],
        out_specs=[row(POOL_WIDTH), row(D_MODEL), row(D_MODEL), row(D_MODEL)],
        out_shape=[jax.ShapeDtypeStruct((s_len, POOL_WIDTH), BF16), jax.ShapeDtypeStruct((s_len, D_MODEL), BF16),
                   jax.ShapeDtypeStruct((s_len, D_MODEL), F32), jax.ShapeDtypeStruct((s_len, D_MODEL), BF16)],
        scratch_shapes=[pltpu.VMEM((D_MODEL, D_MODEL), BF16), pltpu.VMEM((t + POOL_HALO, POOL_WIDTH), F32),
                        pltpu.VMEM((t + POOL_HALO, POOL_WIDTH), F32), pltpu.SemaphoreType.DMA((N_CHIPS,))],
        compiler_params=_params(),
    )(u, ost, x2, *wts, wpool, pool_scale, g_ffn)


def _ffn_fwd(hn2, h1, wts):
    s_len = h1.shape[0]
    t = 256

    def body(hn_ref, h1_ref, sl_ref, lo_ref, me_ref, gate_ref, up_ref, h2_ref, wg_ref, wu_ref, wd_ref, sems):
        @pl.when(pl.program_id(0) == 0)
        def _():
            w_refs = (sl_ref, lo_ref, me_ref)
            _load_rows(w_refs, "gateT", wg_ref, sems)
            _load_rows(w_refs, "upT", wu_ref, sems)
            _load_rows(w_refs, "down", wd_ref, sems)

        hn = hn_ref[...]
        h2 = h1_ref[...]
        for ch in range(D_FF // FF_CHUNK):
            rows = pl.ds(ch * FF_CHUNK, FF_CHUNK)
            cols = slice(ch * FF_CHUNK, (ch + 1) * FF_CHUNK)
            gate = _dot(hn, wg_ref[rows, :], 1, 1)
            up = _dot(hn, wu_ref[rows, :], 1, 1)
            gate_ref[:, cols] = gate
            up_ref[:, cols] = up
            act = (gate * _sigmoid(gate) * up).astype(BF16)
            h2 = h2 + _dot(act, wd_ref[rows, :], 1, 0)
        h2_ref[...] = h2

    row = lambda w: pl.BlockSpec((t, w), lambda i: (i, 0))
    return pl.pallas_call(
        body, name="ffn_fwd", grid=(s_len // t,),
        in_specs=[row(D_MODEL), row(D_MODEL)] + W_SPECS,
        out_specs=[row(D_FF), row(D_FF), row(D_MODEL)],
        out_shape=[jax.ShapeDtypeStruct((s_len, D_FF), F32), jax.ShapeDtypeStruct((s_len, D_FF), F32),
                   jax.ShapeDtypeStruct((s_len, D_MODEL), F32)],
        scratch_shapes=[pltpu.VMEM((D_FF, D_MODEL), BF16)] * 3 + [pltpu.SemaphoreType.DMA((N_CHIPS,))],
        compiler_params=_params(VMEM_LIMIT_BIG),
    )(hn2, h1, *wts)


def _ple_loss(h2, p2, tgt, wts, g_ple):
    s_len = h2.shape[0]
    t = 512
    n_tiles = s_len // t

    def body(h2_ref, p_ref, tgt_ref, sl_ref, lo_ref, me_ref, g_ref, loss_ref, dh2_ref, dgl_ref, dpp_ref, hn_ref,
             dg_ref, w_ref, wp_ref, packed_ref, loss_acc, sems):
        i = pl.program_id(0)

        @pl.when(i == 0)
        def _():
            w_refs = (sl_ref, lo_ref, me_ref)
            _load_rows(w_refs, "plg", w_ref, sems)
            _load_rows(w_refs, "plp", packed_ref, sems)
            for j in range(N_CHIPS):
                for q in range(4):
                    wp_ref[pl.ds(64 * q, 64), 256 * j:256 * j + 256] = packed_ref[pl.ds(64 * j, 64), 256 * q:256 * q + 256]
            loss_acc[...] = jnp.zeros_like(loss_acc)
            dg_ref[...] = jnp.zeros_like(dg_ref)

        h2v = h2_ref[...]
        g = g_ref[...]
        hn = _rms_fwd(h2v, g).astype(BF16)
        hn_ref[...] = hn
        gate = _sigmoid(_dot(hn, w_ref[...], 1, 0))
        pp = _dot(p_ref[...].astype(BF16), wp_ref[...], 1, 0)
        err = h2v + gate * pp - tgt_ref[...]
        loss_acc[...] += jnp.sum(err * err, axis=0, keepdims=True)
        dy = err * (1.0 / D_MODEL)
        dpp_ref[...] = (dy * gate).astype(BF16)
        dgl = (dy * pp * gate * (1.0 - gate)).astype(BF16)
        dgl_ref[...] = dgl
        dx, dg = _rms_bwd(h2v, g, _dot(dgl, w_ref[...], 1, 1))
        dh2_ref[...] = dy + dx
        dg_ref[...] += dg

        @pl.when(i == n_tiles - 1)
        def _():
            total = jnp.sum(loss_acc[...], axis=-1, keepdims=True) * (0.5 / D_MODEL)
            loss_ref[...] = jnp.broadcast_to(total, loss_ref.shape)

    row = lambda w: pl.BlockSpec((t, w), lambda i: (i, 0))
    return pl.pallas_call(
        body, name="ple_loss", grid=(n_tiles,),
        in_specs=[row(D_MODEL), row(PLE_DIM), row(D_MODEL)] + W_SPECS + [_full((1, D_MODEL))],
        out_specs=[_full((1, 128)), row(D_MODEL), row(D_MODEL), row(D_MODEL), row(D_MODEL), _full((1, D_MODEL))],
        out_shape=[jax.ShapeDtypeStruct((1, 128), F32), jax.ShapeDtypeStruct((s_len, D_MODEL), F32),
                   jax.ShapeDtypeStruct((s_len, D_MODEL), BF16), jax.ShapeDtypeStruct((s_len, D_MODEL), BF16),
                   jax.ShapeDtypeStruct((s_len, D_MODEL), BF16), jax.ShapeDtypeStruct((1, D_MODEL), F32)],
        scratch_shapes=[pltpu.VMEM((D_MODEL, D_MODEL), BF16), pltpu.VMEM((PLE_DIM, D_MODEL), BF16),
                        pltpu.VMEM((PLE_DIM, D_MODEL), BF16), pltpu.VMEM((1, D_MODEL), F32),
                        pltpu.SemaphoreType.DMA((N_CHIPS,))],
        compiler_params=_params(),
    )(h2, p2, tgt, *wts, g_ple)


def _ffn_bwd(dh2, gate, up, h1, wts, g_ffn):
    s_len = h1.shape[0]
    t = 256

    def body(dh2_ref, gate_ref, up_ref, h1_ref, sl_ref, lo_ref, me_ref, g_ref, dgate_ref, dup_ref, act_ref, dh1_ref, dg_ref,
             wg_ref, wu_ref, wd_ref, sems):
        @pl.when(pl.program_id(0) == 0)
        def _():
            w_refs = (sl_ref, lo_ref, me_ref)
            _load_rows(w_refs, "gateT", wg_ref, sems)
            _load_rows(w_refs, "upT", wu_ref, sems)
            _load_rows(w_refs, "down", wd_ref, sems)
            dg_ref[...] = jnp.zeros_like(dg_ref)

        dh2v = dh2_ref[...]
        dh2b = dh2v.astype(BF16)
        dhn = jnp.zeros((t, D_MODEL), F32)
        for ch in range(D_FF // FF_CHUNK):
            rows = pl.ds(ch * FF_CHUNK, FF_CHUNK)
            cols = slice(ch * FF_CHUNK, (ch + 1) * FF_CHUNK)
            dact = _dot(dh2b, wd_ref[rows, :], 1, 1)
            gate_v = gate_ref[:, cols]
            up_v = up_ref[:, cols]
            sg = _sigmoid(gate_v)
            silu = gate_v * sg
            act_ref[:, cols] = (silu * up_v).astype(BF16)
            dup = (dact * silu).astype(BF16)
            dgate = (dact * up_v * (sg * (1.0 + gate_v * (1.0 - sg)))).astype(BF16)
            dup_ref[:, cols] = dup
            dgate_ref[:, cols] = dgate
            dhn = dhn + _dot(dgate, wg_ref[rows, :], 1, 0) + _dot(dup, wu_ref[rows, :], 1, 0)
        dx, dg = _rms_bwd(h1_ref[...], g_ref[...], dhn)
        dh1_ref[...] = dh2v + dx
        dg_ref[...] += dg

    row = lambda w: pl.BlockSpec((t, w), lambda i: (i, 0))
    return pl.pallas_call(
        body, name="ffn_bwd", grid=(s_len // t,),
        in_specs=[row(D_MODEL), row(D_FF), row(D_FF), row(D_MODEL)] + W_SPECS + [_full((1, D_MODEL))],
        out_specs=[row(D_FF), row(D_FF), row(D_FF), row(D_MODEL), _full((1, D_MODEL))],
        out_shape=[jax.ShapeDtypeStruct((s_len, D_FF), BF16), jax.ShapeDtypeStruct((s_len, D_FF), BF16),
                   jax.ShapeDtypeStruct((s_len, D_FF), BF16), jax.ShapeDtypeStruct((s_len, D_MODEL), F32),
                   jax.ShapeDtypeStruct((1, D_MODEL), F32)],
        scratch_shapes=[pltpu.VMEM((D_FF, D_MODEL), BF16)] * 3 + [pltpu.SemaphoreType.DMA((N_CHIPS,))],
        compiler_params=_params(VMEM_LIMIT_BIG),
    )(dh2, gate, up, h1, *wts, g_ffn)


def _mix_out_bwd(dh1, wts, pooled, wpool, pool_scale, after):
    s_len = dh1.shape[0]
    t = 512
    n = t + 16
    n_tiles = s_len // t

    def body(dh1_ref, sl_ref, lo_ref, me_ref, pooled_ref, wp_ref, sc_ref, after_ref, dost_ref, du_ref, dyp_ref, dsc_ref,
             w_ref, ext_ref, st_ref, sems):
        del after_ref
        i = pl.program_id(0)

        @pl.when(i == 0)
        def _():
            _load_rows((sl_ref, lo_ref, me_ref), "out", w_ref, sems)
            ext_ref[...] = jnp.zeros_like(ext_ref)
            st_ref[...] = jnp.zeros_like(st_ref)
            dsc_ref[...] = jnp.zeros_like(dsc_ref)

        dmix = _dot(dh1_ref[...].astype(BF16), w_ref[...], 1, 1)
        lo = lax.broadcasted_iota(jnp.int32, (t, 128), 1) < 64
        for p in range(4):
            even, odd = _to_stacked(dmix[:, 128 * p:128 * p + 128], p // 2, lo)
            dost_ref[2 * p] = even.astype(BF16)
            dost_ref[2 * p + 1] = odd.astype(BF16)
        pooled_v = pooled_ref[...]
        counts = _pool_counts(n_tiles - 1 - i, t)
        for g in range(4):
            cols = slice(128 * g, 128 * g + 128)
            dm = dmix[:, ATTN_WIDTH + 128 * g:ATTN_WIDTH + 128 * g + 128]
            ypre = _dot(pooled_v[:, cols], wp_ref[g], 1, 0)
            dsc_ref[:, cols] += jnp.sum(ypre * dm, axis=0, keepdims=True)
            dyp = (dm * sc_ref[:, cols]).astype(BF16)
            dyp_ref[:, cols] = dyp
            dpooled = _dot(dyp, wp_ref[g], 1, 1)
            du_ref[:, cols] = -dpooled
            ext_ref[pl.ds(0, t), cols] = dpooled / counts[:, cols]
        st_ref[pl.ds(0, n), :] = ext_ref[pl.ds(0, n), :] + ext_ref[pl.ds(1, n), :]
        st_ref[pl.ds(0, n), 128:] = st_ref[pl.ds(0, n), 128:] + st_ref[pl.ds(2, n), 128:]
        st_ref[pl.ds(0, n), 256:] = st_ref[pl.ds(0, n), 256:] + st_ref[pl.ds(4, n), 256:]
        st_ref[pl.ds(0, n), 384:] = st_ref[pl.ds(0, n), 384:] + st_ref[pl.ds(8, n), 384:]
        ext_ref[pl.ds(t, POOL_HALO), :] = ext_ref[pl.ds(0, POOL_HALO), :]
        du_ref[...] += st_ref[pl.ds(0, t), :]

    rev = lambda w: pl.BlockSpec((t, w), lambda i: (n_tiles - 1 - i, 0))
    return pl.pallas_call(
        body, name="mix_out_bwd", grid=(n_tiles,),
        in_specs=[rev(D_MODEL)] + W_SPECS + [rev(POOL_WIDTH), _full((4, 128, 128)), _full((1, POOL_WIDTH)), ANY],
        out_specs=[pl.BlockSpec((N_Q_HEADS, t, 128), lambda i: (0, n_tiles - 1 - i, 0)), rev(POOL_WIDTH), rev(POOL_WIDTH),
                   _full((1, POOL_WIDTH))],
        out_shape=[jax.ShapeDtypeStruct((N_Q_HEADS, s_len, 128), BF16), jax.ShapeDtypeStruct((s_len, POOL_WIDTH), F32),
                   jax.ShapeDtypeStruct((s_len, POOL_WIDTH), BF16), jax.ShapeDtypeStruct((1, POOL_WIDTH), F32)],
        scratch_shapes=[pltpu.VMEM((D_MODEL, D_MODEL), BF16), pltpu.VMEM((t + POOL_HALO, POOL_WIDTH), F32),
                        pltpu.VMEM((t + POOL_HALO, POOL_WIDTH), F32), pltpu.SemaphoreType.DMA((N_CHIPS,))],
        compiler_params=_params(),
    )(dh1, *wts, pooled, wpool, pool_scale, after)


def _attn_bwd(qst, kn, vb, dost, bias_st, sinks, after):
    s_len = kn.shape[0]

    def body(q_ref, kp_ref, kc_ref, vp_ref, vc_ref, do_ref, bias_ref, sink_ref, after_ref, dq_ref, dk_ref, dv_ref, dbias_ref,
             dsink_ref, s_ref, dp_ref, p_ref, dl_ref):
        del after_ref
        i = pl.program_id(0)

        @pl.when(i == 0)
        def _():
            dk_ref[...] = jnp.zeros_like(dk_ref)
            dv_ref[...] = jnp.zeros_like(dv_ref)
            dbias_ref[...] = jnp.zeros_like(dbias_ref)
            dsink_ref[...] = jnp.zeros_like(dsink_ref)

        q = q_ref[...].reshape(N_Q_HEADS * BLOCK, 128)
        do = do_ref[...].reshape(N_Q_HEADS * BLOCK, 128)
        k2 = jnp.concatenate([kp_ref[...], kc_ref[...]], axis=0)
        s_ref[...] = _dot(q, k2, 1, 1)
        dp_ref[...] = _dot(do, jnp.concatenate([vp_ref[...], vc_ref[...]], axis=0), 1, 1)

        def head(h, carry):
            rows, probs, p_sink = _head_softmax(s_ref, bias_ref, sink_ref, h)
            dp = dp_ref[rows, :]
            dsum = jnp.sum(probs * dp, axis=-1, keepdims=True)
            dlog = probs * (dp - dsum)
            dsink_ref[rows, :] -= p_sink * dsum
            dbias_ref[rows, :] += dlog
            p_ref[rows, :] = probs.astype(BF16)
            dl_ref[rows, :] = (dlog * (HEAD_DIM ** -0.5)).astype(BF16)
            return carry

        lax.fori_loop(0, N_Q_HEADS, head, 0, unroll=True)
        dlog_s = dl_ref[...]
        dq_ref[...] = jnp.where(_head_lane_mask(), _dot(dlog_s, k2, 1, 0), 0.0).reshape(N_Q_HEADS, BLOCK, 128)
        dk2 = _dot(dlog_s, q, 0, 0)
        dv2 = _dot(p_ref[...], do, 0, 0)
        prev_rows = pl.ds(pl.multiple_of(jnp.maximum(i - 1, 0) * BLOCK, BLOCK), BLOCK)
        cur_rows = pl.ds(pl.multiple_of(i * BLOCK, BLOCK), BLOCK)
        dk_ref[prev_rows, :] += dk2[:BLOCK]
        dk_ref[cur_rows, :] += dk2[BLOCK:]
        dv_ref[prev_rows, :] += dv2[:BLOCK]
        dv_ref[cur_rows, :] += dv2[BLOCK:]

    stacked, kv, consts = _attn_specs()
    band = (N_Q_HEADS * BLOCK, 2 * BLOCK)
    return pl.pallas_call(
        body, name="attn_bwd", grid=(s_len // BLOCK,),
        in_specs=[stacked] + kv + kv + [stacked] + consts + [ANY],
        out_specs=[stacked, _full((s_len, 128)), _full((s_len, 128)), _full(band), _full((N_Q_HEADS * BLOCK, 1))],
        out_shape=[jax.ShapeDtypeStruct((N_Q_HEADS, s_len, 128), F32), jax.ShapeDtypeStruct((s_len, 128), F32),
                   jax.ShapeDtypeStruct((s_len, 128), F32), jax.ShapeDtypeStruct(band, F32),
                   jax.ShapeDtypeStruct((N_Q_HEADS * BLOCK, 1), F32)],
        scratch_shapes=[pltpu.VMEM(band, F32), pltpu.VMEM(band, F32), pltpu.VMEM(band, BF16), pltpu.VMEM(band, BF16)],
        compiler_params=_params(),
    )(qst, kn, kn, vb, vb, dost, bias_st, sinks, after)


def _small_pack(dg_attn, dg_ffn, dg_ple, dscale, dgq, dgk, dbias, dsink_rows, bucket, loss_v, dwpool):
    def body(ga_ref, gf_ref, gp_ref, sc_ref, gq_ref, gk_ref, db_ref, ds_ref, bucket_ref, loss_ref, wp_ref, out_ref):
        out_ref[pl.ds(0, SMALL["w_pool"]), :] = jnp.zeros((SMALL["w_pool"], 128), F32)
        for name, ref, n in (("g_attn", ga_ref, 8), ("g_ffn", gf_ref, 8), ("g_ple", gp_ref, 8), ("pool_scale", sc_ref, 4)):
            for k in range(n):
                out_ref[pl.ds(SMALL[name] + k, 1), :] = ref[:, 128 * k:128 * k + 128]
        for name, ref in (("g_q", gq_ref), ("g_k", gk_ref)):
            both = ref[...]
            out_ref[pl.ds(SMALL[name], 1), :] = both + pltpu.roll(both, 64, axis=1)
        out_ref[pl.ds(SMALL["loss"], 1), :] = loss_ref[...]
        bk = bucket_ref[...]
        rows = lax.broadcasted_iota(jnp.int32, (N_BUCKETS, 128), 0)
        lanes = lax.broadcasted_iota(jnp.int32, (N_BUCKETS, 128), 1)
        lane1 = lax.broadcasted_iota(jnp.int32, (1, 128), 1)
        rb = jnp.zeros((N_BUCKETS, 128), F32)
        sk = jnp.zeros((1, 128), F32)
        for h in range(N_Q_HEADS):
            band = db_ref[pl.ds(h * BLOCK, BLOCK), :]
            for b in range(N_BUCKETS):
                rb = jnp.where((rows == b) & (lanes == h), jnp.sum(jnp.where(bk == b, band, 0.0)), rb)
            sk = jnp.where(lane1 == h, jnp.sum(ds_ref[pl.ds(h * BLOCK, BLOCK), :]), sk)
        out_ref[pl.ds(SMALL["rel_bias"], N_BUCKETS), :] = rb
        out_ref[pl.ds(SMALL["sinks"], 1), :] = sk
        out_ref[pl.ds(SMALL["w_pool"], 512), :] = wp_ref[...].reshape(512, 128)

    return pl.pallas_call(
        body, name="small_pack", in_specs=[VMEM_WHOLE] * 11, out_specs=VMEM_WHOLE,
        out_shape=jax.ShapeDtypeStruct((SMALL_ROWS, 128), F32),
    )(dg_attn, dg_ffn, dg_ple, dscale, dgq, dgk, dbias, dsink_rows, bucket, loss_v, dwpool)


def _attn_in_bwd(dqst, zqk, dk, dv, du, x2, dh1, wts, g_attn, gq, gk):
    s_len = x2.shape[0]
    t = 512

    def body(dq_ref, zqk_ref, dk_ref, dv_ref, du_ref, x_ref, dh1_ref, sl_ref, lo_ref, me_ref, g_ref, gq_ref, gk_ref,
             dz_ref, dx_ref, dg_ref, dgq_ref, dgk_ref, w_ref, sems):
        @pl.when(pl.program_id(0) == 0)
        def _():
            _load_rows((sl_ref, lo_ref, me_ref), "inT", w_ref, sems)
            dg_ref[...] = jnp.zeros_like(dg_ref)
            dgq_ref[...] = jnp.zeros_like(dgq_ref)
            dgk_ref[...] = jnp.zeros_like(dgk_ref)

        lo = lax.broadcasted_iota(jnp.int32, (t, 128), 1) < 64
        for p in range(4):
            dqn = _from_stacked(dq_ref[2 * p], dq_ref[2 * p + 1], p // 2, lo)
            dq_raw, dgq = _pair_norm_bwd(zqk_ref[:, 128 * p:128 * p + 128], gq_ref[...], dqn, lo)
            dz_ref[:, 128 * p:128 * p + 128] = dq_raw.astype(BF16)
            dgq_ref[...] += dgq
        dk_raw, dgk = _pair_norm_bwd(zqk_ref[:, 512:640], gk_ref[...], dk_ref[...], lo)
        dgk_ref[...] += dgk
        dz_ref[:, 512:640] = dk_raw.astype(BF16)
        dz_ref[:, 640:768] = dv_ref[...].astype(BF16)
        dz_ref[:, 768:] = du_ref[...].astype(BF16)
        dx, dg = _rms_bwd(x_ref[...], g_ref[...], _dot(dz_ref[...], w_ref[...], 1, 0))
        dx_ref[...] = dh1_ref[...] + dx
        dg_ref[...] += dg

    row = lambda w: pl.BlockSpec((t, w), lambda i: (i, 0))
    return pl.pallas_call(
        body, name="attn_in_bwd", grid=(s_len // t,),
        in_specs=[pl.BlockSpec((N_Q_HEADS, t, 128), lambda i: (0, i, 0)), row(640), row(128), row(128), row(POOL_WIDTH),
                  row(D_MODEL), row(D_MODEL)] + W_SPECS + [_full((1, D_MODEL)), _full((1, 128)), _full((1, 128))],
        out_specs=[row(IN_WIDTH), row(D_MODEL), _full((1, D_MODEL)), _full((1, 128)), _full((1, 128))],
        out_shape=[jax.ShapeDtypeStruct((s_len, IN_WIDTH), BF16), jax.ShapeDtypeStruct((s_len, D_MODEL), F32),
                   jax.ShapeDtypeStruct((1, D_MODEL), F32), jax.ShapeDtypeStruct((1, 128), F32),
                   jax.ShapeDtypeStruct((1, 128), F32)],
        scratch_shapes=[pltpu.VMEM((IN_WIDTH, D_MODEL), BF16), pltpu.SemaphoreType.DMA((N_CHIPS,))],
        compiler_params=_params(),
    )(dqst, zqk, dk, dv, du, x2, dh1, *wts, g_attn, gq, gk)


def _dw(a, b, name, into=None):
    s_len, m = a.shape
    n_out = b.shape[1]
    tk = 1024
    n_steps = s_len // tk
    tm = m // 2 if m > 1408 else m
    chunk = m // N_CHIPS
    per_tile = tm // chunk

    def accumulate(a_ref, b_ref, acc_ref, k):
        @pl.when(k == 0)
        def _():
            acc_ref[...] = _dot(a_ref[...].astype(BF16), b_ref[...].astype(BF16), 0, 0)

        @pl.when(k > 0)
        def _():
            acc_ref[...] += _dot(a_ref[...].astype(BF16), b_ref[...].astype(BF16), 0, 0)

    in_specs = [pl.BlockSpec((tk, tm), lambda i, k: (k, i)), pl.BlockSpec((tk, n_out), lambda i, k: (k, 0))]
    if into is None:
        def body(a_ref, b_ref, o_ref, acc_ref):
            k = pl.program_id(1)
            accumulate(a_ref, b_ref, acc_ref, k)

            @pl.when(k == n_steps - 1)
            def _():
                o_ref[...] = acc_ref[...].astype(BF16)

        return pl.pallas_call(
            body, name=name, grid=(m // tm, n_steps), in_specs=in_specs,
            out_specs=pl.BlockSpec((tm, n_out), lambda i, k: (i, 0)), out_shape=jax.ShapeDtypeStruct((m, n_out), BF16),
            scratch_shapes=[pltpu.VMEM((tm, n_out), F32)], compiler_params=_params(n_axes=2),
        )(a, b)

    slab, slab_rows, row_off = into
    assert n_out == D_MODEL

    def body_into(a_ref, b_ref, *rest):
        o_ref, acc_ref, stage_ref, sems = rest[-4:]
        i, k = pl.program_id(0), pl.program_id(1)
        accumulate(a_ref, b_ref, acc_ref, k)

        @pl.when(k == n_steps - 1)
        def _():
            stage_ref[...] = acc_ref[...].astype(BF16)
            copies = [pltpu.make_async_copy(stage_ref.at[pl.ds(jj * chunk, chunk), :],
                                            o_ref.at[i * per_tile + jj, pl.ds(row_off, chunk), :], sems.at[jj])
                      for jj in range(per_tile)]
            for cp in copies:
                cp.start()
            for cp in copies:
                cp.wait()

    operands, aliases = [a, b], {}
    if slab is not None:
        in_specs = in_specs + [ANY]
        operands.append(slab)
        aliases = {2: 0}
    return pl.pallas_call(
        body_into, name=name, grid=(m // tm, n_steps), in_specs=in_specs, out_specs=ANY,
        out_shape=jax.ShapeDtypeStruct((N_CHIPS, slab_rows, D_MODEL), BF16), input_output_aliases=aliases,
        scratch_shapes=[pltpu.VMEM((tm, n_out), F32), pltpu.VMEM((tm, n_out), BF16), pltpu.SemaphoreType.DMA((per_tile,))],
        compiler_params=_params(n_axes=2),
    )(*operands)


def _dw_pool(pooled, dyp):
    s_len = pooled.shape[0]
    tk = 512

    def body(a_ref, b_ref, o_ref):
        @pl.when(pl.program_id(0) == 0)
        def _():
            o_ref[...] = jnp.zeros_like(o_ref)

        for g in range(4):
            cols = slice(128 * g, 128 * g + 128)
            o_ref[g] += _dot(a_ref[:, cols], b_ref[:, cols], 0, 0)

    blk = pl.BlockSpec((tk, POOL_WIDTH), lambda k: (k, 0))
    return pl.pallas_call(
        body, name="dw_pool", grid=(s_len // tk,), in_specs=[blk, blk], out_specs=_full((4, 128, 128)),
        out_shape=jax.ShapeDtypeStruct((4, 128, 128), F32), compiler_params=_params(),
    )(pooled, dyp)


def _position():
    x, y, c = lax.axis_index("x"), lax.axis_index("y"), lax.axis_index("c")
    other_chips = [(1 - x, y), (x, 1 - y), (1 - x, 1 - y)]
    return x, y, c, other_chips


def _ag_weights(local_slab, row0, n_rows, name, collective_id):
    half = n_rows // 2
    quarter = half // 2
    assert quarter % 16 == 0

    def body(l_ref, g_ref, send, recv):
        x, y, c, chips = _position()
        me, (via_x, via_y, diagonal) = 2 * x + y, [2 * chip[0] + chip[1] for chip in chips]
        here, sibling, x_nbr, y_nbr = (x, y, c), (x, y, 1 - c), (1 - x, y, c), (x, 1 - y, c)
        peers = [sibling, x_nbr, y_nbr]
        barrier = pltpu.get_barrier_semaphore()
        for peer in peers:
            pl.semaphore_signal(barrier, inc=1, device_id=peer, device_id_type=MESH)
        pl.semaphore_wait(barrier, len(peers))

        def rows(core, part):
            start, size = (core * half, half) if part is None else (core * half + part * quarter, quarter)
            return pl.ds(pl.multiple_of(start, 16), size)

        def copy(k, chip_idx, where, to, src=None):
            dst = g_ref.at[chip_idx, where, :]
            return pltpu.make_async_remote_copy(src_ref=dst if src is None else src, dst_ref=dst, send_sem=send.at[k],
                                                recv_sem=recv.at[k], device_id=to, device_id_type=MESH)

        own_rows = l_ref.at[pl.ds(pl.multiple_of(row0 + c * half, 16), half), :]
        started = [copy(0, me, rows(c, None), x_nbr, src=own_rows), copy(1, me, rows(c, None), y_nbr, src=own_rows)]
        for cp in started:
            cp.start()
        after_arrival = [
            (copy(0, via_x, rows(c, None), here), [copy(4, via_x, rows(c, None), sibling), copy(3, via_x, rows(c, 1), y_nbr)]),
            (copy(1, via_y, rows(c, None), here), [copy(5, via_y, rows(c, None), sibling), copy(2, via_y, rows(c, 0), x_nbr)]),
            (copy(2, diagonal, rows(c, 0), here), [copy(6, diagonal, rows(c, 0), sibling)]),
            (copy(3, diagonal, rows(c, 1), here), [copy(7, diagonal, rows(c, 1), sibling)]),
        ]
        for arrival, onward in after_arrival:
            arrival.wait_recv()
            for cp in onward:
                cp.start()
            started += onward
        for cp in (copy(4, via_x, rows(1 - c, None), here), copy(5, via_y, rows(1 - c, None), here),
                   copy(6, diagonal, rows(1 - c, 0), here), copy(7, diagonal, rows(1 - c, 1), here)):
            cp.wait_recv()
        for cp in started:
            cp.wait_send()

    return pl.kernel(
        body, out_type=jax.ShapeDtypeStruct((N_CHIPS, n_rows, D_MODEL), BF16),
        mesh=plsc.ScalarSubcoreMesh(axis_name="sequencer", num_cores=1), name=name,
        scratch_types=[pltpu.SemaphoreType.DMA((8,)), pltpu.SemaphoreType.DMA((8,))],
        compiler_params=pltpu.CompilerParams(collective_id=collective_id),
    )(local_slab)


def _comm_call(body, peers_of, out_shape, n_sems, operand, name, collective_id):
    sems = [pltpu.SemaphoreType.DMA((n_sems,)), pltpu.SemaphoreType.DMA((n_sems,))]
    if collective_id is None:
        return pl.pallas_call(body, name=name, in_specs=[ANY], out_specs=ANY, out_shape=out_shape, scratch_shapes=sems)(operand)

    def with_handshake(in_ref, out_ref, send, recv):
        x, y, c, _ = _position()
        peers = peers_of(x, y, c)
        barrier = pltpu.get_barrier_semaphore()
        for peer in peers:
            pl.semaphore_signal(barrier, inc=1, device_id=peer, device_id_type=MESH)
        pl.semaphore_wait(barrier, len(peers))
        body(in_ref, out_ref, send, recv)

    return pl.kernel(with_handshake, out_type=out_shape, mesh=plsc.ScalarSubcoreMesh(axis_name="sequencer", num_cores=1),
                     name=name, scratch_types=sems, compiler_params=pltpu.CompilerParams(collective_id=collective_id))(operand)


def _rs_swap_halves(partial, name, collective_id=None):
    half = partial.shape[1] // 2

    def body(p_ref, r_ref, send, recv):
        x, y, c, _ = _position()
        theirs = pl.ds(pl.multiple_of((1 - c) * half, 16), half)
        cp = pltpu.make_async_remote_copy(src_ref=p_ref.at[:, theirs, :], dst_ref=r_ref, send_sem=send.at[0],
                                          recv_sem=recv.at[0], device_id=(x, y, 1 - c), device_id_type=MESH)
        cp.start()
        cp.wait()

    return _comm_call(body, lambda x, y, c: [(x, y, 1 - c)], jax.ShapeDtypeStruct((N_CHIPS, half, D_MODEL), BF16), 1,
                      partial, name, collective_id)


def _rs_add_halves(partial, other, core, name, after):
    half = other.shape[1]
    t = half // 2
    steps = half // t

    def body(core_ref, a_ref, b_ref, after_ref, o_ref):
        del after_ref
        o_ref[...] = (a_ref[...].astype(F32) + b_ref[...].astype(F32)).astype(BF16)

    return pl.pallas_call(
        body, name=name,
        grid_spec=pltpu.PrefetchScalarGridSpec(
            num_scalar_prefetch=1, grid=(N_CHIPS, steps),
            in_specs=[pl.BlockSpec((1, t, D_MODEL), lambda j, i, core_ref: (j, core_ref[0] * steps + i, 0)),
                      pl.BlockSpec((1, t, D_MODEL), lambda j, i, core_ref: (j, i, 0)), ANY],
            out_specs=pl.BlockSpec((1, t, D_MODEL), lambda j, i, core_ref: (j, i, 0))),
        out_shape=jax.ShapeDtypeStruct((N_CHIPS, half, D_MODEL), BF16),
        compiler_params=_params(n_axes=2),
    )(core, partial, other, after)


def _rs_exchange_chips(pre, name, collective_id=None):
    def body(s_ref, r_ref, send, recv):
        x, y, c, chips = _position()

        def copy(k, chunk, to):
            return pltpu.make_async_remote_copy(src_ref=s_ref.at[chunk], dst_ref=r_ref.at[k], send_sem=send.at[k],
                                                recv_sem=recv.at[k], device_id=to, device_id_type=MESH)

        sends = [copy(k, 2 * chip[0] + chip[1], (*chip, c)) for k, chip in enumerate(chips)]
        for cp in sends:
            cp.start()
        for cp in sends:
            cp.wait()

    return _comm_call(body, lambda x, y, c: [(1 - x, y, c), (x, 1 - y, c), (1 - x, 1 - y, c)],
                      jax.ShapeDtypeStruct((3, pre.shape[1], D_MODEL), BF16), 3, pre, name, collective_id)


def _rs_sum_chips(pre, received, place, name):
    half = pre.shape[1]
    t = half // 2 if half > 512 else half
    steps = half // t

    def body(place_ref, own_ref, r_ref, o_ref):
        acc = own_ref[0].astype(F32)
        for k in range(3):
            acc = acc + r_ref[k].astype(F32)
        o_ref[...] = acc

    return pl.pallas_call(
        body, name=name,
        grid_spec=pltpu.PrefetchScalarGridSpec(
            num_scalar_prefetch=1, grid=(steps,),
            in_specs=[pl.BlockSpec((1, t, D_MODEL), lambda i, place_ref: (place_ref[0], i, 0)),
                      pl.BlockSpec((3, t, D_MODEL), lambda i, place_ref: (0, i, 0))],
            out_specs=pl.BlockSpec((t, D_MODEL), lambda i, place_ref: (place_ref[1] * steps + i, 0))),
        out_shape=jax.ShapeDtypeStruct((2 * half, D_MODEL), F32),
        compiler_params=_params(),
    )(place, pre, received)


def _rs_finish(grads_a, grads_b, small):
    def body(fa_ref, fb_ref, s_ref, ga_ref, gb_ref, t_ref, send, recv, local_sem):
        del fa_ref, fb_ref
        x, y, c, chips = _position()
        sibling = (x, y, 1 - c)

        def slot(px, py, pc):
            return t_ref.at[4 * px + 2 * py + pc]

        def copy(k, block, to, src=None):
            return pltpu.make_async_remote_copy(src_ref=slot(*block) if src is None else src, dst_ref=slot(*block),
                                                send_sem=send.at[k], recv_sem=recv.at[k], device_id=to, device_id_type=MESH)

        def half_copies(core, to):
            out = []
            for k, g_ref in ((7, ga_ref), (8, gb_ref)):
                half = g_ref.shape[0] // 2
                rows = g_ref.at[pl.ds(pl.multiple_of(core * half, 8), half), :]
                out.append(pltpu.make_async_remote_copy(src_ref=rows, dst_ref=rows, send_sem=send.at[k], recv_sem=recv.at[k],
                                                        device_id=to, device_id_type=MESH))
            return out

        own_small = pltpu.make_async_copy(s_ref, slot(x, y, c), local_sem)
        own_small.start()
        to_sibling = half_copies(c, sibling)
        for cp in to_sibling:
            cp.start()
        first = [copy(0, (x, y, c), sibling, src=s_ref)]
        first += [copy(1 + k, (x, y, c), (*chip, c), src=s_ref) for k, chip in enumerate(chips)]
        for cp in first:
            cp.start()
        passed = []
        for k, chip in enumerate(chips):
            copy(1 + k, (*chip, c), (x, y, c)).wait_recv()
            fwd = copy(4 + k, (*chip, c), sibling)
            fwd.start()
            passed.append(fwd)
        copy(0, sibling, (x, y, c)).wait_recv()
        for k, chip in enumerate(chips):
            copy(4 + k, (*chip, 1 - c), (x, y, c)).wait_recv()
        for cp in half_copies(1 - c, (x, y, c)):
            cp.wait_recv()
        for cp in first + passed + to_sibling:
            cp.wait_send()
        own_small.wait()

    return pl.pallas_call(
        body, name="rs_finish", in_specs=[ANY, ANY, ANY], out_specs=[ANY, ANY, ANY], input_output_aliases={0: 0, 1: 1},
        out_shape=[jax.ShapeDtypeStruct(grads_a.shape, F32), jax.ShapeDtypeStruct(grads_b.shape, F32),
                   jax.ShapeDtypeStruct((N_DEV, SMALL_ROWS, 128), F32)],
        scratch_shapes=[pltpu.SemaphoreType.DMA((9,)), pltpu.SemaphoreType.DMA((9,)), pltpu.SemaphoreType.DMA],
    )(grads_a, grads_b, small)


def _adam_update(w, g, m, v):
    m_new = ADAM_B1 * m + (1.0 - ADAM_B1) * g
    v_new = ADAM_B2 * v + (1.0 - ADAM_B2) * (g * g)
    m_hat = m_new / (1.0 - ADAM_B1 ** ADAM_STEP)
    v_hat = v_new / (1.0 - ADAM_B2 ** ADAM_STEP)
    return -ADAM_LR * (m_hat / (jnp.sqrt(v_hat) + ADAM_EPS) + ADAM_WD * w), m_new, v_new


def _adamw(w, g_rows, row_off, m, v, name):
    rows, cols = w.shape
    t = rows if rows <= 320 else (rows // 2 if rows % 256 else 256)

    def body(w_ref, g_ref, m_ref, v_ref, go_ref, d_ref, nm_ref, nv_ref):
        g = g_ref[...]
        go_ref[...] = g
        d_ref[...], nm_ref[...], nv_ref[...] = _adam_update(w_ref[...], g, m_ref[...], v_ref[...])

    blk = pl.BlockSpec((t, cols), lambda i: (i, 0))
    assert row_off % 8 == 0 and t % 8 == 0
    g_blk = pl.BlockSpec((pl.Element(t), pl.Element(cols)), lambda i: (pl.multiple_of(row_off + i * t, 8), 0))
    shape = jax.ShapeDtypeStruct((rows, cols), F32)
    return pl.pallas_call(
        body, name=name, grid=(rows // t,), in_specs=[blk, g_blk, blk, blk], out_specs=[blk] * 4, out_shape=[shape] * 4,
        compiler_params=_params(),
    )(w, g_rows, m, v)


SMALL_PARAMS = [("g_attn", (1, D_MODEL), 8), ("g_q", (1, HEAD_DIM), None), ("g_k", (1, HEAD_DIM), None),
                ("sinks", (1, N_Q_HEADS), None), ("rel_bias", (N_BUCKETS, N_Q_HEADS), None), ("w_pool", (512, 128), None),
                ("pool_scale", (1, POOL_WIDTH), 4), ("g_ffn", (1, D_MODEL), 8), ("g_ple", (1, D_MODEL), 8)]


def _adamw_small(tables, wmv):
    n_par = len(SMALL_PARAMS)

    def body(*refs):
        t_ref = refs[0]
        ins = refs[1:1 + 3 * n_par]
        loss_ref = refs[1 + 3 * n_par]
        outs = refs[2 + 3 * n_par:-1]
        tot_ref = refs[-1]
        total = t_ref[0]
        for d in range(1, N_DEV):
            total = total + t_ref[d]
        tot_ref[...] = total
        loss_ref[...] = tot_ref[pl.ds(SMALL["loss"], 1), 0:1]
        for i, (name, shape, split) in enumerate(SMALL_PARAMS):
            g_ref, d_ref, nm_ref, nv_ref = outs[4 * i:4 * i + 4]
            row = SMALL[name]
            if split:
                for k in range(split):
                    g_ref[:, 128 * k:128 * k + 128] = tot_ref[pl.ds(row + k, 1), :]
            else:
                g_ref[...] = tot_ref[pl.ds(row, shape[0]), 0:shape[1]]
            w_ref, m_ref, v_ref = ins[3 * i:3 * i + 3]
            d_ref[...], nm_ref[...], nv_ref[...] = _adam_update(w_ref[...], g_ref[...], m_ref[...], v_ref[...])

    shapes = [jax.ShapeDtypeStruct((1, 1), F32)]
    for _, shape, _ in SMALL_PARAMS:
        shapes += [jax.ShapeDtypeStruct(shape, F32)] * 4
    flat = [a for triple in wmv for a in triple]
    res = pl.pallas_call(
        body, name="adamw_small", in_specs=[VMEM_WHOLE] * (1 + 3 * n_par), out_specs=[VMEM_WHOLE] * len(shapes),
        out_shape=shapes, scratch_shapes=[pltpu.VMEM((SMALL_ROWS, 128), F32)],
    )(tables, *flat)
    return res[0], [res[1 + 4 * i:5 + 4 * i] for i in range(n_par)]


def _pack_ple_proj(shard):
    return shard.reshape(4, 64, 256).transpose(1, 0, 2).reshape(64, D_MODEL)


class _Reduction:
    def __init__(self, tag, place, ids=(None, None)):
        self.tag, self.place, self.ids = tag, place, ids

    def start(self, partial):
        self.partial = partial
        self.other = _rs_swap_halves(partial, "rs_swap_" + self.tag, self.ids[0])
        return partial

    def middle(self, after):
        self.pre = _rs_add_halves(self.partial, self.other, self.place[1:], "rs_add_" + self.tag, after)
        self.received = _rs_exchange_chips(self.pre, "rs_exchange_" + self.tag, self.ids[1])
        return self.pre

    def finish(self):
        return _rs_sum_chips(self.pre, self.received, self.place, "rs_sum_" + self.tag)


def _local_grads(x2, p2, tgt, wts, g_attn_norm, g_q, g_k, attn_sinks, rel_bias, w_pool, pool_scale, g_ffn_norm, g_ple_norm,
                 reduce_a):
    early, late, local_slab, me = wts
    w_early, w_late = (early, local_slab, me), (late, local_slab, me)
    bucket = jnp.asarray(_bucket_table())
    gq = jnp.tile(g_q, (1, 2))
    gk = jnp.tile(g_k, (1, 2))
    wpool = w_pool[0].astype(BF16)
    sinks = attn_sinks[0]
    bias_st = _bias_build(rel_bias.T, bucket)

    hn1, zqk, u, kn, vb, qst = _attn_in(x2, g_attn_norm, gq, gk, w_early)
    ost = _attn_fwd(qst, kn, vb, bias_st, sinks)
    pooled, mix, h1, hn2 = _mix_out(u, ost, x2, w_early, wpool, pool_scale, g_ffn_norm)
    gate, up, h2 = _ffn_fwd(hn2, h1, w_late)
    loss_v, dh2, dgl, dpp, hn3, dg_ple = _ple_loss(h2, p2, tgt, w_late, g_ple_norm)

    dgate, dup, act, dh1, dg_ffn = _ffn_bwd(dh2, gate, up, h1, w_late, g_ffn_norm)
    rows_a = SLAB_ROWS - SLAB["inT"][1]
    partial_a = None
    for name, lhs, rhs in (("out", mix, dh1), ("gateT", dgate, hn2), ("upT", dup, hn2), ("down", act, dh2), ("plg", hn3, dgl)):
        partial_a = _dw(lhs, rhs, "dw_" + name, into=(partial_a, rows_a, SLAB[name][0] - SLAB["inT"][1]))
    dw_plp = _dw(p2, dpp, "dw_plp").reshape(4, 64, N_CHIPS, 256).transpose(2, 1, 0, 3).reshape(N_CHIPS, 64, D_MODEL)
    partial_a = reduce_a.start(lax.dynamic_update_slice(partial_a, dw_plp, (0, SLAB["plp"][0] - SLAB["inT"][1], 0)))
    dost, du, dyp, dscale = _mix_out_bwd(dh1, w_early, pooled, wpool, pool_scale, partial_a)
    pre_a = reduce_a.middle(du)
    dqst, dk, dv, dbias, dsink_rows = _attn_bwd(qst, kn, vb, dost, bias_st, sinks, pre_a)
    dz, dx, dg_attn, dgq, dgk = _attn_in_bwd(dqst, zqk, dk, dv, du, x2, dh1, w_early, g_attn_norm, gq, gk)

    partial_b = _dw(dz, hn1, "dw_in").reshape(N_CHIPS, -1, D_MODEL)
    small = _small_pack(dg_attn, dg_ffn, dg_ple, dscale, dgq, dgk, dbias, dsink_rows, bucket, loss_v, _dw_pool(pooled, dyp))
    return dx, partial_b, small


def kernel(x, p, w_in, w_out, g_attn_norm, g_q, g_k, attn_sinks, rel_bias, w_pool, pool_scale, g_ffn_norm, w_gate, w_up, w_down, g_ple_norm, w_ple_gate, w_ple_proj, loss_target, m_w_in, m_w_out, m_g_attn_norm, m_g_q, m_g_k, m_attn_sinks, m_rel_bias, m_w_pool, m_pool_scale, m_g_ffn_norm, m_w_gate, m_w_up, m_w_down, m_g_ple_norm, m_w_ple_gate, m_w_ple_proj, v_w_in, v_w_out, v_g_attn_norm, v_g_q, v_g_k, v_attn_sinks, v_rel_bias, v_w_pool, v_pool_scale, v_g_ffn_norm, v_w_gate, v_w_up, v_w_down, v_g_ple_norm, v_w_ple_gate, v_w_ple_proj):
    core = lax.axis_index("c").astype(jnp.int32).reshape(1)
    me = (2 * lax.axis_index("x") + lax.axis_index("y")).astype(jnp.int32).reshape(1)

    local_slab = jnp.concatenate(
        [w_in[0].T, w_out[0], w_gate[0].T, w_up[0].T, w_down[0], w_ple_gate[0], _pack_ple_proj(w_ple_proj[0])],
        axis=0).astype(BF16)
    wts = (_ag_weights(local_slab, 0, EARLY_ROWS, "ag_early", 1),
           _ag_weights(local_slab, EARLY_ROWS, SLAB_ROWS - EARLY_ROWS, "ag_late", 2), local_slab, me)

    place = jnp.concatenate([me, core])
    reduce_a = _Reduction("a", place, ids=(3, 4))
    dx, partial_b, small = _local_grads(x[0], p[0, 0], loss_target[0], wts, g_attn_norm, g_q, g_k, attn_sinks, rel_bias,
                                        w_pool, pool_scale, g_ffn_norm, g_ple_norm, reduce_a)
    reduce_b = _Reduction("b", place)
    reduce_b.start(partial_b)
    reduce_b.middle(partial_b)
    grads_a, grads_b, small_all = _rs_finish(reduce_a.finish(), reduce_b.finish(), small)

    def rows(name):
        return (grads_b, 0) if name == "inT" else (grads_a, SLAB[name][0] - SLAB["inT"][1])

    plp_rows = grads_a[SLAB["plp"][0] - SLAB["inT"][1]:]
    big = {
        "w_in": (w_in, m_w_in, v_w_in, rows("inT"), True),
        "w_out": (w_out, m_w_out, v_w_out, rows("out"), False),
        "w_gate": (w_gate, m_w_gate, v_w_gate, rows("gateT"), True),
        "w_up": (w_up, m_w_up, v_w_up, rows("upT"), True),
        "w_down": (w_down, m_w_down, v_w_down, rows("down"), False),
        "w_ple_gate": (w_ple_gate, m_w_ple_gate, v_w_ple_gate, rows("plg"), False),
        "w_ple_proj": (w_ple_proj, m_w_ple_proj, v_w_ple_proj,
                       (plp_rows.reshape(64, 4, 256).transpose(1, 0, 2).reshape(PLE_DIM, PLE_DIM), 0), False),
    }
    small_params = {
        "g_attn_norm": (g_attn_norm, m_g_attn_norm, v_g_attn_norm), "g_q": (g_q, m_g_q, v_g_q), "g_k": (g_k, m_g_k, v_g_k),
        "attn_sinks": (attn_sinks, m_attn_sinks, v_attn_sinks), "rel_bias": (rel_bias, m_rel_bias, v_rel_bias),
        "w_pool": tuple(a.reshape(512, 128) for a in (w_pool, m_w_pool, v_w_pool)),
        "pool_scale": (pool_scale, m_pool_scale, v_pool_scale), "g_ffn_norm": (g_ffn_norm, m_g_ffn_norm, v_g_ffn_norm),
        "g_ple_norm": (g_ple_norm, m_g_ple_norm, v_g_ple_norm),
    }

    grads, deltas, new_ms, new_vs = {}, {}, {}, {}
    for name, (w, m, v, (g_rows, row_off), transposed) in big.items():
        view = (lambda a: a.T) if transposed else (lambda a: a)
        out = _adamw(view(w[0]), g_rows, row_off, view(m[0]), view(v[0]), "adamw_" + name)
        grads[name], deltas[name], new_ms[name], new_vs[name] = (view(a)[None] for a in out)

    loss, small_out = _adamw_small(small_all, list(small_params.values()))
    for name, (g2, d, nm, nv) in zip(small_params, small_out):
        shape = w_pool.shape if name == "w_pool" else g2.shape
        grads[name], deltas[name], new_ms[name], new_vs[name] = (a.reshape(shape) for a in (g2, d, nm, nv))

    order = ["w_in", "w_out", "g_attn_norm", "g_q", "g_k", "attn_sinks", "rel_bias", "w_pool", "pool_scale", "g_ffn_norm",
             "w_gate", "w_up", "w_down", "g_ple_norm", "w_ple_gate", "w_ple_proj"]
    return (loss.reshape(()), dx[None], *[grads[n] for n in order], *[deltas[n] for n in order],
            *[new_ms[n] for n in order], *[new_vs[n] for n in order])
```

```python
import functools

import numpy as np
import jax
import jax.numpy as jnp
from jax import lax
from jax.experimental import pallas as pl
from jax.experimental.pallas import tpu as pltpu
from jax.experimental.pallas import tpu_sc as plsc

F32 = jnp.float32
BF16 = jnp.bfloat16
MESH = pl.DeviceIdType.MESH

D_MODEL = 1024
HEAD_DIM = 64
N_Q_HEADS = 8
ATTN_WIDTH = 512
KV_WIDTH = 128
POOL_WIDTH = 512
IN_WIDTH = 1280
D_FF = 2816
PLE_DIM = 256
FF_CHUNK = 1408
BLOCK = 128
N_BUCKETS = 32
MAX_DISTANCE = 128
POOL_SIZES = (2, 4, 8, 16)
EPS = 1e-6
NEG = -1e30
N_CHIPS = 4
N_DEV = 8

ADAM_LR = 0.001
ADAM_B1 = 0.9
ADAM_B2 = 0.999
ADAM_EPS = 1e-08
ADAM_WD = 0.01
ADAM_STEP = 10

SLAB = {"inT": (0, 320), "out": (320, 256), "gateT": (576, 704), "upT": (1280, 704), "down": (1984, 704),
        "plg": (2688, 256), "plp": (2944, 64)}
SLAB_ROWS = 3008
HALF_ROWS = SLAB_ROWS // 2
GATHER_PARTS = ((0, 320), (320, 576), (576, SLAB_ROWS))
POOL_HALO = 24

SMALL = {"g_attn": 0, "g_ffn": 8, "g_ple": 16, "pool_scale": 24, "g_q": 28, "g_k": 29, "sinks": 30, "loss": 31,
         "rel_bias": 32, "w_pool": 64}
SMALL_ROWS = 576

VMEM_LIMIT_BIG = 60 * 1024 * 1024
VMEM_LIMIT = 48 * 1024 * 1024


def _params(vmem=VMEM_LIMIT, n_axes=1):
    return pltpu.CompilerParams(dimension_semantics=("arbitrary",) * n_axes, vmem_limit_bytes=vmem)


def _dot(a, b, ca, cb):
    return lax.dot_general(a, b, (((ca,), (cb,)), ((), ())), preferred_element_type=F32)


def _full(shape):
    return pl.BlockSpec(shape, lambda i: (0,) * len(shape))


ANY = pl.BlockSpec(memory_space=pl.ANY)
VMEM_WHOLE = pl.BlockSpec(memory_space=pltpu.VMEM)


W_SPECS = [ANY, ANY, pl.BlockSpec(memory_space=pltpu.SMEM)]


def _load_rows(w_refs, name, dst_ref, sems):
    slab_ref, local_ref, me_ref = w_refs
    off, rows = SLAB[name]
    slab_off = off - max(start for start, _ in GATHER_PARTS if start <= off)
    me = me_ref[0]
    for phase in ("start", "wait"):
        for j in range(N_CHIPS):
            dst = dst_ref.at[pl.ds(j * rows, rows), :]
            theirs = pltpu.make_async_copy(slab_ref.at[j, pl.ds(slab_off, rows), :], dst, sems.at[j])
            own = pltpu.make_async_copy(local_ref.at[pl.ds(off, rows), :], dst, sems.at[j])

            @pl.when(me == j)
            def _():
                getattr(own, phase)()

            @pl.when(me != j)
            def _():
                getattr(theirs, phase)()


def _rms_fwd(x, g):
    r = lax.rsqrt(jnp.mean(x * x, axis=-1, keepdims=True) + EPS)
    return x * r * g


def _rms_bwd(x, g, dy):
    r = lax.rsqrt(jnp.mean(x * x, axis=-1, keepdims=True) + EPS)
    xn = x * r
    dyg = dy * g
    dx = r * (dyg - xn * jnp.mean(dyg * xn, axis=-1, keepdims=True))
    return dx, jnp.sum(dy * xn, axis=0, keepdims=True)


def _half_sum(v, lo):
    s_lo = jnp.sum(jnp.where(lo, v, 0.0), axis=-1, keepdims=True)
    s_hi = jnp.sum(jnp.where(lo, 0.0, v), axis=-1, keepdims=True)
    return jnp.where(lo, s_lo, s_hi)


def _half_sum_mxu(v):
    upper = lax.broadcasted_iota(jnp.int32, (128, 128), 0) < 64
    left = lax.broadcasted_iota(jnp.int32, (128, 128), 1) < 64
    ones = jnp.where(upper == left, 1.0, 0.0).astype(BF16)
    high = v.astype(BF16)
    low = (v - high.astype(F32)).astype(BF16)
    return _dot(high, ones, 1, 0) + _dot(low, ones, 1, 0)


def _pair_norm(zp, g, lo):
    r = lax.rsqrt(_half_sum(zp * zp, lo) * (1.0 / HEAD_DIM) + EPS)
    return zp * r * g


def _pair_norm_bwd(zp, g, dy):
    r = lax.rsqrt(_half_sum_mxu(zp * zp) * (1.0 / HEAD_DIM) + EPS)
    xn = zp * r
    dyg = dy * g
    dx = r * (dyg - xn * (_half_sum_mxu(dyg * xn) * (1.0 / HEAD_DIM)))
    return dx, jnp.sum(dy * xn, axis=0, keepdims=True)


def _to_stacked(pair, group, lo):
    rolled = pltpu.roll(pair, 64, axis=1)
    if group == 0:
        return jnp.where(lo, pair, 0.0), jnp.where(lo, rolled, 0.0)
    return jnp.where(lo, 0.0, rolled), jnp.where(lo, 0.0, pair)


def _from_stacked(even, odd, group, lo):
    if group == 0:
        return jnp.where(lo, even, pltpu.roll(odd, 64, axis=1))
    return jnp.where(lo, pltpu.roll(even, 64, axis=1), odd)


def _sigmoid(v):
    return 1.0 / (1.0 + jnp.exp(-v))


def _pool_counts(tile, n_rows):
    t1 = tile * n_rows + lax.broadcasted_iota(jnp.int32, (n_rows, POOL_WIDTH), 0) + 1
    lane = lax.broadcasted_iota(jnp.int32, (n_rows, POOL_WIDTH), 1)
    win = jnp.where(lane < 128, 2, jnp.where(lane < 256, 4, jnp.where(lane < 384, 8, 16)))
    return jnp.minimum(t1, win).astype(F32)


def _attn_in(x2, g_attn, gq, gk, wts):
    s_len = x2.shape[0]
    t = 512

    def body(x_ref, g_ref, gq_ref, gk_ref, sl_ref, lo_ref, me_ref, hn_ref, zqk_ref, u_ref, kn_ref, v_ref, qst_ref, w_ref, sems):
        @pl.when(pl.program_id(0) == 0)
        def _():
            _load_rows((sl_ref, lo_ref, me_ref), "inT", w_ref, sems)

        hn = _rms_fwd(x_ref[...], g_ref[...]).astype(BF16)
        hn_ref[...] = hn
        z = _dot(hn, w_ref[...], 1, 1)
        zqk_ref[...] = z[:, :640]
        u_ref[...] = z[:, 768:]
        v_ref[...] = z[:, 640:768].astype(BF16)
        lo = lax.broadcasted_iota(jnp.int32, (t, 128), 1) < 64
        kn_ref[...] = _pair_norm(z[:, 512:640], gk_ref[...], lo).astype(BF16)
        for p in range(4):
            qn = _pair_norm(z[:, 128 * p:128 * p + 128], gq_ref[...], lo)
            even, odd = _to_stacked(qn, p // 2, lo)
            qst_ref[2 * p] = even.astype(BF16)
            qst_ref[2 * p + 1] = odd.astype(BF16)

    row = lambda w: pl.BlockSpec((t, w), lambda i: (i, 0))
    return pl.pallas_call(
        body, name="attn_in", grid=(s_len // t,),
        in_specs=[row(D_MODEL), _full((1, D_MODEL)), _full((1, 128)), _full((1, 128))] + W_SPECS,
        out_specs=[row(D_MODEL), row(640), row(POOL_WIDTH), row(128), row(128),
                   pl.BlockSpec((N_Q_HEADS, t, 128), lambda i: (0, i, 0))],
        out_shape=[jax.ShapeDtypeStruct((s_len, D_MODEL), BF16), jax.ShapeDtypeStruct((s_len, 640), F32),
                   jax.ShapeDtypeStruct((s_len, POOL_WIDTH), F32), jax.ShapeDtypeStruct((s_len, 128), BF16),
                   jax.ShapeDtypeStruct((s_len, 128), BF16), jax.ShapeDtypeStruct((N_Q_HEADS, s_len, 128), BF16)],
        scratch_shapes=[pltpu.VMEM((IN_WIDTH, D_MODEL), BF16), pltpu.SemaphoreType.DMA((N_CHIPS,))],
        compiler_params=_params(),
    )(x2, g_attn, gq, gk, *wts)


def _bucket_table():
    i_idx = np.arange(BLOCK)[:, None]
    j_idx = np.arange(2 * BLOCK)[None, :]
    d = BLOCK + i_idx - j_idx
    n = np.maximum(d, 0)
    max_exact = N_BUCKETS // 2
    nf = np.maximum(n, 1).astype(np.float64)
    large = max_exact + (np.log(nf / max_exact) / np.log(MAX_DISTANCE / max_exact) * (N_BUCKETS - max_exact)).astype(np.int64)
    large = np.minimum(large, N_BUCKETS - 1)
    bucket = np.where(n < max_exact, n, large)
    return np.where((d >= 0) & (d < BLOCK), bucket, -1).astype(np.int32)


def _bias_build(rel_bias_t, bucket):
    def body(rb_ref, bucket_ref, out_ref):
        bk = bucket_ref[...]
        for h in range(N_Q_HEADS):
            acc = jnp.full((BLOCK, 2 * BLOCK), NEG, F32)
            for b in range(N_BUCKETS):
                acc = jnp.where(bk == b, rb_ref[h, b], acc)
            out_ref[0, pl.ds(h * BLOCK, BLOCK), :] = acc
            out_ref[1, pl.ds(h * BLOCK, BLOCK), :] = acc
            out_ref[1, pl.ds(h * BLOCK, BLOCK), 0:BLOCK] = jnp.full((BLOCK, BLOCK), NEG, F32)

    return pl.pallas_call(
        body, name="bias_build",
        in_specs=[pl.BlockSpec(memory_space=pltpu.SMEM), VMEM_WHOLE], out_specs=VMEM_WHOLE,
        out_shape=jax.ShapeDtypeStruct((2, N_Q_HEADS * BLOCK, 2 * BLOCK), F32),
    )(rel_bias_t, bucket)


def _head_softmax(s_ref, bias_ref, sink_ref, h):
    rows = pl.ds(pl.multiple_of(h * BLOCK, BLOCK), BLOCK)
    s = s_ref[rows, :] * (HEAD_DIM ** -0.5) + bias_ref[rows, :]
    sink = sink_ref[h]
    m = jnp.maximum(jnp.max(s, axis=-1, keepdims=True), sink)
    p = jnp.exp(s - m)
    e_sink = jnp.exp(sink - m)
    inv = 1.0 / (jnp.sum(p, axis=-1, keepdims=True) + e_sink)
    return rows, p * inv, e_sink * inv


def _attn_specs():
    prev = lambda i: (jnp.maximum(i - 1, 0), 0)
    cur = lambda i: (i, 0)
    stacked = pl.BlockSpec((N_Q_HEADS, BLOCK, 128), lambda i: (0, i, 0))
    kv = [pl.BlockSpec((BLOCK, 128), prev), pl.BlockSpec((BLOCK, 128), cur)]
    consts = [pl.BlockSpec((None, N_Q_HEADS * BLOCK, 2 * BLOCK), lambda i: (jnp.where(i == 0, 1, 0), 0, 0)),
              pl.BlockSpec(memory_space=pltpu.SMEM)]
    return stacked, kv, consts


def _head_lane_mask():
    rows = lax.broadcasted_iota(jnp.int32, (N_Q_HEADS * BLOCK, 128), 0)
    lanes = lax.broadcasted_iota(jnp.int32, (N_Q_HEADS * BLOCK, 128), 1)
    return (rows < 4 * BLOCK) == (lanes < 64)


def _attn_fwd(qst, kn, vb, bias_st, sinks):
    s_len = kn.shape[0]

    def body(q_ref, kp_ref, kc_ref, vp_ref, vc_ref, bias_ref, sink_ref, o_ref, s_ref, p_ref):
        q = q_ref[...].reshape(N_Q_HEADS * BLOCK, 128)
        s_ref[...] = _dot(q, jnp.concatenate([kp_ref[...], kc_ref[...]], axis=0), 1, 1)

        def head(h, carry):
            rows, probs, _ = _head_softmax(s_ref, bias_ref, sink_ref, h)
            p_ref[rows, :] = probs.astype(BF16)
            return carry

        lax.fori_loop(0, N_Q_HEADS, head, 0, unroll=True)
        o = _dot(p_ref[...], jnp.concatenate([vp_ref[...], vc_ref[...]], axis=0), 1, 0)
        o_ref[...] = jnp.where(_head_lane_mask(), o, 0.0).astype(BF16).reshape(N_Q_HEADS, BLOCK, 128)

    stacked, kv, consts = _attn_specs()
    return pl.pallas_call(
        body, name="attn_fwd", grid=(s_len // BLOCK,),
        in_specs=[stacked] + kv + kv + consts, out_specs=stacked,
        out_shape=jax.ShapeDtypeStruct((N_Q_HEADS, s_len, 128), BF16),
        scratch_shapes=[pltpu.VMEM((N_Q_HEADS * BLOCK, 2 * BLOCK), F32), pltpu.VMEM((N_Q_HEADS * BLOCK, 2 * BLOCK), BF16)],
        compiler_params=_params(),
    )(qst, kn, kn, vb, vb, bias_st, sinks)


def _mix_out(u, ost, x2, wts, wpool, pool_scale, g_ffn):
    s_len = x2.shape[0]
    t = 512
    n = t + 16

    def body(u_ref, o_ref, x_ref, sl_ref, lo_ref, me_ref, wp_ref, sc_ref, g_ref, pooled_ref, mix_ref, h1_ref, hn_ref,
             w_ref, ext_ref, st_ref, sems):
        i = pl.program_id(0)

        @pl.when(i == 0)
        def _():
            _load_rows((sl_ref, lo_ref, me_ref), "out", w_ref, sems)
            ext_ref[...] = jnp.zeros_like(ext_ref)
            st_ref[...] = jnp.zeros_like(st_ref)

        u_tile = u_ref[...]
        ext_ref[pl.ds(POOL_HALO, t), :] = u_tile
        st_ref[pl.ds(8, n), :] = ext_ref[pl.ds(8, n), :] + ext_ref[pl.ds(7, n), :]
        st_ref[pl.ds(8, n), 128:] = st_ref[pl.ds(8, n), 128:] + st_ref[pl.ds(6, n), 128:]
        st_ref[pl.ds(8, n), 256:] = st_ref[pl.ds(8, n), 256:] + st_ref[pl.ds(4, n), 256:]
        st_ref[pl.ds(8, n), 384:] = st_ref[pl.ds(8, n), 384:] + st_ref[pl.ds(0, n), 384:]
        ext_ref[pl.ds(0, POOL_HALO), :] = ext_ref[pl.ds(t, POOL_HALO), :]
        pooled = (st_ref[pl.ds(POOL_HALO, t), :] / _pool_counts(i, t) - u_tile).astype(BF16)
        pooled_ref[...] = pooled
        for g in range(4):
            cols = slice(128 * g, 128 * g + 128)
            y = _dot(pooled[:, cols], wp_ref[g], 1, 0) * sc_ref[:, cols]
            mix_ref[:, ATTN_WIDTH + 128 * g:ATTN_WIDTH + 128 * g + 128] = y.astype(BF16)
        lo = lax.broadcasted_iota(jnp.int32, (t, 128), 1) < 64
        for p in range(4):
            a = _from_stacked(o_ref[2 * p].astype(F32), o_ref[2 * p + 1].astype(F32), p // 2, lo)
            mix_ref[:, 128 * p:128 * p + 128] = a.astype(BF16)
        h1 = x_ref[...] + _dot(mix_ref[...], w_ref[...], 1, 0)
        h1_ref[...] = h1
        hn_ref[...] = _rms_fwd(h1, g_ref[...]).astype(BF16)

    row = lambda w: pl.BlockSpec((t, w), lambda i: (i, 0))
    return pl.pallas_call(
        body, name="mix_out", grid=(s_len // t,),
        in_specs=[row(POOL_WIDTH), pl.BlockSpec((N_Q_HEADS, t, 128), lambda i: (0, i, 0)), row(D_MODEL)] + W_SPECS
        + [_full((4, 128, 128)), _full((1, POOL_WIDTH)), _full((1, D_MODEL))],
        out_specs=[row(POOL_WIDTH), row(D_MODEL), row(D_MODEL), row(D_MODEL)],
        out_shape=[jax.ShapeDtypeStruct((s_len, POOL_WIDTH), BF16), jax.ShapeDtypeStruct((s_len, D_MODEL), BF16),
                   jax.ShapeDtypeStruct((s_len, D_MODEL), F32), jax.ShapeDtypeStruct((s_len, D_MODEL), BF16)],
        scratch_shapes=[pltpu.VMEM((D_MODEL, D_MODEL), BF16), pltpu.VMEM((t + POOL_HALO, POOL_WIDTH), F32),
                        pltpu.VMEM((t + POOL_HALO, POOL_WIDTH), F32), pltpu.SemaphoreType.DMA((N_CHIPS,))],
        compiler_params=_params(),
    )(u, ost, x2, *wts, wpool, pool_scale, g_ffn)


def _ffn_fwd(hn2, h1, wts):
    s_len = h1.shape[0]
    t = 256

    def body(hn_ref, h1_ref, sl_ref, lo_ref, me_ref, gate_ref, up_ref, h2_ref, wg_ref, wu_ref, wd_ref, sems):
        @pl.when(pl.program_id(0) == 0)
        def _():
            w_refs = (sl_ref, lo_ref, me_ref)
            _load_rows(w_refs, "gateT", wg_ref, sems)
            _load_rows(w_refs, "upT", wu_ref, sems)
            _load_rows(w_refs, "down", wd_ref, sems)

        hn = hn_ref[...]
        h2 = h1_ref[...]
        for ch in range(D_FF // FF_CHUNK):
            rows = pl.ds(ch * FF_CHUNK, FF_CHUNK)
            cols = slice(ch * FF_CHUNK, (ch + 1) * FF_CHUNK)
            gate = _dot(hn, wg_ref[rows, :], 1, 1)
            up = _dot(hn, wu_ref[rows, :], 1, 1)
            gate_ref[:, cols] = gate
            up_ref[:, cols] = up
            act = (gate * _sigmoid(gate) * up).astype(BF16)
            h2 = h2 + _dot(act, wd_ref[rows, :], 1, 0)
        h2_ref[...] = h2

    row = lambda w: pl.BlockSpec((t, w), lambda i: (i, 0))
    return pl.pallas_call(
        body, name="ffn_fwd", grid=(s_len // t,),
        in_specs=[row(D_MODEL), row(D_MODEL)] + W_SPECS,
        out_specs=[row(D_FF), row(D_FF), row(D_MODEL)],
        out_shape=[jax.ShapeDtypeStruct((s_len, D_FF), F32), jax.ShapeDtypeStruct((s_len, D_FF), F32),
                   jax.ShapeDtypeStruct((s_len, D_MODEL), F32)],
        scratch_shapes=[pltpu.VMEM((D_FF, D_MODEL), BF16)] * 3 + [pltpu.SemaphoreType.DMA((N_CHIPS,))],
        compiler_params=_params(VMEM_LIMIT_BIG),
    )(hn2, h1, *wts)


def _ple_loss(h2, p2, tgt, wts, g_ple):
    s_len = h2.shape[0]
    t = 512
    n_tiles = s_len // t

    def body(h2_ref, p_ref, tgt_ref, sl_ref, lo_ref, me_ref, g_ref, loss_ref, dh2_ref, dgl_ref, dpp_ref, hn_ref,
             dg_ref, w_ref, wp_ref, packed_ref, loss_acc, sems):
        i = pl.program_id(0)

        @pl.when(i == 0)
        def _():
            w_refs = (sl_ref, lo_ref, me_ref)
            _load_rows(w_refs, "plg", w_ref, sems)
            _load_rows(w_refs, "plp", packed_ref, sems)
            for j in range(N_CHIPS):
                for q in range(4):
                    wp_ref[pl.ds(64 * q, 64), 256 * j:256 * j + 256] = packed_ref[pl.ds(64 * j, 64), 256 * q:256 * q + 256]
            loss_acc[...] = jnp.zeros_like(loss_acc)
            dg_ref[...] = jnp.zeros_like(dg_ref)

        h2v = h2_ref[...]
        g = g_ref[...]
        hn = _rms_fwd(h2v, g).astype(BF16)
        hn_ref[...] = hn
        gate = _sigmoid(_dot(hn, w_ref[...], 1, 0))
        pp = _dot(p_ref[...].astype(BF16), wp_ref[...], 1, 0)
        err = h2v + gate * pp - tgt_ref[...]
        loss_acc[...] += jnp.sum(err * err, axis=0, keepdims=True)
        dy = err * (1.0 / D_MODEL)
        dpp_ref[...] = (dy * gate).astype(BF16)
        dgl = (dy * pp * gate * (1.0 - gate)).astype(BF16)
        dgl_ref[...] = dgl
        dx, dg = _rms_bwd(h2v, g, _dot(dgl, w_ref[...], 1, 1))
        dh2_ref[...] = dy + dx
        dg_ref[...] += dg

        @pl.when(i == n_tiles - 1)
        def _():
            total = jnp.sum(loss_acc[...], axis=-1, keepdims=True) * (0.5 / D_MODEL)
            loss_ref[...] = jnp.broadcast_to(total, loss_ref.shape)

    row = lambda w: pl.BlockSpec((t, w), lambda i: (i, 0))
    return pl.pallas_call(
        body, name="ple_loss", grid=(n_tiles,),
        in_specs=[row(D_MODEL), row(PLE_DIM), row(D_MODEL)] + W_SPECS + [_full((1, D_MODEL))],
        out_specs=[_full((1, 128)), row(D_MODEL), row(D_MODEL), row(D_MODEL), row(D_MODEL), _full((1, D_MODEL))],
        out_shape=[jax.ShapeDtypeStruct((1, 128), F32), jax.ShapeDtypeStruct((s_len, D_MODEL), F32),
                   jax.ShapeDtypeStruct((s_len, D_MODEL), BF16), jax.ShapeDtypeStruct((s_len, D_MODEL), BF16),
                   jax.ShapeDtypeStruct((s_len, D_MODEL), BF16), jax.ShapeDtypeStruct((1, D_MODEL), F32)],
        scratch_shapes=[pltpu.VMEM((D_MODEL, D_MODEL), BF16), pltpu.VMEM((PLE_DIM, D_MODEL), BF16),
                        pltpu.VMEM((PLE_DIM, D_MODEL), BF16), pltpu.VMEM((1, D_MODEL), F32),
                        pltpu.SemaphoreType.DMA((N_CHIPS,))],
        compiler_params=_params(),
    )(h2, p2, tgt, *wts, g_ple)


def _ffn_bwd(dh2, gate, up, h1, wts, g_ffn):
    s_len = h1.shape[0]
    t = 256

    def body(dh2_ref, gate_ref, up_ref, h1_ref, sl_ref, lo_ref, me_ref, g_ref, dgate_ref, dup_ref, act_ref, dh1_ref, dg_ref,
             wg_ref, wu_ref, wd_ref, sems):
        @pl.when(pl.program_id(0) == 0)
        def _():
            w_refs = (sl_ref, lo_ref, me_ref)
            _load_rows(w_refs, "gateT", wg_ref, sems)
            _load_rows(w_refs, "upT", wu_ref, sems)
            _load_rows(w_refs, "down", wd_ref, sems)
            dg_ref[...] = jnp.zeros_like(dg_ref)

        dh2v = dh2_ref[...]
        dh2b = dh2v.astype(BF16)
        dhn = jnp.zeros((t, D_MODEL), F32)
        for ch in range(D_FF // FF_CHUNK):
            rows = pl.ds(ch * FF_CHUNK, FF_CHUNK)
            cols = slice(ch * FF_CHUNK, (ch + 1) * FF_CHUNK)
            dact = _dot(dh2b, wd_ref[rows, :], 1, 1)
            gate_v = gate_ref[:, cols]
            up_v = up_ref[:, cols]
            sg = _sigmoid(gate_v)
            silu = gate_v * sg
            act_ref[:, cols] = (silu * up_v).astype(BF16)
            dup = (dact * silu).astype(BF16)
            dgate = (dact * up_v * (sg * (1.0 + gate_v * (1.0 - sg)))).astype(BF16)
            dup_ref[:, cols] = dup
            dgate_ref[:, cols] = dgate
            dhn = dhn + _dot(dgate, wg_ref[rows, :], 1, 0) + _dot(dup, wu_ref[rows, :], 1, 0)
        dx, dg = _rms_bwd(h1_ref[...], g_ref[...], dhn)
        dh1_ref[...] = dh2v + dx
        dg_ref[...] += dg

    row = lambda w: pl.BlockSpec((t, w), lambda i: (i, 0))
    return pl.pallas_call(
        body, name="ffn_bwd", grid=(s_len // t,),
        in_specs=[row(D_MODEL), row(D_FF), row(D_FF), row(D_MODEL)] + W_SPECS + [_full((1, D_MODEL))],
        out_specs=[row(D_FF), row(D_FF), row(D_FF), row(D_MODEL), _full((1, D_MODEL))],
        out_shape=[jax.ShapeDtypeStruct((s_len, D_FF), BF16), jax.ShapeDtypeStruct((s_len, D_FF), BF16),
                   jax.ShapeDtypeStruct((s_len, D_FF), BF16), jax.ShapeDtypeStruct((s_len, D_MODEL), F32),
                   jax.ShapeDtypeStruct((1, D_MODEL), F32)],
        scratch_shapes=[pltpu.VMEM((D_FF, D_MODEL), BF16)] * 3 + [pltpu.SemaphoreType.DMA((N_CHIPS,))],
        compiler_params=_params(VMEM_LIMIT_BIG),
    )(dh2, gate, up, h1, *wts, g_ffn)


def _mix_out_bwd(dh1, wts, pooled, wpool, pool_scale, after):
    s_len = dh1.shape[0]
    t = 512
    n = t + 16
    n_tiles = s_len // t

    def body(dh1_ref, sl_ref, lo_ref, me_ref, pooled_ref, wp_ref, sc_ref, after_ref, dost_ref, du_ref, dyp_ref, dsc_ref,
             w_ref, ext_ref, st_ref, sems):
        del after_ref
        i = pl.program_id(0)

        @pl.when(i == 0)
        def _():
            _load_rows((sl_ref, lo_ref, me_ref), "out", w_ref, sems)
            ext_ref[...] = jnp.zeros_like(ext_ref)
            st_ref[...] = jnp.zeros_like(st_ref)
            dsc_ref[...] = jnp.zeros_like(dsc_ref)

        dmix = _dot(dh1_ref[...].astype(BF16), w_ref[...], 1, 1)
        lo = lax.broadcasted_iota(jnp.int32, (t, 128), 1) < 64
        for p in range(4):
            even, odd = _to_stacked(dmix[:, 128 * p:128 * p + 128], p // 2, lo)
            dost_ref[2 * p] = even.astype(BF16)
            dost_ref[2 * p + 1] = odd.astype(BF16)
        pooled_v = pooled_ref[...]
        counts = _pool_counts(n_tiles - 1 - i, t)
        for g in range(4):
            cols = slice(128 * g, 128 * g + 128)
            dm = dmix[:, ATTN_WIDTH + 128 * g:ATTN_WIDTH + 128 * g + 128]
            ypre = _dot(pooled_v[:, cols], wp_ref[g], 1, 0)
            dsc_ref[:, cols] += jnp.sum(ypre * dm, axis=0, keepdims=True)
            dyp = (dm * sc_ref[:, cols]).astype(BF16)
            dyp_ref[:, cols] = dyp
            dpooled = _dot(dyp, wp_ref[g], 1, 1)
            du_ref[:, cols] = -dpooled
            ext_ref[pl.ds(0, t), cols] = dpooled / counts[:, cols]
        st_ref[pl.ds(0, n), :] = ext_ref[pl.ds(0, n), :] + ext_ref[pl.ds(1, n), :]
        st_ref[pl.ds(0, n), 128:] = st_ref[pl.ds(0, n), 128:] + st_ref[pl.ds(2, n), 128:]
        st_ref[pl.ds(0, n), 256:] = st_ref[pl.ds(0, n), 256:] + st_ref[pl.ds(4, n), 256:]
        st_ref[pl.ds(0, n), 384:] = st_ref[pl.ds(0, n), 384:] + st_ref[pl.ds(8, n), 384:]
        ext_ref[pl.ds(t, POOL_HALO), :] = ext_ref[pl.ds(0, POOL_HALO), :]
        du_ref[...] += st_ref[pl.ds(0, t), :]

    rev = lambda w: pl.BlockSpec((t, w), lambda i: (n_tiles - 1 - i, 0))
    return pl.pallas_call(
        body, name="mix_out_bwd", grid=(n_tiles,),
        in_specs=[rev(D_MODEL)] + W_SPECS + [rev(POOL_WIDTH), _full((4, 128, 128)), _full((1, POOL_WIDTH)), ANY],
        out_specs=[pl.BlockSpec((N_Q_HEADS, t, 128), lambda i: (0, n_tiles - 1 - i, 0)), rev(POOL_WIDTH), rev(POOL_WIDTH),
                   _full((1, POOL_WIDTH))],
        out_shape=[jax.ShapeDtypeStruct((N_Q_HEADS, s_len, 128), BF16), jax.ShapeDtypeStruct((s_len, POOL_WIDTH), F32),
                   jax.ShapeDtypeStruct((s_len, POOL_WIDTH), BF16), jax.ShapeDtypeStruct((1, POOL_WIDTH), F32)],
        scratch_shapes=[pltpu.VMEM((D_MODEL, D_MODEL), BF16), pltpu.VMEM((t + POOL_HALO, POOL_WIDTH), F32),
                        pltpu.VMEM((t + POOL_HALO, POOL_WIDTH), F32), pltpu.SemaphoreType.DMA((N_CHIPS,))],
        compiler_params=_params(),
    )(dh1, *wts, pooled, wpool, pool_scale, after)


def _attn_bwd(qst, kn, vb, dost, bias_st, sinks, after):
    s_len = kn.shape[0]

    def body(q_ref, kp_ref, kc_ref, vp_ref, vc_ref, do_ref, bias_ref, sink_ref, after_ref, dq_ref, dk_ref, dv_ref, dbias_ref,
             dsink_ref, s_ref, dp_ref, p_ref, dl_ref):
        del after_ref
        i = pl.program_id(0)

        @pl.when(i == 0)
        def _():
            dk_ref[...] = jnp.zeros_like(dk_ref)
            dv_ref[...] = jnp.zeros_like(dv_ref)
            dbias_ref[...] = jnp.zeros_like(dbias_ref)
            dsink_ref[...] = jnp.zeros_like(dsink_ref)

        q = q_ref[...].reshape(N_Q_HEADS * BLOCK, 128)
        do = do_ref[...].reshape(N_Q_HEADS * BLOCK, 128)
        k2 = jnp.concatenate([kp_ref[...], kc_ref[...]], axis=0)
        s_ref[...] = _dot(q, k2, 1, 1)
        dp_ref[...] = _dot(do, jnp.concatenate([vp_ref[...], vc_ref[...]], axis=0), 1, 1)

        def head(h, carry):
            rows, probs, p_sink = _head_softmax(s_ref, bias_ref, sink_ref, h)
            dp = dp_ref[rows, :]
            dsum = jnp.sum(probs * dp, axis=-1, keepdims=True)
            dlog = probs * (dp - dsum)
            dsink_ref[rows, :] -= p_sink * dsum
            dbias_ref[rows, :] += dlog
            p_ref[rows, :] = probs.astype(BF16)
            dl_ref[rows, :] = (dlog * (HEAD_DIM ** -0.5)).astype(BF16)
            return carry

        lax.fori_loop(0, N_Q_HEADS, head, 0, unroll=True)
        dlog_s = dl_ref[...]
        dq_ref[...] = jnp.where(_head_lane_mask(), _dot(dlog_s, k2, 1, 0), 0.0).reshape(N_Q_HEADS, BLOCK, 128)
        dk2 = _dot(dlog_s, q, 0, 0)
        dv2 = _dot(p_ref[...], do, 0, 0)
        prev_rows = pl.ds(pl.multiple_of(jnp.maximum(i - 1, 0) * BLOCK, BLOCK), BLOCK)
        cur_rows = pl.ds(pl.multiple_of(i * BLOCK, BLOCK), BLOCK)
        dk_ref[prev_rows, :] += dk2[:BLOCK]
        dk_ref[cur_rows, :] += dk2[BLOCK:]
        dv_ref[prev_rows, :] += dv2[:BLOCK]
        dv_ref[cur_rows, :] += dv2[BLOCK:]

    stacked, kv, consts = _attn_specs()
    band = (N_Q_HEADS * BLOCK, 2 * BLOCK)
    return pl.pallas_call(
        body, name="attn_bwd", grid=(s_len // BLOCK,),
        in_specs=[stacked] + kv + kv + [stacked] + consts + [ANY],
        out_specs=[stacked, _full((s_len, 128)), _full((s_len, 128)), _full(band), _full((N_Q_HEADS * BLOCK, 1))],
        out_shape=[jax.ShapeDtypeStruct((N_Q_HEADS, s_len, 128), F32), jax.ShapeDtypeStruct((s_len, 128), F32),
                   jax.ShapeDtypeStruct((s_len, 128), F32), jax.ShapeDtypeStruct(band, F32),
                   jax.ShapeDtypeStruct((N_Q_HEADS * BLOCK, 1), F32)],
        scratch_shapes=[pltpu.VMEM(band, F32), pltpu.VMEM(band, F32), pltpu.VMEM(band, BF16), pltpu.VMEM(band, BF16)],
        compiler_params=_params(),
    )(qst, kn, kn, vb, vb, dost, bias_st, sinks, after)


def _small_pack(dg_attn, dg_ffn, dg_ple, dscale, dgq, dgk, dbias, dsink_rows, bucket, loss_v, dwpool):
    def body(ga_ref, gf_ref, gp_ref, sc_ref, gq_ref, gk_ref, db_ref, ds_ref, bucket_ref, loss_ref, wp_ref, out_ref):
        out_ref[pl.ds(0, SMALL["w_pool"]), :] = jnp.zeros((SMALL["w_pool"], 128), F32)
        for name, ref, n in (("g_attn", ga_ref, 8), ("g_ffn", gf_ref, 8), ("g_ple", gp_ref, 8), ("pool_scale", sc_ref, 4)):
            for k in range(n):
                out_ref[pl.ds(SMALL[name] + k, 1), :] = ref[:, 128 * k:128 * k + 128]
        for name, ref in (("g_q", gq_ref), ("g_k", gk_ref)):
            both = ref[...]
            out_ref[pl.ds(SMALL[name], 1), :] = both + pltpu.roll(both, 64, axis=1)
        out_ref[pl.ds(SMALL["loss"], 1), :] = loss_ref[...]
        bk = bucket_ref[...]
        rows = lax.broadcasted_iota(jnp.int32, (N_BUCKETS, 128), 0)
        lanes = lax.broadcasted_iota(jnp.int32, (N_BUCKETS, 128), 1)
        lane1 = lax.broadcasted_iota(jnp.int32, (1, 128), 1)
        rb = jnp.zeros((N_BUCKETS, 128), F32)
        sk = jnp.zeros((1, 128), F32)
        for h in range(N_Q_HEADS):
            band = db_ref[pl.ds(h * BLOCK, BLOCK), :]
            for b in range(N_BUCKETS):
                rb = jnp.where((rows == b) & (lanes == h), jnp.sum(jnp.where(bk == b, band, 0.0)), rb)
            sk = jnp.where(lane1 == h, jnp.sum(ds_ref[pl.ds(h * BLOCK, BLOCK), :]), sk)
        out_ref[pl.ds(SMALL["rel_bias"], N_BUCKETS), :] = rb
        out_ref[pl.ds(SMALL["sinks"], 1), :] = sk
        out_ref[pl.ds(SMALL["w_pool"], 512), :] = wp_ref[...].reshape(512, 128)

    return pl.pallas_call(
        body, name="small_pack", in_specs=[VMEM_WHOLE] * 11, out_specs=VMEM_WHOLE,
        out_shape=jax.ShapeDtypeStruct((SMALL_ROWS, 128), F32),
    )(dg_attn, dg_ffn, dg_ple, dscale, dgq, dgk, dbias, dsink_rows, bucket, loss_v, dwpool)


def _attn_in_bwd(dqst, zqk, dk, dv, du, x2, dh1, wts, g_attn, gq, gk):
    s_len = x2.shape[0]
    t = 512

    def body(dq_ref, zqk_ref, dk_ref, dv_ref, du_ref, x_ref, dh1_ref, sl_ref, lo_ref, me_ref, g_ref, gq_ref, gk_ref,
             dz_ref, dx_ref, dg_ref, dgq_ref, dgk_ref, w_ref, sems):
        @pl.when(pl.program_id(0) == 0)
        def _():
            _load_rows((sl_ref, lo_ref, me_ref), "inT", w_ref, sems)
            dg_ref[...] = jnp.zeros_like(dg_ref)
            dgq_ref[...] = jnp.zeros_like(dgq_ref)
            dgk_ref[...] = jnp.zeros_like(dgk_ref)

        lo = lax.broadcasted_iota(jnp.int32, (t, 128), 1) < 64
        for p in range(4):
            dqn = _from_stacked(dq_ref[2 * p], dq_ref[2 * p + 1], p // 2, lo)
            dq_raw, dgq = _pair_norm_bwd(zqk_ref[:, 128 * p:128 * p + 128], gq_ref[...], dqn)
            dz_ref[:, 128 * p:128 * p + 128] = dq_raw.astype(BF16)
            dgq_ref[...] += dgq
        dk_raw, dgk = _pair_norm_bwd(zqk_ref[:, 512:640], gk_ref[...], dk_ref[...])
        dgk_ref[...] += dgk
        dz_ref[:, 512:640] = dk_raw.astype(BF16)
        dz_ref[:, 640:768] = dv_ref[...].astype(BF16)
        dz_ref[:, 768:] = du_ref[...].astype(BF16)
        dx, dg = _rms_bwd(x_ref[...], g_ref[...], _dot(dz_ref[...], w_ref[...], 1, 0))
        dx_ref[...] = dh1_ref[...] + dx
        dg_ref[...] += dg

    row = lambda w: pl.BlockSpec((t, w), lambda i: (i, 0))
    return pl.pallas_call(
        body, name="attn_in_bwd", grid=(s_len // t,),
        in_specs=[pl.BlockSpec((N_Q_HEADS, t, 128), lambda i: (0, i, 0)), row(640), row(128), row(128), row(POOL_WIDTH),
                  row(D_MODEL), row(D_MODEL)] + W_SPECS + [_full((1, D_MODEL)), _full((1, 128)), _full((1, 128))],
        out_specs=[row(IN_WIDTH), row(D_MODEL), _full((1, D_MODEL)), _full((1, 128)), _full((1, 128))],
        out_shape=[jax.ShapeDtypeStruct((s_len, IN_WIDTH), BF16), jax.ShapeDtypeStruct((s_len, D_MODEL), F32),
                   jax.ShapeDtypeStruct((1, D_MODEL), F32), jax.ShapeDtypeStruct((1, 128), F32),
                   jax.ShapeDtypeStruct((1, 128), F32)],
        scratch_shapes=[pltpu.VMEM((IN_WIDTH, D_MODEL), BF16), pltpu.SemaphoreType.DMA((N_CHIPS,))],
        compiler_params=_params(),
    )(dqst, zqk, dk, dv, du, x2, dh1, *wts, g_attn, gq, gk)


def _dw(a, b, name, into=None):
    s_len, m = a.shape
    n_out = b.shape[1]
    tk = 1024
    n_steps = s_len // tk
    tm = m // 2 if m > 1408 else m
    chunk = m // N_CHIPS
    per_tile = tm // chunk

    def accumulate(a_ref, b_ref, acc_ref, k):
        @pl.when(k == 0)
        def _():
            acc_ref[...] = _dot(a_ref[...].astype(BF16), b_ref[...].astype(BF16), 0, 0)

        @pl.when(k > 0)
        def _():
            acc_ref[...] += _dot(a_ref[...].astype(BF16), b_ref[...].astype(BF16), 0, 0)

    in_specs = [pl.BlockSpec((tk, tm), lambda i, k: (k, i)), pl.BlockSpec((tk, n_out), lambda i, k: (k, 0))]
    if into is None:
        def body(a_ref, b_ref, o_ref, acc_ref):
            k = pl.program_id(1)
            accumulate(a_ref, b_ref, acc_ref, k)

            @pl.when(k == n_steps - 1)
            def _():
                o_ref[...] = acc_ref[...].astype(BF16)

        return pl.pallas_call(
            body, name=name, grid=(m // tm, n_steps), in_specs=in_specs,
            out_specs=pl.BlockSpec((tm, n_out), lambda i, k: (i, 0)), out_shape=jax.ShapeDtypeStruct((m, n_out), BF16),
            scratch_shapes=[pltpu.VMEM((tm, n_out), F32)], compiler_params=_params(n_axes=2),
        )(a, b)

    slab, slab_rows, row_off = into
    assert n_out == D_MODEL

    def body_into(a_ref, b_ref, *rest):
        o_ref, acc_ref, stage_ref, sems = rest[-4:]
        i, k = pl.program_id(0), pl.program_id(1)
        accumulate(a_ref, b_ref, acc_ref, k)

        @pl.when(k == n_steps - 1)
        def _():
            stage_ref[...] = acc_ref[...].astype(BF16)
            copies = [pltpu.make_async_copy(stage_ref.at[pl.ds(jj * chunk, chunk), :],
                                            o_ref.at[i * per_tile + jj, pl.ds(row_off, chunk), :], sems.at[jj])
                      for jj in range(per_tile)]
            for cp in copies:
                cp.start()
            for cp in copies:
                cp.wait()

    operands, aliases = [a, b], {}
    if slab is not None:
        in_specs = in_specs + [ANY]
        operands.append(slab)
        aliases = {2: 0}
    return pl.pallas_call(
        body_into, name=name, grid=(m // tm, n_steps), in_specs=in_specs, out_specs=ANY,
        out_shape=jax.ShapeDtypeStruct((N_CHIPS, slab_rows, D_MODEL), BF16), input_output_aliases=aliases,
        scratch_shapes=[pltpu.VMEM((tm, n_out), F32), pltpu.VMEM((tm, n_out), BF16), pltpu.SemaphoreType.DMA((per_tile,))],
        compiler_params=_params(n_axes=2),
    )(*operands)


def _dw_pool(pooled, dyp):
    s_len = pooled.shape[0]
    tk = 512

    def body(a_ref, b_ref, o_ref):
        @pl.when(pl.program_id(0) == 0)
        def _():
            o_ref[...] = jnp.zeros_like(o_ref)

        for g in range(4):
            cols = slice(128 * g, 128 * g + 128)
            o_ref[g] += _dot(a_ref[:, cols], b_ref[:, cols], 0, 0)

    blk = pl.BlockSpec((tk, POOL_WIDTH), lambda k: (k, 0))
    return pl.pallas_call(
        body, name="dw_pool", grid=(s_len // tk,), in_specs=[blk, blk], out_specs=_full((4, 128, 128)),
        out_shape=jax.ShapeDtypeStruct((4, 128, 128), F32), compiler_params=_params(),
    )(pooled, dyp)


def _position():
    x, y, c = lax.axis_index("x"), lax.axis_index("y"), lax.axis_index("c")
    other_chips = [(1 - x, y), (x, 1 - y), (1 - x, 1 - y)]
    return x, y, c, other_chips


def _ag_weights(local_slab, row0, n_rows, name, collective_id):
    half = n_rows // 2
    quarter = half // 2
    assert quarter % 16 == 0

    def body(l_ref, g_ref, send, recv):
        x, y, c, chips = _position()
        me, (via_x, via_y, diagonal) = 2 * x + y, [2 * chip[0] + chip[1] for chip in chips]
        here, sibling, x_nbr, y_nbr = (x, y, c), (x, y, 1 - c), (1 - x, y, c), (x, 1 - y, c)
        peers = [sibling, x_nbr, y_nbr]
        barrier = pltpu.get_barrier_semaphore()
        for peer in peers:
            pl.semaphore_signal(barrier, inc=1, device_id=peer, device_id_type=MESH)
        pl.semaphore_wait(barrier, len(peers))

        def rows(core, part):
            start, size = (core * half, half) if part is None else (core * half + part * quarter, quarter)
            return pl.ds(pl.multiple_of(start, 16), size)

        def copy(k, chip_idx, where, to, src=None):
            dst = g_ref.at[chip_idx, where, :]
            return pltpu.make_async_remote_copy(src_ref=dst if src is None else src, dst_ref=dst, send_sem=send.at[k],
                                                recv_sem=recv.at[k], device_id=to, device_id_type=MESH)

        own_rows = l_ref.at[pl.ds(pl.multiple_of(row0 + c * half, 16), half), :]
        started = [copy(0, me, rows(c, None), x_nbr, src=own_rows), copy(1, me, rows(c, None), y_nbr, src=own_rows)]
        for cp in started:
            cp.start()
        after_arrival = [
            (copy(0, via_x, rows(c, None), here), [copy(4, via_x, rows(c, None), sibling), copy(3, via_x, rows(c, 1), y_nbr)]),
            (copy(1, via_y, rows(c, None), here), [copy(5, via_y, rows(c, None), sibling), copy(2, via_y, rows(c, 0), x_nbr)]),
            (copy(2, diagonal, rows(c, 0), here), [copy(6, diagonal, rows(c, 0), sibling)]),
            (copy(3, diagonal, rows(c, 1), here), [copy(7, diagonal, rows(c, 1), sibling)]),
        ]
        for arrival, onward in after_arrival:
            arrival.wait_recv()
            for cp in onward:
                cp.start()
            started += onward
        for cp in (copy(4, via_x, rows(1 - c, None), here), copy(5, via_y, rows(1 - c, None), here),
                   copy(6, diagonal, rows(1 - c, 0), here), copy(7, diagonal, rows(1 - c, 1), here)):
            cp.wait_recv()
        for cp in started:
            cp.wait_send()

    return pl.kernel(
        body, out_type=jax.ShapeDtypeStruct((N_CHIPS, n_rows, D_MODEL), BF16),
        mesh=plsc.ScalarSubcoreMesh(axis_name="sequencer", num_cores=1), name=name,
        scratch_types=[pltpu.SemaphoreType.DMA((8,)), pltpu.SemaphoreType.DMA((8,))],
        compiler_params=pltpu.CompilerParams(collective_id=collective_id),
    )(local_slab)


def _comm_call(body, peers_of, out_shape, n_sems, operand, name, collective_id):
    sems = [pltpu.SemaphoreType.DMA((n_sems,)), pltpu.SemaphoreType.DMA((n_sems,))]
    if collective_id is None:
        return pl.pallas_call(body, name=name, in_specs=[ANY], out_specs=ANY, out_shape=out_shape, scratch_shapes=sems)(operand)

    def with_handshake(in_ref, out_ref, send, recv):
        x, y, c, _ = _position()
        peers = peers_of(x, y, c)
        barrier = pltpu.get_barrier_semaphore()
        for peer in peers:
            pl.semaphore_signal(barrier, inc=1, device_id=peer, device_id_type=MESH)
        pl.semaphore_wait(barrier, len(peers))
        body(in_ref, out_ref, send, recv)

    return pl.kernel(with_handshake, out_type=out_shape, mesh=plsc.ScalarSubcoreMesh(axis_name="sequencer", num_cores=1),
                     name=name, scratch_types=sems, compiler_params=pltpu.CompilerParams(collective_id=collective_id))(operand)


def _rs_swap_halves(partial, name, collective_id=None):
    half = partial.shape[1] // 2

    def body(p_ref, r_ref, send, recv):
        x, y, c, _ = _position()
        theirs = pl.ds(pl.multiple_of((1 - c) * half, 16), half)
        cp = pltpu.make_async_remote_copy(src_ref=p_ref.at[:, theirs, :], dst_ref=r_ref, send_sem=send.at[0],
                                          recv_sem=recv.at[0], device_id=(x, y, 1 - c), device_id_type=MESH)
        cp.start()
        cp.wait()

    return _comm_call(body, lambda x, y, c: [(x, y, 1 - c)], jax.ShapeDtypeStruct((N_CHIPS, half, D_MODEL), BF16), 1,
                      partial, name, collective_id)


def _rs_add_halves(partial, other, core, name, after):
    half = other.shape[1]
    t = half // 2
    steps = half // t

    def body(core_ref, a_ref, b_ref, after_ref, o_ref):
        del after_ref
        o_ref[...] = (a_ref[...].astype(F32) + b_ref[...].astype(F32)).astype(BF16)

    return pl.pallas_call(
        body, name=name,
        grid_spec=pltpu.PrefetchScalarGridSpec(
            num_scalar_prefetch=1, grid=(N_CHIPS, steps),
            in_specs=[pl.BlockSpec((1, t, D_MODEL), lambda j, i, core_ref: (j, core_ref[0] * steps + i, 0)),
                      pl.BlockSpec((1, t, D_MODEL), lambda j, i, core_ref: (j, i, 0)), ANY],
            out_specs=pl.BlockSpec((1, t, D_MODEL), lambda j, i, core_ref: (j, i, 0))),
        out_shape=jax.ShapeDtypeStruct((N_CHIPS, half, D_MODEL), BF16),
        compiler_params=_params(n_axes=2),
    )(core, partial, other, after)


def _rs_exchange_chips(pre, name, collective_id=None):
    def body(s_ref, r_ref, send, recv):
        x, y, c, chips = _position()

        def copy(k, chunk, to):
            return pltpu.make_async_remote_copy(src_ref=s_ref.at[chunk], dst_ref=r_ref.at[k], send_sem=send.at[k],
                                                recv_sem=recv.at[k], device_id=to, device_id_type=MESH)

        sends = [copy(k, 2 * chip[0] + chip[1], (*chip, c)) for k, chip in enumerate(chips)]
        for cp in sends:
            cp.start()
        for cp in sends:
            cp.wait()

    return _comm_call(body, lambda x, y, c: [(1 - x, y, c), (x, 1 - y, c), (1 - x, 1 - y, c)],
                      jax.ShapeDtypeStruct((3, pre.shape[1], D_MODEL), BF16), 3, pre, name, collective_id)


def _rs_sum_chips(pre, received, place, name):
    half = pre.shape[1]
    t = half // 2 if half > 512 else half
    steps = half // t

    def body(place_ref, own_ref, r_ref, o_ref):
        acc = own_ref[0].astype(F32)
        for k in range(3):
            acc = acc + r_ref[k].astype(F32)
        o_ref[...] = acc

    return pl.pallas_call(
        body, name=name,
        grid_spec=pltpu.PrefetchScalarGridSpec(
            num_scalar_prefetch=1, grid=(steps,),
            in_specs=[pl.BlockSpec((1, t, D_MODEL), lambda i, place_ref: (place_ref[0], i, 0)),
                      pl.BlockSpec((3, t, D_MODEL), lambda i, place_ref: (0, i, 0))],
            out_specs=pl.BlockSpec((t, D_MODEL), lambda i, place_ref: (place_ref[1] * steps + i, 0))),
        out_shape=jax.ShapeDtypeStruct((2 * half, D_MODEL), F32),
        compiler_params=_params(),
    )(place, pre, received)


def _rs_finish(grads_a, grads_b, small):
    def body(fa_ref, fb_ref, s_ref, ga_ref, gb_ref, t_ref, send, recv, local_sem):
        del fa_ref, fb_ref
        x, y, c, chips = _position()
        sibling = (x, y, 1 - c)

        def slot(px, py, pc):
            return t_ref.at[4 * px + 2 * py + pc]

        def copy(k, block, to, src=None):
            return pltpu.make_async_remote_copy(src_ref=slot(*block) if src is None else src, dst_ref=slot(*block),
                                                send_sem=send.at[k], recv_sem=recv.at[k], device_id=to, device_id_type=MESH)

        def half_copies(core, to):
            out = []
            for k, g_ref in ((7, ga_ref), (8, gb_ref)):
                half = g_ref.shape[0] // 2
                rows = g_ref.at[pl.ds(pl.multiple_of(core * half, 8), half), :]
                out.append(pltpu.make_async_remote_copy(src_ref=rows, dst_ref=rows, send_sem=send.at[k], recv_sem=recv.at[k],
                                                        device_id=to, device_id_type=MESH))
            return out

        own_small = pltpu.make_async_copy(s_ref, slot(x, y, c), local_sem)
        own_small.start()
        to_sibling = half_copies(c, sibling)
        for cp in to_sibling:
            cp.start()
        first = [copy(0, (x, y, c), sibling, src=s_ref)]
        first += [copy(1 + k, (x, y, c), (*chip, c), src=s_ref) for k, chip in enumerate(chips)]
        for cp in first:
            cp.start()
        passed = []
        for k, chip in enumerate(chips):
            copy(1 + k, (*chip, c), (x, y, c)).wait_recv()
            fwd = copy(4 + k, (*chip, c), sibling)
            fwd.start()
            passed.append(fwd)
        copy(0, sibling, (x, y, c)).wait_recv()
        for k, chip in enumerate(chips):
            copy(4 + k, (*chip, 1 - c), (x, y, c)).wait_recv()
        for cp in half_copies(1 - c, (x, y, c)):
            cp.wait_recv()
        for cp in first + passed + to_sibling:
            cp.wait_send()
        own_small.wait()

    return pl.pallas_call(
        body, name="rs_finish", in_specs=[ANY, ANY, ANY], out_specs=[ANY, ANY, ANY], input_output_aliases={0: 0, 1: 1},
        out_shape=[jax.ShapeDtypeStruct(grads_a.shape, F32), jax.ShapeDtypeStruct(grads_b.shape, F32),
                   jax.ShapeDtypeStruct((N_DEV, SMALL_ROWS, 128), F32)],
        scratch_shapes=[pltpu.SemaphoreType.DMA((9,)), pltpu.SemaphoreType.DMA((9,)), pltpu.SemaphoreType.DMA],
    )(grads_a, grads_b, small)


def _adam_update(w, g, m, v):
    m_new = ADAM_B1 * m + (1.0 - ADAM_B1) * g
    v_new = ADAM_B2 * v + (1.0 - ADAM_B2) * (g * g)
    m_hat = m_new / (1.0 - ADAM_B1 ** ADAM_STEP)
    v_hat = v_new / (1.0 - ADAM_B2 ** ADAM_STEP)
    return -ADAM_LR * (m_hat / (jnp.sqrt(v_hat) + ADAM_EPS) + ADAM_WD * w), m_new, v_new


def _adamw(w, g_rows, row_off, m, v, name):
    rows, cols = w.shape
    t = rows if rows <= 320 else (rows // 2 if rows % 256 else 256)

    def body(w_ref, g_ref, m_ref, v_ref, go_ref, d_ref, nm_ref, nv_ref):
        g = g_ref[...]
        go_ref[...] = g
        d_ref[...], nm_ref[...], nv_ref[...] = _adam_update(w_ref[...], g, m_ref[...], v_ref[...])

    blk = pl.BlockSpec((t, cols), lambda i: (i, 0))
    assert row_off % 8 == 0 and t % 8 == 0
    g_blk = pl.BlockSpec((pl.Element(t), pl.Element(cols)), lambda i: (pl.multiple_of(row_off + i * t, 8), 0))
    shape = jax.ShapeDtypeStruct((rows, cols), F32)
    return pl.pallas_call(
        body, name=name, grid=(rows // t,), in_specs=[blk, g_blk, blk, blk], out_specs=[blk] * 4, out_shape=[shape] * 4,
        compiler_params=_params(),
    )(w, g_rows, m, v)


SMALL_PARAMS = [("g_attn", (1, D_MODEL), 8), ("g_q", (1, HEAD_DIM), None), ("g_k", (1, HEAD_DIM), None),
                ("sinks", (1, N_Q_HEADS), None), ("rel_bias", (N_BUCKETS, N_Q_HEADS), None), ("w_pool", (512, 128), None),
                ("pool_scale", (1, POOL_WIDTH), 4), ("g_ffn", (1, D_MODEL), 8), ("g_ple", (1, D_MODEL), 8)]


def _adamw_small(tables, wmv):
    n_par = len(SMALL_PARAMS)

    def body(*refs):
        t_ref = refs[0]
        ins = refs[1:1 + 3 * n_par]
        loss_ref = refs[1 + 3 * n_par]
        outs = refs[2 + 3 * n_par:-1]
        tot_ref = refs[-1]
        total = t_ref[0]
        for d in range(1, N_DEV):
            total = total + t_ref[d]
        tot_ref[...] = total
        loss_ref[...] = tot_ref[pl.ds(SMALL["loss"], 1), 0:1]
        for i, (name, shape, split) in enumerate(SMALL_PARAMS):
            g_ref, d_ref, nm_ref, nv_ref = outs[4 * i:4 * i + 4]
            row = SMALL[name]
            if split:
                for k in range(split):
                    g_ref[:, 128 * k:128 * k + 128] = tot_ref[pl.ds(row + k, 1), :]
            else:
                g_ref[...] = tot_ref[pl.ds(row, shape[0]), 0:shape[1]]
            w_ref, m_ref, v_ref = ins[3 * i:3 * i + 3]
            d_ref[...], nm_ref[...], nv_ref[...] = _adam_update(w_ref[...], g_ref[...], m_ref[...], v_ref[...])

    shapes = [jax.ShapeDtypeStruct((1, 1), F32)]
    for _, shape, _ in SMALL_PARAMS:
        shapes += [jax.ShapeDtypeStruct(shape, F32)] * 4
    flat = [a for triple in wmv for a in triple]
    res = pl.pallas_call(
        body, name="adamw_small", in_specs=[VMEM_WHOLE] * (1 + 3 * n_par), out_specs=[VMEM_WHOLE] * len(shapes),
        out_shape=shapes, scratch_shapes=[pltpu.VMEM((SMALL_ROWS, 128), F32)],
    )(tables, *flat)
    return res[0], [res[1 + 4 * i:5 + 4 * i] for i in range(n_par)]


def _pack_ple_proj(shard):
    return shard.reshape(4, 64, 256).transpose(1, 0, 2).reshape(64, D_MODEL)


class _Reduction:
    def __init__(self, tag, place, ids=(None, None)):
        self.tag, self.place, self.ids = tag, place, ids

    def start(self, partial):
        self.partial = partial
        self.other = _rs_swap_halves(partial, "rs_swap_" + self.tag, self.ids[0])
        return partial

    def middle(self, after):
        self.pre = _rs_add_halves(self.partial, self.other, self.place[1:], "rs_add_" + self.tag, after)
        self.received = _rs_exchange_chips(self.pre, "rs_exchange_" + self.tag, self.ids[1])
        return self.pre

    def finish(self):
        return _rs_sum_chips(self.pre, self.received, self.place, "rs_sum_" + self.tag)


def _local_grads(x2, p2, tgt, wts, g_attn_norm, g_q, g_k, attn_sinks, rel_bias, w_pool, pool_scale, g_ffn_norm, g_ple_norm,
                 reduce_a):
    part_in, part_out, part_rest, local_slab, me = wts
    w_in, w_out, w_late = (part_in, local_slab, me), (part_out, local_slab, me), (part_rest, local_slab, me)
    bucket = jnp.asarray(_bucket_table())
    gq = jnp.tile(g_q, (1, 2))
    gk = jnp.tile(g_k, (1, 2))
    wpool = w_pool[0].astype(BF16)
    sinks = attn_sinks[0]
    bias_st = _bias_build(rel_bias.T, bucket)

    hn1, zqk, u, kn, vb, qst = _attn_in(x2, g_attn_norm, gq, gk, w_in)
    ost = _attn_fwd(qst, kn, vb, bias_st, sinks)
    pooled, mix, h1, hn2 = _mix_out(u, ost, x2, w_out, wpool, pool_scale, g_ffn_norm)
    gate, up, h2 = _ffn_fwd(hn2, h1, w_late)
    loss_v, dh2, dgl, dpp, hn3, dg_ple = _ple_loss(h2, p2, tgt, w_late, g_ple_norm)

    dgate, dup, act, dh1, dg_ffn = _ffn_bwd(dh2, gate, up, h1, w_late, g_ffn_norm)
    rows_a = SLAB_ROWS - SLAB["inT"][1]
    partial_a = None
    for name, lhs, rhs in (("out", mix, dh1), ("gateT", dgate, hn2), ("upT", dup, hn2), ("down", act, dh2), ("plg", hn3, dgl)):
        partial_a = _dw(lhs, rhs, "dw_" + name, into=(partial_a, rows_a, SLAB[name][0] - SLAB["inT"][1]))
    dw_plp = _dw(p2, dpp, "dw_plp").reshape(4, 64, N_CHIPS, 256).transpose(2, 1, 0, 3).reshape(N_CHIPS, 64, D_MODEL)
    partial_a = reduce_a.start(lax.dynamic_update_slice(partial_a, dw_plp, (0, SLAB["plp"][0] - SLAB["inT"][1], 0)))
    dost, du, dyp, dscale = _mix_out_bwd(dh1, w_out, pooled, wpool, pool_scale, partial_a)
    pre_a = reduce_a.middle(du)
    dqst, dk, dv, dbias, dsink_rows = _attn_bwd(qst, kn, vb, dost, bias_st, sinks, pre_a)
    dz, dx, dg_attn, dgq, dgk = _attn_in_bwd(dqst, zqk, dk, dv, du, x2, dh1, w_in, g_attn_norm, gq, gk)

    partial_b = _dw(dz, hn1, "dw_in").reshape(N_CHIPS, -1, D_MODEL)
    small = _small_pack(dg_attn, dg_ffn, dg_ple, dscale, dgq, dgk, dbias, dsink_rows, bucket, loss_v, _dw_pool(pooled, dyp))
    return dx, partial_b, small


def kernel(x, p, w_in, w_out, g_attn_norm, g_q, g_k, attn_sinks, rel_bias, w_pool, pool_scale, g_ffn_norm, w_gate, w_up, w_down, g_ple_norm, w_ple_gate, w_ple_proj, loss_target, m_w_in, m_w_out, m_g_attn_norm, m_g_q, m_g_k, m_attn_sinks, m_rel_bias, m_w_pool, m_pool_scale, m_g_ffn_norm, m_w_gate, m_w_up, m_w_down, m_g_ple_norm, m_w_ple_gate, m_w_ple_proj, v_w_in, v_w_out, v_g_attn_norm, v_g_q, v_g_k, v_attn_sinks, v_rel_bias, v_w_pool, v_pool_scale, v_g_ffn_norm, v_w_gate, v_w_up, v_w_down, v_g_ple_norm, v_w_ple_gate, v_w_ple_proj):
    core = lax.axis_index("c").astype(jnp.int32).reshape(1)
    me = (2 * lax.axis_index("x") + lax.axis_index("y")).astype(jnp.int32).reshape(1)

    local_slab = jnp.concatenate(
        [w_in[0].T, w_out[0], w_gate[0].T, w_up[0].T, w_down[0], w_ple_gate[0], _pack_ple_proj(w_ple_proj[0])],
        axis=0).astype(BF16)
    gathered = [_ag_weights(local_slab, start, stop - start, name, collective_id)
                for (start, stop), name, collective_id in zip(GATHER_PARTS, ("ag_in", "ag_out", "ag_rest"), (1, 2, 5))]
    wts = (*gathered, local_slab, me)

    place = jnp.concatenate([me, core])
    reduce_a = _Reduction("a", place, ids=(3, 4))
    dx, partial_b, small = _local_grads(x[0], p[0, 0], loss_target[0], wts, g_attn_norm, g_q, g_k, attn_sinks, rel_bias,
                                        w_pool, pool_scale, g_ffn_norm, g_ple_norm, reduce_a)
    reduce_b = _Reduction("b", place)
    reduce_b.start(partial_b)
    reduce_b.middle(partial_b)
    grads_a, grads_b, small_all = _rs_finish(reduce_a.finish(), reduce_b.finish(), small)

    def rows(name):
        return (grads_b, 0) if name == "inT" else (grads_a, SLAB[name][0] - SLAB["inT"][1])

    plp_rows = grads_a[SLAB["plp"][0] - SLAB["inT"][1]:]
    big = {
        "w_in": (w_in, m_w_in, v_w_in, rows("inT"), True),
        "w_out": (w_out, m_w_out, v_w_out, rows("out"), False),
        "w_gate": (w_gate, m_w_gate, v_w_gate, rows("gateT"), True),
        "w_up": (w_up, m_w_up, v_w_up, rows("upT"), True),
        "w_down": (w_down, m_w_down, v_w_down, rows("down"), False),
        "w_ple_gate": (w_ple_gate, m_w_ple_gate, v_w_ple_gate, rows("plg"), False),
        "w_ple_proj": (w_ple_proj, m_w_ple_proj, v_w_ple_proj,
                       (plp_rows.reshape(64, 4, 256).transpose(1, 0, 2).reshape(PLE_DIM, PLE_DIM), 0), False),
    }
    small_params = {
        "g_attn_norm": (g_attn_norm, m_g_attn_norm, v_g_attn_norm), "g_q": (g_q, m_g_q, v_g_q), "g_k": (g_k, m_g_k, v_g_k),
        "attn_sinks": (attn_sinks, m_attn_sinks, v_attn_sinks), "rel_bias": (rel_bias, m_rel_bias, v_rel_bias),
        "w_pool": tuple(a.reshape(512, 128) for a in (w_pool, m_w_pool, v_w_pool)),
        "pool_scale": (pool_scale, m_pool_scale, v_pool_scale), "g_ffn_norm": (g_ffn_norm, m_g_ffn_norm, v_g_ffn_norm),
        "g_ple_norm": (g_ple_norm, m_g_ple_norm, v_g_ple_norm),
    }

    grads, deltas, new_ms, new_vs = {}, {}, {}, {}
    for name, (w, m, v, (g_rows, row_off), transposed) in big.items():
        view = (lambda a: a.T) if transposed else (lambda a: a)
        out = _adamw(view(w[0]), g_rows, row_off, view(m[0]), view(v[0]), "adamw_" + name)
        grads[name], deltas[name], new_ms[name], new_vs[name] = (view(a)[None] for a in out)

    loss, small_out = _adamw_small(small_all, list(small_params.values()))
    for name, (g2, d, nm, nv) in zip(small_params, small_out):
        shape = w_pool.shape if name == "w_pool" else g2.shape
        grads[name], deltas[name], new_ms[name], new_vs[name] = (a.reshape(shape) for a in (g2, d, nm, nv))

    order = ["w_in", "w_out", "g_attn_norm", "g_q", "g_k", "attn_sinks", "rel_bias", "w_pool", "pool_scale", "g_ffn_norm",
             "w_gate", "w_up", "w_down", "g_ple_norm", "w_ple_gate", "w_ple_proj"]
    return (loss.reshape(()), dx[None], *[grads[n] for n in order], *[deltas[n] for n in order],
            *[new_ms[n] for n in order], *[new_vs[n] for n in order])
```

```python
import functools

import numpy as np
import jax
import jax.numpy as jnp
from jax import lax
from jax.experimental import pallas as pl
from jax.experimental.pallas import tpu as pltpu
from jax.experimental.pallas import tpu_sc as plsc

F32 = jnp.float32
BF16 = jnp.bfloat16
MESH = pl.DeviceIdType.MESH

D_MODEL = 1024
HEAD_DIM = 64
N_Q_HEADS = 8
ATTN_WIDTH = 512
KV_WIDTH = 128
POOL_WIDTH = 512
IN_WIDTH = 1280
D_FF = 2816
PLE_DIM = 256
FF_CHUNK = 704
BLOCK = 128
N_BUCKETS = 32
MAX_DISTANCE = 128
POOL_SIZES = (2, 4, 8, 16)
EPS = 1e-6
NEG = -1e30
N_CHIPS = 4
N_DEV = 8

ADAM_LR = 0.001
ADAM_B1 = 0.9
ADAM_B2 = 0.999
ADAM_EPS = 1e-08
ADAM_WD = 0.01
ADAM_STEP = 10

SLAB = {"inT": (0, 320), "out": (320, 256), "gateT": (576, 704), "upT": (1280, 704), "down": (1984, 704),
        "plg": (2688, 256), "plp": (2944, 64)}
SLAB_ROWS = 3008
HALF_ROWS = SLAB_ROWS // 2
GATHER_PARTS = ((0, 320), (320, 576), (576, SLAB_ROWS))
POOL_HALO = 24

SMALL = {"g_attn": 0, "g_ffn": 8, "g_ple": 16, "pool_scale": 24, "g_q": 28, "g_k": 29, "sinks": 30, "loss": 31,
         "rel_bias": 32, "w_pool": 64}
SMALL_ROWS = 576

VMEM_LIMIT_BIG = 60 * 1024 * 1024
VMEM_LIMIT = 48 * 1024 * 1024


def _params(vmem=VMEM_LIMIT, n_axes=1):
    return pltpu.CompilerParams(dimension_semantics=("arbitrary",) * n_axes, vmem_limit_bytes=vmem)


def _dot(a, b, ca, cb):
    return lax.dot_general(a, b, (((ca,), (cb,)), ((), ())), preferred_element_type=F32)


def _full(shape):
    return pl.BlockSpec(shape, lambda i: (0,) * len(shape))


ANY = pl.BlockSpec(memory_space=pl.ANY)
VMEM_WHOLE = pl.BlockSpec(memory_space=pltpu.VMEM)


W_SPECS = [ANY, ANY, pl.BlockSpec(memory_space=pltpu.SMEM)]


def _load_rows(w_refs, name, dst_ref, sems):
    slab_ref, local_ref, me_ref = w_refs
    off, rows = SLAB[name]
    slab_off = off - max(start for start, _ in GATHER_PARTS if start <= off)
    me = me_ref[0]
    for phase in ("start", "wait"):
        for j in range(N_CHIPS):
            dst = dst_ref.at[pl.ds(j * rows, rows), :]
            theirs = pltpu.make_async_copy(slab_ref.at[j, pl.ds(slab_off, rows), :], dst, sems.at[j])
            own = pltpu.make_async_copy(local_ref.at[pl.ds(off, rows), :], dst, sems.at[j])

            @pl.when(me == j)
            def _():
                getattr(own, phase)()

            @pl.when(me != j)
            def _():
                getattr(theirs, phase)()


def _rms_fwd(x, g):
    r = lax.rsqrt(jnp.mean(x * x, axis=-1, keepdims=True) + EPS)
    return x * r * g


def _rms_bwd(x, g, dy):
    r = lax.rsqrt(jnp.mean(x * x, axis=-1, keepdims=True) + EPS)
    xn = x * r
    dyg = dy * g
    dx = r * (dyg - xn * jnp.mean(dyg * xn, axis=-1, keepdims=True))
    return dx, jnp.sum(dy * xn, axis=0, keepdims=True)


def _half_sum(v, lo):
    s_lo = jnp.sum(jnp.where(lo, v, 0.0), axis=-1, keepdims=True)
    s_hi = jnp.sum(jnp.where(lo, 0.0, v), axis=-1, keepdims=True)
    return jnp.where(lo, s_lo, s_hi)


def _half_sum_mxu(v):
    upper = lax.broadcasted_iota(jnp.int32, (128, 128), 0) < 64
    left = lax.broadcasted_iota(jnp.int32, (128, 128), 1) < 64
    ones = jnp.where(upper == left, 1.0, 0.0).astype(BF16)
    high = v.astype(BF16)
    low = (v - high.astype(F32)).astype(BF16)
    return _dot(high, ones, 1, 0) + _dot(low, ones, 1, 0)


def _pair_norm(zp, g, lo):
    r = lax.rsqrt(_half_sum(zp * zp, lo) * (1.0 / HEAD_DIM) + EPS)
    return zp * r * g


def _pair_norm_bwd(zp, g, dy):
    r = lax.rsqrt(_half_sum_mxu(zp * zp) * (1.0 / HEAD_DIM) + EPS)
    xn = zp * r
    dyg = dy * g
    dx = r * (dyg - xn * (_half_sum_mxu(dyg * xn) * (1.0 / HEAD_DIM)))
    return dx, jnp.sum(dy * xn, axis=0, keepdims=True)


def _to_stacked(pair, group, lo):
    rolled = pltpu.roll(pair, 64, axis=1)
    if group == 0:
        return jnp.where(lo, pair, 0.0), jnp.where(lo, rolled, 0.0)
    return jnp.where(lo, 0.0, rolled), jnp.where(lo, 0.0, pair)


def _from_stacked(even, odd, group, lo):
    if group == 0:
        return jnp.where(lo, even, pltpu.roll(odd, 64, axis=1))
    return jnp.where(lo, pltpu.roll(even, 64, axis=1), odd)


def _sigmoid(v):
    return 1.0 / (1.0 + jnp.exp(-v))


def _pool_counts(tile, n_rows):
    t1 = tile * n_rows + lax.broadcasted_iota(jnp.int32, (n_rows, POOL_WIDTH), 0) + 1
    lane = lax.broadcasted_iota(jnp.int32, (n_rows, POOL_WIDTH), 1)
    win = jnp.where(lane < 128, 2, jnp.where(lane < 256, 4, jnp.where(lane < 384, 8, 16)))
    return jnp.minimum(t1, win).astype(F32)


def _attn_in(x2, g_attn, gq, gk, wts):
    s_len = x2.shape[0]
    t = 512

    def body(x_ref, g_ref, gq_ref, gk_ref, sl_ref, lo_ref, me_ref, hn_ref, zqk_ref, u_ref, kn_ref, v_ref, qst_ref, w_ref, sems):
        @pl.when(pl.program_id(0) == 0)
        def _():
            _load_rows((sl_ref, lo_ref, me_ref), "inT", w_ref, sems)

        hn = _rms_fwd(x_ref[...], g_ref[...]).astype(BF16)
        hn_ref[...] = hn
        z = _dot(hn, w_ref[...], 1, 1)
        zqk_ref[...] = z[:, :640]
        u_ref[...] = z[:, 768:]
        v_ref[...] = z[:, 640:768].astype(BF16)
        lo = lax.broadcasted_iota(jnp.int32, (t, 128), 1) < 64
        kn_ref[...] = _pair_norm(z[:, 512:640], gk_ref[...], lo).astype(BF16)
        for p in range(4):
            qn = _pair_norm(z[:, 128 * p:128 * p + 128], gq_ref[...], lo)
            even, odd = _to_stacked(qn, p // 2, lo)
            qst_ref[2 * p] = even.astype(BF16)
            qst_ref[2 * p + 1] = odd.astype(BF16)

    row = lambda w: pl.BlockSpec((t, w), lambda i: (i, 0))
    return pl.pallas_call(
        body, name="attn_in", grid=(s_len // t,),
        in_specs=[row(D_MODEL), _full((1, D_MODEL)), _full((1, 128)), _full((1, 128))] + W_SPECS,
        out_specs=[row(D_MODEL), row(640), row(POOL_WIDTH), row(128), row(128),
                   pl.BlockSpec((N_Q_HEADS, t, 128), lambda i: (0, i, 0))],
        out_shape=[jax.ShapeDtypeStruct((s_len, D_MODEL), BF16), jax.ShapeDtypeStruct((s_len, 640), F32),
                   jax.ShapeDtypeStruct((s_len, POOL_WIDTH), F32), jax.ShapeDtypeStruct((s_len, 128), BF16),
                   jax.ShapeDtypeStruct((s_len, 128), BF16), jax.ShapeDtypeStruct((N_Q_HEADS, s_len, 128), BF16)],
        scratch_shapes=[pltpu.VMEM((IN_WIDTH, D_MODEL), BF16), pltpu.SemaphoreType.DMA((N_CHIPS,))],
        compiler_params=_params(),
    )(x2, g_attn, gq, gk, *wts)


def _bucket_table():
    i_idx = np.arange(BLOCK)[:, None]
    j_idx = np.arange(2 * BLOCK)[None, :]
    d = BLOCK + i_idx - j_idx
    n = np.maximum(d, 0)
    max_exact = N_BUCKETS // 2
    nf = np.maximum(n, 1).astype(np.float64)
    large = max_exact + (np.log(nf / max_exact) / np.log(MAX_DISTANCE / max_exact) * (N_BUCKETS - max_exact)).astype(np.int64)
    large = np.minimum(large, N_BUCKETS - 1)
    bucket = np.where(n < max_exact, n, large)
    return np.where((d >= 0) & (d < BLOCK), bucket, -1).astype(np.int32)


def _bias_build(rel_bias_t, bucket):
    def body(rb_ref, bucket_ref, out_ref):
        bk = bucket_ref[...]
        for h in range(N_Q_HEADS):
            acc = jnp.full((BLOCK, 2 * BLOCK), NEG, F32)
            for b in range(N_BUCKETS):
                acc = jnp.where(bk == b, rb_ref[h, b], acc)
            out_ref[0, pl.ds(h * BLOCK, BLOCK), :] = acc
            out_ref[1, pl.ds(h * BLOCK, BLOCK), :] = acc
            out_ref[1, pl.ds(h * BLOCK, BLOCK), 0:BLOCK] = jnp.full((BLOCK, BLOCK), NEG, F32)

    return pl.pallas_call(
        body, name="bias_build",
        in_specs=[pl.BlockSpec(memory_space=pltpu.SMEM), VMEM_WHOLE], out_specs=VMEM_WHOLE,
        out_shape=jax.ShapeDtypeStruct((2, N_Q_HEADS * BLOCK, 2 * BLOCK), F32),
    )(rel_bias_t, bucket)


def _head_softmax(s_ref, bias_ref, sink_ref, h):
    rows = pl.ds(pl.multiple_of(h * BLOCK, BLOCK), BLOCK)
    s = s_ref[rows, :] * (HEAD_DIM ** -0.5) + bias_ref[rows, :]
    sink = sink_ref[h]
    m = jnp.maximum(jnp.max(s, axis=-1, keepdims=True), sink)
    p = jnp.exp(s - m)
    e_sink = jnp.exp(sink - m)
    inv = 1.0 / (jnp.sum(p, axis=-1, keepdims=True) + e_sink)
    return rows, p * inv, e_sink * inv


def _attn_specs():
    prev = lambda i: (jnp.maximum(i - 1, 0), 0)
    cur = lambda i: (i, 0)
    stacked = pl.BlockSpec((N_Q_HEADS, BLOCK, 128), lambda i: (0, i, 0))
    kv = [pl.BlockSpec((BLOCK, 128), prev), pl.BlockSpec((BLOCK, 128), cur)]
    consts = [pl.BlockSpec((None, N_Q_HEADS * BLOCK, 2 * BLOCK), lambda i: (jnp.where(i == 0, 1, 0), 0, 0)),
              pl.BlockSpec(memory_space=pltpu.SMEM)]
    return stacked, kv, consts


def _head_lane_mask():
    rows = lax.broadcasted_iota(jnp.int32, (N_Q_HEADS * BLOCK, 128), 0)
    lanes = lax.broadcasted_iota(jnp.int32, (N_Q_HEADS * BLOCK, 128), 1)
    return (rows < 4 * BLOCK) == (lanes < 64)


def _attn_fwd(qst, kn, vb, bias_st, sinks):
    s_len = kn.shape[0]

    def body(q_ref, kp_ref, kc_ref, vp_ref, vc_ref, bias_ref, sink_ref, o_ref, s_ref, p_ref):
        q = q_ref[...].reshape(N_Q_HEADS * BLOCK, 128)
        s_ref[...] = _dot(q, jnp.concatenate([kp_ref[...], kc_ref[...]], axis=0), 1, 1)

        def head(h, carry):
            rows, probs, _ = _head_softmax(s_ref, bias_ref, sink_ref, h)
            p_ref[rows, :] = probs.astype(BF16)
            return carry

        lax.fori_loop(0, N_Q_HEADS, head, 0, unroll=True)
        o = _dot(p_ref[...], jnp.concatenate([vp_ref[...], vc_ref[...]], axis=0), 1, 0)
        o_ref[...] = jnp.where(_head_lane_mask(), o, 0.0).astype(BF16).reshape(N_Q_HEADS, BLOCK, 128)

    stacked, kv, consts = _attn_specs()
    return pl.pallas_call(
        body, name="attn_fwd", grid=(s_len // BLOCK,),
        in_specs=[stacked] + kv + kv + consts, out_specs=stacked,
        out_shape=jax.ShapeDtypeStruct((N_Q_HEADS, s_len, 128), BF16),
        scratch_shapes=[pltpu.VMEM((N_Q_HEADS * BLOCK, 2 * BLOCK), F32), pltpu.VMEM((N_Q_HEADS * BLOCK, 2 * BLOCK), BF16)],
        compiler_params=_params(),
    )(qst, kn, kn, vb, vb, bias_st, sinks)


def _mix_out(u, ost, x2, wts, wpool, pool_scale, g_ffn):
    s_len = x2.shape[0]
    t = 512
    n = t + 16

    def body(u_ref, o_ref, x_ref, sl_ref, lo_ref, me_ref, wp_ref, sc_ref, g_ref, pooled_ref, mix_ref, h1_ref, hn_ref,
             w_ref, ext_ref, st_ref, sems):
        i = pl.program_id(0)

        @pl.when(i == 0)
        def _():
            _load_rows((sl_ref, lo_ref, me_ref), "out", w_ref, sems)
            ext_ref[...] = jnp.zeros_like(ext_ref)
            st_ref[...] = jnp.zeros_like(st_ref)

        u_tile = u_ref[...]
        ext_ref[pl.ds(POOL_HALO, t), :] = u_tile
        st_ref[pl.ds(8, n), :] = ext_ref[pl.ds(8, n), :] + ext_ref[pl.ds(7, n), :]
        st_ref[pl.ds(8, n), 128:] = st_ref[pl.ds(8, n), 128:] + st_ref[pl.ds(6, n), 128:]
        st_ref[pl.ds(8, n), 256:] = st_ref[pl.ds(8, n), 256:] + st_ref[pl.ds(4, n), 256:]
        st_ref[pl.ds(8, n), 384:] = st_ref[pl.ds(8, n), 384:] + st_ref[pl.ds(0, n), 384:]
        ext_ref[pl.ds(0, POOL_HALO), :] = ext_ref[pl.ds(t, POOL_HALO), :]
        pooled = (st_ref[pl.ds(POOL_HALO, t), :] / _pool_counts(i, t) - u_tile).astype(BF16)
        pooled_ref[...] = pooled
        for g in range(4):
            cols = slice(128 * g, 128 * g + 128)
            y = _dot(pooled[:, cols], wp_ref[g], 1, 0) * sc_ref[:, cols]
            mix_ref[:, ATTN_WIDTH + 128 * g:ATTN_WIDTH + 128 * g + 128] = y.astype(BF16)
        lo = lax.broadcasted_iota(jnp.int32, (t, 128), 1) < 64
        for p in range(4):
            a = _from_stacked(o_ref[2 * p].astype(F32), o_ref[2 * p + 1].astype(F32), p // 2, lo)
            mix_ref[:, 128 * p:128 * p + 128] = a.astype(BF16)
        h1 = x_ref[...] + _dot(mix_ref[...], w_ref[...], 1, 0)
        h1_ref[...] = h1
        hn_ref[...] = _rms_fwd(h1, g_ref[...]).astype(BF16)

    row = lambda w: pl.BlockSpec((t, w), lambda i: (i, 0))
    return pl.pallas_call(
        body, name="mix_out", grid=(s_len // t,),
        in_specs=[row(POOL_WIDTH), pl.BlockSpec((N_Q_HEADS, t, 128), lambda i: (0, i, 0)), row(D_MODEL)] + W_SPECS
        + [_full((4, 128, 128)), _full((1, POOL_WIDTH)), _full((1, D_MODEL))],
        out_specs=[row(POOL_WIDTH), row(D_MODEL), row(D_MODEL), row(D_MODEL)],
        out_shape=[jax.ShapeDtypeStruct((s_len, POOL_WIDTH), BF16), jax.ShapeDtypeStruct((s_len, D_MODEL), BF16),
                   jax.ShapeDtypeStruct((s_len, D_MODEL), F32), jax.ShapeDtypeStruct((s_len, D_MODEL), BF16)],
        scratch_shapes=[pltpu.VMEM((D_MODEL, D_MODEL), BF16), pltpu.VMEM((t + POOL_HALO, POOL_WIDTH), F32),
                        pltpu.VMEM((t + POOL_HALO, POOL_WIDTH), F32), pltpu.SemaphoreType.DMA((N_CHIPS,))],
        compiler_params=_params(),
    )(u, ost, x2, *wts, wpool, pool_scale, g_ffn)


def _ffn_ple(hn2, h1, p2, tgt, wts, g_ffn, g_ple):
    s_len = h1.shape[0]
    t = 256
    n_tiles = s_len // t

    def body(hn_ref, h1_ref, p_ref, tgt_ref, sl_ref, lo_ref, me_ref, gf_ref, gp_ref,
             loss_ref, dgate_ref, dup_ref, act_ref, dh2b_ref, hn3_ref, dgl_ref, dpp_ref, dh1_ref, dgf_ref, dgp_ref,
             wg_ref, wu_ref, wd_ref, wl_ref, wp_ref, packed_ref, gate_s, up_s, loss_acc, sems):
        i = pl.program_id(0)

        @pl.when(i == 0)
        def _():
            w_refs = (sl_ref, lo_ref, me_ref)
            _load_rows(w_refs, "gateT", wg_ref, sems)
            _load_rows(w_refs, "upT", wu_ref, sems)
            _load_rows(w_refs, "down", wd_ref, sems)
            _load_rows(w_refs, "plg", wl_ref, sems)
            _load_rows(w_refs, "plp", packed_ref, sems)
            for j in range(N_CHIPS):
                for q in range(4):
                    wp_ref[pl.ds(64 * q, 64), 256 * j:256 * j + 256] = packed_ref[pl.ds(64 * j, 64), 256 * q:256 * q + 256]
            loss_acc[...] = jnp.zeros_like(loss_acc)
            dgf_ref[...] = jnp.zeros_like(dgf_ref)
            dgp_ref[...] = jnp.zeros_like(dgp_ref)

        hn = hn_ref[...]
        h1v = h1_ref[...]
        h2 = h1v
        for ch in range(N_CHIPS):
            rows = pl.ds(ch * FF_CHUNK, FF_CHUNK)
            gate = _dot(hn, wg_ref[rows, :], 1, 1)
            up = _dot(hn, wu_ref[rows, :], 1, 1)
            gate_s[ch] = gate
            up_s[ch] = up
            act = (gate * _sigmoid(gate) * up).astype(BF16)
            act_ref[ch] = act
            h2 = h2 + _dot(act, wd_ref[rows, :], 1, 0)
        gp = gp_ref[...]
        hn3 = _rms_fwd(h2, gp).astype(BF16)
        hn3_ref[...] = hn3
        gate2 = _sigmoid(_dot(hn3, wl_ref[...], 1, 0))
        pp = _dot(p_ref[...].astype(BF16), wp_ref[...], 1, 0)
        err = h2 + gate2 * pp - tgt_ref[...]
        loss_acc[...] += jnp.sum(err * err, axis=0, keepdims=True)
        dy = err * (1.0 / D_MODEL)
        dpp_ref[...] = (dy * gate2).astype(BF16)
        dgl = (dy * pp * gate2 * (1.0 - gate2)).astype(BF16)
        dgl_ref[...] = dgl
        dx3, dg3 = _rms_bwd(h2, gp, _dot(dgl, wl_ref[...], 1, 1))
        dh2 = dy + dx3
        dgp_ref[...] += dg3
        dh2b = dh2.astype(BF16)
        dh2b_ref[...] = dh2b
        dhn = jnp.zeros((t, D_MODEL), F32)
        for ch in range(N_CHIPS):
            rows = pl.ds(ch * FF_CHUNK, FF_CHUNK)
            dact = _dot(dh2b, wd_ref[rows, :], 1, 1)
            gate_v = gate_s[ch]
            up_v = up_s[ch]
            sg = _sigmoid(gate_v)
            dup = (dact * (gate_v * sg)).astype(BF16)
            dgate = (dact * up_v * (sg * (1.0 + gate_v * (1.0 - sg)))).astype(BF16)
            dup_ref[ch] = dup
            dgate_ref[ch] = dgate
            dhn = dhn + _dot(dgate, wg_ref[rows, :], 1, 0) + _dot(dup, wu_ref[rows, :], 1, 0)
        dx, dg = _rms_bwd(h1v, gf_ref[...], dhn)
        dh1_ref[...] = dh2 + dx
        dgf_ref[...] += dg

        @pl.when(i == n_tiles - 1)
        def _():
            total = jnp.sum(loss_acc[...], axis=-1, keepdims=True) * (0.5 / D_MODEL)
            loss_ref[...] = jnp.broadcast_to(total, loss_ref.shape)

    row = lambda w: pl.BlockSpec((t, w), lambda i: (i, 0))
    chunked = pl.BlockSpec((N_CHIPS, t, FF_CHUNK), lambda i: (0, i, 0))
    vec = _full((1, D_MODEL))
    act_shape = jax.ShapeDtypeStruct((N_CHIPS, s_len, FF_CHUNK), BF16)
    tok = lambda dtype: jax.ShapeDtypeStruct((s_len, D_MODEL), dtype)
    return pl.pallas_call(
        body, name="ffn_ple", grid=(n_tiles,),
        in_specs=[row(D_MODEL), row(D_MODEL), row(PLE_DIM), row(D_MODEL)] + W_SPECS + [vec, vec],
        out_specs=[_full((1, 128)), chunked, chunked, chunked] + [row(D_MODEL)] * 5 + [vec, vec],
        out_shape=[jax.ShapeDtypeStruct((1, 128), F32), act_shape, act_shape, act_shape, tok(BF16), tok(BF16), tok(BF16),
                   tok(BF16), tok(F32), jax.ShapeDtypeStruct((1, D_MODEL), F32), jax.ShapeDtypeStruct((1, D_MODEL), F32)],
        scratch_shapes=[pltpu.VMEM((D_FF, D_MODEL), BF16)] * 3
        + [pltpu.VMEM((D_MODEL, D_MODEL), BF16), pltpu.VMEM((PLE_DIM, D_MODEL), BF16), pltpu.VMEM((PLE_DIM, D_MODEL), BF16),
           pltpu.VMEM((N_CHIPS, t, FF_CHUNK), F32), pltpu.VMEM((N_CHIPS, t, FF_CHUNK), F32), pltpu.VMEM((1, D_MODEL), F32),
           pltpu.SemaphoreType.DMA((N_CHIPS,))],
        compiler_params=_params(VMEM_LIMIT_BIG),
    )(hn2, h1, p2, tgt, *wts, g_ffn, g_ple)


def _mix_out_bwd(dh1, wts, pooled, wpool, pool_scale, after):
    s_len = dh1.shape[0]
    t = 512
    n = t + 16
    n_tiles = s_len // t

    def body(dh1_ref, sl_ref, lo_ref, me_ref, pooled_ref, wp_ref, sc_ref, after_ref, dost_ref, du_ref, dyp_ref, dsc_ref,
             w_ref, ext_ref, st_ref, sems):
        del after_ref
        i = pl.program_id(0)

        @pl.when(i == 0)
        def _():
            _load_rows((sl_ref, lo_ref, me_ref), "out", w_ref, sems)
            ext_ref[...] = jnp.zeros_like(ext_ref)
            st_ref[...] = jnp.zeros_like(st_ref)
            dsc_ref[...] = jnp.zeros_like(dsc_ref)

        dmix = _dot(dh1_ref[...].astype(BF16), w_ref[...], 1, 1)
        lo = lax.broadcasted_iota(jnp.int32, (t, 128), 1) < 64
        for p in range(4):
            even, odd = _to_stacked(dmix[:, 128 * p:128 * p + 128], p // 2, lo)
            dost_ref[2 * p] = even.astype(BF16)
            dost_ref[2 * p + 1] = odd.astype(BF16)
        pooled_v = pooled_ref[...]
        counts = _pool_counts(n_tiles - 1 - i, t)
        for g in range(4):
            cols = slice(128 * g, 128 * g + 128)
            dm = dmix[:, ATTN_WIDTH + 128 * g:ATTN_WIDTH + 128 * g + 128]
            ypre = _dot(pooled_v[:, cols], wp_ref[g], 1, 0)
            dsc_ref[:, cols] += jnp.sum(ypre * dm, axis=0, keepdims=True)
            dyp = (dm * sc_ref[:, cols]).astype(BF16)
            dyp_ref[:, cols] = dyp
            dpooled = _dot(dyp, wp_ref[g], 1, 1)
            du_ref[:, cols] = -dpooled
            ext_ref[pl.ds(0, t), cols] = dpooled / counts[:, cols]
        st_ref[pl.ds(0, n), :] = ext_ref[pl.ds(0, n), :] + ext_ref[pl.ds(1, n), :]
        st_ref[pl.ds(0, n), 128:] = st_ref[pl.ds(0, n), 128:] + st_ref[pl.ds(2, n), 128:]
        st_ref[pl.ds(0, n), 256:] = st_ref[pl.ds(0, n), 256:] + st_ref[pl.ds(4, n), 256:]
        st_ref[pl.ds(0, n), 384:] = st_ref[pl.ds(0, n), 384:] + st_ref[pl.ds(8, n), 384:]
        ext_ref[pl.ds(t, POOL_HALO), :] = ext_ref[pl.ds(0, POOL_HALO), :]
        du_ref[...] += st_ref[pl.ds(0, t), :]

    rev = lambda w: pl.BlockSpec((t, w), lambda i: (n_tiles - 1 - i, 0))
    return pl.pallas_call(
        body, name="mix_out_bwd", grid=(n_tiles,),
        in_specs=[rev(D_MODEL)] + W_SPECS + [rev(POOL_WIDTH), _full((4, 128, 128)), _full((1, POOL_WIDTH)), ANY],
        out_specs=[pl.BlockSpec((N_Q_HEADS, t, 128), lambda i: (0, n_tiles - 1 - i, 0)), rev(POOL_WIDTH), rev(POOL_WIDTH),
                   _full((1, POOL_WIDTH))],
        out_shape=[jax.ShapeDtypeStruct((N_Q_HEADS, s_len, 128), BF16), jax.ShapeDtypeStruct((s_len, POOL_WIDTH), F32),
                   jax.ShapeDtypeStruct((s_len, POOL_WIDTH), BF16), jax.ShapeDtypeStruct((1, POOL_WIDTH), F32)],
        scratch_shapes=[pltpu.VMEM((D_MODEL, D_MODEL), BF16), pltpu.VMEM((t + POOL_HALO, POOL_WIDTH), F32),
                        pltpu.VMEM((t + POOL_HALO, POOL_WIDTH), F32), pltpu.SemaphoreType.DMA((N_CHIPS,))],
        compiler_params=_params(),
    )(dh1, *wts, pooled, wpool, pool_scale, after)


def _attn_bwd(qst, kn, vb, dost, bias_st, sinks, after):
    s_len = kn.shape[0]

    def body(q_ref, kp_ref, kc_ref, vp_ref, vc_ref, do_ref, bias_ref, sink_ref, after_ref, dq_ref, dk_ref, dv_ref, dbias_ref,
             dsink_ref, s_ref, dp_ref, p_ref, dl_ref):
        del after_ref
        i = pl.program_id(0)

        @pl.when(i == 0)
        def _():
            dk_ref[...] = jnp.zeros_like(dk_ref)
            dv_ref[...] = jnp.zeros_like(dv_ref)
            dbias_ref[...] = jnp.zeros_like(dbias_ref)
            dsink_ref[...] = jnp.zeros_like(dsink_ref)

        q = q_ref[...].reshape(N_Q_HEADS * BLOCK, 128)
        do = do_ref[...].reshape(N_Q_HEADS * BLOCK, 128)
        k2 = jnp.concatenate([kp_ref[...], kc_ref[...]], axis=0)
        s_ref[...] = _dot(q, k2, 1, 1)
        dp_ref[...] = _dot(do, jnp.concatenate([vp_ref[...], vc_ref[...]], axis=0), 1, 1)

        def head(h, carry):
            rows, probs, p_sink = _head_softmax(s_ref, bias_ref, sink_ref, h)
            dp = dp_ref[rows, :]
            dsum = jnp.sum(probs * dp, axis=-1, keepdims=True)
            dlog = probs * (dp - dsum)
            dsink_ref[rows, :] -= p_sink * dsum
            dbias_ref[rows, :] += dlog
            p_ref[rows, :] = probs.astype(BF16)
            dl_ref[rows, :] = (dlog * (HEAD_DIM ** -0.5)).astype(BF16)
            return carry

        lax.fori_loop(0, N_Q_HEADS, head, 0, unroll=True)
        dlog_s = dl_ref[...]
        dq_ref[...] = jnp.where(_head_lane_mask(), _dot(dlog_s, k2, 1, 0), 0.0).reshape(N_Q_HEADS, BLOCK, 128)
        dk2 = _dot(dlog_s, q, 0, 0)
        dv2 = _dot(p_ref[...], do, 0, 0)
        prev_rows = pl.ds(pl.multiple_of(jnp.maximum(i - 1, 0) * BLOCK, BLOCK), BLOCK)
        cur_rows = pl.ds(pl.multiple_of(i * BLOCK, BLOCK), BLOCK)
        dk_ref[prev_rows, :] += dk2[:BLOCK]
        dk_ref[cur_rows, :] += dk2[BLOCK:]
        dv_ref[prev_rows, :] += dv2[:BLOCK]
        dv_ref[cur_rows, :] += dv2[BLOCK:]

    stacked, kv, consts = _attn_specs()
    band = (N_Q_HEADS * BLOCK, 2 * BLOCK)
    return pl.pallas_call(
        body, name="attn_bwd", grid=(s_len // BLOCK,),
        in_specs=[stacked] + kv + kv + [stacked] + consts + [ANY],
        out_specs=[stacked, _full((s_len, 128)), _full((s_len, 128)), _full(band), _full((N_Q_HEADS * BLOCK, 1))],
        out_shape=[jax.ShapeDtypeStruct((N_Q_HEADS, s_len, 128), F32), jax.ShapeDtypeStruct((s_len, 128), F32),
                   jax.ShapeDtypeStruct((s_len, 128), F32), jax.ShapeDtypeStruct(band, F32),
                   jax.ShapeDtypeStruct((N_Q_HEADS * BLOCK, 1), F32)],
        scratch_shapes=[pltpu.VMEM(band, F32), pltpu.VMEM(band, F32), pltpu.VMEM(band, BF16), pltpu.VMEM(band, BF16)],
        compiler_params=_params(),
    )(qst, kn, kn, vb, vb, dost, bias_st, sinks, after)


def _small_pack(dg_attn, dg_ffn, dg_ple, dscale, dgq, dgk, dbias, dsink_rows, bucket, loss_v, dwpool):
    def body(ga_ref, gf_ref, gp_ref, sc_ref, gq_ref, gk_ref, db_ref, ds_ref, bucket_ref, loss_ref, wp_ref, out_ref):
        out_ref[pl.ds(0, SMALL["w_pool"]), :] = jnp.zeros((SMALL["w_pool"], 128), F32)
        for name, ref, n in (("g_attn", ga_ref, 8), ("g_ffn", gf_ref, 8), ("g_ple", gp_ref, 8), ("pool_scale", sc_ref, 4)):
            for k in range(n):
                out_ref[pl.ds(SMALL[name] + k, 1), :] = ref[:, 128 * k:128 * k + 128]
        for name, ref in (("g_q", gq_ref), ("g_k", gk_ref)):
            both = ref[...]
            out_ref[pl.ds(SMALL[name], 1), :] = both + pltpu.roll(both, 64, axis=1)
        out_ref[pl.ds(SMALL["loss"], 1), :] = loss_ref[...]
        bk = bucket_ref[...]
        rows = lax.broadcasted_iota(jnp.int32, (N_BUCKETS, 128), 0)
        lanes = lax.broadcasted_iota(jnp.int32, (N_BUCKETS, 128), 1)
        lane1 = lax.broadcasted_iota(jnp.int32, (1, 128), 1)
        rb = jnp.zeros((N_BUCKETS, 128), F32)
        sk = jnp.zeros((1, 128), F32)
        for h in range(N_Q_HEADS):
            band = db_ref[pl.ds(h * BLOCK, BLOCK), :]
            for b in range(N_BUCKETS):
                rb = jnp.where((rows == b) & (lanes == h), jnp.sum(jnp.where(bk == b, band, 0.0)), rb)
            sk = jnp.where(lane1 == h, jnp.sum(ds_ref[pl.ds(h * BLOCK, BLOCK), :]), sk)
        out_ref[pl.ds(SMALL["rel_bias"], N_BUCKETS), :] = rb
        out_ref[pl.ds(SMALL["sinks"], 1), :] = sk
        out_ref[pl.ds(SMALL["w_pool"], 512), :] = wp_ref[...].reshape(512, 128)

    return pl.pallas_call(
        body, name="small_pack", in_specs=[VMEM_WHOLE] * 11, out_specs=VMEM_WHOLE,
        out_shape=jax.ShapeDtypeStruct((SMALL_ROWS, 128), F32),
    )(dg_attn, dg_ffn, dg_ple, dscale, dgq, dgk, dbias, dsink_rows, bucket, loss_v, dwpool)


def _attn_in_bwd(dqst, zqk, dk, dv, du, x2, dh1, wts, g_attn, gq, gk):
    s_len = x2.shape[0]
    t = 512

    def body(dq_ref, zqk_ref, dk_ref, dv_ref, du_ref, x_ref, dh1_ref, sl_ref, lo_ref, me_ref, g_ref, gq_ref, gk_ref,
             dz_ref, dx_ref, dg_ref, dgq_ref, dgk_ref, w_ref, sems):
        @pl.when(pl.program_id(0) == 0)
        def _():
            _load_rows((sl_ref, lo_ref, me_ref), "inT", w_ref, sems)
            dg_ref[...] = jnp.zeros_like(dg_ref)
            dgq_ref[...] = jnp.zeros_like(dgq_ref)
            dgk_ref[...] = jnp.zeros_like(dgk_ref)

        lo = lax.broadcasted_iota(jnp.int32, (t, 128), 1) < 64
        for p in range(4):
            dqn = _from_stacked(dq_ref[2 * p], dq_ref[2 * p + 1], p // 2, lo)
            dq_raw, dgq = _pair_norm_bwd(zqk_ref[:, 128 * p:128 * p + 128], gq_ref[...], dqn)
            dz_ref[:, 128 * p:128 * p + 128] = dq_raw.astype(BF16)
            dgq_ref[...] += dgq
        dk_raw, dgk = _pair_norm_bwd(zqk_ref[:, 512:640], gk_ref[...], dk_ref[...])
        dgk_ref[...] += dgk
        dz_ref[:, 512:640] = dk_raw.astype(BF16)
        dz_ref[:, 640:768] = dv_ref[...].astype(BF16)
        dz_ref[:, 768:] = du_ref[...].astype(BF16)
        dx, dg = _rms_bwd(x_ref[...], g_ref[...], _dot(dz_ref[...], w_ref[...], 1, 0))
        dx_ref[...] = dh1_ref[...] + dx
        dg_ref[...] += dg

    row = lambda w: pl.BlockSpec((t, w), lambda i: (i, 0))
    return pl.pallas_call(
        body, name="attn_in_bwd", grid=(s_len // t,),
        in_specs=[pl.BlockSpec((N_Q_HEADS, t, 128), lambda i: (0, i, 0)), row(640), row(128), row(128), row(POOL_WIDTH),
                  row(D_MODEL), row(D_MODEL)] + W_SPECS + [_full((1, D_MODEL)), _full((1, 128)), _full((1, 128))],
        out_specs=[row(IN_WIDTH), row(D_MODEL), _full((1, D_MODEL)), _full((1, 128)), _full((1, 128))],
        out_shape=[jax.ShapeDtypeStruct((s_len, IN_WIDTH), BF16), jax.ShapeDtypeStruct((s_len, D_MODEL), F32),
                   jax.ShapeDtypeStruct((1, D_MODEL), F32), jax.ShapeDtypeStruct((1, 128), F32),
                   jax.ShapeDtypeStruct((1, 128), F32)],
        scratch_shapes=[pltpu.VMEM((IN_WIDTH, D_MODEL), BF16), pltpu.SemaphoreType.DMA((N_CHIPS,))],
        compiler_params=_params(),
    )(dqst, zqk, dk, dv, du, x2, dh1, *wts, g_attn, gq, gk)


def _dw(a, b, name, into=None):
    tk = 1024
    n_out = b.shape[1]
    if a.ndim == 3:
        s_len, tm = a.shape[1:]
        m = N_CHIPS * tm
        a_spec = pl.BlockSpec((None, tk, tm), lambda i, k: (i, k, 0))
    else:
        s_len, m = a.shape
        tm = m // 2 if m > 1408 else m
        a_spec = pl.BlockSpec((tk, tm), lambda i, k: (k, i))
    n_steps = s_len // tk
    chunk = m // N_CHIPS
    per_tile = tm // chunk

    def accumulate(a_ref, b_ref, acc_ref, k):
        @pl.when(k == 0)
        def _():
            acc_ref[...] = _dot(a_ref[...].astype(BF16), b_ref[...].astype(BF16), 0, 0)

        @pl.when(k > 0)
        def _():
            acc_ref[...] += _dot(a_ref[...].astype(BF16), b_ref[...].astype(BF16), 0, 0)

    in_specs = [a_spec, pl.BlockSpec((tk, n_out), lambda i, k: (k, 0))]
    if into is None:
        def body(a_ref, b_ref, o_ref, acc_ref):
            k = pl.program_id(1)
            accumulate(a_ref, b_ref, acc_ref, k)

            @pl.when(k == n_steps - 1)
            def _():
                o_ref[...] = acc_ref[...].astype(BF16)

        return pl.pallas_call(
            body, name=name, grid=(m // tm, n_steps), in_specs=in_specs,
            out_specs=pl.BlockSpec((tm, n_out), lambda i, k: (i, 0)), out_shape=jax.ShapeDtypeStruct((m, n_out), BF16),
            scratch_shapes=[pltpu.VMEM((tm, n_out), F32)], compiler_params=_params(n_axes=2),
        )(a, b)

    slab, slab_rows, row_off = into
    assert n_out == D_MODEL

    def body_into(a_ref, b_ref, *rest):
        o_ref, acc_ref, stage_ref, sems = rest[-4:]
        i, k = pl.program_id(0), pl.program_id(1)
        accumulate(a_ref, b_ref, acc_ref, k)

        @pl.when(k == n_steps - 1)
        def _():
            stage_ref[...] = acc_ref[...].astype(BF16)
            copies = [pltpu.make_async_copy(stage_ref.at[pl.ds(jj * chunk, chunk), :],
                                            o_ref.at[i * per_tile + jj, pl.ds(row_off, chunk), :], sems.at[jj])
                      for jj in range(per_tile)]
            for cp in copies:
                cp.start()
            for cp in copies:
                cp.wait()

    operands, aliases = [a, b], {}
    if slab is not None:
        in_specs = in_specs + [ANY]
        operands.append(slab)
        aliases = {2: 0}
    return pl.pallas_call(
        body_into, name=name, grid=(m // tm, n_steps), in_specs=in_specs, out_specs=ANY,
        out_shape=jax.ShapeDtypeStruct((N_CHIPS, slab_rows, D_MODEL), BF16), input_output_aliases=aliases,
        scratch_shapes=[pltpu.VMEM((tm, n_out), F32), pltpu.VMEM((tm, n_out), BF16), pltpu.SemaphoreType.DMA((per_tile,))],
        compiler_params=_params(n_axes=2),
    )(*operands)


def _dw_pool(pooled, dyp):
    s_len = pooled.shape[0]
    tk = 512

    def body(a_ref, b_ref, o_ref):
        @pl.when(pl.program_id(0) == 0)
        def _():
            o_ref[...] = jnp.zeros_like(o_ref)

        for g in range(4):
            cols = slice(128 * g, 128 * g + 128)
            o_ref[g] += _dot(a_ref[:, cols], b_ref[:, cols], 0, 0)

    blk = pl.BlockSpec((tk, POOL_WIDTH), lambda k: (k, 0))
    return pl.pallas_call(
        body, name="dw_pool", grid=(s_len // tk,), in_specs=[blk, blk], out_specs=_full((4, 128, 128)),
        out_shape=jax.ShapeDtypeStruct((4, 128, 128), F32), compiler_params=_params(),
    )(pooled, dyp)


def _position():
    x, y, c = lax.axis_index("x"), lax.axis_index("y"), lax.axis_index("c")
    other_chips = [(1 - x, y), (x, 1 - y), (1 - x, 1 - y)]
    return x, y, c, other_chips


def _ag_weights(local_slab, row0, n_rows, name, collective_id):
    half = n_rows // 2
    quarter = half // 2
    assert quarter % 16 == 0

    def body(l_ref, g_ref, send, recv):
        x, y, c, chips = _position()
        me, (via_x, via_y, diagonal) = 2 * x + y, [2 * chip[0] + chip[1] for chip in chips]
        here, sibling, x_nbr, y_nbr = (x, y, c), (x, y, 1 - c), (1 - x, y, c), (x, 1 - y, c)
        peers = [sibling, x_nbr, y_nbr]
        barrier = pltpu.get_barrier_semaphore()
        for peer in peers:
            pl.semaphore_signal(barrier, inc=1, device_id=peer, device_id_type=MESH)
        pl.semaphore_wait(barrier, len(peers))

        def rows(core, part):
            start, size = (core * half, half) if part is None else (core * half + part * quarter, quarter)
            return pl.ds(pl.multiple_of(start, 16), size)

        def copy(k, chip_idx, where, to, src=None):
            dst = g_ref.at[chip_idx, where, :]
            return pltpu.make_async_remote_copy(src_ref=dst if src is None else src, dst_ref=dst, send_sem=send.at[k],
                                                recv_sem=recv.at[k], device_id=to, device_id_type=MESH)

        own_rows = l_ref.at[pl.ds(pl.multiple_of(row0 + c * half, 16), half), :]
        started = [copy(0, me, rows(c, None), x_nbr, src=own_rows), copy(1, me, rows(c, None), y_nbr, src=own_rows)]
        for cp in started:
            cp.start()
        after_arrival = [
            (copy(0, via_x, rows(c, None), here), [copy(4, via_x, rows(c, None), sibling), copy(3, via_x, rows(c, 1), y_nbr)]),
            (copy(1, via_y, rows(c, None), here), [copy(5, via_y, rows(c, None), sibling), copy(2, via_y, rows(c, 0), x_nbr)]),
            (copy(2, diagonal, rows(c, 0), here), [copy(6, diagonal, rows(c, 0), sibling)]),
            (copy(3, diagonal, rows(c, 1), here), [copy(7, diagonal, rows(c, 1), sibling)]),
        ]
        for arrival, onward in after_arrival:
            arrival.wait_recv()
            for cp in onward:
                cp.start()
            started += onward
        for cp in (copy(4, via_x, rows(1 - c, None), here), copy(5, via_y, rows(1 - c, None), here),
                   copy(6, diagonal, rows(1 - c, 0), here), copy(7, diagonal, rows(1 - c, 1), here)):
            cp.wait_recv()
        for cp in started:
            cp.wait_send()

    return pl.kernel(
        body, out_type=jax.ShapeDtypeStruct((N_CHIPS, n_rows, D_MODEL), BF16),
        mesh=plsc.ScalarSubcoreMesh(axis_name="sequencer", num_cores=1), name=name,
        scratch_types=[pltpu.SemaphoreType.DMA((8,)), pltpu.SemaphoreType.DMA((8,))],
        compiler_params=pltpu.CompilerParams(collective_id=collective_id),
    )(local_slab)


def _comm_call(body, peers_of, out_shape, n_sems, operand, name, collective_id):
    sems = [pltpu.SemaphoreType.DMA((n_sems,)), pltpu.SemaphoreType.DMA((n_sems,))]
    if collective_id is None:
        return pl.pallas_call(body, name=name, in_specs=[ANY], out_specs=ANY, out_shape=out_shape, scratch_shapes=sems)(operand)

    def with_handshake(in_ref, out_ref, send, recv):
        x, y, c, _ = _position()
        peers = peers_of(x, y, c)
        barrier = pltpu.get_barrier_semaphore()
        for peer in peers:
            pl.semaphore_signal(barrier, inc=1, device_id=peer, device_id_type=MESH)
        pl.semaphore_wait(barrier, len(peers))
        body(in_ref, out_ref, send, recv)

    return pl.kernel(with_handshake, out_type=out_shape, mesh=plsc.ScalarSubcoreMesh(axis_name="sequencer", num_cores=1),
                     name=name, scratch_types=sems, compiler_params=pltpu.CompilerParams(collective_id=collective_id))(operand)


def _rs_swap_halves(partial, name, collective_id=None):
    half = partial.shape[1] // 2

    def body(p_ref, r_ref, send, recv):
        x, y, c, _ = _position()
        theirs = pl.ds(pl.multiple_of((1 - c) * half, 16), half)
        cp = pltpu.make_async_remote_copy(src_ref=p_ref.at[:, theirs, :], dst_ref=r_ref, send_sem=send.at[0],
                                          recv_sem=recv.at[0], device_id=(x, y, 1 - c), device_id_type=MESH)
        cp.start()
        cp.wait()

    return _comm_call(body, lambda x, y, c: [(x, y, 1 - c)], jax.ShapeDtypeStruct((N_CHIPS, half, D_MODEL), BF16), 1,
                      partial, name, collective_id)


def _rs_add_halves(partial, other, core, name, after):
    half = other.shape[1]
    t = half // 2
    steps = half // t

    def body(core_ref, a_ref, b_ref, after_ref, o_ref):
        del after_ref
        o_ref[...] = (a_ref[...].astype(F32) + b_ref[...].astype(F32)).astype(BF16)

    return pl.pallas_call(
        body, name=name,
        grid_spec=pltpu.PrefetchScalarGridSpec(
            num_scalar_prefetch=1, grid=(N_CHIPS, steps),
            in_specs=[pl.BlockSpec((1, t, D_MODEL), lambda j, i, core_ref: (j, core_ref[0] * steps + i, 0)),
                      pl.BlockSpec((1, t, D_MODEL), lambda j, i, core_ref: (j, i, 0)), ANY],
            out_specs=pl.BlockSpec((1, t, D_MODEL), lambda j, i, core_ref: (j, i, 0))),
        out_shape=jax.ShapeDtypeStruct((N_CHIPS, half, D_MODEL), BF16),
        compiler_params=_params(n_axes=2),
    )(core, partial, other, after)


def _rs_exchange_chips(pre, name, collective_id=None):
    def body(s_ref, r_ref, send, recv):
        x, y, c, chips = _position()

        def copy(k, chunk, to):
            return pltpu.make_async_remote_copy(src_ref=s_ref.at[chunk], dst_ref=r_ref.at[k], send_sem=send.at[k],
                                                recv_sem=recv.at[k], device_id=to, device_id_type=MESH)

        sends = [copy(k, 2 * chip[0] + chip[1], (*chip, c)) for k, chip in enumerate(chips)]
        for cp in sends:
            cp.start()
        for cp in sends:
            cp.wait()

    return _comm_call(body, lambda x, y, c: [(1 - x, y, c), (x, 1 - y, c), (1 - x, 1 - y, c)],
                      jax.ShapeDtypeStruct((3, pre.shape[1], D_MODEL), BF16), 3, pre, name, collective_id)


def _rs_sum_chips(pre, received, place, name):
    half = pre.shape[1]
    t = half // 2 if half > 512 else half
    steps = half // t

    def body(place_ref, own_ref, r_ref, o_ref):
        acc = own_ref[0].astype(F32)
        for k in range(3):
            acc = acc + r_ref[k].astype(F32)
        o_ref[...] = acc

    return pl.pallas_call(
        body, name=name,
        grid_spec=pltpu.PrefetchScalarGridSpec(
            num_scalar_prefetch=1, grid=(steps,),
            in_specs=[pl.BlockSpec((1, t, D_MODEL), lambda i, place_ref: (place_ref[0], i, 0)),
                      pl.BlockSpec((3, t, D_MODEL), lambda i, place_ref: (0, i, 0))],
            out_specs=pl.BlockSpec((t, D_MODEL), lambda i, place_ref: (place_ref[1] * steps + i, 0))),
        out_shape=jax.ShapeDtypeStruct((2 * half, D_MODEL), F32),
        compiler_params=_params(),
    )(place, pre, received)


def _rs_finish(grads_a, grads_b, small):
    def body(fa_ref, fb_ref, s_ref, ga_ref, gb_ref, t_ref, send, recv, local_sem):
        del fa_ref, fb_ref
        x, y, c, chips = _position()
        sibling = (x, y, 1 - c)

        def slot(px, py, pc):
            return t_ref.at[4 * px + 2 * py + pc]

        def copy(k, block, to, src=None):
            return pltpu.make_async_remote_copy(src_ref=slot(*block) if src is None else src, dst_ref=slot(*block),
                                                send_sem=send.at[k], recv_sem=recv.at[k], device_id=to, device_id_type=MESH)

        def half_copies(core, to):
            out = []
            for k, g_ref in ((7, ga_ref), (8, gb_ref)):
                half = g_ref.shape[0] // 2
                rows = g_ref.at[pl.ds(pl.multiple_of(core * half, 8), half), :]
                out.append(pltpu.make_async_remote_copy(src_ref=rows, dst_ref=rows, send_sem=send.at[k], recv_sem=recv.at[k],
                                                        device_id=to, device_id_type=MESH))
            return out

        own_small = pltpu.make_async_copy(s_ref, slot(x, y, c), local_sem)
        own_small.start()
        to_sibling = half_copies(c, sibling)
        for cp in to_sibling:
            cp.start()
        first = [copy(0, (x, y, c), sibling, src=s_ref)]
        first += [copy(1 + k, (x, y, c), (*chip, c), src=s_ref) for k, chip in enumerate(chips)]
        for cp in first:
            cp.start()
        passed = []
        for k, chip in enumerate(chips):
            copy(1 + k, (*chip, c), (x, y, c)).wait_recv()
            fwd = copy(4 + k, (*chip, c), sibling)
            fwd.start()
            passed.append(fwd)
        copy(0, sibling, (x, y, c)).wait_recv()
        for k, chip in enumerate(chips):
            copy(4 + k, (*chip, 1 - c), (x, y, c)).wait_recv()
        for cp in half_copies(1 - c, (x, y, c)):
            cp.wait_recv()
        for cp in first + passed + to_sibling:
            cp.wait_send()
        own_small.wait()

    return pl.pallas_call(
        body, name="rs_finish", in_specs=[ANY, ANY, ANY], out_specs=[ANY, ANY, ANY], input_output_aliases={0: 0, 1: 1},
        out_shape=[jax.ShapeDtypeStruct(grads_a.shape, F32), jax.ShapeDtypeStruct(grads_b.shape, F32),
                   jax.ShapeDtypeStruct((N_DEV, SMALL_ROWS, 128), F32)],
        scratch_shapes=[pltpu.SemaphoreType.DMA((9,)), pltpu.SemaphoreType.DMA((9,)), pltpu.SemaphoreType.DMA],
    )(grads_a, grads_b, small)


def _adam_update(w, g, m, v):
    m_new = ADAM_B1 * m + (1.0 - ADAM_B1) * g
    v_new = ADAM_B2 * v + (1.0 - ADAM_B2) * (g * g)
    m_hat = m_new / (1.0 - ADAM_B1 ** ADAM_STEP)
    v_hat = v_new / (1.0 - ADAM_B2 ** ADAM_STEP)
    return -ADAM_LR * (m_hat / (jnp.sqrt(v_hat) + ADAM_EPS) + ADAM_WD * w), m_new, v_new


def _adamw(w, g_rows, row_off, m, v, name):
    rows, cols = w.shape
    t = rows if rows <= 320 else (rows // 2 if rows % 256 else 256)

    def body(w_ref, g_ref, m_ref, v_ref, go_ref, d_ref, nm_ref, nv_ref):
        g = g_ref[...]
        go_ref[...] = g
        d_ref[...], nm_ref[...], nv_ref[...] = _adam_update(w_ref[...], g, m_ref[...], v_ref[...])

    blk = pl.BlockSpec((t, cols), lambda i: (i, 0))
    assert row_off % 8 == 0 and t % 8 == 0
    g_blk = pl.BlockSpec((pl.Element(t), pl.Element(cols)), lambda i: (pl.multiple_of(row_off + i * t, 8), 0))
    shape = jax.ShapeDtypeStruct((rows, cols), F32)
    return pl.pallas_call(
        body, name=name, grid=(rows // t,), in_specs=[blk, g_blk, blk, blk], out_specs=[blk] * 4, out_shape=[shape] * 4,
        compiler_params=_params(),
    )(w, g_rows, m, v)


SMALL_PARAMS = [("g_attn", (1, D_MODEL), 8), ("g_q", (1, HEAD_DIM), None), ("g_k", (1, HEAD_DIM), None),
                ("sinks", (1, N_Q_HEADS), None), ("rel_bias", (N_BUCKETS, N_Q_HEADS), None), ("w_pool", (512, 128), None),
                ("pool_scale", (1, POOL_WIDTH), 4), ("g_ffn", (1, D_MODEL), 8), ("g_ple", (1, D_MODEL), 8)]


def _adamw_small(tables, wmv):
    n_par = len(SMALL_PARAMS)

    def body(*refs):
        t_ref = refs[0]
        ins = refs[1:1 + 3 * n_par]
        loss_ref = refs[1 + 3 * n_par]
        outs = refs[2 + 3 * n_par:-1]
        tot_ref = refs[-1]
        total = t_ref[0]
        for d in range(1, N_DEV):
            total = total + t_ref[d]
        tot_ref[...] = total
        loss_ref[...] = tot_ref[pl.ds(SMALL["loss"], 1), 0:1]
        for i, (name, shape, split) in enumerate(SMALL_PARAMS):
            g_ref, d_ref, nm_ref, nv_ref = outs[4 * i:4 * i + 4]
            row = SMALL[name]
            if split:
                for k in range(split):
                    g_ref[:, 128 * k:128 * k + 128] = tot_ref[pl.ds(row + k, 1), :]
            else:
                g_ref[...] = tot_ref[pl.ds(row, shape[0]), 0:shape[1]]
            w_ref, m_ref, v_ref = ins[3 * i:3 * i + 3]
            d_ref[...], nm_ref[...], nv_ref[...] = _adam_update(w_ref[...], g_ref[...], m_ref[...], v_ref[...])

    shapes = [jax.ShapeDtypeStruct((1, 1), F32)]
    for _, shape, _ in SMALL_PARAMS:
        shapes += [jax.ShapeDtypeStruct(shape, F32)] * 4
    flat = [a for triple in wmv for a in triple]
    res = pl.pallas_call(
        body, name="adamw_small", in_specs=[VMEM_WHOLE] * (1 + 3 * n_par), out_specs=[VMEM_WHOLE] * len(shapes),
        out_shape=shapes, scratch_shapes=[pltpu.VMEM((SMALL_ROWS, 128), F32)],
    )(tables, *flat)
    return res[0], [res[1 + 4 * i:5 + 4 * i] for i in range(n_par)]


def _pack_ple_proj(shard):
    return shard.reshape(4, 64, 256).transpose(1, 0, 2).reshape(64, D_MODEL)


class _Reduction:
    def __init__(self, tag, place, ids=(None, None)):
        self.tag, self.place, self.ids = tag, place, ids

    def start(self, partial):
        self.partial = partial
        self.other = _rs_swap_halves(partial, "rs_swap_" + self.tag, self.ids[0])
        return partial

    def middle(self, after):
        self.pre = _rs_add_halves(self.partial, self.other, self.place[1:], "rs_add_" + self.tag, after)
        self.received = _rs_exchange_chips(self.pre, "rs_exchange_" + self.tag, self.ids[1])
        return self.pre

    def finish(self):
        return _rs_sum_chips(self.pre, self.received, self.place, "rs_sum_" + self.tag)


def _local_grads(x2, p2, tgt, wts, g_attn_norm, g_q, g_k, attn_sinks, rel_bias, w_pool, pool_scale, g_ffn_norm, g_ple_norm,
                 reduce_a):
    part_in, part_out, part_rest, local_slab, me = wts
    w_in, w_out, w_late = (part_in, local_slab, me), (part_out, local_slab, me), (part_rest, local_slab, me)
    bucket = jnp.asarray(_bucket_table())
    gq = jnp.tile(g_q, (1, 2))
    gk = jnp.tile(g_k, (1, 2))
    wpool = w_pool[0].astype(BF16)
    sinks = attn_sinks[0]
    bias_st = _bias_build(rel_bias.T, bucket)

    hn1, zqk, u, kn, vb, qst = _attn_in(x2, g_attn_norm, gq, gk, w_in)
    ost = _attn_fwd(qst, kn, vb, bias_st, sinks)
    pooled, mix, h1, hn2 = _mix_out(u, ost, x2, w_out, wpool, pool_scale, g_ffn_norm)
    loss_v, dgate, dup, act, dh2, hn3, dgl, dpp, dh1, dg_ffn, dg_ple = _ffn_ple(hn2, h1, p2, tgt, w_late, g_ffn_norm,
                                                                                   g_ple_norm)

    rows_a = SLAB_ROWS - SLAB["inT"][1]
    partial_a = None
    for name, lhs, rhs in (("out", mix, dh1), ("gateT", dgate, hn2), ("upT", dup, hn2), ("down", act, dh2), ("plg", hn3, dgl)):
        partial_a = _dw(lhs, rhs, "dw_" + name, into=(partial_a, rows_a, SLAB[name][0] - SLAB["inT"][1]))
    dw_plp = _dw(p2, dpp, "dw_plp").reshape(4, 64, N_CHIPS, 256).transpose(2, 1, 0, 3).reshape(N_CHIPS, 64, D_MODEL)
    partial_a = reduce_a.start(lax.dynamic_update_slice(partial_a, dw_plp, (0, SLAB["plp"][0] - SLAB["inT"][1], 0)))
    dost, du, dyp, dscale = _mix_out_bwd(dh1, w_out, pooled, wpool, pool_scale, partial_a)
    pre_a = reduce_a.middle(du)
    dqst, dk, dv, dbias, dsink_rows = _attn_bwd(qst, kn, vb, dost, bias_st, sinks, pre_a)
    dz, dx, dg_attn, dgq, dgk = _attn_in_bwd(dqst, zqk, dk, dv, du, x2, dh1, w_in, g_attn_norm, gq, gk)

    partial_b = _dw(dz, hn1, "dw_in").reshape(N_CHIPS, -1, D_MODEL)
    small = _small_pack(dg_attn, dg_ffn, dg_ple, dscale, dgq, dgk, dbias, dsink_rows, bucket, loss_v, _dw_pool(pooled, dyp))
    return dx, partial_b, small


def kernel(x, p, w_in, w_out, g_attn_norm, g_q, g_k, attn_sinks, rel_bias, w_pool, pool_scale, g_ffn_norm, w_gate, w_up, w_down, g_ple_norm, w_ple_gate, w_ple_proj, loss_target, m_w_in, m_w_out, m_g_attn_norm, m_g_q, m_g_k, m_attn_sinks, m_rel_bias, m_w_pool, m_pool_scale, m_g_ffn_norm, m_w_gate, m_w_up, m_w_down, m_g_ple_norm, m_w_ple_gate, m_w_ple_proj, v_w_in, v_w_out, v_g_attn_norm, v_g_q, v_g_k, v_attn_sinks, v_rel_bias, v_w_pool, v_pool_scale, v_g_ffn_norm, v_w_gate, v_w_up, v_w_down, v_g_ple_norm, v_w_ple_gate, v_w_ple_proj):
    core = lax.axis_index("c").astype(jnp.int32).reshape(1)
    me = (2 * lax.axis_index("x") + lax.axis_index("y")).astype(jnp.int32).reshape(1)

    local_slab = jnp.concatenate(
        [w_in[0].T, w_out[0], w_gate[0].T, w_up[0].T, w_down[0], w_ple_gate[0], _pack_ple_proj(w_ple_proj[0])],
        axis=0).astype(BF16)
    gathered = [_ag_weights(local_slab, start, stop - start, name, collective_id)
                for (start, stop), name, collective_id in zip(GATHER_PARTS, ("ag_in", "ag_out", "ag_rest"), (1, 2, 5))]
    wts = (*gathered, local_slab, me)

    place = jnp.concatenate([me, core])
    reduce_a = _Reduction("a", place, ids=(3, 4))
    dx, partial_b, small = _local_grads(x[0], p[0, 0], loss_target[0], wts, g_attn_norm, g_q, g_k, attn_sinks, rel_bias,
                                        w_pool, pool_scale, g_ffn_norm, g_ple_norm, reduce_a)
    reduce_b = _Reduction("b", place)
    reduce_b.start(partial_b)
    reduce_b.middle(partial_b)
    grads_a, grads_b, small_all = _rs_finish(reduce_a.finish(), reduce_b.finish(), small)

    def rows(name):
        return (grads_b, 0) if name == "inT" else (grads_a, SLAB[name][0] - SLAB["inT"][1])

    plp_rows = grads_a[SLAB["plp"][0] - SLAB["inT"][1]:]
    big = {
        "w_in": (w_in, m_w_in, v_w_in, rows("inT"), True),
        "w_out": (w_out, m_w_out, v_w_out, rows("out"), False),
        "w_gate": (w_gate, m_w_gate, v_w_gate, rows("gateT"), True),
        "w_up": (w_up, m_w_up, v_w_up, rows("upT"), True),
        "w_down": (w_down, m_w_down, v_w_down, rows("down"), False),
        "w_ple_gate": (w_ple_gate, m_w_ple_gate, v_w_ple_gate, rows("plg"), False),
        "w_ple_proj": (w_ple_proj, m_w_ple_proj, v_w_ple_proj,
                       (plp_rows.reshape(64, 4, 256).transpose(1, 0, 2).reshape(PLE_DIM, PLE_DIM), 0), False),
    }
    small_params = {
        "g_attn_norm": (g_attn_norm, m_g_attn_norm, v_g_attn_norm), "g_q": (g_q, m_g_q, v_g_q), "g_k": (g_k, m_g_k, v_g_k),
        "attn_sinks": (attn_sinks, m_attn_sinks, v_attn_sinks), "rel_bias": (rel_bias, m_rel_bias, v_rel_bias),
        "w_pool": tuple(a.reshape(512, 128) for a in (w_pool, m_w_pool, v_w_pool)),
        "pool_scale": (pool_scale, m_pool_scale, v_pool_scale), "g_ffn_norm": (g_ffn_norm, m_g_ffn_norm, v_g_ffn_norm),
        "g_ple_norm": (g_ple_norm, m_g_ple_norm, v_g_ple_norm),
    }

    grads, deltas, new_ms, new_vs = {}, {}, {}, {}
    for name, (w, m, v, (g_rows, row_off), transposed) in big.items():
        view = (lambda a: a.T) if transposed else (lambda a: a)
        out = _adamw(view(w[0]), g_rows, row_off, view(m[0]), view(v[0]), "adamw_" + name)
        grads[name], deltas[name], new_ms[name], new_vs[name] = (view(a)[None] for a in out)

    loss, small_out = _adamw_small(small_all, list(small_params.values()))
    for name, (g2, d, nm, nv) in zip(small_params, small_out):
        shape = w_pool.shape if name == "w_pool" else g2.shape
        grads[name], deltas[name], new_ms[name], new_vs[name] = (a.reshape(shape) for a in (g2, d, nm, nv))

    order = ["w_in", "w_out", "g_attn_norm", "g_q", "g_k", "attn_sinks", "rel_bias", "w_pool", "pool_scale", "g_ffn_norm",
             "w_gate", "w_up", "w_down", "g_ple_norm", "w_ple_gate", "w_ple_proj"]
    return (loss.reshape(()), dx[None], *[grads[n] for n in order], *[deltas[n] for n in order],
            *[new_ms[n] for n in order], *[new_vs[n] for n in order])
```

```python
import functools

import numpy as np
import jax
import jax.numpy as jnp
from jax import lax
from jax.experimental import pallas as pl
from jax.experimental.pallas import tpu as pltpu
from jax.experimental.pallas import tpu_sc as plsc

F32 = jnp.float32
BF16 = jnp.bfloat16
MESH = pl.DeviceIdType.MESH

D_MODEL = 1024
HEAD_DIM = 64
N_Q_HEADS = 8
ATTN_WIDTH = 512
KV_WIDTH = 128
POOL_WIDTH = 512
IN_WIDTH = 1280
D_FF = 2816
PLE_DIM = 256
FF_CHUNK = 704
BLOCK = 128
N_BUCKETS = 32
MAX_DISTANCE = 128
POOL_SIZES = (2, 4, 8, 16)
EPS = 1e-6
NEG = -1e30
N_CHIPS = 4
N_DEV = 8

ADAM_LR = 0.001
ADAM_B1 = 0.9
ADAM_B2 = 0.999
ADAM_EPS = 1e-08
ADAM_WD = 0.01
ADAM_STEP = 10

SLAB = {"inT": (0, 320), "out": (320, 256), "gateT": (576, 704), "upT": (1280, 704), "down": (1984, 704),
        "plg": (2688, 256), "plp": (2944, 64)}
SLAB_ROWS = 3008
HALF_ROWS = SLAB_ROWS // 2
GATHER_PARTS = ((0, 320), (320, 576), (576, SLAB_ROWS))
POOL_HALO = 24

SMALL = {"g_attn": 0, "g_ffn": 8, "g_ple": 16, "pool_scale": 24, "g_q": 28, "g_k": 29, "sinks": 30, "loss": 31,
         "rel_bias": 32, "w_pool": 64}
SMALL_ROWS = 576

VMEM_LIMIT_BIG = 60 * 1024 * 1024
VMEM_LIMIT = 48 * 1024 * 1024


def _params(vmem=VMEM_LIMIT, n_axes=1):
    return pltpu.CompilerParams(dimension_semantics=("arbitrary",) * n_axes, vmem_limit_bytes=vmem)


def _dot(a, b, ca, cb):
    return lax.dot_general(a, b, (((ca,), (cb,)), ((), ())), preferred_element_type=F32)


def _full(shape):
    return pl.BlockSpec(shape, lambda i: (0,) * len(shape))


ANY = pl.BlockSpec(memory_space=pl.ANY)
VMEM_WHOLE = pl.BlockSpec(memory_space=pltpu.VMEM)


W_SPECS = [ANY, ANY, pl.BlockSpec(memory_space=pltpu.SMEM)]


def _load_rows(w_refs, name, dst_ref, sems):
    slab_ref, local_ref, me_ref = w_refs
    off, rows = SLAB[name]
    slab_off = off - max(start for start, _ in GATHER_PARTS if start <= off)
    me = me_ref[0]
    for phase in ("start", "wait"):
        for j in range(N_CHIPS):
            dst = dst_ref.at[pl.ds(j * rows, rows), :]
            theirs = pltpu.make_async_copy(slab_ref.at[j, pl.ds(slab_off, rows), :], dst, sems.at[j])
            own = pltpu.make_async_copy(local_ref.at[pl.ds(off, rows), :], dst, sems.at[j])

            @pl.when(me == j)
            def _():
                getattr(own, phase)()

            @pl.when(me != j)
            def _():
                getattr(theirs, phase)()


def _rms_fwd(x, g):
    r = lax.rsqrt(jnp.mean(x * x, axis=-1, keepdims=True) + EPS)
    return x * r * g


def _rms_bwd(x, g, dy):
    r = lax.rsqrt(jnp.mean(x * x, axis=-1, keepdims=True) + EPS)
    xn = x * r
    dyg = dy * g
    dx = r * (dyg - xn * jnp.mean(dyg * xn, axis=-1, keepdims=True))
    return dx, jnp.sum(dy * xn, axis=0, keepdims=True)


def _half_sum(v, lo):
    s_lo = jnp.sum(jnp.where(lo, v, 0.0), axis=-1, keepdims=True)
    s_hi = jnp.sum(jnp.where(lo, 0.0, v), axis=-1, keepdims=True)
    return jnp.where(lo, s_lo, s_hi)


def _half_sum_mxu(v):
    upper = lax.broadcasted_iota(jnp.int32, (128, 128), 0) < 64
    left = lax.broadcasted_iota(jnp.int32, (128, 128), 1) < 64
    ones = jnp.where(upper == left, 1.0, 0.0).astype(BF16)
    high = v.astype(BF16)
    low = (v - high.astype(F32)).astype(BF16)
    return _dot(high, ones, 1, 0) + _dot(low, ones, 1, 0)


def _pair_norm(zp, g, lo):
    r = lax.rsqrt(_half_sum(zp * zp, lo) * (1.0 / HEAD_DIM) + EPS)
    return zp * r * g


def _pair_norm_bwd(zp, g, dy):
    r = lax.rsqrt(_half_sum_mxu(zp * zp) * (1.0 / HEAD_DIM) + EPS)
    xn = zp * r
    dyg = dy * g
    dx = r * (dyg - xn * (_half_sum_mxu(dyg * xn) * (1.0 / HEAD_DIM)))
    return dx, jnp.sum(dy * xn, axis=0, keepdims=True)


def _to_stacked(pair, group, lo):
    rolled = pltpu.roll(pair, 64, axis=1)
    if group == 0:
        return jnp.where(lo, pair, 0.0), jnp.where(lo, rolled, 0.0)
    return jnp.where(lo, 0.0, rolled), jnp.where(lo, 0.0, pair)


def _from_stacked(even, odd, group, lo):
    if group == 0:
        return jnp.where(lo, even, pltpu.roll(odd, 64, axis=1))
    return jnp.where(lo, pltpu.roll(even, 64, axis=1), odd)


def _sigmoid(v):
    return 1.0 / (1.0 + jnp.exp(-v))


def _pool_counts(tile, n_rows):
    t1 = tile * n_rows + lax.broadcasted_iota(jnp.int32, (n_rows, POOL_WIDTH), 0) + 1
    lane = lax.broadcasted_iota(jnp.int32, (n_rows, POOL_WIDTH), 1)
    win = jnp.where(lane < 128, 2, jnp.where(lane < 256, 4, jnp.where(lane < 384, 8, 16)))
    return jnp.minimum(t1, win).astype(F32)


def _attn_in(x2, g_attn, gq, gk, wts):
    s_len = x2.shape[0]
    t = 512

    def body(x_ref, g_ref, gq_ref, gk_ref, sl_ref, lo_ref, me_ref, hn_ref, zqk_ref, u_ref, kn_ref, v_ref, qst_ref, w_ref, sems):
        @pl.when(pl.program_id(0) == 0)
        def _():
            _load_rows((sl_ref, lo_ref, me_ref), "inT", w_ref, sems)

        hn = _rms_fwd(x_ref[...], g_ref[...]).astype(BF16)
        hn_ref[...] = hn
        z = _dot(hn, w_ref[...], 1, 1)
        zqk_ref[...] = z[:, :640]
        u_ref[...] = z[:, 768:]
        v_ref[...] = z[:, 640:768].astype(BF16)
        lo = lax.broadcasted_iota(jnp.int32, (t, 128), 1) < 64
        kn_ref[...] = _pair_norm(z[:, 512:640], gk_ref[...], lo).astype(BF16)
        for p in range(4):
            qn = _pair_norm(z[:, 128 * p:128 * p + 128], gq_ref[...], lo)
            even, odd = _to_stacked(qn, p // 2, lo)
            qst_ref[2 * p] = even.astype(BF16)
            qst_ref[2 * p + 1] = odd.astype(BF16)

    row = lambda w: pl.BlockSpec((t, w), lambda i: (i, 0))
    return pl.pallas_call(
        body, name="attn_in", grid=(s_len // t,),
        in_specs=[row(D_MODEL), _full((1, D_MODEL)), _full((1, 128)), _full((1, 128))] + W_SPECS,
        out_specs=[row(D_MODEL), row(640), row(POOL_WIDTH), row(128), row(128),
                   pl.BlockSpec((N_Q_HEADS, t, 128), lambda i: (0, i, 0))],
        out_shape=[jax.ShapeDtypeStruct((s_len, D_MODEL), BF16), jax.ShapeDtypeStruct((s_len, 640), F32),
                   jax.ShapeDtypeStruct((s_len, POOL_WIDTH), F32), jax.ShapeDtypeStruct((s_len, 128), BF16),
                   jax.ShapeDtypeStruct((s_len, 128), BF16), jax.ShapeDtypeStruct((N_Q_HEADS, s_len, 128), BF16)],
        scratch_shapes=[pltpu.VMEM((IN_WIDTH, D_MODEL), BF16), pltpu.SemaphoreType.DMA((N_CHIPS,))],
        compiler_params=_params(),
    )(x2, g_attn, gq, gk, *wts)


def _bucket_table():
    i_idx = np.arange(BLOCK)[:, None]
    j_idx = np.arange(2 * BLOCK)[None, :]
    d = BLOCK + i_idx - j_idx
    n = np.maximum(d, 0)
    max_exact = N_BUCKETS // 2
    nf = np.maximum(n, 1).astype(np.float64)
    large = max_exact + (np.log(nf / max_exact) / np.log(MAX_DISTANCE / max_exact) * (N_BUCKETS - max_exact)).astype(np.int64)
    large = np.minimum(large, N_BUCKETS - 1)
    bucket = np.where(n < max_exact, n, large)
    return np.where((d >= 0) & (d < BLOCK), bucket, -1).astype(np.int32)


def _bias_build(rel_bias_t, bucket):
    def body(rb_ref, bucket_ref, out_ref):
        bk = bucket_ref[...]
        for h in range(N_Q_HEADS):
            acc = jnp.full((BLOCK, 2 * BLOCK), NEG, F32)
            for b in range(N_BUCKETS):
                acc = jnp.where(bk == b, rb_ref[h, b], acc)
            out_ref[0, pl.ds(h * BLOCK, BLOCK), :] = acc
            out_ref[1, pl.ds(h * BLOCK, BLOCK), :] = acc
            out_ref[1, pl.ds(h * BLOCK, BLOCK), 0:BLOCK] = jnp.full((BLOCK, BLOCK), NEG, F32)

    return pl.pallas_call(
        body, name="bias_build",
        in_specs=[pl.BlockSpec(memory_space=pltpu.SMEM), VMEM_WHOLE], out_specs=VMEM_WHOLE,
        out_shape=jax.ShapeDtypeStruct((2, N_Q_HEADS * BLOCK, 2 * BLOCK), F32),
    )(rel_bias_t, bucket)


def _head_softmax(s_ref, bias_ref, sink_ref, h):
    rows = pl.ds(pl.multiple_of(h * BLOCK, BLOCK), BLOCK)
    s = s_ref[rows, :] * (HEAD_DIM ** -0.5) + bias_ref[rows, :]
    sink = sink_ref[h]
    m = jnp.maximum(jnp.max(s, axis=-1, keepdims=True), sink)
    p = jnp.exp(s - m)
    e_sink = jnp.exp(sink - m)
    inv = 1.0 / (jnp.sum(p, axis=-1, keepdims=True) + e_sink)
    return rows, p * inv, e_sink * inv


def _attn_specs():
    prev = lambda i: (jnp.maximum(i - 1, 0), 0)
    cur = lambda i: (i, 0)
    stacked = pl.BlockSpec((N_Q_HEADS, BLOCK, 128), lambda i: (0, i, 0))
    kv = [pl.BlockSpec((BLOCK, 128), prev), pl.BlockSpec((BLOCK, 128), cur)]
    consts = [pl.BlockSpec((None, N_Q_HEADS * BLOCK, 2 * BLOCK), lambda i: (jnp.where(i == 0, 1, 0), 0, 0)),
              pl.BlockSpec(memory_space=pltpu.SMEM)]
    return stacked, kv, consts


def _head_lane_mask():
    rows = lax.broadcasted_iota(jnp.int32, (N_Q_HEADS * BLOCK, 128), 0)
    lanes = lax.broadcasted_iota(jnp.int32, (N_Q_HEADS * BLOCK, 128), 1)
    return (rows < 4 * BLOCK) == (lanes < 64)


def _attn_fwd(qst, kn, vb, bias_st, sinks):
    s_len = kn.shape[0]

    def body(q_ref, kp_ref, kc_ref, vp_ref, vc_ref, bias_ref, sink_ref, o_ref, s_ref, p_ref):
        q = q_ref[...].reshape(N_Q_HEADS * BLOCK, 128)
        s_ref[...] = _dot(q, jnp.concatenate([kp_ref[...], kc_ref[...]], axis=0), 1, 1)

        def head(h, carry):
            rows, probs, _ = _head_softmax(s_ref, bias_ref, sink_ref, h)
            p_ref[rows, :] = probs.astype(BF16)
            return carry

        lax.fori_loop(0, N_Q_HEADS, head, 0, unroll=True)
        o = _dot(p_ref[...], jnp.concatenate([vp_ref[...], vc_ref[...]], axis=0), 1, 0)
        o_ref[...] = jnp.where(_head_lane_mask(), o, 0.0).astype(BF16).reshape(N_Q_HEADS, BLOCK, 128)

    stacked, kv, consts = _attn_specs()
    return pl.pallas_call(
        body, name="attn_fwd", grid=(s_len // BLOCK,),
        in_specs=[stacked] + kv + kv + consts, out_specs=stacked,
        out_shape=jax.ShapeDtypeStruct((N_Q_HEADS, s_len, 128), BF16),
        scratch_shapes=[pltpu.VMEM((N_Q_HEADS * BLOCK, 2 * BLOCK), F32), pltpu.VMEM((N_Q_HEADS * BLOCK, 2 * BLOCK), BF16)],
        compiler_params=_params(),
    )(qst, kn, kn, vb, vb, bias_st, sinks)


def _mix_out(u, ost, x2, wts, wpool, pool_scale, g_ffn):
    s_len = x2.shape[0]
    t = 512
    n = t + 16

    def body(u_ref, o_ref, x_ref, sl_ref, lo_ref, me_ref, wp_ref, sc_ref, g_ref, pooled_ref, mix_ref, h1_ref, hn_ref,
             w_ref, ext_ref, st_ref, sems):
        i = pl.program_id(0)

        @pl.when(i == 0)
        def _():
            _load_rows((sl_ref, lo_ref, me_ref), "out", w_ref, sems)
            ext_ref[...] = jnp.zeros_like(ext_ref)
            st_ref[...] = jnp.zeros_like(st_ref)

        u_tile = u_ref[...]
        ext_ref[pl.ds(POOL_HALO, t), :] = u_tile
        st_ref[pl.ds(8, n), :] = ext_ref[pl.ds(8, n), :] + ext_ref[pl.ds(7, n), :]
        st_ref[pl.ds(8, n), 128:] = st_ref[pl.ds(8, n), 128:] + st_ref[pl.ds(6, n), 128:]
        st_ref[pl.ds(8, n), 256:] = st_ref[pl.ds(8, n), 256:] + st_ref[pl.ds(4, n), 256:]
        st_ref[pl.ds(8, n), 384:] = st_ref[pl.ds(8, n), 384:] + st_ref[pl.ds(0, n), 384:]
        ext_ref[pl.ds(0, POOL_HALO), :] = ext_ref[pl.ds(t, POOL_HALO), :]
        pooled = (st_ref[pl.ds(POOL_HALO, t), :] / _pool_counts(i, t) - u_tile).astype(BF16)
        pooled_ref[...] = pooled
        for g in range(4):
            cols = slice(128 * g, 128 * g + 128)
            y = _dot(pooled[:, cols], wp_ref[g], 1, 0) * sc_ref[:, cols]
            mix_ref[:, ATTN_WIDTH + 128 * g:ATTN_WIDTH + 128 * g + 128] = y.astype(BF16)
        lo = lax.broadcasted_iota(jnp.int32, (t, 128), 1) < 64
        for p in range(4):
            a = _from_stacked(o_ref[2 * p].astype(F32), o_ref[2 * p + 1].astype(F32), p // 2, lo)
            mix_ref[:, 128 * p:128 * p + 128] = a.astype(BF16)
        h1 = x_ref[...] + _dot(mix_ref[...], w_ref[...], 1, 0)
        h1_ref[...] = h1
        hn_ref[...] = _rms_fwd(h1, g_ref[...]).astype(BF16)

    row = lambda w: pl.BlockSpec((t, w), lambda i: (i, 0))
    return pl.pallas_call(
        body, name="mix_out", grid=(s_len // t,),
        in_specs=[row(POOL_WIDTH), pl.BlockSpec((N_Q_HEADS, t, 128), lambda i: (0, i, 0)), row(D_MODEL)] + W_SPECS
        + [_full((4, 128, 128)), _full((1, POOL_WIDTH)), _full((1, D_MODEL))],
        out_specs=[row(POOL_WIDTH), row(D_MODEL), row(D_MODEL), row(D_MODEL)],
        out_shape=[jax.ShapeDtypeStruct((s_len, POOL_WIDTH), BF16), jax.ShapeDtypeStruct((s_len, D_MODEL), BF16),
                   jax.ShapeDtypeStruct((s_len, D_MODEL), F32), jax.ShapeDtypeStruct((s_len, D_MODEL), BF16)],
        scratch_shapes=[pltpu.VMEM((D_MODEL, D_MODEL), BF16), pltpu.VMEM((t + POOL_HALO, POOL_WIDTH), F32),
                        pltpu.VMEM((t + POOL_HALO, POOL_WIDTH), F32), pltpu.SemaphoreType.DMA((N_CHIPS,))],
        compiler_params=_params(),
    )(u, ost, x2, *wts, wpool, pool_scale, g_ffn)


def _ffn_ple(hn2, h1, p2, tgt, wts, g_ffn, g_ple):
    s_len = h1.shape[0]
    t = 256
    n_tiles = s_len // t

    def body(hn_ref, h1_ref, p_ref, tgt_ref, sl_ref, lo_ref, me_ref, gf_ref, gp_ref,
             loss_ref, dgate_ref, dup_ref, act_ref, dh2b_ref, hn3_ref, dgl_ref, dpp_ref, dh1_ref, dgf_ref, dgp_ref,
             wg_ref, wu_ref, wd_ref, wl_ref, wp_ref, packed_ref, gate_s, up_s, loss_acc, sems):
        i = pl.program_id(0)

        @pl.when(i == 0)
        def _():
            w_refs = (sl_ref, lo_ref, me_ref)
            _load_rows(w_refs, "gateT", wg_ref, sems)
            _load_rows(w_refs, "upT", wu_ref, sems)
            _load_rows(w_refs, "down", wd_ref, sems)
            _load_rows(w_refs, "plg", wl_ref, sems)
            _load_rows(w_refs, "plp", packed_ref, sems)
            for j in range(N_CHIPS):
                for q in range(4):
                    wp_ref[pl.ds(64 * q, 64), 256 * j:256 * j + 256] = packed_ref[pl.ds(64 * j, 64), 256 * q:256 * q + 256]
            loss_acc[...] = jnp.zeros_like(loss_acc)
            dgf_ref[...] = jnp.zeros_like(dgf_ref)
            dgp_ref[...] = jnp.zeros_like(dgp_ref)

        hn = hn_ref[...]
        h1v = h1_ref[...]
        h2 = h1v
        for ch in range(N_CHIPS):
            rows = pl.ds(ch * FF_CHUNK, FF_CHUNK)
            gate = _dot(hn, wg_ref[rows, :], 1, 1)
            up = _dot(hn, wu_ref[rows, :], 1, 1)
            gate_s[ch] = gate
            up_s[ch] = up
            act = (gate * _sigmoid(gate) * up).astype(BF16)
            act_ref[ch] = act
            h2 = h2 + _dot(act, wd_ref[rows, :], 1, 0)
        gp = gp_ref[...]
        hn3 = _rms_fwd(h2, gp).astype(BF16)
        hn3_ref[...] = hn3
        gate2 = _sigmoid(_dot(hn3, wl_ref[...], 1, 0))
        pp = _dot(p_ref[...].astype(BF16), wp_ref[...], 1, 0)
        err = h2 + gate2 * pp - tgt_ref[...]
        loss_acc[...] += jnp.sum(err * err, axis=0, keepdims=True)
        dy = err * (1.0 / D_MODEL)
        dpp_ref[...] = (dy * gate2).astype(BF16)
        dgl = (dy * pp * gate2 * (1.0 - gate2)).astype(BF16)
        dgl_ref[...] = dgl
        dx3, dg3 = _rms_bwd(h2, gp, _dot(dgl, wl_ref[...], 1, 1))
        dh2 = dy + dx3
        dgp_ref[...] += dg3
        dh2b = dh2.astype(BF16)
        dh2b_ref[...] = dh2b
        dhn = jnp.zeros((t, D_MODEL), F32)
        for ch in range(N_CHIPS):
            rows = pl.ds(ch * FF_CHUNK, FF_CHUNK)
            dact = _dot(dh2b, wd_ref[rows, :], 1, 1)
            gate_v = gate_s[ch]
            up_v = up_s[ch]
            sg = _sigmoid(gate_v)
            dup = (dact * (gate_v * sg)).astype(BF16)
            dgate = (dact * up_v * (sg * (1.0 + gate_v * (1.0 - sg)))).astype(BF16)
            dup_ref[ch] = dup
            dgate_ref[ch] = dgate
            dhn = dhn + _dot(dgate, wg_ref[rows, :], 1, 0) + _dot(dup, wu_ref[rows, :], 1, 0)
        dx, dg = _rms_bwd(h1v, gf_ref[...], dhn)
        dh1_ref[...] = dh2 + dx
        dgf_ref[...] += dg

        @pl.when(i == n_tiles - 1)
        def _():
            total = jnp.sum(loss_acc[...], axis=-1, keepdims=True) * (0.5 / D_MODEL)
            loss_ref[...] = jnp.broadcast_to(total, loss_ref.shape)

    row = lambda w: pl.BlockSpec((t, w), lambda i: (i, 0))
    chunked = pl.BlockSpec((N_CHIPS, t, FF_CHUNK), lambda i: (0, i, 0))
    vec = _full((1, D_MODEL))
    act_shape = jax.ShapeDtypeStruct((N_CHIPS, s_len, FF_CHUNK), BF16)
    tok = lambda dtype: jax.ShapeDtypeStruct((s_len, D_MODEL), dtype)
    return pl.pallas_call(
        body, name="ffn_ple", grid=(n_tiles,),
        in_specs=[row(D_MODEL), row(D_MODEL), row(PLE_DIM), row(D_MODEL)] + W_SPECS + [vec, vec],
        out_specs=[_full((1, 128)), chunked, chunked, chunked] + [row(D_MODEL)] * 5 + [vec, vec],
        out_shape=[jax.ShapeDtypeStruct((1, 128), F32), act_shape, act_shape, act_shape, tok(BF16), tok(BF16), tok(BF16),
                   tok(BF16), tok(F32), jax.ShapeDtypeStruct((1, D_MODEL), F32), jax.ShapeDtypeStruct((1, D_MODEL), F32)],
        scratch_shapes=[pltpu.VMEM((D_FF, D_MODEL), BF16)] * 3
        + [pltpu.VMEM((D_MODEL, D_MODEL), BF16), pltpu.VMEM((PLE_DIM, D_MODEL), BF16), pltpu.VMEM((PLE_DIM, D_MODEL), BF16),
           pltpu.VMEM((N_CHIPS, t, FF_CHUNK), F32), pltpu.VMEM((N_CHIPS, t, FF_CHUNK), F32), pltpu.VMEM((1, D_MODEL), F32),
           pltpu.SemaphoreType.DMA((N_CHIPS,))],
        compiler_params=_params(VMEM_LIMIT_BIG),
    )(hn2, h1, p2, tgt, *wts, g_ffn, g_ple)


def _mix_out_bwd(dh1, wts, pooled, wpool, pool_scale, after):
    s_len = dh1.shape[0]
    t = 512
    n = t + 16
    n_tiles = s_len // t

    def body(dh1_ref, sl_ref, lo_ref, me_ref, pooled_ref, wp_ref, sc_ref, after_ref, dost_ref, du_ref, dyp_ref, dsc_ref,
             w_ref, ext_ref, st_ref, sems):
        del after_ref
        i = pl.program_id(0)

        @pl.when(i == 0)
        def _():
            _load_rows((sl_ref, lo_ref, me_ref), "out", w_ref, sems)
            ext_ref[...] = jnp.zeros_like(ext_ref)
            st_ref[...] = jnp.zeros_like(st_ref)
            dsc_ref[...] = jnp.zeros_like(dsc_ref)

        dmix = _dot(dh1_ref[...].astype(BF16), w_ref[...], 1, 1)
        lo = lax.broadcasted_iota(jnp.int32, (t, 128), 1) < 64
        for p in range(4):
            even, odd = _to_stacked(dmix[:, 128 * p:128 * p + 128], p // 2, lo)
            dost_ref[2 * p] = even.astype(BF16)
            dost_ref[2 * p + 1] = odd.astype(BF16)
        pooled_v = pooled_ref[...]
        counts = _pool_counts(n_tiles - 1 - i, t)
        for g in range(4):
            cols = slice(128 * g, 128 * g + 128)
            dm = dmix[:, ATTN_WIDTH + 128 * g:ATTN_WIDTH + 128 * g + 128]
            ypre = _dot(pooled_v[:, cols], wp_ref[g], 1, 0)
            dsc_ref[:, cols] += jnp.sum(ypre * dm, axis=0, keepdims=True)
            dyp = (dm * sc_ref[:, cols]).astype(BF16)
            dyp_ref[:, cols] = dyp
            dpooled = _dot(dyp, wp_ref[g], 1, 1)
            du_ref[:, cols] = -dpooled
            ext_ref[pl.ds(0, t), cols] = dpooled / counts[:, cols]
        st_ref[pl.ds(0, n), :] = ext_ref[pl.ds(0, n), :] + ext_ref[pl.ds(1, n), :]
        st_ref[pl.ds(0, n), 128:] = st_ref[pl.ds(0, n), 128:] + st_ref[pl.ds(2, n), 128:]
        st_ref[pl.ds(0, n), 256:] = st_ref[pl.ds(0, n), 256:] + st_ref[pl.ds(4, n), 256:]
        st_ref[pl.ds(0, n), 384:] = st_ref[pl.ds(0, n), 384:] + st_ref[pl.ds(8, n), 384:]
        ext_ref[pl.ds(t, POOL_HALO), :] = ext_ref[pl.ds(0, POOL_HALO), :]
        du_ref[...] += st_ref[pl.ds(0, t), :]

    rev = lambda w: pl.BlockSpec((t, w), lambda i: (n_tiles - 1 - i, 0))
    return pl.pallas_call(
        body, name="mix_out_bwd", grid=(n_tiles,),
        in_specs=[rev(D_MODEL)] + W_SPECS + [rev(POOL_WIDTH), _full((4, 128, 128)), _full((1, POOL_WIDTH)), ANY],
        out_specs=[pl.BlockSpec((N_Q_HEADS, t, 128), lambda i: (0, n_tiles - 1 - i, 0)), rev(POOL_WIDTH), rev(POOL_WIDTH),
                   _full((1, POOL_WIDTH))],
        out_shape=[jax.ShapeDtypeStruct((N_Q_HEADS, s_len, 128), BF16), jax.ShapeDtypeStruct((s_len, POOL_WIDTH), F32),
                   jax.ShapeDtypeStruct((s_len, POOL_WIDTH), BF16), jax.ShapeDtypeStruct((1, POOL_WIDTH), F32)],
        scratch_shapes=[pltpu.VMEM((D_MODEL, D_MODEL), BF16), pltpu.VMEM((t + POOL_HALO, POOL_WIDTH), F32),
                        pltpu.VMEM((t + POOL_HALO, POOL_WIDTH), F32), pltpu.SemaphoreType.DMA((N_CHIPS,))],
        compiler_params=_params(),
    )(dh1, *wts, pooled, wpool, pool_scale, after)


def _attn_bwd(qst, kn, vb, dost, bias_st, sinks, after):
    s_len = kn.shape[0]

    def body(q_ref, kp_ref, kc_ref, vp_ref, vc_ref, do_ref, bias_ref, sink_ref, after_ref, dq_ref, dk_ref, dv_ref, dbias_ref,
             dsink_ref, s_ref, dp_ref, p_ref, dl_ref):
        del after_ref
        i = pl.program_id(0)

        @pl.when(i == 0)
        def _():
            dk_ref[...] = jnp.zeros_like(dk_ref)
            dv_ref[...] = jnp.zeros_like(dv_ref)
            dbias_ref[...] = jnp.zeros_like(dbias_ref)
            dsink_ref[...] = jnp.zeros_like(dsink_ref)

        q = q_ref[...].reshape(N_Q_HEADS * BLOCK, 128)
        do = do_ref[...].reshape(N_Q_HEADS * BLOCK, 128)
        k2 = jnp.concatenate([kp_ref[...], kc_ref[...]], axis=0)
        s_ref[...] = _dot(q, k2, 1, 1)
        dp_ref[...] = _dot(do, jnp.concatenate([vp_ref[...], vc_ref[...]], axis=0), 1, 1)

        def head(h, carry):
            rows, probs, p_sink = _head_softmax(s_ref, bias_ref, sink_ref, h)
            dp = dp_ref[rows, :]
            dsum = jnp.sum(probs * dp, axis=-1, keepdims=True)
            dlog = probs * (dp - dsum)
            dsink_ref[rows, :] -= p_sink * dsum
            dbias_ref[rows, :] += dlog
            p_ref[rows, :] = probs.astype(BF16)
            dl_ref[rows, :] = (dlog * (HEAD_DIM ** -0.5)).astype(BF16)
            return carry

        lax.fori_loop(0, N_Q_HEADS, head, 0, unroll=True)
        dlog_s = dl_ref[...]
        dq_ref[...] = jnp.where(_head_lane_mask(), _dot(dlog_s, k2, 1, 0), 0.0).reshape(N_Q_HEADS, BLOCK, 128)
        dk2 = _dot(dlog_s, q, 0, 0)
        dv2 = _dot(p_ref[...], do, 0, 0)
        prev_rows = pl.ds(pl.multiple_of(jnp.maximum(i - 1, 0) * BLOCK, BLOCK), BLOCK)
        cur_rows = pl.ds(pl.multiple_of(i * BLOCK, BLOCK), BLOCK)
        dk_ref[prev_rows, :] += dk2[:BLOCK]
        dk_ref[cur_rows, :] += dk2[BLOCK:]
        dv_ref[prev_rows, :] += dv2[:BLOCK]
        dv_ref[cur_rows, :] += dv2[BLOCK:]

    stacked, kv, consts = _attn_specs()
    band = (N_Q_HEADS * BLOCK, 2 * BLOCK)
    return pl.pallas_call(
        body, name="attn_bwd", grid=(s_len // BLOCK,),
        in_specs=[stacked] + kv + kv + [stacked] + consts + [ANY],
        out_specs=[stacked, _full((s_len, 128)), _full((s_len, 128)), _full(band), _full((N_Q_HEADS * BLOCK, 1))],
        out_shape=[jax.ShapeDtypeStruct((N_Q_HEADS, s_len, 128), F32), jax.ShapeDtypeStruct((s_len, 128), F32),
                   jax.ShapeDtypeStruct((s_len, 128), F32), jax.ShapeDtypeStruct(band, F32),
                   jax.ShapeDtypeStruct((N_Q_HEADS * BLOCK, 1), F32)],
        scratch_shapes=[pltpu.VMEM(band, F32), pltpu.VMEM(band, F32), pltpu.VMEM(band, BF16), pltpu.VMEM(band, BF16)],
        compiler_params=_params(),
    )(qst, kn, kn, vb, vb, dost, bias_st, sinks, after)


def _small_pack(dg_attn, dg_ffn, dg_ple, dscale, dgq, dgk, dbias, dsink_rows, bucket, loss_v, dwpool):
    def body(ga_ref, gf_ref, gp_ref, sc_ref, gq_ref, gk_ref, db_ref, ds_ref, bucket_ref, loss_ref, wp_ref, out_ref):
        out_ref[pl.ds(0, SMALL["w_pool"]), :] = jnp.zeros((SMALL["w_pool"], 128), F32)
        for name, ref, n in (("g_attn", ga_ref, 8), ("g_ffn", gf_ref, 8), ("g_ple", gp_ref, 8), ("pool_scale", sc_ref, 4)):
            for k in range(n):
                out_ref[pl.ds(SMALL[name] + k, 1), :] = ref[:, 128 * k:128 * k + 128]
        for name, ref in (("g_q", gq_ref), ("g_k", gk_ref)):
            both = ref[...]
            out_ref[pl.ds(SMALL[name], 1), :] = both + pltpu.roll(both, 64, axis=1)
        out_ref[pl.ds(SMALL["loss"], 1), :] = loss_ref[...]
        bk = bucket_ref[...]
        rows = lax.broadcasted_iota(jnp.int32, (N_BUCKETS, 128), 0)
        lanes = lax.broadcasted_iota(jnp.int32, (N_BUCKETS, 128), 1)
        lane1 = lax.broadcasted_iota(jnp.int32, (1, 128), 1)
        rb = jnp.zeros((N_BUCKETS, 128), F32)
        sk = jnp.zeros((1, 128), F32)
        for h in range(N_Q_HEADS):
            band = db_ref[pl.ds(h * BLOCK, BLOCK), :]
            for b in range(N_BUCKETS):
                rb = jnp.where((rows == b) & (lanes == h), jnp.sum(jnp.where(bk == b, band, 0.0)), rb)
            sk = jnp.where(lane1 == h, jnp.sum(ds_ref[pl.ds(h * BLOCK, BLOCK), :]), sk)
        out_ref[pl.ds(SMALL["rel_bias"], N_BUCKETS), :] = rb
        out_ref[pl.ds(SMALL["sinks"], 1), :] = sk
        out_ref[pl.ds(SMALL["w_pool"], 512), :] = wp_ref[...].reshape(512, 128)

    return pl.pallas_call(
        body, name="small_pack", in_specs=[VMEM_WHOLE] * 11, out_specs=VMEM_WHOLE,
        out_shape=jax.ShapeDtypeStruct((SMALL_ROWS, 128), F32),
    )(dg_attn, dg_ffn, dg_ple, dscale, dgq, dgk, dbias, dsink_rows, bucket, loss_v, dwpool)


def _attn_in_bwd(dqst, zqk, dk, dv, du, x2, dh1, wts, g_attn, gq, gk):
    s_len = x2.shape[0]
    t = 512

    def body(dq_ref, zqk_ref, dk_ref, dv_ref, du_ref, x_ref, dh1_ref, sl_ref, lo_ref, me_ref, g_ref, gq_ref, gk_ref,
             dz_ref, dx_ref, dg_ref, dgq_ref, dgk_ref, w_ref, sems):
        @pl.when(pl.program_id(0) == 0)
        def _():
            _load_rows((sl_ref, lo_ref, me_ref), "inT", w_ref, sems)
            dg_ref[...] = jnp.zeros_like(dg_ref)
            dgq_ref[...] = jnp.zeros_like(dgq_ref)
            dgk_ref[...] = jnp.zeros_like(dgk_ref)

        lo = lax.broadcasted_iota(jnp.int32, (t, 128), 1) < 64
        for p in range(4):
            dqn = _from_stacked(dq_ref[2 * p], dq_ref[2 * p + 1], p // 2, lo)
            dq_raw, dgq = _pair_norm_bwd(zqk_ref[:, 128 * p:128 * p + 128], gq_ref[...], dqn)
            dz_ref[:, 128 * p:128 * p + 128] = dq_raw.astype(BF16)
            dgq_ref[...] += dgq
        dk_raw, dgk = _pair_norm_bwd(zqk_ref[:, 512:640], gk_ref[...], dk_ref[...])
        dgk_ref[...] += dgk
        dz_ref[:, 512:640] = dk_raw.astype(BF16)
        dz_ref[:, 640:768] = dv_ref[...].astype(BF16)
        dz_ref[:, 768:] = du_ref[...].astype(BF16)
        dx, dg = _rms_bwd(x_ref[...], g_ref[...], _dot(dz_ref[...], w_ref[...], 1, 0))
        dx_ref[...] = dh1_ref[...] + dx
        dg_ref[...] += dg

    row = lambda w: pl.BlockSpec((t, w), lambda i: (i, 0))
    return pl.pallas_call(
        body, name="attn_in_bwd", grid=(s_len // t,),
        in_specs=[pl.BlockSpec((N_Q_HEADS, t, 128), lambda i: (0, i, 0)), row(640), row(128), row(128), row(POOL_WIDTH),
                  row(D_MODEL), row(D_MODEL)] + W_SPECS + [_full((1, D_MODEL)), _full((1, 128)), _full((1, 128))],
        out_specs=[row(IN_WIDTH), row(D_MODEL), _full((1, D_MODEL)), _full((1, 128)), _full((1, 128))],
        out_shape=[jax.ShapeDtypeStruct((s_len, IN_WIDTH), BF16), jax.ShapeDtypeStruct((s_len, D_MODEL), F32),
                   jax.ShapeDtypeStruct((1, D_MODEL), F32), jax.ShapeDtypeStruct((1, 128), F32),
                   jax.ShapeDtypeStruct((1, 128), F32)],
        scratch_shapes=[pltpu.VMEM((IN_WIDTH, D_MODEL), BF16), pltpu.SemaphoreType.DMA((N_CHIPS,))],
        compiler_params=_params(),
    )(dqst, zqk, dk, dv, du, x2, dh1, *wts, g_attn, gq, gk)


def _dw(a, b, name, into=None):
    tk = 1024
    n_out = b.shape[1]
    if a.ndim == 3:
        s_len, tm = a.shape[1:]
        m = N_CHIPS * tm
        a_spec = pl.BlockSpec((None, tk, tm), lambda i, k: (i, k, 0))
    else:
        s_len, m = a.shape
        tm = m // 2 if m > 1408 else m
        a_spec = pl.BlockSpec((tk, tm), lambda i, k: (k, i))
    n_steps = s_len // tk
    chunk = m // N_CHIPS
    per_tile = tm // chunk

    b_resident = b.dtype == BF16 and m // tm > 1

    def accumulate(a_ref, b_ref, acc_ref, k):
        def b_tile():
            rows = pl.ds(pl.multiple_of(k * tk, tk), tk) if b_resident else slice(None)
            return b_ref[rows, :].astype(BF16)

        @pl.when(k == 0)
        def _():
            acc_ref[...] = _dot(a_ref[...].astype(BF16), b_tile(), 0, 0)

        @pl.when(k > 0)
        def _():
            acc_ref[...] += _dot(a_ref[...].astype(BF16), b_tile(), 0, 0)

    b_spec = pl.BlockSpec((s_len, n_out), lambda i, k: (0, 0)) if b_resident else pl.BlockSpec((tk, n_out), lambda i, k: (k, 0))
    in_specs = [a_spec, b_spec]
    if into is None:
        def body(a_ref, b_ref, o_ref, acc_ref):
            k = pl.program_id(1)
            accumulate(a_ref, b_ref, acc_ref, k)

            @pl.when(k == n_steps - 1)
            def _():
                o_ref[...] = acc_ref[...].astype(BF16)

        return pl.pallas_call(
            body, name=name, grid=(m // tm, n_steps), in_specs=in_specs,
            out_specs=pl.BlockSpec((tm, n_out), lambda i, k: (i, 0)), out_shape=jax.ShapeDtypeStruct((m, n_out), BF16),
            scratch_shapes=[pltpu.VMEM((tm, n_out), F32)], compiler_params=_params(n_axes=2),
        )(a, b)

    slab, slab_rows, row_off = into
    assert n_out == D_MODEL

    def body_into(a_ref, b_ref, *rest):
        o_ref, acc_ref, stage_ref, sems = rest[-4:]
        i, k = pl.program_id(0), pl.program_id(1)
        accumulate(a_ref, b_ref, acc_ref, k)

        @pl.when(k == n_steps - 1)
        def _():
            stage_ref[...] = acc_ref[...].astype(BF16)
            copies = [pltpu.make_async_copy(stage_ref.at[pl.ds(jj * chunk, chunk), :],
                                            o_ref.at[i * per_tile + jj, pl.ds(row_off, chunk), :], sems.at[jj])
                      for jj in range(per_tile)]
            for cp in copies:
                cp.start()
            for cp in copies:
                cp.wait()

    operands, aliases = [a, b], {}
    if slab is not None:
        in_specs = in_specs + [ANY]
        operands.append(slab)
        aliases = {2: 0}
    return pl.pallas_call(
        body_into, name=name, grid=(m // tm, n_steps), in_specs=in_specs, out_specs=ANY,
        out_shape=jax.ShapeDtypeStruct((N_CHIPS, slab_rows, D_MODEL), BF16), input_output_aliases=aliases,
        scratch_shapes=[pltpu.VMEM((tm, n_out), F32), pltpu.VMEM((tm, n_out), BF16), pltpu.SemaphoreType.DMA((per_tile,))],
        compiler_params=_params(n_axes=2),
    )(*operands)


def _dw_pool(pooled, dyp):
    s_len = pooled.shape[0]
    tk = 512

    def body(a_ref, b_ref, o_ref):
        @pl.when(pl.program_id(0) == 0)
        def _():
            o_ref[...] = jnp.zeros_like(o_ref)

        for g in range(4):
            cols = slice(128 * g, 128 * g + 128)
            o_ref[g] += _dot(a_ref[:, cols], b_ref[:, cols], 0, 0)

    blk = pl.BlockSpec((tk, POOL_WIDTH), lambda k: (k, 0))
    return pl.pallas_call(
        body, name="dw_pool", grid=(s_len // tk,), in_specs=[blk, blk], out_specs=_full((4, 128, 128)),
        out_shape=jax.ShapeDtypeStruct((4, 128, 128), F32), compiler_params=_params(),
    )(pooled, dyp)


def _position():
    x, y, c = lax.axis_index("x"), lax.axis_index("y"), lax.axis_index("c")
    other_chips = [(1 - x, y), (x, 1 - y), (1 - x, 1 - y)]
    return x, y, c, other_chips


def _ag_weights(local_slab, row0, n_rows, name, collective_id):
    half = n_rows // 2
    quarter = half // 2
    assert quarter % 16 == 0

    def body(l_ref, g_ref, send, recv):
        x, y, c, chips = _position()
        me, (via_x, via_y, diagonal) = 2 * x + y, [2 * chip[0] + chip[1] for chip in chips]
        here, sibling, x_nbr, y_nbr = (x, y, c), (x, y, 1 - c), (1 - x, y, c), (x, 1 - y, c)
        peers = [sibling, x_nbr, y_nbr]
        barrier = pltpu.get_barrier_semaphore()
        for peer in peers:
            pl.semaphore_signal(barrier, inc=1, device_id=peer, device_id_type=MESH)
        pl.semaphore_wait(barrier, len(peers))

        def rows(core, part):
            start, size = (core * half, half) if part is None else (core * half + part * quarter, quarter)
            return pl.ds(pl.multiple_of(start, 16), size)

        def copy(k, chip_idx, where, to, src=None):
            dst = g_ref.at[chip_idx, where, :]
            return pltpu.make_async_remote_copy(src_ref=dst if src is None else src, dst_ref=dst, send_sem=send.at[k],
                                                recv_sem=recv.at[k], device_id=to, device_id_type=MESH)

        own_rows = l_ref.at[pl.ds(pl.multiple_of(row0 + c * half, 16), half), :]
        started = [copy(0, me, rows(c, None), x_nbr, src=own_rows), copy(1, me, rows(c, None), y_nbr, src=own_rows)]
        for cp in started:
            cp.start()
        after_arrival = [
            (copy(0, via_x, rows(c, None), here), [copy(4, via_x, rows(c, None), sibling), copy(3, via_x, rows(c, 1), y_nbr)]),
            (copy(1, via_y, rows(c, None), here), [copy(5, via_y, rows(c, None), sibling), copy(2, via_y, rows(c, 0), x_nbr)]),
            (copy(2, diagonal, rows(c, 0), here), [copy(6, diagonal, rows(c, 0), sibling)]),
            (copy(3, diagonal, rows(c, 1), here), [copy(7, diagonal, rows(c, 1), sibling)]),
        ]
        for arrival, onward in after_arrival:
            arrival.wait_recv()
            for cp in onward:
                cp.start()
            started += onward
        for cp in (copy(4, via_x, rows(1 - c, None), here), copy(5, via_y, rows(1 - c, None), here),
                   copy(6, diagonal, rows(1 - c, 0), here), copy(7, diagonal, rows(1 - c, 1), here)):
            cp.wait_recv()
        for cp in started:
            cp.wait_send()

    return pl.kernel(
        body, out_type=jax.ShapeDtypeStruct((N_CHIPS, n_rows, D_MODEL), BF16),
        mesh=plsc.ScalarSubcoreMesh(axis_name="sequencer", num_cores=1), name=name,
        scratch_types=[pltpu.SemaphoreType.DMA((8,)), pltpu.SemaphoreType.DMA((8,))],
        compiler_params=pltpu.CompilerParams(collective_id=collective_id),
    )(local_slab)


def _comm_call(body, peers_of, out_shape, n_sems, operand, name, collective_id):
    sems = [pltpu.SemaphoreType.DMA((n_sems,)), pltpu.SemaphoreType.DMA((n_sems,))]
    if collective_id is None:
        return pl.pallas_call(body, name=name, in_specs=[ANY], out_specs=ANY, out_shape=out_shape, scratch_shapes=sems)(operand)

    def with_handshake(in_ref, out_ref, send, recv):
        x, y, c, _ = _position()
        peers = peers_of(x, y, c)
        barrier = pltpu.get_barrier_semaphore()
        for peer in peers:
            pl.semaphore_signal(barrier, inc=1, device_id=peer, device_id_type=MESH)
        pl.semaphore_wait(barrier, len(peers))
        body(in_ref, out_ref, send, recv)

    return pl.kernel(with_handshake, out_type=out_shape, mesh=plsc.ScalarSubcoreMesh(axis_name="sequencer", num_cores=1),
                     name=name, scratch_types=sems, compiler_params=pltpu.CompilerParams(collective_id=collective_id))(operand)


def _rs_swap_halves(partial, name, collective_id=None):
    half = partial.shape[1] // 2

    def body(p_ref, r_ref, send, recv):
        x, y, c, _ = _position()
        theirs = pl.ds(pl.multiple_of((1 - c) * half, 16), half)
        cp = pltpu.make_async_remote_copy(src_ref=p_ref.at[:, theirs, :], dst_ref=r_ref, send_sem=send.at[0],
                                          recv_sem=recv.at[0], device_id=(x, y, 1 - c), device_id_type=MESH)
        cp.start()
        cp.wait()

    return _comm_call(body, lambda x, y, c: [(x, y, 1 - c)], jax.ShapeDtypeStruct((N_CHIPS, half, D_MODEL), BF16), 1,
                      partial, name, collective_id)


def _rs_add_halves(partial, other, core, name, after):
    half = other.shape[1]
    t = half // 2
    steps = half // t

    def body(core_ref, a_ref, b_ref, after_ref, o_ref):
        del after_ref
        o_ref[...] = (a_ref[...].astype(F32) + b_ref[...].astype(F32)).astype(BF16)

    return pl.pallas_call(
        body, name=name,
        grid_spec=pltpu.PrefetchScalarGridSpec(
            num_scalar_prefetch=1, grid=(N_CHIPS, steps),
            in_specs=[pl.BlockSpec((1, t, D_MODEL), lambda j, i, core_ref: (j, core_ref[0] * steps + i, 0)),
                      pl.BlockSpec((1, t, D_MODEL), lambda j, i, core_ref: (j, i, 0)), ANY],
            out_specs=pl.BlockSpec((1, t, D_MODEL), lambda j, i, core_ref: (j, i, 0))),
        out_shape=jax.ShapeDtypeStruct((N_CHIPS, half, D_MODEL), BF16),
        compiler_params=_params(n_axes=2),
    )(core, partial, other, after)


def _rs_exchange_chips(pre, name, collective_id=None):
    def body(s_ref, r_ref, send, recv):
        x, y, c, chips = _position()

        def copy(k, chunk, to):
            return pltpu.make_async_remote_copy(src_ref=s_ref.at[chunk], dst_ref=r_ref.at[k], send_sem=send.at[k],
                                                recv_sem=recv.at[k], device_id=to, device_id_type=MESH)

        sends = [copy(k, 2 * chip[0] + chip[1], (*chip, c)) for k, chip in enumerate(chips)]
        for cp in sends:
            cp.start()
        for cp in sends:
            cp.wait()

    return _comm_call(body, lambda x, y, c: [(1 - x, y, c), (x, 1 - y, c), (1 - x, 1 - y, c)],
                      jax.ShapeDtypeStruct((3, pre.shape[1], D_MODEL), BF16), 3, pre, name, collective_id)


def _rs_sum_chips(pre, received, place, name, after):
    half = pre.shape[1]
    t = half // 2 if half > 512 else half
    steps = half // t

    def body(place_ref, own_ref, r_ref, after_ref, o_ref):
        del after_ref
        acc = own_ref[0].astype(F32)
        for k in range(3):
            acc = acc + r_ref[k].astype(F32)
        o_ref[...] = acc

    return pl.pallas_call(
        body, name=name,
        grid_spec=pltpu.PrefetchScalarGridSpec(
            num_scalar_prefetch=1, grid=(steps,),
            in_specs=[pl.BlockSpec((1, t, D_MODEL), lambda i, place_ref: (place_ref[0], i, 0)),
                      pl.BlockSpec((3, t, D_MODEL), lambda i, place_ref: (0, i, 0)), ANY],
            out_specs=pl.BlockSpec((t, D_MODEL), lambda i, place_ref: (place_ref[1] * steps + i, 0))),
        out_shape=jax.ShapeDtypeStruct((2 * half, D_MODEL), F32),
        compiler_params=_params(),
    )(place, pre, received, after)


def _half_swap(g_ref, core, to, send, recv, k):
    half = g_ref.shape[0] // 2
    rows = g_ref.at[pl.ds(pl.multiple_of(core * half, 8), half), :]
    return pltpu.make_async_remote_copy(src_ref=rows, dst_ref=rows, send_sem=send.at[k], recv_sem=recv.at[k],
                                        device_id=to, device_id_type=MESH)


def _rs_finish_rows(grads, name):
    def body(f_ref, g_ref, send, recv):
        del f_ref
        x, y, c, _ = _position()
        mine = _half_swap(g_ref, c, (x, y, 1 - c), send, recv, 0)
        mine.start()
        _half_swap(g_ref, 1 - c, (x, y, c), send, recv, 0).wait_recv()
        mine.wait_send()

    return pl.pallas_call(
        body, name=name, in_specs=[ANY], out_specs=ANY, input_output_aliases={0: 0},
        out_shape=jax.ShapeDtypeStruct(grads.shape, F32),
        scratch_shapes=[pltpu.SemaphoreType.DMA((1,)), pltpu.SemaphoreType.DMA((1,))],
    )(grads)


def _rs_finish(grads_a, small, after):
    def body(fa_ref, s_ref, after_ref, ga_ref, t_ref, send, recv, local_sem):
        del fa_ref, after_ref
        x, y, c, chips = _position()
        sibling = (x, y, 1 - c)

        def slot(px, py, pc):
            return t_ref.at[4 * px + 2 * py + pc]

        def copy(k, block, to, src=None):
            return pltpu.make_async_remote_copy(src_ref=slot(*block) if src is None else src, dst_ref=slot(*block),
                                                send_sem=send.at[k], recv_sem=recv.at[k], device_id=to, device_id_type=MESH)

        own_small = pltpu.make_async_copy(s_ref, slot(x, y, c), local_sem)
        own_small.start()
        to_sibling = [_half_swap(ga_ref, c, sibling, send, recv, 7)]
        for cp in to_sibling:
            cp.start()
        first = [copy(0, (x, y, c), sibling, src=s_ref)]
        first += [copy(1 + k, (x, y, c), (*chip, c), src=s_ref) for k, chip in enumerate(chips)]
        for cp in first:
            cp.start()
        passed = []
        for k, chip in enumerate(chips):
            copy(1 + k, (*chip, c), (x, y, c)).wait_recv()
            fwd = copy(4 + k, (*chip, c), sibling)
            fwd.start()
            passed.append(fwd)
        copy(0, sibling, (x, y, c)).wait_recv()
        for k, chip in enumerate(chips):
            copy(4 + k, (*chip, 1 - c), (x, y, c)).wait_recv()
        _half_swap(ga_ref, 1 - c, (x, y, c), send, recv, 7).wait_recv()
        for cp in first + passed + to_sibling:
            cp.wait_send()
        own_small.wait()

    return pl.pallas_call(
        body, name="rs_finish_a", in_specs=[ANY, ANY, ANY], out_specs=[ANY, ANY], input_output_aliases={0: 0},
        out_shape=[jax.ShapeDtypeStruct(grads_a.shape, F32), jax.ShapeDtypeStruct((N_DEV, SMALL_ROWS, 128), F32)],
        scratch_shapes=[pltpu.SemaphoreType.DMA((8,)), pltpu.SemaphoreType.DMA((8,)), pltpu.SemaphoreType.DMA],
    )(grads_a, small, after)


def _adam_update(w, g, m, v):
    m_new = ADAM_B1 * m + (1.0 - ADAM_B1) * g
    v_new = ADAM_B2 * v + (1.0 - ADAM_B2) * (g * g)
    m_hat = m_new / (1.0 - ADAM_B1 ** ADAM_STEP)
    v_hat = v_new / (1.0 - ADAM_B2 ** ADAM_STEP)
    return -ADAM_LR * (m_hat / (jnp.sqrt(v_hat) + ADAM_EPS) + ADAM_WD * w), m_new, v_new


def _adamw(w, g_rows, row_off, m, v, name):
    rows, cols = w.shape
    t = rows if rows <= 320 else (rows // 2 if rows % 256 else 256)

    def body(w_ref, g_ref, m_ref, v_ref, go_ref, d_ref, nm_ref, nv_ref):
        g = g_ref[...]
        go_ref[...] = g
        d_ref[...], nm_ref[...], nv_ref[...] = _adam_update(w_ref[...], g, m_ref[...], v_ref[...])

    blk = pl.BlockSpec((t, cols), lambda i: (i, 0))
    assert row_off % 8 == 0 and t % 8 == 0
    g_blk = pl.BlockSpec((pl.Element(t), pl.Element(cols)), lambda i: (pl.multiple_of(row_off + i * t, 8), 0))
    shape = jax.ShapeDtypeStruct((rows, cols), F32)
    return pl.pallas_call(
        body, name=name, grid=(rows // t,), in_specs=[blk, g_blk, blk, blk], out_specs=[blk] * 4, out_shape=[shape] * 4,
        compiler_params=_params(),
    )(w, g_rows, m, v)


SMALL_PARAMS = [("g_attn", (1, D_MODEL), 8), ("g_q", (1, HEAD_DIM), None), ("g_k", (1, HEAD_DIM), None),
                ("sinks", (1, N_Q_HEADS), None), ("rel_bias", (N_BUCKETS, N_Q_HEADS), None), ("w_pool", (512, 128), None),
                ("pool_scale", (1, POOL_WIDTH), 4), ("g_ffn", (1, D_MODEL), 8), ("g_ple", (1, D_MODEL), 8)]


def _adamw_small(tables, wmv):
    n_par = len(SMALL_PARAMS)

    def body(*refs):
        t_ref = refs[0]
        ins = refs[1:1 + 3 * n_par]
        loss_ref = refs[1 + 3 * n_par]
        outs = refs[2 + 3 * n_par:-1]
        tot_ref = refs[-1]
        total = t_ref[0]
        for d in range(1, N_DEV):
            total = total + t_ref[d]
        tot_ref[...] = total
        loss_ref[...] = tot_ref[pl.ds(SMALL["loss"], 1), 0:1]
        for i, (name, shape, split) in enumerate(SMALL_PARAMS):
            g_ref, d_ref, nm_ref, nv_ref = outs[4 * i:4 * i + 4]
            row = SMALL[name]
            if split:
                for k in range(split):
                    g_ref[:, 128 * k:128 * k + 128] = tot_ref[pl.ds(row + k, 1), :]
            else:
                g_ref[...] = tot_ref[pl.ds(row, shape[0]), 0:shape[1]]
            w_ref, m_ref, v_ref = ins[3 * i:3 * i + 3]
            d_ref[...], nm_ref[...], nv_ref[...] = _adam_update(w_ref[...], g_ref[...], m_ref[...], v_ref[...])

    shapes = [jax.ShapeDtypeStruct((1, 1), F32)]
    for _, shape, _ in SMALL_PARAMS:
        shapes += [jax.ShapeDtypeStruct(shape, F32)] * 4
    flat = [a for triple in wmv for a in triple]
    res = pl.pallas_call(
        body, name="adamw_small", in_specs=[VMEM_WHOLE] * (1 + 3 * n_par), out_specs=[VMEM_WHOLE] * len(shapes),
        out_shape=shapes, scratch_shapes=[pltpu.VMEM((SMALL_ROWS, 128), F32)],
    )(tables, *flat)
    return res[0], [res[1 + 4 * i:5 + 4 * i] for i in range(n_par)]


def _pack_ple_proj(shard):
    return shard.reshape(4, 64, 256).transpose(1, 0, 2).reshape(64, D_MODEL)


class _Reduction:
    def __init__(self, tag, place, ids=(None, None)):
        self.tag, self.place, self.ids = tag, place, ids

    def start(self, partial):
        self.partial = partial
        self.other = _rs_swap_halves(partial, "rs_swap_" + self.tag, self.ids[0])
        return partial

    def middle(self, after):
        self.pre = _rs_add_halves(self.partial, self.other, self.place[1:], "rs_add_" + self.tag, after)
        self.received = _rs_exchange_chips(self.pre, "rs_exchange_" + self.tag, self.ids[1])
        return self.pre

    def finish(self, after):
        return _rs_sum_chips(self.pre, self.received, self.place, "rs_sum_" + self.tag, after)


def _local_grads(x2, p2, tgt, wts, g_attn_norm, g_q, g_k, attn_sinks, rel_bias, w_pool, pool_scale, g_ffn_norm, g_ple_norm,
                 reduce_a):
    part_in, part_out, part_rest, local_slab, me = wts
    w_in, w_out, w_late = (part_in, local_slab, me), (part_out, local_slab, me), (part_rest, local_slab, me)
    bucket = jnp.asarray(_bucket_table())
    gq = jnp.tile(g_q, (1, 2))
    gk = jnp.tile(g_k, (1, 2))
    wpool = w_pool[0].astype(BF16)
    sinks = attn_sinks[0]
    bias_st = _bias_build(rel_bias.T, bucket)

    hn1, zqk, u, kn, vb, qst = _attn_in(x2, g_attn_norm, gq, gk, w_in)
    ost = _attn_fwd(qst, kn, vb, bias_st, sinks)
    pooled, mix, h1, hn2 = _mix_out(u, ost, x2, w_out, wpool, pool_scale, g_ffn_norm)
    loss_v, dgate, dup, act, dh2, hn3, dgl, dpp, dh1, dg_ffn, dg_ple = _ffn_ple(hn2, h1, p2, tgt, w_late, g_ffn_norm,
                                                                                   g_ple_norm)

    rows_a = SLAB_ROWS - SLAB["inT"][1]
    partial_a = None
    for name, lhs, rhs in (("out", mix, dh1), ("gateT", dgate, hn2), ("upT", dup, hn2), ("down", act, dh2), ("plg", hn3, dgl)):
        partial_a = _dw(lhs, rhs, "dw_" + name, into=(partial_a, rows_a, SLAB[name][0] - SLAB["inT"][1]))
    dw_plp = _dw(p2, dpp, "dw_plp").reshape(4, 64, N_CHIPS, 256).transpose(2, 1, 0, 3).reshape(N_CHIPS, 64, D_MODEL)
    partial_a = reduce_a.start(lax.dynamic_update_slice(partial_a, dw_plp, (0, SLAB["plp"][0] - SLAB["inT"][1], 0)))
    dost, du, dyp, dscale = _mix_out_bwd(dh1, w_out, pooled, wpool, pool_scale, partial_a)
    pre_a = reduce_a.middle(du)
    dqst, dk, dv, dbias, dsink_rows = _attn_bwd(qst, kn, vb, dost, bias_st, sinks, pre_a)
    dz, dx, dg_attn, dgq, dgk = _attn_in_bwd(dqst, zqk, dk, dv, du, x2, dh1, w_in, g_attn_norm, gq, gk)

    partial_b = _dw(dz, hn1, "dw_in").reshape(N_CHIPS, -1, D_MODEL)
    small = _small_pack(dg_attn, dg_ffn, dg_ple, dscale, dgq, dgk, dbias, dsink_rows, bucket, loss_v, _dw_pool(pooled, dyp))
    return dx, partial_b, small


def kernel(x, p, w_in, w_out, g_attn_norm, g_q, g_k, attn_sinks, rel_bias, w_pool, pool_scale, g_ffn_norm, w_gate, w_up, w_down, g_ple_norm, w_ple_gate, w_ple_proj, loss_target, m_w_in, m_w_out, m_g_attn_norm, m_g_q, m_g_k, m_attn_sinks, m_rel_bias, m_w_pool, m_pool_scale, m_g_ffn_norm, m_w_gate, m_w_up, m_w_down, m_g_ple_norm, m_w_ple_gate, m_w_ple_proj, v_w_in, v_w_out, v_g_attn_norm, v_g_q, v_g_k, v_attn_sinks, v_rel_bias, v_w_pool, v_pool_scale, v_g_ffn_norm, v_w_gate, v_w_up, v_w_down, v_g_ple_norm, v_w_ple_gate, v_w_ple_proj):
    core = lax.axis_index("c").astype(jnp.int32).reshape(1)
    me = (2 * lax.axis_index("x") + lax.axis_index("y")).astype(jnp.int32).reshape(1)

    local_slab = jnp.concatenate(
        [w_in[0].T, w_out[0], w_gate[0].T, w_up[0].T, w_down[0], w_ple_gate[0], _pack_ple_proj(w_ple_proj[0])],
        axis=0).astype(BF16)
    gathered = [_ag_weights(local_slab, start, stop - start, name, collective_id)
                for (start, stop), name, collective_id in zip(GATHER_PARTS, ("ag_in", "ag_out", "ag_rest"), (1, 2, 5))]
    wts = (*gathered, local_slab, me)

    place = jnp.concatenate([me, core])
    reduce_a = _Reduction("a", place, ids=(3, 4))
    dx, partial_b, small = _local_grads(x[0], p[0, 0], loss_target[0], wts, g_attn_norm, g_q, g_k, attn_sinks, rel_bias,
                                        w_pool, pool_scale, g_ffn_norm, g_ple_norm, reduce_a)
    reduce_b = _Reduction("b", place, ids=(6, 7))
    reduce_b.start(partial_b)
    summed_a = reduce_a.finish(small)
    pre_b = reduce_b.middle(summed_a)
    grads_a, small_all = _rs_finish(summed_a, small, pre_b)

    def rows(name):
        return grads_a, SLAB[name][0] - SLAB["inT"][1]

    plp_rows = grads_a[SLAB["plp"][0] - SLAB["inT"][1]:]
    big = {
        "w_out": (w_out, m_w_out, v_w_out, rows("out"), False),
        "w_gate": (w_gate, m_w_gate, v_w_gate, rows("gateT"), True),
        "w_up": (w_up, m_w_up, v_w_up, rows("upT"), True),
        "w_down": (w_down, m_w_down, v_w_down, rows("down"), False),
        "w_ple_gate": (w_ple_gate, m_w_ple_gate, v_w_ple_gate, rows("plg"), False),
        "w_ple_proj": (w_ple_proj, m_w_ple_proj, v_w_ple_proj,
                       (plp_rows.reshape(64, 4, 256).transpose(1, 0, 2).reshape(PLE_DIM, PLE_DIM), 0), False),
        "w_in": (w_in, m_w_in, v_w_in, None, True),
    }
    small_params = {
        "g_attn_norm": (g_attn_norm, m_g_attn_norm, v_g_attn_norm), "g_q": (g_q, m_g_q, v_g_q), "g_k": (g_k, m_g_k, v_g_k),
        "attn_sinks": (attn_sinks, m_attn_sinks, v_attn_sinks), "rel_bias": (rel_bias, m_rel_bias, v_rel_bias),
        "w_pool": tuple(a.reshape(512, 128) for a in (w_pool, m_w_pool, v_w_pool)),
        "pool_scale": (pool_scale, m_pool_scale, v_pool_scale), "g_ffn_norm": (g_ffn_norm, m_g_ffn_norm, v_g_ffn_norm),
        "g_ple_norm": (g_ple_norm, m_g_ple_norm, v_g_ple_norm),
    }

    grads, deltas, new_ms, new_vs = {}, {}, {}, {}
    loss, small_out = _adamw_small(small_all, list(small_params.values()))
    for name, (g2, d, nm, nv) in zip(small_params, small_out):
        shape = w_pool.shape if name == "w_pool" else g2.shape
        grads[name], deltas[name], new_ms[name], new_vs[name] = (a.reshape(shape) for a in (g2, d, nm, nv))

    out = small_out[-1]
    for name, (w, m, v, g_src, transposed) in big.items():
        if g_src is None:
            g_src = (_rs_finish_rows(reduce_b.finish(out[-1]), "rs_finish_b"), 0)
        view = (lambda a: a.T) if transposed else (lambda a: a)
        out = _adamw(view(w[0]), *g_src, view(m[0]), view(v[0]), "adamw_" + name)
        grads[name], deltas[name], new_ms[name], new_vs[name] = (view(a)[None] for a in out)

    order = ["w_in", "w_out", "g_attn_norm", "g_q", "g_k", "attn_sinks", "rel_bias", "w_pool", "pool_scale", "g_ffn_norm",
             "w_gate", "w_up", "w_down", "g_ple_norm", "w_ple_gate", "w_ple_proj"]
    return (loss.reshape(()), dx[None], *[grads[n] for n in order], *[deltas[n] for n in order],
            *[new_ms[n] for n in order], *[new_vs[n] for n in order])
```

```python
import functools

import numpy as np
import jax
import jax.numpy as jnp
from jax import lax
from jax.experimental import pallas as pl
from jax.experimental.pallas import tpu as pltpu
from jax.experimental.pallas import tpu_sc as plsc

F32 = jnp.float32
BF16 = jnp.bfloat16
MESH = pl.DeviceIdType.MESH

D_MODEL = 1024
HEAD_DIM = 64
N_Q_HEADS = 8
ATTN_WIDTH = 512
KV_WIDTH = 128
POOL_WIDTH = 512
IN_WIDTH = 1280
D_FF = 2816
PLE_DIM = 256
FF_CHUNK = 704
BLOCK = 128
N_BUCKETS = 32
MAX_DISTANCE = 128
POOL_SIZES = (2, 4, 8, 16)
EPS = 1e-6
NEG = -1e30
N_CHIPS = 4
N_DEV = 8

ADAM_LR = 0.001
ADAM_B1 = 0.9
ADAM_B2 = 0.999
ADAM_EPS = 1e-08
ADAM_WD = 0.01
ADAM_STEP = 10

SLAB = {"inT": (0, 320), "out": (320, 256), "gateT": (576, 704), "upT": (1280, 704), "down": (1984, 704),
        "plg": (2688, 256), "plp": (2944, 64)}
SLAB_ROWS = 3008
HALF_ROWS = SLAB_ROWS // 2
GATHER_PARTS = ((0, 320), (320, 576), (576, SLAB_ROWS))
POOL_HALO = 24

SMALL = {"g_attn": 0, "g_ffn": 8, "g_ple": 16, "pool_scale": 24, "g_q": 28, "g_k": 29, "sinks": 30, "loss": 31,
         "rel_bias": 32, "w_pool": 64}
SMALL_ROWS = 576

VMEM_LIMIT_BIG = 60 * 1024 * 1024
VMEM_LIMIT = 48 * 1024 * 1024


def _params(vmem=VMEM_LIMIT, n_axes=1):
    return pltpu.CompilerParams(dimension_semantics=("arbitrary",) * n_axes, vmem_limit_bytes=vmem)


def _dot(a, b, ca, cb):
    return lax.dot_general(a, b, (((ca,), (cb,)), ((), ())), preferred_element_type=F32)


def _full(shape):
    return pl.BlockSpec(shape, lambda i: (0,) * len(shape))


ANY = pl.BlockSpec(memory_space=pl.ANY)
VMEM_WHOLE = pl.BlockSpec(memory_space=pltpu.VMEM)


W_SPECS = [ANY, ANY, pl.BlockSpec(memory_space=pltpu.SMEM)]


def _load_rows(w_refs, name, dst_ref, sems):
    slab_ref, local_ref, me_ref = w_refs
    off, rows = SLAB[name]
    slab_off = off - max(start for start, _ in GATHER_PARTS if start <= off)
    me = me_ref[0]
    for phase in ("start", "wait"):
        for j in range(N_CHIPS):
            dst = dst_ref.at[pl.ds(j * rows, rows), :]
            theirs = pltpu.make_async_copy(slab_ref.at[j, pl.ds(slab_off, rows), :], dst, sems.at[j])
            own = pltpu.make_async_copy(local_ref.at[pl.ds(off, rows), :], dst, sems.at[j])

            @pl.when(me == j)
            def _():
                getattr(own, phase)()

            @pl.when(me != j)
            def _():
                getattr(theirs, phase)()


def _rms_fwd(x, g):
    r = lax.rsqrt(jnp.mean(x * x, axis=-1, keepdims=True) + EPS)
    return x * r * g


def _rms_bwd(x, g, dy):
    r = lax.rsqrt(jnp.mean(x * x, axis=-1, keepdims=True) + EPS)
    xn = x * r
    dyg = dy * g
    dx = r * (dyg - xn * jnp.mean(dyg * xn, axis=-1, keepdims=True))
    return dx, jnp.sum(dy * xn, axis=0, keepdims=True)


def _half_sum(v, lo):
    s_lo = jnp.sum(jnp.where(lo, v, 0.0), axis=-1, keepdims=True)
    s_hi = jnp.sum(jnp.where(lo, 0.0, v), axis=-1, keepdims=True)
    return jnp.where(lo, s_lo, s_hi)


def _half_sum_mxu(v):
    upper = lax.broadcasted_iota(jnp.int32, (128, 128), 0) < 64
    left = lax.broadcasted_iota(jnp.int32, (128, 128), 1) < 64
    ones = jnp.where(upper == left, 1.0, 0.0).astype(BF16)
    high = v.astype(BF16)
    low = (v - high.astype(F32)).astype(BF16)
    return _dot(high, ones, 1, 0) + _dot(low, ones, 1, 0)


def _pair_norm(zp, g, lo):
    r = lax.rsqrt(_half_sum(zp * zp, lo) * (1.0 / HEAD_DIM) + EPS)
    return zp * r * g


def _pair_norm_bwd(zp, g, dy):
    r = lax.rsqrt(_half_sum_mxu(zp * zp) * (1.0 / HEAD_DIM) + EPS)
    xn = zp * r
    dyg = dy * g
    dx = r * (dyg - xn * (_half_sum_mxu(dyg * xn) * (1.0 / HEAD_DIM)))
    return dx, jnp.sum(dy * xn, axis=0, keepdims=True)


def _to_stacked(pair, group, lo):
    rolled = pltpu.roll(pair, 64, axis=1)
    if group == 0:
        return jnp.where(lo, pair, 0.0), jnp.where(lo, rolled, 0.0)
    return jnp.where(lo, 0.0, rolled), jnp.where(lo, 0.0, pair)


def _from_stacked(even, odd, group, lo):
    if group == 0:
        return jnp.where(lo, even, pltpu.roll(odd, 64, axis=1))
    return jnp.where(lo, pltpu.roll(even, 64, axis=1), odd)


def _sigmoid(v):
    return 1.0 / (1.0 + jnp.exp(-v))


def _pool_counts(tile, n_rows):
    t1 = tile * n_rows + lax.broadcasted_iota(jnp.int32, (n_rows, POOL_WIDTH), 0) + 1
    lane = lax.broadcasted_iota(jnp.int32, (n_rows, POOL_WIDTH), 1)
    win = jnp.where(lane < 128, 2, jnp.where(lane < 256, 4, jnp.where(lane < 384, 8, 16)))
    return jnp.minimum(t1, win).astype(F32)


def _attn_in(x2, g_attn, gq, gk, wts):
    s_len = x2.shape[0]
    t = 512

    def body(x_ref, g_ref, gq_ref, gk_ref, sl_ref, lo_ref, me_ref, hn_ref, zqk_ref, u_ref, kn_ref, v_ref, qst_ref, w_ref, sems):
        @pl.when(pl.program_id(0) == 0)
        def _():
            _load_rows((sl_ref, lo_ref, me_ref), "inT", w_ref, sems)

        hn = _rms_fwd(x_ref[...], g_ref[...]).astype(BF16)
        hn_ref[...] = hn
        z = _dot(hn, w_ref[...], 1, 1)
        zqk_ref[...] = z[:, :640]
        u_ref[...] = z[:, 768:]
        v_ref[...] = z[:, 640:768].astype(BF16)
        lo = lax.broadcasted_iota(jnp.int32, (t, 128), 1) < 64
        kn_ref[...] = _pair_norm(z[:, 512:640], gk_ref[...], lo).astype(BF16)
        for p in range(4):
            qn = _pair_norm(z[:, 128 * p:128 * p + 128], gq_ref[...], lo)
            even, odd = _to_stacked(qn, p // 2, lo)
            qst_ref[2 * p] = even.astype(BF16)
            qst_ref[2 * p + 1] = odd.astype(BF16)

    row = lambda w: pl.BlockSpec((t, w), lambda i: (i, 0))
    return pl.pallas_call(
        body, name="attn_in", grid=(s_len // t,),
        in_specs=[row(D_MODEL), _full((1, D_MODEL)), _full((1, 128)), _full((1, 128))] + W_SPECS,
        out_specs=[row(D_MODEL), row(640), row(POOL_WIDTH), row(128), row(128),
                   pl.BlockSpec((N_Q_HEADS, t, 128), lambda i: (0, i, 0))],
        out_shape=[jax.ShapeDtypeStruct((s_len, D_MODEL), BF16), jax.ShapeDtypeStruct((s_len, 640), F32),
                   jax.ShapeDtypeStruct((s_len, POOL_WIDTH), F32), jax.ShapeDtypeStruct((s_len, 128), BF16),
                   jax.ShapeDtypeStruct((s_len, 128), BF16), jax.ShapeDtypeStruct((N_Q_HEADS, s_len, 128), BF16)],
        scratch_shapes=[pltpu.VMEM((IN_WIDTH, D_MODEL), BF16), pltpu.SemaphoreType.DMA((N_CHIPS,))],
        compiler_params=_params(),
    )(x2, g_attn, gq, gk, *wts)


def _bucket_table():
    i_idx = np.arange(BLOCK)[:, None]
    j_idx = np.arange(2 * BLOCK)[None, :]
    d = BLOCK + i_idx - j_idx
    n = np.maximum(d, 0)
    max_exact = N_BUCKETS // 2
    nf = np.maximum(n, 1).astype(np.float64)
    large = max_exact + (np.log(nf / max_exact) / np.log(MAX_DISTANCE / max_exact) * (N_BUCKETS - max_exact)).astype(np.int64)
    large = np.minimum(large, N_BUCKETS - 1)
    bucket = np.where(n < max_exact, n, large)
    return np.where((d >= 0) & (d < BLOCK), bucket, -1).astype(np.int32)


def _bias_build(rel_bias_t, bucket):
    def body(rb_ref, bucket_ref, out_ref):
        bk = bucket_ref[...]
        for h in range(N_Q_HEADS):
            acc = jnp.full((BLOCK, 2 * BLOCK), NEG, F32)
            for b in range(N_BUCKETS):
                acc = jnp.where(bk == b, rb_ref[h, b], acc)
            out_ref[0, pl.ds(h * BLOCK, BLOCK), :] = acc
            out_ref[1, pl.ds(h * BLOCK, BLOCK), :] = acc
            out_ref[1, pl.ds(h * BLOCK, BLOCK), 0:BLOCK] = jnp.full((BLOCK, BLOCK), NEG, F32)

    return pl.pallas_call(
        body, name="bias_build",
        in_specs=[pl.BlockSpec(memory_space=pltpu.SMEM), VMEM_WHOLE], out_specs=VMEM_WHOLE,
        out_shape=jax.ShapeDtypeStruct((2, N_Q_HEADS * BLOCK, 2 * BLOCK), F32),
    )(rel_bias_t, bucket)


def _head_softmax(s_ref, bias_ref, sink_ref, h):
    rows = pl.ds(pl.multiple_of(h * BLOCK, BLOCK), BLOCK)
    s = s_ref[rows, :] * (HEAD_DIM ** -0.5) + bias_ref[rows, :]
    sink = sink_ref[h]
    m = jnp.maximum(jnp.max(s, axis=-1, keepdims=True), sink)
    p = jnp.exp(s - m)
    e_sink = jnp.exp(sink - m)
    inv = 1.0 / (jnp.sum(p, axis=-1, keepdims=True) + e_sink)
    return rows, p * inv, e_sink * inv


def _attn_specs():
    prev = lambda i: (jnp.maximum(i - 1, 0), 0)
    cur = lambda i: (i, 0)
    stacked = pl.BlockSpec((N_Q_HEADS, BLOCK, 128), lambda i: (0, i, 0))
    kv = [pl.BlockSpec((BLOCK, 128), prev), pl.BlockSpec((BLOCK, 128), cur)]
    consts = [pl.BlockSpec((None, N_Q_HEADS * BLOCK, 2 * BLOCK), lambda i: (jnp.where(i == 0, 1, 0), 0, 0)),
              pl.BlockSpec(memory_space=pltpu.SMEM)]
    return stacked, kv, consts


def _head_lane_mask():
    rows = lax.broadcasted_iota(jnp.int32, (N_Q_HEADS * BLOCK, 128), 0)
    lanes = lax.broadcasted_iota(jnp.int32, (N_Q_HEADS * BLOCK, 128), 1)
    return (rows < 4 * BLOCK) == (lanes < 64)


def _attn_fwd(qst, kn, vb, bias_st, sinks):
    s_len = kn.shape[0]

    def body(q_ref, kp_ref, kc_ref, vp_ref, vc_ref, bias_ref, sink_ref, o_ref, s_ref, p_ref):
        q = q_ref[...].reshape(N_Q_HEADS * BLOCK, 128)
        s_ref[...] = _dot(q, jnp.concatenate([kp_ref[...], kc_ref[...]], axis=0), 1, 1)

        def head(h, carry):
            rows, probs, _ = _head_softmax(s_ref, bias_ref, sink_ref, h)
            p_ref[rows, :] = probs.astype(BF16)
            return carry

        lax.fori_loop(0, N_Q_HEADS, head, 0, unroll=True)
        o = _dot(p_ref[...], jnp.concatenate([vp_ref[...], vc_ref[...]], axis=0), 1, 0)
        o_ref[...] = jnp.where(_head_lane_mask(), o, 0.0).astype(BF16).reshape(N_Q_HEADS, BLOCK, 128)

    stacked, kv, consts = _attn_specs()
    return pl.pallas_call(
        body, name="attn_fwd", grid=(s_len // BLOCK,),
        in_specs=[stacked] + kv + kv + consts, out_specs=stacked,
        out_shape=jax.ShapeDtypeStruct((N_Q_HEADS, s_len, 128), BF16),
        scratch_shapes=[pltpu.VMEM((N_Q_HEADS * BLOCK, 2 * BLOCK), F32), pltpu.VMEM((N_Q_HEADS * BLOCK, 2 * BLOCK), BF16)],
        compiler_params=_params(),
    )(qst, kn, kn, vb, vb, bias_st, sinks)


def _mix_out(u, ost, x2, wts, wpool, pool_scale, g_ffn):
    s_len = x2.shape[0]
    t = 512
    n = t + 16

    def body(u_ref, o_ref, x_ref, sl_ref, lo_ref, me_ref, wp_ref, sc_ref, g_ref, pooled_ref, mix_ref, h1_ref, hn_ref,
             w_ref, ext_ref, st_ref, sems):
        i = pl.program_id(0)

        @pl.when(i == 0)
        def _():
            _load_rows((sl_ref, lo_ref, me_ref), "out", w_ref, sems)
            ext_ref[...] = jnp.zeros_like(ext_ref)
            st_ref[...] = jnp.zeros_like(st_ref)

        u_tile = u_ref[...]
        ext_ref[pl.ds(POOL_HALO, t), :] = u_tile
        st_ref[pl.ds(8, n), :] = ext_ref[pl.ds(8, n), :] + ext_ref[pl.ds(7, n), :]
        st_ref[pl.ds(8, n), 128:] = st_ref[pl.ds(8, n), 128:] + st_ref[pl.ds(6, n), 128:]
        st_ref[pl.ds(8, n), 256:] = st_ref[pl.ds(8, n), 256:] + st_ref[pl.ds(4, n), 256:]
        st_ref[pl.ds(8, n), 384:] = st_ref[pl.ds(8, n), 384:] + st_ref[pl.ds(0, n), 384:]
        ext_ref[pl.ds(0, POOL_HALO), :] = ext_ref[pl.ds(t, POOL_HALO), :]
        pooled = (st_ref[pl.ds(POOL_HALO, t), :] / _pool_counts(i, t) - u_tile).astype(BF16)
        pooled_ref[...] = pooled
        for g in range(4):
            cols = slice(128 * g, 128 * g + 128)
            y = _dot(pooled[:, cols], wp_ref[g], 1, 0) * sc_ref[:, cols]
            mix_ref[:, ATTN_WIDTH + 128 * g:ATTN_WIDTH + 128 * g + 128] = y.astype(BF16)
        lo = lax.broadcasted_iota(jnp.int32, (t, 128), 1) < 64
        for p in range(4):
            a = _from_stacked(o_ref[2 * p].astype(F32), o_ref[2 * p + 1].astype(F32), p // 2, lo)
            mix_ref[:, 128 * p:128 * p + 128] = a.astype(BF16)
        h1 = x_ref[...] + _dot(mix_ref[...], w_ref[...], 1, 0)
        h1_ref[...] = h1
        hn_ref[...] = _rms_fwd(h1, g_ref[...]).astype(BF16)

    row = lambda w: pl.BlockSpec((t, w), lambda i: (i, 0))
    return pl.pallas_call(
        body, name="mix_out", grid=(s_len // t,),
        in_specs=[row(POOL_WIDTH), pl.BlockSpec((N_Q_HEADS, t, 128), lambda i: (0, i, 0)), row(D_MODEL)] + W_SPECS
        + [_full((4, 128, 128)), _full((1, POOL_WIDTH)), _full((1, D_MODEL))],
        out_specs=[row(POOL_WIDTH), row(D_MODEL), row(D_MODEL), row(D_MODEL)],
        out_shape=[jax.ShapeDtypeStruct((s_len, POOL_WIDTH), BF16), jax.ShapeDtypeStruct((s_len, D_MODEL), BF16),
                   jax.ShapeDtypeStruct((s_len, D_MODEL), F32), jax.ShapeDtypeStruct((s_len, D_MODEL), BF16)],
        scratch_shapes=[pltpu.VMEM((D_MODEL, D_MODEL), BF16), pltpu.VMEM((t + POOL_HALO, POOL_WIDTH), F32),
                        pltpu.VMEM((t + POOL_HALO, POOL_WIDTH), F32), pltpu.SemaphoreType.DMA((N_CHIPS,))],
        compiler_params=_params(),
    )(u, ost, x2, *wts, wpool, pool_scale, g_ffn)


def _ffn_ple(hn2, h1, p2, tgt, wts, g_ffn, g_ple):
    s_len = h1.shape[0]
    t = 256
    n_tiles = s_len // t

    def body(hn_ref, h1_ref, p_ref, tgt_ref, sl_ref, lo_ref, me_ref, gf_ref, gp_ref,
             loss_ref, dgate_ref, dup_ref, act_ref, dh2b_ref, hn3_ref, dgl_ref, dpp_ref, dh1_ref, dgf_ref, dgp_ref,
             wg_ref, wu_ref, wd_ref, wl_ref, wp_ref, packed_ref, gate_s, up_s, loss_acc, sems):
        i = pl.program_id(0)

        @pl.when(i == 0)
        def _():
            w_refs = (sl_ref, lo_ref, me_ref)
            _load_rows(w_refs, "gateT", wg_ref, sems)
            _load_rows(w_refs, "upT", wu_ref, sems)
            _load_rows(w_refs, "down", wd_ref, sems)
            _load_rows(w_refs, "plg", wl_ref, sems)
            _load_rows(w_refs, "plp", packed_ref, sems)
            for j in range(N_CHIPS):
                for q in range(4):
                    wp_ref[pl.ds(64 * q, 64), 256 * j:256 * j + 256] = packed_ref[pl.ds(64 * j, 64), 256 * q:256 * q + 256]
            loss_acc[...] = jnp.zeros_like(loss_acc)
            dgf_ref[...] = jnp.zeros_like(dgf_ref)
            dgp_ref[...] = jnp.zeros_like(dgp_ref)

        hn = hn_ref[...]
        h1v = h1_ref[...]
        h2 = h1v
        for ch in range(N_CHIPS):
            rows = pl.ds(ch * FF_CHUNK, FF_CHUNK)
            gate = _dot(hn, wg_ref[rows, :], 1, 1)
            up = _dot(hn, wu_ref[rows, :], 1, 1)
            gate_s[ch] = gate
            up_s[ch] = up
            act = (gate * _sigmoid(gate) * up).astype(BF16)
            act_ref[ch] = act
            h2 = h2 + _dot(act, wd_ref[rows, :], 1, 0)
        gp = gp_ref[...]
        hn3 = _rms_fwd(h2, gp).astype(BF16)
        hn3_ref[...] = hn3
        gate2 = _sigmoid(_dot(hn3, wl_ref[...], 1, 0))
        pp = _dot(p_ref[...].astype(BF16), wp_ref[...], 1, 0)
        err = h2 + gate2 * pp - tgt_ref[...]
        loss_acc[...] += jnp.sum(err * err, axis=0, keepdims=True)
        dy = err * (1.0 / D_MODEL)
        dpp_ref[...] = (dy * gate2).astype(BF16)
        dgl = (dy * pp * gate2 * (1.0 - gate2)).astype(BF16)
        dgl_ref[...] = dgl
        dx3, dg3 = _rms_bwd(h2, gp, _dot(dgl, wl_ref[...], 1, 1))
        dh2 = dy + dx3
        dgp_ref[...] += dg3
        dh2b = dh2.astype(BF16)
        dh2b_ref[...] = dh2b
        dhn = jnp.zeros((t, D_MODEL), F32)
        for ch in range(N_CHIPS):
            rows = pl.ds(ch * FF_CHUNK, FF_CHUNK)
            dact = _dot(dh2b, wd_ref[rows, :], 1, 1)
            gate_v = gate_s[ch]
            up_v = up_s[ch]
            sg = _sigmoid(gate_v)
            dup = (dact * (gate_v * sg)).astype(BF16)
            dgate = (dact * up_v * (sg * (1.0 + gate_v * (1.0 - sg)))).astype(BF16)
            dup_ref[ch] = dup
            dgate_ref[ch] = dgate
            dhn = dhn + _dot(dgate, wg_ref[rows, :], 1, 0) + _dot(dup, wu_ref[rows, :], 1, 0)
        dx, dg = _rms_bwd(h1v, gf_ref[...], dhn)
        dh1_ref[...] = dh2 + dx
        dgf_ref[...] += dg

        @pl.when(i == n_tiles - 1)
        def _():
            total = jnp.sum(loss_acc[...], axis=-1, keepdims=True) * (0.5 / D_MODEL)
            loss_ref[...] = jnp.broadcast_to(total, loss_ref.shape)

    row = lambda w: pl.BlockSpec((t, w), lambda i: (i, 0))
    chunked = pl.BlockSpec((N_CHIPS, t, FF_CHUNK), lambda i: (0, i, 0))
    vec = _full((1, D_MODEL))
    act_shape = jax.ShapeDtypeStruct((N_CHIPS, s_len, FF_CHUNK), BF16)
    tok = lambda dtype: jax.ShapeDtypeStruct((s_len, D_MODEL), dtype)
    return pl.pallas_call(
        body, name="ffn_ple", grid=(n_tiles,),
        in_specs=[row(D_MODEL), row(D_MODEL), row(PLE_DIM), row(D_MODEL)] + W_SPECS + [vec, vec],
        out_specs=[_full((1, 128)), chunked, chunked, chunked] + [row(D_MODEL)] * 5 + [vec, vec],
        out_shape=[jax.ShapeDtypeStruct((1, 128), F32), act_shape, act_shape, act_shape, tok(BF16), tok(BF16), tok(BF16),
                   tok(BF16), tok(F32), jax.ShapeDtypeStruct((1, D_MODEL), F32), jax.ShapeDtypeStruct((1, D_MODEL), F32)],
        scratch_shapes=[pltpu.VMEM((D_FF, D_MODEL), BF16)] * 3
        + [pltpu.VMEM((D_MODEL, D_MODEL), BF16), pltpu.VMEM((PLE_DIM, D_MODEL), BF16), pltpu.VMEM((PLE_DIM, D_MODEL), BF16),
           pltpu.VMEM((N_CHIPS, t, FF_CHUNK), F32), pltpu.VMEM((N_CHIPS, t, FF_CHUNK), F32), pltpu.VMEM((1, D_MODEL), F32),
           pltpu.SemaphoreType.DMA((N_CHIPS,))],
        compiler_params=_params(VMEM_LIMIT_BIG),
    )(hn2, h1, p2, tgt, *wts, g_ffn, g_ple)


def _mix_out_bwd(dh1, wts, pooled, wpool, pool_scale, after):
    s_len = dh1.shape[0]
    t = 512
    n = t + 16
    n_tiles = s_len // t

    def body(dh1_ref, sl_ref, lo_ref, me_ref, pooled_ref, wp_ref, sc_ref, after_ref, dost_ref, du_ref, dyp_ref, dsc_ref,
             w_ref, ext_ref, st_ref, sems):
        del after_ref
        i = pl.program_id(0)

        @pl.when(i == 0)
        def _():
            _load_rows((sl_ref, lo_ref, me_ref), "out", w_ref, sems)
            ext_ref[...] = jnp.zeros_like(ext_ref)
            st_ref[...] = jnp.zeros_like(st_ref)
            dsc_ref[...] = jnp.zeros_like(dsc_ref)

        dmix = _dot(dh1_ref[...].astype(BF16), w_ref[...], 1, 1)
        lo = lax.broadcasted_iota(jnp.int32, (t, 128), 1) < 64
        for p in range(4):
            even, odd = _to_stacked(dmix[:, 128 * p:128 * p + 128], p // 2, lo)
            dost_ref[2 * p] = even.astype(BF16)
            dost_ref[2 * p + 1] = odd.astype(BF16)
        pooled_v = pooled_ref[...]
        counts = _pool_counts(n_tiles - 1 - i, t)
        for g in range(4):
            cols = slice(128 * g, 128 * g + 128)
            dm = dmix[:, ATTN_WIDTH + 128 * g:ATTN_WIDTH + 128 * g + 128]
            ypre = _dot(pooled_v[:, cols], wp_ref[g], 1, 0)
            dsc_ref[:, cols] += jnp.sum(ypre * dm, axis=0, keepdims=True)
            dyp = (dm * sc_ref[:, cols]).astype(BF16)
            dyp_ref[:, cols] = dyp
            dpooled = _dot(dyp, wp_ref[g], 1, 1)
            du_ref[:, cols] = -dpooled
            ext_ref[pl.ds(0, t), cols] = dpooled / counts[:, cols]
        st_ref[pl.ds(0, n), :] = ext_ref[pl.ds(0, n), :] + ext_ref[pl.ds(1, n), :]
        st_ref[pl.ds(0, n), 128:] = st_ref[pl.ds(0, n), 128:] + st_ref[pl.ds(2, n), 128:]
        st_ref[pl.ds(0, n), 256:] = st_ref[pl.ds(0, n), 256:] + st_ref[pl.ds(4, n), 256:]
        st_ref[pl.ds(0, n), 384:] = st_ref[pl.ds(0, n), 384:] + st_ref[pl.ds(8, n), 384:]
        ext_ref[pl.ds(t, POOL_HALO), :] = ext_ref[pl.ds(0, POOL_HALO), :]
        du_ref[...] += st_ref[pl.ds(0, t), :]

    rev = lambda w: pl.BlockSpec((t, w), lambda i: (n_tiles - 1 - i, 0))
    return pl.pallas_call(
        body, name="mix_out_bwd", grid=(n_tiles,),
        in_specs=[rev(D_MODEL)] + W_SPECS + [rev(POOL_WIDTH), _full((4, 128, 128)), _full((1, POOL_WIDTH)), ANY],
        out_specs=[pl.BlockSpec((N_Q_HEADS, t, 128), lambda i: (0, n_tiles - 1 - i, 0)), rev(POOL_WIDTH), rev(POOL_WIDTH),
                   _full((1, POOL_WIDTH))],
        out_shape=[jax.ShapeDtypeStruct((N_Q_HEADS, s_len, 128), BF16), jax.ShapeDtypeStruct((s_len, POOL_WIDTH), F32),
                   jax.ShapeDtypeStruct((s_len, POOL_WIDTH), BF16), jax.ShapeDtypeStruct((1, POOL_WIDTH), F32)],
        scratch_shapes=[pltpu.VMEM((D_MODEL, D_MODEL), BF16), pltpu.VMEM((t + POOL_HALO, POOL_WIDTH), F32),
                        pltpu.VMEM((t + POOL_HALO, POOL_WIDTH), F32), pltpu.SemaphoreType.DMA((N_CHIPS,))],
        compiler_params=_params(),
    )(dh1, *wts, pooled, wpool, pool_scale, after)


def _attn_bwd(qst, kn, vb, dost, bias_st, sinks, after):
    s_len = kn.shape[0]

    def body(q_ref, kp_ref, kc_ref, vp_ref, vc_ref, do_ref, bias_ref, sink_ref, after_ref, dq_ref, dk_ref, dv_ref, dbias_ref,
             dsink_ref, s_ref, dp_ref, p_ref, dl_ref):
        del after_ref
        i = pl.program_id(0)

        @pl.when(i == 0)
        def _():
            dk_ref[...] = jnp.zeros_like(dk_ref)
            dv_ref[...] = jnp.zeros_like(dv_ref)
            dbias_ref[...] = jnp.zeros_like(dbias_ref)
            dsink_ref[...] = jnp.zeros_like(dsink_ref)

        q = q_ref[...].reshape(N_Q_HEADS * BLOCK, 128)
        do = do_ref[...].reshape(N_Q_HEADS * BLOCK, 128)
        k2 = jnp.concatenate([kp_ref[...], kc_ref[...]], axis=0)
        s_ref[...] = _dot(q, k2, 1, 1)
        dp_ref[...] = _dot(do, jnp.concatenate([vp_ref[...], vc_ref[...]], axis=0), 1, 1)

        def head(h, carry):
            rows, probs, p_sink = _head_softmax(s_ref, bias_ref, sink_ref, h)
            dp = dp_ref[rows, :]
            dsum = jnp.sum(probs * dp, axis=-1, keepdims=True)
            dlog = probs * (dp - dsum)
            dsink_ref[rows, :] -= p_sink * dsum
            dbias_ref[rows, :] += dlog
            p_ref[rows, :] = probs.astype(BF16)
            dl_ref[rows, :] = (dlog * (HEAD_DIM ** -0.5)).astype(BF16)
            return carry

        lax.fori_loop(0, N_Q_HEADS, head, 0, unroll=True)
        dlog_s = dl_ref[...]
        dq_ref[...] = jnp.where(_head_lane_mask(), _dot(dlog_s, k2, 1, 0), 0.0).reshape(N_Q_HEADS, BLOCK, 128)
        dk2 = _dot(dlog_s, q, 0, 0)
        dv2 = _dot(p_ref[...], do, 0, 0)
        prev_rows = pl.ds(pl.multiple_of(jnp.maximum(i - 1, 0) * BLOCK, BLOCK), BLOCK)
        cur_rows = pl.ds(pl.multiple_of(i * BLOCK, BLOCK), BLOCK)
        dk_ref[prev_rows, :] += dk2[:BLOCK]
        dk_ref[cur_rows, :] += dk2[BLOCK:]
        dv_ref[prev_rows, :] += dv2[:BLOCK]
        dv_ref[cur_rows, :] += dv2[BLOCK:]

    stacked, kv, consts = _attn_specs()
    band = (N_Q_HEADS * BLOCK, 2 * BLOCK)
    return pl.pallas_call(
        body, name="attn_bwd", grid=(s_len // BLOCK,),
        in_specs=[stacked] + kv + kv + [stacked] + consts + [ANY],
        out_specs=[stacked, _full((s_len, 128)), _full((s_len, 128)), _full(band), _full((N_Q_HEADS * BLOCK, 1))],
        out_shape=[jax.ShapeDtypeStruct((N_Q_HEADS, s_len, 128), F32), jax.ShapeDtypeStruct((s_len, 128), F32),
                   jax.ShapeDtypeStruct((s_len, 128), F32), jax.ShapeDtypeStruct(band, F32),
                   jax.ShapeDtypeStruct((N_Q_HEADS * BLOCK, 1), F32)],
        scratch_shapes=[pltpu.VMEM(band, F32), pltpu.VMEM(band, F32), pltpu.VMEM(band, BF16), pltpu.VMEM(band, BF16)],
        compiler_params=_params(),
    )(qst, kn, kn, vb, vb, dost, bias_st, sinks, after)


def _small_pack(dg_attn, dg_ffn, dg_ple, dscale, dgq, dgk, dbias, dsink_rows, bucket, loss_v, dwpool):
    def body(ga_ref, gf_ref, gp_ref, sc_ref, gq_ref, gk_ref, db_ref, ds_ref, bucket_ref, loss_ref, wp_ref, out_ref):
        out_ref[pl.ds(0, SMALL["w_pool"]), :] = jnp.zeros((SMALL["w_pool"], 128), F32)
        for name, ref, n in (("g_attn", ga_ref, 8), ("g_ffn", gf_ref, 8), ("g_ple", gp_ref, 8), ("pool_scale", sc_ref, 4)):
            for k in range(n):
                out_ref[pl.ds(SMALL[name] + k, 1), :] = ref[:, 128 * k:128 * k + 128]
        for name, ref in (("g_q", gq_ref), ("g_k", gk_ref)):
            both = ref[...]
            out_ref[pl.ds(SMALL[name], 1), :] = both + pltpu.roll(both, 64, axis=1)
        out_ref[pl.ds(SMALL["loss"], 1), :] = loss_ref[...]
        bk = bucket_ref[...]
        rows = lax.broadcasted_iota(jnp.int32, (N_BUCKETS, 128), 0)
        lanes = lax.broadcasted_iota(jnp.int32, (N_BUCKETS, 128), 1)
        lane1 = lax.broadcasted_iota(jnp.int32, (1, 128), 1)
        rb = jnp.zeros((N_BUCKETS, 128), F32)
        sk = jnp.zeros((1, 128), F32)
        for h in range(N_Q_HEADS):
            band = db_ref[pl.ds(h * BLOCK, BLOCK), :]
            for b in range(N_BUCKETS):
                rb = jnp.where((rows == b) & (lanes == h), jnp.sum(jnp.where(bk == b, band, 0.0)), rb)
            sk = jnp.where(lane1 == h, jnp.sum(ds_ref[pl.ds(h * BLOCK, BLOCK), :]), sk)
        out_ref[pl.ds(SMALL["rel_bias"], N_BUCKETS), :] = rb
        out_ref[pl.ds(SMALL["sinks"], 1), :] = sk
        out_ref[pl.ds(SMALL["w_pool"], 512), :] = wp_ref[...].reshape(512, 128)

    return pl.pallas_call(
        body, name="small_pack", in_specs=[VMEM_WHOLE] * 11, out_specs=VMEM_WHOLE,
        out_shape=jax.ShapeDtypeStruct((SMALL_ROWS, 128), F32),
    )(dg_attn, dg_ffn, dg_ple, dscale, dgq, dgk, dbias, dsink_rows, bucket, loss_v, dwpool)


def _attn_in_bwd(dqst, zqk, dk, dv, du, x2, dh1, wts, g_attn, gq, gk):
    s_len = x2.shape[0]
    t = 512

    def body(dq_ref, zqk_ref, dk_ref, dv_ref, du_ref, x_ref, dh1_ref, sl_ref, lo_ref, me_ref, g_ref, gq_ref, gk_ref,
             dz_ref, dx_ref, dg_ref, dgq_ref, dgk_ref, w_ref, sems):
        @pl.when(pl.program_id(0) == 0)
        def _():
            _load_rows((sl_ref, lo_ref, me_ref), "inT", w_ref, sems)
            dg_ref[...] = jnp.zeros_like(dg_ref)
            dgq_ref[...] = jnp.zeros_like(dgq_ref)
            dgk_ref[...] = jnp.zeros_like(dgk_ref)

        lo = lax.broadcasted_iota(jnp.int32, (t, 128), 1) < 64
        for p in range(4):
            dqn = _from_stacked(dq_ref[2 * p], dq_ref[2 * p + 1], p // 2, lo)
            dq_raw, dgq = _pair_norm_bwd(zqk_ref[:, 128 * p:128 * p + 128], gq_ref[...], dqn)
            dz_ref[:, 128 * p:128 * p + 128] = dq_raw.astype(BF16)
            dgq_ref[...] += dgq
        dk_raw, dgk = _pair_norm_bwd(zqk_ref[:, 512:640], gk_ref[...], dk_ref[...])
        dgk_ref[...] += dgk
        dz_ref[:, 512:640] = dk_raw.astype(BF16)
        dz_ref[:, 640:768] = dv_ref[...].astype(BF16)
        dz_ref[:, 768:] = du_ref[...].astype(BF16)
        dx, dg = _rms_bwd(x_ref[...], g_ref[...], _dot(dz_ref[...], w_ref[...], 1, 0))
        dx_ref[...] = dh1_ref[...] + dx
        dg_ref[...] += dg

    row = lambda w: pl.BlockSpec((t, w), lambda i: (i, 0))
    return pl.pallas_call(
        body, name="attn_in_bwd", grid=(s_len // t,),
        in_specs=[pl.BlockSpec((N_Q_HEADS, t, 128), lambda i: (0, i, 0)), row(640), row(128), row(128), row(POOL_WIDTH),
                  row(D_MODEL), row(D_MODEL)] + W_SPECS + [_full((1, D_MODEL)), _full((1, 128)), _full((1, 128))],
        out_specs=[row(IN_WIDTH), row(D_MODEL), _full((1, D_MODEL)), _full((1, 128)), _full((1, 128))],
        out_shape=[jax.ShapeDtypeStruct((s_len, IN_WIDTH), BF16), jax.ShapeDtypeStruct((s_len, D_MODEL), F32),
                   jax.ShapeDtypeStruct((1, D_MODEL), F32), jax.ShapeDtypeStruct((1, 128), F32),
                   jax.ShapeDtypeStruct((1, 128), F32)],
        scratch_shapes=[pltpu.VMEM((IN_WIDTH, D_MODEL), BF16), pltpu.SemaphoreType.DMA((N_CHIPS,))],
        compiler_params=_params(),
    )(dqst, zqk, dk, dv, du, x2, dh1, *wts, g_attn, gq, gk)


def _dw(a, b, name, into=None):
    tk = 1024
    n_out = b.shape[1]
    if a.ndim == 3:
        s_len, tm = a.shape[1:]
        m = N_CHIPS * tm
        a_spec = pl.BlockSpec((None, tk, tm), lambda i, k: (i, k, 0))
    else:
        s_len, m = a.shape
        tm = m // 2 if m > 1408 else m
        a_spec = pl.BlockSpec((tk, tm), lambda i, k: (k, i))
    n_steps = s_len // tk
    chunk = m // N_CHIPS
    per_tile = tm // chunk

    def accumulate(a_ref, b_ref, acc_ref, k):
        @pl.when(k == 0)
        def _():
            acc_ref[...] = _dot(a_ref[...].astype(BF16), b_ref[...].astype(BF16), 0, 0)

        @pl.when(k > 0)
        def _():
            acc_ref[...] += _dot(a_ref[...].astype(BF16), b_ref[...].astype(BF16), 0, 0)

    in_specs = [a_spec, pl.BlockSpec((tk, n_out), lambda i, k: (k, 0))]
    if into is None:
        def body(a_ref, b_ref, o_ref, acc_ref):
            k = pl.program_id(1)
            accumulate(a_ref, b_ref, acc_ref, k)

            @pl.when(k == n_steps - 1)
            def _():
                o_ref[...] = acc_ref[...].astype(BF16)

        return pl.pallas_call(
            body, name=name, grid=(m // tm, n_steps), in_specs=in_specs,
            out_specs=pl.BlockSpec((tm, n_out), lambda i, k: (i, 0)), out_shape=jax.ShapeDtypeStruct((m, n_out), BF16),
            scratch_shapes=[pltpu.VMEM((tm, n_out), F32)], compiler_params=_params(n_axes=2),
        )(a, b)

    slab, slab_rows, row_off = into
    assert n_out == D_MODEL

    n_tiles = m // tm

    def body_into(a_ref, b_ref, *rest):
        o_ref, acc_ref, stage_ref, sems = rest[-4:]
        i, k = pl.program_id(0), pl.program_id(1)
        accumulate(a_ref, b_ref, acc_ref, k)

        def out_copies(tile, slot):
            return [pltpu.make_async_copy(stage_ref.at[slot, pl.ds(jj * chunk, chunk), :],
                                          o_ref.at[tile * per_tile + jj, pl.ds(row_off, chunk), :], sems.at[slot, jj])
                    for jj in range(per_tile)]

        @pl.when(k == n_steps - 1)
        def _():
            slot = i % 2

            @pl.when(i >= 2)
            def _():
                for cp in out_copies(i - 2, slot):
                    cp.wait()

            stage_ref[slot] = acc_ref[...].astype(BF16)
            for cp in out_copies(i, slot):
                cp.start()

            @pl.when(i == n_tiles - 1)
            def _():
                for cp in out_copies(i, slot):
                    cp.wait()
                if n_tiles > 1:
                    for cp in out_copies(i - 1, 1 - slot):
                        cp.wait()

    operands, aliases = [a, b], {}
    if slab is not None:
        in_specs = in_specs + [ANY]
        operands.append(slab)
        aliases = {2: 0}
    return pl.pallas_call(
        body_into, name=name, grid=(n_tiles, n_steps), in_specs=in_specs, out_specs=ANY,
        out_shape=jax.ShapeDtypeStruct((N_CHIPS, slab_rows, D_MODEL), BF16), input_output_aliases=aliases,
        scratch_shapes=[pltpu.VMEM((tm, n_out), F32), pltpu.VMEM((2, tm, n_out), BF16),
                        pltpu.SemaphoreType.DMA((2, per_tile))],
        compiler_params=_params(n_axes=2),
    )(*operands)


def _dw_pool(pooled, dyp):
    s_len = pooled.shape[0]
    tk = 512

    def body(a_ref, b_ref, o_ref):
        @pl.when(pl.program_id(0) == 0)
        def _():
            o_ref[...] = jnp.zeros_like(o_ref)

        for g in range(4):
            cols = slice(128 * g, 128 * g + 128)
            o_ref[g] += _dot(a_ref[:, cols], b_ref[:, cols], 0, 0)

    blk = pl.BlockSpec((tk, POOL_WIDTH), lambda k: (k, 0))
    return pl.pallas_call(
        body, name="dw_pool", grid=(s_len // tk,), in_specs=[blk, blk], out_specs=_full((4, 128, 128)),
        out_shape=jax.ShapeDtypeStruct((4, 128, 128), F32), compiler_params=_params(),
    )(pooled, dyp)


def _position():
    x, y, c = lax.axis_index("x"), lax.axis_index("y"), lax.axis_index("c")
    other_chips = [(1 - x, y), (x, 1 - y), (1 - x, 1 - y)]
    return x, y, c, other_chips


def _ag_weights(local_slab, row0, n_rows, name, collective_id):
    half = n_rows // 2
    quarter = half // 2
    assert quarter % 16 == 0

    def body(l_ref, g_ref, send, recv):
        x, y, c, chips = _position()
        me, (via_x, via_y, diagonal) = 2 * x + y, [2 * chip[0] + chip[1] for chip in chips]
        here, sibling, x_nbr, y_nbr = (x, y, c), (x, y, 1 - c), (1 - x, y, c), (x, 1 - y, c)
        peers = [sibling, x_nbr, y_nbr]
        barrier = pltpu.get_barrier_semaphore()
        for peer in peers:
            pl.semaphore_signal(barrier, inc=1, device_id=peer, device_id_type=MESH)
        pl.semaphore_wait(barrier, len(peers))

        def rows(core, part):
            start, size = (core * half, half) if part is None else (core * half + part * quarter, quarter)
            return pl.ds(pl.multiple_of(start, 16), size)

        def copy(k, chip_idx, where, to, src=None):
            dst = g_ref.at[chip_idx, where, :]
            return pltpu.make_async_remote_copy(src_ref=dst if src is None else src, dst_ref=dst, send_sem=send.at[k],
                                                recv_sem=recv.at[k], device_id=to, device_id_type=MESH)

        own_rows = l_ref.at[pl.ds(pl.multiple_of(row0 + c * half, 16), half), :]
        started = [copy(0, me, rows(c, None), x_nbr, src=own_rows), copy(1, me, rows(c, None), y_nbr, src=own_rows)]
        for cp in started:
            cp.start()
        after_arrival = [
            (copy(0, via_x, rows(c, None), here), [copy(4, via_x, rows(c, None), sibling), copy(3, via_x, rows(c, 1), y_nbr)]),
            (copy(1, via_y, rows(c, None), here), [copy(5, via_y, rows(c, None), sibling), copy(2, via_y, rows(c, 0), x_nbr)]),
            (copy(2, diagonal, rows(c, 0), here), [copy(6, diagonal, rows(c, 0), sibling)]),
            (copy(3, diagonal, rows(c, 1), here), [copy(7, diagonal, rows(c, 1), sibling)]),
        ]
        for arrival, onward in after_arrival:
            arrival.wait_recv()
            for cp in onward:
                cp.start()
            started += onward
        for cp in (copy(4, via_x, rows(1 - c, None), here), copy(5, via_y, rows(1 - c, None), here),
                   copy(6, diagonal, rows(1 - c, 0), here), copy(7, diagonal, rows(1 - c, 1), here)):
            cp.wait_recv()
        for cp in started:
            cp.wait_send()

    return pl.kernel(
        body, out_type=jax.ShapeDtypeStruct((N_CHIPS, n_rows, D_MODEL), BF16),
        mesh=plsc.ScalarSubcoreMesh(axis_name="sequencer", num_cores=1), name=name,
        scratch_types=[pltpu.SemaphoreType.DMA((8,)), pltpu.SemaphoreType.DMA((8,))],
        compiler_params=pltpu.CompilerParams(collective_id=collective_id),
    )(local_slab)


def _comm_call(body, peers_of, out_shape, n_sems, operand, name, collective_id):
    sems = [pltpu.SemaphoreType.DMA((n_sems,)), pltpu.SemaphoreType.DMA((n_sems,))]
    if collective_id is None:
        return pl.pallas_call(body, name=name, in_specs=[ANY], out_specs=ANY, out_shape=out_shape, scratch_shapes=sems)(operand)

    def with_handshake(in_ref, out_ref, send, recv):
        x, y, c, _ = _position()
        peers = peers_of(x, y, c)
        barrier = pltpu.get_barrier_semaphore()
        for peer in peers:
            pl.semaphore_signal(barrier, inc=1, device_id=peer, device_id_type=MESH)
        pl.semaphore_wait(barrier, len(peers))
        body(in_ref, out_ref, send, recv)

    return pl.kernel(with_handshake, out_type=out_shape, mesh=plsc.ScalarSubcoreMesh(axis_name="sequencer", num_cores=1),
                     name=name, scratch_types=sems, compiler_params=pltpu.CompilerParams(collective_id=collective_id))(operand)


def _rs_swap_halves(partial, name, collective_id=None):
    half = partial.shape[1] // 2

    def body(p_ref, r_ref, send, recv):
        x, y, c, _ = _position()
        theirs = pl.ds(pl.multiple_of((1 - c) * half, 16), half)
        cp = pltpu.make_async_remote_copy(src_ref=p_ref.at[:, theirs, :], dst_ref=r_ref, send_sem=send.at[0],
                                          recv_sem=recv.at[0], device_id=(x, y, 1 - c), device_id_type=MESH)
        cp.start()
        cp.wait()

    return _comm_call(body, lambda x, y, c: [(x, y, 1 - c)], jax.ShapeDtypeStruct((N_CHIPS, half, D_MODEL), BF16), 1,
                      partial, name, collective_id)


def _rs_add_halves(partial, other, core, name, after):
    half = other.shape[1]
    t = half // 2
    steps = half // t

    def body(core_ref, a_ref, b_ref, after_ref, o_ref):
        del after_ref
        o_ref[...] = (a_ref[...].astype(F32) + b_ref[...].astype(F32)).astype(BF16)

    return pl.pallas_call(
        body, name=name,
        grid_spec=pltpu.PrefetchScalarGridSpec(
            num_scalar_prefetch=1, grid=(N_CHIPS, steps),
            in_specs=[pl.BlockSpec((1, t, D_MODEL), lambda j, i, core_ref: (j, core_ref[0] * steps + i, 0)),
                      pl.BlockSpec((1, t, D_MODEL), lambda j, i, core_ref: (j, i, 0)), ANY],
            out_specs=pl.BlockSpec((1, t, D_MODEL), lambda j, i, core_ref: (j, i, 0))),
        out_shape=jax.ShapeDtypeStruct((N_CHIPS, half, D_MODEL), BF16),
        compiler_params=_params(n_axes=2),
    )(core, partial, other, after)


def _rs_exchange_chips(pre, name, collective_id=None):
    def body(s_ref, r_ref, send, recv):
        x, y, c, chips = _position()

        def copy(k, chunk, to):
            return pltpu.make_async_remote_copy(src_ref=s_ref.at[chunk], dst_ref=r_ref.at[k], send_sem=send.at[k],
                                                recv_sem=recv.at[k], device_id=to, device_id_type=MESH)

        sends = [copy(k, 2 * chip[0] + chip[1], (*chip, c)) for k, chip in enumerate(chips)]
        for cp in sends:
            cp.start()
        for cp in sends:
            cp.wait()

    return _comm_call(body, lambda x, y, c: [(1 - x, y, c), (x, 1 - y, c), (1 - x, 1 - y, c)],
                      jax.ShapeDtypeStruct((3, pre.shape[1], D_MODEL), BF16), 3, pre, name, collective_id)


def _rs_sum_chips(pre, received, place, name, after):
    half = pre.shape[1]
    t = half // 2 if half > 512 else half
    steps = half // t

    def body(place_ref, own_ref, r_ref, after_ref, o_ref):
        del after_ref
        acc = own_ref[0].astype(F32)
        for k in range(3):
            acc = acc + r_ref[k].astype(F32)
        o_ref[...] = acc

    return pl.pallas_call(
        body, name=name,
        grid_spec=pltpu.PrefetchScalarGridSpec(
            num_scalar_prefetch=1, grid=(steps,),
            in_specs=[pl.BlockSpec((1, t, D_MODEL), lambda i, place_ref: (place_ref[0], i, 0)),
                      pl.BlockSpec((3, t, D_MODEL), lambda i, place_ref: (0, i, 0)), ANY],
            out_specs=pl.BlockSpec((t, D_MODEL), lambda i, place_ref: (place_ref[1] * steps + i, 0))),
        out_shape=jax.ShapeDtypeStruct((2 * half, D_MODEL), F32),
        compiler_params=_params(),
    )(place, pre, received, after)


def _half_swap(g_ref, core, to, send, recv, k):
    half = g_ref.shape[0] // 2
    rows = g_ref.at[pl.ds(pl.multiple_of(core * half, 8), half), :]
    return pltpu.make_async_remote_copy(src_ref=rows, dst_ref=rows, send_sem=send.at[k], recv_sem=recv.at[k],
                                        device_id=to, device_id_type=MESH)


def _rs_finish_rows(grads, name, after):
    def body(f_ref, after_ref, g_ref, send, recv):
        del f_ref, after_ref
        x, y, c, _ = _position()
        mine = _half_swap(g_ref, c, (x, y, 1 - c), send, recv, 0)
        mine.start()
        _half_swap(g_ref, 1 - c, (x, y, c), send, recv, 0).wait_recv()
        mine.wait_send()

    return pl.pallas_call(
        body, name=name, in_specs=[ANY, ANY], out_specs=ANY, input_output_aliases={0: 0},
        out_shape=jax.ShapeDtypeStruct(grads.shape, F32),
        scratch_shapes=[pltpu.SemaphoreType.DMA((1,)), pltpu.SemaphoreType.DMA((1,))],
    )(grads, after)


def _small_gather(small, collective_id):
    def body(s_ref, t_ref, send, recv):
        x, y, c, chips = _position()
        sibling = (x, y, 1 - c)

        def slot(px, py, pc):
            return t_ref.at[4 * px + 2 * py + pc]

        def copy(k, block, to, src=None):
            return pltpu.make_async_remote_copy(src_ref=slot(*block) if src is None else src, dst_ref=slot(*block),
                                                send_sem=send.at[k], recv_sem=recv.at[k], device_id=to, device_id_type=MESH)

        own = pltpu.make_async_copy(s_ref, slot(x, y, c), send.at[7])
        own.start()
        first = [copy(0, (x, y, c), sibling, src=s_ref)]
        first += [copy(1 + k, (x, y, c), (*chip, c), src=s_ref) for k, chip in enumerate(chips)]
        for cp in first:
            cp.start()
        passed = []
        for k, chip in enumerate(chips):
            copy(1 + k, (*chip, c), (x, y, c)).wait_recv()
            fwd = copy(4 + k, (*chip, c), sibling)
            fwd.start()
            passed.append(fwd)
        copy(0, sibling, (x, y, c)).wait_recv()
        for k, chip in enumerate(chips):
            copy(4 + k, (*chip, 1 - c), (x, y, c)).wait_recv()
        for cp in first + passed:
            cp.wait_send()
        own.wait()

    peers_of = lambda x, y, c: [(x, y, 1 - c), (1 - x, y, c), (x, 1 - y, c), (1 - x, 1 - y, c)]
    return _comm_call(body, peers_of, jax.ShapeDtypeStruct((N_DEV, SMALL_ROWS, 128), F32), 8, small, "small_gather",
                      collective_id)


def _adam_update(w, g, m, v):
    m_new = ADAM_B1 * m + (1.0 - ADAM_B1) * g
    v_new = ADAM_B2 * v + (1.0 - ADAM_B2) * (g * g)
    m_hat = m_new / (1.0 - ADAM_B1 ** ADAM_STEP)
    v_hat = v_new / (1.0 - ADAM_B2 ** ADAM_STEP)
    return -ADAM_LR * (m_hat / (jnp.sqrt(v_hat) + ADAM_EPS) + ADAM_WD * w), m_new, v_new


def _adamw(w, g_rows, row_off, m, v, name):
    rows, cols = w.shape
    t = rows if rows <= 320 else (rows // 2 if rows % 256 else 256)

    def body(w_ref, g_ref, m_ref, v_ref, go_ref, d_ref, nm_ref, nv_ref):
        g = g_ref[...]
        go_ref[...] = g
        d_ref[...], nm_ref[...], nv_ref[...] = _adam_update(w_ref[...], g, m_ref[...], v_ref[...])

    blk = pl.BlockSpec((t, cols), lambda i: (i, 0))
    assert row_off % 8 == 0 and t % 8 == 0
    g_blk = pl.BlockSpec((pl.Element(t), pl.Element(cols)), lambda i: (pl.multiple_of(row_off + i * t, 8), 0))
    shape = jax.ShapeDtypeStruct((rows, cols), F32)
    return pl.pallas_call(
        body, name=name, grid=(rows // t,), in_specs=[blk, g_blk, blk, blk], out_specs=[blk] * 4, out_shape=[shape] * 4,
        compiler_params=_params(),
    )(w, g_rows, m, v)


SMALL_PARAMS = [("g_attn", (1, D_MODEL), 8), ("g_q", (1, HEAD_DIM), None), ("g_k", (1, HEAD_DIM), None),
                ("sinks", (1, N_Q_HEADS), None), ("rel_bias", (N_BUCKETS, N_Q_HEADS), None), ("w_pool", (512, 128), None),
                ("pool_scale", (1, POOL_WIDTH), 4), ("g_ffn", (1, D_MODEL), 8), ("g_ple", (1, D_MODEL), 8)]


def _adamw_small(tables, wmv):
    n_par = len(SMALL_PARAMS)

    def body(*refs):
        t_ref = refs[0]
        ins = refs[1:1 + 3 * n_par]
        loss_ref = refs[1 + 3 * n_par]
        outs = refs[2 + 3 * n_par:-1]
        tot_ref = refs[-1]
        total = t_ref[0]
        for d in range(1, N_DEV):
            total = total + t_ref[d]
        tot_ref[...] = total
        loss_ref[...] = tot_ref[pl.ds(SMALL["loss"], 1), 0:1]
        for i, (name, shape, split) in enumerate(SMALL_PARAMS):
            g_ref, d_ref, nm_ref, nv_ref = outs[4 * i:4 * i + 4]
            row = SMALL[name]
            if split:
                for k in range(split):
                    g_ref[:, 128 * k:128 * k + 128] = tot_ref[pl.ds(row + k, 1), :]
            else:
                g_ref[...] = tot_ref[pl.ds(row, shape[0]), 0:shape[1]]
            w_ref, m_ref, v_ref = ins[3 * i:3 * i + 3]
            d_ref[...], nm_ref[...], nv_ref[...] = _adam_update(w_ref[...], g_ref[...], m_ref[...], v_ref[...])

    shapes = [jax.ShapeDtypeStruct((1, 1), F32)]
    for _, shape, _ in SMALL_PARAMS:
        shapes += [jax.ShapeDtypeStruct(shape, F32)] * 4
    flat = [a for triple in wmv for a in triple]
    res = pl.pallas_call(
        body, name="adamw_small", in_specs=[VMEM_WHOLE] * (1 + 3 * n_par), out_specs=[VMEM_WHOLE] * len(shapes),
        out_shape=shapes, scratch_shapes=[pltpu.VMEM((SMALL_ROWS, 128), F32)],
    )(tables, *flat)
    return res[0], [res[1 + 4 * i:5 + 4 * i] for i in range(n_par)]


def _pack_ple_proj(shard):
    return shard.reshape(4, 64, 256).transpose(1, 0, 2).reshape(64, D_MODEL)


class _Reduction:
    def __init__(self, tag, place, ids=(None, None)):
        self.tag, self.place, self.ids = tag, place, ids

    def start(self, partial):
        self.partial = partial
        self.other = _rs_swap_halves(partial, "rs_swap_" + self.tag, self.ids[0])
        return partial

    def middle(self, after):
        self.pre = _rs_add_halves(self.partial, self.other, self.place[1:], "rs_add_" + self.tag, after)
        self.received = _rs_exchange_chips(self.pre, "rs_exchange_" + self.tag, self.ids[1])
        return self.pre

    def finish(self, after):
        return _rs_sum_chips(self.pre, self.received, self.place, "rs_sum_" + self.tag, after)


def _local_grads(x2, p2, tgt, wts, g_attn_norm, g_q, g_k, attn_sinks, rel_bias, w_pool, pool_scale, g_ffn_norm, g_ple_norm,
                 reduce_a):
    part_in, part_out, part_rest, local_slab, me = wts
    w_in, w_out, w_late = (part_in, local_slab, me), (part_out, local_slab, me), (part_rest, local_slab, me)
    bucket = jnp.asarray(_bucket_table())
    gq = jnp.tile(g_q, (1, 2))
    gk = jnp.tile(g_k, (1, 2))
    wpool = w_pool[0].astype(BF16)
    sinks = attn_sinks[0]
    bias_st = _bias_build(rel_bias.T, bucket)

    hn1, zqk, u, kn, vb, qst = _attn_in(x2, g_attn_norm, gq, gk, w_in)
    ost = _attn_fwd(qst, kn, vb, bias_st, sinks)
    pooled, mix, h1, hn2 = _mix_out(u, ost, x2, w_out, wpool, pool_scale, g_ffn_norm)
    loss_v, dgate, dup, act, dh2, hn3, dgl, dpp, dh1, dg_ffn, dg_ple = _ffn_ple(hn2, h1, p2, tgt, w_late, g_ffn_norm,
                                                                                   g_ple_norm)

    rows_a = SLAB_ROWS - SLAB["inT"][1]
    partial_a = None
    for name, lhs, rhs in (("out", mix, dh1), ("gateT", dgate, hn2), ("upT", dup, hn2), ("down", act, dh2), ("plg", hn3, dgl)):
        partial_a = _dw(lhs, rhs, "dw_" + name, into=(partial_a, rows_a, SLAB[name][0] - SLAB["inT"][1]))
    dw_plp = _dw(p2, dpp, "dw_plp").reshape(4, 64, N_CHIPS, 256).transpose(2, 1, 0, 3).reshape(N_CHIPS, 64, D_MODEL)
    partial_a = reduce_a.start(lax.dynamic_update_slice(partial_a, dw_plp, (0, SLAB["plp"][0] - SLAB["inT"][1], 0)))
    dost, du, dyp, dscale = _mix_out_bwd(dh1, w_out, pooled, wpool, pool_scale, partial_a)
    pre_a = reduce_a.middle(du)
    dqst, dk, dv, dbias, dsink_rows = _attn_bwd(qst, kn, vb, dost, bias_st, sinks, pre_a)
    dz, dx, dg_attn, dgq, dgk = _attn_in_bwd(dqst, zqk, dk, dv, du, x2, dh1, w_in, g_attn_norm, gq, gk)

    partial_b = _dw(dz, hn1, "dw_in").reshape(N_CHIPS, -1, D_MODEL)
    small = _small_pack(dg_attn, dg_ffn, dg_ple, dscale, dgq, dgk, dbias, dsink_rows, bucket, loss_v, _dw_pool(pooled, dyp))
    return dx, partial_b, small


def kernel(x, p, w_in, w_out, g_attn_norm, g_q, g_k, attn_sinks, rel_bias, w_pool, pool_scale, g_ffn_norm, w_gate, w_up, w_down, g_ple_norm, w_ple_gate, w_ple_proj, loss_target, m_w_in, m_w_out, m_g_attn_norm, m_g_q, m_g_k, m_attn_sinks, m_rel_bias, m_w_pool, m_pool_scale, m_g_ffn_norm, m_w_gate, m_w_up, m_w_down, m_g_ple_norm, m_w_ple_gate, m_w_ple_proj, v_w_in, v_w_out, v_g_attn_norm, v_g_q, v_g_k, v_attn_sinks, v_rel_bias, v_w_pool, v_pool_scale, v_g_ffn_norm, v_w_gate, v_w_up, v_w_down, v_g_ple_norm, v_w_ple_gate, v_w_ple_proj):
    core = lax.axis_index("c").astype(jnp.int32).reshape(1)
    me = (2 * lax.axis_index("x") + lax.axis_index("y")).astype(jnp.int32).reshape(1)

    local_slab = jnp.concatenate(
        [w_in[0].T, w_out[0], w_gate[0].T, w_up[0].T, w_down[0], w_ple_gate[0], _pack_ple_proj(w_ple_proj[0])],
        axis=0).astype(BF16)
    gathered = [_ag_weights(local_slab, start, stop - start, name, collective_id)
                for (start, stop), name, collective_id in zip(GATHER_PARTS, ("ag_in", "ag_out", "ag_rest"), (1, 2, 5))]
    wts = (*gathered, local_slab, me)

    place = jnp.concatenate([me, core])
    reduce_a = _Reduction("a", place, ids=(3, 4))
    dx, partial_b, small = _local_grads(x[0], p[0, 0], loss_target[0], wts, g_attn_norm, g_q, g_k, attn_sinks, rel_bias,
                                        w_pool, pool_scale, g_ffn_norm, g_ple_norm, reduce_a)
    reduce_b = _Reduction("b", place, ids=(6, 7))
    reduce_b.start(partial_b)
    small_all = _small_gather(small, 8)
    summed_a = reduce_a.finish(small)
    pre_b = reduce_b.middle(summed_a)
    grads_a = _rs_finish_rows(summed_a, "rs_finish_a", pre_b)

    def rows(name):
        return grads_a, SLAB[name][0] - SLAB["inT"][1]

    plp_rows = grads_a[SLAB["plp"][0] - SLAB["inT"][1]:]
    big = {
        "w_out": (w_out, m_w_out, v_w_out, rows("out"), False),
        "w_gate": (w_gate, m_w_gate, v_w_gate, rows("gateT"), True),
        "w_up": (w_up, m_w_up, v_w_up, rows("upT"), True),
        "w_down": (w_down, m_w_down, v_w_down, rows("down"), False),
        "w_ple_gate": (w_ple_gate, m_w_ple_gate, v_w_ple_gate, rows("plg"), False),
        "w_ple_proj": (w_ple_proj, m_w_ple_proj, v_w_ple_proj,
                       (plp_rows.reshape(64, 4, 256).transpose(1, 0, 2).reshape(PLE_DIM, PLE_DIM), 0), False),
        "w_in": (w_in, m_w_in, v_w_in, None, True),
    }
    small_params = {
        "g_attn_norm": (g_attn_norm, m_g_attn_norm, v_g_attn_norm), "g_q": (g_q, m_g_q, v_g_q), "g_k": (g_k, m_g_k, v_g_k),
        "attn_sinks": (attn_sinks, m_attn_sinks, v_attn_sinks), "rel_bias": (rel_bias, m_rel_bias, v_rel_bias),
        "w_pool": tuple(a.reshape(512, 128) for a in (w_pool, m_w_pool, v_w_pool)),
        "pool_scale": (pool_scale, m_pool_scale, v_pool_scale), "g_ffn_norm": (g_ffn_norm, m_g_ffn_norm, v_g_ffn_norm),
        "g_ple_norm": (g_ple_norm, m_g_ple_norm, v_g_ple_norm),
    }

    grads, deltas, new_ms, new_vs = {}, {}, {}, {}
    out = None
    for name, (w, m, v, g_src, transposed) in big.items():
        if g_src is None:
            g_src = (_rs_finish_rows(reduce_b.finish(out[-1]), "rs_finish_b", out[-1]), 0)
        view = (lambda a: a.T) if transposed else (lambda a: a)
        out = _adamw(view(w[0]), *g_src, view(m[0]), view(v[0]), "adamw_" + name)
        grads[name], deltas[name], new_ms[name], new_vs[name] = (view(a)[None] for a in out)

    loss, small_out = _adamw_small(small_all, list(small_params.values()))
    for name, (g2, d, nm, nv) in zip(small_params, small_out):
        shape = w_pool.shape if name == "w_pool" else g2.shape
        grads[name], deltas[name], new_ms[name], new_vs[name] = (a.reshape(shape) for a in (g2, d, nm, nv))

    order = ["w_in", "w_out", "g_attn_norm", "g_q", "g_k", "attn_sinks", "rel_bias", "w_pool", "pool_scale", "g_ffn_norm",
             "w_gate", "w_up", "w_down", "g_ple_norm", "w_ple_gate", "w_ple_proj"]
    return (loss.reshape(()), dx[None], *[grads[n] for n in order], *[deltas[n] for n in order],
            *[new_ms[n] for n in order], *[new_vs[n] for n in order])
```

```python
import functools

import numpy as np
import jax
import jax.numpy as jnp
from jax import lax
from jax.experimental import pallas as pl
from jax.experimental.pallas import tpu as pltpu
from jax.experimental.pallas import tpu_sc as plsc

F32 = jnp.float32
BF16 = jnp.bfloat16
MESH = pl.DeviceIdType.MESH

D_MODEL = 1024
HEAD_DIM = 64
N_Q_HEADS = 8
ATTN_WIDTH = 512
KV_WIDTH = 128
POOL_WIDTH = 512
IN_WIDTH = 1280
D_FF = 2816
PLE_DIM = 256
FF_CHUNK = 704
BLOCK = 128
N_BUCKETS = 32
MAX_DISTANCE = 128
POOL_SIZES = (2, 4, 8, 16)
EPS = 1e-6
NEG = -1e30
N_CHIPS = 4
N_DEV = 8

ADAM_LR = 0.001
ADAM_B1 = 0.9
ADAM_B2 = 0.999
ADAM_EPS = 1e-08
ADAM_WD = 0.01
ADAM_STEP = 10

SLAB = {"inT": (0, 320), "out": (320, 256), "gateT": (576, 704), "upT": (1280, 704), "down": (1984, 704),
        "plg": (2688, 256), "plp": (2944, 64)}
SLAB_ROWS = 3008
HALF_ROWS = SLAB_ROWS // 2
GATHER_PARTS = ((0, 576), (576, SLAB_ROWS))
POOL_HALO = 24

SMALL = {"g_attn": 0, "g_ffn": 8, "g_ple": 16, "pool_scale": 24, "g_q": 28, "g_k": 29, "sinks": 30, "loss": 31,
         "rel_bias": 32, "w_pool": 64}
SMALL_ROWS = 576

VMEM_LIMIT_BIG = 60 * 1024 * 1024
VMEM_LIMIT = 48 * 1024 * 1024


def _params(vmem=VMEM_LIMIT, n_axes=1):
    return pltpu.CompilerParams(dimension_semantics=("arbitrary",) * n_axes, vmem_limit_bytes=vmem)


def _dot(a, b, ca, cb):
    return lax.dot_general(a, b, (((ca,), (cb,)), ((), ())), preferred_element_type=F32)


def _full(shape):
    return pl.BlockSpec(shape, lambda i: (0,) * len(shape))


ANY = pl.BlockSpec(memory_space=pl.ANY)
VMEM_WHOLE = pl.BlockSpec(memory_space=pltpu.VMEM)


W_SPECS = [ANY, ANY, pl.BlockSpec(memory_space=pltpu.SMEM)]


def _load_rows(w_refs, name, dst_ref, sems):
    slab_ref, local_ref, me_ref = w_refs
    off, rows = SLAB[name]
    slab_off = off - max(start for start, _ in GATHER_PARTS if start <= off)
    me = me_ref[0]
    for phase in ("start", "wait"):
        for j in range(N_CHIPS):
            dst = dst_ref.at[pl.ds(j * rows, rows), :]
            theirs = pltpu.make_async_copy(slab_ref.at[j, pl.ds(slab_off, rows), :], dst, sems.at[j])
            own = pltpu.make_async_copy(local_ref.at[pl.ds(slab_off, rows), :], dst, sems.at[j])

            @pl.when(me == j)
            def _():
                getattr(own, phase)()

            @pl.when(me != j)
            def _():
                getattr(theirs, phase)()


def _rms_fwd(x, g):
    r = lax.rsqrt(jnp.mean(x * x, axis=-1, keepdims=True) + EPS)
    return x * r * g


def _rms_bwd(x, g, dy):
    r = lax.rsqrt(jnp.mean(x * x, axis=-1, keepdims=True) + EPS)
    xn = x * r
    dyg = dy * g
    dx = r * (dyg - xn * jnp.mean(dyg * xn, axis=-1, keepdims=True))
    return dx, jnp.sum(dy * xn, axis=0, keepdims=True)


def _half_sum(v, lo):
    s_lo = jnp.sum(jnp.where(lo, v, 0.0), axis=-1, keepdims=True)
    s_hi = jnp.sum(jnp.where(lo, 0.0, v), axis=-1, keepdims=True)
    return jnp.where(lo, s_lo, s_hi)


def _half_sum_mxu(v):
    upper = lax.broadcasted_iota(jnp.int32, (128, 128), 0) < 64
    left = lax.broadcasted_iota(jnp.int32, (128, 128), 1) < 64
    ones = jnp.where(upper == left, 1.0, 0.0).astype(BF16)
    high = v.astype(BF16)
    low = (v - high.astype(F32)).astype(BF16)
    return _dot(high, ones, 1, 0) + _dot(low, ones, 1, 0)


def _pair_norm(zp, g, lo):
    r = lax.rsqrt(_half_sum(zp * zp, lo) * (1.0 / HEAD_DIM) + EPS)
    return zp * r * g


def _pair_norm_bwd(zp, g, dy):
    r = lax.rsqrt(_half_sum_mxu(zp * zp) * (1.0 / HEAD_DIM) + EPS)
    xn = zp * r
    dyg = dy * g
    dx = r * (dyg - xn * (_half_sum_mxu(dyg * xn) * (1.0 / HEAD_DIM)))
    return dx, jnp.sum(dy * xn, axis=0, keepdims=True)


def _to_stacked(pair, group, lo):
    rolled = pltpu.roll(pair, 64, axis=1)
    if group == 0:
        return jnp.where(lo, pair, 0.0), jnp.where(lo, rolled, 0.0)
    return jnp.where(lo, 0.0, rolled), jnp.where(lo, 0.0, pair)


def _from_stacked(even, odd, group, lo):
    if group == 0:
        return jnp.where(lo, even, pltpu.roll(odd, 64, axis=1))
    return jnp.where(lo, pltpu.roll(even, 64, axis=1), odd)


def _sigmoid(v):
    return 1.0 / (1.0 + jnp.exp(-v))


def _pool_counts(tile, n_rows):
    t1 = tile * n_rows + lax.broadcasted_iota(jnp.int32, (n_rows, POOL_WIDTH), 0) + 1
    lane = lax.broadcasted_iota(jnp.int32, (n_rows, POOL_WIDTH), 1)
    win = jnp.where(lane < 128, 2, jnp.where(lane < 256, 4, jnp.where(lane < 384, 8, 16)))
    return jnp.minimum(t1, win).astype(F32)


def _attn_in(x2, g_attn, gq, gk, wts):
    s_len = x2.shape[0]
    t = 512

    def body(x_ref, g_ref, gq_ref, gk_ref, sl_ref, lo_ref, me_ref, hn_ref, zqk_ref, u_ref, kn_ref, v_ref, qst_ref, w_ref, sems):
        @pl.when(pl.program_id(0) == 0)
        def _():
            _load_rows((sl_ref, lo_ref, me_ref), "inT", w_ref, sems)

        hn = _rms_fwd(x_ref[...], g_ref[...]).astype(BF16)
        hn_ref[...] = hn
        z = _dot(hn, w_ref[...], 1, 1)
        zqk_ref[...] = z[:, :640]
        u_ref[...] = z[:, 768:]
        v_ref[...] = z[:, 640:768].astype(BF16)
        lo = lax.broadcasted_iota(jnp.int32, (t, 128), 1) < 64
        kn_ref[...] = _pair_norm(z[:, 512:640], gk_ref[...], lo).astype(BF16)
        for p in range(4):
            qn = _pair_norm(z[:, 128 * p:128 * p + 128], gq_ref[...], lo)
            even, odd = _to_stacked(qn, p // 2, lo)
            qst_ref[2 * p] = even.astype(BF16)
            qst_ref[2 * p + 1] = odd.astype(BF16)

    row = lambda w: pl.BlockSpec((t, w), lambda i: (i, 0))
    return pl.pallas_call(
        body, name="attn_in", grid=(s_len // t,),
        in_specs=[row(D_MODEL), _full((1, D_MODEL)), _full((1, 128)), _full((1, 128))] + W_SPECS,
        out_specs=[row(D_MODEL), row(640), row(POOL_WIDTH), row(128), row(128),
                   pl.BlockSpec((N_Q_HEADS, t, 128), lambda i: (0, i, 0))],
        out_shape=[jax.ShapeDtypeStruct((s_len, D_MODEL), BF16), jax.ShapeDtypeStruct((s_len, 640), F32),
                   jax.ShapeDtypeStruct((s_len, POOL_WIDTH), F32), jax.ShapeDtypeStruct((s_len, 128), BF16),
                   jax.ShapeDtypeStruct((s_len, 128), BF16), jax.ShapeDtypeStruct((N_Q_HEADS, s_len, 128), BF16)],
        scratch_shapes=[pltpu.VMEM((IN_WIDTH, D_MODEL), BF16), pltpu.SemaphoreType.DMA((N_CHIPS,))],
        compiler_params=_params(),
    )(x2, g_attn, gq, gk, *wts)


def _bucket_table():
    i_idx = np.arange(BLOCK)[:, None]
    j_idx = np.arange(2 * BLOCK)[None, :]
    d = BLOCK + i_idx - j_idx
    n = np.maximum(d, 0)
    max_exact = N_BUCKETS // 2
    nf = np.maximum(n, 1).astype(np.float64)
    large = max_exact + (np.log(nf / max_exact) / np.log(MAX_DISTANCE / max_exact) * (N_BUCKETS - max_exact)).astype(np.int64)
    large = np.minimum(large, N_BUCKETS - 1)
    bucket = np.where(n < max_exact, n, large)
    return np.where((d >= 0) & (d < BLOCK), bucket, -1).astype(np.int32)


def _bias_build(rel_bias_t, bucket):
    def body(rb_ref, bucket_ref, out_ref):
        bk = bucket_ref[...]
        for h in range(N_Q_HEADS):
            acc = jnp.full((BLOCK, 2 * BLOCK), NEG, F32)
            for b in range(N_BUCKETS):
                acc = jnp.where(bk == b, rb_ref[h, b], acc)
            out_ref[0, pl.ds(h * BLOCK, BLOCK), :] = acc
            out_ref[1, pl.ds(h * BLOCK, BLOCK), :] = acc
            out_ref[1, pl.ds(h * BLOCK, BLOCK), 0:BLOCK] = jnp.full((BLOCK, BLOCK), NEG, F32)

    return pl.pallas_call(
        body, name="bias_build",
        in_specs=[pl.BlockSpec(memory_space=pltpu.SMEM), VMEM_WHOLE], out_specs=VMEM_WHOLE,
        out_shape=jax.ShapeDtypeStruct((2, N_Q_HEADS * BLOCK, 2 * BLOCK), F32),
    )(rel_bias_t, bucket)


def _head_softmax(s_ref, bias_ref, sink_ref, h):
    rows = pl.ds(pl.multiple_of(h * BLOCK, BLOCK), BLOCK)
    s = s_ref[rows, :] * (HEAD_DIM ** -0.5) + bias_ref[rows, :]
    sink = sink_ref[h]
    m = jnp.maximum(jnp.max(s, axis=-1, keepdims=True), sink)
    p = jnp.exp(s - m)
    e_sink = jnp.exp(sink - m)
    inv = 1.0 / (jnp.sum(p, axis=-1, keepdims=True) + e_sink)
    return rows, p * inv, e_sink * inv


def _attn_specs():
    prev = lambda i: (jnp.maximum(i - 1, 0), 0)
    cur = lambda i: (i, 0)
    stacked = pl.BlockSpec((N_Q_HEADS, BLOCK, 128), lambda i: (0, i, 0))
    kv = [pl.BlockSpec((BLOCK, 128), prev), pl.BlockSpec((BLOCK, 128), cur)]
    consts = [pl.BlockSpec((None, N_Q_HEADS * BLOCK, 2 * BLOCK), lambda i: (jnp.where(i == 0, 1, 0), 0, 0)),
              pl.BlockSpec(memory_space=pltpu.SMEM)]
    return stacked, kv, consts


def _head_lane_mask():
    rows = lax.broadcasted_iota(jnp.int32, (N_Q_HEADS * BLOCK, 128), 0)
    lanes = lax.broadcasted_iota(jnp.int32, (N_Q_HEADS * BLOCK, 128), 1)
    return (rows < 4 * BLOCK) == (lanes < 64)


def _attn_fwd(qst, kn, vb, bias_st, sinks):
    s_len = kn.shape[0]

    def body(q_ref, kp_ref, kc_ref, vp_ref, vc_ref, bias_ref, sink_ref, o_ref, s_ref, p_ref):
        q = q_ref[...].reshape(N_Q_HEADS * BLOCK, 128)
        s_ref[...] = _dot(q, jnp.concatenate([kp_ref[...], kc_ref[...]], axis=0), 1, 1)

        def head(h, carry):
            rows, probs, _ = _head_softmax(s_ref, bias_ref, sink_ref, h)
            p_ref[rows, :] = probs.astype(BF16)
            return carry

        lax.fori_loop(0, N_Q_HEADS, head, 0, unroll=True)
        o = _dot(p_ref[...], jnp.concatenate([vp_ref[...], vc_ref[...]], axis=0), 1, 0)
        o_ref[...] = jnp.where(_head_lane_mask(), o, 0.0).astype(BF16).reshape(N_Q_HEADS, BLOCK, 128)

    stacked, kv, consts = _attn_specs()
    return pl.pallas_call(
        body, name="attn_fwd", grid=(s_len // BLOCK,),
        in_specs=[stacked] + kv + kv + consts, out_specs=stacked,
        out_shape=jax.ShapeDtypeStruct((N_Q_HEADS, s_len, 128), BF16),
        scratch_shapes=[pltpu.VMEM((N_Q_HEADS * BLOCK, 2 * BLOCK), F32), pltpu.VMEM((N_Q_HEADS * BLOCK, 2 * BLOCK), BF16)],
        compiler_params=_params(),
    )(qst, kn, kn, vb, vb, bias_st, sinks)


def _mix_out(u, ost, x2, wts, wpool, pool_scale, g_ffn):
    s_len = x2.shape[0]
    t = 512
    n = t + 16

    def body(u_ref, o_ref, x_ref, sl_ref, lo_ref, me_ref, wp_ref, sc_ref, g_ref, pooled_ref, mix_ref, h1_ref, hn_ref,
             w_ref, ext_ref, st_ref, sems):
        i = pl.program_id(0)

        @pl.when(i == 0)
        def _():
            _load_rows((sl_ref, lo_ref, me_ref), "out", w_ref, sems)
            ext_ref[...] = jnp.zeros_like(ext_ref)
            st_ref[...] = jnp.zeros_like(st_ref)

        u_tile = u_ref[...]
        ext_ref[pl.ds(POOL_HALO, t), :] = u_tile
        st_ref[pl.ds(8, n), :] = ext_ref[pl.ds(8, n), :] + ext_ref[pl.ds(7, n), :]
        st_ref[pl.ds(8, n), 128:] = st_ref[pl.ds(8, n), 128:] + st_ref[pl.ds(6, n), 128:]
        st_ref[pl.ds(8, n), 256:] = st_ref[pl.ds(8, n), 256:] + st_ref[pl.ds(4, n), 256:]
        st_ref[pl.ds(8, n), 384:] = st_ref[pl.ds(8, n), 384:] + st_ref[pl.ds(0, n), 384:]
        ext_ref[pl.ds(0, POOL_HALO), :] = ext_ref[pl.ds(t, POOL_HALO), :]
        pooled = (st_ref[pl.ds(POOL_HALO, t), :] / _pool_counts(i, t) - u_tile).astype(BF16)
        pooled_ref[...] = pooled
        for g in range(4):
            cols = slice(128 * g, 128 * g + 128)
            y = _dot(pooled[:, cols], wp_ref[g], 1, 0) * sc_ref[:, cols]
            mix_ref[:, ATTN_WIDTH + 128 * g:ATTN_WIDTH + 128 * g + 128] = y.astype(BF16)
        lo = lax.broadcasted_iota(jnp.int32, (t, 128), 1) < 64
        for p in range(4):
            a = _from_stacked(o_ref[2 * p].astype(F32), o_ref[2 * p + 1].astype(F32), p // 2, lo)
            mix_ref[:, 128 * p:128 * p + 128] = a.astype(BF16)
        h1 = x_ref[...] + _dot(mix_ref[...], w_ref[...], 1, 0)
        h1_ref[...] = h1
        hn_ref[...] = _rms_fwd(h1, g_ref[...]).astype(BF16)

    row = lambda w: pl.BlockSpec((t, w), lambda i: (i, 0))
    return pl.pallas_call(
        body, name="mix_out", grid=(s_len // t,),
        in_specs=[row(POOL_WIDTH), pl.BlockSpec((N_Q_HEADS, t, 128), lambda i: (0, i, 0)), row(D_MODEL)] + W_SPECS
        + [_full((4, 128, 128)), _full((1, POOL_WIDTH)), _full((1, D_MODEL))],
        out_specs=[row(POOL_WIDTH), row(D_MODEL), row(D_MODEL), row(D_MODEL)],
        out_shape=[jax.ShapeDtypeStruct((s_len, POOL_WIDTH), BF16), jax.ShapeDtypeStruct((s_len, D_MODEL), BF16),
                   jax.ShapeDtypeStruct((s_len, D_MODEL), F32), jax.ShapeDtypeStruct((s_len, D_MODEL), BF16)],
        scratch_shapes=[pltpu.VMEM((D_MODEL, D_MODEL), BF16), pltpu.VMEM((t + POOL_HALO, POOL_WIDTH), F32),
                        pltpu.VMEM((t + POOL_HALO, POOL_WIDTH), F32), pltpu.SemaphoreType.DMA((N_CHIPS,))],
        compiler_params=_params(),
    )(u, ost, x2, *wts, wpool, pool_scale, g_ffn)


def _ffn_ple(hn2, h1, p2, tgt, wts, g_ffn, g_ple):
    s_len = h1.shape[0]
    t = 256
    n_tiles = s_len // t

    def body(hn_ref, h1_ref, p_ref, tgt_ref, sl_ref, lo_ref, me_ref, gf_ref, gp_ref,
             loss_ref, dgate_ref, dup_ref, act_ref, dh2b_ref, hn3_ref, dgl_ref, dpp_ref, dh1_ref, dgf_ref, dgp_ref,
             wg_ref, wu_ref, wd_ref, wl_ref, wp_ref, packed_ref, gate_s, up_s, loss_acc, sems):
        i = pl.program_id(0)

        @pl.when(i == 0)
        def _():
            w_refs = (sl_ref, lo_ref, me_ref)
            _load_rows(w_refs, "gateT", wg_ref, sems)
            _load_rows(w_refs, "upT", wu_ref, sems)
            _load_rows(w_refs, "down", wd_ref, sems)
            _load_rows(w_refs, "plg", wl_ref, sems)
            _load_rows(w_refs, "plp", packed_ref, sems)
            for j in range(N_CHIPS):
                for q in range(4):
                    wp_ref[pl.ds(64 * q, 64), 256 * j:256 * j + 256] = packed_ref[pl.ds(64 * j, 64), 256 * q:256 * q + 256]
            loss_acc[...] = jnp.zeros_like(loss_acc)
            dgf_ref[...] = jnp.zeros_like(dgf_ref)
            dgp_ref[...] = jnp.zeros_like(dgp_ref)

        hn = hn_ref[...]
        h1v = h1_ref[...]
        h2 = h1v
        for ch in range(N_CHIPS):
            rows = pl.ds(ch * FF_CHUNK, FF_CHUNK)
            gate = _dot(hn, wg_ref[rows, :], 1, 1)
            up = _dot(hn, wu_ref[rows, :], 1, 1)
            gate_s[ch] = gate
            up_s[ch] = up
            act = (gate * _sigmoid(gate) * up).astype(BF16)
            act_ref[ch] = act
            h2 = h2 + _dot(act, wd_ref[rows, :], 1, 0)
        gp = gp_ref[...]
        hn3 = _rms_fwd(h2, gp).astype(BF16)
        hn3_ref[...] = hn3
        gate2 = _sigmoid(_dot(hn3, wl_ref[...], 1, 0))
        pp = _dot(p_ref[...].astype(BF16), wp_ref[...], 1, 0)
        err = h2 + gate2 * pp - tgt_ref[...]
        loss_acc[...] += jnp.sum(err * err, axis=0, keepdims=True)
        dy = err * (1.0 / D_MODEL)
        dpp_ref[...] = (dy * gate2).astype(BF16)
        dgl = (dy * pp * gate2 * (1.0 - gate2)).astype(BF16)
        dgl_ref[...] = dgl
        dx3, dg3 = _rms_bwd(h2, gp, _dot(dgl, wl_ref[...], 1, 1))
        dh2 = dy + dx3
        dgp_ref[...] += dg3
        dh2b = dh2.astype(BF16)
        dh2b_ref[...] = dh2b
        dhn = jnp.zeros((t, D_MODEL), F32)
        for ch in range(N_CHIPS):
            rows = pl.ds(ch * FF_CHUNK, FF_CHUNK)
            dact = _dot(dh2b, wd_ref[rows, :], 1, 1)
            gate_v = gate_s[ch]
            up_v = up_s[ch]
            sg = _sigmoid(gate_v)
            dup = (dact * (gate_v * sg)).astype(BF16)
            dgate = (dact * up_v * (sg * (1.0 + gate_v * (1.0 - sg)))).astype(BF16)
            dup_ref[ch] = dup
            dgate_ref[ch] = dgate
            dhn = dhn + _dot(dgate, wg_ref[rows, :], 1, 0) + _dot(dup, wu_ref[rows, :], 1, 0)
        dx, dg = _rms_bwd(h1v, gf_ref[...], dhn)
        dh1_ref[...] = dh2 + dx
        dgf_ref[...] += dg

        @pl.when(i == n_tiles - 1)
        def _():
            total = jnp.sum(loss_acc[...], axis=-1, keepdims=True) * (0.5 / D_MODEL)
            loss_ref[...] = jnp.broadcast_to(total, loss_ref.shape)

    row = lambda w: pl.BlockSpec((t, w), lambda i: (i, 0))
    chunked = pl.BlockSpec((N_CHIPS, t, FF_CHUNK), lambda i: (0, i, 0))
    vec = _full((1, D_MODEL))
    act_shape = jax.ShapeDtypeStruct((N_CHIPS, s_len, FF_CHUNK), BF16)
    tok = lambda dtype: jax.ShapeDtypeStruct((s_len, D_MODEL), dtype)
    return pl.pallas_call(
        body, name="ffn_ple", grid=(n_tiles,),
        in_specs=[row(D_MODEL), row(D_MODEL), row(PLE_DIM), row(D_MODEL)] + W_SPECS + [vec, vec],
        out_specs=[_full((1, 128)), chunked, chunked, chunked] + [row(D_MODEL)] * 5 + [vec, vec],
        out_shape=[jax.ShapeDtypeStruct((1, 128), F32), act_shape, act_shape, act_shape, tok(BF16), tok(BF16), tok(BF16),
                   tok(BF16), tok(F32), jax.ShapeDtypeStruct((1, D_MODEL), F32), jax.ShapeDtypeStruct((1, D_MODEL), F32)],
        scratch_shapes=[pltpu.VMEM((D_FF, D_MODEL), BF16)] * 3
        + [pltpu.VMEM((D_MODEL, D_MODEL), BF16), pltpu.VMEM((PLE_DIM, D_MODEL), BF16), pltpu.VMEM((PLE_DIM, D_MODEL), BF16),
           pltpu.VMEM((N_CHIPS, t, FF_CHUNK), F32), pltpu.VMEM((N_CHIPS, t, FF_CHUNK), F32), pltpu.VMEM((1, D_MODEL), F32),
           pltpu.SemaphoreType.DMA((N_CHIPS,))],
        compiler_params=_params(VMEM_LIMIT_BIG),
    )(hn2, h1, p2, tgt, *wts, g_ffn, g_ple)


def _mix_out_bwd(dh1, wts, pooled, wpool, pool_scale, after):
    s_len = dh1.shape[0]
    t = 512
    n = t + 16
    n_tiles = s_len // t

    def body(dh1_ref, sl_ref, lo_ref, me_ref, pooled_ref, wp_ref, sc_ref, after_ref, dost_ref, du_ref, dyp_ref, dsc_ref,
             w_ref, ext_ref, st_ref, sems):
        del after_ref
        i = pl.program_id(0)

        @pl.when(i == 0)
        def _():
            _load_rows((sl_ref, lo_ref, me_ref), "out", w_ref, sems)
            ext_ref[...] = jnp.zeros_like(ext_ref)
            st_ref[...] = jnp.zeros_like(st_ref)
            dsc_ref[...] = jnp.zeros_like(dsc_ref)

        dmix = _dot(dh1_ref[...].astype(BF16), w_ref[...], 1, 1)
        lo = lax.broadcasted_iota(jnp.int32, (t, 128), 1) < 64
        for p in range(4):
            even, odd = _to_stacked(dmix[:, 128 * p:128 * p + 128], p // 2, lo)
            dost_ref[2 * p] = even.astype(BF16)
            dost_ref[2 * p + 1] = odd.astype(BF16)
        pooled_v = pooled_ref[...]
        counts = _pool_counts(n_tiles - 1 - i, t)
        for g in range(4):
            cols = slice(128 * g, 128 * g + 128)
            dm = dmix[:, ATTN_WIDTH + 128 * g:ATTN_WIDTH + 128 * g + 128]
            ypre = _dot(pooled_v[:, cols], wp_ref[g], 1, 0)
            dsc_ref[:, cols] += jnp.sum(ypre * dm, axis=0, keepdims=True)
            dyp = (dm * sc_ref[:, cols]).astype(BF16)
            dyp_ref[:, cols] = dyp
            dpooled = _dot(dyp, wp_ref[g], 1, 1)
            du_ref[:, cols] = -dpooled
            ext_ref[pl.ds(0, t), cols] = dpooled / counts[:, cols]
        st_ref[pl.ds(0, n), :] = ext_ref[pl.ds(0, n), :] + ext_ref[pl.ds(1, n), :]
        st_ref[pl.ds(0, n), 128:] = st_ref[pl.ds(0, n), 128:] + st_ref[pl.ds(2, n), 128:]
        st_ref[pl.ds(0, n), 256:] = st_ref[pl.ds(0, n), 256:] + st_ref[pl.ds(4, n), 256:]
        st_ref[pl.ds(0, n), 384:] = st_ref[pl.ds(0, n), 384:] + st_ref[pl.ds(8, n), 384:]
        ext_ref[pl.ds(t, POOL_HALO), :] = ext_ref[pl.ds(0, POOL_HALO), :]
        du_ref[...] += st_ref[pl.ds(0, t), :]

    rev = lambda w: pl.BlockSpec((t, w), lambda i: (n_tiles - 1 - i, 0))
    return pl.pallas_call(
        body, name="mix_out_bwd", grid=(n_tiles,),
        in_specs=[rev(D_MODEL)] + W_SPECS + [rev(POOL_WIDTH), _full((4, 128, 128)), _full((1, POOL_WIDTH)), ANY],
        out_specs=[pl.BlockSpec((N_Q_HEADS, t, 128), lambda i: (0, n_tiles - 1 - i, 0)), rev(POOL_WIDTH), rev(POOL_WIDTH),
                   _full((1, POOL_WIDTH))],
        out_shape=[jax.ShapeDtypeStruct((N_Q_HEADS, s_len, 128), BF16), jax.ShapeDtypeStruct((s_len, POOL_WIDTH), F32),
                   jax.ShapeDtypeStruct((s_len, POOL_WIDTH), BF16), jax.ShapeDtypeStruct((1, POOL_WIDTH), F32)],
        scratch_shapes=[pltpu.VMEM((D_MODEL, D_MODEL), BF16), pltpu.VMEM((t + POOL_HALO, POOL_WIDTH), F32),
                        pltpu.VMEM((t + POOL_HALO, POOL_WIDTH), F32), pltpu.SemaphoreType.DMA((N_CHIPS,))],
        compiler_params=_params(),
    )(dh1, *wts, pooled, wpool, pool_scale, after)


def _attn_bwd(qst, kn, vb, dost, bias_st, sinks, after):
    s_len = kn.shape[0]

    def body(q_ref, kp_ref, kc_ref, vp_ref, vc_ref, do_ref, bias_ref, sink_ref, after_ref, dq_ref, dk_ref, dv_ref, dbias_ref,
             dsink_ref, s_ref, dp_ref, p_ref, dl_ref):
        del after_ref
        i = pl.program_id(0)

        @pl.when(i == 0)
        def _():
            dk_ref[...] = jnp.zeros_like(dk_ref)
            dv_ref[...] = jnp.zeros_like(dv_ref)
            dbias_ref[...] = jnp.zeros_like(dbias_ref)
            dsink_ref[...] = jnp.zeros_like(dsink_ref)

        q = q_ref[...].reshape(N_Q_HEADS * BLOCK, 128)
        do = do_ref[...].reshape(N_Q_HEADS * BLOCK, 128)
        k2 = jnp.concatenate([kp_ref[...], kc_ref[...]], axis=0)
        s_ref[...] = _dot(q, k2, 1, 1)
        dp_ref[...] = _dot(do, jnp.concatenate([vp_ref[...], vc_ref[...]], axis=0), 1, 1)

        def head(h, carry):
            rows, probs, p_sink = _head_softmax(s_ref, bias_ref, sink_ref, h)
            dp = dp_ref[rows, :]
            dsum = jnp.sum(probs * dp, axis=-1, keepdims=True)
            dlog = probs * (dp - dsum)
            dsink_ref[rows, :] -= p_sink * dsum
            dbias_ref[rows, :] += dlog
            p_ref[rows, :] = probs.astype(BF16)
            dl_ref[rows, :] = (dlog * (HEAD_DIM ** -0.5)).astype(BF16)
            return carry

        lax.fori_loop(0, N_Q_HEADS, head, 0, unroll=True)
        dlog_s = dl_ref[...]
        dq_ref[...] = jnp.where(_head_lane_mask(), _dot(dlog_s, k2, 1, 0), 0.0).reshape(N_Q_HEADS, BLOCK, 128)
        dk2 = _dot(dlog_s, q, 0, 0)
        dv2 = _dot(p_ref[...], do, 0, 0)
        prev_rows = pl.ds(pl.multiple_of(jnp.maximum(i - 1, 0) * BLOCK, BLOCK), BLOCK)
        cur_rows = pl.ds(pl.multiple_of(i * BLOCK, BLOCK), BLOCK)
        dk_ref[prev_rows, :] += dk2[:BLOCK]
        dk_ref[cur_rows, :] += dk2[BLOCK:]
        dv_ref[prev_rows, :] += dv2[:BLOCK]
        dv_ref[cur_rows, :] += dv2[BLOCK:]

    stacked, kv, consts = _attn_specs()
    band = (N_Q_HEADS * BLOCK, 2 * BLOCK)
    return pl.pallas_call(
        body, name="attn_bwd", grid=(s_len // BLOCK,),
        in_specs=[stacked] + kv + kv + [stacked] + consts + [ANY],
        out_specs=[stacked, _full((s_len, 128)), _full((s_len, 128)), _full(band), _full((N_Q_HEADS * BLOCK, 1))],
        out_shape=[jax.ShapeDtypeStruct((N_Q_HEADS, s_len, 128), F32), jax.ShapeDtypeStruct((s_len, 128), F32),
                   jax.ShapeDtypeStruct((s_len, 128), F32), jax.ShapeDtypeStruct(band, F32),
                   jax.ShapeDtypeStruct((N_Q_HEADS * BLOCK, 1), F32)],
        scratch_shapes=[pltpu.VMEM(band, F32), pltpu.VMEM(band, F32), pltpu.VMEM(band, BF16), pltpu.VMEM(band, BF16)],
        compiler_params=_params(),
    )(qst, kn, kn, vb, vb, dost, bias_st, sinks, after)


def _small_pack(dg_attn, dg_ffn, dg_ple, dscale, dgq, dgk, dbias, dsink_rows, bucket, loss_v, dwpool):
    def body(ga_ref, gf_ref, gp_ref, sc_ref, gq_ref, gk_ref, db_ref, ds_ref, bucket_ref, loss_ref, wp_ref, out_ref):
        out_ref[pl.ds(0, SMALL["w_pool"]), :] = jnp.zeros((SMALL["w_pool"], 128), F32)
        for name, ref, n in (("g_attn", ga_ref, 8), ("g_ffn", gf_ref, 8), ("g_ple", gp_ref, 8), ("pool_scale", sc_ref, 4)):
            for k in range(n):
                out_ref[pl.ds(SMALL[name] + k, 1), :] = ref[:, 128 * k:128 * k + 128]
        for name, ref in (("g_q", gq_ref), ("g_k", gk_ref)):
            both = ref[...]
            out_ref[pl.ds(SMALL[name], 1), :] = both + pltpu.roll(both, 64, axis=1)
        out_ref[pl.ds(SMALL["loss"], 1), :] = loss_ref[...]
        bk = bucket_ref[...]
        rows = lax.broadcasted_iota(jnp.int32, (N_BUCKETS, 128), 0)
        lanes = lax.broadcasted_iota(jnp.int32, (N_BUCKETS, 128), 1)
        lane1 = lax.broadcasted_iota(jnp.int32, (1, 128), 1)
        rb = jnp.zeros((N_BUCKETS, 128), F32)
        sk = jnp.zeros((1, 128), F32)
        for h in range(N_Q_HEADS):
            band = db_ref[pl.ds(h * BLOCK, BLOCK), :]
            for b in range(N_BUCKETS):
                rb = jnp.where((rows == b) & (lanes == h), jnp.sum(jnp.where(bk == b, band, 0.0)), rb)
            sk = jnp.where(lane1 == h, jnp.sum(ds_ref[pl.ds(h * BLOCK, BLOCK), :]), sk)
        out_ref[pl.ds(SMALL["rel_bias"], N_BUCKETS), :] = rb
        out_ref[pl.ds(SMALL["sinks"], 1), :] = sk
        out_ref[pl.ds(SMALL["w_pool"], 512), :] = wp_ref[...].reshape(512, 128)

    return pl.pallas_call(
        body, name="small_pack", in_specs=[VMEM_WHOLE] * 11, out_specs=VMEM_WHOLE,
        out_shape=jax.ShapeDtypeStruct((SMALL_ROWS, 128), F32),
    )(dg_attn, dg_ffn, dg_ple, dscale, dgq, dgk, dbias, dsink_rows, bucket, loss_v, dwpool)


def _attn_in_bwd(dqst, zqk, dk, dv, du, x2, dh1, wts, g_attn, gq, gk):
    s_len = x2.shape[0]
    t = 512

    def body(dq_ref, zqk_ref, dk_ref, dv_ref, du_ref, x_ref, dh1_ref, sl_ref, lo_ref, me_ref, g_ref, gq_ref, gk_ref,
             dz_ref, dx_ref, dg_ref, dgq_ref, dgk_ref, w_ref, sems):
        @pl.when(pl.program_id(0) == 0)
        def _():
            _load_rows((sl_ref, lo_ref, me_ref), "inT", w_ref, sems)
            dg_ref[...] = jnp.zeros_like(dg_ref)
            dgq_ref[...] = jnp.zeros_like(dgq_ref)
            dgk_ref[...] = jnp.zeros_like(dgk_ref)

        lo = lax.broadcasted_iota(jnp.int32, (t, 128), 1) < 64
        for p in range(4):
            dqn = _from_stacked(dq_ref[2 * p], dq_ref[2 * p + 1], p // 2, lo)
            dq_raw, dgq = _pair_norm_bwd(zqk_ref[:, 128 * p:128 * p + 128], gq_ref[...], dqn)
            dz_ref[:, 128 * p:128 * p + 128] = dq_raw.astype(BF16)
            dgq_ref[...] += dgq
        dk_raw, dgk = _pair_norm_bwd(zqk_ref[:, 512:640], gk_ref[...], dk_ref[...])
        dgk_ref[...] += dgk
        dz_ref[:, 512:640] = dk_raw.astype(BF16)
        dz_ref[:, 640:768] = dv_ref[...].astype(BF16)
        dz_ref[:, 768:] = du_ref[...].astype(BF16)
        dx, dg = _rms_bwd(x_ref[...], g_ref[...], _dot(dz_ref[...], w_ref[...], 1, 0))
        dx_ref[...] = dh1_ref[...] + dx
        dg_ref[...] += dg

    row = lambda w: pl.BlockSpec((t, w), lambda i: (i, 0))
    return pl.pallas_call(
        body, name="attn_in_bwd", grid=(s_len // t,),
        in_specs=[pl.BlockSpec((N_Q_HEADS, t, 128), lambda i: (0, i, 0)), row(640), row(128), row(128), row(POOL_WIDTH),
                  row(D_MODEL), row(D_MODEL)] + W_SPECS + [_full((1, D_MODEL)), _full((1, 128)), _full((1, 128))],
        out_specs=[row(IN_WIDTH), row(D_MODEL), _full((1, D_MODEL)), _full((1, 128)), _full((1, 128))],
        out_shape=[jax.ShapeDtypeStruct((s_len, IN_WIDTH), BF16), jax.ShapeDtypeStruct((s_len, D_MODEL), F32),
                   jax.ShapeDtypeStruct((1, D_MODEL), F32), jax.ShapeDtypeStruct((1, 128), F32),
                   jax.ShapeDtypeStruct((1, 128), F32)],
        scratch_shapes=[pltpu.VMEM((IN_WIDTH, D_MODEL), BF16), pltpu.SemaphoreType.DMA((N_CHIPS,))],
        compiler_params=_params(),
    )(dqst, zqk, dk, dv, du, x2, dh1, *wts, g_attn, gq, gk)


def _dw(a, b, name, into=None):
    tk = 1024
    n_out = b.shape[1]
    if a.ndim == 3:
        s_len, tm = a.shape[1:]
        m = N_CHIPS * tm
        a_spec = pl.BlockSpec((None, tk, tm), lambda i, k: (i, k, 0))
    else:
        s_len, m = a.shape
        tm = m // 2 if m > 1408 else m
        a_spec = pl.BlockSpec((tk, tm), lambda i, k: (k, i))
    n_steps = s_len // tk
    chunk = m // N_CHIPS
    per_tile = tm // chunk

    def accumulate(a_ref, b_ref, acc_ref, k):
        @pl.when(k == 0)
        def _():
            acc_ref[...] = _dot(a_ref[...].astype(BF16), b_ref[...].astype(BF16), 0, 0)

        @pl.when(k > 0)
        def _():
            acc_ref[...] += _dot(a_ref[...].astype(BF16), b_ref[...].astype(BF16), 0, 0)

    in_specs = [a_spec, pl.BlockSpec((tk, n_out), lambda i, k: (k, 0))]
    if into is None:
        def body(a_ref, b_ref, o_ref, acc_ref):
            k = pl.program_id(1)
            accumulate(a_ref, b_ref, acc_ref, k)

            @pl.when(k == n_steps - 1)
            def _():
                o_ref[...] = acc_ref[...].astype(BF16)

        return pl.pallas_call(
            body, name=name, grid=(m // tm, n_steps), in_specs=in_specs,
            out_specs=pl.BlockSpec((tm, n_out), lambda i, k: (i, 0)), out_shape=jax.ShapeDtypeStruct((m, n_out), BF16),
            scratch_shapes=[pltpu.VMEM((tm, n_out), F32)], compiler_params=_params(n_axes=2),
        )(a, b)

    slab, slab_rows, row_off = into
    assert n_out == D_MODEL

    n_tiles = m // tm

    def body_into(a_ref, b_ref, *rest):
        o_ref, acc_ref, stage_ref, sems = rest[-4:]
        i, k = pl.program_id(0), pl.program_id(1)
        accumulate(a_ref, b_ref, acc_ref, k)

        def out_copies(tile, slot):
            return [pltpu.make_async_copy(stage_ref.at[slot, pl.ds(jj * chunk, chunk), :],
                                          o_ref.at[tile * per_tile + jj, pl.ds(row_off, chunk), :], sems.at[slot, jj])
                    for jj in range(per_tile)]

        @pl.when(k == n_steps - 1)
        def _():
            slot = i % 2

            @pl.when(i >= 2)
            def _():
                for cp in out_copies(i - 2, slot):
                    cp.wait()

            stage_ref[slot] = acc_ref[...].astype(BF16)
            for cp in out_copies(i, slot):
                cp.start()

            @pl.when(i == n_tiles - 1)
            def _():
                for cp in out_copies(i, slot):
                    cp.wait()
                if n_tiles > 1:
                    for cp in out_copies(i - 1, 1 - slot):
                        cp.wait()

    operands, aliases = [a, b], {}
    if slab is not None:
        in_specs = in_specs + [ANY]
        operands.append(slab)
        aliases = {2: 0}
    return pl.pallas_call(
        body_into, name=name, grid=(n_tiles, n_steps), in_specs=in_specs, out_specs=ANY,
        out_shape=jax.ShapeDtypeStruct((N_CHIPS, slab_rows, D_MODEL), BF16), input_output_aliases=aliases,
        scratch_shapes=[pltpu.VMEM((tm, n_out), F32), pltpu.VMEM((2, tm, n_out), BF16),
                        pltpu.SemaphoreType.DMA((2, per_tile))],
        compiler_params=_params(n_axes=2),
    )(*operands)


def _dw_pool(pooled, dyp):
    s_len = pooled.shape[0]
    tk = 512

    def body(a_ref, b_ref, o_ref):
        @pl.when(pl.program_id(0) == 0)
        def _():
            o_ref[...] = jnp.zeros_like(o_ref)

        for g in range(4):
            cols = slice(128 * g, 128 * g + 128)
            o_ref[g] += _dot(a_ref[:, cols], b_ref[:, cols], 0, 0)

    blk = pl.BlockSpec((tk, POOL_WIDTH), lambda k: (k, 0))
    return pl.pallas_call(
        body, name="dw_pool", grid=(s_len // tk,), in_specs=[blk, blk], out_specs=_full((4, 128, 128)),
        out_shape=jax.ShapeDtypeStruct((4, 128, 128), F32), compiler_params=_params(),
    )(pooled, dyp)


def _position():
    x, y, c = lax.axis_index("x"), lax.axis_index("y"), lax.axis_index("c")
    other_chips = [(1 - x, y), (x, 1 - y), (1 - x, 1 - y)]
    return x, y, c, other_chips


def _ag_weights(local_slab, row0, n_rows, name, collective_id):
    half = n_rows // 2
    quarter = half // 2
    assert quarter % 16 == 0

    def body(l_ref, g_ref, send, recv):
        x, y, c, chips = _position()
        me, (via_x, via_y, diagonal) = 2 * x + y, [2 * chip[0] + chip[1] for chip in chips]
        here, sibling, x_nbr, y_nbr = (x, y, c), (x, y, 1 - c), (1 - x, y, c), (x, 1 - y, c)
        peers = [sibling, x_nbr, y_nbr]
        barrier = pltpu.get_barrier_semaphore()
        for peer in peers:
            pl.semaphore_signal(barrier, inc=1, device_id=peer, device_id_type=MESH)
        pl.semaphore_wait(barrier, len(peers))

        def rows(core, part):
            start, size = (core * half, half) if part is None else (core * half + part * quarter, quarter)
            return pl.ds(pl.multiple_of(start, 16), size)

        def copy(k, chip_idx, where, to, src=None):
            dst = g_ref.at[chip_idx, where, :]
            return pltpu.make_async_remote_copy(src_ref=dst if src is None else src, dst_ref=dst, send_sem=send.at[k],
                                                recv_sem=recv.at[k], device_id=to, device_id_type=MESH)

        own_rows = l_ref.at[pl.ds(pl.multiple_of(row0 + c * half, 16), half), :]
        started = [copy(0, me, rows(c, None), x_nbr, src=own_rows), copy(1, me, rows(c, None), y_nbr, src=own_rows)]
        for cp in started:
            cp.start()
        after_arrival = [
            (copy(0, via_x, rows(c, None), here), [copy(4, via_x, rows(c, None), sibling), copy(3, via_x, rows(c, 1), y_nbr)]),
            (copy(1, via_y, rows(c, None), here), [copy(5, via_y, rows(c, None), sibling), copy(2, via_y, rows(c, 0), x_nbr)]),
            (copy(2, diagonal, rows(c, 0), here), [copy(6, diagonal, rows(c, 0), sibling)]),
            (copy(3, diagonal, rows(c, 1), here), [copy(7, diagonal, rows(c, 1), sibling)]),
        ]
        for arrival, onward in after_arrival:
            arrival.wait_recv()
            for cp in onward:
                cp.start()
            started += onward
        for cp in (copy(4, via_x, rows(1 - c, None), here), copy(5, via_y, rows(1 - c, None), here),
                   copy(6, diagonal, rows(1 - c, 0), here), copy(7, diagonal, rows(1 - c, 1), here)):
            cp.wait_recv()
        for cp in started:
            cp.wait_send()

    return pl.kernel(
        body, out_type=jax.ShapeDtypeStruct((N_CHIPS, n_rows, D_MODEL), BF16),
        mesh=plsc.ScalarSubcoreMesh(axis_name="sequencer", num_cores=1), name=name,
        scratch_types=[pltpu.SemaphoreType.DMA((8,)), pltpu.SemaphoreType.DMA((8,))],
        compiler_params=pltpu.CompilerParams(collective_id=collective_id),
    )(local_slab)


def _comm_call(body, peers_of, out_shape, n_sems, operand, name, collective_id):
    sems = [pltpu.SemaphoreType.DMA((n_sems,)), pltpu.SemaphoreType.DMA((n_sems,))]
    if collective_id is None:
        return pl.pallas_call(body, name=name, in_specs=[ANY], out_specs=ANY, out_shape=out_shape, scratch_shapes=sems)(operand)

    def with_handshake(in_ref, out_ref, send, recv):
        x, y, c, _ = _position()
        peers = peers_of(x, y, c)
        barrier = pltpu.get_barrier_semaphore()
        for peer in peers:
            pl.semaphore_signal(barrier, inc=1, device_id=peer, device_id_type=MESH)
        pl.semaphore_wait(barrier, len(peers))
        body(in_ref, out_ref, send, recv)

    return pl.kernel(with_handshake, out_type=out_shape, mesh=plsc.ScalarSubcoreMesh(axis_name="sequencer", num_cores=1),
                     name=name, scratch_types=sems, compiler_params=pltpu.CompilerParams(collective_id=collective_id))(operand)


def _rs_swap_halves(partial, name, collective_id=None):
    half = partial.shape[1] // 2

    def body(p_ref, r_ref, send, recv):
        x, y, c, _ = _position()
        theirs = pl.ds(pl.multiple_of((1 - c) * half, 16), half)
        cp = pltpu.make_async_remote_copy(src_ref=p_ref.at[:, theirs, :], dst_ref=r_ref, send_sem=send.at[0],
                                          recv_sem=recv.at[0], device_id=(x, y, 1 - c), device_id_type=MESH)
        cp.start()
        cp.wait()

    return _comm_call(body, lambda x, y, c: [(x, y, 1 - c)], jax.ShapeDtypeStruct((N_CHIPS, half, D_MODEL), BF16), 1,
                      partial, name, collective_id)


def _rs_add_halves(partial, other, core, name, after):
    half = other.shape[1]
    t = half // 2
    steps = half // t

    def body(core_ref, a_ref, b_ref, after_ref, o_ref):
        del after_ref
        o_ref[...] = (a_ref[...].astype(F32) + b_ref[...].astype(F32)).astype(BF16)

    return pl.pallas_call(
        body, name=name,
        grid_spec=pltpu.PrefetchScalarGridSpec(
            num_scalar_prefetch=1, grid=(N_CHIPS, steps),
            in_specs=[pl.BlockSpec((1, t, D_MODEL), lambda j, i, core_ref: (j, core_ref[0] * steps + i, 0)),
                      pl.BlockSpec((1, t, D_MODEL), lambda j, i, core_ref: (j, i, 0)), ANY],
            out_specs=pl.BlockSpec((1, t, D_MODEL), lambda j, i, core_ref: (j, i, 0))),
        out_shape=jax.ShapeDtypeStruct((N_CHIPS, half, D_MODEL), BF16),
        compiler_params=_params(n_axes=2),
    )(core, partial, other, after)


def _rs_exchange_chips(pre, name, collective_id=None):
    def body(s_ref, r_ref, send, recv):
        x, y, c, chips = _position()

        def copy(k, chunk, to):
            return pltpu.make_async_remote_copy(src_ref=s_ref.at[chunk], dst_ref=r_ref.at[k], send_sem=send.at[k],
                                                recv_sem=recv.at[k], device_id=to, device_id_type=MESH)

        sends = [copy(k, 2 * chip[0] + chip[1], (*chip, c)) for k, chip in enumerate(chips)]
        for cp in sends:
            cp.start()
        for cp in sends:
            cp.wait()

    return _comm_call(body, lambda x, y, c: [(1 - x, y, c), (x, 1 - y, c), (1 - x, 1 - y, c)],
                      jax.ShapeDtypeStruct((3, pre.shape[1], D_MODEL), BF16), 3, pre, name, collective_id)


def _rs_sum_chips(pre, received, place, name, after):
    half = pre.shape[1]
    t = half // 2 if half > 512 else half
    steps = half // t

    def body(place_ref, own_ref, r_ref, after_ref, o_ref):
        del after_ref
        acc = own_ref[0].astype(F32)
        for k in range(3):
            acc = acc + r_ref[k].astype(F32)
        o_ref[...] = acc

    return pl.pallas_call(
        body, name=name,
        grid_spec=pltpu.PrefetchScalarGridSpec(
            num_scalar_prefetch=1, grid=(steps,),
            in_specs=[pl.BlockSpec((1, t, D_MODEL), lambda i, place_ref: (place_ref[0], i, 0)),
                      pl.BlockSpec((3, t, D_MODEL), lambda i, place_ref: (0, i, 0)), ANY],
            out_specs=pl.BlockSpec((t, D_MODEL), lambda i, place_ref: (place_ref[1] * steps + i, 0))),
        out_shape=jax.ShapeDtypeStruct((2 * half, D_MODEL), F32),
        compiler_params=_params(),
    )(place, pre, received, after)


def _half_swap(g_ref, core, to, send, recv, k):
    half = g_ref.shape[0] // 2
    rows = g_ref.at[pl.ds(pl.multiple_of(core * half, 8), half), :]
    return pltpu.make_async_remote_copy(src_ref=rows, dst_ref=rows, send_sem=send.at[k], recv_sem=recv.at[k],
                                        device_id=to, device_id_type=MESH)


def _rs_finish_rows(grads, name, after):
    def body(f_ref, after_ref, g_ref, send, recv):
        del f_ref, after_ref
        x, y, c, _ = _position()
        mine = _half_swap(g_ref, c, (x, y, 1 - c), send, recv, 0)
        mine.start()
        _half_swap(g_ref, 1 - c, (x, y, c), send, recv, 0).wait_recv()
        mine.wait_send()

    return pl.pallas_call(
        body, name=name, in_specs=[ANY, ANY], out_specs=ANY, input_output_aliases={0: 0},
        out_shape=jax.ShapeDtypeStruct(grads.shape, F32),
        scratch_shapes=[pltpu.SemaphoreType.DMA((1,)), pltpu.SemaphoreType.DMA((1,))],
    )(grads, after)


def _small_gather(small, collective_id):
    def body(s_ref, t_ref, send, recv):
        x, y, c, chips = _position()
        sibling = (x, y, 1 - c)

        def slot(px, py, pc):
            return t_ref.at[4 * px + 2 * py + pc]

        def copy(k, block, to, src=None):
            return pltpu.make_async_remote_copy(src_ref=slot(*block) if src is None else src, dst_ref=slot(*block),
                                                send_sem=send.at[k], recv_sem=recv.at[k], device_id=to, device_id_type=MESH)

        own = pltpu.make_async_copy(s_ref, slot(x, y, c), send.at[7])
        own.start()
        first = [copy(0, (x, y, c), sibling, src=s_ref)]
        first += [copy(1 + k, (x, y, c), (*chip, c), src=s_ref) for k, chip in enumerate(chips)]
        for cp in first:
            cp.start()
        passed = []
        for k, chip in enumerate(chips):
            copy(1 + k, (*chip, c), (x, y, c)).wait_recv()
            fwd = copy(4 + k, (*chip, c), sibling)
            fwd.start()
            passed.append(fwd)
        copy(0, sibling, (x, y, c)).wait_recv()
        for k, chip in enumerate(chips):
            copy(4 + k, (*chip, 1 - c), (x, y, c)).wait_recv()
        for cp in first + passed:
            cp.wait_send()
        own.wait()

    peers_of = lambda x, y, c: [(x, y, 1 - c), (1 - x, y, c), (x, 1 - y, c), (1 - x, 1 - y, c)]
    return _comm_call(body, peers_of, jax.ShapeDtypeStruct((N_DEV, SMALL_ROWS, 128), F32), 8, small, "small_gather",
                      collective_id)


def _adam_update(w, g, m, v):
    m_new = ADAM_B1 * m + (1.0 - ADAM_B1) * g
    v_new = ADAM_B2 * v + (1.0 - ADAM_B2) * (g * g)
    m_hat = m_new / (1.0 - ADAM_B1 ** ADAM_STEP)
    v_hat = v_new / (1.0 - ADAM_B2 ** ADAM_STEP)
    return -ADAM_LR * (m_hat / (jnp.sqrt(v_hat) + ADAM_EPS) + ADAM_WD * w), m_new, v_new


def _adamw(w, g_rows, row_off, m, v, name):
    rows, cols = w.shape
    t = rows if rows <= 320 else (rows // 2 if rows % 256 else 256)

    def body(w_ref, g_ref, m_ref, v_ref, go_ref, d_ref, nm_ref, nv_ref):
        g = g_ref[...]
        go_ref[...] = g
        d_ref[...], nm_ref[...], nv_ref[...] = _adam_update(w_ref[...], g, m_ref[...], v_ref[...])

    blk = pl.BlockSpec((t, cols), lambda i: (i, 0))
    assert row_off % 8 == 0 and t % 8 == 0
    g_blk = pl.BlockSpec((pl.Element(t), pl.Element(cols)), lambda i: (pl.multiple_of(row_off + i * t, 8), 0))
    shape = jax.ShapeDtypeStruct((rows, cols), F32)
    return pl.pallas_call(
        body, name=name, grid=(rows // t,), in_specs=[blk, g_blk, blk, blk], out_specs=[blk] * 4, out_shape=[shape] * 4,
        compiler_params=_params(),
    )(w, g_rows, m, v)


SMALL_PARAMS = [("g_attn", (1, D_MODEL), 8), ("g_q", (1, HEAD_DIM), None), ("g_k", (1, HEAD_DIM), None),
                ("sinks", (1, N_Q_HEADS), None), ("rel_bias", (N_BUCKETS, N_Q_HEADS), None), ("w_pool", (512, 128), None),
                ("pool_scale", (1, POOL_WIDTH), 4), ("g_ffn", (1, D_MODEL), 8), ("g_ple", (1, D_MODEL), 8)]


def _adamw_small(tables, wmv):
    n_par = len(SMALL_PARAMS)

    def body(*refs):
        t_ref = refs[0]
        ins = refs[1:1 + 3 * n_par]
        loss_ref = refs[1 + 3 * n_par]
        outs = refs[2 + 3 * n_par:-1]
        tot_ref = refs[-1]
        total = t_ref[0]
        for d in range(1, N_DEV):
            total = total + t_ref[d]
        tot_ref[...] = total
        loss_ref[...] = tot_ref[pl.ds(SMALL["loss"], 1), 0:1]
        for i, (name, shape, split) in enumerate(SMALL_PARAMS):
            g_ref, d_ref, nm_ref, nv_ref = outs[4 * i:4 * i + 4]
            row = SMALL[name]
            if split:
                for k in range(split):
                    g_ref[:, 128 * k:128 * k + 128] = tot_ref[pl.ds(row + k, 1), :]
            else:
                g_ref[...] = tot_ref[pl.ds(row, shape[0]), 0:shape[1]]
            w_ref, m_ref, v_ref = ins[3 * i:3 * i + 3]
            d_ref[...], nm_ref[...], nv_ref[...] = _adam_update(w_ref[...], g_ref[...], m_ref[...], v_ref[...])

    shapes = [jax.ShapeDtypeStruct((1, 1), F32)]
    for _, shape, _ in SMALL_PARAMS:
        shapes += [jax.ShapeDtypeStruct(shape, F32)] * 4
    flat = [a for triple in wmv for a in triple]
    res = pl.pallas_call(
        body, name="adamw_small", in_specs=[VMEM_WHOLE] * (1 + 3 * n_par), out_specs=[VMEM_WHOLE] * len(shapes),
        out_shape=shapes, scratch_shapes=[pltpu.VMEM((SMALL_ROWS, 128), F32)],
    )(tables, *flat)
    return res[0], [res[1 + 4 * i:5 + 4 * i] for i in range(n_par)]


def _pack_ple_proj(shard):
    return shard.reshape(4, 64, 256).transpose(1, 0, 2).reshape(64, D_MODEL)


class _Reduction:
    def __init__(self, tag, place, ids=(None, None)):
        self.tag, self.place, self.ids = tag, place, ids

    def start(self, partial):
        self.partial = partial
        self.other = _rs_swap_halves(partial, "rs_swap_" + self.tag, self.ids[0])
        return partial

    def middle(self, after):
        self.pre = _rs_add_halves(self.partial, self.other, self.place[1:], "rs_add_" + self.tag, after)
        self.received = _rs_exchange_chips(self.pre, "rs_exchange_" + self.tag, self.ids[1])
        return self.pre

    def finish(self, after):
        return _rs_sum_chips(self.pre, self.received, self.place, "rs_sum_" + self.tag, after)


def _local_grads(x2, p2, tgt, wts, g_attn_norm, g_q, g_k, attn_sinks, rel_bias, w_pool, pool_scale, g_ffn_norm, g_ple_norm,
                 reduce_a):
    w_early, w_late = wts
    w_in = w_out = w_early
    bucket = jnp.asarray(_bucket_table())
    gq = jnp.tile(g_q, (1, 2))
    gk = jnp.tile(g_k, (1, 2))
    wpool = w_pool[0].astype(BF16)
    sinks = attn_sinks[0]
    bias_st = _bias_build(rel_bias.T, bucket)

    hn1, zqk, u, kn, vb, qst = _attn_in(x2, g_attn_norm, gq, gk, w_in)
    ost = _attn_fwd(qst, kn, vb, bias_st, sinks)
    pooled, mix, h1, hn2 = _mix_out(u, ost, x2, w_out, wpool, pool_scale, g_ffn_norm)
    loss_v, dgate, dup, act, dh2, hn3, dgl, dpp, dh1, dg_ffn, dg_ple = _ffn_ple(hn2, h1, p2, tgt, w_late, g_ffn_norm,
                                                                                   g_ple_norm)

    rows_a = SLAB_ROWS - SLAB["inT"][1]
    partial_a = None
    for name, lhs, rhs in (("out", mix, dh1), ("gateT", dgate, hn2), ("upT", dup, hn2), ("down", act, dh2), ("plg", hn3, dgl)):
        partial_a = _dw(lhs, rhs, "dw_" + name, into=(partial_a, rows_a, SLAB[name][0] - SLAB["inT"][1]))
    dw_plp = _dw(p2, dpp, "dw_plp").reshape(4, 64, N_CHIPS, 256).transpose(2, 1, 0, 3).reshape(N_CHIPS, 64, D_MODEL)
    partial_a = reduce_a.start(lax.dynamic_update_slice(partial_a, dw_plp, (0, SLAB["plp"][0] - SLAB["inT"][1], 0)))
    dost, du, dyp, dscale = _mix_out_bwd(dh1, w_out, pooled, wpool, pool_scale, partial_a)
    pre_a = reduce_a.middle(du)
    dqst, dk, dv, dbias, dsink_rows = _attn_bwd(qst, kn, vb, dost, bias_st, sinks, pre_a)
    dz, dx, dg_attn, dgq, dgk = _attn_in_bwd(dqst, zqk, dk, dv, du, x2, dh1, w_in, g_attn_norm, gq, gk)

    partial_b = _dw(dz, hn1, "dw_in").reshape(N_CHIPS, -1, D_MODEL)
    small = _small_pack(dg_attn, dg_ffn, dg_ple, dscale, dgq, dgk, dbias, dsink_rows, bucket, loss_v, _dw_pool(pooled, dyp))
    return dx, partial_b, small


def kernel(x, p, w_in, w_out, g_attn_norm, g_q, g_k, attn_sinks, rel_bias, w_pool, pool_scale, g_ffn_norm, w_gate, w_up, w_down, g_ple_norm, w_ple_gate, w_ple_proj, loss_target, m_w_in, m_w_out, m_g_attn_norm, m_g_q, m_g_k, m_attn_sinks, m_rel_bias, m_w_pool, m_pool_scale, m_g_ffn_norm, m_w_gate, m_w_up, m_w_down, m_g_ple_norm, m_w_ple_gate, m_w_ple_proj, v_w_in, v_w_out, v_g_attn_norm, v_g_q, v_g_k, v_attn_sinks, v_rel_bias, v_w_pool, v_pool_scale, v_g_ffn_norm, v_w_gate, v_w_up, v_w_down, v_g_ple_norm, v_w_ple_gate, v_w_ple_proj):
    core = lax.axis_index("c").astype(jnp.int32).reshape(1)
    me = (2 * lax.axis_index("x") + lax.axis_index("y")).astype(jnp.int32).reshape(1)

    local_parts = [jnp.concatenate(pieces, axis=0).astype(BF16) for pieces in (
        [w_in[0].T, w_out[0]], [w_gate[0].T, w_up[0].T, w_down[0], w_ple_gate[0], _pack_ple_proj(w_ple_proj[0])])]
    wts = [(_ag_weights(local, 0, local.shape[0], name, collective_id), local, me)
           for local, name, collective_id in zip(local_parts, ("ag_early", "ag_late"), (1, 2))]

    place = jnp.concatenate([me, core])
    reduce_a = _Reduction("a", place, ids=(3, 4))
    dx, partial_b, small = _local_grads(x[0], p[0, 0], loss_target[0], wts, g_attn_norm, g_q, g_k, attn_sinks, rel_bias,
                                        w_pool, pool_scale, g_ffn_norm, g_ple_norm, reduce_a)
    reduce_b = _Reduction("b", place, ids=(6, 7))
    reduce_b.start(partial_b)
    small_all = _small_gather(small, 8)
    summed_a = reduce_a.finish(small)
    pre_b = reduce_b.middle(summed_a)
    grads_a = _rs_finish_rows(summed_a, "rs_finish_a", pre_b)

    def rows(name):
        return grads_a, SLAB[name][0] - SLAB["inT"][1]

    plp_rows = grads_a[SLAB["plp"][0] - SLAB["inT"][1]:]
    big = {
        "w_out": (w_out, m_w_out, v_w_out, rows("out"), False),
        "w_gate": (w_gate, m_w_gate, v_w_gate, rows("gateT"), True),
        "w_up": (w_up, m_w_up, v_w_up, rows("upT"), True),
        "w_down": (w_down, m_w_down, v_w_down, rows("down"), False),
        "w_ple_gate": (w_ple_gate, m_w_ple_gate, v_w_ple_gate, rows("plg"), False),
        "w_ple_proj": (w_ple_proj, m_w_ple_proj, v_w_ple_proj,
                       (plp_rows.reshape(64, 4, 256).transpose(1, 0, 2).reshape(PLE_DIM, PLE_DIM), 0), False),
        "w_in": (w_in, m_w_in, v_w_in, None, True),
    }
    small_params = {
        "g_attn_norm": (g_attn_norm, m_g_attn_norm, v_g_attn_norm), "g_q": (g_q, m_g_q, v_g_q), "g_k": (g_k, m_g_k, v_g_k),
        "attn_sinks": (attn_sinks, m_attn_sinks, v_attn_sinks), "rel_bias": (rel_bias, m_rel_bias, v_rel_bias),
        "w_pool": tuple(a.reshape(512, 128) for a in (w_pool, m_w_pool, v_w_pool)),
        "pool_scale": (pool_scale, m_pool_scale, v_pool_scale), "g_ffn_norm": (g_ffn_norm, m_g_ffn_norm, v_g_ffn_norm),
        "g_ple_norm": (g_ple_norm, m_g_ple_norm, v_g_ple_norm),
    }

    grads, deltas, new_ms, new_vs = {}, {}, {}, {}
    out = None
    for name, (w, m, v, g_src, transposed) in big.items():
        if g_src is None:
            g_src = (_rs_finish_rows(reduce_b.finish(out[-1]), "rs_finish_b", out[-1]), 0)
        view = (lambda a: a.T) if transposed else (lambda a: a)
        out = _adamw(view(w[0]), *g_src, view(m[0]), view(v[0]), "adamw_" + name)
        grads[name], deltas[name], new_ms[name], new_vs[name] = (view(a)[None] for a in out)

    loss, small_out = _adamw_small(small_all, list(small_params.values()))
    for name, (g2, d, nm, nv) in zip(small_params, small_out):
        shape = w_pool.shape if name == "w_pool" else g2.shape
        grads[name], deltas[name], new_ms[name], new_vs[name] = (a.reshape(shape) for a in (g2, d, nm, nv))

    order = ["w_in", "w_out", "g_attn_norm", "g_q", "g_k", "attn_sinks", "rel_bias", "w_pool", "pool_scale", "g_ffn_norm",
             "w_gate", "w_up", "w_down", "g_ple_norm", "w_ple_gate", "w_ple_proj"]
    return (loss.reshape(()), dx[None], *[grads[n] for n in order], *[deltas[n] for n in order],
            *[new_ms[n] for n in order], *[new_vs[n] for n in order])
```

```python
import functools

import numpy as np
import jax
import jax.numpy as jnp
from jax import lax
from jax.experimental import pallas as pl
from jax.experimental.pallas import tpu as pltpu
from jax.experimental.pallas import tpu_sc as plsc

F32 = jnp.float32
BF16 = jnp.bfloat16
MESH = pl.DeviceIdType.MESH

D_MODEL = 1024
HEAD_DIM = 64
N_Q_HEADS = 8
ATTN_WIDTH = 512
KV_WIDTH = 128
POOL_WIDTH = 512
IN_WIDTH = 1280
D_FF = 2816
PLE_DIM = 256
FF_CHUNK = 704
BLOCK = 128
N_BUCKETS = 32
MAX_DISTANCE = 128
POOL_SIZES = (2, 4, 8, 16)
EPS = 1e-6
NEG = -1e30
N_CHIPS = 4
N_DEV = 8

ADAM_LR = 0.001
ADAM_B1 = 0.9
ADAM_B2 = 0.999
ADAM_EPS = 1e-08
ADAM_WD = 0.01
ADAM_STEP = 10

SLAB = {"inT": (0, 320), "out": (320, 256), "gateT": (576, 704), "upT": (1280, 704), "down": (1984, 704),
        "plg": (2688, 256), "plp": (2944, 64)}
SLAB_ROWS = 3008
HALF_ROWS = SLAB_ROWS // 2
GATHER_PARTS = ((0, 576), (576, SLAB_ROWS))
POOL_HALO = 24

SMALL = {"g_attn": 0, "g_ffn": 8, "g_ple": 16, "pool_scale": 24, "g_q": 28, "g_k": 29, "sinks": 30, "loss": 31,
         "rel_bias": 32, "w_pool": 64}
SMALL_ROWS = 576

VMEM_LIMIT_BIG = 60 * 1024 * 1024
VMEM_LIMIT = 48 * 1024 * 1024


def _params(vmem=VMEM_LIMIT, n_axes=1):
    return pltpu.CompilerParams(dimension_semantics=("arbitrary",) * n_axes, vmem_limit_bytes=vmem)


def _dot(a, b, ca, cb):
    return lax.dot_general(a, b, (((ca,), (cb,)), ((), ())), preferred_element_type=F32)


def _full(shape):
    return pl.BlockSpec(shape, lambda i: (0,) * len(shape))


ANY = pl.BlockSpec(memory_space=pl.ANY)
VMEM_WHOLE = pl.BlockSpec(memory_space=pltpu.VMEM)


def _hbm(shape, dtype):
    return pltpu.HBM(tuple(shape), dtype)


W_SPECS = [ANY, ANY, pl.BlockSpec(memory_space=pltpu.SMEM)]


def _load_rows(w_refs, name, dst_ref, sems):
    slab_ref, local_ref, me_ref = w_refs
    off, rows = SLAB[name]
    slab_off = off - max(start for start, _ in GATHER_PARTS if start <= off)
    me = me_ref[0]
    for phase in ("start", "wait"):
        for j in range(N_CHIPS):
            dst = dst_ref.at[pl.ds(j * rows, rows), :]
            theirs = pltpu.make_async_copy(slab_ref.at[j, pl.ds(slab_off, rows), :], dst, sems.at[j])
            own = pltpu.make_async_copy(local_ref.at[pl.ds(slab_off, rows), :], dst, sems.at[j])

            @pl.when(me == j)
            def _():
                getattr(own, phase)()

            @pl.when(me != j)
            def _():
                getattr(theirs, phase)()


def _rms_fwd(x, g):
    r = lax.rsqrt(jnp.mean(x * x, axis=-1, keepdims=True) + EPS)
    return x * r * g


def _rms_bwd(x, g, dy):
    r = lax.rsqrt(jnp.mean(x * x, axis=-1, keepdims=True) + EPS)
    xn = x * r
    dyg = dy * g
    dx = r * (dyg - xn * jnp.mean(dyg * xn, axis=-1, keepdims=True))
    return dx, jnp.sum(dy * xn, axis=0, keepdims=True)


def _half_sum(v, lo):
    s_lo = jnp.sum(jnp.where(lo, v, 0.0), axis=-1, keepdims=True)
    s_hi = jnp.sum(jnp.where(lo, 0.0, v), axis=-1, keepdims=True)
    return jnp.where(lo, s_lo, s_hi)


def _half_sum_mxu(v):
    upper = lax.broadcasted_iota(jnp.int32, (128, 128), 0) < 64
    left = lax.broadcasted_iota(jnp.int32, (128, 128), 1) < 64
    ones = jnp.where(upper == left, 1.0, 0.0).astype(BF16)
    high = v.astype(BF16)
    low = (v - high.astype(F32)).astype(BF16)
    return _dot(high, ones, 1, 0) + _dot(low, ones, 1, 0)


def _pair_norm(zp, g, lo):
    r = lax.rsqrt(_half_sum(zp * zp, lo) * (1.0 / HEAD_DIM) + EPS)
    return zp * r * g


def _pair_norm_bwd(zp, g, dy):
    r = lax.rsqrt(_half_sum_mxu(zp * zp) * (1.0 / HEAD_DIM) + EPS)
    xn = zp * r
    dyg = dy * g
    dx = r * (dyg - xn * (_half_sum_mxu(dyg * xn) * (1.0 / HEAD_DIM)))
    return dx, jnp.sum(dy * xn, axis=0, keepdims=True)


def _to_stacked(pair, group, lo):
    rolled = pltpu.roll(pair, 64, axis=1)
    if group == 0:
        return jnp.where(lo, pair, 0.0), jnp.where(lo, rolled, 0.0)
    return jnp.where(lo, 0.0, rolled), jnp.where(lo, 0.0, pair)


def _from_stacked(even, odd, group, lo):
    if group == 0:
        return jnp.where(lo, even, pltpu.roll(odd, 64, axis=1))
    return jnp.where(lo, pltpu.roll(even, 64, axis=1), odd)


def _sigmoid(v):
    return 1.0 / (1.0 + jnp.exp(-v))


def _pool_counts(tile, n_rows):
    t1 = tile * n_rows + lax.broadcasted_iota(jnp.int32, (n_rows, POOL_WIDTH), 0) + 1
    lane = lax.broadcasted_iota(jnp.int32, (n_rows, POOL_WIDTH), 1)
    win = jnp.where(lane < 128, 2, jnp.where(lane < 256, 4, jnp.where(lane < 384, 8, 16)))
    return jnp.minimum(t1, win).astype(F32)


def _attn_in(x2, g_attn, gq, gk, wts):
    s_len = x2.shape[0]
    t = 512

    def body(x_ref, g_ref, gq_ref, gk_ref, sl_ref, lo_ref, me_ref, hn_ref, zqk_ref, u_ref, kn_ref, v_ref, qst_ref, w_ref, sems):
        @pl.when(pl.program_id(0) == 0)
        def _():
            _load_rows((sl_ref, lo_ref, me_ref), "inT", w_ref, sems)

        hn = _rms_fwd(x_ref[...], g_ref[...]).astype(BF16)
        hn_ref[...] = hn
        z = _dot(hn, w_ref[...], 1, 1)
        zqk_ref[...] = z[:, :640]
        u_ref[...] = z[:, 768:]
        v_ref[...] = z[:, 640:768].astype(BF16)
        lo = lax.broadcasted_iota(jnp.int32, (t, 128), 1) < 64
        kn_ref[...] = _pair_norm(z[:, 512:640], gk_ref[...], lo).astype(BF16)
        for p in range(4):
            qn = _pair_norm(z[:, 128 * p:128 * p + 128], gq_ref[...], lo)
            even, odd = _to_stacked(qn, p // 2, lo)
            qst_ref[2 * p] = even.astype(BF16)
            qst_ref[2 * p + 1] = odd.astype(BF16)

    row = lambda w: pl.BlockSpec((t, w), lambda i: (i, 0))
    return pl.pallas_call(
        body, name="attn_in", grid=(s_len // t,),
        in_specs=[row(D_MODEL), _full((1, D_MODEL)), _full((1, 128)), _full((1, 128))] + W_SPECS,
        out_specs=[row(D_MODEL), row(640), row(POOL_WIDTH), row(128), row(128),
                   pl.BlockSpec((N_Q_HEADS, t, 128), lambda i: (0, i, 0))],
        out_shape=[_hbm((s_len, D_MODEL), BF16), _hbm((s_len, 640), F32), _hbm((s_len, POOL_WIDTH), F32),
                   _hbm((s_len, 128), BF16), _hbm((s_len, 128), BF16), _hbm((N_Q_HEADS, s_len, 128), BF16)],
        scratch_shapes=[pltpu.VMEM((IN_WIDTH, D_MODEL), BF16), pltpu.SemaphoreType.DMA((N_CHIPS,))],
        compiler_params=_params(),
    )(x2, g_attn, gq, gk, *wts)


def _bucket_table():
    i_idx = np.arange(BLOCK)[:, None]
    j_idx = np.arange(2 * BLOCK)[None, :]
    d = BLOCK + i_idx - j_idx
    n = np.maximum(d, 0)
    max_exact = N_BUCKETS // 2
    nf = np.maximum(n, 1).astype(np.float64)
    large = max_exact + (np.log(nf / max_exact) / np.log(MAX_DISTANCE / max_exact) * (N_BUCKETS - max_exact)).astype(np.int64)
    large = np.minimum(large, N_BUCKETS - 1)
    bucket = np.where(n < max_exact, n, large)
    return np.where((d >= 0) & (d < BLOCK), bucket, -1).astype(np.int32)


def _bias_build(rel_bias_t, bucket):
    def body(rb_ref, bucket_ref, out_ref):
        bk = bucket_ref[...]
        for h in range(N_Q_HEADS):
            acc = jnp.full((BLOCK, 2 * BLOCK), NEG, F32)
            for b in range(N_BUCKETS):
                acc = jnp.where(bk == b, rb_ref[h, b], acc)
            out_ref[0, pl.ds(h * BLOCK, BLOCK), :] = acc
            out_ref[1, pl.ds(h * BLOCK, BLOCK), :] = acc
            out_ref[1, pl.ds(h * BLOCK, BLOCK), 0:BLOCK] = jnp.full((BLOCK, BLOCK), NEG, F32)

    return pl.pallas_call(
        body, name="bias_build",
        in_specs=[pl.BlockSpec(memory_space=pltpu.SMEM), VMEM_WHOLE], out_specs=VMEM_WHOLE,
        out_shape=jax.ShapeDtypeStruct((2, N_Q_HEADS * BLOCK, 2 * BLOCK), F32),
    )(rel_bias_t, bucket)


def _head_softmax(s_ref, bias_ref, sink_ref, h):
    rows = pl.ds(pl.multiple_of(h * BLOCK, BLOCK), BLOCK)
    s = s_ref[rows, :] * (HEAD_DIM ** -0.5) + bias_ref[rows, :]
    sink = sink_ref[h]
    m = jnp.maximum(jnp.max(s, axis=-1, keepdims=True), sink)
    p = jnp.exp(s - m)
    e_sink = jnp.exp(sink - m)
    inv = 1.0 / (jnp.sum(p, axis=-1, keepdims=True) + e_sink)
    return rows, p * inv, e_sink * inv


def _attn_specs():
    prev = lambda i: (jnp.maximum(i - 1, 0), 0)
    cur = lambda i: (i, 0)
    stacked = pl.BlockSpec((N_Q_HEADS, BLOCK, 128), lambda i: (0, i, 0))
    kv = [pl.BlockSpec((BLOCK, 128), prev), pl.BlockSpec((BLOCK, 128), cur)]
    consts = [pl.BlockSpec((None, N_Q_HEADS * BLOCK, 2 * BLOCK), lambda i: (jnp.where(i == 0, 1, 0), 0, 0)),
              pl.BlockSpec(memory_space=pltpu.SMEM)]
    return stacked, kv, consts


def _head_lane_mask():
    rows = lax.broadcasted_iota(jnp.int32, (N_Q_HEADS * BLOCK, 128), 0)
    lanes = lax.broadcasted_iota(jnp.int32, (N_Q_HEADS * BLOCK, 128), 1)
    return (rows < 4 * BLOCK) == (lanes < 64)


def _attn_fwd(qst, kn, vb, bias_st, sinks):
    s_len = kn.shape[0]

    def body(q_ref, kp_ref, kc_ref, vp_ref, vc_ref, bias_ref, sink_ref, o_ref, s_ref, p_ref):
        q = q_ref[...].reshape(N_Q_HEADS * BLOCK, 128)
        s_ref[...] = _dot(q, jnp.concatenate([kp_ref[...], kc_ref[...]], axis=0), 1, 1)

        def head(h, carry):
            rows, probs, _ = _head_softmax(s_ref, bias_ref, sink_ref, h)
            p_ref[rows, :] = probs.astype(BF16)
            return carry

        lax.fori_loop(0, N_Q_HEADS, head, 0, unroll=True)
        o = _dot(p_ref[...], jnp.concatenate([vp_ref[...], vc_ref[...]], axis=0), 1, 0)
        o_ref[...] = jnp.where(_head_lane_mask(), o, 0.0).astype(BF16).reshape(N_Q_HEADS, BLOCK, 128)

    stacked, kv, consts = _attn_specs()
    return pl.pallas_call(
        body, name="attn_fwd", grid=(s_len // BLOCK,),
        in_specs=[stacked] + kv + kv + consts, out_specs=stacked,
        out_shape=jax.ShapeDtypeStruct((N_Q_HEADS, s_len, 128), BF16),
        scratch_shapes=[pltpu.VMEM((N_Q_HEADS * BLOCK, 2 * BLOCK), F32), pltpu.VMEM((N_Q_HEADS * BLOCK, 2 * BLOCK), BF16)],
        compiler_params=_params(),
    )(qst, kn, kn, vb, vb, bias_st, sinks)


def _mix_out(u, ost, x2, wts, wpool, pool_scale, g_ffn):
    s_len = x2.shape[0]
    t = 512
    n = t + 16

    def body(u_ref, o_ref, x_ref, sl_ref, lo_ref, me_ref, wp_ref, sc_ref, g_ref, pooled_ref, mix_ref, h1_ref, hn_ref,
             w_ref, ext_ref, st_ref, sems):
        i = pl.program_id(0)

        @pl.when(i == 0)
        def _():
            _load_rows((sl_ref, lo_ref, me_ref), "out", w_ref, sems)
            ext_ref[...] = jnp.zeros_like(ext_ref)
            st_ref[...] = jnp.zeros_like(st_ref)

        u_tile = u_ref[...]
        ext_ref[pl.ds(POOL_HALO, t), :] = u_tile
        st_ref[pl.ds(8, n), :] = ext_ref[pl.ds(8, n), :] + ext_ref[pl.ds(7, n), :]
        st_ref[pl.ds(8, n), 128:] = st_ref[pl.ds(8, n), 128:] + st_ref[pl.ds(6, n), 128:]
        st_ref[pl.ds(8, n), 256:] = st_ref[pl.ds(8, n), 256:] + st_ref[pl.ds(4, n), 256:]
        st_ref[pl.ds(8, n), 384:] = st_ref[pl.ds(8, n), 384:] + st_ref[pl.ds(0, n), 384:]
        ext_ref[pl.ds(0, POOL_HALO), :] = ext_ref[pl.ds(t, POOL_HALO), :]
        pooled = (st_ref[pl.ds(POOL_HALO, t), :] / _pool_counts(i, t) - u_tile).astype(BF16)
        pooled_ref[...] = pooled
        for g in range(4):
            cols = slice(128 * g, 128 * g + 128)
            y = _dot(pooled[:, cols], wp_ref[g], 1, 0) * sc_ref[:, cols]
            mix_ref[:, ATTN_WIDTH + 128 * g:ATTN_WIDTH + 128 * g + 128] = y.astype(BF16)
        lo = lax.broadcasted_iota(jnp.int32, (t, 128), 1) < 64
        for p in range(4):
            a = _from_stacked(o_ref[2 * p].astype(F32), o_ref[2 * p + 1].astype(F32), p // 2, lo)
            mix_ref[:, 128 * p:128 * p + 128] = a.astype(BF16)
        h1 = x_ref[...] + _dot(mix_ref[...], w_ref[...], 1, 0)
        h1_ref[...] = h1
        hn_ref[...] = _rms_fwd(h1, g_ref[...]).astype(BF16)

    row = lambda w: pl.BlockSpec((t, w), lambda i: (i, 0))
    return pl.pallas_call(
        body, name="mix_out", grid=(s_len // t,),
        in_specs=[row(POOL_WIDTH), pl.BlockSpec((N_Q_HEADS, t, 128), lambda i: (0, i, 0)), row(D_MODEL)] + W_SPECS
        + [_full((4, 128, 128)), _full((1, POOL_WIDTH)), _full((1, D_MODEL))],
        out_specs=[row(POOL_WIDTH), row(D_MODEL), row(D_MODEL), row(D_MODEL)],
        out_shape=[_hbm((s_len, POOL_WIDTH), BF16), _hbm((s_len, D_MODEL), BF16), _hbm((s_len, D_MODEL), F32),
                   _hbm((s_len, D_MODEL), BF16)],
        scratch_shapes=[pltpu.VMEM((D_MODEL, D_MODEL), BF16), pltpu.VMEM((t + POOL_HALO, POOL_WIDTH), F32),
                        pltpu.VMEM((t + POOL_HALO, POOL_WIDTH), F32), pltpu.SemaphoreType.DMA((N_CHIPS,))],
        compiler_params=_params(),
    )(u, ost, x2, *wts, wpool, pool_scale, g_ffn)


def _ffn_ple(hn2, h1, p2, tgt, wts, g_ffn, g_ple):
    s_len = h1.shape[0]
    t = 256
    n_tiles = s_len // t

    def body(hn_ref, h1_ref, p_ref, tgt_ref, sl_ref, lo_ref, me_ref, gf_ref, gp_ref,
             loss_ref, dgate_ref, dup_ref, act_ref, dh2b_ref, hn3_ref, dgl_ref, dpp_ref, dh1_ref, dgf_ref, dgp_ref,
             wg_ref, wu_ref, wd_ref, wl_ref, wp_ref, packed_ref, gate_s, up_s, loss_acc, sems):
        i = pl.program_id(0)

        @pl.when(i == 0)
        def _():
            w_refs = (sl_ref, lo_ref, me_ref)
            _load_rows(w_refs, "gateT", wg_ref, sems)
            _load_rows(w_refs, "upT", wu_ref, sems)
            _load_rows(w_refs, "down", wd_ref, sems)
            _load_rows(w_refs, "plg", wl_ref, sems)
            _load_rows(w_refs, "plp", packed_ref, sems)
            for j in range(N_CHIPS):
                for q in range(4):
                    wp_ref[pl.ds(64 * q, 64), 256 * j:256 * j + 256] = packed_ref[pl.ds(64 * j, 64), 256 * q:256 * q + 256]
            loss_acc[...] = jnp.zeros_like(loss_acc)
            dgf_ref[...] = jnp.zeros_like(dgf_ref)
            dgp_ref[...] = jnp.zeros_like(dgp_ref)

        hn = hn_ref[...]
        h1v = h1_ref[...]
        h2 = h1v
        for ch in range(N_CHIPS):
            rows = pl.ds(ch * FF_CHUNK, FF_CHUNK)
            gate = _dot(hn, wg_ref[rows, :], 1, 1)
            up = _dot(hn, wu_ref[rows, :], 1, 1)
            gate_s[ch] = gate
            up_s[ch] = up
            act = (gate * _sigmoid(gate) * up).astype(BF16)
            act_ref[ch] = act
            h2 = h2 + _dot(act, wd_ref[rows, :], 1, 0)
        gp = gp_ref[...]
        hn3 = _rms_fwd(h2, gp).astype(BF16)
        hn3_ref[...] = hn3
        gate2 = _sigmoid(_dot(hn3, wl_ref[...], 1, 0))
        pp = _dot(p_ref[...].astype(BF16), wp_ref[...], 1, 0)
        err = h2 + gate2 * pp - tgt_ref[...]
        loss_acc[...] += jnp.sum(err * err, axis=0, keepdims=True)
        dy = err * (1.0 / D_MODEL)
        dpp_ref[...] = (dy * gate2).astype(BF16)
        dgl = (dy * pp * gate2 * (1.0 - gate2)).astype(BF16)
        dgl_ref[...] = dgl
        dx3, dg3 = _rms_bwd(h2, gp, _dot(dgl, wl_ref[...], 1, 1))
        dh2 = dy + dx3
        dgp_ref[...] += dg3
        dh2b = dh2.astype(BF16)
        dh2b_ref[...] = dh2b
        dhn = jnp.zeros((t, D_MODEL), F32)
        for ch in range(N_CHIPS):
            rows = pl.ds(ch * FF_CHUNK, FF_CHUNK)
            dact = _dot(dh2b, wd_ref[rows, :], 1, 1)
            gate_v = gate_s[ch]
            up_v = up_s[ch]
            sg = _sigmoid(gate_v)
            dup = (dact * (gate_v * sg)).astype(BF16)
            dgate = (dact * up_v * (sg * (1.0 + gate_v * (1.0 - sg)))).astype(BF16)
            dup_ref[ch] = dup
            dgate_ref[ch] = dgate
            dhn = dhn + _dot(dgate, wg_ref[rows, :], 1, 0) + _dot(dup, wu_ref[rows, :], 1, 0)
        dx, dg = _rms_bwd(h1v, gf_ref[...], dhn)
        dh1_ref[...] = dh2 + dx
        dgf_ref[...] += dg

        @pl.when(i == n_tiles - 1)
        def _():
            total = jnp.sum(loss_acc[...], axis=-1, keepdims=True) * (0.5 / D_MODEL)
            loss_ref[...] = jnp.broadcast_to(total, loss_ref.shape)

    row = lambda w: pl.BlockSpec((t, w), lambda i: (i, 0))
    chunked = pl.BlockSpec((N_CHIPS, t, FF_CHUNK), lambda i: (0, i, 0))
    vec = _full((1, D_MODEL))
    act_shape = _hbm((N_CHIPS, s_len, FF_CHUNK), BF16)
    tok = lambda dtype: _hbm((s_len, D_MODEL), dtype)
    return pl.pallas_call(
        body, name="ffn_ple", grid=(n_tiles,),
        in_specs=[row(D_MODEL), row(D_MODEL), row(PLE_DIM), row(D_MODEL)] + W_SPECS + [vec, vec],
        out_specs=[_full((1, 128)), chunked, chunked, chunked] + [row(D_MODEL)] * 5 + [vec, vec],
        out_shape=[jax.ShapeDtypeStruct((1, 128), F32), act_shape, act_shape, act_shape, tok(BF16), tok(BF16), tok(BF16),
                   tok(BF16), tok(F32), jax.ShapeDtypeStruct((1, D_MODEL), F32), jax.ShapeDtypeStruct((1, D_MODEL), F32)],
        scratch_shapes=[pltpu.VMEM((D_FF, D_MODEL), BF16)] * 3
        + [pltpu.VMEM((D_MODEL, D_MODEL), BF16), pltpu.VMEM((PLE_DIM, D_MODEL), BF16), pltpu.VMEM((PLE_DIM, D_MODEL), BF16),
           pltpu.VMEM((N_CHIPS, t, FF_CHUNK), F32), pltpu.VMEM((N_CHIPS, t, FF_CHUNK), F32), pltpu.VMEM((1, D_MODEL), F32),
           pltpu.SemaphoreType.DMA((N_CHIPS,))],
        compiler_params=_params(VMEM_LIMIT_BIG),
    )(hn2, h1, p2, tgt, *wts, g_ffn, g_ple)


def _mix_out_bwd(dh1, wts, pooled, wpool, pool_scale, after):
    s_len = dh1.shape[0]
    t = 512
    n = t + 16
    n_tiles = s_len // t

    def body(dh1_ref, sl_ref, lo_ref, me_ref, pooled_ref, wp_ref, sc_ref, after_ref, dost_ref, du_ref, dyp_ref, dsc_ref,
             w_ref, ext_ref, st_ref, sems):
        del after_ref
        i = pl.program_id(0)

        @pl.when(i == 0)
        def _():
            _load_rows((sl_ref, lo_ref, me_ref), "out", w_ref, sems)
            ext_ref[...] = jnp.zeros_like(ext_ref)
            st_ref[...] = jnp.zeros_like(st_ref)
            dsc_ref[...] = jnp.zeros_like(dsc_ref)

        dmix = _dot(dh1_ref[...].astype(BF16), w_ref[...], 1, 1)
        lo = lax.broadcasted_iota(jnp.int32, (t, 128), 1) < 64
        for p in range(4):
            even, odd = _to_stacked(dmix[:, 128 * p:128 * p + 128], p // 2, lo)
            dost_ref[2 * p] = even.astype(BF16)
            dost_ref[2 * p + 1] = odd.astype(BF16)
        pooled_v = pooled_ref[...]
        counts = _pool_counts(n_tiles - 1 - i, t)
        for g in range(4):
            cols = slice(128 * g, 128 * g + 128)
            dm = dmix[:, ATTN_WIDTH + 128 * g:ATTN_WIDTH + 128 * g + 128]
            ypre = _dot(pooled_v[:, cols], wp_ref[g], 1, 0)
            dsc_ref[:, cols] += jnp.sum(ypre * dm, axis=0, keepdims=True)
            dyp = (dm * sc_ref[:, cols]).astype(BF16)
            dyp_ref[:, cols] = dyp
            dpooled = _dot(dyp, wp_ref[g], 1, 1)
            du_ref[:, cols] = -dpooled
            ext_ref[pl.ds(0, t), cols] = dpooled / counts[:, cols]
        st_ref[pl.ds(0, n), :] = ext_ref[pl.ds(0, n), :] + ext_ref[pl.ds(1, n), :]
        st_ref[pl.ds(0, n), 128:] = st_ref[pl.ds(0, n), 128:] + st_ref[pl.ds(2, n), 128:]
        st_ref[pl.ds(0, n), 256:] = st_ref[pl.ds(0, n), 256:] + st_ref[pl.ds(4, n), 256:]
        st_ref[pl.ds(0, n), 384:] = st_ref[pl.ds(0, n), 384:] + st_ref[pl.ds(8, n), 384:]
        ext_ref[pl.ds(t, POOL_HALO), :] = ext_ref[pl.ds(0, POOL_HALO), :]
        du_ref[...] += st_ref[pl.ds(0, t), :]

    rev = lambda w: pl.BlockSpec((t, w), lambda i: (n_tiles - 1 - i, 0))
    return pl.pallas_call(
        body, name="mix_out_bwd", grid=(n_tiles,),
        in_specs=[rev(D_MODEL)] + W_SPECS + [rev(POOL_WIDTH), _full((4, 128, 128)), _full((1, POOL_WIDTH)), ANY],
        out_specs=[pl.BlockSpec((N_Q_HEADS, t, 128), lambda i: (0, n_tiles - 1 - i, 0)), rev(POOL_WIDTH), rev(POOL_WIDTH),
                   _full((1, POOL_WIDTH))],
        out_shape=[_hbm((N_Q_HEADS, s_len, 128), BF16), _hbm((s_len, POOL_WIDTH), F32), _hbm((s_len, POOL_WIDTH), BF16),
                   jax.ShapeDtypeStruct((1, POOL_WIDTH), F32)],
        scratch_shapes=[pltpu.VMEM((D_MODEL, D_MODEL), BF16), pltpu.VMEM((t + POOL_HALO, POOL_WIDTH), F32),
                        pltpu.VMEM((t + POOL_HALO, POOL_WIDTH), F32), pltpu.SemaphoreType.DMA((N_CHIPS,))],
        compiler_params=_params(),
    )(dh1, *wts, pooled, wpool, pool_scale, after)


def _attn_bwd(qst, kn, vb, dost, bias_st, sinks, after):
    s_len = kn.shape[0]

    def body(q_ref, kp_ref, kc_ref, vp_ref, vc_ref, do_ref, bias_ref, sink_ref, after_ref, dq_ref, dk_ref, dv_ref, dbias_ref,
             dsink_ref, s_ref, dp_ref, p_ref, dl_ref):
        del after_ref
        i = pl.program_id(0)

        @pl.when(i == 0)
        def _():
            dk_ref[...] = jnp.zeros_like(dk_ref)
            dv_ref[...] = jnp.zeros_like(dv_ref)
            dbias_ref[...] = jnp.zeros_like(dbias_ref)
            dsink_ref[...] = jnp.zeros_like(dsink_ref)

        q = q_ref[...].reshape(N_Q_HEADS * BLOCK, 128)
        do = do_ref[...].reshape(N_Q_HEADS * BLOCK, 128)
        k2 = jnp.concatenate([kp_ref[...], kc_ref[...]], axis=0)
        s_ref[...] = _dot(q, k2, 1, 1)
        dp_ref[...] = _dot(do, jnp.concatenate([vp_ref[...], vc_ref[...]], axis=0), 1, 1)

        def head(h, carry):
            rows, probs, p_sink = _head_softmax(s_ref, bias_ref, sink_ref, h)
            dp = dp_ref[rows, :]
            dsum = jnp.sum(probs * dp, axis=-1, keepdims=True)
            dlog = probs * (dp - dsum)
            dsink_ref[rows, :] -= p_sink * dsum
            dbias_ref[rows, :] += dlog
            p_ref[rows, :] = probs.astype(BF16)
            dl_ref[rows, :] = (dlog * (HEAD_DIM ** -0.5)).astype(BF16)
            return carry

        lax.fori_loop(0, N_Q_HEADS, head, 0, unroll=True)
        dlog_s = dl_ref[...]
        dq_ref[...] = jnp.where(_head_lane_mask(), _dot(dlog_s, k2, 1, 0), 0.0).reshape(N_Q_HEADS, BLOCK, 128)
        dk2 = _dot(dlog_s, q, 0, 0)
        dv2 = _dot(p_ref[...], do, 0, 0)
        prev_rows = pl.ds(pl.multiple_of(jnp.maximum(i - 1, 0) * BLOCK, BLOCK), BLOCK)
        cur_rows = pl.ds(pl.multiple_of(i * BLOCK, BLOCK), BLOCK)
        dk_ref[prev_rows, :] += dk2[:BLOCK]
        dk_ref[cur_rows, :] += dk2[BLOCK:]
        dv_ref[prev_rows, :] += dv2[:BLOCK]
        dv_ref[cur_rows, :] += dv2[BLOCK:]

    stacked, kv, consts = _attn_specs()
    band = (N_Q_HEADS * BLOCK, 2 * BLOCK)
    return pl.pallas_call(
        body, name="attn_bwd", grid=(s_len // BLOCK,),
        in_specs=[stacked] + kv + kv + [stacked] + consts + [ANY],
        out_specs=[stacked, _full((s_len, 128)), _full((s_len, 128)), _full(band), _full((N_Q_HEADS * BLOCK, 1))],
        out_shape=[_hbm((N_Q_HEADS, s_len, 128), F32), _hbm((s_len, 128), F32), _hbm((s_len, 128), F32), _hbm(band, F32),
                   _hbm((N_Q_HEADS * BLOCK, 1), F32)],
        scratch_shapes=[pltpu.VMEM(band, F32), pltpu.VMEM(band, F32), pltpu.VMEM(band, BF16), pltpu.VMEM(band, BF16)],
        compiler_params=_params(),
    )(qst, kn, kn, vb, vb, dost, bias_st, sinks, after)


def _small_pack(dg_attn, dg_ffn, dg_ple, dscale, dgq, dgk, dbias, dsink_rows, bucket, loss_v, dwpool):
    def body(ga_ref, gf_ref, gp_ref, sc_ref, gq_ref, gk_ref, db_ref, ds_ref, bucket_ref, loss_ref, wp_ref, out_ref):
        out_ref[pl.ds(0, SMALL["w_pool"]), :] = jnp.zeros((SMALL["w_pool"], 128), F32)
        for name, ref, n in (("g_attn", ga_ref, 8), ("g_ffn", gf_ref, 8), ("g_ple", gp_ref, 8), ("pool_scale", sc_ref, 4)):
            for k in range(n):
                out_ref[pl.ds(SMALL[name] + k, 1), :] = ref[:, 128 * k:128 * k + 128]
        for name, ref in (("g_q", gq_ref), ("g_k", gk_ref)):
            both = ref[...]
            out_ref[pl.ds(SMALL[name], 1), :] = both + pltpu.roll(both, 64, axis=1)
        out_ref[pl.ds(SMALL["loss"], 1), :] = loss_ref[...]
        bk = bucket_ref[...]
        rows = lax.broadcasted_iota(jnp.int32, (N_BUCKETS, 128), 0)
        lanes = lax.broadcasted_iota(jnp.int32, (N_BUCKETS, 128), 1)
        lane1 = lax.broadcasted_iota(jnp.int32, (1, 128), 1)
        rb = jnp.zeros((N_BUCKETS, 128), F32)
        sk = jnp.zeros((1, 128), F32)
        for h in range(N_Q_HEADS):
            band = db_ref[pl.ds(h * BLOCK, BLOCK), :]
            for b in range(N_BUCKETS):
                rb = jnp.where((rows == b) & (lanes == h), jnp.sum(jnp.where(bk == b, band, 0.0)), rb)
            sk = jnp.where(lane1 == h, jnp.sum(ds_ref[pl.ds(h * BLOCK, BLOCK), :]), sk)
        out_ref[pl.ds(SMALL["rel_bias"], N_BUCKETS), :] = rb
        out_ref[pl.ds(SMALL["sinks"], 1), :] = sk
        out_ref[pl.ds(SMALL["w_pool"], 512), :] = wp_ref[...].reshape(512, 128)

    return pl.pallas_call(
        body, name="small_pack", in_specs=[VMEM_WHOLE] * 11, out_specs=VMEM_WHOLE,
        out_shape=jax.ShapeDtypeStruct((SMALL_ROWS, 128), F32),
    )(dg_attn, dg_ffn, dg_ple, dscale, dgq, dgk, dbias, dsink_rows, bucket, loss_v, dwpool)


def _attn_in_bwd(dqst, zqk, dk, dv, du, x2, dh1, wts, g_attn, gq, gk):
    s_len = x2.shape[0]
    t = 512

    def body(dq_ref, zqk_ref, dk_ref, dv_ref, du_ref, x_ref, dh1_ref, sl_ref, lo_ref, me_ref, g_ref, gq_ref, gk_ref,
             dz_ref, dx_ref, dg_ref, dgq_ref, dgk_ref, w_ref, sems):
        @pl.when(pl.program_id(0) == 0)
        def _():
            _load_rows((sl_ref, lo_ref, me_ref), "inT", w_ref, sems)
            dg_ref[...] = jnp.zeros_like(dg_ref)
            dgq_ref[...] = jnp.zeros_like(dgq_ref)
            dgk_ref[...] = jnp.zeros_like(dgk_ref)

        lo = lax.broadcasted_iota(jnp.int32, (t, 128), 1) < 64
        for p in range(4):
            dqn = _from_stacked(dq_ref[2 * p], dq_ref[2 * p + 1], p // 2, lo)
            dq_raw, dgq = _pair_norm_bwd(zqk_ref[:, 128 * p:128 * p + 128], gq_ref[...], dqn)
            dz_ref[:, 128 * p:128 * p + 128] = dq_raw.astype(BF16)
            dgq_ref[...] += dgq
        dk_raw, dgk = _pair_norm_bwd(zqk_ref[:, 512:640], gk_ref[...], dk_ref[...])
        dgk_ref[...] += dgk
        dz_ref[:, 512:640] = dk_raw.astype(BF16)
        dz_ref[:, 640:768] = dv_ref[...].astype(BF16)
        dz_ref[:, 768:] = du_ref[...].astype(BF16)
        dx, dg = _rms_bwd(x_ref[...], g_ref[...], _dot(dz_ref[...], w_ref[...], 1, 0))
        dx_ref[...] = dh1_ref[...] + dx
        dg_ref[...] += dg

    row = lambda w: pl.BlockSpec((t, w), lambda i: (i, 0))
    return pl.pallas_call(
        body, name="attn_in_bwd", grid=(s_len // t,),
        in_specs=[pl.BlockSpec((N_Q_HEADS, t, 128), lambda i: (0, i, 0)), row(640), row(128), row(128), row(POOL_WIDTH),
                  row(D_MODEL), row(D_MODEL)] + W_SPECS + [_full((1, D_MODEL)), _full((1, 128)), _full((1, 128))],
        out_specs=[row(IN_WIDTH), row(D_MODEL), _full((1, D_MODEL)), _full((1, 128)), _full((1, 128))],
        out_shape=[jax.ShapeDtypeStruct((s_len, IN_WIDTH), BF16), jax.ShapeDtypeStruct((s_len, D_MODEL), F32),
                   jax.ShapeDtypeStruct((1, D_MODEL), F32), jax.ShapeDtypeStruct((1, 128), F32),
                   jax.ShapeDtypeStruct((1, 128), F32)],
        scratch_shapes=[pltpu.VMEM((IN_WIDTH, D_MODEL), BF16), pltpu.SemaphoreType.DMA((N_CHIPS,))],
        compiler_params=_params(),
    )(dqst, zqk, dk, dv, du, x2, dh1, *wts, g_attn, gq, gk)


def _dw(a, b, name, into=None):
    tk = 1024
    n_out = b.shape[1]
    if a.ndim == 3:
        s_len, tm = a.shape[1:]
        m = N_CHIPS * tm
        a_spec = pl.BlockSpec((None, tk, tm), lambda i, k: (i, k, 0))
    else:
        s_len, m = a.shape
        tm = m // 2 if m > 1408 else m
        a_spec = pl.BlockSpec((tk, tm), lambda i, k: (k, i))
    n_steps = s_len // tk
    chunk = m // N_CHIPS
    per_tile = tm // chunk

    def accumulate(a_ref, b_ref, acc_ref, k):
        @pl.when(k == 0)
        def _():
            acc_ref[...] = _dot(a_ref[...].astype(BF16), b_ref[...].astype(BF16), 0, 0)

        @pl.when(k > 0)
        def _():
            acc_ref[...] += _dot(a_ref[...].astype(BF16), b_ref[...].astype(BF16), 0, 0)

    in_specs = [a_spec, pl.BlockSpec((tk, n_out), lambda i, k: (k, 0))]
    if into is None:
        def body(a_ref, b_ref, o_ref, acc_ref):
            k = pl.program_id(1)
            accumulate(a_ref, b_ref, acc_ref, k)

            @pl.when(k == n_steps - 1)
            def _():
                o_ref[...] = acc_ref[...].astype(BF16)

        return pl.pallas_call(
            body, name=name, grid=(m // tm, n_steps), in_specs=in_specs,
            out_specs=pl.BlockSpec((tm, n_out), lambda i, k: (i, 0)), out_shape=_hbm((m, n_out), BF16),
            scratch_shapes=[pltpu.VMEM((tm, n_out), F32)], compiler_params=_params(n_axes=2),
        )(a, b)

    slab, slab_rows, row_off = into
    assert n_out == D_MODEL

    n_tiles = m // tm

    def body_into(a_ref, b_ref, *rest):
        o_ref, acc_ref, stage_ref, sems = rest[-4:]
        i, k = pl.program_id(0), pl.program_id(1)
        accumulate(a_ref, b_ref, acc_ref, k)

        def out_copies(tile, slot):
            return [pltpu.make_async_copy(stage_ref.at[slot, pl.ds(jj * chunk, chunk), :],
                                          o_ref.at[tile * per_tile + jj, pl.ds(row_off, chunk), :], sems.at[slot, jj])
                    for jj in range(per_tile)]

        @pl.when(k == n_steps - 1)
        def _():
            slot = i % 2

            @pl.when(i >= 2)
            def _():
                for cp in out_copies(i - 2, slot):
                    cp.wait()

            stage_ref[slot] = acc_ref[...].astype(BF16)
            for cp in out_copies(i, slot):
                cp.start()

            @pl.when(i == n_tiles - 1)
            def _():
                for cp in out_copies(i, slot):
                    cp.wait()
                if n_tiles > 1:
                    for cp in out_copies(i - 1, 1 - slot):
                        cp.wait()

    operands, aliases = [a, b], {}
    if slab is not None:
        in_specs = in_specs + [ANY]
        operands.append(slab)
        aliases = {2: 0}
    return pl.pallas_call(
        body_into, name=name, grid=(n_tiles, n_steps), in_specs=in_specs, out_specs=ANY,
        out_shape=jax.ShapeDtypeStruct((N_CHIPS, slab_rows, D_MODEL), BF16), input_output_aliases=aliases,
        scratch_shapes=[pltpu.VMEM((tm, n_out), F32), pltpu.VMEM((2, tm, n_out), BF16),
                        pltpu.SemaphoreType.DMA((2, per_tile))],
        compiler_params=_params(n_axes=2),
    )(*operands)


def _dw_pool(pooled, dyp):
    s_len = pooled.shape[0]
    tk = 512

    def body(a_ref, b_ref, o_ref):
        @pl.when(pl.program_id(0) == 0)
        def _():
            o_ref[...] = jnp.zeros_like(o_ref)

        for g in range(4):
            cols = slice(128 * g, 128 * g + 128)
            o_ref[g] += _dot(a_ref[:, cols], b_ref[:, cols], 0, 0)

    blk = pl.BlockSpec((tk, POOL_WIDTH), lambda k: (k, 0))
    return pl.pallas_call(
        body, name="dw_pool", grid=(s_len // tk,), in_specs=[blk, blk], out_specs=_full((4, 128, 128)),
        out_shape=jax.ShapeDtypeStruct((4, 128, 128), F32), compiler_params=_params(),
    )(pooled, dyp)


def _position():
    x, y, c = lax.axis_index("x"), lax.axis_index("y"), lax.axis_index("c")
    other_chips = [(1 - x, y), (x, 1 - y), (1 - x, 1 - y)]
    return x, y, c, other_chips


def _ag_weights(local_slab, row0, n_rows, name, collective_id):
    half = n_rows // 2
    quarter = half // 2
    assert quarter % 16 == 0

    def body(l_ref, g_ref, send, recv):
        x, y, c, chips = _position()
        me, (via_x, via_y, diagonal) = 2 * x + y, [2 * chip[0] + chip[1] for chip in chips]
        here, sibling, x_nbr, y_nbr = (x, y, c), (x, y, 1 - c), (1 - x, y, c), (x, 1 - y, c)
        peers = [sibling, x_nbr, y_nbr]
        barrier = pltpu.get_barrier_semaphore()
        for peer in peers:
            pl.semaphore_signal(barrier, inc=1, device_id=peer, device_id_type=MESH)
        pl.semaphore_wait(barrier, len(peers))

        def rows(core, part):
            start, size = (core * half, half) if part is None else (core * half + part * quarter, quarter)
            return pl.ds(pl.multiple_of(start, 16), size)

        def copy(k, chip_idx, where, to, src=None):
            dst = g_ref.at[chip_idx, where, :]
            return pltpu.make_async_remote_copy(src_ref=dst if src is None else src, dst_ref=dst, send_sem=send.at[k],
                                                recv_sem=recv.at[k], device_id=to, device_id_type=MESH)

        own_rows = l_ref.at[pl.ds(pl.multiple_of(row0 + c * half, 16), half), :]
        started = [copy(0, me, rows(c, None), x_nbr, src=own_rows), copy(1, me, rows(c, None), y_nbr, src=own_rows)]
        for cp in started:
            cp.start()
        after_arrival = [
            (copy(0, via_x, rows(c, None), here), [copy(4, via_x, rows(c, None), sibling), copy(3, via_x, rows(c, 1), y_nbr)]),
            (copy(1, via_y, rows(c, None), here), [copy(5, via_y, rows(c, None), sibling), copy(2, via_y, rows(c, 0), x_nbr)]),
            (copy(2, diagonal, rows(c, 0), here), [copy(6, diagonal, rows(c, 0), sibling)]),
            (copy(3, diagonal, rows(c, 1), here), [copy(7, diagonal, rows(c, 1), sibling)]),
        ]
        for arrival, onward in after_arrival:
            arrival.wait_recv()
            for cp in onward:
                cp.start()
            started += onward
        for cp in (copy(4, via_x, rows(1 - c, None), here), copy(5, via_y, rows(1 - c, None), here),
                   copy(6, diagonal, rows(1 - c, 0), here), copy(7, diagonal, rows(1 - c, 1), here)):
            cp.wait_recv()
        for cp in started:
            cp.wait_send()

    return pl.kernel(
        body, out_type=jax.ShapeDtypeStruct((N_CHIPS, n_rows, D_MODEL), BF16),
        mesh=plsc.ScalarSubcoreMesh(axis_name="sequencer", num_cores=1), name=name,
        scratch_types=[pltpu.SemaphoreType.DMA((8,)), pltpu.SemaphoreType.DMA((8,))],
        compiler_params=pltpu.CompilerParams(collective_id=collective_id),
    )(local_slab)


def _comm_call(body, peers_of, out_shape, n_sems, operand, name, collective_id):
    sems = [pltpu.SemaphoreType.DMA((n_sems,)), pltpu.SemaphoreType.DMA((n_sems,))]
    if collective_id is None:
        return pl.pallas_call(body, name=name, in_specs=[ANY], out_specs=ANY, out_shape=out_shape, scratch_shapes=sems)(operand)

    def with_handshake(in_ref, out_ref, send, recv):
        x, y, c, _ = _position()
        peers = peers_of(x, y, c)
        barrier = pltpu.get_barrier_semaphore()
        for peer in peers:
            pl.semaphore_signal(barrier, inc=1, device_id=peer, device_id_type=MESH)
        pl.semaphore_wait(barrier, len(peers))
        body(in_ref, out_ref, send, recv)

    return pl.kernel(with_handshake, out_type=out_shape, mesh=plsc.ScalarSubcoreMesh(axis_name="sequencer", num_cores=1),
                     name=name, scratch_types=sems, compiler_params=pltpu.CompilerParams(collective_id=collective_id))(operand)


def _rs_swap_halves(partial, name, collective_id=None):
    half = partial.shape[1] // 2

    def body(p_ref, r_ref, send, recv):
        x, y, c, _ = _position()
        theirs = pl.ds(pl.multiple_of((1 - c) * half, 16), half)
        cp = pltpu.make_async_remote_copy(src_ref=p_ref.at[:, theirs, :], dst_ref=r_ref, send_sem=send.at[0],
                                          recv_sem=recv.at[0], device_id=(x, y, 1 - c), device_id_type=MESH)
        cp.start()
        cp.wait()

    return _comm_call(body, lambda x, y, c: [(x, y, 1 - c)], jax.ShapeDtypeStruct((N_CHIPS, half, D_MODEL), BF16), 1,
                      partial, name, collective_id)


def _rs_add_halves(partial, other, core, name, after):
    half = other.shape[1]
    t = half // 2
    steps = half // t

    def body(core_ref, a_ref, b_ref, after_ref, o_ref):
        del after_ref
        o_ref[...] = (a_ref[...].astype(F32) + b_ref[...].astype(F32)).astype(BF16)

    return pl.pallas_call(
        body, name=name,
        grid_spec=pltpu.PrefetchScalarGridSpec(
            num_scalar_prefetch=1, grid=(N_CHIPS, steps),
            in_specs=[pl.BlockSpec((1, t, D_MODEL), lambda j, i, core_ref: (j, core_ref[0] * steps + i, 0)),
                      pl.BlockSpec((1, t, D_MODEL), lambda j, i, core_ref: (j, i, 0)), ANY],
            out_specs=pl.BlockSpec((1, t, D_MODEL), lambda j, i, core_ref: (j, i, 0))),
        out_shape=jax.ShapeDtypeStruct((N_CHIPS, half, D_MODEL), BF16),
        compiler_params=_params(n_axes=2),
    )(core, partial, other, after)


def _rs_exchange_chips(pre, name, collective_id=None):
    def body(s_ref, r_ref, send, recv):
        x, y, c, chips = _position()

        def copy(k, chunk, to):
            return pltpu.make_async_remote_copy(src_ref=s_ref.at[chunk], dst_ref=r_ref.at[k], send_sem=send.at[k],
                                                recv_sem=recv.at[k], device_id=to, device_id_type=MESH)

        sends = [copy(k, 2 * chip[0] + chip[1], (*chip, c)) for k, chip in enumerate(chips)]
        for cp in sends:
            cp.start()
        for cp in sends:
            cp.wait()

    return _comm_call(body, lambda x, y, c: [(1 - x, y, c), (x, 1 - y, c), (1 - x, 1 - y, c)],
                      jax.ShapeDtypeStruct((3, pre.shape[1], D_MODEL), BF16), 3, pre, name, collective_id)


def _rs_sum_chips(pre, received, place, name, after):
    half = pre.shape[1]
    t = half // 2 if half > 512 else half
    steps = half // t

    def body(place_ref, own_ref, r_ref, after_ref, o_ref):
        del after_ref
        acc = own_ref[0].astype(F32)
        for k in range(3):
            acc = acc + r_ref[k].astype(F32)
        o_ref[...] = acc

    return pl.pallas_call(
        body, name=name,
        grid_spec=pltpu.PrefetchScalarGridSpec(
            num_scalar_prefetch=1, grid=(steps,),
            in_specs=[pl.BlockSpec((1, t, D_MODEL), lambda i, place_ref: (place_ref[0], i, 0)),
                      pl.BlockSpec((3, t, D_MODEL), lambda i, place_ref: (0, i, 0)), ANY],
            out_specs=pl.BlockSpec((t, D_MODEL), lambda i, place_ref: (place_ref[1] * steps + i, 0))),
        out_shape=jax.ShapeDtypeStruct((2 * half, D_MODEL), F32),
        compiler_params=_params(),
    )(place, pre, received, after)


def _half_swap(g_ref, core, to, send, recv, k):
    half = g_ref.shape[0] // 2
    rows = g_ref.at[pl.ds(pl.multiple_of(core * half, 8), half), :]
    return pltpu.make_async_remote_copy(src_ref=rows, dst_ref=rows, send_sem=send.at[k], recv_sem=recv.at[k],
                                        device_id=to, device_id_type=MESH)


def _rs_finish_rows(grads, name, after):
    def body(f_ref, after_ref, g_ref, send, recv):
        del f_ref, after_ref
        x, y, c, _ = _position()
        mine = _half_swap(g_ref, c, (x, y, 1 - c), send, recv, 0)
        mine.start()
        _half_swap(g_ref, 1 - c, (x, y, c), send, recv, 0).wait_recv()
        mine.wait_send()

    return pl.pallas_call(
        body, name=name, in_specs=[ANY, ANY], out_specs=ANY, input_output_aliases={0: 0},
        out_shape=jax.ShapeDtypeStruct(grads.shape, F32),
        scratch_shapes=[pltpu.SemaphoreType.DMA((1,)), pltpu.SemaphoreType.DMA((1,))],
    )(grads, after)


def _small_gather(small, collective_id):
    def body(s_ref, t_ref, send, recv):
        x, y, c, chips = _position()
        sibling = (x, y, 1 - c)

        def slot(px, py, pc):
            return t_ref.at[4 * px + 2 * py + pc]

        def copy(k, block, to, src=None):
            return pltpu.make_async_remote_copy(src_ref=slot(*block) if src is None else src, dst_ref=slot(*block),
                                                send_sem=send.at[k], recv_sem=recv.at[k], device_id=to, device_id_type=MESH)

        own = pltpu.make_async_copy(s_ref, slot(x, y, c), send.at[7])
        own.start()
        first = [copy(0, (x, y, c), sibling, src=s_ref)]
        first += [copy(1 + k, (x, y, c), (*chip, c), src=s_ref) for k, chip in enumerate(chips)]
        for cp in first:
            cp.start()
        passed = []
        for k, chip in enumerate(chips):
            copy(1 + k, (*chip, c), (x, y, c)).wait_recv()
            fwd = copy(4 + k, (*chip, c), sibling)
            fwd.start()
            passed.append(fwd)
        copy(0, sibling, (x, y, c)).wait_recv()
        for k, chip in enumerate(chips):
            copy(4 + k, (*chip, 1 - c), (x, y, c)).wait_recv()
        for cp in first + passed:
            cp.wait_send()
        own.wait()

    peers_of = lambda x, y, c: [(x, y, 1 - c), (1 - x, y, c), (x, 1 - y, c), (1 - x, 1 - y, c)]
    return _comm_call(body, peers_of, jax.ShapeDtypeStruct((N_DEV, SMALL_ROWS, 128), F32), 8, small, "small_gather",
                      collective_id)


def _adam_update(w, g, m, v):
    m_new = ADAM_B1 * m + (1.0 - ADAM_B1) * g
    v_new = ADAM_B2 * v + (1.0 - ADAM_B2) * (g * g)
    m_hat = m_new / (1.0 - ADAM_B1 ** ADAM_STEP)
    v_hat = v_new / (1.0 - ADAM_B2 ** ADAM_STEP)
    return -ADAM_LR * (m_hat / (jnp.sqrt(v_hat) + ADAM_EPS) + ADAM_WD * w), m_new, v_new


def _adamw(w, g_rows, row_off, m, v, name):
    rows, cols = w.shape
    t = rows if rows <= 320 else (rows // 2 if rows % 256 else 256)

    def body(w_ref, g_ref, m_ref, v_ref, go_ref, d_ref, nm_ref, nv_ref):
        g = g_ref[...]
        go_ref[...] = g
        d_ref[...], nm_ref[...], nv_ref[...] = _adam_update(w_ref[...], g, m_ref[...], v_ref[...])

    blk = pl.BlockSpec((t, cols), lambda i: (i, 0))
    assert row_off % 8 == 0 and t % 8 == 0
    g_blk = pl.BlockSpec((pl.Element(t), pl.Element(cols)), lambda i: (pl.multiple_of(row_off + i * t, 8), 0))
    shape = jax.ShapeDtypeStruct((rows, cols), F32)
    in_hbm = lambda a: pltpu.with_memory_space_constraint(a, pltpu.HBM)
    return pl.pallas_call(
        body, name=name, grid=(rows // t,), in_specs=[blk, g_blk, blk, blk], out_specs=[blk] * 4, out_shape=[shape] * 4,
        compiler_params=_params(),
    )(in_hbm(w), g_rows, in_hbm(m), in_hbm(v))


SMALL_PARAMS = [("g_attn", (1, D_MODEL), 8), ("g_q", (1, HEAD_DIM), None), ("g_k", (1, HEAD_DIM), None),
                ("sinks", (1, N_Q_HEADS), None), ("rel_bias", (N_BUCKETS, N_Q_HEADS), None), ("w_pool", (512, 128), None),
                ("pool_scale", (1, POOL_WIDTH), 4), ("g_ffn", (1, D_MODEL), 8), ("g_ple", (1, D_MODEL), 8)]


def _adamw_small(tables, wmv):
    n_par = len(SMALL_PARAMS)

    def body(*refs):
        t_ref = refs[0]
        ins = refs[1:1 + 3 * n_par]
        loss_ref = refs[1 + 3 * n_par]
        outs = refs[2 + 3 * n_par:-1]
        tot_ref = refs[-1]
        total = t_ref[0]
        for d in range(1, N_DEV):
            total = total + t_ref[d]
        tot_ref[...] = total
        loss_ref[...] = tot_ref[pl.ds(SMALL["loss"], 1), 0:1]
        for i, (name, shape, split) in enumerate(SMALL_PARAMS):
            g_ref, d_ref, nm_ref, nv_ref = outs[4 * i:4 * i + 4]
            row = SMALL[name]
            if split:
                for k in range(split):
                    g_ref[:, 128 * k:128 * k + 128] = tot_ref[pl.ds(row + k, 1), :]
            else:
                g_ref[...] = tot_ref[pl.ds(row, shape[0]), 0:shape[1]]
            w_ref, m_ref, v_ref = ins[3 * i:3 * i + 3]
            d_ref[...], nm_ref[...], nv_ref[...] = _adam_update(w_ref[...], g_ref[...], m_ref[...], v_ref[...])

    shapes = [jax.ShapeDtypeStruct((1, 1), F32)]
    for _, shape, _ in SMALL_PARAMS:
        shapes += [jax.ShapeDtypeStruct(shape, F32)] * 4
    flat = [a for triple in wmv for a in triple]
    res = pl.pallas_call(
        body, name="adamw_small", in_specs=[VMEM_WHOLE] * (1 + 3 * n_par), out_specs=[VMEM_WHOLE] * len(shapes),
        out_shape=shapes, scratch_shapes=[pltpu.VMEM((SMALL_ROWS, 128), F32)],
    )(tables, *flat)
    return res[0], [res[1 + 4 * i:5 + 4 * i] for i in range(n_par)]


def _pack_ple_proj(shard):
    return shard.reshape(4, 64, 256).transpose(1, 0, 2).reshape(64, D_MODEL)


class _Reduction:
    def __init__(self, tag, place, ids=(None, None)):
        self.tag, self.place, self.ids = tag, place, ids

    def start(self, partial):
        self.partial = partial
        self.other = _rs_swap_halves(partial, "rs_swap_" + self.tag, self.ids[0])
        return partial

    def middle(self, after):
        self.pre = _rs_add_halves(self.partial, self.other, self.place[1:], "rs_add_" + self.tag, after)
        self.received = _rs_exchange_chips(self.pre, "rs_exchange_" + self.tag, self.ids[1])
        return self.pre

    def finish(self, after):
        return _rs_sum_chips(self.pre, self.received, self.place, "rs_sum_" + self.tag, after)


def _local_grads(x2, p2, tgt, wts, g_attn_norm, g_q, g_k, attn_sinks, rel_bias, w_pool, pool_scale, g_ffn_norm, g_ple_norm,
                 reduce_a):
    w_early, w_late = wts
    w_in = w_out = w_early
    bucket = jnp.asarray(_bucket_table())
    gq = jnp.tile(g_q, (1, 2))
    gk = jnp.tile(g_k, (1, 2))
    wpool = w_pool[0].astype(BF16)
    sinks = attn_sinks[0]
    bias_st = _bias_build(rel_bias.T, bucket)

    hn1, zqk, u, kn, vb, qst = _attn_in(x2, g_attn_norm, gq, gk, w_in)
    ost = _attn_fwd(qst, kn, vb, bias_st, sinks)
    pooled, mix, h1, hn2 = _mix_out(u, ost, x2, w_out, wpool, pool_scale, g_ffn_norm)
    loss_v, dgate, dup, act, dh2, hn3, dgl, dpp, dh1, dg_ffn, dg_ple = _ffn_ple(hn2, h1, p2, tgt, w_late, g_ffn_norm,
                                                                                   g_ple_norm)

    rows_a = SLAB_ROWS - SLAB["inT"][1]
    partial_a = None
    for name, lhs, rhs in (("out", mix, dh1), ("gateT", dgate, hn2), ("upT", dup, hn2), ("down", act, dh2), ("plg", hn3, dgl)):
        partial_a = _dw(lhs, rhs, "dw_" + name, into=(partial_a, rows_a, SLAB[name][0] - SLAB["inT"][1]))
    dw_plp = _dw(p2, dpp, "dw_plp").reshape(4, 64, N_CHIPS, 256).transpose(2, 1, 0, 3).reshape(N_CHIPS, 64, D_MODEL)
    partial_a = reduce_a.start(lax.dynamic_update_slice(partial_a, dw_plp, (0, SLAB["plp"][0] - SLAB["inT"][1], 0)))
    dost, du, dyp, dscale = _mix_out_bwd(dh1, w_out, pooled, wpool, pool_scale, partial_a)
    pre_a = reduce_a.middle(du)
    dqst, dk, dv, dbias, dsink_rows = _attn_bwd(qst, kn, vb, dost, bias_st, sinks, pre_a)
    dz, dx, dg_attn, dgq, dgk = _attn_in_bwd(dqst, zqk, dk, dv, du, x2, dh1, w_in, g_attn_norm, gq, gk)

    partial_b = _dw(dz, hn1, "dw_in").reshape(N_CHIPS, -1, D_MODEL)
    small = _small_pack(dg_attn, dg_ffn, dg_ple, dscale, dgq, dgk, dbias, dsink_rows, bucket, loss_v, _dw_pool(pooled, dyp))
    return dx, partial_b, small


def kernel(x, p, w_in, w_out, g_attn_norm, g_q, g_k, attn_sinks, rel_bias, w_pool, pool_scale, g_ffn_norm, w_gate, w_up, w_down, g_ple_norm, w_ple_gate, w_ple_proj, loss_target, m_w_in, m_w_out, m_g_attn_norm, m_g_q, m_g_k, m_attn_sinks, m_rel_bias, m_w_pool, m_pool_scale, m_g_ffn_norm, m_w_gate, m_w_up, m_w_down, m_g_ple_norm, m_w_ple_gate, m_w_ple_proj, v_w_in, v_w_out, v_g_attn_norm, v_g_q, v_g_k, v_attn_sinks, v_rel_bias, v_w_pool, v_pool_scale, v_g_ffn_norm, v_w_gate, v_w_up, v_w_down, v_g_ple_norm, v_w_ple_gate, v_w_ple_proj):
    core = lax.axis_index("c").astype(jnp.int32).reshape(1)
    me = (2 * lax.axis_index("x") + lax.axis_index("y")).astype(jnp.int32).reshape(1)

    local_parts = [jnp.concatenate(pieces, axis=0).astype(BF16) for pieces in (
        [w_in[0].T, w_out[0]], [w_gate[0].T, w_up[0].T, w_down[0], w_ple_gate[0], _pack_ple_proj(w_ple_proj[0])])]
    wts = [(_ag_weights(local, 0, local.shape[0], name, collective_id), local, me)
           for local, name, collective_id in zip(local_parts, ("ag_early", "ag_late"), (1, 2))]

    place = jnp.concatenate([me, core])
    reduce_a = _Reduction("a", place, ids=(3, 4))
    dx, partial_b, small = _local_grads(x[0], pltpu.with_memory_space_constraint(p[0, 0], pltpu.HBM), loss_target[0], wts, g_attn_norm, g_q, g_k, attn_sinks, rel_bias,
                                        w_pool, pool_scale, g_ffn_norm, g_ple_norm, reduce_a)
    reduce_b = _Reduction("b", place, ids=(6, 7))
    reduce_b.start(partial_b)
    small_all = _small_gather(small, 8)
    summed_a = reduce_a.finish(small)
    pre_b = reduce_b.middle(summed_a)
    grads_a = _rs_finish_rows(summed_a, "rs_finish_a", pre_b)

    def rows(name):
        return grads_a, SLAB[name][0] - SLAB["inT"][1]

    plp_rows = grads_a[SLAB["plp"][0] - SLAB["inT"][1]:]
    big = {
        "w_out": (w_out, m_w_out, v_w_out, rows("out"), False),
        "w_gate": (w_gate, m_w_gate, v_w_gate, rows("gateT"), True),
        "w_up": (w_up, m_w_up, v_w_up, rows("upT"), True),
        "w_down": (w_down, m_w_down, v_w_down, rows("down"), False),
        "w_ple_gate": (w_ple_gate, m_w_ple_gate, v_w_ple_gate, rows("plg"), False),
        "w_ple_proj": (w_ple_proj, m_w_ple_proj, v_w_ple_proj,
                       (plp_rows.reshape(64, 4, 256).transpose(1, 0, 2).reshape(PLE_DIM, PLE_DIM), 0), False),
        "w_in": (w_in, m_w_in, v_w_in, None, True),
    }
    small_params = {
        "g_attn_norm": (g_attn_norm, m_g_attn_norm, v_g_attn_norm), "g_q": (g_q, m_g_q, v_g_q), "g_k": (g_k, m_g_k, v_g_k),
        "attn_sinks": (attn_sinks, m_attn_sinks, v_attn_sinks), "rel_bias": (rel_bias, m_rel_bias, v_rel_bias),
        "w_pool": tuple(a.reshape(512, 128) for a in (w_pool, m_w_pool, v_w_pool)),
        "pool_scale": (pool_scale, m_pool_scale, v_pool_scale), "g_ffn_norm": (g_ffn_norm, m_g_ffn_norm, v_g_ffn_norm),
        "g_ple_norm": (g_ple_norm, m_g_ple_norm, v_g_ple_norm),
    }

    grads, deltas, new_ms, new_vs = {}, {}, {}, {}
    out = None
    for name, (w, m, v, g_src, transposed) in big.items():
        if g_src is None:
            g_src = (_rs_finish_rows(reduce_b.finish(out[-1]), "rs_finish_b", out[-1]), 0)
        view = (lambda a: a.T) if transposed else (lambda a: a)
        out = _adamw(view(w[0]), *g_src, view(m[0]), view(v[0]), "adamw_" + name)
        grads[name], deltas[name], new_ms[name], new_vs[name] = (view(a)[None] for a in out)

    loss, small_out = _adamw_small(small_all, list(small_params.values()))
    for name, (g2, d, nm, nv) in zip(small_params, small_out):
        shape = w_pool.shape if name == "w_pool" else g2.shape
        grads[name], deltas[name], new_ms[name], new_vs[name] = (a.reshape(shape) for a in (g2, d, nm, nv))

    order = ["w_in", "w_out", "g_attn_norm", "g_q", "g_k", "attn_sinks", "rel_bias", "w_pool", "pool_scale", "g_ffn_norm",
             "w_gate", "w_up", "w_down", "g_ple_norm", "w_ple_gate", "w_ple_proj"]
    return (loss.reshape(()), dx[None], *[grads[n] for n in order], *[deltas[n] for n in order],
            *[new_ms[n] for n in order], *[new_vs[n] for n in order])
```

```python
import functools

import numpy as np
import jax
import jax.numpy as jnp
from jax import lax
from jax.experimental import pallas as pl
from jax.experimental.pallas import tpu as pltpu
from jax.experimental.pallas import tpu_sc as plsc

F32 = jnp.float32
BF16 = jnp.bfloat16
MESH = pl.DeviceIdType.MESH

D_MODEL = 1024
HEAD_DIM = 64
N_Q_HEADS = 8
ATTN_WIDTH = 512
KV_WIDTH = 128
POOL_WIDTH = 512
IN_WIDTH = 1280
D_FF = 2816
PLE_DIM = 256
FF_CHUNK = 704
BLOCK = 128
N_BUCKETS = 32
MAX_DISTANCE = 128
POOL_SIZES = (2, 4, 8, 16)
EPS = 1e-6
NEG = -1e30
N_CHIPS = 4
N_DEV = 8

ADAM_LR = 0.001
ADAM_B1 = 0.9
ADAM_B2 = 0.999
ADAM_EPS = 1e-08
ADAM_WD = 0.01
ADAM_STEP = 10

SLAB = {"inT": (0, 320), "out": (320, 256), "gateT": (576, 704), "upT": (1280, 704), "down": (1984, 704),
        "plg": (2688, 256), "plp": (2944, 64)}
SLAB_ROWS = 3008
HALF_ROWS = SLAB_ROWS // 2
GATHER_PARTS = ((0, 576), (576, SLAB_ROWS))
POOL_HALO = 24

SMALL = {"g_attn": 0, "g_ffn": 8, "g_ple": 16, "pool_scale": 24, "g_q": 28, "g_k": 29, "sinks": 30, "loss": 31,
         "rel_bias": 32, "w_pool": 64}
SMALL_ROWS = 576

VMEM_LIMIT_BIG = 60 * 1024 * 1024
VMEM_LIMIT = 48 * 1024 * 1024


def _params(vmem=VMEM_LIMIT, n_axes=1):
    return pltpu.CompilerParams(dimension_semantics=("arbitrary",) * n_axes, vmem_limit_bytes=vmem)


def _dot(a, b, ca, cb):
    return lax.dot_general(a, b, (((ca,), (cb,)), ((), ())), preferred_element_type=F32)


def _full(shape):
    return pl.BlockSpec(shape, lambda i: (0,) * len(shape))


ANY = pl.BlockSpec(memory_space=pl.ANY)
VMEM_WHOLE = pl.BlockSpec(memory_space=pltpu.VMEM)


def _hbm(shape, dtype):
    return pltpu.HBM(tuple(shape), dtype)


W_SPECS = [ANY, ANY, pl.BlockSpec(memory_space=pltpu.SMEM)]


def _load_rows(w_refs, name, dst_ref, sems):
    slab_ref, local_ref, me_ref = w_refs
    off, rows = SLAB[name]
    slab_off = off - max(start for start, _ in GATHER_PARTS if start <= off)
    me = me_ref[0]
    for phase in ("start", "wait"):
        for j in range(N_CHIPS):
            dst = dst_ref.at[pl.ds(j * rows, rows), :]
            theirs = pltpu.make_async_copy(slab_ref.at[j, pl.ds(slab_off, rows), :], dst, sems.at[j])
            own = pltpu.make_async_copy(local_ref.at[pl.ds(slab_off, rows), :], dst, sems.at[j])

            @pl.when(me == j)
            def _():
                getattr(own, phase)()

            @pl.when(me != j)
            def _():
                getattr(theirs, phase)()


def _rms_fwd(x, g):
    r = lax.rsqrt(jnp.mean(x * x, axis=-1, keepdims=True) + EPS)
    return x * r * g


def _rms_bwd(x, g, dy):
    r = lax.rsqrt(jnp.mean(x * x, axis=-1, keepdims=True) + EPS)
    xn = x * r
    dyg = dy * g
    dx = r * (dyg - xn * jnp.mean(dyg * xn, axis=-1, keepdims=True))
    return dx, jnp.sum(dy * xn, axis=0, keepdims=True)


def _half_sum(v, lo):
    s_lo = jnp.sum(jnp.where(lo, v, 0.0), axis=-1, keepdims=True)
    s_hi = jnp.sum(jnp.where(lo, 0.0, v), axis=-1, keepdims=True)
    return jnp.where(lo, s_lo, s_hi)


def _half_sum_mxu(v):
    upper = lax.broadcasted_iota(jnp.int32, (128, 128), 0) < 64
    left = lax.broadcasted_iota(jnp.int32, (128, 128), 1) < 64
    ones = jnp.where(upper == left, 1.0, 0.0).astype(BF16)
    high = v.astype(BF16)
    low = (v - high.astype(F32)).astype(BF16)
    return _dot(high, ones, 1, 0) + _dot(low, ones, 1, 0)


def _pair_norm(zp, g, lo):
    r = lax.rsqrt(_half_sum(zp * zp, lo) * (1.0 / HEAD_DIM) + EPS)
    return zp * r * g


def _pair_norm_bwd(zp, g, dy):
    r = lax.rsqrt(_half_sum_mxu(zp * zp) * (1.0 / HEAD_DIM) + EPS)
    xn = zp * r
    dyg = dy * g
    dx = r * (dyg - xn * (_half_sum_mxu(dyg * xn) * (1.0 / HEAD_DIM)))
    return dx, jnp.sum(dy * xn, axis=0, keepdims=True)


def _to_stacked(pair, group, lo):
    rolled = pltpu.roll(pair, 64, axis=1)
    if group == 0:
        return jnp.where(lo, pair, 0.0), jnp.where(lo, rolled, 0.0)
    return jnp.where(lo, 0.0, rolled), jnp.where(lo, 0.0, pair)


def _from_stacked(even, odd, group, lo):
    if group == 0:
        return jnp.where(lo, even, pltpu.roll(odd, 64, axis=1))
    return jnp.where(lo, pltpu.roll(even, 64, axis=1), odd)


def _sigmoid(v):
    return 1.0 / (1.0 + jnp.exp(-v))


def _pool_counts(tile, n_rows):
    t1 = tile * n_rows + lax.broadcasted_iota(jnp.int32, (n_rows, POOL_WIDTH), 0) + 1
    lane = lax.broadcasted_iota(jnp.int32, (n_rows, POOL_WIDTH), 1)
    win = jnp.where(lane < 128, 2, jnp.where(lane < 256, 4, jnp.where(lane < 384, 8, 16)))
    return jnp.minimum(t1, win).astype(F32)


def _attn_in(x2, g_attn, gq, gk, wts):
    s_len = x2.shape[0]
    t = 512

    def body(x_ref, g_ref, gq_ref, gk_ref, sl_ref, lo_ref, me_ref, hn_ref, zqk_ref, u_ref, kn_ref, v_ref, qst_ref, w_ref, sems):
        @pl.when(pl.program_id(0) == 0)
        def _():
            _load_rows((sl_ref, lo_ref, me_ref), "inT", w_ref, sems)

        hn = _rms_fwd(x_ref[...], g_ref[...]).astype(BF16)
        hn_ref[...] = hn
        z = _dot(hn, w_ref[...], 1, 1)
        zqk_ref[...] = z[:, :640]
        u_ref[...] = z[:, 768:]
        v_ref[...] = z[:, 640:768].astype(BF16)
        lo = lax.broadcasted_iota(jnp.int32, (t, 128), 1) < 64
        kn_ref[...] = _pair_norm(z[:, 512:640], gk_ref[...], lo).astype(BF16)
        for p in range(4):
            qn = _pair_norm(z[:, 128 * p:128 * p + 128], gq_ref[...], lo)
            even, odd = _to_stacked(qn, p // 2, lo)
            qst_ref[2 * p] = even.astype(BF16)
            qst_ref[2 * p + 1] = odd.astype(BF16)

    row = lambda w: pl.BlockSpec((t, w), lambda i: (i, 0))
    return pl.pallas_call(
        body, name="attn_in", grid=(s_len // t,),
        in_specs=[row(D_MODEL), _full((1, D_MODEL)), _full((1, 128)), _full((1, 128))] + W_SPECS,
        out_specs=[row(D_MODEL), row(640), row(POOL_WIDTH), row(128), row(128),
                   pl.BlockSpec((N_Q_HEADS, t, 128), lambda i: (0, i, 0))],
        out_shape=[_hbm((s_len, D_MODEL), BF16), _hbm((s_len, 640), F32), _hbm((s_len, POOL_WIDTH), F32),
                   _hbm((s_len, 128), BF16), _hbm((s_len, 128), BF16), _hbm((N_Q_HEADS, s_len, 128), BF16)],
        scratch_shapes=[pltpu.VMEM((IN_WIDTH, D_MODEL), BF16), pltpu.SemaphoreType.DMA((N_CHIPS,))],
        compiler_params=_params(),
    )(x2, g_attn, gq, gk, *wts)


def _bucket_table():
    i_idx = np.arange(BLOCK)[:, None]
    j_idx = np.arange(2 * BLOCK)[None, :]
    d = BLOCK + i_idx - j_idx
    n = np.maximum(d, 0)
    max_exact = N_BUCKETS // 2
    nf = np.maximum(n, 1).astype(np.float64)
    large = max_exact + (np.log(nf / max_exact) / np.log(MAX_DISTANCE / max_exact) * (N_BUCKETS - max_exact)).astype(np.int64)
    large = np.minimum(large, N_BUCKETS - 1)
    bucket = np.where(n < max_exact, n, large)
    return np.where((d >= 0) & (d < BLOCK), bucket, -1).astype(np.int32)


def _bias_build(rel_bias_t, bucket):
    def body(rb_ref, bucket_ref, out_ref):
        bk = bucket_ref[...]
        for h in range(N_Q_HEADS):
            acc = jnp.full((BLOCK, 2 * BLOCK), NEG, F32)
            for b in range(N_BUCKETS):
                acc = jnp.where(bk == b, rb_ref[h, b], acc)
            out_ref[0, pl.ds(h * BLOCK, BLOCK), :] = acc
            out_ref[1, pl.ds(h * BLOCK, BLOCK), :] = acc
            out_ref[1, pl.ds(h * BLOCK, BLOCK), 0:BLOCK] = jnp.full((BLOCK, BLOCK), NEG, F32)

    return pl.pallas_call(
        body, name="bias_build",
        in_specs=[pl.BlockSpec(memory_space=pltpu.SMEM), VMEM_WHOLE], out_specs=VMEM_WHOLE,
        out_shape=jax.ShapeDtypeStruct((2, N_Q_HEADS * BLOCK, 2 * BLOCK), F32),
    )(rel_bias_t, bucket)


def _head_softmax(s_ref, bias_ref, sink_ref, h):
    rows = pl.ds(pl.multiple_of(h * BLOCK, BLOCK), BLOCK)
    s = s_ref[rows, :] * (HEAD_DIM ** -0.5) + bias_ref[rows, :]
    sink = sink_ref[h]
    m = jnp.maximum(jnp.max(s, axis=-1, keepdims=True), sink)
    p = jnp.exp(s - m)
    e_sink = jnp.exp(sink - m)
    inv = 1.0 / (jnp.sum(p, axis=-1, keepdims=True) + e_sink)
    return rows, p * inv, e_sink * inv


def _attn_specs():
    prev = lambda i: (jnp.maximum(i - 1, 0), 0)
    cur = lambda i: (i, 0)
    stacked = pl.BlockSpec((N_Q_HEADS, BLOCK, 128), lambda i: (0, i, 0))
    kv = [pl.BlockSpec((BLOCK, 128), prev), pl.BlockSpec((BLOCK, 128), cur)]
    consts = [pl.BlockSpec((None, N_Q_HEADS * BLOCK, 2 * BLOCK), lambda i: (jnp.where(i == 0, 1, 0), 0, 0)),
              pl.BlockSpec(memory_space=pltpu.SMEM)]
    return stacked, kv, consts


def _head_lane_mask():
    rows = lax.broadcasted_iota(jnp.int32, (N_Q_HEADS * BLOCK, 128), 0)
    lanes = lax.broadcasted_iota(jnp.int32, (N_Q_HEADS * BLOCK, 128), 1)
    return (rows < 4 * BLOCK) == (lanes < 64)


def _attn_fwd(qst, kn, vb, bias_st, sinks):
    s_len = kn.shape[0]

    def body(q_ref, kp_ref, kc_ref, vp_ref, vc_ref, bias_ref, sink_ref, o_ref, s_ref, p_ref):
        q = q_ref[...].reshape(N_Q_HEADS * BLOCK, 128)
        s_ref[...] = _dot(q, jnp.concatenate([kp_ref[...], kc_ref[...]], axis=0), 1, 1)

        def head(h, carry):
            rows, probs, _ = _head_softmax(s_ref, bias_ref, sink_ref, h)
            p_ref[rows, :] = probs.astype(BF16)
            return carry

        lax.fori_loop(0, N_Q_HEADS, head, 0, unroll=True)
        o = _dot(p_ref[...], jnp.concatenate([vp_ref[...], vc_ref[...]], axis=0), 1, 0)
        o_ref[...] = jnp.where(_head_lane_mask(), o, 0.0).astype(BF16).reshape(N_Q_HEADS, BLOCK, 128)

    stacked, kv, consts = _attn_specs()
    return pl.pallas_call(
        body, name="attn_fwd", grid=(s_len // BLOCK,),
        in_specs=[stacked] + kv + kv + consts, out_specs=stacked,
        out_shape=jax.ShapeDtypeStruct((N_Q_HEADS, s_len, 128), BF16),
        scratch_shapes=[pltpu.VMEM((N_Q_HEADS * BLOCK, 2 * BLOCK), F32), pltpu.VMEM((N_Q_HEADS * BLOCK, 2 * BLOCK), BF16)],
        compiler_params=_params(),
    )(qst, kn, kn, vb, vb, bias_st, sinks)


def _mix_out(u, ost, x2, wts, wpool, pool_scale, g_ffn):
    s_len = x2.shape[0]
    t = 512
    n = t + 16

    def body(u_ref, o_ref, x_ref, sl_ref, lo_ref, me_ref, wp_ref, sc_ref, g_ref, pooled_ref, mix_ref, h1_ref, hn_ref,
             w_ref, ext_ref, st_ref, sems):
        i = pl.program_id(0)

        @pl.when(i == 0)
        def _():
            _load_rows((sl_ref, lo_ref, me_ref), "out", w_ref, sems)
            ext_ref[...] = jnp.zeros_like(ext_ref)
            st_ref[...] = jnp.zeros_like(st_ref)

        u_tile = u_ref[...]
        ext_ref[pl.ds(POOL_HALO, t), :] = u_tile
        st_ref[pl.ds(8, n), :] = ext_ref[pl.ds(8, n), :] + ext_ref[pl.ds(7, n), :]
        st_ref[pl.ds(8, n), 128:] = st_ref[pl.ds(8, n), 128:] + st_ref[pl.ds(6, n), 128:]
        st_ref[pl.ds(8, n), 256:] = st_ref[pl.ds(8, n), 256:] + st_ref[pl.ds(4, n), 256:]
        st_ref[pl.ds(8, n), 384:] = st_ref[pl.ds(8, n), 384:] + st_ref[pl.ds(0, n), 384:]
        ext_ref[pl.ds(0, POOL_HALO), :] = ext_ref[pl.ds(t, POOL_HALO), :]
        pooled = (st_ref[pl.ds(POOL_HALO, t), :] / _pool_counts(i, t) - u_tile).astype(BF16)
        pooled_ref[...] = pooled
        for g in range(4):
            cols = slice(128 * g, 128 * g + 128)
            y = _dot(pooled[:, cols], wp_ref[g], 1, 0) * sc_ref[:, cols]
            mix_ref[:, ATTN_WIDTH + 128 * g:ATTN_WIDTH + 128 * g + 128] = y.astype(BF16)
        lo = lax.broadcasted_iota(jnp.int32, (t, 128), 1) < 64
        for p in range(4):
            a = _from_stacked(o_ref[2 * p].astype(F32), o_ref[2 * p + 1].astype(F32), p // 2, lo)
            mix_ref[:, 128 * p:128 * p + 128] = a.astype(BF16)
        h1 = x_ref[...] + _dot(mix_ref[...], w_ref[...], 1, 0)
        h1_ref[...] = h1
        hn_ref[...] = _rms_fwd(h1, g_ref[...]).astype(BF16)

    row = lambda w: pl.BlockSpec((t, w), lambda i: (i, 0))
    return pl.pallas_call(
        body, name="mix_out", grid=(s_len // t,),
        in_specs=[row(POOL_WIDTH), pl.BlockSpec((N_Q_HEADS, t, 128), lambda i: (0, i, 0)), row(D_MODEL)] + W_SPECS
        + [_full((4, 128, 128)), _full((1, POOL_WIDTH)), _full((1, D_MODEL))],
        out_specs=[row(POOL_WIDTH), row(D_MODEL), row(D_MODEL), row(D_MODEL)],
        out_shape=[_hbm((s_len, POOL_WIDTH), BF16), _hbm((s_len, D_MODEL), BF16), _hbm((s_len, D_MODEL), F32),
                   _hbm((s_len, D_MODEL), BF16)],
        scratch_shapes=[pltpu.VMEM((D_MODEL, D_MODEL), BF16), pltpu.VMEM((t + POOL_HALO, POOL_WIDTH), F32),
                        pltpu.VMEM((t + POOL_HALO, POOL_WIDTH), F32), pltpu.SemaphoreType.DMA((N_CHIPS,))],
        compiler_params=_params(),
    )(u, ost, x2, *wts, wpool, pool_scale, g_ffn)


def _ffn_ple(hn2, h1, p2, tgt, wts, g_ffn, g_ple):
    s_len = h1.shape[0]
    t = 256
    n_tiles = s_len // t

    def body(hn_ref, h1_ref, p_ref, tgt_ref, sl_ref, lo_ref, me_ref, gf_ref, gp_ref,
             loss_ref, dgate_ref, dup_ref, act_ref, dh2b_ref, hn3_ref, dgl_ref, dpp_ref, dh1_ref, dgf_ref, dgp_ref,
             wg_ref, wu_ref, wd_ref, wl_ref, wp_ref, packed_ref, gate_s, up_s, loss_acc, sems):
        i = pl.program_id(0)

        @pl.when(i == 0)
        def _():
            w_refs = (sl_ref, lo_ref, me_ref)
            _load_rows(w_refs, "gateT", wg_ref, sems)
            _load_rows(w_refs, "upT", wu_ref, sems)
            _load_rows(w_refs, "down", wd_ref, sems)
            _load_rows(w_refs, "plg", wl_ref, sems)
            _load_rows(w_refs, "plp", packed_ref, sems)
            for j in range(N_CHIPS):
                for q in range(4):
                    wp_ref[pl.ds(64 * q, 64), 256 * j:256 * j + 256] = packed_ref[pl.ds(64 * j, 64), 256 * q:256 * q + 256]
            loss_acc[...] = jnp.zeros_like(loss_acc)
            dgf_ref[...] = jnp.zeros_like(dgf_ref)
            dgp_ref[...] = jnp.zeros_like(dgp_ref)

        hn = hn_ref[...]
        h1v = h1_ref[...]
        h2 = h1v
        for ch in range(N_CHIPS):
            rows = pl.ds(ch * FF_CHUNK, FF_CHUNK)
            gate = _dot(hn, wg_ref[rows, :], 1, 1)
            up = _dot(hn, wu_ref[rows, :], 1, 1)
            gate_s[ch] = gate
            up_s[ch] = up
            act = (gate * _sigmoid(gate) * up).astype(BF16)
            act_ref[ch] = act
            h2 = h2 + _dot(act, wd_ref[rows, :], 1, 0)
        gp = gp_ref[...]
        hn3 = _rms_fwd(h2, gp).astype(BF16)
        hn3_ref[...] = hn3
        gate2 = _sigmoid(_dot(hn3, wl_ref[...], 1, 0))
        pp = _dot(p_ref[...].astype(BF16), wp_ref[...], 1, 0)
        err = h2 + gate2 * pp - tgt_ref[...]
        loss_acc[...] += jnp.sum(err * err, axis=0, keepdims=True)
        dy = err * (1.0 / D_MODEL)
        dpp_ref[...] = (dy * gate2).astype(BF16)
        dgl = (dy * pp * gate2 * (1.0 - gate2)).astype(BF16)
        dgl_ref[...] = dgl
        dx3, dg3 = _rms_bwd(h2, gp, _dot(dgl, wl_ref[...], 1, 1))
        dh2 = dy + dx3
        dgp_ref[...] += dg3
        dh2b = dh2.astype(BF16)
        dh2b_ref[...] = dh2b
        dhn = jnp.zeros((t, D_MODEL), F32)
        for ch in range(N_CHIPS):
            rows = pl.ds(ch * FF_CHUNK, FF_CHUNK)
            dact = _dot(dh2b, wd_ref[rows, :], 1, 1)
            gate_v = gate_s[ch]
            up_v = up_s[ch]
            sg = _sigmoid(gate_v)
            dup = (dact * (gate_v * sg)).astype(BF16)
            dgate = (dact * up_v * (sg * (1.0 + gate_v * (1.0 - sg)))).astype(BF16)
            dup_ref[ch] = dup
            dgate_ref[ch] = dgate
            dhn = dhn + _dot(dgate, wg_ref[rows, :], 1, 0) + _dot(dup, wu_ref[rows, :], 1, 0)
        dx, dg = _rms_bwd(h1v, gf_ref[...], dhn)
        dh1_ref[...] = dh2 + dx
        dgf_ref[...] += dg

        @pl.when(i == n_tiles - 1)
        def _():
            total = jnp.sum(loss_acc[...], axis=-1, keepdims=True) * (0.5 / D_MODEL)
            loss_ref[...] = jnp.broadcast_to(total, loss_ref.shape)

    row = lambda w: pl.BlockSpec((t, w), lambda i: (i, 0))
    chunked = pl.BlockSpec((N_CHIPS, t, FF_CHUNK), lambda i: (0, i, 0))
    vec = _full((1, D_MODEL))
    act_shape = _hbm((N_CHIPS, s_len, FF_CHUNK), BF16)
    tok = lambda dtype: _hbm((s_len, D_MODEL), dtype)
    return pl.pallas_call(
        body, name="ffn_ple", grid=(n_tiles,),
        in_specs=[row(D_MODEL), row(D_MODEL), row(PLE_DIM), row(D_MODEL)] + W_SPECS + [vec, vec],
        out_specs=[_full((1, 128)), chunked, chunked, chunked] + [row(D_MODEL)] * 5 + [vec, vec],
        out_shape=[jax.ShapeDtypeStruct((1, 128), F32), act_shape, act_shape, act_shape, tok(BF16), tok(BF16), tok(BF16),
                   tok(BF16), tok(F32), jax.ShapeDtypeStruct((1, D_MODEL), F32), jax.ShapeDtypeStruct((1, D_MODEL), F32)],
        scratch_shapes=[pltpu.VMEM((D_FF, D_MODEL), BF16)] * 3
        + [pltpu.VMEM((D_MODEL, D_MODEL), BF16), pltpu.VMEM((PLE_DIM, D_MODEL), BF16), pltpu.VMEM((PLE_DIM, D_MODEL), BF16),
           pltpu.VMEM((N_CHIPS, t, FF_CHUNK), F32), pltpu.VMEM((N_CHIPS, t, FF_CHUNK), F32), pltpu.VMEM((1, D_MODEL), F32),
           pltpu.SemaphoreType.DMA((N_CHIPS,))],
        compiler_params=_params(VMEM_LIMIT_BIG),
    )(hn2, h1, p2, tgt, *wts, g_ffn, g_ple)


def _mix_out_bwd(dh1, wts, pooled, wpool, pool_scale, after):
    s_len = dh1.shape[0]
    t = 512
    n = t + 16
    n_tiles = s_len // t

    def body(dh1_ref, sl_ref, lo_ref, me_ref, pooled_ref, wp_ref, sc_ref, after_ref, dost_ref, du_ref, dyp_ref, dsc_ref,
             w_ref, ext_ref, st_ref, sems):
        del after_ref
        i = pl.program_id(0)

        @pl.when(i == 0)
        def _():
            _load_rows((sl_ref, lo_ref, me_ref), "out", w_ref, sems)
            ext_ref[...] = jnp.zeros_like(ext_ref)
            st_ref[...] = jnp.zeros_like(st_ref)
            dsc_ref[...] = jnp.zeros_like(dsc_ref)

        dmix = _dot(dh1_ref[...].astype(BF16), w_ref[...], 1, 1)
        lo = lax.broadcasted_iota(jnp.int32, (t, 128), 1) < 64
        for p in range(4):
            even, odd = _to_stacked(dmix[:, 128 * p:128 * p + 128], p // 2, lo)
            dost_ref[2 * p] = even.astype(BF16)
            dost_ref[2 * p + 1] = odd.astype(BF16)
        pooled_v = pooled_ref[...]
        counts = _pool_counts(n_tiles - 1 - i, t)
        for g in range(4):
            cols = slice(128 * g, 128 * g + 128)
            dm = dmix[:, ATTN_WIDTH + 128 * g:ATTN_WIDTH + 128 * g + 128]
            ypre = _dot(pooled_v[:, cols], wp_ref[g], 1, 0)
            dsc_ref[:, cols] += jnp.sum(ypre * dm, axis=0, keepdims=True)
            dyp = (dm * sc_ref[:, cols]).astype(BF16)
            dyp_ref[:, cols] = dyp
            dpooled = _dot(dyp, wp_ref[g], 1, 1)
            du_ref[:, cols] = -dpooled
            ext_ref[pl.ds(0, t), cols] = dpooled / counts[:, cols]
        st_ref[pl.ds(0, n), :] = ext_ref[pl.ds(0, n), :] + ext_ref[pl.ds(1, n), :]
        st_ref[pl.ds(0, n), 128:] = st_ref[pl.ds(0, n), 128:] + st_ref[pl.ds(2, n), 128:]
        st_ref[pl.ds(0, n), 256:] = st_ref[pl.ds(0, n), 256:] + st_ref[pl.ds(4, n), 256:]
        st_ref[pl.ds(0, n), 384:] = st_ref[pl.ds(0, n), 384:] + st_ref[pl.ds(8, n), 384:]
        ext_ref[pl.ds(t, POOL_HALO), :] = ext_ref[pl.ds(0, POOL_HALO), :]
        du_ref[...] += st_ref[pl.ds(0, t), :]

    rev = lambda w: pl.BlockSpec((t, w), lambda i: (n_tiles - 1 - i, 0))
    return pl.pallas_call(
        body, name="mix_out_bwd", grid=(n_tiles,),
        in_specs=[rev(D_MODEL)] + W_SPECS + [rev(POOL_WIDTH), _full((4, 128, 128)), _full((1, POOL_WIDTH)), ANY],
        out_specs=[pl.BlockSpec((N_Q_HEADS, t, 128), lambda i: (0, n_tiles - 1 - i, 0)), rev(POOL_WIDTH), rev(POOL_WIDTH),
                   _full((1, POOL_WIDTH))],
        out_shape=[_hbm((N_Q_HEADS, s_len, 128), BF16), _hbm((s_len, POOL_WIDTH), F32), _hbm((s_len, POOL_WIDTH), BF16),
                   jax.ShapeDtypeStruct((1, POOL_WIDTH), F32)],
        scratch_shapes=[pltpu.VMEM((D_MODEL, D_MODEL), BF16), pltpu.VMEM((t + POOL_HALO, POOL_WIDTH), F32),
                        pltpu.VMEM((t + POOL_HALO, POOL_WIDTH), F32), pltpu.SemaphoreType.DMA((N_CHIPS,))],
        compiler_params=_params(),
    )(dh1, *wts, pooled, wpool, pool_scale, after)


def _attn_bwd(qst, kn, vb, dost, bias_st, sinks, after):
    s_len = kn.shape[0]

    def body(q_ref, kp_ref, kc_ref, vp_ref, vc_ref, do_ref, bias_ref, sink_ref, after_ref, dq_ref, dk_ref, dv_ref, dbias_ref,
             dsink_ref, s_ref, dp_ref, p_ref, dl_ref):
        del after_ref
        i = pl.program_id(0)

        @pl.when(i == 0)
        def _():
            dk_ref[...] = jnp.zeros_like(dk_ref)
            dv_ref[...] = jnp.zeros_like(dv_ref)
            dbias_ref[...] = jnp.zeros_like(dbias_ref)
            dsink_ref[...] = jnp.zeros_like(dsink_ref)

        q = q_ref[...].reshape(N_Q_HEADS * BLOCK, 128)
        do = do_ref[...].reshape(N_Q_HEADS * BLOCK, 128)
        k2 = jnp.concatenate([kp_ref[...], kc_ref[...]], axis=0)
        s_ref[...] = _dot(q, k2, 1, 1)
        dp_ref[...] = _dot(do, jnp.concatenate([vp_ref[...], vc_ref[...]], axis=0), 1, 1)

        def head(h, carry):
            rows, probs, p_sink = _head_softmax(s_ref, bias_ref, sink_ref, h)
            dp = dp_ref[rows, :]
            dsum = jnp.sum(probs * dp, axis=-1, keepdims=True)
            dlog = probs * (dp - dsum)
            dsink_ref[rows, :] -= p_sink * dsum
            dbias_ref[rows, :] += dlog
            p_ref[rows, :] = probs.astype(BF16)
            dl_ref[rows, :] = (dlog * (HEAD_DIM ** -0.5)).astype(BF16)
            return carry

        lax.fori_loop(0, N_Q_HEADS, head, 0, unroll=True)
        dlog_s = dl_ref[...]
        dq_ref[...] = jnp.where(_head_lane_mask(), _dot(dlog_s, k2, 1, 0), 0.0).reshape(N_Q_HEADS, BLOCK, 128)
        dk2 = _dot(dlog_s, q, 0, 0)
        dv2 = _dot(p_ref[...], do, 0, 0)
        prev_rows = pl.ds(pl.multiple_of(jnp.maximum(i - 1, 0) * BLOCK, BLOCK), BLOCK)
        cur_rows = pl.ds(pl.multiple_of(i * BLOCK, BLOCK), BLOCK)
        dk_ref[prev_rows, :] += dk2[:BLOCK]
        dk_ref[cur_rows, :] += dk2[BLOCK:]
        dv_ref[prev_rows, :] += dv2[:BLOCK]
        dv_ref[cur_rows, :] += dv2[BLOCK:]

    stacked, kv, consts = _attn_specs()
    band = (N_Q_HEADS * BLOCK, 2 * BLOCK)
    return pl.pallas_call(
        body, name="attn_bwd", grid=(s_len // BLOCK,),
        in_specs=[stacked] + kv + kv + [stacked] + consts + [ANY],
        out_specs=[stacked, _full((s_len, 128)), _full((s_len, 128)), _full(band), _full((N_Q_HEADS * BLOCK, 1))],
        out_shape=[_hbm((N_Q_HEADS, s_len, 128), F32), _hbm((s_len, 128), F32), _hbm((s_len, 128), F32), _hbm(band, F32),
                   _hbm((N_Q_HEADS * BLOCK, 1), F32)],
        scratch_shapes=[pltpu.VMEM(band, F32), pltpu.VMEM(band, F32), pltpu.VMEM(band, BF16), pltpu.VMEM(band, BF16)],
        compiler_params=_params(),
    )(qst, kn, kn, vb, vb, dost, bias_st, sinks, after)


def _small_pack(dg_attn, dg_ffn, dg_ple, dscale, dgq, dgk, dbias, dsink_rows, bucket, loss_v, dwpool):
    def body(ga_ref, gf_ref, gp_ref, sc_ref, gq_ref, gk_ref, db_ref, ds_ref, bucket_ref, loss_ref, wp_ref, out_ref):
        out_ref[pl.ds(0, SMALL["w_pool"]), :] = jnp.zeros((SMALL["w_pool"], 128), F32)
        for name, ref, n in (("g_attn", ga_ref, 8), ("g_ffn", gf_ref, 8), ("g_ple", gp_ref, 8), ("pool_scale", sc_ref, 4)):
            for k in range(n):
                out_ref[pl.ds(SMALL[name] + k, 1), :] = ref[:, 128 * k:128 * k + 128]
        for name, ref in (("g_q", gq_ref), ("g_k", gk_ref)):
            both = ref[...]
            out_ref[pl.ds(SMALL[name], 1), :] = both + pltpu.roll(both, 64, axis=1)
        out_ref[pl.ds(SMALL["loss"], 1), :] = loss_ref[...]
        bk = bucket_ref[...]
        rows = lax.broadcasted_iota(jnp.int32, (N_BUCKETS, 128), 0)
        lanes = lax.broadcasted_iota(jnp.int32, (N_BUCKETS, 128), 1)
        lane1 = lax.broadcasted_iota(jnp.int32, (1, 128), 1)
        rb = jnp.zeros((N_BUCKETS, 128), F32)
        sk = jnp.zeros((1, 128), F32)
        for h in range(N_Q_HEADS):
            band = db_ref[pl.ds(h * BLOCK, BLOCK), :]
            for b in range(N_BUCKETS):
                rb = jnp.where((rows == b) & (lanes == h), jnp.sum(jnp.where(bk == b, band, 0.0)), rb)
            sk = jnp.where(lane1 == h, jnp.sum(ds_ref[pl.ds(h * BLOCK, BLOCK), :]), sk)
        out_ref[pl.ds(SMALL["rel_bias"], N_BUCKETS), :] = rb
        out_ref[pl.ds(SMALL["sinks"], 1), :] = sk
        out_ref[pl.ds(SMALL["w_pool"], 512), :] = wp_ref[...].reshape(512, 128)

    return pl.pallas_call(
        body, name="small_pack", in_specs=[VMEM_WHOLE] * 11, out_specs=VMEM_WHOLE,
        out_shape=jax.ShapeDtypeStruct((SMALL_ROWS, 128), F32),
    )(dg_attn, dg_ffn, dg_ple, dscale, dgq, dgk, dbias, dsink_rows, bucket, loss_v, dwpool)


def _attn_in_bwd(dqst, zqk, dk, dv, du, x2, dh1, wts, g_attn, gq, gk):
    s_len = x2.shape[0]
    t = 512

    def body(dq_ref, zqk_ref, dk_ref, dv_ref, du_ref, x_ref, dh1_ref, sl_ref, lo_ref, me_ref, g_ref, gq_ref, gk_ref,
             dz_ref, dx_ref, dg_ref, dgq_ref, dgk_ref, w_ref, sems):
        @pl.when(pl.program_id(0) == 0)
        def _():
            _load_rows((sl_ref, lo_ref, me_ref), "inT", w_ref, sems)
            dg_ref[...] = jnp.zeros_like(dg_ref)
            dgq_ref[...] = jnp.zeros_like(dgq_ref)
            dgk_ref[...] = jnp.zeros_like(dgk_ref)

        lo = lax.broadcasted_iota(jnp.int32, (t, 128), 1) < 64
        for p in range(4):
            dqn = _from_stacked(dq_ref[2 * p], dq_ref[2 * p + 1], p // 2, lo)
            dq_raw, dgq = _pair_norm_bwd(zqk_ref[:, 128 * p:128 * p + 128], gq_ref[...], dqn)
            dz_ref[:, 128 * p:128 * p + 128] = dq_raw.astype(BF16)
            dgq_ref[...] += dgq
        dk_raw, dgk = _pair_norm_bwd(zqk_ref[:, 512:640], gk_ref[...], dk_ref[...])
        dgk_ref[...] += dgk
        dz_ref[:, 512:640] = dk_raw.astype(BF16)
        dz_ref[:, 640:768] = dv_ref[...].astype(BF16)
        dz_ref[:, 768:] = du_ref[...].astype(BF16)
        dx, dg = _rms_bwd(x_ref[...], g_ref[...], _dot(dz_ref[...], w_ref[...], 1, 0))
        dx_ref[...] = dh1_ref[...] + dx
        dg_ref[...] += dg

    row = lambda w: pl.BlockSpec((t, w), lambda i: (i, 0))
    return pl.pallas_call(
        body, name="attn_in_bwd", grid=(s_len // t,),
        in_specs=[pl.BlockSpec((N_Q_HEADS, t, 128), lambda i: (0, i, 0)), row(640), row(128), row(128), row(POOL_WIDTH),
                  row(D_MODEL), row(D_MODEL)] + W_SPECS + [_full((1, D_MODEL)), _full((1, 128)), _full((1, 128))],
        out_specs=[row(IN_WIDTH), row(D_MODEL), _full((1, D_MODEL)), _full((1, 128)), _full((1, 128))],
        out_shape=[jax.ShapeDtypeStruct((s_len, IN_WIDTH), BF16), jax.ShapeDtypeStruct((s_len, D_MODEL), F32),
                   jax.ShapeDtypeStruct((1, D_MODEL), F32), jax.ShapeDtypeStruct((1, 128), F32),
                   jax.ShapeDtypeStruct((1, 128), F32)],
        scratch_shapes=[pltpu.VMEM((IN_WIDTH, D_MODEL), BF16), pltpu.SemaphoreType.DMA((N_CHIPS,))],
        compiler_params=_params(),
    )(dqst, zqk, dk, dv, du, x2, dh1, *wts, g_attn, gq, gk)


def _dw(a, b, name, into=None):
    tk = 1024
    n_out = b.shape[1]
    if a.ndim == 3:
        s_len, tm = a.shape[1:]
        m = N_CHIPS * tm
        a_spec = pl.BlockSpec((None, tk, tm), lambda i, k: (i, k, 0))
    else:
        s_len, m = a.shape
        tm = m // 2 if m > 1408 else m
        a_spec = pl.BlockSpec((tk, tm), lambda i, k: (k, i))
    n_steps = s_len // tk
    chunk = m // N_CHIPS
    per_tile = tm // chunk

    def accumulate(a_ref, b_ref, acc_ref, k):
        @pl.when(k == 0)
        def _():
            acc_ref[...] = _dot(a_ref[...].astype(BF16), b_ref[...].astype(BF16), 0, 0)

        @pl.when(k > 0)
        def _():
            acc_ref[...] += _dot(a_ref[...].astype(BF16), b_ref[...].astype(BF16), 0, 0)

    in_specs = [a_spec, pl.BlockSpec((tk, n_out), lambda i, k: (k, 0))]
    if into is None:
        def body(a_ref, b_ref, o_ref, acc_ref):
            k = pl.program_id(1)
            accumulate(a_ref, b_ref, acc_ref, k)

            @pl.when(k == n_steps - 1)
            def _():
                o_ref[...] = acc_ref[...].astype(BF16)

        return pl.pallas_call(
            body, name=name, grid=(m // tm, n_steps), in_specs=in_specs,
            out_specs=pl.BlockSpec((tm, n_out), lambda i, k: (i, 0)), out_shape=_hbm((m, n_out), BF16),
            scratch_shapes=[pltpu.VMEM((tm, n_out), F32)], compiler_params=_params(n_axes=2),
        )(a, b)

    slab, slab_rows, row_off = into
    assert n_out == D_MODEL

    n_tiles = m // tm

    def body_into(a_ref, b_ref, *rest):
        o_ref, acc_ref, stage_ref, sems = rest[-4:]
        i, k = pl.program_id(0), pl.program_id(1)
        accumulate(a_ref, b_ref, acc_ref, k)

        def out_copies(tile, slot):
            return [pltpu.make_async_copy(stage_ref.at[slot, pl.ds(jj * chunk, chunk), :],
                                          o_ref.at[tile * per_tile + jj, pl.ds(row_off, chunk), :], sems.at[slot, jj])
                    for jj in range(per_tile)]

        @pl.when(k == n_steps - 1)
        def _():
            slot = i % 2

            @pl.when(i >= 2)
            def _():
                for cp in out_copies(i - 2, slot):
                    cp.wait()

            stage_ref[slot] = acc_ref[...].astype(BF16)
            for cp in out_copies(i, slot):
                cp.start()

            @pl.when(i == n_tiles - 1)
            def _():
                for cp in out_copies(i, slot):
                    cp.wait()
                if n_tiles > 1:
                    for cp in out_copies(i - 1, 1 - slot):
                        cp.wait()

    operands, aliases = [a, b], {}
    if slab is not None:
        in_specs = in_specs + [ANY]
        operands.append(slab)
        aliases = {2: 0}
    return pl.pallas_call(
        body_into, name=name, grid=(n_tiles, n_steps), in_specs=in_specs, out_specs=ANY,
        out_shape=jax.ShapeDtypeStruct((N_CHIPS, slab_rows, D_MODEL), BF16), input_output_aliases=aliases,
        scratch_shapes=[pltpu.VMEM((tm, n_out), F32), pltpu.VMEM((2, tm, n_out), BF16),
                        pltpu.SemaphoreType.DMA((2, per_tile))],
        compiler_params=_params(n_axes=2),
    )(*operands)


def _dw_pool(pooled, dyp):
    s_len = pooled.shape[0]
    tk = 512

    def body(a_ref, b_ref, o_ref):
        @pl.when(pl.program_id(0) == 0)
        def _():
            o_ref[...] = jnp.zeros_like(o_ref)

        for g in range(4):
            cols = slice(128 * g, 128 * g + 128)
            o_ref[g] += _dot(a_ref[:, cols], b_ref[:, cols], 0, 0)

    blk = pl.BlockSpec((tk, POOL_WIDTH), lambda k: (k, 0))
    return pl.pallas_call(
        body, name="dw_pool", grid=(s_len // tk,), in_specs=[blk, blk], out_specs=_full((4, 128, 128)),
        out_shape=jax.ShapeDtypeStruct((4, 128, 128), F32), compiler_params=_params(),
    )(pooled, dyp)


def _position():
    x, y, c = lax.axis_index("x"), lax.axis_index("y"), lax.axis_index("c")
    other_chips = [(1 - x, y), (x, 1 - y), (1 - x, 1 - y)]
    return x, y, c, other_chips


def _ag_weights(local_slab, row0, n_rows, name, collective_id):
    half = n_rows // 2
    quarter = half // 2
    assert quarter % 16 == 0

    def body(l_ref, g_ref, send, recv):
        x, y, c, chips = _position()
        me, (via_x, via_y, diagonal) = 2 * x + y, [2 * chip[0] + chip[1] for chip in chips]
        here, sibling, x_nbr, y_nbr = (x, y, c), (x, y, 1 - c), (1 - x, y, c), (x, 1 - y, c)
        peers = [sibling, x_nbr, y_nbr]
        barrier = pltpu.get_barrier_semaphore()
        for peer in peers:
            pl.semaphore_signal(barrier, inc=1, device_id=peer, device_id_type=MESH)
        pl.semaphore_wait(barrier, len(peers))

        def rows(core, part):
            start, size = (core * half, half) if part is None else (core * half + part * quarter, quarter)
            return pl.ds(pl.multiple_of(start, 16), size)

        def copy(k, chip_idx, where, to, src=None):
            dst = g_ref.at[chip_idx, where, :]
            return pltpu.make_async_remote_copy(src_ref=dst if src is None else src, dst_ref=dst, send_sem=send.at[k],
                                                recv_sem=recv.at[k], device_id=to, device_id_type=MESH)

        own_rows = l_ref.at[pl.ds(pl.multiple_of(row0 + c * half, 16), half), :]
        started = [copy(0, me, rows(c, None), x_nbr, src=own_rows), copy(1, me, rows(c, None), y_nbr, src=own_rows)]
        for cp in started:
            cp.start()
        after_arrival = [
            (copy(0, via_x, rows(c, None), here), [copy(4, via_x, rows(c, None), sibling), copy(3, via_x, rows(c, 1), y_nbr)]),
            (copy(1, via_y, rows(c, None), here), [copy(5, via_y, rows(c, None), sibling), copy(2, via_y, rows(c, 0), x_nbr)]),
            (copy(2, diagonal, rows(c, 0), here), [copy(6, diagonal, rows(c, 0), sibling)]),
            (copy(3, diagonal, rows(c, 1), here), [copy(7, diagonal, rows(c, 1), sibling)]),
        ]
        for arrival, onward in after_arrival:
            arrival.wait_recv()
            for cp in onward:
                cp.start()
            started += onward
        for cp in (copy(4, via_x, rows(1 - c, None), here), copy(5, via_y, rows(1 - c, None), here),
                   copy(6, diagonal, rows(1 - c, 0), here), copy(7, diagonal, rows(1 - c, 1), here)):
            cp.wait_recv()
        for cp in started:
            cp.wait_send()

    return pl.kernel(
        body, out_type=jax.ShapeDtypeStruct((N_CHIPS, n_rows, D_MODEL), BF16),
        mesh=plsc.ScalarSubcoreMesh(axis_name="sequencer", num_cores=1), name=name,
        scratch_types=[pltpu.SemaphoreType.DMA((8,)), pltpu.SemaphoreType.DMA((8,))],
        compiler_params=pltpu.CompilerParams(collective_id=collective_id),
    )(local_slab)


def _comm_call(body, peers_of, out_shape, n_sems, operand, name, collective_id):
    sems = [pltpu.SemaphoreType.DMA((n_sems,)), pltpu.SemaphoreType.DMA((n_sems,))]
    if collective_id is None:
        return pl.pallas_call(body, name=name, in_specs=[ANY], out_specs=ANY, out_shape=out_shape, scratch_shapes=sems)(operand)

    def with_handshake(in_ref, out_ref, send, recv):
        x, y, c, _ = _position()
        peers = peers_of(x, y, c)
        barrier = pltpu.get_barrier_semaphore()
        for peer in peers:
            pl.semaphore_signal(barrier, inc=1, device_id=peer, device_id_type=MESH)
        pl.semaphore_wait(barrier, len(peers))
        body(in_ref, out_ref, send, recv)

    return pl.kernel(with_handshake, out_type=out_shape, mesh=plsc.ScalarSubcoreMesh(axis_name="sequencer", num_cores=1),
                     name=name, scratch_types=sems, compiler_params=pltpu.CompilerParams(collective_id=collective_id))(operand)


def _rs_swap_halves(partial, name, collective_id=None):
    half = partial.shape[1] // 2

    def body(p_ref, r_ref, send, recv):
        x, y, c, _ = _position()
        theirs = pl.ds(pl.multiple_of((1 - c) * half, 16), half)
        cp = pltpu.make_async_remote_copy(src_ref=p_ref.at[:, theirs, :], dst_ref=r_ref, send_sem=send.at[0],
                                          recv_sem=recv.at[0], device_id=(x, y, 1 - c), device_id_type=MESH)
        cp.start()
        cp.wait()

    return _comm_call(body, lambda x, y, c: [(x, y, 1 - c)], jax.ShapeDtypeStruct((N_CHIPS, half, D_MODEL), BF16), 1,
                      partial, name, collective_id)


def _rs_add_halves(partial, other, core, name, after):
    half = other.shape[1]
    t = half // 2
    steps = half // t

    def body(core_ref, a_ref, b_ref, after_ref, o_ref):
        del after_ref
        o_ref[...] = (a_ref[...].astype(F32) + b_ref[...].astype(F32)).astype(BF16)

    return pl.pallas_call(
        body, name=name,
        grid_spec=pltpu.PrefetchScalarGridSpec(
            num_scalar_prefetch=1, grid=(N_CHIPS, steps),
            in_specs=[pl.BlockSpec((1, t, D_MODEL), lambda j, i, core_ref: (j, core_ref[0] * steps + i, 0)),
                      pl.BlockSpec((1, t, D_MODEL), lambda j, i, core_ref: (j, i, 0)), ANY],
            out_specs=pl.BlockSpec((1, t, D_MODEL), lambda j, i, core_ref: (j, i, 0))),
        out_shape=jax.ShapeDtypeStruct((N_CHIPS, half, D_MODEL), BF16),
        compiler_params=_params(n_axes=2),
    )(core, partial, other, after)


def _rs_exchange_chips(pre, name, collective_id=None):
    def body(s_ref, r_ref, send, recv):
        x, y, c, chips = _position()

        def copy(k, chunk, to):
            return pltpu.make_async_remote_copy(src_ref=s_ref.at[chunk], dst_ref=r_ref.at[k], send_sem=send.at[k],
                                                recv_sem=recv.at[k], device_id=to, device_id_type=MESH)

        sends = [copy(k, 2 * chip[0] + chip[1], (*chip, c)) for k, chip in enumerate(chips)]
        for cp in sends:
            cp.start()
        for cp in sends:
            cp.wait()

    return _comm_call(body, lambda x, y, c: [(1 - x, y, c), (x, 1 - y, c), (1 - x, 1 - y, c)],
                      jax.ShapeDtypeStruct((3, pre.shape[1], D_MODEL), BF16), 3, pre, name, collective_id)


def _rs_sum_chips(pre, received, place, name, after):
    half = pre.shape[1]
    t = half // 2 if half > 512 else half
    steps = half // t

    def body(place_ref, own_ref, r_ref, after_ref, o_ref):
        del after_ref
        acc = own_ref[0].astype(F32)
        for k in range(3):
            acc = acc + r_ref[k].astype(F32)
        o_ref[...] = acc

    return pl.pallas_call(
        body, name=name,
        grid_spec=pltpu.PrefetchScalarGridSpec(
            num_scalar_prefetch=1, grid=(steps,),
            in_specs=[pl.BlockSpec((1, t, D_MODEL), lambda i, place_ref: (place_ref[0], i, 0)),
                      pl.BlockSpec((3, t, D_MODEL), lambda i, place_ref: (0, i, 0)), ANY],
            out_specs=pl.BlockSpec((t, D_MODEL), lambda i, place_ref: (place_ref[1] * steps + i, 0))),
        out_shape=jax.ShapeDtypeStruct((2 * half, D_MODEL), F32),
        compiler_params=_params(),
    )(place, pre, received, after)


def _half_swap(g_ref, core, to, send, recv, k):
    half = g_ref.shape[0] // 2
    rows = g_ref.at[pl.ds(pl.multiple_of(core * half, 8), half), :]
    return pltpu.make_async_remote_copy(src_ref=rows, dst_ref=rows, send_sem=send.at[k], recv_sem=recv.at[k],
                                        device_id=to, device_id_type=MESH)


def _rs_finish_rows(grads, name, after):
    def body(f_ref, after_ref, g_ref, send, recv):
        del f_ref, after_ref
        x, y, c, _ = _position()
        mine = _half_swap(g_ref, c, (x, y, 1 - c), send, recv, 0)
        mine.start()
        _half_swap(g_ref, 1 - c, (x, y, c), send, recv, 0).wait_recv()
        mine.wait_send()

    return pl.pallas_call(
        body, name=name, in_specs=[ANY, ANY], out_specs=ANY, input_output_aliases={0: 0},
        out_shape=jax.ShapeDtypeStruct(grads.shape, F32),
        scratch_shapes=[pltpu.SemaphoreType.DMA((1,)), pltpu.SemaphoreType.DMA((1,))],
    )(grads, after)


def _small_gather(small, collective_id):
    def body(s_ref, t_ref, send, recv):
        x, y, c, chips = _position()
        sibling = (x, y, 1 - c)

        def slot(px, py, pc):
            return t_ref.at[4 * px + 2 * py + pc]

        def copy(k, block, to, src=None):
            return pltpu.make_async_remote_copy(src_ref=slot(*block) if src is None else src, dst_ref=slot(*block),
                                                send_sem=send.at[k], recv_sem=recv.at[k], device_id=to, device_id_type=MESH)

        own = pltpu.make_async_copy(s_ref, slot(x, y, c), send.at[7])
        own.start()
        first = [copy(0, (x, y, c), sibling, src=s_ref)]
        first += [copy(1 + k, (x, y, c), (*chip, c), src=s_ref) for k, chip in enumerate(chips)]
        for cp in first:
            cp.start()
        passed = []
        for k, chip in enumerate(chips):
            copy(1 + k, (*chip, c), (x, y, c)).wait_recv()
            fwd = copy(4 + k, (*chip, c), sibling)
            fwd.start()
            passed.append(fwd)
        copy(0, sibling, (x, y, c)).wait_recv()
        for k, chip in enumerate(chips):
            copy(4 + k, (*chip, 1 - c), (x, y, c)).wait_recv()
        for cp in first + passed:
            cp.wait_send()
        own.wait()

    peers_of = lambda x, y, c: [(x, y, 1 - c), (1 - x, y, c), (x, 1 - y, c), (1 - x, 1 - y, c)]
    return _comm_call(body, peers_of, jax.ShapeDtypeStruct((N_DEV, SMALL_ROWS, 128), F32), 8, small, "small_gather",
                      collective_id)


def _adam_update(w, g, m, v):
    m_new = ADAM_B1 * m + (1.0 - ADAM_B1) * g
    v_new = ADAM_B2 * v + (1.0 - ADAM_B2) * (g * g)
    m_hat = m_new / (1.0 - ADAM_B1 ** ADAM_STEP)
    v_hat = v_new / (1.0 - ADAM_B2 ** ADAM_STEP)
    return -ADAM_LR * (m_hat / (jnp.sqrt(v_hat) + ADAM_EPS) + ADAM_WD * w), m_new, v_new


def _adamw(w, g_rows, row_off, m, v, name):
    rows, cols = w.shape
    t = rows if rows <= 320 else (rows // 2 if rows % 256 else 256)

    def body(w_ref, g_ref, m_ref, v_ref, go_ref, d_ref, nm_ref, nv_ref):
        g = g_ref[...]
        go_ref[...] = g
        d_ref[...], nm_ref[...], nv_ref[...] = _adam_update(w_ref[...], g, m_ref[...], v_ref[...])

    blk = pl.BlockSpec((t, cols), lambda i: (i, 0))
    assert row_off % 8 == 0 and t % 8 == 0
    g_blk = pl.BlockSpec((pl.Element(t), pl.Element(cols)), lambda i: (pl.multiple_of(row_off + i * t, 8), 0))
    shape = jax.ShapeDtypeStruct((rows, cols), F32)
    return pl.pallas_call(
        body, name=name, grid=(rows // t,), in_specs=[blk, g_blk, blk, blk], out_specs=[blk] * 4, out_shape=[shape] * 4,
        compiler_params=_params(),
    )(w, g_rows, m, v)


SMALL_PARAMS = [("g_attn", (1, D_MODEL), 8), ("g_q", (1, HEAD_DIM), None), ("g_k", (1, HEAD_DIM), None),
                ("sinks", (1, N_Q_HEADS), None), ("rel_bias", (N_BUCKETS, N_Q_HEADS), None), ("w_pool", (512, 128), None),
                ("pool_scale", (1, POOL_WIDTH), 4), ("g_ffn", (1, D_MODEL), 8), ("g_ple", (1, D_MODEL), 8)]


def _adamw_small(tables, wmv):
    n_par = len(SMALL_PARAMS)

    def body(*refs):
        t_ref = refs[0]
        ins = refs[1:1 + 3 * n_par]
        loss_ref = refs[1 + 3 * n_par]
        outs = refs[2 + 3 * n_par:-1]
        tot_ref = refs[-1]
        total = t_ref[0]
        for d in range(1, N_DEV):
            total = total + t_ref[d]
        tot_ref[...] = total
        loss_ref[...] = tot_ref[pl.ds(SMALL["loss"], 1), 0:1]
        for i, (name, shape, split) in enumerate(SMALL_PARAMS):
            g_ref, d_ref, nm_ref, nv_ref = outs[4 * i:4 * i + 4]
            row = SMALL[name]
            if split:
                for k in range(split):
                    g_ref[:, 128 * k:128 * k + 128] = tot_ref[pl.ds(row + k, 1), :]
            else:
                g_ref[...] = tot_ref[pl.ds(row, shape[0]), 0:shape[1]]
            w_ref, m_ref, v_ref = ins[3 * i:3 * i + 3]
            d_ref[...], nm_ref[...], nv_ref[...] = _adam_update(w_ref[...], g_ref[...], m_ref[...], v_ref[...])

    shapes = [jax.ShapeDtypeStruct((1, 1), F32)]
    for _, shape, _ in SMALL_PARAMS:
        shapes += [jax.ShapeDtypeStruct(shape, F32)] * 4
    flat = [a for triple in wmv for a in triple]
    res = pl.pallas_call(
        body, name="adamw_small", in_specs=[VMEM_WHOLE] * (1 + 3 * n_par), out_specs=[VMEM_WHOLE] * len(shapes),
        out_shape=shapes, scratch_shapes=[pltpu.VMEM((SMALL_ROWS, 128), F32)],
    )(tables, *flat)
    return res[0], [res[1 + 4 * i:5 + 4 * i] for i in range(n_par)]


def _pack_ple_proj(shard):
    return shard.reshape(4, 64, 256).transpose(1, 0, 2).reshape(64, D_MODEL)


class _Reduction:
    def __init__(self, tag, place, ids=(None, None)):
        self.tag, self.place, self.ids = tag, place, ids

    def start(self, partial):
        self.partial = partial
        self.other = _rs_swap_halves(partial, "rs_swap_" + self.tag, self.ids[0])
        return partial

    def middle(self, after):
        self.pre = _rs_add_halves(self.partial, self.other, self.place[1:], "rs_add_" + self.tag, after)
        self.received = _rs_exchange_chips(self.pre, "rs_exchange_" + self.tag, self.ids[1])
        return self.pre

    def finish(self, after):
        return _rs_sum_chips(self.pre, self.received, self.place, "rs_sum_" + self.tag, after)


def _local_grads(x2, p2, tgt, wts, g_attn_norm, g_q, g_k, attn_sinks, rel_bias, w_pool, pool_scale, g_ffn_norm, g_ple_norm,
                 reduce_a):
    w_early, w_late = wts
    w_in = w_out = w_early
    bucket = jnp.asarray(_bucket_table())
    gq = jnp.tile(g_q, (1, 2))
    gk = jnp.tile(g_k, (1, 2))
    wpool = w_pool[0].astype(BF16)
    sinks = attn_sinks[0]
    bias_st = _bias_build(rel_bias.T, bucket)

    hn1, zqk, u, kn, vb, qst = _attn_in(x2, g_attn_norm, gq, gk, w_in)
    ost = _attn_fwd(qst, kn, vb, bias_st, sinks)
    pooled, mix, h1, hn2 = _mix_out(u, ost, x2, w_out, wpool, pool_scale, g_ffn_norm)
    loss_v, dgate, dup, act, dh2, hn3, dgl, dpp, dh1, dg_ffn, dg_ple = _ffn_ple(hn2, h1, p2, tgt, w_late, g_ffn_norm,
                                                                                   g_ple_norm)

    rows_a = SLAB_ROWS - SLAB["inT"][1]
    partial_a = None
    for name, lhs, rhs in (("out", mix, dh1), ("gateT", dgate, hn2), ("upT", dup, hn2), ("down", act, dh2), ("plg", hn3, dgl)):
        partial_a = _dw(lhs, rhs, "dw_" + name, into=(partial_a, rows_a, SLAB[name][0] - SLAB["inT"][1]))
    dw_plp = _dw(p2, dpp, "dw_plp").reshape(4, 64, N_CHIPS, 256).transpose(2, 1, 0, 3).reshape(N_CHIPS, 64, D_MODEL)
    partial_a = reduce_a.start(lax.dynamic_update_slice(partial_a, dw_plp, (0, SLAB["plp"][0] - SLAB["inT"][1], 0)))
    dost, du, dyp, dscale = _mix_out_bwd(dh1, w_out, pooled, wpool, pool_scale, partial_a)
    pre_a = reduce_a.middle(du)
    dqst, dk, dv, dbias, dsink_rows = _attn_bwd(qst, kn, vb, dost, bias_st, sinks, pre_a)
    dz, dx, dg_attn, dgq, dgk = _attn_in_bwd(dqst, zqk, dk, dv, du, x2, dh1, w_in, g_attn_norm, gq, gk)

    partial_b = _dw(dz, hn1, "dw_in").reshape(N_CHIPS, -1, D_MODEL)
    small = _small_pack(dg_attn, dg_ffn, dg_ple, dscale, dgq, dgk, dbias, dsink_rows, bucket, loss_v, _dw_pool(pooled, dyp))
    return dx, partial_b, small


def kernel(x, p, w_in, w_out, g_attn_norm, g_q, g_k, attn_sinks, rel_bias, w_pool, pool_scale, g_ffn_norm, w_gate, w_up, w_down, g_ple_norm, w_ple_gate, w_ple_proj, loss_target, m_w_in, m_w_out, m_g_attn_norm, m_g_q, m_g_k, m_attn_sinks, m_rel_bias, m_w_pool, m_pool_scale, m_g_ffn_norm, m_w_gate, m_w_up, m_w_down, m_g_ple_norm, m_w_ple_gate, m_w_ple_proj, v_w_in, v_w_out, v_g_attn_norm, v_g_q, v_g_k, v_attn_sinks, v_rel_bias, v_w_pool, v_pool_scale, v_g_ffn_norm, v_w_gate, v_w_up, v_w_down, v_g_ple_norm, v_w_ple_gate, v_w_ple_proj):
    core = lax.axis_index("c").astype(jnp.int32).reshape(1)
    me = (2 * lax.axis_index("x") + lax.axis_index("y")).astype(jnp.int32).reshape(1)

    local_parts = [jnp.concatenate(pieces, axis=0).astype(BF16) for pieces in (
        [w_in[0].T, w_out[0]], [w_gate[0].T, w_up[0].T, w_down[0], w_ple_gate[0], _pack_ple_proj(w_ple_proj[0])])]
    wts = [(_ag_weights(local, 0, local.shape[0], name, collective_id), local, me)
           for local, name, collective_id in zip(local_parts, ("ag_early", "ag_late"), (1, 2))]

    place = jnp.concatenate([me, core])
    reduce_a = _Reduction("a", place, ids=(3, 4))
    dx, partial_b, small = _local_grads(x[0], pltpu.with_memory_space_constraint(p[0, 0], pltpu.HBM), loss_target[0], wts, g_attn_norm, g_q, g_k, attn_sinks, rel_bias,
                                        w_pool, pool_scale, g_ffn_norm, g_ple_norm, reduce_a)
    reduce_b = _Reduction("b", place, ids=(6, 7))
    reduce_b.start(partial_b)
    small_all = _small_gather(small, 8)
    summed_a = reduce_a.finish(small)
    pre_b = reduce_b.middle(summed_a)
    grads_a = _rs_finish_rows(summed_a, "rs_finish_a", pre_b)

    def rows(name):
        return grads_a, SLAB[name][0] - SLAB["inT"][1]

    plp_rows = grads_a[SLAB["plp"][0] - SLAB["inT"][1]:]
    big = {
        "w_out": (w_out, m_w_out, v_w_out, rows("out"), False),
        "w_gate": (w_gate, m_w_gate, v_w_gate, rows("gateT"), True),
        "w_up": (w_up, m_w_up, v_w_up, rows("upT"), True),
        "w_down": (w_down, m_w_down, v_w_down, rows("down"), False),
        "w_ple_gate": (w_ple_gate, m_w_ple_gate, v_w_ple_gate, rows("plg"), False),
        "w_ple_proj": (w_ple_proj, m_w_ple_proj, v_w_ple_proj,
                       (plp_rows.reshape(64, 4, 256).transpose(1, 0, 2).reshape(PLE_DIM, PLE_DIM), 0), False),
        "w_in": (w_in, m_w_in, v_w_in, None, True),
    }
    small_params = {
        "g_attn_norm": (g_attn_norm, m_g_attn_norm, v_g_attn_norm), "g_q": (g_q, m_g_q, v_g_q), "g_k": (g_k, m_g_k, v_g_k),
        "attn_sinks": (attn_sinks, m_attn_sinks, v_attn_sinks), "rel_bias": (rel_bias, m_rel_bias, v_rel_bias),
        "w_pool": tuple(a.reshape(512, 128) for a in (w_pool, m_w_pool, v_w_pool)),
        "pool_scale": (pool_scale, m_pool_scale, v_pool_scale), "g_ffn_norm": (g_ffn_norm, m_g_ffn_norm, v_g_ffn_norm),
        "g_ple_norm": (g_ple_norm, m_g_ple_norm, v_g_ple_norm),
    }

    grads, deltas, new_ms, new_vs = {}, {}, {}, {}
    out = None
    for name, (w, m, v, g_src, transposed) in big.items():
        if g_src is None:
            g_src = (_rs_finish_rows(reduce_b.finish(out[-1]), "rs_finish_b", out[-1]), 0)
        view = (lambda a: a.T) if transposed else (lambda a: a)
        out = _adamw(view(w[0]), *g_src, view(m[0]), view(v[0]), "adamw_" + name)
        grads[name], deltas[name], new_ms[name], new_vs[name] = (view(a)[None] for a in out)

    loss, small_out = _adamw_small(small_all, list(small_params.values()))
    for name, (g2, d, nm, nv) in zip(small_params, small_out):
        shape = w_pool.shape if name == "w_pool" else g2.shape
        grads[name], deltas[name], new_ms[name], new_vs[name] = (a.reshape(shape) for a in (g2, d, nm, nv))

    order = ["w_in", "w_out", "g_attn_norm", "g_q", "g_k", "attn_sinks", "rel_bias", "w_pool", "pool_scale", "g_ffn_norm",
             "w_gate", "w_up", "w_down", "g_ple_norm", "w_ple_gate", "w_ple_proj"]
    return (loss.reshape(()), dx[None], *[grads[n] for n in order], *[deltas[n] for n in order],
            *[new_ms[n] for n in order], *[new_vs[n] for n in order])
```

```python
import functools

import numpy as np
import jax
import jax.numpy as jnp
from jax import lax
from jax.experimental import pallas as pl
from jax.experimental.pallas import tpu as pltpu
from jax.experimental.pallas import tpu_sc as plsc

F32 = jnp.float32
BF16 = jnp.bfloat16
MESH = pl.DeviceIdType.MESH

D_MODEL = 1024
HEAD_DIM = 64
N_Q_HEADS = 8
ATTN_WIDTH = 512
KV_WIDTH = 128
POOL_WIDTH = 512
IN_WIDTH = 1280
D_FF = 2816
PLE_DIM = 256
FF_CHUNK = 704
BLOCK = 128
N_BUCKETS = 32
MAX_DISTANCE = 128
POOL_SIZES = (2, 4, 8, 16)
EPS = 1e-6
NEG = -1e30
N_CHIPS = 4
N_DEV = 8

ADAM_LR = 0.001
ADAM_B1 = 0.9
ADAM_B2 = 0.999
ADAM_EPS = 1e-08
ADAM_WD = 0.01
ADAM_STEP = 10

SLAB = {"inT": (0, 320), "out": (320, 256), "gateT": (576, 704), "upT": (1280, 704), "down": (1984, 704),
        "plg": (2688, 256), "plp": (2944, 64)}
SLAB_ROWS = 3008
HALF_ROWS = SLAB_ROWS // 2
GATHER_PARTS = ((0, 576), (576, SLAB_ROWS))
POOL_HALO = 24

SMALL = {"g_attn": 0, "g_ffn": 8, "g_ple": 16, "pool_scale": 24, "g_q": 28, "g_k": 29, "sinks": 30, "loss": 31,
         "rel_bias": 32, "w_pool": 64}
SMALL_ROWS = 576

VMEM_LIMIT_BIG = 60 * 1024 * 1024
VMEM_LIMIT = 48 * 1024 * 1024


def _params(vmem=VMEM_LIMIT, n_axes=1):
    return pltpu.CompilerParams(dimension_semantics=("arbitrary",) * n_axes, vmem_limit_bytes=vmem)


def _dot(a, b, ca, cb):
    return lax.dot_general(a, b, (((ca,), (cb,)), ((), ())), preferred_element_type=F32)


def _full(shape):
    return pl.BlockSpec(shape, lambda i: (0,) * len(shape))


ANY = pl.BlockSpec(memory_space=pl.ANY)
VMEM_WHOLE = pl.BlockSpec(memory_space=pltpu.VMEM)


W_SPECS = [ANY, ANY, pl.BlockSpec(memory_space=pltpu.SMEM)]


def _load_rows(w_refs, name, dst_ref, sems):
    slab_ref, local_ref, me_ref = w_refs
    off, rows = SLAB[name]
    slab_off = off - max(start for start, _ in GATHER_PARTS if start <= off)
    me = me_ref[0]
    for phase in ("start", "wait"):
        for j in range(N_CHIPS):
            dst = dst_ref.at[pl.ds(j * rows, rows), :]
            theirs = pltpu.make_async_copy(slab_ref.at[j, pl.ds(slab_off, rows), :], dst, sems.at[j])
            own = pltpu.make_async_copy(local_ref.at[pl.ds(slab_off, rows), :], dst, sems.at[j])

            @pl.when(me == j)
            def _():
                getattr(own, phase)()

            @pl.when(me != j)
            def _():
                getattr(theirs, phase)()


def _rms_fwd(x, g):
    r = lax.rsqrt(jnp.mean(x * x, axis=-1, keepdims=True) + EPS)
    return x * r * g


def _rms_bwd(x, g, dy):
    r = lax.rsqrt(jnp.mean(x * x, axis=-1, keepdims=True) + EPS)
    xn = x * r
    dyg = dy * g
    dx = r * (dyg - xn * jnp.mean(dyg * xn, axis=-1, keepdims=True))
    return dx, jnp.sum(dy * xn, axis=0, keepdims=True)


def _half_sum(v, lo):
    s_lo = jnp.sum(jnp.where(lo, v, 0.0), axis=-1, keepdims=True)
    s_hi = jnp.sum(jnp.where(lo, 0.0, v), axis=-1, keepdims=True)
    return jnp.where(lo, s_lo, s_hi)


def _half_sum_mxu(v):
    upper = lax.broadcasted_iota(jnp.int32, (128, 128), 0) < 64
    left = lax.broadcasted_iota(jnp.int32, (128, 128), 1) < 64
    ones = jnp.where(upper == left, 1.0, 0.0).astype(BF16)
    high = v.astype(BF16)
    low = (v - high.astype(F32)).astype(BF16)
    return _dot(high, ones, 1, 0) + _dot(low, ones, 1, 0)


def _pair_norm(zp, g, lo):
    r = lax.rsqrt(_half_sum(zp * zp, lo) * (1.0 / HEAD_DIM) + EPS)
    return zp * r * g


def _pair_norm_bwd(zp, g, dy):
    r = lax.rsqrt(_half_sum_mxu(zp * zp) * (1.0 / HEAD_DIM) + EPS)
    xn = zp * r
    dyg = dy * g
    dx = r * (dyg - xn * (_half_sum_mxu(dyg * xn) * (1.0 / HEAD_DIM)))
    return dx, jnp.sum(dy * xn, axis=0, keepdims=True)


def _to_stacked(pair, group, lo):
    rolled = pltpu.roll(pair, 64, axis=1)
    if group == 0:
        return jnp.where(lo, pair, 0.0), jnp.where(lo, rolled, 0.0)
    return jnp.where(lo, 0.0, rolled), jnp.where(lo, 0.0, pair)


def _from_stacked(even, odd, group, lo):
    if group == 0:
        return jnp.where(lo, even, pltpu.roll(odd, 64, axis=1))
    return jnp.where(lo, pltpu.roll(even, 64, axis=1), odd)


def _sigmoid(v):
    return 1.0 / (1.0 + jnp.exp(-v))


def _pool_counts(tile, n_rows):
    t1 = tile * n_rows + lax.broadcasted_iota(jnp.int32, (n_rows, POOL_WIDTH), 0) + 1
    lane = lax.broadcasted_iota(jnp.int32, (n_rows, POOL_WIDTH), 1)
    win = jnp.where(lane < 128, 2, jnp.where(lane < 256, 4, jnp.where(lane < 384, 8, 16)))
    return jnp.minimum(t1, win).astype(F32)


def _attn_in(x2, g_attn, gq, gk, wts):
    s_len = x2.shape[0]
    t = 512

    def body(x_ref, g_ref, gq_ref, gk_ref, sl_ref, lo_ref, me_ref, hn_ref, zqk_ref, u_ref, kn_ref, v_ref, qst_ref, w_ref, sems):
        @pl.when(pl.program_id(0) == 0)
        def _():
            _load_rows((sl_ref, lo_ref, me_ref), "inT", w_ref, sems)

        hn = _rms_fwd(x_ref[...], g_ref[...]).astype(BF16)
        hn_ref[...] = hn
        z = _dot(hn, w_ref[...], 1, 1)
        zqk_ref[...] = z[:, :640]
        u_ref[...] = z[:, 768:]
        v_ref[...] = z[:, 640:768].astype(BF16)
        lo = lax.broadcasted_iota(jnp.int32, (t, 128), 1) < 64
        kn_ref[...] = _pair_norm(z[:, 512:640], gk_ref[...], lo).astype(BF16)
        for p in range(4):
            qn = _pair_norm(z[:, 128 * p:128 * p + 128], gq_ref[...], lo)
            even, odd = _to_stacked(qn, p // 2, lo)
            qst_ref[2 * p] = even.astype(BF16)
            qst_ref[2 * p + 1] = odd.astype(BF16)

    row = lambda w: pl.BlockSpec((t, w), lambda i: (i, 0))
    return pl.pallas_call(
        body, name="attn_in", grid=(s_len // t,),
        in_specs=[row(D_MODEL), _full((1, D_MODEL)), _full((1, 128)), _full((1, 128))] + W_SPECS,
        out_specs=[row(D_MODEL), row(640), row(POOL_WIDTH), row(128), row(128),
                   pl.BlockSpec((N_Q_HEADS, t, 128), lambda i: (0, i, 0))],
        out_shape=[jax.ShapeDtypeStruct((s_len, D_MODEL), BF16), jax.ShapeDtypeStruct((s_len, 640), F32),
                   jax.ShapeDtypeStruct((s_len, POOL_WIDTH), F32), jax.ShapeDtypeStruct((s_len, 128), BF16),
                   jax.ShapeDtypeStruct((s_len, 128), BF16), jax.ShapeDtypeStruct((N_Q_HEADS, s_len, 128), BF16)],
        scratch_shapes=[pltpu.VMEM((IN_WIDTH, D_MODEL), BF16), pltpu.SemaphoreType.DMA((N_CHIPS,))],
        compiler_params=_params(),
    )(x2, g_attn, gq, gk, *wts)


def _bucket_table():
    i_idx = np.arange(BLOCK)[:, None]
    j_idx = np.arange(2 * BLOCK)[None, :]
    d = BLOCK + i_idx - j_idx
    n = np.maximum(d, 0)
    max_exact = N_BUCKETS // 2
    nf = np.maximum(n, 1).astype(np.float64)
    large = max_exact + (np.log(nf / max_exact) / np.log(MAX_DISTANCE / max_exact) * (N_BUCKETS - max_exact)).astype(np.int64)
    large = np.minimum(large, N_BUCKETS - 1)
    bucket = np.where(n < max_exact, n, large)
    return np.where((d >= 0) & (d < BLOCK), bucket, -1).astype(np.int32)


def _bias_build(rel_bias_t, bucket):
    def body(rb_ref, bucket_ref, out_ref):
        bk = bucket_ref[...]
        for h in range(N_Q_HEADS):
            acc = jnp.full((BLOCK, 2 * BLOCK), NEG, F32)
            for b in range(N_BUCKETS):
                acc = jnp.where(bk == b, rb_ref[h, b], acc)
            out_ref[0, pl.ds(h * BLOCK, BLOCK), :] = acc
            out_ref[1, pl.ds(h * BLOCK, BLOCK), :] = acc
            out_ref[1, pl.ds(h * BLOCK, BLOCK), 0:BLOCK] = jnp.full((BLOCK, BLOCK), NEG, F32)

    return pl.pallas_call(
        body, name="bias_build",
        in_specs=[pl.BlockSpec(memory_space=pltpu.SMEM), VMEM_WHOLE], out_specs=VMEM_WHOLE,
        out_shape=jax.ShapeDtypeStruct((2, N_Q_HEADS * BLOCK, 2 * BLOCK), F32),
    )(rel_bias_t, bucket)


def _head_softmax(s_ref, bias_ref, sink_ref, h):
    rows = pl.ds(pl.multiple_of(h * BLOCK, BLOCK), BLOCK)
    s = s_ref[rows, :] * (HEAD_DIM ** -0.5) + bias_ref[rows, :]
    sink = sink_ref[h]
    m = jnp.maximum(jnp.max(s, axis=-1, keepdims=True), sink)
    p = jnp.exp(s - m)
    e_sink = jnp.exp(sink - m)
    inv = 1.0 / (jnp.sum(p, axis=-1, keepdims=True) + e_sink)
    return rows, p * inv, e_sink * inv


def _attn_specs():
    prev = lambda i: (jnp.maximum(i - 1, 0), 0)
    cur = lambda i: (i, 0)
    stacked = pl.BlockSpec((N_Q_HEADS, BLOCK, 128), lambda i: (0, i, 0))
    kv = [pl.BlockSpec((BLOCK, 128), prev), pl.BlockSpec((BLOCK, 128), cur)]
    consts = [pl.BlockSpec((None, N_Q_HEADS * BLOCK, 2 * BLOCK), lambda i: (jnp.where(i == 0, 1, 0), 0, 0)),
              pl.BlockSpec(memory_space=pltpu.SMEM)]
    return stacked, kv, consts


def _head_lane_mask():
    rows = lax.broadcasted_iota(jnp.int32, (N_Q_HEADS * BLOCK, 128), 0)
    lanes = lax.broadcasted_iota(jnp.int32, (N_Q_HEADS * BLOCK, 128), 1)
    return (rows < 4 * BLOCK) == (lanes < 64)


def _attn_fwd(qst, kn, vb, bias_st, sinks):
    s_len = kn.shape[0]

    def body(q_ref, kp_ref, kc_ref, vp_ref, vc_ref, bias_ref, sink_ref, o_ref, s_ref, p_ref):
        q = q_ref[...].reshape(N_Q_HEADS * BLOCK, 128)
        s_ref[...] = _dot(q, jnp.concatenate([kp_ref[...], kc_ref[...]], axis=0), 1, 1)

        def head(h, carry):
            rows, probs, _ = _head_softmax(s_ref, bias_ref, sink_ref, h)
            p_ref[rows, :] = probs.astype(BF16)
            return carry

        lax.fori_loop(0, N_Q_HEADS, head, 0, unroll=True)
        o = _dot(p_ref[...], jnp.concatenate([vp_ref[...], vc_ref[...]], axis=0), 1, 0)
        o_ref[...] = jnp.where(_head_lane_mask(), o, 0.0).astype(BF16).reshape(N_Q_HEADS, BLOCK, 128)

    stacked, kv, consts = _attn_specs()
    return pl.pallas_call(
        body, name="attn_fwd", grid=(s_len // BLOCK,),
        in_specs=[stacked] + kv + kv + consts, out_specs=stacked,
        out_shape=jax.ShapeDtypeStruct((N_Q_HEADS, s_len, 128), BF16),
        scratch_shapes=[pltpu.VMEM((N_Q_HEADS * BLOCK, 2 * BLOCK), F32), pltpu.VMEM((N_Q_HEADS * BLOCK, 2 * BLOCK), BF16)],
        compiler_params=_params(),
    )(qst, kn, kn, vb, vb, bias_st, sinks)


def _mix_out(u, ost, x2, wts, wpool, pool_scale, g_ffn):
    s_len = x2.shape[0]
    t = 512
    n = t + 16

    def body(u_ref, o_ref, x_ref, sl_ref, lo_ref, me_ref, wp_ref, sc_ref, g_ref, pooled_ref, mix_ref, h1_ref, hn_ref,
             w_ref, ext_ref, st_ref, sems):
        i = pl.program_id(0)

        @pl.when(i == 0)
        def _():
            _load_rows((sl_ref, lo_ref, me_ref), "out", w_ref, sems)
            ext_ref[...] = jnp.zeros_like(ext_ref)
            st_ref[...] = jnp.zeros_like(st_ref)

        u_tile = u_ref[...]
        ext_ref[pl.ds(POOL_HALO, t), :] = u_tile
        st_ref[pl.ds(8, n), :] = ext_ref[pl.ds(8, n), :] + ext_ref[pl.ds(7, n), :]
        st_ref[pl.ds(8, n), 128:] = st_ref[pl.ds(8, n), 128:] + st_ref[pl.ds(6, n), 128:]
        st_ref[pl.ds(8, n), 256:] = st_ref[pl.ds(8, n), 256:] + st_ref[pl.ds(4, n), 256:]
        st_ref[pl.ds(8, n), 384:] = st_ref[pl.ds(8, n), 384:] + st_ref[pl.ds(0, n), 384:]
        ext_ref[pl.ds(0, POOL_HALO), :] = ext_ref[pl.ds(t, POOL_HALO), :]
        pooled = (st_ref[pl.ds(POOL_HALO, t), :] / _pool_counts(i, t) - u_tile).astype(BF16)
        pooled_ref[...] = pooled
        for g in range(4):
            cols = slice(128 * g, 128 * g + 128)
            y = _dot(pooled[:, cols], wp_ref[g], 1, 0) * sc_ref[:, cols]
            mix_ref[:, ATTN_WIDTH + 128 * g:ATTN_WIDTH + 128 * g + 128] = y.astype(BF16)
        lo = lax.broadcasted_iota(jnp.int32, (t, 128), 1) < 64
        for p in range(4):
            a = _from_stacked(o_ref[2 * p].astype(F32), o_ref[2 * p + 1].astype(F32), p // 2, lo)
            mix_ref[:, 128 * p:128 * p + 128] = a.astype(BF16)
        h1 = x_ref[...] + _dot(mix_ref[...], w_ref[...], 1, 0)
        h1_ref[...] = h1
        hn_ref[...] = _rms_fwd(h1, g_ref[...]).astype(BF16)

    row = lambda w: pl.BlockSpec((t, w), lambda i: (i, 0))
    return pl.pallas_call(
        body, name="mix_out", grid=(s_len // t,),
        in_specs=[row(POOL_WIDTH), pl.BlockSpec((N_Q_HEADS, t, 128), lambda i: (0, i, 0)), row(D_MODEL)] + W_SPECS
        + [_full((4, 128, 128)), _full((1, POOL_WIDTH)), _full((1, D_MODEL))],
        out_specs=[row(POOL_WIDTH), row(D_MODEL), row(D_MODEL), row(D_MODEL)],
        out_shape=[jax.ShapeDtypeStruct((s_len, POOL_WIDTH), BF16), jax.ShapeDtypeStruct((s_len, D_MODEL), BF16),
                   jax.ShapeDtypeStruct((s_len, D_MODEL), F32), jax.ShapeDtypeStruct((s_len, D_MODEL), BF16)],
        scratch_shapes=[pltpu.VMEM((D_MODEL, D_MODEL), BF16), pltpu.VMEM((t + POOL_HALO, POOL_WIDTH), F32),
                        pltpu.VMEM((t + POOL_HALO, POOL_WIDTH), F32), pltpu.SemaphoreType.DMA((N_CHIPS,))],
        compiler_params=_params(),
    )(u, ost, x2, *wts, wpool, pool_scale, g_ffn)


def _ffn_ple(hn2, h1, p2, tgt, wts, g_ffn, g_ple):
    s_len = h1.shape[0]
    t = 256
    n_tiles = s_len // t

    def body(hn_ref, h1_ref, p_ref, tgt_ref, sl_ref, lo_ref, me_ref, gf_ref, gp_ref,
             loss_ref, dgate_ref, dup_ref, act_ref, dh2b_ref, hn3_ref, dgl_ref, dwp_ref, dh1_ref, dgf_ref, dgp_ref,
             wg_ref, wu_ref, wd_ref, wl_ref, wp_ref, packed_ref, gate_s, up_s, loss_acc, dwp_acc, sems):
        i = pl.program_id(0)

        @pl.when(i == 0)
        def _():
            w_refs = (sl_ref, lo_ref, me_ref)
            _load_rows(w_refs, "gateT", wg_ref, sems)
            _load_rows(w_refs, "upT", wu_ref, sems)
            _load_rows(w_refs, "down", wd_ref, sems)
            _load_rows(w_refs, "plg", wl_ref, sems)
            _load_rows(w_refs, "plp", packed_ref, sems)
            for j in range(N_CHIPS):
                for q in range(4):
                    wp_ref[pl.ds(64 * q, 64), 256 * j:256 * j + 256] = packed_ref[pl.ds(64 * j, 64), 256 * q:256 * q + 256]
            loss_acc[...] = jnp.zeros_like(loss_acc)
            dgf_ref[...] = jnp.zeros_like(dgf_ref)
            dgp_ref[...] = jnp.zeros_like(dgp_ref)

        hn = hn_ref[...]
        h1v = h1_ref[...]
        h2 = h1v
        for ch in range(N_CHIPS):
            rows = pl.ds(ch * FF_CHUNK, FF_CHUNK)
            gate = _dot(hn, wg_ref[rows, :], 1, 1)
            up = _dot(hn, wu_ref[rows, :], 1, 1)
            gate_s[ch] = gate
            up_s[ch] = up
            act = (gate * _sigmoid(gate) * up).astype(BF16)
            act_ref[ch] = act
            h2 = h2 + _dot(act, wd_ref[rows, :], 1, 0)
        gp = gp_ref[...]
        hn3 = _rms_fwd(h2, gp).astype(BF16)
        hn3_ref[...] = hn3
        gate2 = _sigmoid(_dot(hn3, wl_ref[...], 1, 0))
        p_tile = p_ref[...].astype(BF16)
        pp = _dot(p_tile, wp_ref[...], 1, 0)
        err = h2 + gate2 * pp - tgt_ref[...]
        loss_acc[...] += jnp.sum(err * err, axis=0, keepdims=True)
        dy = err * (1.0 / D_MODEL)
        _accumulate_tn(dwp_acc, p_tile, (dy * gate2).astype(BF16), i == 0)
        dgl = (dy * pp * gate2 * (1.0 - gate2)).astype(BF16)
        dgl_ref[...] = dgl
        dx3, dg3 = _rms_bwd(h2, gp, _dot(dgl, wl_ref[...], 1, 1))
        dh2 = dy + dx3
        dgp_ref[...] += dg3
        dh2b = dh2.astype(BF16)
        dh2b_ref[...] = dh2b
        dhn = jnp.zeros((t, D_MODEL), F32)
        for ch in range(N_CHIPS):
            rows = pl.ds(ch * FF_CHUNK, FF_CHUNK)
            dact = _dot(dh2b, wd_ref[rows, :], 1, 1)
            gate_v = gate_s[ch]
            up_v = up_s[ch]
            sg = _sigmoid(gate_v)
            dup = (dact * (gate_v * sg)).astype(BF16)
            dgate = (dact * up_v * (sg * (1.0 + gate_v * (1.0 - sg)))).astype(BF16)
            dup_ref[ch] = dup
            dgate_ref[ch] = dgate
            dhn = dhn + _dot(dgate, wg_ref[rows, :], 1, 0) + _dot(dup, wu_ref[rows, :], 1, 0)
        dx, dg = _rms_bwd(h1v, gf_ref[...], dhn)
        dh1_ref[...] = dh2 + dx
        dgf_ref[...] += dg

        @pl.when(i == n_tiles - 1)
        def _():
            total = jnp.sum(loss_acc[...], axis=-1, keepdims=True) * (0.5 / D_MODEL)
            loss_ref[...] = jnp.broadcast_to(total, loss_ref.shape)
            dwp_ref[...] = dwp_acc[...].astype(BF16)

    row = lambda w: pl.BlockSpec((t, w), lambda i: (i, 0))
    chunked = pl.BlockSpec((N_CHIPS, t, FF_CHUNK), lambda i: (0, i, 0))
    vec = _full((1, D_MODEL))
    act_shape = jax.ShapeDtypeStruct((N_CHIPS, s_len, FF_CHUNK), BF16)
    tok = lambda dtype: jax.ShapeDtypeStruct((s_len, D_MODEL), dtype)
    return pl.pallas_call(
        body, name="ffn_ple", grid=(n_tiles,),
        in_specs=[row(D_MODEL), row(D_MODEL), row(PLE_DIM), row(D_MODEL)] + W_SPECS + [vec, vec],
        out_specs=[_full((1, 128)), chunked, chunked, chunked] + [row(D_MODEL)] * 3 + [_full((PLE_DIM, D_MODEL)), row(D_MODEL),
                                                                                       vec, vec],
        out_shape=[jax.ShapeDtypeStruct((1, 128), F32), act_shape, act_shape, act_shape, tok(BF16), tok(BF16), tok(BF16),
                   jax.ShapeDtypeStruct((PLE_DIM, D_MODEL), BF16), tok(F32), jax.ShapeDtypeStruct((1, D_MODEL), F32),
                   jax.ShapeDtypeStruct((1, D_MODEL), F32)],
        scratch_shapes=[pltpu.VMEM((D_FF, D_MODEL), BF16)] * 3
        + [pltpu.VMEM((D_MODEL, D_MODEL), BF16), pltpu.VMEM((PLE_DIM, D_MODEL), BF16), pltpu.VMEM((PLE_DIM, D_MODEL), BF16),
           pltpu.VMEM((N_CHIPS, t, FF_CHUNK), F32), pltpu.VMEM((N_CHIPS, t, FF_CHUNK), F32), pltpu.VMEM((1, D_MODEL), F32),
           pltpu.VMEM((PLE_DIM, D_MODEL), F32), pltpu.SemaphoreType.DMA((N_CHIPS,))],
        compiler_params=_params(VMEM_LIMIT_BIG),
    )(hn2, h1, p2, tgt, *wts, g_ffn, g_ple)


def _accumulate_tn(acc_ref, a, b, first):
    @pl.when(first)
    def _():
        acc_ref[...] = _dot(a, b, 0, 0)

    @pl.when(jnp.logical_not(first))
    def _():
        acc_ref[...] += _dot(a, b, 0, 0)


def _flush_chunks(acc_ref, stage_ref, slab_ref, name, sems):
    stage_ref[...] = acc_ref[...].astype(BF16)
    off, rows = SLAB[name]
    copies = [pltpu.make_async_copy(stage_ref.at[pl.ds(j * rows, rows), :], slab_ref.at[j, pl.ds(off, rows), :], sems.at[j])
              for j in range(N_CHIPS)]
    for cp in copies:
        cp.start()
    for cp in copies:
        cp.wait()


def _mix_out_bwd(dh1, wts, pooled, wpool, pool_scale, mix, after):
    s_len = dh1.shape[0]
    t = 512
    n = t + 16
    n_tiles = s_len // t
    early_rows = GATHER_PARTS[0][1]

    def body(dh1_ref, sl_ref, lo_ref, me_ref, pooled_ref, wp_ref, sc_ref, mix_ref, after_ref, dost_ref, du_ref, dyp_ref,
             dsc_ref, slab_ref, w_ref, ext_ref, st_ref, acc_ref, stage_ref, sems):
        del after_ref
        i = pl.program_id(0)

        @pl.when(i == 0)
        def _():
            _load_rows((sl_ref, lo_ref, me_ref), "out", w_ref, sems)
            ext_ref[...] = jnp.zeros_like(ext_ref)
            st_ref[...] = jnp.zeros_like(st_ref)
            dsc_ref[...] = jnp.zeros_like(dsc_ref)

        dh1b = dh1_ref[...].astype(BF16)
        _accumulate_tn(acc_ref, mix_ref[...], dh1b, i == 0)

        @pl.when(i == n_tiles - 1)
        def _():
            _flush_chunks(acc_ref, stage_ref, slab_ref, "out", sems)

        dmix = _dot(dh1b, w_ref[...], 1, 1)
        lo = lax.broadcasted_iota(jnp.int32, (t, 128), 1) < 64
        for p in range(4):
            even, odd = _to_stacked(dmix[:, 128 * p:128 * p + 128], p // 2, lo)
            dost_ref[2 * p] = even.astype(BF16)
            dost_ref[2 * p + 1] = odd.astype(BF16)
        pooled_v = pooled_ref[...]
        counts = _pool_counts(n_tiles - 1 - i, t)
        for g in range(4):
            cols = slice(128 * g, 128 * g + 128)
            dm = dmix[:, ATTN_WIDTH + 128 * g:ATTN_WIDTH + 128 * g + 128]
            ypre = _dot(pooled_v[:, cols], wp_ref[g], 1, 0)
            dsc_ref[:, cols] += jnp.sum(ypre * dm, axis=0, keepdims=True)
            dyp = (dm * sc_ref[:, cols]).astype(BF16)
            dyp_ref[:, cols] = dyp
            dpooled = _dot(dyp, wp_ref[g], 1, 1)
            du_ref[:, cols] = -dpooled
            ext_ref[pl.ds(0, t), cols] = dpooled / counts[:, cols]
        st_ref[pl.ds(0, n), :] = ext_ref[pl.ds(0, n), :] + ext_ref[pl.ds(1, n), :]
        st_ref[pl.ds(0, n), 128:] = st_ref[pl.ds(0, n), 128:] + st_ref[pl.ds(2, n), 128:]
        st_ref[pl.ds(0, n), 256:] = st_ref[pl.ds(0, n), 256:] + st_ref[pl.ds(4, n), 256:]
        st_ref[pl.ds(0, n), 384:] = st_ref[pl.ds(0, n), 384:] + st_ref[pl.ds(8, n), 384:]
        ext_ref[pl.ds(t, POOL_HALO), :] = ext_ref[pl.ds(0, POOL_HALO), :]
        du_ref[...] += st_ref[pl.ds(0, t), :]

    rev = lambda w: pl.BlockSpec((t, w), lambda i: (n_tiles - 1 - i, 0))
    return pl.pallas_call(
        body, name="mix_out_bwd", grid=(n_tiles,),
        in_specs=[rev(D_MODEL)] + W_SPECS + [rev(POOL_WIDTH), _full((4, 128, 128)), _full((1, POOL_WIDTH)), rev(D_MODEL), ANY],
        out_specs=[pl.BlockSpec((N_Q_HEADS, t, 128), lambda i: (0, n_tiles - 1 - i, 0)), rev(POOL_WIDTH), rev(POOL_WIDTH),
                   _full((1, POOL_WIDTH)), ANY],
        out_shape=[jax.ShapeDtypeStruct((N_Q_HEADS, s_len, 128), BF16), jax.ShapeDtypeStruct((s_len, POOL_WIDTH), F32),
                   jax.ShapeDtypeStruct((s_len, POOL_WIDTH), BF16), jax.ShapeDtypeStruct((1, POOL_WIDTH), F32),
                   jax.ShapeDtypeStruct((N_CHIPS, early_rows, D_MODEL), BF16)],
        scratch_shapes=[pltpu.VMEM((D_MODEL, D_MODEL), BF16), pltpu.VMEM((t + POOL_HALO, POOL_WIDTH), F32),
                        pltpu.VMEM((t + POOL_HALO, POOL_WIDTH), F32), pltpu.VMEM((D_MODEL, D_MODEL), F32),
                        pltpu.VMEM((D_MODEL, D_MODEL), BF16), pltpu.SemaphoreType.DMA((N_CHIPS,))],
        compiler_params=_params(),
    )(dh1, *wts, pooled, wpool, pool_scale, mix, after)


def _attn_bwd(qst, kn, vb, dost, bias_st, sinks, after):
    s_len = kn.shape[0]

    def body(q_ref, kp_ref, kc_ref, vp_ref, vc_ref, do_ref, bias_ref, sink_ref, after_ref, dq_ref, dk_ref, dv_ref, dbias_ref,
             dsink_ref, s_ref, dp_ref, p_ref, dl_ref):
        del after_ref
        i = pl.program_id(0)

        @pl.when(i == 0)
        def _():
            dk_ref[...] = jnp.zeros_like(dk_ref)
            dv_ref[...] = jnp.zeros_like(dv_ref)
            dbias_ref[...] = jnp.zeros_like(dbias_ref)
            dsink_ref[...] = jnp.zeros_like(dsink_ref)

        q = q_ref[...].reshape(N_Q_HEADS * BLOCK, 128)
        do = do_ref[...].reshape(N_Q_HEADS * BLOCK, 128)
        k2 = jnp.concatenate([kp_ref[...], kc_ref[...]], axis=0)
        s_ref[...] = _dot(q, k2, 1, 1)
        dp_ref[...] = _dot(do, jnp.concatenate([vp_ref[...], vc_ref[...]], axis=0), 1, 1)

        def head(h, carry):
            rows, probs, p_sink = _head_softmax(s_ref, bias_ref, sink_ref, h)
            dp = dp_ref[rows, :]
            dsum = jnp.sum(probs * dp, axis=-1, keepdims=True)
            dlog = probs * (dp - dsum)
            dsink_ref[rows, :] -= p_sink * dsum
            dbias_ref[rows, :] += dlog
            p_ref[rows, :] = probs.astype(BF16)
            dl_ref[rows, :] = (dlog * (HEAD_DIM ** -0.5)).astype(BF16)
            return carry

        lax.fori_loop(0, N_Q_HEADS, head, 0, unroll=True)
        dlog_s = dl_ref[...]
        dq_ref[...] = jnp.where(_head_lane_mask(), _dot(dlog_s, k2, 1, 0), 0.0).reshape(N_Q_HEADS, BLOCK, 128)
        dk2 = _dot(dlog_s, q, 0, 0)
        dv2 = _dot(p_ref[...], do, 0, 0)
        prev_rows = pl.ds(pl.multiple_of(jnp.maximum(i - 1, 0) * BLOCK, BLOCK), BLOCK)
        cur_rows = pl.ds(pl.multiple_of(i * BLOCK, BLOCK), BLOCK)
        dk_ref[prev_rows, :] += dk2[:BLOCK]
        dk_ref[cur_rows, :] += dk2[BLOCK:]
        dv_ref[prev_rows, :] += dv2[:BLOCK]
        dv_ref[cur_rows, :] += dv2[BLOCK:]

    stacked, kv, consts = _attn_specs()
    band = (N_Q_HEADS * BLOCK, 2 * BLOCK)
    return pl.pallas_call(
        body, name="attn_bwd", grid=(s_len // BLOCK,),
        in_specs=[stacked] + kv + kv + [stacked] + consts + [ANY],
        out_specs=[stacked, _full((s_len, 128)), _full((s_len, 128)), _full(band), _full((N_Q_HEADS * BLOCK, 1))],
        out_shape=[jax.ShapeDtypeStruct((N_Q_HEADS, s_len, 128), F32), jax.ShapeDtypeStruct((s_len, 128), F32),
                   jax.ShapeDtypeStruct((s_len, 128), F32), jax.ShapeDtypeStruct(band, F32),
                   jax.ShapeDtypeStruct((N_Q_HEADS * BLOCK, 1), F32)],
        scratch_shapes=[pltpu.VMEM(band, F32), pltpu.VMEM(band, F32), pltpu.VMEM(band, BF16), pltpu.VMEM(band, BF16)],
        compiler_params=_params(),
    )(qst, kn, kn, vb, vb, dost, bias_st, sinks, after)


def _small_pack(dg_attn, dg_ffn, dg_ple, dscale, dgq, dgk, dbias, dsink_rows, bucket, loss_v, dwpool):
    def body(ga_ref, gf_ref, gp_ref, sc_ref, gq_ref, gk_ref, db_ref, ds_ref, bucket_ref, loss_ref, wp_ref, out_ref):
        out_ref[pl.ds(0, SMALL["w_pool"]), :] = jnp.zeros((SMALL["w_pool"], 128), F32)
        for name, ref, n in (("g_attn", ga_ref, 8), ("g_ffn", gf_ref, 8), ("g_ple", gp_ref, 8), ("pool_scale", sc_ref, 4)):
            for k in range(n):
                out_ref[pl.ds(SMALL[name] + k, 1), :] = ref[:, 128 * k:128 * k + 128]
        for name, ref in (("g_q", gq_ref), ("g_k", gk_ref)):
            both = ref[...]
            out_ref[pl.ds(SMALL[name], 1), :] = both + pltpu.roll(both, 64, axis=1)
        out_ref[pl.ds(SMALL["loss"], 1), :] = loss_ref[...]
        bk = bucket_ref[...]
        rows = lax.broadcasted_iota(jnp.int32, (N_BUCKETS, 128), 0)
        lanes = lax.broadcasted_iota(jnp.int32, (N_BUCKETS, 128), 1)
        lane1 = lax.broadcasted_iota(jnp.int32, (1, 128), 1)
        rb = jnp.zeros((N_BUCKETS, 128), F32)
        sk = jnp.zeros((1, 128), F32)
        for h in range(N_Q_HEADS):
            band = db_ref[pl.ds(h * BLOCK, BLOCK), :]
            for b in range(N_BUCKETS):
                rb = jnp.where((rows == b) & (lanes == h), jnp.sum(jnp.where(bk == b, band, 0.0)), rb)
            sk = jnp.where(lane1 == h, jnp.sum(ds_ref[pl.ds(h * BLOCK, BLOCK), :]), sk)
        out_ref[pl.ds(SMALL["rel_bias"], N_BUCKETS), :] = rb
        out_ref[pl.ds(SMALL["sinks"], 1), :] = sk
        out_ref[pl.ds(SMALL["w_pool"], 512), :] = wp_ref[...].reshape(512, 128)

    return pl.pallas_call(
        body, name="small_pack", in_specs=[VMEM_WHOLE] * 11, out_specs=VMEM_WHOLE,
        out_shape=jax.ShapeDtypeStruct((SMALL_ROWS, 128), F32),
    )(dg_attn, dg_ffn, dg_ple, dscale, dgq, dgk, dbias, dsink_rows, bucket, loss_v, dwpool)


def _attn_in_bwd(dqst, zqk, dk, dv, du, x2, dh1, hn1, slab, wts, g_attn, gq, gk):
    s_len = x2.shape[0]
    t = 512
    n_tiles = s_len // t

    def body(dq_ref, zqk_ref, dk_ref, dv_ref, du_ref, x_ref, dh1_ref, hn_ref, slab_in_ref, sl_ref, lo_ref, me_ref, g_ref,
             gq_ref, gk_ref, dx_ref, dg_ref, dgq_ref, dgk_ref, slab_ref, w_ref, dz_ref, acc_ref, stage_ref, sems):
        del slab_in_ref
        i = pl.program_id(0)

        @pl.when(i == 0)
        def _():
            _load_rows((sl_ref, lo_ref, me_ref), "inT", w_ref, sems)
            dg_ref[...] = jnp.zeros_like(dg_ref)
            dgq_ref[...] = jnp.zeros_like(dgq_ref)
            dgk_ref[...] = jnp.zeros_like(dgk_ref)

        lo = lax.broadcasted_iota(jnp.int32, (t, 128), 1) < 64
        for p in range(4):
            dqn = _from_stacked(dq_ref[2 * p], dq_ref[2 * p + 1], p // 2, lo)
            dq_raw, dgq = _pair_norm_bwd(zqk_ref[:, 128 * p:128 * p + 128], gq_ref[...], dqn)
            dz_ref[:, 128 * p:128 * p + 128] = dq_raw.astype(BF16)
            dgq_ref[...] += dgq
        dk_raw, dgk = _pair_norm_bwd(zqk_ref[:, 512:640], gk_ref[...], dk_ref[...])
        dgk_ref[...] += dgk
        dz_ref[:, 512:640] = dk_raw.astype(BF16)
        dz_ref[:, 640:768] = dv_ref[...].astype(BF16)
        dz_ref[:, 768:] = du_ref[...].astype(BF16)
        dz = dz_ref[...]
        _accumulate_tn(acc_ref, dz, hn_ref[...], i == 0)
        dx, dg = _rms_bwd(x_ref[...], g_ref[...], _dot(dz, w_ref[...], 1, 0))
        dx_ref[...] = dh1_ref[...] + dx
        dg_ref[...] += dg

        @pl.when(i == n_tiles - 1)
        def _():
            _flush_chunks(acc_ref, stage_ref, slab_ref, "inT", sems)

    row = lambda w: pl.BlockSpec((t, w), lambda i: (i, 0))
    return pl.pallas_call(
        body, name="attn_in_bwd", grid=(n_tiles,),
        in_specs=[pl.BlockSpec((N_Q_HEADS, t, 128), lambda i: (0, i, 0)), row(640), row(128), row(128), row(POOL_WIDTH),
                  row(D_MODEL), row(D_MODEL), row(D_MODEL), ANY] + W_SPECS + [_full((1, D_MODEL)), _full((1, 128)),
                                                                              _full((1, 128))],
        out_specs=[row(D_MODEL), _full((1, D_MODEL)), _full((1, 128)), _full((1, 128)), ANY],
        out_shape=[jax.ShapeDtypeStruct((s_len, D_MODEL), F32), jax.ShapeDtypeStruct((1, D_MODEL), F32),
                   jax.ShapeDtypeStruct((1, 128), F32), jax.ShapeDtypeStruct((1, 128), F32),
                   jax.ShapeDtypeStruct(slab.shape, BF16)],
        input_output_aliases={8: 4},
        scratch_shapes=[pltpu.VMEM((IN_WIDTH, D_MODEL), BF16), pltpu.VMEM((t, IN_WIDTH), BF16),
                        pltpu.VMEM((IN_WIDTH, D_MODEL), F32), pltpu.VMEM((IN_WIDTH, D_MODEL), BF16),
                        pltpu.SemaphoreType.DMA((N_CHIPS,))],
        compiler_params=_params(),
    )(dqst, zqk, dk, dv, du, x2, dh1, hn1, slab, *wts, g_attn, gq, gk)


def _dw(a, b, name, into):
    tk = 1024
    n_out = b.shape[1]
    if a.ndim == 3:
        s_len, tm = a.shape[1:]
        m = N_CHIPS * tm
        a_spec = pl.BlockSpec((None, tk, tm), lambda i, k: (i, k, 0))
    else:
        s_len, m = a.shape
        tm = m // 2 if m > 1408 else m
        a_spec = pl.BlockSpec((tk, tm), lambda i, k: (k, i))
    n_steps = s_len // tk
    chunk = m // N_CHIPS
    per_tile = tm // chunk

    def accumulate(a_ref, b_ref, acc_ref, k):
        @pl.when(k == 0)
        def _():
            acc_ref[...] = _dot(a_ref[...].astype(BF16), b_ref[...].astype(BF16), 0, 0)

        @pl.when(k > 0)
        def _():
            acc_ref[...] += _dot(a_ref[...].astype(BF16), b_ref[...].astype(BF16), 0, 0)

    in_specs = [a_spec, pl.BlockSpec((tk, n_out), lambda i, k: (k, 0))]
    slab, slab_rows, row_off = into
    assert n_out == D_MODEL

    n_tiles = m // tm

    def body_into(a_ref, b_ref, *rest):
        o_ref, acc_ref, stage_ref, sems = rest[-4:]
        i, k = pl.program_id(0), pl.program_id(1)
        accumulate(a_ref, b_ref, acc_ref, k)

        def out_copies(tile, slot):
            return [pltpu.make_async_copy(stage_ref.at[slot, pl.ds(jj * chunk, chunk), :],
                                          o_ref.at[tile * per_tile + jj, pl.ds(row_off, chunk), :], sems.at[slot, jj])
                    for jj in range(per_tile)]

        @pl.when(k == n_steps - 1)
        def _():
            slot = i % 2

            @pl.when(i >= 2)
            def _():
                for cp in out_copies(i - 2, slot):
                    cp.wait()

            stage_ref[slot] = acc_ref[...].astype(BF16)
            for cp in out_copies(i, slot):
                cp.start()

            @pl.when(i == n_tiles - 1)
            def _():
                for cp in out_copies(i, slot):
                    cp.wait()
                if n_tiles > 1:
                    for cp in out_copies(i - 1, 1 - slot):
                        cp.wait()

    operands, aliases = [a, b], {}
    if slab is not None:
        in_specs = in_specs + [ANY]
        operands.append(slab)
        aliases = {2: 0}
    return pl.pallas_call(
        body_into, name=name, grid=(n_tiles, n_steps), in_specs=in_specs, out_specs=ANY,
        out_shape=jax.ShapeDtypeStruct((N_CHIPS, slab_rows, D_MODEL), BF16), input_output_aliases=aliases,
        scratch_shapes=[pltpu.VMEM((tm, n_out), F32), pltpu.VMEM((2, tm, n_out), BF16),
                        pltpu.SemaphoreType.DMA((2, per_tile))],
        compiler_params=_params(n_axes=2),
    )(*operands)


def _dw_pool(pooled, dyp):
    s_len = pooled.shape[0]
    tk = 512

    def body(a_ref, b_ref, o_ref):
        @pl.when(pl.program_id(0) == 0)
        def _():
            o_ref[...] = jnp.zeros_like(o_ref)

        for g in range(4):
            cols = slice(128 * g, 128 * g + 128)
            o_ref[g] += _dot(a_ref[:, cols], b_ref[:, cols], 0, 0)

    blk = pl.BlockSpec((tk, POOL_WIDTH), lambda k: (k, 0))
    return pl.pallas_call(
        body, name="dw_pool", grid=(s_len // tk,), in_specs=[blk, blk], out_specs=_full((4, 128, 128)),
        out_shape=jax.ShapeDtypeStruct((4, 128, 128), F32), compiler_params=_params(),
    )(pooled, dyp)


def _position():
    x, y, c = lax.axis_index("x"), lax.axis_index("y"), lax.axis_index("c")
    other_chips = [(1 - x, y), (x, 1 - y), (1 - x, 1 - y)]
    return x, y, c, other_chips


def _ag_weights(local_slab, row0, n_rows, name, collective_id):
    half = n_rows // 2
    quarter = half // 2
    assert quarter % 16 == 0

    def body(l_ref, g_ref, send, recv):
        x, y, c, chips = _position()
        me, (via_x, via_y, diagonal) = 2 * x + y, [2 * chip[0] + chip[1] for chip in chips]
        here, sibling, x_nbr, y_nbr = (x, y, c), (x, y, 1 - c), (1 - x, y, c), (x, 1 - y, c)
        peers = [sibling, x_nbr, y_nbr]
        barrier = pltpu.get_barrier_semaphore()
        for peer in peers:
            pl.semaphore_signal(barrier, inc=1, device_id=peer, device_id_type=MESH)
        pl.semaphore_wait(barrier, len(peers))

        def rows(core, part):
            start, size = (core * half, half) if part is None else (core * half + part * quarter, quarter)
            return pl.ds(pl.multiple_of(start, 16), size)

        def copy(k, chip_idx, where, to, src=None):
            dst = g_ref.at[chip_idx, where, :]
            return pltpu.make_async_remote_copy(src_ref=dst if src is None else src, dst_ref=dst, send_sem=send.at[k],
                                                recv_sem=recv.at[k], device_id=to, device_id_type=MESH)

        own_rows = l_ref.at[pl.ds(pl.multiple_of(row0 + c * half, 16), half), :]
        started = [copy(0, me, rows(c, None), x_nbr, src=own_rows), copy(1, me, rows(c, None), y_nbr, src=own_rows)]
        for cp in started:
            cp.start()
        after_arrival = [
            (copy(0, via_x, rows(c, None), here), [copy(4, via_x, rows(c, None), sibling), copy(3, via_x, rows(c, 1), y_nbr)]),
            (copy(1, via_y, rows(c, None), here), [copy(5, via_y, rows(c, None), sibling), copy(2, via_y, rows(c, 0), x_nbr)]),
            (copy(2, diagonal, rows(c, 0), here), [copy(6, diagonal, rows(c, 0), sibling)]),
            (copy(3, diagonal, rows(c, 1), here), [copy(7, diagonal, rows(c, 1), sibling)]),
        ]
        for arrival, onward in after_arrival:
            arrival.wait_recv()
            for cp in onward:
                cp.start()
            started += onward
        for cp in (copy(4, via_x, rows(1 - c, None), here), copy(5, via_y, rows(1 - c, None), here),
                   copy(6, diagonal, rows(1 - c, 0), here), copy(7, diagonal, rows(1 - c, 1), here)):
            cp.wait_recv()
        for cp in started:
            cp.wait_send()

    return pl.kernel(
        body, out_type=jax.ShapeDtypeStruct((N_CHIPS, n_rows, D_MODEL), BF16),
        mesh=plsc.ScalarSubcoreMesh(axis_name="sequencer", num_cores=1), name=name,
        scratch_types=[pltpu.SemaphoreType.DMA((8,)), pltpu.SemaphoreType.DMA((8,))],
        compiler_params=pltpu.CompilerParams(collective_id=collective_id),
    )(local_slab)


def _comm_call(body, peers_of, out_shape, n_sems, operand, name, collective_id):
    sems = [pltpu.SemaphoreType.DMA((n_sems,)), pltpu.SemaphoreType.DMA((n_sems,))]
    if collective_id is None:
        return pl.pallas_call(body, name=name, in_specs=[ANY], out_specs=ANY, out_shape=out_shape, scratch_shapes=sems)(operand)

    def with_handshake(in_ref, out_ref, send, recv):
        x, y, c, _ = _position()
        peers = peers_of(x, y, c)
        barrier = pltpu.get_barrier_semaphore()
        for peer in peers:
            pl.semaphore_signal(barrier, inc=1, device_id=peer, device_id_type=MESH)
        pl.semaphore_wait(barrier, len(peers))
        body(in_ref, out_ref, send, recv)

    return pl.kernel(with_handshake, out_type=out_shape, mesh=plsc.ScalarSubcoreMesh(axis_name="sequencer", num_cores=1),
                     name=name, scratch_types=sems, compiler_params=pltpu.CompilerParams(collective_id=collective_id))(operand)


def _rs_swap_halves(partial, name, collective_id=None):
    half = partial.shape[1] // 2

    def body(p_ref, r_ref, send, recv):
        x, y, c, _ = _position()
        theirs = pl.ds(pl.multiple_of((1 - c) * half, 16), half)
        cp = pltpu.make_async_remote_copy(src_ref=p_ref.at[:, theirs, :], dst_ref=r_ref, send_sem=send.at[0],
                                          recv_sem=recv.at[0], device_id=(x, y, 1 - c), device_id_type=MESH)
        cp.start()
        cp.wait()

    return _comm_call(body, lambda x, y, c: [(x, y, 1 - c)], jax.ShapeDtypeStruct((N_CHIPS, half, D_MODEL), BF16), 1,
                      partial, name, collective_id)


def _rs_add_halves(partial, other, core, name, after):
    half = other.shape[1]
    t = half // 2
    steps = half // t

    def body(core_ref, a_ref, b_ref, after_ref, o_ref):
        del after_ref
        o_ref[...] = (a_ref[...].astype(F32) + b_ref[...].astype(F32)).astype(BF16)

    return pl.pallas_call(
        body, name=name,
        grid_spec=pltpu.PrefetchScalarGridSpec(
            num_scalar_prefetch=1, grid=(N_CHIPS, steps),
            in_specs=[pl.BlockSpec((1, t, D_MODEL), lambda j, i, core_ref: (j, core_ref[0] * steps + i, 0)),
                      pl.BlockSpec((1, t, D_MODEL), lambda j, i, core_ref: (j, i, 0)), ANY],
            out_specs=pl.BlockSpec((1, t, D_MODEL), lambda j, i, core_ref: (j, i, 0))),
        out_shape=jax.ShapeDtypeStruct((N_CHIPS, half, D_MODEL), BF16),
        compiler_params=_params(n_axes=2),
    )(core, partial, other, after)


def _rs_exchange_chips(pre, name, collective_id=None):
    def body(s_ref, r_ref, send, recv):
        x, y, c, chips = _position()

        def copy(k, chunk, to):
            return pltpu.make_async_remote_copy(src_ref=s_ref.at[chunk], dst_ref=r_ref.at[k], send_sem=send.at[k],
                                                recv_sem=recv.at[k], device_id=to, device_id_type=MESH)

        sends = [copy(k, 2 * chip[0] + chip[1], (*chip, c)) for k, chip in enumerate(chips)]
        for cp in sends:
            cp.start()
        for cp in sends:
            cp.wait()

    return _comm_call(body, lambda x, y, c: [(1 - x, y, c), (x, 1 - y, c), (1 - x, 1 - y, c)],
                      jax.ShapeDtypeStruct((3, pre.shape[1], D_MODEL), BF16), 3, pre, name, collective_id)


def _rs_sum_chips(pre, received, place, name, after):
    half = pre.shape[1]
    t = half // 2 if half > 512 else half
    steps = half // t

    def body(place_ref, own_ref, r_ref, after_ref, o_ref):
        del after_ref
        acc = own_ref[0].astype(F32)
        for k in range(3):
            acc = acc + r_ref[k].astype(F32)
        o_ref[...] = acc

    return pl.pallas_call(
        body, name=name,
        grid_spec=pltpu.PrefetchScalarGridSpec(
            num_scalar_prefetch=1, grid=(steps,),
            in_specs=[pl.BlockSpec((1, t, D_MODEL), lambda i, place_ref: (place_ref[0], i, 0)),
                      pl.BlockSpec((3, t, D_MODEL), lambda i, place_ref: (0, i, 0)), ANY],
            out_specs=pl.BlockSpec((t, D_MODEL), lambda i, place_ref: (place_ref[1] * steps + i, 0))),
        out_shape=jax.ShapeDtypeStruct((2 * half, D_MODEL), F32),
        compiler_params=_params(),
    )(place, pre, received, after)


def _half_swap(g_ref, core, to, send, recv, k):
    half = g_ref.shape[0] // 2
    rows = g_ref.at[pl.ds(pl.multiple_of(core * half, 8), half), :]
    return pltpu.make_async_remote_copy(src_ref=rows, dst_ref=rows, send_sem=send.at[k], recv_sem=recv.at[k],
                                        device_id=to, device_id_type=MESH)


def _rs_finish_rows(grads, name, after):
    def body(f_ref, after_ref, g_ref, send, recv):
        del f_ref, after_ref
        x, y, c, _ = _position()
        mine = _half_swap(g_ref, c, (x, y, 1 - c), send, recv, 0)
        mine.start()
        _half_swap(g_ref, 1 - c, (x, y, c), send, recv, 0).wait_recv()
        mine.wait_send()

    return pl.pallas_call(
        body, name=name, in_specs=[ANY, ANY], out_specs=ANY, input_output_aliases={0: 0},
        out_shape=jax.ShapeDtypeStruct(grads.shape, F32),
        scratch_shapes=[pltpu.SemaphoreType.DMA((1,)), pltpu.SemaphoreType.DMA((1,))],
    )(grads, after)


def _small_gather(small, collective_id):
    def body(s_ref, t_ref, send, recv):
        x, y, c, chips = _position()
        sibling = (x, y, 1 - c)

        def slot(px, py, pc):
            return t_ref.at[4 * px + 2 * py + pc]

        def copy(k, block, to, src=None):
            return pltpu.make_async_remote_copy(src_ref=slot(*block) if src is None else src, dst_ref=slot(*block),
                                                send_sem=send.at[k], recv_sem=recv.at[k], device_id=to, device_id_type=MESH)

        own = pltpu.make_async_copy(s_ref, slot(x, y, c), send.at[7])
        own.start()
        first = [copy(0, (x, y, c), sibling, src=s_ref)]
        first += [copy(1 + k, (x, y, c), (*chip, c), src=s_ref) for k, chip in enumerate(chips)]
        for cp in first:
            cp.start()
        passed = []
        for k, chip in enumerate(chips):
            copy(1 + k, (*chip, c), (x, y, c)).wait_recv()
            fwd = copy(4 + k, (*chip, c), sibling)
            fwd.start()
            passed.append(fwd)
        copy(0, sibling, (x, y, c)).wait_recv()
        for k, chip in enumerate(chips):
            copy(4 + k, (*chip, 1 - c), (x, y, c)).wait_recv()
        for cp in first + passed:
            cp.wait_send()
        own.wait()

    peers_of = lambda x, y, c: [(x, y, 1 - c), (1 - x, y, c), (x, 1 - y, c), (1 - x, 1 - y, c)]
    return _comm_call(body, peers_of, jax.ShapeDtypeStruct((N_DEV, SMALL_ROWS, 128), F32), 8, small, "small_gather",
                      collective_id)


def _adam_update(w, g, m, v):
    m_new = ADAM_B1 * m + (1.0 - ADAM_B1) * g
    v_new = ADAM_B2 * v + (1.0 - ADAM_B2) * (g * g)
    m_hat = m_new / (1.0 - ADAM_B1 ** ADAM_STEP)
    v_hat = v_new / (1.0 - ADAM_B2 ** ADAM_STEP)
    return -ADAM_LR * (m_hat / (jnp.sqrt(v_hat) + ADAM_EPS) + ADAM_WD * w), m_new, v_new


def _adamw(w, g_rows, row_off, m, v, name):
    rows, cols = w.shape
    t = rows if rows <= 320 else (rows // 2 if rows % 256 else 256)

    def body(w_ref, g_ref, m_ref, v_ref, go_ref, d_ref, nm_ref, nv_ref):
        g = g_ref[...]
        go_ref[...] = g
        d_ref[...], nm_ref[...], nv_ref[...] = _adam_update(w_ref[...], g, m_ref[...], v_ref[...])

    blk = pl.BlockSpec((t, cols), lambda i: (i, 0))
    assert row_off % 8 == 0 and t % 8 == 0
    g_blk = pl.BlockSpec((pl.Element(t), pl.Element(cols)), lambda i: (pl.multiple_of(row_off + i * t, 8), 0))
    shape = jax.ShapeDtypeStruct((rows, cols), F32)
    return pl.pallas_call(
        body, name=name, grid=(rows // t,), in_specs=[blk, g_blk, blk, blk], out_specs=[blk] * 4, out_shape=[shape] * 4,
        compiler_params=_params(),
    )(w, g_rows, m, v)


SMALL_PARAMS = [("g_attn", (1, D_MODEL), 8), ("g_q", (1, HEAD_DIM), None), ("g_k", (1, HEAD_DIM), None),
                ("sinks", (1, N_Q_HEADS), None), ("rel_bias", (N_BUCKETS, N_Q_HEADS), None), ("w_pool", (512, 128), None),
                ("pool_scale", (1, POOL_WIDTH), 4), ("g_ffn", (1, D_MODEL), 8), ("g_ple", (1, D_MODEL), 8)]


def _adamw_small(tables, wmv):
    n_par = len(SMALL_PARAMS)

    def body(*refs):
        t_ref = refs[0]
        ins = refs[1:1 + 3 * n_par]
        loss_ref = refs[1 + 3 * n_par]
        outs = refs[2 + 3 * n_par:-1]
        tot_ref = refs[-1]
        total = t_ref[0]
        for d in range(1, N_DEV):
            total = total + t_ref[d]
        tot_ref[...] = total
        loss_ref[...] = tot_ref[pl.ds(SMALL["loss"], 1), 0:1]
        for i, (name, shape, split) in enumerate(SMALL_PARAMS):
            g_ref, d_ref, nm_ref, nv_ref = outs[4 * i:4 * i + 4]
            row = SMALL[name]
            if split:
                for k in range(split):
                    g_ref[:, 128 * k:128 * k + 128] = tot_ref[pl.ds(row + k, 1), :]
            else:
                g_ref[...] = tot_ref[pl.ds(row, shape[0]), 0:shape[1]]
            w_ref, m_ref, v_ref = ins[3 * i:3 * i + 3]
            d_ref[...], nm_ref[...], nv_ref[...] = _adam_update(w_ref[...], g_ref[...], m_ref[...], v_ref[...])

    shapes = [jax.ShapeDtypeStruct((1, 1), F32)]
    for _, shape, _ in SMALL_PARAMS:
        shapes += [jax.ShapeDtypeStruct(shape, F32)] * 4
    flat = [a for triple in wmv for a in triple]
    res = pl.pallas_call(
        body, name="adamw_small", in_specs=[VMEM_WHOLE] * (1 + 3 * n_par), out_specs=[VMEM_WHOLE] * len(shapes),
        out_shape=shapes, scratch_shapes=[pltpu.VMEM((SMALL_ROWS, 128), F32)],
    )(tables, *flat)
    return res[0], [res[1 + 4 * i:5 + 4 * i] for i in range(n_par)]


def _pack_ple_proj(shard):
    return shard.reshape(4, 64, 256).transpose(1, 0, 2).reshape(64, D_MODEL)


class _Reduction:
    def __init__(self, tag, place, ids=(None, None)):
        self.tag, self.place, self.ids = tag, place, ids

    def start(self, partial):
        self.partial = partial
        self.other = _rs_swap_halves(partial, "rs_swap_" + self.tag, self.ids[0])
        return partial

    def middle(self, after):
        self.pre = _rs_add_halves(self.partial, self.other, self.place[1:], "rs_add_" + self.tag, after)
        self.received = _rs_exchange_chips(self.pre, "rs_exchange_" + self.tag, self.ids[1])
        return self.pre

    def finish(self, after):
        return _rs_sum_chips(self.pre, self.received, self.place, "rs_sum_" + self.tag, after)


def _local_grads(x2, p2, tgt, wts, g_attn_norm, g_q, g_k, attn_sinks, rel_bias, w_pool, pool_scale, g_ffn_norm, g_ple_norm,
                 reduce_a):
    w_early, w_late = wts
    w_in = w_out = w_early
    bucket = jnp.asarray(_bucket_table())
    gq = jnp.tile(g_q, (1, 2))
    gk = jnp.tile(g_k, (1, 2))
    wpool = w_pool[0].astype(BF16)
    sinks = attn_sinks[0]
    bias_st = _bias_build(rel_bias.T, bucket)

    hn1, zqk, u, kn, vb, qst = _attn_in(x2, g_attn_norm, gq, gk, w_in)
    ost = _attn_fwd(qst, kn, vb, bias_st, sinks)
    pooled, mix, h1, hn2 = _mix_out(u, ost, x2, w_out, wpool, pool_scale, g_ffn_norm)
    loss_v, dgate, dup, act, dh2, hn3, dgl, dw_plp, dh1, dg_ffn, dg_ple = _ffn_ple(hn2, h1, p2, tgt, w_late, g_ffn_norm,
                                                                                      g_ple_norm)

    late0, late_rows = GATHER_PARTS[1][0], SLAB_ROWS - GATHER_PARTS[1][0]
    partial_a = None
    for name, lhs, rhs in (("gateT", dgate, hn2), ("upT", dup, hn2), ("down", act, dh2), ("plg", hn3, dgl)):
        partial_a = _dw(lhs, rhs, "dw_" + name, into=(partial_a, late_rows, SLAB[name][0] - late0))
    dw_plp = dw_plp.reshape(4, 64, N_CHIPS, 256).transpose(2, 1, 0, 3).reshape(N_CHIPS, 64, D_MODEL)
    partial_a = reduce_a.start(lax.dynamic_update_slice(partial_a, dw_plp, (0, SLAB["plp"][0] - late0, 0)))
    dost, du, dyp, dscale, partial_b = _mix_out_bwd(dh1, w_out, pooled, wpool, pool_scale, mix, partial_a)
    pre_a = reduce_a.middle(du)
    dqst, dk, dv, dbias, dsink_rows = _attn_bwd(qst, kn, vb, dost, bias_st, sinks, pre_a)
    dx, dg_attn, dgq, dgk, partial_b = _attn_in_bwd(dqst, zqk, dk, dv, du, x2, dh1, hn1, partial_b, w_in, g_attn_norm, gq, gk)

    small = _small_pack(dg_attn, dg_ffn, dg_ple, dscale, dgq, dgk, dbias, dsink_rows, bucket, loss_v, _dw_pool(pooled, dyp))
    return dx, partial_b, small


def kernel(x, p, w_in, w_out, g_attn_norm, g_q, g_k, attn_sinks, rel_bias, w_pool, pool_scale, g_ffn_norm, w_gate, w_up, w_down, g_ple_norm, w_ple_gate, w_ple_proj, loss_target, m_w_in, m_w_out, m_g_attn_norm, m_g_q, m_g_k, m_attn_sinks, m_rel_bias, m_w_pool, m_pool_scale, m_g_ffn_norm, m_w_gate, m_w_up, m_w_down, m_g_ple_norm, m_w_ple_gate, m_w_ple_proj, v_w_in, v_w_out, v_g_attn_norm, v_g_q, v_g_k, v_attn_sinks, v_rel_bias, v_w_pool, v_pool_scale, v_g_ffn_norm, v_w_gate, v_w_up, v_w_down, v_g_ple_norm, v_w_ple_gate, v_w_ple_proj):
    core = lax.axis_index("c").astype(jnp.int32).reshape(1)
    me = (2 * lax.axis_index("x") + lax.axis_index("y")).astype(jnp.int32).reshape(1)

    local_parts = [jnp.concatenate(pieces, axis=0).astype(BF16) for pieces in (
        [w_in[0].T, w_out[0]], [w_gate[0].T, w_up[0].T, w_down[0], w_ple_gate[0], _pack_ple_proj(w_ple_proj[0])])]
    wts = [(_ag_weights(local, 0, local.shape[0], name, collective_id), local, me)
           for local, name, collective_id in zip(local_parts, ("ag_early", "ag_late"), (1, 2))]

    place = jnp.concatenate([me, core])
    reduce_a = _Reduction("a", place, ids=(3, 4))
    dx, partial_b, small = _local_grads(x[0], p[0, 0], loss_target[0], wts, g_attn_norm, g_q, g_k, attn_sinks, rel_bias,
                                        w_pool, pool_scale, g_ffn_norm, g_ple_norm, reduce_a)
    reduce_b = _Reduction("b", place, ids=(6, 7))
    reduce_b.start(partial_b)
    small_all = _small_gather(small, 8)
    summed_a = reduce_a.finish(small)
    pre_b = reduce_b.middle(summed_a)
    grads_a = _rs_finish_rows(summed_a, "rs_finish_a", pre_b)

    late0 = GATHER_PARTS[1][0]

    def rows(name):
        return grads_a, SLAB[name][0] - late0

    plp_rows = grads_a[SLAB["plp"][0] - late0:]
    big = {
        "w_gate": (w_gate, m_w_gate, v_w_gate, rows("gateT"), True),
        "w_up": (w_up, m_w_up, v_w_up, rows("upT"), True),
        "w_down": (w_down, m_w_down, v_w_down, rows("down"), False),
        "w_ple_gate": (w_ple_gate, m_w_ple_gate, v_w_ple_gate, rows("plg"), False),
        "w_ple_proj": (w_ple_proj, m_w_ple_proj, v_w_ple_proj,
                       (plp_rows.reshape(64, 4, 256).transpose(1, 0, 2).reshape(PLE_DIM, PLE_DIM), 0), False),
        "w_out": (w_out, m_w_out, v_w_out, None, False),
        "w_in": (w_in, m_w_in, v_w_in, None, True),
    }
    small_params = {
        "g_attn_norm": (g_attn_norm, m_g_attn_norm, v_g_attn_norm), "g_q": (g_q, m_g_q, v_g_q), "g_k": (g_k, m_g_k, v_g_k),
        "attn_sinks": (attn_sinks, m_attn_sinks, v_attn_sinks), "rel_bias": (rel_bias, m_rel_bias, v_rel_bias),
        "w_pool": tuple(a.reshape(512, 128) for a in (w_pool, m_w_pool, v_w_pool)),
        "pool_scale": (pool_scale, m_pool_scale, v_pool_scale), "g_ffn_norm": (g_ffn_norm, m_g_ffn_norm, v_g_ffn_norm),
        "g_ple_norm": (g_ple_norm, m_g_ple_norm, v_g_ple_norm),
    }

    grads, deltas, new_ms, new_vs = {}, {}, {}, {}
    out = grads_b = None
    for name, (w, m, v, g_src, transposed) in big.items():
        if g_src is None:
            if grads_b is None:
                grads_b = _rs_finish_rows(reduce_b.finish(out[-1]), "rs_finish_b", out[-1])
            g_src = (grads_b, SLAB["out" if name == "w_out" else "inT"][0])
        view = (lambda a: a.T) if transposed else (lambda a: a)
        out = _adamw(view(w[0]), *g_src, view(m[0]), view(v[0]), "adamw_" + name)
        grads[name], deltas[name], new_ms[name], new_vs[name] = (view(a)[None] for a in out)

    loss, small_out = _adamw_small(small_all, list(small_params.values()))
    for name, (g2, d, nm, nv) in zip(small_params, small_out):
        shape = w_pool.shape if name == "w_pool" else g2.shape
        grads[name], deltas[name], new_ms[name], new_vs[name] = (a.reshape(shape) for a in (g2, d, nm, nv))

    order = ["w_in", "w_out", "g_attn_norm", "g_q", "g_k", "attn_sinks", "rel_bias", "w_pool", "pool_scale", "g_ffn_norm",
             "w_gate", "w_up", "w_down", "g_ple_norm", "w_ple_gate", "w_ple_proj"]
    return (loss.reshape(()), dx[None], *[grads[n] for n in order], *[deltas[n] for n in order],
            *[new_ms[n] for n in order], *[new_vs[n] for n in order])
```

```python
import functools

import numpy as np
import jax
import jax.numpy as jnp
from jax import lax
from jax.experimental import pallas as pl
from jax.experimental.pallas import tpu as pltpu
from jax.experimental.pallas import tpu_sc as plsc

F32 = jnp.float32
BF16 = jnp.bfloat16
MESH = pl.DeviceIdType.MESH

D_MODEL = 1024
HEAD_DIM = 64
N_Q_HEADS = 8
ATTN_WIDTH = 512
KV_WIDTH = 128
POOL_WIDTH = 512
IN_WIDTH = 1280
D_FF = 2816
PLE_DIM = 256
FF_CHUNK = 704
BLOCK = 128
N_BUCKETS = 32
MAX_DISTANCE = 128
POOL_SIZES = (2, 4, 8, 16)
EPS = 1e-6
NEG = -1e30
N_CHIPS = 4
N_DEV = 8

ADAM_LR = 0.001
ADAM_B1 = 0.9
ADAM_B2 = 0.999
ADAM_EPS = 1e-08
ADAM_WD = 0.01
ADAM_STEP = 10

SLAB = {"inT": (0, 320), "out": (320, 256), "gateT": (576, 704), "upT": (1280, 704), "down": (1984, 704),
        "plg": (2688, 256), "plp": (2944, 64)}
SLAB_ROWS = 3008
HALF_ROWS = SLAB_ROWS // 2
GATHER_PARTS = ((0, 576), (576, SLAB_ROWS))
POOL_HALO = 24

SMALL = {"g_attn": 0, "g_ffn": 8, "g_ple": 16, "pool_scale": 24, "g_q": 28, "g_k": 29, "sinks": 30, "loss": 31,
         "rel_bias": 32, "w_pool": 64}
SMALL_ROWS = 576

VMEM_LIMIT_BIG = 60 * 1024 * 1024
VMEM_LIMIT = 48 * 1024 * 1024


def _params(vmem=VMEM_LIMIT, n_axes=1):
    return pltpu.CompilerParams(dimension_semantics=("arbitrary",) * n_axes, vmem_limit_bytes=vmem)


def _dot(a, b, ca, cb):
    return lax.dot_general(a, b, (((ca,), (cb,)), ((), ())), preferred_element_type=F32)


def _full(shape):
    return pl.BlockSpec(shape, lambda i: (0,) * len(shape))


ANY = pl.BlockSpec(memory_space=pl.ANY)
VMEM_WHOLE = pl.BlockSpec(memory_space=pltpu.VMEM)


W_SPECS = [ANY, ANY, pl.BlockSpec(memory_space=pltpu.SMEM)]


def _load_rows(w_refs, name, dst_ref, sems):
    slab_ref, local_ref, me_ref = w_refs
    off, rows = SLAB[name]
    slab_off = off - max(start for start, _ in GATHER_PARTS if start <= off)
    me = me_ref[0]
    for phase in ("start", "wait"):
        for j in range(N_CHIPS):
            dst = dst_ref.at[pl.ds(j * rows, rows), :]
            theirs = pltpu.make_async_copy(slab_ref.at[j, pl.ds(slab_off, rows), :], dst, sems.at[j])
            own = pltpu.make_async_copy(local_ref.at[pl.ds(slab_off, rows), :], dst, sems.at[j])

            @pl.when(me == j)
            def _():
                getattr(own, phase)()

            @pl.when(me != j)
            def _():
                getattr(theirs, phase)()


def _rms_fwd(x, g):
    r = lax.rsqrt(jnp.mean(x * x, axis=-1, keepdims=True) + EPS)
    return x * r * g


def _rms_bwd(x, g, dy):
    r = lax.rsqrt(jnp.mean(x * x, axis=-1, keepdims=True) + EPS)
    xn = x * r
    dyg = dy * g
    dx = r * (dyg - xn * jnp.mean(dyg * xn, axis=-1, keepdims=True))
    return dx, jnp.sum(dy * xn, axis=0, keepdims=True)


def _half_sum(v, lo):
    s_lo = jnp.sum(jnp.where(lo, v, 0.0), axis=-1, keepdims=True)
    s_hi = jnp.sum(jnp.where(lo, 0.0, v), axis=-1, keepdims=True)
    return jnp.where(lo, s_lo, s_hi)


def _half_sum_mxu(v):
    upper = lax.broadcasted_iota(jnp.int32, (128, 128), 0) < 64
    left = lax.broadcasted_iota(jnp.int32, (128, 128), 1) < 64
    ones = jnp.where(upper == left, 1.0, 0.0).astype(BF16)
    high = v.astype(BF16)
    low = (v - high.astype(F32)).astype(BF16)
    return _dot(high, ones, 1, 0) + _dot(low, ones, 1, 0)


def _pair_norm(zp, g, lo):
    r = lax.rsqrt(_half_sum(zp * zp, lo) * (1.0 / HEAD_DIM) + EPS)
    return zp * r * g


def _pair_norm_bwd(zp, g, dy):
    r = lax.rsqrt(_half_sum_mxu(zp * zp) * (1.0 / HEAD_DIM) + EPS)
    xn = zp * r
    dyg = dy * g
    dx = r * (dyg - xn * (_half_sum_mxu(dyg * xn) * (1.0 / HEAD_DIM)))
    return dx, jnp.sum(dy * xn, axis=0, keepdims=True)


def _to_stacked(pair, group, lo):
    rolled = pltpu.roll(pair, 64, axis=1)
    if group == 0:
        return jnp.where(lo, pair, 0.0), jnp.where(lo, rolled, 0.0)
    return jnp.where(lo, 0.0, rolled), jnp.where(lo, 0.0, pair)


def _from_stacked(even, odd, group, lo):
    if group == 0:
        return jnp.where(lo, even, pltpu.roll(odd, 64, axis=1))
    return jnp.where(lo, pltpu.roll(even, 64, axis=1), odd)


def _sigmoid(v):
    return 1.0 / (1.0 + jnp.exp(-v))


def _pool_counts(tile, n_rows):
    t1 = tile * n_rows + lax.broadcasted_iota(jnp.int32, (n_rows, POOL_WIDTH), 0) + 1
    lane = lax.broadcasted_iota(jnp.int32, (n_rows, POOL_WIDTH), 1)
    win = jnp.where(lane < 128, 2, jnp.where(lane < 256, 4, jnp.where(lane < 384, 8, 16)))
    return jnp.minimum(t1, win).astype(F32)


def _attn_in(x2, g_attn, gq, gk, wts):
    s_len = x2.shape[0]
    t = 512

    def body(x_ref, g_ref, gq_ref, gk_ref, sl_ref, lo_ref, me_ref, hn_ref, zqk_ref, u_ref, kn_ref, v_ref, qst_ref, w_ref, sems):
        @pl.when(pl.program_id(0) == 0)
        def _():
            _load_rows((sl_ref, lo_ref, me_ref), "inT", w_ref, sems)

        hn = _rms_fwd(x_ref[...], g_ref[...]).astype(BF16)
        hn_ref[...] = hn
        z = _dot(hn, w_ref[...], 1, 1)
        zqk_ref[...] = z[:, :640]
        u_ref[...] = z[:, 768:]
        v_ref[...] = z[:, 640:768].astype(BF16)
        lo = lax.broadcasted_iota(jnp.int32, (t, 128), 1) < 64
        kn_ref[...] = _pair_norm(z[:, 512:640], gk_ref[...], lo).astype(BF16)
        for p in range(4):
            qn = _pair_norm(z[:, 128 * p:128 * p + 128], gq_ref[...], lo)
            even, odd = _to_stacked(qn, p // 2, lo)
            qst_ref[2 * p] = even.astype(BF16)
            qst_ref[2 * p + 1] = odd.astype(BF16)

    row = lambda w: pl.BlockSpec((t, w), lambda i: (i, 0))
    return pl.pallas_call(
        body, name="attn_in", grid=(s_len // t,),
        in_specs=[row(D_MODEL), _full((1, D_MODEL)), _full((1, 128)), _full((1, 128))] + W_SPECS,
        out_specs=[row(D_MODEL), row(640), row(POOL_WIDTH), row(128), row(128),
                   pl.BlockSpec((N_Q_HEADS, t, 128), lambda i: (0, i, 0))],
        out_shape=[jax.ShapeDtypeStruct((s_len, D_MODEL), BF16), jax.ShapeDtypeStruct((s_len, 640), F32),
                   jax.ShapeDtypeStruct((s_len, POOL_WIDTH), F32), jax.ShapeDtypeStruct((s_len, 128), BF16),
                   jax.ShapeDtypeStruct((s_len, 128), BF16), jax.ShapeDtypeStruct((N_Q_HEADS, s_len, 128), BF16)],
        scratch_shapes=[pltpu.VMEM((IN_WIDTH, D_MODEL), BF16), pltpu.SemaphoreType.DMA((N_CHIPS,))],
        compiler_params=_params(),
    )(x2, g_attn, gq, gk, *wts)


def _bucket_table():
    i_idx = np.arange(BLOCK)[:, None]
    j_idx = np.arange(2 * BLOCK)[None, :]
    d = BLOCK + i_idx - j_idx
    n = np.maximum(d, 0)
    max_exact = N_BUCKETS // 2
    nf = np.maximum(n, 1).astype(np.float64)
    large = max_exact + (np.log(nf / max_exact) / np.log(MAX_DISTANCE / max_exact) * (N_BUCKETS - max_exact)).astype(np.int64)
    large = np.minimum(large, N_BUCKETS - 1)
    bucket = np.where(n < max_exact, n, large)
    return np.where((d >= 0) & (d < BLOCK), bucket, -1).astype(np.int32)


def _bias_build(rel_bias_t, bucket):
    def body(rb_ref, bucket_ref, out_ref):
        bk = bucket_ref[...]
        for h in range(N_Q_HEADS):
            acc = jnp.full((BLOCK, 2 * BLOCK), NEG, F32)
            for b in range(N_BUCKETS):
                acc = jnp.where(bk == b, rb_ref[h, b], acc)
            out_ref[0, pl.ds(h * BLOCK, BLOCK), :] = acc
            out_ref[1, pl.ds(h * BLOCK, BLOCK), :] = acc
            out_ref[1, pl.ds(h * BLOCK, BLOCK), 0:BLOCK] = jnp.full((BLOCK, BLOCK), NEG, F32)

    return pl.pallas_call(
        body, name="bias_build",
        in_specs=[pl.BlockSpec(memory_space=pltpu.SMEM), VMEM_WHOLE], out_specs=VMEM_WHOLE,
        out_shape=jax.ShapeDtypeStruct((2, N_Q_HEADS * BLOCK, 2 * BLOCK), F32),
    )(rel_bias_t, bucket)


def _head_softmax(s_ref, bias_ref, sink_ref, h):
    rows = pl.ds(pl.multiple_of(h * BLOCK, BLOCK), BLOCK)
    s = s_ref[rows, :] * (HEAD_DIM ** -0.5) + bias_ref[rows, :]
    sink = sink_ref[h]
    m = jnp.maximum(jnp.max(s, axis=-1, keepdims=True), sink)
    p = jnp.exp(s - m)
    e_sink = jnp.exp(sink - m)
    inv = 1.0 / (jnp.sum(p, axis=-1, keepdims=True) + e_sink)
    return rows, p * inv, e_sink * inv


def _attn_specs():
    prev = lambda i: (jnp.maximum(i - 1, 0), 0)
    cur = lambda i: (i, 0)
    stacked = pl.BlockSpec((N_Q_HEADS, BLOCK, 128), lambda i: (0, i, 0))
    kv = [pl.BlockSpec((BLOCK, 128), prev), pl.BlockSpec((BLOCK, 128), cur)]
    consts = [pl.BlockSpec((None, N_Q_HEADS * BLOCK, 2 * BLOCK), lambda i: (jnp.where(i == 0, 1, 0), 0, 0)),
              pl.BlockSpec(memory_space=pltpu.SMEM)]
    return stacked, kv, consts


def _head_lane_mask():
    rows = lax.broadcasted_iota(jnp.int32, (N_Q_HEADS * BLOCK, 128), 0)
    lanes = lax.broadcasted_iota(jnp.int32, (N_Q_HEADS * BLOCK, 128), 1)
    return (rows < 4 * BLOCK) == (lanes < 64)


def _attn_fwd(qst, kn, vb, bias_st, sinks):
    s_len = kn.shape[0]

    def body(q_ref, kp_ref, kc_ref, vp_ref, vc_ref, bias_ref, sink_ref, o_ref, s_ref, p_ref):
        q = q_ref[...].reshape(N_Q_HEADS * BLOCK, 128)
        s_ref[...] = _dot(q, jnp.concatenate([kp_ref[...], kc_ref[...]], axis=0), 1, 1)

        def head(h, carry):
            rows, probs, _ = _head_softmax(s_ref, bias_ref, sink_ref, h)
            p_ref[rows, :] = probs.astype(BF16)
            return carry

        lax.fori_loop(0, N_Q_HEADS, head, 0, unroll=True)
        o = _dot(p_ref[...], jnp.concatenate([vp_ref[...], vc_ref[...]], axis=0), 1, 0)
        o_ref[...] = jnp.where(_head_lane_mask(), o, 0.0).astype(BF16).reshape(N_Q_HEADS, BLOCK, 128)

    stacked, kv, consts = _attn_specs()
    return pl.pallas_call(
        body, name="attn_fwd", grid=(s_len // BLOCK,),
        in_specs=[stacked] + kv + kv + consts, out_specs=stacked,
        out_shape=jax.ShapeDtypeStruct((N_Q_HEADS, s_len, 128), BF16),
        scratch_shapes=[pltpu.VMEM((N_Q_HEADS * BLOCK, 2 * BLOCK), F32), pltpu.VMEM((N_Q_HEADS * BLOCK, 2 * BLOCK), BF16)],
        compiler_params=_params(),
    )(qst, kn, kn, vb, vb, bias_st, sinks)


def _mix_out(u, ost, x2, wts, wpool, pool_scale, g_ffn):
    s_len = x2.shape[0]
    t = 512
    n = t + 16

    def body(u_ref, o_ref, x_ref, sl_ref, lo_ref, me_ref, wp_ref, sc_ref, g_ref, pooled_ref, mix_ref, h1_ref, hn_ref,
             w_ref, ext_ref, st_ref, sems):
        i = pl.program_id(0)

        @pl.when(i == 0)
        def _():
            _load_rows((sl_ref, lo_ref, me_ref), "out", w_ref, sems)
            ext_ref[...] = jnp.zeros_like(ext_ref)
            st_ref[...] = jnp.zeros_like(st_ref)

        u_tile = u_ref[...]
        ext_ref[pl.ds(POOL_HALO, t), :] = u_tile
        st_ref[pl.ds(8, n), :] = ext_ref[pl.ds(8, n), :] + ext_ref[pl.ds(7, n), :]
        st_ref[pl.ds(8, n), 128:] = st_ref[pl.ds(8, n), 128:] + st_ref[pl.ds(6, n), 128:]
        st_ref[pl.ds(8, n), 256:] = st_ref[pl.ds(8, n), 256:] + st_ref[pl.ds(4, n), 256:]
        st_ref[pl.ds(8, n), 384:] = st_ref[pl.ds(8, n), 384:] + st_ref[pl.ds(0, n), 384:]
        ext_ref[pl.ds(0, POOL_HALO), :] = ext_ref[pl.ds(t, POOL_HALO), :]
        pooled = (st_ref[pl.ds(POOL_HALO, t), :] / _pool_counts(i, t) - u_tile).astype(BF16)
        pooled_ref[...] = pooled
        for g in range(4):
            cols = slice(128 * g, 128 * g + 128)
            y = _dot(pooled[:, cols], wp_ref[g], 1, 0) * sc_ref[:, cols]
            mix_ref[:, ATTN_WIDTH + 128 * g:ATTN_WIDTH + 128 * g + 128] = y.astype(BF16)
        lo = lax.broadcasted_iota(jnp.int32, (t, 128), 1) < 64
        for p in range(4):
            a = _from_stacked(o_ref[2 * p].astype(F32), o_ref[2 * p + 1].astype(F32), p // 2, lo)
            mix_ref[:, 128 * p:128 * p + 128] = a.astype(BF16)
        h1 = x_ref[...] + _dot(mix_ref[...], w_ref[...], 1, 0)
        h1_ref[...] = h1
        hn_ref[...] = _rms_fwd(h1, g_ref[...]).astype(BF16)

    row = lambda w: pl.BlockSpec((t, w), lambda i: (i, 0))
    return pl.pallas_call(
        body, name="mix_out", grid=(s_len // t,),
        in_specs=[row(POOL_WIDTH), pl.BlockSpec((N_Q_HEADS, t, 128), lambda i: (0, i, 0)), row(D_MODEL)] + W_SPECS
        + [_full((4, 128, 128)), _full((1, POOL_WIDTH)), _full((1, D_MODEL))],
        out_specs=[row(POOL_WIDTH), row(D_MODEL), row(D_MODEL), row(D_MODEL)],
        out_shape=[jax.ShapeDtypeStruct((s_len, POOL_WIDTH), BF16), jax.ShapeDtypeStruct((s_len, D_MODEL), BF16),
                   jax.ShapeDtypeStruct((s_len, D_MODEL), F32), jax.ShapeDtypeStruct((s_len, D_MODEL), BF16)],
        scratch_shapes=[pltpu.VMEM((D_MODEL, D_MODEL), BF16), pltpu.VMEM((t + POOL_HALO, POOL_WIDTH), F32),
                        pltpu.VMEM((t + POOL_HALO, POOL_WIDTH), F32), pltpu.SemaphoreType.DMA((N_CHIPS,))],
        compiler_params=_params(),
    )(u, ost, x2, *wts, wpool, pool_scale, g_ffn)


def _ffn_ple(hn2, h1, p2, tgt, wts, g_ffn, g_ple):
    s_len = h1.shape[0]
    t = 256
    n_tiles = s_len // t

    def body(hn_ref, h1_ref, p_ref, tgt_ref, sl_ref, lo_ref, me_ref, gf_ref, gp_ref,
             loss_ref, dgate_ref, dup_ref, act_ref, dh2b_ref, hn3_ref, dgl_ref, dwp_ref, dh1_ref, dgf_ref, dgp_ref,
             wg_ref, wu_ref, wd_ref, wl_ref, wp_ref, packed_ref, gate_s, up_s, loss_acc, dwp_acc, sems):
        i = pl.program_id(0)

        @pl.when(i == 0)
        def _():
            w_refs = (sl_ref, lo_ref, me_ref)
            _load_rows(w_refs, "gateT", wg_ref, sems)
            _load_rows(w_refs, "upT", wu_ref, sems)
            _load_rows(w_refs, "down", wd_ref, sems)
            _load_rows(w_refs, "plg", wl_ref, sems)
            _load_rows(w_refs, "plp", packed_ref, sems)
            for j in range(N_CHIPS):
                for q in range(4):
                    wp_ref[pl.ds(64 * q, 64), 256 * j:256 * j + 256] = packed_ref[pl.ds(64 * j, 64), 256 * q:256 * q + 256]
            loss_acc[...] = jnp.zeros_like(loss_acc)
            dgf_ref[...] = jnp.zeros_like(dgf_ref)
            dgp_ref[...] = jnp.zeros_like(dgp_ref)

        hn = hn_ref[...]
        h1v = h1_ref[...]
        h2 = h1v
        for ch in range(N_CHIPS):
            rows = pl.ds(ch * FF_CHUNK, FF_CHUNK)
            gate = _dot(hn, wg_ref[rows, :], 1, 1)
            up = _dot(hn, wu_ref[rows, :], 1, 1)
            gate_s[ch] = gate
            up_s[ch] = up
            act = (gate * _sigmoid(gate) * up).astype(BF16)
            act_ref[ch] = act
            h2 = h2 + _dot(act, wd_ref[rows, :], 1, 0)
        gp = gp_ref[...]
        hn3 = _rms_fwd(h2, gp).astype(BF16)
        hn3_ref[...] = hn3
        gate2 = _sigmoid(_dot(hn3, wl_ref[...], 1, 0))
        p_tile = p_ref[...].astype(BF16)
        pp = _dot(p_tile, wp_ref[...], 1, 0)
        err = h2 + gate2 * pp - tgt_ref[...]
        loss_acc[...] += jnp.sum(err * err, axis=0, keepdims=True)
        dy = err * (1.0 / D_MODEL)
        _accumulate_tn(dwp_acc, p_tile, (dy * gate2).astype(BF16), i == 0)
        dgl = (dy * pp * gate2 * (1.0 - gate2)).astype(BF16)
        dgl_ref[...] = dgl
        dx3, dg3 = _rms_bwd(h2, gp, _dot(dgl, wl_ref[...], 1, 1))
        dh2 = dy + dx3
        dgp_ref[...] += dg3
        dh2b = dh2.astype(BF16)
        dh2b_ref[...] = dh2b
        dhn = jnp.zeros((t, D_MODEL), F32)
        for ch in range(N_CHIPS):
            rows = pl.ds(ch * FF_CHUNK, FF_CHUNK)
            dact = _dot(dh2b, wd_ref[rows, :], 1, 1)
            gate_v = gate_s[ch]
            up_v = up_s[ch]
            sg = _sigmoid(gate_v)
            dup = (dact * (gate_v * sg)).astype(BF16)
            dgate = (dact * up_v * (sg * (1.0 + gate_v * (1.0 - sg)))).astype(BF16)
            dup_ref[ch] = dup
            dgate_ref[ch] = dgate
            dhn = dhn + _dot(dgate, wg_ref[rows, :], 1, 0) + _dot(dup, wu_ref[rows, :], 1, 0)
        dx, dg = _rms_bwd(h1v, gf_ref[...], dhn)
        dh1_ref[...] = dh2 + dx
        dgf_ref[...] += dg

        @pl.when(i == n_tiles - 1)
        def _():
            total = jnp.sum(loss_acc[...], axis=-1, keepdims=True) * (0.5 / D_MODEL)
            loss_ref[...] = jnp.broadcast_to(total, loss_ref.shape)
            dwp_ref[...] = dwp_acc[...].astype(BF16)

    row = lambda w: pl.BlockSpec((t, w), lambda i: (i, 0))
    chunked = pl.BlockSpec((N_CHIPS, t, FF_CHUNK), lambda i: (0, i, 0))
    vec = _full((1, D_MODEL))
    act_shape = jax.ShapeDtypeStruct((N_CHIPS, s_len, FF_CHUNK), BF16)
    tok = lambda dtype: jax.ShapeDtypeStruct((s_len, D_MODEL), dtype)
    return pl.pallas_call(
        body, name="ffn_ple", grid=(n_tiles,),
        in_specs=[row(D_MODEL), row(D_MODEL), row(PLE_DIM), row(D_MODEL)] + W_SPECS + [vec, vec],
        out_specs=[_full((1, 128)), chunked, chunked, chunked] + [row(D_MODEL)] * 3 + [_full((PLE_DIM, D_MODEL)), row(D_MODEL),
                                                                                       vec, vec],
        out_shape=[jax.ShapeDtypeStruct((1, 128), F32), act_shape, act_shape, act_shape, tok(BF16), tok(BF16), tok(BF16),
                   jax.ShapeDtypeStruct((PLE_DIM, D_MODEL), BF16), tok(F32), jax.ShapeDtypeStruct((1, D_MODEL), F32),
                   jax.ShapeDtypeStruct((1, D_MODEL), F32)],
        scratch_shapes=[pltpu.VMEM((D_FF, D_MODEL), BF16)] * 3
        + [pltpu.VMEM((D_MODEL, D_MODEL), BF16), pltpu.VMEM((PLE_DIM, D_MODEL), BF16), pltpu.VMEM((PLE_DIM, D_MODEL), BF16),
           pltpu.VMEM((N_CHIPS, t, FF_CHUNK), F32), pltpu.VMEM((N_CHIPS, t, FF_CHUNK), F32), pltpu.VMEM((1, D_MODEL), F32),
           pltpu.VMEM((PLE_DIM, D_MODEL), F32), pltpu.SemaphoreType.DMA((N_CHIPS,))],
        compiler_params=_params(VMEM_LIMIT_BIG),
    )(hn2, h1, p2, tgt, *wts, g_ffn, g_ple)


def _accumulate_tn(acc_ref, a, b, first):
    @pl.when(first)
    def _():
        acc_ref[...] = _dot(a, b, 0, 0)

    @pl.when(jnp.logical_not(first))
    def _():
        acc_ref[...] += _dot(a, b, 0, 0)


def _flush_chunks(acc_ref, stage_ref, slab_ref, name, sems):
    stage_ref[...] = acc_ref[...].astype(BF16)
    off, rows = SLAB[name]
    copies = [pltpu.make_async_copy(stage_ref.at[pl.ds(j * rows, rows), :], slab_ref.at[j, pl.ds(off, rows), :], sems.at[j])
              for j in range(N_CHIPS)]
    for cp in copies:
        cp.start()
    for cp in copies:
        cp.wait()


def _mix_out_bwd(dh1, wts, pooled, wpool, pool_scale, mix, after):
    s_len = dh1.shape[0]
    t = 512
    n = t + 16
    n_tiles = s_len // t
    early_rows = GATHER_PARTS[0][1]

    def body(dh1_ref, sl_ref, lo_ref, me_ref, pooled_ref, wp_ref, sc_ref, mix_ref, after_ref, dost_ref, du_ref, dyp_ref,
             dsc_ref, slab_ref, w_ref, ext_ref, st_ref, acc_ref, stage_ref, sems):
        del after_ref
        i = pl.program_id(0)

        @pl.when(i == 0)
        def _():
            _load_rows((sl_ref, lo_ref, me_ref), "out", w_ref, sems)
            ext_ref[...] = jnp.zeros_like(ext_ref)
            st_ref[...] = jnp.zeros_like(st_ref)
            dsc_ref[...] = jnp.zeros_like(dsc_ref)

        dh1b = dh1_ref[...].astype(BF16)
        _accumulate_tn(acc_ref, mix_ref[...], dh1b, i == 0)

        @pl.when(i == n_tiles - 1)
        def _():
            _flush_chunks(acc_ref, stage_ref, slab_ref, "out", sems)

        dmix = _dot(dh1b, w_ref[...], 1, 1)
        lo = lax.broadcasted_iota(jnp.int32, (t, 128), 1) < 64
        for p in range(4):
            even, odd = _to_stacked(dmix[:, 128 * p:128 * p + 128], p // 2, lo)
            dost_ref[2 * p] = even.astype(BF16)
            dost_ref[2 * p + 1] = odd.astype(BF16)
        pooled_v = pooled_ref[...]
        counts = _pool_counts(n_tiles - 1 - i, t)
        for g in range(4):
            cols = slice(128 * g, 128 * g + 128)
            dm = dmix[:, ATTN_WIDTH + 128 * g:ATTN_WIDTH + 128 * g + 128]
            ypre = _dot(pooled_v[:, cols], wp_ref[g], 1, 0)
            dsc_ref[:, cols] += jnp.sum(ypre * dm, axis=0, keepdims=True)
            dyp = (dm * sc_ref[:, cols]).astype(BF16)
            dyp_ref[:, cols] = dyp
            dpooled = _dot(dyp, wp_ref[g], 1, 1)
            du_ref[:, cols] = -dpooled
            ext_ref[pl.ds(0, t), cols] = dpooled / counts[:, cols]
        st_ref[pl.ds(0, n), :] = ext_ref[pl.ds(0, n), :] + ext_ref[pl.ds(1, n), :]
        st_ref[pl.ds(0, n), 128:] = st_ref[pl.ds(0, n), 128:] + st_ref[pl.ds(2, n), 128:]
        st_ref[pl.ds(0, n), 256:] = st_ref[pl.ds(0, n), 256:] + st_ref[pl.ds(4, n), 256:]
        st_ref[pl.ds(0, n), 384:] = st_ref[pl.ds(0, n), 384:] + st_ref[pl.ds(8, n), 384:]
        ext_ref[pl.ds(t, POOL_HALO), :] = ext_ref[pl.ds(0, POOL_HALO), :]
        du_ref[...] += st_ref[pl.ds(0, t), :]

    rev = lambda w: pl.BlockSpec((t, w), lambda i: (n_tiles - 1 - i, 0))
    return pl.pallas_call(
        body, name="mix_out_bwd", grid=(n_tiles,),
        in_specs=[rev(D_MODEL)] + W_SPECS + [rev(POOL_WIDTH), _full((4, 128, 128)), _full((1, POOL_WIDTH)), rev(D_MODEL), ANY],
        out_specs=[pl.BlockSpec((N_Q_HEADS, t, 128), lambda i: (0, n_tiles - 1 - i, 0)), rev(POOL_WIDTH), rev(POOL_WIDTH),
                   _full((1, POOL_WIDTH)), ANY],
        out_shape=[jax.ShapeDtypeStruct((N_Q_HEADS, s_len, 128), BF16), jax.ShapeDtypeStruct((s_len, POOL_WIDTH), F32),
                   jax.ShapeDtypeStruct((s_len, POOL_WIDTH), BF16), jax.ShapeDtypeStruct((1, POOL_WIDTH), F32),
                   jax.ShapeDtypeStruct((N_CHIPS, early_rows, D_MODEL), BF16)],
        scratch_shapes=[pltpu.VMEM((D_MODEL, D_MODEL), BF16), pltpu.VMEM((t + POOL_HALO, POOL_WIDTH), F32),
                        pltpu.VMEM((t + POOL_HALO, POOL_WIDTH), F32), pltpu.VMEM((D_MODEL, D_MODEL), F32),
                        pltpu.VMEM((D_MODEL, D_MODEL), BF16), pltpu.SemaphoreType.DMA((N_CHIPS,))],
        compiler_params=_params(),
    )(dh1, *wts, pooled, wpool, pool_scale, mix, after)


def _attn_bwd(qst, kn, vb, dost, bias_st, sinks, after):
    s_len = kn.shape[0]

    def body(q_ref, kp_ref, kc_ref, vp_ref, vc_ref, do_ref, bias_ref, sink_ref, after_ref, dq_ref, dk_ref, dv_ref, dbias_ref,
             dsink_ref, s_ref, dp_ref, p_ref, dl_ref):
        del after_ref
        i = pl.program_id(0)

        @pl.when(i == 0)
        def _():
            dk_ref[...] = jnp.zeros_like(dk_ref)
            dv_ref[...] = jnp.zeros_like(dv_ref)
            dbias_ref[...] = jnp.zeros_like(dbias_ref)
            dsink_ref[...] = jnp.zeros_like(dsink_ref)

        q = q_ref[...].reshape(N_Q_HEADS * BLOCK, 128)
        do = do_ref[...].reshape(N_Q_HEADS * BLOCK, 128)
        k2 = jnp.concatenate([kp_ref[...], kc_ref[...]], axis=0)
        s_ref[...] = _dot(q, k2, 1, 1)
        dp_ref[...] = _dot(do, jnp.concatenate([vp_ref[...], vc_ref[...]], axis=0), 1, 1)

        def head(h, carry):
            rows, probs, p_sink = _head_softmax(s_ref, bias_ref, sink_ref, h)
            dp = dp_ref[rows, :]
            dsum = jnp.sum(probs * dp, axis=-1, keepdims=True)
            dlog = probs * (dp - dsum)
            dsink_ref[rows, :] -= p_sink * dsum
            dbias_ref[rows, :] += dlog
            p_ref[rows, :] = probs.astype(BF16)
            dl_ref[rows, :] = (dlog * (HEAD_DIM ** -0.5)).astype(BF16)
            return carry

        lax.fori_loop(0, N_Q_HEADS, head, 0, unroll=True)
        dlog_s = dl_ref[...]
        dq_ref[...] = jnp.where(_head_lane_mask(), _dot(dlog_s, k2, 1, 0), 0.0).reshape(N_Q_HEADS, BLOCK, 128)
        dk2 = _dot(dlog_s, q, 0, 0)
        dv2 = _dot(p_ref[...], do, 0, 0)
        prev_rows = pl.ds(pl.multiple_of(jnp.maximum(i - 1, 0) * BLOCK, BLOCK), BLOCK)
        cur_rows = pl.ds(pl.multiple_of(i * BLOCK, BLOCK), BLOCK)
        dk_ref[prev_rows, :] += dk2[:BLOCK]
        dk_ref[cur_rows, :] += dk2[BLOCK:]
        dv_ref[prev_rows, :] += dv2[:BLOCK]
        dv_ref[cur_rows, :] += dv2[BLOCK:]

    stacked, kv, consts = _attn_specs()
    band = (N_Q_HEADS * BLOCK, 2 * BLOCK)
    return pl.pallas_call(
        body, name="attn_bwd", grid=(s_len // BLOCK,),
        in_specs=[stacked] + kv + kv + [stacked] + consts + [ANY],
        out_specs=[stacked, _full((s_len, 128)), _full((s_len, 128)), _full(band), _full((N_Q_HEADS * BLOCK, 1))],
        out_shape=[jax.ShapeDtypeStruct((N_Q_HEADS, s_len, 128), F32), jax.ShapeDtypeStruct((s_len, 128), F32),
                   jax.ShapeDtypeStruct((s_len, 128), F32), jax.ShapeDtypeStruct(band, F32),
                   jax.ShapeDtypeStruct((N_Q_HEADS * BLOCK, 1), F32)],
        scratch_shapes=[pltpu.VMEM(band, F32), pltpu.VMEM(band, F32), pltpu.VMEM(band, BF16), pltpu.VMEM(band, BF16)],
        compiler_params=_params(),
    )(qst, kn, kn, vb, vb, dost, bias_st, sinks, after)


def _small_pack(dg_attn, dg_ffn, dg_ple, dscale, dgq, dgk, dbias, dsink_rows, bucket, loss_v, dwpool):
    def body(ga_ref, gf_ref, gp_ref, sc_ref, gq_ref, gk_ref, db_ref, ds_ref, bucket_ref, loss_ref, wp_ref, out_ref):
        out_ref[pl.ds(0, SMALL["w_pool"]), :] = jnp.zeros((SMALL["w_pool"], 128), F32)
        for name, ref, n in (("g_attn", ga_ref, 8), ("g_ffn", gf_ref, 8), ("g_ple", gp_ref, 8), ("pool_scale", sc_ref, 4)):
            for k in range(n):
                out_ref[pl.ds(SMALL[name] + k, 1), :] = ref[:, 128 * k:128 * k + 128]
        for name, ref in (("g_q", gq_ref), ("g_k", gk_ref)):
            both = ref[...]
            out_ref[pl.ds(SMALL[name], 1), :] = both + pltpu.roll(both, 64, axis=1)
        out_ref[pl.ds(SMALL["loss"], 1), :] = loss_ref[...]
        bk = bucket_ref[...]
        rows = lax.broadcasted_iota(jnp.int32, (N_BUCKETS, 128), 0)
        lanes = lax.broadcasted_iota(jnp.int32, (N_BUCKETS, 128), 1)
        lane1 = lax.broadcasted_iota(jnp.int32, (1, 128), 1)
        rb = jnp.zeros((N_BUCKETS, 128), F32)
        sk = jnp.zeros((1, 128), F32)
        for h in range(N_Q_HEADS):
            band = db_ref[pl.ds(h * BLOCK, BLOCK), :]
            for b in range(N_BUCKETS):
                rb = jnp.where((rows == b) & (lanes == h), jnp.sum(jnp.where(bk == b, band, 0.0)), rb)
            sk = jnp.where(lane1 == h, jnp.sum(ds_ref[pl.ds(h * BLOCK, BLOCK), :]), sk)
        out_ref[pl.ds(SMALL["rel_bias"], N_BUCKETS), :] = rb
        out_ref[pl.ds(SMALL["sinks"], 1), :] = sk
        out_ref[pl.ds(SMALL["w_pool"], 512), :] = wp_ref[...].reshape(512, 128)

    return pl.pallas_call(
        body, name="small_pack", in_specs=[VMEM_WHOLE] * 11, out_specs=VMEM_WHOLE,
        out_shape=jax.ShapeDtypeStruct((SMALL_ROWS, 128), F32),
    )(dg_attn, dg_ffn, dg_ple, dscale, dgq, dgk, dbias, dsink_rows, bucket, loss_v, dwpool)


def _attn_in_bwd(dqst, zqk, dk, dv, du, x2, dh1, hn1, slab, wts, g_attn, gq, gk):
    s_len = x2.shape[0]
    t = 512
    n_tiles = s_len // t

    def body(dq_ref, zqk_ref, dk_ref, dv_ref, du_ref, x_ref, dh1_ref, hn_ref, slab_in_ref, sl_ref, lo_ref, me_ref, g_ref,
             gq_ref, gk_ref, dx_ref, dg_ref, dgq_ref, dgk_ref, slab_ref, w_ref, dz_ref, acc_ref, stage_ref, sems):
        del slab_in_ref
        i = pl.program_id(0)

        @pl.when(i == 0)
        def _():
            _load_rows((sl_ref, lo_ref, me_ref), "inT", w_ref, sems)
            dg_ref[...] = jnp.zeros_like(dg_ref)
            dgq_ref[...] = jnp.zeros_like(dgq_ref)
            dgk_ref[...] = jnp.zeros_like(dgk_ref)

        lo = lax.broadcasted_iota(jnp.int32, (t, 128), 1) < 64
        for p in range(4):
            dqn = _from_stacked(dq_ref[2 * p], dq_ref[2 * p + 1], p // 2, lo)
            dq_raw, dgq = _pair_norm_bwd(zqk_ref[:, 128 * p:128 * p + 128], gq_ref[...], dqn)
            dz_ref[:, 128 * p:128 * p + 128] = dq_raw.astype(BF16)
            dgq_ref[...] += dgq
        dk_raw, dgk = _pair_norm_bwd(zqk_ref[:, 512:640], gk_ref[...], dk_ref[...])
        dgk_ref[...] += dgk
        dz_ref[:, 512:640] = dk_raw.astype(BF16)
        dz_ref[:, 640:768] = dv_ref[...].astype(BF16)
        dz_ref[:, 768:] = du_ref[...].astype(BF16)
        dz = dz_ref[...]
        _accumulate_tn(acc_ref, dz, hn_ref[...], i == 0)
        dx, dg = _rms_bwd(x_ref[...], g_ref[...], _dot(dz, w_ref[...], 1, 0))
        dx_ref[...] = dh1_ref[...] + dx
        dg_ref[...] += dg

        @pl.when(i == n_tiles - 1)
        def _():
            _flush_chunks(acc_ref, stage_ref, slab_ref, "inT", sems)

    row = lambda w: pl.BlockSpec((t, w), lambda i: (i, 0))
    return pl.pallas_call(
        body, name="attn_in_bwd", grid=(n_tiles,),
        in_specs=[pl.BlockSpec((N_Q_HEADS, t, 128), lambda i: (0, i, 0)), row(640), row(128), row(128), row(POOL_WIDTH),
                  row(D_MODEL), row(D_MODEL), row(D_MODEL), ANY] + W_SPECS + [_full((1, D_MODEL)), _full((1, 128)),
                                                                              _full((1, 128))],
        out_specs=[row(D_MODEL), _full((1, D_MODEL)), _full((1, 128)), _full((1, 128)), ANY],
        out_shape=[jax.ShapeDtypeStruct((s_len, D_MODEL), F32), jax.ShapeDtypeStruct((1, D_MODEL), F32),
                   jax.ShapeDtypeStruct((1, 128), F32), jax.ShapeDtypeStruct((1, 128), F32),
                   jax.ShapeDtypeStruct(slab.shape, BF16)],
        input_output_aliases={8: 4},
        scratch_shapes=[pltpu.VMEM((IN_WIDTH, D_MODEL), BF16), pltpu.VMEM((t, IN_WIDTH), BF16),
                        pltpu.VMEM((IN_WIDTH, D_MODEL), F32), pltpu.VMEM((IN_WIDTH, D_MODEL), BF16),
                        pltpu.SemaphoreType.DMA((N_CHIPS,))],
        compiler_params=_params(),
    )(dqst, zqk, dk, dv, du, x2, dh1, hn1, slab, *wts, g_attn, gq, gk)


def _dw(lefts, b, name, slab, slab_rows, row_offs):
    tk = 2048
    a0, n_a = lefts[0], len(lefts)
    assert b.shape[1] == D_MODEL
    if a0.ndim == 3:
        s_len, tm = a0.shape[1:]
        m = N_CHIPS * tm
        a_spec = pl.BlockSpec((None, tk, tm), lambda i, k: (i, k, 0))
    else:
        s_len, tm = a0.shape
        m = tm
        a_spec = pl.BlockSpec((tk, tm), lambda i, k: (k, i))
    n_steps, n_tiles = s_len // tk, m // tm
    chunk = m // N_CHIPS
    per_tile = tm // chunk

    def body(*refs):
        a_refs, b_ref = refs[:n_a], refs[n_a]
        o_ref, acc_ref, stage_ref, sems = refs[-4:]
        i, k = pl.program_id(0), pl.program_id(1)
        b_tile = b_ref[...].astype(BF16)
        for w, a_ref in enumerate(a_refs):
            _accumulate_tn(acc_ref.at[w], a_ref[...].astype(BF16), b_tile, k == 0)

        def out_copies(tile, slot):
            return [pltpu.make_async_copy(stage_ref.at[slot, w, pl.ds(jj * chunk, chunk), :],
                                          o_ref.at[tile * per_tile + jj, pl.ds(row_offs[w], chunk), :], sems.at[slot, w, jj])
                    for w in range(n_a) for jj in range(per_tile)]

        @pl.when(k == n_steps - 1)
        def _():
            slot = i % 2

            @pl.when(i >= 2)
            def _():
                for cp in out_copies(i - 2, slot):
                    cp.wait()

            stage_ref[slot] = acc_ref[...].astype(BF16)
            for cp in out_copies(i, slot):
                cp.start()

            @pl.when(i == n_tiles - 1)
            def _():
                for cp in out_copies(i, slot):
                    cp.wait()
                if n_tiles > 1:
                    for cp in out_copies(i - 1, 1 - slot):
                        cp.wait()

    in_specs = [a_spec] * n_a + [pl.BlockSpec((tk, D_MODEL), lambda i, k: (k, 0))]
    operands, aliases = [*lefts, b], {}
    if slab is not None:
        in_specs.append(ANY)
        operands.append(slab)
        aliases = {n_a + 1: 0}
    return pl.pallas_call(
        body, name=name, grid=(n_tiles, n_steps), in_specs=in_specs, out_specs=ANY,
        out_shape=jax.ShapeDtypeStruct((N_CHIPS, slab_rows, D_MODEL), BF16), input_output_aliases=aliases,
        scratch_shapes=[pltpu.VMEM((n_a, tm, D_MODEL), F32), pltpu.VMEM((2, n_a, tm, D_MODEL), BF16),
                        pltpu.SemaphoreType.DMA((2, n_a, per_tile))],
        compiler_params=_params(n_axes=2),
    )(*operands)


def _dw_pool(pooled, dyp):
    s_len = pooled.shape[0]
    tk = 512

    def body(a_ref, b_ref, o_ref):
        @pl.when(pl.program_id(0) == 0)
        def _():
            o_ref[...] = jnp.zeros_like(o_ref)

        for g in range(4):
            cols = slice(128 * g, 128 * g + 128)
            o_ref[g] += _dot(a_ref[:, cols], b_ref[:, cols], 0, 0)

    blk = pl.BlockSpec((tk, POOL_WIDTH), lambda k: (k, 0))
    return pl.pallas_call(
        body, name="dw_pool", grid=(s_len // tk,), in_specs=[blk, blk], out_specs=_full((4, 128, 128)),
        out_shape=jax.ShapeDtypeStruct((4, 128, 128), F32), compiler_params=_params(),
    )(pooled, dyp)


def _position():
    x, y, c = lax.axis_index("x"), lax.axis_index("y"), lax.axis_index("c")
    other_chips = [(1 - x, y), (x, 1 - y), (1 - x, 1 - y)]
    return x, y, c, other_chips


def _ag_weights(local_slab, row0, n_rows, name, collective_id):
    half = n_rows // 2
    quarter = half // 2
    assert quarter % 16 == 0

    def body(l_ref, g_ref, send, recv):
        x, y, c, chips = _position()
        me, (via_x, via_y, diagonal) = 2 * x + y, [2 * chip[0] + chip[1] for chip in chips]
        here, sibling, x_nbr, y_nbr = (x, y, c), (x, y, 1 - c), (1 - x, y, c), (x, 1 - y, c)
        peers = [sibling, x_nbr, y_nbr]
        barrier = pltpu.get_barrier_semaphore()
        for peer in peers:
            pl.semaphore_signal(barrier, inc=1, device_id=peer, device_id_type=MESH)
        pl.semaphore_wait(barrier, len(peers))

        def rows(core, part):
            start, size = (core * half, half) if part is None else (core * half + part * quarter, quarter)
            return pl.ds(pl.multiple_of(start, 16), size)

        def copy(k, chip_idx, where, to, src=None):
            dst = g_ref.at[chip_idx, where, :]
            return pltpu.make_async_remote_copy(src_ref=dst if src is None else src, dst_ref=dst, send_sem=send.at[k],
                                                recv_sem=recv.at[k], device_id=to, device_id_type=MESH)

        own_rows = l_ref.at[pl.ds(pl.multiple_of(row0 + c * half, 16), half), :]
        started = [copy(0, me, rows(c, None), x_nbr, src=own_rows), copy(1, me, rows(c, None), y_nbr, src=own_rows)]
        for cp in started:
            cp.start()
        after_arrival = [
            (copy(0, via_x, rows(c, None), here), [copy(4, via_x, rows(c, None), sibling), copy(3, via_x, rows(c, 1), y_nbr)]),
            (copy(1, via_y, rows(c, None), here), [copy(5, via_y, rows(c, None), sibling), copy(2, via_y, rows(c, 0), x_nbr)]),
            (copy(2, diagonal, rows(c, 0), here), [copy(6, diagonal, rows(c, 0), sibling)]),
            (copy(3, diagonal, rows(c, 1), here), [copy(7, diagonal, rows(c, 1), sibling)]),
        ]
        for arrival, onward in after_arrival:
            arrival.wait_recv()
            for cp in onward:
                cp.start()
            started += onward
        for cp in (copy(4, via_x, rows(1 - c, None), here), copy(5, via_y, rows(1 - c, None), here),
                   copy(6, diagonal, rows(1 - c, 0), here), copy(7, diagonal, rows(1 - c, 1), here)):
            cp.wait_recv()
        for cp in started:
            cp.wait_send()

    return pl.kernel(
        body, out_type=jax.ShapeDtypeStruct((N_CHIPS, n_rows, D_MODEL), BF16),
        mesh=plsc.ScalarSubcoreMesh(axis_name="sequencer", num_cores=1), name=name,
        scratch_types=[pltpu.SemaphoreType.DMA((8,)), pltpu.SemaphoreType.DMA((8,))],
        compiler_params=pltpu.CompilerParams(collective_id=collective_id),
    )(local_slab)


def _comm_call(body, peers_of, out_shape, n_sems, operand, name, collective_id):
    sems = [pltpu.SemaphoreType.DMA((n_sems,)), pltpu.SemaphoreType.DMA((n_sems,))]
    if collective_id is None:
        return pl.pallas_call(body, name=name, in_specs=[ANY], out_specs=ANY, out_shape=out_shape, scratch_shapes=sems)(operand)

    def with_handshake(in_ref, out_ref, send, recv):
        x, y, c, _ = _position()
        peers = peers_of(x, y, c)
        barrier = pltpu.get_barrier_semaphore()
        for peer in peers:
            pl.semaphore_signal(barrier, inc=1, device_id=peer, device_id_type=MESH)
        pl.semaphore_wait(barrier, len(peers))
        body(in_ref, out_ref, send, recv)

    return pl.kernel(with_handshake, out_type=out_shape, mesh=plsc.ScalarSubcoreMesh(axis_name="sequencer", num_cores=1),
                     name=name, scratch_types=sems, compiler_params=pltpu.CompilerParams(collective_id=collective_id))(operand)


def _rs_swap_halves(partial, name, collective_id=None):
    half = partial.shape[1] // 2

    def body(p_ref, r_ref, send, recv):
        x, y, c, _ = _position()
        theirs = pl.ds(pl.multiple_of((1 - c) * half, 16), half)
        cp = pltpu.make_async_remote_copy(src_ref=p_ref.at[:, theirs, :], dst_ref=r_ref, send_sem=send.at[0],
                                          recv_sem=recv.at[0], device_id=(x, y, 1 - c), device_id_type=MESH)
        cp.start()
        cp.wait()

    return _comm_call(body, lambda x, y, c: [(x, y, 1 - c)], jax.ShapeDtypeStruct((N_CHIPS, half, D_MODEL), BF16), 1,
                      partial, name, collective_id)


def _rs_add_halves(partial, other, core, name, after):
    half = other.shape[1]
    t = half // 2
    steps = half // t

    def body(core_ref, a_ref, b_ref, after_ref, o_ref):
        del after_ref
        o_ref[...] = (a_ref[...].astype(F32) + b_ref[...].astype(F32)).astype(BF16)

    return pl.pallas_call(
        body, name=name,
        grid_spec=pltpu.PrefetchScalarGridSpec(
            num_scalar_prefetch=1, grid=(N_CHIPS, steps),
            in_specs=[pl.BlockSpec((1, t, D_MODEL), lambda j, i, core_ref: (j, core_ref[0] * steps + i, 0)),
                      pl.BlockSpec((1, t, D_MODEL), lambda j, i, core_ref: (j, i, 0)), ANY],
            out_specs=pl.BlockSpec((1, t, D_MODEL), lambda j, i, core_ref: (j, i, 0))),
        out_shape=jax.ShapeDtypeStruct((N_CHIPS, half, D_MODEL), BF16),
        compiler_params=_params(n_axes=2),
    )(core, partial, other, after)


def _rs_exchange_chips(pre, name, collective_id=None):
    def body(s_ref, r_ref, send, recv):
        x, y, c, chips = _position()

        def copy(k, chunk, to):
            return pltpu.make_async_remote_copy(src_ref=s_ref.at[chunk], dst_ref=r_ref.at[k], send_sem=send.at[k],
                                                recv_sem=recv.at[k], device_id=to, device_id_type=MESH)

        sends = [copy(k, 2 * chip[0] + chip[1], (*chip, c)) for k, chip in enumerate(chips)]
        for cp in sends:
            cp.start()
        for cp in sends:
            cp.wait()

    return _comm_call(body, lambda x, y, c: [(1 - x, y, c), (x, 1 - y, c), (1 - x, 1 - y, c)],
                      jax.ShapeDtypeStruct((3, pre.shape[1], D_MODEL), BF16), 3, pre, name, collective_id)


def _rs_sum_chips(pre, received, place, name, after):
    half = pre.shape[1]
    t = half // 2 if half > 512 else half
    steps = half // t

    def body(place_ref, own_ref, r_ref, after_ref, o_ref):
        del after_ref
        acc = own_ref[0].astype(F32)
        for k in range(3):
            acc = acc + r_ref[k].astype(F32)
        o_ref[...] = acc

    return pl.pallas_call(
        body, name=name,
        grid_spec=pltpu.PrefetchScalarGridSpec(
            num_scalar_prefetch=1, grid=(steps,),
            in_specs=[pl.BlockSpec((1, t, D_MODEL), lambda i, place_ref: (place_ref[0], i, 0)),
                      pl.BlockSpec((3, t, D_MODEL), lambda i, place_ref: (0, i, 0)), ANY],
            out_specs=pl.BlockSpec((t, D_MODEL), lambda i, place_ref: (place_ref[1] * steps + i, 0))),
        out_shape=jax.ShapeDtypeStruct((2 * half, D_MODEL), F32),
        compiler_params=_params(),
    )(place, pre, received, after)


def _half_swap(g_ref, core, to, send, recv, k):
    half = g_ref.shape[0] // 2
    rows = g_ref.at[pl.ds(pl.multiple_of(core * half, 8), half), :]
    return pltpu.make_async_remote_copy(src_ref=rows, dst_ref=rows, send_sem=send.at[k], recv_sem=recv.at[k],
                                        device_id=to, device_id_type=MESH)


def _rs_finish_rows(grads, name, after):
    def body(f_ref, after_ref, g_ref, send, recv):
        del f_ref, after_ref
        x, y, c, _ = _position()
        mine = _half_swap(g_ref, c, (x, y, 1 - c), send, recv, 0)
        mine.start()
        _half_swap(g_ref, 1 - c, (x, y, c), send, recv, 0).wait_recv()
        mine.wait_send()

    return pl.pallas_call(
        body, name=name, in_specs=[ANY, ANY], out_specs=ANY, input_output_aliases={0: 0},
        out_shape=jax.ShapeDtypeStruct(grads.shape, F32),
        scratch_shapes=[pltpu.SemaphoreType.DMA((1,)), pltpu.SemaphoreType.DMA((1,))],
    )(grads, after)


def _small_gather(small, collective_id):
    def body(s_ref, t_ref, send, recv):
        x, y, c, chips = _position()
        sibling = (x, y, 1 - c)

        def slot(px, py, pc):
            return t_ref.at[4 * px + 2 * py + pc]

        def copy(k, block, to, src=None):
            return pltpu.make_async_remote_copy(src_ref=slot(*block) if src is None else src, dst_ref=slot(*block),
                                                send_sem=send.at[k], recv_sem=recv.at[k], device_id=to, device_id_type=MESH)

        own = pltpu.make_async_copy(s_ref, slot(x, y, c), send.at[7])
        own.start()
        first = [copy(0, (x, y, c), sibling, src=s_ref)]
        first += [copy(1 + k, (x, y, c), (*chip, c), src=s_ref) for k, chip in enumerate(chips)]
        for cp in first:
            cp.start()
        passed = []
        for k, chip in enumerate(chips):
            copy(1 + k, (*chip, c), (x, y, c)).wait_recv()
            fwd = copy(4 + k, (*chip, c), sibling)
            fwd.start()
            passed.append(fwd)
        copy(0, sibling, (x, y, c)).wait_recv()
        for k, chip in enumerate(chips):
            copy(4 + k, (*chip, 1 - c), (x, y, c)).wait_recv()
        for cp in first + passed:
            cp.wait_send()
        own.wait()

    peers_of = lambda x, y, c: [(x, y, 1 - c), (1 - x, y, c), (x, 1 - y, c), (1 - x, 1 - y, c)]
    return _comm_call(body, peers_of, jax.ShapeDtypeStruct((N_DEV, SMALL_ROWS, 128), F32), 8, small, "small_gather",
                      collective_id)


def _adam_update(w, g, m, v):
    m_new = ADAM_B1 * m + (1.0 - ADAM_B1) * g
    v_new = ADAM_B2 * v + (1.0 - ADAM_B2) * (g * g)
    m_hat = m_new / (1.0 - ADAM_B1 ** ADAM_STEP)
    v_hat = v_new / (1.0 - ADAM_B2 ** ADAM_STEP)
    return -ADAM_LR * (m_hat / (jnp.sqrt(v_hat) + ADAM_EPS) + ADAM_WD * w), m_new, v_new


def _adamw(w, g_rows, row_off, m, v, name):
    rows, cols = w.shape
    t = rows if rows <= 320 else (rows // 2 if rows % 256 else 256)

    def body(w_ref, g_ref, m_ref, v_ref, go_ref, d_ref, nm_ref, nv_ref):
        g = g_ref[...]
        go_ref[...] = g
        d_ref[...], nm_ref[...], nv_ref[...] = _adam_update(w_ref[...], g, m_ref[...], v_ref[...])

    blk = pl.BlockSpec((t, cols), lambda i: (i, 0))
    assert row_off % 8 == 0 and t % 8 == 0
    g_blk = pl.BlockSpec((pl.Element(t), pl.Element(cols)), lambda i: (pl.multiple_of(row_off + i * t, 8), 0))
    shape = jax.ShapeDtypeStruct((rows, cols), F32)
    return pl.pallas_call(
        body, name=name, grid=(rows // t,), in_specs=[blk, g_blk, blk, blk], out_specs=[blk] * 4, out_shape=[shape] * 4,
        compiler_params=_params(),
    )(w, g_rows, m, v)


SMALL_PARAMS = [("g_attn", (1, D_MODEL), 8), ("g_q", (1, HEAD_DIM), None), ("g_k", (1, HEAD_DIM), None),
                ("sinks", (1, N_Q_HEADS), None), ("rel_bias", (N_BUCKETS, N_Q_HEADS), None), ("w_pool", (512, 128), None),
                ("pool_scale", (1, POOL_WIDTH), 4), ("g_ffn", (1, D_MODEL), 8), ("g_ple", (1, D_MODEL), 8)]


def _adamw_small(tables, wmv):
    n_par = len(SMALL_PARAMS)

    def body(*refs):
        t_ref = refs[0]
        ins = refs[1:1 + 3 * n_par]
        loss_ref = refs[1 + 3 * n_par]
        outs = refs[2 + 3 * n_par:-1]
        tot_ref = refs[-1]
        total = t_ref[0]
        for d in range(1, N_DEV):
            total = total + t_ref[d]
        tot_ref[...] = total
        loss_ref[...] = tot_ref[pl.ds(SMALL["loss"], 1), 0:1]
        for i, (name, shape, split) in enumerate(SMALL_PARAMS):
            g_ref, d_ref, nm_ref, nv_ref = outs[4 * i:4 * i + 4]
            row = SMALL[name]
            if split:
                for k in range(split):
                    g_ref[:, 128 * k:128 * k + 128] = tot_ref[pl.ds(row + k, 1), :]
            else:
                g_ref[...] = tot_ref[pl.ds(row, shape[0]), 0:shape[1]]
            w_ref, m_ref, v_ref = ins[3 * i:3 * i + 3]
            d_ref[...], nm_ref[...], nv_ref[...] = _adam_update(w_ref[...], g_ref[...], m_ref[...], v_ref[...])

    shapes = [jax.ShapeDtypeStruct((1, 1), F32)]
    for _, shape, _ in SMALL_PARAMS:
        shapes += [jax.ShapeDtypeStruct(shape, F32)] * 4
    flat = [a for triple in wmv for a in triple]
    res = pl.pallas_call(
        body, name="adamw_small", in_specs=[VMEM_WHOLE] * (1 + 3 * n_par), out_specs=[VMEM_WHOLE] * len(shapes),
        out_shape=shapes, scratch_shapes=[pltpu.VMEM((SMALL_ROWS, 128), F32)],
    )(tables, *flat)
    return res[0], [res[1 + 4 * i:5 + 4 * i] for i in range(n_par)]


def _pack_ple_proj(shard):
    return shard.reshape(4, 64, 256).transpose(1, 0, 2).reshape(64, D_MODEL)


class _Reduction:
    def __init__(self, tag, place, ids=(None, None)):
        self.tag, self.place, self.ids = tag, place, ids

    def start(self, partial):
        self.partial = partial
        self.other = _rs_swap_halves(partial, "rs_swap_" + self.tag, self.ids[0])
        return partial

    def middle(self, after):
        self.pre = _rs_add_halves(self.partial, self.other, self.place[1:], "rs_add_" + self.tag, after)
        self.received = _rs_exchange_chips(self.pre, "rs_exchange_" + self.tag, self.ids[1])
        return self.pre

    def finish(self, after):
        return _rs_sum_chips(self.pre, self.received, self.place, "rs_sum_" + self.tag, after)


def _local_grads(x2, p2, tgt, wts, g_attn_norm, g_q, g_k, attn_sinks, rel_bias, w_pool, pool_scale, g_ffn_norm, g_ple_norm,
                 reduce_a):
    w_early, w_late = wts
    w_in = w_out = w_early
    bucket = jnp.asarray(_bucket_table())
    gq = jnp.tile(g_q, (1, 2))
    gk = jnp.tile(g_k, (1, 2))
    wpool = w_pool[0].astype(BF16)
    sinks = attn_sinks[0]
    bias_st = _bias_build(rel_bias.T, bucket)

    hn1, zqk, u, kn, vb, qst = _attn_in(x2, g_attn_norm, gq, gk, w_in)
    ost = _attn_fwd(qst, kn, vb, bias_st, sinks)
    pooled, mix, h1, hn2 = _mix_out(u, ost, x2, w_out, wpool, pool_scale, g_ffn_norm)
    loss_v, dgate, dup, act, dh2, hn3, dgl, dw_plp, dh1, dg_ffn, dg_ple = _ffn_ple(hn2, h1, p2, tgt, w_late, g_ffn_norm,
                                                                                      g_ple_norm)

    late0, late_rows = GATHER_PARTS[1][0], SLAB_ROWS - GATHER_PARTS[1][0]
    partial_a = None
    for names, lefts, right in ((("gateT", "upT"), [dgate, dup], hn2), (("down",), [act], dh2), (("plg",), [hn3], dgl)):
        partial_a = _dw(lefts, right, "dw_" + names[0], partial_a, late_rows, [SLAB[name][0] - late0 for name in names])
    dw_plp = dw_plp.reshape(4, 64, N_CHIPS, 256).transpose(2, 1, 0, 3).reshape(N_CHIPS, 64, D_MODEL)
    partial_a = reduce_a.start(lax.dynamic_update_slice(partial_a, dw_plp, (0, SLAB["plp"][0] - late0, 0)))
    dost, du, dyp, dscale, partial_b = _mix_out_bwd(dh1, w_out, pooled, wpool, pool_scale, mix, partial_a)
    pre_a = reduce_a.middle(du)
    dqst, dk, dv, dbias, dsink_rows = _attn_bwd(qst, kn, vb, dost, bias_st, sinks, pre_a)
    dx, dg_attn, dgq, dgk, partial_b = _attn_in_bwd(dqst, zqk, dk, dv, du, x2, dh1, hn1, partial_b, w_in, g_attn_norm, gq, gk)

    small = _small_pack(dg_attn, dg_ffn, dg_ple, dscale, dgq, dgk, dbias, dsink_rows, bucket, loss_v, _dw_pool(pooled, dyp))
    return dx, partial_b, small


def kernel(x, p, w_in, w_out, g_attn_norm, g_q, g_k, attn_sinks, rel_bias, w_pool, pool_scale, g_ffn_norm, w_gate, w_up, w_down, g_ple_norm, w_ple_gate, w_ple_proj, loss_target, m_w_in, m_w_out, m_g_attn_norm, m_g_q, m_g_k, m_attn_sinks, m_rel_bias, m_w_pool, m_pool_scale, m_g_ffn_norm, m_w_gate, m_w_up, m_w_down, m_g_ple_norm, m_w_ple_gate, m_w_ple_proj, v_w_in, v_w_out, v_g_attn_norm, v_g_q, v_g_k, v_attn_sinks, v_rel_bias, v_w_pool, v_pool_scale, v_g_ffn_norm, v_w_gate, v_w_up, v_w_down, v_g_ple_norm, v_w_ple_gate, v_w_ple_proj):
    core = lax.axis_index("c").astype(jnp.int32).reshape(1)
    me = (2 * lax.axis_index("x") + lax.axis_index("y")).astype(jnp.int32).reshape(1)

    local_parts = [jnp.concatenate(pieces, axis=0).astype(BF16) for pieces in (
        [w_in[0].T, w_out[0]], [w_gate[0].T, w_up[0].T, w_down[0], w_ple_gate[0], _pack_ple_proj(w_ple_proj[0])])]
    wts = [(_ag_weights(local, 0, local.shape[0], name, collective_id), local, me)
           for local, name, collective_id in zip(local_parts, ("ag_early", "ag_late"), (1, 2))]

    place = jnp.concatenate([me, core])
    reduce_a = _Reduction("a", place, ids=(3, 4))
    dx, partial_b, small = _local_grads(x[0], p[0, 0], loss_target[0], wts, g_attn_norm, g_q, g_k, attn_sinks, rel_bias,
                                        w_pool, pool_scale, g_ffn_norm, g_ple_norm, reduce_a)
    reduce_b = _Reduction("b", place, ids=(6, 7))
    reduce_b.start(partial_b)
    small_all = _small_gather(small, 8)
    summed_a = reduce_a.finish(small)
    pre_b = reduce_b.middle(summed_a)
    grads_a = _rs_finish_rows(summed_a, "rs_finish_a", pre_b)

    late0 = GATHER_PARTS[1][0]

    def rows(name):
        return grads_a, SLAB[name][0] - late0

    plp_rows = grads_a[SLAB["plp"][0] - late0:]
    big = {
        "w_gate": (w_gate, m_w_gate, v_w_gate, rows("gateT"), True),
        "w_up": (w_up, m_w_up, v_w_up, rows("upT"), True),
        "w_down": (w_down, m_w_down, v_w_down, rows("down"), False),
        "w_ple_gate": (w_ple_gate, m_w_ple_gate, v_w_ple_gate, rows("plg"), False),
        "w_ple_proj": (w_ple_proj, m_w_ple_proj, v_w_ple_proj,
                       (plp_rows.reshape(64, 4, 256).transpose(1, 0, 2).reshape(PLE_DIM, PLE_DIM), 0), False),
        "w_out": (w_out, m_w_out, v_w_out, None, False),
        "w_in": (w_in, m_w_in, v_w_in, None, True),
    }
    small_params = {
        "g_attn_norm": (g_attn_norm, m_g_attn_norm, v_g_attn_norm), "g_q": (g_q, m_g_q, v_g_q), "g_k": (g_k, m_g_k, v_g_k),
        "attn_sinks": (attn_sinks, m_attn_sinks, v_attn_sinks), "rel_bias": (rel_bias, m_rel_bias, v_rel_bias),
        "w_pool": tuple(a.reshape(512, 128) for a in (w_pool, m_w_pool, v_w_pool)),
        "pool_scale": (pool_scale, m_pool_scale, v_pool_scale), "g_ffn_norm": (g_ffn_norm, m_g_ffn_norm, v_g_ffn_norm),
        "g_ple_norm": (g_ple_norm, m_g_ple_norm, v_g_ple_norm),
    }

    grads, deltas, new_ms, new_vs = {}, {}, {}, {}
    out = grads_b = None
    for name, (w, m, v, g_src, transposed) in big.items():
        if g_src is None:
            if grads_b is None:
                grads_b = _rs_finish_rows(reduce_b.finish(out[-1]), "rs_finish_b", out[-1])
            g_src = (grads_b, SLAB["out" if name == "w_out" else "inT"][0])
        view = (lambda a: a.T) if transposed else (lambda a: a)
        out = _adamw(view(w[0]), *g_src, view(m[0]), view(v[0]), "adamw_" + name)
        grads[name], deltas[name], new_ms[name], new_vs[name] = (view(a)[None] for a in out)

    loss, small_out = _adamw_small(small_all, list(small_params.values()))
    for name, (g2, d, nm, nv) in zip(small_params, small_out):
        shape = w_pool.shape if name == "w_pool" else g2.shape
        grads[name], deltas[name], new_ms[name], new_vs[name] = (a.reshape(shape) for a in (g2, d, nm, nv))

    order = ["w_in", "w_out", "g_attn_norm", "g_q", "g_k", "attn_sinks", "rel_bias", "w_pool", "pool_scale", "g_ffn_norm",
             "w_gate", "w_up", "w_down", "g_ple_norm", "w_ple_gate", "w_ple_proj"]
    return (loss.reshape(()), dx[None], *[grads[n] for n in order], *[deltas[n] for n in order],
            *[new_ms[n] for n in order], *[new_vs[n] for n in order])
```

```python
import numpy as np
import jax
import jax.numpy as jnp
from jax import lax
from jax.experimental import pallas as pl
from jax.experimental.pallas import tpu as pltpu
from jax.experimental.pallas import tpu_sc as plsc

F32 = jnp.float32
BF16 = jnp.bfloat16
MESH = pl.DeviceIdType.MESH

D_MODEL = 1024
HEAD_DIM = 64
N_Q_HEADS = 8
ATTN_WIDTH = 512
POOL_WIDTH = 512
IN_WIDTH = 1280
D_FF = 2816
PLE_DIM = 256
FF_CHUNK = 704
BLOCK = 128
N_BUCKETS = 32
MAX_DISTANCE = 128
EPS = 1e-6
NEG = -1e30
N_CHIPS = 4
N_DEV = 8

ADAM_LR = 0.001
ADAM_B1 = 0.9
ADAM_B2 = 0.999
ADAM_EPS = 1e-08
ADAM_WD = 0.01
ADAM_STEP = 10

SLAB = {"inT": (0, 320), "out": (320, 256), "gateT": (576, 704), "upT": (1280, 704), "down": (1984, 704),
        "plg": (2688, 256), "plp": (2944, 64)}
SLAB_ROWS = 3008
GATHER_PARTS = ((0, 576), (576, SLAB_ROWS))
POOL_HALO = 24

SMALL = {"g_attn": 0, "g_ffn": 8, "g_ple": 16, "pool_scale": 24, "g_q": 28, "g_k": 29, "sinks": 30, "loss": 31,
         "rel_bias": 32, "w_pool": 64}
SMALL_ROWS = 576

VMEM_LIMIT_BIG = 60 * 1024 * 1024
VMEM_LIMIT = 48 * 1024 * 1024


def _params(vmem=VMEM_LIMIT, n_axes=1):
    return pltpu.CompilerParams(dimension_semantics=("arbitrary",) * n_axes, vmem_limit_bytes=vmem)


def _dot(a, b, ca, cb):
    return lax.dot_general(a, b, (((ca,), (cb,)), ((), ())), preferred_element_type=F32)


def _full(shape):
    return pl.BlockSpec(shape, lambda i: (0,) * len(shape))


ANY = pl.BlockSpec(memory_space=pl.ANY)
VMEM_WHOLE = pl.BlockSpec(memory_space=pltpu.VMEM)


W_SPECS = [ANY, ANY, pl.BlockSpec(memory_space=pltpu.SMEM)]


def _load_rows(w_refs, name, dst_ref, sems):
    slab_ref, local_ref, me_ref = w_refs
    off, rows = SLAB[name]
    slab_off = off - max(start for start, _ in GATHER_PARTS if start <= off)
    me = me_ref[0]
    for phase in ("start", "wait"):
        for j in range(N_CHIPS):
            dst = dst_ref.at[pl.ds(j * rows, rows), :]
            theirs = pltpu.make_async_copy(slab_ref.at[j, pl.ds(slab_off, rows), :], dst, sems.at[j])
            own = pltpu.make_async_copy(local_ref.at[pl.ds(slab_off, rows), :], dst, sems.at[j])

            @pl.when(me == j)
            def _():
                getattr(own, phase)()

            @pl.when(me != j)
            def _():
                getattr(theirs, phase)()


def _rms_fwd(x, g):
    r = lax.rsqrt(jnp.mean(x * x, axis=-1, keepdims=True) + EPS)
    return x * r * g


def _rms_bwd(x, g, dy):
    r = lax.rsqrt(jnp.mean(x * x, axis=-1, keepdims=True) + EPS)
    xn = x * r
    dyg = dy * g
    dx = r * (dyg - xn * jnp.mean(dyg * xn, axis=-1, keepdims=True))
    return dx, jnp.sum(dy * xn, axis=0, keepdims=True)


def _half_sum(v, lo):
    s_lo = jnp.sum(jnp.where(lo, v, 0.0), axis=-1, keepdims=True)
    s_hi = jnp.sum(jnp.where(lo, 0.0, v), axis=-1, keepdims=True)
    return jnp.where(lo, s_lo, s_hi)


def _half_sum_mxu(v):
    upper = lax.broadcasted_iota(jnp.int32, (128, 128), 0) < 64
    left = lax.broadcasted_iota(jnp.int32, (128, 128), 1) < 64
    ones = jnp.where(upper == left, 1.0, 0.0).astype(BF16)
    high = v.astype(BF16)
    low = (v - high.astype(F32)).astype(BF16)
    return _dot(high, ones, 1, 0) + _dot(low, ones, 1, 0)


def _pair_norm(zp, g, lo):
    r = lax.rsqrt(_half_sum(zp * zp, lo) * (1.0 / HEAD_DIM) + EPS)
    return zp * r * g


def _pair_norm_bwd(zp, g, dy):
    r = lax.rsqrt(_half_sum_mxu(zp * zp) * (1.0 / HEAD_DIM) + EPS)
    xn = zp * r
    dyg = dy * g
    dx = r * (dyg - xn * (_half_sum_mxu(dyg * xn) * (1.0 / HEAD_DIM)))
    return dx, jnp.sum(dy * xn, axis=0, keepdims=True)


def _to_stacked(pair, group, lo):
    rolled = pltpu.roll(pair, 64, axis=1)
    if group == 0:
        return jnp.where(lo, pair, 0.0), jnp.where(lo, rolled, 0.0)
    return jnp.where(lo, 0.0, rolled), jnp.where(lo, 0.0, pair)


def _from_stacked(even, odd, group, lo):
    if group == 0:
        return jnp.where(lo, even, pltpu.roll(odd, 64, axis=1))
    return jnp.where(lo, pltpu.roll(even, 64, axis=1), odd)


def _sigmoid(v):
    return 1.0 / (1.0 + jnp.exp(-v))


def _pool_counts(tile, n_rows):
    t1 = tile * n_rows + lax.broadcasted_iota(jnp.int32, (n_rows, POOL_WIDTH), 0) + 1
    lane = lax.broadcasted_iota(jnp.int32, (n_rows, POOL_WIDTH), 1)
    win = jnp.where(lane < 128, 2, jnp.where(lane < 256, 4, jnp.where(lane < 384, 8, 16)))
    return jnp.minimum(t1, win).astype(F32)


def _attn_in(x2, g_attn, gq, gk, wts):
    s_len = x2.shape[0]
    t = 512

    def body(x_ref, g_ref, gq_ref, gk_ref, sl_ref, lo_ref, me_ref, hn_ref, zqk_ref, u_ref, kn_ref, v_ref, qst_ref, w_ref, sems):
        @pl.when(pl.program_id(0) == 0)
        def _():
            _load_rows((sl_ref, lo_ref, me_ref), "inT", w_ref, sems)

        hn = _rms_fwd(x_ref[...], g_ref[...]).astype(BF16)
        hn_ref[...] = hn
        z = _dot(hn, w_ref[...], 1, 1)
        zqk_ref[...] = z[:, :640]
        u_ref[...] = z[:, 768:]
        v_ref[...] = z[:, 640:768].astype(BF16)
        lo = lax.broadcasted_iota(jnp.int32, (t, 128), 1) < 64
        kn_ref[...] = _pair_norm(z[:, 512:640], gk_ref[...], lo).astype(BF16)
        for p in range(4):
            qn = _pair_norm(z[:, 128 * p:128 * p + 128], gq_ref[...], lo)
            even, odd = _to_stacked(qn, p // 2, lo)
            qst_ref[2 * p] = even.astype(BF16)
            qst_ref[2 * p + 1] = odd.astype(BF16)

    row = lambda w: pl.BlockSpec((t, w), lambda i: (i, 0))
    return pl.pallas_call(
        body, name="attn_in", grid=(s_len // t,),
        in_specs=[row(D_MODEL), _full((1, D_MODEL)), _full((1, 128)), _full((1, 128))] + W_SPECS,
        out_specs=[row(D_MODEL), row(640), row(POOL_WIDTH), row(128), row(128),
                   pl.BlockSpec((N_Q_HEADS, t, 128), lambda i: (0, i, 0))],
        out_shape=[jax.ShapeDtypeStruct((s_len, D_MODEL), BF16), jax.ShapeDtypeStruct((s_len, 640), F32),
                   jax.ShapeDtypeStruct((s_len, POOL_WIDTH), F32), jax.ShapeDtypeStruct((s_len, 128), BF16),
                   jax.ShapeDtypeStruct((s_len, 128), BF16), jax.ShapeDtypeStruct((N_Q_HEADS, s_len, 128), BF16)],
        scratch_shapes=[pltpu.VMEM((IN_WIDTH, D_MODEL), BF16), pltpu.SemaphoreType.DMA((N_CHIPS,))],
        compiler_params=_params(),
    )(x2, g_attn, gq, gk, *wts)


def _bucket_table():
    i_idx = np.arange(BLOCK)[:, None]
    j_idx = np.arange(2 * BLOCK)[None, :]
    d = BLOCK + i_idx - j_idx
    n = np.maximum(d, 0)
    max_exact = N_BUCKETS // 2
    nf = np.maximum(n, 1).astype(np.float64)
    large = max_exact + (np.log(nf / max_exact) / np.log(MAX_DISTANCE / max_exact) * (N_BUCKETS - max_exact)).astype(np.int64)
    large = np.minimum(large, N_BUCKETS - 1)
    bucket = np.where(n < max_exact, n, large)
    return np.where((d >= 0) & (d < BLOCK), bucket, -1).astype(np.int32)


def _bias_build(rel_bias_t, bucket):
    def body(rb_ref, bucket_ref, out_ref):
        bk = bucket_ref[...]
        for h in range(N_Q_HEADS):
            acc = jnp.full((BLOCK, 2 * BLOCK), NEG, F32)
            for b in range(N_BUCKETS):
                acc = jnp.where(bk == b, rb_ref[h, b], acc)
            out_ref[0, pl.ds(h * BLOCK, BLOCK), :] = acc
            out_ref[1, pl.ds(h * BLOCK, BLOCK), :] = acc
            out_ref[1, pl.ds(h * BLOCK, BLOCK), 0:BLOCK] = jnp.full((BLOCK, BLOCK), NEG, F32)

    return pl.pallas_call(
        body, name="bias_build",
        in_specs=[pl.BlockSpec(memory_space=pltpu.SMEM), VMEM_WHOLE], out_specs=VMEM_WHOLE,
        out_shape=jax.ShapeDtypeStruct((2, N_Q_HEADS * BLOCK, 2 * BLOCK), F32),
    )(rel_bias_t, bucket)


def _head_softmax(s_ref, bias_ref, sink_ref, h):
    rows = pl.ds(pl.multiple_of(h * BLOCK, BLOCK), BLOCK)
    s = s_ref[rows, :] * (HEAD_DIM ** -0.5) + bias_ref[rows, :]
    sink = sink_ref[h]
    m = jnp.maximum(jnp.max(s, axis=-1, keepdims=True), sink)
    p = jnp.exp(s - m)
    e_sink = jnp.exp(sink - m)
    inv = 1.0 / (jnp.sum(p, axis=-1, keepdims=True) + e_sink)
    return rows, p * inv, e_sink * inv


def _attn_specs():
    prev = lambda i: (jnp.maximum(i - 1, 0), 0)
    cur = lambda i: (i, 0)
    stacked = pl.BlockSpec((N_Q_HEADS, BLOCK, 128), lambda i: (0, i, 0))
    kv = [pl.BlockSpec((BLOCK, 128), prev), pl.BlockSpec((BLOCK, 128), cur)]
    consts = [pl.BlockSpec((None, N_Q_HEADS * BLOCK, 2 * BLOCK), lambda i: (jnp.where(i == 0, 1, 0), 0, 0)),
              pl.BlockSpec(memory_space=pltpu.SMEM)]
    return stacked, kv, consts


def _head_lane_mask():
    rows = lax.broadcasted_iota(jnp.int32, (N_Q_HEADS * BLOCK, 128), 0)
    lanes = lax.broadcasted_iota(jnp.int32, (N_Q_HEADS * BLOCK, 128), 1)
    return (rows < 4 * BLOCK) == (lanes < 64)


def _attn_fwd(qst, kn, vb, bias_st, sinks):
    s_len = kn.shape[0]

    def body(q_ref, kp_ref, kc_ref, vp_ref, vc_ref, bias_ref, sink_ref, o_ref, s_ref, p_ref):
        q = q_ref[...].reshape(N_Q_HEADS * BLOCK, 128)
        s_ref[...] = _dot(q, jnp.concatenate([kp_ref[...], kc_ref[...]], axis=0), 1, 1)

        def head(h, carry):
            rows, probs, _ = _head_softmax(s_ref, bias_ref, sink_ref, h)
            p_ref[rows, :] = probs.astype(BF16)
            return carry

        lax.fori_loop(0, N_Q_HEADS, head, 0, unroll=True)
        o = _dot(p_ref[...], jnp.concatenate([vp_ref[...], vc_ref[...]], axis=0), 1, 0)
        o_ref[...] = jnp.where(_head_lane_mask(), o, 0.0).astype(BF16).reshape(N_Q_HEADS, BLOCK, 128)

    stacked, kv, consts = _attn_specs()
    return pl.pallas_call(
        body, name="attn_fwd", grid=(s_len // BLOCK,),
        in_specs=[stacked] + kv + kv + consts, out_specs=stacked,
        out_shape=jax.ShapeDtypeStruct((N_Q_HEADS, s_len, 128), BF16),
        scratch_shapes=[pltpu.VMEM((N_Q_HEADS * BLOCK, 2 * BLOCK), F32), pltpu.VMEM((N_Q_HEADS * BLOCK, 2 * BLOCK), BF16)],
        compiler_params=_params(),
    )(qst, kn, kn, vb, vb, bias_st, sinks)


def _mix_out(u, ost, x2, wts, wpool, pool_scale, g_ffn):
    s_len = x2.shape[0]
    t = 512
    n = t + 16

    def body(u_ref, o_ref, x_ref, sl_ref, lo_ref, me_ref, wp_ref, sc_ref, g_ref, pooled_ref, mix_ref, h1_ref, hn_ref,
             w_ref, ext_ref, st_ref, sems):
        i = pl.program_id(0)

        @pl.when(i == 0)
        def _():
            _load_rows((sl_ref, lo_ref, me_ref), "out", w_ref, sems)
            ext_ref[...] = jnp.zeros_like(ext_ref)
            st_ref[...] = jnp.zeros_like(st_ref)

        u_tile = u_ref[...]
        ext_ref[pl.ds(POOL_HALO, t), :] = u_tile
        st_ref[pl.ds(8, n), :] = ext_ref[pl.ds(8, n), :] + ext_ref[pl.ds(7, n), :]
        st_ref[pl.ds(8, n), 128:] = st_ref[pl.ds(8, n), 128:] + st_ref[pl.ds(6, n), 128:]
        st_ref[pl.ds(8, n), 256:] = st_ref[pl.ds(8, n), 256:] + st_ref[pl.ds(4, n), 256:]
        st_ref[pl.ds(8, n), 384:] = st_ref[pl.ds(8, n), 384:] + st_ref[pl.ds(0, n), 384:]
        ext_ref[pl.ds(0, POOL_HALO), :] = ext_ref[pl.ds(t, POOL_HALO), :]
        pooled = (st_ref[pl.ds(POOL_HALO, t), :] / _pool_counts(i, t) - u_tile).astype(BF16)
        pooled_ref[...] = pooled
        for g in range(4):
            cols = slice(128 * g, 128 * g + 128)
            y = _dot(pooled[:, cols], wp_ref[g], 1, 0) * sc_ref[:, cols]
            mix_ref[:, ATTN_WIDTH + 128 * g:ATTN_WIDTH + 128 * g + 128] = y.astype(BF16)
        lo = lax.broadcasted_iota(jnp.int32, (t, 128), 1) < 64
        for p in range(4):
            a = _from_stacked(o_ref[2 * p].astype(F32), o_ref[2 * p + 1].astype(F32), p // 2, lo)
            mix_ref[:, 128 * p:128 * p + 128] = a.astype(BF16)
        h1 = x_ref[...] + _dot(mix_ref[...], w_ref[...], 1, 0)
        h1_ref[...] = h1
        hn_ref[...] = _rms_fwd(h1, g_ref[...]).astype(BF16)

    row = lambda w: pl.BlockSpec((t, w), lambda i: (i, 0))
    return pl.pallas_call(
        body, name="mix_out", grid=(s_len // t,),
        in_specs=[row(POOL_WIDTH), pl.BlockSpec((N_Q_HEADS, t, 128), lambda i: (0, i, 0)), row(D_MODEL)] + W_SPECS
        + [_full((4, 128, 128)), _full((1, POOL_WIDTH)), _full((1, D_MODEL))],
        out_specs=[row(POOL_WIDTH), row(D_MODEL), row(D_MODEL), row(D_MODEL)],
        out_shape=[jax.ShapeDtypeStruct((s_len, POOL_WIDTH), BF16), jax.ShapeDtypeStruct((s_len, D_MODEL), BF16),
                   jax.ShapeDtypeStruct((s_len, D_MODEL), F32), jax.ShapeDtypeStruct((s_len, D_MODEL), BF16)],
        scratch_shapes=[pltpu.VMEM((D_MODEL, D_MODEL), BF16), pltpu.VMEM((t + POOL_HALO, POOL_WIDTH), F32),
                        pltpu.VMEM((t + POOL_HALO, POOL_WIDTH), F32), pltpu.SemaphoreType.DMA((N_CHIPS,))],
        compiler_params=_params(),
    )(u, ost, x2, *wts, wpool, pool_scale, g_ffn)


def _ffn_ple(hn2, h1, p2, tgt, wts, g_ffn, g_ple):
    s_len = h1.shape[0]
    t = 256
    n_tiles = s_len // t

    def body(hn_ref, h1_ref, p_ref, tgt_ref, sl_ref, lo_ref, me_ref, gf_ref, gp_ref,
             loss_ref, dgate_ref, dup_ref, act_ref, dh2b_ref, hn3_ref, dgl_ref, dwp_ref, dh1_ref, dgf_ref, dgp_ref,
             wg_ref, wu_ref, wd_ref, wl_ref, wp_ref, packed_ref, gate_s, up_s, loss_acc, dwp_acc, sems):
        i = pl.program_id(0)

        @pl.when(i == 0)
        def _():
            w_refs = (sl_ref, lo_ref, me_ref)
            _load_rows(w_refs, "gateT", wg_ref, sems)
            _load_rows(w_refs, "upT", wu_ref, sems)
            _load_rows(w_refs, "down", wd_ref, sems)
            _load_rows(w_refs, "plg", wl_ref, sems)
            _load_rows(w_refs, "plp", packed_ref, sems)
            for j in range(N_CHIPS):
                for q in range(4):
                    wp_ref[pl.ds(64 * q, 64), 256 * j:256 * j + 256] = packed_ref[pl.ds(64 * j, 64), 256 * q:256 * q + 256]
            loss_acc[...] = jnp.zeros_like(loss_acc)
            dgf_ref[...] = jnp.zeros_like(dgf_ref)
            dgp_ref[...] = jnp.zeros_like(dgp_ref)

        hn = hn_ref[...]
        h1v = h1_ref[...]
        h2 = h1v
        for ch in range(N_CHIPS):
            rows = pl.ds(ch * FF_CHUNK, FF_CHUNK)
            gate = _dot(hn, wg_ref[rows, :], 1, 1)
            up = _dot(hn, wu_ref[rows, :], 1, 1)
            gate_s[ch] = gate
            up_s[ch] = up
            act = (gate * _sigmoid(gate) * up).astype(BF16)
            act_ref[ch] = act
            h2 = h2 + _dot(act, wd_ref[rows, :], 1, 0)
        gp = gp_ref[...]
        hn3 = _rms_fwd(h2, gp).astype(BF16)
        hn3_ref[...] = hn3
        gate2 = _sigmoid(_dot(hn3, wl_ref[...], 1, 0))
        p_tile = p_ref[...].astype(BF16)
        pp = _dot(p_tile, wp_ref[...], 1, 0)
        err = h2 + gate2 * pp - tgt_ref[...]
        loss_acc[...] += jnp.sum(err * err, axis=0, keepdims=True)
        dy = err * (1.0 / D_MODEL)
        _accumulate_tn(dwp_acc, p_tile, (dy * gate2).astype(BF16), i == 0)
        dgl = (dy * pp * gate2 * (1.0 - gate2)).astype(BF16)
        dgl_ref[...] = dgl
        dx3, dg3 = _rms_bwd(h2, gp, _dot(dgl, wl_ref[...], 1, 1))
        dh2 = dy + dx3
        dgp_ref[...] += dg3
        dh2b = dh2.astype(BF16)
        dh2b_ref[...] = dh2b
        dhn = jnp.zeros((t, D_MODEL), F32)
        for ch in range(N_CHIPS):
            rows = pl.ds(ch * FF_CHUNK, FF_CHUNK)
            dact = _dot(dh2b, wd_ref[rows, :], 1, 1)
            gate_v = gate_s[ch]
            up_v = up_s[ch]
            sg = _sigmoid(gate_v)
            dup = (dact * (gate_v * sg)).astype(BF16)
            dgate = (dact * up_v * (sg * (1.0 + gate_v * (1.0 - sg)))).astype(BF16)
            dup_ref[ch] = dup
            dgate_ref[ch] = dgate
            dhn = dhn + _dot(dgate, wg_ref[rows, :], 1, 0) + _dot(dup, wu_ref[rows, :], 1, 0)
        dx, dg = _rms_bwd(h1v, gf_ref[...], dhn)
        dh1_ref[...] = dh2 + dx
        dgf_ref[...] += dg

        @pl.when(i == n_tiles - 1)
        def _():
            total = jnp.sum(loss_acc[...], axis=-1, keepdims=True) * (0.5 / D_MODEL)
            loss_ref[...] = jnp.broadcast_to(total, loss_ref.shape)
            dwp_ref[...] = dwp_acc[...].astype(BF16)

    row = lambda w: pl.BlockSpec((t, w), lambda i: (i, 0))
    chunked = pl.BlockSpec((N_CHIPS, t, FF_CHUNK), lambda i: (0, i, 0))
    vec = _full((1, D_MODEL))
    act_shape = jax.ShapeDtypeStruct((N_CHIPS, s_len, FF_CHUNK), BF16)
    tok = lambda dtype: jax.ShapeDtypeStruct((s_len, D_MODEL), dtype)
    return pl.pallas_call(
        body, name="ffn_ple", grid=(n_tiles,),
        in_specs=[row(D_MODEL), row(D_MODEL), row(PLE_DIM), row(D_MODEL)] + W_SPECS + [vec, vec],
        out_specs=[_full((1, 128)), chunked, chunked, chunked] + [row(D_MODEL)] * 3 + [_full((PLE_DIM, D_MODEL)), row(D_MODEL),
                                                                                       vec, vec],
        out_shape=[jax.ShapeDtypeStruct((1, 128), F32), act_shape, act_shape, act_shape, tok(BF16), tok(BF16), tok(BF16),
                   jax.ShapeDtypeStruct((PLE_DIM, D_MODEL), BF16), tok(F32), jax.ShapeDtypeStruct((1, D_MODEL), F32),
                   jax.ShapeDtypeStruct((1, D_MODEL), F32)],
        scratch_shapes=[pltpu.VMEM((D_FF, D_MODEL), BF16)] * 3
        + [pltpu.VMEM((D_MODEL, D_MODEL), BF16), pltpu.VMEM((PLE_DIM, D_MODEL), BF16), pltpu.VMEM((PLE_DIM, D_MODEL), BF16),
           pltpu.VMEM((N_CHIPS, t, FF_CHUNK), F32), pltpu.VMEM((N_CHIPS, t, FF_CHUNK), F32), pltpu.VMEM((1, D_MODEL), F32),
           pltpu.VMEM((PLE_DIM, D_MODEL), F32), pltpu.SemaphoreType.DMA((N_CHIPS,))],
        compiler_params=_params(VMEM_LIMIT_BIG),
    )(hn2, h1, p2, tgt, *wts, g_ffn, g_ple)


def _accumulate_tn(acc_ref, a, b, first):
    @pl.when(first)
    def _():
        acc_ref[...] = _dot(a, b, 0, 0)

    @pl.when(jnp.logical_not(first))
    def _():
        acc_ref[...] += _dot(a, b, 0, 0)


def _flush_chunks(acc_ref, stage_ref, slab_ref, name, sems):
    stage_ref[...] = acc_ref[...].astype(BF16)
    off, rows = SLAB[name]
    copies = [pltpu.make_async_copy(stage_ref.at[pl.ds(j * rows, rows), :], slab_ref.at[j, pl.ds(off, rows), :], sems.at[j])
              for j in range(N_CHIPS)]
    for cp in copies:
        cp.start()
    for cp in copies:
        cp.wait()


def _mix_out_bwd(dh1, wts, pooled, wpool, pool_scale, mix, after):
    s_len = dh1.shape[0]
    t = 512
    n = t + 16
    n_tiles = s_len // t
    early_rows = GATHER_PARTS[0][1]

    def body(dh1_ref, sl_ref, lo_ref, me_ref, pooled_ref, wp_ref, sc_ref, mix_ref, after_ref, dost_ref, du_ref, dwp_ref,
             dsc_ref, slab_ref, w_ref, ext_ref, st_ref, acc_ref, stage_ref, sems):
        del after_ref
        i = pl.program_id(0)

        @pl.when(i == 0)
        def _():
            _load_rows((sl_ref, lo_ref, me_ref), "out", w_ref, sems)
            ext_ref[...] = jnp.zeros_like(ext_ref)
            st_ref[...] = jnp.zeros_like(st_ref)
            dsc_ref[...] = jnp.zeros_like(dsc_ref)
            dwp_ref[...] = jnp.zeros_like(dwp_ref)

        dh1b = dh1_ref[...].astype(BF16)
        _accumulate_tn(acc_ref, mix_ref[...], dh1b, i == 0)

        @pl.when(i == n_tiles - 1)
        def _():
            _flush_chunks(acc_ref, stage_ref, slab_ref, "out", sems)

        dmix = _dot(dh1b, w_ref[...], 1, 1)
        lo = lax.broadcasted_iota(jnp.int32, (t, 128), 1) < 64
        for p in range(4):
            even, odd = _to_stacked(dmix[:, 128 * p:128 * p + 128], p // 2, lo)
            dost_ref[2 * p] = even.astype(BF16)
            dost_ref[2 * p + 1] = odd.astype(BF16)
        pooled_v = pooled_ref[...]
        counts = _pool_counts(n_tiles - 1 - i, t)
        for g in range(4):
            cols = slice(128 * g, 128 * g + 128)
            dm = dmix[:, ATTN_WIDTH + 128 * g:ATTN_WIDTH + 128 * g + 128]
            ypre = _dot(pooled_v[:, cols], wp_ref[g], 1, 0)
            dsc_ref[:, cols] += jnp.sum(ypre * dm, axis=0, keepdims=True)
            dyp = (dm * sc_ref[:, cols]).astype(BF16)
            dwp_ref[g] += _dot(pooled_v[:, cols], dyp, 0, 0)
            dpooled = _dot(dyp, wp_ref[g], 1, 1)
            du_ref[:, cols] = -dpooled
            ext_ref[pl.ds(0, t), cols] = dpooled / counts[:, cols]
        st_ref[pl.ds(0, n), :] = ext_ref[pl.ds(0, n), :] + ext_ref[pl.ds(1, n), :]
        st_ref[pl.ds(0, n), 128:] = st_ref[pl.ds(0, n), 128:] + st_ref[pl.ds(2, n), 128:]
        st_ref[pl.ds(0, n), 256:] = st_ref[pl.ds(0, n), 256:] + st_ref[pl.ds(4, n), 256:]
        st_ref[pl.ds(0, n), 384:] = st_ref[pl.ds(0, n), 384:] + st_ref[pl.ds(8, n), 384:]
        ext_ref[pl.ds(t, POOL_HALO), :] = ext_ref[pl.ds(0, POOL_HALO), :]
        du_ref[...] += st_ref[pl.ds(0, t), :]

    rev = lambda w: pl.BlockSpec((t, w), lambda i: (n_tiles - 1 - i, 0))
    return pl.pallas_call(
        body, name="mix_out_bwd", grid=(n_tiles,),
        in_specs=[rev(D_MODEL)] + W_SPECS + [rev(POOL_WIDTH), _full((4, 128, 128)), _full((1, POOL_WIDTH)), rev(D_MODEL), ANY],
        out_specs=[pl.BlockSpec((N_Q_HEADS, t, 128), lambda i: (0, n_tiles - 1 - i, 0)), rev(POOL_WIDTH),
                   _full((4, 128, 128)), _full((1, POOL_WIDTH)), ANY],
        out_shape=[jax.ShapeDtypeStruct((N_Q_HEADS, s_len, 128), BF16), jax.ShapeDtypeStruct((s_len, POOL_WIDTH), F32),
                   jax.ShapeDtypeStruct((4, 128, 128), F32), jax.ShapeDtypeStruct((1, POOL_WIDTH), F32),
                   jax.ShapeDtypeStruct((N_CHIPS, early_rows, D_MODEL), BF16)],
        scratch_shapes=[pltpu.VMEM((D_MODEL, D_MODEL), BF16), pltpu.VMEM((t + POOL_HALO, POOL_WIDTH), F32),
                        pltpu.VMEM((t + POOL_HALO, POOL_WIDTH), F32), pltpu.VMEM((D_MODEL, D_MODEL), F32),
                        pltpu.VMEM((D_MODEL, D_MODEL), BF16), pltpu.SemaphoreType.DMA((N_CHIPS,))],
        compiler_params=_params(),
    )(dh1, *wts, pooled, wpool, pool_scale, mix, after)


def _attn_bwd(qst, kn, vb, dost, bias_st, sinks, after):
    s_len = kn.shape[0]

    def body(q_ref, kp_ref, kc_ref, vp_ref, vc_ref, do_ref, bias_ref, sink_ref, after_ref, dq_ref, dk_ref, dv_ref, dbias_ref,
             dsink_ref, s_ref, dp_ref, p_ref, dl_ref):
        del after_ref
        i = pl.program_id(0)

        @pl.when(i == 0)
        def _():
            dk_ref[...] = jnp.zeros_like(dk_ref)
            dv_ref[...] = jnp.zeros_like(dv_ref)
            dbias_ref[...] = jnp.zeros_like(dbias_ref)
            dsink_ref[...] = jnp.zeros_like(dsink_ref)

        q = q_ref[...].reshape(N_Q_HEADS * BLOCK, 128)
        do = do_ref[...].reshape(N_Q_HEADS * BLOCK, 128)
        k2 = jnp.concatenate([kp_ref[...], kc_ref[...]], axis=0)
        s_ref[...] = _dot(q, k2, 1, 1)
        dp_ref[...] = _dot(do, jnp.concatenate([vp_ref[...], vc_ref[...]], axis=0), 1, 1)

        def head(h, carry):
            rows, probs, p_sink = _head_softmax(s_ref, bias_ref, sink_ref, h)
            dp = dp_ref[rows, :]
            dsum = jnp.sum(probs * dp, axis=-1, keepdims=True)
            dlog = probs * (dp - dsum)
            dsink_ref[rows, :] -= p_sink * dsum
            dbias_ref[rows, :] += dlog
            p_ref[rows, :] = probs.astype(BF16)
            dl_ref[rows, :] = (dlog * (HEAD_DIM ** -0.5)).astype(BF16)
            return carry

        lax.fori_loop(0, N_Q_HEADS, head, 0, unroll=True)
        dlog_s = dl_ref[...]
        dq_ref[...] = jnp.where(_head_lane_mask(), _dot(dlog_s, k2, 1, 0), 0.0).reshape(N_Q_HEADS, BLOCK, 128)
        dk2 = _dot(dlog_s, q, 0, 0)
        dv2 = _dot(p_ref[...], do, 0, 0)
        prev_rows = pl.ds(pl.multiple_of(jnp.maximum(i - 1, 0) * BLOCK, BLOCK), BLOCK)
        cur_rows = pl.ds(pl.multiple_of(i * BLOCK, BLOCK), BLOCK)
        dk_ref[prev_rows, :] += dk2[:BLOCK]
        dk_ref[cur_rows, :] += dk2[BLOCK:]
        dv_ref[prev_rows, :] += dv2[:BLOCK]
        dv_ref[cur_rows, :] += dv2[BLOCK:]

    stacked, kv, consts = _attn_specs()
    band = (N_Q_HEADS * BLOCK, 2 * BLOCK)
    return pl.pallas_call(
        body, name="attn_bwd", grid=(s_len // BLOCK,),
        in_specs=[stacked] + kv + kv + [stacked] + consts + [ANY],
        out_specs=[stacked, _full((s_len, 128)), _full((s_len, 128)), _full(band), _full((N_Q_HEADS * BLOCK, 1))],
        out_shape=[jax.ShapeDtypeStruct((N_Q_HEADS, s_len, 128), F32), jax.ShapeDtypeStruct((s_len, 128), F32),
                   jax.ShapeDtypeStruct((s_len, 128), F32), jax.ShapeDtypeStruct(band, F32),
                   jax.ShapeDtypeStruct((N_Q_HEADS * BLOCK, 1), F32)],
        scratch_shapes=[pltpu.VMEM(band, F32), pltpu.VMEM(band, F32), pltpu.VMEM(band, BF16), pltpu.VMEM(band, BF16)],
        compiler_params=_params(),
    )(qst, kn, kn, vb, vb, dost, bias_st, sinks, after)


def _small_pack(dg_attn, dg_ffn, dg_ple, dscale, dgq, dgk, dbias, dsink_rows, bucket, loss_v, dwpool):
    def body(ga_ref, gf_ref, gp_ref, sc_ref, gq_ref, gk_ref, db_ref, ds_ref, bucket_ref, loss_ref, wp_ref, out_ref):
        out_ref[pl.ds(0, SMALL["w_pool"]), :] = jnp.zeros((SMALL["w_pool"], 128), F32)
        for name, ref, n in (("g_attn", ga_ref, 8), ("g_ffn", gf_ref, 8), ("g_ple", gp_ref, 8), ("pool_scale", sc_ref, 4)):
            for k in range(n):
                out_ref[pl.ds(SMALL[name] + k, 1), :] = ref[:, 128 * k:128 * k + 128]
        for name, ref in (("g_q", gq_ref), ("g_k", gk_ref)):
            both = ref[...]
            out_ref[pl.ds(SMALL[name], 1), :] = both + pltpu.roll(both, 64, axis=1)
        out_ref[pl.ds(SMALL["loss"], 1), :] = loss_ref[...]
        bk = bucket_ref[...]
        rows = lax.broadcasted_iota(jnp.int32, (N_BUCKETS, 128), 0)
        lanes = lax.broadcasted_iota(jnp.int32, (N_BUCKETS, 128), 1)
        lane1 = lax.broadcasted_iota(jnp.int32, (1, 128), 1)
        rb = jnp.zeros((N_BUCKETS, 128), F32)
        sk = jnp.zeros((1, 128), F32)
        for h in range(N_Q_HEADS):
            band = db_ref[pl.ds(h * BLOCK, BLOCK), :]
            for b in range(N_BUCKETS):
                rb = jnp.where((rows == b) & (lanes == h), jnp.sum(jnp.where(bk == b, band, 0.0)), rb)
            sk = jnp.where(lane1 == h, jnp.sum(ds_ref[pl.ds(h * BLOCK, BLOCK), :]), sk)
        out_ref[pl.ds(SMALL["rel_bias"], N_BUCKETS), :] = rb
        out_ref[pl.ds(SMALL["sinks"], 1), :] = sk
        out_ref[pl.ds(SMALL["w_pool"], 512), :] = wp_ref[...].reshape(512, 128)

    return pl.pallas_call(
        body, name="small_pack", in_specs=[VMEM_WHOLE] * 11, out_specs=VMEM_WHOLE,
        out_shape=jax.ShapeDtypeStruct((SMALL_ROWS, 128), F32),
    )(dg_attn, dg_ffn, dg_ple, dscale, dgq, dgk, dbias, dsink_rows, bucket, loss_v, dwpool)


def _attn_in_bwd(dqst, zqk, dk, dv, du, x2, dh1, hn1, slab, wts, g_attn, gq, gk):
    s_len = x2.shape[0]
    t = 512
    n_tiles = s_len // t

    def body(dq_ref, zqk_ref, dk_ref, dv_ref, du_ref, x_ref, dh1_ref, hn_ref, slab_in_ref, sl_ref, lo_ref, me_ref, g_ref,
             gq_ref, gk_ref, dx_ref, dg_ref, dgq_ref, dgk_ref, slab_ref, w_ref, dz_ref, acc_ref, stage_ref, sems):
        del slab_in_ref
        i = pl.program_id(0)

        @pl.when(i == 0)
        def _():
            _load_rows((sl_ref, lo_ref, me_ref), "inT", w_ref, sems)
            dg_ref[...] = jnp.zeros_like(dg_ref)
            dgq_ref[...] = jnp.zeros_like(dgq_ref)
            dgk_ref[...] = jnp.zeros_like(dgk_ref)

        lo = lax.broadcasted_iota(jnp.int32, (t, 128), 1) < 64
        for p in range(4):
            dqn = _from_stacked(dq_ref[2 * p], dq_ref[2 * p + 1], p // 2, lo)
            dq_raw, dgq = _pair_norm_bwd(zqk_ref[:, 128 * p:128 * p + 128], gq_ref[...], dqn)
            dz_ref[:, 128 * p:128 * p + 128] = dq_raw.astype(BF16)
            dgq_ref[...] += dgq
        dk_raw, dgk = _pair_norm_bwd(zqk_ref[:, 512:640], gk_ref[...], dk_ref[...])
        dgk_ref[...] += dgk
        dz_ref[:, 512:640] = dk_raw.astype(BF16)
        dz_ref[:, 640:768] = dv_ref[...].astype(BF16)
        dz_ref[:, 768:] = du_ref[...].astype(BF16)
        dz = dz_ref[...]
        _accumulate_tn(acc_ref, dz, hn_ref[...], i == 0)
        dx, dg = _rms_bwd(x_ref[...], g_ref[...], _dot(dz, w_ref[...], 1, 0))
        dx_ref[...] = dh1_ref[...] + dx
        dg_ref[...] += dg

        @pl.when(i == n_tiles - 1)
        def _():
            _flush_chunks(acc_ref, stage_ref, slab_ref, "inT", sems)

    row = lambda w: pl.BlockSpec((t, w), lambda i: (i, 0))
    return pl.pallas_call(
        body, name="attn_in_bwd", grid=(n_tiles,),
        in_specs=[pl.BlockSpec((N_Q_HEADS, t, 128), lambda i: (0, i, 0)), row(640), row(128), row(128), row(POOL_WIDTH),
                  row(D_MODEL), row(D_MODEL), row(D_MODEL), ANY] + W_SPECS + [_full((1, D_MODEL)), _full((1, 128)),
                                                                              _full((1, 128))],
        out_specs=[row(D_MODEL), _full((1, D_MODEL)), _full((1, 128)), _full((1, 128)), ANY],
        out_shape=[jax.ShapeDtypeStruct((s_len, D_MODEL), F32), jax.ShapeDtypeStruct((1, D_MODEL), F32),
                   jax.ShapeDtypeStruct((1, 128), F32), jax.ShapeDtypeStruct((1, 128), F32),
                   jax.ShapeDtypeStruct(slab.shape, BF16)],
        input_output_aliases={8: 4},
        scratch_shapes=[pltpu.VMEM((IN_WIDTH, D_MODEL), BF16), pltpu.VMEM((t, IN_WIDTH), BF16),
                        pltpu.VMEM((IN_WIDTH, D_MODEL), F32), pltpu.VMEM((IN_WIDTH, D_MODEL), BF16),
                        pltpu.SemaphoreType.DMA((N_CHIPS,))],
        compiler_params=_params(),
    )(dqst, zqk, dk, dv, du, x2, dh1, hn1, slab, *wts, g_attn, gq, gk)


def _dw(lefts, b, name, slab, slab_rows, row_offs):
    tk = 2048
    a0, n_a = lefts[0], len(lefts)
    assert b.shape[1] == D_MODEL
    if a0.ndim == 3:
        s_len, tm = a0.shape[1:]
        m = N_CHIPS * tm
        a_spec = pl.BlockSpec((None, tk, tm), lambda i, k: (i, k, 0))
    else:
        s_len, tm = a0.shape
        m = tm
        a_spec = pl.BlockSpec((tk, tm), lambda i, k: (k, i))
    n_steps, n_tiles = s_len // tk, m // tm
    chunk = m // N_CHIPS
    per_tile = tm // chunk

    def body(*refs):
        a_refs, b_ref = refs[:n_a], refs[n_a]
        o_ref, acc_ref, stage_ref, sems = refs[-4:]
        i, k = pl.program_id(0), pl.program_id(1)
        b_tile = b_ref[...].astype(BF16)
        for w, a_ref in enumerate(a_refs):
            _accumulate_tn(acc_ref.at[w], a_ref[...].astype(BF16), b_tile, k == 0)

        def out_copies(tile, slot):
            return [pltpu.make_async_copy(stage_ref.at[slot, w, pl.ds(jj * chunk, chunk), :],
                                          o_ref.at[tile * per_tile + jj, pl.ds(row_offs[w], chunk), :], sems.at[slot, w, jj])
                    for w in range(n_a) for jj in range(per_tile)]

        @pl.when(k == n_steps - 1)
        def _():
            slot = i % 2

            @pl.when(i >= 2)
            def _():
                for cp in out_copies(i - 2, slot):
                    cp.wait()

            stage_ref[slot] = acc_ref[...].astype(BF16)
            for cp in out_copies(i, slot):
                cp.start()

            @pl.when(i == n_tiles - 1)
            def _():
                for cp in out_copies(i, slot):
                    cp.wait()
                if n_tiles > 1:
                    for cp in out_copies(i - 1, 1 - slot):
                        cp.wait()

    in_specs = [a_spec] * n_a + [pl.BlockSpec((tk, D_MODEL), lambda i, k: (k, 0))]
    operands, aliases = [*lefts, b], {}
    if slab is not None:
        in_specs.append(ANY)
        operands.append(slab)
        aliases = {n_a + 1: 0}
    return pl.pallas_call(
        body, name=name, grid=(n_tiles, n_steps), in_specs=in_specs, out_specs=ANY,
        out_shape=jax.ShapeDtypeStruct((N_CHIPS, slab_rows, D_MODEL), BF16), input_output_aliases=aliases,
        scratch_shapes=[pltpu.VMEM((n_a, tm, D_MODEL), F32), pltpu.VMEM((2, n_a, tm, D_MODEL), BF16),
                        pltpu.SemaphoreType.DMA((2, n_a, per_tile))],
        compiler_params=_params(n_axes=2),
    )(*operands)


def _position():
    x, y, c = lax.axis_index("x"), lax.axis_index("y"), lax.axis_index("c")
    other_chips = [(1 - x, y), (x, 1 - y), (1 - x, 1 - y)]
    return x, y, c, other_chips


def _ag_weights(local_slab, row0, n_rows, name, collective_id):
    half = n_rows // 2
    quarter = half // 2
    assert quarter % 16 == 0

    def body(l_ref, g_ref, send, recv):
        x, y, c, chips = _position()
        me, (via_x, via_y, diagonal) = 2 * x + y, [2 * chip[0] + chip[1] for chip in chips]
        here, sibling, x_nbr, y_nbr = (x, y, c), (x, y, 1 - c), (1 - x, y, c), (x, 1 - y, c)
        peers = [sibling, x_nbr, y_nbr]
        barrier = pltpu.get_barrier_semaphore()
        for peer in peers:
            pl.semaphore_signal(barrier, inc=1, device_id=peer, device_id_type=MESH)
        pl.semaphore_wait(barrier, len(peers))

        def rows(core, part):
            start, size = (core * half, half) if part is None else (core * half + part * quarter, quarter)
            return pl.ds(pl.multiple_of(start, 16), size)

        def copy(k, chip_idx, where, to, src=None):
            dst = g_ref.at[chip_idx, where, :]
            return pltpu.make_async_remote_copy(src_ref=dst if src is None else src, dst_ref=dst, send_sem=send.at[k],
                                                recv_sem=recv.at[k], device_id=to, device_id_type=MESH)

        own_rows = l_ref.at[pl.ds(pl.multiple_of(row0 + c * half, 16), half), :]
        started = [copy(0, me, rows(c, None), x_nbr, src=own_rows), copy(1, me, rows(c, None), y_nbr, src=own_rows)]
        for cp in started:
            cp.start()
        after_arrival = [
            (copy(0, via_x, rows(c, None), here), [copy(4, via_x, rows(c, None), sibling), copy(3, via_x, rows(c, 1), y_nbr)]),
            (copy(1, via_y, rows(c, None), here), [copy(5, via_y, rows(c, None), sibling), copy(2, via_y, rows(c, 0), x_nbr)]),
            (copy(2, diagonal, rows(c, 0), here), [copy(6, diagonal, rows(c, 0), sibling)]),
            (copy(3, diagonal, rows(c, 1), here), [copy(7, diagonal, rows(c, 1), sibling)]),
        ]
        for arrival, onward in after_arrival:
            arrival.wait_recv()
            for cp in onward:
                cp.start()
            started += onward
        for cp in (copy(4, via_x, rows(1 - c, None), here), copy(5, via_y, rows(1 - c, None), here),
                   copy(6, diagonal, rows(1 - c, 0), here), copy(7, diagonal, rows(1 - c, 1), here)):
            cp.wait_recv()
        for cp in started:
            cp.wait_send()

    return pl.kernel(
        body, out_type=jax.ShapeDtypeStruct((N_CHIPS, n_rows, D_MODEL), BF16),
        mesh=plsc.ScalarSubcoreMesh(axis_name="sequencer", num_cores=1), name=name,
        scratch_types=[pltpu.SemaphoreType.DMA((8,)), pltpu.SemaphoreType.DMA((8,))],
        compiler_params=pltpu.CompilerParams(collective_id=collective_id),
    )(local_slab)


def _comm_call(body, peers_of, out_shape, n_sems, operand, name, collective_id):
    sems = [pltpu.SemaphoreType.DMA((n_sems,)), pltpu.SemaphoreType.DMA((n_sems,))]
    if collective_id is None:
        return pl.pallas_call(body, name=name, in_specs=[ANY], out_specs=ANY, out_shape=out_shape, scratch_shapes=sems)(operand)

    def with_handshake(in_ref, out_ref, send, recv):
        x, y, c, _ = _position()
        peers = peers_of(x, y, c)
        barrier = pltpu.get_barrier_semaphore()
        for peer in peers:
            pl.semaphore_signal(barrier, inc=1, device_id=peer, device_id_type=MESH)
        pl.semaphore_wait(barrier, len(peers))
        body(in_ref, out_ref, send, recv)

    return pl.kernel(with_handshake, out_type=out_shape, mesh=plsc.ScalarSubcoreMesh(axis_name="sequencer", num_cores=1),
                     name=name, scratch_types=sems, compiler_params=pltpu.CompilerParams(collective_id=collective_id))(operand)


def _rs_swap_halves(partial, name, collective_id=None):
    half = partial.shape[1] // 2

    def body(p_ref, r_ref, send, recv):
        x, y, c, _ = _position()
        theirs = pl.ds(pl.multiple_of((1 - c) * half, 16), half)
        cp = pltpu.make_async_remote_copy(src_ref=p_ref.at[:, theirs, :], dst_ref=r_ref, send_sem=send.at[0],
                                          recv_sem=recv.at[0], device_id=(x, y, 1 - c), device_id_type=MESH)
        cp.start()
        cp.wait()

    return _comm_call(body, lambda x, y, c: [(x, y, 1 - c)], jax.ShapeDtypeStruct((N_CHIPS, half, D_MODEL), BF16), 1,
                      partial, name, collective_id)


def _rs_add_halves(partial, other, core, name, after):
    half = other.shape[1]
    t = half // 2
    steps = half // t

    def body(core_ref, a_ref, b_ref, after_ref, o_ref):
        del after_ref
        o_ref[...] = (a_ref[...].astype(F32) + b_ref[...].astype(F32)).astype(BF16)

    return pl.pallas_call(
        body, name=name,
        grid_spec=pltpu.PrefetchScalarGridSpec(
            num_scalar_prefetch=1, grid=(N_CHIPS, steps),
            in_specs=[pl.BlockSpec((1, t, D_MODEL), lambda j, i, core_ref: (j, core_ref[0] * steps + i, 0)),
                      pl.BlockSpec((1, t, D_MODEL), lambda j, i, core_ref: (j, i, 0)), ANY],
            out_specs=pl.BlockSpec((1, t, D_MODEL), lambda j, i, core_ref: (j, i, 0))),
        out_shape=jax.ShapeDtypeStruct((N_CHIPS, half, D_MODEL), BF16),
        compiler_params=_params(n_axes=2),
    )(core, partial, other, after)


def _rs_exchange_chips(pre, name, collective_id=None):
    def body(s_ref, r_ref, send, recv):
        x, y, c, chips = _position()

        def copy(k, chunk, to):
            return pltpu.make_async_remote_copy(src_ref=s_ref.at[chunk], dst_ref=r_ref.at[k], send_sem=send.at[k],
                                                recv_sem=recv.at[k], device_id=to, device_id_type=MESH)

        sends = [copy(k, 2 * chip[0] + chip[1], (*chip, c)) for k, chip in enumerate(chips)]
        for cp in sends:
            cp.start()
        for cp in sends:
            cp.wait()

    return _comm_call(body, lambda x, y, c: [(1 - x, y, c), (x, 1 - y, c), (1 - x, 1 - y, c)],
                      jax.ShapeDtypeStruct((3, pre.shape[1], D_MODEL), BF16), 3, pre, name, collective_id)


def _rs_sum_chips(pre, received, place, name, after):
    half = pre.shape[1]
    t = half // 2 if half > 512 else half
    steps = half // t

    def body(place_ref, own_ref, r_ref, after_ref, o_ref):
        del after_ref
        acc = own_ref[0].astype(F32)
        for k in range(3):
            acc = acc + r_ref[k].astype(F32)
        o_ref[...] = acc

    return pl.pallas_call(
        body, name=name,
        grid_spec=pltpu.PrefetchScalarGridSpec(
            num_scalar_prefetch=1, grid=(steps,),
            in_specs=[pl.BlockSpec((1, t, D_MODEL), lambda i, place_ref: (place_ref[0], i, 0)),
                      pl.BlockSpec((3, t, D_MODEL), lambda i, place_ref: (0, i, 0)), ANY],
            out_specs=pl.BlockSpec((t, D_MODEL), lambda i, place_ref: (place_ref[1] * steps + i, 0))),
        out_shape=jax.ShapeDtypeStruct((2 * half, D_MODEL), F32),
        compiler_params=_params(),
    )(place, pre, received, after)


def _half_swap(g_ref, core, to, send, recv, k):
    half = g_ref.shape[0] // 2
    rows = g_ref.at[pl.ds(pl.multiple_of(core * half, 8), half), :]
    return pltpu.make_async_remote_copy(src_ref=rows, dst_ref=rows, send_sem=send.at[k], recv_sem=recv.at[k],
                                        device_id=to, device_id_type=MESH)


def _rs_finish_rows(grads, name, after):
    def body(f_ref, after_ref, g_ref, send, recv):
        del f_ref, after_ref
        x, y, c, _ = _position()
        mine = _half_swap(g_ref, c, (x, y, 1 - c), send, recv, 0)
        mine.start()
        _half_swap(g_ref, 1 - c, (x, y, c), send, recv, 0).wait_recv()
        mine.wait_send()

    return pl.pallas_call(
        body, name=name, in_specs=[ANY, ANY], out_specs=ANY, input_output_aliases={0: 0},
        out_shape=jax.ShapeDtypeStruct(grads.shape, F32),
        scratch_shapes=[pltpu.SemaphoreType.DMA((1,)), pltpu.SemaphoreType.DMA((1,))],
    )(grads, after)


def _small_gather(small, collective_id):
    def body(s_ref, t_ref, send, recv):
        x, y, c, chips = _position()
        sibling = (x, y, 1 - c)

        def slot(px, py, pc):
            return t_ref.at[4 * px + 2 * py + pc]

        def copy(k, block, to, src=None):
            return pltpu.make_async_remote_copy(src_ref=slot(*block) if src is None else src, dst_ref=slot(*block),
                                                send_sem=send.at[k], recv_sem=recv.at[k], device_id=to, device_id_type=MESH)

        own = pltpu.make_async_copy(s_ref, slot(x, y, c), send.at[7])
        own.start()
        first = [copy(0, (x, y, c), sibling, src=s_ref)]
        first += [copy(1 + k, (x, y, c), (*chip, c), src=s_ref) for k, chip in enumerate(chips)]
        for cp in first:
            cp.start()
        passed = []
        for k, chip in enumerate(chips):
            copy(1 + k, (*chip, c), (x, y, c)).wait_recv()
            fwd = copy(4 + k, (*chip, c), sibling)
            fwd.start()
            passed.append(fwd)
        copy(0, sibling, (x, y, c)).wait_recv()
        for k, chip in enumerate(chips):
            copy(4 + k, (*chip, 1 - c), (x, y, c)).wait_recv()
        for cp in first + passed:
            cp.wait_send()
        own.wait()

    peers_of = lambda x, y, c: [(x, y, 1 - c), (1 - x, y, c), (x, 1 - y, c), (1 - x, 1 - y, c)]
    return _comm_call(body, peers_of, jax.ShapeDtypeStruct((N_DEV, SMALL_ROWS, 128), F32), 8, small, "small_gather",
                      collective_id)


def _adam_update(w, g, m, v):
    m_new = ADAM_B1 * m + (1.0 - ADAM_B1) * g
    v_new = ADAM_B2 * v + (1.0 - ADAM_B2) * (g * g)
    m_hat = m_new / (1.0 - ADAM_B1 ** ADAM_STEP)
    v_hat = v_new / (1.0 - ADAM_B2 ** ADAM_STEP)
    return -ADAM_LR * (m_hat / (jnp.sqrt(v_hat) + ADAM_EPS) + ADAM_WD * w), m_new, v_new


def _adamw(w, g_rows, row_off, m, v, name):
    rows, cols = w.shape
    t = rows if rows <= 320 else (rows // 2 if rows % 256 else 256)

    def body(w_ref, g_ref, m_ref, v_ref, go_ref, d_ref, nm_ref, nv_ref):
        g = g_ref[...]
        go_ref[...] = g
        d_ref[...], nm_ref[...], nv_ref[...] = _adam_update(w_ref[...], g, m_ref[...], v_ref[...])

    blk = pl.BlockSpec((t, cols), lambda i: (i, 0))
    assert row_off % 8 == 0 and t % 8 == 0
    g_blk = pl.BlockSpec((pl.Element(t), pl.Element(cols)), lambda i: (pl.multiple_of(row_off + i * t, 8), 0))
    shape = jax.ShapeDtypeStruct((rows, cols), F32)
    return pl.pallas_call(
        body, name=name, grid=(rows // t,), in_specs=[blk, g_blk, blk, blk], out_specs=[blk] * 4, out_shape=[shape] * 4,
        compiler_params=_params(),
    )(w, g_rows, m, v)


SMALL_PARAMS = [("g_attn", (1, D_MODEL), 8), ("g_q", (1, HEAD_DIM), None), ("g_k", (1, HEAD_DIM), None),
                ("sinks", (1, N_Q_HEADS), None), ("rel_bias", (N_BUCKETS, N_Q_HEADS), None), ("w_pool", (512, 128), None),
                ("pool_scale", (1, POOL_WIDTH), 4), ("g_ffn", (1, D_MODEL), 8), ("g_ple", (1, D_MODEL), 8)]


def _adamw_small(tables, wmv):
    n_par = len(SMALL_PARAMS)

    def body(*refs):
        t_ref = refs[0]
        ins = refs[1:1 + 3 * n_par]
        loss_ref = refs[1 + 3 * n_par]
        outs = refs[2 + 3 * n_par:-1]
        tot_ref = refs[-1]
        total = t_ref[0]
        for d in range(1, N_DEV):
            total = total + t_ref[d]
        tot_ref[...] = total
        loss_ref[...] = tot_ref[pl.ds(SMALL["loss"], 1), 0:1]
        for i, (name, shape, split) in enumerate(SMALL_PARAMS):
            g_ref, d_ref, nm_ref, nv_ref = outs[4 * i:4 * i + 4]
            row = SMALL[name]
            if split:
                for k in range(split):
                    g_ref[:, 128 * k:128 * k + 128] = tot_ref[pl.ds(row + k, 1), :]
            else:
                g_ref[...] = tot_ref[pl.ds(row, shape[0]), 0:shape[1]]
            w_ref, m_ref, v_ref = ins[3 * i:3 * i + 3]
            d_ref[...], nm_ref[...], nv_ref[...] = _adam_update(w_ref[...], g_ref[...], m_ref[...], v_ref[...])

    shapes = [jax.ShapeDtypeStruct((1, 1), F32)]
    for _, shape, _ in SMALL_PARAMS:
        shapes += [jax.ShapeDtypeStruct(shape, F32)] * 4
    flat = [a for triple in wmv for a in triple]
    res = pl.pallas_call(
        body, name="adamw_small", in_specs=[VMEM_WHOLE] * (1 + 3 * n_par), out_specs=[VMEM_WHOLE] * len(shapes),
        out_shape=shapes, scratch_shapes=[pltpu.VMEM((SMALL_ROWS, 128), F32)],
    )(tables, *flat)
    return res[0], [res[1 + 4 * i:5 + 4 * i] for i in range(n_par)]


def _pack_ple_proj(shard):
    return shard.reshape(4, 64, 256).transpose(1, 0, 2).reshape(64, D_MODEL)


class _Reduction:
    def __init__(self, tag, place, ids=(None, None)):
        self.tag, self.place, self.ids = tag, place, ids

    def start(self, partial):
        self.partial = partial
        self.other = _rs_swap_halves(partial, "rs_swap_" + self.tag, self.ids[0])
        return partial

    def middle(self, after):
        self.pre = _rs_add_halves(self.partial, self.other, self.place[1:], "rs_add_" + self.tag, after)
        self.received = _rs_exchange_chips(self.pre, "rs_exchange_" + self.tag, self.ids[1])
        return self.pre

    def finish(self, after):
        return _rs_sum_chips(self.pre, self.received, self.place, "rs_sum_" + self.tag, after)


def _local_grads(x2, p2, tgt, wts, g_attn_norm, g_q, g_k, attn_sinks, rel_bias, w_pool, pool_scale, g_ffn_norm, g_ple_norm,
                 reduce_a):
    w_early, w_late = wts
    w_in = w_out = w_early
    bucket = jnp.asarray(_bucket_table())
    gq = jnp.tile(g_q, (1, 2))
    gk = jnp.tile(g_k, (1, 2))
    wpool = w_pool[0].astype(BF16)
    sinks = attn_sinks[0]
    bias_st = _bias_build(rel_bias.T, bucket)

    hn1, zqk, u, kn, vb, qst = _attn_in(x2, g_attn_norm, gq, gk, w_in)
    ost = _attn_fwd(qst, kn, vb, bias_st, sinks)
    pooled, mix, h1, hn2 = _mix_out(u, ost, x2, w_out, wpool, pool_scale, g_ffn_norm)
    loss_v, dgate, dup, act, dh2, hn3, dgl, dw_plp, dh1, dg_ffn, dg_ple = _ffn_ple(hn2, h1, p2, tgt, w_late, g_ffn_norm,
                                                                                      g_ple_norm)

    late0, late_rows = GATHER_PARTS[1][0], SLAB_ROWS - GATHER_PARTS[1][0]
    partial_a = None
    for names, lefts, right in ((("gateT", "upT"), [dgate, dup], hn2), (("down",), [act], dh2), (("plg",), [hn3], dgl)):
        partial_a = _dw(lefts, right, "dw_" + names[0], partial_a, late_rows, [SLAB[name][0] - late0 for name in names])
    dw_plp = dw_plp.reshape(4, 64, N_CHIPS, 256).transpose(2, 1, 0, 3).reshape(N_CHIPS, 64, D_MODEL)
    partial_a = reduce_a.start(lax.dynamic_update_slice(partial_a, dw_plp, (0, SLAB["plp"][0] - late0, 0)))
    dost, du, dw_pool, dscale, partial_b = _mix_out_bwd(dh1, w_out, pooled, wpool, pool_scale, mix, partial_a)
    pre_a = reduce_a.middle(du)
    dqst, dk, dv, dbias, dsink_rows = _attn_bwd(qst, kn, vb, dost, bias_st, sinks, pre_a)
    dx, dg_attn, dgq, dgk, partial_b = _attn_in_bwd(dqst, zqk, dk, dv, du, x2, dh1, hn1, partial_b, w_in, g_attn_norm, gq, gk)

    small = _small_pack(dg_attn, dg_ffn, dg_ple, dscale, dgq, dgk, dbias, dsink_rows, bucket, loss_v, dw_pool)
    return dx, partial_b, small


def kernel(x, p, w_in, w_out, g_attn_norm, g_q, g_k, attn_sinks, rel_bias, w_pool, pool_scale, g_ffn_norm, w_gate, w_up, w_down, g_ple_norm, w_ple_gate, w_ple_proj, loss_target, m_w_in, m_w_out, m_g_attn_norm, m_g_q, m_g_k, m_attn_sinks, m_rel_bias, m_w_pool, m_pool_scale, m_g_ffn_norm, m_w_gate, m_w_up, m_w_down, m_g_ple_norm, m_w_ple_gate, m_w_ple_proj, v_w_in, v_w_out, v_g_attn_norm, v_g_q, v_g_k, v_attn_sinks, v_rel_bias, v_w_pool, v_pool_scale, v_g_ffn_norm, v_w_gate, v_w_up, v_w_down, v_g_ple_norm, v_w_ple_gate, v_w_ple_proj):
    core = lax.axis_index("c").astype(jnp.int32).reshape(1)
    me = (2 * lax.axis_index("x") + lax.axis_index("y")).astype(jnp.int32).reshape(1)

    local_parts = [jnp.concatenate(pieces, axis=0).astype(BF16) for pieces in (
        [w_in[0].T, w_out[0]], [w_gate[0].T, w_up[0].T, w_down[0], w_ple_gate[0], _pack_ple_proj(w_ple_proj[0])])]
    wts = [(_ag_weights(local, 0, local.shape[0], name, collective_id), local, me)
           for local, name, collective_id in zip(local_parts, ("ag_early", "ag_late"), (1, 2))]

    place = jnp.concatenate([me, core])
    reduce_a = _Reduction("a", place, ids=(3, 4))
    dx, partial_b, small = _local_grads(x[0], p[0, 0], loss_target[0], wts, g_attn_norm, g_q, g_k, attn_sinks, rel_bias,
                                        w_pool, pool_scale, g_ffn_norm, g_ple_norm, reduce_a)
    reduce_b = _Reduction("b", place, ids=(6, 7))
    reduce_b.start(partial_b)
    small_all = _small_gather(small, 8)
    summed_a = reduce_a.finish(small)
    pre_b = reduce_b.middle(summed_a)
    grads_a = _rs_finish_rows(summed_a, "rs_finish_a", pre_b)

    late0 = GATHER_PARTS[1][0]

    def rows(name):
        return grads_a, SLAB[name][0] - late0

    plp_rows = grads_a[SLAB["plp"][0] - late0:]
    big = {
        "w_gate": (w_gate, m_w_gate, v_w_gate, rows("gateT"), True),
        "w_up": (w_up, m_w_up, v_w_up, rows("upT"), True),
        "w_down": (w_down, m_w_down, v_w_down, rows("down"), False),
        "w_ple_gate": (w_ple_gate, m_w_ple_gate, v_w_ple_gate, rows("plg"), False),
        "w_ple_proj": (w_ple_proj, m_w_ple_proj, v_w_ple_proj,
                       (plp_rows.reshape(64, 4, 256).transpose(1, 0, 2).reshape(PLE_DIM, PLE_DIM), 0), False),
        "w_out": (w_out, m_w_out, v_w_out, None, False),
        "w_in": (w_in, m_w_in, v_w_in, None, True),
    }
    small_params = {
        "g_attn_norm": (g_attn_norm, m_g_attn_norm, v_g_attn_norm), "g_q": (g_q, m_g_q, v_g_q), "g_k": (g_k, m_g_k, v_g_k),
        "attn_sinks": (attn_sinks, m_attn_sinks, v_attn_sinks), "rel_bias": (rel_bias, m_rel_bias, v_rel_bias),
        "w_pool": tuple(a.reshape(512, 128) for a in (w_pool, m_w_pool, v_w_pool)),
        "pool_scale": (pool_scale, m_pool_scale, v_pool_scale), "g_ffn_norm": (g_ffn_norm, m_g_ffn_norm, v_g_ffn_norm),
        "g_ple_norm": (g_ple_norm, m_g_ple_norm, v_g_ple_norm),
    }

    grads, deltas, new_ms, new_vs = {}, {}, {}, {}
    out = grads_b = None
    for name, (w, m, v, g_src, transposed) in big.items():
        if g_src is None:
            if grads_b is None:
                grads_b = _rs_finish_rows(reduce_b.finish(out[-1]), "rs_finish_b", out[-1])
            g_src = (grads_b, SLAB["out" if name == "w_out" else "inT"][0])
        view = (lambda a: a.T) if transposed else (lambda a: a)
        out = _adamw(view(w[0]), *g_src, view(m[0]), view(v[0]), "adamw_" + name)
        grads[name], deltas[name], new_ms[name], new_vs[name] = (view(a)[None] for a in out)

    loss, small_out = _adamw_small(small_all, list(small_params.values()))
    for name, (g2, d, nm, nv) in zip(small_params, small_out):
        shape = w_pool.shape if name == "w_pool" else g2.shape
        grads[name], deltas[name], new_ms[name], new_vs[name] = (a.reshape(shape) for a in (g2, d, nm, nv))

    order = ["w_in", "w_out", "g_attn_norm", "g_q", "g_k", "attn_sinks", "rel_bias", "w_pool", "pool_scale", "g_ffn_norm",
             "w_gate", "w_up", "w_down", "g_ple_norm", "w_ple_gate", "w_ple_proj"]
    return (loss.reshape(()), dx[None], *[grads[n] for n in order], *[deltas[n] for n in order],
            *[new_ms[n] for n in order], *[new_vs[n] for n in order])
```

```python
import numpy as np
import jax
import jax.numpy as jnp
from jax import lax
from jax.experimental import pallas as pl
from jax.experimental.pallas import tpu as pltpu
from jax.experimental.pallas import tpu_sc as plsc

F32 = jnp.float32
BF16 = jnp.bfloat16
MESH = pl.DeviceIdType.MESH

D_MODEL = 1024
HEAD_DIM = 64
N_Q_HEADS = 8
ATTN_WIDTH = 512
POOL_WIDTH = 512
IN_WIDTH = 1280
D_FF = 2816
PLE_DIM = 256
FF_CHUNK = 704
BLOCK = 128
N_BUCKETS = 32
MAX_DISTANCE = 128
EPS = 1e-6
NEG = -1e30
N_CHIPS = 4
N_DEV = 8

ADAM_LR = 0.001
ADAM_B1 = 0.9
ADAM_B2 = 0.999
ADAM_EPS = 1e-08
ADAM_WD = 0.01
ADAM_STEP = 10

SLAB = {"inT": (0, 320), "out": (320, 256), "gateT": (576, 704), "upT": (1280, 704), "down": (1984, 704),
        "plg": (2688, 256), "plp": (2944, 64)}
SLAB_ROWS = 3008
GATHER_PARTS = ((0, 576), (576, SLAB_ROWS))
POOL_HALO = 24

SMALL = {"g_attn": 0, "g_ffn": 8, "g_ple": 16, "pool_scale": 24, "g_q": 28, "g_k": 29, "sinks": 30, "loss": 31,
         "rel_bias": 32, "w_pool": 64}
SMALL_ROWS = 576

VMEM_LIMIT_BIG = 60 * 1024 * 1024
VMEM_LIMIT = 48 * 1024 * 1024


def _params(vmem=VMEM_LIMIT, n_axes=1):
    return pltpu.CompilerParams(dimension_semantics=("arbitrary",) * n_axes, vmem_limit_bytes=vmem)


def _dot(a, b, ca, cb):
    return lax.dot_general(a, b, (((ca,), (cb,)), ((), ())), preferred_element_type=F32)


def _full(shape):
    return pl.BlockSpec(shape, lambda i: (0,) * len(shape))


ANY = pl.BlockSpec(memory_space=pl.ANY)
VMEM_WHOLE = pl.BlockSpec(memory_space=pltpu.VMEM)


W_SPECS = [ANY, ANY, pl.BlockSpec(memory_space=pltpu.SMEM)]


def _load_rows(w_refs, name, dst_ref, sems):
    slab_ref, local_ref, me_ref = w_refs
    off, rows = SLAB[name]
    slab_off = off - max(start for start, _ in GATHER_PARTS if start <= off)
    me = me_ref[0]
    for phase in ("start", "wait"):
        for j in range(N_CHIPS):
            dst = dst_ref.at[pl.ds(j * rows, rows), :]
            theirs = pltpu.make_async_copy(slab_ref.at[j, pl.ds(slab_off, rows), :], dst, sems.at[j])
            own = pltpu.make_async_copy(local_ref.at[pl.ds(slab_off, rows), :], dst, sems.at[j])

            @pl.when(me == j)
            def _():
                getattr(own, phase)()

            @pl.when(me != j)
            def _():
                getattr(theirs, phase)()


def _rms_fwd(x, g):
    r = lax.rsqrt(jnp.mean(x * x, axis=-1, keepdims=True) + EPS)
    return x * r * g


def _rms_bwd(x, g, dy):
    r = lax.rsqrt(jnp.mean(x * x, axis=-1, keepdims=True) + EPS)
    xn = x * r
    dyg = dy * g
    dx = r * (dyg - xn * jnp.mean(dyg * xn, axis=-1, keepdims=True))
    return dx, jnp.sum(dy * xn, axis=0, keepdims=True)


def _half_sum(v, lo):
    s_lo = jnp.sum(jnp.where(lo, v, 0.0), axis=-1, keepdims=True)
    s_hi = jnp.sum(jnp.where(lo, 0.0, v), axis=-1, keepdims=True)
    return jnp.where(lo, s_lo, s_hi)


def _half_sum_mxu(v):
    upper = lax.broadcasted_iota(jnp.int32, (128, 128), 0) < 64
    left = lax.broadcasted_iota(jnp.int32, (128, 128), 1) < 64
    ones = jnp.where(upper == left, 1.0, 0.0).astype(BF16)
    high = v.astype(BF16)
    low = (v - high.astype(F32)).astype(BF16)
    return _dot(high, ones, 1, 0) + _dot(low, ones, 1, 0)


def _pair_norm(zp, g, lo):
    r = lax.rsqrt(_half_sum(zp * zp, lo) * (1.0 / HEAD_DIM) + EPS)
    return zp * r * g


def _pair_norm_bwd(zp, g, dy):
    r = lax.rsqrt(_half_sum_mxu(zp * zp) * (1.0 / HEAD_DIM) + EPS)
    xn = zp * r
    dyg = dy * g
    dx = r * (dyg - xn * (_half_sum_mxu(dyg * xn) * (1.0 / HEAD_DIM)))
    return dx, jnp.sum(dy * xn, axis=0, keepdims=True)


def _to_stacked(pair, group, lo):
    rolled = pltpu.roll(pair, 64, axis=1)
    if group == 0:
        return jnp.where(lo, pair, 0.0), jnp.where(lo, rolled, 0.0)
    return jnp.where(lo, 0.0, rolled), jnp.where(lo, 0.0, pair)


def _from_stacked(even, odd, group, lo):
    if group == 0:
        return jnp.where(lo, even, pltpu.roll(odd, 64, axis=1))
    return jnp.where(lo, pltpu.roll(even, 64, axis=1), odd)


def _sigmoid(v):
    return 1.0 / (1.0 + jnp.exp(-v))


def _pool_counts(tile, n_rows):
    t1 = tile * n_rows + lax.broadcasted_iota(jnp.int32, (n_rows, POOL_WIDTH), 0) + 1
    lane = lax.broadcasted_iota(jnp.int32, (n_rows, POOL_WIDTH), 1)
    win = jnp.where(lane < 128, 2, jnp.where(lane < 256, 4, jnp.where(lane < 384, 8, 16)))
    return jnp.minimum(t1, win).astype(F32)


def _attn_in(x2, g_attn, gq, gk, wts):
    s_len = x2.shape[0]
    t = 512

    def body(x_ref, g_ref, gq_ref, gk_ref, sl_ref, lo_ref, me_ref, hn_ref, zqk_ref, u_ref, kn_ref, v_ref, qst_ref, w_ref, sems):
        @pl.when(pl.program_id(0) == 0)
        def _():
            _load_rows((sl_ref, lo_ref, me_ref), "inT", w_ref, sems)

        hn = _rms_fwd(x_ref[...], g_ref[...]).astype(BF16)
        hn_ref[...] = hn
        z = _dot(hn, w_ref[...], 1, 1)
        zqk_ref[...] = z[:, :640]
        u_ref[...] = z[:, 768:]
        v_ref[...] = z[:, 640:768].astype(BF16)
        lo = lax.broadcasted_iota(jnp.int32, (t, 128), 1) < 64
        kn_ref[...] = _pair_norm(z[:, 512:640], gk_ref[...], lo).astype(BF16)
        for p in range(4):
            qn = _pair_norm(z[:, 128 * p:128 * p + 128], gq_ref[...], lo)
            even, odd = _to_stacked(qn, p // 2, lo)
            qst_ref[2 * p] = even.astype(BF16)
            qst_ref[2 * p + 1] = odd.astype(BF16)

    row = lambda w: pl.BlockSpec((t, w), lambda i: (i, 0))
    return pl.pallas_call(
        body, name="attn_in", grid=(s_len // t,),
        in_specs=[row(D_MODEL), _full((1, D_MODEL)), _full((1, 128)), _full((1, 128))] + W_SPECS,
        out_specs=[row(D_MODEL), row(640), row(POOL_WIDTH), row(128), row(128),
                   pl.BlockSpec((N_Q_HEADS, t, 128), lambda i: (0, i, 0))],
        out_shape=[jax.ShapeDtypeStruct((s_len, D_MODEL), BF16), jax.ShapeDtypeStruct((s_len, 640), F32),
                   jax.ShapeDtypeStruct((s_len, POOL_WIDTH), F32), jax.ShapeDtypeStruct((s_len, 128), BF16),
                   jax.ShapeDtypeStruct((s_len, 128), BF16), jax.ShapeDtypeStruct((N_Q_HEADS, s_len, 128), BF16)],
        scratch_shapes=[pltpu.VMEM((IN_WIDTH, D_MODEL), BF16), pltpu.SemaphoreType.DMA((N_CHIPS,))],
        compiler_params=_params(),
    )(x2, g_attn, gq, gk, *wts)


def _bucket_table():
    i_idx = np.arange(BLOCK)[:, None]
    j_idx = np.arange(2 * BLOCK)[None, :]
    d = BLOCK + i_idx - j_idx
    n = np.maximum(d, 0)
    max_exact = N_BUCKETS // 2
    nf = np.maximum(n, 1).astype(np.float64)
    large = max_exact + (np.log(nf / max_exact) / np.log(MAX_DISTANCE / max_exact) * (N_BUCKETS - max_exact)).astype(np.int64)
    large = np.minimum(large, N_BUCKETS - 1)
    bucket = np.where(n < max_exact, n, large)
    return np.where((d >= 0) & (d < BLOCK), bucket, -1).astype(np.int32)


def _bias_build(rel_bias_t, bucket):
    def body(rb_ref, bucket_ref, out_ref):
        bk = bucket_ref[...]
        for h in range(N_Q_HEADS):
            acc = jnp.full((BLOCK, 2 * BLOCK), NEG, F32)
            for b in range(N_BUCKETS):
                acc = jnp.where(bk == b, rb_ref[h, b], acc)
            out_ref[0, pl.ds(h * BLOCK, BLOCK), :] = acc
            out_ref[1, pl.ds(h * BLOCK, BLOCK), :] = acc
            out_ref[1, pl.ds(h * BLOCK, BLOCK), 0:BLOCK] = jnp.full((BLOCK, BLOCK), NEG, F32)

    return pl.pallas_call(
        body, name="bias_build",
        in_specs=[pl.BlockSpec(memory_space=pltpu.SMEM), VMEM_WHOLE], out_specs=VMEM_WHOLE,
        out_shape=jax.ShapeDtypeStruct((2, N_Q_HEADS * BLOCK, 2 * BLOCK), F32),
    )(rel_bias_t, bucket)


def _head_softmax(s_ref, bias_ref, sink_ref, h):
    rows = pl.ds(pl.multiple_of(h * BLOCK, BLOCK), BLOCK)
    s = s_ref[rows, :] * (HEAD_DIM ** -0.5) + bias_ref[rows, :]
    sink = sink_ref[h]
    m = jnp.maximum(jnp.max(s, axis=-1, keepdims=True), sink)
    p = jnp.exp(s - m)
    e_sink = jnp.exp(sink - m)
    inv = 1.0 / (jnp.sum(p, axis=-1, keepdims=True) + e_sink)
    return rows, p * inv, e_sink * inv


ATTN_STEP_BLOCKS = 2
BAND = (N_Q_HEADS * BLOCK, 2 * BLOCK)


def _attn_specs():
    nb = ATTN_STEP_BLOCKS
    stacked = pl.BlockSpec((N_Q_HEADS, nb * BLOCK, 128), lambda i: (0, i, 0))
    kv = [pl.BlockSpec((BLOCK, 128), lambda i: (jnp.maximum(nb * i - 1, 0), 0)), pl.BlockSpec((nb * BLOCK, 128), lambda i: (i, 0))]
    consts = [_full((2,) + BAND), pl.BlockSpec(memory_space=pltpu.SMEM)]
    return stacked, kv, consts


def _step_blocks(i, kp_ref, kc_ref, vp_ref, vc_ref, bias_ref):
    blocks = []
    for b in range(ATTN_STEP_BLOCKS):
        if b == 0:
            k2 = jnp.concatenate([kp_ref[...], kc_ref[pl.ds(0, BLOCK), :]], axis=0)
            v2 = jnp.concatenate([vp_ref[...], vc_ref[pl.ds(0, BLOCK), :]], axis=0)
            bias = bias_ref.at[jnp.where(i == 0, 1, 0)]
        else:
            k2, v2, bias = kc_ref[pl.ds((b - 1) * BLOCK, 2 * BLOCK), :], vc_ref[pl.ds((b - 1) * BLOCK, 2 * BLOCK), :], bias_ref.at[0]
        blocks.append((pl.ds(b * BLOCK, BLOCK), k2, v2, bias))
    return blocks


def _head_lane_mask():
    rows = lax.broadcasted_iota(jnp.int32, (N_Q_HEADS * BLOCK, 128), 0)
    lanes = lax.broadcasted_iota(jnp.int32, (N_Q_HEADS * BLOCK, 128), 1)
    return (rows < 4 * BLOCK) == (lanes < 64)


def _attn_fwd(qst, kn, vb, bias_st, sinks):
    s_len = kn.shape[0]

    def body(q_ref, kp_ref, kc_ref, vp_ref, vc_ref, bias_ref, sink_ref, o_ref, s_ref, p_ref):
        for b, (rows, k2, v2, bias) in enumerate(_step_blocks(pl.program_id(0), kp_ref, kc_ref, vp_ref, vc_ref, bias_ref)):
            s_b, p_b = s_ref.at[b], p_ref.at[b]
            s_b[...] = _dot(q_ref[:, rows, :].reshape(N_Q_HEADS * BLOCK, 128), k2, 1, 1)

            def head(h, carry):
                head_rows, probs, _ = _head_softmax(s_b, bias, sink_ref, h)
                p_b[head_rows, :] = probs.astype(BF16)
                return carry

            lax.fori_loop(0, N_Q_HEADS, head, 0, unroll=True)
            o = jnp.where(_head_lane_mask(), _dot(p_b[...], v2, 1, 0), 0.0)
            o_ref[:, rows, :] = o.astype(BF16).reshape(N_Q_HEADS, BLOCK, 128)

    stacked, kv, consts = _attn_specs()
    return pl.pallas_call(
        body, name="attn_fwd", grid=(s_len // (ATTN_STEP_BLOCKS * BLOCK),),
        in_specs=[stacked] + kv + kv + consts, out_specs=stacked,
        out_shape=jax.ShapeDtypeStruct((N_Q_HEADS, s_len, 128), BF16),
        scratch_shapes=[pltpu.VMEM((ATTN_STEP_BLOCKS,) + BAND, F32), pltpu.VMEM((ATTN_STEP_BLOCKS,) + BAND, BF16)],
        compiler_params=_params(),
    )(qst, kn, kn, vb, vb, bias_st, sinks)


def _mix_out(u, ost, x2, wts, wpool, pool_scale, g_ffn):
    s_len = x2.shape[0]
    t = 512
    n = t + 16

    def body(u_ref, o_ref, x_ref, sl_ref, lo_ref, me_ref, wp_ref, sc_ref, g_ref, pooled_ref, mix_ref, h1_ref, hn_ref,
             w_ref, ext_ref, st_ref, sems):
        i = pl.program_id(0)

        @pl.when(i == 0)
        def _():
            _load_rows((sl_ref, lo_ref, me_ref), "out", w_ref, sems)
            ext_ref[...] = jnp.zeros_like(ext_ref)
            st_ref[...] = jnp.zeros_like(st_ref)

        u_tile = u_ref[...]
        ext_ref[pl.ds(POOL_HALO, t), :] = u_tile
        st_ref[pl.ds(8, n), :] = ext_ref[pl.ds(8, n), :] + ext_ref[pl.ds(7, n), :]
        st_ref[pl.ds(8, n), 128:] = st_ref[pl.ds(8, n), 128:] + st_ref[pl.ds(6, n), 128:]
        st_ref[pl.ds(8, n), 256:] = st_ref[pl.ds(8, n), 256:] + st_ref[pl.ds(4, n), 256:]
        st_ref[pl.ds(8, n), 384:] = st_ref[pl.ds(8, n), 384:] + st_ref[pl.ds(0, n), 384:]
        ext_ref[pl.ds(0, POOL_HALO), :] = ext_ref[pl.ds(t, POOL_HALO), :]
        pooled = (st_ref[pl.ds(POOL_HALO, t), :] / _pool_counts(i, t) - u_tile).astype(BF16)
        pooled_ref[...] = pooled
        for g in range(4):
            cols = slice(128 * g, 128 * g + 128)
            y = _dot(pooled[:, cols], wp_ref[g], 1, 0) * sc_ref[:, cols]
            mix_ref[:, ATTN_WIDTH + 128 * g:ATTN_WIDTH + 128 * g + 128] = y.astype(BF16)
        lo = lax.broadcasted_iota(jnp.int32, (t, 128), 1) < 64
        for p in range(4):
            a = _from_stacked(o_ref[2 * p].astype(F32), o_ref[2 * p + 1].astype(F32), p // 2, lo)
            mix_ref[:, 128 * p:128 * p + 128] = a.astype(BF16)
        h1 = x_ref[...] + _dot(mix_ref[...], w_ref[...], 1, 0)
        h1_ref[...] = h1
        hn_ref[...] = _rms_fwd(h1, g_ref[...]).astype(BF16)

    row = lambda w: pl.BlockSpec((t, w), lambda i: (i, 0))
    return pl.pallas_call(
        body, name="mix_out", grid=(s_len // t,),
        in_specs=[row(POOL_WIDTH), pl.BlockSpec((N_Q_HEADS, t, 128), lambda i: (0, i, 0)), row(D_MODEL)] + W_SPECS
        + [_full((4, 128, 128)), _full((1, POOL_WIDTH)), _full((1, D_MODEL))],
        out_specs=[row(POOL_WIDTH), row(D_MODEL), row(D_MODEL), row(D_MODEL)],
        out_shape=[jax.ShapeDtypeStruct((s_len, POOL_WIDTH), BF16), jax.ShapeDtypeStruct((s_len, D_MODEL), BF16),
                   jax.ShapeDtypeStruct((s_len, D_MODEL), F32), jax.ShapeDtypeStruct((s_len, D_MODEL), BF16)],
        scratch_shapes=[pltpu.VMEM((D_MODEL, D_MODEL), BF16), pltpu.VMEM((t + POOL_HALO, POOL_WIDTH), F32),
                        pltpu.VMEM((t + POOL_HALO, POOL_WIDTH), F32), pltpu.SemaphoreType.DMA((N_CHIPS,))],
        compiler_params=_params(),
    )(u, ost, x2, *wts, wpool, pool_scale, g_ffn)


def _ffn_ple(hn2, h1, p2, tgt, wts, g_ffn, g_ple):
    s_len = h1.shape[0]
    t = 256
    n_tiles = s_len // t

    def body(hn_ref, h1_ref, p_ref, tgt_ref, sl_ref, lo_ref, me_ref, gf_ref, gp_ref,
             loss_ref, dgate_ref, dup_ref, act_ref, dh2b_ref, hn3_ref, dgl_ref, dwp_ref, dh1_ref, dgf_ref, dgp_ref,
             wg_ref, wu_ref, wd_ref, wl_ref, wp_ref, packed_ref, gate_s, up_s, loss_acc, dwp_acc, sems):
        i = pl.program_id(0)

        @pl.when(i == 0)
        def _():
            w_refs = (sl_ref, lo_ref, me_ref)
            _load_rows(w_refs, "gateT", wg_ref, sems)
            _load_rows(w_refs, "upT", wu_ref, sems)
            _load_rows(w_refs, "down", wd_ref, sems)
            _load_rows(w_refs, "plg", wl_ref, sems)
            _load_rows(w_refs, "plp", packed_ref, sems)
            for j in range(N_CHIPS):
                for q in range(4):
                    wp_ref[pl.ds(64 * q, 64), 256 * j:256 * j + 256] = packed_ref[pl.ds(64 * j, 64), 256 * q:256 * q + 256]
            loss_acc[...] = jnp.zeros_like(loss_acc)
            dgf_ref[...] = jnp.zeros_like(dgf_ref)
            dgp_ref[...] = jnp.zeros_like(dgp_ref)

        hn = hn_ref[...]
        h1v = h1_ref[...]
        h2 = h1v
        for ch in range(N_CHIPS):
            rows = pl.ds(ch * FF_CHUNK, FF_CHUNK)
            gate = _dot(hn, wg_ref[rows, :], 1, 1)
            up = _dot(hn, wu_ref[rows, :], 1, 1)
            gate_s[ch] = gate
            up_s[ch] = up
            act = (gate * _sigmoid(gate) * up).astype(BF16)
            act_ref[ch] = act
            h2 = h2 + _dot(act, wd_ref[rows, :], 1, 0)
        gp = gp_ref[...]
        hn3 = _rms_fwd(h2, gp).astype(BF16)
        hn3_ref[...] = hn3
        gate2 = _sigmoid(_dot(hn3, wl_ref[...], 1, 0))
        p_tile = p_ref[...].astype(BF16)
        pp = _dot(p_tile, wp_ref[...], 1, 0)
        err = h2 + gate2 * pp - tgt_ref[...]
        loss_acc[...] += jnp.sum(err * err, axis=0, keepdims=True)
        dy = err * (1.0 / D_MODEL)
        _accumulate_tn(dwp_acc, p_tile, (dy * gate2).astype(BF16), i == 0)
        dgl = (dy * pp * gate2 * (1.0 - gate2)).astype(BF16)
        dgl_ref[...] = dgl
        dx3, dg3 = _rms_bwd(h2, gp, _dot(dgl, wl_ref[...], 1, 1))
        dh2 = dy + dx3
        dgp_ref[...] += dg3
        dh2b = dh2.astype(BF16)
        dh2b_ref[...] = dh2b
        dhn = jnp.zeros((t, D_MODEL), F32)
        for ch in range(N_CHIPS):
            rows = pl.ds(ch * FF_CHUNK, FF_CHUNK)
            dact = _dot(dh2b, wd_ref[rows, :], 1, 1)
            gate_v = gate_s[ch]
            up_v = up_s[ch]
            sg = _sigmoid(gate_v)
            dup = (dact * (gate_v * sg)).astype(BF16)
            dgate = (dact * up_v * (sg * (1.0 + gate_v * (1.0 - sg)))).astype(BF16)
            dup_ref[ch] = dup
            dgate_ref[ch] = dgate
            dhn = dhn + _dot(dgate, wg_ref[rows, :], 1, 0) + _dot(dup, wu_ref[rows, :], 1, 0)
        dx, dg = _rms_bwd(h1v, gf_ref[...], dhn)
        dh1_ref[...] = dh2 + dx
        dgf_ref[...] += dg

        @pl.when(i == n_tiles - 1)
        def _():
            total = jnp.sum(loss_acc[...], axis=-1, keepdims=True) * (0.5 / D_MODEL)
            loss_ref[...] = jnp.broadcast_to(total, loss_ref.shape)
            dwp_ref[...] = dwp_acc[...].astype(BF16)

    row = lambda w: pl.BlockSpec((t, w), lambda i: (i, 0))
    chunked = pl.BlockSpec((N_CHIPS, t, FF_CHUNK), lambda i: (0, i, 0))
    vec = _full((1, D_MODEL))
    act_shape = jax.ShapeDtypeStruct((N_CHIPS, s_len, FF_CHUNK), BF16)
    tok = lambda dtype: jax.ShapeDtypeStruct((s_len, D_MODEL), dtype)
    return pl.pallas_call(
        body, name="ffn_ple", grid=(n_tiles,),
        in_specs=[row(D_MODEL), row(D_MODEL), row(PLE_DIM), row(D_MODEL)] + W_SPECS + [vec, vec],
        out_specs=[_full((1, 128)), chunked, chunked, chunked] + [row(D_MODEL)] * 3 + [_full((PLE_DIM, D_MODEL)), row(D_MODEL),
                                                                                       vec, vec],
        out_shape=[jax.ShapeDtypeStruct((1, 128), F32), act_shape, act_shape, act_shape, tok(BF16), tok(BF16), tok(BF16),
                   jax.ShapeDtypeStruct((PLE_DIM, D_MODEL), BF16), tok(F32), jax.ShapeDtypeStruct((1, D_MODEL), F32),
                   jax.ShapeDtypeStruct((1, D_MODEL), F32)],
        scratch_shapes=[pltpu.VMEM((D_FF, D_MODEL), BF16)] * 3
        + [pltpu.VMEM((D_MODEL, D_MODEL), BF16), pltpu.VMEM((PLE_DIM, D_MODEL), BF16), pltpu.VMEM((PLE_DIM, D_MODEL), BF16),
           pltpu.VMEM((N_CHIPS, t, FF_CHUNK), F32), pltpu.VMEM((N_CHIPS, t, FF_CHUNK), F32), pltpu.VMEM((1, D_MODEL), F32),
           pltpu.VMEM((PLE_DIM, D_MODEL), F32), pltpu.SemaphoreType.DMA((N_CHIPS,))],
        compiler_params=_params(VMEM_LIMIT_BIG),
    )(hn2, h1, p2, tgt, *wts, g_ffn, g_ple)


def _accumulate_tn(acc_ref, a, b, first):
    @pl.when(first)
    def _():
        acc_ref[...] = _dot(a, b, 0, 0)

    @pl.when(jnp.logical_not(first))
    def _():
        acc_ref[...] += _dot(a, b, 0, 0)


def _flush_chunks(acc_ref, stage_ref, slab_ref, name, sems):
    stage_ref[...] = acc_ref[...].astype(BF16)
    off, rows = SLAB[name]
    copies = [pltpu.make_async_copy(stage_ref.at[pl.ds(j * rows, rows), :], slab_ref.at[j, pl.ds(off, rows), :], sems.at[j])
              for j in range(N_CHIPS)]
    for cp in copies:
        cp.start()
    for cp in copies:
        cp.wait()


def _mix_out_bwd(dh1, wts, pooled, wpool, pool_scale, mix, after):
    s_len = dh1.shape[0]
    t = 512
    n = t + 16
    n_tiles = s_len // t
    early_rows = GATHER_PARTS[0][1]

    def body(dh1_ref, sl_ref, lo_ref, me_ref, pooled_ref, wp_ref, sc_ref, mix_ref, after_ref, dost_ref, du_ref, dwp_ref,
             dsc_ref, slab_ref, w_ref, ext_ref, st_ref, acc_ref, stage_ref, sems):
        del after_ref
        i = pl.program_id(0)

        @pl.when(i == 0)
        def _():
            _load_rows((sl_ref, lo_ref, me_ref), "out", w_ref, sems)
            ext_ref[...] = jnp.zeros_like(ext_ref)
            st_ref[...] = jnp.zeros_like(st_ref)
            dsc_ref[...] = jnp.zeros_like(dsc_ref)
            dwp_ref[...] = jnp.zeros_like(dwp_ref)

        dh1b = dh1_ref[...].astype(BF16)
        _accumulate_tn(acc_ref, mix_ref[...], dh1b, i == 0)

        @pl.when(i == n_tiles - 1)
        def _():
            _flush_chunks(acc_ref, stage_ref, slab_ref, "out", sems)

        dmix = _dot(dh1b, w_ref[...], 1, 1)
        lo = lax.broadcasted_iota(jnp.int32, (t, 128), 1) < 64
        for p in range(4):
            even, odd = _to_stacked(dmix[:, 128 * p:128 * p + 128], p // 2, lo)
            dost_ref[2 * p] = even.astype(BF16)
            dost_ref[2 * p + 1] = odd.astype(BF16)
        pooled_v = pooled_ref[...]
        counts = _pool_counts(n_tiles - 1 - i, t)
        for g in range(4):
            cols = slice(128 * g, 128 * g + 128)
            dm = dmix[:, ATTN_WIDTH + 128 * g:ATTN_WIDTH + 128 * g + 128]
            ypre = _dot(pooled_v[:, cols], wp_ref[g], 1, 0)
            dsc_ref[:, cols] += jnp.sum(ypre * dm, axis=0, keepdims=True)
            dyp = (dm * sc_ref[:, cols]).astype(BF16)
            dwp_ref[g] += _dot(pooled_v[:, cols], dyp, 0, 0)
            dpooled = _dot(dyp, wp_ref[g], 1, 1)
            du_ref[:, cols] = -dpooled
            ext_ref[pl.ds(0, t), cols] = dpooled / counts[:, cols]
        st_ref[pl.ds(0, n), :] = ext_ref[pl.ds(0, n), :] + ext_ref[pl.ds(1, n), :]
        st_ref[pl.ds(0, n), 128:] = st_ref[pl.ds(0, n), 128:] + st_ref[pl.ds(2, n), 128:]
        st_ref[pl.ds(0, n), 256:] = st_ref[pl.ds(0, n), 256:] + st_ref[pl.ds(4, n), 256:]
        st_ref[pl.ds(0, n), 384:] = st_ref[pl.ds(0, n), 384:] + st_ref[pl.ds(8, n), 384:]
        ext_ref[pl.ds(t, POOL_HALO), :] = ext_ref[pl.ds(0, POOL_HALO), :]
        du_ref[...] += st_ref[pl.ds(0, t), :]

    rev = lambda w: pl.BlockSpec((t, w), lambda i: (n_tiles - 1 - i, 0))
    return pl.pallas_call(
        body, name="mix_out_bwd", grid=(n_tiles,),
        in_specs=[rev(D_MODEL)] + W_SPECS + [rev(POOL_WIDTH), _full((4, 128, 128)), _full((1, POOL_WIDTH)), rev(D_MODEL), ANY],
        out_specs=[pl.BlockSpec((N_Q_HEADS, t, 128), lambda i: (0, n_tiles - 1 - i, 0)), rev(POOL_WIDTH),
                   _full((4, 128, 128)), _full((1, POOL_WIDTH)), ANY],
        out_shape=[jax.ShapeDtypeStruct((N_Q_HEADS, s_len, 128), BF16), jax.ShapeDtypeStruct((s_len, POOL_WIDTH), F32),
                   jax.ShapeDtypeStruct((4, 128, 128), F32), jax.ShapeDtypeStruct((1, POOL_WIDTH), F32),
                   jax.ShapeDtypeStruct((N_CHIPS, early_rows, D_MODEL), BF16)],
        scratch_shapes=[pltpu.VMEM((D_MODEL, D_MODEL), BF16), pltpu.VMEM((t + POOL_HALO, POOL_WIDTH), F32),
                        pltpu.VMEM((t + POOL_HALO, POOL_WIDTH), F32), pltpu.VMEM((D_MODEL, D_MODEL), F32),
                        pltpu.VMEM((D_MODEL, D_MODEL), BF16), pltpu.SemaphoreType.DMA((N_CHIPS,))],
        compiler_params=_params(),
    )(dh1, *wts, pooled, wpool, pool_scale, mix, after)


def _attn_bwd(qst, kn, vb, dost, bias_st, sinks, after):
    s_len = kn.shape[0]

    def body(q_ref, kp_ref, kc_ref, vp_ref, vc_ref, do_ref, bias_ref, sink_ref, after_ref, dq_ref, dk_ref, dv_ref, dbias_ref,
             dsink_ref, s_ref, dp_ref, p_ref, dl_ref):
        del after_ref
        i = pl.program_id(0)

        @pl.when(i == 0)
        def _():
            dk_ref[...] = jnp.zeros_like(dk_ref)
            dv_ref[...] = jnp.zeros_like(dv_ref)
            dbias_ref[...] = jnp.zeros_like(dbias_ref)
            dsink_ref[...] = jnp.zeros_like(dsink_ref)

        for b, (rows, k2, v2, bias) in enumerate(_step_blocks(i, kp_ref, kc_ref, vp_ref, vc_ref, bias_ref)):
            s_b, dp_b, p_b, dl_b = s_ref.at[b], dp_ref.at[b], p_ref.at[b], dl_ref.at[b]
            q = q_ref[:, rows, :].reshape(N_Q_HEADS * BLOCK, 128)
            do = do_ref[:, rows, :].reshape(N_Q_HEADS * BLOCK, 128)
            s_b[...] = _dot(q, k2, 1, 1)
            dp_b[...] = _dot(do, v2, 1, 1)

            def head(h, carry):
                head_rows, probs, p_sink = _head_softmax(s_b, bias, sink_ref, h)
                dp = dp_b[head_rows, :]
                dsum = jnp.sum(probs * dp, axis=-1, keepdims=True)
                dlog = probs * (dp - dsum)
                dsink_ref[head_rows, :] -= p_sink * dsum
                dbias_ref[head_rows, :] += dlog
                p_b[head_rows, :] = probs.astype(BF16)
                dl_b[head_rows, :] = (dlog * (HEAD_DIM ** -0.5)).astype(BF16)
                return carry

            lax.fori_loop(0, N_Q_HEADS, head, 0, unroll=True)
            dlog_s = dl_b[...]
            dq_ref[:, rows, :] = jnp.where(_head_lane_mask(), _dot(dlog_s, k2, 1, 0), 0.0).reshape(N_Q_HEADS, BLOCK, 128)
            dk2 = _dot(dlog_s, q, 0, 0)
            dv2 = _dot(p_b[...], do, 0, 0)
            block = ATTN_STEP_BLOCKS * i + b
            prev_rows = pl.ds(pl.multiple_of(jnp.maximum(block - 1, 0) * BLOCK, BLOCK), BLOCK)
            cur_rows = pl.ds(pl.multiple_of(block * BLOCK, BLOCK), BLOCK)
            dk_ref[prev_rows, :] += dk2[:BLOCK]
            dk_ref[cur_rows, :] += dk2[BLOCK:]
            dv_ref[prev_rows, :] += dv2[:BLOCK]
            dv_ref[cur_rows, :] += dv2[BLOCK:]

    stacked, kv, consts = _attn_specs()
    per_step = (ATTN_STEP_BLOCKS,) + BAND
    return pl.pallas_call(
        body, name="attn_bwd", grid=(s_len // (ATTN_STEP_BLOCKS * BLOCK),),
        in_specs=[stacked] + kv + kv + [stacked] + consts + [ANY],
        out_specs=[stacked, _full((s_len, 128)), _full((s_len, 128)), _full(BAND), _full((N_Q_HEADS * BLOCK, 1))],
        out_shape=[jax.ShapeDtypeStruct((N_Q_HEADS, s_len, 128), F32), jax.ShapeDtypeStruct((s_len, 128), F32),
                   jax.ShapeDtypeStruct((s_len, 128), F32), jax.ShapeDtypeStruct(BAND, F32),
                   jax.ShapeDtypeStruct((N_Q_HEADS * BLOCK, 1), F32)],
        scratch_shapes=[pltpu.VMEM(per_step, F32), pltpu.VMEM(per_step, F32), pltpu.VMEM(per_step, BF16),
                        pltpu.VMEM(per_step, BF16)],
        compiler_params=_params(),
    )(qst, kn, kn, vb, vb, dost, bias_st, sinks, after)


def _small_pack(dg_attn, dg_ffn, dg_ple, dscale, dgq, dgk, dbias, dsink_rows, bucket, loss_v, dwpool):
    def body(ga_ref, gf_ref, gp_ref, sc_ref, gq_ref, gk_ref, db_ref, ds_ref, bucket_ref, loss_ref, wp_ref, out_ref):
        out_ref[pl.ds(0, SMALL["w_pool"]), :] = jnp.zeros((SMALL["w_pool"], 128), F32)
        for name, ref, n in (("g_attn", ga_ref, 8), ("g_ffn", gf_ref, 8), ("g_ple", gp_ref, 8), ("pool_scale", sc_ref, 4)):
            for k in range(n):
                out_ref[pl.ds(SMALL[name] + k, 1), :] = ref[:, 128 * k:128 * k + 128]
        for name, ref in (("g_q", gq_ref), ("g_k", gk_ref)):
            both = ref[...]
            out_ref[pl.ds(SMALL[name], 1), :] = both + pltpu.roll(both, 64, axis=1)
        out_ref[pl.ds(SMALL["loss"], 1), :] = loss_ref[...]
        bk = bucket_ref[...]
        rows = lax.broadcasted_iota(jnp.int32, (N_BUCKETS, 128), 0)
        lanes = lax.broadcasted_iota(jnp.int32, (N_BUCKETS, 128), 1)
        lane1 = lax.broadcasted_iota(jnp.int32, (1, 128), 1)
        rb = jnp.zeros((N_BUCKETS, 128), F32)
        sk = jnp.zeros((1, 128), F32)
        for h in range(N_Q_HEADS):
            band = db_ref[pl.ds(h * BLOCK, BLOCK), :]
            for b in range(N_BUCKETS):
                rb = jnp.where((rows == b) & (lanes == h), jnp.sum(jnp.where(bk == b, band, 0.0)), rb)
            sk = jnp.where(lane1 == h, jnp.sum(ds_ref[pl.ds(h * BLOCK, BLOCK), :]), sk)
        out_ref[pl.ds(SMALL["rel_bias"], N_BUCKETS), :] = rb
        out_ref[pl.ds(SMALL["sinks"], 1), :] = sk
        out_ref[pl.ds(SMALL["w_pool"], 512), :] = wp_ref[...].reshape(512, 128)

    return pl.pallas_call(
        body, name="small_pack", in_specs=[VMEM_WHOLE] * 11, out_specs=VMEM_WHOLE,
        out_shape=jax.ShapeDtypeStruct((SMALL_ROWS, 128), F32),
    )(dg_attn, dg_ffn, dg_ple, dscale, dgq, dgk, dbias, dsink_rows, bucket, loss_v, dwpool)


def _attn_in_bwd(dqst, zqk, dk, dv, du, x2, dh1, hn1, slab, wts, g_attn, gq, gk):
    s_len = x2.shape[0]
    t = 512
    n_tiles = s_len // t

    def body(dq_ref, zqk_ref, dk_ref, dv_ref, du_ref, x_ref, dh1_ref, hn_ref, slab_in_ref, sl_ref, lo_ref, me_ref, g_ref,
             gq_ref, gk_ref, dx_ref, dg_ref, dgq_ref, dgk_ref, slab_ref, w_ref, dz_ref, acc_ref, stage_ref, sems):
        del slab_in_ref
        i = pl.program_id(0)

        @pl.when(i == 0)
        def _():
            _load_rows((sl_ref, lo_ref, me_ref), "inT", w_ref, sems)
            dg_ref[...] = jnp.zeros_like(dg_ref)
            dgq_ref[...] = jnp.zeros_like(dgq_ref)
            dgk_ref[...] = jnp.zeros_like(dgk_ref)

        lo = lax.broadcasted_iota(jnp.int32, (t, 128), 1) < 64
        for p in range(4):
            dqn = _from_stacked(dq_ref[2 * p], dq_ref[2 * p + 1], p // 2, lo)
            dq_raw, dgq = _pair_norm_bwd(zqk_ref[:, 128 * p:128 * p + 128], gq_ref[...], dqn)
            dz_ref[:, 128 * p:128 * p + 128] = dq_raw.astype(BF16)
            dgq_ref[...] += dgq
        dk_raw, dgk = _pair_norm_bwd(zqk_ref[:, 512:640], gk_ref[...], dk_ref[...])
        dgk_ref[...] += dgk
        dz_ref[:, 512:640] = dk_raw.astype(BF16)
        dz_ref[:, 640:768] = dv_ref[...].astype(BF16)
        dz_ref[:, 768:] = du_ref[...].astype(BF16)
        dz = dz_ref[...]
        _accumulate_tn(acc_ref, dz, hn_ref[...], i == 0)
        dx, dg = _rms_bwd(x_ref[...], g_ref[...], _dot(dz, w_ref[...], 1, 0))
        dx_ref[...] = dh1_ref[...] + dx
        dg_ref[...] += dg

        @pl.when(i == n_tiles - 1)
        def _():
            _flush_chunks(acc_ref, stage_ref, slab_ref, "inT", sems)

    row = lambda w: pl.BlockSpec((t, w), lambda i: (i, 0))
    return pl.pallas_call(
        body, name="attn_in_bwd", grid=(n_tiles,),
        in_specs=[pl.BlockSpec((N_Q_HEADS, t, 128), lambda i: (0, i, 0)), row(640), row(128), row(128), row(POOL_WIDTH),
                  row(D_MODEL), row(D_MODEL), row(D_MODEL), ANY] + W_SPECS + [_full((1, D_MODEL)), _full((1, 128)),
                                                                              _full((1, 128))],
        out_specs=[row(D_MODEL), _full((1, D_MODEL)), _full((1, 128)), _full((1, 128)), ANY],
        out_shape=[jax.ShapeDtypeStruct((s_len, D_MODEL), F32), jax.ShapeDtypeStruct((1, D_MODEL), F32),
                   jax.ShapeDtypeStruct((1, 128), F32), jax.ShapeDtypeStruct((1, 128), F32),
                   jax.ShapeDtypeStruct(slab.shape, BF16)],
        input_output_aliases={8: 4},
        scratch_shapes=[pltpu.VMEM((IN_WIDTH, D_MODEL), BF16), pltpu.VMEM((t, IN_WIDTH), BF16),
                        pltpu.VMEM((IN_WIDTH, D_MODEL), F32), pltpu.VMEM((IN_WIDTH, D_MODEL), BF16),
                        pltpu.SemaphoreType.DMA((N_CHIPS,))],
        compiler_params=_params(),
    )(dqst, zqk, dk, dv, du, x2, dh1, hn1, slab, *wts, g_attn, gq, gk)


def _dw(lefts, b, name, slab, slab_rows, row_offs):
    tk = 2048
    a0, n_a = lefts[0], len(lefts)
    assert b.shape[1] == D_MODEL
    if a0.ndim == 3:
        s_len, tm = a0.shape[1:]
        m = N_CHIPS * tm
        a_spec = pl.BlockSpec((None, tk, tm), lambda i, k: (i, k, 0))
    else:
        s_len, tm = a0.shape
        m = tm
        a_spec = pl.BlockSpec((tk, tm), lambda i, k: (k, i))
    n_steps, n_tiles = s_len // tk, m // tm
    chunk = m // N_CHIPS
    per_tile = tm // chunk

    def body(*refs):
        a_refs, b_ref = refs[:n_a], refs[n_a]
        o_ref, acc_ref, stage_ref, sems = refs[-4:]
        i, k = pl.program_id(0), pl.program_id(1)
        b_tile = b_ref[...].astype(BF16)
        for w, a_ref in enumerate(a_refs):
            _accumulate_tn(acc_ref.at[w], a_ref[...].astype(BF16), b_tile, k == 0)

        def out_copies(tile, slot):
            return [pltpu.make_async_copy(stage_ref.at[slot, w, pl.ds(jj * chunk, chunk), :],
                                          o_ref.at[tile * per_tile + jj, pl.ds(row_offs[w], chunk), :], sems.at[slot, w, jj])
                    for w in range(n_a) for jj in range(per_tile)]

        @pl.when(k == n_steps - 1)
        def _():
            slot = i % 2

            @pl.when(i >= 2)
            def _():
                for cp in out_copies(i - 2, slot):
                    cp.wait()

            stage_ref[slot] = acc_ref[...].astype(BF16)
            for cp in out_copies(i, slot):
                cp.start()

            @pl.when(i == n_tiles - 1)
            def _():
                for cp in out_copies(i, slot):
                    cp.wait()
                if n_tiles > 1:
                    for cp in out_copies(i - 1, 1 - slot):
                        cp.wait()

    in_specs = [a_spec] * n_a + [pl.BlockSpec((tk, D_MODEL), lambda i, k: (k, 0))]
    operands, aliases = [*lefts, b], {}
    if slab is not None:
        in_specs.append(ANY)
        operands.append(slab)
        aliases = {n_a + 1: 0}
    return pl.pallas_call(
        body, name=name, grid=(n_tiles, n_steps), in_specs=in_specs, out_specs=ANY,
        out_shape=jax.ShapeDtypeStruct((N_CHIPS, slab_rows, D_MODEL), BF16), input_output_aliases=aliases,
        scratch_shapes=[pltpu.VMEM((n_a, tm, D_MODEL), F32), pltpu.VMEM((2, n_a, tm, D_MODEL), BF16),
                        pltpu.SemaphoreType.DMA((2, n_a, per_tile))],
        compiler_params=_params(n_axes=2),
    )(*operands)


def _position():
    x, y, c = lax.axis_index("x"), lax.axis_index("y"), lax.axis_index("c")
    other_chips = [(1 - x, y), (x, 1 - y), (1 - x, 1 - y)]
    return x, y, c, other_chips


def _ag_weights(local_slab, row0, n_rows, name, collective_id):
    half = n_rows // 2
    quarter = half // 2
    assert quarter % 16 == 0

    def body(l_ref, g_ref, send, recv):
        x, y, c, chips = _position()
        me, (via_x, via_y, diagonal) = 2 * x + y, [2 * chip[0] + chip[1] for chip in chips]
        here, sibling, x_nbr, y_nbr = (x, y, c), (x, y, 1 - c), (1 - x, y, c), (x, 1 - y, c)
        peers = [sibling, x_nbr, y_nbr]
        barrier = pltpu.get_barrier_semaphore()
        for peer in peers:
            pl.semaphore_signal(barrier, inc=1, device_id=peer, device_id_type=MESH)
        pl.semaphore_wait(barrier, len(peers))

        def rows(core, part):
            start, size = (core * half, half) if part is None else (core * half + part * quarter, quarter)
            return pl.ds(pl.multiple_of(start, 16), size)

        def copy(k, chip_idx, where, to, src=None):
            dst = g_ref.at[chip_idx, where, :]
            return pltpu.make_async_remote_copy(src_ref=dst if src is None else src, dst_ref=dst, send_sem=send.at[k],
                                                recv_sem=recv.at[k], device_id=to, device_id_type=MESH)

        own_rows = l_ref.at[pl.ds(pl.multiple_of(row0 + c * half, 16), half), :]
        started = [copy(0, me, rows(c, None), x_nbr, src=own_rows), copy(1, me, rows(c, None), y_nbr, src=own_rows)]
        for cp in started:
            cp.start()
        after_arrival = [
            (copy(0, via_x, rows(c, None), here), [copy(4, via_x, rows(c, None), sibling), copy(3, via_x, rows(c, 1), y_nbr)]),
            (copy(1, via_y, rows(c, None), here), [copy(5, via_y, rows(c, None), sibling), copy(2, via_y, rows(c, 0), x_nbr)]),
            (copy(2, diagonal, rows(c, 0), here), [copy(6, diagonal, rows(c, 0), sibling)]),
            (copy(3, diagonal, rows(c, 1), here), [copy(7, diagonal, rows(c, 1), sibling)]),
        ]
        for arrival, onward in after_arrival:
            arrival.wait_recv()
            for cp in onward:
                cp.start()
            started += onward
        for cp in (copy(4, via_x, rows(1 - c, None), here), copy(5, via_y, rows(1 - c, None), here),
                   copy(6, diagonal, rows(1 - c, 0), here), copy(7, diagonal, rows(1 - c, 1), here)):
            cp.wait_recv()
        for cp in started:
            cp.wait_send()

    return pl.kernel(
        body, out_type=jax.ShapeDtypeStruct((N_CHIPS, n_rows, D_MODEL), BF16),
        mesh=plsc.ScalarSubcoreMesh(axis_name="sequencer", num_cores=1), name=name,
        scratch_types=[pltpu.SemaphoreType.DMA((8,)), pltpu.SemaphoreType.DMA((8,))],
        compiler_params=pltpu.CompilerParams(collective_id=collective_id),
    )(local_slab)


def _comm_call(body, peers_of, out_shape, n_sems, operand, name, collective_id):
    sems = [pltpu.SemaphoreType.DMA((n_sems,)), pltpu.SemaphoreType.DMA((n_sems,))]
    if collective_id is None:
        return pl.pallas_call(body, name=name, in_specs=[ANY], out_specs=ANY, out_shape=out_shape, scratch_shapes=sems)(operand)

    def with_handshake(in_ref, out_ref, send, recv):
        x, y, c, _ = _position()
        peers = peers_of(x, y, c)
        barrier = pltpu.get_barrier_semaphore()
        for peer in peers:
            pl.semaphore_signal(barrier, inc=1, device_id=peer, device_id_type=MESH)
        pl.semaphore_wait(barrier, len(peers))
        body(in_ref, out_ref, send, recv)

    return pl.kernel(with_handshake, out_type=out_shape, mesh=plsc.ScalarSubcoreMesh(axis_name="sequencer", num_cores=1),
                     name=name, scratch_types=sems, compiler_params=pltpu.CompilerParams(collective_id=collective_id))(operand)


def _rs_swap_halves(partial, name, collective_id=None):
    half = partial.shape[1] // 2

    def body(p_ref, r_ref, send, recv):
        x, y, c, _ = _position()
        theirs = pl.ds(pl.multiple_of((1 - c) * half, 16), half)
        cp = pltpu.make_async_remote_copy(src_ref=p_ref.at[:, theirs, :], dst_ref=r_ref, send_sem=send.at[0],
                                          recv_sem=recv.at[0], device_id=(x, y, 1 - c), device_id_type=MESH)
        cp.start()
        cp.wait()

    return _comm_call(body, lambda x, y, c: [(x, y, 1 - c)], jax.ShapeDtypeStruct((N_CHIPS, half, D_MODEL), BF16), 1,
                      partial, name, collective_id)


def _rs_add_halves(partial, other, core, name, after):
    half = other.shape[1]
    t = half // 2
    steps = half // t

    def body(core_ref, a_ref, b_ref, after_ref, o_ref):
        del after_ref
        o_ref[...] = (a_ref[...].astype(F32) + b_ref[...].astype(F32)).astype(BF16)

    return pl.pallas_call(
        body, name=name,
        grid_spec=pltpu.PrefetchScalarGridSpec(
            num_scalar_prefetch=1, grid=(N_CHIPS, steps),
            in_specs=[pl.BlockSpec((1, t, D_MODEL), lambda j, i, core_ref: (j, core_ref[0] * steps + i, 0)),
                      pl.BlockSpec((1, t, D_MODEL), lambda j, i, core_ref: (j, i, 0)), ANY],
            out_specs=pl.BlockSpec((1, t, D_MODEL), lambda j, i, core_ref: (j, i, 0))),
        out_shape=jax.ShapeDtypeStruct((N_CHIPS, half, D_MODEL), BF16),
        compiler_params=_params(n_axes=2),
    )(core, partial, other, after)


def _rs_exchange_chips(pre, name, collective_id=None):
    def body(s_ref, r_ref, send, recv):
        x, y, c, chips = _position()

        def copy(k, chunk, to):
            return pltpu.make_async_remote_copy(src_ref=s_ref.at[chunk], dst_ref=r_ref.at[k], send_sem=send.at[k],
                                                recv_sem=recv.at[k], device_id=to, device_id_type=MESH)

        sends = [copy(k, 2 * chip[0] + chip[1], (*chip, c)) for k, chip in enumerate(chips)]
        for cp in sends:
            cp.start()
        for cp in sends:
            cp.wait()

    return _comm_call(body, lambda x, y, c: [(1 - x, y, c), (x, 1 - y, c), (1 - x, 1 - y, c)],
                      jax.ShapeDtypeStruct((3, pre.shape[1], D_MODEL), BF16), 3, pre, name, collective_id)


def _rs_sum_chips(pre, received, place, name, after):
    half = pre.shape[1]
    t = half // 2 if half > 512 else half
    steps = half // t

    def body(place_ref, own_ref, r_ref, after_ref, o_ref):
        del after_ref
        acc = own_ref[0].astype(F32)
        for k in range(3):
            acc = acc + r_ref[k].astype(F32)
        o_ref[...] = acc

    return pl.pallas_call(
        body, name=name,
        grid_spec=pltpu.PrefetchScalarGridSpec(
            num_scalar_prefetch=1, grid=(steps,),
            in_specs=[pl.BlockSpec((1, t, D_MODEL), lambda i, place_ref: (place_ref[0], i, 0)),
                      pl.BlockSpec((3, t, D_MODEL), lambda i, place_ref: (0, i, 0)), ANY],
            out_specs=pl.BlockSpec((t, D_MODEL), lambda i, place_ref: (place_ref[1] * steps + i, 0))),
        out_shape=jax.ShapeDtypeStruct((2 * half, D_MODEL), F32),
        compiler_params=_params(),
    )(place, pre, received, after)


def _half_swap(g_ref, core, to, send, recv, k):
    half = g_ref.shape[0] // 2
    rows = g_ref.at[pl.ds(pl.multiple_of(core * half, 8), half), :]
    return pltpu.make_async_remote_copy(src_ref=rows, dst_ref=rows, send_sem=send.at[k], recv_sem=recv.at[k],
                                        device_id=to, device_id_type=MESH)


def _rs_finish_rows(grads, name, after):
    def body(f_ref, after_ref, g_ref, send, recv):
        del f_ref, after_ref
        x, y, c, _ = _position()
        mine = _half_swap(g_ref, c, (x, y, 1 - c), send, recv, 0)
        mine.start()
        _half_swap(g_ref, 1 - c, (x, y, c), send, recv, 0).wait_recv()
        mine.wait_send()

    return pl.pallas_call(
        body, name=name, in_specs=[ANY, ANY], out_specs=ANY, input_output_aliases={0: 0},
        out_shape=jax.ShapeDtypeStruct(grads.shape, F32),
        scratch_shapes=[pltpu.SemaphoreType.DMA((1,)), pltpu.SemaphoreType.DMA((1,))],
    )(grads, after)


def _small_gather(small, collective_id):
    def body(s_ref, t_ref, send, recv):
        x, y, c, chips = _position()
        sibling = (x, y, 1 - c)

        def slot(px, py, pc):
            return t_ref.at[4 * px + 2 * py + pc]

        def copy(k, block, to, src=None):
            return pltpu.make_async_remote_copy(src_ref=slot(*block) if src is None else src, dst_ref=slot(*block),
                                                send_sem=send.at[k], recv_sem=recv.at[k], device_id=to, device_id_type=MESH)

        own = pltpu.make_async_copy(s_ref, slot(x, y, c), send.at[7])
        own.start()
        first = [copy(0, (x, y, c), sibling, src=s_ref)]
        first += [copy(1 + k, (x, y, c), (*chip, c), src=s_ref) for k, chip in enumerate(chips)]
        for cp in first:
            cp.start()
        passed = []
        for k, chip in enumerate(chips):
            copy(1 + k, (*chip, c), (x, y, c)).wait_recv()
            fwd = copy(4 + k, (*chip, c), sibling)
            fwd.start()
            passed.append(fwd)
        copy(0, sibling, (x, y, c)).wait_recv()
        for k, chip in enumerate(chips):
            copy(4 + k, (*chip, 1 - c), (x, y, c)).wait_recv()
        for cp in first + passed:
            cp.wait_send()
        own.wait()

    peers_of = lambda x, y, c: [(x, y, 1 - c), (1 - x, y, c), (x, 1 - y, c), (1 - x, 1 - y, c)]
    return _comm_call(body, peers_of, jax.ShapeDtypeStruct((N_DEV, SMALL_ROWS, 128), F32), 8, small, "small_gather",
                      collective_id)


def _adam_update(w, g, m, v):
    m_new = ADAM_B1 * m + (1.0 - ADAM_B1) * g
    v_new = ADAM_B2 * v + (1.0 - ADAM_B2) * (g * g)
    m_hat = m_new / (1.0 - ADAM_B1 ** ADAM_STEP)
    v_hat = v_new / (1.0 - ADAM_B2 ** ADAM_STEP)
    return -ADAM_LR * (m_hat / (jnp.sqrt(v_hat) + ADAM_EPS) + ADAM_WD * w), m_new, v_new


def _adamw(w, g_rows, row_off, m, v, name):
    rows, cols = w.shape
    t = rows if rows <= 320 else (rows // 2 if rows % 256 else 256)

    def body(w_ref, g_ref, m_ref, v_ref, go_ref, d_ref, nm_ref, nv_ref):
        g = g_ref[...]
        go_ref[...] = g
        d_ref[...], nm_ref[...], nv_ref[...] = _adam_update(w_ref[...], g, m_ref[...], v_ref[...])

    blk = pl.BlockSpec((t, cols), lambda i: (i, 0))
    assert row_off % 8 == 0 and t % 8 == 0
    g_blk = pl.BlockSpec((pl.Element(t), pl.Element(cols)), lambda i: (pl.multiple_of(row_off + i * t, 8), 0))
    shape = jax.ShapeDtypeStruct((rows, cols), F32)
    return pl.pallas_call(
        body, name=name, grid=(rows // t,), in_specs=[blk, g_blk, blk, blk], out_specs=[blk] * 4, out_shape=[shape] * 4,
        compiler_params=_params(),
    )(w, g_rows, m, v)


SMALL_PARAMS = [("g_attn", (1, D_MODEL), 8), ("g_q", (1, HEAD_DIM), None), ("g_k", (1, HEAD_DIM), None),
                ("sinks", (1, N_Q_HEADS), None), ("rel_bias", (N_BUCKETS, N_Q_HEADS), None), ("w_pool", (512, 128), None),
                ("pool_scale", (1, POOL_WIDTH), 4), ("g_ffn", (1, D_MODEL), 8), ("g_ple", (1, D_MODEL), 8)]


def _adamw_small(tables, wmv):
    n_par = len(SMALL_PARAMS)

    def body(*refs):
        t_ref = refs[0]
        ins = refs[1:1 + 3 * n_par]
        loss_ref = refs[1 + 3 * n_par]
        outs = refs[2 + 3 * n_par:-1]
        tot_ref = refs[-1]
        total = t_ref[0]
        for d in range(1, N_DEV):
            total = total + t_ref[d]
        tot_ref[...] = total
        loss_ref[...] = tot_ref[pl.ds(SMALL["loss"], 1), 0:1]
        for i, (name, shape, split) in enumerate(SMALL_PARAMS):
            g_ref, d_ref, nm_ref, nv_ref = outs[4 * i:4 * i + 4]
            row = SMALL[name]
            if split:
                for k in range(split):
                    g_ref[:, 128 * k:128 * k + 128] = tot_ref[pl.ds(row + k, 1), :]
            else:
                g_ref[...] = tot_ref[pl.ds(row, shape[0]), 0:shape[1]]
            w_ref, m_ref, v_ref = ins[3 * i:3 * i + 3]
            d_ref[...], nm_ref[...], nv_ref[...] = _adam_update(w_ref[...], g_ref[...], m_ref[...], v_ref[...])

    shapes = [jax.ShapeDtypeStruct((1, 1), F32)]
    for _, shape, _ in SMALL_PARAMS:
        shapes += [jax.ShapeDtypeStruct(shape, F32)] * 4
    flat = [a for triple in wmv for a in triple]
    res = pl.pallas_call(
        body, name="adamw_small", in_specs=[VMEM_WHOLE] * (1 + 3 * n_par), out_specs=[VMEM_WHOLE] * len(shapes),
        out_shape=shapes, scratch_shapes=[pltpu.VMEM((SMALL_ROWS, 128), F32)],
    )(tables, *flat)
    return res[0], [res[1 + 4 * i:5 + 4 * i] for i in range(n_par)]


def _pack_ple_proj(shard):
    return shard.reshape(4, 64, 256).transpose(1, 0, 2).reshape(64, D_MODEL)


class _Reduction:
    def __init__(self, tag, place, ids=(None, None)):
        self.tag, self.place, self.ids = tag, place, ids

    def start(self, partial):
        self.partial = partial
        self.other = _rs_swap_halves(partial, "rs_swap_" + self.tag, self.ids[0])
        return partial

    def middle(self, after):
        self.pre = _rs_add_halves(self.partial, self.other, self.place[1:], "rs_add_" + self.tag, after)
        self.received = _rs_exchange_chips(self.pre, "rs_exchange_" + self.tag, self.ids[1])
        return self.pre

    def finish(self, after):
        return _rs_sum_chips(self.pre, self.received, self.place, "rs_sum_" + self.tag, after)


def _local_grads(x2, p2, tgt, wts, g_attn_norm, g_q, g_k, attn_sinks, rel_bias, w_pool, pool_scale, g_ffn_norm, g_ple_norm,
                 reduce_a):
    w_early, w_late = wts
    w_in = w_out = w_early
    bucket = jnp.asarray(_bucket_table())
    gq = jnp.tile(g_q, (1, 2))
    gk = jnp.tile(g_k, (1, 2))
    wpool = w_pool[0].astype(BF16)
    sinks = attn_sinks[0]
    bias_st = _bias_build(rel_bias.T, bucket)

    hn1, zqk, u, kn, vb, qst = _attn_in(x2, g_attn_norm, gq, gk, w_in)
    ost = _attn_fwd(qst, kn, vb, bias_st, sinks)
    pooled, mix, h1, hn2 = _mix_out(u, ost, x2, w_out, wpool, pool_scale, g_ffn_norm)
    loss_v, dgate, dup, act, dh2, hn3, dgl, dw_plp, dh1, dg_ffn, dg_ple = _ffn_ple(hn2, h1, p2, tgt, w_late, g_ffn_norm,
                                                                                      g_ple_norm)

    late0, late_rows = GATHER_PARTS[1][0], SLAB_ROWS - GATHER_PARTS[1][0]
    partial_a = None
    for names, lefts, right in ((("gateT", "upT"), [dgate, dup], hn2), (("down",), [act], dh2), (("plg",), [hn3], dgl)):
        partial_a = _dw(lefts, right, "dw_" + names[0], partial_a, late_rows, [SLAB[name][0] - late0 for name in names])
    dw_plp = dw_plp.reshape(4, 64, N_CHIPS, 256).transpose(2, 1, 0, 3).reshape(N_CHIPS, 64, D_MODEL)
    partial_a = reduce_a.start(lax.dynamic_update_slice(partial_a, dw_plp, (0, SLAB["plp"][0] - late0, 0)))
    dost, du, dw_pool, dscale, partial_b = _mix_out_bwd(dh1, w_out, pooled, wpool, pool_scale, mix, partial_a)
    pre_a = reduce_a.middle(du)
    dqst, dk, dv, dbias, dsink_rows = _attn_bwd(qst, kn, vb, dost, bias_st, sinks, pre_a)
    dx, dg_attn, dgq, dgk, partial_b = _attn_in_bwd(dqst, zqk, dk, dv, du, x2, dh1, hn1, partial_b, w_in, g_attn_norm, gq, gk)

    small = _small_pack(dg_attn, dg_ffn, dg_ple, dscale, dgq, dgk, dbias, dsink_rows, bucket, loss_v, dw_pool)
    return dx, partial_b, small


def kernel(x, p, w_in, w_out, g_attn_norm, g_q, g_k, attn_sinks, rel_bias, w_pool, pool_scale, g_ffn_norm, w_gate, w_up, w_down, g_ple_norm, w_ple_gate, w_ple_proj, loss_target, m_w_in, m_w_out, m_g_attn_norm, m_g_q, m_g_k, m_attn_sinks, m_rel_bias, m_w_pool, m_pool_scale, m_g_ffn_norm, m_w_gate, m_w_up, m_w_down, m_g_ple_norm, m_w_ple_gate, m_w_ple_proj, v_w_in, v_w_out, v_g_attn_norm, v_g_q, v_g_k, v_attn_sinks, v_rel_bias, v_w_pool, v_pool_scale, v_g_ffn_norm, v_w_gate, v_w_up, v_w_down, v_g_ple_norm, v_w_ple_gate, v_w_ple_proj):
    core = lax.axis_index("c").astype(jnp.int32).reshape(1)
    me = (2 * lax.axis_index("x") + lax.axis_index("y")).astype(jnp.int32).reshape(1)

    local_parts = [jnp.concatenate(pieces, axis=0).astype(BF16) for pieces in (
        [w_in[0].T, w_out[0]], [w_gate[0].T, w_up[0].T, w_down[0], w_ple_gate[0], _pack_ple_proj(w_ple_proj[0])])]
    wts = [(_ag_weights(local, 0, local.shape[0], name, collective_id), local, me)
           for local, name, collective_id in zip(local_parts, ("ag_early", "ag_late"), (1, 2))]

    place = jnp.concatenate([me, core])
    reduce_a = _Reduction("a", place, ids=(3, 4))
    dx, partial_b, small = _local_grads(x[0], p[0, 0], loss_target[0], wts, g_attn_norm, g_q, g_k, attn_sinks, rel_bias,
                                        w_pool, pool_scale, g_ffn_norm, g_ple_norm, reduce_a)
    reduce_b = _Reduction("b", place, ids=(6, 7))
    reduce_b.start(partial_b)
    small_all = _small_gather(small, 8)
    summed_a = reduce_a.finish(small)
    pre_b = reduce_b.middle(summed_a)
    grads_a = _rs_finish_rows(summed_a, "rs_finish_a", pre_b)

    late0 = GATHER_PARTS[1][0]

    def rows(name):
        return grads_a, SLAB[name][0] - late0

    plp_rows = grads_a[SLAB["plp"][0] - late0:]
    big = {
        "w_gate": (w_gate, m_w_gate, v_w_gate, rows("gateT"), True),
        "w_up": (w_up, m_w_up, v_w_up, rows("upT"), True),
        "w_down": (w_down, m_w_down, v_w_down, rows("down"), False),
        "w_ple_gate": (w_ple_gate, m_w_ple_gate, v_w_ple_gate, rows("plg"), False),
        "w_ple_proj": (w_ple_proj, m_w_ple_proj, v_w_ple_proj,
                       (plp_rows.reshape(64, 4, 256).transpose(1, 0, 2).reshape(PLE_DIM, PLE_DIM), 0), False),
        "w_out": (w_out, m_w_out, v_w_out, None, False),
        "w_in": (w_in, m_w_in, v_w_in, None, True),
    }
    small_params = {
        "g_attn_norm": (g_attn_norm, m_g_attn_norm, v_g_attn_norm), "g_q": (g_q, m_g_q, v_g_q), "g_k": (g_k, m_g_k, v_g_k),
        "attn_sinks": (attn_sinks, m_attn_sinks, v_attn_sinks), "rel_bias": (rel_bias, m_rel_bias, v_rel_bias),
        "w_pool": tuple(a.reshape(512, 128) for a in (w_pool, m_w_pool, v_w_pool)),
        "pool_scale": (pool_scale, m_pool_scale, v_pool_scale), "g_ffn_norm": (g_ffn_norm, m_g_ffn_norm, v_g_ffn_norm),
        "g_ple_norm": (g_ple_norm, m_g_ple_norm, v_g_ple_norm),
    }

    grads, deltas, new_ms, new_vs = {}, {}, {}, {}
    out = grads_b = None
    for name, (w, m, v, g_src, transposed) in big.items():
        if g_src is None:
            if grads_b is None:
                grads_b = _rs_finish_rows(reduce_b.finish(out[-1]), "rs_finish_b", out[-1])
            g_src = (grads_b, SLAB["out" if name == "w_out" else "inT"][0])
        view = (lambda a: a.T) if transposed else (lambda a: a)
        out = _adamw(view(w[0]), *g_src, view(m[0]), view(v[0]), "adamw_" + name)
        grads[name], deltas[name], new_ms[name], new_vs[name] = (view(a)[None] for a in out)

    loss, small_out = _adamw_small(small_all, list(small_params.values()))
    for name, (g2, d, nm, nv) in zip(small_params, small_out):
        shape = w_pool.shape if name == "w_pool" else g2.shape
        grads[name], deltas[name], new_ms[name], new_vs[name] = (a.reshape(shape) for a in (g2, d, nm, nv))

    order = ["w_in", "w_out", "g_attn_norm", "g_q", "g_k", "attn_sinks", "rel_bias", "w_pool", "pool_scale", "g_ffn_norm",
             "w_gate", "w_up", "w_down", "g_ple_norm", "w_ple_gate", "w_ple_proj"]
    return (loss.reshape(()), dx[None], *[grads[n] for n in order], *[deltas[n] for n in order],
            *[new_ms[n] for n in order], *[new_vs[n] for n in order])
```

```python
import numpy as np
import jax
import jax.numpy as jnp
from jax import lax
from jax.experimental import pallas as pl
from jax.experimental.pallas import tpu as pltpu
from jax.experimental.pallas import tpu_sc as plsc

F32 = jnp.float32
BF16 = jnp.bfloat16
MESH = pl.DeviceIdType.MESH

D_MODEL = 1024
HEAD_DIM = 64
N_Q_HEADS = 8
ATTN_WIDTH = 512
POOL_WIDTH = 512
IN_WIDTH = 1280
D_FF = 2816
PLE_DIM = 256
FF_CHUNK = 704
BLOCK = 128
N_BUCKETS = 32
MAX_DISTANCE = 128
EPS = 1e-6
NEG = -1e30
N_CHIPS = 4
N_DEV = 8

ADAM_LR = 0.001
ADAM_B1 = 0.9
ADAM_B2 = 0.999
ADAM_EPS = 1e-08
ADAM_WD = 0.01
ADAM_STEP = 10

SLAB = {"inT": (0, 320), "out": (320, 256), "gateT": (576, 704), "upT": (1280, 704), "down": (1984, 704),
        "plg": (2688, 256), "plp": (2944, 64)}
SLAB_ROWS = 3008
GATHER_PARTS = ((0, 576), (576, SLAB_ROWS))
POOL_HALO = 24

SMALL = {"g_attn": 0, "g_ffn": 8, "g_ple": 16, "pool_scale": 24, "g_q": 28, "g_k": 29, "sinks": 30, "loss": 31,
         "rel_bias": 32, "w_pool": 64}
SMALL_ROWS = 576

VMEM_LIMIT_BIG = 60 * 1024 * 1024
VMEM_LIMIT = 48 * 1024 * 1024


def _params(vmem=VMEM_LIMIT, n_axes=1):
    return pltpu.CompilerParams(dimension_semantics=("arbitrary",) * n_axes, vmem_limit_bytes=vmem)


def _dot(a, b, ca, cb):
    return lax.dot_general(a, b, (((ca,), (cb,)), ((), ())), preferred_element_type=F32)


def _full(shape):
    return pl.BlockSpec(shape, lambda i: (0,) * len(shape))


ANY = pl.BlockSpec(memory_space=pl.ANY)
VMEM_WHOLE = pl.BlockSpec(memory_space=pltpu.VMEM)


W_SPECS = [ANY, ANY, pl.BlockSpec(memory_space=pltpu.SMEM)]


def _load_rows(w_refs, name, dst_ref, sems):
    slab_ref, local_ref, me_ref = w_refs
    off, rows = SLAB[name]
    slab_off = off - max(start for start, _ in GATHER_PARTS if start <= off)
    me = me_ref[0]
    for phase in ("start", "wait"):
        for j in range(N_CHIPS):
            dst = dst_ref.at[pl.ds(j * rows, rows), :]
            theirs = pltpu.make_async_copy(slab_ref.at[j, pl.ds(slab_off, rows), :], dst, sems.at[j])
            own = pltpu.make_async_copy(local_ref.at[pl.ds(slab_off, rows), :], dst, sems.at[j])

            @pl.when(me == j)
            def _():
                getattr(own, phase)()

            @pl.when(me != j)
            def _():
                getattr(theirs, phase)()


def _rms_fwd(x, g):
    r = lax.rsqrt(jnp.mean(x * x, axis=-1, keepdims=True) + EPS)
    return x * r * g


def _rms_bwd(x, g, dy):
    r = lax.rsqrt(jnp.mean(x * x, axis=-1, keepdims=True) + EPS)
    xn = x * r
    dyg = dy * g
    dx = r * (dyg - xn * jnp.mean(dyg * xn, axis=-1, keepdims=True))
    return dx, jnp.sum(dy * xn, axis=0, keepdims=True)


def _half_sum(v, lo):
    s_lo = jnp.sum(jnp.where(lo, v, 0.0), axis=-1, keepdims=True)
    s_hi = jnp.sum(jnp.where(lo, 0.0, v), axis=-1, keepdims=True)
    return jnp.where(lo, s_lo, s_hi)


def _half_sum_mxu(v):
    upper = lax.broadcasted_iota(jnp.int32, (128, 128), 0) < 64
    left = lax.broadcasted_iota(jnp.int32, (128, 128), 1) < 64
    ones = jnp.where(upper == left, 1.0, 0.0).astype(BF16)
    high = v.astype(BF16)
    low = (v - high.astype(F32)).astype(BF16)
    return _dot(high, ones, 1, 0) + _dot(low, ones, 1, 0)


def _pair_norm(zp, g, lo):
    r = lax.rsqrt(_half_sum(zp * zp, lo) * (1.0 / HEAD_DIM) + EPS)
    return zp * r * g


def _pair_norm_bwd(zp, g, dy):
    r = lax.rsqrt(_half_sum_mxu(zp * zp) * (1.0 / HEAD_DIM) + EPS)
    xn = zp * r
    dyg = dy * g
    dx = r * (dyg - xn * (_half_sum_mxu(dyg * xn) * (1.0 / HEAD_DIM)))
    return dx, jnp.sum(dy * xn, axis=0, keepdims=True)


def _to_stacked(pair, group, lo):
    rolled = pltpu.roll(pair, 64, axis=1)
    if group == 0:
        return jnp.where(lo, pair, 0.0), jnp.where(lo, rolled, 0.0)
    return jnp.where(lo, 0.0, rolled), jnp.where(lo, 0.0, pair)


def _from_stacked(even, odd, group, lo):
    if group == 0:
        return jnp.where(lo, even, pltpu.roll(odd, 64, axis=1))
    return jnp.where(lo, pltpu.roll(even, 64, axis=1), odd)


def _sigmoid(v):
    return 1.0 / (1.0 + jnp.exp(-v))


def _pool_counts(tile, n_rows):
    t1 = tile * n_rows + lax.broadcasted_iota(jnp.int32, (n_rows, POOL_WIDTH), 0) + 1
    lane = lax.broadcasted_iota(jnp.int32, (n_rows, POOL_WIDTH), 1)
    win = jnp.where(lane < 128, 2, jnp.where(lane < 256, 4, jnp.where(lane < 384, 8, 16)))
    return jnp.minimum(t1, win).astype(F32)


def _attn_in(x2, g_attn, gq, gk, wts):
    s_len = x2.shape[0]
    t = 512

    def body(x_ref, g_ref, gq_ref, gk_ref, sl_ref, lo_ref, me_ref, hn_ref, zqk_ref, u_ref, kn_ref, v_ref, qst_ref, w_ref, sems):
        @pl.when(pl.program_id(0) == 0)
        def _():
            _load_rows((sl_ref, lo_ref, me_ref), "inT", w_ref, sems)

        hn = _rms_fwd(x_ref[...], g_ref[...]).astype(BF16)
        hn_ref[...] = hn
        z = _dot(hn, w_ref[...], 1, 1)
        zqk_ref[...] = z[:, :640]
        u_ref[...] = z[:, 768:]
        v_ref[...] = z[:, 640:768].astype(BF16)
        lo = lax.broadcasted_iota(jnp.int32, (t, 128), 1) < 64
        kn_ref[...] = _pair_norm(z[:, 512:640], gk_ref[...], lo).astype(BF16)
        for p in range(4):
            qn = _pair_norm(z[:, 128 * p:128 * p + 128], gq_ref[...], lo)
            even, odd = _to_stacked(qn, p // 2, lo)
            qst_ref[2 * p] = even.astype(BF16)
            qst_ref[2 * p + 1] = odd.astype(BF16)

    row = lambda w: pl.BlockSpec((t, w), lambda i: (i, 0))
    return pl.pallas_call(
        body, name="attn_in", grid=(s_len // t,),
        in_specs=[row(D_MODEL), _full((1, D_MODEL)), _full((1, 128)), _full((1, 128))] + W_SPECS,
        out_specs=[row(D_MODEL), row(640), row(POOL_WIDTH), row(128), row(128),
                   pl.BlockSpec((N_Q_HEADS, t, 128), lambda i: (0, i, 0))],
        out_shape=[jax.ShapeDtypeStruct((s_len, D_MODEL), BF16), jax.ShapeDtypeStruct((s_len, 640), F32),
                   jax.ShapeDtypeStruct((s_len, POOL_WIDTH), F32), jax.ShapeDtypeStruct((s_len, 128), BF16),
                   jax.ShapeDtypeStruct((s_len, 128), BF16), jax.ShapeDtypeStruct((N_Q_HEADS, s_len, 128), BF16)],
        scratch_shapes=[pltpu.VMEM((IN_WIDTH, D_MODEL), BF16), pltpu.SemaphoreType.DMA((N_CHIPS,))],
        compiler_params=_params(),
    )(x2, g_attn, gq, gk, *wts)


def _bucket_table():
    i_idx = np.arange(BLOCK)[:, None]
    j_idx = np.arange(2 * BLOCK)[None, :]
    d = BLOCK + i_idx - j_idx
    n = np.maximum(d, 0)
    max_exact = N_BUCKETS // 2
    nf = np.maximum(n, 1).astype(np.float64)
    large = max_exact + (np.log(nf / max_exact) / np.log(MAX_DISTANCE / max_exact) * (N_BUCKETS - max_exact)).astype(np.int64)
    large = np.minimum(large, N_BUCKETS - 1)
    bucket = np.where(n < max_exact, n, large)
    return np.where((d >= 0) & (d < BLOCK), bucket, -1).astype(np.int32)


def _bias_build(rel_bias_t, bucket):
    def body(rb_ref, bucket_ref, out_ref):
        bk = bucket_ref[...]
        for h in range(N_Q_HEADS):
            acc = jnp.full((BLOCK, 2 * BLOCK), NEG, F32)
            for b in range(N_BUCKETS):
                acc = jnp.where(bk == b, rb_ref[h, b], acc)
            out_ref[0, pl.ds(h * BLOCK, BLOCK), :] = acc
            out_ref[1, pl.ds(h * BLOCK, BLOCK), :] = acc
            out_ref[1, pl.ds(h * BLOCK, BLOCK), 0:BLOCK] = jnp.full((BLOCK, BLOCK), NEG, F32)

    return pl.pallas_call(
        body, name="bias_build",
        in_specs=[pl.BlockSpec(memory_space=pltpu.SMEM), VMEM_WHOLE], out_specs=VMEM_WHOLE,
        out_shape=jax.ShapeDtypeStruct((2, N_Q_HEADS * BLOCK, 2 * BLOCK), F32),
    )(rel_bias_t, bucket)


def _head_softmax(s_ref, bias_ref, sink_ref, h):
    rows = pl.ds(pl.multiple_of(h * BLOCK, BLOCK), BLOCK)
    s = s_ref[rows, :] * (HEAD_DIM ** -0.5) + bias_ref[rows, :]
    sink = sink_ref[h]
    m = jnp.maximum(jnp.max(s, axis=-1, keepdims=True), sink)
    p = jnp.exp(s - m)
    e_sink = jnp.exp(sink - m)
    inv = 1.0 / (jnp.sum(p, axis=-1, keepdims=True) + e_sink)
    return rows, p * inv, e_sink * inv


ATTN_STEP_BLOCKS = 4
BAND = (N_Q_HEADS * BLOCK, 2 * BLOCK)


def _attn_specs():
    nb = ATTN_STEP_BLOCKS
    stacked = pl.BlockSpec((N_Q_HEADS, nb * BLOCK, 128), lambda i: (0, i, 0))
    kv = [pl.BlockSpec((BLOCK, 128), lambda i: (jnp.maximum(nb * i - 1, 0), 0)), pl.BlockSpec((nb * BLOCK, 128), lambda i: (i, 0))]
    consts = [_full((2,) + BAND), pl.BlockSpec(memory_space=pltpu.SMEM)]
    return stacked, kv, consts


def _step_blocks(i, kp_ref, kc_ref, vp_ref, vc_ref, bias_ref):
    blocks = []
    for b in range(ATTN_STEP_BLOCKS):
        if b == 0:
            k2 = jnp.concatenate([kp_ref[...], kc_ref[pl.ds(0, BLOCK), :]], axis=0)
            v2 = jnp.concatenate([vp_ref[...], vc_ref[pl.ds(0, BLOCK), :]], axis=0)
            bias = bias_ref.at[jnp.where(i == 0, 1, 0)]
        else:
            k2, v2, bias = kc_ref[pl.ds((b - 1) * BLOCK, 2 * BLOCK), :], vc_ref[pl.ds((b - 1) * BLOCK, 2 * BLOCK), :], bias_ref.at[0]
        blocks.append((pl.ds(b * BLOCK, BLOCK), k2, v2, bias))
    return blocks


def _head_lane_mask():
    rows = lax.broadcasted_iota(jnp.int32, (N_Q_HEADS * BLOCK, 128), 0)
    lanes = lax.broadcasted_iota(jnp.int32, (N_Q_HEADS * BLOCK, 128), 1)
    return (rows < 4 * BLOCK) == (lanes < 64)


def _attn_fwd(qst, kn, vb, bias_st, sinks):
    s_len = kn.shape[0]

    def body(q_ref, kp_ref, kc_ref, vp_ref, vc_ref, bias_ref, sink_ref, o_ref, s_ref, p_ref):
        for b, (rows, k2, v2, bias) in enumerate(_step_blocks(pl.program_id(0), kp_ref, kc_ref, vp_ref, vc_ref, bias_ref)):
            s_b, p_b = s_ref.at[b], p_ref.at[b]
            s_b[...] = _dot(q_ref[:, rows, :].reshape(N_Q_HEADS * BLOCK, 128), k2, 1, 1)

            def head(h, carry):
                head_rows, probs, _ = _head_softmax(s_b, bias, sink_ref, h)
                p_b[head_rows, :] = probs.astype(BF16)
                return carry

            lax.fori_loop(0, N_Q_HEADS, head, 0, unroll=True)
            o = jnp.where(_head_lane_mask(), _dot(p_b[...], v2, 1, 0), 0.0)
            o_ref[:, rows, :] = o.astype(BF16).reshape(N_Q_HEADS, BLOCK, 128)

    stacked, kv, consts = _attn_specs()
    return pl.pallas_call(
        body, name="attn_fwd", grid=(s_len // (ATTN_STEP_BLOCKS * BLOCK),),
        in_specs=[stacked] + kv + kv + consts, out_specs=stacked,
        out_shape=jax.ShapeDtypeStruct((N_Q_HEADS, s_len, 128), BF16),
        scratch_shapes=[pltpu.VMEM((ATTN_STEP_BLOCKS,) + BAND, F32), pltpu.VMEM((ATTN_STEP_BLOCKS,) + BAND, BF16)],
        compiler_params=_params(),
    )(qst, kn, kn, vb, vb, bias_st, sinks)


def _mix_out(u, ost, x2, wts, wpool, pool_scale, g_ffn):
    s_len = x2.shape[0]
    t = 512
    n = t + 16

    def body(u_ref, o_ref, x_ref, sl_ref, lo_ref, me_ref, wp_ref, sc_ref, g_ref, pooled_ref, mix_ref, h1_ref, hn_ref,
             w_ref, ext_ref, st_ref, sems):
        i = pl.program_id(0)

        @pl.when(i == 0)
        def _():
            _load_rows((sl_ref, lo_ref, me_ref), "out", w_ref, sems)
            ext_ref[...] = jnp.zeros_like(ext_ref)
            st_ref[...] = jnp.zeros_like(st_ref)

        u_tile = u_ref[...]
        ext_ref[pl.ds(POOL_HALO, t), :] = u_tile
        st_ref[pl.ds(8, n), :] = ext_ref[pl.ds(8, n), :] + ext_ref[pl.ds(7, n), :]
        st_ref[pl.ds(8, n), 128:] = st_ref[pl.ds(8, n), 128:] + st_ref[pl.ds(6, n), 128:]
        st_ref[pl.ds(8, n), 256:] = st_ref[pl.ds(8, n), 256:] + st_ref[pl.ds(4, n), 256:]
        st_ref[pl.ds(8, n), 384:] = st_ref[pl.ds(8, n), 384:] + st_ref[pl.ds(0, n), 384:]
        ext_ref[pl.ds(0, POOL_HALO), :] = ext_ref[pl.ds(t, POOL_HALO), :]
        pooled = (st_ref[pl.ds(POOL_HALO, t), :] / _pool_counts(i, t) - u_tile).astype(BF16)
        pooled_ref[...] = pooled
        for g in range(4):
            cols = slice(128 * g, 128 * g + 128)
            y = _dot(pooled[:, cols], wp_ref[g], 1, 0) * sc_ref[:, cols]
            mix_ref[:, ATTN_WIDTH + 128 * g:ATTN_WIDTH + 128 * g + 128] = y.astype(BF16)
        lo = lax.broadcasted_iota(jnp.int32, (t, 128), 1) < 64
        for p in range(4):
            a = _from_stacked(o_ref[2 * p].astype(F32), o_ref[2 * p + 1].astype(F32), p // 2, lo)
            mix_ref[:, 128 * p:128 * p + 128] = a.astype(BF16)
        h1 = x_ref[...] + _dot(mix_ref[...], w_ref[...], 1, 0)
        h1_ref[...] = h1
        hn_ref[...] = _rms_fwd(h1, g_ref[...]).astype(BF16)

    row = lambda w: pl.BlockSpec((t, w), lambda i: (i, 0))
    return pl.pallas_call(
        body, name="mix_out", grid=(s_len // t,),
        in_specs=[row(POOL_WIDTH), pl.BlockSpec((N_Q_HEADS, t, 128), lambda i: (0, i, 0)), row(D_MODEL)] + W_SPECS
        + [_full((4, 128, 128)), _full((1, POOL_WIDTH)), _full((1, D_MODEL))],
        out_specs=[row(POOL_WIDTH), row(D_MODEL), row(D_MODEL), row(D_MODEL)],
        out_shape=[jax.ShapeDtypeStruct((s_len, POOL_WIDTH), BF16), jax.ShapeDtypeStruct((s_len, D_MODEL), BF16),
                   jax.ShapeDtypeStruct((s_len, D_MODEL), F32), jax.ShapeDtypeStruct((s_len, D_MODEL), BF16)],
        scratch_shapes=[pltpu.VMEM((D_MODEL, D_MODEL), BF16), pltpu.VMEM((t + POOL_HALO, POOL_WIDTH), F32),
                        pltpu.VMEM((t + POOL_HALO, POOL_WIDTH), F32), pltpu.SemaphoreType.DMA((N_CHIPS,))],
        compiler_params=_params(),
    )(u, ost, x2, *wts, wpool, pool_scale, g_ffn)


def _ffn_ple(hn2, h1, p2, tgt, wts, g_ffn, g_ple):
    s_len = h1.shape[0]
    t = 256
    n_tiles = s_len // t

    def body(hn_ref, h1_ref, p_ref, tgt_ref, sl_ref, lo_ref, me_ref, gf_ref, gp_ref,
             loss_ref, dgate_ref, dup_ref, act_ref, dh2b_ref, hn3_ref, dgl_ref, dwp_ref, dh1_ref, dgf_ref, dgp_ref,
             wg_ref, wu_ref, wd_ref, wl_ref, wp_ref, packed_ref, gate_s, up_s, loss_acc, dwp_acc, sems):
        i = pl.program_id(0)

        @pl.when(i == 0)
        def _():
            w_refs = (sl_ref, lo_ref, me_ref)
            _load_rows(w_refs, "gateT", wg_ref, sems)
            _load_rows(w_refs, "upT", wu_ref, sems)
            _load_rows(w_refs, "down", wd_ref, sems)
            _load_rows(w_refs, "plg", wl_ref, sems)
            _load_rows(w_refs, "plp", packed_ref, sems)
            for j in range(N_CHIPS):
                for q in range(4):
                    wp_ref[pl.ds(64 * q, 64), 256 * j:256 * j + 256] = packed_ref[pl.ds(64 * j, 64), 256 * q:256 * q + 256]
            loss_acc[...] = jnp.zeros_like(loss_acc)
            dgf_ref[...] = jnp.zeros_like(dgf_ref)
            dgp_ref[...] = jnp.zeros_like(dgp_ref)

        hn = hn_ref[...]
        h1v = h1_ref[...]
        h2 = h1v
        for ch in range(N_CHIPS):
            rows = pl.ds(ch * FF_CHUNK, FF_CHUNK)
            gate = _dot(hn, wg_ref[rows, :], 1, 1)
            up = _dot(hn, wu_ref[rows, :], 1, 1)
            gate_s[ch] = gate
            up_s[ch] = up
            act = (gate * _sigmoid(gate) * up).astype(BF16)
            act_ref[ch] = act
            h2 = h2 + _dot(act, wd_ref[rows, :], 1, 0)
        gp = gp_ref[...]
        hn3 = _rms_fwd(h2, gp).astype(BF16)
        hn3_ref[...] = hn3
        gate2 = _sigmoid(_dot(hn3, wl_ref[...], 1, 0))
        p_tile = p_ref[...].astype(BF16)
        pp = _dot(p_tile, wp_ref[...], 1, 0)
        err = h2 + gate2 * pp - tgt_ref[...]
        loss_acc[...] += jnp.sum(err * err, axis=0, keepdims=True)
        dy = err * (1.0 / D_MODEL)
        _accumulate_tn(dwp_acc, p_tile, (dy * gate2).astype(BF16), i == 0)
        dgl = (dy * pp * gate2 * (1.0 - gate2)).astype(BF16)
        dgl_ref[...] = dgl
        dx3, dg3 = _rms_bwd(h2, gp, _dot(dgl, wl_ref[...], 1, 1))
        dh2 = dy + dx3
        dgp_ref[...] += dg3
        dh2b = dh2.astype(BF16)
        dh2b_ref[...] = dh2b
        dhn = jnp.zeros((t, D_MODEL), F32)
        for ch in range(N_CHIPS):
            rows = pl.ds(ch * FF_CHUNK, FF_CHUNK)
            dact = _dot(dh2b, wd_ref[rows, :], 1, 1)
            gate_v = gate_s[ch]
            up_v = up_s[ch]
            sg = _sigmoid(gate_v)
            dup = (dact * (gate_v * sg)).astype(BF16)
            dgate = (dact * up_v * (sg * (1.0 + gate_v * (1.0 - sg)))).astype(BF16)
            dup_ref[ch] = dup
            dgate_ref[ch] = dgate
            dhn = dhn + _dot(dgate, wg_ref[rows, :], 1, 0) + _dot(dup, wu_ref[rows, :], 1, 0)
        dx, dg = _rms_bwd(h1v, gf_ref[...], dhn)
        dh1_ref[...] = dh2 + dx
        dgf_ref[...] += dg

        @pl.when(i == n_tiles - 1)
        def _():
            total = jnp.sum(loss_acc[...], axis=-1, keepdims=True) * (0.5 / D_MODEL)
            loss_ref[...] = jnp.broadcast_to(total, loss_ref.shape)
            dwp_ref[...] = dwp_acc[...].astype(BF16)

    row = lambda w: pl.BlockSpec((t, w), lambda i: (i, 0))
    chunked = pl.BlockSpec((N_CHIPS, t, FF_CHUNK), lambda i: (0, i, 0))
    vec = _full((1, D_MODEL))
    act_shape = jax.ShapeDtypeStruct((N_CHIPS, s_len, FF_CHUNK), BF16)
    tok = lambda dtype: jax.ShapeDtypeStruct((s_len, D_MODEL), dtype)
    return pl.pallas_call(
        body, name="ffn_ple", grid=(n_tiles,),
        in_specs=[row(D_MODEL), row(D_MODEL), row(PLE_DIM), row(D_MODEL)] + W_SPECS + [vec, vec],
        out_specs=[_full((1, 128)), chunked, chunked, chunked] + [row(D_MODEL)] * 3 + [_full((PLE_DIM, D_MODEL)), row(D_MODEL),
                                                                                       vec, vec],
        out_shape=[jax.ShapeDtypeStruct((1, 128), F32), act_shape, act_shape, act_shape, tok(BF16), tok(BF16), tok(BF16),
                   jax.ShapeDtypeStruct((PLE_DIM, D_MODEL), BF16), tok(F32), jax.ShapeDtypeStruct((1, D_MODEL), F32),
                   jax.ShapeDtypeStruct((1, D_MODEL), F32)],
        scratch_shapes=[pltpu.VMEM((D_FF, D_MODEL), BF16)] * 3
        + [pltpu.VMEM((D_MODEL, D_MODEL), BF16), pltpu.VMEM((PLE_DIM, D_MODEL), BF16), pltpu.VMEM((PLE_DIM, D_MODEL), BF16),
           pltpu.VMEM((N_CHIPS, t, FF_CHUNK), F32), pltpu.VMEM((N_CHIPS, t, FF_CHUNK), F32), pltpu.VMEM((1, D_MODEL), F32),
           pltpu.VMEM((PLE_DIM, D_MODEL), F32), pltpu.SemaphoreType.DMA((N_CHIPS,))],
        compiler_params=_params(VMEM_LIMIT_BIG),
    )(hn2, h1, p2, tgt, *wts, g_ffn, g_ple)


def _accumulate_tn(acc_ref, a, b, first):
    @pl.when(first)
    def _():
        acc_ref[...] = _dot(a, b, 0, 0)

    @pl.when(jnp.logical_not(first))
    def _():
        acc_ref[...] += _dot(a, b, 0, 0)


def _flush_chunks(acc_ref, stage_ref, slab_ref, name, sems):
    stage_ref[...] = acc_ref[...].astype(BF16)
    off, rows = SLAB[name]
    copies = [pltpu.make_async_copy(stage_ref.at[pl.ds(j * rows, rows), :], slab_ref.at[j, pl.ds(off, rows), :], sems.at[j])
              for j in range(N_CHIPS)]
    for cp in copies:
        cp.start()
    for cp in copies:
        cp.wait()


def _mix_out_bwd(dh1, wts, pooled, wpool, pool_scale, mix, after):
    s_len = dh1.shape[0]
    t = 512
    n = t + 16
    n_tiles = s_len // t
    early_rows = GATHER_PARTS[0][1]

    def body(dh1_ref, sl_ref, lo_ref, me_ref, pooled_ref, wp_ref, sc_ref, mix_ref, after_ref, dost_ref, du_ref, dwp_ref,
             dsc_ref, slab_ref, w_ref, ext_ref, st_ref, acc_ref, stage_ref, sems):
        del after_ref
        i = pl.program_id(0)

        @pl.when(i == 0)
        def _():
            _load_rows((sl_ref, lo_ref, me_ref), "out", w_ref, sems)
            ext_ref[...] = jnp.zeros_like(ext_ref)
            st_ref[...] = jnp.zeros_like(st_ref)
            dsc_ref[...] = jnp.zeros_like(dsc_ref)
            dwp_ref[...] = jnp.zeros_like(dwp_ref)

        dh1b = dh1_ref[...].astype(BF16)
        _accumulate_tn(acc_ref, mix_ref[...], dh1b, i == 0)

        @pl.when(i == n_tiles - 1)
        def _():
            _flush_chunks(acc_ref, stage_ref, slab_ref, "out", sems)

        dmix = _dot(dh1b, w_ref[...], 1, 1)
        lo = lax.broadcasted_iota(jnp.int32, (t, 128), 1) < 64
        for p in range(4):
            even, odd = _to_stacked(dmix[:, 128 * p:128 * p + 128], p // 2, lo)
            dost_ref[2 * p] = even.astype(BF16)
            dost_ref[2 * p + 1] = odd.astype(BF16)
        pooled_v = pooled_ref[...]
        counts = _pool_counts(n_tiles - 1 - i, t)
        for g in range(4):
            cols = slice(128 * g, 128 * g + 128)
            dm = dmix[:, ATTN_WIDTH + 128 * g:ATTN_WIDTH + 128 * g + 128]
            ypre = _dot(pooled_v[:, cols], wp_ref[g], 1, 0)
            dsc_ref[:, cols] += jnp.sum(ypre * dm, axis=0, keepdims=True)
            dyp = (dm * sc_ref[:, cols]).astype(BF16)
            dwp_ref[g] += _dot(pooled_v[:, cols], dyp, 0, 0)
            dpooled = _dot(dyp, wp_ref[g], 1, 1)
            du_ref[:, cols] = -dpooled
            ext_ref[pl.ds(0, t), cols] = dpooled / counts[:, cols]
        st_ref[pl.ds(0, n), :] = ext_ref[pl.ds(0, n), :] + ext_ref[pl.ds(1, n), :]
        st_ref[pl.ds(0, n), 128:] = st_ref[pl.ds(0, n), 128:] + st_ref[pl.ds(2, n), 128:]
        st_ref[pl.ds(0, n), 256:] = st_ref[pl.ds(0, n), 256:] + st_ref[pl.ds(4, n), 256:]
        st_ref[pl.ds(0, n), 384:] = st_ref[pl.ds(0, n), 384:] + st_ref[pl.ds(8, n), 384:]
        ext_ref[pl.ds(t, POOL_HALO), :] = ext_ref[pl.ds(0, POOL_HALO), :]
        du_ref[...] += st_ref[pl.ds(0, t), :]

    rev = lambda w: pl.BlockSpec((t, w), lambda i: (n_tiles - 1 - i, 0))
    return pl.pallas_call(
        body, name="mix_out_bwd", grid=(n_tiles,),
        in_specs=[rev(D_MODEL)] + W_SPECS + [rev(POOL_WIDTH), _full((4, 128, 128)), _full((1, POOL_WIDTH)), rev(D_MODEL), ANY],
        out_specs=[pl.BlockSpec((N_Q_HEADS, t, 128), lambda i: (0, n_tiles - 1 - i, 0)), rev(POOL_WIDTH),
                   _full((4, 128, 128)), _full((1, POOL_WIDTH)), ANY],
        out_shape=[jax.ShapeDtypeStruct((N_Q_HEADS, s_len, 128), BF16), jax.ShapeDtypeStruct((s_len, POOL_WIDTH), F32),
                   jax.ShapeDtypeStruct((4, 128, 128), F32), jax.ShapeDtypeStruct((1, POOL_WIDTH), F32),
                   jax.ShapeDtypeStruct((N_CHIPS, early_rows, D_MODEL), BF16)],
        scratch_shapes=[pltpu.VMEM((D_MODEL, D_MODEL), BF16), pltpu.VMEM((t + POOL_HALO, POOL_WIDTH), F32),
                        pltpu.VMEM((t + POOL_HALO, POOL_WIDTH), F32), pltpu.VMEM((D_MODEL, D_MODEL), F32),
                        pltpu.VMEM((D_MODEL, D_MODEL), BF16), pltpu.SemaphoreType.DMA((N_CHIPS,))],
        compiler_params=_params(),
    )(dh1, *wts, pooled, wpool, pool_scale, mix, after)


def _attn_bwd(qst, kn, vb, dost, bias_st, sinks, after):
    s_len = kn.shape[0]

    def body(q_ref, kp_ref, kc_ref, vp_ref, vc_ref, do_ref, bias_ref, sink_ref, after_ref, dq_ref, dk_ref, dv_ref, dbias_ref,
             dsink_ref, s_ref, dp_ref, p_ref, dl_ref):
        del after_ref
        i = pl.program_id(0)

        @pl.when(i == 0)
        def _():
            dk_ref[...] = jnp.zeros_like(dk_ref)
            dv_ref[...] = jnp.zeros_like(dv_ref)
            dbias_ref[...] = jnp.zeros_like(dbias_ref)
            dsink_ref[...] = jnp.zeros_like(dsink_ref)

        for b, (rows, k2, v2, bias) in enumerate(_step_blocks(i, kp_ref, kc_ref, vp_ref, vc_ref, bias_ref)):
            s_b, dp_b, p_b, dl_b = s_ref.at[b], dp_ref.at[b], p_ref.at[b], dl_ref.at[b]
            q = q_ref[:, rows, :].reshape(N_Q_HEADS * BLOCK, 128)
            do = do_ref[:, rows, :].reshape(N_Q_HEADS * BLOCK, 128)
            s_b[...] = _dot(q, k2, 1, 1)
            dp_b[...] = _dot(do, v2, 1, 1)

            def head(h, carry):
                head_rows, probs, p_sink = _head_softmax(s_b, bias, sink_ref, h)
                dp = dp_b[head_rows, :]
                dsum = jnp.sum(probs * dp, axis=-1, keepdims=True)
                dlog = probs * (dp - dsum)
                dsink_ref[head_rows, :] -= p_sink * dsum
                dbias_ref[head_rows, :] += dlog
                p_b[head_rows, :] = probs.astype(BF16)
                dl_b[head_rows, :] = (dlog * (HEAD_DIM ** -0.5)).astype(BF16)
                return carry

            lax.fori_loop(0, N_Q_HEADS, head, 0, unroll=True)
            dlog_s = dl_b[...]
            dq_ref[:, rows, :] = jnp.where(_head_lane_mask(), _dot(dlog_s, k2, 1, 0), 0.0).reshape(N_Q_HEADS, BLOCK, 128)
            dk2 = _dot(dlog_s, q, 0, 0)
            dv2 = _dot(p_b[...], do, 0, 0)
            block = ATTN_STEP_BLOCKS * i + b
            prev_rows = pl.ds(pl.multiple_of(jnp.maximum(block - 1, 0) * BLOCK, BLOCK), BLOCK)
            cur_rows = pl.ds(pl.multiple_of(block * BLOCK, BLOCK), BLOCK)
            dk_ref[prev_rows, :] += dk2[:BLOCK]
            dk_ref[cur_rows, :] += dk2[BLOCK:]
            dv_ref[prev_rows, :] += dv2[:BLOCK]
            dv_ref[cur_rows, :] += dv2[BLOCK:]

    stacked, kv, consts = _attn_specs()
    per_step = (ATTN_STEP_BLOCKS,) + BAND
    return pl.pallas_call(
        body, name="attn_bwd", grid=(s_len // (ATTN_STEP_BLOCKS * BLOCK),),
        in_specs=[stacked] + kv + kv + [stacked] + consts + [ANY],
        out_specs=[stacked, _full((s_len, 128)), _full((s_len, 128)), _full(BAND), _full((N_Q_HEADS * BLOCK, 1))],
        out_shape=[jax.ShapeDtypeStruct((N_Q_HEADS, s_len, 128), F32), jax.ShapeDtypeStruct((s_len, 128), F32),
                   jax.ShapeDtypeStruct((s_len, 128), F32), jax.ShapeDtypeStruct(BAND, F32),
                   jax.ShapeDtypeStruct((N_Q_HEADS * BLOCK, 1), F32)],
        scratch_shapes=[pltpu.VMEM(per_step, F32), pltpu.VMEM(per_step, F32), pltpu.VMEM(per_step, BF16),
                        pltpu.VMEM(per_step, BF16)],
        compiler_params=_params(),
    )(qst, kn, kn, vb, vb, dost, bias_st, sinks, after)


def _small_pack(dg_attn, dg_ffn, dg_ple, dscale, dgq, dgk, dbias, dsink_rows, bucket, loss_v, dwpool):
    def body(ga_ref, gf_ref, gp_ref, sc_ref, gq_ref, gk_ref, db_ref, ds_ref, bucket_ref, loss_ref, wp_ref, out_ref):
        out_ref[pl.ds(0, SMALL["w_pool"]), :] = jnp.zeros((SMALL["w_pool"], 128), F32)
        for name, ref, n in (("g_attn", ga_ref, 8), ("g_ffn", gf_ref, 8), ("g_ple", gp_ref, 8), ("pool_scale", sc_ref, 4)):
            for k in range(n):
                out_ref[pl.ds(SMALL[name] + k, 1), :] = ref[:, 128 * k:128 * k + 128]
        for name, ref in (("g_q", gq_ref), ("g_k", gk_ref)):
            both = ref[...]
            out_ref[pl.ds(SMALL[name], 1), :] = both + pltpu.roll(both, 64, axis=1)
        out_ref[pl.ds(SMALL["loss"], 1), :] = loss_ref[...]
        bk = bucket_ref[...]
        rows = lax.broadcasted_iota(jnp.int32, (N_BUCKETS, 128), 0)
        lanes = lax.broadcasted_iota(jnp.int32, (N_BUCKETS, 128), 1)
        lane1 = lax.broadcasted_iota(jnp.int32, (1, 128), 1)
        rb = jnp.zeros((N_BUCKETS, 128), F32)
        sk = jnp.zeros((1, 128), F32)
        for h in range(N_Q_HEADS):
            band = db_ref[pl.ds(h * BLOCK, BLOCK), :]
            for b in range(N_BUCKETS):
                rb = jnp.where((rows == b) & (lanes == h), jnp.sum(jnp.where(bk == b, band, 0.0)), rb)
            sk = jnp.where(lane1 == h, jnp.sum(ds_ref[pl.ds(h * BLOCK, BLOCK), :]), sk)
        out_ref[pl.ds(SMALL["rel_bias"], N_BUCKETS), :] = rb
        out_ref[pl.ds(SMALL["sinks"], 1), :] = sk
        out_ref[pl.ds(SMALL["w_pool"], 512), :] = wp_ref[...].reshape(512, 128)

    return pl.pallas_call(
        body, name="small_pack", in_specs=[VMEM_WHOLE] * 11, out_specs=VMEM_WHOLE,
        out_shape=jax.ShapeDtypeStruct((SMALL_ROWS, 128), F32),
    )(dg_attn, dg_ffn, dg_ple, dscale, dgq, dgk, dbias, dsink_rows, bucket, loss_v, dwpool)


def _attn_in_bwd(dqst, zqk, dk, dv, du, x2, dh1, hn1, slab, wts, g_attn, gq, gk):
    s_len = x2.shape[0]
    t = 512
    n_tiles = s_len // t

    def body(dq_ref, zqk_ref, dk_ref, dv_ref, du_ref, x_ref, dh1_ref, hn_ref, slab_in_ref, sl_ref, lo_ref, me_ref, g_ref,
             gq_ref, gk_ref, dx_ref, dg_ref, dgq_ref, dgk_ref, slab_ref, w_ref, dz_ref, acc_ref, stage_ref, sems):
        del slab_in_ref
        i = pl.program_id(0)

        @pl.when(i == 0)
        def _():
            _load_rows((sl_ref, lo_ref, me_ref), "inT", w_ref, sems)
            dg_ref[...] = jnp.zeros_like(dg_ref)
            dgq_ref[...] = jnp.zeros_like(dgq_ref)
            dgk_ref[...] = jnp.zeros_like(dgk_ref)

        lo = lax.broadcasted_iota(jnp.int32, (t, 128), 1) < 64
        for p in range(4):
            dqn = _from_stacked(dq_ref[2 * p], dq_ref[2 * p + 1], p // 2, lo)
            dq_raw, dgq = _pair_norm_bwd(zqk_ref[:, 128 * p:128 * p + 128], gq_ref[...], dqn)
            dz_ref[:, 128 * p:128 * p + 128] = dq_raw.astype(BF16)
            dgq_ref[...] += dgq
        dk_raw, dgk = _pair_norm_bwd(zqk_ref[:, 512:640], gk_ref[...], dk_ref[...])
        dgk_ref[...] += dgk
        dz_ref[:, 512:640] = dk_raw.astype(BF16)
        dz_ref[:, 640:768] = dv_ref[...].astype(BF16)
        dz_ref[:, 768:] = du_ref[...].astype(BF16)
        dz = dz_ref[...]
        _accumulate_tn(acc_ref, dz, hn_ref[...], i == 0)
        dx, dg = _rms_bwd(x_ref[...], g_ref[...], _dot(dz, w_ref[...], 1, 0))
        dx_ref[...] = dh1_ref[...] + dx
        dg_ref[...] += dg

        @pl.when(i == n_tiles - 1)
        def _():
            _flush_chunks(acc_ref, stage_ref, slab_ref, "inT", sems)

    row = lambda w: pl.BlockSpec((t, w), lambda i: (i, 0))
    return pl.pallas_call(
        body, name="attn_in_bwd", grid=(n_tiles,),
        in_specs=[pl.BlockSpec((N_Q_HEADS, t, 128), lambda i: (0, i, 0)), row(640), row(128), row(128), row(POOL_WIDTH),
                  row(D_MODEL), row(D_MODEL), row(D_MODEL), ANY] + W_SPECS + [_full((1, D_MODEL)), _full((1, 128)),
                                                                              _full((1, 128))],
        out_specs=[row(D_MODEL), _full((1, D_MODEL)), _full((1, 128)), _full((1, 128)), ANY],
        out_shape=[jax.ShapeDtypeStruct((s_len, D_MODEL), F32), jax.ShapeDtypeStruct((1, D_MODEL), F32),
                   jax.ShapeDtypeStruct((1, 128), F32), jax.ShapeDtypeStruct((1, 128), F32),
                   jax.ShapeDtypeStruct(slab.shape, BF16)],
        input_output_aliases={8: 4},
        scratch_shapes=[pltpu.VMEM((IN_WIDTH, D_MODEL), BF16), pltpu.VMEM((t, IN_WIDTH), BF16),
                        pltpu.VMEM((IN_WIDTH, D_MODEL), F32), pltpu.VMEM((IN_WIDTH, D_MODEL), BF16),
                        pltpu.SemaphoreType.DMA((N_CHIPS,))],
        compiler_params=_params(),
    )(dqst, zqk, dk, dv, du, x2, dh1, hn1, slab, *wts, g_attn, gq, gk)


def _dw(lefts, b, name, slab, slab_rows, row_offs):
    tk = 2048
    a0, n_a = lefts[0], len(lefts)
    assert b.shape[1] == D_MODEL
    if a0.ndim == 3:
        s_len, tm = a0.shape[1:]
        m = N_CHIPS * tm
        a_spec = pl.BlockSpec((None, tk, tm), lambda i, k: (i, k, 0))
    else:
        s_len, tm = a0.shape
        m = tm
        a_spec = pl.BlockSpec((tk, tm), lambda i, k: (k, i))
    n_steps, n_tiles = s_len // tk, m // tm
    chunk = m // N_CHIPS
    per_tile = tm // chunk

    def body(*refs):
        a_refs, b_ref = refs[:n_a], refs[n_a]
        o_ref, acc_ref, stage_ref, sems = refs[-4:]
        i, k = pl.program_id(0), pl.program_id(1)
        b_tile = b_ref[...].astype(BF16)
        for w, a_ref in enumerate(a_refs):
            _accumulate_tn(acc_ref.at[w], a_ref[...].astype(BF16), b_tile, k == 0)

        def out_copies(tile, slot):
            return [pltpu.make_async_copy(stage_ref.at[slot, w, pl.ds(jj * chunk, chunk), :],
                                          o_ref.at[tile * per_tile + jj, pl.ds(row_offs[w], chunk), :], sems.at[slot, w, jj])
                    for w in range(n_a) for jj in range(per_tile)]

        @pl.when(k == n_steps - 1)
        def _():
            slot = i % 2

            @pl.when(i >= 2)
            def _():
                for cp in out_copies(i - 2, slot):
                    cp.wait()

            stage_ref[slot] = acc_ref[...].astype(BF16)
            for cp in out_copies(i, slot):
                cp.start()

            @pl.when(i == n_tiles - 1)
            def _():
                for cp in out_copies(i, slot):
                    cp.wait()
                if n_tiles > 1:
                    for cp in out_copies(i - 1, 1 - slot):
                        cp.wait()

    in_specs = [a_spec] * n_a + [pl.BlockSpec((tk, D_MODEL), lambda i, k: (k, 0))]
    operands, aliases = [*lefts, b], {}
    if slab is not None:
        in_specs.append(ANY)
        operands.append(slab)
        aliases = {n_a + 1: 0}
    return pl.pallas_call(
        body, name=name, grid=(n_tiles, n_steps), in_specs=in_specs, out_specs=ANY,
        out_shape=jax.ShapeDtypeStruct((N_CHIPS, slab_rows, D_MODEL), BF16), input_output_aliases=aliases,
        scratch_shapes=[pltpu.VMEM((n_a, tm, D_MODEL), F32), pltpu.VMEM((2, n_a, tm, D_MODEL), BF16),
                        pltpu.SemaphoreType.DMA((2, n_a, per_tile))],
        compiler_params=_params(n_axes=2),
    )(*operands)


def _position():
    x, y, c = lax.axis_index("x"), lax.axis_index("y"), lax.axis_index("c")
    other_chips = [(1 - x, y), (x, 1 - y), (1 - x, 1 - y)]
    return x, y, c, other_chips


def _ag_weights(local_slab, row0, n_rows, name, collective_id):
    half = n_rows // 2
    quarter = half // 2
    assert quarter % 16 == 0

    def body(l_ref, g_ref, send, recv):
        x, y, c, chips = _position()
        me, (via_x, via_y, diagonal) = 2 * x + y, [2 * chip[0] + chip[1] for chip in chips]
        here, sibling, x_nbr, y_nbr = (x, y, c), (x, y, 1 - c), (1 - x, y, c), (x, 1 - y, c)
        peers = [sibling, x_nbr, y_nbr]
        barrier = pltpu.get_barrier_semaphore()
        for peer in peers:
            pl.semaphore_signal(barrier, inc=1, device_id=peer, device_id_type=MESH)
        pl.semaphore_wait(barrier, len(peers))

        def rows(core, part):
            start, size = (core * half, half) if part is None else (core * half + part * quarter, quarter)
            return pl.ds(pl.multiple_of(start, 16), size)

        def copy(k, chip_idx, where, to, src=None):
            dst = g_ref.at[chip_idx, where, :]
            return pltpu.make_async_remote_copy(src_ref=dst if src is None else src, dst_ref=dst, send_sem=send.at[k],
                                                recv_sem=recv.at[k], device_id=to, device_id_type=MESH)

        own_rows = l_ref.at[pl.ds(pl.multiple_of(row0 + c * half, 16), half), :]
        started = [copy(0, me, rows(c, None), x_nbr, src=own_rows), copy(1, me, rows(c, None), y_nbr, src=own_rows)]
        for cp in started:
            cp.start()
        after_arrival = [
            (copy(0, via_x, rows(c, None), here), [copy(4, via_x, rows(c, None), sibling), copy(3, via_x, rows(c, 1), y_nbr)]),
            (copy(1, via_y, rows(c, None), here), [copy(5, via_y, rows(c, None), sibling), copy(2, via_y, rows(c, 0), x_nbr)]),
            (copy(2, diagonal, rows(c, 0), here), [copy(6, diagonal, rows(c, 0), sibling)]),
            (copy(3, diagonal, rows(c, 1), here), [copy(7, diagonal, rows(c, 1), sibling)]),
        ]
        for arrival, onward in after_arrival:
            arrival.wait_recv()
            for cp in onward:
                cp.start()
            started += onward
        for cp in (copy(4, via_x, rows(1 - c, None), here), copy(5, via_y, rows(1 - c, None), here),
                   copy(6, diagonal, rows(1 - c, 0), here), copy(7, diagonal, rows(1 - c, 1), here)):
            cp.wait_recv()
        for cp in started:
            cp.wait_send()

    return pl.kernel(
        body, out_type=jax.ShapeDtypeStruct((N_CHIPS, n_rows, D_MODEL), BF16),
        mesh=plsc.ScalarSubcoreMesh(axis_name="sequencer", num_cores=1), name=name,
        scratch_types=[pltpu.SemaphoreType.DMA((8,)), pltpu.SemaphoreType.DMA((8,))],
        compiler_params=pltpu.CompilerParams(collective_id=collective_id),
    )(local_slab)


def _comm_call(body, peers_of, out_shape, n_sems, operand, name, collective_id):
    sems = [pltpu.SemaphoreType.DMA((n_sems,)), pltpu.SemaphoreType.DMA((n_sems,))]
    if collective_id is None:
        return pl.pallas_call(body, name=name, in_specs=[ANY], out_specs=ANY, out_shape=out_shape, scratch_shapes=sems)(operand)

    def with_handshake(in_ref, out_ref, send, recv):
        x, y, c, _ = _position()
        peers = peers_of(x, y, c)
        barrier = pltpu.get_barrier_semaphore()
        for peer in peers:
            pl.semaphore_signal(barrier, inc=1, device_id=peer, device_id_type=MESH)
        pl.semaphore_wait(barrier, len(peers))
        body(in_ref, out_ref, send, recv)

    return pl.kernel(with_handshake, out_type=out_shape, mesh=plsc.ScalarSubcoreMesh(axis_name="sequencer", num_cores=1),
                     name=name, scratch_types=sems, compiler_params=pltpu.CompilerParams(collective_id=collective_id))(operand)


def _rs_swap_halves(partial, name, collective_id=None):
    half = partial.shape[1] // 2

    def body(p_ref, r_ref, send, recv):
        x, y, c, _ = _position()
        theirs = pl.ds(pl.multiple_of((1 - c) * half, 16), half)
        cp = pltpu.make_async_remote_copy(src_ref=p_ref.at[:, theirs, :], dst_ref=r_ref, send_sem=send.at[0],
                                          recv_sem=recv.at[0], device_id=(x, y, 1 - c), device_id_type=MESH)
        cp.start()
        cp.wait()

    return _comm_call(body, lambda x, y, c: [(x, y, 1 - c)], jax.ShapeDtypeStruct((N_CHIPS, half, D_MODEL), BF16), 1,
                      partial, name, collective_id)


def _rs_add_halves(partial, other, core, name, after):
    half = other.shape[1]
    t = half // 2
    steps = half // t

    def body(core_ref, a_ref, b_ref, after_ref, o_ref):
        del after_ref
        o_ref[...] = (a_ref[...].astype(F32) + b_ref[...].astype(F32)).astype(BF16)

    return pl.pallas_call(
        body, name=name,
        grid_spec=pltpu.PrefetchScalarGridSpec(
            num_scalar_prefetch=1, grid=(N_CHIPS, steps),
            in_specs=[pl.BlockSpec((1, t, D_MODEL), lambda j, i, core_ref: (j, core_ref[0] * steps + i, 0)),
                      pl.BlockSpec((1, t, D_MODEL), lambda j, i, core_ref: (j, i, 0)), ANY],
            out_specs=pl.BlockSpec((1, t, D_MODEL), lambda j, i, core_ref: (j, i, 0))),
        out_shape=jax.ShapeDtypeStruct((N_CHIPS, half, D_MODEL), BF16),
        compiler_params=_params(n_axes=2),
    )(core, partial, other, after)


def _rs_exchange_chips(pre, name, collective_id=None):
    def body(s_ref, r_ref, send, recv):
        x, y, c, chips = _position()

        def copy(k, chunk, to):
            return pltpu.make_async_remote_copy(src_ref=s_ref.at[chunk], dst_ref=r_ref.at[k], send_sem=send.at[k],
                                                recv_sem=recv.at[k], device_id=to, device_id_type=MESH)

        sends = [copy(k, 2 * chip[0] + chip[1], (*chip, c)) for k, chip in enumerate(chips)]
        for cp in sends:
            cp.start()
        for cp in sends:
            cp.wait()

    return _comm_call(body, lambda x, y, c: [(1 - x, y, c), (x, 1 - y, c), (1 - x, 1 - y, c)],
                      jax.ShapeDtypeStruct((3, pre.shape[1], D_MODEL), BF16), 3, pre, name, collective_id)


def _rs_sum_chips(pre, received, place, name, after):
    half = pre.shape[1]
    t = half // 2 if half > 512 else half
    steps = half // t

    def body(place_ref, own_ref, r_ref, after_ref, o_ref):
        del after_ref
        acc = own_ref[0].astype(F32)
        for k in range(3):
            acc = acc + r_ref[k].astype(F32)
        o_ref[...] = acc

    return pl.pallas_call(
        body, name=name,
        grid_spec=pltpu.PrefetchScalarGridSpec(
            num_scalar_prefetch=1, grid=(steps,),
            in_specs=[pl.BlockSpec((1, t, D_MODEL), lambda i, place_ref: (place_ref[0], i, 0)),
                      pl.BlockSpec((3, t, D_MODEL), lambda i, place_ref: (0, i, 0)), ANY],
            out_specs=pl.BlockSpec((t, D_MODEL), lambda i, place_ref: (place_ref[1] * steps + i, 0))),
        out_shape=jax.ShapeDtypeStruct((2 * half, D_MODEL), F32),
        compiler_params=_params(),
    )(place, pre, received, after)


def _half_swap(g_ref, core, to, send, recv, k):
    half = g_ref.shape[0] // 2
    rows = g_ref.at[pl.ds(pl.multiple_of(core * half, 8), half), :]
    return pltpu.make_async_remote_copy(src_ref=rows, dst_ref=rows, send_sem=send.at[k], recv_sem=recv.at[k],
                                        device_id=to, device_id_type=MESH)


def _rs_finish_rows(grads, name, after):
    def body(f_ref, after_ref, g_ref, send, recv):
        del f_ref, after_ref
        x, y, c, _ = _position()
        mine = _half_swap(g_ref, c, (x, y, 1 - c), send, recv, 0)
        mine.start()
        _half_swap(g_ref, 1 - c, (x, y, c), send, recv, 0).wait_recv()
        mine.wait_send()

    return pl.pallas_call(
        body, name=name, in_specs=[ANY, ANY], out_specs=ANY, input_output_aliases={0: 0},
        out_shape=jax.ShapeDtypeStruct(grads.shape, F32),
        scratch_shapes=[pltpu.SemaphoreType.DMA((1,)), pltpu.SemaphoreType.DMA((1,))],
    )(grads, after)


def _small_gather(small, collective_id):
    def body(s_ref, t_ref, send, recv):
        x, y, c, chips = _position()
        sibling = (x, y, 1 - c)

        def slot(px, py, pc):
            return t_ref.at[4 * px + 2 * py + pc]

        def copy(k, block, to, src=None):
            return pltpu.make_async_remote_copy(src_ref=slot(*block) if src is None else src, dst_ref=slot(*block),
                                                send_sem=send.at[k], recv_sem=recv.at[k], device_id=to, device_id_type=MESH)

        own = pltpu.make_async_copy(s_ref, slot(x, y, c), send.at[7])
        own.start()
        first = [copy(0, (x, y, c), sibling, src=s_ref)]
        first += [copy(1 + k, (x, y, c), (*chip, c), src=s_ref) for k, chip in enumerate(chips)]
        for cp in first:
            cp.start()
        passed = []
        for k, chip in enumerate(chips):
            copy(1 + k, (*chip, c), (x, y, c)).wait_recv()
            fwd = copy(4 + k, (*chip, c), sibling)
            fwd.start()
            passed.append(fwd)
        copy(0, sibling, (x, y, c)).wait_recv()
        for k, chip in enumerate(chips):
            copy(4 + k, (*chip, 1 - c), (x, y, c)).wait_recv()
        for cp in first + passed:
            cp.wait_send()
        own.wait()

    peers_of = lambda x, y, c: [(x, y, 1 - c), (1 - x, y, c), (x, 1 - y, c), (1 - x, 1 - y, c)]
    return _comm_call(body, peers_of, jax.ShapeDtypeStruct((N_DEV, SMALL_ROWS, 128), F32), 8, small, "small_gather",
                      collective_id)


def _adam_update(w, g, m, v):
    m_new = ADAM_B1 * m + (1.0 - ADAM_B1) * g
    v_new = ADAM_B2 * v + (1.0 - ADAM_B2) * (g * g)
    m_hat = m_new / (1.0 - ADAM_B1 ** ADAM_STEP)
    v_hat = v_new / (1.0 - ADAM_B2 ** ADAM_STEP)
    return -ADAM_LR * (m_hat / (jnp.sqrt(v_hat) + ADAM_EPS) + ADAM_WD * w), m_new, v_new


def _adamw(w, g_rows, row_off, m, v, name):
    rows, cols = w.shape
    t = rows if rows <= 320 else (rows // 2 if rows % 256 else 256)

    def body(w_ref, g_ref, m_ref, v_ref, go_ref, d_ref, nm_ref, nv_ref):
        g = g_ref[...]
        go_ref[...] = g
        d_ref[...], nm_ref[...], nv_ref[...] = _adam_update(w_ref[...], g, m_ref[...], v_ref[...])

    blk = pl.BlockSpec((t, cols), lambda i: (i, 0))
    assert row_off % 8 == 0 and t % 8 == 0
    g_blk = pl.BlockSpec((pl.Element(t), pl.Element(cols)), lambda i: (pl.multiple_of(row_off + i * t, 8), 0))
    shape = jax.ShapeDtypeStruct((rows, cols), F32)
    return pl.pallas_call(
        body, name=name, grid=(rows // t,), in_specs=[blk, g_blk, blk, blk], out_specs=[blk] * 4, out_shape=[shape] * 4,
        compiler_params=_params(),
    )(w, g_rows, m, v)


SMALL_PARAMS = [("g_attn", (1, D_MODEL), 8), ("g_q", (1, HEAD_DIM), None), ("g_k", (1, HEAD_DIM), None),
                ("sinks", (1, N_Q_HEADS), None), ("rel_bias", (N_BUCKETS, N_Q_HEADS), None), ("w_pool", (512, 128), None),
                ("pool_scale", (1, POOL_WIDTH), 4), ("g_ffn", (1, D_MODEL), 8), ("g_ple", (1, D_MODEL), 8)]


def _adamw_small(tables, wmv):
    n_par = len(SMALL_PARAMS)

    def body(*refs):
        t_ref = refs[0]
        ins = refs[1:1 + 3 * n_par]
        loss_ref = refs[1 + 3 * n_par]
        outs = refs[2 + 3 * n_par:-1]
        tot_ref = refs[-1]
        total = t_ref[0]
        for d in range(1, N_DEV):
            total = total + t_ref[d]
        tot_ref[...] = total
        loss_ref[...] = tot_ref[pl.ds(SMALL["loss"], 1), 0:1]
        for i, (name, shape, split) in enumerate(SMALL_PARAMS):
            g_ref, d_ref, nm_ref, nv_ref = outs[4 * i:4 * i + 4]
            row = SMALL[name]
            if split:
                for k in range(split):
                    g_ref[:, 128 * k:128 * k + 128] = tot_ref[pl.ds(row + k, 1), :]
            else:
                g_ref[...] = tot_ref[pl.ds(row, shape[0]), 0:shape[1]]
            w_ref, m_ref, v_ref = ins[3 * i:3 * i + 3]
            d_ref[...], nm_ref[...], nv_ref[...] = _adam_update(w_ref[...], g_ref[...], m_ref[...], v_ref[...])

    shapes = [jax.ShapeDtypeStruct((1, 1), F32)]
    for _, shape, _ in SMALL_PARAMS:
        shapes += [jax.ShapeDtypeStruct(shape, F32)] * 4
    flat = [a for triple in wmv for a in triple]
    res = pl.pallas_call(
        body, name="adamw_small", in_specs=[VMEM_WHOLE] * (1 + 3 * n_par), out_specs=[VMEM_WHOLE] * len(shapes),
        out_shape=shapes, scratch_shapes=[pltpu.VMEM((SMALL_ROWS, 128), F32)],
    )(tables, *flat)
    return res[0], [res[1 + 4 * i:5 + 4 * i] for i in range(n_par)]


def _pack_ple_proj(shard):
    return shard.reshape(4, 64, 256).transpose(1, 0, 2).reshape(64, D_MODEL)


class _Reduction:
    def __init__(self, tag, place, ids=(None, None)):
        self.tag, self.place, self.ids = tag, place, ids

    def start(self, partial):
        self.partial = partial
        self.other = _rs_swap_halves(partial, "rs_swap_" + self.tag, self.ids[0])
        return partial

    def middle(self, after):
        self.pre = _rs_add_halves(self.partial, self.other, self.place[1:], "rs_add_" + self.tag, after)
        self.received = _rs_exchange_chips(self.pre, "rs_exchange_" + self.tag, self.ids[1])
        return self.pre

    def finish(self, after):
        return _rs_sum_chips(self.pre, self.received, self.place, "rs_sum_" + self.tag, after)


def _local_grads(x2, p2, tgt, wts, g_attn_norm, g_q, g_k, attn_sinks, rel_bias, w_pool, pool_scale, g_ffn_norm, g_ple_norm,
                 reduce_a):
    w_early, w_late = wts
    w_in = w_out = w_early
    bucket = jnp.asarray(_bucket_table())
    gq = jnp.tile(g_q, (1, 2))
    gk = jnp.tile(g_k, (1, 2))
    wpool = w_pool[0].astype(BF16)
    sinks = attn_sinks[0]
    bias_st = _bias_build(rel_bias.T, bucket)

    hn1, zqk, u, kn, vb, qst = _attn_in(x2, g_attn_norm, gq, gk, w_in)
    ost = _attn_fwd(qst, kn, vb, bias_st, sinks)
    pooled, mix, h1, hn2 = _mix_out(u, ost, x2, w_out, wpool, pool_scale, g_ffn_norm)
    loss_v, dgate, dup, act, dh2, hn3, dgl, dw_plp, dh1, dg_ffn, dg_ple = _ffn_ple(hn2, h1, p2, tgt, w_late, g_ffn_norm,
                                                                                      g_ple_norm)

    late0, late_rows = GATHER_PARTS[1][0], SLAB_ROWS - GATHER_PARTS[1][0]
    partial_a = None
    for names, lefts, right in ((("gateT", "upT"), [dgate, dup], hn2), (("down",), [act], dh2), (("plg",), [hn3], dgl)):
        partial_a = _dw(lefts, right, "dw_" + names[0], partial_a, late_rows, [SLAB[name][0] - late0 for name in names])
    dw_plp = dw_plp.reshape(4, 64, N_CHIPS, 256).transpose(2, 1, 0, 3).reshape(N_CHIPS, 64, D_MODEL)
    partial_a = reduce_a.start(lax.dynamic_update_slice(partial_a, dw_plp, (0, SLAB["plp"][0] - late0, 0)))
    dost, du, dw_pool, dscale, partial_b = _mix_out_bwd(dh1, w_out, pooled, wpool, pool_scale, mix, partial_a)
    pre_a = reduce_a.middle(du)
    dqst, dk, dv, dbias, dsink_rows = _attn_bwd(qst, kn, vb, dost, bias_st, sinks, pre_a)
    dx, dg_attn, dgq, dgk, partial_b = _attn_in_bwd(dqst, zqk, dk, dv, du, x2, dh1, hn1, partial_b, w_in, g_attn_norm, gq, gk)

    small = _small_pack(dg_attn, dg_ffn, dg_ple, dscale, dgq, dgk, dbias, dsink_rows, bucket, loss_v, dw_pool)
    return dx, partial_b, small


def kernel(x, p, w_in, w_out, g_attn_norm, g_q, g_k, attn_sinks, rel_bias, w_pool, pool_scale, g_ffn_norm, w_gate, w_up, w_down, g_ple_norm, w_ple_gate, w_ple_proj, loss_target, m_w_in, m_w_out, m_g_attn_norm, m_g_q, m_g_k, m_attn_sinks, m_rel_bias, m_w_pool, m_pool_scale, m_g_ffn_norm, m_w_gate, m_w_up, m_w_down, m_g_ple_norm, m_w_ple_gate, m_w_ple_proj, v_w_in, v_w_out, v_g_attn_norm, v_g_q, v_g_k, v_attn_sinks, v_rel_bias, v_w_pool, v_pool_scale, v_g_ffn_norm, v_w_gate, v_w_up, v_w_down, v_g_ple_norm, v_w_ple_gate, v_w_ple_proj):
    core = lax.axis_index("c").astype(jnp.int32).reshape(1)
    me = (2 * lax.axis_index("x") + lax.axis_index("y")).astype(jnp.int32).reshape(1)

    local_parts = [jnp.concatenate(pieces, axis=0).astype(BF16) for pieces in (
        [w_in[0].T, w_out[0]], [w_gate[0].T, w_up[0].T, w_down[0], w_ple_gate[0], _pack_ple_proj(w_ple_proj[0])])]
    wts = [(_ag_weights(local, 0, local.shape[0], name, collective_id), local, me)
           for local, name, collective_id in zip(local_parts, ("ag_early", "ag_late"), (1, 2))]

    place = jnp.concatenate([me, core])
    reduce_a = _Reduction("a", place, ids=(3, 4))
    dx, partial_b, small = _local_grads(x[0], p[0, 0], loss_target[0], wts, g_attn_norm, g_q, g_k, attn_sinks, rel_bias,
                                        w_pool, pool_scale, g_ffn_norm, g_ple_norm, reduce_a)
    reduce_b = _Reduction("b", place, ids=(6, 7))
    reduce_b.start(partial_b)
    small_all = _small_gather(small, 8)
    summed_a = reduce_a.finish(small)
    pre_b = reduce_b.middle(summed_a)
    grads_a = _rs_finish_rows(summed_a, "rs_finish_a", pre_b)

    late0 = GATHER_PARTS[1][0]

    def rows(name):
        return grads_a, SLAB[name][0] - late0

    plp_rows = grads_a[SLAB["plp"][0] - late0:]
    big = {
        "w_gate": (w_gate, m_w_gate, v_w_gate, rows("gateT"), True),
        "w_up": (w_up, m_w_up, v_w_up, rows("upT"), True),
        "w_down": (w_down, m_w_down, v_w_down, rows("down"), False),
        "w_ple_gate": (w_ple_gate, m_w_ple_gate, v_w_ple_gate, rows("plg"), False),
        "w_ple_proj": (w_ple_proj, m_w_ple_proj, v_w_ple_proj,
                       (plp_rows.reshape(64, 4, 256).transpose(1, 0, 2).reshape(PLE_DIM, PLE_DIM), 0), False),
        "w_out": (w_out, m_w_out, v_w_out, None, False),
        "w_in": (w_in, m_w_in, v_w_in, None, True),
    }
    small_params = {
        "g_attn_norm": (g_attn_norm, m_g_attn_norm, v_g_attn_norm), "g_q": (g_q, m_g_q, v_g_q), "g_k": (g_k, m_g_k, v_g_k),
        "attn_sinks": (attn_sinks, m_attn_sinks, v_attn_sinks), "rel_bias": (rel_bias, m_rel_bias, v_rel_bias),
        "w_pool": tuple(a.reshape(512, 128) for a in (w_pool, m_w_pool, v_w_pool)),
        "pool_scale": (pool_scale, m_pool_scale, v_pool_scale), "g_ffn_norm": (g_ffn_norm, m_g_ffn_norm, v_g_ffn_norm),
        "g_ple_norm": (g_ple_norm, m_g_ple_norm, v_g_ple_norm),
    }

    grads, deltas, new_ms, new_vs = {}, {}, {}, {}
    out = grads_b = None
    for name, (w, m, v, g_src, transposed) in big.items():
        if g_src is None:
            if grads_b is None:
                grads_b = _rs_finish_rows(reduce_b.finish(out[-1]), "rs_finish_b", out[-1])
            g_src = (grads_b, SLAB["out" if name == "w_out" else "inT"][0])
        view = (lambda a: a.T) if transposed else (lambda a: a)
        out = _adamw(view(w[0]), *g_src, view(m[0]), view(v[0]), "adamw_" + name)
        grads[name], deltas[name], new_ms[name], new_vs[name] = (view(a)[None] for a in out)

    loss, small_out = _adamw_small(small_all, list(small_params.values()))
    for name, (g2, d, nm, nv) in zip(small_params, small_out):
        shape = w_pool.shape if name == "w_pool" else g2.shape
        grads[name], deltas[name], new_ms[name], new_vs[name] = (a.reshape(shape) for a in (g2, d, nm, nv))

    order = ["w_in", "w_out", "g_attn_norm", "g_q", "g_k", "attn_sinks", "rel_bias", "w_pool", "pool_scale", "g_ffn_norm",
             "w_gate", "w_up", "w_down", "g_ple_norm", "w_ple_gate", "w_ple_proj"]
    return (loss.reshape(()), dx[None], *[grads[n] for n in order], *[deltas[n] for n in order],
            *[new_ms[n] for n in order], *[new_vs[n] for n in order])
```

```python
import numpy as np
import jax
import jax.numpy as jnp
from jax import lax
from jax.experimental import pallas as pl
from jax.experimental.pallas import tpu as pltpu
from jax.experimental.pallas import tpu_sc as plsc

F32 = jnp.float32
BF16 = jnp.bfloat16
MESH = pl.DeviceIdType.MESH

D_MODEL = 1024
HEAD_DIM = 64
N_Q_HEADS = 8
ATTN_WIDTH = 512
POOL_WIDTH = 512
IN_WIDTH = 1280
D_FF = 2816
PLE_DIM = 256
FF_CHUNK = 1408
N_FF_CHUNKS = D_FF // FF_CHUNK
BLOCK = 128
N_BUCKETS = 32
MAX_DISTANCE = 128
EPS = 1e-6
NEG = -1e30
N_CHIPS = 4
N_DEV = 8

ADAM_LR = 0.001
ADAM_B1 = 0.9
ADAM_B2 = 0.999
ADAM_EPS = 1e-08
ADAM_WD = 0.01
ADAM_STEP = 10

SLAB = {"inT": (0, 320), "out": (320, 256), "gateT": (576, 704), "upT": (1280, 704), "down": (1984, 704),
        "plg": (2688, 256), "plp": (2944, 64)}
SLAB_ROWS = 3008
GATHER_PARTS = ((0, 576), (576, SLAB_ROWS))
POOL_HALO = 24

SMALL = {"g_attn": 0, "g_ffn": 8, "g_ple": 16, "pool_scale": 24, "g_q": 28, "g_k": 29, "sinks": 30, "loss": 31,
         "rel_bias": 32, "w_pool": 64}
SMALL_ROWS = 576

VMEM_LIMIT_BIG = 60 * 1024 * 1024
VMEM_LIMIT = 48 * 1024 * 1024


def _params(vmem=VMEM_LIMIT, n_axes=1):
    return pltpu.CompilerParams(dimension_semantics=("arbitrary",) * n_axes, vmem_limit_bytes=vmem)


def _dot(a, b, ca, cb):
    return lax.dot_general(a, b, (((ca,), (cb,)), ((), ())), preferred_element_type=F32)


def _full(shape):
    return pl.BlockSpec(shape, lambda i: (0,) * len(shape))


ANY = pl.BlockSpec(memory_space=pl.ANY)
VMEM_WHOLE = pl.BlockSpec(memory_space=pltpu.VMEM)


W_SPECS = [ANY, ANY, pl.BlockSpec(memory_space=pltpu.SMEM)]


def _load_rows(w_refs, name, dst_ref, sems):
    slab_ref, local_ref, me_ref = w_refs
    off, rows = SLAB[name]
    slab_off = off - max(start for start, _ in GATHER_PARTS if start <= off)
    me = me_ref[0]
    for phase in ("start", "wait"):
        for j in range(N_CHIPS):
            dst = dst_ref.at[pl.ds(j * rows, rows), :]
            theirs = pltpu.make_async_copy(slab_ref.at[j, pl.ds(slab_off, rows), :], dst, sems.at[j])
            own = pltpu.make_async_copy(local_ref.at[pl.ds(slab_off, rows), :], dst, sems.at[j])

            @pl.when(me == j)
            def _():
                getattr(own, phase)()

            @pl.when(me != j)
            def _():
                getattr(theirs, phase)()


def _rms_fwd(x, g):
    r = lax.rsqrt(jnp.mean(x * x, axis=-1, keepdims=True) + EPS)
    return x * r * g


def _rms_bwd(x, g, dy):
    r = lax.rsqrt(jnp.mean(x * x, axis=-1, keepdims=True) + EPS)
    xn = x * r
    dyg = dy * g
    dx = r * (dyg - xn * jnp.mean(dyg * xn, axis=-1, keepdims=True))
    return dx, jnp.sum(dy * xn, axis=0, keepdims=True)


def _half_sum(v, lo):
    s_lo = jnp.sum(jnp.where(lo, v, 0.0), axis=-1, keepdims=True)
    s_hi = jnp.sum(jnp.where(lo, 0.0, v), axis=-1, keepdims=True)
    return jnp.where(lo, s_lo, s_hi)


def _half_sum_mxu(v):
    upper = lax.broadcasted_iota(jnp.int32, (128, 128), 0) < 64
    left = lax.broadcasted_iota(jnp.int32, (128, 128), 1) < 64
    ones = jnp.where(upper == left, 1.0, 0.0).astype(BF16)
    high = v.astype(BF16)
    low = (v - high.astype(F32)).astype(BF16)
    return _dot(high, ones, 1, 0) + _dot(low, ones, 1, 0)


def _pair_norm(zp, g, lo):
    r = lax.rsqrt(_half_sum(zp * zp, lo) * (1.0 / HEAD_DIM) + EPS)
    return zp * r * g


def _pair_norm_bwd(zp, g, dy):
    r = lax.rsqrt(_half_sum_mxu(zp * zp) * (1.0 / HEAD_DIM) + EPS)
    xn = zp * r
    dyg = dy * g
    dx = r * (dyg - xn * (_half_sum_mxu(dyg * xn) * (1.0 / HEAD_DIM)))
    return dx, jnp.sum(dy * xn, axis=0, keepdims=True)


def _to_stacked(pair, group, lo):
    rolled = pltpu.roll(pair, 64, axis=1)
    if group == 0:
        return jnp.where(lo, pair, 0.0), jnp.where(lo, rolled, 0.0)
    return jnp.where(lo, 0.0, rolled), jnp.where(lo, 0.0, pair)


def _from_stacked(even, odd, group, lo):
    if group == 0:
        return jnp.where(lo, even, pltpu.roll(odd, 64, axis=1))
    return jnp.where(lo, pltpu.roll(even, 64, axis=1), odd)


def _sigmoid(v):
    return 1.0 / (1.0 + jnp.exp(-v))


def _pool_counts(tile, n_rows):
    t1 = tile * n_rows + lax.broadcasted_iota(jnp.int32, (n_rows, POOL_WIDTH), 0) + 1
    lane = lax.broadcasted_iota(jnp.int32, (n_rows, POOL_WIDTH), 1)
    win = jnp.where(lane < 128, 2, jnp.where(lane < 256, 4, jnp.where(lane < 384, 8, 16)))
    return jnp.minimum(t1, win).astype(F32)


def _attn_in(x2, g_attn, gq, gk, wts):
    s_len = x2.shape[0]
    t = 512

    def body(x_ref, g_ref, gq_ref, gk_ref, sl_ref, lo_ref, me_ref, hn_ref, zqk_ref, u_ref, kn_ref, v_ref, qst_ref, w_ref, sems):
        @pl.when(pl.program_id(0) == 0)
        def _():
            _load_rows((sl_ref, lo_ref, me_ref), "inT", w_ref, sems)

        hn = _rms_fwd(x_ref[...], g_ref[...]).astype(BF16)
        hn_ref[...] = hn
        z = _dot(hn, w_ref[...], 1, 1)
        zqk_ref[...] = z[:, :640]
        u_ref[...] = z[:, 768:]
        v_ref[...] = z[:, 640:768].astype(BF16)
        lo = lax.broadcasted_iota(jnp.int32, (t, 128), 1) < 64
        kn_ref[...] = _pair_norm(z[:, 512:640], gk_ref[...], lo).astype(BF16)
        for p in range(4):
            qn = _pair_norm(z[:, 128 * p:128 * p + 128], gq_ref[...], lo)
            even, odd = _to_stacked(qn, p // 2, lo)
            qst_ref[2 * p] = even.astype(BF16)
            qst_ref[2 * p + 1] = odd.astype(BF16)

    row = lambda w: pl.BlockSpec((t, w), lambda i: (i, 0))
    return pl.pallas_call(
        body, name="attn_in", grid=(s_len // t,),
        in_specs=[row(D_MODEL), _full((1, D_MODEL)), _full((1, 128)), _full((1, 128))] + W_SPECS,
        out_specs=[row(D_MODEL), row(640), row(POOL_WIDTH), row(128), row(128),
                   pl.BlockSpec((N_Q_HEADS, t, 128), lambda i: (0, i, 0))],
        out_shape=[jax.ShapeDtypeStruct((s_len, D_MODEL), BF16), jax.ShapeDtypeStruct((s_len, 640), F32),
                   jax.ShapeDtypeStruct((s_len, POOL_WIDTH), F32), jax.ShapeDtypeStruct((s_len, 128), BF16),
                   jax.ShapeDtypeStruct((s_len, 128), BF16), jax.ShapeDtypeStruct((N_Q_HEADS, s_len, 128), BF16)],
        scratch_shapes=[pltpu.VMEM((IN_WIDTH, D_MODEL), BF16), pltpu.SemaphoreType.DMA((N_CHIPS,))],
        compiler_params=_params(),
    )(x2, g_attn, gq, gk, *wts)


def _bucket_table():
    i_idx = np.arange(BLOCK)[:, None]
    j_idx = np.arange(2 * BLOCK)[None, :]
    d = BLOCK + i_idx - j_idx
    n = np.maximum(d, 0)
    max_exact = N_BUCKETS // 2
    nf = np.maximum(n, 1).astype(np.float64)
    large = max_exact + (np.log(nf / max_exact) / np.log(MAX_DISTANCE / max_exact) * (N_BUCKETS - max_exact)).astype(np.int64)
    large = np.minimum(large, N_BUCKETS - 1)
    bucket = np.where(n < max_exact, n, large)
    return np.where((d >= 0) & (d < BLOCK), bucket, -1).astype(np.int32)


def _bias_build(rel_bias_t, bucket):
    def body(rb_ref, bucket_ref, out_ref):
        bk = bucket_ref[...]
        for h in range(N_Q_HEADS):
            acc = jnp.full((BLOCK, 2 * BLOCK), NEG, F32)
            for b in range(N_BUCKETS):
                acc = jnp.where(bk == b, rb_ref[h, b], acc)
            out_ref[0, pl.ds(h * BLOCK, BLOCK), :] = acc
            out_ref[1, pl.ds(h * BLOCK, BLOCK), :] = acc
            out_ref[1, pl.ds(h * BLOCK, BLOCK), 0:BLOCK] = jnp.full((BLOCK, BLOCK), NEG, F32)

    return pl.pallas_call(
        body, name="bias_build",
        in_specs=[pl.BlockSpec(memory_space=pltpu.SMEM), VMEM_WHOLE], out_specs=VMEM_WHOLE,
        out_shape=jax.ShapeDtypeStruct((2, N_Q_HEADS * BLOCK, 2 * BLOCK), F32),
    )(rel_bias_t, bucket)


def _head_softmax(s_ref, bias_ref, sink_ref, h):
    rows = pl.ds(pl.multiple_of(h * BLOCK, BLOCK), BLOCK)
    s = s_ref[rows, :] * (HEAD_DIM ** -0.5) + bias_ref[rows, :]
    sink = sink_ref[h]
    m = jnp.maximum(jnp.max(s, axis=-1, keepdims=True), sink)
    p = jnp.exp(s - m)
    e_sink = jnp.exp(sink - m)
    inv = 1.0 / (jnp.sum(p, axis=-1, keepdims=True) + e_sink)
    return rows, p * inv, e_sink * inv


ATTN_STEP_BLOCKS = 4
BAND = (N_Q_HEADS * BLOCK, 2 * BLOCK)


def _attn_specs():
    nb = ATTN_STEP_BLOCKS
    stacked = pl.BlockSpec((N_Q_HEADS, nb * BLOCK, 128), lambda i: (0, i, 0))
    kv = [pl.BlockSpec((BLOCK, 128), lambda i: (jnp.maximum(nb * i - 1, 0), 0)), pl.BlockSpec((nb * BLOCK, 128), lambda i: (i, 0))]
    consts = [_full((2,) + BAND), pl.BlockSpec(memory_space=pltpu.SMEM)]
    return stacked, kv, consts


def _step_blocks(i, kp_ref, kc_ref, vp_ref, vc_ref, bias_ref):
    blocks = []
    for b in range(ATTN_STEP_BLOCKS):
        if b == 0:
            k2 = jnp.concatenate([kp_ref[...], kc_ref[pl.ds(0, BLOCK), :]], axis=0)
            v2 = jnp.concatenate([vp_ref[...], vc_ref[pl.ds(0, BLOCK), :]], axis=0)
            bias = bias_ref.at[jnp.where(i == 0, 1, 0)]
        else:
            k2, v2, bias = kc_ref[pl.ds((b - 1) * BLOCK, 2 * BLOCK), :], vc_ref[pl.ds((b - 1) * BLOCK, 2 * BLOCK), :], bias_ref.at[0]
        blocks.append((pl.ds(b * BLOCK, BLOCK), k2, v2, bias))
    return blocks


def _head_lane_mask():
    rows = lax.broadcasted_iota(jnp.int32, (N_Q_HEADS * BLOCK, 128), 0)
    lanes = lax.broadcasted_iota(jnp.int32, (N_Q_HEADS * BLOCK, 128), 1)
    return (rows < 4 * BLOCK) == (lanes < 64)


def _attn_fwd(qst, kn, vb, bias_st, sinks):
    s_len = kn.shape[0]

    def body(q_ref, kp_ref, kc_ref, vp_ref, vc_ref, bias_ref, sink_ref, o_ref, s_ref, p_ref):
        for b, (rows, k2, v2, bias) in enumerate(_step_blocks(pl.program_id(0), kp_ref, kc_ref, vp_ref, vc_ref, bias_ref)):
            s_b, p_b = s_ref.at[b], p_ref.at[b]
            s_b[...] = _dot(q_ref[:, rows, :].reshape(N_Q_HEADS * BLOCK, 128), k2, 1, 1)

            def head(h, carry):
                head_rows, probs, _ = _head_softmax(s_b, bias, sink_ref, h)
                p_b[head_rows, :] = probs.astype(BF16)
                return carry

            lax.fori_loop(0, N_Q_HEADS, head, 0, unroll=True)
            o = jnp.where(_head_lane_mask(), _dot(p_b[...], v2, 1, 0), 0.0)
            o_ref[:, rows, :] = o.astype(BF16).reshape(N_Q_HEADS, BLOCK, 128)

    stacked, kv, consts = _attn_specs()
    return pl.pallas_call(
        body, name="attn_fwd", grid=(s_len // (ATTN_STEP_BLOCKS * BLOCK),),
        in_specs=[stacked] + kv + kv + consts, out_specs=stacked,
        out_shape=jax.ShapeDtypeStruct((N_Q_HEADS, s_len, 128), BF16),
        scratch_shapes=[pltpu.VMEM((ATTN_STEP_BLOCKS,) + BAND, F32), pltpu.VMEM((ATTN_STEP_BLOCKS,) + BAND, BF16)],
        compiler_params=_params(),
    )(qst, kn, kn, vb, vb, bias_st, sinks)


def _mix_out(u, ost, x2, wts, wpool, pool_scale, g_ffn):
    s_len = x2.shape[0]
    t = 512
    n = t + 16

    def body(u_ref, o_ref, x_ref, sl_ref, lo_ref, me_ref, wp_ref, sc_ref, g_ref, pooled_ref, mix_ref, h1_ref, hn_ref,
             w_ref, ext_ref, st_ref, sems):
        i = pl.program_id(0)

        @pl.when(i == 0)
        def _():
            _load_rows((sl_ref, lo_ref, me_ref), "out", w_ref, sems)
            ext_ref[...] = jnp.zeros_like(ext_ref)
            st_ref[...] = jnp.zeros_like(st_ref)

        u_tile = u_ref[...]
        ext_ref[pl.ds(POOL_HALO, t), :] = u_tile
        st_ref[pl.ds(8, n), :] = ext_ref[pl.ds(8, n), :] + ext_ref[pl.ds(7, n), :]
        st_ref[pl.ds(8, n), 128:] = st_ref[pl.ds(8, n), 128:] + st_ref[pl.ds(6, n), 128:]
        st_ref[pl.ds(8, n), 256:] = st_ref[pl.ds(8, n), 256:] + st_ref[pl.ds(4, n), 256:]
        st_ref[pl.ds(8, n), 384:] = st_ref[pl.ds(8, n), 384:] + st_ref[pl.ds(0, n), 384:]
        ext_ref[pl.ds(0, POOL_HALO), :] = ext_ref[pl.ds(t, POOL_HALO), :]
        pooled = (st_ref[pl.ds(POOL_HALO, t), :] / _pool_counts(i, t) - u_tile).astype(BF16)
        pooled_ref[...] = pooled
        for g in range(4):
            cols = slice(128 * g, 128 * g + 128)
            y = _dot(pooled[:, cols], wp_ref[g], 1, 0) * sc_ref[:, cols]
            mix_ref[:, ATTN_WIDTH + 128 * g:ATTN_WIDTH + 128 * g + 128] = y.astype(BF16)
        lo = lax.broadcasted_iota(jnp.int32, (t, 128), 1) < 64
        for p in range(4):
            a = _from_stacked(o_ref[2 * p].astype(F32), o_ref[2 * p + 1].astype(F32), p // 2, lo)
            mix_ref[:, 128 * p:128 * p + 128] = a.astype(BF16)
        h1 = x_ref[...] + _dot(mix_ref[...], w_ref[...], 1, 0)
        h1_ref[...] = h1
        hn_ref[...] = _rms_fwd(h1, g_ref[...]).astype(BF16)

    row = lambda w: pl.BlockSpec((t, w), lambda i: (i, 0))
    return pl.pallas_call(
        body, name="mix_out", grid=(s_len // t,),
        in_specs=[row(POOL_WIDTH), pl.BlockSpec((N_Q_HEADS, t, 128), lambda i: (0, i, 0)), row(D_MODEL)] + W_SPECS
        + [_full((4, 128, 128)), _full((1, POOL_WIDTH)), _full((1, D_MODEL))],
        out_specs=[row(POOL_WIDTH), row(D_MODEL), row(D_MODEL), row(D_MODEL)],
        out_shape=[jax.ShapeDtypeStruct((s_len, POOL_WIDTH), BF16), jax.ShapeDtypeStruct((s_len, D_MODEL), BF16),
                   jax.ShapeDtypeStruct((s_len, D_MODEL), F32), jax.ShapeDtypeStruct((s_len, D_MODEL), BF16)],
        scratch_shapes=[pltpu.VMEM((D_MODEL, D_MODEL), BF16), pltpu.VMEM((t + POOL_HALO, POOL_WIDTH), F32),
                        pltpu.VMEM((t + POOL_HALO, POOL_WIDTH), F32), pltpu.SemaphoreType.DMA((N_CHIPS,))],
        compiler_params=_params(),
    )(u, ost, x2, *wts, wpool, pool_scale, g_ffn)


def _ffn_ple(hn2, h1, p2, tgt, wts, g_ffn, g_ple):
    s_len = h1.shape[0]
    t = 256
    n_tiles = s_len // t

    def body(hn_ref, h1_ref, p_ref, tgt_ref, sl_ref, lo_ref, me_ref, gf_ref, gp_ref,
             loss_ref, dgate_ref, dup_ref, act_ref, dh2b_ref, hn3_ref, dgl_ref, dwp_ref, dh1_ref, dgf_ref, dgp_ref,
             wg_ref, wu_ref, wd_ref, wl_ref, wp_ref, packed_ref, gate_s, up_s, loss_acc, dwp_acc, sems):
        i = pl.program_id(0)

        @pl.when(i == 0)
        def _():
            w_refs = (sl_ref, lo_ref, me_ref)
            _load_rows(w_refs, "gateT", wg_ref, sems)
            _load_rows(w_refs, "upT", wu_ref, sems)
            _load_rows(w_refs, "down", wd_ref, sems)
            _load_rows(w_refs, "plg", wl_ref, sems)
            _load_rows(w_refs, "plp", packed_ref, sems)
            for j in range(N_CHIPS):
                for q in range(4):
                    wp_ref[pl.ds(64 * q, 64), 256 * j:256 * j + 256] = packed_ref[pl.ds(64 * j, 64), 256 * q:256 * q + 256]
            loss_acc[...] = jnp.zeros_like(loss_acc)
            dgf_ref[...] = jnp.zeros_like(dgf_ref)
            dgp_ref[...] = jnp.zeros_like(dgp_ref)

        hn = hn_ref[...]
        h1v = h1_ref[...]
        h2 = h1v
        for ch in range(N_FF_CHUNKS):
            rows = pl.ds(ch * FF_CHUNK, FF_CHUNK)
            gate = _dot(hn, wg_ref[rows, :], 1, 1)
            up = _dot(hn, wu_ref[rows, :], 1, 1)
            gate_s[ch] = gate
            up_s[ch] = up
            act = (gate * _sigmoid(gate) * up).astype(BF16)
            act_ref[ch] = act
            h2 = h2 + _dot(act, wd_ref[rows, :], 1, 0)
        gp = gp_ref[...]
        hn3 = _rms_fwd(h2, gp).astype(BF16)
        hn3_ref[...] = hn3
        gate2 = _sigmoid(_dot(hn3, wl_ref[...], 1, 0))
        p_tile = p_ref[...].astype(BF16)
        pp = _dot(p_tile, wp_ref[...], 1, 0)
        err = h2 + gate2 * pp - tgt_ref[...]
        loss_acc[...] += jnp.sum(err * err, axis=0, keepdims=True)
        dy = err * (1.0 / D_MODEL)
        _accumulate_tn(dwp_acc, p_tile, (dy * gate2).astype(BF16), i == 0)
        dgl = (dy * pp * gate2 * (1.0 - gate2)).astype(BF16)
        dgl_ref[...] = dgl
        dx3, dg3 = _rms_bwd(h2, gp, _dot(dgl, wl_ref[...], 1, 1))
        dh2 = dy + dx3
        dgp_ref[...] += dg3
        dh2b = dh2.astype(BF16)
        dh2b_ref[...] = dh2b
        dhn = jnp.zeros((t, D_MODEL), F32)
        for ch in range(N_FF_CHUNKS):
            rows = pl.ds(ch * FF_CHUNK, FF_CHUNK)
            dact = _dot(dh2b, wd_ref[rows, :], 1, 1)
            gate_v = gate_s[ch]
            up_v = up_s[ch]
            sg = _sigmoid(gate_v)
            dup = (dact * (gate_v * sg)).astype(BF16)
            dgate = (dact * up_v * (sg * (1.0 + gate_v * (1.0 - sg)))).astype(BF16)
            dup_ref[ch] = dup
            dgate_ref[ch] = dgate
            dhn = dhn + _dot(dgate, wg_ref[rows, :], 1, 0) + _dot(dup, wu_ref[rows, :], 1, 0)
        dx, dg = _rms_bwd(h1v, gf_ref[...], dhn)
        dh1_ref[...] = dh2 + dx
        dgf_ref[...] += dg

        @pl.when(i == n_tiles - 1)
        def _():
            total = jnp.sum(loss_acc[...], axis=-1, keepdims=True) * (0.5 / D_MODEL)
            loss_ref[...] = jnp.broadcast_to(total, loss_ref.shape)
            dwp_ref[...] = dwp_acc[...].astype(BF16)

    row = lambda w: pl.BlockSpec((t, w), lambda i: (i, 0))
    chunked = pl.BlockSpec((N_FF_CHUNKS, t, FF_CHUNK), lambda i: (0, i, 0))
    vec = _full((1, D_MODEL))
    act_shape = jax.ShapeDtypeStruct((N_FF_CHUNKS, s_len, FF_CHUNK), BF16)
    tok = lambda dtype: jax.ShapeDtypeStruct((s_len, D_MODEL), dtype)
    return pl.pallas_call(
        body, name="ffn_ple", grid=(n_tiles,),
        in_specs=[row(D_MODEL), row(D_MODEL), row(PLE_DIM), row(D_MODEL)] + W_SPECS + [vec, vec],
        out_specs=[_full((1, 128)), chunked, chunked, chunked] + [row(D_MODEL)] * 3 + [_full((PLE_DIM, D_MODEL)), row(D_MODEL),
                                                                                       vec, vec],
        out_shape=[jax.ShapeDtypeStruct((1, 128), F32), act_shape, act_shape, act_shape, tok(BF16), tok(BF16), tok(BF16),
                   jax.ShapeDtypeStruct((PLE_DIM, D_MODEL), BF16), tok(F32), jax.ShapeDtypeStruct((1, D_MODEL), F32),
                   jax.ShapeDtypeStruct((1, D_MODEL), F32)],
        scratch_shapes=[pltpu.VMEM((D_FF, D_MODEL), BF16)] * 3
        + [pltpu.VMEM((D_MODEL, D_MODEL), BF16), pltpu.VMEM((PLE_DIM, D_MODEL), BF16), pltpu.VMEM((PLE_DIM, D_MODEL), BF16),
           pltpu.VMEM((N_FF_CHUNKS, t, FF_CHUNK), F32), pltpu.VMEM((N_FF_CHUNKS, t, FF_CHUNK), F32), pltpu.VMEM((1, D_MODEL), F32),
           pltpu.VMEM((PLE_DIM, D_MODEL), F32), pltpu.SemaphoreType.DMA((N_CHIPS,))],
        compiler_params=_params(VMEM_LIMIT_BIG),
    )(hn2, h1, p2, tgt, *wts, g_ffn, g_ple)


def _accumulate_tn(acc_ref, a, b, first):
    @pl.when(first)
    def _():
        acc_ref[...] = _dot(a, b, 0, 0)

    @pl.when(jnp.logical_not(first))
    def _():
        acc_ref[...] += _dot(a, b, 0, 0)


def _flush_chunks(acc_ref, stage_ref, slab_ref, name, sems):
    stage_ref[...] = acc_ref[...].astype(BF16)
    off, rows = SLAB[name]
    copies = [pltpu.make_async_copy(stage_ref.at[pl.ds(j * rows, rows), :], slab_ref.at[j, pl.ds(off, rows), :], sems.at[j])
              for j in range(N_CHIPS)]
    for cp in copies:
        cp.start()
    for cp in copies:
        cp.wait()


def _mix_out_bwd(dh1, wts, pooled, wpool, pool_scale, mix, after):
    s_len = dh1.shape[0]
    t = 512
    n = t + 16
    n_tiles = s_len // t
    early_rows = GATHER_PARTS[0][1]

    def body(dh1_ref, sl_ref, lo_ref, me_ref, pooled_ref, wp_ref, sc_ref, mix_ref, after_ref, dost_ref, du_ref, dwp_ref,
             dsc_ref, slab_ref, w_ref, ext_ref, st_ref, acc_ref, stage_ref, sems):
        del after_ref
        i = pl.program_id(0)

        @pl.when(i == 0)
        def _():
            _load_rows((sl_ref, lo_ref, me_ref), "out", w_ref, sems)
            ext_ref[...] = jnp.zeros_like(ext_ref)
            st_ref[...] = jnp.zeros_like(st_ref)
            dsc_ref[...] = jnp.zeros_like(dsc_ref)
            dwp_ref[...] = jnp.zeros_like(dwp_ref)

        dh1b = dh1_ref[...].astype(BF16)
        _accumulate_tn(acc_ref, mix_ref[...], dh1b, i == 0)

        @pl.when(i == n_tiles - 1)
        def _():
            _flush_chunks(acc_ref, stage_ref, slab_ref, "out", sems)

        dmix = _dot(dh1b, w_ref[...], 1, 1)
        lo = lax.broadcasted_iota(jnp.int32, (t, 128), 1) < 64
        for p in range(4):
            even, odd = _to_stacked(dmix[:, 128 * p:128 * p + 128], p // 2, lo)
            dost_ref[2 * p] = even.astype(BF16)
            dost_ref[2 * p + 1] = odd.astype(BF16)
        pooled_v = pooled_ref[...]
        counts = _pool_counts(n_tiles - 1 - i, t)
        for g in range(4):
            cols = slice(128 * g, 128 * g + 128)
            dm = dmix[:, ATTN_WIDTH + 128 * g:ATTN_WIDTH + 128 * g + 128]
            ypre = _dot(pooled_v[:, cols], wp_ref[g], 1, 0)
            dsc_ref[:, cols] += jnp.sum(ypre * dm, axis=0, keepdims=True)
            dyp = (dm * sc_ref[:, cols]).astype(BF16)
            dwp_ref[g] += _dot(pooled_v[:, cols], dyp, 0, 0)
            dpooled = _dot(dyp, wp_ref[g], 1, 1)
            du_ref[:, cols] = -dpooled
            ext_ref[pl.ds(0, t), cols] = dpooled / counts[:, cols]
        st_ref[pl.ds(0, n), :] = ext_ref[pl.ds(0, n), :] + ext_ref[pl.ds(1, n), :]
        st_ref[pl.ds(0, n), 128:] = st_ref[pl.ds(0, n), 128:] + st_ref[pl.ds(2, n), 128:]
        st_ref[pl.ds(0, n), 256:] = st_ref[pl.ds(0, n), 256:] + st_ref[pl.ds(4, n), 256:]
        st_ref[pl.ds(0, n), 384:] = st_ref[pl.ds(0, n), 384:] + st_ref[pl.ds(8, n), 384:]
        ext_ref[pl.ds(t, POOL_HALO), :] = ext_ref[pl.ds(0, POOL_HALO), :]
        du_ref[...] += st_ref[pl.ds(0, t), :]

    rev = lambda w: pl.BlockSpec((t, w), lambda i: (n_tiles - 1 - i, 0))
    return pl.pallas_call(
        body, name="mix_out_bwd", grid=(n_tiles,),
        in_specs=[rev(D_MODEL)] + W_SPECS + [rev(POOL_WIDTH), _full((4, 128, 128)), _full((1, POOL_WIDTH)), rev(D_MODEL), ANY],
        out_specs=[pl.BlockSpec((N_Q_HEADS, t, 128), lambda i: (0, n_tiles - 1 - i, 0)), rev(POOL_WIDTH),
                   _full((4, 128, 128)), _full((1, POOL_WIDTH)), ANY],
        out_shape=[jax.ShapeDtypeStruct((N_Q_HEADS, s_len, 128), BF16), jax.ShapeDtypeStruct((s_len, POOL_WIDTH), F32),
                   jax.ShapeDtypeStruct((4, 128, 128), F32), jax.ShapeDtypeStruct((1, POOL_WIDTH), F32),
                   jax.ShapeDtypeStruct((N_CHIPS, early_rows, D_MODEL), BF16)],
        scratch_shapes=[pltpu.VMEM((D_MODEL, D_MODEL), BF16), pltpu.VMEM((t + POOL_HALO, POOL_WIDTH), F32),
                        pltpu.VMEM((t + POOL_HALO, POOL_WIDTH), F32), pltpu.VMEM((D_MODEL, D_MODEL), F32),
                        pltpu.VMEM((D_MODEL, D_MODEL), BF16), pltpu.SemaphoreType.DMA((N_CHIPS,))],
        compiler_params=_params(),
    )(dh1, *wts, pooled, wpool, pool_scale, mix, after)


def _attn_bwd(qst, kn, vb, dost, bias_st, sinks, after):
    s_len = kn.shape[0]

    def body(q_ref, kp_ref, kc_ref, vp_ref, vc_ref, do_ref, bias_ref, sink_ref, after_ref, dq_ref, dk_ref, dv_ref, dbias_ref,
             dsink_ref, s_ref, dp_ref, p_ref, dl_ref):
        del after_ref
        i = pl.program_id(0)

        @pl.when(i == 0)
        def _():
            dk_ref[...] = jnp.zeros_like(dk_ref)
            dv_ref[...] = jnp.zeros_like(dv_ref)
            dbias_ref[...] = jnp.zeros_like(dbias_ref)
            dsink_ref[...] = jnp.zeros_like(dsink_ref)

        for b, (rows, k2, v2, bias) in enumerate(_step_blocks(i, kp_ref, kc_ref, vp_ref, vc_ref, bias_ref)):
            s_b, dp_b, p_b, dl_b = s_ref.at[b], dp_ref.at[b], p_ref.at[b], dl_ref.at[b]
            q = q_ref[:, rows, :].reshape(N_Q_HEADS * BLOCK, 128)
            do = do_ref[:, rows, :].reshape(N_Q_HEADS * BLOCK, 128)
            s_b[...] = _dot(q, k2, 1, 1)
            dp_b[...] = _dot(do, v2, 1, 1)

            def head(h, carry):
                head_rows, probs, p_sink = _head_softmax(s_b, bias, sink_ref, h)
                dp = dp_b[head_rows, :]
                dsum = jnp.sum(probs * dp, axis=-1, keepdims=True)
                dlog = probs * (dp - dsum)
                dsink_ref[head_rows, :] -= p_sink * dsum
                dbias_ref[head_rows, :] += dlog
                p_b[head_rows, :] = probs.astype(BF16)
                dl_b[head_rows, :] = (dlog * (HEAD_DIM ** -0.5)).astype(BF16)
                return carry

            lax.fori_loop(0, N_Q_HEADS, head, 0, unroll=True)
            dlog_s = dl_b[...]
            dq_ref[:, rows, :] = jnp.where(_head_lane_mask(), _dot(dlog_s, k2, 1, 0), 0.0).reshape(N_Q_HEADS, BLOCK, 128)
            dk2 = _dot(dlog_s, q, 0, 0)
            dv2 = _dot(p_b[...], do, 0, 0)
            block = ATTN_STEP_BLOCKS * i + b
            prev_rows = pl.ds(pl.multiple_of(jnp.maximum(block - 1, 0) * BLOCK, BLOCK), BLOCK)
            cur_rows = pl.ds(pl.multiple_of(block * BLOCK, BLOCK), BLOCK)
            dk_ref[prev_rows, :] += dk2[:BLOCK]
            dk_ref[cur_rows, :] += dk2[BLOCK:]
            dv_ref[prev_rows, :] += dv2[:BLOCK]
            dv_ref[cur_rows, :] += dv2[BLOCK:]

    stacked, kv, consts = _attn_specs()
    per_step = (ATTN_STEP_BLOCKS,) + BAND
    return pl.pallas_call(
        body, name="attn_bwd", grid=(s_len // (ATTN_STEP_BLOCKS * BLOCK),),
        in_specs=[stacked] + kv + kv + [stacked] + consts + [ANY],
        out_specs=[stacked, _full((s_len, 128)), _full((s_len, 128)), _full(BAND), _full((N_Q_HEADS * BLOCK, 1))],
        out_shape=[jax.ShapeDtypeStruct((N_Q_HEADS, s_len, 128), F32), jax.ShapeDtypeStruct((s_len, 128), F32),
                   jax.ShapeDtypeStruct((s_len, 128), F32), jax.ShapeDtypeStruct(BAND, F32),
                   jax.ShapeDtypeStruct((N_Q_HEADS * BLOCK, 1), F32)],
        scratch_shapes=[pltpu.VMEM(per_step, F32), pltpu.VMEM(per_step, F32), pltpu.VMEM(per_step, BF16),
                        pltpu.VMEM(per_step, BF16)],
        compiler_params=_params(),
    )(qst, kn, kn, vb, vb, dost, bias_st, sinks, after)


def _small_pack(dg_attn, dg_ffn, dg_ple, dscale, dgq, dgk, dbias, dsink_rows, bucket, loss_v, dwpool):
    def body(ga_ref, gf_ref, gp_ref, sc_ref, gq_ref, gk_ref, db_ref, ds_ref, bucket_ref, loss_ref, wp_ref, out_ref):
        out_ref[pl.ds(0, SMALL["w_pool"]), :] = jnp.zeros((SMALL["w_pool"], 128), F32)
        for name, ref, n in (("g_attn", ga_ref, 8), ("g_ffn", gf_ref, 8), ("g_ple", gp_ref, 8), ("pool_scale", sc_ref, 4)):
            for k in range(n):
                out_ref[pl.ds(SMALL[name] + k, 1), :] = ref[:, 128 * k:128 * k + 128]
        for name, ref in (("g_q", gq_ref), ("g_k", gk_ref)):
            both = ref[...]
            out_ref[pl.ds(SMALL[name], 1), :] = both + pltpu.roll(both, 64, axis=1)
        out_ref[pl.ds(SMALL["loss"], 1), :] = loss_ref[...]
        bk = bucket_ref[...]
        rows = lax.broadcasted_iota(jnp.int32, (N_BUCKETS, 128), 0)
        lanes = lax.broadcasted_iota(jnp.int32, (N_BUCKETS, 128), 1)
        lane1 = lax.broadcasted_iota(jnp.int32, (1, 128), 1)
        rb = jnp.zeros((N_BUCKETS, 128), F32)
        sk = jnp.zeros((1, 128), F32)
        for h in range(N_Q_HEADS):
            band = db_ref[pl.ds(h * BLOCK, BLOCK), :]
            for b in range(N_BUCKETS):
                rb = jnp.where((rows == b) & (lanes == h), jnp.sum(jnp.where(bk == b, band, 0.0)), rb)
            sk = jnp.where(lane1 == h, jnp.sum(ds_ref[pl.ds(h * BLOCK, BLOCK), :]), sk)
        out_ref[pl.ds(SMALL["rel_bias"], N_BUCKETS), :] = rb
        out_ref[pl.ds(SMALL["sinks"], 1), :] = sk
        out_ref[pl.ds(SMALL["w_pool"], 512), :] = wp_ref[...].reshape(512, 128)

    return pl.pallas_call(
        body, name="small_pack", in_specs=[VMEM_WHOLE] * 11, out_specs=VMEM_WHOLE,
        out_shape=jax.ShapeDtypeStruct((SMALL_ROWS, 128), F32),
    )(dg_attn, dg_ffn, dg_ple, dscale, dgq, dgk, dbias, dsink_rows, bucket, loss_v, dwpool)


def _attn_in_bwd(dqst, zqk, dk, dv, du, x2, dh1, hn1, slab, wts, g_attn, gq, gk):
    s_len = x2.shape[0]
    t = 512
    n_tiles = s_len // t

    def body(dq_ref, zqk_ref, dk_ref, dv_ref, du_ref, x_ref, dh1_ref, hn_ref, slab_in_ref, sl_ref, lo_ref, me_ref, g_ref,
             gq_ref, gk_ref, dx_ref, dg_ref, dgq_ref, dgk_ref, slab_ref, w_ref, dz_ref, acc_ref, stage_ref, sems):
        del slab_in_ref
        i = pl.program_id(0)

        @pl.when(i == 0)
        def _():
            _load_rows((sl_ref, lo_ref, me_ref), "inT", w_ref, sems)
            dg_ref[...] = jnp.zeros_like(dg_ref)
            dgq_ref[...] = jnp.zeros_like(dgq_ref)
            dgk_ref[...] = jnp.zeros_like(dgk_ref)

        lo = lax.broadcasted_iota(jnp.int32, (t, 128), 1) < 64
        for p in range(4):
            dqn = _from_stacked(dq_ref[2 * p], dq_ref[2 * p + 1], p // 2, lo)
            dq_raw, dgq = _pair_norm_bwd(zqk_ref[:, 128 * p:128 * p + 128], gq_ref[...], dqn)
            dz_ref[:, 128 * p:128 * p + 128] = dq_raw.astype(BF16)
            dgq_ref[...] += dgq
        dk_raw, dgk = _pair_norm_bwd(zqk_ref[:, 512:640], gk_ref[...], dk_ref[...])
        dgk_ref[...] += dgk
        dz_ref[:, 512:640] = dk_raw.astype(BF16)
        dz_ref[:, 640:768] = dv_ref[...].astype(BF16)
        dz_ref[:, 768:] = du_ref[...].astype(BF16)
        dz = dz_ref[...]
        _accumulate_tn(acc_ref, dz, hn_ref[...], i == 0)
        dx, dg = _rms_bwd(x_ref[...], g_ref[...], _dot(dz, w_ref[...], 1, 0))
        dx_ref[...] = dh1_ref[...] + dx
        dg_ref[...] += dg

        @pl.when(i == n_tiles - 1)
        def _():
            _flush_chunks(acc_ref, stage_ref, slab_ref, "inT", sems)

    row = lambda w: pl.BlockSpec((t, w), lambda i: (i, 0))
    return pl.pallas_call(
        body, name="attn_in_bwd", grid=(n_tiles,),
        in_specs=[pl.BlockSpec((N_Q_HEADS, t, 128), lambda i: (0, i, 0)), row(640), row(128), row(128), row(POOL_WIDTH),
                  row(D_MODEL), row(D_MODEL), row(D_MODEL), ANY] + W_SPECS + [_full((1, D_MODEL)), _full((1, 128)),
                                                                              _full((1, 128))],
        out_specs=[row(D_MODEL), _full((1, D_MODEL)), _full((1, 128)), _full((1, 128)), ANY],
        out_shape=[jax.ShapeDtypeStruct((s_len, D_MODEL), F32), jax.ShapeDtypeStruct((1, D_MODEL), F32),
                   jax.ShapeDtypeStruct((1, 128), F32), jax.ShapeDtypeStruct((1, 128), F32),
                   jax.ShapeDtypeStruct(slab.shape, BF16)],
        input_output_aliases={8: 4},
        scratch_shapes=[pltpu.VMEM((IN_WIDTH, D_MODEL), BF16), pltpu.VMEM((t, IN_WIDTH), BF16),
                        pltpu.VMEM((IN_WIDTH, D_MODEL), F32), pltpu.VMEM((IN_WIDTH, D_MODEL), BF16),
                        pltpu.SemaphoreType.DMA((N_CHIPS,))],
        compiler_params=_params(),
    )(dqst, zqk, dk, dv, du, x2, dh1, hn1, slab, *wts, g_attn, gq, gk)


def _dw(lefts, b, name, slab, slab_rows, row_offs):
    a0, n_a = lefts[0], len(lefts)
    assert b.shape[1] == D_MODEL
    if a0.ndim == 3:
        n_chunks, s_len, tm = a0.shape
        m = n_chunks * tm
    else:
        s_len, tm = a0.shape
        m = tm
    tk = 2048 if n_a * tm <= 1408 else 1024
    if a0.ndim == 3:
        a_spec = pl.BlockSpec((None, tk, tm), lambda i, k: (i, k, 0))
    else:
        a_spec = pl.BlockSpec((tk, tm), lambda i, k: (k, i))
    n_steps, n_tiles = s_len // tk, m // tm
    chunk = m // N_CHIPS
    per_tile = tm // chunk

    def body(*refs):
        a_refs, b_ref = refs[:n_a], refs[n_a]
        o_ref, acc_ref, stage_ref, sems = refs[-4:]
        i, k = pl.program_id(0), pl.program_id(1)
        b_tile = b_ref[...].astype(BF16)
        for w, a_ref in enumerate(a_refs):
            _accumulate_tn(acc_ref.at[w], a_ref[...].astype(BF16), b_tile, k == 0)

        def out_copies(tile, slot):
            return [pltpu.make_async_copy(stage_ref.at[slot, w, pl.ds(jj * chunk, chunk), :],
                                          o_ref.at[tile * per_tile + jj, pl.ds(row_offs[w], chunk), :], sems.at[slot, w, jj])
                    for w in range(n_a) for jj in range(per_tile)]

        @pl.when(k == n_steps - 1)
        def _():
            slot = i % 2

            @pl.when(i >= 2)
            def _():
                for cp in out_copies(i - 2, slot):
                    cp.wait()

            stage_ref[slot] = acc_ref[...].astype(BF16)
            for cp in out_copies(i, slot):
                cp.start()

            @pl.when(i == n_tiles - 1)
            def _():
                for cp in out_copies(i, slot):
                    cp.wait()
                if n_tiles > 1:
                    for cp in out_copies(i - 1, 1 - slot):
                        cp.wait()

    in_specs = [a_spec] * n_a + [pl.BlockSpec((tk, D_MODEL), lambda i, k: (k, 0))]
    operands, aliases = [*lefts, b], {}
    if slab is not None:
        in_specs.append(ANY)
        operands.append(slab)
        aliases = {n_a + 1: 0}
    return pl.pallas_call(
        body, name=name, grid=(n_tiles, n_steps), in_specs=in_specs, out_specs=ANY,
        out_shape=jax.ShapeDtypeStruct((N_CHIPS, slab_rows, D_MODEL), BF16), input_output_aliases=aliases,
        scratch_shapes=[pltpu.VMEM((n_a, tm, D_MODEL), F32), pltpu.VMEM((2, n_a, tm, D_MODEL), BF16),
                        pltpu.SemaphoreType.DMA((2, n_a, per_tile))],
        compiler_params=_params(VMEM_LIMIT_BIG, n_axes=2),
    )(*operands)


def _position():
    x, y, c = lax.axis_index("x"), lax.axis_index("y"), lax.axis_index("c")
    other_chips = [(1 - x, y), (x, 1 - y), (1 - x, 1 - y)]
    return x, y, c, other_chips


def _ag_weights(local_slab, row0, n_rows, name, collective_id):
    half = n_rows // 2
    quarter = half // 2
    assert quarter % 16 == 0

    def body(l_ref, g_ref, send, recv):
        x, y, c, chips = _position()
        me, (via_x, via_y, diagonal) = 2 * x + y, [2 * chip[0] + chip[1] for chip in chips]
        here, sibling, x_nbr, y_nbr = (x, y, c), (x, y, 1 - c), (1 - x, y, c), (x, 1 - y, c)
        peers = [sibling, x_nbr, y_nbr]
        barrier = pltpu.get_barrier_semaphore()
        for peer in peers:
            pl.semaphore_signal(barrier, inc=1, device_id=peer, device_id_type=MESH)
        pl.semaphore_wait(barrier, len(peers))

        def rows(core, part):
            start, size = (core * half, half) if part is None else (core * half + part * quarter, quarter)
            return pl.ds(pl.multiple_of(start, 16), size)

        def copy(k, chip_idx, where, to, src=None):
            dst = g_ref.at[chip_idx, where, :]
            return pltpu.make_async_remote_copy(src_ref=dst if src is None else src, dst_ref=dst, send_sem=send.at[k],
                                                recv_sem=recv.at[k], device_id=to, device_id_type=MESH)

        own_rows = l_ref.at[pl.ds(pl.multiple_of(row0 + c * half, 16), half), :]
        started = [copy(0, me, rows(c, None), x_nbr, src=own_rows), copy(1, me, rows(c, None), y_nbr, src=own_rows)]
        for cp in started:
            cp.start()
        after_arrival = [
            (copy(0, via_x, rows(c, None), here), [copy(4, via_x, rows(c, None), sibling), copy(3, via_x, rows(c, 1), y_nbr)]),
            (copy(1, via_y, rows(c, None), here), [copy(5, via_y, rows(c, None), sibling), copy(2, via_y, rows(c, 0), x_nbr)]),
            (copy(2, diagonal, rows(c, 0), here), [copy(6, diagonal, rows(c, 0), sibling)]),
            (copy(3, diagonal, rows(c, 1), here), [copy(7, diagonal, rows(c, 1), sibling)]),
        ]
        for arrival, onward in after_arrival:
            arrival.wait_recv()
            for cp in onward:
                cp.start()
            started += onward
        for cp in (copy(4, via_x, rows(1 - c, None), here), copy(5, via_y, rows(1 - c, None), here),
                   copy(6, diagonal, rows(1 - c, 0), here), copy(7, diagonal, rows(1 - c, 1), here)):
            cp.wait_recv()
        for cp in started:
            cp.wait_send()

    return pl.kernel(
        body, out_type=jax.ShapeDtypeStruct((N_CHIPS, n_rows, D_MODEL), BF16),
        mesh=plsc.ScalarSubcoreMesh(axis_name="sequencer", num_cores=1), name=name,
        scratch_types=[pltpu.SemaphoreType.DMA((8,)), pltpu.SemaphoreType.DMA((8,))],
        compiler_params=pltpu.CompilerParams(collective_id=collective_id),
    )(local_slab)


def _comm_call(body, peers_of, out_shape, n_sems, operand, name, collective_id):
    sems = [pltpu.SemaphoreType.DMA((n_sems,)), pltpu.SemaphoreType.DMA((n_sems,))]
    if collective_id is None:
        return pl.pallas_call(body, name=name, in_specs=[ANY], out_specs=ANY, out_shape=out_shape, scratch_shapes=sems)(operand)

    def with_handshake(in_ref, out_ref, send, recv):
        x, y, c, _ = _position()
        peers = peers_of(x, y, c)
        barrier = pltpu.get_barrier_semaphore()
        for peer in peers:
            pl.semaphore_signal(barrier, inc=1, device_id=peer, device_id_type=MESH)
        pl.semaphore_wait(barrier, len(peers))
        body(in_ref, out_ref, send, recv)

    return pl.kernel(with_handshake, out_type=out_shape, mesh=plsc.ScalarSubcoreMesh(axis_name="sequencer", num_cores=1),
                     name=name, scratch_types=sems, compiler_params=pltpu.CompilerParams(collective_id=collective_id))(operand)


def _rs_swap_halves(partial, name, collective_id=None):
    half = partial.shape[1] // 2

    def body(p_ref, r_ref, send, recv):
        x, y, c, _ = _position()
        theirs = pl.ds(pl.multiple_of((1 - c) * half, 16), half)
        cp = pltpu.make_async_remote_copy(src_ref=p_ref.at[:, theirs, :], dst_ref=r_ref, send_sem=send.at[0],
                                          recv_sem=recv.at[0], device_id=(x, y, 1 - c), device_id_type=MESH)
        cp.start()
        cp.wait()

    return _comm_call(body, lambda x, y, c: [(x, y, 1 - c)], jax.ShapeDtypeStruct((N_CHIPS, half, D_MODEL), BF16), 1,
                      partial, name, collective_id)


def _rs_add_halves(partial, other, core, name, after):
    half = other.shape[1]
    t = half // 2
    steps = half // t

    def body(core_ref, a_ref, b_ref, after_ref, o_ref):
        del after_ref
        o_ref[...] = (a_ref[...].astype(F32) + b_ref[...].astype(F32)).astype(BF16)

    return pl.pallas_call(
        body, name=name,
        grid_spec=pltpu.PrefetchScalarGridSpec(
            num_scalar_prefetch=1, grid=(N_CHIPS, steps),
            in_specs=[pl.BlockSpec((1, t, D_MODEL), lambda j, i, core_ref: (j, core_ref[0] * steps + i, 0)),
                      pl.BlockSpec((1, t, D_MODEL), lambda j, i, core_ref: (j, i, 0)), ANY],
            out_specs=pl.BlockSpec((1, t, D_MODEL), lambda j, i, core_ref: (j, i, 0))),
        out_shape=jax.ShapeDtypeStruct((N_CHIPS, half, D_MODEL), BF16),
        compiler_params=_params(n_axes=2),
    )(core, partial, other, after)


def _rs_exchange_chips(pre, name, collective_id=None):
    def body(s_ref, r_ref, send, recv):
        x, y, c, chips = _position()

        def copy(k, chunk, to):
            return pltpu.make_async_remote_copy(src_ref=s_ref.at[chunk], dst_ref=r_ref.at[k], send_sem=send.at[k],
                                                recv_sem=recv.at[k], device_id=to, device_id_type=MESH)

        sends = [copy(k, 2 * chip[0] + chip[1], (*chip, c)) for k, chip in enumerate(chips)]
        for cp in sends:
            cp.start()
        for cp in sends:
            cp.wait()

    return _comm_call(body, lambda x, y, c: [(1 - x, y, c), (x, 1 - y, c), (1 - x, 1 - y, c)],
                      jax.ShapeDtypeStruct((3, pre.shape[1], D_MODEL), BF16), 3, pre, name, collective_id)


def _rs_sum_chips(pre, received, place, name, after):
    half = pre.shape[1]
    t = half // 2 if half > 512 else half
    steps = half // t

    def body(place_ref, own_ref, r_ref, after_ref, o_ref):
        del after_ref
        acc = own_ref[0].astype(F32)
        for k in range(3):
            acc = acc + r_ref[k].astype(F32)
        o_ref[...] = acc

    return pl.pallas_call(
        body, name=name,
        grid_spec=pltpu.PrefetchScalarGridSpec(
            num_scalar_prefetch=1, grid=(steps,),
            in_specs=[pl.BlockSpec((1, t, D_MODEL), lambda i, place_ref: (place_ref[0], i, 0)),
                      pl.BlockSpec((3, t, D_MODEL), lambda i, place_ref: (0, i, 0)), ANY],
            out_specs=pl.BlockSpec((t, D_MODEL), lambda i, place_ref: (place_ref[1] * steps + i, 0))),
        out_shape=jax.ShapeDtypeStruct((2 * half, D_MODEL), F32),
        compiler_params=_params(),
    )(place, pre, received, after)


def _half_swap(g_ref, core, to, send, recv, k):
    half = g_ref.shape[0] // 2
    rows = g_ref.at[pl.ds(pl.multiple_of(core * half, 8), half), :]
    return pltpu.make_async_remote_copy(src_ref=rows, dst_ref=rows, send_sem=send.at[k], recv_sem=recv.at[k],
                                        device_id=to, device_id_type=MESH)


def _rs_finish_rows(grads, name, after):
    def body(f_ref, after_ref, g_ref, send, recv):
        del f_ref, after_ref
        x, y, c, _ = _position()
        mine = _half_swap(g_ref, c, (x, y, 1 - c), send, recv, 0)
        mine.start()
        _half_swap(g_ref, 1 - c, (x, y, c), send, recv, 0).wait_recv()
        mine.wait_send()

    return pl.pallas_call(
        body, name=name, in_specs=[ANY, ANY], out_specs=ANY, input_output_aliases={0: 0},
        out_shape=jax.ShapeDtypeStruct(grads.shape, F32),
        scratch_shapes=[pltpu.SemaphoreType.DMA((1,)), pltpu.SemaphoreType.DMA((1,))],
    )(grads, after)


def _small_gather(small, collective_id):
    def body(s_ref, t_ref, send, recv):
        x, y, c, chips = _position()
        sibling = (x, y, 1 - c)

        def slot(px, py, pc):
            return t_ref.at[4 * px + 2 * py + pc]

        def copy(k, block, to, src=None):
            return pltpu.make_async_remote_copy(src_ref=slot(*block) if src is None else src, dst_ref=slot(*block),
                                                send_sem=send.at[k], recv_sem=recv.at[k], device_id=to, device_id_type=MESH)

        own = pltpu.make_async_copy(s_ref, slot(x, y, c), send.at[7])
        own.start()
        first = [copy(0, (x, y, c), sibling, src=s_ref)]
        first += [copy(1 + k, (x, y, c), (*chip, c), src=s_ref) for k, chip in enumerate(chips)]
        for cp in first:
            cp.start()
        passed = []
        for k, chip in enumerate(chips):
            copy(1 + k, (*chip, c), (x, y, c)).wait_recv()
            fwd = copy(4 + k, (*chip, c), sibling)
            fwd.start()
            passed.append(fwd)
        copy(0, sibling, (x, y, c)).wait_recv()
        for k, chip in enumerate(chips):
            copy(4 + k, (*chip, 1 - c), (x, y, c)).wait_recv()
        for cp in first + passed:
            cp.wait_send()
        own.wait()

    peers_of = lambda x, y, c: [(x, y, 1 - c), (1 - x, y, c), (x, 1 - y, c), (1 - x, 1 - y, c)]
    return _comm_call(body, peers_of, jax.ShapeDtypeStruct((N_DEV, SMALL_ROWS, 128), F32), 8, small, "small_gather",
                      collective_id)


def _adam_update(w, g, m, v):
    m_new = ADAM_B1 * m + (1.0 - ADAM_B1) * g
    v_new = ADAM_B2 * v + (1.0 - ADAM_B2) * (g * g)
    m_hat = m_new / (1.0 - ADAM_B1 ** ADAM_STEP)
    v_hat = v_new / (1.0 - ADAM_B2 ** ADAM_STEP)
    return -ADAM_LR * (m_hat / (jnp.sqrt(v_hat) + ADAM_EPS) + ADAM_WD * w), m_new, v_new


def _adamw(w, g_rows, row_off, m, v, name):
    rows, cols = w.shape
    t = rows if rows <= 320 else (rows // 2 if rows % 256 else 256)

    def body(w_ref, g_ref, m_ref, v_ref, go_ref, d_ref, nm_ref, nv_ref):
        g = g_ref[...]
        go_ref[...] = g
        d_ref[...], nm_ref[...], nv_ref[...] = _adam_update(w_ref[...], g, m_ref[...], v_ref[...])

    blk = pl.BlockSpec((t, cols), lambda i: (i, 0))
    assert row_off % 8 == 0 and t % 8 == 0
    g_blk = pl.BlockSpec((pl.Element(t), pl.Element(cols)), lambda i: (pl.multiple_of(row_off + i * t, 8), 0))
    shape = jax.ShapeDtypeStruct((rows, cols), F32)
    return pl.pallas_call(
        body, name=name, grid=(rows // t,), in_specs=[blk, g_blk, blk, blk], out_specs=[blk] * 4, out_shape=[shape] * 4,
        compiler_params=_params(),
    )(w, g_rows, m, v)


SMALL_PARAMS = [("g_attn", (1, D_MODEL), 8), ("g_q", (1, HEAD_DIM), None), ("g_k", (1, HEAD_DIM), None),
                ("sinks", (1, N_Q_HEADS), None), ("rel_bias", (N_BUCKETS, N_Q_HEADS), None), ("w_pool", (512, 128), None),
                ("pool_scale", (1, POOL_WIDTH), 4), ("g_ffn", (1, D_MODEL), 8), ("g_ple", (1, D_MODEL), 8)]


def _adamw_small(tables, wmv):
    n_par = len(SMALL_PARAMS)

    def body(*refs):
        t_ref = refs[0]
        ins = refs[1:1 + 3 * n_par]
        loss_ref = refs[1 + 3 * n_par]
        outs = refs[2 + 3 * n_par:-1]
        tot_ref = refs[-1]
        total = t_ref[0]
        for d in range(1, N_DEV):
            total = total + t_ref[d]
        tot_ref[...] = total
        loss_ref[...] = tot_ref[pl.ds(SMALL["loss"], 1), 0:1]
        for i, (name, shape, split) in enumerate(SMALL_PARAMS):
            g_ref, d_ref, nm_ref, nv_ref = outs[4 * i:4 * i + 4]
            row = SMALL[name]
            if split:
                for k in range(split):
                    g_ref[:, 128 * k:128 * k + 128] = tot_ref[pl.ds(row + k, 1), :]
            else:
                g_ref[...] = tot_ref[pl.ds(row, shape[0]), 0:shape[1]]
            w_ref, m_ref, v_ref = ins[3 * i:3 * i + 3]
            d_ref[...], nm_ref[...], nv_ref[...] = _adam_update(w_ref[...], g_ref[...], m_ref[...], v_ref[...])

    shapes = [jax.ShapeDtypeStruct((1, 1), F32)]
    for _, shape, _ in SMALL_PARAMS:
        shapes += [jax.ShapeDtypeStruct(shape, F32)] * 4
    flat = [a for triple in wmv for a in triple]
    res = pl.pallas_call(
        body, name="adamw_small", in_specs=[VMEM_WHOLE] * (1 + 3 * n_par), out_specs=[VMEM_WHOLE] * len(shapes),
        out_shape=shapes, scratch_shapes=[pltpu.VMEM((SMALL_ROWS, 128), F32)],
    )(tables, *flat)
    return res[0], [res[1 + 4 * i:5 + 4 * i] for i in range(n_par)]


def _pack_ple_proj(shard):
    return shard.reshape(4, 64, 256).transpose(1, 0, 2).reshape(64, D_MODEL)


class _Reduction:
    def __init__(self, tag, place, ids=(None, None)):
        self.tag, self.place, self.ids = tag, place, ids

    def start(self, partial):
        self.partial = partial
        self.other = _rs_swap_halves(partial, "rs_swap_" + self.tag, self.ids[0])
        return partial

    def middle(self, after):
        self.pre = _rs_add_halves(self.partial, self.other, self.place[1:], "rs_add_" + self.tag, after)
        self.received = _rs_exchange_chips(self.pre, "rs_exchange_" + self.tag, self.ids[1])
        return self.pre

    def finish(self, after):
        return _rs_sum_chips(self.pre, self.received, self.place, "rs_sum_" + self.tag, after)


def _local_grads(x2, p2, tgt, wts, g_attn_norm, g_q, g_k, attn_sinks, rel_bias, w_pool, pool_scale, g_ffn_norm, g_ple_norm,
                 reduce_a):
    w_early, w_late = wts
    w_in = w_out = w_early
    bucket = jnp.asarray(_bucket_table())
    gq = jnp.tile(g_q, (1, 2))
    gk = jnp.tile(g_k, (1, 2))
    wpool = w_pool[0].astype(BF16)
    sinks = attn_sinks[0]
    bias_st = _bias_build(rel_bias.T, bucket)

    hn1, zqk, u, kn, vb, qst = _attn_in(x2, g_attn_norm, gq, gk, w_in)
    ost = _attn_fwd(qst, kn, vb, bias_st, sinks)
    pooled, mix, h1, hn2 = _mix_out(u, ost, x2, w_out, wpool, pool_scale, g_ffn_norm)
    loss_v, dgate, dup, act, dh2, hn3, dgl, dw_plp, dh1, dg_ffn, dg_ple = _ffn_ple(hn2, h1, p2, tgt, w_late, g_ffn_norm,
                                                                                      g_ple_norm)

    late0, late_rows = GATHER_PARTS[1][0], SLAB_ROWS - GATHER_PARTS[1][0]
    partial_a = None
    for names, lefts, right in ((("gateT", "upT"), [dgate, dup], hn2), (("down",), [act], dh2), (("plg",), [hn3], dgl)):
        partial_a = _dw(lefts, right, "dw_" + names[0], partial_a, late_rows, [SLAB[name][0] - late0 for name in names])
    dw_plp = dw_plp.reshape(4, 64, N_CHIPS, 256).transpose(2, 1, 0, 3).reshape(N_CHIPS, 64, D_MODEL)
    partial_a = reduce_a.start(lax.dynamic_update_slice(partial_a, dw_plp, (0, SLAB["plp"][0] - late0, 0)))
    dost, du, dw_pool, dscale, partial_b = _mix_out_bwd(dh1, w_out, pooled, wpool, pool_scale, mix, partial_a)
    pre_a = reduce_a.middle(du)
    dqst, dk, dv, dbias, dsink_rows = _attn_bwd(qst, kn, vb, dost, bias_st, sinks, pre_a)
    dx, dg_attn, dgq, dgk, partial_b = _attn_in_bwd(dqst, zqk, dk, dv, du, x2, dh1, hn1, partial_b, w_in, g_attn_norm, gq, gk)

    small = _small_pack(dg_attn, dg_ffn, dg_ple, dscale, dgq, dgk, dbias, dsink_rows, bucket, loss_v, dw_pool)
    return dx, partial_b, small


def kernel(x, p, w_in, w_out, g_attn_norm, g_q, g_k, attn_sinks, rel_bias, w_pool, pool_scale, g_ffn_norm, w_gate, w_up, w_down, g_ple_norm, w_ple_gate, w_ple_proj, loss_target, m_w_in, m_w_out, m_g_attn_norm, m_g_q, m_g_k, m_attn_sinks, m_rel_bias, m_w_pool, m_pool_scale, m_g_ffn_norm, m_w_gate, m_w_up, m_w_down, m_g_ple_norm, m_w_ple_gate, m_w_ple_proj, v_w_in, v_w_out, v_g_attn_norm, v_g_q, v_g_k, v_attn_sinks, v_rel_bias, v_w_pool, v_pool_scale, v_g_ffn_norm, v_w_gate, v_w_up, v_w_down, v_g_ple_norm, v_w_ple_gate, v_w_ple_proj):
    core = lax.axis_index("c").astype(jnp.int32).reshape(1)
    me = (2 * lax.axis_index("x") + lax.axis_index("y")).astype(jnp.int32).reshape(1)

    local_parts = [jnp.concatenate(pieces, axis=0).astype(BF16) for pieces in (
        [w_in[0].T, w_out[0]], [w_gate[0].T, w_up[0].T, w_down[0], w_ple_gate[0], _pack_ple_proj(w_ple_proj[0])])]
    wts = [(_ag_weights(local, 0, local.shape[0], name, collective_id), local, me)
           for local, name, collective_id in zip(local_parts, ("ag_early", "ag_late"), (1, 2))]

    place = jnp.concatenate([me, core])
    reduce_a = _Reduction("a", place, ids=(3, 4))
    dx, partial_b, small = _local_grads(x[0], p[0, 0], loss_target[0], wts, g_attn_norm, g_q, g_k, attn_sinks, rel_bias,
                                        w_pool, pool_scale, g_ffn_norm, g_ple_norm, reduce_a)
    reduce_b = _Reduction("b", place, ids=(6, 7))
    reduce_b.start(partial_b)
    small_all = _small_gather(small, 8)
    summed_a = reduce_a.finish(small)
    pre_b = reduce_b.middle(summed_a)
    grads_a = _rs_finish_rows(summed_a, "rs_finish_a", pre_b)

    late0 = GATHER_PARTS[1][0]

    def rows(name):
        return grads_a, SLAB[name][0] - late0

    plp_rows = grads_a[SLAB["plp"][0] - late0:]
    big = {
        "w_gate": (w_gate, m_w_gate, v_w_gate, rows("gateT"), True),
        "w_up": (w_up, m_w_up, v_w_up, rows("upT"), True),
        "w_down": (w_down, m_w_down, v_w_down, rows("down"), False),
        "w_ple_gate": (w_ple_gate, m_w_ple_gate, v_w_ple_gate, rows("plg"), False),
        "w_ple_proj": (w_ple_proj, m_w_ple_proj, v_w_ple_proj,
                       (plp_rows.reshape(64, 4, 256).transpose(1, 0, 2).reshape(PLE_DIM, PLE_DIM), 0), False),
        "w_out": (w_out, m_w_out, v_w_out, None, False),
        "w_in": (w_in, m_w_in, v_w_in, None, True),
    }
    small_params = {
        "g_attn_norm": (g_attn_norm, m_g_attn_norm, v_g_attn_norm), "g_q": (g_q, m_g_q, v_g_q), "g_k": (g_k, m_g_k, v_g_k),
        "attn_sinks": (attn_sinks, m_attn_sinks, v_attn_sinks), "rel_bias": (rel_bias, m_rel_bias, v_rel_bias),
        "w_pool": tuple(a.reshape(512, 128) for a in (w_pool, m_w_pool, v_w_pool)),
        "pool_scale": (pool_scale, m_pool_scale, v_pool_scale), "g_ffn_norm": (g_ffn_norm, m_g_ffn_norm, v_g_ffn_norm),
        "g_ple_norm": (g_ple_norm, m_g_ple_norm, v_g_ple_norm),
    }

    grads, deltas, new_ms, new_vs = {}, {}, {}, {}
    out = grads_b = None
    for name, (w, m, v, g_src, transposed) in big.items():
        if g_src is None:
            if grads_b is None:
                grads_b = _rs_finish_rows(reduce_b.finish(out[-1]), "rs_finish_b", out[-1])
            g_src = (grads_b, SLAB["out" if name == "w_out" else "inT"][0])
        view = (lambda a: a.T) if transposed else (lambda a: a)
        out = _adamw(view(w[0]), *g_src, view(m[0]), view(v[0]), "adamw_" + name)
        grads[name], deltas[name], new_ms[name], new_vs[name] = (view(a)[None] for a in out)

    loss, small_out = _adamw_small(small_all, list(small_params.values()))
    for name, (g2, d, nm, nv) in zip(small_params, small_out):
        shape = w_pool.shape if name == "w_pool" else g2.shape
        grads[name], deltas[name], new_ms[name], new_vs[name] = (a.reshape(shape) for a in (g2, d, nm, nv))

    order = ["w_in", "w_out", "g_attn_norm", "g_q", "g_k", "attn_sinks", "rel_bias", "w_pool", "pool_scale", "g_ffn_norm",
             "w_gate", "w_up", "w_down", "g_ple_norm", "w_ple_gate", "w_ple_proj"]
    return (loss.reshape(()), dx[None], *[grads[n] for n in order], *[deltas[n] for n in order],
            *[new_ms[n] for n in order], *[new_vs[n] for n in order])
```

```python
import numpy as np
import jax
import jax.numpy as jnp
from jax import lax
from jax.experimental import pallas as pl
from jax.experimental.pallas import tpu as pltpu
from jax.experimental.pallas import tpu_sc as plsc

F32 = jnp.float32
BF16 = jnp.bfloat16
MESH = pl.DeviceIdType.MESH

D_MODEL = 1024
HEAD_DIM = 64
N_Q_HEADS = 8
ATTN_WIDTH = 512
POOL_WIDTH = 512
IN_WIDTH = 1280
D_FF = 2816
PLE_DIM = 256
FF_CHUNK = 1408
N_FF_CHUNKS = D_FF // FF_CHUNK
BLOCK = 128
N_BUCKETS = 32
MAX_DISTANCE = 128
EPS = 1e-6
NEG = -1e30
N_CHIPS = 4
N_DEV = 8

ADAM_LR = 0.001
ADAM_B1 = 0.9
ADAM_B2 = 0.999
ADAM_EPS = 1e-08
ADAM_WD = 0.01
ADAM_STEP = 10

SLAB = {"inT": (0, 320), "out": (320, 256), "gateT": (576, 704), "upT": (1280, 704), "down": (1984, 704),
        "plg": (2688, 256), "plp": (2944, 64)}
SLAB_ROWS = 3008
GATHER_PARTS = ((0, 576), (576, SLAB_ROWS))
POOL_HALO = 24

SMALL = {"g_attn": 0, "g_ffn": 8, "g_ple": 16, "pool_scale": 24, "g_q": 28, "g_k": 29, "sinks": 30, "loss": 31,
         "rel_bias": 32, "w_pool": 64}
SMALL_ROWS = 576

VMEM_LIMIT_BIG = 60 * 1024 * 1024
VMEM_LIMIT = 48 * 1024 * 1024


def _params(vmem=VMEM_LIMIT, n_axes=1):
    return pltpu.CompilerParams(dimension_semantics=("arbitrary",) * n_axes, vmem_limit_bytes=vmem)


def _dot(a, b, ca, cb):
    return lax.dot_general(a, b, (((ca,), (cb,)), ((), ())), preferred_element_type=F32)


def _full(shape):
    return pl.BlockSpec(shape, lambda i: (0,) * len(shape))


ANY = pl.BlockSpec(memory_space=pl.ANY)
VMEM_WHOLE = pl.BlockSpec(memory_space=pltpu.VMEM)


W_SPECS = [ANY, ANY, pl.BlockSpec(memory_space=pltpu.SMEM)]


def _load_rows(w_refs, name, dst_ref, sems):
    slab_ref, local_ref, me_ref = w_refs
    off, rows = SLAB[name]
    slab_off = off - max(start for start, _ in GATHER_PARTS if start <= off)
    me = me_ref[0]
    for phase in ("start", "wait"):
        for j in range(N_CHIPS):
            dst = dst_ref.at[pl.ds(j * rows, rows), :]
            theirs = pltpu.make_async_copy(slab_ref.at[j, pl.ds(slab_off, rows), :], dst, sems.at[j])
            own = pltpu.make_async_copy(local_ref.at[pl.ds(slab_off, rows), :], dst, sems.at[j])

            @pl.when(me == j)
            def _():
                getattr(own, phase)()

            @pl.when(me != j)
            def _():
                getattr(theirs, phase)()


def _rms_fwd(x, g):
    r = lax.rsqrt(jnp.mean(x * x, axis=-1, keepdims=True) + EPS)
    return x * r * g


def _rms_bwd(x, g, dy):
    r = lax.rsqrt(jnp.mean(x * x, axis=-1, keepdims=True) + EPS)
    xn = x * r
    dyg = dy * g
    dx = r * (dyg - xn * jnp.mean(dyg * xn, axis=-1, keepdims=True))
    return dx, jnp.sum(dy * xn, axis=0, keepdims=True)


def _half_sum(v, lo):
    s_lo = jnp.sum(jnp.where(lo, v, 0.0), axis=-1, keepdims=True)
    s_hi = jnp.sum(jnp.where(lo, 0.0, v), axis=-1, keepdims=True)
    return jnp.where(lo, s_lo, s_hi)


def _half_sum_mxu(v):
    upper = lax.broadcasted_iota(jnp.int32, (128, 128), 0) < 64
    left = lax.broadcasted_iota(jnp.int32, (128, 128), 1) < 64
    ones = jnp.where(upper == left, 1.0, 0.0).astype(BF16)
    high = v.astype(BF16)
    low = (v - high.astype(F32)).astype(BF16)
    return _dot(high, ones, 1, 0) + _dot(low, ones, 1, 0)


def _pair_norm(zp, g, lo):
    r = lax.rsqrt(_half_sum(zp * zp, lo) * (1.0 / HEAD_DIM) + EPS)
    return zp * r * g


def _pair_norm_bwd(zp, g, dy):
    r = lax.rsqrt(_half_sum_mxu(zp * zp) * (1.0 / HEAD_DIM) + EPS)
    xn = zp * r
    dyg = dy * g
    dx = r * (dyg - xn * (_half_sum_mxu(dyg * xn) * (1.0 / HEAD_DIM)))
    return dx, jnp.sum(dy * xn, axis=0, keepdims=True)


def _to_stacked(pair, group, lo):
    rolled = pltpu.roll(pair, 64, axis=1)
    if group == 0:
        return jnp.where(lo, pair, 0.0), jnp.where(lo, rolled, 0.0)
    return jnp.where(lo, 0.0, rolled), jnp.where(lo, 0.0, pair)


def _from_stacked(even, odd, group, lo):
    if group == 0:
        return jnp.where(lo, even, pltpu.roll(odd, 64, axis=1))
    return jnp.where(lo, pltpu.roll(even, 64, axis=1), odd)


def _sigmoid(v):
    return 1.0 / (1.0 + jnp.exp(-v))


def _pool_counts(tile, n_rows):
    t1 = tile * n_rows + lax.broadcasted_iota(jnp.int32, (n_rows, POOL_WIDTH), 0) + 1
    lane = lax.broadcasted_iota(jnp.int32, (n_rows, POOL_WIDTH), 1)
    win = jnp.where(lane < 128, 2, jnp.where(lane < 256, 4, jnp.where(lane < 384, 8, 16)))
    return jnp.minimum(t1, win).astype(F32)


def _first_norm(x2, g_attn):
    s_len = x2.shape[0]
    t = 512

    def body(x_ref, g_ref, hn_ref):
        hn_ref[...] = _rms_fwd(x_ref[...], g_ref[...]).astype(BF16)

    row = pl.BlockSpec((t, D_MODEL), lambda i: (i, 0))
    return pl.pallas_call(
        body, name="first_norm", grid=(s_len // t,), in_specs=[row, _full((1, D_MODEL))], out_specs=row,
        out_shape=jax.ShapeDtypeStruct((s_len, D_MODEL), BF16), compiler_params=_params(),
    )(x2, g_attn)


def _attn_in(hn1, gq, gk, wts):
    s_len = hn1.shape[0]
    t = 512

    def body(hn_ref, gq_ref, gk_ref, sl_ref, lo_ref, me_ref, zqk_ref, u_ref, kn_ref, v_ref, qst_ref, w_ref, sems):
        @pl.when(pl.program_id(0) == 0)
        def _():
            _load_rows((sl_ref, lo_ref, me_ref), "inT", w_ref, sems)

        z = _dot(hn_ref[...], w_ref[...], 1, 1)
        zqk_ref[...] = z[:, :640]
        u_ref[...] = z[:, 768:]
        v_ref[...] = z[:, 640:768].astype(BF16)
        lo = lax.broadcasted_iota(jnp.int32, (t, 128), 1) < 64
        kn_ref[...] = _pair_norm(z[:, 512:640], gk_ref[...], lo).astype(BF16)
        for p in range(4):
            qn = _pair_norm(z[:, 128 * p:128 * p + 128], gq_ref[...], lo)
            even, odd = _to_stacked(qn, p // 2, lo)
            qst_ref[2 * p] = even.astype(BF16)
            qst_ref[2 * p + 1] = odd.astype(BF16)

    row = lambda w: pl.BlockSpec((t, w), lambda i: (i, 0))
    return pl.pallas_call(
        body, name="attn_in", grid=(s_len // t,),
        in_specs=[row(D_MODEL), _full((1, 128)), _full((1, 128))] + W_SPECS,
        out_specs=[row(640), row(POOL_WIDTH), row(128), row(128), pl.BlockSpec((N_Q_HEADS, t, 128), lambda i: (0, i, 0))],
        out_shape=[jax.ShapeDtypeStruct((s_len, 640), F32), jax.ShapeDtypeStruct((s_len, POOL_WIDTH), F32),
                   jax.ShapeDtypeStruct((s_len, 128), BF16), jax.ShapeDtypeStruct((s_len, 128), BF16),
                   jax.ShapeDtypeStruct((N_Q_HEADS, s_len, 128), BF16)],
        scratch_shapes=[pltpu.VMEM((IN_WIDTH, D_MODEL), BF16), pltpu.SemaphoreType.DMA((N_CHIPS,))],
        compiler_params=_params(),
    )(hn1, gq, gk, *wts)


def _bucket_table():
    i_idx = np.arange(BLOCK)[:, None]
    j_idx = np.arange(2 * BLOCK)[None, :]
    d = BLOCK + i_idx - j_idx
    n = np.maximum(d, 0)
    max_exact = N_BUCKETS // 2
    nf = np.maximum(n, 1).astype(np.float64)
    large = max_exact + (np.log(nf / max_exact) / np.log(MAX_DISTANCE / max_exact) * (N_BUCKETS - max_exact)).astype(np.int64)
    large = np.minimum(large, N_BUCKETS - 1)
    bucket = np.where(n < max_exact, n, large)
    return np.where((d >= 0) & (d < BLOCK), bucket, -1).astype(np.int32)


def _bias_build(rel_bias_t, bucket):
    def body(rb_ref, bucket_ref, out_ref):
        bk = bucket_ref[...]
        for h in range(N_Q_HEADS):
            acc = jnp.full((BLOCK, 2 * BLOCK), NEG, F32)
            for b in range(N_BUCKETS):
                acc = jnp.where(bk == b, rb_ref[h, b], acc)
            out_ref[0, pl.ds(h * BLOCK, BLOCK), :] = acc
            out_ref[1, pl.ds(h * BLOCK, BLOCK), :] = acc
            out_ref[1, pl.ds(h * BLOCK, BLOCK), 0:BLOCK] = jnp.full((BLOCK, BLOCK), NEG, F32)

    return pl.pallas_call(
        body, name="bias_build",
        in_specs=[pl.BlockSpec(memory_space=pltpu.SMEM), VMEM_WHOLE], out_specs=VMEM_WHOLE,
        out_shape=jax.ShapeDtypeStruct((2, N_Q_HEADS * BLOCK, 2 * BLOCK), F32),
    )(rel_bias_t, bucket)


def _head_softmax(s_ref, bias_ref, sink_ref, h):
    rows = pl.ds(pl.multiple_of(h * BLOCK, BLOCK), BLOCK)
    s = s_ref[rows, :] * (HEAD_DIM ** -0.5) + bias_ref[rows, :]
    sink = sink_ref[h]
    m = jnp.maximum(jnp.max(s, axis=-1, keepdims=True), sink)
    p = jnp.exp(s - m)
    e_sink = jnp.exp(sink - m)
    inv = 1.0 / (jnp.sum(p, axis=-1, keepdims=True) + e_sink)
    return rows, p * inv, e_sink * inv


ATTN_STEP_BLOCKS = 4
BAND = (N_Q_HEADS * BLOCK, 2 * BLOCK)


def _attn_specs():
    nb = ATTN_STEP_BLOCKS
    stacked = pl.BlockSpec((N_Q_HEADS, nb * BLOCK, 128), lambda i: (0, i, 0))
    kv = [pl.BlockSpec((BLOCK, 128), lambda i: (jnp.maximum(nb * i - 1, 0), 0)), pl.BlockSpec((nb * BLOCK, 128), lambda i: (i, 0))]
    consts = [_full((2,) + BAND), pl.BlockSpec(memory_space=pltpu.SMEM)]
    return stacked, kv, consts


def _step_blocks(i, kp_ref, kc_ref, vp_ref, vc_ref, bias_ref):
    blocks = []
    for b in range(ATTN_STEP_BLOCKS):
        if b == 0:
            k2 = jnp.concatenate([kp_ref[...], kc_ref[pl.ds(0, BLOCK), :]], axis=0)
            v2 = jnp.concatenate([vp_ref[...], vc_ref[pl.ds(0, BLOCK), :]], axis=0)
            bias = bias_ref.at[jnp.where(i == 0, 1, 0)]
        else:
            k2, v2, bias = kc_ref[pl.ds((b - 1) * BLOCK, 2 * BLOCK), :], vc_ref[pl.ds((b - 1) * BLOCK, 2 * BLOCK), :], bias_ref.at[0]
        blocks.append((pl.ds(b * BLOCK, BLOCK), k2, v2, bias))
    return blocks


def _head_lane_mask():
    rows = lax.broadcasted_iota(jnp.int32, (N_Q_HEADS * BLOCK, 128), 0)
    lanes = lax.broadcasted_iota(jnp.int32, (N_Q_HEADS * BLOCK, 128), 1)
    return (rows < 4 * BLOCK) == (lanes < 64)


def _attn_fwd(qst, kn, vb, bias_st, sinks):
    s_len = kn.shape[0]

    def body(q_ref, kp_ref, kc_ref, vp_ref, vc_ref, bias_ref, sink_ref, o_ref, s_ref, p_ref):
        for b, (rows, k2, v2, bias) in enumerate(_step_blocks(pl.program_id(0), kp_ref, kc_ref, vp_ref, vc_ref, bias_ref)):
            s_b, p_b = s_ref.at[b], p_ref.at[b]
            s_b[...] = _dot(q_ref[:, rows, :].reshape(N_Q_HEADS * BLOCK, 128), k2, 1, 1)

            def head(h, carry):
                head_rows, probs, _ = _head_softmax(s_b, bias, sink_ref, h)
                p_b[head_rows, :] = probs.astype(BF16)
                return carry

            lax.fori_loop(0, N_Q_HEADS, head, 0, unroll=True)
            o = jnp.where(_head_lane_mask(), _dot(p_b[...], v2, 1, 0), 0.0)
            o_ref[:, rows, :] = o.astype(BF16).reshape(N_Q_HEADS, BLOCK, 128)

    stacked, kv, consts = _attn_specs()
    return pl.pallas_call(
        body, name="attn_fwd", grid=(s_len // (ATTN_STEP_BLOCKS * BLOCK),),
        in_specs=[stacked] + kv + kv + consts, out_specs=stacked,
        out_shape=jax.ShapeDtypeStruct((N_Q_HEADS, s_len, 128), BF16),
        scratch_shapes=[pltpu.VMEM((ATTN_STEP_BLOCKS,) + BAND, F32), pltpu.VMEM((ATTN_STEP_BLOCKS,) + BAND, BF16)],
        compiler_params=_params(),
    )(qst, kn, kn, vb, vb, bias_st, sinks)


def _mix_out(u, ost, x2, wts, wpool, pool_scale, g_ffn):
    s_len = x2.shape[0]
    t = 512
    n = t + 16

    def body(u_ref, o_ref, x_ref, sl_ref, lo_ref, me_ref, wp_ref, sc_ref, g_ref, pooled_ref, mix_ref, h1_ref, hn_ref,
             w_ref, ext_ref, st_ref, sems):
        i = pl.program_id(0)

        @pl.when(i == 0)
        def _():
            _load_rows((sl_ref, lo_ref, me_ref), "out", w_ref, sems)
            ext_ref[...] = jnp.zeros_like(ext_ref)
            st_ref[...] = jnp.zeros_like(st_ref)

        u_tile = u_ref[...]
        ext_ref[pl.ds(POOL_HALO, t), :] = u_tile
        st_ref[pl.ds(8, n), :] = ext_ref[pl.ds(8, n), :] + ext_ref[pl.ds(7, n), :]
        st_ref[pl.ds(8, n), 128:] = st_ref[pl.ds(8, n), 128:] + st_ref[pl.ds(6, n), 128:]
        st_ref[pl.ds(8, n), 256:] = st_ref[pl.ds(8, n), 256:] + st_ref[pl.ds(4, n), 256:]
        st_ref[pl.ds(8, n), 384:] = st_ref[pl.ds(8, n), 384:] + st_ref[pl.ds(0, n), 384:]
        ext_ref[pl.ds(0, POOL_HALO), :] = ext_ref[pl.ds(t, POOL_HALO), :]
        pooled = (st_ref[pl.ds(POOL_HALO, t), :] / _pool_counts(i, t) - u_tile).astype(BF16)
        pooled_ref[...] = pooled
        for g in range(4):
            cols = slice(128 * g, 128 * g + 128)
            y = _dot(pooled[:, cols], wp_ref[g], 1, 0) * sc_ref[:, cols]
            mix_ref[:, ATTN_WIDTH + 128 * g:ATTN_WIDTH + 128 * g + 128] = y.astype(BF16)
        lo = lax.broadcasted_iota(jnp.int32, (t, 128), 1) < 64
        for p in range(4):
            a = _from_stacked(o_ref[2 * p].astype(F32), o_ref[2 * p + 1].astype(F32), p // 2, lo)
            mix_ref[:, 128 * p:128 * p + 128] = a.astype(BF16)
        h1 = x_ref[...] + _dot(mix_ref[...], w_ref[...], 1, 0)
        h1_ref[...] = h1
        hn_ref[...] = _rms_fwd(h1, g_ref[...]).astype(BF16)

    row = lambda w: pl.BlockSpec((t, w), lambda i: (i, 0))
    return pl.pallas_call(
        body, name="mix_out", grid=(s_len // t,),
        in_specs=[row(POOL_WIDTH), pl.BlockSpec((N_Q_HEADS, t, 128), lambda i: (0, i, 0)), row(D_MODEL)] + W_SPECS
        + [_full((4, 128, 128)), _full((1, POOL_WIDTH)), _full((1, D_MODEL))],
        out_specs=[row(POOL_WIDTH), row(D_MODEL), row(D_MODEL), row(D_MODEL)],
        out_shape=[jax.ShapeDtypeStruct((s_len, POOL_WIDTH), BF16), jax.ShapeDtypeStruct((s_len, D_MODEL), BF16),
                   jax.ShapeDtypeStruct((s_len, D_MODEL), F32), jax.ShapeDtypeStruct((s_len, D_MODEL), BF16)],
        scratch_shapes=[pltpu.VMEM((D_MODEL, D_MODEL), BF16), pltpu.VMEM((t + POOL_HALO, POOL_WIDTH), F32),
                        pltpu.VMEM((t + POOL_HALO, POOL_WIDTH), F32), pltpu.SemaphoreType.DMA((N_CHIPS,))],
        compiler_params=_params(),
    )(u, ost, x2, *wts, wpool, pool_scale, g_ffn)


def _ffn_ple(hn2, h1, p2, tgt, wts, g_ffn, g_ple):
    s_len = h1.shape[0]
    t = 256
    n_tiles = s_len // t

    def body(hn_ref, h1_ref, p_ref, tgt_ref, sl_ref, lo_ref, me_ref, gf_ref, gp_ref,
             loss_ref, dgate_ref, dup_ref, act_ref, dh2b_ref, hn3_ref, dgl_ref, dwp_ref, dh1_ref, dgf_ref, dgp_ref,
             wg_ref, wu_ref, wd_ref, wl_ref, wp_ref, packed_ref, gate_s, up_s, loss_acc, dwp_acc, sems):
        i = pl.program_id(0)

        @pl.when(i == 0)
        def _():
            w_refs = (sl_ref, lo_ref, me_ref)
            _load_rows(w_refs, "gateT", wg_ref, sems)
            _load_rows(w_refs, "upT", wu_ref, sems)
            _load_rows(w_refs, "down", wd_ref, sems)
            _load_rows(w_refs, "plg", wl_ref, sems)
            _load_rows(w_refs, "plp", packed_ref, sems)
            for j in range(N_CHIPS):
                for q in range(4):
                    wp_ref[pl.ds(64 * q, 64), 256 * j:256 * j + 256] = packed_ref[pl.ds(64 * j, 64), 256 * q:256 * q + 256]
            loss_acc[...] = jnp.zeros_like(loss_acc)
            dgf_ref[...] = jnp.zeros_like(dgf_ref)
            dgp_ref[...] = jnp.zeros_like(dgp_ref)

        hn = hn_ref[...]
        h1v = h1_ref[...]
        h2 = h1v
        for ch in range(N_FF_CHUNKS):
            rows = pl.ds(ch * FF_CHUNK, FF_CHUNK)
            gate = _dot(hn, wg_ref[rows, :], 1, 1)
            up = _dot(hn, wu_ref[rows, :], 1, 1)
            gate_s[ch] = gate
            up_s[ch] = up
            act = (gate * _sigmoid(gate) * up).astype(BF16)
            act_ref[ch] = act
            h2 = h2 + _dot(act, wd_ref[rows, :], 1, 0)
        gp = gp_ref[...]
        hn3 = _rms_fwd(h2, gp).astype(BF16)
        hn3_ref[...] = hn3
        gate2 = _sigmoid(_dot(hn3, wl_ref[...], 1, 0))
        p_tile = p_ref[...].astype(BF16)
        pp = _dot(p_tile, wp_ref[...], 1, 0)
        err = h2 + gate2 * pp - tgt_ref[...]
        loss_acc[...] += jnp.sum(err * err, axis=0, keepdims=True)
        dy = err * (1.0 / D_MODEL)
        _accumulate_tn(dwp_acc, p_tile, (dy * gate2).astype(BF16), i == 0)
        dgl = (dy * pp * gate2 * (1.0 - gate2)).astype(BF16)
        dgl_ref[...] = dgl
        dx3, dg3 = _rms_bwd(h2, gp, _dot(dgl, wl_ref[...], 1, 1))
        dh2 = dy + dx3
        dgp_ref[...] += dg3
        dh2b = dh2.astype(BF16)
        dh2b_ref[...] = dh2b
        dhn = jnp.zeros((t, D_MODEL), F32)
        for ch in range(N_FF_CHUNKS):
            rows = pl.ds(ch * FF_CHUNK, FF_CHUNK)
            dact = _dot(dh2b, wd_ref[rows, :], 1, 1)
            gate_v = gate_s[ch]
            up_v = up_s[ch]
            sg = _sigmoid(gate_v)
            dup = (dact * (gate_v * sg)).astype(BF16)
            dgate = (dact * up_v * (sg * (1.0 + gate_v * (1.0 - sg)))).astype(BF16)
            dup_ref[ch] = dup
            dgate_ref[ch] = dgate
            dhn = dhn + _dot(dgate, wg_ref[rows, :], 1, 0) + _dot(dup, wu_ref[rows, :], 1, 0)
        dx, dg = _rms_bwd(h1v, gf_ref[...], dhn)
        dh1_ref[...] = dh2 + dx
        dgf_ref[...] += dg

        @pl.when(i == n_tiles - 1)
        def _():
            total = jnp.sum(loss_acc[...], axis=-1, keepdims=True) * (0.5 / D_MODEL)
            loss_ref[...] = jnp.broadcast_to(total, loss_ref.shape)
            dwp_ref[...] = dwp_acc[...].astype(BF16)

    row = lambda w: pl.BlockSpec((t, w), lambda i: (i, 0))
    chunked = pl.BlockSpec((N_FF_CHUNKS, t, FF_CHUNK), lambda i: (0, i, 0))
    vec = _full((1, D_MODEL))
    act_shape = jax.ShapeDtypeStruct((N_FF_CHUNKS, s_len, FF_CHUNK), BF16)
    tok = lambda dtype: jax.ShapeDtypeStruct((s_len, D_MODEL), dtype)
    return pl.pallas_call(
        body, name="ffn_ple", grid=(n_tiles,),
        in_specs=[row(D_MODEL), row(D_MODEL), row(PLE_DIM), row(D_MODEL)] + W_SPECS + [vec, vec],
        out_specs=[_full((1, 128)), chunked, chunked, chunked] + [row(D_MODEL)] * 3 + [_full((PLE_DIM, D_MODEL)), row(D_MODEL),
                                                                                       vec, vec],
        out_shape=[jax.ShapeDtypeStruct((1, 128), F32), act_shape, act_shape, act_shape, tok(BF16), tok(BF16), tok(BF16),
                   jax.ShapeDtypeStruct((PLE_DIM, D_MODEL), BF16), tok(F32), jax.ShapeDtypeStruct((1, D_MODEL), F32),
                   jax.ShapeDtypeStruct((1, D_MODEL), F32)],
        scratch_shapes=[pltpu.VMEM((D_FF, D_MODEL), BF16)] * 3
        + [pltpu.VMEM((D_MODEL, D_MODEL), BF16), pltpu.VMEM((PLE_DIM, D_MODEL), BF16), pltpu.VMEM((PLE_DIM, D_MODEL), BF16),
           pltpu.VMEM((N_FF_CHUNKS, t, FF_CHUNK), F32), pltpu.VMEM((N_FF_CHUNKS, t, FF_CHUNK), F32), pltpu.VMEM((1, D_MODEL), F32),
           pltpu.VMEM((PLE_DIM, D_MODEL), F32), pltpu.SemaphoreType.DMA((N_CHIPS,))],
        compiler_params=_params(VMEM_LIMIT_BIG),
    )(hn2, h1, p2, tgt, *wts, g_ffn, g_ple)


def _accumulate_tn(acc_ref, a, b, first):
    @pl.when(first)
    def _():
        acc_ref[...] = _dot(a, b, 0, 0)

    @pl.when(jnp.logical_not(first))
    def _():
        acc_ref[...] += _dot(a, b, 0, 0)


def _flush_chunks(acc_ref, stage_ref, slab_ref, name, sems):
    stage_ref[...] = acc_ref[...].astype(BF16)
    off, rows = SLAB[name]
    copies = [pltpu.make_async_copy(stage_ref.at[pl.ds(j * rows, rows), :], slab_ref.at[j, pl.ds(off, rows), :], sems.at[j])
              for j in range(N_CHIPS)]
    for cp in copies:
        cp.start()
    for cp in copies:
        cp.wait()


def _mix_out_bwd(dh1, wts, pooled, wpool, pool_scale, mix, after):
    s_len = dh1.shape[0]
    t = 512
    n = t + 16
    n_tiles = s_len // t
    early_rows = GATHER_PARTS[0][1]

    def body(dh1_ref, sl_ref, lo_ref, me_ref, pooled_ref, wp_ref, sc_ref, mix_ref, after_ref, dost_ref, du_ref, dwp_ref,
             dsc_ref, slab_ref, w_ref, ext_ref, st_ref, acc_ref, stage_ref, sems):
        del after_ref
        i = pl.program_id(0)

        @pl.when(i == 0)
        def _():
            _load_rows((sl_ref, lo_ref, me_ref), "out", w_ref, sems)
            ext_ref[...] = jnp.zeros_like(ext_ref)
            st_ref[...] = jnp.zeros_like(st_ref)
            dsc_ref[...] = jnp.zeros_like(dsc_ref)
            dwp_ref[...] = jnp.zeros_like(dwp_ref)

        dh1b = dh1_ref[...].astype(BF16)
        _accumulate_tn(acc_ref, mix_ref[...], dh1b, i == 0)

        @pl.when(i == n_tiles - 1)
        def _():
            _flush_chunks(acc_ref, stage_ref, slab_ref, "out", sems)

        dmix = _dot(dh1b, w_ref[...], 1, 1)
        lo = lax.broadcasted_iota(jnp.int32, (t, 128), 1) < 64
        for p in range(4):
            even, odd = _to_stacked(dmix[:, 128 * p:128 * p + 128], p // 2, lo)
            dost_ref[2 * p] = even.astype(BF16)
            dost_ref[2 * p + 1] = odd.astype(BF16)
        pooled_v = pooled_ref[...]
        counts = _pool_counts(n_tiles - 1 - i, t)
        for g in range(4):
            cols = slice(128 * g, 128 * g + 128)
            dm = dmix[:, ATTN_WIDTH + 128 * g:ATTN_WIDTH + 128 * g + 128]
            ypre = _dot(pooled_v[:, cols], wp_ref[g], 1, 0)
            dsc_ref[:, cols] += jnp.sum(ypre * dm, axis=0, keepdims=True)
            dyp = (dm * sc_ref[:, cols]).astype(BF16)
            dwp_ref[g] += _dot(pooled_v[:, cols], dyp, 0, 0)
            dpooled = _dot(dyp, wp_ref[g], 1, 1)
            du_ref[:, cols] = -dpooled
            ext_ref[pl.ds(0, t), cols] = dpooled / counts[:, cols]
        st_ref[pl.ds(0, n), :] = ext_ref[pl.ds(0, n), :] + ext_ref[pl.ds(1, n), :]
        st_ref[pl.ds(0, n), 128:] = st_ref[pl.ds(0, n), 128:] + st_ref[pl.ds(2, n), 128:]
        st_ref[pl.ds(0, n), 256:] = st_ref[pl.ds(0, n), 256:] + st_ref[pl.ds(4, n), 256:]
        st_ref[pl.ds(0, n), 384:] = st_ref[pl.ds(0, n), 384:] + st_ref[pl.ds(8, n), 384:]
        ext_ref[pl.ds(t, POOL_HALO), :] = ext_ref[pl.ds(0, POOL_HALO), :]
        du_ref[...] += st_ref[pl.ds(0, t), :]

    rev = lambda w: pl.BlockSpec((t, w), lambda i: (n_tiles - 1 - i, 0))
    return pl.pallas_call(
        body, name="mix_out_bwd", grid=(n_tiles,),
        in_specs=[rev(D_MODEL)] + W_SPECS + [rev(POOL_WIDTH), _full((4, 128, 128)), _full((1, POOL_WIDTH)), rev(D_MODEL), ANY],
        out_specs=[pl.BlockSpec((N_Q_HEADS, t, 128), lambda i: (0, n_tiles - 1 - i, 0)), rev(POOL_WIDTH),
                   _full((4, 128, 128)), _full((1, POOL_WIDTH)), ANY],
        out_shape=[jax.ShapeDtypeStruct((N_Q_HEADS, s_len, 128), BF16), jax.ShapeDtypeStruct((s_len, POOL_WIDTH), F32),
                   jax.ShapeDtypeStruct((4, 128, 128), F32), jax.ShapeDtypeStruct((1, POOL_WIDTH), F32),
                   jax.ShapeDtypeStruct((N_CHIPS, early_rows, D_MODEL), BF16)],
        scratch_shapes=[pltpu.VMEM((D_MODEL, D_MODEL), BF16), pltpu.VMEM((t + POOL_HALO, POOL_WIDTH), F32),
                        pltpu.VMEM((t + POOL_HALO, POOL_WIDTH), F32), pltpu.VMEM((D_MODEL, D_MODEL), F32),
                        pltpu.VMEM((D_MODEL, D_MODEL), BF16), pltpu.SemaphoreType.DMA((N_CHIPS,))],
        compiler_params=_params(),
    )(dh1, *wts, pooled, wpool, pool_scale, mix, after)


def _attn_bwd(qst, kn, vb, dost, bias_st, sinks, after):
    s_len = kn.shape[0]

    def body(q_ref, kp_ref, kc_ref, vp_ref, vc_ref, do_ref, bias_ref, sink_ref, after_ref, dq_ref, dk_ref, dv_ref, dbias_ref,
             dsink_ref, s_ref, dp_ref, p_ref, dl_ref):
        del after_ref
        i = pl.program_id(0)

        @pl.when(i == 0)
        def _():
            dk_ref[...] = jnp.zeros_like(dk_ref)
            dv_ref[...] = jnp.zeros_like(dv_ref)
            dbias_ref[...] = jnp.zeros_like(dbias_ref)
            dsink_ref[...] = jnp.zeros_like(dsink_ref)

        for b, (rows, k2, v2, bias) in enumerate(_step_blocks(i, kp_ref, kc_ref, vp_ref, vc_ref, bias_ref)):
            s_b, dp_b, p_b, dl_b = s_ref.at[b], dp_ref.at[b], p_ref.at[b], dl_ref.at[b]
            q = q_ref[:, rows, :].reshape(N_Q_HEADS * BLOCK, 128)
            do = do_ref[:, rows, :].reshape(N_Q_HEADS * BLOCK, 128)
            s_b[...] = _dot(q, k2, 1, 1)
            dp_b[...] = _dot(do, v2, 1, 1)

            def head(h, carry):
                head_rows, probs, p_sink = _head_softmax(s_b, bias, sink_ref, h)
                dp = dp_b[head_rows, :]
                dsum = jnp.sum(probs * dp, axis=-1, keepdims=True)
                dlog = probs * (dp - dsum)
                dsink_ref[head_rows, :] -= p_sink * dsum
                dbias_ref[head_rows, :] += dlog
                p_b[head_rows, :] = probs.astype(BF16)
                dl_b[head_rows, :] = (dlog * (HEAD_DIM ** -0.5)).astype(BF16)
                return carry

            lax.fori_loop(0, N_Q_HEADS, head, 0, unroll=True)
            dlog_s = dl_b[...]
            dq_ref[:, rows, :] = jnp.where(_head_lane_mask(), _dot(dlog_s, k2, 1, 0), 0.0).reshape(N_Q_HEADS, BLOCK, 128)
            dk2 = _dot(dlog_s, q, 0, 0)
            dv2 = _dot(p_b[...], do, 0, 0)
            block = ATTN_STEP_BLOCKS * i + b
            prev_rows = pl.ds(pl.multiple_of(jnp.maximum(block - 1, 0) * BLOCK, BLOCK), BLOCK)
            cur_rows = pl.ds(pl.multiple_of(block * BLOCK, BLOCK), BLOCK)
            dk_ref[prev_rows, :] += dk2[:BLOCK]
            dk_ref[cur_rows, :] += dk2[BLOCK:]
            dv_ref[prev_rows, :] += dv2[:BLOCK]
            dv_ref[cur_rows, :] += dv2[BLOCK:]

    stacked, kv, consts = _attn_specs()
    per_step = (ATTN_STEP_BLOCKS,) + BAND
    return pl.pallas_call(
        body, name="attn_bwd", grid=(s_len // (ATTN_STEP_BLOCKS * BLOCK),),
        in_specs=[stacked] + kv + kv + [stacked] + consts + [ANY],
        out_specs=[stacked, _full((s_len, 128)), _full((s_len, 128)), _full(BAND), _full((N_Q_HEADS * BLOCK, 1))],
        out_shape=[jax.ShapeDtypeStruct((N_Q_HEADS, s_len, 128), F32), jax.ShapeDtypeStruct((s_len, 128), F32),
                   jax.ShapeDtypeStruct((s_len, 128), F32), jax.ShapeDtypeStruct(BAND, F32),
                   jax.ShapeDtypeStruct((N_Q_HEADS * BLOCK, 1), F32)],
        scratch_shapes=[pltpu.VMEM(per_step, F32), pltpu.VMEM(per_step, F32), pltpu.VMEM(per_step, BF16),
                        pltpu.VMEM(per_step, BF16)],
        compiler_params=_params(),
    )(qst, kn, kn, vb, vb, dost, bias_st, sinks, after)


def _small_pack(dg_attn, dg_ffn, dg_ple, dscale, dgq, dgk, dbias, dsink_rows, bucket, loss_v, dwpool):
    def body(ga_ref, gf_ref, gp_ref, sc_ref, gq_ref, gk_ref, db_ref, ds_ref, bucket_ref, loss_ref, wp_ref, out_ref):
        out_ref[pl.ds(0, SMALL["w_pool"]), :] = jnp.zeros((SMALL["w_pool"], 128), F32)
        for name, ref, n in (("g_attn", ga_ref, 8), ("g_ffn", gf_ref, 8), ("g_ple", gp_ref, 8), ("pool_scale", sc_ref, 4)):
            for k in range(n):
                out_ref[pl.ds(SMALL[name] + k, 1), :] = ref[:, 128 * k:128 * k + 128]
        for name, ref in (("g_q", gq_ref), ("g_k", gk_ref)):
            both = ref[...]
            out_ref[pl.ds(SMALL[name], 1), :] = both + pltpu.roll(both, 64, axis=1)
        out_ref[pl.ds(SMALL["loss"], 1), :] = loss_ref[...]
        bk = bucket_ref[...]
        rows = lax.broadcasted_iota(jnp.int32, (N_BUCKETS, 128), 0)
        lanes = lax.broadcasted_iota(jnp.int32, (N_BUCKETS, 128), 1)
        lane1 = lax.broadcasted_iota(jnp.int32, (1, 128), 1)
        rb = jnp.zeros((N_BUCKETS, 128), F32)
        sk = jnp.zeros((1, 128), F32)
        for h in range(N_Q_HEADS):
            band = db_ref[pl.ds(h * BLOCK, BLOCK), :]
            for b in range(N_BUCKETS):
                rb = jnp.where((rows == b) & (lanes == h), jnp.sum(jnp.where(bk == b, band, 0.0)), rb)
            sk = jnp.where(lane1 == h, jnp.sum(ds_ref[pl.ds(h * BLOCK, BLOCK), :]), sk)
        out_ref[pl.ds(SMALL["rel_bias"], N_BUCKETS), :] = rb
        out_ref[pl.ds(SMALL["sinks"], 1), :] = sk
        out_ref[pl.ds(SMALL["w_pool"], 512), :] = wp_ref[...].reshape(512, 128)

    return pl.pallas_call(
        body, name="small_pack", in_specs=[VMEM_WHOLE] * 11, out_specs=VMEM_WHOLE,
        out_shape=jax.ShapeDtypeStruct((SMALL_ROWS, 128), F32),
    )(dg_attn, dg_ffn, dg_ple, dscale, dgq, dgk, dbias, dsink_rows, bucket, loss_v, dwpool)


def _attn_in_bwd(dqst, zqk, dk, dv, du, x2, dh1, hn1, slab, wts, g_attn, gq, gk):
    s_len = x2.shape[0]
    t = 512
    n_tiles = s_len // t

    def body(dq_ref, zqk_ref, dk_ref, dv_ref, du_ref, x_ref, dh1_ref, hn_ref, slab_in_ref, sl_ref, lo_ref, me_ref, g_ref,
             gq_ref, gk_ref, dx_ref, dg_ref, dgq_ref, dgk_ref, slab_ref, w_ref, dz_ref, acc_ref, stage_ref, sems):
        del slab_in_ref
        i = pl.program_id(0)

        @pl.when(i == 0)
        def _():
            _load_rows((sl_ref, lo_ref, me_ref), "inT", w_ref, sems)
            dg_ref[...] = jnp.zeros_like(dg_ref)
            dgq_ref[...] = jnp.zeros_like(dgq_ref)
            dgk_ref[...] = jnp.zeros_like(dgk_ref)

        lo = lax.broadcasted_iota(jnp.int32, (t, 128), 1) < 64
        for p in range(4):
            dqn = _from_stacked(dq_ref[2 * p], dq_ref[2 * p + 1], p // 2, lo)
            dq_raw, dgq = _pair_norm_bwd(zqk_ref[:, 128 * p:128 * p + 128], gq_ref[...], dqn)
            dz_ref[:, 128 * p:128 * p + 128] = dq_raw.astype(BF16)
            dgq_ref[...] += dgq
        dk_raw, dgk = _pair_norm_bwd(zqk_ref[:, 512:640], gk_ref[...], dk_ref[...])
        dgk_ref[...] += dgk
        dz_ref[:, 512:640] = dk_raw.astype(BF16)
        dz_ref[:, 640:768] = dv_ref[...].astype(BF16)
        dz_ref[:, 768:] = du_ref[...].astype(BF16)
        dz = dz_ref[...]
        _accumulate_tn(acc_ref, dz, hn_ref[...], i == 0)
        dx, dg = _rms_bwd(x_ref[...], g_ref[...], _dot(dz, w_ref[...], 1, 0))
        dx_ref[...] = dh1_ref[...] + dx
        dg_ref[...] += dg

        @pl.when(i == n_tiles - 1)
        def _():
            _flush_chunks(acc_ref, stage_ref, slab_ref, "inT", sems)

    row = lambda w: pl.BlockSpec((t, w), lambda i: (i, 0))
    return pl.pallas_call(
        body, name="attn_in_bwd", grid=(n_tiles,),
        in_specs=[pl.BlockSpec((N_Q_HEADS, t, 128), lambda i: (0, i, 0)), row(640), row(128), row(128), row(POOL_WIDTH),
                  row(D_MODEL), row(D_MODEL), row(D_MODEL), ANY] + W_SPECS + [_full((1, D_MODEL)), _full((1, 128)),
                                                                              _full((1, 128))],
        out_specs=[row(D_MODEL), _full((1, D_MODEL)), _full((1, 128)), _full((1, 128)), ANY],
        out_shape=[jax.ShapeDtypeStruct((s_len, D_MODEL), F32), jax.ShapeDtypeStruct((1, D_MODEL), F32),
                   jax.ShapeDtypeStruct((1, 128), F32), jax.ShapeDtypeStruct((1, 128), F32),
                   jax.ShapeDtypeStruct(slab.shape, BF16)],
        input_output_aliases={8: 4},
        scratch_shapes=[pltpu.VMEM((IN_WIDTH, D_MODEL), BF16), pltpu.VMEM((t, IN_WIDTH), BF16),
                        pltpu.VMEM((IN_WIDTH, D_MODEL), F32), pltpu.VMEM((IN_WIDTH, D_MODEL), BF16),
                        pltpu.SemaphoreType.DMA((N_CHIPS,))],
        compiler_params=_params(),
    )(dqst, zqk, dk, dv, du, x2, dh1, hn1, slab, *wts, g_attn, gq, gk)


def _dw(lefts, b, name, slab, slab_rows, row_offs):
    a0, n_a = lefts[0], len(lefts)
    assert b.shape[1] == D_MODEL
    if a0.ndim == 3:
        n_chunks, s_len, tm = a0.shape
        m = n_chunks * tm
    else:
        s_len, tm = a0.shape
        m = tm
    tk = 2048 if n_a * tm <= 1408 else 1024
    if a0.ndim == 3:
        a_spec = pl.BlockSpec((None, tk, tm), lambda i, k: (i, k, 0))
    else:
        a_spec = pl.BlockSpec((tk, tm), lambda i, k: (k, i))
    n_steps, n_tiles = s_len // tk, m // tm
    chunk = m // N_CHIPS
    per_tile = tm // chunk

    def body(*refs):
        a_refs, b_ref = refs[:n_a], refs[n_a]
        o_ref, acc_ref, stage_ref, sems = refs[-4:]
        i, k = pl.program_id(0), pl.program_id(1)
        b_tile = b_ref[...].astype(BF16)
        for w, a_ref in enumerate(a_refs):
            _accumulate_tn(acc_ref.at[w], a_ref[...].astype(BF16), b_tile, k == 0)

        def out_copies(tile, slot):
            return [pltpu.make_async_copy(stage_ref.at[slot, w, pl.ds(jj * chunk, chunk), :],
                                          o_ref.at[tile * per_tile + jj, pl.ds(row_offs[w], chunk), :], sems.at[slot, w, jj])
                    for w in range(n_a) for jj in range(per_tile)]

        @pl.when(k == n_steps - 1)
        def _():
            slot = i % 2

            @pl.when(i >= 2)
            def _():
                for cp in out_copies(i - 2, slot):
                    cp.wait()

            stage_ref[slot] = acc_ref[...].astype(BF16)
            for cp in out_copies(i, slot):
                cp.start()

            @pl.when(i == n_tiles - 1)
            def _():
                for cp in out_copies(i, slot):
                    cp.wait()
                if n_tiles > 1:
                    for cp in out_copies(i - 1, 1 - slot):
                        cp.wait()

    in_specs = [a_spec] * n_a + [pl.BlockSpec((tk, D_MODEL), lambda i, k: (k, 0))]
    operands, aliases = [*lefts, b], {}
    if slab is not None:
        in_specs.append(ANY)
        operands.append(slab)
        aliases = {n_a + 1: 0}
    return pl.pallas_call(
        body, name=name, grid=(n_tiles, n_steps), in_specs=in_specs, out_specs=ANY,
        out_shape=jax.ShapeDtypeStruct((N_CHIPS, slab_rows, D_MODEL), BF16), input_output_aliases=aliases,
        scratch_shapes=[pltpu.VMEM((n_a, tm, D_MODEL), F32), pltpu.VMEM((2, n_a, tm, D_MODEL), BF16),
                        pltpu.SemaphoreType.DMA((2, n_a, per_tile))],
        compiler_params=_params(VMEM_LIMIT_BIG, n_axes=2),
    )(*operands)


def _position():
    x, y, c = lax.axis_index("x"), lax.axis_index("y"), lax.axis_index("c")
    other_chips = [(1 - x, y), (x, 1 - y), (1 - x, 1 - y)]
    return x, y, c, other_chips


def _ag_weights(local_slab, row0, n_rows, name, collective_id):
    half = n_rows // 2
    quarter = half // 2
    assert quarter % 16 == 0

    def body(l_ref, g_ref, send, recv):
        x, y, c, chips = _position()
        me, (via_x, via_y, diagonal) = 2 * x + y, [2 * chip[0] + chip[1] for chip in chips]
        here, sibling, x_nbr, y_nbr = (x, y, c), (x, y, 1 - c), (1 - x, y, c), (x, 1 - y, c)
        peers = [sibling, x_nbr, y_nbr]
        barrier = pltpu.get_barrier_semaphore()
        for peer in peers:
            pl.semaphore_signal(barrier, inc=1, device_id=peer, device_id_type=MESH)
        pl.semaphore_wait(barrier, len(peers))

        def rows(core, part):
            start, size = (core * half, half) if part is None else (core * half + part * quarter, quarter)
            return pl.ds(pl.multiple_of(start, 16), size)

        def copy(k, chip_idx, where, to, src=None):
            dst = g_ref.at[chip_idx, where, :]
            return pltpu.make_async_remote_copy(src_ref=dst if src is None else src, dst_ref=dst, send_sem=send.at[k],
                                                recv_sem=recv.at[k], device_id=to, device_id_type=MESH)

        own_rows = l_ref.at[pl.ds(pl.multiple_of(row0 + c * half, 16), half), :]
        started = [copy(0, me, rows(c, None), x_nbr, src=own_rows), copy(1, me, rows(c, None), y_nbr, src=own_rows)]
        for cp in started:
            cp.start()
        after_arrival = [
            (copy(0, via_x, rows(c, None), here), [copy(4, via_x, rows(c, None), sibling), copy(3, via_x, rows(c, 1), y_nbr)]),
            (copy(1, via_y, rows(c, None), here), [copy(5, via_y, rows(c, None), sibling), copy(2, via_y, rows(c, 0), x_nbr)]),
            (copy(2, diagonal, rows(c, 0), here), [copy(6, diagonal, rows(c, 0), sibling)]),
            (copy(3, diagonal, rows(c, 1), here), [copy(7, diagonal, rows(c, 1), sibling)]),
        ]
        for arrival, onward in after_arrival:
            arrival.wait_recv()
            for cp in onward:
                cp.start()
            started += onward
        for cp in (copy(4, via_x, rows(1 - c, None), here), copy(5, via_y, rows(1 - c, None), here),
                   copy(6, diagonal, rows(1 - c, 0), here), copy(7, diagonal, rows(1 - c, 1), here)):
            cp.wait_recv()
        for cp in started:
            cp.wait_send()

    return pl.kernel(
        body, out_type=jax.ShapeDtypeStruct((N_CHIPS, n_rows, D_MODEL), BF16),
        mesh=plsc.ScalarSubcoreMesh(axis_name="sequencer", num_cores=1), name=name,
        scratch_types=[pltpu.SemaphoreType.DMA((8,)), pltpu.SemaphoreType.DMA((8,))],
        compiler_params=pltpu.CompilerParams(collective_id=collective_id),
    )(local_slab)


def _comm_call(body, peers_of, out_shape, n_sems, operand, name, collective_id):
    sems = [pltpu.SemaphoreType.DMA((n_sems,)), pltpu.SemaphoreType.DMA((n_sems,))]
    if collective_id is None:
        return pl.pallas_call(body, name=name, in_specs=[ANY], out_specs=ANY, out_shape=out_shape, scratch_shapes=sems)(operand)

    def with_handshake(in_ref, out_ref, send, recv):
        x, y, c, _ = _position()
        peers = peers_of(x, y, c)
        barrier = pltpu.get_barrier_semaphore()
        for peer in peers:
            pl.semaphore_signal(barrier, inc=1, device_id=peer, device_id_type=MESH)
        pl.semaphore_wait(barrier, len(peers))
        body(in_ref, out_ref, send, recv)

    return pl.kernel(with_handshake, out_type=out_shape, mesh=plsc.ScalarSubcoreMesh(axis_name="sequencer", num_cores=1),
                     name=name, scratch_types=sems, compiler_params=pltpu.CompilerParams(collective_id=collective_id))(operand)


def _rs_swap_halves(partial, name, collective_id=None):
    half = partial.shape[1] // 2

    def body(p_ref, r_ref, send, recv):
        x, y, c, _ = _position()
        theirs = pl.ds(pl.multiple_of((1 - c) * half, 16), half)
        cp = pltpu.make_async_remote_copy(src_ref=p_ref.at[:, theirs, :], dst_ref=r_ref, send_sem=send.at[0],
                                          recv_sem=recv.at[0], device_id=(x, y, 1 - c), device_id_type=MESH)
        cp.start()
        cp.wait()

    return _comm_call(body, lambda x, y, c: [(x, y, 1 - c)], jax.ShapeDtypeStruct((N_CHIPS, half, D_MODEL), BF16), 1,
                      partial, name, collective_id)


def _rs_add_halves(partial, other, core, name, after):
    half = other.shape[1]
    t = half // 2
    steps = half // t

    def body(core_ref, a_ref, b_ref, after_ref, o_ref):
        del after_ref
        o_ref[...] = (a_ref[...].astype(F32) + b_ref[...].astype(F32)).astype(BF16)

    return pl.pallas_call(
        body, name=name,
        grid_spec=pltpu.PrefetchScalarGridSpec(
            num_scalar_prefetch=1, grid=(N_CHIPS, steps),
            in_specs=[pl.BlockSpec((1, t, D_MODEL), lambda j, i, core_ref: (j, core_ref[0] * steps + i, 0)),
                      pl.BlockSpec((1, t, D_MODEL), lambda j, i, core_ref: (j, i, 0)), ANY],
            out_specs=pl.BlockSpec((1, t, D_MODEL), lambda j, i, core_ref: (j, i, 0))),
        out_shape=jax.ShapeDtypeStruct((N_CHIPS, half, D_MODEL), BF16),
        compiler_params=_params(n_axes=2),
    )(core, partial, other, after)


def _rs_exchange_chips(pre, name, collective_id=None):
    def body(s_ref, r_ref, send, recv):
        x, y, c, chips = _position()

        def copy(k, chunk, to):
            return pltpu.make_async_remote_copy(src_ref=s_ref.at[chunk], dst_ref=r_ref.at[k], send_sem=send.at[k],
                                                recv_sem=recv.at[k], device_id=to, device_id_type=MESH)

        sends = [copy(k, 2 * chip[0] + chip[1], (*chip, c)) for k, chip in enumerate(chips)]
        for cp in sends:
            cp.start()
        for cp in sends:
            cp.wait()

    return _comm_call(body, lambda x, y, c: [(1 - x, y, c), (x, 1 - y, c), (1 - x, 1 - y, c)],
                      jax.ShapeDtypeStruct((3, pre.shape[1], D_MODEL), BF16), 3, pre, name, collective_id)


def _rs_sum_chips(pre, received, place, name, after):
    half = pre.shape[1]
    t = half // 2 if half > 512 else half
    steps = half // t

    def body(place_ref, own_ref, r_ref, after_ref, o_ref):
        del after_ref
        acc = own_ref[0].astype(F32)
        for k in range(3):
            acc = acc + r_ref[k].astype(F32)
        o_ref[...] = acc

    return pl.pallas_call(
        body, name=name,
        grid_spec=pltpu.PrefetchScalarGridSpec(
            num_scalar_prefetch=1, grid=(steps,),
            in_specs=[pl.BlockSpec((1, t, D_MODEL), lambda i, place_ref: (place_ref[0], i, 0)),
                      pl.BlockSpec((3, t, D_MODEL), lambda i, place_ref: (0, i, 0)), ANY],
            out_specs=pl.BlockSpec((t, D_MODEL), lambda i, place_ref: (place_ref[1] * steps + i, 0))),
        out_shape=jax.ShapeDtypeStruct((2 * half, D_MODEL), F32),
        compiler_params=_params(),
    )(place, pre, received, after)


def _half_swap(g_ref, core, to, send, recv, k):
    half = g_ref.shape[0] // 2
    rows = g_ref.at[pl.ds(pl.multiple_of(core * half, 8), half), :]
    return pltpu.make_async_remote_copy(src_ref=rows, dst_ref=rows, send_sem=send.at[k], recv_sem=recv.at[k],
                                        device_id=to, device_id_type=MESH)


def _rs_finish_rows(grads, name, after):
    def body(f_ref, after_ref, g_ref, send, recv):
        del f_ref, after_ref
        x, y, c, _ = _position()
        mine = _half_swap(g_ref, c, (x, y, 1 - c), send, recv, 0)
        mine.start()
        _half_swap(g_ref, 1 - c, (x, y, c), send, recv, 0).wait_recv()
        mine.wait_send()

    return pl.pallas_call(
        body, name=name, in_specs=[ANY, ANY], out_specs=ANY, input_output_aliases={0: 0},
        out_shape=jax.ShapeDtypeStruct(grads.shape, F32),
        scratch_shapes=[pltpu.SemaphoreType.DMA((1,)), pltpu.SemaphoreType.DMA((1,))],
    )(grads, after)


def _small_gather(small, collective_id):
    def body(s_ref, t_ref, send, recv):
        x, y, c, chips = _position()
        sibling = (x, y, 1 - c)

        def slot(px, py, pc):
            return t_ref.at[4 * px + 2 * py + pc]

        def copy(k, block, to, src=None):
            return pltpu.make_async_remote_copy(src_ref=slot(*block) if src is None else src, dst_ref=slot(*block),
                                                send_sem=send.at[k], recv_sem=recv.at[k], device_id=to, device_id_type=MESH)

        own = pltpu.make_async_copy(s_ref, slot(x, y, c), send.at[7])
        own.start()
        first = [copy(0, (x, y, c), sibling, src=s_ref)]
        first += [copy(1 + k, (x, y, c), (*chip, c), src=s_ref) for k, chip in enumerate(chips)]
        for cp in first:
            cp.start()
        passed = []
        for k, chip in enumerate(chips):
            copy(1 + k, (*chip, c), (x, y, c)).wait_recv()
            fwd = copy(4 + k, (*chip, c), sibling)
            fwd.start()
            passed.append(fwd)
        copy(0, sibling, (x, y, c)).wait_recv()
        for k, chip in enumerate(chips):
            copy(4 + k, (*chip, 1 - c), (x, y, c)).wait_recv()
        for cp in first + passed:
            cp.wait_send()
        own.wait()

    peers_of = lambda x, y, c: [(x, y, 1 - c), (1 - x, y, c), (x, 1 - y, c), (1 - x, 1 - y, c)]
    return _comm_call(body, peers_of, jax.ShapeDtypeStruct((N_DEV, SMALL_ROWS, 128), F32), 8, small, "small_gather",
                      collective_id)


def _adam_update(w, g, m, v):
    m_new = ADAM_B1 * m + (1.0 - ADAM_B1) * g
    v_new = ADAM_B2 * v + (1.0 - ADAM_B2) * (g * g)
    m_hat = m_new / (1.0 - ADAM_B1 ** ADAM_STEP)
    v_hat = v_new / (1.0 - ADAM_B2 ** ADAM_STEP)
    return -ADAM_LR * (m_hat / (jnp.sqrt(v_hat) + ADAM_EPS) + ADAM_WD * w), m_new, v_new


def _adamw(w, g_rows, row_off, m, v, name):
    rows, cols = w.shape
    t = rows if rows <= 320 else (rows // 2 if rows % 256 else 256)

    def body(w_ref, g_ref, m_ref, v_ref, go_ref, d_ref, nm_ref, nv_ref):
        g = g_ref[...]
        go_ref[...] = g
        d_ref[...], nm_ref[...], nv_ref[...] = _adam_update(w_ref[...], g, m_ref[...], v_ref[...])

    blk = pl.BlockSpec((t, cols), lambda i: (i, 0))
    assert row_off % 8 == 0 and t % 8 == 0
    g_blk = pl.BlockSpec((pl.Element(t), pl.Element(cols)), lambda i: (pl.multiple_of(row_off + i * t, 8), 0))
    shape = jax.ShapeDtypeStruct((rows, cols), F32)
    return pl.pallas_call(
        body, name=name, grid=(rows // t,), in_specs=[blk, g_blk, blk, blk], out_specs=[blk] * 4, out_shape=[shape] * 4,
        compiler_params=_params(),
    )(w, g_rows, m, v)


SMALL_PARAMS = [("g_attn", (1, D_MODEL), 8), ("g_q", (1, HEAD_DIM), None), ("g_k", (1, HEAD_DIM), None),
                ("sinks", (1, N_Q_HEADS), None), ("rel_bias", (N_BUCKETS, N_Q_HEADS), None), ("w_pool", (512, 128), None),
                ("pool_scale", (1, POOL_WIDTH), 4), ("g_ffn", (1, D_MODEL), 8), ("g_ple", (1, D_MODEL), 8)]


def _adamw_small(tables, wmv):
    n_par = len(SMALL_PARAMS)

    def body(*refs):
        t_ref = refs[0]
        ins = refs[1:1 + 3 * n_par]
        loss_ref = refs[1 + 3 * n_par]
        outs = refs[2 + 3 * n_par:-1]
        tot_ref = refs[-1]
        total = t_ref[0]
        for d in range(1, N_DEV):
            total = total + t_ref[d]
        tot_ref[...] = total
        loss_ref[...] = tot_ref[pl.ds(SMALL["loss"], 1), 0:1]
        for i, (name, shape, split) in enumerate(SMALL_PARAMS):
            g_ref, d_ref, nm_ref, nv_ref = outs[4 * i:4 * i + 4]
            row = SMALL[name]
            if split:
                for k in range(split):
                    g_ref[:, 128 * k:128 * k + 128] = tot_ref[pl.ds(row + k, 1), :]
            else:
                g_ref[...] = tot_ref[pl.ds(row, shape[0]), 0:shape[1]]
            w_ref, m_ref, v_ref = ins[3 * i:3 * i + 3]
            d_ref[...], nm_ref[...], nv_ref[...] = _adam_update(w_ref[...], g_ref[...], m_ref[...], v_ref[...])

    shapes = [jax.ShapeDtypeStruct((1, 1), F32)]
    for _, shape, _ in SMALL_PARAMS:
        shapes += [jax.ShapeDtypeStruct(shape, F32)] * 4
    flat = [a for triple in wmv for a in triple]
    res = pl.pallas_call(
        body, name="adamw_small", in_specs=[VMEM_WHOLE] * (1 + 3 * n_par), out_specs=[VMEM_WHOLE] * len(shapes),
        out_shape=shapes, scratch_shapes=[pltpu.VMEM((SMALL_ROWS, 128), F32)],
    )(tables, *flat)
    return res[0], [res[1 + 4 * i:5 + 4 * i] for i in range(n_par)]


def _pack_ple_proj(shard):
    return shard.reshape(4, 64, 256).transpose(1, 0, 2).reshape(64, D_MODEL)


class _Reduction:
    def __init__(self, tag, place, ids=(None, None)):
        self.tag, self.place, self.ids = tag, place, ids

    def start(self, partial):
        self.partial = partial
        self.other = _rs_swap_halves(partial, "rs_swap_" + self.tag, self.ids[0])
        return partial

    def middle(self, after):
        self.pre = _rs_add_halves(self.partial, self.other, self.place[1:], "rs_add_" + self.tag, after)
        self.received = _rs_exchange_chips(self.pre, "rs_exchange_" + self.tag, self.ids[1])
        return self.pre

    def finish(self, after):
        return _rs_sum_chips(self.pre, self.received, self.place, "rs_sum_" + self.tag, after)


def _local_grads(x2, p2, tgt, wts, g_attn_norm, g_q, g_k, attn_sinks, rel_bias, w_pool, pool_scale, g_ffn_norm, g_ple_norm,
                 reduce_a):
    w_early, w_late = wts
    w_in = w_out = w_early
    bucket = jnp.asarray(_bucket_table())
    gq = jnp.tile(g_q, (1, 2))
    gk = jnp.tile(g_k, (1, 2))
    wpool = w_pool[0].astype(BF16)
    sinks = attn_sinks[0]
    bias_st = _bias_build(rel_bias.T, bucket)

    hn1 = _first_norm(x2, g_attn_norm)
    zqk, u, kn, vb, qst = _attn_in(hn1, gq, gk, w_in)
    ost = _attn_fwd(qst, kn, vb, bias_st, sinks)
    pooled, mix, h1, hn2 = _mix_out(u, ost, x2, w_out, wpool, pool_scale, g_ffn_norm)
    loss_v, dgate, dup, act, dh2, hn3, dgl, dw_plp, dh1, dg_ffn, dg_ple = _ffn_ple(hn2, h1, p2, tgt, w_late, g_ffn_norm,
                                                                                      g_ple_norm)

    late0, late_rows = GATHER_PARTS[1][0], SLAB_ROWS - GATHER_PARTS[1][0]
    partial_a = None
    for names, lefts, right in ((("gateT", "upT"), [dgate, dup], hn2), (("down",), [act], dh2), (("plg",), [hn3], dgl)):
        partial_a = _dw(lefts, right, "dw_" + names[0], partial_a, late_rows, [SLAB[name][0] - late0 for name in names])
    dw_plp = dw_plp.reshape(4, 64, N_CHIPS, 256).transpose(2, 1, 0, 3).reshape(N_CHIPS, 64, D_MODEL)
    partial_a = reduce_a.start(lax.dynamic_update_slice(partial_a, dw_plp, (0, SLAB["plp"][0] - late0, 0)))
    dost, du, dw_pool, dscale, partial_b = _mix_out_bwd(dh1, w_out, pooled, wpool, pool_scale, mix, partial_a)
    pre_a = reduce_a.middle(du)
    dqst, dk, dv, dbias, dsink_rows = _attn_bwd(qst, kn, vb, dost, bias_st, sinks, pre_a)
    dx, dg_attn, dgq, dgk, partial_b = _attn_in_bwd(dqst, zqk, dk, dv, du, x2, dh1, hn1, partial_b, w_in, g_attn_norm, gq, gk)

    small = _small_pack(dg_attn, dg_ffn, dg_ple, dscale, dgq, dgk, dbias, dsink_rows, bucket, loss_v, dw_pool)
    return dx, partial_b, small


def kernel(x, p, w_in, w_out, g_attn_norm, g_q, g_k, attn_sinks, rel_bias, w_pool, pool_scale, g_ffn_norm, w_gate, w_up, w_down, g_ple_norm, w_ple_gate, w_ple_proj, loss_target, m_w_in, m_w_out, m_g_attn_norm, m_g_q, m_g_k, m_attn_sinks, m_rel_bias, m_w_pool, m_pool_scale, m_g_ffn_norm, m_w_gate, m_w_up, m_w_down, m_g_ple_norm, m_w_ple_gate, m_w_ple_proj, v_w_in, v_w_out, v_g_attn_norm, v_g_q, v_g_k, v_attn_sinks, v_rel_bias, v_w_pool, v_pool_scale, v_g_ffn_norm, v_w_gate, v_w_up, v_w_down, v_g_ple_norm, v_w_ple_gate, v_w_ple_proj):
    core = lax.axis_index("c").astype(jnp.int32).reshape(1)
    me = (2 * lax.axis_index("x") + lax.axis_index("y")).astype(jnp.int32).reshape(1)

    local_parts = [jnp.concatenate(pieces, axis=0).astype(BF16) for pieces in (
        [w_in[0].T, w_out[0]], [w_gate[0].T, w_up[0].T, w_down[0], w_ple_gate[0], _pack_ple_proj(w_ple_proj[0])])]
    wts = [(_ag_weights(local, 0, local.shape[0], name, collective_id), local, me)
           for local, name, collective_id in zip(local_parts, ("ag_early", "ag_late"), (1, 2))]

    place = jnp.concatenate([me, core])
    reduce_a = _Reduction("a", place, ids=(3, 4))
    dx, partial_b, small = _local_grads(x[0], p[0, 0], loss_target[0], wts, g_attn_norm, g_q, g_k, attn_sinks, rel_bias,
                                        w_pool, pool_scale, g_ffn_norm, g_ple_norm, reduce_a)
    reduce_b = _Reduction("b", place, ids=(6, 7))
    reduce_b.start(partial_b)
    small_all = _small_gather(small, 8)
    summed_a = reduce_a.finish(small)
    pre_b = reduce_b.middle(summed_a)
    grads_a = _rs_finish_rows(summed_a, "rs_finish_a", pre_b)

    late0 = GATHER_PARTS[1][0]

    def rows(name):
        return grads_a, SLAB[name][0] - late0

    plp_rows = grads_a[SLAB["plp"][0] - late0:]
    big = {
        "w_gate": (w_gate, m_w_gate, v_w_gate, rows("gateT"), True),
        "w_up": (w_up, m_w_up, v_w_up, rows("upT"), True),
        "w_down": (w_down, m_w_down, v_w_down, rows("down"), False),
        "w_ple_gate": (w_ple_gate, m_w_ple_gate, v_w_ple_gate, rows("plg"), False),
        "w_ple_proj": (w_ple_proj, m_w_ple_proj, v_w_ple_proj,
                       (plp_rows.reshape(64, 4, 256).transpose(1, 0, 2).reshape(PLE_DIM, PLE_DIM), 0), False),
        "w_out": (w_out, m_w_out, v_w_out, None, False),
        "w_in": (w_in, m_w_in, v_w_in, None, True),
    }
    small_params = {
        "g_attn_norm": (g_attn_norm, m_g_attn_norm, v_g_attn_norm), "g_q": (g_q, m_g_q, v_g_q), "g_k": (g_k, m_g_k, v_g_k),
        "attn_sinks": (attn_sinks, m_attn_sinks, v_attn_sinks), "rel_bias": (rel_bias, m_rel_bias, v_rel_bias),
        "w_pool": tuple(a.reshape(512, 128) for a in (w_pool, m_w_pool, v_w_pool)),
        "pool_scale": (pool_scale, m_pool_scale, v_pool_scale), "g_ffn_norm": (g_ffn_norm, m_g_ffn_norm, v_g_ffn_norm),
        "g_ple_norm": (g_ple_norm, m_g_ple_norm, v_g_ple_norm),
    }

    grads, deltas, new_ms, new_vs = {}, {}, {}, {}
    out = grads_b = None
    for name, (w, m, v, g_src, transposed) in big.items():
        if g_src is None:
            if grads_b is None:
                grads_b = _rs_finish_rows(reduce_b.finish(out[-1]), "rs_finish_b", out[-1])
            g_src = (grads_b, SLAB["out" if name == "w_out" else "inT"][0])
        view = (lambda a: a.T) if transposed else (lambda a: a)
        out = _adamw(view(w[0]), *g_src, view(m[0]), view(v[0]), "adamw_" + name)
        grads[name], deltas[name], new_ms[name], new_vs[name] = (view(a)[None] for a in out)

    loss, small_out = _adamw_small(small_all, list(small_params.values()))
    for name, (g2, d, nm, nv) in zip(small_params, small_out):
        shape = w_pool.shape if name == "w_pool" else g2.shape
        grads[name], deltas[name], new_ms[name], new_vs[name] = (a.reshape(shape) for a in (g2, d, nm, nv))

    order = ["w_in", "w_out", "g_attn_norm", "g_q", "g_k", "attn_sinks", "rel_bias", "w_pool", "pool_scale", "g_ffn_norm",
             "w_gate", "w_up", "w_down", "g_ple_norm", "w_ple_gate", "w_ple_proj"]
    return (loss.reshape(()), dx[None], *[grads[n] for n in order], *[deltas[n] for n in order],
            *[new_ms[n] for n in order], *[new_vs[n] for n in order])
```

```python
import numpy as np
import jax
import jax.numpy as jnp
from jax import lax
from jax.experimental import pallas as pl
from jax.experimental.pallas import tpu as pltpu
from jax.experimental.pallas import tpu_sc as plsc

F32 = jnp.float32
BF16 = jnp.bfloat16
MESH = pl.DeviceIdType.MESH

D_MODEL = 1024
HEAD_DIM = 64
N_Q_HEADS = 8
ATTN_WIDTH = 512
POOL_WIDTH = 512
IN_WIDTH = 1280
D_FF = 2816
PLE_DIM = 256
FF_CHUNK = 1408
N_FF_CHUNKS = D_FF // FF_CHUNK
BLOCK = 128
N_BUCKETS = 32
MAX_DISTANCE = 128
EPS = 1e-6
NEG = -1e30
N_CHIPS = 4
N_DEV = 8

ADAM_LR = 0.001
ADAM_B1 = 0.9
ADAM_B2 = 0.999
ADAM_EPS = 1e-08
ADAM_WD = 0.01
ADAM_STEP = 10

SLAB = {"inT": (0, 320), "out": (320, 256), "gateT": (576, 704), "upT": (1280, 704), "down": (1984, 704),
        "plg": (2688, 256), "plp": (2944, 64)}
SLAB_ROWS = 3008
GATHER_PARTS = ((0, 576), (576, SLAB_ROWS))
POOL_HALO = 24

SMALL = {"g_attn": 0, "g_ffn": 8, "g_ple": 16, "pool_scale": 24, "g_q": 28, "g_k": 29, "sinks": 30, "loss": 31,
         "rel_bias": 32, "w_pool": 64}
SMALL_ROWS = 576

VMEM_LIMIT_BIG = 60 * 1024 * 1024
VMEM_LIMIT = 48 * 1024 * 1024


def _params(vmem=VMEM_LIMIT, n_axes=1):
    return pltpu.CompilerParams(dimension_semantics=("arbitrary",) * n_axes, vmem_limit_bytes=vmem)


def _dot(a, b, ca, cb):
    return lax.dot_general(a, b, (((ca,), (cb,)), ((), ())), preferred_element_type=F32)


def _full(shape):
    return pl.BlockSpec(shape, lambda i: (0,) * len(shape))


ANY = pl.BlockSpec(memory_space=pl.ANY)
VMEM_WHOLE = pl.BlockSpec(memory_space=pltpu.VMEM)


W_SPECS = [ANY, ANY, pl.BlockSpec(memory_space=pltpu.SMEM)]


def _load_rows(w_refs, name, dst_ref, sems):
    slab_ref, local_ref, me_ref = w_refs
    off, rows = SLAB[name]
    slab_off = off - max(start for start, _ in GATHER_PARTS if start <= off)
    me = me_ref[0]
    for phase in ("start", "wait"):
        for j in range(N_CHIPS):
            dst = dst_ref.at[pl.ds(j * rows, rows), :]
            theirs = pltpu.make_async_copy(slab_ref.at[j, pl.ds(slab_off, rows), :], dst, sems.at[j])
            own = pltpu.make_async_copy(local_ref.at[pl.ds(slab_off, rows), :], dst, sems.at[j])

            @pl.when(me == j)
            def _():
                getattr(own, phase)()

            @pl.when(me != j)
            def _():
                getattr(theirs, phase)()


def _rms_fwd(x, g):
    r = lax.rsqrt(jnp.mean(x * x, axis=-1, keepdims=True) + EPS)
    return x * r * g


def _rms_bwd(x, g, dy):
    r = lax.rsqrt(jnp.mean(x * x, axis=-1, keepdims=True) + EPS)
    xn = x * r
    dyg = dy * g
    dx = r * (dyg - xn * jnp.mean(dyg * xn, axis=-1, keepdims=True))
    return dx, jnp.sum(dy * xn, axis=0, keepdims=True)


def _half_sum(v, lo):
    s_lo = jnp.sum(jnp.where(lo, v, 0.0), axis=-1, keepdims=True)
    s_hi = jnp.sum(jnp.where(lo, 0.0, v), axis=-1, keepdims=True)
    return jnp.where(lo, s_lo, s_hi)


def _half_sum_mxu(v):
    upper = lax.broadcasted_iota(jnp.int32, (128, 128), 0) < 64
    left = lax.broadcasted_iota(jnp.int32, (128, 128), 1) < 64
    ones = jnp.where(upper == left, 1.0, 0.0).astype(BF16)
    high = v.astype(BF16)
    low = (v - high.astype(F32)).astype(BF16)
    return _dot(high, ones, 1, 0) + _dot(low, ones, 1, 0)


def _pair_norm(zp, g, lo):
    r = lax.rsqrt(_half_sum(zp * zp, lo) * (1.0 / HEAD_DIM) + EPS)
    return zp * r * g


def _pair_norm_bwd(zp, g, dy):
    r = lax.rsqrt(_half_sum_mxu(zp * zp) * (1.0 / HEAD_DIM) + EPS)
    xn = zp * r
    dyg = dy * g
    dx = r * (dyg - xn * (_half_sum_mxu(dyg * xn) * (1.0 / HEAD_DIM)))
    return dx, jnp.sum(dy * xn, axis=0, keepdims=True)


def _to_stacked(pair, group, lo):
    rolled = pltpu.roll(pair, 64, axis=1)
    if group == 0:
        return jnp.where(lo, pair, 0.0), jnp.where(lo, rolled, 0.0)
    return jnp.where(lo, 0.0, rolled), jnp.where(lo, 0.0, pair)


def _from_stacked(even, odd, group, lo):
    if group == 0:
        return jnp.where(lo, even, pltpu.roll(odd, 64, axis=1))
    return jnp.where(lo, pltpu.roll(even, 64, axis=1), odd)


def _sigmoid(v):
    return 1.0 / (1.0 + jnp.exp(-v))


def _pool_counts(tile, n_rows):
    t1 = tile * n_rows + lax.broadcasted_iota(jnp.int32, (n_rows, POOL_WIDTH), 0) + 1
    lane = lax.broadcasted_iota(jnp.int32, (n_rows, POOL_WIDTH), 1)
    win = jnp.where(lane < 128, 2, jnp.where(lane < 256, 4, jnp.where(lane < 384, 8, 16)))
    return jnp.minimum(t1, win).astype(F32)


def _first_norm(x2, g_attn):
    s_len = x2.shape[0]
    t = 512

    def body(x_ref, g_ref, hn_ref):
        hn_ref[...] = _rms_fwd(x_ref[...], g_ref[...]).astype(BF16)

    row = pl.BlockSpec((t, D_MODEL), lambda i: (i, 0))
    return pl.pallas_call(
        body, name="first_norm", grid=(s_len // t,), in_specs=[row, _full((1, D_MODEL))], out_specs=row,
        out_shape=jax.ShapeDtypeStruct((s_len, D_MODEL), BF16), compiler_params=_params(),
    )(x2, g_attn)


def _attn_in(hn1, gq, gk, wts):
    s_len = hn1.shape[0]
    t = 512

    def body(hn_ref, gq_ref, gk_ref, sl_ref, lo_ref, me_ref, zqk_ref, u_ref, kn_ref, v_ref, qst_ref, w_ref, sems):
        @pl.when(pl.program_id(0) == 0)
        def _():
            _load_rows((sl_ref, lo_ref, me_ref), "inT", w_ref, sems)

        z = _dot(hn_ref[...], w_ref[...], 1, 1)
        zqk_ref[...] = z[:, :640]
        u_ref[...] = z[:, 768:]
        v_ref[...] = z[:, 640:768].astype(BF16)
        lo = lax.broadcasted_iota(jnp.int32, (t, 128), 1) < 64
        kn_ref[...] = _pair_norm(z[:, 512:640], gk_ref[...], lo).astype(BF16)
        for p in range(4):
            qn = _pair_norm(z[:, 128 * p:128 * p + 128], gq_ref[...], lo)
            even, odd = _to_stacked(qn, p // 2, lo)
            qst_ref[2 * p] = even.astype(BF16)
            qst_ref[2 * p + 1] = odd.astype(BF16)

    row = lambda w: pl.BlockSpec((t, w), lambda i: (i, 0))
    return pl.pallas_call(
        body, name="attn_in", grid=(s_len // t,),
        in_specs=[row(D_MODEL), _full((1, 128)), _full((1, 128))] + W_SPECS,
        out_specs=[row(640), row(POOL_WIDTH), row(128), row(128), pl.BlockSpec((N_Q_HEADS, t, 128), lambda i: (0, i, 0))],
        out_shape=[jax.ShapeDtypeStruct((s_len, 640), F32), jax.ShapeDtypeStruct((s_len, POOL_WIDTH), F32),
                   jax.ShapeDtypeStruct((s_len, 128), BF16), jax.ShapeDtypeStruct((s_len, 128), BF16),
                   jax.ShapeDtypeStruct((N_Q_HEADS, s_len, 128), BF16)],
        scratch_shapes=[pltpu.VMEM((IN_WIDTH, D_MODEL), BF16), pltpu.SemaphoreType.DMA((N_CHIPS,))],
        compiler_params=_params(),
    )(hn1, gq, gk, *wts)


def _bucket_table():
    i_idx = np.arange(BLOCK)[:, None]
    j_idx = np.arange(2 * BLOCK)[None, :]
    d = BLOCK + i_idx - j_idx
    n = np.maximum(d, 0)
    max_exact = N_BUCKETS // 2
    nf = np.maximum(n, 1).astype(np.float64)
    large = max_exact + (np.log(nf / max_exact) / np.log(MAX_DISTANCE / max_exact) * (N_BUCKETS - max_exact)).astype(np.int64)
    large = np.minimum(large, N_BUCKETS - 1)
    bucket = np.where(n < max_exact, n, large)
    return np.where((d >= 0) & (d < BLOCK), bucket, -1).astype(np.int32)


def _bias_build(rel_bias_t, bucket):
    def body(rb_ref, bucket_ref, out_ref):
        bk = bucket_ref[...]
        for h in range(N_Q_HEADS):
            acc = jnp.full((BLOCK, 2 * BLOCK), NEG, F32)
            for b in range(N_BUCKETS):
                acc = jnp.where(bk == b, rb_ref[h, b], acc)
            out_ref[0, pl.ds(h * BLOCK, BLOCK), :] = acc
            out_ref[1, pl.ds(h * BLOCK, BLOCK), :] = acc
            out_ref[1, pl.ds(h * BLOCK, BLOCK), 0:BLOCK] = jnp.full((BLOCK, BLOCK), NEG, F32)

    return pl.pallas_call(
        body, name="bias_build",
        in_specs=[pl.BlockSpec(memory_space=pltpu.SMEM), VMEM_WHOLE], out_specs=VMEM_WHOLE,
        out_shape=jax.ShapeDtypeStruct((2, N_Q_HEADS * BLOCK, 2 * BLOCK), F32),
    )(rel_bias_t, bucket)


def _head_softmax(s_ref, bias_ref, sink_ref, h):
    rows = pl.ds(pl.multiple_of(h * BLOCK, BLOCK), BLOCK)
    s = s_ref[rows, :] * (HEAD_DIM ** -0.5) + bias_ref[rows, :]
    sink = sink_ref[h]
    m = jnp.maximum(jnp.max(s, axis=-1, keepdims=True), sink)
    p = jnp.exp(s - m)
    e_sink = jnp.exp(sink - m)
    inv = 1.0 / (jnp.sum(p, axis=-1, keepdims=True) + e_sink)
    return rows, p * inv, e_sink * inv


ATTN_STEP_BLOCKS = 4
BAND = (N_Q_HEADS * BLOCK, 2 * BLOCK)


def _attn_specs():
    nb = ATTN_STEP_BLOCKS
    stacked = pl.BlockSpec((N_Q_HEADS, nb * BLOCK, 128), lambda i: (0, i, 0))
    kv = [pl.BlockSpec((BLOCK, 128), lambda i: (jnp.maximum(nb * i - 1, 0), 0)), pl.BlockSpec((nb * BLOCK, 128), lambda i: (i, 0))]
    consts = [_full((2,) + BAND), pl.BlockSpec(memory_space=pltpu.SMEM)]
    return stacked, kv, consts


def _step_blocks(i, kp_ref, kc_ref, vp_ref, vc_ref, bias_ref):
    blocks = []
    for b in range(ATTN_STEP_BLOCKS):
        if b == 0:
            k2 = jnp.concatenate([kp_ref[...], kc_ref[pl.ds(0, BLOCK), :]], axis=0)
            v2 = jnp.concatenate([vp_ref[...], vc_ref[pl.ds(0, BLOCK), :]], axis=0)
            bias = bias_ref.at[jnp.where(i == 0, 1, 0)]
        else:
            k2, v2, bias = kc_ref[pl.ds((b - 1) * BLOCK, 2 * BLOCK), :], vc_ref[pl.ds((b - 1) * BLOCK, 2 * BLOCK), :], bias_ref.at[0]
        blocks.append((pl.ds(b * BLOCK, BLOCK), k2, v2, bias))
    return blocks


def _head_lane_mask():
    rows = lax.broadcasted_iota(jnp.int32, (N_Q_HEADS * BLOCK, 128), 0)
    lanes = lax.broadcasted_iota(jnp.int32, (N_Q_HEADS * BLOCK, 128), 1)
    return (rows < 4 * BLOCK) == (lanes < 64)


def _attn_fwd(qst, kn, vb, bias_st, sinks):
    s_len = kn.shape[0]

    def body(q_ref, kp_ref, kc_ref, vp_ref, vc_ref, bias_ref, sink_ref, o_ref, s_ref, p_ref):
        for b, (rows, k2, v2, bias) in enumerate(_step_blocks(pl.program_id(0), kp_ref, kc_ref, vp_ref, vc_ref, bias_ref)):
            s_b, p_b = s_ref.at[b], p_ref.at[b]
            s_b[...] = _dot(q_ref[:, rows, :].reshape(N_Q_HEADS * BLOCK, 128), k2, 1, 1)

            def head(h, carry):
                head_rows, probs, _ = _head_softmax(s_b, bias, sink_ref, h)
                p_b[head_rows, :] = probs.astype(BF16)
                return carry

            lax.fori_loop(0, N_Q_HEADS, head, 0, unroll=True)
            o = jnp.where(_head_lane_mask(), _dot(p_b[...], v2, 1, 0), 0.0)
            o_ref[:, rows, :] = o.astype(BF16).reshape(N_Q_HEADS, BLOCK, 128)

    stacked, kv, consts = _attn_specs()
    return pl.pallas_call(
        body, name="attn_fwd", grid=(s_len // (ATTN_STEP_BLOCKS * BLOCK),),
        in_specs=[stacked] + kv + kv + consts, out_specs=stacked,
        out_shape=jax.ShapeDtypeStruct((N_Q_HEADS, s_len, 128), BF16),
        scratch_shapes=[pltpu.VMEM((ATTN_STEP_BLOCKS,) + BAND, F32), pltpu.VMEM((ATTN_STEP_BLOCKS,) + BAND, BF16)],
        compiler_params=_params(),
    )(qst, kn, kn, vb, vb, bias_st, sinks)


def _mix_out(u, ost, x2, wts, wpool, pool_scale, g_ffn):
    s_len = x2.shape[0]
    t = 512
    n = t + 16

    def body(u_ref, o_ref, x_ref, sl_ref, lo_ref, me_ref, wp_ref, sc_ref, g_ref, pooled_ref, mix_ref, h1_ref, hn_ref,
             w_ref, ext_ref, st_ref, sems):
        i = pl.program_id(0)

        @pl.when(i == 0)
        def _():
            _load_rows((sl_ref, lo_ref, me_ref), "out", w_ref, sems)
            ext_ref[...] = jnp.zeros_like(ext_ref)
            st_ref[...] = jnp.zeros_like(st_ref)

        u_tile = u_ref[...]
        ext_ref[pl.ds(POOL_HALO, t), :] = u_tile
        st_ref[pl.ds(8, n), :] = ext_ref[pl.ds(8, n), :] + ext_ref[pl.ds(7, n), :]
        st_ref[pl.ds(8, n), 128:] = st_ref[pl.ds(8, n), 128:] + st_ref[pl.ds(6, n), 128:]
        st_ref[pl.ds(8, n), 256:] = st_ref[pl.ds(8, n), 256:] + st_ref[pl.ds(4, n), 256:]
        st_ref[pl.ds(8, n), 384:] = st_ref[pl.ds(8, n), 384:] + st_ref[pl.ds(0, n), 384:]
        ext_ref[pl.ds(0, POOL_HALO), :] = ext_ref[pl.ds(t, POOL_HALO), :]
        pooled = (st_ref[pl.ds(POOL_HALO, t), :] / _pool_counts(i, t) - u_tile).astype(BF16)
        pooled_ref[...] = pooled
        for g in range(4):
            cols = slice(128 * g, 128 * g + 128)
            y = _dot(pooled[:, cols], wp_ref[g], 1, 0) * sc_ref[:, cols]
            mix_ref[:, ATTN_WIDTH + 128 * g:ATTN_WIDTH + 128 * g + 128] = y.astype(BF16)
        lo = lax.broadcasted_iota(jnp.int32, (t, 128), 1) < 64
        for p in range(4):
            a = _from_stacked(o_ref[2 * p].astype(F32), o_ref[2 * p + 1].astype(F32), p // 2, lo)
            mix_ref[:, 128 * p:128 * p + 128] = a.astype(BF16)
        h1 = x_ref[...] + _dot(mix_ref[...], w_ref[...], 1, 0)
        h1_ref[...] = h1
        hn_ref[...] = _rms_fwd(h1, g_ref[...]).astype(BF16)

    row = lambda w: pl.BlockSpec((t, w), lambda i: (i, 0))
    return pl.pallas_call(
        body, name="mix_out", grid=(s_len // t,),
        in_specs=[row(POOL_WIDTH), pl.BlockSpec((N_Q_HEADS, t, 128), lambda i: (0, i, 0)), row(D_MODEL)] + W_SPECS
        + [_full((4, 128, 128)), _full((1, POOL_WIDTH)), _full((1, D_MODEL))],
        out_specs=[row(POOL_WIDTH), row(D_MODEL), row(D_MODEL), row(D_MODEL)],
        out_shape=[jax.ShapeDtypeStruct((s_len, POOL_WIDTH), BF16), jax.ShapeDtypeStruct((s_len, D_MODEL), BF16),
                   jax.ShapeDtypeStruct((s_len, D_MODEL), F32), jax.ShapeDtypeStruct((s_len, D_MODEL), BF16)],
        scratch_shapes=[pltpu.VMEM((D_MODEL, D_MODEL), BF16), pltpu.VMEM((t + POOL_HALO, POOL_WIDTH), F32),
                        pltpu.VMEM((t + POOL_HALO, POOL_WIDTH), F32), pltpu.SemaphoreType.DMA((N_CHIPS,))],
        compiler_params=_params(),
    )(u, ost, x2, *wts, wpool, pool_scale, g_ffn)


def _ffn_ple(hn2, h1, p2, tgt, wts, g_ffn, g_ple):
    s_len = h1.shape[0]
    t = 256
    n_tiles = s_len // t

    def body(hn_ref, h1_ref, p_ref, tgt_ref, sl_ref, lo_ref, me_ref, gf_ref, gp_ref,
             loss_ref, dgate_ref, dup_ref, act_ref, dh2b_ref, hn3_ref, dgl_ref, dwp_ref, dh1_ref, dgf_ref, dgp_ref,
             wg_ref, wu_ref, wd_ref, wl_ref, wp_ref, packed_ref, gate_s, up_s, loss_acc, dwp_acc, sems):
        i = pl.program_id(0)

        @pl.when(i == 0)
        def _():
            w_refs = (sl_ref, lo_ref, me_ref)
            _load_rows(w_refs, "gateT", wg_ref, sems)
            _load_rows(w_refs, "upT", wu_ref, sems)
            _load_rows(w_refs, "down", wd_ref, sems)
            _load_rows(w_refs, "plg", wl_ref, sems)
            _load_rows(w_refs, "plp", packed_ref, sems)
            for j in range(N_CHIPS):
                for q in range(4):
                    wp_ref[pl.ds(64 * q, 64), 256 * j:256 * j + 256] = packed_ref[pl.ds(64 * j, 64), 256 * q:256 * q + 256]
            loss_acc[...] = jnp.zeros_like(loss_acc)
            dgf_ref[...] = jnp.zeros_like(dgf_ref)
            dgp_ref[...] = jnp.zeros_like(dgp_ref)

        hn = hn_ref[...]
        h1v = h1_ref[...]
        h2 = h1v
        for ch in range(N_FF_CHUNKS):
            rows = pl.ds(ch * FF_CHUNK, FF_CHUNK)
            gate = _dot(hn, wg_ref[rows, :], 1, 1)
            up = _dot(hn, wu_ref[rows, :], 1, 1)
            gate_s[ch] = gate
            up_s[ch] = up
            act = (gate * _sigmoid(gate) * up).astype(BF16)
            act_ref[ch] = act
            h2 = h2 + _dot(act, wd_ref[rows, :], 1, 0)
        gp = gp_ref[...]
        hn3 = _rms_fwd(h2, gp).astype(BF16)
        hn3_ref[...] = hn3
        gate2 = _sigmoid(_dot(hn3, wl_ref[...], 1, 0))
        p_tile = p_ref[...].astype(BF16)
        pp = _dot(p_tile, wp_ref[...], 1, 0)
        err = h2 + gate2 * pp - tgt_ref[...]
        loss_acc[...] += jnp.sum(err * err, axis=0, keepdims=True)
        dy = err * (1.0 / D_MODEL)
        _accumulate_tn(dwp_acc, p_tile, (dy * gate2).astype(BF16), i == 0)
        dgl = (dy * pp * gate2 * (1.0 - gate2)).astype(BF16)
        dgl_ref[...] = dgl
        dx3, dg3 = _rms_bwd(h2, gp, _dot(dgl, wl_ref[...], 1, 1))
        dh2 = dy + dx3
        dgp_ref[...] += dg3
        dh2b = dh2.astype(BF16)
        dh2b_ref[...] = dh2b
        dhn = jnp.zeros((t, D_MODEL), F32)
        for ch in range(N_FF_CHUNKS):
            rows = pl.ds(ch * FF_CHUNK, FF_CHUNK)
            dact = _dot(dh2b, wd_ref[rows, :], 1, 1)
            gate_v = gate_s[ch]
            up_v = up_s[ch]
            sg = _sigmoid(gate_v)
            dup = (dact * (gate_v * sg)).astype(BF16)
            dgate = (dact * up_v * (sg * (1.0 + gate_v * (1.0 - sg)))).astype(BF16)
            dup_ref[ch] = dup
            dgate_ref[ch] = dgate
            dhn = dhn + _dot(dgate, wg_ref[rows, :], 1, 0) + _dot(dup, wu_ref[rows, :], 1, 0)
        dx, dg = _rms_bwd(h1v, gf_ref[...], dhn)
        dh1_ref[...] = dh2 + dx
        dgf_ref[...] += dg

        @pl.when(i == n_tiles - 1)
        def _():
            total = jnp.sum(loss_acc[...], axis=-1, keepdims=True) * (0.5 / D_MODEL)
            loss_ref[...] = jnp.broadcast_to(total, loss_ref.shape)
            dwp_ref[...] = dwp_acc[...].astype(BF16)

    row = lambda w: pl.BlockSpec((t, w), lambda i: (i, 0))
    chunked = pl.BlockSpec((N_FF_CHUNKS, t, FF_CHUNK), lambda i: (0, i, 0))
    vec = _full((1, D_MODEL))
    act_shape = jax.ShapeDtypeStruct((N_FF_CHUNKS, s_len, FF_CHUNK), BF16)
    tok = lambda dtype: jax.ShapeDtypeStruct((s_len, D_MODEL), dtype)
    return pl.pallas_call(
        body, name="ffn_ple", grid=(n_tiles,),
        in_specs=[row(D_MODEL), row(D_MODEL), row(PLE_DIM), row(D_MODEL)] + W_SPECS + [vec, vec],
        out_specs=[_full((1, 128)), chunked, chunked, chunked] + [row(D_MODEL)] * 3 + [_full((PLE_DIM, D_MODEL)), row(D_MODEL),
                                                                                       vec, vec],
        out_shape=[jax.ShapeDtypeStruct((1, 128), F32), act_shape, act_shape, act_shape, tok(BF16), tok(BF16), tok(BF16),
                   jax.ShapeDtypeStruct((PLE_DIM, D_MODEL), BF16), tok(F32), jax.ShapeDtypeStruct((1, D_MODEL), F32),
                   jax.ShapeDtypeStruct((1, D_MODEL), F32)],
        scratch_shapes=[pltpu.VMEM((D_FF, D_MODEL), BF16)] * 3
        + [pltpu.VMEM((D_MODEL, D_MODEL), BF16), pltpu.VMEM((PLE_DIM, D_MODEL), BF16), pltpu.VMEM((PLE_DIM, D_MODEL), BF16),
           pltpu.VMEM((N_FF_CHUNKS, t, FF_CHUNK), F32), pltpu.VMEM((N_FF_CHUNKS, t, FF_CHUNK), F32), pltpu.VMEM((1, D_MODEL), F32),
           pltpu.VMEM((PLE_DIM, D_MODEL), F32), pltpu.SemaphoreType.DMA((N_CHIPS,))],
        compiler_params=_params(VMEM_LIMIT_BIG),
    )(hn2, h1, p2, tgt, *wts, g_ffn, g_ple)


def _accumulate_tn(acc_ref, a, b, first):
    @pl.when(first)
    def _():
        acc_ref[...] = _dot(a, b, 0, 0)

    @pl.when(jnp.logical_not(first))
    def _():
        acc_ref[...] += _dot(a, b, 0, 0)


def _flush_chunks(acc_ref, stage_ref, slab_ref, name, sems):
    stage_ref[...] = acc_ref[...].astype(BF16)
    off, rows = SLAB[name]
    copies = [pltpu.make_async_copy(stage_ref.at[pl.ds(j * rows, rows), :], slab_ref.at[j, pl.ds(off, rows), :], sems.at[j])
              for j in range(N_CHIPS)]
    for cp in copies:
        cp.start()
    for cp in copies:
        cp.wait()


def _mix_out_bwd(dh1, wts, pooled, wpool, pool_scale, mix, after):
    s_len = dh1.shape[0]
    t = 512
    n = t + 16
    n_tiles = s_len // t
    early_rows = GATHER_PARTS[0][1]

    def body(dh1_ref, sl_ref, lo_ref, me_ref, pooled_ref, wp_ref, sc_ref, mix_ref, after_ref, dost_ref, du_ref, dwp_ref,
             dsc_ref, slab_ref, w_ref, ext_ref, st_ref, acc_ref, stage_ref, sems):
        del after_ref
        i = pl.program_id(0)

        @pl.when(i == 0)
        def _():
            _load_rows((sl_ref, lo_ref, me_ref), "out", w_ref, sems)
            ext_ref[...] = jnp.zeros_like(ext_ref)
            st_ref[...] = jnp.zeros_like(st_ref)
            dsc_ref[...] = jnp.zeros_like(dsc_ref)
            dwp_ref[...] = jnp.zeros_like(dwp_ref)

        dh1b = dh1_ref[...].astype(BF16)
        _accumulate_tn(acc_ref, mix_ref[...], dh1b, i == 0)

        @pl.when(i == n_tiles - 1)
        def _():
            _flush_chunks(acc_ref, stage_ref, slab_ref, "out", sems)

        dmix = _dot(dh1b, w_ref[...], 1, 1)
        lo = lax.broadcasted_iota(jnp.int32, (t, 128), 1) < 64
        for p in range(4):
            even, odd = _to_stacked(dmix[:, 128 * p:128 * p + 128], p // 2, lo)
            dost_ref[2 * p] = even.astype(BF16)
            dost_ref[2 * p + 1] = odd.astype(BF16)
        pooled_v = pooled_ref[...]
        counts = _pool_counts(n_tiles - 1 - i, t)
        for g in range(4):
            cols = slice(128 * g, 128 * g + 128)
            dm = dmix[:, ATTN_WIDTH + 128 * g:ATTN_WIDTH + 128 * g + 128]
            ypre = _dot(pooled_v[:, cols], wp_ref[g], 1, 0)
            dsc_ref[:, cols] += jnp.sum(ypre * dm, axis=0, keepdims=True)
            dyp = (dm * sc_ref[:, cols]).astype(BF16)
            dwp_ref[g] += _dot(pooled_v[:, cols], dyp, 0, 0)
            dpooled = _dot(dyp, wp_ref[g], 1, 1)
            du_ref[:, cols] = -dpooled
            ext_ref[pl.ds(0, t), cols] = dpooled / counts[:, cols]
        st_ref[pl.ds(0, n), :] = ext_ref[pl.ds(0, n), :] + ext_ref[pl.ds(1, n), :]
        st_ref[pl.ds(0, n), 128:] = st_ref[pl.ds(0, n), 128:] + st_ref[pl.ds(2, n), 128:]
        st_ref[pl.ds(0, n), 256:] = st_ref[pl.ds(0, n), 256:] + st_ref[pl.ds(4, n), 256:]
        st_ref[pl.ds(0, n), 384:] = st_ref[pl.ds(0, n), 384:] + st_ref[pl.ds(8, n), 384:]
        ext_ref[pl.ds(t, POOL_HALO), :] = ext_ref[pl.ds(0, POOL_HALO), :]
        du_ref[...] += st_ref[pl.ds(0, t), :]

    rev = lambda w: pl.BlockSpec((t, w), lambda i: (n_tiles - 1 - i, 0))
    return pl.pallas_call(
        body, name="mix_out_bwd", grid=(n_tiles,),
        in_specs=[rev(D_MODEL)] + W_SPECS + [rev(POOL_WIDTH), _full((4, 128, 128)), _full((1, POOL_WIDTH)), rev(D_MODEL), ANY],
        out_specs=[pl.BlockSpec((N_Q_HEADS, t, 128), lambda i: (0, n_tiles - 1 - i, 0)), rev(POOL_WIDTH),
                   _full((4, 128, 128)), _full((1, POOL_WIDTH)), ANY],
        out_shape=[jax.ShapeDtypeStruct((N_Q_HEADS, s_len, 128), BF16), jax.ShapeDtypeStruct((s_len, POOL_WIDTH), F32),
                   jax.ShapeDtypeStruct((4, 128, 128), F32), jax.ShapeDtypeStruct((1, POOL_WIDTH), F32),
                   jax.ShapeDtypeStruct((N_CHIPS, early_rows, D_MODEL), BF16)],
        scratch_shapes=[pltpu.VMEM((D_MODEL, D_MODEL), BF16), pltpu.VMEM((t + POOL_HALO, POOL_WIDTH), F32),
                        pltpu.VMEM((t + POOL_HALO, POOL_WIDTH), F32), pltpu.VMEM((D_MODEL, D_MODEL), F32),
                        pltpu.VMEM((D_MODEL, D_MODEL), BF16), pltpu.SemaphoreType.DMA((N_CHIPS,))],
        compiler_params=_params(),
    )(dh1, *wts, pooled, wpool, pool_scale, mix, after)


def _attn_bwd(qst, kn, vb, dost, bias_st, sinks, after):
    s_len = kn.shape[0]

    def body(q_ref, kp_ref, kc_ref, vp_ref, vc_ref, do_ref, bias_ref, sink_ref, after_ref, dq_ref, dk_ref, dv_ref, dbias_ref,
             dsink_ref, s_ref, dp_ref, p_ref, dl_ref):
        del after_ref
        i = pl.program_id(0)

        @pl.when(i == 0)
        def _():
            dk_ref[...] = jnp.zeros_like(dk_ref)
            dv_ref[...] = jnp.zeros_like(dv_ref)
            dbias_ref[...] = jnp.zeros_like(dbias_ref)
            dsink_ref[...] = jnp.zeros_like(dsink_ref)

        for b, (rows, k2, v2, bias) in enumerate(_step_blocks(i, kp_ref, kc_ref, vp_ref, vc_ref, bias_ref)):
            s_b, dp_b, p_b, dl_b = s_ref.at[b], dp_ref.at[b], p_ref.at[b], dl_ref.at[b]
            q = q_ref[:, rows, :].reshape(N_Q_HEADS * BLOCK, 128)
            do = do_ref[:, rows, :].reshape(N_Q_HEADS * BLOCK, 128)
            s_b[...] = _dot(q, k2, 1, 1)
            dp_b[...] = _dot(do, v2, 1, 1)

            def head(h, carry):
                head_rows, probs, p_sink = _head_softmax(s_b, bias, sink_ref, h)
                dp = dp_b[head_rows, :]
                dsum = jnp.sum(probs * dp, axis=-1, keepdims=True)
                dlog = probs * (dp - dsum)
                dsink_ref[head_rows, :] -= p_sink * dsum
                dbias_ref[head_rows, :] += dlog
                p_b[head_rows, :] = probs.astype(BF16)
                dl_b[head_rows, :] = (dlog * (HEAD_DIM ** -0.5)).astype(BF16)
                return carry

            lax.fori_loop(0, N_Q_HEADS, head, 0, unroll=True)
            dlog_s = dl_b[...]
            dq_ref[:, rows, :] = jnp.where(_head_lane_mask(), _dot(dlog_s, k2, 1, 0), 0.0).reshape(N_Q_HEADS, BLOCK, 128)
            dk2 = _dot(dlog_s, q, 0, 0)
            dv2 = _dot(p_b[...], do, 0, 0)
            block = ATTN_STEP_BLOCKS * i + b
            prev_rows = pl.ds(pl.multiple_of(jnp.maximum(block - 1, 0) * BLOCK, BLOCK), BLOCK)
            cur_rows = pl.ds(pl.multiple_of(block * BLOCK, BLOCK), BLOCK)
            dk_ref[prev_rows, :] += dk2[:BLOCK]
            dk_ref[cur_rows, :] += dk2[BLOCK:]
            dv_ref[prev_rows, :] += dv2[:BLOCK]
            dv_ref[cur_rows, :] += dv2[BLOCK:]

    stacked, kv, consts = _attn_specs()
    per_step = (ATTN_STEP_BLOCKS,) + BAND
    return pl.pallas_call(
        body, name="attn_bwd", grid=(s_len // (ATTN_STEP_BLOCKS * BLOCK),),
        in_specs=[stacked] + kv + kv + [stacked] + consts + [ANY],
        out_specs=[stacked, _full((s_len, 128)), _full((s_len, 128)), _full(BAND), _full((N_Q_HEADS * BLOCK, 1))],
        out_shape=[jax.ShapeDtypeStruct((N_Q_HEADS, s_len, 128), F32), jax.ShapeDtypeStruct((s_len, 128), F32),
                   jax.ShapeDtypeStruct((s_len, 128), F32), jax.ShapeDtypeStruct(BAND, F32),
                   jax.ShapeDtypeStruct((N_Q_HEADS * BLOCK, 1), F32)],
        scratch_shapes=[pltpu.VMEM(per_step, F32), pltpu.VMEM(per_step, F32), pltpu.VMEM(per_step, BF16),
                        pltpu.VMEM(per_step, BF16)],
        compiler_params=_params(),
    )(qst, kn, kn, vb, vb, dost, bias_st, sinks, after)


def _small_pack(dg_attn, dg_ffn, dg_ple, dscale, dgq, dgk, dbias, dsink_rows, bucket, loss_v, dwpool):
    def body(ga_ref, gf_ref, gp_ref, sc_ref, gq_ref, gk_ref, db_ref, ds_ref, bucket_ref, loss_ref, wp_ref, out_ref):
        out_ref[pl.ds(0, SMALL["w_pool"]), :] = jnp.zeros((SMALL["w_pool"], 128), F32)
        for name, ref, n in (("g_attn", ga_ref, 8), ("g_ffn", gf_ref, 8), ("g_ple", gp_ref, 8), ("pool_scale", sc_ref, 4)):
            for k in range(n):
                out_ref[pl.ds(SMALL[name] + k, 1), :] = ref[:, 128 * k:128 * k + 128]
        for name, ref in (("g_q", gq_ref), ("g_k", gk_ref)):
            both = ref[...]
            out_ref[pl.ds(SMALL[name], 1), :] = both + pltpu.roll(both, 64, axis=1)
        out_ref[pl.ds(SMALL["loss"], 1), :] = loss_ref[...]
        bk = bucket_ref[...]
        rows = lax.broadcasted_iota(jnp.int32, (N_Q_HEADS, 128), 0)
        lanes = lax.broadcasted_iota(jnp.int32, (N_Q_HEADS, 128), 1)
        lane1 = lax.broadcasted_iota(jnp.int32, (1, 128), 1)
        rb = jnp.zeros((N_Q_HEADS, 128), F32)
        sk = jnp.zeros((1, 128), F32)
        for h in range(N_Q_HEADS):
            band = db_ref[pl.ds(h * BLOCK, BLOCK), :]
            for b in range(N_BUCKETS):
                rb = jnp.where((rows == h) & (lanes == b), jnp.sum(jnp.where(bk == b, band, 0.0)), rb)
            sk = jnp.where(lane1 == h, jnp.sum(ds_ref[pl.ds(h * BLOCK, BLOCK), :]), sk)
        out_ref[pl.ds(SMALL["rel_bias"], N_Q_HEADS), :] = rb
        out_ref[pl.ds(SMALL["sinks"], 1), :] = sk
        out_ref[pl.ds(SMALL["w_pool"], 512), :] = wp_ref[...].reshape(512, 128)

    return pl.pallas_call(
        body, name="small_pack", in_specs=[VMEM_WHOLE] * 11, out_specs=VMEM_WHOLE,
        out_shape=jax.ShapeDtypeStruct((SMALL_ROWS, 128), F32),
    )(dg_attn, dg_ffn, dg_ple, dscale, dgq, dgk, dbias, dsink_rows, bucket, loss_v, dwpool)


def _attn_in_bwd(dqst, zqk, dk, dv, du, x2, dh1, hn1, slab, wts, g_attn, gq, gk):
    s_len = x2.shape[0]
    t = 512
    n_tiles = s_len // t

    def body(dq_ref, zqk_ref, dk_ref, dv_ref, du_ref, x_ref, dh1_ref, hn_ref, slab_in_ref, sl_ref, lo_ref, me_ref, g_ref,
             gq_ref, gk_ref, dx_ref, dg_ref, dgq_ref, dgk_ref, slab_ref, w_ref, dz_ref, acc_ref, stage_ref, sems):
        del slab_in_ref
        i = pl.program_id(0)

        @pl.when(i == 0)
        def _():
            _load_rows((sl_ref, lo_ref, me_ref), "inT", w_ref, sems)
            dg_ref[...] = jnp.zeros_like(dg_ref)
            dgq_ref[...] = jnp.zeros_like(dgq_ref)
            dgk_ref[...] = jnp.zeros_like(dgk_ref)

        lo = lax.broadcasted_iota(jnp.int32, (t, 128), 1) < 64
        for p in range(4):
            dqn = _from_stacked(dq_ref[2 * p], dq_ref[2 * p + 1], p // 2, lo)
            dq_raw, dgq = _pair_norm_bwd(zqk_ref[:, 128 * p:128 * p + 128], gq_ref[...], dqn)
            dz_ref[:, 128 * p:128 * p + 128] = dq_raw.astype(BF16)
            dgq_ref[...] += dgq
        dk_raw, dgk = _pair_norm_bwd(zqk_ref[:, 512:640], gk_ref[...], dk_ref[...])
        dgk_ref[...] += dgk
        dz_ref[:, 512:640] = dk_raw.astype(BF16)
        dz_ref[:, 640:768] = dv_ref[...].astype(BF16)
        dz_ref[:, 768:] = du_ref[...].astype(BF16)
        dz = dz_ref[...]
        _accumulate_tn(acc_ref, dz, hn_ref[...], i == 0)
        dx, dg = _rms_bwd(x_ref[...], g_ref[...], _dot(dz, w_ref[...], 1, 0))
        dx_ref[...] = dh1_ref[...] + dx
        dg_ref[...] += dg

        @pl.when(i == n_tiles - 1)
        def _():
            _flush_chunks(acc_ref, stage_ref, slab_ref, "inT", sems)

    row = lambda w: pl.BlockSpec((t, w), lambda i: (i, 0))
    return pl.pallas_call(
        body, name="attn_in_bwd", grid=(n_tiles,),
        in_specs=[pl.BlockSpec((N_Q_HEADS, t, 128), lambda i: (0, i, 0)), row(640), row(128), row(128), row(POOL_WIDTH),
                  row(D_MODEL), row(D_MODEL), row(D_MODEL), ANY] + W_SPECS + [_full((1, D_MODEL)), _full((1, 128)),
                                                                              _full((1, 128))],
        out_specs=[row(D_MODEL), _full((1, D_MODEL)), _full((1, 128)), _full((1, 128)), ANY],
        out_shape=[jax.ShapeDtypeStruct((s_len, D_MODEL), F32), jax.ShapeDtypeStruct((1, D_MODEL), F32),
                   jax.ShapeDtypeStruct((1, 128), F32), jax.ShapeDtypeStruct((1, 128), F32),
                   jax.ShapeDtypeStruct(slab.shape, BF16)],
        input_output_aliases={8: 4},
        scratch_shapes=[pltpu.VMEM((IN_WIDTH, D_MODEL), BF16), pltpu.VMEM((t, IN_WIDTH), BF16),
                        pltpu.VMEM((IN_WIDTH, D_MODEL), F32), pltpu.VMEM((IN_WIDTH, D_MODEL), BF16),
                        pltpu.SemaphoreType.DMA((N_CHIPS,))],
        compiler_params=_params(),
    )(dqst, zqk, dk, dv, du, x2, dh1, hn1, slab, *wts, g_attn, gq, gk)


def _dw(lefts, b, name, slab, slab_rows, row_offs):
    a0, n_a = lefts[0], len(lefts)
    assert b.shape[1] == D_MODEL
    if a0.ndim == 3:
        n_chunks, s_len, tm = a0.shape
        m = n_chunks * tm
    else:
        s_len, tm = a0.shape
        m = tm
    tk = 2048 if n_a * tm <= 1408 else 1024
    if a0.ndim == 3:
        a_spec = pl.BlockSpec((None, tk, tm), lambda i, k: (i, k, 0))
    else:
        a_spec = pl.BlockSpec((tk, tm), lambda i, k: (k, i))
    n_steps, n_tiles = s_len // tk, m // tm
    chunk = m // N_CHIPS
    per_tile = tm // chunk

    def body(*refs):
        a_refs, b_ref = refs[:n_a], refs[n_a]
        o_ref, acc_ref, stage_ref, sems = refs[-4:]
        i, k = pl.program_id(0), pl.program_id(1)
        b_tile = b_ref[...].astype(BF16)
        for w, a_ref in enumerate(a_refs):
            _accumulate_tn(acc_ref.at[w], a_ref[...].astype(BF16), b_tile, k == 0)

        def out_copies(tile, slot):
            return [pltpu.make_async_copy(stage_ref.at[slot, w, pl.ds(jj * chunk, chunk), :],
                                          o_ref.at[tile * per_tile + jj, pl.ds(row_offs[w], chunk), :], sems.at[slot, w, jj])
                    for w in range(n_a) for jj in range(per_tile)]

        @pl.when(k == n_steps - 1)
        def _():
            slot = i % 2

            @pl.when(i >= 2)
            def _():
                for cp in out_copies(i - 2, slot):
                    cp.wait()

            stage_ref[slot] = acc_ref[...].astype(BF16)
            for cp in out_copies(i, slot):
                cp.start()

            @pl.when(i == n_tiles - 1)
            def _():
                for cp in out_copies(i, slot):
                    cp.wait()
                if n_tiles > 1:
                    for cp in out_copies(i - 1, 1 - slot):
                        cp.wait()

    in_specs = [a_spec] * n_a + [pl.BlockSpec((tk, D_MODEL), lambda i, k: (k, 0))]
    operands, aliases = [*lefts, b], {}
    if slab is not None:
        in_specs.append(ANY)
        operands.append(slab)
        aliases = {n_a + 1: 0}
    return pl.pallas_call(
        body, name=name, grid=(n_tiles, n_steps), in_specs=in_specs, out_specs=ANY,
        out_shape=jax.ShapeDtypeStruct((N_CHIPS, slab_rows, D_MODEL), BF16), input_output_aliases=aliases,
        scratch_shapes=[pltpu.VMEM((n_a, tm, D_MODEL), F32), pltpu.VMEM((2, n_a, tm, D_MODEL), BF16),
                        pltpu.SemaphoreType.DMA((2, n_a, per_tile))],
        compiler_params=_params(VMEM_LIMIT_BIG, n_axes=2),
    )(*operands)


def _position():
    x, y, c = lax.axis_index("x"), lax.axis_index("y"), lax.axis_index("c")
    other_chips = [(1 - x, y), (x, 1 - y), (1 - x, 1 - y)]
    return x, y, c, other_chips


def _ag_weights(local_slab, row0, n_rows, name, collective_id):
    half = n_rows // 2
    quarter = half // 2
    assert quarter % 16 == 0

    def body(l_ref, g_ref, send, recv):
        x, y, c, chips = _position()
        me, (via_x, via_y, diagonal) = 2 * x + y, [2 * chip[0] + chip[1] for chip in chips]
        here, sibling, x_nbr, y_nbr = (x, y, c), (x, y, 1 - c), (1 - x, y, c), (x, 1 - y, c)
        peers = [sibling, x_nbr, y_nbr]
        barrier = pltpu.get_barrier_semaphore()
        for peer in peers:
            pl.semaphore_signal(barrier, inc=1, device_id=peer, device_id_type=MESH)
        pl.semaphore_wait(barrier, len(peers))

        def rows(core, part):
            start, size = (core * half, half) if part is None else (core * half + part * quarter, quarter)
            return pl.ds(pl.multiple_of(start, 16), size)

        def copy(k, chip_idx, where, to, src=None):
            dst = g_ref.at[chip_idx, where, :]
            return pltpu.make_async_remote_copy(src_ref=dst if src is None else src, dst_ref=dst, send_sem=send.at[k],
                                                recv_sem=recv.at[k], device_id=to, device_id_type=MESH)

        own_rows = l_ref.at[pl.ds(pl.multiple_of(row0 + c * half, 16), half), :]
        started = [copy(0, me, rows(c, None), x_nbr, src=own_rows), copy(1, me, rows(c, None), y_nbr, src=own_rows)]
        for cp in started:
            cp.start()
        after_arrival = [
            (copy(0, via_x, rows(c, None), here), [copy(4, via_x, rows(c, None), sibling), copy(3, via_x, rows(c, 1), y_nbr)]),
            (copy(1, via_y, rows(c, None), here), [copy(5, via_y, rows(c, None), sibling), copy(2, via_y, rows(c, 0), x_nbr)]),
            (copy(2, diagonal, rows(c, 0), here), [copy(6, diagonal, rows(c, 0), sibling)]),
            (copy(3, diagonal, rows(c, 1), here), [copy(7, diagonal, rows(c, 1), sibling)]),
        ]
        for arrival, onward in after_arrival:
            arrival.wait_recv()
            for cp in onward:
                cp.start()
            started += onward
        for cp in (copy(4, via_x, rows(1 - c, None), here), copy(5, via_y, rows(1 - c, None), here),
                   copy(6, diagonal, rows(1 - c, 0), here), copy(7, diagonal, rows(1 - c, 1), here)):
            cp.wait_recv()
        for cp in started:
            cp.wait_send()

    return pl.kernel(
        body, out_type=jax.ShapeDtypeStruct((N_CHIPS, n_rows, D_MODEL), BF16),
        mesh=plsc.ScalarSubcoreMesh(axis_name="sequencer", num_cores=1), name=name,
        scratch_types=[pltpu.SemaphoreType.DMA((8,)), pltpu.SemaphoreType.DMA((8,))],
        compiler_params=pltpu.CompilerParams(collective_id=collective_id),
    )(local_slab)


def _comm_call(body, peers_of, out_shape, n_sems, operand, name, collective_id):
    sems = [pltpu.SemaphoreType.DMA((n_sems,)), pltpu.SemaphoreType.DMA((n_sems,))]

    def with_handshake(in_ref, out_ref, send, recv):
        x, y, c, _ = _position()
        peers = peers_of(x, y, c)
        barrier = pltpu.get_barrier_semaphore()
        for peer in peers:
            pl.semaphore_signal(barrier, inc=1, device_id=peer, device_id_type=MESH)
        pl.semaphore_wait(barrier, len(peers))
        body(in_ref, out_ref, send, recv)

    return pl.kernel(with_handshake, out_type=out_shape, mesh=plsc.ScalarSubcoreMesh(axis_name="sequencer", num_cores=1),
                     name=name, scratch_types=sems, compiler_params=pltpu.CompilerParams(collective_id=collective_id))(operand)


def _rs_swap_halves(partial, name, collective_id):
    half = partial.shape[1] // 2

    def body(p_ref, r_ref, send, recv):
        x, y, c, _ = _position()
        theirs = pl.ds(pl.multiple_of((1 - c) * half, 16), half)
        cp = pltpu.make_async_remote_copy(src_ref=p_ref.at[:, theirs, :], dst_ref=r_ref, send_sem=send.at[0],
                                          recv_sem=recv.at[0], device_id=(x, y, 1 - c), device_id_type=MESH)
        cp.start()
        cp.wait()

    return _comm_call(body, lambda x, y, c: [(x, y, 1 - c)], jax.ShapeDtypeStruct((N_CHIPS, half, D_MODEL), BF16), 1,
                      partial, name, collective_id)


def _rs_add_halves(partial, other, core, name, after):
    half = other.shape[1]
    t = half // 2
    steps = half // t

    def body(core_ref, a_ref, b_ref, after_ref, o_ref):
        del after_ref
        o_ref[...] = (a_ref[...].astype(F32) + b_ref[...].astype(F32)).astype(BF16)

    return pl.pallas_call(
        body, name=name,
        grid_spec=pltpu.PrefetchScalarGridSpec(
            num_scalar_prefetch=1, grid=(N_CHIPS, steps),
            in_specs=[pl.BlockSpec((1, t, D_MODEL), lambda j, i, core_ref: (j, core_ref[0] * steps + i, 0)),
                      pl.BlockSpec((1, t, D_MODEL), lambda j, i, core_ref: (j, i, 0)), ANY],
            out_specs=pl.BlockSpec((1, t, D_MODEL), lambda j, i, core_ref: (j, i, 0))),
        out_shape=jax.ShapeDtypeStruct((N_CHIPS, half, D_MODEL), BF16),
        compiler_params=_params(n_axes=2),
    )(core, partial, other, after)


def _rs_exchange_chips(pre, name, collective_id):
    def body(s_ref, r_ref, send, recv):
        x, y, c, chips = _position()

        def copy(k, chunk, to):
            return pltpu.make_async_remote_copy(src_ref=s_ref.at[chunk], dst_ref=r_ref.at[k], send_sem=send.at[k],
                                                recv_sem=recv.at[k], device_id=to, device_id_type=MESH)

        sends = [copy(k, 2 * chip[0] + chip[1], (*chip, c)) for k, chip in enumerate(chips)]
        for cp in sends:
            cp.start()
        for cp in sends:
            cp.wait()

    return _comm_call(body, lambda x, y, c: [(1 - x, y, c), (x, 1 - y, c), (1 - x, 1 - y, c)],
                      jax.ShapeDtypeStruct((3, pre.shape[1], D_MODEL), BF16), 3, pre, name, collective_id)


def _gather_small(s_ref, t_ref, send, recv):
    x, y, c, chips = _position()
    sibling = (x, y, 1 - c)

    def slot(px, py, pc):
        return t_ref.at[4 * px + 2 * py + pc]

    def copy(k, block, to, src=None):
        return pltpu.make_async_remote_copy(src_ref=slot(*block) if src is None else src, dst_ref=slot(*block),
                                            send_sem=send.at[k], recv_sem=recv.at[k], device_id=to, device_id_type=MESH)

    own = pltpu.make_async_copy(s_ref, slot(x, y, c), send.at[7])
    first = [copy(0, (x, y, c), sibling, src=s_ref)]
    first += [copy(1 + k, (x, y, c), (*chip, c), src=s_ref) for k, chip in enumerate(chips)]

    def start():
        own.start()
        for cp in first:
            cp.start()

    def finish():
        passed = []
        for k, chip in enumerate(chips):
            copy(1 + k, (*chip, c), (x, y, c)).wait_recv()
            fwd = copy(4 + k, (*chip, c), sibling)
            fwd.start()
            passed.append(fwd)
        copy(0, sibling, (x, y, c)).wait_recv()
        for k, chip in enumerate(chips):
            copy(4 + k, (*chip, 1 - c), (x, y, c)).wait_recv()
        for cp in first + passed:
            cp.wait_send()
        own.wait()

    return start, finish


def _rs_sum_chips(pre, received, place, name, after, small=None):
    half = pre.shape[1]
    steps = 4 if half > 512 else 2
    t = half // steps
    assert t % 16 == 0 and t * steps == half

    def body(place_ref, own_ref, r_ref, after_ref, *rest):
        del place_ref, after_ref
        if small is None:
            o_ref, stage, kept_sems, send, recv = rest
        else:
            small_ref, o_ref, t_ref, stage, kept_sems, send, recv, t_send, t_recv = rest
            start_tables, finish_tables = _gather_small(small_ref, t_ref, t_send, t_recv)
        i = pl.program_id(0)
        x, y, c, _ = _position()

        def rows(core, step):
            return o_ref.at[pl.ds(pl.multiple_of((core * steps + step) * t, 8), t), :]

        def kept(step):
            return pltpu.make_async_copy(stage.at[step], rows(c, step), kept_sems.at[step])

        def sent(core, step):
            return pltpu.make_async_remote_copy(src_ref=stage.at[step], dst_ref=rows(core, step), send_sem=send.at[step],
                                                recv_sem=recv.at[step], device_id=(x, y, 1 - core), device_id_type=MESH)

        if small is not None:
            pl.when(i == 0)(start_tables)
        acc = own_ref[0].astype(F32)
        for k in range(3):
            acc = acc + r_ref[k].astype(F32)
        stage[i] = acc
        kept(i).start()
        sent(c, i).start()

        @pl.when(i == steps - 1)
        def _():
            if small is not None:
                finish_tables()
            for step in range(steps):
                kept(step).wait()
                sent(c, step).wait_send()
                sent(1 - c, step).wait_recv()

    in_specs = [pl.BlockSpec((1, t, D_MODEL), lambda i, place_ref: (place_ref[0], i, 0)),
                pl.BlockSpec((3, t, D_MODEL), lambda i, place_ref: (0, i, 0)), ANY]
    out_shape = [jax.ShapeDtypeStruct((2 * half, D_MODEL), F32)]
    scratch = [pltpu.VMEM((steps, t, D_MODEL), F32)] + [pltpu.SemaphoreType.DMA((steps,))] * 3
    operands = [place, pre, received, after]
    if small is not None:
        in_specs.append(VMEM_WHOLE)
        out_shape.append(jax.ShapeDtypeStruct((N_DEV, *small.shape), F32))
        scratch += [pltpu.SemaphoreType.DMA((8,))] * 2
        operands.append(small)
    res = pl.pallas_call(
        body, name=name,
        grid_spec=pltpu.PrefetchScalarGridSpec(num_scalar_prefetch=1, grid=(steps,), in_specs=in_specs,
                                               out_specs=[ANY] * len(out_shape), scratch_shapes=scratch),
        out_shape=out_shape, compiler_params=_params(),
    )(*operands)
    return res[0] if small is None else res


def _adam_update(w, g, m, v):
    m_new = ADAM_B1 * m + (1.0 - ADAM_B1) * g
    v_new = ADAM_B2 * v + (1.0 - ADAM_B2) * (g * g)
    m_hat = m_new / (1.0 - ADAM_B1 ** ADAM_STEP)
    v_hat = v_new / (1.0 - ADAM_B2 ** ADAM_STEP)
    return -ADAM_LR * (m_hat / (jnp.sqrt(v_hat) + ADAM_EPS) + ADAM_WD * w), m_new, v_new


def _adamw(w, g_rows, row_off, m, v, name):
    rows, cols = w.shape
    t = rows if rows <= 320 else (rows // 2 if rows % 256 else 256)

    def body(w_ref, g_ref, m_ref, v_ref, go_ref, d_ref, nm_ref, nv_ref):
        g = g_ref[...]
        go_ref[...] = g
        d_ref[...], nm_ref[...], nv_ref[...] = _adam_update(w_ref[...], g, m_ref[...], v_ref[...])

    blk = pl.BlockSpec((t, cols), lambda i: (i, 0))
    assert row_off % 8 == 0 and t % 8 == 0
    g_blk = pl.BlockSpec((pl.Element(t), pl.Element(cols)), lambda i: (pl.multiple_of(row_off + i * t, 8), 0))
    shape = jax.ShapeDtypeStruct((rows, cols), F32)
    return pl.pallas_call(
        body, name=name, grid=(rows // t,), in_specs=[blk, g_blk, blk, blk], out_specs=[blk] * 4, out_shape=[shape] * 4,
        compiler_params=_params(),
    )(w, g_rows, m, v)


SMALL_PARAMS = [("g_attn", (1, D_MODEL), 8), ("g_q", (1, HEAD_DIM), None), ("g_k", (1, HEAD_DIM), None),
                ("sinks", (1, N_Q_HEADS), None), ("rel_bias", (N_Q_HEADS, N_BUCKETS), None), ("w_pool", (512, 128), None),
                ("pool_scale", (1, POOL_WIDTH), 4), ("g_ffn", (1, D_MODEL), 8), ("g_ple", (1, D_MODEL), 8)]


def _adamw_small(tables, wmv):
    n_par = len(SMALL_PARAMS)

    def body(*refs):
        t_ref = refs[0]
        ins = refs[1:1 + 3 * n_par]
        loss_ref = refs[1 + 3 * n_par]
        outs = refs[2 + 3 * n_par:-1]
        tot_ref = refs[-1]
        total = t_ref[0]
        for d in range(1, N_DEV):
            total = total + t_ref[d]
        tot_ref[...] = total
        loss_ref[...] = tot_ref[pl.ds(SMALL["loss"], 1), 0:1]
        for i, (name, shape, split) in enumerate(SMALL_PARAMS):
            g_ref, d_ref, nm_ref, nv_ref = outs[4 * i:4 * i + 4]
            row = SMALL[name]
            if split:
                for k in range(split):
                    g_ref[:, 128 * k:128 * k + 128] = tot_ref[pl.ds(row + k, 1), :]
            else:
                g_ref[...] = tot_ref[pl.ds(row, shape[0]), 0:shape[1]]
            w_ref, m_ref, v_ref = ins[3 * i:3 * i + 3]
            d_ref[...], nm_ref[...], nv_ref[...] = _adam_update(w_ref[...], g_ref[...], m_ref[...], v_ref[...])

    shapes = [jax.ShapeDtypeStruct((1, 1), F32)]
    for _, shape, _ in SMALL_PARAMS:
        shapes += [jax.ShapeDtypeStruct(shape, F32)] * 4
    flat = [a for triple in wmv for a in triple]
    res = pl.pallas_call(
        body, name="adamw_small", in_specs=[VMEM_WHOLE] * (1 + 3 * n_par), out_specs=[VMEM_WHOLE] * len(shapes),
        out_shape=shapes, scratch_shapes=[pltpu.VMEM((SMALL_ROWS, 128), F32)],
    )(tables, *flat)
    return res[0], [res[1 + 4 * i:5 + 4 * i] for i in range(n_par)]


def _pack_ple_proj(shard):
    return shard.reshape(4, 64, 256).transpose(1, 0, 2).reshape(64, D_MODEL)


class _Reduction:
    def __init__(self, tag, place, ids=(None, None)):
        self.tag, self.place, self.ids = tag, place, ids

    def start(self, partial):
        self.partial = partial
        self.other = _rs_swap_halves(partial, "rs_swap_" + self.tag, self.ids[0])
        return partial

    def middle(self, after):
        self.pre = _rs_add_halves(self.partial, self.other, self.place[1:], "rs_add_" + self.tag, after)
        self.received = _rs_exchange_chips(self.pre, "rs_exchange_" + self.tag, self.ids[1])
        return self.pre

    def finish(self, after, small=None):
        return _rs_sum_chips(self.pre, self.received, self.place, "rs_sum_" + self.tag, after, small)


def _local_grads(x2, p2, tgt, wts, g_attn_norm, g_q, g_k, attn_sinks, rel_bias, w_pool, pool_scale, g_ffn_norm, g_ple_norm,
                 reduce_a):
    w_early, w_late = wts
    w_in = w_out = w_early
    bucket = jnp.asarray(_bucket_table())
    gq = jnp.tile(g_q, (1, 2))
    gk = jnp.tile(g_k, (1, 2))
    wpool = w_pool[0].astype(BF16)
    sinks = attn_sinks[0]
    bias_st = _bias_build(rel_bias.T, bucket)

    hn1 = _first_norm(x2, g_attn_norm)
    zqk, u, kn, vb, qst = _attn_in(hn1, gq, gk, w_in)
    ost = _attn_fwd(qst, kn, vb, bias_st, sinks)
    pooled, mix, h1, hn2 = _mix_out(u, ost, x2, w_out, wpool, pool_scale, g_ffn_norm)
    loss_v, dgate, dup, act, dh2, hn3, dgl, dw_plp, dh1, dg_ffn, dg_ple = _ffn_ple(hn2, h1, p2, tgt, w_late, g_ffn_norm,
                                                                                      g_ple_norm)

    late0, late_rows = GATHER_PARTS[1][0], SLAB_ROWS - GATHER_PARTS[1][0]
    partial_a = None
    for names, lefts, right in ((("gateT", "upT"), [dgate, dup], hn2), (("down",), [act], dh2), (("plg",), [hn3], dgl)):
        partial_a = _dw(lefts, right, "dw_" + names[0], partial_a, late_rows, [SLAB[name][0] - late0 for name in names])
    dw_plp = dw_plp.reshape(4, 64, N_CHIPS, 256).transpose(2, 1, 0, 3).reshape(N_CHIPS, 64, D_MODEL)
    partial_a = reduce_a.start(lax.dynamic_update_slice(partial_a, dw_plp, (0, SLAB["plp"][0] - late0, 0)))
    dost, du, dw_pool, dscale, partial_b = _mix_out_bwd(dh1, w_out, pooled, wpool, pool_scale, mix, partial_a)
    pre_a = reduce_a.middle(du)
    dqst, dk, dv, dbias, dsink_rows = _attn_bwd(qst, kn, vb, dost, bias_st, sinks, pre_a)
    dx, dg_attn, dgq, dgk, partial_b = _attn_in_bwd(dqst, zqk, dk, dv, du, x2, dh1, hn1, partial_b, w_in, g_attn_norm, gq, gk)

    small = _small_pack(dg_attn, dg_ffn, dg_ple, dscale, dgq, dgk, dbias, dsink_rows, bucket, loss_v, dw_pool)
    return dx, partial_b, small


def kernel(x, p, w_in, w_out, g_attn_norm, g_q, g_k, attn_sinks, rel_bias, w_pool, pool_scale, g_ffn_norm, w_gate, w_up, w_down, g_ple_norm, w_ple_gate, w_ple_proj, loss_target, m_w_in, m_w_out, m_g_attn_norm, m_g_q, m_g_k, m_attn_sinks, m_rel_bias, m_w_pool, m_pool_scale, m_g_ffn_norm, m_w_gate, m_w_up, m_w_down, m_g_ple_norm, m_w_ple_gate, m_w_ple_proj, v_w_in, v_w_out, v_g_attn_norm, v_g_q, v_g_k, v_attn_sinks, v_rel_bias, v_w_pool, v_pool_scale, v_g_ffn_norm, v_w_gate, v_w_up, v_w_down, v_g_ple_norm, v_w_ple_gate, v_w_ple_proj):
    core = lax.axis_index("c").astype(jnp.int32).reshape(1)
    me = (2 * lax.axis_index("x") + lax.axis_index("y")).astype(jnp.int32).reshape(1)

    local_parts = [jnp.concatenate(pieces, axis=0).astype(BF16) for pieces in (
        [w_in[0].T, w_out[0]], [w_gate[0].T, w_up[0].T, w_down[0], w_ple_gate[0], _pack_ple_proj(w_ple_proj[0])])]
    wts = [(_ag_weights(local, 0, local.shape[0], name, collective_id), local, me)
           for local, name, collective_id in zip(local_parts, ("ag_early", "ag_late"), (1, 2))]

    place = jnp.concatenate([me, core])
    reduce_a = _Reduction("a", place, ids=(3, 4))
    dx, partial_b, small = _local_grads(x[0], p[0, 0], loss_target[0], wts, g_attn_norm, g_q, g_k, attn_sinks, rel_bias,
                                        w_pool, pool_scale, g_ffn_norm, g_ple_norm, reduce_a)
    reduce_b = _Reduction("b", place, ids=(6, 7))
    reduce_b.start(partial_b)
    grads_a, small_all = reduce_a.finish(partial_b, small)
    reduce_b.middle(grads_a)

    late0 = GATHER_PARTS[1][0]

    def rows(name):
        return grads_a, SLAB[name][0] - late0

    plp_rows = grads_a[SLAB["plp"][0] - late0:]
    big = {
        "w_gate": (w_gate, m_w_gate, v_w_gate, rows("gateT"), True),
        "w_up": (w_up, m_w_up, v_w_up, rows("upT"), True),
        "w_down": (w_down, m_w_down, v_w_down, rows("down"), False),
        "w_ple_gate": (w_ple_gate, m_w_ple_gate, v_w_ple_gate, rows("plg"), False),
        "w_ple_proj": (w_ple_proj, m_w_ple_proj, v_w_ple_proj,
                       (plp_rows.reshape(64, 4, 256).transpose(1, 0, 2).reshape(PLE_DIM, PLE_DIM), 0), False),
        "w_out": (w_out, m_w_out, v_w_out, None, False),
        "w_in": (w_in, m_w_in, v_w_in, None, True),
    }
    small_params = {
        "g_attn_norm": (g_attn_norm, m_g_attn_norm, v_g_attn_norm), "g_q": (g_q, m_g_q, v_g_q), "g_k": (g_k, m_g_k, v_g_k),
        "attn_sinks": (attn_sinks, m_attn_sinks, v_attn_sinks), "rel_bias": (rel_bias.T, m_rel_bias.T, v_rel_bias.T),
        "w_pool": tuple(a.reshape(512, 128) for a in (w_pool, m_w_pool, v_w_pool)),
        "pool_scale": (pool_scale, m_pool_scale, v_pool_scale), "g_ffn_norm": (g_ffn_norm, m_g_ffn_norm, v_g_ffn_norm),
        "g_ple_norm": (g_ple_norm, m_g_ple_norm, v_g_ple_norm),
    }

    grads, deltas, new_ms, new_vs = {}, {}, {}, {}
    out = grads_b = None
    for name, (w, m, v, g_src, transposed) in big.items():
        if g_src is None:
            if grads_b is None:
                grads_b = reduce_b.finish(out[-1])
            g_src = (grads_b, SLAB["out" if name == "w_out" else "inT"][0])
        view = (lambda a: a.T) if transposed else (lambda a: a)
        out = _adamw(view(w[0]), *g_src, view(m[0]), view(v[0]), "adamw_" + name)
        grads[name], deltas[name], new_ms[name], new_vs[name] = (view(a)[None] for a in out)

    loss, small_out = _adamw_small(small_all, list(small_params.values()))
    for name, (g2, d, nm, nv) in zip(small_params, small_out):
        restore = {"w_pool": lambda a: a.reshape(w_pool.shape), "rel_bias": lambda a: a.T}.get(name, lambda a: a)
        grads[name], deltas[name], new_ms[name], new_vs[name] = (restore(a) for a in (g2, d, nm, nv))

    order = ["w_in", "w_out", "g_attn_norm", "g_q", "g_k", "attn_sinks", "rel_bias", "w_pool", "pool_scale", "g_ffn_norm",
             "w_gate", "w_up", "w_down", "g_ple_norm", "w_ple_gate", "w_ple_proj"]
    return (loss.reshape(()), dx[None], *[grads[n] for n in order], *[deltas[n] for n in order],
            *[new_ms[n] for n in order], *[new_vs[n] for n in order])
```

```python
import numpy as np
import jax
import jax.numpy as jnp
from jax import lax
from jax.experimental import pallas as pl
from jax.experimental.pallas import tpu as pltpu
from jax.experimental.pallas import tpu_sc as plsc

F32 = jnp.float32
BF16 = jnp.bfloat16
MESH = pl.DeviceIdType.MESH

D_MODEL = 1024
HEAD_DIM = 64
N_Q_HEADS = 8
ATTN_WIDTH = 512
POOL_WIDTH = 512
IN_WIDTH = 1280
D_FF = 2816
PLE_DIM = 256
FF_CHUNK = 1408
N_FF_CHUNKS = D_FF // FF_CHUNK
BLOCK = 128
N_BUCKETS = 32
MAX_DISTANCE = 128
EPS = 1e-6
NEG = -1e30
N_CHIPS = 4
N_DEV = 8

ADAM_LR = 0.001
ADAM_B1 = 0.9
ADAM_B2 = 0.999
ADAM_EPS = 1e-08
ADAM_WD = 0.01
ADAM_STEP = 10

SLAB = {"inT": (0, 320), "out": (320, 256), "gateT": (576, 704), "upT": (1280, 704), "down": (1984, 704),
        "plg": (2688, 256), "plp": (2944, 64)}
SLAB_ROWS = 3008
GATHER_PARTS = ((0, 576), (576, SLAB_ROWS))
POOL_HALO = 24

SMALL = {"g_attn": 0, "g_ffn": 8, "g_ple": 16, "pool_scale": 24, "g_q": 28, "g_k": 29, "sinks": 30, "loss": 31,
         "rel_bias": 32, "w_pool": 64}
SMALL_ROWS = 576

VMEM_LIMIT_BIG = 60 * 1024 * 1024
VMEM_LIMIT = 48 * 1024 * 1024


def _params(vmem=VMEM_LIMIT, n_axes=1):
    return pltpu.CompilerParams(dimension_semantics=("arbitrary",) * n_axes, vmem_limit_bytes=vmem)


def _dot(a, b, ca, cb):
    return lax.dot_general(a, b, (((ca,), (cb,)), ((), ())), preferred_element_type=F32)


def _full(shape):
    return pl.BlockSpec(shape, lambda i: (0,) * len(shape))


ANY = pl.BlockSpec(memory_space=pl.ANY)
VMEM_WHOLE = pl.BlockSpec(memory_space=pltpu.VMEM)


W_SPECS = [ANY, ANY, pl.BlockSpec(memory_space=pltpu.SMEM)]


def _load_rows(w_refs, name, dst_ref, sems):
    slab_ref, local_ref, me_ref = w_refs
    off, rows = SLAB[name]
    slab_off = off - max(start for start, _ in GATHER_PARTS if start <= off)
    me = me_ref[0]
    for phase in ("start", "wait"):
        for j in range(N_CHIPS):
            dst = dst_ref.at[pl.ds(j * rows, rows), :]
            theirs = pltpu.make_async_copy(slab_ref.at[j, pl.ds(slab_off, rows), :], dst, sems.at[j])
            own = pltpu.make_async_copy(local_ref.at[pl.ds(slab_off, rows), :], dst, sems.at[j])

            @pl.when(me == j)
            def _():
                getattr(own, phase)()

            @pl.when(me != j)
            def _():
                getattr(theirs, phase)()


def _rms_fwd(x, g):
    r = lax.rsqrt(jnp.mean(x * x, axis=-1, keepdims=True) + EPS)
    return x * r * g


def _rms_bwd(x, g, dy):
    r = lax.rsqrt(jnp.mean(x * x, axis=-1, keepdims=True) + EPS)
    xn = x * r
    dyg = dy * g
    dx = r * (dyg - xn * jnp.mean(dyg * xn, axis=-1, keepdims=True))
    return dx, jnp.sum(dy * xn, axis=0, keepdims=True)


def _half_sum(v, lo):
    s_lo = jnp.sum(jnp.where(lo, v, 0.0), axis=-1, keepdims=True)
    s_hi = jnp.sum(jnp.where(lo, 0.0, v), axis=-1, keepdims=True)
    return jnp.where(lo, s_lo, s_hi)


def _half_sum_mxu(v):
    upper = lax.broadcasted_iota(jnp.int32, (128, 128), 0) < 64
    left = lax.broadcasted_iota(jnp.int32, (128, 128), 1) < 64
    ones = jnp.where(upper == left, 1.0, 0.0).astype(BF16)
    high = v.astype(BF16)
    low = (v - high.astype(F32)).astype(BF16)
    return _dot(high, ones, 1, 0) + _dot(low, ones, 1, 0)


def _pair_norm(zp, g, lo):
    r = lax.rsqrt(_half_sum(zp * zp, lo) * (1.0 / HEAD_DIM) + EPS)
    return zp * r * g


def _pair_norm_bwd(zp, g, dy):
    r = lax.rsqrt(_half_sum_mxu(zp * zp) * (1.0 / HEAD_DIM) + EPS)
    xn = zp * r
    dyg = dy * g
    dx = r * (dyg - xn * (_half_sum_mxu(dyg * xn) * (1.0 / HEAD_DIM)))
    return dx, jnp.sum(dy * xn, axis=0, keepdims=True)


def _to_stacked(pair, group, lo):
    rolled = pltpu.roll(pair, 64, axis=1)
    if group == 0:
        return jnp.where(lo, pair, 0.0), jnp.where(lo, rolled, 0.0)
    return jnp.where(lo, 0.0, rolled), jnp.where(lo, 0.0, pair)


def _from_stacked(even, odd, group, lo):
    if group == 0:
        return jnp.where(lo, even, pltpu.roll(odd, 64, axis=1))
    return jnp.where(lo, pltpu.roll(even, 64, axis=1), odd)


def _sigmoid(v):
    return 1.0 / (1.0 + jnp.exp(-v))


def _pool_counts(tile, n_rows):
    t1 = tile * n_rows + lax.broadcasted_iota(jnp.int32, (n_rows, POOL_WIDTH), 0) + 1
    lane = lax.broadcasted_iota(jnp.int32, (n_rows, POOL_WIDTH), 1)
    win = jnp.where(lane < 128, 2, jnp.where(lane < 256, 4, jnp.where(lane < 384, 8, 16)))
    return jnp.minimum(t1, win).astype(F32)


def _first_norm(x2, g_attn):
    s_len = x2.shape[0]
    t = 512

    def body(x_ref, g_ref, hn_ref):
        hn_ref[...] = _rms_fwd(x_ref[...], g_ref[...]).astype(BF16)

    row = pl.BlockSpec((t, D_MODEL), lambda i: (i, 0))
    return pl.pallas_call(
        body, name="first_norm", grid=(s_len // t,), in_specs=[row, _full((1, D_MODEL))], out_specs=row,
        out_shape=jax.ShapeDtypeStruct((s_len, D_MODEL), BF16), compiler_params=_params(),
    )(x2, g_attn)


def _attn_in(hn1, gq, gk, wts):
    s_len = hn1.shape[0]
    t = 512

    def body(hn_ref, gq_ref, gk_ref, sl_ref, lo_ref, me_ref, zqk_ref, u_ref, kn_ref, v_ref, qst_ref, w_ref, sems):
        @pl.when(pl.program_id(0) == 0)
        def _():
            _load_rows((sl_ref, lo_ref, me_ref), "inT", w_ref, sems)

        z = _dot(hn_ref[...], w_ref[...], 1, 1)
        zqk_ref[...] = z[:, :640]
        u_ref[...] = z[:, 768:]
        v_ref[...] = z[:, 640:768].astype(BF16)
        lo = lax.broadcasted_iota(jnp.int32, (t, 128), 1) < 64
        kn_ref[...] = _pair_norm(z[:, 512:640], gk_ref[...], lo).astype(BF16)
        for p in range(4):
            qn = _pair_norm(z[:, 128 * p:128 * p + 128], gq_ref[...], lo)
            even, odd = _to_stacked(qn, p // 2, lo)
            qst_ref[2 * p] = even.astype(BF16)
            qst_ref[2 * p + 1] = odd.astype(BF16)

    row = lambda w: pl.BlockSpec((t, w), lambda i: (i, 0))
    return pl.pallas_call(
        body, name="attn_in", grid=(s_len // t,),
        in_specs=[row(D_MODEL), _full((1, 128)), _full((1, 128))] + W_SPECS,
        out_specs=[row(640), row(POOL_WIDTH), row(128), row(128), pl.BlockSpec((N_Q_HEADS, t, 128), lambda i: (0, i, 0))],
        out_shape=[jax.ShapeDtypeStruct((s_len, 640), F32), jax.ShapeDtypeStruct((s_len, POOL_WIDTH), F32),
                   jax.ShapeDtypeStruct((s_len, 128), BF16), jax.ShapeDtypeStruct((s_len, 128), BF16),
                   jax.ShapeDtypeStruct((N_Q_HEADS, s_len, 128), BF16)],
        scratch_shapes=[pltpu.VMEM((IN_WIDTH, D_MODEL), BF16), pltpu.SemaphoreType.DMA((N_CHIPS,))],
        compiler_params=_params(),
    )(hn1, gq, gk, *wts)


def _bucket_table():
    i_idx = np.arange(BLOCK)[:, None]
    j_idx = np.arange(2 * BLOCK)[None, :]
    d = BLOCK + i_idx - j_idx
    n = np.maximum(d, 0)
    max_exact = N_BUCKETS // 2
    nf = np.maximum(n, 1).astype(np.float64)
    large = max_exact + (np.log(nf / max_exact) / np.log(MAX_DISTANCE / max_exact) * (N_BUCKETS - max_exact)).astype(np.int64)
    large = np.minimum(large, N_BUCKETS - 1)
    bucket = np.where(n < max_exact, n, large)
    return np.where((d >= 0) & (d < BLOCK), bucket, -1).astype(np.int32)


def _bias_build(rel_bias_t, bucket):
    def body(rb_ref, bucket_ref, out_ref):
        bk = bucket_ref[...]
        for h in range(N_Q_HEADS):
            acc = jnp.full((BLOCK, 2 * BLOCK), NEG, F32)
            for b in range(N_BUCKETS):
                acc = jnp.where(bk == b, rb_ref[h, b], acc)
            out_ref[0, pl.ds(h * BLOCK, BLOCK), :] = acc
            out_ref[1, pl.ds(h * BLOCK, BLOCK), :] = acc
            out_ref[1, pl.ds(h * BLOCK, BLOCK), 0:BLOCK] = jnp.full((BLOCK, BLOCK), NEG, F32)

    return pl.pallas_call(
        body, name="bias_build",
        in_specs=[pl.BlockSpec(memory_space=pltpu.SMEM), VMEM_WHOLE], out_specs=VMEM_WHOLE,
        out_shape=jax.ShapeDtypeStruct((2, N_Q_HEADS * BLOCK, 2 * BLOCK), F32),
    )(rel_bias_t, bucket)


def _head_softmax(s_ref, bias_ref, sink_ref, h):
    rows = pl.ds(pl.multiple_of(h * BLOCK, BLOCK), BLOCK)
    s = s_ref[rows, :] * (HEAD_DIM ** -0.5) + bias_ref[rows, :]
    sink = sink_ref[h]
    m = jnp.maximum(jnp.max(s, axis=-1, keepdims=True), sink)
    p = jnp.exp(s - m)
    e_sink = jnp.exp(sink - m)
    inv = 1.0 / (jnp.sum(p, axis=-1, keepdims=True) + e_sink)
    return rows, p * inv, e_sink * inv


ATTN_STEP_BLOCKS = 4
BAND = (N_Q_HEADS * BLOCK, 2 * BLOCK)


def _attn_specs():
    nb = ATTN_STEP_BLOCKS
    stacked = pl.BlockSpec((N_Q_HEADS, nb * BLOCK, 128), lambda i: (0, i, 0))
    kv = [pl.BlockSpec((BLOCK, 128), lambda i: (jnp.maximum(nb * i - 1, 0), 0)), pl.BlockSpec((nb * BLOCK, 128), lambda i: (i, 0))]
    consts = [_full((2,) + BAND), pl.BlockSpec(memory_space=pltpu.SMEM)]
    return stacked, kv, consts


def _step_blocks(i, kp_ref, kc_ref, vp_ref, vc_ref, bias_ref):
    blocks = []
    for b in range(ATTN_STEP_BLOCKS):
        if b == 0:
            k2 = jnp.concatenate([kp_ref[...], kc_ref[pl.ds(0, BLOCK), :]], axis=0)
            v2 = jnp.concatenate([vp_ref[...], vc_ref[pl.ds(0, BLOCK), :]], axis=0)
            bias = bias_ref.at[jnp.where(i == 0, 1, 0)]
        else:
            k2, v2, bias = kc_ref[pl.ds((b - 1) * BLOCK, 2 * BLOCK), :], vc_ref[pl.ds((b - 1) * BLOCK, 2 * BLOCK), :], bias_ref.at[0]
        blocks.append((pl.ds(b * BLOCK, BLOCK), k2, v2, bias))
    return blocks


def _head_lane_mask():
    rows = lax.broadcasted_iota(jnp.int32, (N_Q_HEADS * BLOCK, 128), 0)
    lanes = lax.broadcasted_iota(jnp.int32, (N_Q_HEADS * BLOCK, 128), 1)
    return (rows < 4 * BLOCK) == (lanes < 64)


def _attn_fwd(qst, kn, vb, bias_st, sinks):
    s_len = kn.shape[0]

    def body(q_ref, kp_ref, kc_ref, vp_ref, vc_ref, bias_ref, sink_ref, o_ref, s_ref, p_ref):
        for b, (rows, k2, v2, bias) in enumerate(_step_blocks(pl.program_id(0), kp_ref, kc_ref, vp_ref, vc_ref, bias_ref)):
            s_b, p_b = s_ref.at[b], p_ref.at[b]
            s_b[...] = _dot(q_ref[:, rows, :].reshape(N_Q_HEADS * BLOCK, 128), k2, 1, 1)

            def head(h, carry):
                head_rows, probs, _ = _head_softmax(s_b, bias, sink_ref, h)
                p_b[head_rows, :] = probs.astype(BF16)
                return carry

            lax.fori_loop(0, N_Q_HEADS, head, 0, unroll=True)
            o = jnp.where(_head_lane_mask(), _dot(p_b[...], v2, 1, 0), 0.0)
            o_ref[:, rows, :] = o.astype(BF16).reshape(N_Q_HEADS, BLOCK, 128)

    stacked, kv, consts = _attn_specs()
    return pl.pallas_call(
        body, name="attn_fwd", grid=(s_len // (ATTN_STEP_BLOCKS * BLOCK),),
        in_specs=[stacked] + kv + kv + consts, out_specs=stacked,
        out_shape=jax.ShapeDtypeStruct((N_Q_HEADS, s_len, 128), BF16),
        scratch_shapes=[pltpu.VMEM((ATTN_STEP_BLOCKS,) + BAND, F32), pltpu.VMEM((ATTN_STEP_BLOCKS,) + BAND, BF16)],
        compiler_params=_params(),
    )(qst, kn, kn, vb, vb, bias_st, sinks)


def _mix_out(u, ost, x2, wts, wpool, pool_scale, g_ffn):
    s_len = x2.shape[0]
    t = 512
    n = t + 16

    def body(u_ref, o_ref, x_ref, sl_ref, lo_ref, me_ref, wp_ref, sc_ref, g_ref, pooled_ref, mix_ref, h1_ref, hn_ref,
             w_ref, ext_ref, st_ref, sems):
        i = pl.program_id(0)

        @pl.when(i == 0)
        def _():
            _load_rows((sl_ref, lo_ref, me_ref), "out", w_ref, sems)
            ext_ref[...] = jnp.zeros_like(ext_ref)
            st_ref[...] = jnp.zeros_like(st_ref)

        u_tile = u_ref[...]
        ext_ref[pl.ds(POOL_HALO, t), :] = u_tile
        st_ref[pl.ds(8, n), :] = ext_ref[pl.ds(8, n), :] + ext_ref[pl.ds(7, n), :]
        st_ref[pl.ds(8, n), 128:] = st_ref[pl.ds(8, n), 128:] + st_ref[pl.ds(6, n), 128:]
        st_ref[pl.ds(8, n), 256:] = st_ref[pl.ds(8, n), 256:] + st_ref[pl.ds(4, n), 256:]
        st_ref[pl.ds(8, n), 384:] = st_ref[pl.ds(8, n), 384:] + st_ref[pl.ds(0, n), 384:]
        ext_ref[pl.ds(0, POOL_HALO), :] = ext_ref[pl.ds(t, POOL_HALO), :]
        pooled = (st_ref[pl.ds(POOL_HALO, t), :] / _pool_counts(i, t) - u_tile).astype(BF16)
        pooled_ref[...] = pooled
        for g in range(4):
            cols = slice(128 * g, 128 * g + 128)
            y = _dot(pooled[:, cols], wp_ref[g], 1, 0) * sc_ref[:, cols]
            mix_ref[:, ATTN_WIDTH + 128 * g:ATTN_WIDTH + 128 * g + 128] = y.astype(BF16)
        lo = lax.broadcasted_iota(jnp.int32, (t, 128), 1) < 64
        for p in range(4):
            a = _from_stacked(o_ref[2 * p].astype(F32), o_ref[2 * p + 1].astype(F32), p // 2, lo)
            mix_ref[:, 128 * p:128 * p + 128] = a.astype(BF16)
        h1 = x_ref[...] + _dot(mix_ref[...], w_ref[...], 1, 0)
        h1_ref[...] = h1
        hn_ref[...] = _rms_fwd(h1, g_ref[...]).astype(BF16)

    row = lambda w: pl.BlockSpec((t, w), lambda i: (i, 0))
    return pl.pallas_call(
        body, name="mix_out", grid=(s_len // t,),
        in_specs=[row(POOL_WIDTH), pl.BlockSpec((N_Q_HEADS, t, 128), lambda i: (0, i, 0)), row(D_MODEL)] + W_SPECS
        + [_full((4, 128, 128)), _full((1, POOL_WIDTH)), _full((1, D_MODEL))],
        out_specs=[row(POOL_WIDTH), row(D_MODEL), row(D_MODEL), row(D_MODEL)],
        out_shape=[jax.ShapeDtypeStruct((s_len, POOL_WIDTH), BF16), jax.ShapeDtypeStruct((s_len, D_MODEL), BF16),
                   jax.ShapeDtypeStruct((s_len, D_MODEL), F32), jax.ShapeDtypeStruct((s_len, D_MODEL), BF16)],
        scratch_shapes=[pltpu.VMEM((D_MODEL, D_MODEL), BF16), pltpu.VMEM((t + POOL_HALO, POOL_WIDTH), F32),
                        pltpu.VMEM((t + POOL_HALO, POOL_WIDTH), F32), pltpu.SemaphoreType.DMA((N_CHIPS,))],
        compiler_params=_params(),
    )(u, ost, x2, *wts, wpool, pool_scale, g_ffn)


def _ffn_ple(hn2, h1, p2, tgt, wts, g_ffn, g_ple):
    s_len = h1.shape[0]
    t = 256
    n_tiles = s_len // t

    def body(hn_ref, h1_ref, p_ref, tgt_ref, sl_ref, lo_ref, me_ref, gf_ref, gp_ref,
             loss_ref, dgate_ref, dup_ref, act_ref, dh2b_ref, hn3_ref, dgl_ref, dwp_ref, dh1_ref, dgf_ref, dgp_ref,
             wg_ref, wu_ref, wd_ref, wl_ref, wp_ref, packed_ref, gate_s, up_s, loss_acc, dwp_acc, sems):
        i = pl.program_id(0)

        @pl.when(i == 0)
        def _():
            w_refs = (sl_ref, lo_ref, me_ref)
            _load_rows(w_refs, "gateT", wg_ref, sems)
            _load_rows(w_refs, "upT", wu_ref, sems)
            _load_rows(w_refs, "down", wd_ref, sems)
            _load_rows(w_refs, "plg", wl_ref, sems)
            _load_rows(w_refs, "plp", packed_ref, sems)
            for j in range(N_CHIPS):
                for q in range(4):
                    wp_ref[pl.ds(64 * q, 64), 256 * j:256 * j + 256] = packed_ref[pl.ds(64 * j, 64), 256 * q:256 * q + 256]
            loss_acc[...] = jnp.zeros_like(loss_acc)
            dgf_ref[...] = jnp.zeros_like(dgf_ref)
            dgp_ref[...] = jnp.zeros_like(dgp_ref)

        hn = hn_ref[...]
        h1v = h1_ref[...]
        h2 = h1v
        for ch in range(N_FF_CHUNKS):
            rows = pl.ds(ch * FF_CHUNK, FF_CHUNK)
            gate = _dot(hn, wg_ref[rows, :], 1, 1)
            up = _dot(hn, wu_ref[rows, :], 1, 1)
            gate_s[ch] = gate
            up_s[ch] = up
            act = (gate * _sigmoid(gate) * up).astype(BF16)
            act_ref[ch] = act
            h2 = h2 + _dot(act, wd_ref[rows, :], 1, 0)
        gp = gp_ref[...]
        hn3 = _rms_fwd(h2, gp).astype(BF16)
        hn3_ref[...] = hn3
        gate2 = _sigmoid(_dot(hn3, wl_ref[...], 1, 0))
        p_tile = p_ref[...].astype(BF16)
        pp = _dot(p_tile, wp_ref[...], 1, 0)
        err = h2 + gate2 * pp - tgt_ref[...]
        loss_acc[...] += jnp.sum(err * err, axis=0, keepdims=True)
        dy = err * (1.0 / D_MODEL)
        _accumulate_tn(dwp_acc, p_tile, (dy * gate2).astype(BF16), i == 0)
        dgl = (dy * pp * gate2 * (1.0 - gate2)).astype(BF16)
        dgl_ref[...] = dgl
        dx3, dg3 = _rms_bwd(h2, gp, _dot(dgl, wl_ref[...], 1, 1))
        dh2 = dy + dx3
        dgp_ref[...] += dg3
        dh2b = dh2.astype(BF16)
        dh2b_ref[...] = dh2b
        dhn = jnp.zeros((t, D_MODEL), F32)
        for ch in range(N_FF_CHUNKS):
            rows = pl.ds(ch * FF_CHUNK, FF_CHUNK)
            dact = _dot(dh2b, wd_ref[rows, :], 1, 1)
            gate_v = gate_s[ch]
            up_v = up_s[ch]
            sg = _sigmoid(gate_v)
            dup = (dact * (gate_v * sg)).astype(BF16)
            dgate = (dact * up_v * (sg * (1.0 + gate_v * (1.0 - sg)))).astype(BF16)
            dup_ref[ch] = dup
            dgate_ref[ch] = dgate
            dhn = dhn + _dot(dgate, wg_ref[rows, :], 1, 0) + _dot(dup, wu_ref[rows, :], 1, 0)
        dx, dg = _rms_bwd(h1v, gf_ref[...], dhn)
        dh1_ref[...] = dh2 + dx
        dgf_ref[...] += dg

        @pl.when(i == n_tiles - 1)
        def _():
            total = jnp.sum(loss_acc[...], axis=-1, keepdims=True) * (0.5 / D_MODEL)
            loss_ref[...] = jnp.broadcast_to(total, loss_ref.shape)
            dwp_ref[...] = dwp_acc[...].astype(BF16)

    row = lambda w: pl.BlockSpec((t, w), lambda i: (i, 0))
    chunked = pl.BlockSpec((N_FF_CHUNKS, t, FF_CHUNK), lambda i: (0, i, 0))
    vec = _full((1, D_MODEL))
    act_shape = jax.ShapeDtypeStruct((N_FF_CHUNKS, s_len, FF_CHUNK), BF16)
    tok = lambda dtype: jax.ShapeDtypeStruct((s_len, D_MODEL), dtype)
    return pl.pallas_call(
        body, name="ffn_ple", grid=(n_tiles,),
        in_specs=[row(D_MODEL), row(D_MODEL), row(PLE_DIM), row(D_MODEL)] + W_SPECS + [vec, vec],
        out_specs=[_full((1, 128)), chunked, chunked, chunked] + [row(D_MODEL)] * 3 + [_full((PLE_DIM, D_MODEL)), row(D_MODEL),
                                                                                       vec, vec],
        out_shape=[jax.ShapeDtypeStruct((1, 128), F32), act_shape, act_shape, act_shape, tok(BF16), tok(BF16), tok(BF16),
                   jax.ShapeDtypeStruct((PLE_DIM, D_MODEL), BF16), tok(F32), jax.ShapeDtypeStruct((1, D_MODEL), F32),
                   jax.ShapeDtypeStruct((1, D_MODEL), F32)],
        scratch_shapes=[pltpu.VMEM((D_FF, D_MODEL), BF16)] * 3
        + [pltpu.VMEM((D_MODEL, D_MODEL), BF16), pltpu.VMEM((PLE_DIM, D_MODEL), BF16), pltpu.VMEM((PLE_DIM, D_MODEL), BF16),
           pltpu.VMEM((N_FF_CHUNKS, t, FF_CHUNK), F32), pltpu.VMEM((N_FF_CHUNKS, t, FF_CHUNK), F32), pltpu.VMEM((1, D_MODEL), F32),
           pltpu.VMEM((PLE_DIM, D_MODEL), F32), pltpu.SemaphoreType.DMA((N_CHIPS,))],
        compiler_params=_params(VMEM_LIMIT_BIG),
    )(hn2, h1, p2, tgt, *wts, g_ffn, g_ple)


def _accumulate_tn(acc_ref, a, b, first):
    @pl.when(first)
    def _():
        acc_ref[...] = _dot(a, b, 0, 0)

    @pl.when(jnp.logical_not(first))
    def _():
        acc_ref[...] += _dot(a, b, 0, 0)


def _flush_chunks(acc_ref, stage_ref, slab_ref, name, sems):
    stage_ref[...] = acc_ref[...].astype(BF16)
    off, rows = SLAB[name]
    copies = [pltpu.make_async_copy(stage_ref.at[pl.ds(j * rows, rows), :], slab_ref.at[j, pl.ds(off, rows), :], sems.at[j])
              for j in range(N_CHIPS)]
    for cp in copies:
        cp.start()
    for cp in copies:
        cp.wait()


def _mix_out_bwd(dh1, wts, pooled, wpool, pool_scale, mix, after):
    s_len = dh1.shape[0]
    t = 512
    n = t + 16
    n_tiles = s_len // t
    early_rows = GATHER_PARTS[0][1]

    def body(dh1_ref, sl_ref, lo_ref, me_ref, pooled_ref, wp_ref, sc_ref, mix_ref, after_ref, dost_ref, du_ref, dwp_ref,
             dsc_ref, slab_ref, w_ref, ext_ref, st_ref, acc_ref, stage_ref, sems):
        del after_ref
        i = pl.program_id(0)

        @pl.when(i == 0)
        def _():
            _load_rows((sl_ref, lo_ref, me_ref), "out", w_ref, sems)
            ext_ref[...] = jnp.zeros_like(ext_ref)
            st_ref[...] = jnp.zeros_like(st_ref)
            dsc_ref[...] = jnp.zeros_like(dsc_ref)
            dwp_ref[...] = jnp.zeros_like(dwp_ref)

        dh1b = dh1_ref[...].astype(BF16)
        _accumulate_tn(acc_ref, mix_ref[...], dh1b, i == 0)

        @pl.when(i == n_tiles - 1)
        def _():
            _flush_chunks(acc_ref, stage_ref, slab_ref, "out", sems)

        dmix = _dot(dh1b, w_ref[...], 1, 1)
        lo = lax.broadcasted_iota(jnp.int32, (t, 128), 1) < 64
        for p in range(4):
            even, odd = _to_stacked(dmix[:, 128 * p:128 * p + 128], p // 2, lo)
            dost_ref[2 * p] = even.astype(BF16)
            dost_ref[2 * p + 1] = odd.astype(BF16)
        pooled_v = pooled_ref[...]
        counts = _pool_counts(n_tiles - 1 - i, t)
        for g in range(4):
            cols = slice(128 * g, 128 * g + 128)
            dm = dmix[:, ATTN_WIDTH + 128 * g:ATTN_WIDTH + 128 * g + 128]
            ypre = _dot(pooled_v[:, cols], wp_ref[g], 1, 0)
            dsc_ref[:, cols] += jnp.sum(ypre * dm, axis=0, keepdims=True)
            dyp = (dm * sc_ref[:, cols]).astype(BF16)
            dwp_ref[g] += _dot(pooled_v[:, cols], dyp, 0, 0)
            dpooled = _dot(dyp, wp_ref[g], 1, 1)
            du_ref[:, cols] = -dpooled
            ext_ref[pl.ds(0, t), cols] = dpooled / counts[:, cols]
        st_ref[pl.ds(0, n), :] = ext_ref[pl.ds(0, n), :] + ext_ref[pl.ds(1, n), :]
        st_ref[pl.ds(0, n), 128:] = st_ref[pl.ds(0, n), 128:] + st_ref[pl.ds(2, n), 128:]
        st_ref[pl.ds(0, n), 256:] = st_ref[pl.ds(0, n), 256:] + st_ref[pl.ds(4, n), 256:]
        st_ref[pl.ds(0, n), 384:] = st_ref[pl.ds(0, n), 384:] + st_ref[pl.ds(8, n), 384:]
        ext_ref[pl.ds(t, POOL_HALO), :] = ext_ref[pl.ds(0, POOL_HALO), :]
        du_ref[...] += st_ref[pl.ds(0, t), :]

    rev = lambda w: pl.BlockSpec((t, w), lambda i: (n_tiles - 1 - i, 0))
    return pl.pallas_call(
        body, name="mix_out_bwd", grid=(n_tiles,),
        in_specs=[rev(D_MODEL)] + W_SPECS + [rev(POOL_WIDTH), _full((4, 128, 128)), _full((1, POOL_WIDTH)), rev(D_MODEL), ANY],
        out_specs=[pl.BlockSpec((N_Q_HEADS, t, 128), lambda i: (0, n_tiles - 1 - i, 0)), rev(POOL_WIDTH),
                   _full((4, 128, 128)), _full((1, POOL_WIDTH)), ANY],
        out_shape=[jax.ShapeDtypeStruct((N_Q_HEADS, s_len, 128), BF16), jax.ShapeDtypeStruct((s_len, POOL_WIDTH), F32),
                   jax.ShapeDtypeStruct((4, 128, 128), F32), jax.ShapeDtypeStruct((1, POOL_WIDTH), F32),
                   jax.ShapeDtypeStruct((N_CHIPS, early_rows, D_MODEL), BF16)],
        scratch_shapes=[pltpu.VMEM((D_MODEL, D_MODEL), BF16), pltpu.VMEM((t + POOL_HALO, POOL_WIDTH), F32),
                        pltpu.VMEM((t + POOL_HALO, POOL_WIDTH), F32), pltpu.VMEM((D_MODEL, D_MODEL), F32),
                        pltpu.VMEM((D_MODEL, D_MODEL), BF16), pltpu.SemaphoreType.DMA((N_CHIPS,))],
        compiler_params=_params(),
    )(dh1, *wts, pooled, wpool, pool_scale, mix, after)


def _attn_bwd(qst, kn, vb, dost, bias_st, sinks, after):
    s_len = kn.shape[0]

    def body(q_ref, kp_ref, kc_ref, vp_ref, vc_ref, do_ref, bias_ref, sink_ref, after_ref, dq_ref, dk_ref, dv_ref, dbias_ref,
             dsink_ref, s_ref, dp_ref, p_ref, dl_ref):
        del after_ref
        i = pl.program_id(0)

        @pl.when(i == 0)
        def _():
            dk_ref[...] = jnp.zeros_like(dk_ref)
            dv_ref[...] = jnp.zeros_like(dv_ref)
            dbias_ref[...] = jnp.zeros_like(dbias_ref)
            dsink_ref[...] = jnp.zeros_like(dsink_ref)

        for b, (rows, k2, v2, bias) in enumerate(_step_blocks(i, kp_ref, kc_ref, vp_ref, vc_ref, bias_ref)):
            s_b, dp_b, p_b, dl_b = s_ref.at[b], dp_ref.at[b], p_ref.at[b], dl_ref.at[b]
            q = q_ref[:, rows, :].reshape(N_Q_HEADS * BLOCK, 128)
            do = do_ref[:, rows, :].reshape(N_Q_HEADS * BLOCK, 128)
            s_b[...] = _dot(q, k2, 1, 1)
            dp_b[...] = _dot(do, v2, 1, 1)

            def head(h, carry):
                head_rows, probs, p_sink = _head_softmax(s_b, bias, sink_ref, h)
                dp = dp_b[head_rows, :]
                dsum = jnp.sum(probs * dp, axis=-1, keepdims=True)
                dlog = probs * (dp - dsum)
                dsink_ref[head_rows, :] -= p_sink * dsum
                dbias_ref[head_rows, :] += dlog
                p_b[head_rows, :] = probs.astype(BF16)
                dl_b[head_rows, :] = (dlog * (HEAD_DIM ** -0.5)).astype(BF16)
                return carry

            lax.fori_loop(0, N_Q_HEADS, head, 0, unroll=True)
            dlog_s = dl_b[...]
            dq_ref[:, rows, :] = jnp.where(_head_lane_mask(), _dot(dlog_s, k2, 1, 0), 0.0).reshape(N_Q_HEADS, BLOCK, 128)
            dk2 = _dot(dlog_s, q, 0, 0)
            dv2 = _dot(p_b[...], do, 0, 0)
            block = ATTN_STEP_BLOCKS * i + b
            prev_rows = pl.ds(pl.multiple_of(jnp.maximum(block - 1, 0) * BLOCK, BLOCK), BLOCK)
            cur_rows = pl.ds(pl.multiple_of(block * BLOCK, BLOCK), BLOCK)
            dk_ref[prev_rows, :] += dk2[:BLOCK]
            dk_ref[cur_rows, :] += dk2[BLOCK:]
            dv_ref[prev_rows, :] += dv2[:BLOCK]
            dv_ref[cur_rows, :] += dv2[BLOCK:]

    stacked, kv, consts = _attn_specs()
    per_step = (ATTN_STEP_BLOCKS,) + BAND
    return pl.pallas_call(
        body, name="attn_bwd", grid=(s_len // (ATTN_STEP_BLOCKS * BLOCK),),
        in_specs=[stacked] + kv + kv + [stacked] + consts + [ANY],
        out_specs=[stacked, _full((s_len, 128)), _full((s_len, 128)), _full(BAND), _full((N_Q_HEADS * BLOCK, 1))],
        out_shape=[jax.ShapeDtypeStruct((N_Q_HEADS, s_len, 128), F32), jax.ShapeDtypeStruct((s_len, 128), F32),
                   jax.ShapeDtypeStruct((s_len, 128), F32), jax.ShapeDtypeStruct(BAND, F32),
                   jax.ShapeDtypeStruct((N_Q_HEADS * BLOCK, 1), F32)],
        scratch_shapes=[pltpu.VMEM(per_step, F32), pltpu.VMEM(per_step, F32), pltpu.VMEM(per_step, BF16),
                        pltpu.VMEM(per_step, BF16)],
        compiler_params=_params(),
    )(qst, kn, kn, vb, vb, dost, bias_st, sinks, after)


def _small_pack(dg_attn, dg_ffn, dg_ple, dscale, dgq, dgk, dbias, dsink_rows, bucket, loss_v, dwpool):
    def body(ga_ref, gf_ref, gp_ref, sc_ref, gq_ref, gk_ref, db_ref, ds_ref, bucket_ref, loss_ref, wp_ref, out_ref):
        out_ref[pl.ds(0, SMALL["w_pool"]), :] = jnp.zeros((SMALL["w_pool"], 128), F32)
        for name, ref, n in (("g_attn", ga_ref, 8), ("g_ffn", gf_ref, 8), ("g_ple", gp_ref, 8), ("pool_scale", sc_ref, 4)):
            for k in range(n):
                out_ref[pl.ds(SMALL[name] + k, 1), :] = ref[:, 128 * k:128 * k + 128]
        for name, ref in (("g_q", gq_ref), ("g_k", gk_ref)):
            both = ref[...]
            out_ref[pl.ds(SMALL[name], 1), :] = both + pltpu.roll(both, 64, axis=1)
        out_ref[pl.ds(SMALL["loss"], 1), :] = loss_ref[...]
        bk = bucket_ref[...]
        rows = lax.broadcasted_iota(jnp.int32, (N_Q_HEADS, 128), 0)
        lanes = lax.broadcasted_iota(jnp.int32, (N_Q_HEADS, 128), 1)
        lane1 = lax.broadcasted_iota(jnp.int32, (1, 128), 1)
        rb = jnp.zeros((N_Q_HEADS, 128), F32)
        sk = jnp.zeros((1, 128), F32)
        for h in range(N_Q_HEADS):
            band = db_ref[pl.ds(h * BLOCK, BLOCK), :]
            for b in range(N_BUCKETS):
                rb = jnp.where((rows == h) & (lanes == b), jnp.sum(jnp.where(bk == b, band, 0.0)), rb)
            sk = jnp.where(lane1 == h, jnp.sum(ds_ref[pl.ds(h * BLOCK, BLOCK), :]), sk)
        out_ref[pl.ds(SMALL["rel_bias"], N_Q_HEADS), :] = rb
        out_ref[pl.ds(SMALL["sinks"], 1), :] = sk
        out_ref[pl.ds(SMALL["w_pool"], 512), :] = wp_ref[...].reshape(512, 128)

    return pl.pallas_call(
        body, name="small_pack", in_specs=[VMEM_WHOLE] * 11, out_specs=VMEM_WHOLE,
        out_shape=jax.ShapeDtypeStruct((SMALL_ROWS, 128), F32),
    )(dg_attn, dg_ffn, dg_ple, dscale, dgq, dgk, dbias, dsink_rows, bucket, loss_v, dwpool)


def _attn_in_bwd(dqst, zqk, dk, dv, du, x2, dh1, hn1, slab, wts, g_attn, gq, gk):
    s_len = x2.shape[0]
    t = 512
    n_tiles = s_len // t

    def body(dq_ref, zqk_ref, dk_ref, dv_ref, du_ref, x_ref, dh1_ref, hn_ref, slab_in_ref, sl_ref, lo_ref, me_ref, g_ref,
             gq_ref, gk_ref, dx_ref, dg_ref, dgq_ref, dgk_ref, slab_ref, w_ref, dz_ref, acc_ref, stage_ref, sems):
        del slab_in_ref
        i = pl.program_id(0)

        @pl.when(i == 0)
        def _():
            _load_rows((sl_ref, lo_ref, me_ref), "inT", w_ref, sems)
            dg_ref[...] = jnp.zeros_like(dg_ref)
            dgq_ref[...] = jnp.zeros_like(dgq_ref)
            dgk_ref[...] = jnp.zeros_like(dgk_ref)

        lo = lax.broadcasted_iota(jnp.int32, (t, 128), 1) < 64
        for p in range(4):
            dqn = _from_stacked(dq_ref[2 * p], dq_ref[2 * p + 1], p // 2, lo)
            dq_raw, dgq = _pair_norm_bwd(zqk_ref[:, 128 * p:128 * p + 128], gq_ref[...], dqn)
            dz_ref[:, 128 * p:128 * p + 128] = dq_raw.astype(BF16)
            dgq_ref[...] += dgq
        dk_raw, dgk = _pair_norm_bwd(zqk_ref[:, 512:640], gk_ref[...], dk_ref[...])
        dgk_ref[...] += dgk
        dz_ref[:, 512:640] = dk_raw.astype(BF16)
        dz_ref[:, 640:768] = dv_ref[...].astype(BF16)
        dz_ref[:, 768:] = du_ref[...].astype(BF16)
        dz = dz_ref[...]
        _accumulate_tn(acc_ref, dz, hn_ref[...], i == 0)
        dx, dg = _rms_bwd(x_ref[...], g_ref[...], _dot(dz, w_ref[...], 1, 0))
        dx_ref[...] = dh1_ref[...] + dx
        dg_ref[...] += dg

        @pl.when(i == n_tiles - 1)
        def _():
            _flush_chunks(acc_ref, stage_ref, slab_ref, "inT", sems)

    row = lambda w: pl.BlockSpec((t, w), lambda i: (i, 0))
    return pl.pallas_call(
        body, name="attn_in_bwd", grid=(n_tiles,),
        in_specs=[pl.BlockSpec((N_Q_HEADS, t, 128), lambda i: (0, i, 0)), row(640), row(128), row(128), row(POOL_WIDTH),
                  row(D_MODEL), row(D_MODEL), row(D_MODEL), ANY] + W_SPECS + [_full((1, D_MODEL)), _full((1, 128)),
                                                                              _full((1, 128))],
        out_specs=[row(D_MODEL), _full((1, D_MODEL)), _full((1, 128)), _full((1, 128)), ANY],
        out_shape=[jax.ShapeDtypeStruct((s_len, D_MODEL), F32), jax.ShapeDtypeStruct((1, D_MODEL), F32),
                   jax.ShapeDtypeStruct((1, 128), F32), jax.ShapeDtypeStruct((1, 128), F32),
                   jax.ShapeDtypeStruct(slab.shape, BF16)],
        input_output_aliases={8: 4},
        scratch_shapes=[pltpu.VMEM((IN_WIDTH, D_MODEL), BF16), pltpu.VMEM((t, IN_WIDTH), BF16),
                        pltpu.VMEM((IN_WIDTH, D_MODEL), F32), pltpu.VMEM((IN_WIDTH, D_MODEL), BF16),
                        pltpu.SemaphoreType.DMA((N_CHIPS,))],
        compiler_params=_params(),
    )(dqst, zqk, dk, dv, du, x2, dh1, hn1, slab, *wts, g_attn, gq, gk)


def _dw(lefts, b, name, slab, slab_rows, row_offs):
    a0, n_a = lefts[0], len(lefts)
    assert b.shape[1] == D_MODEL
    if a0.ndim == 3:
        n_chunks, s_len, tm = a0.shape
        m = n_chunks * tm
    else:
        s_len, tm = a0.shape
        m = tm
    tk = 2048 if n_a * tm <= 1408 else 1024
    if a0.ndim == 3:
        a_spec = pl.BlockSpec((None, tk, tm), lambda i, k: (i, k, 0))
    else:
        a_spec = pl.BlockSpec((tk, tm), lambda i, k: (k, i))
    n_steps, n_tiles = s_len // tk, m // tm
    chunk = m // N_CHIPS
    per_tile = tm // chunk

    def body(*refs):
        a_refs, b_ref = refs[:n_a], refs[n_a]
        o_ref, acc_ref, stage_ref, sems = refs[-4:]
        i, k = pl.program_id(0), pl.program_id(1)
        b_tile = b_ref[...].astype(BF16)
        for w, a_ref in enumerate(a_refs):
            _accumulate_tn(acc_ref.at[w], a_ref[...].astype(BF16), b_tile, k == 0)

        def out_copies(tile, slot):
            return [pltpu.make_async_copy(stage_ref.at[slot, w, pl.ds(jj * chunk, chunk), :],
                                          o_ref.at[tile * per_tile + jj, pl.ds(row_offs[w], chunk), :], sems.at[slot, w, jj])
                    for w in range(n_a) for jj in range(per_tile)]

        @pl.when(k == n_steps - 1)
        def _():
            slot = i % 2

            @pl.when(i >= 2)
            def _():
                for cp in out_copies(i - 2, slot):
                    cp.wait()

            stage_ref[slot] = acc_ref[...].astype(BF16)
            for cp in out_copies(i, slot):
                cp.start()

            @pl.when(i == n_tiles - 1)
            def _():
                for cp in out_copies(i, slot):
                    cp.wait()
                if n_tiles > 1:
                    for cp in out_copies(i - 1, 1 - slot):
                        cp.wait()

    in_specs = [a_spec] * n_a + [pl.BlockSpec((tk, D_MODEL), lambda i, k: (k, 0))]
    operands, aliases = [*lefts, b], {}
    if slab is not None:
        in_specs.append(ANY)
        operands.append(slab)
        aliases = {n_a + 1: 0}
    return pl.pallas_call(
        body, name=name, grid=(n_tiles, n_steps), in_specs=in_specs, out_specs=ANY,
        out_shape=jax.ShapeDtypeStruct((N_CHIPS, slab_rows, D_MODEL), BF16), input_output_aliases=aliases,
        scratch_shapes=[pltpu.VMEM((n_a, tm, D_MODEL), F32), pltpu.VMEM((2, n_a, tm, D_MODEL), BF16),
                        pltpu.SemaphoreType.DMA((2, n_a, per_tile))],
        compiler_params=_params(VMEM_LIMIT_BIG, n_axes=2),
    )(*operands)


def _position():
    x, y, c = lax.axis_index("x"), lax.axis_index("y"), lax.axis_index("c")
    other_chips = [(1 - x, y), (x, 1 - y), (1 - x, 1 - y)]
    return x, y, c, other_chips


def _ag_weights(local_slab, row0, n_rows, name, collective_id):
    half = n_rows // 2
    quarter = half // 2
    assert quarter % 16 == 0

    def body(l_ref, g_ref, send, recv):
        x, y, c, chips = _position()
        me, (via_x, via_y, diagonal) = 2 * x + y, [2 * chip[0] + chip[1] for chip in chips]
        here, sibling, x_nbr, y_nbr = (x, y, c), (x, y, 1 - c), (1 - x, y, c), (x, 1 - y, c)
        peers = [sibling, x_nbr, y_nbr]
        barrier = pltpu.get_barrier_semaphore()
        for peer in peers:
            pl.semaphore_signal(barrier, inc=1, device_id=peer, device_id_type=MESH)
        pl.semaphore_wait(barrier, len(peers))

        def rows(core, part):
            start, size = (core * half, half) if part is None else (core * half + part * quarter, quarter)
            return pl.ds(pl.multiple_of(start, 16), size)

        def copy(k, chip_idx, where, to, src=None):
            dst = g_ref.at[chip_idx, where, :]
            return pltpu.make_async_remote_copy(src_ref=dst if src is None else src, dst_ref=dst, send_sem=send.at[k],
                                                recv_sem=recv.at[k], device_id=to, device_id_type=MESH)

        own_rows = l_ref.at[pl.ds(pl.multiple_of(row0 + c * half, 16), half), :]
        started = [copy(0, me, rows(c, None), x_nbr, src=own_rows), copy(1, me, rows(c, None), y_nbr, src=own_rows)]
        for cp in started:
            cp.start()
        after_arrival = [
            (copy(0, via_x, rows(c, None), here), [copy(4, via_x, rows(c, None), sibling), copy(3, via_x, rows(c, 1), y_nbr)]),
            (copy(1, via_y, rows(c, None), here), [copy(5, via_y, rows(c, None), sibling), copy(2, via_y, rows(c, 0), x_nbr)]),
            (copy(2, diagonal, rows(c, 0), here), [copy(6, diagonal, rows(c, 0), sibling)]),
            (copy(3, diagonal, rows(c, 1), here), [copy(7, diagonal, rows(c, 1), sibling)]),
        ]
        for arrival, onward in after_arrival:
            arrival.wait_recv()
            for cp in onward:
                cp.start()
            started += onward
        for cp in (copy(4, via_x, rows(1 - c, None), here), copy(5, via_y, rows(1 - c, None), here),
                   copy(6, diagonal, rows(1 - c, 0), here), copy(7, diagonal, rows(1 - c, 1), here)):
            cp.wait_recv()
        for cp in started:
            cp.wait_send()

    return pl.kernel(
        body, out_type=jax.ShapeDtypeStruct((N_CHIPS, n_rows, D_MODEL), BF16),
        mesh=plsc.ScalarSubcoreMesh(axis_name="sequencer", num_cores=1), name=name,
        scratch_types=[pltpu.SemaphoreType.DMA((8,)), pltpu.SemaphoreType.DMA((8,))],
        compiler_params=pltpu.CompilerParams(collective_id=collective_id),
    )(local_slab)


def _comm_call(body, peers_of, out_shape, n_sems, operand, name, collective_id):
    sems = [pltpu.SemaphoreType.DMA((n_sems,)), pltpu.SemaphoreType.DMA((n_sems,))]

    def with_handshake(in_ref, out_ref, send, recv):
        x, y, c, _ = _position()
        peers = peers_of(x, y, c)
        barrier = pltpu.get_barrier_semaphore()
        for peer in peers:
            pl.semaphore_signal(barrier, inc=1, device_id=peer, device_id_type=MESH)
        pl.semaphore_wait(barrier, len(peers))
        body(in_ref, out_ref, send, recv)

    return pl.kernel(with_handshake, out_type=out_shape, mesh=plsc.ScalarSubcoreMesh(axis_name="sequencer", num_cores=1),
                     name=name, scratch_types=sems, compiler_params=pltpu.CompilerParams(collective_id=collective_id))(operand)


def _rs_swap_halves(partial, name, collective_id):
    half = partial.shape[1] // 2

    def body(p_ref, r_ref, send, recv):
        x, y, c, _ = _position()
        theirs = pl.ds(pl.multiple_of((1 - c) * half, 16), half)
        cp = pltpu.make_async_remote_copy(src_ref=p_ref.at[:, theirs, :], dst_ref=r_ref, send_sem=send.at[0],
                                          recv_sem=recv.at[0], device_id=(x, y, 1 - c), device_id_type=MESH)
        cp.start()
        cp.wait()

    return _comm_call(body, lambda x, y, c: [(x, y, 1 - c)], jax.ShapeDtypeStruct((N_CHIPS, half, D_MODEL), BF16), 1,
                      partial, name, collective_id)


def _rs_add_halves(partial, other, core, name, after):
    half = other.shape[1]
    t = half // 2
    steps = half // t

    def body(core_ref, a_ref, b_ref, after_ref, o_ref):
        del after_ref
        o_ref[...] = (a_ref[...].astype(F32) + b_ref[...].astype(F32)).astype(BF16)

    return pl.pallas_call(
        body, name=name,
        grid_spec=pltpu.PrefetchScalarGridSpec(
            num_scalar_prefetch=1, grid=(N_CHIPS, steps),
            in_specs=[pl.BlockSpec((1, t, D_MODEL), lambda j, i, core_ref: (j, core_ref[0] * steps + i, 0)),
                      pl.BlockSpec((1, t, D_MODEL), lambda j, i, core_ref: (j, i, 0)), ANY],
            out_specs=pl.BlockSpec((1, t, D_MODEL), lambda j, i, core_ref: (j, i, 0))),
        out_shape=jax.ShapeDtypeStruct((N_CHIPS, half, D_MODEL), BF16),
        compiler_params=_params(n_axes=2),
    )(core, partial, other, after)


def _rs_exchange_chips(pre, name, collective_id):
    def body(s_ref, r_ref, send, recv):
        x, y, c, chips = _position()

        def copy(k, chunk, to):
            return pltpu.make_async_remote_copy(src_ref=s_ref.at[chunk], dst_ref=r_ref.at[k], send_sem=send.at[k],
                                                recv_sem=recv.at[k], device_id=to, device_id_type=MESH)

        sends = [copy(k, 2 * chip[0] + chip[1], (*chip, c)) for k, chip in enumerate(chips)]
        for cp in sends:
            cp.start()
        for cp in sends:
            cp.wait()

    return _comm_call(body, lambda x, y, c: [(1 - x, y, c), (x, 1 - y, c), (1 - x, 1 - y, c)],
                      jax.ShapeDtypeStruct((3, pre.shape[1], D_MODEL), BF16), 3, pre, name, collective_id)


def _gather_small(s_ref, t_ref, send, recv):
    x, y, c, chips = _position()
    sibling = (x, y, 1 - c)

    def slot(px, py, pc):
        return t_ref.at[4 * px + 2 * py + pc]

    def copy(k, block, to, src=None):
        return pltpu.make_async_remote_copy(src_ref=slot(*block) if src is None else src, dst_ref=slot(*block),
                                            send_sem=send.at[k], recv_sem=recv.at[k], device_id=to, device_id_type=MESH)

    own = pltpu.make_async_copy(s_ref, slot(x, y, c), send.at[7])
    first = [copy(0, (x, y, c), sibling, src=s_ref)]
    first += [copy(1 + k, (x, y, c), (*chip, c), src=s_ref) for k, chip in enumerate(chips)]

    def start():
        own.start()
        for cp in first:
            cp.start()

    def finish():
        passed = []
        for k, chip in enumerate(chips):
            copy(1 + k, (*chip, c), (x, y, c)).wait_recv()
            fwd = copy(4 + k, (*chip, c), sibling)
            fwd.start()
            passed.append(fwd)
        copy(0, sibling, (x, y, c)).wait_recv()
        for k, chip in enumerate(chips):
            copy(4 + k, (*chip, 1 - c), (x, y, c)).wait_recv()
        for cp in first + passed:
            cp.wait_send()
        own.wait()

    return start, finish


def _rs_sum_chips(pre, received, place, name, after):
    half = pre.shape[1]
    steps = 4 if half > 512 else 2
    t = half // steps
    assert t % 16 == 0 and t * steps == half

    def body(place_ref, own_ref, r_ref, after_ref, o_ref, stage, kept_sems, send, recv):
        del place_ref, after_ref
        i = pl.program_id(0)
        x, y, c, _ = _position()

        def rows(core, step):
            return o_ref.at[pl.ds(pl.multiple_of((core * steps + step) * t, 8), t), :]

        def kept(step):
            return pltpu.make_async_copy(stage.at[step], rows(c, step), kept_sems.at[step])

        def sent(core, step):
            return pltpu.make_async_remote_copy(src_ref=stage.at[step], dst_ref=rows(core, step), send_sem=send.at[step],
                                                recv_sem=recv.at[step], device_id=(x, y, 1 - core), device_id_type=MESH)

        acc = own_ref[0].astype(F32)
        for k in range(3):
            acc = acc + r_ref[k].astype(F32)
        stage[i] = acc
        kept(i).start()
        sent(c, i).start()

        @pl.when(i == steps - 1)
        def _():
            for step in range(steps):
                kept(step).wait()
                sent(c, step).wait_send()
                sent(1 - c, step).wait_recv()

    return pl.pallas_call(
        body, name=name,
        grid_spec=pltpu.PrefetchScalarGridSpec(
            num_scalar_prefetch=1, grid=(steps,),
            in_specs=[pl.BlockSpec((1, t, D_MODEL), lambda i, place_ref: (place_ref[0], i, 0)),
                      pl.BlockSpec((3, t, D_MODEL), lambda i, place_ref: (0, i, 0)), ANY],
            out_specs=ANY,
            scratch_shapes=[pltpu.VMEM((steps, t, D_MODEL), F32)] + [pltpu.SemaphoreType.DMA((steps,))] * 3),
        out_shape=jax.ShapeDtypeStruct((2 * half, D_MODEL), F32), compiler_params=_params(),
    )(place, pre, received, after)


def _adam_update(w, g, m, v):
    m_new = ADAM_B1 * m + (1.0 - ADAM_B1) * g
    v_new = ADAM_B2 * v + (1.0 - ADAM_B2) * (g * g)
    m_hat = m_new / (1.0 - ADAM_B1 ** ADAM_STEP)
    v_hat = v_new / (1.0 - ADAM_B2 ** ADAM_STEP)
    return -ADAM_LR * (m_hat / (jnp.sqrt(v_hat) + ADAM_EPS) + ADAM_WD * w), m_new, v_new


def _adamw(w, g_rows, row_off, m, v, name):
    rows, cols = w.shape
    t = rows if rows <= 320 else (rows // 2 if rows % 256 else 256)

    def body(w_ref, g_ref, m_ref, v_ref, go_ref, d_ref, nm_ref, nv_ref):
        g = g_ref[...]
        go_ref[...] = g
        d_ref[...], nm_ref[...], nv_ref[...] = _adam_update(w_ref[...], g, m_ref[...], v_ref[...])

    blk = pl.BlockSpec((t, cols), lambda i: (i, 0))
    assert row_off % 8 == 0 and t % 8 == 0
    g_blk = pl.BlockSpec((pl.Element(t), pl.Element(cols)), lambda i: (pl.multiple_of(row_off + i * t, 8), 0))
    shape = jax.ShapeDtypeStruct((rows, cols), F32)
    return pl.pallas_call(
        body, name=name, grid=(rows // t,), in_specs=[blk, g_blk, blk, blk], out_specs=[blk] * 4, out_shape=[shape] * 4,
        compiler_params=_params(),
    )(w, g_rows, m, v)


def _adamw_ffn(ws, g_rows, row_off, ms, vs, small):
    n = len(ws)
    rows, cols = ws[0].shape
    steps = 4
    t = rows // steps
    assert t % 8 == 0 and t * steps == rows and row_off % 8 == 0

    def body(*refs):
        w_refs, g_ref, m_refs, v_refs, small_ref = refs[:n], refs[n], refs[n + 1:2 * n + 1], refs[2 * n + 1:3 * n + 1], refs[3 * n + 1]
        outs, t_ref, t_send, t_recv = refs[3 * n + 2:7 * n + 2], refs[7 * n + 2], refs[7 * n + 3], refs[7 * n + 4]
        k, i = pl.program_id(0), pl.program_id(1)
        start_tables, finish_tables = _gather_small(small_ref, t_ref, t_send, t_recv)
        pl.when((k == 0) & (i == 0))(start_tables)
        for j in range(n):
            @pl.when(k == j)
            def _(j=j):
                go_ref, d_ref, nm_ref, nv_ref = outs[4 * j:4 * j + 4]
                g = g_ref[...]
                go_ref[...] = g
                d_ref[...], nm_ref[...], nv_ref[...] = _adam_update(w_refs[j][...], g, m_refs[j][...], v_refs[j][...])
        pl.when((k == n - 1) & (i == steps - 1))(finish_tables)

    def blk(j):
        return pl.BlockSpec((t, cols), lambda k, i: (jnp.where(k == j, i, jnp.where(k < j, 0, steps - 1)), 0))

    g_blk = pl.BlockSpec((pl.Element(t), pl.Element(cols)), lambda k, i: (pl.multiple_of(row_off + (k * steps + i) * t, 8), 0))
    per_weight = [blk(j) for j in range(n)]
    shape = jax.ShapeDtypeStruct((rows, cols), F32)
    res = pl.pallas_call(
        body, name="adamw_ffn", grid=(n, steps), in_specs=per_weight + [g_blk] + per_weight * 2 + [VMEM_WHOLE],
        out_specs=[blk(j) for j in range(n) for _ in range(4)] + [ANY],
        out_shape=[shape] * (4 * n) + [jax.ShapeDtypeStruct((N_DEV, *small.shape), F32)],
        scratch_shapes=[pltpu.SemaphoreType.DMA((8,))] * 2, compiler_params=_params(n_axes=2),
    )(*ws, g_rows, *ms, *vs, small)
    return [res[4 * j:4 * j + 4] for j in range(n)], res[4 * n]


SMALL_PARAMS = [("g_attn", (1, D_MODEL), 8), ("g_q", (1, HEAD_DIM), None), ("g_k", (1, HEAD_DIM), None),
                ("sinks", (1, N_Q_HEADS), None), ("rel_bias", (N_Q_HEADS, N_BUCKETS), None), ("w_pool", (512, 128), None),
                ("pool_scale", (1, POOL_WIDTH), 4), ("g_ffn", (1, D_MODEL), 8), ("g_ple", (1, D_MODEL), 8)]


def _adamw_small(tables, wmv):
    n_par = len(SMALL_PARAMS)

    def body(*refs):
        t_ref = refs[0]
        ins = refs[1:1 + 3 * n_par]
        loss_ref = refs[1 + 3 * n_par]
        outs = refs[2 + 3 * n_par:-1]
        tot_ref = refs[-1]
        total = t_ref[0]
        for d in range(1, N_DEV):
            total = total + t_ref[d]
        tot_ref[...] = total
        loss_ref[...] = tot_ref[pl.ds(SMALL["loss"], 1), 0:1]
        for i, (name, shape, split) in enumerate(SMALL_PARAMS):
            g_ref, d_ref, nm_ref, nv_ref = outs[4 * i:4 * i + 4]
            row = SMALL[name]
            if split:
                for k in range(split):
                    g_ref[:, 128 * k:128 * k + 128] = tot_ref[pl.ds(row + k, 1), :]
            else:
                g_ref[...] = tot_ref[pl.ds(row, shape[0]), 0:shape[1]]
            w_ref, m_ref, v_ref = ins[3 * i:3 * i + 3]
            d_ref[...], nm_ref[...], nv_ref[...] = _adam_update(w_ref[...], g_ref[...], m_ref[...], v_ref[...])

    shapes = [jax.ShapeDtypeStruct((1, 1), F32)]
    for _, shape, _ in SMALL_PARAMS:
        shapes += [jax.ShapeDtypeStruct(shape, F32)] * 4
    flat = [a for triple in wmv for a in triple]
    res = pl.pallas_call(
        body, name="adamw_small", in_specs=[VMEM_WHOLE] * (1 + 3 * n_par), out_specs=[VMEM_WHOLE] * len(shapes),
        out_shape=shapes, scratch_shapes=[pltpu.VMEM((SMALL_ROWS, 128), F32)],
    )(tables, *flat)
    return res[0], [res[1 + 4 * i:5 + 4 * i] for i in range(n_par)]


def _pack_ple_proj(shard):
    return shard.reshape(4, 64, 256).transpose(1, 0, 2).reshape(64, D_MODEL)


class _Reduction:
    def __init__(self, tag, place, ids=(None, None)):
        self.tag, self.place, self.ids = tag, place, ids

    def start(self, partial):
        self.partial = partial
        self.other = _rs_swap_halves(partial, "rs_swap_" + self.tag, self.ids[0])
        return partial

    def middle(self, after):
        self.pre = _rs_add_halves(self.partial, self.other, self.place[1:], "rs_add_" + self.tag, after)
        self.received = _rs_exchange_chips(self.pre, "rs_exchange_" + self.tag, self.ids[1])
        return self.pre

    def finish(self, after):
        return _rs_sum_chips(self.pre, self.received, self.place, "rs_sum_" + self.tag, after)


def _local_grads(x2, p2, tgt, wts, g_attn_norm, g_q, g_k, attn_sinks, rel_bias, w_pool, pool_scale, g_ffn_norm, g_ple_norm,
                 reduce_a):
    w_early, w_late = wts
    w_in = w_out = w_early
    bucket = jnp.asarray(_bucket_table())
    gq = jnp.tile(g_q, (1, 2))
    gk = jnp.tile(g_k, (1, 2))
    wpool = w_pool[0].astype(BF16)
    sinks = attn_sinks[0]
    bias_st = _bias_build(rel_bias.T, bucket)

    hn1 = _first_norm(x2, g_attn_norm)
    zqk, u, kn, vb, qst = _attn_in(hn1, gq, gk, w_in)
    ost = _attn_fwd(qst, kn, vb, bias_st, sinks)
    pooled, mix, h1, hn2 = _mix_out(u, ost, x2, w_out, wpool, pool_scale, g_ffn_norm)
    loss_v, dgate, dup, act, dh2, hn3, dgl, dw_plp, dh1, dg_ffn, dg_ple = _ffn_ple(hn2, h1, p2, tgt, w_late, g_ffn_norm,
                                                                                      g_ple_norm)

    late0, late_rows = GATHER_PARTS[1][0], SLAB_ROWS - GATHER_PARTS[1][0]
    partial_a = None
    for names, lefts, right in ((("gateT", "upT"), [dgate, dup], hn2), (("down",), [act], dh2), (("plg",), [hn3], dgl)):
        partial_a = _dw(lefts, right, "dw_" + names[0], partial_a, late_rows, [SLAB[name][0] - late0 for name in names])
    dw_plp = dw_plp.reshape(4, 64, N_CHIPS, 256).transpose(2, 1, 0, 3).reshape(N_CHIPS, 64, D_MODEL)
    partial_a = reduce_a.start(lax.dynamic_update_slice(partial_a, dw_plp, (0, SLAB["plp"][0] - late0, 0)))
    dost, du, dw_pool, dscale, partial_b = _mix_out_bwd(dh1, w_out, pooled, wpool, pool_scale, mix, partial_a)
    pre_a = reduce_a.middle(du)
    dqst, dk, dv, dbias, dsink_rows = _attn_bwd(qst, kn, vb, dost, bias_st, sinks, pre_a)
    dx, dg_attn, dgq, dgk, partial_b = _attn_in_bwd(dqst, zqk, dk, dv, du, x2, dh1, hn1, partial_b, w_in, g_attn_norm, gq, gk)

    small = _small_pack(dg_attn, dg_ffn, dg_ple, dscale, dgq, dgk, dbias, dsink_rows, bucket, loss_v, dw_pool)
    return dx, partial_b, small


def kernel(x, p, w_in, w_out, g_attn_norm, g_q, g_k, attn_sinks, rel_bias, w_pool, pool_scale, g_ffn_norm, w_gate, w_up, w_down, g_ple_norm, w_ple_gate, w_ple_proj, loss_target, m_w_in, m_w_out, m_g_attn_norm, m_g_q, m_g_k, m_attn_sinks, m_rel_bias, m_w_pool, m_pool_scale, m_g_ffn_norm, m_w_gate, m_w_up, m_w_down, m_g_ple_norm, m_w_ple_gate, m_w_ple_proj, v_w_in, v_w_out, v_g_attn_norm, v_g_q, v_g_k, v_attn_sinks, v_rel_bias, v_w_pool, v_pool_scale, v_g_ffn_norm, v_w_gate, v_w_up, v_w_down, v_g_ple_norm, v_w_ple_gate, v_w_ple_proj):
    core = lax.axis_index("c").astype(jnp.int32).reshape(1)
    me = (2 * lax.axis_index("x") + lax.axis_index("y")).astype(jnp.int32).reshape(1)

    local_parts = [jnp.concatenate(pieces, axis=0).astype(BF16) for pieces in (
        [w_in[0].T, w_out[0]], [w_gate[0].T, w_up[0].T, w_down[0], w_ple_gate[0], _pack_ple_proj(w_ple_proj[0])])]
    wts = [(_ag_weights(local, 0, local.shape[0], name, collective_id), local, me)
           for local, name, collective_id in zip(local_parts, ("ag_early", "ag_late"), (1, 2))]

    place = jnp.concatenate([me, core])
    reduce_a = _Reduction("a", place, ids=(3, 4))
    dx, partial_b, small = _local_grads(x[0], p[0, 0], loss_target[0], wts, g_attn_norm, g_q, g_k, attn_sinks, rel_bias,
                                        w_pool, pool_scale, g_ffn_norm, g_ple_norm, reduce_a)
    reduce_b = _Reduction("b", place, ids=(6, 7))
    reduce_b.start(partial_b)
    grads_a = reduce_a.finish(small)
    reduce_b.middle(grads_a)

    late0 = GATHER_PARTS[1][0]

    def rows(name):
        return grads_a, SLAB[name][0] - late0

    plp_rows = grads_a[SLAB["plp"][0] - late0:]
    grads, deltas, new_ms, new_vs = {}, {}, {}, {}
    ffn = {"w_gate": (w_gate, m_w_gate, v_w_gate, True), "w_up": (w_up, m_w_up, v_w_up, True),
           "w_down": (w_down, m_w_down, v_w_down, False)}
    views = {name: ((lambda a: a.T) if transposed else (lambda a: a)) for name, (_, _, _, transposed) in ffn.items()}
    assert [SLAB[n][0] for n in ("gateT", "upT", "down")] == [late0 + k * SLAB["gateT"][1] for k in range(3)]
    ws, ms, vs = ([views[name](wmv[k][0]) for name, wmv in ffn.items()] for k in range(3))
    ffn_out, small_all = _adamw_ffn(ws, *rows("gateT"), ms, vs, small)
    for name, out in zip(ffn, ffn_out):
        grads[name], deltas[name], new_ms[name], new_vs[name] = (views[name](a)[None] for a in out)
    big = {
        "w_ple_gate": (w_ple_gate, m_w_ple_gate, v_w_ple_gate, rows("plg"), False),
        "w_ple_proj": (w_ple_proj, m_w_ple_proj, v_w_ple_proj,
                       (plp_rows.reshape(64, 4, 256).transpose(1, 0, 2).reshape(PLE_DIM, PLE_DIM), 0), False),
        "w_out": (w_out, m_w_out, v_w_out, None, False),
        "w_in": (w_in, m_w_in, v_w_in, None, True),
    }
    small_params = {
        "g_attn_norm": (g_attn_norm, m_g_attn_norm, v_g_attn_norm), "g_q": (g_q, m_g_q, v_g_q), "g_k": (g_k, m_g_k, v_g_k),
        "attn_sinks": (attn_sinks, m_attn_sinks, v_attn_sinks), "rel_bias": (rel_bias.T, m_rel_bias.T, v_rel_bias.T),
        "w_pool": tuple(a.reshape(512, 128) for a in (w_pool, m_w_pool, v_w_pool)),
        "pool_scale": (pool_scale, m_pool_scale, v_pool_scale), "g_ffn_norm": (g_ffn_norm, m_g_ffn_norm, v_g_ffn_norm),
        "g_ple_norm": (g_ple_norm, m_g_ple_norm, v_g_ple_norm),
    }

    grads_b = None
    for name, (w, m, v, g_src, transposed) in big.items():
        if g_src is None:
            if grads_b is None:
                grads_b = reduce_b.finish(out[-1])
            g_src = (grads_b, SLAB["out" if name == "w_out" else "inT"][0])
        view = (lambda a: a.T) if transposed else (lambda a: a)
        out = _adamw(view(w[0]), *g_src, view(m[0]), view(v[0]), "adamw_" + name)
        grads[name], deltas[name], new_ms[name], new_vs[name] = (view(a)[None] for a in out)

    loss, small_out = _adamw_small(small_all, list(small_params.values()))
    for name, (g2, d, nm, nv) in zip(small_params, small_out):
        restore = {"w_pool": lambda a: a.reshape(w_pool.shape), "rel_bias": lambda a: a.T}.get(name, lambda a: a)
        grads[name], deltas[name], new_ms[name], new_vs[name] = (restore(a) for a in (g2, d, nm, nv))

    order = ["w_in", "w_out", "g_attn_norm", "g_q", "g_k", "attn_sinks", "rel_bias", "w_pool", "pool_scale", "g_ffn_norm",
             "w_gate", "w_up", "w_down", "g_ple_norm", "w_ple_gate", "w_ple_proj"]
    return (loss.reshape(()), dx[None], *[grads[n] for n in order], *[deltas[n] for n in order],
            *[new_ms[n] for n in order], *[new_vs[n] for n in order])
```

```python
import numpy as np
import jax
import jax.numpy as jnp
from jax import lax
from jax.experimental import pallas as pl
from jax.experimental.pallas import tpu as pltpu
from jax.experimental.pallas import tpu_sc as plsc

F32 = jnp.float32
BF16 = jnp.bfloat16
MESH = pl.DeviceIdType.MESH

D_MODEL = 1024
HEAD_DIM = 64
N_Q_HEADS = 8
ATTN_WIDTH = 512
POOL_WIDTH = 512
IN_WIDTH = 1280
D_FF = 2816
PLE_DIM = 256
FF_CHUNK = 1408
N_FF_CHUNKS = D_FF // FF_CHUNK
BLOCK = 128
N_BUCKETS = 32
MAX_DISTANCE = 128
EPS = 1e-6
NEG = -1e30
N_CHIPS = 4
N_DEV = 8

ADAM_LR = 0.001
ADAM_B1 = 0.9
ADAM_B2 = 0.999
ADAM_EPS = 1e-08
ADAM_WD = 0.01
ADAM_STEP = 10

SLAB = {"inT": (0, 320), "out": (320, 256), "gateT": (576, 704), "upT": (1280, 704), "down": (1984, 704),
        "plg": (2688, 256), "plp": (2944, 64)}
SLAB_ROWS = 3008
GATHER_PARTS = ((0, 576), (576, SLAB_ROWS))
POOL_HALO = 24

SMALL = {"g_attn": 0, "g_ffn": 8, "g_ple": 16, "pool_scale": 24, "g_q": 28, "g_k": 29, "sinks": 30, "loss": 31,
         "rel_bias": 32}
SMALL_ROWS = 64

VMEM_LIMIT_BIG = 60 * 1024 * 1024
VMEM_LIMIT = 48 * 1024 * 1024


def _params(vmem=VMEM_LIMIT, n_axes=1):
    return pltpu.CompilerParams(dimension_semantics=("arbitrary",) * n_axes, vmem_limit_bytes=vmem)


def _dot(a, b, ca, cb):
    return lax.dot_general(a, b, (((ca,), (cb,)), ((), ())), preferred_element_type=F32)


def _full(shape):
    return pl.BlockSpec(shape, lambda i: (0,) * len(shape))


ANY = pl.BlockSpec(memory_space=pl.ANY)
VMEM_WHOLE = pl.BlockSpec(memory_space=pltpu.VMEM)


W_SPECS = [ANY, ANY, pl.BlockSpec(memory_space=pltpu.SMEM)]


def _load_rows(w_refs, name, dst_ref, sems):
    slab_ref, local_ref, me_ref = w_refs
    off, rows = SLAB[name]
    slab_off = off - max(start for start, _ in GATHER_PARTS if start <= off)
    me = me_ref[0]
    for phase in ("start", "wait"):
        for j in range(N_CHIPS):
            dst = dst_ref.at[pl.ds(j * rows, rows), :]
            theirs = pltpu.make_async_copy(slab_ref.at[j, pl.ds(slab_off, rows), :], dst, sems.at[j])
            own = pltpu.make_async_copy(local_ref.at[pl.ds(slab_off, rows), :], dst, sems.at[j])

            @pl.when(me == j)
            def _():
                getattr(own, phase)()

            @pl.when(me != j)
            def _():
                getattr(theirs, phase)()


def _rms_fwd(x, g):
    r = lax.rsqrt(jnp.mean(x * x, axis=-1, keepdims=True) + EPS)
    return x * r * g


def _rms_bwd(x, g, dy):
    r = lax.rsqrt(jnp.mean(x * x, axis=-1, keepdims=True) + EPS)
    xn = x * r
    dyg = dy * g
    dx = r * (dyg - xn * jnp.mean(dyg * xn, axis=-1, keepdims=True))
    return dx, jnp.sum(dy * xn, axis=0, keepdims=True)


def _half_sum(v, lo):
    s_lo = jnp.sum(jnp.where(lo, v, 0.0), axis=-1, keepdims=True)
    s_hi = jnp.sum(jnp.where(lo, 0.0, v), axis=-1, keepdims=True)
    return jnp.where(lo, s_lo, s_hi)


def _half_sum_mxu(v):
    upper = lax.broadcasted_iota(jnp.int32, (128, 128), 0) < 64
    left = lax.broadcasted_iota(jnp.int32, (128, 128), 1) < 64
    ones = jnp.where(upper == left, 1.0, 0.0).astype(BF16)
    high = v.astype(BF16)
    low = (v - high.astype(F32)).astype(BF16)
    return _dot(high, ones, 1, 0) + _dot(low, ones, 1, 0)


def _pair_norm(zp, g, lo):
    r = lax.rsqrt(_half_sum(zp * zp, lo) * (1.0 / HEAD_DIM) + EPS)
    return zp * r * g


def _pair_norm_bwd(zp, g, dy):
    r = lax.rsqrt(_half_sum_mxu(zp * zp) * (1.0 / HEAD_DIM) + EPS)
    xn = zp * r
    dyg = dy * g
    dx = r * (dyg - xn * (_half_sum_mxu(dyg * xn) * (1.0 / HEAD_DIM)))
    return dx, jnp.sum(dy * xn, axis=0, keepdims=True)


def _to_stacked(pair, group, lo):
    rolled = pltpu.roll(pair, 64, axis=1)
    if group == 0:
        return jnp.where(lo, pair, 0.0), jnp.where(lo, rolled, 0.0)
    return jnp.where(lo, 0.0, rolled), jnp.where(lo, 0.0, pair)


def _from_stacked(even, odd, group, lo):
    if group == 0:
        return jnp.where(lo, even, pltpu.roll(odd, 64, axis=1))
    return jnp.where(lo, pltpu.roll(even, 64, axis=1), odd)


def _sigmoid(v):
    return 1.0 / (1.0 + jnp.exp(-v))


def _pool_counts(tile, n_rows):
    t1 = tile * n_rows + lax.broadcasted_iota(jnp.int32, (n_rows, POOL_WIDTH), 0) + 1
    lane = lax.broadcasted_iota(jnp.int32, (n_rows, POOL_WIDTH), 1)
    win = jnp.where(lane < 128, 2, jnp.where(lane < 256, 4, jnp.where(lane < 384, 8, 16)))
    return jnp.minimum(t1, win).astype(F32)


def _first_norm(x2, g_attn):
    s_len = x2.shape[0]
    t = 512

    def body(x_ref, g_ref, hn_ref):
        hn_ref[...] = _rms_fwd(x_ref[...], g_ref[...]).astype(BF16)

    row = pl.BlockSpec((t, D_MODEL), lambda i: (i, 0))
    return pl.pallas_call(
        body, name="first_norm", grid=(s_len // t,), in_specs=[row, _full((1, D_MODEL))], out_specs=row,
        out_shape=jax.ShapeDtypeStruct((s_len, D_MODEL), BF16), compiler_params=_params(),
    )(x2, g_attn)


def _attn_in(hn1, gq, gk, wts):
    s_len = hn1.shape[0]
    t = 512

    def body(hn_ref, gq_ref, gk_ref, sl_ref, lo_ref, me_ref, zqk_ref, u_ref, kn_ref, v_ref, qst_ref, w_ref, sems):
        @pl.when(pl.program_id(0) == 0)
        def _():
            _load_rows((sl_ref, lo_ref, me_ref), "inT", w_ref, sems)

        z = _dot(hn_ref[...], w_ref[...], 1, 1)
        zqk_ref[...] = z[:, :640]
        u_ref[...] = z[:, 768:]
        v_ref[...] = z[:, 640:768].astype(BF16)
        lo = lax.broadcasted_iota(jnp.int32, (t, 128), 1) < 64
        kn_ref[...] = _pair_norm(z[:, 512:640], gk_ref[...], lo).astype(BF16)
        for p in range(4):
            qn = _pair_norm(z[:, 128 * p:128 * p + 128], gq_ref[...], lo)
            even, odd = _to_stacked(qn, p // 2, lo)
            qst_ref[2 * p] = even.astype(BF16)
            qst_ref[2 * p + 1] = odd.astype(BF16)

    row = lambda w: pl.BlockSpec((t, w), lambda i: (i, 0))
    return pl.pallas_call(
        body, name="attn_in", grid=(s_len // t,),
        in_specs=[row(D_MODEL), _full((1, 128)), _full((1, 128))] + W_SPECS,
        out_specs=[row(640), row(POOL_WIDTH), row(128), row(128), pl.BlockSpec((N_Q_HEADS, t, 128), lambda i: (0, i, 0))],
        out_shape=[jax.ShapeDtypeStruct((s_len, 640), F32), jax.ShapeDtypeStruct((s_len, POOL_WIDTH), F32),
                   jax.ShapeDtypeStruct((s_len, 128), BF16), jax.ShapeDtypeStruct((s_len, 128), BF16),
                   jax.ShapeDtypeStruct((N_Q_HEADS, s_len, 128), BF16)],
        scratch_shapes=[pltpu.VMEM((IN_WIDTH, D_MODEL), BF16), pltpu.SemaphoreType.DMA((N_CHIPS,))],
        compiler_params=_params(),
    )(hn1, gq, gk, *wts)


def _bucket_table():
    i_idx = np.arange(BLOCK)[:, None]
    j_idx = np.arange(2 * BLOCK)[None, :]
    d = BLOCK + i_idx - j_idx
    n = np.maximum(d, 0)
    max_exact = N_BUCKETS // 2
    nf = np.maximum(n, 1).astype(np.float64)
    large = max_exact + (np.log(nf / max_exact) / np.log(MAX_DISTANCE / max_exact) * (N_BUCKETS - max_exact)).astype(np.int64)
    large = np.minimum(large, N_BUCKETS - 1)
    bucket = np.where(n < max_exact, n, large)
    return np.where((d >= 0) & (d < BLOCK), bucket, -1).astype(np.int32)


def _bias_build(rel_bias_t, bucket):
    def body(rb_ref, bucket_ref, out_ref):
        bk = bucket_ref[...]
        for h in range(N_Q_HEADS):
            acc = jnp.full((BLOCK, 2 * BLOCK), NEG, F32)
            for b in range(N_BUCKETS):
                acc = jnp.where(bk == b, rb_ref[h, b], acc)
            out_ref[0, pl.ds(h * BLOCK, BLOCK), :] = acc
            out_ref[1, pl.ds(h * BLOCK, BLOCK), :] = acc
            out_ref[1, pl.ds(h * BLOCK, BLOCK), 0:BLOCK] = jnp.full((BLOCK, BLOCK), NEG, F32)

    return pl.pallas_call(
        body, name="bias_build",
        in_specs=[pl.BlockSpec(memory_space=pltpu.SMEM), VMEM_WHOLE], out_specs=VMEM_WHOLE,
        out_shape=jax.ShapeDtypeStruct((2, N_Q_HEADS * BLOCK, 2 * BLOCK), F32),
    )(rel_bias_t, bucket)


def _head_softmax(s_ref, bias_ref, sink_ref, h):
    rows = pl.ds(pl.multiple_of(h * BLOCK, BLOCK), BLOCK)
    s = s_ref[rows, :] * (HEAD_DIM ** -0.5) + bias_ref[rows, :]
    sink = sink_ref[h]
    m = jnp.maximum(jnp.max(s, axis=-1, keepdims=True), sink)
    p = jnp.exp(s - m)
    e_sink = jnp.exp(sink - m)
    inv = 1.0 / (jnp.sum(p, axis=-1, keepdims=True) + e_sink)
    return rows, p * inv, e_sink * inv


ATTN_STEP_BLOCKS = 4
BAND = (N_Q_HEADS * BLOCK, 2 * BLOCK)


def _attn_specs():
    nb = ATTN_STEP_BLOCKS
    stacked = pl.BlockSpec((N_Q_HEADS, nb * BLOCK, 128), lambda i: (0, i, 0))
    kv = [pl.BlockSpec((BLOCK, 128), lambda i: (jnp.maximum(nb * i - 1, 0), 0)), pl.BlockSpec((nb * BLOCK, 128), lambda i: (i, 0))]
    consts = [_full((2,) + BAND), pl.BlockSpec(memory_space=pltpu.SMEM)]
    return stacked, kv, consts


def _step_blocks(i, kp_ref, kc_ref, vp_ref, vc_ref, bias_ref):
    blocks = []
    for b in range(ATTN_STEP_BLOCKS):
        if b == 0:
            k2 = jnp.concatenate([kp_ref[...], kc_ref[pl.ds(0, BLOCK), :]], axis=0)
            v2 = jnp.concatenate([vp_ref[...], vc_ref[pl.ds(0, BLOCK), :]], axis=0)
            bias = bias_ref.at[jnp.where(i == 0, 1, 0)]
        else:
            k2, v2, bias = kc_ref[pl.ds((b - 1) * BLOCK, 2 * BLOCK), :], vc_ref[pl.ds((b - 1) * BLOCK, 2 * BLOCK), :], bias_ref.at[0]
        blocks.append((pl.ds(b * BLOCK, BLOCK), k2, v2, bias))
    return blocks


def _head_lane_mask():
    rows = lax.broadcasted_iota(jnp.int32, (N_Q_HEADS * BLOCK, 128), 0)
    lanes = lax.broadcasted_iota(jnp.int32, (N_Q_HEADS * BLOCK, 128), 1)
    return (rows < 4 * BLOCK) == (lanes < 64)


def _attn_fwd(qst, kn, vb, bias_st, sinks):
    s_len = kn.shape[0]

    def body(q_ref, kp_ref, kc_ref, vp_ref, vc_ref, bias_ref, sink_ref, o_ref, s_ref, p_ref):
        for b, (rows, k2, v2, bias) in enumerate(_step_blocks(pl.program_id(0), kp_ref, kc_ref, vp_ref, vc_ref, bias_ref)):
            s_b, p_b = s_ref.at[b], p_ref.at[b]
            s_b[...] = _dot(q_ref[:, rows, :].reshape(N_Q_HEADS * BLOCK, 128), k2, 1, 1)

            def head(h, carry):
                head_rows, probs, _ = _head_softmax(s_b, bias, sink_ref, h)
                p_b[head_rows, :] = probs.astype(BF16)
                return carry

            lax.fori_loop(0, N_Q_HEADS, head, 0, unroll=True)
            o = jnp.where(_head_lane_mask(), _dot(p_b[...], v2, 1, 0), 0.0)
            o_ref[:, rows, :] = o.astype(BF16).reshape(N_Q_HEADS, BLOCK, 128)

    stacked, kv, consts = _attn_specs()
    return pl.pallas_call(
        body, name="attn_fwd", grid=(s_len // (ATTN_STEP_BLOCKS * BLOCK),),
        in_specs=[stacked] + kv + kv + consts, out_specs=stacked,
        out_shape=jax.ShapeDtypeStruct((N_Q_HEADS, s_len, 128), BF16),
        scratch_shapes=[pltpu.VMEM((ATTN_STEP_BLOCKS,) + BAND, F32), pltpu.VMEM((ATTN_STEP_BLOCKS,) + BAND, BF16)],
        compiler_params=_params(),
    )(qst, kn, kn, vb, vb, bias_st, sinks)


def _mix_out(u, ost, x2, wts, wpool, pool_scale, g_ffn):
    s_len = x2.shape[0]
    t = 512
    n = t + 16

    def body(u_ref, o_ref, x_ref, sl_ref, lo_ref, me_ref, wp_ref, sc_ref, g_ref, pooled_ref, mix_ref, h1_ref, hn_ref,
             w_ref, ext_ref, st_ref, sems):
        i = pl.program_id(0)

        @pl.when(i == 0)
        def _():
            _load_rows((sl_ref, lo_ref, me_ref), "out", w_ref, sems)
            ext_ref[...] = jnp.zeros_like(ext_ref)
            st_ref[...] = jnp.zeros_like(st_ref)

        u_tile = u_ref[...]
        ext_ref[pl.ds(POOL_HALO, t), :] = u_tile
        st_ref[pl.ds(8, n), :] = ext_ref[pl.ds(8, n), :] + ext_ref[pl.ds(7, n), :]
        st_ref[pl.ds(8, n), 128:] = st_ref[pl.ds(8, n), 128:] + st_ref[pl.ds(6, n), 128:]
        st_ref[pl.ds(8, n), 256:] = st_ref[pl.ds(8, n), 256:] + st_ref[pl.ds(4, n), 256:]
        st_ref[pl.ds(8, n), 384:] = st_ref[pl.ds(8, n), 384:] + st_ref[pl.ds(0, n), 384:]
        ext_ref[pl.ds(0, POOL_HALO), :] = ext_ref[pl.ds(t, POOL_HALO), :]
        pooled = (st_ref[pl.ds(POOL_HALO, t), :] / _pool_counts(i, t) - u_tile).astype(BF16)
        pooled_ref[...] = pooled
        for g in range(4):
            cols = slice(128 * g, 128 * g + 128)
            y = _dot(pooled[:, cols], wp_ref[g], 1, 0) * sc_ref[:, cols]
            mix_ref[:, ATTN_WIDTH + 128 * g:ATTN_WIDTH + 128 * g + 128] = y.astype(BF16)
        lo = lax.broadcasted_iota(jnp.int32, (t, 128), 1) < 64
        for p in range(4):
            a = _from_stacked(o_ref[2 * p].astype(F32), o_ref[2 * p + 1].astype(F32), p // 2, lo)
            mix_ref[:, 128 * p:128 * p + 128] = a.astype(BF16)
        h1 = x_ref[...] + _dot(mix_ref[...], w_ref[...], 1, 0)
        h1_ref[...] = h1
        hn_ref[...] = _rms_fwd(h1, g_ref[...]).astype(BF16)

    row = lambda w: pl.BlockSpec((t, w), lambda i: (i, 0))
    return pl.pallas_call(
        body, name="mix_out", grid=(s_len // t,),
        in_specs=[row(POOL_WIDTH), pl.BlockSpec((N_Q_HEADS, t, 128), lambda i: (0, i, 0)), row(D_MODEL)] + W_SPECS
        + [_full((4, 128, 128)), _full((1, POOL_WIDTH)), _full((1, D_MODEL))],
        out_specs=[row(POOL_WIDTH), row(D_MODEL), row(D_MODEL), row(D_MODEL)],
        out_shape=[jax.ShapeDtypeStruct((s_len, POOL_WIDTH), BF16), jax.ShapeDtypeStruct((s_len, D_MODEL), BF16),
                   jax.ShapeDtypeStruct((s_len, D_MODEL), F32), jax.ShapeDtypeStruct((s_len, D_MODEL), BF16)],
        scratch_shapes=[pltpu.VMEM((D_MODEL, D_MODEL), BF16), pltpu.VMEM((t + POOL_HALO, POOL_WIDTH), F32),
                        pltpu.VMEM((t + POOL_HALO, POOL_WIDTH), F32), pltpu.SemaphoreType.DMA((N_CHIPS,))],
        compiler_params=_params(),
    )(u, ost, x2, *wts, wpool, pool_scale, g_ffn)


def _ffn_ple(hn2, h1, p2, tgt, wts, g_ffn, g_ple):
    s_len = h1.shape[0]
    t = 256
    n_tiles = s_len // t

    def body(hn_ref, h1_ref, p_ref, tgt_ref, sl_ref, lo_ref, me_ref, gf_ref, gp_ref,
             loss_ref, dgate_ref, dup_ref, act_ref, dh2b_ref, hn3_ref, dgl_ref, dwp_ref, dh1_ref, dgf_ref, dgp_ref,
             wg_ref, wu_ref, wd_ref, wl_ref, wp_ref, packed_ref, gate_s, up_s, loss_acc, dwp_acc, sems):
        i = pl.program_id(0)

        @pl.when(i == 0)
        def _():
            w_refs = (sl_ref, lo_ref, me_ref)
            _load_rows(w_refs, "gateT", wg_ref, sems)
            _load_rows(w_refs, "upT", wu_ref, sems)
            _load_rows(w_refs, "down", wd_ref, sems)
            _load_rows(w_refs, "plg", wl_ref, sems)
            _load_rows(w_refs, "plp", packed_ref, sems)
            for j in range(N_CHIPS):
                for q in range(4):
                    wp_ref[pl.ds(64 * q, 64), 256 * j:256 * j + 256] = packed_ref[pl.ds(64 * j, 64), 256 * q:256 * q + 256]
            loss_acc[...] = jnp.zeros_like(loss_acc)
            dgf_ref[...] = jnp.zeros_like(dgf_ref)
            dgp_ref[...] = jnp.zeros_like(dgp_ref)

        hn = hn_ref[...]
        h1v = h1_ref[...]
        h2 = h1v
        for ch in range(N_FF_CHUNKS):
            rows = pl.ds(ch * FF_CHUNK, FF_CHUNK)
            gate = _dot(hn, wg_ref[rows, :], 1, 1)
            up = _dot(hn, wu_ref[rows, :], 1, 1)
            gate_s[ch] = gate
            up_s[ch] = up
            act = (gate * _sigmoid(gate) * up).astype(BF16)
            act_ref[ch] = act
            h2 = h2 + _dot(act, wd_ref[rows, :], 1, 0)
        gp = gp_ref[...]
        hn3 = _rms_fwd(h2, gp).astype(BF16)
        hn3_ref[...] = hn3
        gate2 = _sigmoid(_dot(hn3, wl_ref[...], 1, 0))
        p_tile = p_ref[...].astype(BF16)
        pp = _dot(p_tile, wp_ref[...], 1, 0)
        err = h2 + gate2 * pp - tgt_ref[...]
        loss_acc[...] += jnp.sum(err * err, axis=0, keepdims=True)
        dy = err * (1.0 / D_MODEL)
        _accumulate_tn(dwp_acc, p_tile, (dy * gate2).astype(BF16), i == 0)
        dgl = (dy * pp * gate2 * (1.0 - gate2)).astype(BF16)
        dgl_ref[...] = dgl
        dx3, dg3 = _rms_bwd(h2, gp, _dot(dgl, wl_ref[...], 1, 1))
        dh2 = dy + dx3
        dgp_ref[...] += dg3
        dh2b = dh2.astype(BF16)
        dh2b_ref[...] = dh2b
        dhn = jnp.zeros((t, D_MODEL), F32)
        for ch in range(N_FF_CHUNKS):
            rows = pl.ds(ch * FF_CHUNK, FF_CHUNK)
            dact = _dot(dh2b, wd_ref[rows, :], 1, 1)
            gate_v = gate_s[ch]
            up_v = up_s[ch]
            sg = _sigmoid(gate_v)
            dup = (dact * (gate_v * sg)).astype(BF16)
            dgate = (dact * up_v * (sg * (1.0 + gate_v * (1.0 - sg)))).astype(BF16)
            dup_ref[ch] = dup
            dgate_ref[ch] = dgate
            dhn = dhn + _dot(dgate, wg_ref[rows, :], 1, 0) + _dot(dup, wu_ref[rows, :], 1, 0)
        dx, dg = _rms_bwd(h1v, gf_ref[...], dhn)
        dh1_ref[...] = dh2 + dx
        dgf_ref[...] += dg

        @pl.when(i == n_tiles - 1)
        def _():
            total = jnp.sum(loss_acc[...], axis=-1, keepdims=True) * (0.5 / D_MODEL)
            loss_ref[...] = jnp.broadcast_to(total, loss_ref.shape)
            dwp_ref[...] = dwp_acc[...].astype(BF16)

    row = lambda w: pl.BlockSpec((t, w), lambda i: (i, 0))
    chunked = pl.BlockSpec((N_FF_CHUNKS, t, FF_CHUNK), lambda i: (0, i, 0))
    vec = _full((1, D_MODEL))
    act_shape = jax.ShapeDtypeStruct((N_FF_CHUNKS, s_len, FF_CHUNK), BF16)
    tok = lambda dtype: jax.ShapeDtypeStruct((s_len, D_MODEL), dtype)
    return pl.pallas_call(
        body, name="ffn_ple", grid=(n_tiles,),
        in_specs=[row(D_MODEL), row(D_MODEL), row(PLE_DIM), row(D_MODEL)] + W_SPECS + [vec, vec],
        out_specs=[_full((1, 128)), chunked, chunked, chunked] + [row(D_MODEL)] * 3 + [_full((PLE_DIM, D_MODEL)), row(D_MODEL),
                                                                                       vec, vec],
        out_shape=[jax.ShapeDtypeStruct((1, 128), F32), act_shape, act_shape, act_shape, tok(BF16), tok(BF16), tok(BF16),
                   jax.ShapeDtypeStruct((PLE_DIM, D_MODEL), BF16), tok(F32), jax.ShapeDtypeStruct((1, D_MODEL), F32),
                   jax.ShapeDtypeStruct((1, D_MODEL), F32)],
        scratch_shapes=[pltpu.VMEM((D_FF, D_MODEL), BF16)] * 3
        + [pltpu.VMEM((D_MODEL, D_MODEL), BF16), pltpu.VMEM((PLE_DIM, D_MODEL), BF16), pltpu.VMEM((PLE_DIM, D_MODEL), BF16),
           pltpu.VMEM((N_FF_CHUNKS, t, FF_CHUNK), F32), pltpu.VMEM((N_FF_CHUNKS, t, FF_CHUNK), F32), pltpu.VMEM((1, D_MODEL), F32),
           pltpu.VMEM((PLE_DIM, D_MODEL), F32), pltpu.SemaphoreType.DMA((N_CHIPS,))],
        compiler_params=_params(VMEM_LIMIT_BIG),
    )(hn2, h1, p2, tgt, *wts, g_ffn, g_ple)


def _accumulate_tn(acc_ref, a, b, first):
    @pl.when(first)
    def _():
        acc_ref[...] = _dot(a, b, 0, 0)

    @pl.when(jnp.logical_not(first))
    def _():
        acc_ref[...] += _dot(a, b, 0, 0)


def _flush_chunks(acc_ref, stage_ref, slab_ref, name, sems):
    stage_ref[...] = acc_ref[...].astype(BF16)
    off, rows = SLAB[name]
    copies = [pltpu.make_async_copy(stage_ref.at[pl.ds(j * rows, rows), :], slab_ref.at[j, pl.ds(off, rows), :], sems.at[j])
              for j in range(N_CHIPS)]
    for cp in copies:
        cp.start()
    for cp in copies:
        cp.wait()


def _mix_out_bwd(dh1, wts, pooled, wpool, pool_scale, mix, after):
    s_len = dh1.shape[0]
    t = 512
    n = t + 16
    n_tiles = s_len // t
    early_rows = GATHER_PARTS[0][1]

    def body(dh1_ref, sl_ref, lo_ref, me_ref, pooled_ref, wp_ref, sc_ref, mix_ref, after_ref, dost_ref, du_ref, dwp_ref,
             dsc_ref, slab_ref, w_ref, ext_ref, st_ref, acc_ref, stage_ref, sems):
        del after_ref
        i = pl.program_id(0)

        @pl.when(i == 0)
        def _():
            _load_rows((sl_ref, lo_ref, me_ref), "out", w_ref, sems)
            ext_ref[...] = jnp.zeros_like(ext_ref)
            st_ref[...] = jnp.zeros_like(st_ref)
            dsc_ref[...] = jnp.zeros_like(dsc_ref)
            dwp_ref[...] = jnp.zeros_like(dwp_ref)

        dh1b = dh1_ref[...].astype(BF16)
        _accumulate_tn(acc_ref, mix_ref[...], dh1b, i == 0)

        @pl.when(i == n_tiles - 1)
        def _():
            _flush_chunks(acc_ref, stage_ref, slab_ref, "out", sems)

        dmix = _dot(dh1b, w_ref[...], 1, 1)
        lo = lax.broadcasted_iota(jnp.int32, (t, 128), 1) < 64
        for p in range(4):
            even, odd = _to_stacked(dmix[:, 128 * p:128 * p + 128], p // 2, lo)
            dost_ref[2 * p] = even.astype(BF16)
            dost_ref[2 * p + 1] = odd.astype(BF16)
        pooled_v = pooled_ref[...]
        counts = _pool_counts(n_tiles - 1 - i, t)
        for g in range(4):
            cols = slice(128 * g, 128 * g + 128)
            dm = dmix[:, ATTN_WIDTH + 128 * g:ATTN_WIDTH + 128 * g + 128]
            ypre = _dot(pooled_v[:, cols], wp_ref[g], 1, 0)
            dsc_ref[:, cols] += jnp.sum(ypre * dm, axis=0, keepdims=True)
            dyp = (dm * sc_ref[:, cols]).astype(BF16)
            dwp_ref[g] += _dot(pooled_v[:, cols], dyp, 0, 0)
            dpooled = _dot(dyp, wp_ref[g], 1, 1)
            du_ref[:, cols] = -dpooled
            ext_ref[pl.ds(0, t), cols] = dpooled / counts[:, cols]
        st_ref[pl.ds(0, n), :] = ext_ref[pl.ds(0, n), :] + ext_ref[pl.ds(1, n), :]
        st_ref[pl.ds(0, n), 128:] = st_ref[pl.ds(0, n), 128:] + st_ref[pl.ds(2, n), 128:]
        st_ref[pl.ds(0, n), 256:] = st_ref[pl.ds(0, n), 256:] + st_ref[pl.ds(4, n), 256:]
        st_ref[pl.ds(0, n), 384:] = st_ref[pl.ds(0, n), 384:] + st_ref[pl.ds(8, n), 384:]
        ext_ref[pl.ds(t, POOL_HALO), :] = ext_ref[pl.ds(0, POOL_HALO), :]
        du_ref[...] += st_ref[pl.ds(0, t), :]

    rev = lambda w: pl.BlockSpec((t, w), lambda i: (n_tiles - 1 - i, 0))
    return pl.pallas_call(
        body, name="mix_out_bwd", grid=(n_tiles,),
        in_specs=[rev(D_MODEL)] + W_SPECS + [rev(POOL_WIDTH), _full((4, 128, 128)), _full((1, POOL_WIDTH)), rev(D_MODEL), ANY],
        out_specs=[pl.BlockSpec((N_Q_HEADS, t, 128), lambda i: (0, n_tiles - 1 - i, 0)), rev(POOL_WIDTH),
                   _full((4, 128, 128)), _full((1, POOL_WIDTH)), ANY],
        out_shape=[jax.ShapeDtypeStruct((N_Q_HEADS, s_len, 128), BF16), jax.ShapeDtypeStruct((s_len, POOL_WIDTH), F32),
                   jax.ShapeDtypeStruct((4, 128, 128), F32), jax.ShapeDtypeStruct((1, POOL_WIDTH), F32),
                   jax.ShapeDtypeStruct((N_CHIPS, early_rows, D_MODEL), BF16)],
        scratch_shapes=[pltpu.VMEM((D_MODEL, D_MODEL), BF16), pltpu.VMEM((t + POOL_HALO, POOL_WIDTH), F32),
                        pltpu.VMEM((t + POOL_HALO, POOL_WIDTH), F32), pltpu.VMEM((D_MODEL, D_MODEL), F32),
                        pltpu.VMEM((D_MODEL, D_MODEL), BF16), pltpu.SemaphoreType.DMA((N_CHIPS,))],
        compiler_params=_params(),
    )(dh1, *wts, pooled, wpool, pool_scale, mix, after)


def _attn_bwd(qst, kn, vb, dost, bias_st, sinks, after):
    s_len = kn.shape[0]

    def body(q_ref, kp_ref, kc_ref, vp_ref, vc_ref, do_ref, bias_ref, sink_ref, after_ref, dq_ref, dk_ref, dv_ref, dbias_ref,
             dsink_ref, s_ref, dp_ref, p_ref, dl_ref):
        del after_ref
        i = pl.program_id(0)

        @pl.when(i == 0)
        def _():
            dk_ref[...] = jnp.zeros_like(dk_ref)
            dv_ref[...] = jnp.zeros_like(dv_ref)
            dbias_ref[...] = jnp.zeros_like(dbias_ref)
            dsink_ref[...] = jnp.zeros_like(dsink_ref)

        for b, (rows, k2, v2, bias) in enumerate(_step_blocks(i, kp_ref, kc_ref, vp_ref, vc_ref, bias_ref)):
            s_b, dp_b, p_b, dl_b = s_ref.at[b], dp_ref.at[b], p_ref.at[b], dl_ref.at[b]
            q = q_ref[:, rows, :].reshape(N_Q_HEADS * BLOCK, 128)
            do = do_ref[:, rows, :].reshape(N_Q_HEADS * BLOCK, 128)
            s_b[...] = _dot(q, k2, 1, 1)
            dp_b[...] = _dot(do, v2, 1, 1)

            def head(h, carry):
                head_rows, probs, p_sink = _head_softmax(s_b, bias, sink_ref, h)
                dp = dp_b[head_rows, :]
                dsum = jnp.sum(probs * dp, axis=-1, keepdims=True)
                dlog = probs * (dp - dsum)
                dsink_ref[head_rows, :] -= p_sink * dsum
                dbias_ref[head_rows, :] += dlog
                p_b[head_rows, :] = probs.astype(BF16)
                dl_b[head_rows, :] = (dlog * (HEAD_DIM ** -0.5)).astype(BF16)
                return carry

            lax.fori_loop(0, N_Q_HEADS, head, 0, unroll=True)
            dlog_s = dl_b[...]
            dq_ref[:, rows, :] = jnp.where(_head_lane_mask(), _dot(dlog_s, k2, 1, 0), 0.0).reshape(N_Q_HEADS, BLOCK, 128)
            dk2 = _dot(dlog_s, q, 0, 0)
            dv2 = _dot(p_b[...], do, 0, 0)
            block = ATTN_STEP_BLOCKS * i + b
            prev_rows = pl.ds(pl.multiple_of(jnp.maximum(block - 1, 0) * BLOCK, BLOCK), BLOCK)
            cur_rows = pl.ds(pl.multiple_of(block * BLOCK, BLOCK), BLOCK)
            dk_ref[prev_rows, :] += dk2[:BLOCK]
            dk_ref[cur_rows, :] += dk2[BLOCK:]
            dv_ref[prev_rows, :] += dv2[:BLOCK]
            dv_ref[cur_rows, :] += dv2[BLOCK:]

    stacked, kv, consts = _attn_specs()
    per_step = (ATTN_STEP_BLOCKS,) + BAND
    return pl.pallas_call(
        body, name="attn_bwd", grid=(s_len // (ATTN_STEP_BLOCKS * BLOCK),),
        in_specs=[stacked] + kv + kv + [stacked] + consts + [ANY],
        out_specs=[stacked, _full((s_len, 128)), _full((s_len, 128)), _full(BAND), _full((N_Q_HEADS * BLOCK, 1))],
        out_shape=[jax.ShapeDtypeStruct((N_Q_HEADS, s_len, 128), F32), jax.ShapeDtypeStruct((s_len, 128), F32),
                   jax.ShapeDtypeStruct((s_len, 128), F32), jax.ShapeDtypeStruct(BAND, F32),
                   jax.ShapeDtypeStruct((N_Q_HEADS * BLOCK, 1), F32)],
        scratch_shapes=[pltpu.VMEM(per_step, F32), pltpu.VMEM(per_step, F32), pltpu.VMEM(per_step, BF16),
                        pltpu.VMEM(per_step, BF16)],
        compiler_params=_params(),
    )(qst, kn, kn, vb, vb, dost, bias_st, sinks, after)


def _small_pack(dg_attn, dg_ffn, dg_ple, dscale, dgq, dgk, dbias, dsink_rows, bucket, loss_v):
    def body(ga_ref, gf_ref, gp_ref, sc_ref, gq_ref, gk_ref, db_ref, ds_ref, bucket_ref, loss_ref, out_ref):
        out_ref[...] = jnp.zeros((SMALL_ROWS, 128), F32)
        for name, ref, n in (("g_attn", ga_ref, 8), ("g_ffn", gf_ref, 8), ("g_ple", gp_ref, 8), ("pool_scale", sc_ref, 4)):
            for k in range(n):
                out_ref[pl.ds(SMALL[name] + k, 1), :] = ref[:, 128 * k:128 * k + 128]
        for name, ref in (("g_q", gq_ref), ("g_k", gk_ref)):
            both = ref[...]
            out_ref[pl.ds(SMALL[name], 1), :] = both + pltpu.roll(both, 64, axis=1)
        out_ref[pl.ds(SMALL["loss"], 1), :] = loss_ref[...]
        bk = bucket_ref[...]
        rows = lax.broadcasted_iota(jnp.int32, (N_Q_HEADS, 128), 0)
        lanes = lax.broadcasted_iota(jnp.int32, (N_Q_HEADS, 128), 1)
        lane1 = lax.broadcasted_iota(jnp.int32, (1, 128), 1)
        rb = jnp.zeros((N_Q_HEADS, 128), F32)
        sk = jnp.zeros((1, 128), F32)
        for h in range(N_Q_HEADS):
            band = db_ref[pl.ds(h * BLOCK, BLOCK), :]
            for b in range(N_BUCKETS):
                rb = jnp.where((rows == h) & (lanes == b), jnp.sum(jnp.where(bk == b, band, 0.0)), rb)
            sk = jnp.where(lane1 == h, jnp.sum(ds_ref[pl.ds(h * BLOCK, BLOCK), :]), sk)
        out_ref[pl.ds(SMALL["rel_bias"], N_Q_HEADS), :] = rb
        out_ref[pl.ds(SMALL["sinks"], 1), :] = sk

    return pl.pallas_call(
        body, name="small_pack", in_specs=[VMEM_WHOLE] * 10, out_specs=VMEM_WHOLE,
        out_shape=jax.ShapeDtypeStruct((SMALL_ROWS, 128), F32),
    )(dg_attn, dg_ffn, dg_ple, dscale, dgq, dgk, dbias, dsink_rows, bucket, loss_v)


def _attn_in_bwd(dqst, zqk, dk, dv, du, x2, dh1, hn1, slab, wts, g_attn, gq, gk):
    s_len = x2.shape[0]
    t = 512
    n_tiles = s_len // t

    def body(dq_ref, zqk_ref, dk_ref, dv_ref, du_ref, x_ref, dh1_ref, hn_ref, slab_in_ref, sl_ref, lo_ref, me_ref, g_ref,
             gq_ref, gk_ref, dx_ref, dg_ref, dgq_ref, dgk_ref, slab_ref, w_ref, dz_ref, acc_ref, stage_ref, sems):
        del slab_in_ref
        i = pl.program_id(0)

        @pl.when(i == 0)
        def _():
            _load_rows((sl_ref, lo_ref, me_ref), "inT", w_ref, sems)
            dg_ref[...] = jnp.zeros_like(dg_ref)
            dgq_ref[...] = jnp.zeros_like(dgq_ref)
            dgk_ref[...] = jnp.zeros_like(dgk_ref)

        lo = lax.broadcasted_iota(jnp.int32, (t, 128), 1) < 64
        for p in range(4):
            dqn = _from_stacked(dq_ref[2 * p], dq_ref[2 * p + 1], p // 2, lo)
            dq_raw, dgq = _pair_norm_bwd(zqk_ref[:, 128 * p:128 * p + 128], gq_ref[...], dqn)
            dz_ref[:, 128 * p:128 * p + 128] = dq_raw.astype(BF16)
            dgq_ref[...] += dgq
        dk_raw, dgk = _pair_norm_bwd(zqk_ref[:, 512:640], gk_ref[...], dk_ref[...])
        dgk_ref[...] += dgk
        dz_ref[:, 512:640] = dk_raw.astype(BF16)
        dz_ref[:, 640:768] = dv_ref[...].astype(BF16)
        dz_ref[:, 768:] = du_ref[...].astype(BF16)
        dz = dz_ref[...]
        _accumulate_tn(acc_ref, dz, hn_ref[...], i == 0)
        dx, dg = _rms_bwd(x_ref[...], g_ref[...], _dot(dz, w_ref[...], 1, 0))
        dx_ref[...] = dh1_ref[...] + dx
        dg_ref[...] += dg

        @pl.when(i == n_tiles - 1)
        def _():
            _flush_chunks(acc_ref, stage_ref, slab_ref, "inT", sems)

    row = lambda w: pl.BlockSpec((t, w), lambda i: (i, 0))
    return pl.pallas_call(
        body, name="attn_in_bwd", grid=(n_tiles,),
        in_specs=[pl.BlockSpec((N_Q_HEADS, t, 128), lambda i: (0, i, 0)), row(640), row(128), row(128), row(POOL_WIDTH),
                  row(D_MODEL), row(D_MODEL), row(D_MODEL), ANY] + W_SPECS + [_full((1, D_MODEL)), _full((1, 128)),
                                                                              _full((1, 128))],
        out_specs=[row(D_MODEL), _full((1, D_MODEL)), _full((1, 128)), _full((1, 128)), ANY],
        out_shape=[jax.ShapeDtypeStruct((s_len, D_MODEL), F32), jax.ShapeDtypeStruct((1, D_MODEL), F32),
                   jax.ShapeDtypeStruct((1, 128), F32), jax.ShapeDtypeStruct((1, 128), F32),
                   jax.ShapeDtypeStruct(slab.shape, BF16)],
        input_output_aliases={8: 4},
        scratch_shapes=[pltpu.VMEM((IN_WIDTH, D_MODEL), BF16), pltpu.VMEM((t, IN_WIDTH), BF16),
                        pltpu.VMEM((IN_WIDTH, D_MODEL), F32), pltpu.VMEM((IN_WIDTH, D_MODEL), BF16),
                        pltpu.SemaphoreType.DMA((N_CHIPS,))],
        compiler_params=_params(),
    )(dqst, zqk, dk, dv, du, x2, dh1, hn1, slab, *wts, g_attn, gq, gk)


def _dw(lefts, b, name, slab, slab_rows, row_offs):
    a0, n_a = lefts[0], len(lefts)
    assert b.shape[1] == D_MODEL
    if a0.ndim == 3:
        n_chunks, s_len, tm = a0.shape
        m = n_chunks * tm
    else:
        s_len, tm = a0.shape
        m = tm
    tk = 2048 if n_a * tm <= 1408 else 1024
    if a0.ndim == 3:
        a_spec = pl.BlockSpec((None, tk, tm), lambda i, k: (i, k, 0))
    else:
        a_spec = pl.BlockSpec((tk, tm), lambda i, k: (k, i))
    n_steps, n_tiles = s_len // tk, m // tm
    chunk = m // N_CHIPS
    per_tile = tm // chunk

    def body(*refs):
        a_refs, b_ref = refs[:n_a], refs[n_a]
        o_ref, acc_ref, stage_ref, sems = refs[-4:]
        i, k = pl.program_id(0), pl.program_id(1)
        b_tile = b_ref[...].astype(BF16)
        for w, a_ref in enumerate(a_refs):
            _accumulate_tn(acc_ref.at[w], a_ref[...].astype(BF16), b_tile, k == 0)

        def out_copies(tile, slot):
            return [pltpu.make_async_copy(stage_ref.at[slot, w, pl.ds(jj * chunk, chunk), :],
                                          o_ref.at[tile * per_tile + jj, pl.ds(row_offs[w], chunk), :], sems.at[slot, w, jj])
                    for w in range(n_a) for jj in range(per_tile)]

        @pl.when(k == n_steps - 1)
        def _():
            slot = i % 2

            @pl.when(i >= 2)
            def _():
                for cp in out_copies(i - 2, slot):
                    cp.wait()

            stage_ref[slot] = acc_ref[...].astype(BF16)
            for cp in out_copies(i, slot):
                cp.start()

            @pl.when(i == n_tiles - 1)
            def _():
                for cp in out_copies(i, slot):
                    cp.wait()
                if n_tiles > 1:
                    for cp in out_copies(i - 1, 1 - slot):
                        cp.wait()

    in_specs = [a_spec] * n_a + [pl.BlockSpec((tk, D_MODEL), lambda i, k: (k, 0))]
    operands, aliases = [*lefts, b], {}
    if slab is not None:
        in_specs.append(ANY)
        operands.append(slab)
        aliases = {n_a + 1: 0}
    return pl.pallas_call(
        body, name=name, grid=(n_tiles, n_steps), in_specs=in_specs, out_specs=ANY,
        out_shape=jax.ShapeDtypeStruct((N_CHIPS, slab_rows, D_MODEL), BF16), input_output_aliases=aliases,
        scratch_shapes=[pltpu.VMEM((n_a, tm, D_MODEL), F32), pltpu.VMEM((2, n_a, tm, D_MODEL), BF16),
                        pltpu.SemaphoreType.DMA((2, n_a, per_tile))],
        compiler_params=_params(VMEM_LIMIT_BIG, n_axes=2),
    )(*operands)


def _position():
    x, y, c = lax.axis_index("x"), lax.axis_index("y"), lax.axis_index("c")
    other_chips = [(1 - x, y), (x, 1 - y), (1 - x, 1 - y)]
    return x, y, c, other_chips


def _ag_weights(local_slab, row0, n_rows, name, collective_id):
    half = n_rows // 2
    quarter = half // 2
    assert quarter % 16 == 0

    def body(l_ref, g_ref, send, recv):
        x, y, c, chips = _position()
        me, (via_x, via_y, diagonal) = 2 * x + y, [2 * chip[0] + chip[1] for chip in chips]
        here, sibling, x_nbr, y_nbr = (x, y, c), (x, y, 1 - c), (1 - x, y, c), (x, 1 - y, c)
        peers = [sibling, x_nbr, y_nbr]
        barrier = pltpu.get_barrier_semaphore()
        for peer in peers:
            pl.semaphore_signal(barrier, inc=1, device_id=peer, device_id_type=MESH)
        pl.semaphore_wait(barrier, len(peers))

        def rows(core, part):
            start, size = (core * half, half) if part is None else (core * half + part * quarter, quarter)
            return pl.ds(pl.multiple_of(start, 16), size)

        def copy(k, chip_idx, where, to, src=None):
            dst = g_ref.at[chip_idx, where, :]
            return pltpu.make_async_remote_copy(src_ref=dst if src is None else src, dst_ref=dst, send_sem=send.at[k],
                                                recv_sem=recv.at[k], device_id=to, device_id_type=MESH)

        own_rows = l_ref.at[pl.ds(pl.multiple_of(row0 + c * half, 16), half), :]
        started = [copy(0, me, rows(c, None), x_nbr, src=own_rows), copy(1, me, rows(c, None), y_nbr, src=own_rows)]
        for cp in started:
            cp.start()
        after_arrival = [
            (copy(0, via_x, rows(c, None), here), [copy(4, via_x, rows(c, None), sibling), copy(3, via_x, rows(c, 1), y_nbr)]),
            (copy(1, via_y, rows(c, None), here), [copy(5, via_y, rows(c, None), sibling), copy(2, via_y, rows(c, 0), x_nbr)]),
            (copy(2, diagonal, rows(c, 0), here), [copy(6, diagonal, rows(c, 0), sibling)]),
            (copy(3, diagonal, rows(c, 1), here), [copy(7, diagonal, rows(c, 1), sibling)]),
        ]
        for arrival, onward in after_arrival:
            arrival.wait_recv()
            for cp in onward:
                cp.start()
            started += onward
        for cp in (copy(4, via_x, rows(1 - c, None), here), copy(5, via_y, rows(1 - c, None), here),
                   copy(6, diagonal, rows(1 - c, 0), here), copy(7, diagonal, rows(1 - c, 1), here)):
            cp.wait_recv()
        for cp in started:
            cp.wait_send()

    return pl.kernel(
        body, out_type=jax.ShapeDtypeStruct((N_CHIPS, n_rows, D_MODEL), BF16),
        mesh=plsc.ScalarSubcoreMesh(axis_name="sequencer", num_cores=1), name=name,
        scratch_types=[pltpu.SemaphoreType.DMA((8,)), pltpu.SemaphoreType.DMA((8,))],
        compiler_params=pltpu.CompilerParams(collective_id=collective_id),
    )(local_slab)


def _comm_call(body, peers_of, out_shape, n_sems, operand, name, collective_id):
    sems = [pltpu.SemaphoreType.DMA((n_sems,)), pltpu.SemaphoreType.DMA((n_sems,))]

    def with_handshake(in_ref, out_ref, send, recv):
        x, y, c, _ = _position()
        peers = peers_of(x, y, c)
        barrier = pltpu.get_barrier_semaphore()
        for peer in peers:
            pl.semaphore_signal(barrier, inc=1, device_id=peer, device_id_type=MESH)
        pl.semaphore_wait(barrier, len(peers))
        body(in_ref, out_ref, send, recv)

    return pl.kernel(with_handshake, out_type=out_shape, mesh=plsc.ScalarSubcoreMesh(axis_name="sequencer", num_cores=1),
                     name=name, scratch_types=sems, compiler_params=pltpu.CompilerParams(collective_id=collective_id))(operand)


def _rs_swap_halves(partial, name, collective_id):
    half = partial.shape[1] // 2

    def body(p_ref, r_ref, send, recv):
        x, y, c, _ = _position()
        theirs = pl.ds(pl.multiple_of((1 - c) * half, 16), half)
        cp = pltpu.make_async_remote_copy(src_ref=p_ref.at[:, theirs, :], dst_ref=r_ref, send_sem=send.at[0],
                                          recv_sem=recv.at[0], device_id=(x, y, 1 - c), device_id_type=MESH)
        cp.start()
        cp.wait()

    return _comm_call(body, lambda x, y, c: [(x, y, 1 - c)], jax.ShapeDtypeStruct((N_CHIPS, half, D_MODEL), BF16), 1,
                      partial, name, collective_id)


def _rs_add_halves(partial, other, core, name, after, small=None):
    half = other.shape[1]
    t = half // 2
    steps = half // t

    def body(core_ref, a_ref, b_ref, after_ref, *rest):
        del after_ref
        o_ref = rest[0] if small is None else rest[1]
        if small is not None:
            small_ref, _, t_ref, t_send, t_recv = rest
            start_tables, finish_tables = _gather_small(small_ref, t_ref, t_send, t_recv)
            j, i = pl.program_id(0), pl.program_id(1)
            pl.when((j == 0) & (i == 0))(start_tables)
        o_ref[...] = (a_ref[...].astype(F32) + b_ref[...].astype(F32)).astype(BF16)
        if small is not None:
            pl.when((j == N_CHIPS - 1) & (i == steps - 1))(finish_tables)

    t_in, t_out, t_scratch = _table_gather_parts(small)
    res = pl.pallas_call(
        body, name=name,
        grid_spec=pltpu.PrefetchScalarGridSpec(
            num_scalar_prefetch=1, grid=(N_CHIPS, steps),
            in_specs=[pl.BlockSpec((1, t, D_MODEL), lambda j, i, core_ref: (j, core_ref[0] * steps + i, 0)),
                      pl.BlockSpec((1, t, D_MODEL), lambda j, i, core_ref: (j, i, 0)), ANY] + t_in,
            out_specs=[pl.BlockSpec((1, t, D_MODEL), lambda j, i, core_ref: (j, i, 0))] + [ANY] * len(t_out),
            scratch_shapes=t_scratch),
        out_shape=[jax.ShapeDtypeStruct((N_CHIPS, half, D_MODEL), BF16)] + t_out,
        compiler_params=_params(n_axes=2),
    )(core, partial, other, after, *([] if small is None else [small]))
    return res[0] if small is None else res


def _rs_exchange_chips(pre, name, collective_id):
    def body(s_ref, r_ref, send, recv):
        x, y, c, chips = _position()

        def copy(k, chunk, to):
            return pltpu.make_async_remote_copy(src_ref=s_ref.at[chunk], dst_ref=r_ref.at[k], send_sem=send.at[k],
                                                recv_sem=recv.at[k], device_id=to, device_id_type=MESH)

        sends = [copy(k, 2 * chip[0] + chip[1], (*chip, c)) for k, chip in enumerate(chips)]
        for cp in sends:
            cp.start()
        for cp in sends:
            cp.wait()

    return _comm_call(body, lambda x, y, c: [(1 - x, y, c), (x, 1 - y, c), (1 - x, 1 - y, c)],
                      jax.ShapeDtypeStruct((3, pre.shape[1], D_MODEL), BF16), 3, pre, name, collective_id)


def _gather_small(s_ref, t_ref, send, recv):
    x, y, c, chips = _position()
    sibling = (x, y, 1 - c)

    def slot(px, py, pc):
        return t_ref.at[4 * px + 2 * py + pc]

    def copy(k, block, to, src=None):
        return pltpu.make_async_remote_copy(src_ref=slot(*block) if src is None else src, dst_ref=slot(*block),
                                            send_sem=send.at[k], recv_sem=recv.at[k], device_id=to, device_id_type=MESH)

    own = pltpu.make_async_copy(s_ref, slot(x, y, c), send.at[7])
    first = [copy(0, (x, y, c), sibling, src=s_ref)]
    first += [copy(1 + k, (x, y, c), (*chip, c), src=s_ref) for k, chip in enumerate(chips)]

    def start():
        own.start()
        for cp in first:
            cp.start()

    def finish():
        passed = []
        for k, chip in enumerate(chips):
            copy(1 + k, (*chip, c), (x, y, c)).wait_recv()
            fwd = copy(4 + k, (*chip, c), sibling)
            fwd.start()
            passed.append(fwd)
        copy(0, sibling, (x, y, c)).wait_recv()
        for k, chip in enumerate(chips):
            copy(4 + k, (*chip, 1 - c), (x, y, c)).wait_recv()
        for cp in first + passed:
            cp.wait_send()
        own.wait()

    return start, finish


def _table_gather_parts(small):
    if small is None:
        return [], [], []
    return [VMEM_WHOLE], [jax.ShapeDtypeStruct((N_DEV, *small.shape), F32)], [pltpu.SemaphoreType.DMA((8,))] * 2


def _rs_sum_chips(pre, received, place, name, after, small=None):
    half = pre.shape[1]
    steps = 4 if half > 512 else 2
    t = half // steps
    assert t % 16 == 0 and t * steps == half

    def body(place_ref, own_ref, r_ref, after_ref, *rest):
        del place_ref, after_ref
        if small is None:
            o_ref, stage, kept_sems, send, recv = rest
        else:
            small_ref, o_ref, t_ref, stage, kept_sems, send, recv, t_send, t_recv = rest
            start_tables, finish_tables = _gather_small(small_ref, t_ref, t_send, t_recv)
            pl.when(pl.program_id(0) == 0)(start_tables)
        i = pl.program_id(0)
        x, y, c, _ = _position()

        def rows(core, step):
            return o_ref.at[pl.ds(pl.multiple_of((core * steps + step) * t, 8), t), :]

        def kept(step):
            return pltpu.make_async_copy(stage.at[step], rows(c, step), kept_sems.at[step])

        def sent(core, step):
            return pltpu.make_async_remote_copy(src_ref=stage.at[step], dst_ref=rows(core, step), send_sem=send.at[step],
                                                recv_sem=recv.at[step], device_id=(x, y, 1 - core), device_id_type=MESH)

        acc = own_ref[0].astype(F32)
        for k in range(3):
            acc = acc + r_ref[k].astype(F32)
        stage[i] = acc
        kept(i).start()
        sent(c, i).start()

        @pl.when(i == steps - 1)
        def _():
            if small is not None:
                finish_tables()
            for step in range(steps):
                kept(step).wait()
                sent(c, step).wait_send()
                sent(1 - c, step).wait_recv()

    t_in, t_out, t_scratch = _table_gather_parts(small)
    res = pl.pallas_call(
        body, name=name,
        grid_spec=pltpu.PrefetchScalarGridSpec(
            num_scalar_prefetch=1, grid=(steps,),
            in_specs=[pl.BlockSpec((1, t, D_MODEL), lambda i, place_ref: (place_ref[0], i, 0)),
                      pl.BlockSpec((3, t, D_MODEL), lambda i, place_ref: (0, i, 0)), ANY] + t_in,
            out_specs=[ANY] * (1 + len(t_out)),
            scratch_shapes=[pltpu.VMEM((steps, t, D_MODEL), F32)] + [pltpu.SemaphoreType.DMA((steps,))] * 3 + t_scratch),
        out_shape=[jax.ShapeDtypeStruct((2 * half, D_MODEL), F32)] + t_out, compiler_params=_params(),
    )(place, pre, received, after, *([] if small is None else [small]))
    return res[0] if small is None else res


def _adam_update(w, g, m, v):
    m_new = ADAM_B1 * m + (1.0 - ADAM_B1) * g
    v_new = ADAM_B2 * v + (1.0 - ADAM_B2) * (g * g)
    m_hat = m_new / (1.0 - ADAM_B1 ** ADAM_STEP)
    v_hat = v_new / (1.0 - ADAM_B2 ** ADAM_STEP)
    return -ADAM_LR * (m_hat / (jnp.sqrt(v_hat) + ADAM_EPS) + ADAM_WD * w), m_new, v_new


def _adamw(w, g_rows, row_off, m, v, name):
    rows, cols = w.shape
    t = rows if rows <= 320 else (rows // 2 if rows % 256 else 256)

    def body(w_ref, g_ref, m_ref, v_ref, go_ref, d_ref, nm_ref, nv_ref):
        g = g_ref[...]
        go_ref[...] = g
        d_ref[...], nm_ref[...], nv_ref[...] = _adam_update(w_ref[...], g, m_ref[...], v_ref[...])

    blk = pl.BlockSpec((t, cols), lambda i: (i, 0))
    assert row_off % 8 == 0 and t % 8 == 0
    g_blk = pl.BlockSpec((pl.Element(t), pl.Element(cols)), lambda i: (pl.multiple_of(row_off + i * t, 8), 0))
    shape = jax.ShapeDtypeStruct((rows, cols), F32)
    return pl.pallas_call(
        body, name=name, grid=(rows // t,), in_specs=[blk, g_blk, blk, blk], out_specs=[blk] * 4, out_shape=[shape] * 4,
        compiler_params=_params(),
    )(w, g_rows, m, v)


SMALL_PARAMS = [("g_attn", (1, D_MODEL), 8), ("g_q", (1, HEAD_DIM), None), ("g_k", (1, HEAD_DIM), None),
                ("sinks", (1, N_Q_HEADS), None), ("rel_bias", (N_Q_HEADS, N_BUCKETS), None), ("w_pool", (512, 128), None),
                ("pool_scale", (1, POOL_WIDTH), 4), ("g_ffn", (1, D_MODEL), 8), ("g_ple", (1, D_MODEL), 8)]


def _adamw_small(tables, pool_tables, wmv):
    n_par = len(SMALL_PARAMS)

    def body(*refs):
        t_ref, p_ref = refs[:2]
        ins = refs[2:2 + 3 * n_par]
        loss_ref = refs[2 + 3 * n_par]
        outs = refs[3 + 3 * n_par:-1]
        tot_ref = refs[-1]

        def in_device_order(ref):
            total = ref[0]
            for d in range(1, N_DEV):
                total = total + ref[d]
            return total

        tot_ref[...] = in_device_order(t_ref)
        loss_ref[...] = tot_ref[pl.ds(SMALL["loss"], 1), 0:1]
        for i, (name, shape, split) in enumerate(SMALL_PARAMS):
            g_ref, d_ref, nm_ref, nv_ref = outs[4 * i:4 * i + 4]
            row = SMALL.get(name)
            if name == "w_pool":
                g_ref[...] = in_device_order(p_ref)
            elif split:
                for k in range(split):
                    g_ref[:, 128 * k:128 * k + 128] = tot_ref[pl.ds(row + k, 1), :]
            else:
                g_ref[...] = tot_ref[pl.ds(row, shape[0]), 0:shape[1]]
            w_ref, m_ref, v_ref = ins[3 * i:3 * i + 3]
            d_ref[...], nm_ref[...], nv_ref[...] = _adam_update(w_ref[...], g_ref[...], m_ref[...], v_ref[...])

    shapes = [jax.ShapeDtypeStruct((1, 1), F32)]
    for _, shape, _ in SMALL_PARAMS:
        shapes += [jax.ShapeDtypeStruct(shape, F32)] * 4
    flat = [a for triple in wmv for a in triple]
    res = pl.pallas_call(
        body, name="adamw_small", in_specs=[VMEM_WHOLE] * (2 + 3 * n_par), out_specs=[VMEM_WHOLE] * len(shapes),
        out_shape=shapes, scratch_shapes=[pltpu.VMEM((SMALL_ROWS, 128), F32)],
    )(tables, pool_tables, *flat)
    return res[0], [res[1 + 4 * i:5 + 4 * i] for i in range(n_par)]


def _pack_ple_proj(shard):
    return shard.reshape(4, 64, 256).transpose(1, 0, 2).reshape(64, D_MODEL)


class _Reduction:
    def __init__(self, tag, place, ids=(None, None)):
        self.tag, self.place, self.ids = tag, place, ids

    def start(self, partial):
        self.partial = partial
        self.other = _rs_swap_halves(partial, "rs_swap_" + self.tag, self.ids[0])
        return partial

    def middle(self, after, small=None):
        res = _rs_add_halves(self.partial, self.other, self.place[1:], "rs_add_" + self.tag, after, small)
        self.pre, self.tables = (res, None) if small is None else res
        self.received = _rs_exchange_chips(self.pre, "rs_exchange_" + self.tag, self.ids[1])
        return self.pre

    def finish(self, after, small=None):
        return _rs_sum_chips(self.pre, self.received, self.place, "rs_sum_" + self.tag, after, small)


def _local_grads(x2, p2, tgt, wts, g_attn_norm, g_q, g_k, attn_sinks, rel_bias, w_pool, pool_scale, g_ffn_norm, g_ple_norm,
                 reduce_a):
    w_early, w_late = wts
    w_in = w_out = w_early
    bucket = jnp.asarray(_bucket_table())
    gq = jnp.tile(g_q, (1, 2))
    gk = jnp.tile(g_k, (1, 2))
    wpool = w_pool[0].astype(BF16)
    sinks = attn_sinks[0]
    bias_st = _bias_build(rel_bias.T, bucket)

    hn1 = _first_norm(x2, g_attn_norm)
    zqk, u, kn, vb, qst = _attn_in(hn1, gq, gk, w_in)
    ost = _attn_fwd(qst, kn, vb, bias_st, sinks)
    pooled, mix, h1, hn2 = _mix_out(u, ost, x2, w_out, wpool, pool_scale, g_ffn_norm)
    loss_v, dgate, dup, act, dh2, hn3, dgl, dw_plp, dh1, dg_ffn, dg_ple = _ffn_ple(hn2, h1, p2, tgt, w_late, g_ffn_norm,
                                                                                      g_ple_norm)

    late0, late_rows = GATHER_PARTS[1][0], SLAB_ROWS - GATHER_PARTS[1][0]
    partial_a = None
    for names, lefts, right in ((("gateT", "upT"), [dgate, dup], hn2), (("down",), [act], dh2), (("plg",), [hn3], dgl)):
        partial_a = _dw(lefts, right, "dw_" + names[0], partial_a, late_rows, [SLAB[name][0] - late0 for name in names])
    dw_plp = dw_plp.reshape(4, 64, N_CHIPS, 256).transpose(2, 1, 0, 3).reshape(N_CHIPS, 64, D_MODEL)
    partial_a = reduce_a.start(lax.dynamic_update_slice(partial_a, dw_plp, (0, SLAB["plp"][0] - late0, 0)))
    dost, du, dw_pool, dscale, partial_b = _mix_out_bwd(dh1, w_out, pooled, wpool, pool_scale, mix, partial_a)
    pre_a = reduce_a.middle(du, dw_pool.reshape(512, 128))
    dqst, dk, dv, dbias, dsink_rows = _attn_bwd(qst, kn, vb, dost, bias_st, sinks, pre_a)
    dx, dg_attn, dgq, dgk, partial_b = _attn_in_bwd(dqst, zqk, dk, dv, du, x2, dh1, hn1, partial_b, w_in, g_attn_norm, gq, gk)

    small = _small_pack(dg_attn, dg_ffn, dg_ple, dscale, dgq, dgk, dbias, dsink_rows, bucket, loss_v)
    return dx, partial_b, small


def kernel(x, p, w_in, w_out, g_attn_norm, g_q, g_k, attn_sinks, rel_bias, w_pool, pool_scale, g_ffn_norm, w_gate, w_up, w_down, g_ple_norm, w_ple_gate, w_ple_proj, loss_target, m_w_in, m_w_out, m_g_attn_norm, m_g_q, m_g_k, m_attn_sinks, m_rel_bias, m_w_pool, m_pool_scale, m_g_ffn_norm, m_w_gate, m_w_up, m_w_down, m_g_ple_norm, m_w_ple_gate, m_w_ple_proj, v_w_in, v_w_out, v_g_attn_norm, v_g_q, v_g_k, v_attn_sinks, v_rel_bias, v_w_pool, v_pool_scale, v_g_ffn_norm, v_w_gate, v_w_up, v_w_down, v_g_ple_norm, v_w_ple_gate, v_w_ple_proj):
    core = lax.axis_index("c").astype(jnp.int32).reshape(1)
    me = (2 * lax.axis_index("x") + lax.axis_index("y")).astype(jnp.int32).reshape(1)

    local_parts = [jnp.concatenate(pieces, axis=0).astype(BF16) for pieces in (
        [w_in[0].T, w_out[0]], [w_gate[0].T, w_up[0].T, w_down[0], w_ple_gate[0], _pack_ple_proj(w_ple_proj[0])])]
    wts = [(_ag_weights(local, 0, local.shape[0], name, collective_id), local, me)
           for local, name, collective_id in zip(local_parts, ("ag_early", "ag_late"), (1, 2))]

    place = jnp.concatenate([me, core])
    reduce_a = _Reduction("a", place, ids=(3, 4))
    dx, partial_b, small = _local_grads(x[0], p[0, 0], loss_target[0], wts, g_attn_norm, g_q, g_k, attn_sinks, rel_bias,
                                        w_pool, pool_scale, g_ffn_norm, g_ple_norm, reduce_a)
    reduce_b = _Reduction("b", place, ids=(6, 7))
    reduce_b.start(partial_b)
    grads_a, small_all = reduce_a.finish(partial_b, small)
    reduce_b.middle(grads_a)

    late0 = GATHER_PARTS[1][0]

    def rows(name):
        return grads_a, SLAB[name][0] - late0

    plp_rows = grads_a[SLAB["plp"][0] - late0:]
    big = {
        "w_gate": (w_gate, m_w_gate, v_w_gate, rows("gateT"), True),
        "w_up": (w_up, m_w_up, v_w_up, rows("upT"), True),
        "w_down": (w_down, m_w_down, v_w_down, rows("down"), False),
        "w_ple_gate": (w_ple_gate, m_w_ple_gate, v_w_ple_gate, rows("plg"), False),
        "w_ple_proj": (w_ple_proj, m_w_ple_proj, v_w_ple_proj,
                       (plp_rows.reshape(64, 4, 256).transpose(1, 0, 2).reshape(PLE_DIM, PLE_DIM), 0), False),
        "w_out": (w_out, m_w_out, v_w_out, None, False),
        "w_in": (w_in, m_w_in, v_w_in, None, True),
    }
    small_params = {
        "g_attn_norm": (g_attn_norm, m_g_attn_norm, v_g_attn_norm), "g_q": (g_q, m_g_q, v_g_q), "g_k": (g_k, m_g_k, v_g_k),
        "attn_sinks": (attn_sinks, m_attn_sinks, v_attn_sinks), "rel_bias": (rel_bias.T, m_rel_bias.T, v_rel_bias.T),
        "w_pool": tuple(a.reshape(512, 128) for a in (w_pool, m_w_pool, v_w_pool)),
        "pool_scale": (pool_scale, m_pool_scale, v_pool_scale), "g_ffn_norm": (g_ffn_norm, m_g_ffn_norm, v_g_ffn_norm),
        "g_ple_norm": (g_ple_norm, m_g_ple_norm, v_g_ple_norm),
    }

    grads, deltas, new_ms, new_vs = {}, {}, {}, {}
    out = grads_b = None
    for name, (w, m, v, g_src, transposed) in big.items():
        if g_src is None:
            if grads_b is None:
                grads_b = reduce_b.finish(out[-1])
            g_src = (grads_b, SLAB["out" if name == "w_out" else "inT"][0])
        view = (lambda a: a.T) if transposed else (lambda a: a)
        out = _adamw(view(w[0]), *g_src, view(m[0]), view(v[0]), "adamw_" + name)
        grads[name], deltas[name], new_ms[name], new_vs[name] = (view(a)[None] for a in out)

    loss, small_out = _adamw_small(small_all, reduce_a.tables, list(small_params.values()))
    for name, (g2, d, nm, nv) in zip(small_params, small_out):
        restore = {"w_pool": lambda a: a.reshape(w_pool.shape), "rel_bias": lambda a: a.T}.get(name, lambda a: a)
        grads[name], deltas[name], new_ms[name], new_vs[name] = (restore(a) for a in (g2, d, nm, nv))

    order = ["w_in", "w_out", "g_attn_norm", "g_q", "g_k", "attn_sinks", "rel_bias", "w_pool", "pool_scale", "g_ffn_norm",
             "w_gate", "w_up", "w_down", "g_ple_norm", "w_ple_gate", "w_ple_proj"]
    return (loss.reshape(()), dx[None], *[grads[n] for n in order], *[deltas[n] for n in order],
            *[new_ms[n] for n in order], *[new_vs[n] for n in order])
```

```python
import numpy as np
import jax
import jax.numpy as jnp
from jax import lax
from jax.experimental import pallas as pl
from jax.experimental.pallas import tpu as pltpu
from jax.experimental.pallas import tpu_sc as plsc

F32 = jnp.float32
BF16 = jnp.bfloat16
MESH = pl.DeviceIdType.MESH

D_MODEL = 1024
HEAD_DIM = 64
N_Q_HEADS = 8
ATTN_WIDTH = 512
POOL_WIDTH = 512
IN_WIDTH = 1280
D_FF = 2816
PLE_DIM = 256
FF_CHUNK = 1408
N_FF_CHUNKS = D_FF // FF_CHUNK
BLOCK = 128
N_BUCKETS = 32
MAX_DISTANCE = 128
EPS = 1e-6
NEG = -1e30
N_CHIPS = 4
N_DEV = 8

ADAM_LR = 0.001
ADAM_B1 = 0.9
ADAM_B2 = 0.999
ADAM_EPS = 1e-08
ADAM_WD = 0.01
ADAM_STEP = 10

SLAB = {"inT": (0, 320), "out": (320, 256), "gateT": (576, 704), "upT": (1280, 704), "down": (1984, 704),
        "plg": (2688, 256), "plp": (2944, 64)}
SLAB_ROWS = 3008
GATHER_PARTS = ((0, 576), (576, SLAB_ROWS))
POOL_HALO = 24

SMALL = {"g_attn": 0, "g_ffn": 8, "g_ple": 16, "pool_scale": 24, "g_q": 28, "g_k": 29, "sinks": 30, "loss": 31,
         "rel_bias": 32}
SMALL_ROWS = 64

VMEM_LIMIT_BIG = 60 * 1024 * 1024
VMEM_LIMIT = 48 * 1024 * 1024


def _params(vmem=VMEM_LIMIT, n_axes=1):
    return pltpu.CompilerParams(dimension_semantics=("arbitrary",) * n_axes, vmem_limit_bytes=vmem)


def _dot(a, b, ca, cb):
    return lax.dot_general(a, b, (((ca,), (cb,)), ((), ())), preferred_element_type=F32)


def _full(shape):
    return pl.BlockSpec(shape, lambda i: (0,) * len(shape))


ANY = pl.BlockSpec(memory_space=pl.ANY)
VMEM_WHOLE = pl.BlockSpec(memory_space=pltpu.VMEM)


W_SPECS = [ANY, ANY, pl.BlockSpec(memory_space=pltpu.SMEM)]


def _load_rows(w_refs, name, dst_ref, sems):
    slab_ref, local_ref, me_ref = w_refs
    off, rows = SLAB[name]
    slab_off = off - max(start for start, _ in GATHER_PARTS if start <= off)
    me = me_ref[0]
    for phase in ("start", "wait"):
        for j in range(N_CHIPS):
            dst = dst_ref.at[pl.ds(j * rows, rows), :]
            theirs = pltpu.make_async_copy(slab_ref.at[j, pl.ds(slab_off, rows), :], dst, sems.at[j])
            own = pltpu.make_async_copy(local_ref.at[pl.ds(slab_off, rows), :], dst, sems.at[j])

            @pl.when(me == j)
            def _():
                getattr(own, phase)()

            @pl.when(me != j)
            def _():
                getattr(theirs, phase)()


def _rms_fwd(x, g):
    r = lax.rsqrt(jnp.mean(x * x, axis=-1, keepdims=True) + EPS)
    return x * r * g


def _rms_bwd(x, g, dy):
    r = lax.rsqrt(jnp.mean(x * x, axis=-1, keepdims=True) + EPS)
    xn = x * r
    dyg = dy * g
    dx = r * (dyg - xn * jnp.mean(dyg * xn, axis=-1, keepdims=True))
    return dx, jnp.sum(dy * xn, axis=0, keepdims=True)


def _half_sum(v, lo):
    s_lo = jnp.sum(jnp.where(lo, v, 0.0), axis=-1, keepdims=True)
    s_hi = jnp.sum(jnp.where(lo, 0.0, v), axis=-1, keepdims=True)
    return jnp.where(lo, s_lo, s_hi)


def _half_sum_mxu(v):
    upper = lax.broadcasted_iota(jnp.int32, (128, 128), 0) < 64
    left = lax.broadcasted_iota(jnp.int32, (128, 128), 1) < 64
    ones = jnp.where(upper == left, 1.0, 0.0).astype(BF16)
    high = v.astype(BF16)
    low = (v - high.astype(F32)).astype(BF16)
    return _dot(high, ones, 1, 0) + _dot(low, ones, 1, 0)


def _pair_norm(zp, g, lo):
    r = lax.rsqrt(_half_sum(zp * zp, lo) * (1.0 / HEAD_DIM) + EPS)
    return zp * r * g


def _pair_norm_bwd(zp, g, dy):
    r = lax.rsqrt(_half_sum_mxu(zp * zp) * (1.0 / HEAD_DIM) + EPS)
    xn = zp * r
    dyg = dy * g
    dx = r * (dyg - xn * (_half_sum_mxu(dyg * xn) * (1.0 / HEAD_DIM)))
    return dx, jnp.sum(dy * xn, axis=0, keepdims=True)


def _pack_heads(pairs, lo):
    packed = [None] * 4
    for m in range(2):
        a, b = pairs[m], pairs[m + 2]
        packed[2 * m] = jnp.where(lo, a, pltpu.roll(b, 64, axis=1))
        packed[2 * m + 1] = jnp.where(lo, pltpu.roll(a, 64, axis=1), b)
    return packed


def _unpack_heads(packed, lo):
    pairs = [None] * 4
    for m in range(2):
        a, b = packed[2 * m], packed[2 * m + 1]
        pairs[m] = jnp.where(lo, a, pltpu.roll(b, 64, axis=1))
        pairs[m + 2] = jnp.where(lo, pltpu.roll(a, 64, axis=1), b)
    return pairs


def _expand_heads(packed):
    flat = packed.reshape(4 * BLOCK, 128)
    lo = lax.broadcasted_iota(jnp.int32, flat.shape, 1) < 64
    zero = jnp.zeros_like(flat)
    return jnp.concatenate([jnp.where(lo, flat, zero), jnp.where(lo, zero, flat)], axis=0)


def _fold_heads(stacked):
    half = 4 * BLOCK
    lo = lax.broadcasted_iota(jnp.int32, (half, 128), 1) < 64
    return jnp.where(lo, stacked[:half], stacked[half:]).reshape(4, BLOCK, 128)


def _sigmoid(v):
    return 1.0 / (1.0 + jnp.exp(-v))


def _pool_counts(tile, n_rows):
    t1 = tile * n_rows + lax.broadcasted_iota(jnp.int32, (n_rows, POOL_WIDTH), 0) + 1
    lane = lax.broadcasted_iota(jnp.int32, (n_rows, POOL_WIDTH), 1)
    win = jnp.where(lane < 128, 2, jnp.where(lane < 256, 4, jnp.where(lane < 384, 8, 16)))
    return jnp.minimum(t1, win).astype(F32)


def _first_norm(x2, g_attn):
    s_len = x2.shape[0]
    t = 512

    def body(x_ref, g_ref, hn_ref):
        hn_ref[...] = _rms_fwd(x_ref[...], g_ref[...]).astype(BF16)

    row = pl.BlockSpec((t, D_MODEL), lambda i: (i, 0))
    return pl.pallas_call(
        body, name="first_norm", grid=(s_len // t,), in_specs=[row, _full((1, D_MODEL))], out_specs=row,
        out_shape=jax.ShapeDtypeStruct((s_len, D_MODEL), BF16), compiler_params=_params(),
    )(x2, g_attn)


def _attn_in(hn1, gq, gk, wts):
    s_len = hn1.shape[0]
    t = 512

    def body(hn_ref, gq_ref, gk_ref, sl_ref, lo_ref, me_ref, zqk_ref, u_ref, kn_ref, v_ref, qst_ref, w_ref, sems):
        @pl.when(pl.program_id(0) == 0)
        def _():
            _load_rows((sl_ref, lo_ref, me_ref), "inT", w_ref, sems)

        z = _dot(hn_ref[...], w_ref[...], 1, 1)
        zqk_ref[...] = z[:, :640]
        u_ref[...] = z[:, 768:]
        v_ref[...] = z[:, 640:768].astype(BF16)
        lo = lax.broadcasted_iota(jnp.int32, (t, 128), 1) < 64
        kn_ref[...] = _pair_norm(z[:, 512:640], gk_ref[...], lo).astype(BF16)
        pairs = [_pair_norm(z[:, 128 * p:128 * p + 128], gq_ref[...], lo) for p in range(4)]
        for j, entry in enumerate(_pack_heads(pairs, lo)):
            qst_ref[j] = entry.astype(BF16)

    row = lambda w: pl.BlockSpec((t, w), lambda i: (i, 0))
    return pl.pallas_call(
        body, name="attn_in", grid=(s_len // t,),
        in_specs=[row(D_MODEL), _full((1, 128)), _full((1, 128))] + W_SPECS,
        out_specs=[row(640), row(POOL_WIDTH), row(128), row(128), pl.BlockSpec((4, t, 128), lambda i: (0, i, 0))],
        out_shape=[jax.ShapeDtypeStruct((s_len, 640), F32), jax.ShapeDtypeStruct((s_len, POOL_WIDTH), F32),
                   jax.ShapeDtypeStruct((s_len, 128), BF16), jax.ShapeDtypeStruct((s_len, 128), BF16),
                   jax.ShapeDtypeStruct((4, s_len, 128), BF16)],
        scratch_shapes=[pltpu.VMEM((IN_WIDTH, D_MODEL), BF16), pltpu.SemaphoreType.DMA((N_CHIPS,))],
        compiler_params=_params(),
    )(hn1, gq, gk, *wts)


def _bucket_table():
    i_idx = np.arange(BLOCK)[:, None]
    j_idx = np.arange(2 * BLOCK)[None, :]
    d = BLOCK + i_idx - j_idx
    n = np.maximum(d, 0)
    max_exact = N_BUCKETS // 2
    nf = np.maximum(n, 1).astype(np.float64)
    large = max_exact + (np.log(nf / max_exact) / np.log(MAX_DISTANCE / max_exact) * (N_BUCKETS - max_exact)).astype(np.int64)
    large = np.minimum(large, N_BUCKETS - 1)
    bucket = np.where(n < max_exact, n, large)
    return np.where((d >= 0) & (d < BLOCK), bucket, -1).astype(np.int32)


def _bias_build(rel_bias_t, bucket):
    def body(rb_ref, bucket_ref, out_ref):
        bk = bucket_ref[...]
        for h in range(N_Q_HEADS):
            acc = jnp.full((BLOCK, 2 * BLOCK), NEG, F32)
            for b in range(N_BUCKETS):
                acc = jnp.where(bk == b, rb_ref[h, b], acc)
            out_ref[0, pl.ds(h * BLOCK, BLOCK), :] = acc
            out_ref[1, pl.ds(h * BLOCK, BLOCK), :] = acc
            out_ref[1, pl.ds(h * BLOCK, BLOCK), 0:BLOCK] = jnp.full((BLOCK, BLOCK), NEG, F32)

    return pl.pallas_call(
        body, name="bias_build",
        in_specs=[pl.BlockSpec(memory_space=pltpu.SMEM), VMEM_WHOLE], out_specs=VMEM_WHOLE,
        out_shape=jax.ShapeDtypeStruct((2, N_Q_HEADS * BLOCK, 2 * BLOCK), F32),
    )(rel_bias_t, bucket)


def _head_softmax(s_ref, bias_ref, sink_ref, h):
    rows = pl.ds(pl.multiple_of(h * BLOCK, BLOCK), BLOCK)
    s = s_ref[rows, :] * (HEAD_DIM ** -0.5) + bias_ref[rows, :]
    sink = sink_ref[h]
    m = jnp.maximum(jnp.max(s, axis=-1, keepdims=True), sink)
    p = jnp.exp(s - m)
    e_sink = jnp.exp(sink - m)
    inv = 1.0 / (jnp.sum(p, axis=-1, keepdims=True) + e_sink)
    return rows, p * inv, e_sink * inv


ATTN_STEP_BLOCKS = 4
BAND = (N_Q_HEADS * BLOCK, 2 * BLOCK)


def _attn_specs():
    nb = ATTN_STEP_BLOCKS
    stacked = pl.BlockSpec((4, nb * BLOCK, 128), lambda i: (0, i, 0))
    kv = [pl.BlockSpec((BLOCK, 128), lambda i: (jnp.maximum(nb * i - 1, 0), 0)), pl.BlockSpec((nb * BLOCK, 128), lambda i: (i, 0))]
    consts = [_full((2,) + BAND), pl.BlockSpec(memory_space=pltpu.SMEM)]
    return stacked, kv, consts


def _step_blocks(i, kp_ref, kc_ref, vp_ref, vc_ref, bias_ref):
    blocks = []
    for b in range(ATTN_STEP_BLOCKS):
        if b == 0:
            k2 = jnp.concatenate([kp_ref[...], kc_ref[pl.ds(0, BLOCK), :]], axis=0)
            v2 = jnp.concatenate([vp_ref[...], vc_ref[pl.ds(0, BLOCK), :]], axis=0)
            bias = bias_ref.at[jnp.where(i == 0, 1, 0)]
        else:
            k2, v2, bias = kc_ref[pl.ds((b - 1) * BLOCK, 2 * BLOCK), :], vc_ref[pl.ds((b - 1) * BLOCK, 2 * BLOCK), :], bias_ref.at[0]
        blocks.append((pl.ds(b * BLOCK, BLOCK), k2, v2, bias))
    return blocks


def _attn_fwd(qst, kn, vb, bias_st, sinks):
    s_len = kn.shape[0]

    def body(q_ref, kp_ref, kc_ref, vp_ref, vc_ref, bias_ref, sink_ref, o_ref, s_ref, p_ref):
        for b, (rows, k2, v2, bias) in enumerate(_step_blocks(pl.program_id(0), kp_ref, kc_ref, vp_ref, vc_ref, bias_ref)):
            s_b, p_b = s_ref.at[b], p_ref.at[b]
            s_b[...] = _dot(_expand_heads(q_ref[:, rows, :]), k2, 1, 1)

            def head(h, carry):
                head_rows, probs, _ = _head_softmax(s_b, bias, sink_ref, h)
                p_b[head_rows, :] = probs.astype(BF16)
                return carry

            lax.fori_loop(0, N_Q_HEADS, head, 0, unroll=True)
            o_ref[:, rows, :] = _fold_heads(_dot(p_b[...], v2, 1, 0)).astype(BF16)

    stacked, kv, consts = _attn_specs()
    return pl.pallas_call(
        body, name="attn_fwd", grid=(s_len // (ATTN_STEP_BLOCKS * BLOCK),),
        in_specs=[stacked] + kv + kv + consts, out_specs=stacked,
        out_shape=jax.ShapeDtypeStruct((4, s_len, 128), BF16),
        scratch_shapes=[pltpu.VMEM((ATTN_STEP_BLOCKS,) + BAND, F32), pltpu.VMEM((ATTN_STEP_BLOCKS,) + BAND, BF16)],
        compiler_params=_params(),
    )(qst, kn, kn, vb, vb, bias_st, sinks)


def _mix_out(u, ost, x2, wts, wpool, pool_scale, g_ffn):
    s_len = x2.shape[0]
    t = 512
    n = t + 16

    def body(u_ref, o_ref, x_ref, sl_ref, lo_ref, me_ref, wp_ref, sc_ref, g_ref, pooled_ref, mix_ref, h1_ref, hn_ref,
             w_ref, ext_ref, st_ref, sems):
        i = pl.program_id(0)

        @pl.when(i == 0)
        def _():
            _load_rows((sl_ref, lo_ref, me_ref), "out", w_ref, sems)
            ext_ref[...] = jnp.zeros_like(ext_ref)
            st_ref[...] = jnp.zeros_like(st_ref)

        u_tile = u_ref[...]
        ext_ref[pl.ds(POOL_HALO, t), :] = u_tile
        st_ref[pl.ds(8, n), :] = ext_ref[pl.ds(8, n), :] + ext_ref[pl.ds(7, n), :]
        st_ref[pl.ds(8, n), 128:] = st_ref[pl.ds(8, n), 128:] + st_ref[pl.ds(6, n), 128:]
        st_ref[pl.ds(8, n), 256:] = st_ref[pl.ds(8, n), 256:] + st_ref[pl.ds(4, n), 256:]
        st_ref[pl.ds(8, n), 384:] = st_ref[pl.ds(8, n), 384:] + st_ref[pl.ds(0, n), 384:]
        ext_ref[pl.ds(0, POOL_HALO), :] = ext_ref[pl.ds(t, POOL_HALO), :]
        pooled = (st_ref[pl.ds(POOL_HALO, t), :] / _pool_counts(i, t) - u_tile).astype(BF16)
        pooled_ref[...] = pooled
        for g in range(4):
            cols = slice(128 * g, 128 * g + 128)
            y = _dot(pooled[:, cols], wp_ref[g], 1, 0) * sc_ref[:, cols]
            mix_ref[:, ATTN_WIDTH + 128 * g:ATTN_WIDTH + 128 * g + 128] = y.astype(BF16)
        lo = lax.broadcasted_iota(jnp.int32, (t, 128), 1) < 64
        for p, pair in enumerate(_unpack_heads([o_ref[j].astype(F32) for j in range(4)], lo)):
            mix_ref[:, 128 * p:128 * p + 128] = pair.astype(BF16)
        h1 = x_ref[...] + _dot(mix_ref[...], w_ref[...], 1, 0)
        h1_ref[...] = h1
        hn_ref[...] = _rms_fwd(h1, g_ref[...]).astype(BF16)

    row = lambda w: pl.BlockSpec((t, w), lambda i: (i, 0))
    return pl.pallas_call(
        body, name="mix_out", grid=(s_len // t,),
        in_specs=[row(POOL_WIDTH), pl.BlockSpec((4, t, 128), lambda i: (0, i, 0)), row(D_MODEL)] + W_SPECS
        + [_full((4, 128, 128)), _full((1, POOL_WIDTH)), _full((1, D_MODEL))],
        out_specs=[row(POOL_WIDTH), row(D_MODEL), row(D_MODEL), row(D_MODEL)],
        out_shape=[jax.ShapeDtypeStruct((s_len, POOL_WIDTH), BF16), jax.ShapeDtypeStruct((s_len, D_MODEL), BF16),
                   jax.ShapeDtypeStruct((s_len, D_MODEL), F32), jax.ShapeDtypeStruct((s_len, D_MODEL), BF16)],
        scratch_shapes=[pltpu.VMEM((D_MODEL, D_MODEL), BF16), pltpu.VMEM((t + POOL_HALO, POOL_WIDTH), F32),
                        pltpu.VMEM((t + POOL_HALO, POOL_WIDTH), F32), pltpu.SemaphoreType.DMA((N_CHIPS,))],
        compiler_params=_params(),
    )(u, ost, x2, *wts, wpool, pool_scale, g_ffn)


def _ffn_ple(hn2, h1, p2, tgt, wts, g_ffn, g_ple):
    s_len = h1.shape[0]
    t = 256
    n_tiles = s_len // t

    def body(hn_ref, h1_ref, p_ref, tgt_ref, sl_ref, lo_ref, me_ref, gf_ref, gp_ref,
             loss_ref, dgate_ref, dup_ref, act_ref, dh2b_ref, hn3_ref, dgl_ref, dwp_ref, dh1_ref, dgf_ref, dgp_ref,
             wg_ref, wu_ref, wd_ref, wl_ref, wp_ref, packed_ref, gate_s, up_s, loss_acc, dwp_acc, sems):
        i = pl.program_id(0)

        @pl.when(i == 0)
        def _():
            w_refs = (sl_ref, lo_ref, me_ref)
            _load_rows(w_refs, "gateT", wg_ref, sems)
            _load_rows(w_refs, "upT", wu_ref, sems)
            _load_rows(w_refs, "down", wd_ref, sems)
            _load_rows(w_refs, "plg", wl_ref, sems)
            _load_rows(w_refs, "plp", packed_ref, sems)
            for j in range(N_CHIPS):
                for q in range(4):
                    wp_ref[pl.ds(64 * q, 64), 256 * j:256 * j + 256] = packed_ref[pl.ds(64 * j, 64), 256 * q:256 * q + 256]
            loss_acc[...] = jnp.zeros_like(loss_acc)
            dgf_ref[...] = jnp.zeros_like(dgf_ref)
            dgp_ref[...] = jnp.zeros_like(dgp_ref)

        hn = hn_ref[...]
        h1v = h1_ref[...]
        h2 = h1v
        for ch in range(N_FF_CHUNKS):
            rows = pl.ds(ch * FF_CHUNK, FF_CHUNK)
            gate = _dot(hn, wg_ref[rows, :], 1, 1)
            up = _dot(hn, wu_ref[rows, :], 1, 1)
            gate_s[ch] = gate
            up_s[ch] = up
            act = (gate * _sigmoid(gate) * up).astype(BF16)
            act_ref[ch] = act
            h2 = h2 + _dot(act, wd_ref[rows, :], 1, 0)
        gp = gp_ref[...]
        hn3 = _rms_fwd(h2, gp).astype(BF16)
        hn3_ref[...] = hn3
        gate2 = _sigmoid(_dot(hn3, wl_ref[...], 1, 0))
        p_tile = p_ref[...].astype(BF16)
        pp = _dot(p_tile, wp_ref[...], 1, 0)
        err = h2 + gate2 * pp - tgt_ref[...]
        loss_acc[...] += jnp.sum(err * err, axis=0, keepdims=True)
        dy = err * (1.0 / D_MODEL)
        _accumulate_tn(dwp_acc, p_tile, (dy * gate2).astype(BF16), i == 0)
        dgl = (dy * pp * gate2 * (1.0 - gate2)).astype(BF16)
        dgl_ref[...] = dgl
        dx3, dg3 = _rms_bwd(h2, gp, _dot(dgl, wl_ref[...], 1, 1))
        dh2 = dy + dx3
        dgp_ref[...] += dg3
        dh2b = dh2.astype(BF16)
        dh2b_ref[...] = dh2b
        dhn = jnp.zeros((t, D_MODEL), F32)
        for ch in range(N_FF_CHUNKS):
            rows = pl.ds(ch * FF_CHUNK, FF_CHUNK)
            dact = _dot(dh2b, wd_ref[rows, :], 1, 1)
            gate_v = gate_s[ch]
            up_v = up_s[ch]
            sg = _sigmoid(gate_v)
            dup = (dact * (gate_v * sg)).astype(BF16)
            dgate = (dact * up_v * (sg * (1.0 + gate_v * (1.0 - sg)))).astype(BF16)
            dup_ref[ch] = dup
            dgate_ref[ch] = dgate
            dhn = dhn + _dot(dgate, wg_ref[rows, :], 1, 0) + _dot(dup, wu_ref[rows, :], 1, 0)
        dx, dg = _rms_bwd(h1v, gf_ref[...], dhn)
        dh1_ref[...] = dh2 + dx
        dgf_ref[...] += dg

        @pl.when(i == n_tiles - 1)
        def _():
            total = jnp.sum(loss_acc[...], axis=-1, keepdims=True) * (0.5 / D_MODEL)
            loss_ref[...] = jnp.broadcast_to(total, loss_ref.shape)
            dwp_ref[...] = dwp_acc[...].astype(BF16)

    row = lambda w: pl.BlockSpec((t, w), lambda i: (i, 0))
    chunked = pl.BlockSpec((N_FF_CHUNKS, t, FF_CHUNK), lambda i: (0, i, 0))
    vec = _full((1, D_MODEL))
    act_shape = jax.ShapeDtypeStruct((N_FF_CHUNKS, s_len, FF_CHUNK), BF16)
    tok = lambda dtype: jax.ShapeDtypeStruct((s_len, D_MODEL), dtype)
    return pl.pallas_call(
        body, name="ffn_ple", grid=(n_tiles,),
        in_specs=[row(D_MODEL), row(D_MODEL), row(PLE_DIM), row(D_MODEL)] + W_SPECS + [vec, vec],
        out_specs=[_full((1, 128)), chunked, chunked, chunked] + [row(D_MODEL)] * 3 + [_full((PLE_DIM, D_MODEL)), row(D_MODEL),
                                                                                       vec, vec],
        out_shape=[jax.ShapeDtypeStruct((1, 128), F32), act_shape, act_shape, act_shape, tok(BF16), tok(BF16), tok(BF16),
                   jax.ShapeDtypeStruct((PLE_DIM, D_MODEL), BF16), tok(F32), jax.ShapeDtypeStruct((1, D_MODEL), F32),
                   jax.ShapeDtypeStruct((1, D_MODEL), F32)],
        scratch_shapes=[pltpu.VMEM((D_FF, D_MODEL), BF16)] * 3
        + [pltpu.VMEM((D_MODEL, D_MODEL), BF16), pltpu.VMEM((PLE_DIM, D_MODEL), BF16), pltpu.VMEM((PLE_DIM, D_MODEL), BF16),
           pltpu.VMEM((N_FF_CHUNKS, t, FF_CHUNK), F32), pltpu.VMEM((N_FF_CHUNKS, t, FF_CHUNK), F32), pltpu.VMEM((1, D_MODEL), F32),
           pltpu.VMEM((PLE_DIM, D_MODEL), F32), pltpu.SemaphoreType.DMA((N_CHIPS,))],
        compiler_params=_params(VMEM_LIMIT_BIG),
    )(hn2, h1, p2, tgt, *wts, g_ffn, g_ple)


def _accumulate_tn(acc_ref, a, b, first):
    @pl.when(first)
    def _():
        acc_ref[...] = _dot(a, b, 0, 0)

    @pl.when(jnp.logical_not(first))
    def _():
        acc_ref[...] += _dot(a, b, 0, 0)


def _flush_chunks(acc_ref, stage_ref, slab_ref, name, sems):
    stage_ref[...] = acc_ref[...].astype(BF16)
    off, rows = SLAB[name]
    copies = [pltpu.make_async_copy(stage_ref.at[pl.ds(j * rows, rows), :], slab_ref.at[j, pl.ds(off, rows), :], sems.at[j])
              for j in range(N_CHIPS)]
    for cp in copies:
        cp.start()
    for cp in copies:
        cp.wait()


def _mix_out_bwd(dh1, wts, pooled, wpool, pool_scale, mix, after):
    s_len = dh1.shape[0]
    t = 512
    n = t + 16
    n_tiles = s_len // t
    early_rows = GATHER_PARTS[0][1]

    def body(dh1_ref, sl_ref, lo_ref, me_ref, pooled_ref, wp_ref, sc_ref, mix_ref, after_ref, dost_ref, du_ref, dwp_ref,
             dsc_ref, slab_ref, w_ref, ext_ref, st_ref, acc_ref, stage_ref, sems):
        del after_ref
        i = pl.program_id(0)

        @pl.when(i == 0)
        def _():
            _load_rows((sl_ref, lo_ref, me_ref), "out", w_ref, sems)
            ext_ref[...] = jnp.zeros_like(ext_ref)
            st_ref[...] = jnp.zeros_like(st_ref)
            dsc_ref[...] = jnp.zeros_like(dsc_ref)
            dwp_ref[...] = jnp.zeros_like(dwp_ref)

        dh1b = dh1_ref[...].astype(BF16)
        _accumulate_tn(acc_ref, mix_ref[...], dh1b, i == 0)

        @pl.when(i == n_tiles - 1)
        def _():
            _flush_chunks(acc_ref, stage_ref, slab_ref, "out", sems)

        dmix = _dot(dh1b, w_ref[...], 1, 1)
        lo = lax.broadcasted_iota(jnp.int32, (t, 128), 1) < 64
        for j, entry in enumerate(_pack_heads([dmix[:, 128 * p:128 * p + 128] for p in range(4)], lo)):
            dost_ref[j] = entry.astype(BF16)
        pooled_v = pooled_ref[...]
        counts = _pool_counts(n_tiles - 1 - i, t)
        for g in range(4):
            cols = slice(128 * g, 128 * g + 128)
            dm = dmix[:, ATTN_WIDTH + 128 * g:ATTN_WIDTH + 128 * g + 128]
            ypre = _dot(pooled_v[:, cols], wp_ref[g], 1, 0)
            dsc_ref[:, cols] += jnp.sum(ypre * dm, axis=0, keepdims=True)
            dyp = (dm * sc_ref[:, cols]).astype(BF16)
            dwp_ref[g] += _dot(pooled_v[:, cols], dyp, 0, 0)
            dpooled = _dot(dyp, wp_ref[g], 1, 1)
            du_ref[:, cols] = -dpooled
            ext_ref[pl.ds(0, t), cols] = dpooled / counts[:, cols]
        st_ref[pl.ds(0, n), :] = ext_ref[pl.ds(0, n), :] + ext_ref[pl.ds(1, n), :]
        st_ref[pl.ds(0, n), 128:] = st_ref[pl.ds(0, n), 128:] + st_ref[pl.ds(2, n), 128:]
        st_ref[pl.ds(0, n), 256:] = st_ref[pl.ds(0, n), 256:] + st_ref[pl.ds(4, n), 256:]
        st_ref[pl.ds(0, n), 384:] = st_ref[pl.ds(0, n), 384:] + st_ref[pl.ds(8, n), 384:]
        ext_ref[pl.ds(t, POOL_HALO), :] = ext_ref[pl.ds(0, POOL_HALO), :]
        du_ref[...] += st_ref[pl.ds(0, t), :]

    rev = lambda w: pl.BlockSpec((t, w), lambda i: (n_tiles - 1 - i, 0))
    return pl.pallas_call(
        body, name="mix_out_bwd", grid=(n_tiles,),
        in_specs=[rev(D_MODEL)] + W_SPECS + [rev(POOL_WIDTH), _full((4, 128, 128)), _full((1, POOL_WIDTH)), rev(D_MODEL), ANY],
        out_specs=[pl.BlockSpec((4, t, 128), lambda i: (0, n_tiles - 1 - i, 0)), rev(POOL_WIDTH),
                   _full((4, 128, 128)), _full((1, POOL_WIDTH)), ANY],
        out_shape=[jax.ShapeDtypeStruct((4, s_len, 128), BF16), jax.ShapeDtypeStruct((s_len, POOL_WIDTH), F32),
                   jax.ShapeDtypeStruct((4, 128, 128), F32), jax.ShapeDtypeStruct((1, POOL_WIDTH), F32),
                   jax.ShapeDtypeStruct((N_CHIPS, early_rows, D_MODEL), BF16)],
        scratch_shapes=[pltpu.VMEM((D_MODEL, D_MODEL), BF16), pltpu.VMEM((t + POOL_HALO, POOL_WIDTH), F32),
                        pltpu.VMEM((t + POOL_HALO, POOL_WIDTH), F32), pltpu.VMEM((D_MODEL, D_MODEL), F32),
                        pltpu.VMEM((D_MODEL, D_MODEL), BF16), pltpu.SemaphoreType.DMA((N_CHIPS,))],
        compiler_params=_params(),
    )(dh1, *wts, pooled, wpool, pool_scale, mix, after)


def _attn_bwd(qst, kn, vb, dost, bias_st, sinks, after):
    s_len = kn.shape[0]

    def body(q_ref, kp_ref, kc_ref, vp_ref, vc_ref, do_ref, bias_ref, sink_ref, after_ref, dq_ref, dk_ref, dv_ref, dbias_ref,
             dsink_ref, s_ref, dp_ref, p_ref, dl_ref):
        del after_ref
        i = pl.program_id(0)

        @pl.when(i == 0)
        def _():
            dk_ref[...] = jnp.zeros_like(dk_ref)
            dv_ref[...] = jnp.zeros_like(dv_ref)
            dbias_ref[...] = jnp.zeros_like(dbias_ref)
            dsink_ref[...] = jnp.zeros_like(dsink_ref)

        for b, (rows, k2, v2, bias) in enumerate(_step_blocks(i, kp_ref, kc_ref, vp_ref, vc_ref, bias_ref)):
            s_b, dp_b, p_b, dl_b = s_ref.at[b], dp_ref.at[b], p_ref.at[b], dl_ref.at[b]
            q = _expand_heads(q_ref[:, rows, :])
            do = _expand_heads(do_ref[:, rows, :])
            s_b[...] = _dot(q, k2, 1, 1)
            dp_b[...] = _dot(do, v2, 1, 1)

            def head(h, carry):
                head_rows, probs, p_sink = _head_softmax(s_b, bias, sink_ref, h)
                dp = dp_b[head_rows, :]
                dsum = jnp.sum(probs * dp, axis=-1, keepdims=True)
                dlog = probs * (dp - dsum)
                dsink_ref[head_rows, :] -= p_sink * dsum
                dbias_ref[head_rows, :] += dlog
                p_b[head_rows, :] = probs.astype(BF16)
                dl_b[head_rows, :] = (dlog * (HEAD_DIM ** -0.5)).astype(BF16)
                return carry

            lax.fori_loop(0, N_Q_HEADS, head, 0, unroll=True)
            dlog_s = dl_b[...]
            dq_ref[:, rows, :] = _fold_heads(_dot(dlog_s, k2, 1, 0))
            dk2 = _dot(dlog_s, q, 0, 0)
            dv2 = _dot(p_b[...], do, 0, 0)
            block = ATTN_STEP_BLOCKS * i + b
            prev_rows = pl.ds(pl.multiple_of(jnp.maximum(block - 1, 0) * BLOCK, BLOCK), BLOCK)
            cur_rows = pl.ds(pl.multiple_of(block * BLOCK, BLOCK), BLOCK)
            dk_ref[prev_rows, :] += dk2[:BLOCK]
            dk_ref[cur_rows, :] += dk2[BLOCK:]
            dv_ref[prev_rows, :] += dv2[:BLOCK]
            dv_ref[cur_rows, :] += dv2[BLOCK:]

    stacked, kv, consts = _attn_specs()
    per_step = (ATTN_STEP_BLOCKS,) + BAND
    return pl.pallas_call(
        body, name="attn_bwd", grid=(s_len // (ATTN_STEP_BLOCKS * BLOCK),),
        in_specs=[stacked] + kv + kv + [stacked] + consts + [ANY],
        out_specs=[stacked, _full((s_len, 128)), _full((s_len, 128)), _full(BAND), _full((N_Q_HEADS * BLOCK, 1))],
        out_shape=[jax.ShapeDtypeStruct((4, s_len, 128), F32), jax.ShapeDtypeStruct((s_len, 128), F32),
                   jax.ShapeDtypeStruct((s_len, 128), F32), jax.ShapeDtypeStruct(BAND, F32),
                   jax.ShapeDtypeStruct((N_Q_HEADS * BLOCK, 1), F32)],
        scratch_shapes=[pltpu.VMEM(per_step, F32), pltpu.VMEM(per_step, F32), pltpu.VMEM(per_step, BF16),
                        pltpu.VMEM(per_step, BF16)],
        compiler_params=_params(),
    )(qst, kn, kn, vb, vb, dost, bias_st, sinks, after)


def _small_pack(dg_attn, dg_ffn, dg_ple, dscale, dgq, dgk, dbias, dsink_rows, bucket, loss_v):
    def body(ga_ref, gf_ref, gp_ref, sc_ref, gq_ref, gk_ref, db_ref, ds_ref, bucket_ref, loss_ref, out_ref):
        out_ref[...] = jnp.zeros((SMALL_ROWS, 128), F32)
        for name, ref, n in (("g_attn", ga_ref, 8), ("g_ffn", gf_ref, 8), ("g_ple", gp_ref, 8), ("pool_scale", sc_ref, 4)):
            for k in range(n):
                out_ref[pl.ds(SMALL[name] + k, 1), :] = ref[:, 128 * k:128 * k + 128]
        for name, ref in (("g_q", gq_ref), ("g_k", gk_ref)):
            both = ref[...]
            out_ref[pl.ds(SMALL[name], 1), :] = both + pltpu.roll(both, 64, axis=1)
        out_ref[pl.ds(SMALL["loss"], 1), :] = loss_ref[...]
        bk = bucket_ref[...]
        rows = lax.broadcasted_iota(jnp.int32, (N_Q_HEADS, 128), 0)
        lanes = lax.broadcasted_iota(jnp.int32, (N_Q_HEADS, 128), 1)
        lane1 = lax.broadcasted_iota(jnp.int32, (1, 128), 1)
        rb = jnp.zeros((N_Q_HEADS, 128), F32)
        sk = jnp.zeros((1, 128), F32)
        for h in range(N_Q_HEADS):
            band = db_ref[pl.ds(h * BLOCK, BLOCK), :]
            for b in range(N_BUCKETS):
                rb = jnp.where((rows == h) & (lanes == b), jnp.sum(jnp.where(bk == b, band, 0.0)), rb)
            sk = jnp.where(lane1 == h, jnp.sum(ds_ref[pl.ds(h * BLOCK, BLOCK), :]), sk)
        out_ref[pl.ds(SMALL["rel_bias"], N_Q_HEADS), :] = rb
        out_ref[pl.ds(SMALL["sinks"], 1), :] = sk

    return pl.pallas_call(
        body, name="small_pack", in_specs=[VMEM_WHOLE] * 10, out_specs=VMEM_WHOLE,
        out_shape=jax.ShapeDtypeStruct((SMALL_ROWS, 128), F32),
    )(dg_attn, dg_ffn, dg_ple, dscale, dgq, dgk, dbias, dsink_rows, bucket, loss_v)


def _attn_in_bwd(dqst, zqk, dk, dv, du, x2, dh1, hn1, slab, wts, g_attn, gq, gk):
    s_len = x2.shape[0]
    t = 512
    n_tiles = s_len // t

    def body(dq_ref, zqk_ref, dk_ref, dv_ref, du_ref, x_ref, dh1_ref, hn_ref, slab_in_ref, sl_ref, lo_ref, me_ref, g_ref,
             gq_ref, gk_ref, dx_ref, dg_ref, dgq_ref, dgk_ref, slab_ref, w_ref, dz_ref, acc_ref, stage_ref, sems):
        del slab_in_ref
        i = pl.program_id(0)

        @pl.when(i == 0)
        def _():
            _load_rows((sl_ref, lo_ref, me_ref), "inT", w_ref, sems)
            dg_ref[...] = jnp.zeros_like(dg_ref)
            dgq_ref[...] = jnp.zeros_like(dgq_ref)
            dgk_ref[...] = jnp.zeros_like(dgk_ref)

        lo = lax.broadcasted_iota(jnp.int32, (t, 128), 1) < 64
        for p, dqn in enumerate(_unpack_heads([dq_ref[j] for j in range(4)], lo)):
            dq_raw, dgq = _pair_norm_bwd(zqk_ref[:, 128 * p:128 * p + 128], gq_ref[...], dqn)
            dz_ref[:, 128 * p:128 * p + 128] = dq_raw.astype(BF16)
            dgq_ref[...] += dgq
        dk_raw, dgk = _pair_norm_bwd(zqk_ref[:, 512:640], gk_ref[...], dk_ref[...])
        dgk_ref[...] += dgk
        dz_ref[:, 512:640] = dk_raw.astype(BF16)
        dz_ref[:, 640:768] = dv_ref[...].astype(BF16)
        dz_ref[:, 768:] = du_ref[...].astype(BF16)
        dz = dz_ref[...]
        _accumulate_tn(acc_ref, dz, hn_ref[...], i == 0)
        dx, dg = _rms_bwd(x_ref[...], g_ref[...], _dot(dz, w_ref[...], 1, 0))
        dx_ref[...] = dh1_ref[...] + dx
        dg_ref[...] += dg

        @pl.when(i == n_tiles - 1)
        def _():
            _flush_chunks(acc_ref, stage_ref, slab_ref, "inT", sems)

    row = lambda w: pl.BlockSpec((t, w), lambda i: (i, 0))
    return pl.pallas_call(
        body, name="attn_in_bwd", grid=(n_tiles,),
        in_specs=[pl.BlockSpec((4, t, 128), lambda i: (0, i, 0)), row(640), row(128), row(128), row(POOL_WIDTH),
                  row(D_MODEL), row(D_MODEL), row(D_MODEL), ANY] + W_SPECS + [_full((1, D_MODEL)), _full((1, 128)),
                                                                              _full((1, 128))],
        out_specs=[row(D_MODEL), _full((1, D_MODEL)), _full((1, 128)), _full((1, 128)), ANY],
        out_shape=[jax.ShapeDtypeStruct((s_len, D_MODEL), F32), jax.ShapeDtypeStruct((1, D_MODEL), F32),
                   jax.ShapeDtypeStruct((1, 128), F32), jax.ShapeDtypeStruct((1, 128), F32),
                   jax.ShapeDtypeStruct(slab.shape, BF16)],
        input_output_aliases={8: 4},
        scratch_shapes=[pltpu.VMEM((IN_WIDTH, D_MODEL), BF16), pltpu.VMEM((t, IN_WIDTH), BF16),
                        pltpu.VMEM((IN_WIDTH, D_MODEL), F32), pltpu.VMEM((IN_WIDTH, D_MODEL), BF16),
                        pltpu.SemaphoreType.DMA((N_CHIPS,))],
        compiler_params=_params(),
    )(dqst, zqk, dk, dv, du, x2, dh1, hn1, slab, *wts, g_attn, gq, gk)


def _dw(lefts, b, name, slab, slab_rows, row_offs):
    a0, n_a = lefts[0], len(lefts)
    assert b.shape[1] == D_MODEL
    if a0.ndim == 3:
        n_chunks, s_len, tm = a0.shape
        m = n_chunks * tm
    else:
        s_len, tm = a0.shape
        m = tm
    tk = 2048 if n_a * tm <= 1408 else 1024
    if a0.ndim == 3:
        a_spec = pl.BlockSpec((None, tk, tm), lambda i, k: (i, k, 0))
    else:
        a_spec = pl.BlockSpec((tk, tm), lambda i, k: (k, i))
    n_steps, n_tiles = s_len // tk, m // tm
    chunk = m // N_CHIPS
    per_tile = tm // chunk

    def body(*refs):
        a_refs, b_ref = refs[:n_a], refs[n_a]
        o_ref, acc_ref, stage_ref, sems = refs[-4:]
        i, k = pl.program_id(0), pl.program_id(1)
        b_tile = b_ref[...].astype(BF16)
        for w, a_ref in enumerate(a_refs):
            _accumulate_tn(acc_ref.at[w], a_ref[...].astype(BF16), b_tile, k == 0)

        def out_copies(tile, slot):
            return [pltpu.make_async_copy(stage_ref.at[slot, w, pl.ds(jj * chunk, chunk), :],
                                          o_ref.at[tile * per_tile + jj, pl.ds(row_offs[w], chunk), :], sems.at[slot, w, jj])
                    for w in range(n_a) for jj in range(per_tile)]

        @pl.when(k == n_steps - 1)
        def _():
            slot = i % 2

            @pl.when(i >= 2)
            def _():
                for cp in out_copies(i - 2, slot):
                    cp.wait()

            stage_ref[slot] = acc_ref[...].astype(BF16)
            for cp in out_copies(i, slot):
                cp.start()

            @pl.when(i == n_tiles - 1)
            def _():
                for cp in out_copies(i, slot):
                    cp.wait()
                if n_tiles > 1:
                    for cp in out_copies(i - 1, 1 - slot):
                        cp.wait()

    in_specs = [a_spec] * n_a + [pl.BlockSpec((tk, D_MODEL), lambda i, k: (k, 0))]
    operands, aliases = [*lefts, b], {}
    if slab is not None:
        in_specs.append(ANY)
        operands.append(slab)
        aliases = {n_a + 1: 0}
    return pl.pallas_call(
        body, name=name, grid=(n_tiles, n_steps), in_specs=in_specs, out_specs=ANY,
        out_shape=jax.ShapeDtypeStruct((N_CHIPS, slab_rows, D_MODEL), BF16), input_output_aliases=aliases,
        scratch_shapes=[pltpu.VMEM((n_a, tm, D_MODEL), F32), pltpu.VMEM((2, n_a, tm, D_MODEL), BF16),
                        pltpu.SemaphoreType.DMA((2, n_a, per_tile))],
        compiler_params=_params(VMEM_LIMIT_BIG, n_axes=2),
    )(*operands)


def _position():
    x, y, c = lax.axis_index("x"), lax.axis_index("y"), lax.axis_index("c")
    other_chips = [(1 - x, y), (x, 1 - y), (1 - x, 1 - y)]
    return x, y, c, other_chips


def _ag_weights(local_slab, row0, n_rows, name, collective_id):
    half = n_rows // 2
    quarter = half // 2
    assert quarter % 16 == 0

    def body(l_ref, g_ref, send, recv):
        x, y, c, chips = _position()
        me, (via_x, via_y, diagonal) = 2 * x + y, [2 * chip[0] + chip[1] for chip in chips]
        here, sibling, x_nbr, y_nbr = (x, y, c), (x, y, 1 - c), (1 - x, y, c), (x, 1 - y, c)
        peers = [sibling, x_nbr, y_nbr]
        barrier = pltpu.get_barrier_semaphore()
        for peer in peers:
            pl.semaphore_signal(barrier, inc=1, device_id=peer, device_id_type=MESH)
        pl.semaphore_wait(barrier, len(peers))

        def rows(core, part):
            start, size = (core * half, half) if part is None else (core * half + part * quarter, quarter)
            return pl.ds(pl.multiple_of(start, 16), size)

        def copy(k, chip_idx, where, to, src=None):
            dst = g_ref.at[chip_idx, where, :]
            return pltpu.make_async_remote_copy(src_ref=dst if src is None else src, dst_ref=dst, send_sem=send.at[k],
                                                recv_sem=recv.at[k], device_id=to, device_id_type=MESH)

        own_rows = l_ref.at[pl.ds(pl.multiple_of(row0 + c * half, 16), half), :]
        started = [copy(0, me, rows(c, None), x_nbr, src=own_rows), copy(1, me, rows(c, None), y_nbr, src=own_rows)]
        for cp in started:
            cp.start()
        after_arrival = [
            (copy(0, via_x, rows(c, None), here), [copy(4, via_x, rows(c, None), sibling), copy(3, via_x, rows(c, 1), y_nbr)]),
            (copy(1, via_y, rows(c, None), here), [copy(5, via_y, rows(c, None), sibling), copy(2, via_y, rows(c, 0), x_nbr)]),
            (copy(2, diagonal, rows(c, 0), here), [copy(6, diagonal, rows(c, 0), sibling)]),
            (copy(3, diagonal, rows(c, 1), here), [copy(7, diagonal, rows(c, 1), sibling)]),
        ]
        for arrival, onward in after_arrival:
            arrival.wait_recv()
            for cp in onward:
                cp.start()
            started += onward
        for cp in (copy(4, via_x, rows(1 - c, None), here), copy(5, via_y, rows(1 - c, None), here),
                   copy(6, diagonal, rows(1 - c, 0), here), copy(7, diagonal, rows(1 - c, 1), here)):
            cp.wait_recv()
        for cp in started:
            cp.wait_send()

    return pl.kernel(
        body, out_type=jax.ShapeDtypeStruct((N_CHIPS, n_rows, D_MODEL), BF16),
        mesh=plsc.ScalarSubcoreMesh(axis_name="sequencer", num_cores=1), name=name,
        scratch_types=[pltpu.SemaphoreType.DMA((8,)), pltpu.SemaphoreType.DMA((8,))],
        compiler_params=pltpu.CompilerParams(collective_id=collective_id),
    )(local_slab)


def _comm_call(body, peers_of, out_shape, n_sems, operand, name, collective_id):
    sems = [pltpu.SemaphoreType.DMA((n_sems,)), pltpu.SemaphoreType.DMA((n_sems,))]

    def with_handshake(in_ref, out_ref, send, recv):
        x, y, c, _ = _position()
        peers = peers_of(x, y, c)
        barrier = pltpu.get_barrier_semaphore()
        for peer in peers:
            pl.semaphore_signal(barrier, inc=1, device_id=peer, device_id_type=MESH)
        pl.semaphore_wait(barrier, len(peers))
        body(in_ref, out_ref, send, recv)

    return pl.kernel(with_handshake, out_type=out_shape, mesh=plsc.ScalarSubcoreMesh(axis_name="sequencer", num_cores=1),
                     name=name, scratch_types=sems, compiler_params=pltpu.CompilerParams(collective_id=collective_id))(operand)


def _rs_swap_halves(partial, name, collective_id):
    half = partial.shape[1] // 2

    def body(p_ref, r_ref, send, recv):
        x, y, c, _ = _position()
        theirs = pl.ds(pl.multiple_of((1 - c) * half, 16), half)
        cp = pltpu.make_async_remote_copy(src_ref=p_ref.at[:, theirs, :], dst_ref=r_ref, send_sem=send.at[0],
                                          recv_sem=recv.at[0], device_id=(x, y, 1 - c), device_id_type=MESH)
        cp.start()
        cp.wait()

    return _comm_call(body, lambda x, y, c: [(x, y, 1 - c)], jax.ShapeDtypeStruct((N_CHIPS, half, D_MODEL), BF16), 1,
                      partial, name, collective_id)


def _rs_add_halves(partial, other, core, name, after, small=None):
    half = other.shape[1]
    t = half // 2
    steps = half // t

    def body(core_ref, a_ref, b_ref, after_ref, *rest):
        del after_ref
        o_ref = rest[0] if small is None else rest[1]
        if small is not None:
            small_ref, _, t_ref, t_send, t_recv = rest
            start_tables, finish_tables = _gather_small(small_ref, t_ref, t_send, t_recv)
            j, i = pl.program_id(0), pl.program_id(1)
            pl.when((j == 0) & (i == 0))(start_tables)
        o_ref[...] = (a_ref[...].astype(F32) + b_ref[...].astype(F32)).astype(BF16)
        if small is not None:
            pl.when((j == N_CHIPS - 1) & (i == steps - 1))(finish_tables)

    t_in, t_out, t_scratch = _table_gather_parts(small)
    res = pl.pallas_call(
        body, name=name,
        grid_spec=pltpu.PrefetchScalarGridSpec(
            num_scalar_prefetch=1, grid=(N_CHIPS, steps),
            in_specs=[pl.BlockSpec((1, t, D_MODEL), lambda j, i, core_ref: (j, core_ref[0] * steps + i, 0)),
                      pl.BlockSpec((1, t, D_MODEL), lambda j, i, core_ref: (j, i, 0)), ANY] + t_in,
            out_specs=[pl.BlockSpec((1, t, D_MODEL), lambda j, i, core_ref: (j, i, 0))] + [ANY] * len(t_out),
            scratch_shapes=t_scratch),
        out_shape=[jax.ShapeDtypeStruct((N_CHIPS, half, D_MODEL), BF16)] + t_out,
        compiler_params=_params(n_axes=2),
    )(core, partial, other, after, *([] if small is None else [small]))
    return res[0] if small is None else res


def _rs_exchange_chips(pre, name, collective_id):
    def body(s_ref, r_ref, send, recv):
        x, y, c, chips = _position()

        def copy(k, chunk, to):
            return pltpu.make_async_remote_copy(src_ref=s_ref.at[chunk], dst_ref=r_ref.at[k], send_sem=send.at[k],
                                                recv_sem=recv.at[k], device_id=to, device_id_type=MESH)

        sends = [copy(k, 2 * chip[0] + chip[1], (*chip, c)) for k, chip in enumerate(chips)]
        for cp in sends:
            cp.start()
        for cp in sends:
            cp.wait()

    return _comm_call(body, lambda x, y, c: [(1 - x, y, c), (x, 1 - y, c), (1 - x, 1 - y, c)],
                      jax.ShapeDtypeStruct((3, pre.shape[1], D_MODEL), BF16), 3, pre, name, collective_id)


def _gather_small(s_ref, t_ref, send, recv):
    x, y, c, chips = _position()
    sibling = (x, y, 1 - c)

    def slot(px, py, pc):
        return t_ref.at[4 * px + 2 * py + pc]

    def copy(k, block, to, src=None):
        return pltpu.make_async_remote_copy(src_ref=slot(*block) if src is None else src, dst_ref=slot(*block),
                                            send_sem=send.at[k], recv_sem=recv.at[k], device_id=to, device_id_type=MESH)

    own = pltpu.make_async_copy(s_ref, slot(x, y, c), send.at[7])
    first = [copy(0, (x, y, c), sibling, src=s_ref)]
    first += [copy(1 + k, (x, y, c), (*chip, c), src=s_ref) for k, chip in enumerate(chips)]

    def start():
        own.start()
        for cp in first:
            cp.start()

    def finish():
        passed = []
        for k, chip in enumerate(chips):
            copy(1 + k, (*chip, c), (x, y, c)).wait_recv()
            fwd = copy(4 + k, (*chip, c), sibling)
            fwd.start()
            passed.append(fwd)
        copy(0, sibling, (x, y, c)).wait_recv()
        for k, chip in enumerate(chips):
            copy(4 + k, (*chip, 1 - c), (x, y, c)).wait_recv()
        for cp in first + passed:
            cp.wait_send()
        own.wait()

    return start, finish


def _table_gather_parts(small):
    if small is None:
        return [], [], []
    return [VMEM_WHOLE], [jax.ShapeDtypeStruct((N_DEV, *small.shape), F32)], [pltpu.SemaphoreType.DMA((8,))] * 2


def _rs_sum_chips(pre, received, place, name, after, small=None):
    half = pre.shape[1]
    steps = 4 if half > 512 else 2
    t = half // steps
    assert t % 16 == 0 and t * steps == half

    def body(place_ref, own_ref, r_ref, after_ref, *rest):
        del place_ref, after_ref
        if small is None:
            o_ref, stage, kept_sems, send, recv = rest
        else:
            small_ref, o_ref, t_ref, stage, kept_sems, send, recv, t_send, t_recv = rest
            start_tables, finish_tables = _gather_small(small_ref, t_ref, t_send, t_recv)
            pl.when(pl.program_id(0) == 0)(start_tables)
        i = pl.program_id(0)
        x, y, c, _ = _position()

        def rows(core, step):
            return o_ref.at[pl.ds(pl.multiple_of((core * steps + step) * t, 8), t), :]

        def kept(step):
            return pltpu.make_async_copy(stage.at[step], rows(c, step), kept_sems.at[step])

        def sent(core, step):
            return pltpu.make_async_remote_copy(src_ref=stage.at[step], dst_ref=rows(core, step), send_sem=send.at[step],
                                                recv_sem=recv.at[step], device_id=(x, y, 1 - core), device_id_type=MESH)

        acc = own_ref[0].astype(F32)
        for k in range(3):
            acc = acc + r_ref[k].astype(F32)
        stage[i] = acc
        kept(i).start()
        sent(c, i).start()

        @pl.when(i == steps - 1)
        def _():
            if small is not None:
                finish_tables()
            for step in range(steps):
                kept(step).wait()
                sent(c, step).wait_send()
                sent(1 - c, step).wait_recv()

    t_in, t_out, t_scratch = _table_gather_parts(small)
    res = pl.pallas_call(
        body, name=name,
        grid_spec=pltpu.PrefetchScalarGridSpec(
            num_scalar_prefetch=1, grid=(steps,),
            in_specs=[pl.BlockSpec((1, t, D_MODEL), lambda i, place_ref: (place_ref[0], i, 0)),
                      pl.BlockSpec((3, t, D_MODEL), lambda i, place_ref: (0, i, 0)), ANY] + t_in,
            out_specs=[ANY] * (1 + len(t_out)),
            scratch_shapes=[pltpu.VMEM((steps, t, D_MODEL), F32)] + [pltpu.SemaphoreType.DMA((steps,))] * 3 + t_scratch),
        out_shape=[jax.ShapeDtypeStruct((2 * half, D_MODEL), F32)] + t_out, compiler_params=_params(),
    )(place, pre, received, after, *([] if small is None else [small]))
    return res[0] if small is None else res


def _adam_update(w, g, m, v):
    m_new = ADAM_B1 * m + (1.0 - ADAM_B1) * g
    v_new = ADAM_B2 * v + (1.0 - ADAM_B2) * (g * g)
    m_hat = m_new / (1.0 - ADAM_B1 ** ADAM_STEP)
    v_hat = v_new / (1.0 - ADAM_B2 ** ADAM_STEP)
    return -ADAM_LR * (m_hat / (jnp.sqrt(v_hat) + ADAM_EPS) + ADAM_WD * w), m_new, v_new


def _adamw(w, g_rows, row_off, m, v, name):
    rows, cols = w.shape
    t = rows if rows <= 320 else (rows // 2 if rows % 256 else 256)

    def body(w_ref, g_ref, m_ref, v_ref, go_ref, d_ref, nm_ref, nv_ref):
        g = g_ref[...]
        go_ref[...] = g
        d_ref[...], nm_ref[...], nv_ref[...] = _adam_update(w_ref[...], g, m_ref[...], v_ref[...])

    blk = pl.BlockSpec((t, cols), lambda i: (i, 0))
    assert row_off % 8 == 0 and t % 8 == 0
    g_blk = pl.BlockSpec((pl.Element(t), pl.Element(cols)), lambda i: (pl.multiple_of(row_off + i * t, 8), 0))
    shape = jax.ShapeDtypeStruct((rows, cols), F32)
    return pl.pallas_call(
        body, name=name, grid=(rows // t,), in_specs=[blk, g_blk, blk, blk], out_specs=[blk] * 4, out_shape=[shape] * 4,
        compiler_params=_params(),
    )(w, g_rows, m, v)


SMALL_PARAMS = [("g_attn", (1, D_MODEL), 8), ("g_q", (1, HEAD_DIM), None), ("g_k", (1, HEAD_DIM), None),
                ("sinks", (1, N_Q_HEADS), None), ("rel_bias", (N_Q_HEADS, N_BUCKETS), None), ("w_pool", (512, 128), None),
                ("pool_scale", (1, POOL_WIDTH), 4), ("g_ffn", (1, D_MODEL), 8), ("g_ple", (1, D_MODEL), 8)]


def _adamw_small(tables, pool_tables, wmv):
    n_par = len(SMALL_PARAMS)

    def body(*refs):
        t_ref, p_ref = refs[:2]
        ins = refs[2:2 + 3 * n_par]
        loss_ref = refs[2 + 3 * n_par]
        outs = refs[3 + 3 * n_par:-1]
        tot_ref = refs[-1]

        def in_device_order(ref):
            total = ref[0]
            for d in range(1, N_DEV):
                total = total + ref[d]
            return total

        tot_ref[...] = in_device_order(t_ref)
        loss_ref[...] = tot_ref[pl.ds(SMALL["loss"], 1), 0:1]
        for i, (name, shape, split) in enumerate(SMALL_PARAMS):
            g_ref, d_ref, nm_ref, nv_ref = outs[4 * i:4 * i + 4]
            row = SMALL.get(name)
            if name == "w_pool":
                g_ref[...] = in_device_order(p_ref)
            elif split:
                for k in range(split):
                    g_ref[:, 128 * k:128 * k + 128] = tot_ref[pl.ds(row + k, 1), :]
            else:
                g_ref[...] = tot_ref[pl.ds(row, shape[0]), 0:shape[1]]
            w_ref, m_ref, v_ref = ins[3 * i:3 * i + 3]
            d_ref[...], nm_ref[...], nv_ref[...] = _adam_update(w_ref[...], g_ref[...], m_ref[...], v_ref[...])

    shapes = [jax.ShapeDtypeStruct((1, 1), F32)]
    for _, shape, _ in SMALL_PARAMS:
        shapes += [jax.ShapeDtypeStruct(shape, F32)] * 4
    flat = [a for triple in wmv for a in triple]
    res = pl.pallas_call(
        body, name="adamw_small", in_specs=[VMEM_WHOLE] * (2 + 3 * n_par), out_specs=[VMEM_WHOLE] * len(shapes),
        out_shape=shapes, scratch_shapes=[pltpu.VMEM((SMALL_ROWS, 128), F32)],
    )(tables, pool_tables, *flat)
    return res[0], [res[1 + 4 * i:5 + 4 * i] for i in range(n_par)]


def _pack_ple_proj(shard):
    return shard.reshape(4, 64, 256).transpose(1, 0, 2).reshape(64, D_MODEL)


class _Reduction:
    def __init__(self, tag, place, ids=(None, None)):
        self.tag, self.place, self.ids = tag, place, ids

    def start(self, partial):
        self.partial = partial
        self.other = _rs_swap_halves(partial, "rs_swap_" + self.tag, self.ids[0])
        return partial

    def middle(self, after, small=None):
        res = _rs_add_halves(self.partial, self.other, self.place[1:], "rs_add_" + self.tag, after, small)
        self.pre, self.tables = (res, None) if small is None else res
        self.received = _rs_exchange_chips(self.pre, "rs_exchange_" + self.tag, self.ids[1])
        return self.pre

    def finish(self, after, small=None):
        return _rs_sum_chips(self.pre, self.received, self.place, "rs_sum_" + self.tag, after, small)


def _local_grads(x2, p2, tgt, wts, g_attn_norm, g_q, g_k, attn_sinks, rel_bias, w_pool, pool_scale, g_ffn_norm, g_ple_norm,
                 reduce_a):
    w_early, w_late = wts
    w_in = w_out = w_early
    bucket = jnp.asarray(_bucket_table())
    gq = jnp.tile(g_q, (1, 2))
    gk = jnp.tile(g_k, (1, 2))
    wpool = w_pool[0].astype(BF16)
    sinks = attn_sinks[0]
    bias_st = _bias_build(rel_bias.T, bucket)

    hn1 = _first_norm(x2, g_attn_norm)
    zqk, u, kn, vb, qst = _attn_in(hn1, gq, gk, w_in)
    ost = _attn_fwd(qst, kn, vb, bias_st, sinks)
    pooled, mix, h1, hn2 = _mix_out(u, ost, x2, w_out, wpool, pool_scale, g_ffn_norm)
    loss_v, dgate, dup, act, dh2, hn3, dgl, dw_plp, dh1, dg_ffn, dg_ple = _ffn_ple(hn2, h1, p2, tgt, w_late, g_ffn_norm,
                                                                                      g_ple_norm)

    late0, late_rows = GATHER_PARTS[1][0], SLAB_ROWS - GATHER_PARTS[1][0]
    partial_a = None
    for names, lefts, right in ((("gateT", "upT"), [dgate, dup], hn2), (("down",), [act], dh2), (("plg",), [hn3], dgl)):
        partial_a = _dw(lefts, right, "dw_" + names[0], partial_a, late_rows, [SLAB[name][0] - late0 for name in names])
    dw_plp = dw_plp.reshape(4, 64, N_CHIPS, 256).transpose(2, 1, 0, 3).reshape(N_CHIPS, 64, D_MODEL)
    partial_a = reduce_a.start(lax.dynamic_update_slice(partial_a, dw_plp, (0, SLAB["plp"][0] - late0, 0)))
    dost, du, dw_pool, dscale, partial_b = _mix_out_bwd(dh1, w_out, pooled, wpool, pool_scale, mix, partial_a)
    pre_a = reduce_a.middle(du, dw_pool.reshape(512, 128))
    dqst, dk, dv, dbias, dsink_rows = _attn_bwd(qst, kn, vb, dost, bias_st, sinks, pre_a)
    dx, dg_attn, dgq, dgk, partial_b = _attn_in_bwd(dqst, zqk, dk, dv, du, x2, dh1, hn1, partial_b, w_in, g_attn_norm, gq, gk)

    small = _small_pack(dg_attn, dg_ffn, dg_ple, dscale, dgq, dgk, dbias, dsink_rows, bucket, loss_v)
    return dx, partial_b, small


def kernel(x, p, w_in, w_out, g_attn_norm, g_q, g_k, attn_sinks, rel_bias, w_pool, pool_scale, g_ffn_norm, w_gate, w_up, w_down, g_ple_norm, w_ple_gate, w_ple_proj, loss_target, m_w_in, m_w_out, m_g_attn_norm, m_g_q, m_g_k, m_attn_sinks, m_rel_bias, m_w_pool, m_pool_scale, m_g_ffn_norm, m_w_gate, m_w_up, m_w_down, m_g_ple_norm, m_w_ple_gate, m_w_ple_proj, v_w_in, v_w_out, v_g_attn_norm, v_g_q, v_g_k, v_attn_sinks, v_rel_bias, v_w_pool, v_pool_scale, v_g_ffn_norm, v_w_gate, v_w_up, v_w_down, v_g_ple_norm, v_w_ple_gate, v_w_ple_proj):
    core = lax.axis_index("c").astype(jnp.int32).reshape(1)
    me = (2 * lax.axis_index("x") + lax.axis_index("y")).astype(jnp.int32).reshape(1)

    local_parts = [jnp.concatenate(pieces, axis=0).astype(BF16) for pieces in (
        [w_in[0].T, w_out[0]], [w_gate[0].T, w_up[0].T, w_down[0], w_ple_gate[0], _pack_ple_proj(w_ple_proj[0])])]
    wts = [(_ag_weights(local, 0, local.shape[0], name, collective_id), local, me)
           for local, name, collective_id in zip(local_parts, ("ag_early", "ag_late"), (1, 2))]

    place = jnp.concatenate([me, core])
    reduce_a = _Reduction("a", place, ids=(3, 4))
    dx, partial_b, small = _local_grads(x[0], p[0, 0], loss_target[0], wts, g_attn_norm, g_q, g_k, attn_sinks, rel_bias,
                                        w_pool, pool_scale, g_ffn_norm, g_ple_norm, reduce_a)
    reduce_b = _Reduction("b", place, ids=(6, 7))
    reduce_b.start(partial_b)
    grads_a, small_all = reduce_a.finish(partial_b, small)
    reduce_b.middle(grads_a)

    late0 = GATHER_PARTS[1][0]

    def rows(name):
        return grads_a, SLAB[name][0] - late0

    plp_rows = grads_a[SLAB["plp"][0] - late0:]
    big = {
        "w_gate": (w_gate, m_w_gate, v_w_gate, rows("gateT"), True),
        "w_up": (w_up, m_w_up, v_w_up, rows("upT"), True),
        "w_down": (w_down, m_w_down, v_w_down, rows("down"), False),
        "w_ple_gate": (w_ple_gate, m_w_ple_gate, v_w_ple_gate, rows("plg"), False),
        "w_ple_proj": (w_ple_proj, m_w_ple_proj, v_w_ple_proj,
                       (plp_rows.reshape(64, 4, 256).transpose(1, 0, 2).reshape(PLE_DIM, PLE_DIM), 0), False),
        "w_out": (w_out, m_w_out, v_w_out, None, False),
        "w_in": (w_in, m_w_in, v_w_in, None, True),
    }
    small_params = {
        "g_attn_norm": (g_attn_norm, m_g_attn_norm, v_g_attn_norm), "g_q": (g_q, m_g_q, v_g_q), "g_k": (g_k, m_g_k, v_g_k),
        "attn_sinks": (attn_sinks, m_attn_sinks, v_attn_sinks), "rel_bias": (rel_bias.T, m_rel_bias.T, v_rel_bias.T),
        "w_pool": tuple(a.reshape(512, 128) for a in (w_pool, m_w_pool, v_w_pool)),
        "pool_scale": (pool_scale, m_pool_scale, v_pool_scale), "g_ffn_norm": (g_ffn_norm, m_g_ffn_norm, v_g_ffn_norm),
        "g_ple_norm": (g_ple_norm, m_g_ple_norm, v_g_ple_norm),
    }

    grads, deltas, new_ms, new_vs = {}, {}, {}, {}
    out = grads_b = None
    for name, (w, m, v, g_src, transposed) in big.items():
        if g_src is None:
            if grads_b is None:
                grads_b = reduce_b.finish(out[-1])
            g_src = (grads_b, SLAB["out" if name == "w_out" else "inT"][0])
        view = (lambda a: a.T) if transposed else (lambda a: a)
        out = _adamw(view(w[0]), *g_src, view(m[0]), view(v[0]), "adamw_" + name)
        grads[name], deltas[name], new_ms[name], new_vs[name] = (view(a)[None] for a in out)

    loss, small_out = _adamw_small(small_all, reduce_a.tables, list(small_params.values()))
    for name, (g2, d, nm, nv) in zip(small_params, small_out):
        restore = {"w_pool": lambda a: a.reshape(w_pool.shape), "rel_bias": lambda a: a.T}.get(name, lambda a: a)
        grads[name], deltas[name], new_ms[name], new_vs[name] = (restore(a) for a in (g2, d, nm, nv))

    order = ["w_in", "w_out", "g_attn_norm", "g_q", "g_k", "attn_sinks", "rel_bias", "w_pool", "pool_scale", "g_ffn_norm",
             "w_gate", "w_up", "w_down", "g_ple_norm", "w_ple_gate", "w_ple_proj"]
    return (loss.reshape(()), dx[None], *[grads[n] for n in order], *[deltas[n] for n in order],
            *[new_ms[n] for n in order], *[new_vs[n] for n in order])
```

```python
import numpy as np
import jax
import jax.numpy as jnp
from jax import lax
from jax.experimental import pallas as pl
from jax.experimental.pallas import tpu as pltpu
from jax.experimental.pallas import tpu_sc as plsc

F32 = jnp.float32
BF16 = jnp.bfloat16
MESH = pl.DeviceIdType.MESH

D_MODEL = 1024
HEAD_DIM = 64
N_Q_HEADS = 8
ATTN_WIDTH = 512
POOL_WIDTH = 512
IN_WIDTH = 1280
D_FF = 2816
PLE_DIM = 256
FF_CHUNK = 1408
N_FF_CHUNKS = D_FF // FF_CHUNK
BLOCK = 128
N_BUCKETS = 32
MAX_DISTANCE = 128
EPS = 1e-6
NEG = -1e30
N_CHIPS = 4
N_DEV = 8

ADAM_LR = 0.001
ADAM_B1 = 0.9
ADAM_B2 = 0.999
ADAM_EPS = 1e-08
ADAM_WD = 0.01
ADAM_STEP = 10

SLAB = {"inT": (0, 320), "out": (320, 256), "gateT": (576, 704), "upT": (1280, 704), "down": (1984, 704),
        "plg": (2688, 256), "plp": (2944, 64)}
SLAB_ROWS = 3008
GATHER_PARTS = ((0, 576), (576, SLAB_ROWS))
POOL_HALO = 24

SMALL = {"g_attn": 0, "g_ffn": 8, "g_ple": 16, "pool_scale": 24, "g_q": 28, "g_k": 29, "sinks": 30, "loss": 31,
         "rel_bias": 32}
SMALL_ROWS = 64

VMEM_LIMIT_BIG = 60 * 1024 * 1024
VMEM_LIMIT = 48 * 1024 * 1024


def _params(vmem=VMEM_LIMIT, n_axes=1):
    return pltpu.CompilerParams(dimension_semantics=("arbitrary",) * n_axes, vmem_limit_bytes=vmem)


def _dot(a, b, ca, cb):
    return lax.dot_general(a, b, (((ca,), (cb,)), ((), ())), preferred_element_type=F32)


def _full(shape):
    return pl.BlockSpec(shape, lambda i: (0,) * len(shape))


ANY = pl.BlockSpec(memory_space=pl.ANY)
VMEM_WHOLE = pl.BlockSpec(memory_space=pltpu.VMEM)


W_SPECS = [ANY, ANY, pl.BlockSpec(memory_space=pltpu.SMEM)]


def _load_rows(w_refs, name, dst_ref, sems):
    slab_ref, local_ref, me_ref = w_refs
    off, rows = SLAB[name]
    slab_off = off - max(start for start, _ in GATHER_PARTS if start <= off)
    me = me_ref[0]
    for phase in ("start", "wait"):
        for j in range(N_CHIPS):
            dst = dst_ref.at[pl.ds(j * rows, rows), :]
            theirs = pltpu.make_async_copy(slab_ref.at[j, pl.ds(slab_off, rows), :], dst, sems.at[j])
            own = pltpu.make_async_copy(local_ref.at[pl.ds(slab_off, rows), :], dst, sems.at[j])

            @pl.when(me == j)
            def _():
                getattr(own, phase)()

            @pl.when(me != j)
            def _():
                getattr(theirs, phase)()


def _rms_fwd(x, g):
    r = lax.rsqrt(jnp.mean(x * x, axis=-1, keepdims=True) + EPS)
    return x * r * g


def _rms_bwd(x, g, dy):
    r = lax.rsqrt(jnp.mean(x * x, axis=-1, keepdims=True) + EPS)
    xn = x * r
    dyg = dy * g
    dx = r * (dyg - xn * jnp.mean(dyg * xn, axis=-1, keepdims=True))
    return dx, jnp.sum(dy * xn, axis=0, keepdims=True)


def _half_sum(v, lo):
    s_lo = jnp.sum(jnp.where(lo, v, 0.0), axis=-1, keepdims=True)
    s_hi = jnp.sum(jnp.where(lo, 0.0, v), axis=-1, keepdims=True)
    return jnp.where(lo, s_lo, s_hi)


def _half_sum_mxu(v):
    upper = lax.broadcasted_iota(jnp.int32, (128, 128), 0) < 64
    left = lax.broadcasted_iota(jnp.int32, (128, 128), 1) < 64
    ones = jnp.where(upper == left, 1.0, 0.0).astype(BF16)
    high = v.astype(BF16)
    low = (v - high.astype(F32)).astype(BF16)
    return _dot(high, ones, 1, 0) + _dot(low, ones, 1, 0)


def _pair_norm(zp, g, lo):
    r = lax.rsqrt(_half_sum(zp * zp, lo) * (1.0 / HEAD_DIM) + EPS)
    return zp * r * g


def _pair_norm_bwd(zp, g, dy):
    r = lax.rsqrt(_half_sum_mxu(zp * zp) * (1.0 / HEAD_DIM) + EPS)
    xn = zp * r
    dyg = dy * g
    dx = r * (dyg - xn * (_half_sum_mxu(dyg * xn) * (1.0 / HEAD_DIM)))
    return dx, jnp.sum(dy * xn, axis=0, keepdims=True)


def _pack_heads(pairs, lo):
    packed = [None] * 4
    for m in range(2):
        a, b = pairs[m], pairs[m + 2]
        packed[2 * m] = jnp.where(lo, a, pltpu.roll(b, 64, axis=1))
        packed[2 * m + 1] = jnp.where(lo, pltpu.roll(a, 64, axis=1), b)
    return packed


def _unpack_heads(packed, lo):
    pairs = [None] * 4
    for m in range(2):
        a, b = packed[2 * m], packed[2 * m + 1]
        pairs[m] = jnp.where(lo, a, pltpu.roll(b, 64, axis=1))
        pairs[m + 2] = jnp.where(lo, pltpu.roll(a, 64, axis=1), b)
    return pairs


def _expand_heads(packed):
    flat = packed.reshape(4 * BLOCK, 128)
    lo = lax.broadcasted_iota(jnp.int32, flat.shape, 1) < 64
    zero = jnp.zeros_like(flat)
    return jnp.concatenate([jnp.where(lo, flat, zero), jnp.where(lo, zero, flat)], axis=0)


def _fold_heads(stacked):
    half = 4 * BLOCK
    lo = lax.broadcasted_iota(jnp.int32, (half, 128), 1) < 64
    return jnp.where(lo, stacked[:half], stacked[half:]).reshape(4, BLOCK, 128)


def _sigmoid(v):
    return 1.0 / (1.0 + jnp.exp(-v))


def _pool_counts(tile, n_rows):
    t1 = tile * n_rows + lax.broadcasted_iota(jnp.int32, (n_rows, POOL_WIDTH), 0) + 1
    lane = lax.broadcasted_iota(jnp.int32, (n_rows, POOL_WIDTH), 1)
    win = jnp.where(lane < 128, 2, jnp.where(lane < 256, 4, jnp.where(lane < 384, 8, 16)))
    return jnp.minimum(t1, win).astype(F32)


def _first_norm(x2, g_attn):
    s_len = x2.shape[0]
    t = 512

    def body(x_ref, g_ref, hn_ref):
        hn_ref[...] = _rms_fwd(x_ref[...], g_ref[...]).astype(BF16)

    row = pl.BlockSpec((t, D_MODEL), lambda i: (i, 0))
    return pl.pallas_call(
        body, name="first_norm", grid=(s_len // t,), in_specs=[row, _full((1, D_MODEL))], out_specs=row,
        out_shape=jax.ShapeDtypeStruct((s_len, D_MODEL), BF16), compiler_params=_params(),
    )(x2, g_attn)


def _attn_in(hn1, gq, gk, wts):
    s_len = hn1.shape[0]
    t = 512

    def body(hn_ref, gq_ref, gk_ref, sl_ref, lo_ref, me_ref, zqk_ref, u_ref, kn_ref, v_ref, qst_ref, w_ref, sems):
        @pl.when(pl.program_id(0) == 0)
        def _():
            _load_rows((sl_ref, lo_ref, me_ref), "inT", w_ref, sems)

        z = _dot(hn_ref[...], w_ref[...], 1, 1)
        zqk_ref[...] = z[:, :640]
        u_ref[...] = z[:, 768:]
        v_ref[...] = z[:, 640:768].astype(BF16)
        lo = lax.broadcasted_iota(jnp.int32, (t, 128), 1) < 64
        kn_ref[...] = _pair_norm(z[:, 512:640], gk_ref[...], lo).astype(BF16)
        pairs = [_pair_norm(z[:, 128 * p:128 * p + 128], gq_ref[...], lo) for p in range(4)]
        for j, entry in enumerate(_pack_heads(pairs, lo)):
            qst_ref[j] = entry.astype(BF16)

    row = lambda w: pl.BlockSpec((t, w), lambda i: (i, 0))
    return pl.pallas_call(
        body, name="attn_in", grid=(s_len // t,),
        in_specs=[row(D_MODEL), _full((1, 128)), _full((1, 128))] + W_SPECS,
        out_specs=[row(640), row(POOL_WIDTH), row(128), row(128), pl.BlockSpec((4, t, 128), lambda i: (0, i, 0))],
        out_shape=[jax.ShapeDtypeStruct((s_len, 640), F32), jax.ShapeDtypeStruct((s_len, POOL_WIDTH), F32),
                   jax.ShapeDtypeStruct((s_len, 128), BF16), jax.ShapeDtypeStruct((s_len, 128), BF16),
                   jax.ShapeDtypeStruct((4, s_len, 128), BF16)],
        scratch_shapes=[pltpu.VMEM((IN_WIDTH, D_MODEL), BF16), pltpu.SemaphoreType.DMA((N_CHIPS,))],
        compiler_params=_params(),
    )(hn1, gq, gk, *wts)


def _bucket_table():
    i_idx = np.arange(BLOCK)[:, None]
    j_idx = np.arange(2 * BLOCK)[None, :]
    d = BLOCK + i_idx - j_idx
    n = np.maximum(d, 0)
    max_exact = N_BUCKETS // 2
    nf = np.maximum(n, 1).astype(np.float64)
    large = max_exact + (np.log(nf / max_exact) / np.log(MAX_DISTANCE / max_exact) * (N_BUCKETS - max_exact)).astype(np.int64)
    large = np.minimum(large, N_BUCKETS - 1)
    bucket = np.where(n < max_exact, n, large)
    return np.where((d >= 0) & (d < BLOCK), bucket, -1).astype(np.int32)


def _bias_build(rel_bias_t, bucket):
    def body(rb_ref, bucket_ref, out_ref):
        bk = bucket_ref[...]
        for h in range(N_Q_HEADS):
            acc = jnp.full((BLOCK, 2 * BLOCK), NEG, F32)
            for b in range(N_BUCKETS):
                acc = jnp.where(bk == b, rb_ref[h, b], acc)
            out_ref[0, pl.ds(h * BLOCK, BLOCK), :] = acc
            out_ref[1, pl.ds(h * BLOCK, BLOCK), :] = acc
            out_ref[1, pl.ds(h * BLOCK, BLOCK), 0:BLOCK] = jnp.full((BLOCK, BLOCK), NEG, F32)

    return pl.pallas_call(
        body, name="bias_build",
        in_specs=[pl.BlockSpec(memory_space=pltpu.SMEM), VMEM_WHOLE], out_specs=VMEM_WHOLE,
        out_shape=jax.ShapeDtypeStruct((2, N_Q_HEADS * BLOCK, 2 * BLOCK), F32),
    )(rel_bias_t, bucket)


def _head_softmax(s_ref, bias_ref, sink_ref, h):
    rows = pl.ds(pl.multiple_of(h * BLOCK, BLOCK), BLOCK)
    s = s_ref[rows, :] * (HEAD_DIM ** -0.5) + bias_ref[rows, :]
    sink = sink_ref[h]
    m = jnp.maximum(jnp.max(s, axis=-1, keepdims=True), sink)
    p = jnp.exp(s - m)
    e_sink = jnp.exp(sink - m)
    inv = 1.0 / (jnp.sum(p, axis=-1, keepdims=True) + e_sink)
    return rows, p * inv, e_sink * inv


ATTN_STEP_BLOCKS = 4
BAND = (N_Q_HEADS * BLOCK, 2 * BLOCK)


def _attn_specs():
    nb = ATTN_STEP_BLOCKS
    stacked = pl.BlockSpec((4, nb * BLOCK, 128), lambda i: (0, i, 0))
    kv = [pl.BlockSpec((BLOCK, 128), lambda i: (jnp.maximum(nb * i - 1, 0), 0)), pl.BlockSpec((nb * BLOCK, 128), lambda i: (i, 0))]
    consts = [_full((2,) + BAND), pl.BlockSpec(memory_space=pltpu.SMEM)]
    return stacked, kv, consts


def _step_blocks(i, kp_ref, kc_ref, vp_ref, vc_ref, bias_ref):
    blocks = []
    for b in range(ATTN_STEP_BLOCKS):
        if b == 0:
            k2 = jnp.concatenate([kp_ref[...], kc_ref[pl.ds(0, BLOCK), :]], axis=0)
            v2 = jnp.concatenate([vp_ref[...], vc_ref[pl.ds(0, BLOCK), :]], axis=0)
            bias = bias_ref.at[jnp.where(i == 0, 1, 0)]
        else:
            k2, v2, bias = kc_ref[pl.ds((b - 1) * BLOCK, 2 * BLOCK), :], vc_ref[pl.ds((b - 1) * BLOCK, 2 * BLOCK), :], bias_ref.at[0]
        blocks.append((pl.ds(b * BLOCK, BLOCK), k2, v2, bias))
    return blocks


def _attn_fwd(qst, kn, vb, bias_st, sinks):
    s_len = kn.shape[0]

    def body(q_ref, kp_ref, kc_ref, vp_ref, vc_ref, bias_ref, sink_ref, o_ref, s_ref, p_ref):
        for b, (rows, k2, v2, bias) in enumerate(_step_blocks(pl.program_id(0), kp_ref, kc_ref, vp_ref, vc_ref, bias_ref)):
            s_b, p_b = s_ref.at[b], p_ref.at[b]
            s_b[...] = _dot(_expand_heads(q_ref[:, rows, :]), k2, 1, 1)

            def head(h, carry):
                head_rows, probs, _ = _head_softmax(s_b, bias, sink_ref, h)
                p_b[head_rows, :] = probs.astype(BF16)
                return carry

            lax.fori_loop(0, N_Q_HEADS, head, 0, unroll=True)
            o_ref[:, rows, :] = _fold_heads(_dot(p_b[...], v2, 1, 0)).astype(BF16)

    stacked, kv, consts = _attn_specs()
    return pl.pallas_call(
        body, name="attn_fwd", grid=(s_len // (ATTN_STEP_BLOCKS * BLOCK),),
        in_specs=[stacked] + kv + kv + consts, out_specs=stacked,
        out_shape=jax.ShapeDtypeStruct((4, s_len, 128), BF16),
        scratch_shapes=[pltpu.VMEM((ATTN_STEP_BLOCKS,) + BAND, F32), pltpu.VMEM((ATTN_STEP_BLOCKS,) + BAND, BF16)],
        compiler_params=_params(),
    )(qst, kn, kn, vb, vb, bias_st, sinks)


def _mix_out(u, ost, x2, wts, wpool, pool_scale, g_ffn):
    s_len = x2.shape[0]
    t = 512
    n = t + 16

    def body(u_ref, o_ref, x_ref, sl_ref, lo_ref, me_ref, wp_ref, sc_ref, g_ref, pooled_ref, mix_ref, h1_ref, hn_ref,
             w_ref, ext_ref, st_ref, sems):
        i = pl.program_id(0)

        @pl.when(i == 0)
        def _():
            _load_rows((sl_ref, lo_ref, me_ref), "out", w_ref, sems)
            ext_ref[...] = jnp.zeros_like(ext_ref)
            st_ref[...] = jnp.zeros_like(st_ref)

        u_tile = u_ref[...]
        ext_ref[pl.ds(POOL_HALO, t), :] = u_tile
        st_ref[pl.ds(8, n), :] = ext_ref[pl.ds(8, n), :] + ext_ref[pl.ds(7, n), :]
        st_ref[pl.ds(8, n), 128:] = st_ref[pl.ds(8, n), 128:] + st_ref[pl.ds(6, n), 128:]
        st_ref[pl.ds(8, n), 256:] = st_ref[pl.ds(8, n), 256:] + st_ref[pl.ds(4, n), 256:]
        st_ref[pl.ds(8, n), 384:] = st_ref[pl.ds(8, n), 384:] + st_ref[pl.ds(0, n), 384:]
        ext_ref[pl.ds(0, POOL_HALO), :] = ext_ref[pl.ds(t, POOL_HALO), :]
        pooled = (st_ref[pl.ds(POOL_HALO, t), :] / _pool_counts(i, t) - u_tile).astype(BF16)
        pooled_ref[...] = pooled
        for g in range(4):
            cols = slice(128 * g, 128 * g + 128)
            y = _dot(pooled[:, cols], wp_ref[g], 1, 0) * sc_ref[:, cols]
            mix_ref[:, ATTN_WIDTH + 128 * g:ATTN_WIDTH + 128 * g + 128] = y.astype(BF16)
        lo = lax.broadcasted_iota(jnp.int32, (t, 128), 1) < 64
        for p, pair in enumerate(_unpack_heads([o_ref[j].astype(F32) for j in range(4)], lo)):
            mix_ref[:, 128 * p:128 * p + 128] = pair.astype(BF16)
        h1 = x_ref[...] + _dot(mix_ref[...], w_ref[...], 1, 0)
        h1_ref[...] = h1
        hn_ref[...] = _rms_fwd(h1, g_ref[...]).astype(BF16)

    row = lambda w: pl.BlockSpec((t, w), lambda i: (i, 0))
    return pl.pallas_call(
        body, name="mix_out", grid=(s_len // t,),
        in_specs=[row(POOL_WIDTH), pl.BlockSpec((4, t, 128), lambda i: (0, i, 0)), row(D_MODEL)] + W_SPECS
        + [_full((4, 128, 128)), _full((1, POOL_WIDTH)), _full((1, D_MODEL))],
        out_specs=[row(POOL_WIDTH), row(D_MODEL), row(D_MODEL), row(D_MODEL)],
        out_shape=[jax.ShapeDtypeStruct((s_len, POOL_WIDTH), BF16), jax.ShapeDtypeStruct((s_len, D_MODEL), BF16),
                   jax.ShapeDtypeStruct((s_len, D_MODEL), F32), jax.ShapeDtypeStruct((s_len, D_MODEL), BF16)],
        scratch_shapes=[pltpu.VMEM((D_MODEL, D_MODEL), BF16), pltpu.VMEM((t + POOL_HALO, POOL_WIDTH), F32),
                        pltpu.VMEM((t + POOL_HALO, POOL_WIDTH), F32), pltpu.SemaphoreType.DMA((N_CHIPS,))],
        compiler_params=_params(),
    )(u, ost, x2, *wts, wpool, pool_scale, g_ffn)


def _ffn_ple(hn2, h1, p2, tgt, wts, g_ffn, g_ple):
    s_len = h1.shape[0]
    t = 256
    n_tiles = s_len // t

    def body(hn_ref, h1_ref, p_ref, tgt_ref, sl_ref, lo_ref, me_ref, gf_ref, gp_ref,
             loss_ref, dgate_ref, dup_ref, act_ref, dh2b_ref, hn3_ref, dgl_ref, dwp_ref, dh1_ref, dgf_ref, dgp_ref,
             wg_ref, wu_ref, wd_ref, wl_ref, wp_ref, packed_ref, gate_s, up_s, loss_acc, dwp_acc, sems):
        i = pl.program_id(0)

        @pl.when(i == 0)
        def _():
            w_refs = (sl_ref, lo_ref, me_ref)
            _load_rows(w_refs, "gateT", wg_ref, sems)
            _load_rows(w_refs, "upT", wu_ref, sems)
            _load_rows(w_refs, "down", wd_ref, sems)
            _load_rows(w_refs, "plg", wl_ref, sems)
            _load_rows(w_refs, "plp", packed_ref, sems)
            for j in range(N_CHIPS):
                for q in range(4):
                    wp_ref[pl.ds(64 * q, 64), 256 * j:256 * j + 256] = packed_ref[pl.ds(64 * j, 64), 256 * q:256 * q + 256]
            loss_acc[...] = jnp.zeros_like(loss_acc)
            dgf_ref[...] = jnp.zeros_like(dgf_ref)
            dgp_ref[...] = jnp.zeros_like(dgp_ref)

        hn = hn_ref[...]
        h1v = h1_ref[...]
        h2 = h1v
        for ch in range(N_FF_CHUNKS):
            rows = pl.ds(ch * FF_CHUNK, FF_CHUNK)
            gate = _dot(hn, wg_ref[rows, :], 1, 1)
            up = _dot(hn, wu_ref[rows, :], 1, 1)
            gate_s[ch] = gate
            up_s[ch] = up
            act = (gate * _sigmoid(gate) * up).astype(BF16)
            act_ref[ch] = act
            h2 = h2 + _dot(act, wd_ref[rows, :], 1, 0)
        gp = gp_ref[...]
        hn3 = _rms_fwd(h2, gp).astype(BF16)
        hn3_ref[...] = hn3
        gate2 = _sigmoid(_dot(hn3, wl_ref[...], 1, 0))
        p_tile = p_ref[...].astype(BF16)
        pp = _dot(p_tile, wp_ref[...], 1, 0)
        err = h2 + gate2 * pp - tgt_ref[...]
        loss_acc[...] += jnp.sum(err * err, axis=0, keepdims=True)
        dy = err * (1.0 / D_MODEL)
        _accumulate_tn(dwp_acc, p_tile, (dy * gate2).astype(BF16), i == 0)
        dgl = (dy * pp * gate2 * (1.0 - gate2)).astype(BF16)
        dgl_ref[...] = dgl
        dx3, dg3 = _rms_bwd(h2, gp, _dot(dgl, wl_ref[...], 1, 1))
        dh2 = dy + dx3
        dgp_ref[...] += dg3
        dh2b = dh2.astype(BF16)
        dh2b_ref[...] = dh2b
        dhn = jnp.zeros((t, D_MODEL), F32)
        for ch in range(N_FF_CHUNKS):
            rows = pl.ds(ch * FF_CHUNK, FF_CHUNK)
            dact = _dot(dh2b, wd_ref[rows, :], 1, 1)
            gate_v = gate_s[ch]
            up_v = up_s[ch]
            sg = _sigmoid(gate_v)
            dup = (dact * (gate_v * sg)).astype(BF16)
            dgate = (dact * up_v * (sg * (1.0 + gate_v * (1.0 - sg)))).astype(BF16)
            dup_ref[ch] = dup
            dgate_ref[ch] = dgate
            dhn = dhn + _dot(dgate, wg_ref[rows, :], 1, 0) + _dot(dup, wu_ref[rows, :], 1, 0)
        dx, dg = _rms_bwd(h1v, gf_ref[...], dhn)
        dh1_ref[...] = dh2 + dx
        dgf_ref[...] += dg

        @pl.when(i == n_tiles - 1)
        def _():
            total = jnp.sum(loss_acc[...], axis=-1, keepdims=True) * (0.5 / D_MODEL)
            loss_ref[...] = jnp.broadcast_to(total, loss_ref.shape)
            dwp_ref[...] = dwp_acc[...].astype(BF16)

    row = lambda w: pl.BlockSpec((t, w), lambda i: (i, 0))
    chunked = pl.BlockSpec((N_FF_CHUNKS, t, FF_CHUNK), lambda i: (0, i, 0))
    vec = _full((1, D_MODEL))
    act_shape = jax.ShapeDtypeStruct((N_FF_CHUNKS, s_len, FF_CHUNK), BF16)
    tok = lambda dtype: jax.ShapeDtypeStruct((s_len, D_MODEL), dtype)
    return pl.pallas_call(
        body, name="ffn_ple", grid=(n_tiles,),
        in_specs=[row(D_MODEL), row(D_MODEL), row(PLE_DIM), row(D_MODEL)] + W_SPECS + [vec, vec],
        out_specs=[_full((1, 128)), chunked, chunked, chunked] + [row(D_MODEL)] * 3 + [_full((PLE_DIM, D_MODEL)), row(D_MODEL),
                                                                                       vec, vec],
        out_shape=[jax.ShapeDtypeStruct((1, 128), F32), act_shape, act_shape, act_shape, tok(BF16), tok(BF16), tok(BF16),
                   jax.ShapeDtypeStruct((PLE_DIM, D_MODEL), BF16), tok(F32), jax.ShapeDtypeStruct((1, D_MODEL), F32),
                   jax.ShapeDtypeStruct((1, D_MODEL), F32)],
        scratch_shapes=[pltpu.VMEM((D_FF, D_MODEL), BF16)] * 3
        + [pltpu.VMEM((D_MODEL, D_MODEL), BF16), pltpu.VMEM((PLE_DIM, D_MODEL), BF16), pltpu.VMEM((PLE_DIM, D_MODEL), BF16),
           pltpu.VMEM((N_FF_CHUNKS, t, FF_CHUNK), F32), pltpu.VMEM((N_FF_CHUNKS, t, FF_CHUNK), F32), pltpu.VMEM((1, D_MODEL), F32),
           pltpu.VMEM((PLE_DIM, D_MODEL), F32), pltpu.SemaphoreType.DMA((N_CHIPS,))],
        compiler_params=_params(VMEM_LIMIT_BIG),
    )(hn2, h1, p2, tgt, *wts, g_ffn, g_ple)


def _accumulate_tn(acc_ref, a, b, first):
    @pl.when(first)
    def _():
        acc_ref[...] = _dot(a, b, 0, 0)

    @pl.when(jnp.logical_not(first))
    def _():
        acc_ref[...] += _dot(a, b, 0, 0)


def _flush_chunks(acc_ref, stage_ref, slab_ref, name, sems):
    stage_ref[...] = acc_ref[...].astype(BF16)
    off, rows = SLAB[name]
    copies = [pltpu.make_async_copy(stage_ref.at[pl.ds(j * rows, rows), :], slab_ref.at[j, pl.ds(off, rows), :], sems.at[j])
              for j in range(N_CHIPS)]
    for cp in copies:
        cp.start()
    for cp in copies:
        cp.wait()


def _mix_out_bwd(dh1, wts, pooled, wpool, pool_scale, mix, after):
    s_len = dh1.shape[0]
    t = 512
    n = t + 16
    n_tiles = s_len // t
    early_rows = GATHER_PARTS[0][1]

    def body(dh1_ref, sl_ref, lo_ref, me_ref, pooled_ref, wp_ref, sc_ref, mix_ref, after_ref, dost_ref, du_ref, dwp_ref,
             dsc_ref, slab_ref, w_ref, ext_ref, st_ref, acc_ref, stage_ref, sems):
        del after_ref
        i = pl.program_id(0)

        @pl.when(i == 0)
        def _():
            _load_rows((sl_ref, lo_ref, me_ref), "out", w_ref, sems)
            ext_ref[...] = jnp.zeros_like(ext_ref)
            st_ref[...] = jnp.zeros_like(st_ref)
            dsc_ref[...] = jnp.zeros_like(dsc_ref)
            dwp_ref[...] = jnp.zeros_like(dwp_ref)

        dh1b = dh1_ref[...].astype(BF16)
        _accumulate_tn(acc_ref, mix_ref[...], dh1b, i == 0)

        @pl.when(i == n_tiles - 1)
        def _():
            _flush_chunks(acc_ref, stage_ref, slab_ref, "out", sems)

        dmix = _dot(dh1b, w_ref[...], 1, 1)
        lo = lax.broadcasted_iota(jnp.int32, (t, 128), 1) < 64
        for j, entry in enumerate(_pack_heads([dmix[:, 128 * p:128 * p + 128] for p in range(4)], lo)):
            dost_ref[j] = entry.astype(BF16)
        pooled_v = pooled_ref[...]
        counts = _pool_counts(n_tiles - 1 - i, t)
        for g in range(4):
            cols = slice(128 * g, 128 * g + 128)
            dm = dmix[:, ATTN_WIDTH + 128 * g:ATTN_WIDTH + 128 * g + 128]
            ypre = _dot(pooled_v[:, cols], wp_ref[g], 1, 0)
            dsc_ref[:, cols] += jnp.sum(ypre * dm, axis=0, keepdims=True)
            dyp = (dm * sc_ref[:, cols]).astype(BF16)
            dwp_ref[g] += _dot(pooled_v[:, cols], dyp, 0, 0)
            dpooled = _dot(dyp, wp_ref[g], 1, 1)
            du_ref[:, cols] = -dpooled
            ext_ref[pl.ds(0, t), cols] = dpooled / counts[:, cols]
        st_ref[pl.ds(0, n), :] = ext_ref[pl.ds(0, n), :] + ext_ref[pl.ds(1, n), :]
        st_ref[pl.ds(0, n), 128:] = st_ref[pl.ds(0, n), 128:] + st_ref[pl.ds(2, n), 128:]
        st_ref[pl.ds(0, n), 256:] = st_ref[pl.ds(0, n), 256:] + st_ref[pl.ds(4, n), 256:]
        st_ref[pl.ds(0, n), 384:] = st_ref[pl.ds(0, n), 384:] + st_ref[pl.ds(8, n), 384:]
        ext_ref[pl.ds(t, POOL_HALO), :] = ext_ref[pl.ds(0, POOL_HALO), :]
        du_ref[...] += st_ref[pl.ds(0, t), :]

    rev = lambda w: pl.BlockSpec((t, w), lambda i: (n_tiles - 1 - i, 0))
    return pl.pallas_call(
        body, name="mix_out_bwd", grid=(n_tiles,),
        in_specs=[rev(D_MODEL)] + W_SPECS + [rev(POOL_WIDTH), _full((4, 128, 128)), _full((1, POOL_WIDTH)), rev(D_MODEL), ANY],
        out_specs=[pl.BlockSpec((4, t, 128), lambda i: (0, n_tiles - 1 - i, 0)), rev(POOL_WIDTH),
                   _full((4, 128, 128)), _full((1, POOL_WIDTH)), ANY],
        out_shape=[jax.ShapeDtypeStruct((4, s_len, 128), BF16), jax.ShapeDtypeStruct((s_len, POOL_WIDTH), F32),
                   jax.ShapeDtypeStruct((4, 128, 128), F32), jax.ShapeDtypeStruct((1, POOL_WIDTH), F32),
                   jax.ShapeDtypeStruct((N_CHIPS, early_rows, D_MODEL), BF16)],
        scratch_shapes=[pltpu.VMEM((D_MODEL, D_MODEL), BF16), pltpu.VMEM((t + POOL_HALO, POOL_WIDTH), F32),
                        pltpu.VMEM((t + POOL_HALO, POOL_WIDTH), F32), pltpu.VMEM((D_MODEL, D_MODEL), F32),
                        pltpu.VMEM((D_MODEL, D_MODEL), BF16), pltpu.SemaphoreType.DMA((N_CHIPS,))],
        compiler_params=_params(),
    )(dh1, *wts, pooled, wpool, pool_scale, mix, after)


def _attn_bwd(qst, kn, vb, dost, bias_st, sinks, after):
    s_len = kn.shape[0]

    def body(q_ref, kp_ref, kc_ref, vp_ref, vc_ref, do_ref, bias_ref, sink_ref, after_ref, dq_ref, dk_ref, dv_ref, dbias_ref,
             dsink_ref, s_ref, dp_ref, p_ref, dl_ref):
        del after_ref
        i = pl.program_id(0)

        @pl.when(i == 0)
        def _():
            dk_ref[...] = jnp.zeros_like(dk_ref)
            dv_ref[...] = jnp.zeros_like(dv_ref)
            dbias_ref[...] = jnp.zeros_like(dbias_ref)
            dsink_ref[...] = jnp.zeros_like(dsink_ref)

        for b, (rows, k2, v2, bias) in enumerate(_step_blocks(i, kp_ref, kc_ref, vp_ref, vc_ref, bias_ref)):
            s_b, dp_b, p_b, dl_b = s_ref.at[b], dp_ref.at[b], p_ref.at[b], dl_ref.at[b]
            q = _expand_heads(q_ref[:, rows, :])
            do = _expand_heads(do_ref[:, rows, :])
            s_b[...] = _dot(q, k2, 1, 1)
            dp_b[...] = _dot(do, v2, 1, 1)

            def head(h, carry):
                head_rows, probs, p_sink = _head_softmax(s_b, bias, sink_ref, h)
                dp = dp_b[head_rows, :]
                dsum = jnp.sum(probs * dp, axis=-1, keepdims=True)
                dlog = probs * (dp - dsum)
                dsink_ref[head_rows, :] -= p_sink * dsum
                dbias_ref[head_rows, :] += dlog
                p_b[head_rows, :] = probs.astype(BF16)
                dl_b[head_rows, :] = (dlog * (HEAD_DIM ** -0.5)).astype(BF16)
                return carry

            lax.fori_loop(0, N_Q_HEADS, head, 0, unroll=True)
            dlog_s = dl_b[...]
            dq_ref[:, rows, :] = _fold_heads(_dot(dlog_s, k2, 1, 0))
            dk2 = _dot(dlog_s, q, 0, 0)
            dv2 = _dot(p_b[...], do, 0, 0)
            block = ATTN_STEP_BLOCKS * i + b
            prev_rows = pl.ds(pl.multiple_of(jnp.maximum(block - 1, 0) * BLOCK, BLOCK), BLOCK)
            cur_rows = pl.ds(pl.multiple_of(block * BLOCK, BLOCK), BLOCK)
            dk_ref[prev_rows, :] += dk2[:BLOCK]
            dk_ref[cur_rows, :] += dk2[BLOCK:]
            dv_ref[prev_rows, :] += dv2[:BLOCK]
            dv_ref[cur_rows, :] += dv2[BLOCK:]

    stacked, kv, consts = _attn_specs()
    per_step = (ATTN_STEP_BLOCKS,) + BAND
    return pl.pallas_call(
        body, name="attn_bwd", grid=(s_len // (ATTN_STEP_BLOCKS * BLOCK),),
        in_specs=[stacked] + kv + kv + [stacked] + consts + [ANY],
        out_specs=[stacked, _full((s_len, 128)), _full((s_len, 128)), _full(BAND), _full((N_Q_HEADS * BLOCK, 1))],
        out_shape=[jax.ShapeDtypeStruct((4, s_len, 128), F32), jax.ShapeDtypeStruct((s_len, 128), F32),
                   jax.ShapeDtypeStruct((s_len, 128), F32), jax.ShapeDtypeStruct(BAND, F32),
                   jax.ShapeDtypeStruct((N_Q_HEADS * BLOCK, 1), F32)],
        scratch_shapes=[pltpu.VMEM(per_step, F32), pltpu.VMEM(per_step, F32), pltpu.VMEM(per_step, BF16),
                        pltpu.VMEM(per_step, BF16)],
        compiler_params=_params(),
    )(qst, kn, kn, vb, vb, dost, bias_st, sinks, after)


def _flip_rows(x):
    n = x.shape[0]
    exchange = (lax.broadcasted_iota(jnp.int32, (n, n), 0) + lax.broadcasted_iota(jnp.int32, (n, n), 1) == n - 1)
    exchange = jnp.where(exchange, 1.0, 0.0).astype(BF16)
    flipped, rest = None, x
    for _ in range(3):
        term = rest.astype(BF16)
        rest = rest - term.astype(F32)
        part = _dot(exchange, term, 1, 0)
        flipped = part if flipped is None else flipped + part
    return flipped


def _small_pack(dg_attn, dg_ffn, dg_ple, dscale, dgq, dgk, dbias, dsink_rows, loss_v):
    def body(ga_ref, gf_ref, gp_ref, sc_ref, gq_ref, gk_ref, db_ref, ds_ref, bucket_ref, loss_ref, out_ref):
        out_ref[...] = jnp.zeros((SMALL_ROWS, 128), F32)
        for name, ref, n in (("g_attn", ga_ref, 8), ("g_ffn", gf_ref, 8), ("g_ple", gp_ref, 8), ("pool_scale", sc_ref, 4)):
            for k in range(n):
                out_ref[pl.ds(SMALL[name] + k, 1), :] = ref[:, 128 * k:128 * k + 128]
        for name, ref in (("g_q", gq_ref), ("g_k", gk_ref)):
            both = ref[...]
            out_ref[pl.ds(SMALL[name], 1), :] = both + pltpu.roll(both, 64, axis=1)
        out_ref[pl.ds(SMALL["loss"], 1), :] = loss_ref[...]
        by_diagonal = lambda flipped: pltpu.roll(flipped, 0, 1, stride=1, stride_axis=0)
        bucket_of = jnp.max(by_diagonal(bucket_ref[...]), axis=0, keepdims=True)
        sums = jnp.concatenate([jnp.sum(by_diagonal(_flip_rows(db_ref[pl.ds(h * BLOCK, BLOCK), :])), axis=0, keepdims=True)
                                for h in range(N_Q_HEADS)], axis=0)
        lanes = lax.broadcasted_iota(jnp.int32, (N_Q_HEADS, 128), 1)
        lane1 = lax.broadcasted_iota(jnp.int32, (1, 128), 1)
        rb = jnp.zeros((N_Q_HEADS, 128), F32)
        for b in range(N_BUCKETS):
            rb = jnp.where(lanes == b, jnp.sum(jnp.where(bucket_of == float(b), sums, 0.0), axis=1, keepdims=True), rb)
        sk = jnp.zeros((1, 128), F32)
        for h in range(N_Q_HEADS):
            sk = jnp.where(lane1 == h, jnp.sum(ds_ref[pl.ds(h * BLOCK, BLOCK), :]), sk)
        out_ref[pl.ds(SMALL["rel_bias"], N_Q_HEADS), :] = rb
        out_ref[pl.ds(SMALL["sinks"], 1), :] = sk

    bucket = jnp.asarray(_bucket_table()[::-1].astype(np.float32))
    return pl.pallas_call(
        body, name="small_pack", in_specs=[VMEM_WHOLE] * 10, out_specs=VMEM_WHOLE,
        out_shape=jax.ShapeDtypeStruct((SMALL_ROWS, 128), F32),
    )(dg_attn, dg_ffn, dg_ple, dscale, dgq, dgk, dbias, dsink_rows, bucket, loss_v)


def _attn_in_bwd(dqst, zqk, dk, dv, du, x2, dh1, hn1, slab, wts, g_attn, gq, gk):
    s_len = x2.shape[0]
    t = 512
    n_tiles = s_len // t

    def body(dq_ref, zqk_ref, dk_ref, dv_ref, du_ref, x_ref, dh1_ref, hn_ref, slab_in_ref, sl_ref, lo_ref, me_ref, g_ref,
             gq_ref, gk_ref, dx_ref, dg_ref, dgq_ref, dgk_ref, slab_ref, w_ref, dz_ref, acc_ref, stage_ref, sems):
        del slab_in_ref
        i = pl.program_id(0)

        @pl.when(i == 0)
        def _():
            _load_rows((sl_ref, lo_ref, me_ref), "inT", w_ref, sems)
            dg_ref[...] = jnp.zeros_like(dg_ref)
            dgq_ref[...] = jnp.zeros_like(dgq_ref)
            dgk_ref[...] = jnp.zeros_like(dgk_ref)

        lo = lax.broadcasted_iota(jnp.int32, (t, 128), 1) < 64
        for p, dqn in enumerate(_unpack_heads([dq_ref[j] for j in range(4)], lo)):
            dq_raw, dgq = _pair_norm_bwd(zqk_ref[:, 128 * p:128 * p + 128], gq_ref[...], dqn)
            dz_ref[:, 128 * p:128 * p + 128] = dq_raw.astype(BF16)
            dgq_ref[...] += dgq
        dk_raw, dgk = _pair_norm_bwd(zqk_ref[:, 512:640], gk_ref[...], dk_ref[...])
        dgk_ref[...] += dgk
        dz_ref[:, 512:640] = dk_raw.astype(BF16)
        dz_ref[:, 640:768] = dv_ref[...].astype(BF16)
        dz_ref[:, 768:] = du_ref[...].astype(BF16)
        dz = dz_ref[...]
        _accumulate_tn(acc_ref, dz, hn_ref[...], i == 0)
        dx, dg = _rms_bwd(x_ref[...], g_ref[...], _dot(dz, w_ref[...], 1, 0))
        dx_ref[...] = dh1_ref[...] + dx
        dg_ref[...] += dg

        @pl.when(i == n_tiles - 1)
        def _():
            _flush_chunks(acc_ref, stage_ref, slab_ref, "inT", sems)

    row = lambda w: pl.BlockSpec((t, w), lambda i: (i, 0))
    return pl.pallas_call(
        body, name="attn_in_bwd", grid=(n_tiles,),
        in_specs=[pl.BlockSpec((4, t, 128), lambda i: (0, i, 0)), row(640), row(128), row(128), row(POOL_WIDTH),
                  row(D_MODEL), row(D_MODEL), row(D_MODEL), ANY] + W_SPECS + [_full((1, D_MODEL)), _full((1, 128)),
                                                                              _full((1, 128))],
        out_specs=[row(D_MODEL), _full((1, D_MODEL)), _full((1, 128)), _full((1, 128)), ANY],
        out_shape=[jax.ShapeDtypeStruct((s_len, D_MODEL), F32), jax.ShapeDtypeStruct((1, D_MODEL), F32),
                   jax.ShapeDtypeStruct((1, 128), F32), jax.ShapeDtypeStruct((1, 128), F32),
                   jax.ShapeDtypeStruct(slab.shape, BF16)],
        input_output_aliases={8: 4},
        scratch_shapes=[pltpu.VMEM((IN_WIDTH, D_MODEL), BF16), pltpu.VMEM((t, IN_WIDTH), BF16),
                        pltpu.VMEM((IN_WIDTH, D_MODEL), F32), pltpu.VMEM((IN_WIDTH, D_MODEL), BF16),
                        pltpu.SemaphoreType.DMA((N_CHIPS,))],
        compiler_params=_params(),
    )(dqst, zqk, dk, dv, du, x2, dh1, hn1, slab, *wts, g_attn, gq, gk)


def _dw(lefts, b, name, slab, slab_rows, row_offs):
    a0, n_a = lefts[0], len(lefts)
    assert b.shape[1] == D_MODEL
    if a0.ndim == 3:
        n_chunks, s_len, tm = a0.shape
        m = n_chunks * tm
    else:
        s_len, tm = a0.shape
        m = tm
    tk = 2048 if n_a * tm <= 1408 else 1024
    if a0.ndim == 3:
        a_spec = pl.BlockSpec((None, tk, tm), lambda i, k: (i, k, 0))
    else:
        a_spec = pl.BlockSpec((tk, tm), lambda i, k: (k, i))
    n_steps, n_tiles = s_len // tk, m // tm
    chunk = m // N_CHIPS
    per_tile = tm // chunk

    def body(*refs):
        a_refs, b_ref = refs[:n_a], refs[n_a]
        o_ref, acc_ref, stage_ref, sems = refs[-4:]
        i, k = pl.program_id(0), pl.program_id(1)
        b_tile = b_ref[...].astype(BF16)
        for w, a_ref in enumerate(a_refs):
            _accumulate_tn(acc_ref.at[w], a_ref[...].astype(BF16), b_tile, k == 0)

        def out_copies(tile, slot):
            return [pltpu.make_async_copy(stage_ref.at[slot, w, pl.ds(jj * chunk, chunk), :],
                                          o_ref.at[tile * per_tile + jj, pl.ds(row_offs[w], chunk), :], sems.at[slot, w, jj])
                    for w in range(n_a) for jj in range(per_tile)]

        @pl.when(k == n_steps - 1)
        def _():
            slot = i % 2

            @pl.when(i >= 2)
            def _():
                for cp in out_copies(i - 2, slot):
                    cp.wait()

            stage_ref[slot] = acc_ref[...].astype(BF16)
            for cp in out_copies(i, slot):
                cp.start()

            @pl.when(i == n_tiles - 1)
            def _():
                for cp in out_copies(i, slot):
                    cp.wait()
                if n_tiles > 1:
                    for cp in out_copies(i - 1, 1 - slot):
                        cp.wait()

    in_specs = [a_spec] * n_a + [pl.BlockSpec((tk, D_MODEL), lambda i, k: (k, 0))]
    operands, aliases = [*lefts, b], {}
    if slab is not None:
        in_specs.append(ANY)
        operands.append(slab)
        aliases = {n_a + 1: 0}
    return pl.pallas_call(
        body, name=name, grid=(n_tiles, n_steps), in_specs=in_specs, out_specs=ANY,
        out_shape=jax.ShapeDtypeStruct((N_CHIPS, slab_rows, D_MODEL), BF16), input_output_aliases=aliases,
        scratch_shapes=[pltpu.VMEM((n_a, tm, D_MODEL), F32), pltpu.VMEM((2, n_a, tm, D_MODEL), BF16),
                        pltpu.SemaphoreType.DMA((2, n_a, per_tile))],
        compiler_params=_params(VMEM_LIMIT_BIG, n_axes=2),
    )(*operands)


def _position():
    x, y, c = lax.axis_index("x"), lax.axis_index("y"), lax.axis_index("c")
    other_chips = [(1 - x, y), (x, 1 - y), (1 - x, 1 - y)]
    return x, y, c, other_chips


def _ag_weights(local_slab, row0, n_rows, name, collective_id):
    half = n_rows // 2
    quarter = half // 2
    assert quarter % 16 == 0

    def body(l_ref, g_ref, send, recv):
        x, y, c, chips = _position()
        me, (via_x, via_y, diagonal) = 2 * x + y, [2 * chip[0] + chip[1] for chip in chips]
        here, sibling, x_nbr, y_nbr = (x, y, c), (x, y, 1 - c), (1 - x, y, c), (x, 1 - y, c)
        peers = [sibling, x_nbr, y_nbr]
        barrier = pltpu.get_barrier_semaphore()
        for peer in peers:
            pl.semaphore_signal(barrier, inc=1, device_id=peer, device_id_type=MESH)
        pl.semaphore_wait(barrier, len(peers))

        def rows(core, part):
            start, size = (core * half, half) if part is None else (core * half + part * quarter, quarter)
            return pl.ds(pl.multiple_of(start, 16), size)

        def copy(k, chip_idx, where, to, src=None):
            dst = g_ref.at[chip_idx, where, :]
            return pltpu.make_async_remote_copy(src_ref=dst if src is None else src, dst_ref=dst, send_sem=send.at[k],
                                                recv_sem=recv.at[k], device_id=to, device_id_type=MESH)

        own_rows = l_ref.at[pl.ds(pl.multiple_of(row0 + c * half, 16), half), :]
        started = [copy(0, me, rows(c, None), x_nbr, src=own_rows), copy(1, me, rows(c, None), y_nbr, src=own_rows)]
        for cp in started:
            cp.start()
        after_arrival = [
            (copy(0, via_x, rows(c, None), here), [copy(4, via_x, rows(c, None), sibling), copy(3, via_x, rows(c, 1), y_nbr)]),
            (copy(1, via_y, rows(c, None), here), [copy(5, via_y, rows(c, None), sibling), copy(2, via_y, rows(c, 0), x_nbr)]),
            (copy(2, diagonal, rows(c, 0), here), [copy(6, diagonal, rows(c, 0), sibling)]),
            (copy(3, diagonal, rows(c, 1), here), [copy(7, diagonal, rows(c, 1), sibling)]),
        ]
        for arrival, onward in after_arrival:
            arrival.wait_recv()
            for cp in onward:
                cp.start()
            started += onward
        for cp in (copy(4, via_x, rows(1 - c, None), here), copy(5, via_y, rows(1 - c, None), here),
                   copy(6, diagonal, rows(1 - c, 0), here), copy(7, diagonal, rows(1 - c, 1), here)):
            cp.wait_recv()
        for cp in started:
            cp.wait_send()

    return pl.kernel(
        body, out_type=jax.ShapeDtypeStruct((N_CHIPS, n_rows, D_MODEL), BF16),
        mesh=plsc.ScalarSubcoreMesh(axis_name="sequencer", num_cores=1), name=name,
        scratch_types=[pltpu.SemaphoreType.DMA((8,)), pltpu.SemaphoreType.DMA((8,))],
        compiler_params=pltpu.CompilerParams(collective_id=collective_id),
    )(local_slab)


def _comm_call(body, peers_of, out_shape, n_sems, operand, name, collective_id):
    sems = [pltpu.SemaphoreType.DMA((n_sems,)), pltpu.SemaphoreType.DMA((n_sems,))]

    def with_handshake(in_ref, out_ref, send, recv):
        x, y, c, _ = _position()
        peers = peers_of(x, y, c)
        barrier = pltpu.get_barrier_semaphore()
        for peer in peers:
            pl.semaphore_signal(barrier, inc=1, device_id=peer, device_id_type=MESH)
        pl.semaphore_wait(barrier, len(peers))
        body(in_ref, out_ref, send, recv)

    return pl.kernel(with_handshake, out_type=out_shape, mesh=plsc.ScalarSubcoreMesh(axis_name="sequencer", num_cores=1),
                     name=name, scratch_types=sems, compiler_params=pltpu.CompilerParams(collective_id=collective_id))(operand)


def _rs_swap_halves(partial, name, collective_id):
    half = partial.shape[1] // 2

    def body(p_ref, r_ref, send, recv):
        x, y, c, _ = _position()
        theirs = pl.ds(pl.multiple_of((1 - c) * half, 16), half)
        cp = pltpu.make_async_remote_copy(src_ref=p_ref.at[:, theirs, :], dst_ref=r_ref, send_sem=send.at[0],
                                          recv_sem=recv.at[0], device_id=(x, y, 1 - c), device_id_type=MESH)
        cp.start()
        cp.wait()

    return _comm_call(body, lambda x, y, c: [(x, y, 1 - c)], jax.ShapeDtypeStruct((N_CHIPS, half, D_MODEL), BF16), 1,
                      partial, name, collective_id)


def _rs_add_halves(partial, other, core, name, after, small=None):
    half = other.shape[1]
    t = half // 2
    steps = half // t

    def body(core_ref, a_ref, b_ref, after_ref, *rest):
        del after_ref
        o_ref = rest[0] if small is None else rest[1]
        if small is not None:
            small_ref, _, t_ref, t_send, t_recv = rest
            start_tables, finish_tables = _gather_small(small_ref, t_ref, t_send, t_recv)
            j, i = pl.program_id(0), pl.program_id(1)
            pl.when((j == 0) & (i == 0))(start_tables)
        o_ref[...] = (a_ref[...].astype(F32) + b_ref[...].astype(F32)).astype(BF16)
        if small is not None:
            pl.when((j == N_CHIPS - 1) & (i == steps - 1))(finish_tables)

    t_in, t_out, t_scratch = _table_gather_parts(small)
    res = pl.pallas_call(
        body, name=name,
        grid_spec=pltpu.PrefetchScalarGridSpec(
            num_scalar_prefetch=1, grid=(N_CHIPS, steps),
            in_specs=[pl.BlockSpec((1, t, D_MODEL), lambda j, i, core_ref: (j, core_ref[0] * steps + i, 0)),
                      pl.BlockSpec((1, t, D_MODEL), lambda j, i, core_ref: (j, i, 0)), ANY] + t_in,
            out_specs=[pl.BlockSpec((1, t, D_MODEL), lambda j, i, core_ref: (j, i, 0))] + [ANY] * len(t_out),
            scratch_shapes=t_scratch),
        out_shape=[jax.ShapeDtypeStruct((N_CHIPS, half, D_MODEL), BF16)] + t_out,
        compiler_params=_params(n_axes=2),
    )(core, partial, other, after, *([] if small is None else [small]))
    return res[0] if small is None else res


def _rs_exchange_chips(pre, name, collective_id):
    def body(s_ref, r_ref, send, recv):
        x, y, c, chips = _position()

        def copy(k, chunk, to):
            return pltpu.make_async_remote_copy(src_ref=s_ref.at[chunk], dst_ref=r_ref.at[k], send_sem=send.at[k],
                                                recv_sem=recv.at[k], device_id=to, device_id_type=MESH)

        sends = [copy(k, 2 * chip[0] + chip[1], (*chip, c)) for k, chip in enumerate(chips)]
        for cp in sends:
            cp.start()
        for cp in sends:
            cp.wait()

    return _comm_call(body, lambda x, y, c: [(1 - x, y, c), (x, 1 - y, c), (1 - x, 1 - y, c)],
                      jax.ShapeDtypeStruct((3, pre.shape[1], D_MODEL), BF16), 3, pre, name, collective_id)


def _gather_small(s_ref, t_ref, send, recv):
    x, y, c, chips = _position()
    sibling = (x, y, 1 - c)

    def slot(px, py, pc):
        return t_ref.at[4 * px + 2 * py + pc]

    def copy(k, block, to, src=None):
        return pltpu.make_async_remote_copy(src_ref=slot(*block) if src is None else src, dst_ref=slot(*block),
                                            send_sem=send.at[k], recv_sem=recv.at[k], device_id=to, device_id_type=MESH)

    own = pltpu.make_async_copy(s_ref, slot(x, y, c), send.at[7])
    first = [copy(0, (x, y, c), sibling, src=s_ref)]
    first += [copy(1 + k, (x, y, c), (*chip, c), src=s_ref) for k, chip in enumerate(chips)]

    def start():
        own.start()
        for cp in first:
            cp.start()

    def finish():
        passed = []
        for k, chip in enumerate(chips):
            copy(1 + k, (*chip, c), (x, y, c)).wait_recv()
            fwd = copy(4 + k, (*chip, c), sibling)
            fwd.start()
            passed.append(fwd)
        copy(0, sibling, (x, y, c)).wait_recv()
        for k, chip in enumerate(chips):
            copy(4 + k, (*chip, 1 - c), (x, y, c)).wait_recv()
        for cp in first + passed:
            cp.wait_send()
        own.wait()

    return start, finish


def _table_gather_parts(small):
    if small is None:
        return [], [], []
    return [VMEM_WHOLE], [jax.ShapeDtypeStruct((N_DEV, *small.shape), F32)], [pltpu.SemaphoreType.DMA((8,))] * 2


def _rs_sum_chips(pre, received, place, name, after, small=None):
    half = pre.shape[1]
    steps = 4 if half > 512 else 2
    t = half // steps
    assert t % 16 == 0 and t * steps == half

    def body(place_ref, own_ref, r_ref, after_ref, *rest):
        del place_ref, after_ref
        if small is None:
            o_ref, stage, kept_sems, send, recv = rest
        else:
            small_ref, o_ref, t_ref, stage, kept_sems, send, recv, t_send, t_recv = rest
            start_tables, finish_tables = _gather_small(small_ref, t_ref, t_send, t_recv)
            pl.when(pl.program_id(0) == 0)(start_tables)
        i = pl.program_id(0)
        x, y, c, _ = _position()

        def rows(core, step):
            return o_ref.at[pl.ds(pl.multiple_of((core * steps + step) * t, 8), t), :]

        def kept(step):
            return pltpu.make_async_copy(stage.at[step], rows(c, step), kept_sems.at[step])

        def sent(core, step):
            return pltpu.make_async_remote_copy(src_ref=stage.at[step], dst_ref=rows(core, step), send_sem=send.at[step],
                                                recv_sem=recv.at[step], device_id=(x, y, 1 - core), device_id_type=MESH)

        acc = own_ref[0].astype(F32)
        for k in range(3):
            acc = acc + r_ref[k].astype(F32)
        stage[i] = acc
        kept(i).start()
        sent(c, i).start()

        @pl.when(i == steps - 1)
        def _():
            if small is not None:
                finish_tables()
            for step in range(steps):
                kept(step).wait()
                sent(c, step).wait_send()
                sent(1 - c, step).wait_recv()

    t_in, t_out, t_scratch = _table_gather_parts(small)
    res = pl.pallas_call(
        body, name=name,
        grid_spec=pltpu.PrefetchScalarGridSpec(
            num_scalar_prefetch=1, grid=(steps,),
            in_specs=[pl.BlockSpec((1, t, D_MODEL), lambda i, place_ref: (place_ref[0], i, 0)),
                      pl.BlockSpec((3, t, D_MODEL), lambda i, place_ref: (0, i, 0)), ANY] + t_in,
            out_specs=[ANY] * (1 + len(t_out)),
            scratch_shapes=[pltpu.VMEM((steps, t, D_MODEL), F32)] + [pltpu.SemaphoreType.DMA((steps,))] * 3 + t_scratch),
        out_shape=[jax.ShapeDtypeStruct((2 * half, D_MODEL), F32)] + t_out, compiler_params=_params(),
    )(place, pre, received, after, *([] if small is None else [small]))
    return res[0] if small is None else res


def _adam_update(w, g, m, v):
    m_new = ADAM_B1 * m + (1.0 - ADAM_B1) * g
    v_new = ADAM_B2 * v + (1.0 - ADAM_B2) * (g * g)
    m_hat = m_new / (1.0 - ADAM_B1 ** ADAM_STEP)
    v_hat = v_new / (1.0 - ADAM_B2 ** ADAM_STEP)
    return -ADAM_LR * (m_hat / (jnp.sqrt(v_hat) + ADAM_EPS) + ADAM_WD * w), m_new, v_new


def _adamw(w, g_rows, row_off, m, v, name):
    rows, cols = w.shape
    t = rows if rows <= 320 else (rows // 2 if rows % 256 else 256)

    def body(w_ref, g_ref, m_ref, v_ref, go_ref, d_ref, nm_ref, nv_ref):
        g = g_ref[...]
        go_ref[...] = g
        d_ref[...], nm_ref[...], nv_ref[...] = _adam_update(w_ref[...], g, m_ref[...], v_ref[...])

    blk = pl.BlockSpec((t, cols), lambda i: (i, 0))
    assert row_off % 8 == 0 and t % 8 == 0
    g_blk = pl.BlockSpec((pl.Element(t), pl.Element(cols)), lambda i: (pl.multiple_of(row_off + i * t, 8), 0))
    shape = jax.ShapeDtypeStruct((rows, cols), F32)
    return pl.pallas_call(
        body, name=name, grid=(rows // t,), in_specs=[blk, g_blk, blk, blk], out_specs=[blk] * 4, out_shape=[shape] * 4,
        compiler_params=_params(),
    )(w, g_rows, m, v)


SMALL_PARAMS = [("g_attn", (1, D_MODEL), 8), ("g_q", (1, HEAD_DIM), None), ("g_k", (1, HEAD_DIM), None),
                ("sinks", (1, N_Q_HEADS), None), ("rel_bias", (N_Q_HEADS, N_BUCKETS), None), ("w_pool", (512, 128), None),
                ("pool_scale", (1, POOL_WIDTH), 4), ("g_ffn", (1, D_MODEL), 8), ("g_ple", (1, D_MODEL), 8)]


def _adamw_small(tables, pool_tables, wmv):
    n_par = len(SMALL_PARAMS)

    def body(*refs):
        t_ref, p_ref = refs[:2]
        ins = refs[2:2 + 3 * n_par]
        loss_ref = refs[2 + 3 * n_par]
        outs = refs[3 + 3 * n_par:-1]
        tot_ref = refs[-1]

        def in_device_order(ref):
            total = ref[0]
            for d in range(1, N_DEV):
                total = total + ref[d]
            return total

        tot_ref[...] = in_device_order(t_ref)
        loss_ref[...] = tot_ref[pl.ds(SMALL["loss"], 1), 0:1]
        for i, (name, shape, split) in enumerate(SMALL_PARAMS):
            g_ref, d_ref, nm_ref, nv_ref = outs[4 * i:4 * i + 4]
            row = SMALL.get(name)
            if name == "w_pool":
                g_ref[...] = in_device_order(p_ref)
            elif split:
                for k in range(split):
                    g_ref[:, 128 * k:128 * k + 128] = tot_ref[pl.ds(row + k, 1), :]
            else:
                g_ref[...] = tot_ref[pl.ds(row, shape[0]), 0:shape[1]]
            w_ref, m_ref, v_ref = ins[3 * i:3 * i + 3]
            d_ref[...], nm_ref[...], nv_ref[...] = _adam_update(w_ref[...], g_ref[...], m_ref[...], v_ref[...])

    shapes = [jax.ShapeDtypeStruct((1, 1), F32)]
    for _, shape, _ in SMALL_PARAMS:
        shapes += [jax.ShapeDtypeStruct(shape, F32)] * 4
    flat = [a for triple in wmv for a in triple]
    res = pl.pallas_call(
        body, name="adamw_small", in_specs=[VMEM_WHOLE] * (2 + 3 * n_par), out_specs=[VMEM_WHOLE] * len(shapes),
        out_shape=shapes, scratch_shapes=[pltpu.VMEM((SMALL_ROWS, 128), F32)],
    )(tables, pool_tables, *flat)
    return res[0], [res[1 + 4 * i:5 + 4 * i] for i in range(n_par)]


def _pack_ple_proj(shard):
    return shard.reshape(4, 64, 256).transpose(1, 0, 2).reshape(64, D_MODEL)


class _Reduction:
    def __init__(self, tag, place, ids=(None, None)):
        self.tag, self.place, self.ids = tag, place, ids

    def start(self, partial):
        self.partial = partial
        self.other = _rs_swap_halves(partial, "rs_swap_" + self.tag, self.ids[0])
        return partial

    def middle(self, after, small=None):
        res = _rs_add_halves(self.partial, self.other, self.place[1:], "rs_add_" + self.tag, after, small)
        self.pre, self.tables = (res, None) if small is None else res
        self.received = _rs_exchange_chips(self.pre, "rs_exchange_" + self.tag, self.ids[1])
        return self.pre

    def finish(self, after, small=None):
        return _rs_sum_chips(self.pre, self.received, self.place, "rs_sum_" + self.tag, after, small)


def _local_grads(x2, p2, tgt, wts, g_attn_norm, g_q, g_k, attn_sinks, rel_bias, w_pool, pool_scale, g_ffn_norm, g_ple_norm,
                 reduce_a):
    w_early, w_late = wts
    w_in = w_out = w_early
    bucket = jnp.asarray(_bucket_table())
    gq = jnp.tile(g_q, (1, 2))
    gk = jnp.tile(g_k, (1, 2))
    wpool = w_pool[0].astype(BF16)
    sinks = attn_sinks[0]
    bias_st = _bias_build(rel_bias.T, bucket)

    hn1 = _first_norm(x2, g_attn_norm)
    zqk, u, kn, vb, qst = _attn_in(hn1, gq, gk, w_in)
    ost = _attn_fwd(qst, kn, vb, bias_st, sinks)
    pooled, mix, h1, hn2 = _mix_out(u, ost, x2, w_out, wpool, pool_scale, g_ffn_norm)
    loss_v, dgate, dup, act, dh2, hn3, dgl, dw_plp, dh1, dg_ffn, dg_ple = _ffn_ple(hn2, h1, p2, tgt, w_late, g_ffn_norm,
                                                                                      g_ple_norm)

    late0, late_rows = GATHER_PARTS[1][0], SLAB_ROWS - GATHER_PARTS[1][0]
    partial_a = None
    for names, lefts, right in ((("gateT", "upT"), [dgate, dup], hn2), (("down",), [act], dh2), (("plg",), [hn3], dgl)):
        partial_a = _dw(lefts, right, "dw_" + names[0], partial_a, late_rows, [SLAB[name][0] - late0 for name in names])
    dw_plp = dw_plp.reshape(4, 64, N_CHIPS, 256).transpose(2, 1, 0, 3).reshape(N_CHIPS, 64, D_MODEL)
    partial_a = reduce_a.start(lax.dynamic_update_slice(partial_a, dw_plp, (0, SLAB["plp"][0] - late0, 0)))
    dost, du, dw_pool, dscale, partial_b = _mix_out_bwd(dh1, w_out, pooled, wpool, pool_scale, mix, partial_a)
    pre_a = reduce_a.middle(du, dw_pool.reshape(512, 128))
    dqst, dk, dv, dbias, dsink_rows = _attn_bwd(qst, kn, vb, dost, bias_st, sinks, pre_a)
    dx, dg_attn, dgq, dgk, partial_b = _attn_in_bwd(dqst, zqk, dk, dv, du, x2, dh1, hn1, partial_b, w_in, g_attn_norm, gq, gk)

    small = _small_pack(dg_attn, dg_ffn, dg_ple, dscale, dgq, dgk, dbias, dsink_rows, loss_v)
    return dx, partial_b, small


def kernel(x, p, w_in, w_out, g_attn_norm, g_q, g_k, attn_sinks, rel_bias, w_pool, pool_scale, g_ffn_norm, w_gate, w_up, w_down, g_ple_norm, w_ple_gate, w_ple_proj, loss_target, m_w_in, m_w_out, m_g_attn_norm, m_g_q, m_g_k, m_attn_sinks, m_rel_bias, m_w_pool, m_pool_scale, m_g_ffn_norm, m_w_gate, m_w_up, m_w_down, m_g_ple_norm, m_w_ple_gate, m_w_ple_proj, v_w_in, v_w_out, v_g_attn_norm, v_g_q, v_g_k, v_attn_sinks, v_rel_bias, v_w_pool, v_pool_scale, v_g_ffn_norm, v_w_gate, v_w_up, v_w_down, v_g_ple_norm, v_w_ple_gate, v_w_ple_proj):
    core = lax.axis_index("c").astype(jnp.int32).reshape(1)
    me = (2 * lax.axis_index("x") + lax.axis_index("y")).astype(jnp.int32).reshape(1)

    local_parts = [jnp.concatenate(pieces, axis=0).astype(BF16) for pieces in (
        [w_in[0].T, w_out[0]], [w_gate[0].T, w_up[0].T, w_down[0], w_ple_gate[0], _pack_ple_proj(w_ple_proj[0])])]
    wts = [(_ag_weights(local, 0, local.shape[0], name, collective_id), local, me)
           for local, name, collective_id in zip(local_parts, ("ag_early", "ag_late"), (1, 2))]

    place = jnp.concatenate([me, core])
    reduce_a = _Reduction("a", place, ids=(3, 4))
    dx, partial_b, small = _local_grads(x[0], p[0, 0], loss_target[0], wts, g_attn_norm, g_q, g_k, attn_sinks, rel_bias,
                                        w_pool, pool_scale, g_ffn_norm, g_ple_norm, reduce_a)
    reduce_b = _Reduction("b", place, ids=(6, 7))
    reduce_b.start(partial_b)
    grads_a, small_all = reduce_a.finish(partial_b, small)
    reduce_b.middle(grads_a)

    late0 = GATHER_PARTS[1][0]

    def rows(name):
        return grads_a, SLAB[name][0] - late0

    plp_rows = grads_a[SLAB["plp"][0] - late0:]
    big = {
        "w_gate": (w_gate, m_w_gate, v_w_gate, rows("gateT"), True),
        "w_up": (w_up, m_w_up, v_w_up, rows("upT"), True),
        "w_down": (w_down, m_w_down, v_w_down, rows("down"), False),
        "w_ple_gate": (w_ple_gate, m_w_ple_gate, v_w_ple_gate, rows("plg"), False),
        "w_ple_proj": (w_ple_proj, m_w_ple_proj, v_w_ple_proj,
                       (plp_rows.reshape(64, 4, 256).transpose(1, 0, 2).reshape(PLE_DIM, PLE_DIM), 0), False),
        "w_out": (w_out, m_w_out, v_w_out, None, False),
        "w_in": (w_in, m_w_in, v_w_in, None, True),
    }
    small_params = {
        "g_attn_norm": (g_attn_norm, m_g_attn_norm, v_g_attn_norm), "g_q": (g_q, m_g_q, v_g_q), "g_k": (g_k, m_g_k, v_g_k),
        "attn_sinks": (attn_sinks, m_attn_sinks, v_attn_sinks), "rel_bias": (rel_bias.T, m_rel_bias.T, v_rel_bias.T),
        "w_pool": tuple(a.reshape(512, 128) for a in (w_pool, m_w_pool, v_w_pool)),
        "pool_scale": (pool_scale, m_pool_scale, v_pool_scale), "g_ffn_norm": (g_ffn_norm, m_g_ffn_norm, v_g_ffn_norm),
        "g_ple_norm": (g_ple_norm, m_g_ple_norm, v_g_ple_norm),
    }

    grads, deltas, new_ms, new_vs = {}, {}, {}, {}
    out = grads_b = None
    for name, (w, m, v, g_src, transposed) in big.items():
        if g_src is None:
            if grads_b is None:
                grads_b = reduce_b.finish(out[-1])
            g_src = (grads_b, SLAB["out" if name == "w_out" else "inT"][0])
        view = (lambda a: a.T) if transposed else (lambda a: a)
        out = _adamw(view(w[0]), *g_src, view(m[0]), view(v[0]), "adamw_" + name)
        grads[name], deltas[name], new_ms[name], new_vs[name] = (view(a)[None] for a in out)

    loss, small_out = _adamw_small(small_all, reduce_a.tables, list(small_params.values()))
    for name, (g2, d, nm, nv) in zip(small_params, small_out):
        restore = {"w_pool": lambda a: a.reshape(w_pool.shape), "rel_bias": lambda a: a.T}.get(name, lambda a: a)
        grads[name], deltas[name], new_ms[name], new_vs[name] = (restore(a) for a in (g2, d, nm, nv))

    order = ["w_in", "w_out", "g_attn_norm", "g_q", "g_k", "attn_sinks", "rel_bias", "w_pool", "pool_scale", "g_ffn_norm",
             "w_gate", "w_up", "w_down", "g_ple_norm", "w_ple_gate", "w_ple_proj"]
    return (loss.reshape(()), dx[None], *[grads[n] for n in order], *[deltas[n] for n in order],
            *[new_ms[n] for n in order], *[new_vs[n] for n in order])
```

```python
import numpy as np
import jax
import jax.numpy as jnp
from jax import lax
from jax.experimental import pallas as pl
from jax.experimental.pallas import tpu as pltpu
from jax.experimental.pallas import tpu_sc as plsc

F32 = jnp.float32
BF16 = jnp.bfloat16
MESH = pl.DeviceIdType.MESH

D_MODEL = 1024
HEAD_DIM = 64
N_Q_HEADS = 8
ATTN_WIDTH = 512
POOL_WIDTH = 512
IN_WIDTH = 1280
D_FF = 2816
PLE_DIM = 256
FF_CHUNK = 1408
N_FF_CHUNKS = D_FF // FF_CHUNK
BLOCK = 128
N_BUCKETS = 32
MAX_DISTANCE = 128
EPS = 1e-6
NEG = -1e30
N_CHIPS = 4
N_DEV = 8

ADAM_LR = 0.001
ADAM_B1 = 0.9
ADAM_B2 = 0.999
ADAM_EPS = 1e-08
ADAM_WD = 0.01
ADAM_STEP = 10

SLAB = {"inT": (0, 320), "out": (320, 256), "gateT": (576, 704), "upT": (1280, 704), "down": (1984, 704),
        "plg": (2688, 256), "plp": (2944, 64)}
SLAB_ROWS = 3008
GATHER_PARTS = ((0, 576), (576, SLAB_ROWS))
POOL_HALO = 24

SMALL = {"g_attn": 0, "g_ffn": 8, "g_ple": 16, "pool_scale": 24, "g_q": 28, "g_k": 29, "sinks": 30, "loss": 31,
         "rel_bias": 32}
SMALL_ROWS = 64

VMEM_LIMIT_BIG = 60 * 1024 * 1024
VMEM_LIMIT = 48 * 1024 * 1024


def _params(vmem=VMEM_LIMIT, n_axes=1):
    return pltpu.CompilerParams(dimension_semantics=("arbitrary",) * n_axes, vmem_limit_bytes=vmem)


def _dot(a, b, ca, cb):
    return lax.dot_general(a, b, (((ca,), (cb,)), ((), ())), preferred_element_type=F32)


def _full(shape):
    return pl.BlockSpec(shape, lambda i: (0,) * len(shape))


ANY = pl.BlockSpec(memory_space=pl.ANY)
VMEM_WHOLE = pl.BlockSpec(memory_space=pltpu.VMEM)


W_SPECS = [ANY, ANY, pl.BlockSpec(memory_space=pltpu.SMEM)]


def _load_rows(w_refs, name, dst_ref, sems):
    slab_ref, local_ref, me_ref = w_refs
    off, rows = SLAB[name]
    slab_off = off - max(start for start, _ in GATHER_PARTS if start <= off)
    me = me_ref[0]
    for phase in ("start", "wait"):
        for j in range(N_CHIPS):
            dst = dst_ref.at[pl.ds(j * rows, rows), :]
            theirs = pltpu.make_async_copy(slab_ref.at[j, pl.ds(slab_off, rows), :], dst, sems.at[j])
            own = pltpu.make_async_copy(local_ref.at[pl.ds(slab_off, rows), :], dst, sems.at[j])

            @pl.when(me == j)
            def _():
                getattr(own, phase)()

            @pl.when(me != j)
            def _():
                getattr(theirs, phase)()


def _rms_fwd(x, g):
    r = lax.rsqrt(jnp.mean(x * x, axis=-1, keepdims=True) + EPS)
    return x * r * g


def _rms_bwd(x, g, dy):
    r = lax.rsqrt(jnp.mean(x * x, axis=-1, keepdims=True) + EPS)
    xn = x * r
    dyg = dy * g
    dx = r * (dyg - xn * jnp.mean(dyg * xn, axis=-1, keepdims=True))
    return dx, jnp.sum(dy * xn, axis=0, keepdims=True)


def _half_sum(v, lo):
    s_lo = jnp.sum(jnp.where(lo, v, 0.0), axis=-1, keepdims=True)
    s_hi = jnp.sum(jnp.where(lo, 0.0, v), axis=-1, keepdims=True)
    return jnp.where(lo, s_lo, s_hi)


def _half_sum_mxu(v):
    upper = lax.broadcasted_iota(jnp.int32, (128, 128), 0) < 64
    left = lax.broadcasted_iota(jnp.int32, (128, 128), 1) < 64
    ones = jnp.where(upper == left, 1.0, 0.0).astype(BF16)
    high = v.astype(BF16)
    low = (v - high.astype(F32)).astype(BF16)
    return _dot(high, ones, 1, 0) + _dot(low, ones, 1, 0)


def _pair_norm(zp, g, lo):
    r = lax.rsqrt(_half_sum(zp * zp, lo) * (1.0 / HEAD_DIM) + EPS)
    return zp * r * g


def _pair_norm_bwd(zp, g, dy):
    r = lax.rsqrt(_half_sum_mxu(zp * zp) * (1.0 / HEAD_DIM) + EPS)
    xn = zp * r
    dyg = dy * g
    dx = r * (dyg - xn * (_half_sum_mxu(dyg * xn) * (1.0 / HEAD_DIM)))
    return dx, jnp.sum(dy * xn, axis=0, keepdims=True)


def _pack_heads(pairs, lo):
    packed = [None] * 4
    for m in range(2):
        a, b = pairs[m], pairs[m + 2]
        packed[2 * m] = jnp.where(lo, a, pltpu.roll(b, 64, axis=1))
        packed[2 * m + 1] = jnp.where(lo, pltpu.roll(a, 64, axis=1), b)
    return packed


def _unpack_heads(packed, lo):
    pairs = [None] * 4
    for m in range(2):
        a, b = packed[2 * m], packed[2 * m + 1]
        pairs[m] = jnp.where(lo, a, pltpu.roll(b, 64, axis=1))
        pairs[m + 2] = jnp.where(lo, pltpu.roll(a, 64, axis=1), b)
    return pairs


def _expand_heads(packed):
    flat = packed.reshape(4 * BLOCK, 128)
    lo = lax.broadcasted_iota(jnp.int32, flat.shape, 1) < 64
    zero = jnp.zeros_like(flat)
    return jnp.concatenate([jnp.where(lo, flat, zero), jnp.where(lo, zero, flat)], axis=0)


def _fold_heads(stacked):
    half = 4 * BLOCK
    lo = lax.broadcasted_iota(jnp.int32, (half, 128), 1) < 64
    return jnp.where(lo, stacked[:half], stacked[half:]).reshape(4, BLOCK, 128)


def _sigmoid(v):
    return 1.0 / (1.0 + jnp.exp(-v))


def _pool_counts(tile, n_rows):
    t1 = tile * n_rows + lax.broadcasted_iota(jnp.int32, (n_rows, POOL_WIDTH), 0) + 1
    lane = lax.broadcasted_iota(jnp.int32, (n_rows, POOL_WIDTH), 1)
    win = jnp.where(lane < 128, 2, jnp.where(lane < 256, 4, jnp.where(lane < 384, 8, 16)))
    return jnp.minimum(t1, win).astype(F32)


def _first_norm(x2, g_attn):
    s_len = x2.shape[0]
    t = 512

    def body(x_ref, g_ref, hn_ref):
        hn_ref[...] = _rms_fwd(x_ref[...], g_ref[...]).astype(BF16)

    row = pl.BlockSpec((t, D_MODEL), lambda i: (i, 0))
    return pl.pallas_call(
        body, name="first_norm", grid=(s_len // t,), in_specs=[row, _full((1, D_MODEL))], out_specs=row,
        out_shape=jax.ShapeDtypeStruct((s_len, D_MODEL), BF16), compiler_params=_params(),
    )(x2, g_attn)


def _attn_in(hn1, gq, gk, wts):
    s_len = hn1.shape[0]
    t = 512

    def body(hn_ref, gq_ref, gk_ref, sl_ref, lo_ref, me_ref, zqk_ref, u_ref, kn_ref, v_ref, qst_ref, w_ref, sems):
        @pl.when(pl.program_id(0) == 0)
        def _():
            _load_rows((sl_ref, lo_ref, me_ref), "inT", w_ref, sems)

        z = _dot(hn_ref[...], w_ref[...], 1, 1)
        zqk_ref[...] = z[:, :640]
        u_ref[...] = z[:, 768:]
        v_ref[...] = z[:, 640:768].astype(BF16)
        lo = lax.broadcasted_iota(jnp.int32, (t, 128), 1) < 64
        kn_ref[...] = _pair_norm(z[:, 512:640], gk_ref[...], lo).astype(BF16)
        pairs = [_pair_norm(z[:, 128 * p:128 * p + 128], gq_ref[...], lo) for p in range(4)]
        for j, entry in enumerate(_pack_heads(pairs, lo)):
            qst_ref[j] = entry.astype(BF16)

    row = lambda w: pl.BlockSpec((t, w), lambda i: (i, 0))
    return pl.pallas_call(
        body, name="attn_in", grid=(s_len // t,),
        in_specs=[row(D_MODEL), _full((1, 128)), _full((1, 128))] + W_SPECS,
        out_specs=[row(640), row(POOL_WIDTH), row(128), row(128), pl.BlockSpec((4, t, 128), lambda i: (0, i, 0))],
        out_shape=[jax.ShapeDtypeStruct((s_len, 640), F32), jax.ShapeDtypeStruct((s_len, POOL_WIDTH), F32),
                   jax.ShapeDtypeStruct((s_len, 128), BF16), jax.ShapeDtypeStruct((s_len, 128), BF16),
                   jax.ShapeDtypeStruct((4, s_len, 128), BF16)],
        scratch_shapes=[pltpu.VMEM((IN_WIDTH, D_MODEL), BF16), pltpu.SemaphoreType.DMA((N_CHIPS,))],
        compiler_params=_params(),
    )(hn1, gq, gk, *wts)


def _bucket_table():
    i_idx = np.arange(BLOCK)[:, None]
    j_idx = np.arange(2 * BLOCK)[None, :]
    d = BLOCK + i_idx - j_idx
    n = np.maximum(d, 0)
    max_exact = N_BUCKETS // 2
    nf = np.maximum(n, 1).astype(np.float64)
    large = max_exact + (np.log(nf / max_exact) / np.log(MAX_DISTANCE / max_exact) * (N_BUCKETS - max_exact)).astype(np.int64)
    large = np.minimum(large, N_BUCKETS - 1)
    bucket = np.where(n < max_exact, n, large)
    return np.where((d >= 0) & (d < BLOCK), bucket, -1).astype(np.int32)


def _bias_build(rel_bias_t, bucket):
    def body(rb_ref, bucket_ref, out_ref):
        bk = bucket_ref[...]
        for h in range(N_Q_HEADS):
            acc = jnp.full((BLOCK, 2 * BLOCK), NEG, F32)
            for b in range(N_BUCKETS):
                acc = jnp.where(bk == b, rb_ref[h, b], acc)
            out_ref[0, pl.ds(h * BLOCK, BLOCK), :] = acc
            out_ref[1, pl.ds(h * BLOCK, BLOCK), :] = acc
            out_ref[1, pl.ds(h * BLOCK, BLOCK), 0:BLOCK] = jnp.full((BLOCK, BLOCK), NEG, F32)

    return pl.pallas_call(
        body, name="bias_build",
        in_specs=[pl.BlockSpec(memory_space=pltpu.SMEM), VMEM_WHOLE], out_specs=VMEM_WHOLE,
        out_shape=jax.ShapeDtypeStruct((2, N_Q_HEADS * BLOCK, 2 * BLOCK), F32),
    )(rel_bias_t, bucket)


def _head_softmax(s_ref, bias_ref, sink_ref, h):
    rows = pl.ds(pl.multiple_of(h * BLOCK, BLOCK), BLOCK)
    s = s_ref[rows, :] * (HEAD_DIM ** -0.5) + bias_ref[rows, :]
    sink = sink_ref[h]
    m = jnp.maximum(jnp.max(s, axis=-1, keepdims=True), sink)
    p = jnp.exp(s - m)
    e_sink = jnp.exp(sink - m)
    inv = 1.0 / (jnp.sum(p, axis=-1, keepdims=True) + e_sink)
    return rows, p * inv, e_sink * inv


ATTN_STEP_BLOCKS = 4
BAND = (N_Q_HEADS * BLOCK, 2 * BLOCK)


def _attn_specs():
    nb = ATTN_STEP_BLOCKS
    stacked = pl.BlockSpec((4, nb * BLOCK, 128), lambda i: (0, i, 0))
    kv = [pl.BlockSpec((BLOCK, 128), lambda i: (jnp.maximum(nb * i - 1, 0), 0)), pl.BlockSpec((nb * BLOCK, 128), lambda i: (i, 0))]
    consts = [_full((2,) + BAND), pl.BlockSpec(memory_space=pltpu.SMEM)]
    return stacked, kv, consts


def _step_blocks(i, kp_ref, kc_ref, vp_ref, vc_ref, bias_ref):
    blocks = []
    for b in range(ATTN_STEP_BLOCKS):
        if b == 0:
            k2 = jnp.concatenate([kp_ref[...], kc_ref[pl.ds(0, BLOCK), :]], axis=0)
            v2 = jnp.concatenate([vp_ref[...], vc_ref[pl.ds(0, BLOCK), :]], axis=0)
            bias = bias_ref.at[jnp.where(i == 0, 1, 0)]
        else:
            k2, v2, bias = kc_ref[pl.ds((b - 1) * BLOCK, 2 * BLOCK), :], vc_ref[pl.ds((b - 1) * BLOCK, 2 * BLOCK), :], bias_ref.at[0]
        blocks.append((pl.ds(b * BLOCK, BLOCK), k2, v2, bias))
    return blocks


def _attn_fwd(qst, kn, vb, bias_st, sinks):
    s_len = kn.shape[0]

    def body(q_ref, kp_ref, kc_ref, vp_ref, vc_ref, bias_ref, sink_ref, o_ref, s_ref, p_ref):
        for b, (rows, k2, v2, bias) in enumerate(_step_blocks(pl.program_id(0), kp_ref, kc_ref, vp_ref, vc_ref, bias_ref)):
            s_b, p_b = s_ref.at[b], p_ref.at[b]
            s_b[...] = _dot(_expand_heads(q_ref[:, rows, :]), k2, 1, 1)

            def head(h, carry):
                head_rows, probs, _ = _head_softmax(s_b, bias, sink_ref, h)
                p_b[head_rows, :] = probs.astype(BF16)
                return carry

            lax.fori_loop(0, N_Q_HEADS, head, 0, unroll=True)
            o_ref[:, rows, :] = _fold_heads(_dot(p_b[...], v2, 1, 0)).astype(BF16)

    stacked, kv, consts = _attn_specs()
    return pl.pallas_call(
        body, name="attn_fwd", grid=(s_len // (ATTN_STEP_BLOCKS * BLOCK),),
        in_specs=[stacked] + kv + kv + consts, out_specs=stacked,
        out_shape=jax.ShapeDtypeStruct((4, s_len, 128), BF16),
        scratch_shapes=[pltpu.VMEM((ATTN_STEP_BLOCKS,) + BAND, F32), pltpu.VMEM((ATTN_STEP_BLOCKS,) + BAND, BF16)],
        compiler_params=_params(),
    )(qst, kn, kn, vb, vb, bias_st, sinks)


def _mix_out(u, ost, x2, wts, wpool, pool_scale, g_ffn):
    s_len = x2.shape[0]
    t = 512
    n = t + 16

    def body(u_ref, o_ref, x_ref, sl_ref, lo_ref, me_ref, wp_ref, sc_ref, g_ref, pooled_ref, mix_ref, h1_ref, hn_ref,
             w_ref, ext_ref, st_ref, sems):
        i = pl.program_id(0)

        @pl.when(i == 0)
        def _():
            _load_rows((sl_ref, lo_ref, me_ref), "out", w_ref, sems)
            ext_ref[...] = jnp.zeros_like(ext_ref)
            st_ref[...] = jnp.zeros_like(st_ref)

        u_tile = u_ref[...]
        ext_ref[pl.ds(POOL_HALO, t), :] = u_tile
        st_ref[pl.ds(8, n), :] = ext_ref[pl.ds(8, n), :] + ext_ref[pl.ds(7, n), :]
        st_ref[pl.ds(8, n), 128:] = st_ref[pl.ds(8, n), 128:] + st_ref[pl.ds(6, n), 128:]
        st_ref[pl.ds(8, n), 256:] = st_ref[pl.ds(8, n), 256:] + st_ref[pl.ds(4, n), 256:]
        st_ref[pl.ds(8, n), 384:] = st_ref[pl.ds(8, n), 384:] + st_ref[pl.ds(0, n), 384:]
        ext_ref[pl.ds(0, POOL_HALO), :] = ext_ref[pl.ds(t, POOL_HALO), :]
        pooled = (st_ref[pl.ds(POOL_HALO, t), :] / _pool_counts(i, t) - u_tile).astype(BF16)
        pooled_ref[...] = pooled
        for g in range(4):
            cols = slice(128 * g, 128 * g + 128)
            y = _dot(pooled[:, cols], wp_ref[g], 1, 0) * sc_ref[:, cols]
            mix_ref[:, ATTN_WIDTH + 128 * g:ATTN_WIDTH + 128 * g + 128] = y.astype(BF16)
        lo = lax.broadcasted_iota(jnp.int32, (t, 128), 1) < 64
        for p, pair in enumerate(_unpack_heads([o_ref[j].astype(F32) for j in range(4)], lo)):
            mix_ref[:, 128 * p:128 * p + 128] = pair.astype(BF16)
        h1 = x_ref[...] + _dot(mix_ref[...], w_ref[...], 1, 0)
        h1_ref[...] = h1
        hn_ref[...] = _rms_fwd(h1, g_ref[...]).astype(BF16)

    row = lambda w: pl.BlockSpec((t, w), lambda i: (i, 0))
    return pl.pallas_call(
        body, name="mix_out", grid=(s_len // t,),
        in_specs=[row(POOL_WIDTH), pl.BlockSpec((4, t, 128), lambda i: (0, i, 0)), row(D_MODEL)] + W_SPECS
        + [_full((4, 128, 128)), _full((1, POOL_WIDTH)), _full((1, D_MODEL))],
        out_specs=[row(POOL_WIDTH), row(D_MODEL), row(D_MODEL), row(D_MODEL)],
        out_shape=[jax.ShapeDtypeStruct((s_len, POOL_WIDTH), BF16), jax.ShapeDtypeStruct((s_len, D_MODEL), BF16),
                   jax.ShapeDtypeStruct((s_len, D_MODEL), F32), jax.ShapeDtypeStruct((s_len, D_MODEL), BF16)],
        scratch_shapes=[pltpu.VMEM((D_MODEL, D_MODEL), BF16), pltpu.VMEM((t + POOL_HALO, POOL_WIDTH), F32),
                        pltpu.VMEM((t + POOL_HALO, POOL_WIDTH), F32), pltpu.SemaphoreType.DMA((N_CHIPS,))],
        compiler_params=_params(),
    )(u, ost, x2, *wts, wpool, pool_scale, g_ffn)


def _ffn_ple(hn2, h1, p2, tgt, wts, g_ffn, g_ple):
    s_len = h1.shape[0]
    t = 256
    n_tiles = s_len // t

    def body(hn_ref, h1_ref, p_ref, tgt_ref, sl_ref, lo_ref, me_ref, gf_ref, gp_ref,
             loss_ref, dgate_ref, dup_ref, act_ref, dh2b_ref, hn3_ref, dgl_ref, dwp_ref, dh1_ref, dgf_ref, dgp_ref,
             wg_ref, wu_ref, wd_ref, wl_ref, wp_ref, packed_ref, gate_s, up_s, loss_acc, dwp_acc, sems):
        i = pl.program_id(0)

        @pl.when(i == 0)
        def _():
            w_refs = (sl_ref, lo_ref, me_ref)
            _load_rows(w_refs, "gateT", wg_ref, sems)
            _load_rows(w_refs, "upT", wu_ref, sems)
            _load_rows(w_refs, "down", wd_ref, sems)
            _load_rows(w_refs, "plg", wl_ref, sems)
            _load_rows(w_refs, "plp", packed_ref, sems)
            for j in range(N_CHIPS):
                for q in range(4):
                    wp_ref[pl.ds(64 * q, 64), 256 * j:256 * j + 256] = packed_ref[pl.ds(64 * j, 64), 256 * q:256 * q + 256]
            loss_acc[...] = jnp.zeros_like(loss_acc)
            dwp_acc[...] = jnp.zeros_like(dwp_acc)
            dgf_ref[...] = jnp.zeros_like(dgf_ref)
            dgp_ref[...] = jnp.zeros_like(dgp_ref)

        hn = hn_ref[...]
        h1v = h1_ref[...]
        h2 = h1v
        for ch in range(N_FF_CHUNKS):
            rows = pl.ds(ch * FF_CHUNK, FF_CHUNK)
            gate = _dot(hn, wg_ref[rows, :], 1, 1)
            up = _dot(hn, wu_ref[rows, :], 1, 1)
            gate_s[ch] = gate
            up_s[ch] = up
            act = (gate * _sigmoid(gate) * up).astype(BF16)
            act_ref[ch] = act
            h2 = h2 + _dot(act, wd_ref[rows, :], 1, 0)
        gp = gp_ref[...]
        hn3 = _rms_fwd(h2, gp).astype(BF16)
        hn3_ref[...] = hn3
        gate2 = _sigmoid(_dot(hn3, wl_ref[...], 1, 0))
        p_tile = p_ref[...].astype(BF16)
        pp = _dot(p_tile, wp_ref[...], 1, 0)
        err = h2 + gate2 * pp - tgt_ref[...]
        loss_acc[...] += jnp.sum(err * err, axis=0, keepdims=True)
        dy = err * (1.0 / D_MODEL)
        dwp_acc[...] += _dot(p_tile, (dy * gate2).astype(BF16), 0, 0)
        dgl = (dy * pp * gate2 * (1.0 - gate2)).astype(BF16)
        dgl_ref[...] = dgl
        dx3, dg3 = _rms_bwd(h2, gp, _dot(dgl, wl_ref[...], 1, 1))
        dh2 = dy + dx3
        dgp_ref[...] += dg3
        dh2b = dh2.astype(BF16)
        dh2b_ref[...] = dh2b
        dhn = jnp.zeros((t, D_MODEL), F32)
        for ch in range(N_FF_CHUNKS):
            rows = pl.ds(ch * FF_CHUNK, FF_CHUNK)
            dact = _dot(dh2b, wd_ref[rows, :], 1, 1)
            gate_v = gate_s[ch]
            up_v = up_s[ch]
            sg = _sigmoid(gate_v)
            dup = (dact * (gate_v * sg)).astype(BF16)
            dgate = (dact * up_v * (sg * (1.0 + gate_v * (1.0 - sg)))).astype(BF16)
            dup_ref[ch] = dup
            dgate_ref[ch] = dgate
            dhn = dhn + _dot(dgate, wg_ref[rows, :], 1, 0) + _dot(dup, wu_ref[rows, :], 1, 0)
        dx, dg = _rms_bwd(h1v, gf_ref[...], dhn)
        dh1_ref[...] = dh2 + dx
        dgf_ref[...] += dg

        @pl.when(i == n_tiles - 1)
        def _():
            total = jnp.sum(loss_acc[...], axis=-1, keepdims=True) * (0.5 / D_MODEL)
            loss_ref[...] = jnp.broadcast_to(total, loss_ref.shape)
            dwp_ref[...] = dwp_acc[...].astype(BF16)

    row = lambda w: pl.BlockSpec((t, w), lambda i: (i, 0))
    chunked = pl.BlockSpec((N_FF_CHUNKS, t, FF_CHUNK), lambda i: (0, i, 0))
    vec = _full((1, D_MODEL))
    act_shape = jax.ShapeDtypeStruct((N_FF_CHUNKS, s_len, FF_CHUNK), BF16)
    tok = lambda dtype: jax.ShapeDtypeStruct((s_len, D_MODEL), dtype)
    return pl.pallas_call(
        body, name="ffn_ple", grid=(n_tiles,),
        in_specs=[row(D_MODEL), row(D_MODEL), row(PLE_DIM), row(D_MODEL)] + W_SPECS + [vec, vec],
        out_specs=[_full((1, 128)), chunked, chunked, chunked] + [row(D_MODEL)] * 3 + [_full((PLE_DIM, D_MODEL)), row(D_MODEL),
                                                                                       vec, vec],
        out_shape=[jax.ShapeDtypeStruct((1, 128), F32), act_shape, act_shape, act_shape, tok(BF16), tok(BF16), tok(BF16),
                   jax.ShapeDtypeStruct((PLE_DIM, D_MODEL), BF16), tok(F32), jax.ShapeDtypeStruct((1, D_MODEL), F32),
                   jax.ShapeDtypeStruct((1, D_MODEL), F32)],
        scratch_shapes=[pltpu.VMEM((D_FF, D_MODEL), BF16)] * 3
        + [pltpu.VMEM((D_MODEL, D_MODEL), BF16), pltpu.VMEM((PLE_DIM, D_MODEL), BF16), pltpu.VMEM((PLE_DIM, D_MODEL), BF16),
           pltpu.VMEM((N_FF_CHUNKS, t, FF_CHUNK), F32), pltpu.VMEM((N_FF_CHUNKS, t, FF_CHUNK), F32), pltpu.VMEM((1, D_MODEL), F32),
           pltpu.VMEM((PLE_DIM, D_MODEL), F32), pltpu.SemaphoreType.DMA((N_CHIPS,))],
        compiler_params=_params(VMEM_LIMIT_BIG),
    )(hn2, h1, p2, tgt, *wts, g_ffn, g_ple)


def _accumulate_tn(acc_ref, a, b, first):
    @pl.when(first)
    def _():
        acc_ref[...] = _dot(a, b, 0, 0)

    @pl.when(jnp.logical_not(first))
    def _():
        acc_ref[...] += _dot(a, b, 0, 0)


def _flush_chunks(acc_ref, stage_ref, slab_ref, name, sems):
    stage_ref[...] = acc_ref[...].astype(BF16)
    off, rows = SLAB[name]
    copies = [pltpu.make_async_copy(stage_ref.at[pl.ds(j * rows, rows), :], slab_ref.at[j, pl.ds(off, rows), :], sems.at[j])
              for j in range(N_CHIPS)]
    for cp in copies:
        cp.start()
    for cp in copies:
        cp.wait()


def _mix_out_bwd(dh1, wts, pooled, wpool, pool_scale, mix, after):
    s_len = dh1.shape[0]
    t = 512
    n = t + 16
    n_tiles = s_len // t
    early_rows = GATHER_PARTS[0][1]

    def body(dh1_ref, sl_ref, lo_ref, me_ref, pooled_ref, wp_ref, sc_ref, mix_ref, after_ref, dost_ref, du_ref, dwp_ref,
             dsc_ref, slab_ref, w_ref, ext_ref, st_ref, acc_ref, stage_ref, sems):
        del after_ref
        i = pl.program_id(0)

        @pl.when(i == 0)
        def _():
            _load_rows((sl_ref, lo_ref, me_ref), "out", w_ref, sems)
            ext_ref[...] = jnp.zeros_like(ext_ref)
            st_ref[...] = jnp.zeros_like(st_ref)
            dsc_ref[...] = jnp.zeros_like(dsc_ref)
            dwp_ref[...] = jnp.zeros_like(dwp_ref)
            acc_ref[...] = jnp.zeros_like(acc_ref)

        dh1b = dh1_ref[...].astype(BF16)
        acc_ref[...] += _dot(mix_ref[...], dh1b, 0, 0)
        dmix = _dot(dh1b, w_ref[...], 1, 1)
        lo = lax.broadcasted_iota(jnp.int32, (t, 128), 1) < 64
        for j, entry in enumerate(_pack_heads([dmix[:, 128 * p:128 * p + 128] for p in range(4)], lo)):
            dost_ref[j] = entry.astype(BF16)
        pooled_v = pooled_ref[...]
        counts = _pool_counts(n_tiles - 1 - i, t)
        for g in range(4):
            cols = slice(128 * g, 128 * g + 128)
            dm = dmix[:, ATTN_WIDTH + 128 * g:ATTN_WIDTH + 128 * g + 128]
            ypre = _dot(pooled_v[:, cols], wp_ref[g], 1, 0)
            dsc_ref[:, cols] += jnp.sum(ypre * dm, axis=0, keepdims=True)
            dyp = (dm * sc_ref[:, cols]).astype(BF16)
            dwp_ref[g] += _dot(pooled_v[:, cols], dyp, 0, 0)
            dpooled = _dot(dyp, wp_ref[g], 1, 1)
            du_ref[:, cols] = -dpooled
            ext_ref[pl.ds(0, t), cols] = dpooled / counts[:, cols]
        st_ref[pl.ds(0, n), :] = ext_ref[pl.ds(0, n), :] + ext_ref[pl.ds(1, n), :]
        st_ref[pl.ds(0, n), 128:] = st_ref[pl.ds(0, n), 128:] + st_ref[pl.ds(2, n), 128:]
        st_ref[pl.ds(0, n), 256:] = st_ref[pl.ds(0, n), 256:] + st_ref[pl.ds(4, n), 256:]
        st_ref[pl.ds(0, n), 384:] = st_ref[pl.ds(0, n), 384:] + st_ref[pl.ds(8, n), 384:]
        ext_ref[pl.ds(t, POOL_HALO), :] = ext_ref[pl.ds(0, POOL_HALO), :]
        du_ref[...] += st_ref[pl.ds(0, t), :]

        @pl.when(i == n_tiles - 1)
        def _():
            _flush_chunks(acc_ref, stage_ref, slab_ref, "out", sems)

    rev = lambda w: pl.BlockSpec((t, w), lambda i: (n_tiles - 1 - i, 0))
    return pl.pallas_call(
        body, name="mix_out_bwd", grid=(n_tiles,),
        in_specs=[rev(D_MODEL)] + W_SPECS + [rev(POOL_WIDTH), _full((4, 128, 128)), _full((1, POOL_WIDTH)), rev(D_MODEL), ANY],
        out_specs=[pl.BlockSpec((4, t, 128), lambda i: (0, n_tiles - 1 - i, 0)), rev(POOL_WIDTH),
                   _full((4, 128, 128)), _full((1, POOL_WIDTH)), ANY],
        out_shape=[jax.ShapeDtypeStruct((4, s_len, 128), BF16), jax.ShapeDtypeStruct((s_len, POOL_WIDTH), F32),
                   jax.ShapeDtypeStruct((4, 128, 128), F32), jax.ShapeDtypeStruct((1, POOL_WIDTH), F32),
                   jax.ShapeDtypeStruct((N_CHIPS, early_rows, D_MODEL), BF16)],
        scratch_shapes=[pltpu.VMEM((D_MODEL, D_MODEL), BF16), pltpu.VMEM((t + POOL_HALO, POOL_WIDTH), F32),
                        pltpu.VMEM((t + POOL_HALO, POOL_WIDTH), F32), pltpu.VMEM((D_MODEL, D_MODEL), F32),
                        pltpu.VMEM((D_MODEL, D_MODEL), BF16), pltpu.SemaphoreType.DMA((N_CHIPS,))],
        compiler_params=_params(),
    )(dh1, *wts, pooled, wpool, pool_scale, mix, after)


def _attn_bwd(qst, kn, vb, dost, bias_st, sinks, after):
    s_len = kn.shape[0]

    def body(q_ref, kp_ref, kc_ref, vp_ref, vc_ref, do_ref, bias_ref, sink_ref, after_ref, dq_ref, dk_ref, dv_ref, dbias_ref,
             dsink_ref, s_ref, dp_ref, p_ref, dl_ref):
        del after_ref
        i = pl.program_id(0)

        @pl.when(i == 0)
        def _():
            dk_ref[...] = jnp.zeros_like(dk_ref)
            dv_ref[...] = jnp.zeros_like(dv_ref)
            dbias_ref[...] = jnp.zeros_like(dbias_ref)
            dsink_ref[...] = jnp.zeros_like(dsink_ref)

        for b, (rows, k2, v2, bias) in enumerate(_step_blocks(i, kp_ref, kc_ref, vp_ref, vc_ref, bias_ref)):
            s_b, dp_b, p_b, dl_b = s_ref.at[b], dp_ref.at[b], p_ref.at[b], dl_ref.at[b]
            q = _expand_heads(q_ref[:, rows, :])
            do = _expand_heads(do_ref[:, rows, :])
            s_b[...] = _dot(q, k2, 1, 1)
            dp_b[...] = _dot(do, v2, 1, 1)

            def head(h, carry):
                head_rows, probs, p_sink = _head_softmax(s_b, bias, sink_ref, h)
                dp = dp_b[head_rows, :]
                dsum = jnp.sum(probs * dp, axis=-1, keepdims=True)
                dlog = probs * (dp - dsum)
                dsink_ref[head_rows, :] -= p_sink * dsum
                dbias_ref[head_rows, :] += dlog
                p_b[head_rows, :] = probs.astype(BF16)
                dl_b[head_rows, :] = (dlog * (HEAD_DIM ** -0.5)).astype(BF16)
                return carry

            lax.fori_loop(0, N_Q_HEADS, head, 0, unroll=True)
            dlog_s = dl_b[...]
            dq_ref[:, rows, :] = _fold_heads(_dot(dlog_s, k2, 1, 0))
            dk2 = _dot(dlog_s, q, 0, 0)
            dv2 = _dot(p_b[...], do, 0, 0)
            block = ATTN_STEP_BLOCKS * i + b
            prev_rows = pl.ds(pl.multiple_of(jnp.maximum(block - 1, 0) * BLOCK, BLOCK), BLOCK)
            cur_rows = pl.ds(pl.multiple_of(block * BLOCK, BLOCK), BLOCK)
            dk_ref[prev_rows, :] += dk2[:BLOCK]
            dk_ref[cur_rows, :] += dk2[BLOCK:]
            dv_ref[prev_rows, :] += dv2[:BLOCK]
            dv_ref[cur_rows, :] += dv2[BLOCK:]

    stacked, kv, consts = _attn_specs()
    per_step = (ATTN_STEP_BLOCKS,) + BAND
    return pl.pallas_call(
        body, name="attn_bwd", grid=(s_len // (ATTN_STEP_BLOCKS * BLOCK),),
        in_specs=[stacked] + kv + kv + [stacked] + consts + [ANY],
        out_specs=[stacked, _full((s_len, 128)), _full((s_len, 128)), _full(BAND), _full((N_Q_HEADS * BLOCK, 1))],
        out_shape=[jax.ShapeDtypeStruct((4, s_len, 128), F32), jax.ShapeDtypeStruct((s_len, 128), F32),
                   jax.ShapeDtypeStruct((s_len, 128), F32), jax.ShapeDtypeStruct(BAND, F32),
                   jax.ShapeDtypeStruct((N_Q_HEADS * BLOCK, 1), F32)],
        scratch_shapes=[pltpu.VMEM(per_step, F32), pltpu.VMEM(per_step, F32), pltpu.VMEM(per_step, BF16),
                        pltpu.VMEM(per_step, BF16)],
        compiler_params=_params(),
    )(qst, kn, kn, vb, vb, dost, bias_st, sinks, after)


def _flip_rows(x):
    n = x.shape[0]
    exchange = (lax.broadcasted_iota(jnp.int32, (n, n), 0) + lax.broadcasted_iota(jnp.int32, (n, n), 1) == n - 1)
    exchange = jnp.where(exchange, 1.0, 0.0).astype(BF16)
    flipped, rest = None, x
    for _ in range(3):
        term = rest.astype(BF16)
        rest = rest - term.astype(F32)
        part = _dot(exchange, term, 1, 0)
        flipped = part if flipped is None else flipped + part
    return flipped


def _small_pack(dg_attn, dg_ffn, dg_ple, dscale, dgq, dgk, dbias, dsink_rows, loss_v):
    def body(ga_ref, gf_ref, gp_ref, sc_ref, gq_ref, gk_ref, db_ref, ds_ref, bucket_ref, loss_ref, out_ref):
        out_ref[...] = jnp.zeros((SMALL_ROWS, 128), F32)
        for name, ref, n in (("g_attn", ga_ref, 8), ("g_ffn", gf_ref, 8), ("g_ple", gp_ref, 8), ("pool_scale", sc_ref, 4)):
            for k in range(n):
                out_ref[pl.ds(SMALL[name] + k, 1), :] = ref[:, 128 * k:128 * k + 128]
        for name, ref in (("g_q", gq_ref), ("g_k", gk_ref)):
            both = ref[...]
            out_ref[pl.ds(SMALL[name], 1), :] = both + pltpu.roll(both, 64, axis=1)
        out_ref[pl.ds(SMALL["loss"], 1), :] = loss_ref[...]
        by_diagonal = lambda flipped: pltpu.roll(flipped, 0, 1, stride=1, stride_axis=0)
        bucket_of = jnp.max(by_diagonal(bucket_ref[...]), axis=0, keepdims=True)
        sums = jnp.concatenate([jnp.sum(by_diagonal(_flip_rows(db_ref[pl.ds(h * BLOCK, BLOCK), :])), axis=0, keepdims=True)
                                for h in range(N_Q_HEADS)], axis=0)
        lanes = lax.broadcasted_iota(jnp.int32, (N_Q_HEADS, 128), 1)
        lane1 = lax.broadcasted_iota(jnp.int32, (1, 128), 1)
        rb = jnp.zeros((N_Q_HEADS, 128), F32)
        for b in range(N_BUCKETS):
            rb = jnp.where(lanes == b, jnp.sum(jnp.where(bucket_of == float(b), sums, 0.0), axis=1, keepdims=True), rb)
        sk = jnp.zeros((1, 128), F32)
        for h in range(N_Q_HEADS):
            sk = jnp.where(lane1 == h, jnp.sum(ds_ref[pl.ds(h * BLOCK, BLOCK), :]), sk)
        out_ref[pl.ds(SMALL["rel_bias"], N_Q_HEADS), :] = rb
        out_ref[pl.ds(SMALL["sinks"], 1), :] = sk

    bucket = jnp.asarray(_bucket_table()[::-1].astype(np.float32))
    return pl.pallas_call(
        body, name="small_pack", in_specs=[VMEM_WHOLE] * 10, out_specs=VMEM_WHOLE,
        out_shape=jax.ShapeDtypeStruct((SMALL_ROWS, 128), F32),
    )(dg_attn, dg_ffn, dg_ple, dscale, dgq, dgk, dbias, dsink_rows, bucket, loss_v)


def _attn_in_bwd(dqst, zqk, dk, dv, du, x2, dh1, hn1, slab, wts, g_attn, gq, gk):
    s_len = x2.shape[0]
    t = 512
    n_tiles = s_len // t

    def body(dq_ref, zqk_ref, dk_ref, dv_ref, du_ref, x_ref, dh1_ref, hn_ref, slab_in_ref, sl_ref, lo_ref, me_ref, g_ref,
             gq_ref, gk_ref, dx_ref, dg_ref, dgq_ref, dgk_ref, slab_ref, w_ref, dz_ref, acc_ref, stage_ref, sems):
        del slab_in_ref
        i = pl.program_id(0)

        @pl.when(i == 0)
        def _():
            _load_rows((sl_ref, lo_ref, me_ref), "inT", w_ref, sems)
            dg_ref[...] = jnp.zeros_like(dg_ref)
            dgq_ref[...] = jnp.zeros_like(dgq_ref)
            dgk_ref[...] = jnp.zeros_like(dgk_ref)
            acc_ref[...] = jnp.zeros_like(acc_ref)

        lo = lax.broadcasted_iota(jnp.int32, (t, 128), 1) < 64
        for p, dqn in enumerate(_unpack_heads([dq_ref[j] for j in range(4)], lo)):
            dq_raw, dgq = _pair_norm_bwd(zqk_ref[:, 128 * p:128 * p + 128], gq_ref[...], dqn)
            dz_ref[:, 128 * p:128 * p + 128] = dq_raw.astype(BF16)
            dgq_ref[...] += dgq
        dk_raw, dgk = _pair_norm_bwd(zqk_ref[:, 512:640], gk_ref[...], dk_ref[...])
        dgk_ref[...] += dgk
        dz_ref[:, 512:640] = dk_raw.astype(BF16)
        dz_ref[:, 640:768] = dv_ref[...].astype(BF16)
        dz_ref[:, 768:] = du_ref[...].astype(BF16)
        dz = dz_ref[...]
        acc_ref[...] += _dot(dz, hn_ref[...], 0, 0)
        dx, dg = _rms_bwd(x_ref[...], g_ref[...], _dot(dz, w_ref[...], 1, 0))
        dx_ref[...] = dh1_ref[...] + dx
        dg_ref[...] += dg

        @pl.when(i == n_tiles - 1)
        def _():
            _flush_chunks(acc_ref, stage_ref, slab_ref, "inT", sems)

    row = lambda w: pl.BlockSpec((t, w), lambda i: (i, 0))
    return pl.pallas_call(
        body, name="attn_in_bwd", grid=(n_tiles,),
        in_specs=[pl.BlockSpec((4, t, 128), lambda i: (0, i, 0)), row(640), row(128), row(128), row(POOL_WIDTH),
                  row(D_MODEL), row(D_MODEL), row(D_MODEL), ANY] + W_SPECS + [_full((1, D_MODEL)), _full((1, 128)),
                                                                              _full((1, 128))],
        out_specs=[row(D_MODEL), _full((1, D_MODEL)), _full((1, 128)), _full((1, 128)), ANY],
        out_shape=[jax.ShapeDtypeStruct((s_len, D_MODEL), F32), jax.ShapeDtypeStruct((1, D_MODEL), F32),
                   jax.ShapeDtypeStruct((1, 128), F32), jax.ShapeDtypeStruct((1, 128), F32),
                   jax.ShapeDtypeStruct(slab.shape, BF16)],
        input_output_aliases={8: 4},
        scratch_shapes=[pltpu.VMEM((IN_WIDTH, D_MODEL), BF16), pltpu.VMEM((t, IN_WIDTH), BF16),
                        pltpu.VMEM((IN_WIDTH, D_MODEL), F32), pltpu.VMEM((IN_WIDTH, D_MODEL), BF16),
                        pltpu.SemaphoreType.DMA((N_CHIPS,))],
        compiler_params=_params(),
    )(dqst, zqk, dk, dv, du, x2, dh1, hn1, slab, *wts, g_attn, gq, gk)


def _dw(lefts, b, name, slab, slab_rows, row_offs):
    a0, n_a = lefts[0], len(lefts)
    assert b.shape[1] == D_MODEL
    if a0.ndim == 3:
        n_chunks, s_len, tm = a0.shape
        m = n_chunks * tm
    else:
        s_len, tm = a0.shape
        m = tm
    tk = 2048 if n_a * tm <= 1408 else 1024
    if a0.ndim == 3:
        a_spec = pl.BlockSpec((None, tk, tm), lambda i, k: (i, k, 0))
    else:
        a_spec = pl.BlockSpec((tk, tm), lambda i, k: (k, i))
    n_steps, n_tiles = s_len // tk, m // tm
    chunk = m // N_CHIPS
    per_tile = tm // chunk

    def body(*refs):
        a_refs, b_ref = refs[:n_a], refs[n_a]
        o_ref, acc_ref, stage_ref, sems = refs[-4:]
        i, k = pl.program_id(0), pl.program_id(1)
        b_tile = b_ref[...].astype(BF16)
        for w, a_ref in enumerate(a_refs):
            _accumulate_tn(acc_ref.at[w], a_ref[...].astype(BF16), b_tile, k == 0)

        def out_copies(tile, slot):
            return [pltpu.make_async_copy(stage_ref.at[slot, w, pl.ds(jj * chunk, chunk), :],
                                          o_ref.at[tile * per_tile + jj, pl.ds(row_offs[w], chunk), :], sems.at[slot, w, jj])
                    for w in range(n_a) for jj in range(per_tile)]

        @pl.when(k == n_steps - 1)
        def _():
            slot = i % 2

            @pl.when(i >= 2)
            def _():
                for cp in out_copies(i - 2, slot):
                    cp.wait()

            stage_ref[slot] = acc_ref[...].astype(BF16)
            for cp in out_copies(i, slot):
                cp.start()

            @pl.when(i == n_tiles - 1)
            def _():
                for cp in out_copies(i, slot):
                    cp.wait()
                if n_tiles > 1:
                    for cp in out_copies(i - 1, 1 - slot):
                        cp.wait()

    in_specs = [a_spec] * n_a + [pl.BlockSpec((tk, D_MODEL), lambda i, k: (k, 0))]
    operands, aliases = [*lefts, b], {}
    if slab is not None:
        in_specs.append(ANY)
        operands.append(slab)
        aliases = {n_a + 1: 0}
    return pl.pallas_call(
        body, name=name, grid=(n_tiles, n_steps), in_specs=in_specs, out_specs=ANY,
        out_shape=jax.ShapeDtypeStruct((N_CHIPS, slab_rows, D_MODEL), BF16), input_output_aliases=aliases,
        scratch_shapes=[pltpu.VMEM((n_a, tm, D_MODEL), F32), pltpu.VMEM((2, n_a, tm, D_MODEL), BF16),
                        pltpu.SemaphoreType.DMA((2, n_a, per_tile))],
        compiler_params=_params(VMEM_LIMIT_BIG, n_axes=2),
    )(*operands)


def _position():
    x, y, c = lax.axis_index("x"), lax.axis_index("y"), lax.axis_index("c")
    other_chips = [(1 - x, y), (x, 1 - y), (1 - x, 1 - y)]
    return x, y, c, other_chips


def _ag_weights(local_slab, row0, n_rows, name, collective_id):
    half = n_rows // 2
    quarter = half // 2
    assert quarter % 16 == 0

    def body(l_ref, g_ref, send, recv):
        x, y, c, chips = _position()
        me, (via_x, via_y, diagonal) = 2 * x + y, [2 * chip[0] + chip[1] for chip in chips]
        here, sibling, x_nbr, y_nbr = (x, y, c), (x, y, 1 - c), (1 - x, y, c), (x, 1 - y, c)
        peers = [sibling, x_nbr, y_nbr]
        barrier = pltpu.get_barrier_semaphore()
        for peer in peers:
            pl.semaphore_signal(barrier, inc=1, device_id=peer, device_id_type=MESH)
        pl.semaphore_wait(barrier, len(peers))

        def rows(core, part):
            start, size = (core * half, half) if part is None else (core * half + part * quarter, quarter)
            return pl.ds(pl.multiple_of(start, 16), size)

        def copy(k, chip_idx, where, to, src=None):
            dst = g_ref.at[chip_idx, where, :]
            return pltpu.make_async_remote_copy(src_ref=dst if src is None else src, dst_ref=dst, send_sem=send.at[k],
                                                recv_sem=recv.at[k], device_id=to, device_id_type=MESH)

        own_rows = l_ref.at[pl.ds(pl.multiple_of(row0 + c * half, 16), half), :]
        started = [copy(0, me, rows(c, None), x_nbr, src=own_rows), copy(1, me, rows(c, None), y_nbr, src=own_rows)]
        for cp in started:
            cp.start()
        after_arrival = [
            (copy(0, via_x, rows(c, None), here), [copy(4, via_x, rows(c, None), sibling), copy(3, via_x, rows(c, 1), y_nbr)]),
            (copy(1, via_y, rows(c, None), here), [copy(5, via_y, rows(c, None), sibling), copy(2, via_y, rows(c, 0), x_nbr)]),
            (copy(2, diagonal, rows(c, 0), here), [copy(6, diagonal, rows(c, 0), sibling)]),
            (copy(3, diagonal, rows(c, 1), here), [copy(7, diagonal, rows(c, 1), sibling)]),
        ]
        for arrival, onward in after_arrival:
            arrival.wait_recv()
            for cp in onward:
                cp.start()
            started += onward
        for cp in (copy(4, via_x, rows(1 - c, None), here), copy(5, via_y, rows(1 - c, None), here),
                   copy(6, diagonal, rows(1 - c, 0), here), copy(7, diagonal, rows(1 - c, 1), here)):
            cp.wait_recv()
        for cp in started:
            cp.wait_send()

    return pl.kernel(
        body, out_type=jax.ShapeDtypeStruct((N_CHIPS, n_rows, D_MODEL), BF16),
        mesh=plsc.ScalarSubcoreMesh(axis_name="sequencer", num_cores=1), name=name,
        scratch_types=[pltpu.SemaphoreType.DMA((8,)), pltpu.SemaphoreType.DMA((8,))],
        compiler_params=pltpu.CompilerParams(collective_id=collective_id),
    )(local_slab)


def _comm_call(body, peers_of, out_shape, n_sems, operand, name, collective_id):
    sems = [pltpu.SemaphoreType.DMA((n_sems,)), pltpu.SemaphoreType.DMA((n_sems,))]

    def with_handshake(in_ref, out_ref, send, recv):
        x, y, c, _ = _position()
        peers = peers_of(x, y, c)
        barrier = pltpu.get_barrier_semaphore()
        for peer in peers:
            pl.semaphore_signal(barrier, inc=1, device_id=peer, device_id_type=MESH)
        pl.semaphore_wait(barrier, len(peers))
        body(in_ref, out_ref, send, recv)

    return pl.kernel(with_handshake, out_type=out_shape, mesh=plsc.ScalarSubcoreMesh(axis_name="sequencer", num_cores=1),
                     name=name, scratch_types=sems, compiler_params=pltpu.CompilerParams(collective_id=collective_id))(operand)


def _rs_swap_halves(partial, name, collective_id):
    half = partial.shape[1] // 2

    def body(p_ref, r_ref, send, recv):
        x, y, c, _ = _position()
        theirs = pl.ds(pl.multiple_of((1 - c) * half, 16), half)
        cp = pltpu.make_async_remote_copy(src_ref=p_ref.at[:, theirs, :], dst_ref=r_ref, send_sem=send.at[0],
                                          recv_sem=recv.at[0], device_id=(x, y, 1 - c), device_id_type=MESH)
        cp.start()
        cp.wait()

    return _comm_call(body, lambda x, y, c: [(x, y, 1 - c)], jax.ShapeDtypeStruct((N_CHIPS, half, D_MODEL), BF16), 1,
                      partial, name, collective_id)


def _rs_add_halves(partial, other, core, name, after, small=None):
    half = other.shape[1]
    t = half // 2
    steps = half // t

    def body(core_ref, a_ref, b_ref, after_ref, *rest):
        del after_ref
        o_ref = rest[0] if small is None else rest[1]
        if small is not None:
            small_ref, _, t_ref, t_send, t_recv = rest
            start_tables, finish_tables = _gather_small(small_ref, t_ref, t_send, t_recv)
            j, i = pl.program_id(0), pl.program_id(1)
            pl.when((j == 0) & (i == 0))(start_tables)
        o_ref[...] = (a_ref[...].astype(F32) + b_ref[...].astype(F32)).astype(BF16)
        if small is not None:
            pl.when((j == N_CHIPS - 1) & (i == steps - 1))(finish_tables)

    t_in, t_out, t_scratch = _table_gather_parts(small)
    res = pl.pallas_call(
        body, name=name,
        grid_spec=pltpu.PrefetchScalarGridSpec(
            num_scalar_prefetch=1, grid=(N_CHIPS, steps),
            in_specs=[pl.BlockSpec((1, t, D_MODEL), lambda j, i, core_ref: (j, core_ref[0] * steps + i, 0)),
                      pl.BlockSpec((1, t, D_MODEL), lambda j, i, core_ref: (j, i, 0)), ANY] + t_in,
            out_specs=[pl.BlockSpec((1, t, D_MODEL), lambda j, i, core_ref: (j, i, 0))] + [ANY] * len(t_out),
            scratch_shapes=t_scratch),
        out_shape=[jax.ShapeDtypeStruct((N_CHIPS, half, D_MODEL), BF16)] + t_out,
        compiler_params=_params(n_axes=2),
    )(core, partial, other, after, *([] if small is None else [small]))
    return res[0] if small is None else res


def _rs_exchange_chips(pre, name, collective_id):
    def body(s_ref, r_ref, send, recv):
        x, y, c, chips = _position()

        def copy(k, chunk, to):
            return pltpu.make_async_remote_copy(src_ref=s_ref.at[chunk], dst_ref=r_ref.at[k], send_sem=send.at[k],
                                                recv_sem=recv.at[k], device_id=to, device_id_type=MESH)

        sends = [copy(k, 2 * chip[0] + chip[1], (*chip, c)) for k, chip in enumerate(chips)]
        for cp in sends:
            cp.start()
        for cp in sends:
            cp.wait()

    return _comm_call(body, lambda x, y, c: [(1 - x, y, c), (x, 1 - y, c), (1 - x, 1 - y, c)],
                      jax.ShapeDtypeStruct((3, pre.shape[1], D_MODEL), BF16), 3, pre, name, collective_id)


def _gather_small(s_ref, t_ref, send, recv):
    x, y, c, chips = _position()
    sibling = (x, y, 1 - c)

    def slot(px, py, pc):
        return t_ref.at[4 * px + 2 * py + pc]

    def copy(k, block, to, src=None):
        return pltpu.make_async_remote_copy(src_ref=slot(*block) if src is None else src, dst_ref=slot(*block),
                                            send_sem=send.at[k], recv_sem=recv.at[k], device_id=to, device_id_type=MESH)

    own = pltpu.make_async_copy(s_ref, slot(x, y, c), send.at[7])
    first = [copy(0, (x, y, c), sibling, src=s_ref)]
    first += [copy(1 + k, (x, y, c), (*chip, c), src=s_ref) for k, chip in enumerate(chips)]

    def start():
        own.start()
        for cp in first:
            cp.start()

    def finish():
        passed = []
        for k, chip in enumerate(chips):
            copy(1 + k, (*chip, c), (x, y, c)).wait_recv()
            fwd = copy(4 + k, (*chip, c), sibling)
            fwd.start()
            passed.append(fwd)
        copy(0, sibling, (x, y, c)).wait_recv()
        for k, chip in enumerate(chips):
            copy(4 + k, (*chip, 1 - c), (x, y, c)).wait_recv()
        for cp in first + passed:
            cp.wait_send()
        own.wait()

    return start, finish


def _table_gather_parts(small):
    if small is None:
        return [], [], []
    return [VMEM_WHOLE], [jax.ShapeDtypeStruct((N_DEV, *small.shape), F32)], [pltpu.SemaphoreType.DMA((8,))] * 2


def _rs_sum_chips(pre, received, place, name, after, small=None):
    half = pre.shape[1]
    steps = 4 if half > 512 else 2
    t = half // steps
    assert t % 16 == 0 and t * steps == half

    def body(place_ref, own_ref, r_ref, after_ref, *rest):
        del place_ref, after_ref
        if small is None:
            o_ref, stage, kept_sems, send, recv = rest
        else:
            small_ref, o_ref, t_ref, stage, kept_sems, send, recv, t_send, t_recv = rest
            start_tables, finish_tables = _gather_small(small_ref, t_ref, t_send, t_recv)
            pl.when(pl.program_id(0) == 0)(start_tables)
        i = pl.program_id(0)
        x, y, c, _ = _position()

        def rows(core, step):
            return o_ref.at[pl.ds(pl.multiple_of((core * steps + step) * t, 8), t), :]

        def kept(step):
            return pltpu.make_async_copy(stage.at[step], rows(c, step), kept_sems.at[step])

        def sent(core, step):
            return pltpu.make_async_remote_copy(src_ref=stage.at[step], dst_ref=rows(core, step), send_sem=send.at[step],
                                                recv_sem=recv.at[step], device_id=(x, y, 1 - core), device_id_type=MESH)

        acc = own_ref[0].astype(F32)
        for k in range(3):
            acc = acc + r_ref[k].astype(F32)
        stage[i] = acc
        kept(i).start()
        sent(c, i).start()

        @pl.when(i == steps - 1)
        def _():
            if small is not None:
                finish_tables()
            for step in range(steps):
                kept(step).wait()
                sent(c, step).wait_send()
                sent(1 - c, step).wait_recv()

    t_in, t_out, t_scratch = _table_gather_parts(small)
    res = pl.pallas_call(
        body, name=name,
        grid_spec=pltpu.PrefetchScalarGridSpec(
            num_scalar_prefetch=1, grid=(steps,),
            in_specs=[pl.BlockSpec((1, t, D_MODEL), lambda i, place_ref: (place_ref[0], i, 0)),
                      pl.BlockSpec((3, t, D_MODEL), lambda i, place_ref: (0, i, 0)), ANY] + t_in,
            out_specs=[ANY] * (1 + len(t_out)),
            scratch_shapes=[pltpu.VMEM((steps, t, D_MODEL), F32)] + [pltpu.SemaphoreType.DMA((steps,))] * 3 + t_scratch),
        out_shape=[jax.ShapeDtypeStruct((2 * half, D_MODEL), F32)] + t_out, compiler_params=_params(),
    )(place, pre, received, after, *([] if small is None else [small]))
    return res[0] if small is None else res


def _adam_update(w, g, m, v):
    m_new = ADAM_B1 * m + (1.0 - ADAM_B1) * g
    v_new = ADAM_B2 * v + (1.0 - ADAM_B2) * (g * g)
    m_hat = m_new / (1.0 - ADAM_B1 ** ADAM_STEP)
    v_hat = v_new / (1.0 - ADAM_B2 ** ADAM_STEP)
    return -ADAM_LR * (m_hat / (jnp.sqrt(v_hat) + ADAM_EPS) + ADAM_WD * w), m_new, v_new


def _adamw(w, g_rows, row_off, m, v, name):
    rows, cols = w.shape
    t = rows if rows <= 320 else (rows // 2 if rows % 256 else 256)

    def body(w_ref, g_ref, m_ref, v_ref, go_ref, d_ref, nm_ref, nv_ref):
        g = g_ref[...]
        go_ref[...] = g
        d_ref[...], nm_ref[...], nv_ref[...] = _adam_update(w_ref[...], g, m_ref[...], v_ref[...])

    blk = pl.BlockSpec((t, cols), lambda i: (i, 0))
    assert row_off % 8 == 0 and t % 8 == 0
    g_blk = pl.BlockSpec((pl.Element(t), pl.Element(cols)), lambda i: (pl.multiple_of(row_off + i * t, 8), 0))
    shape = jax.ShapeDtypeStruct((rows, cols), F32)
    return pl.pallas_call(
        body, name=name, grid=(rows // t,), in_specs=[blk, g_blk, blk, blk], out_specs=[blk] * 4, out_shape=[shape] * 4,
        compiler_params=_params(),
    )(w, g_rows, m, v)


SMALL_PARAMS = [("g_attn", (1, D_MODEL), 8), ("g_q", (1, HEAD_DIM), None), ("g_k", (1, HEAD_DIM), None),
                ("sinks", (1, N_Q_HEADS), None), ("rel_bias", (N_Q_HEADS, N_BUCKETS), None), ("w_pool", (512, 128), None),
                ("pool_scale", (1, POOL_WIDTH), 4), ("g_ffn", (1, D_MODEL), 8), ("g_ple", (1, D_MODEL), 8)]


def _adamw_small(tables, pool_tables, wmv):
    n_par = len(SMALL_PARAMS)

    def body(*refs):
        t_ref, p_ref = refs[:2]
        ins = refs[2:2 + 3 * n_par]
        loss_ref = refs[2 + 3 * n_par]
        outs = refs[3 + 3 * n_par:-1]
        tot_ref = refs[-1]

        def in_device_order(ref):
            total = ref[0]
            for d in range(1, N_DEV):
                total = total + ref[d]
            return total

        tot_ref[...] = in_device_order(t_ref)
        loss_ref[...] = tot_ref[pl.ds(SMALL["loss"], 1), 0:1]
        for i, (name, shape, split) in enumerate(SMALL_PARAMS):
            g_ref, d_ref, nm_ref, nv_ref = outs[4 * i:4 * i + 4]
            row = SMALL.get(name)
            if name == "w_pool":
                g_ref[...] = in_device_order(p_ref)
            elif split:
                for k in range(split):
                    g_ref[:, 128 * k:128 * k + 128] = tot_ref[pl.ds(row + k, 1), :]
            else:
                g_ref[...] = tot_ref[pl.ds(row, shape[0]), 0:shape[1]]
            w_ref, m_ref, v_ref = ins[3 * i:3 * i + 3]
            d_ref[...], nm_ref[...], nv_ref[...] = _adam_update(w_ref[...], g_ref[...], m_ref[...], v_ref[...])

    shapes = [jax.ShapeDtypeStruct((1, 1), F32)]
    for _, shape, _ in SMALL_PARAMS:
        shapes += [jax.ShapeDtypeStruct(shape, F32)] * 4
    flat = [a for triple in wmv for a in triple]
    res = pl.pallas_call(
        body, name="adamw_small", in_specs=[VMEM_WHOLE] * (2 + 3 * n_par), out_specs=[VMEM_WHOLE] * len(shapes),
        out_shape=shapes, scratch_shapes=[pltpu.VMEM((SMALL_ROWS, 128), F32)],
    )(tables, pool_tables, *flat)
    return res[0], [res[1 + 4 * i:5 + 4 * i] for i in range(n_par)]


def _pack_ple_proj(shard):
    return shard.reshape(4, 64, 256).transpose(1, 0, 2).reshape(64, D_MODEL)


class _Reduction:
    def __init__(self, tag, place, ids=(None, None)):
        self.tag, self.place, self.ids = tag, place, ids

    def start(self, partial):
        self.partial = partial
        self.other = _rs_swap_halves(partial, "rs_swap_" + self.tag, self.ids[0])
        return partial

    def middle(self, after, small=None):
        res = _rs_add_halves(self.partial, self.other, self.place[1:], "rs_add_" + self.tag, after, small)
        self.pre, self.tables = (res, None) if small is None else res
        self.received = _rs_exchange_chips(self.pre, "rs_exchange_" + self.tag, self.ids[1])
        return self.pre

    def finish(self, after, small=None):
        return _rs_sum_chips(self.pre, self.received, self.place, "rs_sum_" + self.tag, after, small)


def _local_grads(x2, p2, tgt, wts, g_attn_norm, g_q, g_k, attn_sinks, rel_bias, w_pool, pool_scale, g_ffn_norm, g_ple_norm,
                 reduce_a):
    w_early, w_late = wts
    w_in = w_out = w_early
    bucket = jnp.asarray(_bucket_table())
    gq = jnp.tile(g_q, (1, 2))
    gk = jnp.tile(g_k, (1, 2))
    wpool = w_pool[0].astype(BF16)
    sinks = attn_sinks[0]
    bias_st = _bias_build(rel_bias.T, bucket)

    hn1 = _first_norm(x2, g_attn_norm)
    zqk, u, kn, vb, qst = _attn_in(hn1, gq, gk, w_in)
    ost = _attn_fwd(qst, kn, vb, bias_st, sinks)
    pooled, mix, h1, hn2 = _mix_out(u, ost, x2, w_out, wpool, pool_scale, g_ffn_norm)
    loss_v, dgate, dup, act, dh2, hn3, dgl, dw_plp, dh1, dg_ffn, dg_ple = _ffn_ple(hn2, h1, p2, tgt, w_late, g_ffn_norm,
                                                                                      g_ple_norm)

    late0, late_rows = GATHER_PARTS[1][0], SLAB_ROWS - GATHER_PARTS[1][0]
    partial_a = None
    for names, lefts, right in ((("gateT", "upT"), [dgate, dup], hn2), (("down",), [act], dh2), (("plg",), [hn3], dgl)):
        partial_a = _dw(lefts, right, "dw_" + names[0], partial_a, late_rows, [SLAB[name][0] - late0 for name in names])
    dw_plp = dw_plp.reshape(4, 64, N_CHIPS, 256).transpose(2, 1, 0, 3).reshape(N_CHIPS, 64, D_MODEL)
    partial_a = reduce_a.start(lax.dynamic_update_slice(partial_a, dw_plp, (0, SLAB["plp"][0] - late0, 0)))
    dost, du, dw_pool, dscale, partial_b = _mix_out_bwd(dh1, w_out, pooled, wpool, pool_scale, mix, partial_a)
    pre_a = reduce_a.middle(du, dw_pool.reshape(512, 128))
    dqst, dk, dv, dbias, dsink_rows = _attn_bwd(qst, kn, vb, dost, bias_st, sinks, pre_a)
    dx, dg_attn, dgq, dgk, partial_b = _attn_in_bwd(dqst, zqk, dk, dv, du, x2, dh1, hn1, partial_b, w_in, g_attn_norm, gq, gk)

    small = _small_pack(dg_attn, dg_ffn, dg_ple, dscale, dgq, dgk, dbias, dsink_rows, loss_v)
    return dx, partial_b, small


def kernel(x, p, w_in, w_out, g_attn_norm, g_q, g_k, attn_sinks, rel_bias, w_pool, pool_scale, g_ffn_norm, w_gate, w_up, w_down, g_ple_norm, w_ple_gate, w_ple_proj, loss_target, m_w_in, m_w_out, m_g_attn_norm, m_g_q, m_g_k, m_attn_sinks, m_rel_bias, m_w_pool, m_pool_scale, m_g_ffn_norm, m_w_gate, m_w_up, m_w_down, m_g_ple_norm, m_w_ple_gate, m_w_ple_proj, v_w_in, v_w_out, v_g_attn_norm, v_g_q, v_g_k, v_attn_sinks, v_rel_bias, v_w_pool, v_pool_scale, v_g_ffn_norm, v_w_gate, v_w_up, v_w_down, v_g_ple_norm, v_w_ple_gate, v_w_ple_proj):
    core = lax.axis_index("c").astype(jnp.int32).reshape(1)
    me = (2 * lax.axis_index("x") + lax.axis_index("y")).astype(jnp.int32).reshape(1)

    local_parts = [jnp.concatenate(pieces, axis=0).astype(BF16) for pieces in (
        [w_in[0].T, w_out[0]], [w_gate[0].T, w_up[0].T, w_down[0], w_ple_gate[0], _pack_ple_proj(w_ple_proj[0])])]
    wts = [(_ag_weights(local, 0, local.shape[0], name, collective_id), local, me)
           for local, name, collective_id in zip(local_parts, ("ag_early", "ag_late"), (1, 2))]

    place = jnp.concatenate([me, core])
    reduce_a = _Reduction("a", place, ids=(3, 4))
    dx, partial_b, small = _local_grads(x[0], p[0, 0], loss_target[0], wts, g_attn_norm, g_q, g_k, attn_sinks, rel_bias,
                                        w_pool, pool_scale, g_ffn_norm, g_ple_norm, reduce_a)
    reduce_b = _Reduction("b", place, ids=(6, 7))
    reduce_b.start(partial_b)
    grads_a, small_all = reduce_a.finish(partial_b, small)
    reduce_b.middle(grads_a)

    late0 = GATHER_PARTS[1][0]

    def rows(name):
        return grads_a, SLAB[name][0] - late0

    plp_rows = grads_a[SLAB["plp"][0] - late0:]
    big = {
        "w_gate": (w_gate, m_w_gate, v_w_gate, rows("gateT"), True),
        "w_up": (w_up, m_w_up, v_w_up, rows("upT"), True),
        "w_down": (w_down, m_w_down, v_w_down, rows("down"), False),
        "w_ple_gate": (w_ple_gate, m_w_ple_gate, v_w_ple_gate, rows("plg"), False),
        "w_ple_proj": (w_ple_proj, m_w_ple_proj, v_w_ple_proj,
                       (plp_rows.reshape(64, 4, 256).transpose(1, 0, 2).reshape(PLE_DIM, PLE_DIM), 0), False),
        "w_out": (w_out, m_w_out, v_w_out, None, False),
        "w_in": (w_in, m_w_in, v_w_in, None, True),
    }
    small_params = {
        "g_attn_norm": (g_attn_norm, m_g_attn_norm, v_g_attn_norm), "g_q": (g_q, m_g_q, v_g_q), "g_k": (g_k, m_g_k, v_g_k),
        "attn_sinks": (attn_sinks, m_attn_sinks, v_attn_sinks), "rel_bias": (rel_bias.T, m_rel_bias.T, v_rel_bias.T),
        "w_pool": tuple(a.reshape(512, 128) for a in (w_pool, m_w_pool, v_w_pool)),
        "pool_scale": (pool_scale, m_pool_scale, v_pool_scale), "g_ffn_norm": (g_ffn_norm, m_g_ffn_norm, v_g_ffn_norm),
        "g_ple_norm": (g_ple_norm, m_g_ple_norm, v_g_ple_norm),
    }

    grads, deltas, new_ms, new_vs = {}, {}, {}, {}
    out = grads_b = None
    for name, (w, m, v, g_src, transposed) in big.items():
        if g_src is None:
            if grads_b is None:
                grads_b = reduce_b.finish(out[-1])
            g_src = (grads_b, SLAB["out" if name == "w_out" else "inT"][0])
        view = (lambda a: a.T) if transposed else (lambda a: a)
        out = _adamw(view(w[0]), *g_src, view(m[0]), view(v[0]), "adamw_" + name)
        grads[name], deltas[name], new_ms[name], new_vs[name] = (view(a)[None] for a in out)

    loss, small_out = _adamw_small(small_all, reduce_a.tables, list(small_params.values()))
    for name, (g2, d, nm, nv) in zip(small_params, small_out):
        restore = {"w_pool": lambda a: a.reshape(w_pool.shape), "rel_bias": lambda a: a.T}.get(name, lambda a: a)
        grads[name], deltas[name], new_ms[name], new_vs[name] = (restore(a) for a in (g2, d, nm, nv))

    order = ["w_in", "w_out", "g_attn_norm", "g_q", "g_k", "attn_sinks", "rel_bias", "w_pool", "pool_scale", "g_ffn_norm",
             "w_gate", "w_up", "w_down", "g_ple_norm", "w_ple_gate", "w_ple_proj"]
    return (loss.reshape(()), dx[None], *[grads[n] for n in order], *[deltas[n] for n in order],
            *[new_ms[n] for n in order], *[new_vs[n] for n in order])
```

```python
import numpy as np
import jax
import jax.numpy as jnp
from jax import lax
from jax.experimental import pallas as pl
from jax.experimental.pallas import tpu as pltpu
from jax.experimental.pallas import tpu_sc as plsc

F32 = jnp.float32
BF16 = jnp.bfloat16
MESH = pl.DeviceIdType.MESH

D_MODEL = 1024
HEAD_DIM = 64
N_Q_HEADS = 8
ATTN_WIDTH = 512
POOL_WIDTH = 512
IN_WIDTH = 1280
D_FF = 2816
PLE_DIM = 256
FF_CHUNK = 1408
N_FF_CHUNKS = D_FF // FF_CHUNK
BLOCK = 128
N_BUCKETS = 32
MAX_DISTANCE = 128
EPS = 1e-6
NEG = -1e30
N_CHIPS = 4
N_DEV = 8

ADAM_LR = 0.001
ADAM_B1 = 0.9
ADAM_B2 = 0.999
ADAM_EPS = 1e-08
ADAM_WD = 0.01
ADAM_STEP = 10

SLAB = {"inT": (0, 320), "out": (320, 256), "gateT": (576, 704), "upT": (1280, 704), "down": (1984, 704),
        "plg": (2688, 256), "plp": (2944, 64)}
SLAB_ROWS = 3008
GATHER_PARTS = ((0, 576), (576, SLAB_ROWS))
POOL_HALO = 24

SMALL = {"g_attn": 0, "g_ffn": 8, "g_ple": 16, "pool_scale": 24, "g_q": 28, "g_k": 29, "sinks": 30, "loss": 31,
         "rel_bias": 32}
SMALL_ROWS = 64

VMEM_LIMIT_BIG = 60 * 1024 * 1024
VMEM_LIMIT = 48 * 1024 * 1024


def _params(vmem=VMEM_LIMIT, n_axes=1):
    return pltpu.CompilerParams(dimension_semantics=("arbitrary",) * n_axes, vmem_limit_bytes=vmem)


def _dot(a, b, ca, cb):
    return lax.dot_general(a, b, (((ca,), (cb,)), ((), ())), preferred_element_type=F32)


def _full(shape):
    return pl.BlockSpec(shape, lambda i: (0,) * len(shape))


ANY = pl.BlockSpec(memory_space=pl.ANY)
VMEM_WHOLE = pl.BlockSpec(memory_space=pltpu.VMEM)


W_SPECS = [ANY, ANY, pl.BlockSpec(memory_space=pltpu.SMEM)]


def _load_rows(w_refs, name, dst_ref, sems):
    slab_ref, local_ref, me_ref = w_refs
    off, rows = SLAB[name]
    slab_off = off - max(start for start, _ in GATHER_PARTS if start <= off)
    me = me_ref[0]
    for phase in ("start", "wait"):
        for j in range(N_CHIPS):
            dst = dst_ref.at[pl.ds(j * rows, rows), :]
            theirs = pltpu.make_async_copy(slab_ref.at[j, pl.ds(slab_off, rows), :], dst, sems.at[j])
            own = pltpu.make_async_copy(local_ref.at[pl.ds(slab_off, rows), :], dst, sems.at[j])

            @pl.when(me == j)
            def _():
                getattr(own, phase)()

            @pl.when(me != j)
            def _():
                getattr(theirs, phase)()


def _rms_fwd(x, g):
    r = lax.rsqrt(jnp.mean(x * x, axis=-1, keepdims=True) + EPS)
    return x * r * g


def _rms_bwd(x, g, dy):
    r = lax.rsqrt(jnp.mean(x * x, axis=-1, keepdims=True) + EPS)
    xn = x * r
    dyg = dy * g
    dx = r * (dyg - xn * jnp.mean(dyg * xn, axis=-1, keepdims=True))
    return dx, jnp.sum(dy * xn, axis=0, keepdims=True)


def _half_sum(v, lo):
    s_lo = jnp.sum(jnp.where(lo, v, 0.0), axis=-1, keepdims=True)
    s_hi = jnp.sum(jnp.where(lo, 0.0, v), axis=-1, keepdims=True)
    return jnp.where(lo, s_lo, s_hi)


def _half_sum_mxu(v):
    upper = lax.broadcasted_iota(jnp.int32, (128, 128), 0) < 64
    left = lax.broadcasted_iota(jnp.int32, (128, 128), 1) < 64
    ones = jnp.where(upper == left, 1.0, 0.0).astype(BF16)
    high = v.astype(BF16)
    low = (v - high.astype(F32)).astype(BF16)
    return _dot(high, ones, 1, 0) + _dot(low, ones, 1, 0)


def _pair_norm(zp, g, lo):
    r = lax.rsqrt(_half_sum(zp * zp, lo) * (1.0 / HEAD_DIM) + EPS)
    return zp * r * g


def _pair_norm_bwd(zp, g, dy):
    r = lax.rsqrt(_half_sum_mxu(zp * zp) * (1.0 / HEAD_DIM) + EPS)
    xn = zp * r
    dyg = dy * g
    dx = r * (dyg - xn * (_half_sum_mxu(dyg * xn) * (1.0 / HEAD_DIM)))
    return dx, jnp.sum(dy * xn, axis=0, keepdims=True)


def _pack_heads(pairs, lo):
    packed = [None] * 4
    for m in range(2):
        a, b = pairs[m], pairs[m + 2]
        packed[2 * m] = jnp.where(lo, a, pltpu.roll(b, 64, axis=1))
        packed[2 * m + 1] = jnp.where(lo, pltpu.roll(a, 64, axis=1), b)
    return packed


def _unpack_heads(packed, lo):
    pairs = [None] * 4
    for m in range(2):
        a, b = packed[2 * m], packed[2 * m + 1]
        pairs[m] = jnp.where(lo, a, pltpu.roll(b, 64, axis=1))
        pairs[m + 2] = jnp.where(lo, pltpu.roll(a, 64, axis=1), b)
    return pairs


def _expand_heads(packed):
    flat = packed.reshape(4 * BLOCK, 128)
    lo = lax.broadcasted_iota(jnp.int32, flat.shape, 1) < 64
    zero = jnp.zeros_like(flat)
    return jnp.concatenate([jnp.where(lo, flat, zero), jnp.where(lo, zero, flat)], axis=0)


def _fold_heads(stacked):
    half = 4 * BLOCK
    lo = lax.broadcasted_iota(jnp.int32, (half, 128), 1) < 64
    return jnp.where(lo, stacked[:half], stacked[half:]).reshape(4, BLOCK, 128)


def _sigmoid(v):
    return 1.0 / (1.0 + jnp.exp(-v))


def _pool_counts(tile, n_rows):
    t1 = tile * n_rows + lax.broadcasted_iota(jnp.int32, (n_rows, POOL_WIDTH), 0) + 1
    lane = lax.broadcasted_iota(jnp.int32, (n_rows, POOL_WIDTH), 1)
    win = jnp.where(lane < 128, 2, jnp.where(lane < 256, 4, jnp.where(lane < 384, 8, 16)))
    return jnp.minimum(t1, win).astype(F32)


def _first_norm(x2, g_attn):
    s_len = x2.shape[0]
    t = 512

    def body(x_ref, g_ref, hn_ref):
        hn_ref[...] = _rms_fwd(x_ref[...], g_ref[...]).astype(BF16)

    row = pl.BlockSpec((t, D_MODEL), lambda i: (i, 0))
    return pl.pallas_call(
        body, name="first_norm", grid=(s_len // t,), in_specs=[row, _full((1, D_MODEL))], out_specs=row,
        out_shape=jax.ShapeDtypeStruct((s_len, D_MODEL), BF16), compiler_params=_params(),
    )(x2, g_attn)


def _attn_in(hn1, gq, gk, wts):
    s_len = hn1.shape[0]
    t = 512

    def body(hn_ref, gq_ref, gk_ref, sl_ref, lo_ref, me_ref, zqk_ref, u_ref, kn_ref, v_ref, qst_ref, w_ref, sems):
        @pl.when(pl.program_id(0) == 0)
        def _():
            _load_rows((sl_ref, lo_ref, me_ref), "inT", w_ref, sems)

        z = _dot(hn_ref[...], w_ref[...], 1, 1)
        zqk_ref[...] = z[:, :640]
        u_ref[...] = z[:, 768:]
        v_ref[...] = z[:, 640:768].astype(BF16)
        lo = lax.broadcasted_iota(jnp.int32, (t, 128), 1) < 64
        kn_ref[...] = _pair_norm(z[:, 512:640], gk_ref[...], lo).astype(BF16)
        pairs = [_pair_norm(z[:, 128 * p:128 * p + 128], gq_ref[...], lo) for p in range(4)]
        for j, entry in enumerate(_pack_heads(pairs, lo)):
            qst_ref[j] = entry.astype(BF16)

    row = lambda w: pl.BlockSpec((t, w), lambda i: (i, 0))
    return pl.pallas_call(
        body, name="attn_in", grid=(s_len // t,),
        in_specs=[row(D_MODEL), _full((1, 128)), _full((1, 128))] + W_SPECS,
        out_specs=[row(640), row(POOL_WIDTH), row(128), row(128), pl.BlockSpec((4, t, 128), lambda i: (0, i, 0))],
        out_shape=[jax.ShapeDtypeStruct((s_len, 640), F32), jax.ShapeDtypeStruct((s_len, POOL_WIDTH), F32),
                   jax.ShapeDtypeStruct((s_len, 128), BF16), jax.ShapeDtypeStruct((s_len, 128), BF16),
                   jax.ShapeDtypeStruct((4, s_len, 128), BF16)],
        scratch_shapes=[pltpu.VMEM((IN_WIDTH, D_MODEL), BF16), pltpu.SemaphoreType.DMA((N_CHIPS,))],
        compiler_params=_params(),
    )(hn1, gq, gk, *wts)


def _bucket_table():
    i_idx = np.arange(BLOCK)[:, None]
    j_idx = np.arange(2 * BLOCK)[None, :]
    d = BLOCK + i_idx - j_idx
    n = np.maximum(d, 0)
    max_exact = N_BUCKETS // 2
    nf = np.maximum(n, 1).astype(np.float64)
    large = max_exact + (np.log(nf / max_exact) / np.log(MAX_DISTANCE / max_exact) * (N_BUCKETS - max_exact)).astype(np.int64)
    large = np.minimum(large, N_BUCKETS - 1)
    bucket = np.where(n < max_exact, n, large)
    return np.where((d >= 0) & (d < BLOCK), bucket, -1).astype(np.int32)


def _bias_build(rel_bias_t, bucket):
    def body(rb_ref, bucket_ref, out_ref):
        bk = bucket_ref[...]
        for h in range(N_Q_HEADS):
            acc = jnp.full((BLOCK, 2 * BLOCK), NEG, F32)
            for b in range(N_BUCKETS):
                acc = jnp.where(bk == b, rb_ref[h, b], acc)
            out_ref[0, pl.ds(h * BLOCK, BLOCK), :] = acc
            out_ref[1, pl.ds(h * BLOCK, BLOCK), :] = acc
            out_ref[1, pl.ds(h * BLOCK, BLOCK), 0:BLOCK] = jnp.full((BLOCK, BLOCK), NEG, F32)

    return pl.pallas_call(
        body, name="bias_build",
        in_specs=[pl.BlockSpec(memory_space=pltpu.SMEM), VMEM_WHOLE], out_specs=VMEM_WHOLE,
        out_shape=jax.ShapeDtypeStruct((2, N_Q_HEADS * BLOCK, 2 * BLOCK), F32),
    )(rel_bias_t, bucket)


def _head_softmax(s_ref, bias_ref, sink_ref, h):
    rows = pl.ds(pl.multiple_of(h * BLOCK, BLOCK), BLOCK)
    s = s_ref[rows, :] * (HEAD_DIM ** -0.5) + bias_ref[rows, :]
    sink = sink_ref[h]
    m = jnp.maximum(jnp.max(s, axis=-1, keepdims=True), sink)
    p = jnp.exp(s - m)
    e_sink = jnp.exp(sink - m)
    inv = 1.0 / (jnp.sum(p, axis=-1, keepdims=True) + e_sink)
    return rows, p * inv, e_sink * inv


ATTN_STEP_BLOCKS = 4
BAND = (N_Q_HEADS * BLOCK, 2 * BLOCK)


def _attn_specs():
    nb = ATTN_STEP_BLOCKS
    stacked = pl.BlockSpec((4, nb * BLOCK, 128), lambda i: (0, i, 0))
    kv = [pl.BlockSpec((BLOCK, 128), lambda i: (jnp.maximum(nb * i - 1, 0), 0)), pl.BlockSpec((nb * BLOCK, 128), lambda i: (i, 0))]
    consts = [_full((2,) + BAND), pl.BlockSpec(memory_space=pltpu.SMEM)]
    return stacked, kv, consts


def _step_blocks(i, kp_ref, kc_ref, vp_ref, vc_ref, bias_ref):
    blocks = []
    for b in range(ATTN_STEP_BLOCKS):
        if b == 0:
            k2 = jnp.concatenate([kp_ref[...], kc_ref[pl.ds(0, BLOCK), :]], axis=0)
            v2 = jnp.concatenate([vp_ref[...], vc_ref[pl.ds(0, BLOCK), :]], axis=0)
            bias = bias_ref.at[jnp.where(i == 0, 1, 0)]
        else:
            k2, v2, bias = kc_ref[pl.ds((b - 1) * BLOCK, 2 * BLOCK), :], vc_ref[pl.ds((b - 1) * BLOCK, 2 * BLOCK), :], bias_ref.at[0]
        blocks.append((pl.ds(b * BLOCK, BLOCK), k2, v2, bias))
    return blocks


def _attn_fwd(qst, kn, vb, bias_st, sinks):
    s_len = kn.shape[0]

    def body(q_ref, kp_ref, kc_ref, vp_ref, vc_ref, bias_ref, sink_ref, o_ref, s_ref, p_ref):
        for b, (rows, k2, v2, bias) in enumerate(_step_blocks(pl.program_id(0), kp_ref, kc_ref, vp_ref, vc_ref, bias_ref)):
            s_b, p_b = s_ref.at[b], p_ref.at[b]
            s_b[...] = _dot(_expand_heads(q_ref[:, rows, :]), k2, 1, 1)

            def head(h, carry):
                head_rows, probs, _ = _head_softmax(s_b, bias, sink_ref, h)
                p_b[head_rows, :] = probs.astype(BF16)
                return carry

            lax.fori_loop(0, N_Q_HEADS, head, 0, unroll=True)
            o_ref[:, rows, :] = _fold_heads(_dot(p_b[...], v2, 1, 0)).astype(BF16)

    stacked, kv, consts = _attn_specs()
    return pl.pallas_call(
        body, name="attn_fwd", grid=(s_len // (ATTN_STEP_BLOCKS * BLOCK),),
        in_specs=[stacked] + kv + kv + consts, out_specs=stacked,
        out_shape=jax.ShapeDtypeStruct((4, s_len, 128), BF16),
        scratch_shapes=[pltpu.VMEM((ATTN_STEP_BLOCKS,) + BAND, F32), pltpu.VMEM((ATTN_STEP_BLOCKS,) + BAND, BF16)],
        compiler_params=_params(),
    )(qst, kn, kn, vb, vb, bias_st, sinks)


def _mix_out(u, ost, x2, wts, wpool, pool_scale, g_ffn):
    s_len = x2.shape[0]
    t = 512
    n = t + 16

    def body(u_ref, o_ref, x_ref, sl_ref, lo_ref, me_ref, wp_ref, sc_ref, g_ref, pooled_ref, mix_ref, h1_ref, hn_ref,
             w_ref, ext_ref, st_ref, sems):
        i = pl.program_id(0)

        @pl.when(i == 0)
        def _():
            _load_rows((sl_ref, lo_ref, me_ref), "out", w_ref, sems)
            ext_ref[...] = jnp.zeros_like(ext_ref)
            st_ref[...] = jnp.zeros_like(st_ref)

        u_tile = u_ref[...]
        ext_ref[pl.ds(POOL_HALO, t), :] = u_tile
        st_ref[pl.ds(8, n), :] = ext_ref[pl.ds(8, n), :] + ext_ref[pl.ds(7, n), :]
        st_ref[pl.ds(8, n), 128:] = st_ref[pl.ds(8, n), 128:] + st_ref[pl.ds(6, n), 128:]
        st_ref[pl.ds(8, n), 256:] = st_ref[pl.ds(8, n), 256:] + st_ref[pl.ds(4, n), 256:]
        st_ref[pl.ds(8, n), 384:] = st_ref[pl.ds(8, n), 384:] + st_ref[pl.ds(0, n), 384:]
        ext_ref[pl.ds(0, POOL_HALO), :] = ext_ref[pl.ds(t, POOL_HALO), :]
        pooled = (st_ref[pl.ds(POOL_HALO, t), :] / _pool_counts(i, t) - u_tile).astype(BF16)
        pooled_ref[...] = pooled
        for g in range(4):
            cols = slice(128 * g, 128 * g + 128)
            y = _dot(pooled[:, cols], wp_ref[g], 1, 0) * sc_ref[:, cols]
            mix_ref[:, ATTN_WIDTH + 128 * g:ATTN_WIDTH + 128 * g + 128] = y.astype(BF16)
        lo = lax.broadcasted_iota(jnp.int32, (t, 128), 1) < 64
        for p, pair in enumerate(_unpack_heads([o_ref[j].astype(F32) for j in range(4)], lo)):
            mix_ref[:, 128 * p:128 * p + 128] = pair.astype(BF16)
        h1 = x_ref[...] + _dot(mix_ref[...], w_ref[...], 1, 0)
        h1_ref[...] = h1
        hn_ref[...] = _rms_fwd(h1, g_ref[...]).astype(BF16)

    row = lambda w: pl.BlockSpec((t, w), lambda i: (i, 0))
    return pl.pallas_call(
        body, name="mix_out", grid=(s_len // t,),
        in_specs=[row(POOL_WIDTH), pl.BlockSpec((4, t, 128), lambda i: (0, i, 0)), row(D_MODEL)] + W_SPECS
        + [_full((4, 128, 128)), _full((1, POOL_WIDTH)), _full((1, D_MODEL))],
        out_specs=[row(POOL_WIDTH), row(D_MODEL), row(D_MODEL), row(D_MODEL)],
        out_shape=[jax.ShapeDtypeStruct((s_len, POOL_WIDTH), BF16), jax.ShapeDtypeStruct((s_len, D_MODEL), BF16),
                   jax.ShapeDtypeStruct((s_len, D_MODEL), F32), jax.ShapeDtypeStruct((s_len, D_MODEL), BF16)],
        scratch_shapes=[pltpu.VMEM((D_MODEL, D_MODEL), BF16), pltpu.VMEM((t + POOL_HALO, POOL_WIDTH), F32),
                        pltpu.VMEM((t + POOL_HALO, POOL_WIDTH), F32), pltpu.SemaphoreType.DMA((N_CHIPS,))],
        compiler_params=_params(),
    )(u, ost, x2, *wts, wpool, pool_scale, g_ffn)


def _ffn_ple(hn2, h1, p2, tgt, wts, g_ffn, g_ple):
    s_len = h1.shape[0]
    t = 256
    n_tiles = s_len // t

    def body(hn_ref, h1_ref, p_ref, tgt_ref, sl_ref, lo_ref, me_ref, gf_ref, gp_ref,
             loss_ref, dgate_ref, dup_ref, act_ref, dh2b_ref, hn3_ref, dgl_ref, dwp_ref, dh1_ref, dgf_ref, dgp_ref,
             wg_ref, wu_ref, wd_ref, wl_ref, wp_ref, packed_ref, gate_s, up_s, loss_acc, dwp_acc, sems):
        i = pl.program_id(0)

        @pl.when(i == 0)
        def _():
            w_refs = (sl_ref, lo_ref, me_ref)
            _load_rows(w_refs, "gateT", wg_ref, sems)
            _load_rows(w_refs, "upT", wu_ref, sems)
            _load_rows(w_refs, "down", wd_ref, sems)
            _load_rows(w_refs, "plg", wl_ref, sems)
            _load_rows(w_refs, "plp", packed_ref, sems)
            for j in range(N_CHIPS):
                for q in range(4):
                    wp_ref[pl.ds(64 * q, 64), 256 * j:256 * j + 256] = packed_ref[pl.ds(64 * j, 64), 256 * q:256 * q + 256]
            loss_acc[...] = jnp.zeros_like(loss_acc)
            dwp_acc[...] = jnp.zeros_like(dwp_acc)
            dgf_ref[...] = jnp.zeros_like(dgf_ref)
            dgp_ref[...] = jnp.zeros_like(dgp_ref)

        hn = hn_ref[...]
        h1v = h1_ref[...]
        h2 = h1v
        for ch in range(N_FF_CHUNKS):
            rows = pl.ds(ch * FF_CHUNK, FF_CHUNK)
            gate = _dot(hn, wg_ref[rows, :], 1, 1)
            up = _dot(hn, wu_ref[rows, :], 1, 1)
            gate_s[ch] = gate
            up_s[ch] = up
            act = (gate * _sigmoid(gate) * up).astype(BF16)
            act_ref[ch] = act
            h2 = h2 + _dot(act, wd_ref[rows, :], 1, 0)
        gp = gp_ref[...]
        hn3 = _rms_fwd(h2, gp).astype(BF16)
        hn3_ref[...] = hn3
        gate2 = _sigmoid(_dot(hn3, wl_ref[...], 1, 0))
        p_tile = p_ref[...].astype(BF16)
        pp = _dot(p_tile, wp_ref[...], 1, 0)
        err = h2 + gate2 * pp - tgt_ref[...]
        loss_acc[...] += jnp.sum(err * err, axis=0, keepdims=True)
        dy = err * (1.0 / D_MODEL)
        dwp_acc[...] += _dot(p_tile, (dy * gate2).astype(BF16), 0, 0)
        dgl = (dy * pp * gate2 * (1.0 - gate2)).astype(BF16)
        dgl_ref[...] = dgl
        dx3, dg3 = _rms_bwd(h2, gp, _dot(dgl, wl_ref[...], 1, 1))
        dh2 = dy + dx3
        dgp_ref[...] += dg3
        dh2b = dh2.astype(BF16)
        dh2b_ref[...] = dh2b
        dhn = jnp.zeros((t, D_MODEL), F32)
        for ch in range(N_FF_CHUNKS):
            rows = pl.ds(ch * FF_CHUNK, FF_CHUNK)
            dact = _dot(dh2b, wd_ref[rows, :], 1, 1)
            gate_v = gate_s[ch]
            up_v = up_s[ch]
            sg = _sigmoid(gate_v)
            dup = (dact * (gate_v * sg)).astype(BF16)
            dgate = (dact * up_v * (sg * (1.0 + gate_v * (1.0 - sg)))).astype(BF16)
            dup_ref[ch] = dup
            dgate_ref[ch] = dgate
            dhn = dhn + _dot(dgate, wg_ref[rows, :], 1, 0) + _dot(dup, wu_ref[rows, :], 1, 0)
        dx, dg = _rms_bwd(h1v, gf_ref[...], dhn)
        dh1_ref[...] = dh2 + dx
        dgf_ref[...] += dg

        @pl.when(i == n_tiles - 1)
        def _():
            total = jnp.sum(loss_acc[...], axis=-1, keepdims=True) * (0.5 / D_MODEL)
            loss_ref[...] = jnp.broadcast_to(total, loss_ref.shape)
            dwp_ref[...] = dwp_acc[...].astype(BF16)

    row = lambda w: pl.BlockSpec((t, w), lambda i: (i, 0))
    chunked = pl.BlockSpec((N_FF_CHUNKS, t, FF_CHUNK), lambda i: (0, i, 0))
    vec = _full((1, D_MODEL))
    act_shape = jax.ShapeDtypeStruct((N_FF_CHUNKS, s_len, FF_CHUNK), BF16)
    tok = lambda dtype: jax.ShapeDtypeStruct((s_len, D_MODEL), dtype)
    return pl.pallas_call(
        body, name="ffn_ple", grid=(n_tiles,),
        in_specs=[row(D_MODEL), row(D_MODEL), row(PLE_DIM), row(D_MODEL)] + W_SPECS + [vec, vec],
        out_specs=[_full((1, 128)), chunked, chunked, chunked] + [row(D_MODEL)] * 3 + [_full((PLE_DIM, D_MODEL)), row(D_MODEL),
                                                                                       vec, vec],
        out_shape=[jax.ShapeDtypeStruct((1, 128), F32), act_shape, act_shape, act_shape, tok(BF16), tok(BF16), tok(BF16),
                   jax.ShapeDtypeStruct((PLE_DIM, D_MODEL), BF16), tok(F32), jax.ShapeDtypeStruct((1, D_MODEL), F32),
                   jax.ShapeDtypeStruct((1, D_MODEL), F32)],
        scratch_shapes=[pltpu.VMEM((D_FF, D_MODEL), BF16)] * 3
        + [pltpu.VMEM((D_MODEL, D_MODEL), BF16), pltpu.VMEM((PLE_DIM, D_MODEL), BF16), pltpu.VMEM((PLE_DIM, D_MODEL), BF16),
           pltpu.VMEM((N_FF_CHUNKS, t, FF_CHUNK), F32), pltpu.VMEM((N_FF_CHUNKS, t, FF_CHUNK), F32), pltpu.VMEM((1, D_MODEL), F32),
           pltpu.VMEM((PLE_DIM, D_MODEL), F32), pltpu.SemaphoreType.DMA((N_CHIPS,))],
        compiler_params=_params(VMEM_LIMIT_BIG),
    )(hn2, h1, p2, tgt, *wts, g_ffn, g_ple)


def _accumulate_tn(acc_ref, a, b, first):
    @pl.when(first)
    def _():
        acc_ref[...] = _dot(a, b, 0, 0)

    @pl.when(jnp.logical_not(first))
    def _():
        acc_ref[...] += _dot(a, b, 0, 0)


def _flush_chunks(acc_ref, stage_ref, slab_ref, name, sems):
    stage_ref[...] = acc_ref[...].astype(BF16)
    off, rows = SLAB[name]
    copies = [pltpu.make_async_copy(stage_ref.at[pl.ds(j * rows, rows), :], slab_ref.at[j, pl.ds(off, rows), :], sems.at[j])
              for j in range(N_CHIPS)]
    for cp in copies:
        cp.start()
    for cp in copies:
        cp.wait()


def _mix_out_bwd(dh1, wts, pooled, wpool, pool_scale, mix, after):
    s_len = dh1.shape[0]
    t = 512
    n = t + 16
    n_tiles = s_len // t
    early_rows = GATHER_PARTS[0][1]

    def body(dh1_ref, sl_ref, lo_ref, me_ref, pooled_ref, wp_ref, sc_ref, mix_ref, after_ref, dost_ref, du_ref, dwp_ref,
             dsc_ref, slab_ref, w_ref, ext_ref, st_ref, acc_ref, stage_ref, sems):
        del after_ref
        i = pl.program_id(0)

        @pl.when(i == 0)
        def _():
            _load_rows((sl_ref, lo_ref, me_ref), "out", w_ref, sems)
            ext_ref[...] = jnp.zeros_like(ext_ref)
            st_ref[...] = jnp.zeros_like(st_ref)
            dsc_ref[...] = jnp.zeros_like(dsc_ref)
            dwp_ref[...] = jnp.zeros_like(dwp_ref)
            acc_ref[...] = jnp.zeros_like(acc_ref)

        dh1b = dh1_ref[...].astype(BF16)
        acc_ref[...] += _dot(mix_ref[...], dh1b, 0, 0)
        dmix = _dot(dh1b, w_ref[...], 1, 1)
        lo = lax.broadcasted_iota(jnp.int32, (t, 128), 1) < 64
        for j, entry in enumerate(_pack_heads([dmix[:, 128 * p:128 * p + 128] for p in range(4)], lo)):
            dost_ref[j] = entry.astype(BF16)
        pooled_v = pooled_ref[...]
        counts = _pool_counts(n_tiles - 1 - i, t)
        for g in range(4):
            cols = slice(128 * g, 128 * g + 128)
            dm = dmix[:, ATTN_WIDTH + 128 * g:ATTN_WIDTH + 128 * g + 128]
            ypre = _dot(pooled_v[:, cols], wp_ref[g], 1, 0)
            dsc_ref[:, cols] += jnp.sum(ypre * dm, axis=0, keepdims=True)
            dyp = (dm * sc_ref[:, cols]).astype(BF16)
            dwp_ref[g] += _dot(pooled_v[:, cols], dyp, 0, 0)
            dpooled = _dot(dyp, wp_ref[g], 1, 1)
            du_ref[:, cols] = -dpooled
            ext_ref[pl.ds(0, t), cols] = dpooled / counts[:, cols]
        st_ref[pl.ds(0, n), :] = ext_ref[pl.ds(0, n), :] + ext_ref[pl.ds(1, n), :]
        st_ref[pl.ds(0, n), 128:] = st_ref[pl.ds(0, n), 128:] + st_ref[pl.ds(2, n), 128:]
        st_ref[pl.ds(0, n), 256:] = st_ref[pl.ds(0, n), 256:] + st_ref[pl.ds(4, n), 256:]
        st_ref[pl.ds(0, n), 384:] = st_ref[pl.ds(0, n), 384:] + st_ref[pl.ds(8, n), 384:]
        ext_ref[pl.ds(t, POOL_HALO), :] = ext_ref[pl.ds(0, POOL_HALO), :]
        du_ref[...] += st_ref[pl.ds(0, t), :]

        @pl.when(i == n_tiles - 1)
        def _():
            _flush_chunks(acc_ref, stage_ref, slab_ref, "out", sems)

    rev = lambda w: pl.BlockSpec((t, w), lambda i: (n_tiles - 1 - i, 0))
    return pl.pallas_call(
        body, name="mix_out_bwd", grid=(n_tiles,),
        in_specs=[rev(D_MODEL)] + W_SPECS + [rev(POOL_WIDTH), _full((4, 128, 128)), _full((1, POOL_WIDTH)), rev(D_MODEL), ANY],
        out_specs=[pl.BlockSpec((4, t, 128), lambda i: (0, n_tiles - 1 - i, 0)), rev(POOL_WIDTH),
                   _full((4, 128, 128)), _full((1, POOL_WIDTH)), ANY],
        out_shape=[jax.ShapeDtypeStruct((4, s_len, 128), BF16), jax.ShapeDtypeStruct((s_len, POOL_WIDTH), F32),
                   jax.ShapeDtypeStruct((4, 128, 128), F32), jax.ShapeDtypeStruct((1, POOL_WIDTH), F32),
                   jax.ShapeDtypeStruct((N_CHIPS, early_rows, D_MODEL), BF16)],
        scratch_shapes=[pltpu.VMEM((D_MODEL, D_MODEL), BF16), pltpu.VMEM((t + POOL_HALO, POOL_WIDTH), F32),
                        pltpu.VMEM((t + POOL_HALO, POOL_WIDTH), F32), pltpu.VMEM((D_MODEL, D_MODEL), F32),
                        pltpu.VMEM((D_MODEL, D_MODEL), BF16), pltpu.SemaphoreType.DMA((N_CHIPS,))],
        compiler_params=_params(),
    )(dh1, *wts, pooled, wpool, pool_scale, mix, after)


def _attn_bwd(qst, kn, vb, dost, bias_st, sinks, after):
    s_len = kn.shape[0]

    def body(q_ref, kp_ref, kc_ref, vp_ref, vc_ref, do_ref, bias_ref, sink_ref, after_ref, dq_ref, dk_ref, dv_ref, dbias_ref,
             dsink_ref, s_ref, dp_ref, p_ref, dl_ref):
        del after_ref
        i = pl.program_id(0)

        @pl.when(i == 0)
        def _():
            dk_ref[...] = jnp.zeros_like(dk_ref)
            dv_ref[...] = jnp.zeros_like(dv_ref)
            dbias_ref[...] = jnp.zeros_like(dbias_ref)
            dsink_ref[...] = jnp.zeros_like(dsink_ref)

        for b, (rows, k2, v2, bias) in enumerate(_step_blocks(i, kp_ref, kc_ref, vp_ref, vc_ref, bias_ref)):
            s_b, dp_b, p_b, dl_b = s_ref.at[b], dp_ref.at[b], p_ref.at[b], dl_ref.at[b]
            q = _expand_heads(q_ref[:, rows, :])
            do = _expand_heads(do_ref[:, rows, :])
            s_b[...] = _dot(q, k2, 1, 1)
            dp_b[...] = _dot(do, v2, 1, 1)

            def head(h, carry):
                head_rows, probs, p_sink = _head_softmax(s_b, bias, sink_ref, h)
                dp = dp_b[head_rows, :]
                dsum = jnp.sum(probs * dp, axis=-1, keepdims=True)
                dlog = probs * (dp - dsum)
                dsink_ref[head_rows, :] -= p_sink * dsum
                dbias_ref[head_rows, :] += dlog
                p_b[head_rows, :] = probs.astype(BF16)
                dl_b[head_rows, :] = (dlog * (HEAD_DIM ** -0.5)).astype(BF16)
                return carry

            lax.fori_loop(0, N_Q_HEADS, head, 0, unroll=True)
            dlog_s = dl_b[...]
            dq_ref[:, rows, :] = _fold_heads(_dot(dlog_s, k2, 1, 0))
            dk2 = _dot(dlog_s, q, 0, 0)
            dv2 = _dot(p_b[...], do, 0, 0)
            block = ATTN_STEP_BLOCKS * i + b
            prev_rows = pl.ds(pl.multiple_of(jnp.maximum(block - 1, 0) * BLOCK, BLOCK), BLOCK)
            cur_rows = pl.ds(pl.multiple_of(block * BLOCK, BLOCK), BLOCK)
            dk_ref[prev_rows, :] += dk2[:BLOCK]
            dk_ref[cur_rows, :] += dk2[BLOCK:]
            dv_ref[prev_rows, :] += dv2[:BLOCK]
            dv_ref[cur_rows, :] += dv2[BLOCK:]

    stacked, kv, consts = _attn_specs()
    per_step = (ATTN_STEP_BLOCKS,) + BAND
    return pl.pallas_call(
        body, name="attn_bwd", grid=(s_len // (ATTN_STEP_BLOCKS * BLOCK),),
        in_specs=[stacked] + kv + kv + [stacked] + consts + [ANY],
        out_specs=[stacked, _full((s_len, 128)), _full((s_len, 128)), _full(BAND), _full((N_Q_HEADS * BLOCK, 1))],
        out_shape=[jax.ShapeDtypeStruct((4, s_len, 128), F32), jax.ShapeDtypeStruct((s_len, 128), F32),
                   jax.ShapeDtypeStruct((s_len, 128), F32), jax.ShapeDtypeStruct(BAND, F32),
                   jax.ShapeDtypeStruct((N_Q_HEADS * BLOCK, 1), F32)],
        scratch_shapes=[pltpu.VMEM(per_step, F32), pltpu.VMEM(per_step, F32), pltpu.VMEM(per_step, BF16),
                        pltpu.VMEM(per_step, BF16)],
        compiler_params=_params(),
    )(qst, kn, kn, vb, vb, dost, bias_st, sinks, after)


def _flip_rows(x):
    n = x.shape[0]
    exchange = (lax.broadcasted_iota(jnp.int32, (n, n), 0) + lax.broadcasted_iota(jnp.int32, (n, n), 1) == n - 1)
    exchange = jnp.where(exchange, 1.0, 0.0).astype(BF16)
    flipped, rest = None, x
    for _ in range(3):
        term = rest.astype(BF16)
        rest = rest - term.astype(F32)
        part = _dot(exchange, term, 1, 0)
        flipped = part if flipped is None else flipped + part
    return flipped


def _small_pack(dg_attn, dg_ffn, dg_ple, dscale, dgq, dgk, dbias, dsink_rows, loss_v):
    def body(ga_ref, gf_ref, gp_ref, sc_ref, gq_ref, gk_ref, db_ref, ds_ref, bucket_ref, loss_ref, out_ref):
        out_ref[...] = jnp.zeros((SMALL_ROWS, 128), F32)
        for name, ref, n in (("g_attn", ga_ref, 8), ("g_ffn", gf_ref, 8), ("g_ple", gp_ref, 8), ("pool_scale", sc_ref, 4)):
            for k in range(n):
                out_ref[pl.ds(SMALL[name] + k, 1), :] = ref[:, 128 * k:128 * k + 128]
        for name, ref in (("g_q", gq_ref), ("g_k", gk_ref)):
            both = ref[...]
            out_ref[pl.ds(SMALL[name], 1), :] = both + pltpu.roll(both, 64, axis=1)
        out_ref[pl.ds(SMALL["loss"], 1), :] = loss_ref[...]
        by_diagonal = lambda flipped: pltpu.roll(flipped, 0, 1, stride=1, stride_axis=0)
        bucket_of = jnp.max(by_diagonal(bucket_ref[...]), axis=0, keepdims=True)
        sums = jnp.concatenate([jnp.sum(by_diagonal(_flip_rows(db_ref[pl.ds(h * BLOCK, BLOCK), :])), axis=0, keepdims=True)
                                for h in range(N_Q_HEADS)], axis=0)
        lanes = lax.broadcasted_iota(jnp.int32, (N_Q_HEADS, 128), 1)
        lane1 = lax.broadcasted_iota(jnp.int32, (1, 128), 1)
        rb = jnp.zeros((N_Q_HEADS, 128), F32)
        for b in range(N_BUCKETS):
            rb = jnp.where(lanes == b, jnp.sum(jnp.where(bucket_of == float(b), sums, 0.0), axis=1, keepdims=True), rb)
        sk = jnp.zeros((1, 128), F32)
        for h in range(N_Q_HEADS):
            sk = jnp.where(lane1 == h, jnp.sum(ds_ref[pl.ds(h * BLOCK, BLOCK), :]), sk)
        out_ref[pl.ds(SMALL["rel_bias"], N_Q_HEADS), :] = rb
        out_ref[pl.ds(SMALL["sinks"], 1), :] = sk

    bucket = jnp.asarray(_bucket_table()[::-1].astype(np.float32))
    return pl.pallas_call(
        body, name="small_pack", in_specs=[VMEM_WHOLE] * 10, out_specs=VMEM_WHOLE,
        out_shape=jax.ShapeDtypeStruct((SMALL_ROWS, 128), F32),
    )(dg_attn, dg_ffn, dg_ple, dscale, dgq, dgk, dbias, dsink_rows, bucket, loss_v)


def _attn_in_bwd(dqst, zqk, dk, dv, du, x2, dh1, hn1, slab, wts, g_attn, gq, gk):
    s_len = x2.shape[0]
    t = 512
    n_tiles = s_len // t

    def body(dq_ref, zqk_ref, dk_ref, dv_ref, du_ref, x_ref, dh1_ref, hn_ref, slab_in_ref, sl_ref, lo_ref, me_ref, g_ref,
             gq_ref, gk_ref, dx_ref, dg_ref, dgq_ref, dgk_ref, slab_ref, w_ref, dz_ref, acc_ref, stage_ref, sems):
        del slab_in_ref
        i = pl.program_id(0)

        @pl.when(i == 0)
        def _():
            _load_rows((sl_ref, lo_ref, me_ref), "inT", w_ref, sems)
            dg_ref[...] = jnp.zeros_like(dg_ref)
            dgq_ref[...] = jnp.zeros_like(dgq_ref)
            dgk_ref[...] = jnp.zeros_like(dgk_ref)
            acc_ref[...] = jnp.zeros_like(acc_ref)

        lo = lax.broadcasted_iota(jnp.int32, (t, 128), 1) < 64
        for p, dqn in enumerate(_unpack_heads([dq_ref[j] for j in range(4)], lo)):
            dq_raw, dgq = _pair_norm_bwd(zqk_ref[:, 128 * p:128 * p + 128], gq_ref[...], dqn)
            dz_ref[:, 128 * p:128 * p + 128] = dq_raw.astype(BF16)
            dgq_ref[...] += dgq
        dk_raw, dgk = _pair_norm_bwd(zqk_ref[:, 512:640], gk_ref[...], dk_ref[...])
        dgk_ref[...] += dgk
        dz_ref[:, 512:640] = dk_raw.astype(BF16)
        dz_ref[:, 640:768] = dv_ref[...].astype(BF16)
        dz_ref[:, 768:] = du_ref[...].astype(BF16)
        dz = dz_ref[...]
        acc_ref[...] += _dot(dz, hn_ref[...], 0, 0)
        dx, dg = _rms_bwd(x_ref[...], g_ref[...], _dot(dz, w_ref[...], 1, 0))
        dx_ref[...] = dh1_ref[...] + dx
        dg_ref[...] += dg

        @pl.when(i == n_tiles - 1)
        def _():
            _flush_chunks(acc_ref, stage_ref, slab_ref, "inT", sems)

    row = lambda w: pl.BlockSpec((t, w), lambda i: (i, 0))
    return pl.pallas_call(
        body, name="attn_in_bwd", grid=(n_tiles,),
        in_specs=[pl.BlockSpec((4, t, 128), lambda i: (0, i, 0)), row(640), row(128), row(128), row(POOL_WIDTH),
                  row(D_MODEL), row(D_MODEL), row(D_MODEL), ANY] + W_SPECS + [_full((1, D_MODEL)), _full((1, 128)),
                                                                              _full((1, 128))],
        out_specs=[row(D_MODEL), _full((1, D_MODEL)), _full((1, 128)), _full((1, 128)), ANY],
        out_shape=[jax.ShapeDtypeStruct((s_len, D_MODEL), F32), jax.ShapeDtypeStruct((1, D_MODEL), F32),
                   jax.ShapeDtypeStruct((1, 128), F32), jax.ShapeDtypeStruct((1, 128), F32),
                   jax.ShapeDtypeStruct(slab.shape, BF16)],
        input_output_aliases={8: 4},
        scratch_shapes=[pltpu.VMEM((IN_WIDTH, D_MODEL), BF16), pltpu.VMEM((t, IN_WIDTH), BF16),
                        pltpu.VMEM((IN_WIDTH, D_MODEL), F32), pltpu.VMEM((IN_WIDTH, D_MODEL), BF16),
                        pltpu.SemaphoreType.DMA((N_CHIPS,))],
        compiler_params=_params(),
    )(dqst, zqk, dk, dv, du, x2, dh1, hn1, slab, *wts, g_attn, gq, gk)


def _dw(lefts, b, name, slab, slab_rows, row_offs):
    a0, n_a = lefts[0], len(lefts)
    assert b.shape[1] == D_MODEL
    if a0.ndim == 3:
        n_chunks, s_len, tm = a0.shape
        m = n_chunks * tm
    else:
        s_len, tm = a0.shape
        m = tm
    tk = 2048 if n_a * tm <= 1408 else 1024
    if a0.ndim == 3:
        a_spec = pl.BlockSpec((None, tk, tm), lambda i, k: (i, k, 0))
    else:
        a_spec = pl.BlockSpec((tk, tm), lambda i, k: (k, i))
    n_steps, n_tiles = s_len // tk, m // tm
    chunk = m // N_CHIPS
    per_tile = tm // chunk

    def body(*refs):
        a_refs, b_ref = refs[:n_a], refs[n_a]
        o_ref, acc_ref, stage_ref, sems = refs[-4:]
        i, k = pl.program_id(0), pl.program_id(1)
        b_tile = b_ref[...].astype(BF16)
        for w, a_ref in enumerate(a_refs):
            _accumulate_tn(acc_ref.at[w], a_ref[...].astype(BF16), b_tile, k == 0)

        def out_copies(tile, slot):
            return [pltpu.make_async_copy(stage_ref.at[slot, w, pl.ds(jj * chunk, chunk), :],
                                          o_ref.at[tile * per_tile + jj, pl.ds(row_offs[w], chunk), :], sems.at[slot, w, jj])
                    for w in range(n_a) for jj in range(per_tile)]

        @pl.when(k == n_steps - 1)
        def _():
            slot = i % 2

            @pl.when(i >= 2)
            def _():
                for cp in out_copies(i - 2, slot):
                    cp.wait()

            stage_ref[slot] = acc_ref[...].astype(BF16)
            for cp in out_copies(i, slot):
                cp.start()

            @pl.when(i == n_tiles - 1)
            def _():
                for cp in out_copies(i, slot):
                    cp.wait()
                if n_tiles > 1:
                    for cp in out_copies(i - 1, 1 - slot):
                        cp.wait()

    in_specs = [a_spec] * n_a + [pl.BlockSpec((tk, D_MODEL), lambda i, k: (k, 0))]
    operands, aliases = [*lefts, b], {}
    if slab is not None:
        in_specs.append(ANY)
        operands.append(slab)
        aliases = {n_a + 1: 0}
    return pl.pallas_call(
        body, name=name, grid=(n_tiles, n_steps), in_specs=in_specs, out_specs=ANY,
        out_shape=jax.ShapeDtypeStruct((N_CHIPS, slab_rows, D_MODEL), BF16), input_output_aliases=aliases,
        scratch_shapes=[pltpu.VMEM((n_a, tm, D_MODEL), F32), pltpu.VMEM((2, n_a, tm, D_MODEL), BF16),
                        pltpu.SemaphoreType.DMA((2, n_a, per_tile))],
        compiler_params=_params(VMEM_LIMIT_BIG, n_axes=2),
    )(*operands)


def _position():
    x, y, c = lax.axis_index("x"), lax.axis_index("y"), lax.axis_index("c")
    other_chips = [(1 - x, y), (x, 1 - y), (1 - x, 1 - y)]
    return x, y, c, other_chips


def _ag_weights(local_slab, row0, n_rows, name, collective_id):
    half = n_rows // 2
    quarter = half // 2
    assert quarter % 16 == 0

    def body(l_ref, g_ref, send, recv):
        x, y, c, chips = _position()
        me, (via_x, via_y, diagonal) = 2 * x + y, [2 * chip[0] + chip[1] for chip in chips]
        here, sibling, x_nbr, y_nbr = (x, y, c), (x, y, 1 - c), (1 - x, y, c), (x, 1 - y, c)
        peers = [sibling, x_nbr, y_nbr]
        barrier = pltpu.get_barrier_semaphore()
        for peer in peers:
            pl.semaphore_signal(barrier, inc=1, device_id=peer, device_id_type=MESH)
        pl.semaphore_wait(barrier, len(peers))

        def rows(core, part):
            start, size = (core * half, half) if part is None else (core * half + part * quarter, quarter)
            return pl.ds(pl.multiple_of(start, 16), size)

        def copy(k, chip_idx, where, to, src=None):
            dst = g_ref.at[chip_idx, where, :]
            return pltpu.make_async_remote_copy(src_ref=dst if src is None else src, dst_ref=dst, send_sem=send.at[k],
                                                recv_sem=recv.at[k], device_id=to, device_id_type=MESH)

        own_rows = l_ref.at[pl.ds(pl.multiple_of(row0 + c * half, 16), half), :]
        started = [copy(0, me, rows(c, None), x_nbr, src=own_rows), copy(1, me, rows(c, None), y_nbr, src=own_rows)]
        for cp in started:
            cp.start()
        after_arrival = [
            (copy(0, via_x, rows(c, None), here), [copy(4, via_x, rows(c, None), sibling), copy(3, via_x, rows(c, 1), y_nbr)]),
            (copy(1, via_y, rows(c, None), here), [copy(5, via_y, rows(c, None), sibling), copy(2, via_y, rows(c, 0), x_nbr)]),
            (copy(2, diagonal, rows(c, 0), here), [copy(6, diagonal, rows(c, 0), sibling)]),
            (copy(3, diagonal, rows(c, 1), here), [copy(7, diagonal, rows(c, 1), sibling)]),
        ]
        for arrival, onward in after_arrival:
            arrival.wait_recv()
            for cp in onward:
                cp.start()
            started += onward
        for cp in (copy(4, via_x, rows(1 - c, None), here), copy(5, via_y, rows(1 - c, None), here),
                   copy(6, diagonal, rows(1 - c, 0), here), copy(7, diagonal, rows(1 - c, 1), here)):
            cp.wait_recv()
        for cp in started:
            cp.wait_send()

    return pl.kernel(
        body, out_type=jax.ShapeDtypeStruct((N_CHIPS, n_rows, D_MODEL), BF16),
        mesh=plsc.ScalarSubcoreMesh(axis_name="sequencer", num_cores=1), name=name,
        scratch_types=[pltpu.SemaphoreType.DMA((8,)), pltpu.SemaphoreType.DMA((8,))],
        compiler_params=pltpu.CompilerParams(collective_id=collective_id),
    )(local_slab)


def _comm_call(body, peers_of, out_shape, n_sems, operand, name, collective_id):
    sems = [pltpu.SemaphoreType.DMA((n_sems,)), pltpu.SemaphoreType.DMA((n_sems,))]

    def with_handshake(in_ref, out_ref, send, recv):
        x, y, c, _ = _position()
        peers = peers_of(x, y, c)
        barrier = pltpu.get_barrier_semaphore()
        for peer in peers:
            pl.semaphore_signal(barrier, inc=1, device_id=peer, device_id_type=MESH)
        pl.semaphore_wait(barrier, len(peers))
        body(in_ref, out_ref, send, recv)

    return pl.kernel(with_handshake, out_type=out_shape, mesh=plsc.ScalarSubcoreMesh(axis_name="sequencer", num_cores=1),
                     name=name, scratch_types=sems, compiler_params=pltpu.CompilerParams(collective_id=collective_id))(operand)


def _rs_swap_halves(partial, name, collective_id):
    half = partial.shape[1] // 2

    def body(p_ref, r_ref, send, recv):
        x, y, c, _ = _position()
        theirs = pl.ds(pl.multiple_of((1 - c) * half, 16), half)
        cp = pltpu.make_async_remote_copy(src_ref=p_ref.at[:, theirs, :], dst_ref=r_ref, send_sem=send.at[0],
                                          recv_sem=recv.at[0], device_id=(x, y, 1 - c), device_id_type=MESH)
        cp.start()
        cp.wait()

    return _comm_call(body, lambda x, y, c: [(x, y, 1 - c)], jax.ShapeDtypeStruct((N_CHIPS, half, D_MODEL), BF16), 1,
                      partial, name, collective_id)


def _gather_chip_sums(s_ref, sib_ref, sum_ref, o_ref, send, recv):
    x, y, c, chips = _position()
    me, sibling, here = 2 * x + y, (x, y, 1 - c), (x, y, c)
    half = s_ref.shape[0] // 2

    def rows(core):
        return pl.ds(pl.multiple_of(core * half, 8), half)

    def copy(k, src, dst, to):
        return pltpu.make_async_remote_copy(src_ref=src, dst_ref=dst, send_sem=send.at[k], recv_sem=recv.at[k], device_id=to,
                                            device_id_type=MESH)

    swap = copy(0, s_ref, sib_ref, sibling)
    keep = pltpu.make_async_copy(sum_ref, o_ref.at[me], send.at[7])
    sends = [copy(1 + k, sum_ref.at[rows(c), :], o_ref.at[me, rows(c), :], (*chip, c)) for k, chip in enumerate(chips)]

    def landed(chip, core):
        return o_ref.at[2 * chip[0] + chip[1], rows(core), :]

    def add_and_send():
        swap.wait_recv()
        sum_ref[...] = s_ref[...] + sib_ref[...]
        keep.start()
        for cp in sends:
            cp.start()

    def finish():
        passed = []
        for k, chip in enumerate(chips):
            copy(1 + k, landed(chip, c), landed(chip, c), here).wait_recv()
            fwd = copy(4 + k, landed(chip, c), landed(chip, c), sibling)
            fwd.start()
            passed.append(fwd)
        for k, chip in enumerate(chips):
            copy(4 + k, landed(chip, 1 - c), landed(chip, 1 - c), here).wait_recv()
        for cp in [swap] + sends + passed:
            cp.wait_send()
        keep.wait()

    return swap.start, add_and_send, finish


def _rs_add_halves(partial, other, core, name, after, small=None):
    half = other.shape[1]
    t = half // 2
    steps = half // t

    def body(core_ref, a_ref, b_ref, after_ref, *rest):
        del after_ref
        o_ref = rest[0] if small is None else rest[1]
        if small is not None:
            small_ref, _, t_ref, sib_ref, sum_ref, t_send, t_recv = rest
            swap, add_and_send, finish_tables = _gather_chip_sums(small_ref, sib_ref, sum_ref, t_ref, t_send, t_recv)
            step = pl.program_id(0) * steps + pl.program_id(1)
            pl.when(step == 0)(swap)
            pl.when(step == 1)(add_and_send)
        o_ref[...] = (a_ref[...].astype(F32) + b_ref[...].astype(F32)).astype(BF16)
        if small is not None:
            pl.when(step == N_CHIPS * steps - 1)(finish_tables)

    t_in, t_out, t_scratch = [], [], []
    if small is not None:
        t_in, t_out = [VMEM_WHOLE], [jax.ShapeDtypeStruct((N_CHIPS, *small.shape), F32)]
        t_scratch = [pltpu.VMEM(small.shape, F32)] * 2 + [pltpu.SemaphoreType.DMA((8,))] * 2
    res = pl.pallas_call(
        body, name=name,
        grid_spec=pltpu.PrefetchScalarGridSpec(
            num_scalar_prefetch=1, grid=(N_CHIPS, steps),
            in_specs=[pl.BlockSpec((1, t, D_MODEL), lambda j, i, core_ref: (j, core_ref[0] * steps + i, 0)),
                      pl.BlockSpec((1, t, D_MODEL), lambda j, i, core_ref: (j, i, 0)), ANY] + t_in,
            out_specs=[pl.BlockSpec((1, t, D_MODEL), lambda j, i, core_ref: (j, i, 0))] + [ANY] * len(t_out),
            scratch_shapes=t_scratch),
        out_shape=[jax.ShapeDtypeStruct((N_CHIPS, half, D_MODEL), BF16)] + t_out,
        compiler_params=_params(n_axes=2),
    )(core, partial, other, after, *([] if small is None else [small]))
    return res[0] if small is None else res


def _rs_exchange_chips(pre, name, collective_id):
    def body(s_ref, r_ref, send, recv):
        x, y, c, chips = _position()

        def copy(k, chunk, to):
            return pltpu.make_async_remote_copy(src_ref=s_ref.at[chunk], dst_ref=r_ref.at[k], send_sem=send.at[k],
                                                recv_sem=recv.at[k], device_id=to, device_id_type=MESH)

        sends = [copy(k, 2 * chip[0] + chip[1], (*chip, c)) for k, chip in enumerate(chips)]
        for cp in sends:
            cp.start()
        for cp in sends:
            cp.wait()

    return _comm_call(body, lambda x, y, c: [(1 - x, y, c), (x, 1 - y, c), (1 - x, 1 - y, c)],
                      jax.ShapeDtypeStruct((3, pre.shape[1], D_MODEL), BF16), 3, pre, name, collective_id)


def _gather_small(s_ref, t_ref, send, recv):
    x, y, c, chips = _position()
    sibling = (x, y, 1 - c)

    def slot(px, py, pc):
        return t_ref.at[4 * px + 2 * py + pc]

    def copy(k, block, to, src=None):
        return pltpu.make_async_remote_copy(src_ref=slot(*block) if src is None else src, dst_ref=slot(*block),
                                            send_sem=send.at[k], recv_sem=recv.at[k], device_id=to, device_id_type=MESH)

    own = pltpu.make_async_copy(s_ref, slot(x, y, c), send.at[7])
    first = [copy(0, (x, y, c), sibling, src=s_ref)]
    first += [copy(1 + k, (x, y, c), (*chip, c), src=s_ref) for k, chip in enumerate(chips)]

    def start():
        own.start()
        for cp in first:
            cp.start()

    def finish():
        passed = []
        for k, chip in enumerate(chips):
            copy(1 + k, (*chip, c), (x, y, c)).wait_recv()
            fwd = copy(4 + k, (*chip, c), sibling)
            fwd.start()
            passed.append(fwd)
        copy(0, sibling, (x, y, c)).wait_recv()
        for k, chip in enumerate(chips):
            copy(4 + k, (*chip, 1 - c), (x, y, c)).wait_recv()
        for cp in first + passed:
            cp.wait_send()
        own.wait()

    return start, finish


def _table_gather_parts(small):
    if small is None:
        return [], [], []
    return [VMEM_WHOLE], [jax.ShapeDtypeStruct((N_DEV, *small.shape), F32)], [pltpu.SemaphoreType.DMA((8,))] * 2


def _rs_sum_chips(pre, received, place, name, after, small=None):
    half = pre.shape[1]
    steps = 4 if half > 512 else 2
    t = half // steps
    assert t % 16 == 0 and t * steps == half

    def body(place_ref, own_ref, r_ref, after_ref, *rest):
        del place_ref, after_ref
        if small is None:
            o_ref, stage, kept_sems, send, recv = rest
        else:
            small_ref, o_ref, t_ref, stage, kept_sems, send, recv, t_send, t_recv = rest
            start_tables, finish_tables = _gather_small(small_ref, t_ref, t_send, t_recv)
            pl.when(pl.program_id(0) == 0)(start_tables)
        i = pl.program_id(0)
        x, y, c, _ = _position()

        def rows(core, step):
            return o_ref.at[pl.ds(pl.multiple_of((core * steps + step) * t, 8), t), :]

        def kept(step):
            return pltpu.make_async_copy(stage.at[step], rows(c, step), kept_sems.at[step])

        def sent(core, step):
            return pltpu.make_async_remote_copy(src_ref=stage.at[step], dst_ref=rows(core, step), send_sem=send.at[step],
                                                recv_sem=recv.at[step], device_id=(x, y, 1 - core), device_id_type=MESH)

        acc = own_ref[0].astype(F32)
        for k in range(3):
            acc = acc + r_ref[k].astype(F32)
        stage[i] = acc
        kept(i).start()
        sent(c, i).start()

        @pl.when(i == steps - 1)
        def _():
            if small is not None:
                finish_tables()
            for step in range(steps):
                kept(step).wait()
                sent(c, step).wait_send()
                sent(1 - c, step).wait_recv()

    t_in, t_out, t_scratch = _table_gather_parts(small)
    res = pl.pallas_call(
        body, name=name,
        grid_spec=pltpu.PrefetchScalarGridSpec(
            num_scalar_prefetch=1, grid=(steps,),
            in_specs=[pl.BlockSpec((1, t, D_MODEL), lambda i, place_ref: (place_ref[0], i, 0)),
                      pl.BlockSpec((3, t, D_MODEL), lambda i, place_ref: (0, i, 0)), ANY] + t_in,
            out_specs=[ANY] * (1 + len(t_out)),
            scratch_shapes=[pltpu.VMEM((steps, t, D_MODEL), F32)] + [pltpu.SemaphoreType.DMA((steps,))] * 3 + t_scratch),
        out_shape=[jax.ShapeDtypeStruct((2 * half, D_MODEL), F32)] + t_out, compiler_params=_params(),
    )(place, pre, received, after, *([] if small is None else [small]))
    return res[0] if small is None else res


def _adam_update(w, g, m, v):
    m_new = ADAM_B1 * m + (1.0 - ADAM_B1) * g
    v_new = ADAM_B2 * v + (1.0 - ADAM_B2) * (g * g)
    m_hat = m_new / (1.0 - ADAM_B1 ** ADAM_STEP)
    v_hat = v_new / (1.0 - ADAM_B2 ** ADAM_STEP)
    return -ADAM_LR * (m_hat / (jnp.sqrt(v_hat) + ADAM_EPS) + ADAM_WD * w), m_new, v_new


def _adamw(w, g_rows, row_off, m, v, name):
    rows, cols = w.shape
    t = rows if rows <= 320 else (rows // 2 if rows % 256 else 256)

    def body(w_ref, g_ref, m_ref, v_ref, go_ref, d_ref, nm_ref, nv_ref):
        g = g_ref[...]
        go_ref[...] = g
        d_ref[...], nm_ref[...], nv_ref[...] = _adam_update(w_ref[...], g, m_ref[...], v_ref[...])

    blk = pl.BlockSpec((t, cols), lambda i: (i, 0))
    assert row_off % 8 == 0 and t % 8 == 0
    g_blk = pl.BlockSpec((pl.Element(t), pl.Element(cols)), lambda i: (pl.multiple_of(row_off + i * t, 8), 0))
    shape = jax.ShapeDtypeStruct((rows, cols), F32)
    return pl.pallas_call(
        body, name=name, grid=(rows // t,), in_specs=[blk, g_blk, blk, blk], out_specs=[blk] * 4, out_shape=[shape] * 4,
        compiler_params=_params(),
    )(w, g_rows, m, v)


SMALL_PARAMS = [("g_attn", (1, D_MODEL), 8), ("g_q", (1, HEAD_DIM), None), ("g_k", (1, HEAD_DIM), None),
                ("sinks", (1, N_Q_HEADS), None), ("rel_bias", (N_Q_HEADS, N_BUCKETS), None), ("w_pool", (512, 128), None),
                ("pool_scale", (1, POOL_WIDTH), 4), ("g_ffn", (1, D_MODEL), 8), ("g_ple", (1, D_MODEL), 8)]


def _adamw_small(tables, pool_tables, wmv):
    n_par = len(SMALL_PARAMS)

    def body(*refs):
        t_ref, p_ref = refs[:2]
        ins = refs[2:2 + 3 * n_par]
        loss_ref = refs[2 + 3 * n_par]
        outs = refs[3 + 3 * n_par:-1]
        tot_ref = refs[-1]

        def in_order(ref):
            total = ref[0]
            for d in range(1, ref.shape[0]):
                total = total + ref[d]
            return total

        tot_ref[...] = in_order(t_ref)
        loss_ref[...] = tot_ref[pl.ds(SMALL["loss"], 1), 0:1]
        for i, (name, shape, split) in enumerate(SMALL_PARAMS):
            g_ref, d_ref, nm_ref, nv_ref = outs[4 * i:4 * i + 4]
            row = SMALL.get(name)
            if name == "w_pool":
                g_ref[...] = in_order(p_ref)
            elif split:
                for k in range(split):
                    g_ref[:, 128 * k:128 * k + 128] = tot_ref[pl.ds(row + k, 1), :]
            else:
                g_ref[...] = tot_ref[pl.ds(row, shape[0]), 0:shape[1]]
            w_ref, m_ref, v_ref = ins[3 * i:3 * i + 3]
            d_ref[...], nm_ref[...], nv_ref[...] = _adam_update(w_ref[...], g_ref[...], m_ref[...], v_ref[...])

    shapes = [jax.ShapeDtypeStruct((1, 1), F32)]
    for _, shape, _ in SMALL_PARAMS:
        shapes += [jax.ShapeDtypeStruct(shape, F32)] * 4
    flat = [a for triple in wmv for a in triple]
    res = pl.pallas_call(
        body, name="adamw_small", in_specs=[VMEM_WHOLE] * (2 + 3 * n_par), out_specs=[VMEM_WHOLE] * len(shapes),
        out_shape=shapes, scratch_shapes=[pltpu.VMEM((SMALL_ROWS, 128), F32)],
    )(tables, pool_tables, *flat)
    return res[0], [res[1 + 4 * i:5 + 4 * i] for i in range(n_par)]


def _pack_ple_proj(shard):
    return shard.reshape(4, 64, 256).transpose(1, 0, 2).reshape(64, D_MODEL)


class _Reduction:
    def __init__(self, tag, place, ids=(None, None)):
        self.tag, self.place, self.ids = tag, place, ids

    def start(self, partial):
        self.partial = partial
        self.other = _rs_swap_halves(partial, "rs_swap_" + self.tag, self.ids[0])
        return partial

    def middle(self, after, small=None):
        res = _rs_add_halves(self.partial, self.other, self.place[1:], "rs_add_" + self.tag, after, small)
        self.pre, self.tables = (res, None) if small is None else res
        self.received = _rs_exchange_chips(self.pre, "rs_exchange_" + self.tag, self.ids[1])
        return self.pre

    def finish(self, after, small=None):
        return _rs_sum_chips(self.pre, self.received, self.place, "rs_sum_" + self.tag, after, small)


def _local_grads(x2, p2, tgt, wts, g_attn_norm, g_q, g_k, attn_sinks, rel_bias, w_pool, pool_scale, g_ffn_norm, g_ple_norm,
                 reduce_a):
    w_early, w_late = wts
    w_in = w_out = w_early
    bucket = jnp.asarray(_bucket_table())
    gq = jnp.tile(g_q, (1, 2))
    gk = jnp.tile(g_k, (1, 2))
    wpool = w_pool[0].astype(BF16)
    sinks = attn_sinks[0]
    bias_st = _bias_build(rel_bias.T, bucket)

    hn1 = _first_norm(x2, g_attn_norm)
    zqk, u, kn, vb, qst = _attn_in(hn1, gq, gk, w_in)
    ost = _attn_fwd(qst, kn, vb, bias_st, sinks)
    pooled, mix, h1, hn2 = _mix_out(u, ost, x2, w_out, wpool, pool_scale, g_ffn_norm)
    loss_v, dgate, dup, act, dh2, hn3, dgl, dw_plp, dh1, dg_ffn, dg_ple = _ffn_ple(hn2, h1, p2, tgt, w_late, g_ffn_norm,
                                                                                      g_ple_norm)

    late0, late_rows = GATHER_PARTS[1][0], SLAB_ROWS - GATHER_PARTS[1][0]
    partial_a = None
    for names, lefts, right in ((("gateT", "upT"), [dgate, dup], hn2), (("down",), [act], dh2), (("plg",), [hn3], dgl)):
        partial_a = _dw(lefts, right, "dw_" + names[0], partial_a, late_rows, [SLAB[name][0] - late0 for name in names])
    dw_plp = dw_plp.reshape(4, 64, N_CHIPS, 256).transpose(2, 1, 0, 3).reshape(N_CHIPS, 64, D_MODEL)
    partial_a = reduce_a.start(lax.dynamic_update_slice(partial_a, dw_plp, (0, SLAB["plp"][0] - late0, 0)))
    dost, du, dw_pool, dscale, partial_b = _mix_out_bwd(dh1, w_out, pooled, wpool, pool_scale, mix, partial_a)
    pre_a = reduce_a.middle(du, dw_pool.reshape(512, 128))
    dqst, dk, dv, dbias, dsink_rows = _attn_bwd(qst, kn, vb, dost, bias_st, sinks, pre_a)
    dx, dg_attn, dgq, dgk, partial_b = _attn_in_bwd(dqst, zqk, dk, dv, du, x2, dh1, hn1, partial_b, w_in, g_attn_norm, gq, gk)

    small = _small_pack(dg_attn, dg_ffn, dg_ple, dscale, dgq, dgk, dbias, dsink_rows, loss_v)
    return dx, partial_b, small


def kernel(x, p, w_in, w_out, g_attn_norm, g_q, g_k, attn_sinks, rel_bias, w_pool, pool_scale, g_ffn_norm, w_gate, w_up, w_down, g_ple_norm, w_ple_gate, w_ple_proj, loss_target, m_w_in, m_w_out, m_g_attn_norm, m_g_q, m_g_k, m_attn_sinks, m_rel_bias, m_w_pool, m_pool_scale, m_g_ffn_norm, m_w_gate, m_w_up, m_w_down, m_g_ple_norm, m_w_ple_gate, m_w_ple_proj, v_w_in, v_w_out, v_g_attn_norm, v_g_q, v_g_k, v_attn_sinks, v_rel_bias, v_w_pool, v_pool_scale, v_g_ffn_norm, v_w_gate, v_w_up, v_w_down, v_g_ple_norm, v_w_ple_gate, v_w_ple_proj):
    core = lax.axis_index("c").astype(jnp.int32).reshape(1)
    me = (2 * lax.axis_index("x") + lax.axis_index("y")).astype(jnp.int32).reshape(1)

    local_parts = [jnp.concatenate(pieces, axis=0).astype(BF16) for pieces in (
        [w_in[0].T, w_out[0]], [w_gate[0].T, w_up[0].T, w_down[0], w_ple_gate[0], _pack_ple_proj(w_ple_proj[0])])]
    wts = [(_ag_weights(local, 0, local.shape[0], name, collective_id), local, me)
           for local, name, collective_id in zip(local_parts, ("ag_early", "ag_late"), (1, 2))]

    place = jnp.concatenate([me, core])
    reduce_a = _Reduction("a", place, ids=(3, 4))
    dx, partial_b, small = _local_grads(x[0], p[0, 0], loss_target[0], wts, g_attn_norm, g_q, g_k, attn_sinks, rel_bias,
                                        w_pool, pool_scale, g_ffn_norm, g_ple_norm, reduce_a)
    reduce_b = _Reduction("b", place, ids=(6, 7))
    reduce_b.start(partial_b)
    grads_a, small_all = reduce_a.finish(partial_b, small)
    reduce_b.middle(grads_a)

    late0 = GATHER_PARTS[1][0]

    def rows(name):
        return grads_a, SLAB[name][0] - late0

    plp_rows = grads_a[SLAB["plp"][0] - late0:]
    big = {
        "w_gate": (w_gate, m_w_gate, v_w_gate, rows("gateT"), True),
        "w_up": (w_up, m_w_up, v_w_up, rows("upT"), True),
        "w_down": (w_down, m_w_down, v_w_down, rows("down"), False),
        "w_ple_gate": (w_ple_gate, m_w_ple_gate, v_w_ple_gate, rows("plg"), False),
        "w_ple_proj": (w_ple_proj, m_w_ple_proj, v_w_ple_proj,
                       (plp_rows.reshape(64, 4, 256).transpose(1, 0, 2).reshape(PLE_DIM, PLE_DIM), 0), False),
        "w_out": (w_out, m_w_out, v_w_out, None, False),
        "w_in": (w_in, m_w_in, v_w_in, None, True),
    }
    small_params = {
        "g_attn_norm": (g_attn_norm, m_g_attn_norm, v_g_attn_norm), "g_q": (g_q, m_g_q, v_g_q), "g_k": (g_k, m_g_k, v_g_k),
        "attn_sinks": (attn_sinks, m_attn_sinks, v_attn_sinks), "rel_bias": (rel_bias.T, m_rel_bias.T, v_rel_bias.T),
        "w_pool": tuple(a.reshape(512, 128) for a in (w_pool, m_w_pool, v_w_pool)),
        "pool_scale": (pool_scale, m_pool_scale, v_pool_scale), "g_ffn_norm": (g_ffn_norm, m_g_ffn_norm, v_g_ffn_norm),
        "g_ple_norm": (g_ple_norm, m_g_ple_norm, v_g_ple_norm),
    }

    grads, deltas, new_ms, new_vs = {}, {}, {}, {}
    out = grads_b = None
    for name, (w, m, v, g_src, transposed) in big.items():
        if g_src is None:
            if grads_b is None:
                grads_b = reduce_b.finish(out[-1])
            g_src = (grads_b, SLAB["out" if name == "w_out" else "inT"][0])
        view = (lambda a: a.T) if transposed else (lambda a: a)
        out = _adamw(view(w[0]), *g_src, view(m[0]), view(v[0]), "adamw_" + name)
        grads[name], deltas[name], new_ms[name], new_vs[name] = (view(a)[None] for a in out)

    loss, small_out = _adamw_small(small_all, reduce_a.tables, list(small_params.values()))
    for name, (g2, d, nm, nv) in zip(small_params, small_out):
        restore = {"w_pool": lambda a: a.reshape(w_pool.shape), "rel_bias": lambda a: a.T}.get(name, lambda a: a)
        grads[name], deltas[name], new_ms[name], new_vs[name] = (restore(a) for a in (g2, d, nm, nv))

    order = ["w_in", "w_out", "g_attn_norm", "g_q", "g_k", "attn_sinks", "rel_bias", "w_pool", "pool_scale", "g_ffn_norm",
             "w_gate", "w_up", "w_down", "g_ple_norm", "w_ple_gate", "w_ple_proj"]
    return (loss.reshape(()), dx[None], *[grads[n] for n in order], *[deltas[n] for n in order],
            *[new_ms[n] for n in order], *[new_vs[n] for n in order])
```

```python
import numpy as np
import jax
import jax.numpy as jnp
from jax import lax
from jax.experimental import pallas as pl
from jax.experimental.pallas import tpu as pltpu
from jax.experimental.pallas import tpu_sc as plsc

F32 = jnp.float32
BF16 = jnp.bfloat16
MESH = pl.DeviceIdType.MESH

D_MODEL = 1024
HEAD_DIM = 64
N_Q_HEADS = 8
ATTN_WIDTH = 512
POOL_WIDTH = 512
IN_WIDTH = 1280
D_FF = 2816
PLE_DIM = 256
FF_CHUNK = 1408
N_FF_CHUNKS = D_FF // FF_CHUNK
BLOCK = 128
N_BUCKETS = 32
MAX_DISTANCE = 128
EPS = 1e-6
NEG = -1e30
N_CHIPS = 4
N_DEV = 8

ADAM_LR = 0.001
ADAM_B1 = 0.9
ADAM_B2 = 0.999
ADAM_EPS = 1e-08
ADAM_WD = 0.01
ADAM_STEP = 10

SLAB = {"inT": (0, 320), "out": (320, 256), "gateT": (576, 704), "upT": (1280, 704), "down": (1984, 704),
        "plg": (2688, 256), "plp": (2944, 64)}
SLAB_ROWS = 3008
GATHER_PARTS = ((0, 576), (576, SLAB_ROWS))
POOL_HALO = 24

SMALL = {"g_attn": 0, "g_ffn": 8, "g_ple": 16, "pool_scale": 24, "g_q": 28, "g_k": 29, "sinks": 30, "loss": 31,
         "rel_bias": 32}
SMALL_ROWS = 64

VMEM_LIMIT_BIG = 62 * 1024 * 1024
VMEM_LIMIT = 48 * 1024 * 1024


def _params(vmem=VMEM_LIMIT, n_axes=1):
    return pltpu.CompilerParams(dimension_semantics=("arbitrary",) * n_axes, vmem_limit_bytes=vmem)


def _dot(a, b, ca, cb):
    return lax.dot_general(a, b, (((ca,), (cb,)), ((), ())), preferred_element_type=F32)


def _full(shape):
    return pl.BlockSpec(shape, lambda i: (0,) * len(shape))


ANY = pl.BlockSpec(memory_space=pl.ANY)
VMEM_WHOLE = pl.BlockSpec(memory_space=pltpu.VMEM)


W_SPECS = [ANY, ANY, pl.BlockSpec(memory_space=pltpu.SMEM)]


def _load_rows(w_refs, name, dst_ref, sems):
    slab_ref, local_ref, me_ref = w_refs
    off, rows = SLAB[name]
    slab_off = off - max(start for start, _ in GATHER_PARTS if start <= off)
    me = me_ref[0]
    for phase in ("start", "wait"):
        for j in range(N_CHIPS):
            dst = dst_ref.at[pl.ds(j * rows, rows), :]
            theirs = pltpu.make_async_copy(slab_ref.at[j, pl.ds(slab_off, rows), :], dst, sems.at[j])
            own = pltpu.make_async_copy(local_ref.at[pl.ds(slab_off, rows), :], dst, sems.at[j])

            @pl.when(me == j)
            def _():
                getattr(own, phase)()

            @pl.when(me != j)
            def _():
                getattr(theirs, phase)()


def _rms_fwd(x, g):
    r = lax.rsqrt(jnp.mean(x * x, axis=-1, keepdims=True) + EPS)
    return x * r * g


def _rms_bwd(x, g, dy):
    r = lax.rsqrt(jnp.mean(x * x, axis=-1, keepdims=True) + EPS)
    xn = x * r
    dyg = dy * g
    dx = r * (dyg - xn * jnp.mean(dyg * xn, axis=-1, keepdims=True))
    return dx, jnp.sum(dy * xn, axis=0, keepdims=True)


def _half_sum(v, lo):
    s_lo = jnp.sum(jnp.where(lo, v, 0.0), axis=-1, keepdims=True)
    s_hi = jnp.sum(jnp.where(lo, 0.0, v), axis=-1, keepdims=True)
    return jnp.where(lo, s_lo, s_hi)


def _half_sum_mxu(v):
    upper = lax.broadcasted_iota(jnp.int32, (128, 128), 0) < 64
    left = lax.broadcasted_iota(jnp.int32, (128, 128), 1) < 64
    ones = jnp.where(upper == left, 1.0, 0.0).astype(BF16)
    high = v.astype(BF16)
    low = (v - high.astype(F32)).astype(BF16)
    return _dot(high, ones, 1, 0) + _dot(low, ones, 1, 0)


def _pair_norm(zp, g, lo):
    r = lax.rsqrt(_half_sum(zp * zp, lo) * (1.0 / HEAD_DIM) + EPS)
    return zp * r * g


def _pair_norm_bwd(zp, g, dy):
    r = lax.rsqrt(_half_sum_mxu(zp * zp) * (1.0 / HEAD_DIM) + EPS)
    xn = zp * r
    dyg = dy * g
    dx = r * (dyg - xn * (_half_sum_mxu(dyg * xn) * (1.0 / HEAD_DIM)))
    return dx, jnp.sum(dy * xn, axis=0, keepdims=True)


def _pack_heads(pairs, lo):
    packed = [None] * 4
    for m in range(2):
        a, b = pairs[m], pairs[m + 2]
        packed[2 * m] = jnp.where(lo, a, pltpu.roll(b, 64, axis=1))
        packed[2 * m + 1] = jnp.where(lo, pltpu.roll(a, 64, axis=1), b)
    return packed


def _unpack_heads(packed, lo):
    pairs = [None] * 4
    for m in range(2):
        a, b = packed[2 * m], packed[2 * m + 1]
        pairs[m] = jnp.where(lo, a, pltpu.roll(b, 64, axis=1))
        pairs[m + 2] = jnp.where(lo, pltpu.roll(a, 64, axis=1), b)
    return pairs


def _expand_heads(packed):
    flat = packed.reshape(4 * BLOCK, 128)
    lo = lax.broadcasted_iota(jnp.int32, flat.shape, 1) < 64
    zero = jnp.zeros_like(flat)
    return jnp.concatenate([jnp.where(lo, flat, zero), jnp.where(lo, zero, flat)], axis=0)


def _fold_heads(stacked):
    half = 4 * BLOCK
    lo = lax.broadcasted_iota(jnp.int32, (half, 128), 1) < 64
    return jnp.where(lo, stacked[:half], stacked[half:]).reshape(4, BLOCK, 128)


def _sigmoid(v):
    return 1.0 / (1.0 + jnp.exp(-v))


def _pool_counts(tile, n_rows):
    t1 = tile * n_rows + lax.broadcasted_iota(jnp.int32, (n_rows, POOL_WIDTH), 0) + 1
    lane = lax.broadcasted_iota(jnp.int32, (n_rows, POOL_WIDTH), 1)
    win = jnp.where(lane < 128, 2, jnp.where(lane < 256, 4, jnp.where(lane < 384, 8, 16)))
    return jnp.minimum(t1, win).astype(F32)


def _first_norm(x2, g_attn):
    s_len = x2.shape[0]
    t = 512

    def body(x_ref, g_ref, hn_ref):
        hn_ref[...] = _rms_fwd(x_ref[...], g_ref[...]).astype(BF16)

    row = pl.BlockSpec((t, D_MODEL), lambda i: (i, 0))
    return pl.pallas_call(
        body, name="first_norm", grid=(s_len // t,), in_specs=[row, _full((1, D_MODEL))], out_specs=row,
        out_shape=jax.ShapeDtypeStruct((s_len, D_MODEL), BF16), compiler_params=_params(),
    )(x2, g_attn)


def _attn_in(hn1, gq, gk, wts):
    s_len = hn1.shape[0]
    t = 512

    def body(hn_ref, gq_ref, gk_ref, sl_ref, lo_ref, me_ref, zqk_ref, u_ref, kn_ref, v_ref, qst_ref, w_ref, sems):
        @pl.when(pl.program_id(0) == 0)
        def _():
            _load_rows((sl_ref, lo_ref, me_ref), "inT", w_ref, sems)

        z = _dot(hn_ref[...], w_ref[...], 1, 1)
        zqk_ref[...] = z[:, :640]
        u_ref[...] = z[:, 768:]
        v_ref[...] = z[:, 640:768].astype(BF16)
        lo = lax.broadcasted_iota(jnp.int32, (t, 128), 1) < 64
        kn_ref[...] = _pair_norm(z[:, 512:640], gk_ref[...], lo).astype(BF16)
        pairs = [_pair_norm(z[:, 128 * p:128 * p + 128], gq_ref[...], lo) for p in range(4)]
        for j, entry in enumerate(_pack_heads(pairs, lo)):
            qst_ref[j] = entry.astype(BF16)

    row = lambda w: pl.BlockSpec((t, w), lambda i: (i, 0))
    return pl.pallas_call(
        body, name="attn_in", grid=(s_len // t,),
        in_specs=[row(D_MODEL), _full((1, 128)), _full((1, 128))] + W_SPECS,
        out_specs=[row(640), row(POOL_WIDTH), row(128), row(128), pl.BlockSpec((4, t, 128), lambda i: (0, i, 0))],
        out_shape=[jax.ShapeDtypeStruct((s_len, 640), F32), jax.ShapeDtypeStruct((s_len, POOL_WIDTH), F32),
                   jax.ShapeDtypeStruct((s_len, 128), BF16), jax.ShapeDtypeStruct((s_len, 128), BF16),
                   jax.ShapeDtypeStruct((4, s_len, 128), BF16)],
        scratch_shapes=[pltpu.VMEM((IN_WIDTH, D_MODEL), BF16), pltpu.SemaphoreType.DMA((N_CHIPS,))],
        compiler_params=_params(),
    )(hn1, gq, gk, *wts)


def _bucket_table():
    i_idx = np.arange(BLOCK)[:, None]
    j_idx = np.arange(2 * BLOCK)[None, :]
    d = BLOCK + i_idx - j_idx
    n = np.maximum(d, 0)
    max_exact = N_BUCKETS // 2
    nf = np.maximum(n, 1).astype(np.float64)
    large = max_exact + (np.log(nf / max_exact) / np.log(MAX_DISTANCE / max_exact) * (N_BUCKETS - max_exact)).astype(np.int64)
    large = np.minimum(large, N_BUCKETS - 1)
    bucket = np.where(n < max_exact, n, large)
    return np.where((d >= 0) & (d < BLOCK), bucket, -1).astype(np.int32)


def _bias_build(rel_bias_t, bucket):
    def body(rb_ref, bucket_ref, out_ref):
        bk = bucket_ref[...]
        for h in range(N_Q_HEADS):
            acc = jnp.full((BLOCK, 2 * BLOCK), NEG, F32)
            for b in range(N_BUCKETS):
                acc = jnp.where(bk == b, rb_ref[h, b], acc)
            out_ref[0, pl.ds(h * BLOCK, BLOCK), :] = acc
            out_ref[1, pl.ds(h * BLOCK, BLOCK), :] = acc
            out_ref[1, pl.ds(h * BLOCK, BLOCK), 0:BLOCK] = jnp.full((BLOCK, BLOCK), NEG, F32)

    return pl.pallas_call(
        body, name="bias_build",
        in_specs=[pl.BlockSpec(memory_space=pltpu.SMEM), VMEM_WHOLE], out_specs=VMEM_WHOLE,
        out_shape=jax.ShapeDtypeStruct((2, N_Q_HEADS * BLOCK, 2 * BLOCK), F32),
    )(rel_bias_t, bucket)


def _head_softmax(s_ref, bias_ref, sink_ref, h):
    rows = pl.ds(pl.multiple_of(h * BLOCK, BLOCK), BLOCK)
    s = s_ref[rows, :] * (HEAD_DIM ** -0.5) + bias_ref[rows, :]
    sink = sink_ref[h]
    m = jnp.maximum(jnp.max(s, axis=-1, keepdims=True), sink)
    p = jnp.exp(s - m)
    e_sink = jnp.exp(sink - m)
    inv = 1.0 / (jnp.sum(p, axis=-1, keepdims=True) + e_sink)
    return rows, p * inv, e_sink * inv


ATTN_STEP_BLOCKS = 4
BAND = (N_Q_HEADS * BLOCK, 2 * BLOCK)


def _attn_specs():
    nb = ATTN_STEP_BLOCKS
    stacked = pl.BlockSpec((4, nb * BLOCK, 128), lambda i: (0, i, 0))
    kv = [pl.BlockSpec((BLOCK, 128), lambda i: (jnp.maximum(nb * i - 1, 0), 0)), pl.BlockSpec((nb * BLOCK, 128), lambda i: (i, 0))]
    consts = [_full((2,) + BAND), pl.BlockSpec(memory_space=pltpu.SMEM)]
    return stacked, kv, consts


def _step_blocks(i, kp_ref, kc_ref, vp_ref, vc_ref, bias_ref):
    blocks = []
    for b in range(ATTN_STEP_BLOCKS):
        if b == 0:
            k2 = jnp.concatenate([kp_ref[...], kc_ref[pl.ds(0, BLOCK), :]], axis=0)
            v2 = jnp.concatenate([vp_ref[...], vc_ref[pl.ds(0, BLOCK), :]], axis=0)
            bias = bias_ref.at[jnp.where(i == 0, 1, 0)]
        else:
            k2, v2, bias = kc_ref[pl.ds((b - 1) * BLOCK, 2 * BLOCK), :], vc_ref[pl.ds((b - 1) * BLOCK, 2 * BLOCK), :], bias_ref.at[0]
        blocks.append((pl.ds(b * BLOCK, BLOCK), k2, v2, bias))
    return blocks


def _attn_fwd(qst, kn, vb, bias_st, sinks):
    s_len = kn.shape[0]

    def body(q_ref, kp_ref, kc_ref, vp_ref, vc_ref, bias_ref, sink_ref, o_ref, s_ref, p_ref):
        for b, (rows, k2, v2, bias) in enumerate(_step_blocks(pl.program_id(0), kp_ref, kc_ref, vp_ref, vc_ref, bias_ref)):
            s_b, p_b = s_ref.at[b], p_ref.at[b]
            s_b[...] = _dot(_expand_heads(q_ref[:, rows, :]), k2, 1, 1)

            def head(h, carry):
                head_rows, probs, _ = _head_softmax(s_b, bias, sink_ref, h)
                p_b[head_rows, :] = probs.astype(BF16)
                return carry

            lax.fori_loop(0, N_Q_HEADS, head, 0, unroll=True)
            o_ref[:, rows, :] = _fold_heads(_dot(p_b[...], v2, 1, 0)).astype(BF16)

    stacked, kv, consts = _attn_specs()
    return pl.pallas_call(
        body, name="attn_fwd", grid=(s_len // (ATTN_STEP_BLOCKS * BLOCK),),
        in_specs=[stacked] + kv + kv + consts, out_specs=stacked,
        out_shape=jax.ShapeDtypeStruct((4, s_len, 128), BF16),
        scratch_shapes=[pltpu.VMEM((ATTN_STEP_BLOCKS,) + BAND, F32), pltpu.VMEM((ATTN_STEP_BLOCKS,) + BAND, BF16)],
        compiler_params=_params(),
    )(qst, kn, kn, vb, vb, bias_st, sinks)


def _mix_out(u, ost, x2, wts, wpool, pool_scale, g_ffn):
    s_len = x2.shape[0]
    t = 512
    n = t + 16

    def body(u_ref, o_ref, x_ref, sl_ref, lo_ref, me_ref, wp_ref, sc_ref, g_ref, pooled_ref, mix_ref, h1_ref, hn_ref,
             w_ref, ext_ref, st_ref, sems):
        i = pl.program_id(0)

        @pl.when(i == 0)
        def _():
            _load_rows((sl_ref, lo_ref, me_ref), "out", w_ref, sems)
            ext_ref[...] = jnp.zeros_like(ext_ref)
            st_ref[...] = jnp.zeros_like(st_ref)

        u_tile = u_ref[...]
        ext_ref[pl.ds(POOL_HALO, t), :] = u_tile
        st_ref[pl.ds(8, n), :] = ext_ref[pl.ds(8, n), :] + ext_ref[pl.ds(7, n), :]
        st_ref[pl.ds(8, n), 128:] = st_ref[pl.ds(8, n), 128:] + st_ref[pl.ds(6, n), 128:]
        st_ref[pl.ds(8, n), 256:] = st_ref[pl.ds(8, n), 256:] + st_ref[pl.ds(4, n), 256:]
        st_ref[pl.ds(8, n), 384:] = st_ref[pl.ds(8, n), 384:] + st_ref[pl.ds(0, n), 384:]
        ext_ref[pl.ds(0, POOL_HALO), :] = ext_ref[pl.ds(t, POOL_HALO), :]
        pooled = (st_ref[pl.ds(POOL_HALO, t), :] / _pool_counts(i, t) - u_tile).astype(BF16)
        pooled_ref[...] = pooled
        for g in range(4):
            cols = slice(128 * g, 128 * g + 128)
            y = _dot(pooled[:, cols], wp_ref[g], 1, 0) * sc_ref[:, cols]
            mix_ref[:, ATTN_WIDTH + 128 * g:ATTN_WIDTH + 128 * g + 128] = y.astype(BF16)
        lo = lax.broadcasted_iota(jnp.int32, (t, 128), 1) < 64
        for p, pair in enumerate(_unpack_heads([o_ref[j].astype(F32) for j in range(4)], lo)):
            mix_ref[:, 128 * p:128 * p + 128] = pair.astype(BF16)
        h1 = x_ref[...] + _dot(mix_ref[...], w_ref[...], 1, 0)
        h1_ref[...] = h1
        hn_ref[...] = _rms_fwd(h1, g_ref[...]).astype(BF16)

    row = lambda w: pl.BlockSpec((t, w), lambda i: (i, 0))
    return pl.pallas_call(
        body, name="mix_out", grid=(s_len // t,),
        in_specs=[row(POOL_WIDTH), pl.BlockSpec((4, t, 128), lambda i: (0, i, 0)), row(D_MODEL)] + W_SPECS
        + [_full((4, 128, 128)), _full((1, POOL_WIDTH)), _full((1, D_MODEL))],
        out_specs=[row(POOL_WIDTH), row(D_MODEL), row(D_MODEL), row(D_MODEL)],
        out_shape=[jax.ShapeDtypeStruct((s_len, POOL_WIDTH), BF16), jax.ShapeDtypeStruct((s_len, D_MODEL), BF16),
                   jax.ShapeDtypeStruct((s_len, D_MODEL), F32), jax.ShapeDtypeStruct((s_len, D_MODEL), BF16)],
        scratch_shapes=[pltpu.VMEM((D_MODEL, D_MODEL), BF16), pltpu.VMEM((t + POOL_HALO, POOL_WIDTH), F32),
                        pltpu.VMEM((t + POOL_HALO, POOL_WIDTH), F32), pltpu.SemaphoreType.DMA((N_CHIPS,))],
        compiler_params=_params(),
    )(u, ost, x2, *wts, wpool, pool_scale, g_ffn)


def _ffn_ple(hn2, h1, p2, tgt, wts, g_ffn, g_ple):
    s_len = h1.shape[0]
    t = 256
    n_tiles = s_len // t

    def body(hn_ref, h1_ref, p_ref, tgt_ref, sl_ref, lo_ref, me_ref, gf_ref, gp_ref,
             loss_ref, dgate_ref, dup_ref, act_ref, dh2b_ref, hn3_ref, dgl_ref, dwp_ref, dh1_ref, dgf_ref, dgp_ref,
             wg_ref, wu_ref, wd_ref, wl_ref, wp_ref, packed_ref, gate_s, up_s, loss_acc, dwp_acc, sems,
             dhn_s, h1_s, dh1_s, dh1_sems):
        i = pl.program_id(0)

        def finish_tile(slot):
            dx, dg = _rms_bwd(h1_s[...], gf_ref[...], dhn_s[...])
            dh1_s[slot] += dx
            dgf_ref[...] += dg

        def dh1_copy(tile):
            slot = tile % 3
            return pltpu.make_async_copy(dh1_s.at[slot], dh1_ref.at[pl.ds(pl.multiple_of(tile * t, t), t), :], dh1_sems.at[slot])

        @pl.when(i == 0)
        def _():
            w_refs = (sl_ref, lo_ref, me_ref)
            _load_rows(w_refs, "gateT", wg_ref, sems)
            _load_rows(w_refs, "upT", wu_ref, sems)
            _load_rows(w_refs, "down", wd_ref, sems)
            _load_rows(w_refs, "plg", wl_ref, sems)
            _load_rows(w_refs, "plp", packed_ref, sems)
            for j in range(N_CHIPS):
                for q in range(4):
                    wp_ref[pl.ds(64 * q, 64), 256 * j:256 * j + 256] = packed_ref[pl.ds(64 * j, 64), 256 * q:256 * q + 256]
            loss_acc[...] = jnp.zeros_like(loss_acc)
            dwp_acc[...] = jnp.zeros_like(dwp_acc)
            dgf_ref[...] = jnp.zeros_like(dgf_ref)
            dgp_ref[...] = jnp.zeros_like(dgp_ref)
            dhn_s[...] = jnp.zeros_like(dhn_s)
            h1_s[...] = jnp.zeros_like(h1_s)
            dh1_s[...] = jnp.zeros_like(dh1_s)

        finish_tile((i + 2) % 3)
        hn = hn_ref[...]
        h1v = h1_ref[...]
        h2 = h1v
        for ch in range(N_FF_CHUNKS):
            rows = pl.ds(ch * FF_CHUNK, FF_CHUNK)
            gate = _dot(hn, wg_ref[rows, :], 1, 1)
            up = _dot(hn, wu_ref[rows, :], 1, 1)
            gate_s[ch] = gate
            up_s[ch] = up
            act = (gate * _sigmoid(gate) * up).astype(BF16)
            act_ref[ch] = act
            h2 = h2 + _dot(act, wd_ref[rows, :], 1, 0)
        gp = gp_ref[...]
        hn3 = _rms_fwd(h2, gp).astype(BF16)
        hn3_ref[...] = hn3
        gate2 = _sigmoid(_dot(hn3, wl_ref[...], 1, 0))
        p_tile = p_ref[...].astype(BF16)
        pp = _dot(p_tile, wp_ref[...], 1, 0)
        err = h2 + gate2 * pp - tgt_ref[...]
        loss_acc[...] += jnp.sum(err * err, axis=0, keepdims=True)
        dy = err * (1.0 / D_MODEL)
        dwp_acc[...] += _dot(p_tile, (dy * gate2).astype(BF16), 0, 0)
        dgl = (dy * pp * gate2 * (1.0 - gate2)).astype(BF16)
        dgl_ref[...] = dgl
        dx3, dg3 = _rms_bwd(h2, gp, _dot(dgl, wl_ref[...], 1, 1))
        dh2 = dy + dx3
        dgp_ref[...] += dg3
        dh2b = dh2.astype(BF16)
        dh2b_ref[...] = dh2b
        dhn = jnp.zeros((t, D_MODEL), F32)
        for ch in range(N_FF_CHUNKS):
            rows = pl.ds(ch * FF_CHUNK, FF_CHUNK)
            dact = _dot(dh2b, wd_ref[rows, :], 1, 1)
            gate_v = gate_s[ch]
            up_v = up_s[ch]
            sg = _sigmoid(gate_v)
            dup = (dact * (gate_v * sg)).astype(BF16)
            dgate = (dact * up_v * (sg * (1.0 + gate_v * (1.0 - sg)))).astype(BF16)
            dup_ref[ch] = dup
            dgate_ref[ch] = dgate
            dhn = dhn + _dot(dgate, wg_ref[rows, :], 1, 0) + _dot(dup, wu_ref[rows, :], 1, 0)
        dhn_s[...] = dhn
        h1_s[...] = h1v
        dh1_s[i % 3] = dh2

        @pl.when(i > 0)
        def _():
            dh1_copy(i - 1).start()

        @pl.when(i > 1)
        def _():
            dh1_copy(i - 2).wait()

        @pl.when(i == n_tiles - 1)
        def _():
            total = jnp.sum(loss_acc[...], axis=-1, keepdims=True) * (0.5 / D_MODEL)
            loss_ref[...] = jnp.broadcast_to(total, loss_ref.shape)
            dwp_ref[...] = dwp_acc[...].astype(BF16)
            finish_tile(i % 3)
            dh1_copy(i).start()
            dh1_copy(i - 1).wait()
            dh1_copy(i).wait()

    row = lambda w: pl.BlockSpec((t, w), lambda i: (i, 0))
    chunked = pl.BlockSpec((N_FF_CHUNKS, t, FF_CHUNK), lambda i: (0, i, 0))
    vec = _full((1, D_MODEL))
    act_shape = jax.ShapeDtypeStruct((N_FF_CHUNKS, s_len, FF_CHUNK), BF16)
    tok = lambda dtype: jax.ShapeDtypeStruct((s_len, D_MODEL), dtype)
    return pl.pallas_call(
        body, name="ffn_ple", grid=(n_tiles,),
        in_specs=[row(D_MODEL), row(D_MODEL), row(PLE_DIM), row(D_MODEL)] + W_SPECS + [vec, vec],
        out_specs=[_full((1, 128)), chunked, chunked, chunked] + [row(D_MODEL)] * 3 + [_full((PLE_DIM, D_MODEL)), ANY, vec, vec],
        out_shape=[jax.ShapeDtypeStruct((1, 128), F32), act_shape, act_shape, act_shape, tok(BF16), tok(BF16), tok(BF16),
                   jax.ShapeDtypeStruct((PLE_DIM, D_MODEL), BF16), tok(F32), jax.ShapeDtypeStruct((1, D_MODEL), F32),
                   jax.ShapeDtypeStruct((1, D_MODEL), F32)],
        scratch_shapes=[pltpu.VMEM((D_FF, D_MODEL), BF16)] * 3
        + [pltpu.VMEM((D_MODEL, D_MODEL), BF16), pltpu.VMEM((PLE_DIM, D_MODEL), BF16), pltpu.VMEM((PLE_DIM, D_MODEL), BF16),
           pltpu.VMEM((N_FF_CHUNKS, t, FF_CHUNK), F32), pltpu.VMEM((N_FF_CHUNKS, t, FF_CHUNK), F32), pltpu.VMEM((1, D_MODEL), F32),
           pltpu.VMEM((PLE_DIM, D_MODEL), F32), pltpu.SemaphoreType.DMA((N_CHIPS,)),
           pltpu.VMEM((t, D_MODEL), F32), pltpu.VMEM((t, D_MODEL), F32), pltpu.VMEM((3, t, D_MODEL), F32),
           pltpu.SemaphoreType.DMA((3,))],
        compiler_params=_params(VMEM_LIMIT_BIG),
    )(hn2, h1, p2, tgt, *wts, g_ffn, g_ple)


def _accumulate_tn(acc_ref, a, b, first):
    @pl.when(first)
    def _():
        acc_ref[...] = _dot(a, b, 0, 0)

    @pl.when(jnp.logical_not(first))
    def _():
        acc_ref[...] += _dot(a, b, 0, 0)


def _flush_chunks(acc_ref, stage_ref, slab_ref, name, sems):
    stage_ref[...] = acc_ref[...].astype(BF16)
    off, rows = SLAB[name]
    copies = [pltpu.make_async_copy(stage_ref.at[pl.ds(j * rows, rows), :], slab_ref.at[j, pl.ds(off, rows), :], sems.at[j])
              for j in range(N_CHIPS)]
    for cp in copies:
        cp.start()
    for cp in copies:
        cp.wait()


def _mix_out_bwd(dh1, wts, pooled, wpool, pool_scale, mix, after):
    s_len = dh1.shape[0]
    t = 512
    n = t + 16
    n_tiles = s_len // t
    early_rows = GATHER_PARTS[0][1]

    def body(dh1_ref, sl_ref, lo_ref, me_ref, pooled_ref, wp_ref, sc_ref, mix_ref, after_ref, dost_ref, du_ref, dwp_ref,
             dsc_ref, slab_ref, w_ref, ext_ref, st_ref, acc_ref, stage_ref, sems):
        del after_ref
        i = pl.program_id(0)

        @pl.when(i == 0)
        def _():
            _load_rows((sl_ref, lo_ref, me_ref), "out", w_ref, sems)
            ext_ref[...] = jnp.zeros_like(ext_ref)
            st_ref[...] = jnp.zeros_like(st_ref)
            dsc_ref[...] = jnp.zeros_like(dsc_ref)
            dwp_ref[...] = jnp.zeros_like(dwp_ref)
            acc_ref[...] = jnp.zeros_like(acc_ref)

        dh1b = dh1_ref[...].astype(BF16)
        acc_ref[...] += _dot(mix_ref[...], dh1b, 0, 0)
        dmix = _dot(dh1b, w_ref[...], 1, 1)
        lo = lax.broadcasted_iota(jnp.int32, (t, 128), 1) < 64
        for j, entry in enumerate(_pack_heads([dmix[:, 128 * p:128 * p + 128] for p in range(4)], lo)):
            dost_ref[j] = entry.astype(BF16)
        pooled_v = pooled_ref[...]
        counts = _pool_counts(n_tiles - 1 - i, t)
        for g in range(4):
            cols = slice(128 * g, 128 * g + 128)
            dm = dmix[:, ATTN_WIDTH + 128 * g:ATTN_WIDTH + 128 * g + 128]
            ypre = _dot(pooled_v[:, cols], wp_ref[g], 1, 0)
            dsc_ref[:, cols] += jnp.sum(ypre * dm, axis=0, keepdims=True)
            dyp = (dm * sc_ref[:, cols]).astype(BF16)
            dwp_ref[g] += _dot(pooled_v[:, cols], dyp, 0, 0)
            dpooled = _dot(dyp, wp_ref[g], 1, 1)
            du_ref[:, cols] = -dpooled
            ext_ref[pl.ds(0, t), cols] = dpooled / counts[:, cols]
        st_ref[pl.ds(0, n), :] = ext_ref[pl.ds(0, n), :] + ext_ref[pl.ds(1, n), :]
        st_ref[pl.ds(0, n), 128:] = st_ref[pl.ds(0, n), 128:] + st_ref[pl.ds(2, n), 128:]
        st_ref[pl.ds(0, n), 256:] = st_ref[pl.ds(0, n), 256:] + st_ref[pl.ds(4, n), 256:]
        st_ref[pl.ds(0, n), 384:] = st_ref[pl.ds(0, n), 384:] + st_ref[pl.ds(8, n), 384:]
        ext_ref[pl.ds(t, POOL_HALO), :] = ext_ref[pl.ds(0, POOL_HALO), :]
        du_ref[...] += st_ref[pl.ds(0, t), :]

        @pl.when(i == n_tiles - 1)
        def _():
            _flush_chunks(acc_ref, stage_ref, slab_ref, "out", sems)

    rev = lambda w: pl.BlockSpec((t, w), lambda i: (n_tiles - 1 - i, 0))
    return pl.pallas_call(
        body, name="mix_out_bwd", grid=(n_tiles,),
        in_specs=[rev(D_MODEL)] + W_SPECS + [rev(POOL_WIDTH), _full((4, 128, 128)), _full((1, POOL_WIDTH)), rev(D_MODEL), ANY],
        out_specs=[pl.BlockSpec((4, t, 128), lambda i: (0, n_tiles - 1 - i, 0)), rev(POOL_WIDTH),
                   _full((4, 128, 128)), _full((1, POOL_WIDTH)), ANY],
        out_shape=[jax.ShapeDtypeStruct((4, s_len, 128), BF16), jax.ShapeDtypeStruct((s_len, POOL_WIDTH), F32),
                   jax.ShapeDtypeStruct((4, 128, 128), F32), jax.ShapeDtypeStruct((1, POOL_WIDTH), F32),
                   jax.ShapeDtypeStruct((N_CHIPS, early_rows, D_MODEL), BF16)],
        scratch_shapes=[pltpu.VMEM((D_MODEL, D_MODEL), BF16), pltpu.VMEM((t + POOL_HALO, POOL_WIDTH), F32),
                        pltpu.VMEM((t + POOL_HALO, POOL_WIDTH), F32), pltpu.VMEM((D_MODEL, D_MODEL), F32),
                        pltpu.VMEM((D_MODEL, D_MODEL), BF16), pltpu.SemaphoreType.DMA((N_CHIPS,))],
        compiler_params=_params(),
    )(dh1, *wts, pooled, wpool, pool_scale, mix, after)


def _attn_bwd(qst, kn, vb, dost, bias_st, sinks, after):
    s_len = kn.shape[0]

    def body(q_ref, kp_ref, kc_ref, vp_ref, vc_ref, do_ref, bias_ref, sink_ref, after_ref, dq_ref, dk_ref, dv_ref, dbias_ref,
             dsink_ref, s_ref, dp_ref, p_ref, dl_ref):
        del after_ref
        i = pl.program_id(0)

        @pl.when(i == 0)
        def _():
            dk_ref[...] = jnp.zeros_like(dk_ref)
            dv_ref[...] = jnp.zeros_like(dv_ref)
            dbias_ref[...] = jnp.zeros_like(dbias_ref)
            dsink_ref[...] = jnp.zeros_like(dsink_ref)

        for b, (rows, k2, v2, bias) in enumerate(_step_blocks(i, kp_ref, kc_ref, vp_ref, vc_ref, bias_ref)):
            s_b, dp_b, p_b, dl_b = s_ref.at[b], dp_ref.at[b], p_ref.at[b], dl_ref.at[b]
            q = _expand_heads(q_ref[:, rows, :])
            do = _expand_heads(do_ref[:, rows, :])
            s_b[...] = _dot(q, k2, 1, 1)
            dp_b[...] = _dot(do, v2, 1, 1)

            def head(h, carry):
                head_rows, probs, p_sink = _head_softmax(s_b, bias, sink_ref, h)
                dp = dp_b[head_rows, :]
                dsum = jnp.sum(probs * dp, axis=-1, keepdims=True)
                dlog = probs * (dp - dsum)
                dsink_ref[head_rows, :] -= p_sink * dsum
                dbias_ref[head_rows, :] += dlog
                p_b[head_rows, :] = probs.astype(BF16)
                dl_b[head_rows, :] = (dlog * (HEAD_DIM ** -0.5)).astype(BF16)
                return carry

            lax.fori_loop(0, N_Q_HEADS, head, 0, unroll=True)
            dlog_s = dl_b[...]
            dq_ref[:, rows, :] = _fold_heads(_dot(dlog_s, k2, 1, 0))
            dk2 = _dot(dlog_s, q, 0, 0)
            dv2 = _dot(p_b[...], do, 0, 0)
            block = ATTN_STEP_BLOCKS * i + b
            prev_rows = pl.ds(pl.multiple_of(jnp.maximum(block - 1, 0) * BLOCK, BLOCK), BLOCK)
            cur_rows = pl.ds(pl.multiple_of(block * BLOCK, BLOCK), BLOCK)
            dk_ref[prev_rows, :] += dk2[:BLOCK]
            dk_ref[cur_rows, :] += dk2[BLOCK:]
            dv_ref[prev_rows, :] += dv2[:BLOCK]
            dv_ref[cur_rows, :] += dv2[BLOCK:]

    stacked, kv, consts = _attn_specs()
    per_step = (ATTN_STEP_BLOCKS,) + BAND
    return pl.pallas_call(
        body, name="attn_bwd", grid=(s_len // (ATTN_STEP_BLOCKS * BLOCK),),
        in_specs=[stacked] + kv + kv + [stacked] + consts + [ANY],
        out_specs=[stacked, _full((s_len, 128)), _full((s_len, 128)), _full(BAND), _full((N_Q_HEADS * BLOCK, 1))],
        out_shape=[jax.ShapeDtypeStruct((4, s_len, 128), F32), jax.ShapeDtypeStruct((s_len, 128), F32),
                   jax.ShapeDtypeStruct((s_len, 128), F32), jax.ShapeDtypeStruct(BAND, F32),
                   jax.ShapeDtypeStruct((N_Q_HEADS * BLOCK, 1), F32)],
        scratch_shapes=[pltpu.VMEM(per_step, F32), pltpu.VMEM(per_step, F32), pltpu.VMEM(per_step, BF16),
                        pltpu.VMEM(per_step, BF16)],
        compiler_params=_params(),
    )(qst, kn, kn, vb, vb, dost, bias_st, sinks, after)


def _flip_rows(x):
    n = x.shape[0]
    exchange = (lax.broadcasted_iota(jnp.int32, (n, n), 0) + lax.broadcasted_iota(jnp.int32, (n, n), 1) == n - 1)
    exchange = jnp.where(exchange, 1.0, 0.0).astype(BF16)
    flipped, rest = None, x
    for _ in range(3):
        term = rest.astype(BF16)
        rest = rest - term.astype(F32)
        part = _dot(exchange, term, 1, 0)
        flipped = part if flipped is None else flipped + part
    return flipped


def _small_pack(dg_attn, dg_ffn, dg_ple, dscale, dgq, dgk, dbias, dsink_rows, loss_v):
    def body(ga_ref, gf_ref, gp_ref, sc_ref, gq_ref, gk_ref, db_ref, ds_ref, bucket_ref, loss_ref, out_ref):
        out_ref[...] = jnp.zeros((SMALL_ROWS, 128), F32)
        for name, ref, n in (("g_attn", ga_ref, 8), ("g_ffn", gf_ref, 8), ("g_ple", gp_ref, 8), ("pool_scale", sc_ref, 4)):
            for k in range(n):
                out_ref[pl.ds(SMALL[name] + k, 1), :] = ref[:, 128 * k:128 * k + 128]
        for name, ref in (("g_q", gq_ref), ("g_k", gk_ref)):
            both = ref[...]
            out_ref[pl.ds(SMALL[name], 1), :] = both + pltpu.roll(both, 64, axis=1)
        out_ref[pl.ds(SMALL["loss"], 1), :] = loss_ref[...]
        by_diagonal = lambda flipped: pltpu.roll(flipped, 0, 1, stride=1, stride_axis=0)
        bucket_of = jnp.max(by_diagonal(bucket_ref[...]), axis=0, keepdims=True)
        sums = jnp.concatenate([jnp.sum(by_diagonal(_flip_rows(db_ref[pl.ds(h * BLOCK, BLOCK), :])), axis=0, keepdims=True)
                                for h in range(N_Q_HEADS)], axis=0)
        lanes = lax.broadcasted_iota(jnp.int32, (N_Q_HEADS, 128), 1)
        lane1 = lax.broadcasted_iota(jnp.int32, (1, 128), 1)
        rb = jnp.zeros((N_Q_HEADS, 128), F32)
        for b in range(N_BUCKETS):
            rb = jnp.where(lanes == b, jnp.sum(jnp.where(bucket_of == float(b), sums, 0.0), axis=1, keepdims=True), rb)
        sk = jnp.zeros((1, 128), F32)
        for h in range(N_Q_HEADS):
            sk = jnp.where(lane1 == h, jnp.sum(ds_ref[pl.ds(h * BLOCK, BLOCK), :]), sk)
        out_ref[pl.ds(SMALL["rel_bias"], N_Q_HEADS), :] = rb
        out_ref[pl.ds(SMALL["sinks"], 1), :] = sk

    bucket = jnp.asarray(_bucket_table()[::-1].astype(np.float32))
    return pl.pallas_call(
        body, name="small_pack", in_specs=[VMEM_WHOLE] * 10, out_specs=VMEM_WHOLE,
        out_shape=jax.ShapeDtypeStruct((SMALL_ROWS, 128), F32),
    )(dg_attn, dg_ffn, dg_ple, dscale, dgq, dgk, dbias, dsink_rows, bucket, loss_v)


def _attn_in_bwd(dqst, zqk, dk, dv, du, x2, dh1, hn1, slab, wts, g_attn, gq, gk):
    s_len = x2.shape[0]
    t = 512
    n_tiles = s_len // t

    def body(dq_ref, zqk_ref, dk_ref, dv_ref, du_ref, x_ref, dh1_ref, hn_ref, slab_in_ref, sl_ref, lo_ref, me_ref, g_ref,
             gq_ref, gk_ref, dx_ref, dg_ref, dgq_ref, dgk_ref, slab_ref, w_ref, dz_ref, acc_ref, stage_ref, sems):
        del slab_in_ref
        i = pl.program_id(0)

        @pl.when(i == 0)
        def _():
            _load_rows((sl_ref, lo_ref, me_ref), "inT", w_ref, sems)
            dg_ref[...] = jnp.zeros_like(dg_ref)
            dgq_ref[...] = jnp.zeros_like(dgq_ref)
            dgk_ref[...] = jnp.zeros_like(dgk_ref)
            acc_ref[...] = jnp.zeros_like(acc_ref)

        lo = lax.broadcasted_iota(jnp.int32, (t, 128), 1) < 64
        for p, dqn in enumerate(_unpack_heads([dq_ref[j] for j in range(4)], lo)):
            dq_raw, dgq = _pair_norm_bwd(zqk_ref[:, 128 * p:128 * p + 128], gq_ref[...], dqn)
            dz_ref[:, 128 * p:128 * p + 128] = dq_raw.astype(BF16)
            dgq_ref[...] += dgq
        dk_raw, dgk = _pair_norm_bwd(zqk_ref[:, 512:640], gk_ref[...], dk_ref[...])
        dgk_ref[...] += dgk
        dz_ref[:, 512:640] = dk_raw.astype(BF16)
        dz_ref[:, 640:768] = dv_ref[...].astype(BF16)
        dz_ref[:, 768:] = du_ref[...].astype(BF16)
        dz = dz_ref[...]
        acc_ref[...] += _dot(dz, hn_ref[...], 0, 0)
        dx, dg = _rms_bwd(x_ref[...], g_ref[...], _dot(dz, w_ref[...], 1, 0))
        dx_ref[...] = dh1_ref[...] + dx
        dg_ref[...] += dg

        @pl.when(i == n_tiles - 1)
        def _():
            _flush_chunks(acc_ref, stage_ref, slab_ref, "inT", sems)

    row = lambda w: pl.BlockSpec((t, w), lambda i: (i, 0))
    return pl.pallas_call(
        body, name="attn_in_bwd", grid=(n_tiles,),
        in_specs=[pl.BlockSpec((4, t, 128), lambda i: (0, i, 0)), row(640), row(128), row(128), row(POOL_WIDTH),
                  row(D_MODEL), row(D_MODEL), row(D_MODEL), ANY] + W_SPECS + [_full((1, D_MODEL)), _full((1, 128)),
                                                                              _full((1, 128))],
        out_specs=[row(D_MODEL), _full((1, D_MODEL)), _full((1, 128)), _full((1, 128)), ANY],
        out_shape=[jax.ShapeDtypeStruct((s_len, D_MODEL), F32), jax.ShapeDtypeStruct((1, D_MODEL), F32),
                   jax.ShapeDtypeStruct((1, 128), F32), jax.ShapeDtypeStruct((1, 128), F32),
                   jax.ShapeDtypeStruct(slab.shape, BF16)],
        input_output_aliases={8: 4},
        scratch_shapes=[pltpu.VMEM((IN_WIDTH, D_MODEL), BF16), pltpu.VMEM((t, IN_WIDTH), BF16),
                        pltpu.VMEM((IN_WIDTH, D_MODEL), F32), pltpu.VMEM((IN_WIDTH, D_MODEL), BF16),
                        pltpu.SemaphoreType.DMA((N_CHIPS,))],
        compiler_params=_params(),
    )(dqst, zqk, dk, dv, du, x2, dh1, hn1, slab, *wts, g_attn, gq, gk)


def _dw(lefts, b, name, slab, slab_rows, row_offs):
    a0, n_a = lefts[0], len(lefts)
    assert b.shape[1] == D_MODEL
    if a0.ndim == 3:
        n_chunks, s_len, tm = a0.shape
        m = n_chunks * tm
    else:
        s_len, tm = a0.shape
        m = tm
    tk = 2048 if n_a * tm <= 1408 else 1024
    if a0.ndim == 3:
        a_spec = pl.BlockSpec((None, tk, tm), lambda i, k: (i, k, 0))
    else:
        a_spec = pl.BlockSpec((tk, tm), lambda i, k: (k, i))
    n_steps, n_tiles = s_len // tk, m // tm
    chunk = m // N_CHIPS
    per_tile = tm // chunk

    def body(*refs):
        a_refs, b_ref = refs[:n_a], refs[n_a]
        o_ref, acc_ref, stage_ref, sems = refs[-4:]
        i, k = pl.program_id(0), pl.program_id(1)
        b_tile = b_ref[...].astype(BF16)
        for w, a_ref in enumerate(a_refs):
            _accumulate_tn(acc_ref.at[w], a_ref[...].astype(BF16), b_tile, k == 0)

        def out_copies(tile, slot):
            return [pltpu.make_async_copy(stage_ref.at[slot, w, pl.ds(jj * chunk, chunk), :],
                                          o_ref.at[tile * per_tile + jj, pl.ds(row_offs[w], chunk), :], sems.at[slot, w, jj])
                    for w in range(n_a) for jj in range(per_tile)]

        @pl.when(k == n_steps - 1)
        def _():
            slot = i % 2

            @pl.when(i >= 2)
            def _():
                for cp in out_copies(i - 2, slot):
                    cp.wait()

            stage_ref[slot] = acc_ref[...].astype(BF16)
            for cp in out_copies(i, slot):
                cp.start()

            @pl.when(i == n_tiles - 1)
            def _():
                for cp in out_copies(i, slot):
                    cp.wait()
                if n_tiles > 1:
                    for cp in out_copies(i - 1, 1 - slot):
                        cp.wait()

    in_specs = [a_spec] * n_a + [pl.BlockSpec((tk, D_MODEL), lambda i, k: (k, 0))]
    operands, aliases = [*lefts, b], {}
    if slab is not None:
        in_specs.append(ANY)
        operands.append(slab)
        aliases = {n_a + 1: 0}
    return pl.pallas_call(
        body, name=name, grid=(n_tiles, n_steps), in_specs=in_specs, out_specs=ANY,
        out_shape=jax.ShapeDtypeStruct((N_CHIPS, slab_rows, D_MODEL), BF16), input_output_aliases=aliases,
        scratch_shapes=[pltpu.VMEM((n_a, tm, D_MODEL), F32), pltpu.VMEM((2, n_a, tm, D_MODEL), BF16),
                        pltpu.SemaphoreType.DMA((2, n_a, per_tile))],
        compiler_params=_params(VMEM_LIMIT_BIG, n_axes=2),
    )(*operands)


def _position():
    x, y, c = lax.axis_index("x"), lax.axis_index("y"), lax.axis_index("c")
    other_chips = [(1 - x, y), (x, 1 - y), (1 - x, 1 - y)]
    return x, y, c, other_chips


def _ag_weights(local_slab, row0, n_rows, name, collective_id):
    half = n_rows // 2
    quarter = half // 2
    assert quarter % 16 == 0

    def body(l_ref, g_ref, send, recv):
        x, y, c, chips = _position()
        me, (via_x, via_y, diagonal) = 2 * x + y, [2 * chip[0] + chip[1] for chip in chips]
        here, sibling, x_nbr, y_nbr = (x, y, c), (x, y, 1 - c), (1 - x, y, c), (x, 1 - y, c)
        peers = [sibling, x_nbr, y_nbr]
        barrier = pltpu.get_barrier_semaphore()
        for peer in peers:
            pl.semaphore_signal(barrier, inc=1, device_id=peer, device_id_type=MESH)
        pl.semaphore_wait(barrier, len(peers))

        def rows(core, part):
            start, size = (core * half, half) if part is None else (core * half + part * quarter, quarter)
            return pl.ds(pl.multiple_of(start, 16), size)

        def copy(k, chip_idx, where, to, src=None):
            dst = g_ref.at[chip_idx, where, :]
            return pltpu.make_async_remote_copy(src_ref=dst if src is None else src, dst_ref=dst, send_sem=send.at[k],
                                                recv_sem=recv.at[k], device_id=to, device_id_type=MESH)

        own_rows = l_ref.at[pl.ds(pl.multiple_of(row0 + c * half, 16), half), :]
        started = [copy(0, me, rows(c, None), x_nbr, src=own_rows), copy(1, me, rows(c, None), y_nbr, src=own_rows)]
        for cp in started:
            cp.start()
        after_arrival = [
            (copy(0, via_x, rows(c, None), here), [copy(4, via_x, rows(c, None), sibling), copy(3, via_x, rows(c, 1), y_nbr)]),
            (copy(1, via_y, rows(c, None), here), [copy(5, via_y, rows(c, None), sibling), copy(2, via_y, rows(c, 0), x_nbr)]),
            (copy(2, diagonal, rows(c, 0), here), [copy(6, diagonal, rows(c, 0), sibling)]),
            (copy(3, diagonal, rows(c, 1), here), [copy(7, diagonal, rows(c, 1), sibling)]),
        ]
        for arrival, onward in after_arrival:
            arrival.wait_recv()
            for cp in onward:
                cp.start()
            started += onward
        for cp in (copy(4, via_x, rows(1 - c, None), here), copy(5, via_y, rows(1 - c, None), here),
                   copy(6, diagonal, rows(1 - c, 0), here), copy(7, diagonal, rows(1 - c, 1), here)):
            cp.wait_recv()
        for cp in started:
            cp.wait_send()

    return pl.kernel(
        body, out_type=jax.ShapeDtypeStruct((N_CHIPS, n_rows, D_MODEL), BF16),
        mesh=plsc.ScalarSubcoreMesh(axis_name="sequencer", num_cores=1), name=name,
        scratch_types=[pltpu.SemaphoreType.DMA((8,)), pltpu.SemaphoreType.DMA((8,))],
        compiler_params=pltpu.CompilerParams(collective_id=collective_id),
    )(local_slab)


def _comm_call(body, peers_of, out_shape, n_sems, operand, name, collective_id):
    sems = [pltpu.SemaphoreType.DMA((n_sems,)), pltpu.SemaphoreType.DMA((n_sems,))]

    def with_handshake(in_ref, out_ref, send, recv):
        x, y, c, _ = _position()
        peers = peers_of(x, y, c)
        barrier = pltpu.get_barrier_semaphore()
        for peer in peers:
            pl.semaphore_signal(barrier, inc=1, device_id=peer, device_id_type=MESH)
        pl.semaphore_wait(barrier, len(peers))
        body(in_ref, out_ref, send, recv)

    return pl.kernel(with_handshake, out_type=out_shape, mesh=plsc.ScalarSubcoreMesh(axis_name="sequencer", num_cores=1),
                     name=name, scratch_types=sems, compiler_params=pltpu.CompilerParams(collective_id=collective_id))(operand)


def _rs_swap_halves(partial, name, collective_id):
    half = partial.shape[1] // 2

    def body(p_ref, r_ref, send, recv):
        x, y, c, _ = _position()
        theirs = pl.ds(pl.multiple_of((1 - c) * half, 16), half)
        cp = pltpu.make_async_remote_copy(src_ref=p_ref.at[:, theirs, :], dst_ref=r_ref, send_sem=send.at[0],
                                          recv_sem=recv.at[0], device_id=(x, y, 1 - c), device_id_type=MESH)
        cp.start()
        cp.wait()

    return _comm_call(body, lambda x, y, c: [(x, y, 1 - c)], jax.ShapeDtypeStruct((N_CHIPS, half, D_MODEL), BF16), 1,
                      partial, name, collective_id)


def _gather_chip_sums(s_ref, sib_ref, sum_ref, o_ref, send, recv):
    x, y, c, chips = _position()
    me, sibling, here = 2 * x + y, (x, y, 1 - c), (x, y, c)
    half = s_ref.shape[0] // 2

    def rows(core):
        return pl.ds(pl.multiple_of(core * half, 8), half)

    def copy(k, src, dst, to):
        return pltpu.make_async_remote_copy(src_ref=src, dst_ref=dst, send_sem=send.at[k], recv_sem=recv.at[k], device_id=to,
                                            device_id_type=MESH)

    swap = copy(0, s_ref, sib_ref, sibling)
    keep = pltpu.make_async_copy(sum_ref, o_ref.at[me], send.at[7])
    sends = [copy(1 + k, sum_ref.at[rows(c), :], o_ref.at[me, rows(c), :], (*chip, c)) for k, chip in enumerate(chips)]

    def landed(chip, core):
        return o_ref.at[2 * chip[0] + chip[1], rows(core), :]

    def add_and_send():
        swap.wait_recv()
        sum_ref[...] = s_ref[...] + sib_ref[...]
        keep.start()
        for cp in sends:
            cp.start()

    def finish():
        passed = []
        for k, chip in enumerate(chips):
            copy(1 + k, landed(chip, c), landed(chip, c), here).wait_recv()
            fwd = copy(4 + k, landed(chip, c), landed(chip, c), sibling)
            fwd.start()
            passed.append(fwd)
        for k, chip in enumerate(chips):
            copy(4 + k, landed(chip, 1 - c), landed(chip, 1 - c), here).wait_recv()
        for cp in [swap] + sends + passed:
            cp.wait_send()
        keep.wait()

    return swap.start, add_and_send, finish


def _rs_add_halves(partial, other, core, name, after, small=None):
    half = other.shape[1]
    t = half // 2
    steps = half // t

    def body(core_ref, a_ref, b_ref, after_ref, *rest):
        del after_ref
        o_ref = rest[0] if small is None else rest[1]
        if small is not None:
            small_ref, _, t_ref, sib_ref, sum_ref, t_send, t_recv = rest
            swap, add_and_send, finish_tables = _gather_chip_sums(small_ref, sib_ref, sum_ref, t_ref, t_send, t_recv)
            step = pl.program_id(0) * steps + pl.program_id(1)
            pl.when(step == 0)(swap)
            pl.when(step == 1)(add_and_send)
        o_ref[...] = (a_ref[...].astype(F32) + b_ref[...].astype(F32)).astype(BF16)
        if small is not None:
            pl.when(step == N_CHIPS * steps - 1)(finish_tables)

    t_in, t_out, t_scratch = [], [], []
    if small is not None:
        t_in, t_out = [VMEM_WHOLE], [jax.ShapeDtypeStruct((N_CHIPS, *small.shape), F32)]
        t_scratch = [pltpu.VMEM(small.shape, F32)] * 2 + [pltpu.SemaphoreType.DMA((8,))] * 2
    res = pl.pallas_call(
        body, name=name,
        grid_spec=pltpu.PrefetchScalarGridSpec(
            num_scalar_prefetch=1, grid=(N_CHIPS, steps),
            in_specs=[pl.BlockSpec((1, t, D_MODEL), lambda j, i, core_ref: (j, core_ref[0] * steps + i, 0)),
                      pl.BlockSpec((1, t, D_MODEL), lambda j, i, core_ref: (j, i, 0)), ANY] + t_in,
            out_specs=[pl.BlockSpec((1, t, D_MODEL), lambda j, i, core_ref: (j, i, 0))] + [ANY] * len(t_out),
            scratch_shapes=t_scratch),
        out_shape=[jax.ShapeDtypeStruct((N_CHIPS, half, D_MODEL), BF16)] + t_out,
        compiler_params=_params(n_axes=2),
    )(core, partial, other, after, *([] if small is None else [small]))
    return res[0] if small is None else res


def _rs_exchange_chips(pre, name, collective_id):
    def body(s_ref, r_ref, send, recv):
        x, y, c, chips = _position()

        def copy(k, chunk, to):
            return pltpu.make_async_remote_copy(src_ref=s_ref.at[chunk], dst_ref=r_ref.at[k], send_sem=send.at[k],
                                                recv_sem=recv.at[k], device_id=to, device_id_type=MESH)

        sends = [copy(k, 2 * chip[0] + chip[1], (*chip, c)) for k, chip in enumerate(chips)]
        for cp in sends:
            cp.start()
        for cp in sends:
            cp.wait()

    return _comm_call(body, lambda x, y, c: [(1 - x, y, c), (x, 1 - y, c), (1 - x, 1 - y, c)],
                      jax.ShapeDtypeStruct((3, pre.shape[1], D_MODEL), BF16), 3, pre, name, collective_id)


def _gather_small(s_ref, t_ref, send, recv):
    x, y, c, chips = _position()
    sibling = (x, y, 1 - c)

    def slot(px, py, pc):
        return t_ref.at[4 * px + 2 * py + pc]

    def copy(k, block, to, src=None):
        return pltpu.make_async_remote_copy(src_ref=slot(*block) if src is None else src, dst_ref=slot(*block),
                                            send_sem=send.at[k], recv_sem=recv.at[k], device_id=to, device_id_type=MESH)

    own = pltpu.make_async_copy(s_ref, slot(x, y, c), send.at[7])
    first = [copy(0, (x, y, c), sibling, src=s_ref)]
    first += [copy(1 + k, (x, y, c), (*chip, c), src=s_ref) for k, chip in enumerate(chips)]

    def start():
        own.start()
        for cp in first:
            cp.start()

    def finish():
        passed = []
        for k, chip in enumerate(chips):
            copy(1 + k, (*chip, c), (x, y, c)).wait_recv()
            fwd = copy(4 + k, (*chip, c), sibling)
            fwd.start()
            passed.append(fwd)
        copy(0, sibling, (x, y, c)).wait_recv()
        for k, chip in enumerate(chips):
            copy(4 + k, (*chip, 1 - c), (x, y, c)).wait_recv()
        for cp in first + passed:
            cp.wait_send()
        own.wait()

    return start, finish


def _table_gather_parts(small):
    if small is None:
        return [], [], []
    return [VMEM_WHOLE], [jax.ShapeDtypeStruct((N_DEV, *small.shape), F32)], [pltpu.SemaphoreType.DMA((8,))] * 2


def _rs_sum_chips(pre, received, place, name, after, small=None):
    half = pre.shape[1]
    steps = 4 if half > 512 else 2
    t = half // steps
    assert t % 16 == 0 and t * steps == half

    def body(place_ref, own_ref, r_ref, after_ref, *rest):
        del place_ref, after_ref
        if small is None:
            o_ref, stage, kept_sems, send, recv = rest
        else:
            small_ref, o_ref, t_ref, stage, kept_sems, send, recv, t_send, t_recv = rest
            start_tables, finish_tables = _gather_small(small_ref, t_ref, t_send, t_recv)
            pl.when(pl.program_id(0) == 0)(start_tables)
        i = pl.program_id(0)
        x, y, c, _ = _position()

        def rows(core, step):
            return o_ref.at[pl.ds(pl.multiple_of((core * steps + step) * t, 8), t), :]

        def kept(step):
            return pltpu.make_async_copy(stage.at[step], rows(c, step), kept_sems.at[step])

        def sent(core, step):
            return pltpu.make_async_remote_copy(src_ref=stage.at[step], dst_ref=rows(core, step), send_sem=send.at[step],
                                                recv_sem=recv.at[step], device_id=(x, y, 1 - core), device_id_type=MESH)

        acc = own_ref[0].astype(F32)
        for k in range(3):
            acc = acc + r_ref[k].astype(F32)
        stage[i] = acc
        kept(i).start()
        sent(c, i).start()

        @pl.when(i == steps - 1)
        def _():
            if small is not None:
                finish_tables()
            for step in range(steps):
                kept(step).wait()
                sent(c, step).wait_send()
                sent(1 - c, step).wait_recv()

    t_in, t_out, t_scratch = _table_gather_parts(small)
    res = pl.pallas_call(
        body, name=name,
        grid_spec=pltpu.PrefetchScalarGridSpec(
            num_scalar_prefetch=1, grid=(steps,),
            in_specs=[pl.BlockSpec((1, t, D_MODEL), lambda i, place_ref: (place_ref[0], i, 0)),
                      pl.BlockSpec((3, t, D_MODEL), lambda i, place_ref: (0, i, 0)), ANY] + t_in,
            out_specs=[ANY] * (1 + len(t_out)),
            scratch_shapes=[pltpu.VMEM((steps, t, D_MODEL), F32)] + [pltpu.SemaphoreType.DMA((steps,))] * 3 + t_scratch),
        out_shape=[jax.ShapeDtypeStruct((2 * half, D_MODEL), F32)] + t_out, compiler_params=_params(),
    )(place, pre, received, after, *([] if small is None else [small]))
    return res[0] if small is None else res


def _adam_update(w, g, m, v):
    m_new = ADAM_B1 * m + (1.0 - ADAM_B1) * g
    v_new = ADAM_B2 * v + (1.0 - ADAM_B2) * (g * g)
    m_hat = m_new / (1.0 - ADAM_B1 ** ADAM_STEP)
    v_hat = v_new / (1.0 - ADAM_B2 ** ADAM_STEP)
    return -ADAM_LR * (m_hat / (jnp.sqrt(v_hat) + ADAM_EPS) + ADAM_WD * w), m_new, v_new


def _adamw(w, g_rows, row_off, m, v, name):
    rows, cols = w.shape
    t = rows if rows <= 320 else (rows // 2 if rows % 256 else 256)

    def body(w_ref, g_ref, m_ref, v_ref, go_ref, d_ref, nm_ref, nv_ref):
        g = g_ref[...]
        go_ref[...] = g
        d_ref[...], nm_ref[...], nv_ref[...] = _adam_update(w_ref[...], g, m_ref[...], v_ref[...])

    blk = pl.BlockSpec((t, cols), lambda i: (i, 0))
    assert row_off % 8 == 0 and t % 8 == 0
    g_blk = pl.BlockSpec((pl.Element(t), pl.Element(cols)), lambda i: (pl.multiple_of(row_off + i * t, 8), 0))
    shape = jax.ShapeDtypeStruct((rows, cols), F32)
    return pl.pallas_call(
        body, name=name, grid=(rows // t,), in_specs=[blk, g_blk, blk, blk], out_specs=[blk] * 4, out_shape=[shape] * 4,
        compiler_params=_params(),
    )(w, g_rows, m, v)


SMALL_PARAMS = [("g_attn", (1, D_MODEL), 8), ("g_q", (1, HEAD_DIM), None), ("g_k", (1, HEAD_DIM), None),
                ("sinks", (1, N_Q_HEADS), None), ("rel_bias", (N_Q_HEADS, N_BUCKETS), None), ("w_pool", (512, 128), None),
                ("pool_scale", (1, POOL_WIDTH), 4), ("g_ffn", (1, D_MODEL), 8), ("g_ple", (1, D_MODEL), 8)]


def _adamw_small(tables, pool_tables, wmv):
    n_par = len(SMALL_PARAMS)

    def body(*refs):
        t_ref, p_ref = refs[:2]
        ins = refs[2:2 + 3 * n_par]
        loss_ref = refs[2 + 3 * n_par]
        outs = refs[3 + 3 * n_par:-1]
        tot_ref = refs[-1]

        def in_order(ref):
            total = ref[0]
            for d in range(1, ref.shape[0]):
                total = total + ref[d]
            return total

        tot_ref[...] = in_order(t_ref)
        loss_ref[...] = tot_ref[pl.ds(SMALL["loss"], 1), 0:1]
        for i, (name, shape, split) in enumerate(SMALL_PARAMS):
            g_ref, d_ref, nm_ref, nv_ref = outs[4 * i:4 * i + 4]
            row = SMALL.get(name)
            if name == "w_pool":
                g_ref[...] = in_order(p_ref)
            elif split:
                for k in range(split):
                    g_ref[:, 128 * k:128 * k + 128] = tot_ref[pl.ds(row + k, 1), :]
            else:
                g_ref[...] = tot_ref[pl.ds(row, shape[0]), 0:shape[1]]
            w_ref, m_ref, v_ref = ins[3 * i:3 * i + 3]
            d_ref[...], nm_ref[...], nv_ref[...] = _adam_update(w_ref[...], g_ref[...], m_ref[...], v_ref[...])

    shapes = [jax.ShapeDtypeStruct((1, 1), F32)]
    for _, shape, _ in SMALL_PARAMS:
        shapes += [jax.ShapeDtypeStruct(shape, F32)] * 4
    flat = [a for triple in wmv for a in triple]
    res = pl.pallas_call(
        body, name="adamw_small", in_specs=[VMEM_WHOLE] * (2 + 3 * n_par), out_specs=[VMEM_WHOLE] * len(shapes),
        out_shape=shapes, scratch_shapes=[pltpu.VMEM((SMALL_ROWS, 128), F32)],
    )(tables, pool_tables, *flat)
    return res[0], [res[1 + 4 * i:5 + 4 * i] for i in range(n_par)]


def _pack_ple_proj(shard):
    return shard.reshape(4, 64, 256).transpose(1, 0, 2).reshape(64, D_MODEL)


class _Reduction:
    def __init__(self, tag, place, ids=(None, None)):
        self.tag, self.place, self.ids = tag, place, ids

    def start(self, partial):
        self.partial = partial
        self.other = _rs_swap_halves(partial, "rs_swap_" + self.tag, self.ids[0])
        return partial

    def middle(self, after, small=None):
        res = _rs_add_halves(self.partial, self.other, self.place[1:], "rs_add_" + self.tag, after, small)
        self.pre, self.tables = (res, None) if small is None else res
        self.received = _rs_exchange_chips(self.pre, "rs_exchange_" + self.tag, self.ids[1])
        return self.pre

    def finish(self, after, small=None):
        return _rs_sum_chips(self.pre, self.received, self.place, "rs_sum_" + self.tag, after, small)


def _local_grads(x2, p2, tgt, wts, g_attn_norm, g_q, g_k, attn_sinks, rel_bias, w_pool, pool_scale, g_ffn_norm, g_ple_norm,
                 reduce_a):
    w_early, w_late = wts
    w_in = w_out = w_early
    bucket = jnp.asarray(_bucket_table())
    gq = jnp.tile(g_q, (1, 2))
    gk = jnp.tile(g_k, (1, 2))
    wpool = w_pool[0].astype(BF16)
    sinks = attn_sinks[0]
    bias_st = _bias_build(rel_bias.T, bucket)

    hn1 = _first_norm(x2, g_attn_norm)
    zqk, u, kn, vb, qst = _attn_in(hn1, gq, gk, w_in)
    ost = _attn_fwd(qst, kn, vb, bias_st, sinks)
    pooled, mix, h1, hn2 = _mix_out(u, ost, x2, w_out, wpool, pool_scale, g_ffn_norm)
    loss_v, dgate, dup, act, dh2, hn3, dgl, dw_plp, dh1, dg_ffn, dg_ple = _ffn_ple(hn2, h1, p2, tgt, w_late, g_ffn_norm,
                                                                                      g_ple_norm)

    late0, late_rows = GATHER_PARTS[1][0], SLAB_ROWS - GATHER_PARTS[1][0]
    partial_a = None
    for names, lefts, right in ((("gateT", "upT"), [dgate, dup], hn2), (("down",), [act], dh2), (("plg",), [hn3], dgl)):
        partial_a = _dw(lefts, right, "dw_" + names[0], partial_a, late_rows, [SLAB[name][0] - late0 for name in names])
    dw_plp = dw_plp.reshape(4, 64, N_CHIPS, 256).transpose(2, 1, 0, 3).reshape(N_CHIPS, 64, D_MODEL)
    partial_a = reduce_a.start(lax.dynamic_update_slice(partial_a, dw_plp, (0, SLAB["plp"][0] - late0, 0)))
    dost, du, dw_pool, dscale, partial_b = _mix_out_bwd(dh1, w_out, pooled, wpool, pool_scale, mix, partial_a)
    pre_a = reduce_a.middle(du, dw_pool.reshape(512, 128))
    dqst, dk, dv, dbias, dsink_rows = _attn_bwd(qst, kn, vb, dost, bias_st, sinks, pre_a)
    dx, dg_attn, dgq, dgk, partial_b = _attn_in_bwd(dqst, zqk, dk, dv, du, x2, dh1, hn1, partial_b, w_in, g_attn_norm, gq, gk)

    small = _small_pack(dg_attn, dg_ffn, dg_ple, dscale, dgq, dgk, dbias, dsink_rows, loss_v)
    return dx, partial_b, small


def kernel(x, p, w_in, w_out, g_attn_norm, g_q, g_k, attn_sinks, rel_bias, w_pool, pool_scale, g_ffn_norm, w_gate, w_up, w_down, g_ple_norm, w_ple_gate, w_ple_proj, loss_target, m_w_in, m_w_out, m_g_attn_norm, m_g_q, m_g_k, m_attn_sinks, m_rel_bias, m_w_pool, m_pool_scale, m_g_ffn_norm, m_w_gate, m_w_up, m_w_down, m_g_ple_norm, m_w_ple_gate, m_w_ple_proj, v_w_in, v_w_out, v_g_attn_norm, v_g_q, v_g_k, v_attn_sinks, v_rel_bias, v_w_pool, v_pool_scale, v_g_ffn_norm, v_w_gate, v_w_up, v_w_down, v_g_ple_norm, v_w_ple_gate, v_w_ple_proj):
    core = lax.axis_index("c").astype(jnp.int32).reshape(1)
    me = (2 * lax.axis_index("x") + lax.axis_index("y")).astype(jnp.int32).reshape(1)

    local_parts = [jnp.concatenate(pieces, axis=0).astype(BF16) for pieces in (
        [w_in[0].T, w_out[0]], [w_gate[0].T, w_up[0].T, w_down[0], w_ple_gate[0], _pack_ple_proj(w_ple_proj[0])])]
    wts = [(_ag_weights(local, 0, local.shape[0], name, collective_id), local, me)
           for local, name, collective_id in zip(local_parts, ("ag_early", "ag_late"), (1, 2))]

    place = jnp.concatenate([me, core])
    reduce_a = _Reduction("a", place, ids=(3, 4))
    dx, partial_b, small = _local_grads(x[0], p[0, 0], loss_target[0], wts, g_attn_norm, g_q, g_k, attn_sinks, rel_bias,
                                        w_pool, pool_scale, g_ffn_norm, g_ple_norm, reduce_a)
    reduce_b = _Reduction("b", place, ids=(6, 7))
    reduce_b.start(partial_b)
    grads_a, small_all = reduce_a.finish(partial_b, small)
    reduce_b.middle(grads_a)

    late0 = GATHER_PARTS[1][0]

    def rows(name):
        return grads_a, SLAB[name][0] - late0

    plp_rows = grads_a[SLAB["plp"][0] - late0:]
    big = {
        "w_gate": (w_gate, m_w_gate, v_w_gate, rows("gateT"), True),
        "w_up": (w_up, m_w_up, v_w_up, rows("upT"), True),
        "w_down": (w_down, m_w_down, v_w_down, rows("down"), False),
        "w_ple_gate": (w_ple_gate, m_w_ple_gate, v_w_ple_gate, rows("plg"), False),
        "w_ple_proj": (w_ple_proj, m_w_ple_proj, v_w_ple_proj,
                       (plp_rows.reshape(64, 4, 256).transpose(1, 0, 2).reshape(PLE_DIM, PLE_DIM), 0), False),
        "w_out": (w_out, m_w_out, v_w_out, None, False),
        "w_in": (w_in, m_w_in, v_w_in, None, True),
    }
    small_params = {
        "g_attn_norm": (g_attn_norm, m_g_attn_norm, v_g_attn_norm), "g_q": (g_q, m_g_q, v_g_q), "g_k": (g_k, m_g_k, v_g_k),
        "attn_sinks": (attn_sinks, m_attn_sinks, v_attn_sinks), "rel_bias": (rel_bias.T, m_rel_bias.T, v_rel_bias.T),
        "w_pool": tuple(a.reshape(512, 128) for a in (w_pool, m_w_pool, v_w_pool)),
        "pool_scale": (pool_scale, m_pool_scale, v_pool_scale), "g_ffn_norm": (g_ffn_norm, m_g_ffn_norm, v_g_ffn_norm),
        "g_ple_norm": (g_ple_norm, m_g_ple_norm, v_g_ple_norm),
    }

    grads, deltas, new_ms, new_vs = {}, {}, {}, {}
    out = grads_b = None
    for name, (w, m, v, g_src, transposed) in big.items():
        if g_src is None:
            if grads_b is None:
                grads_b = reduce_b.finish(out[-1])
            g_src = (grads_b, SLAB["out" if name == "w_out" else "inT"][0])
        view = (lambda a: a.T) if transposed else (lambda a: a)
        out = _adamw(view(w[0]), *g_src, view(m[0]), view(v[0]), "adamw_" + name)
        grads[name], deltas[name], new_ms[name], new_vs[name] = (view(a)[None] for a in out)

    loss, small_out = _adamw_small(small_all, reduce_a.tables, list(small_params.values()))
    for name, (g2, d, nm, nv) in zip(small_params, small_out):
        restore = {"w_pool": lambda a: a.reshape(w_pool.shape), "rel_bias": lambda a: a.T}.get(name, lambda a: a)
        grads[name], deltas[name], new_ms[name], new_vs[name] = (restore(a) for a in (g2, d, nm, nv))

    order = ["w_in", "w_out", "g_attn_norm", "g_q", "g_k", "attn_sinks", "rel_bias", "w_pool", "pool_scale", "g_ffn_norm",
             "w_gate", "w_up", "w_down", "g_ple_norm", "w_ple_gate", "w_ple_proj"]
    return (loss.reshape(()), dx[None], *[grads[n] for n in order], *[deltas[n] for n in order],
            *[new_ms[n] for n in order], *[new_vs[n] for n in order])
```

```python
import numpy as np
import jax
import jax.numpy as jnp
from jax import lax
from jax.experimental import pallas as pl
from jax.experimental.pallas import tpu as pltpu
from jax.experimental.pallas import tpu_sc as plsc

F32 = jnp.float32
BF16 = jnp.bfloat16
MESH = pl.DeviceIdType.MESH

D_MODEL = 1024
HEAD_DIM = 64
N_Q_HEADS = 8
ATTN_WIDTH = 512
POOL_WIDTH = 512
IN_WIDTH = 1280
D_FF = 2816
PLE_DIM = 256
FF_CHUNK = 1408
N_FF_CHUNKS = D_FF // FF_CHUNK
BLOCK = 128
N_BUCKETS = 32
MAX_DISTANCE = 128
EPS = 1e-6
NEG = -1e30
N_CHIPS = 4
N_DEV = 8

ADAM_LR = 0.001
ADAM_B1 = 0.9
ADAM_B2 = 0.999
ADAM_EPS = 1e-08
ADAM_WD = 0.01
ADAM_STEP = 10

SLAB = {"inT": (0, 320), "out": (320, 256), "gateT": (576, 704), "upT": (1280, 704), "down": (1984, 704),
        "plg": (2688, 256), "plp": (2944, 64)}
SLAB_ROWS = 3008
GATHER_PARTS = ((0, 576), (576, SLAB_ROWS))
POOL_HALO = 24

SMALL = {"g_attn": 0, "g_ffn": 8, "g_ple": 16, "pool_scale": 24, "g_q": 28, "g_k": 29, "sinks": 30, "loss": 31,
         "rel_bias": 32}
SMALL_ROWS = 64

VMEM_LIMIT_BIG = 60 * 1024 * 1024
VMEM_LIMIT = 48 * 1024 * 1024


def _params(vmem=VMEM_LIMIT, n_axes=1):
    return pltpu.CompilerParams(dimension_semantics=("arbitrary",) * n_axes, vmem_limit_bytes=vmem)


def _dot(a, b, ca, cb):
    return lax.dot_general(a, b, (((ca,), (cb,)), ((), ())), preferred_element_type=F32)


def _full(shape):
    return pl.BlockSpec(shape, lambda i: (0,) * len(shape))


ANY = pl.BlockSpec(memory_space=pl.ANY)
VMEM_WHOLE = pl.BlockSpec(memory_space=pltpu.VMEM)


W_SPECS = [ANY, ANY, pl.BlockSpec(memory_space=pltpu.SMEM)]


def _load_rows(w_refs, name, dst_ref, sems):
    slab_ref, local_ref, me_ref = w_refs
    off, rows = SLAB[name]
    slab_off = off - max(start for start, _ in GATHER_PARTS if start <= off)
    me = me_ref[0]
    for phase in ("start", "wait"):
        for j in range(N_CHIPS):
            dst = dst_ref.at[pl.ds(j * rows, rows), :]
            theirs = pltpu.make_async_copy(slab_ref.at[j, pl.ds(slab_off, rows), :], dst, sems.at[j])
            own = pltpu.make_async_copy(local_ref.at[pl.ds(slab_off, rows), :], dst, sems.at[j])

            @pl.when(me == j)
            def _():
                getattr(own, phase)()

            @pl.when(me != j)
            def _():
                getattr(theirs, phase)()


def _rms_fwd(x, g):
    r = lax.rsqrt(jnp.mean(x * x, axis=-1, keepdims=True) + EPS)
    return x * r * g


def _rms_bwd(x, g, dy):
    r = lax.rsqrt(jnp.mean(x * x, axis=-1, keepdims=True) + EPS)
    xn = x * r
    dyg = dy * g
    dx = r * (dyg - xn * jnp.mean(dyg * xn, axis=-1, keepdims=True))
    return dx, jnp.sum(dy * xn, axis=0, keepdims=True)


def _half_sum(v, lo):
    s_lo = jnp.sum(jnp.where(lo, v, 0.0), axis=-1, keepdims=True)
    s_hi = jnp.sum(jnp.where(lo, 0.0, v), axis=-1, keepdims=True)
    return jnp.where(lo, s_lo, s_hi)


def _half_sum_mxu(v):
    upper = lax.broadcasted_iota(jnp.int32, (128, 128), 0) < 64
    left = lax.broadcasted_iota(jnp.int32, (128, 128), 1) < 64
    ones = jnp.where(upper == left, 1.0, 0.0).astype(BF16)
    high = v.astype(BF16)
    low = (v - high.astype(F32)).astype(BF16)
    return _dot(high, ones, 1, 0) + _dot(low, ones, 1, 0)


def _pair_norm(zp, g, lo):
    r = lax.rsqrt(_half_sum(zp * zp, lo) * (1.0 / HEAD_DIM) + EPS)
    return zp * r * g


def _pair_norm_bwd(zp, g, dy):
    r = lax.rsqrt(_half_sum_mxu(zp * zp) * (1.0 / HEAD_DIM) + EPS)
    xn = zp * r
    dyg = dy * g
    dx = r * (dyg - xn * (_half_sum_mxu(dyg * xn) * (1.0 / HEAD_DIM)))
    return dx, jnp.sum(dy * xn, axis=0, keepdims=True)


def _pack_heads(pairs, lo):
    packed = [None] * 4
    for m in range(2):
        a, b = pairs[m], pairs[m + 2]
        packed[2 * m] = jnp.where(lo, a, pltpu.roll(b, 64, axis=1))
        packed[2 * m + 1] = jnp.where(lo, pltpu.roll(a, 64, axis=1), b)
    return packed


def _unpack_heads(packed, lo):
    pairs = [None] * 4
    for m in range(2):
        a, b = packed[2 * m], packed[2 * m + 1]
        pairs[m] = jnp.where(lo, a, pltpu.roll(b, 64, axis=1))
        pairs[m + 2] = jnp.where(lo, pltpu.roll(a, 64, axis=1), b)
    return pairs


def _expand_heads(packed):
    flat = packed.reshape(4 * BLOCK, 128)
    lo = lax.broadcasted_iota(jnp.int32, flat.shape, 1) < 64
    zero = jnp.zeros_like(flat)
    return jnp.concatenate([jnp.where(lo, flat, zero), jnp.where(lo, zero, flat)], axis=0)


def _fold_heads(stacked):
    half = 4 * BLOCK
    lo = lax.broadcasted_iota(jnp.int32, (half, 128), 1) < 64
    return jnp.where(lo, stacked[:half], stacked[half:]).reshape(4, BLOCK, 128)


def _sigmoid(v):
    return 1.0 / (1.0 + jnp.exp(-v))


def _pool_counts(tile, n_rows):
    t1 = tile * n_rows + lax.broadcasted_iota(jnp.int32, (n_rows, POOL_WIDTH), 0) + 1
    lane = lax.broadcasted_iota(jnp.int32, (n_rows, POOL_WIDTH), 1)
    win = jnp.where(lane < 128, 2, jnp.where(lane < 256, 4, jnp.where(lane < 384, 8, 16)))
    return jnp.minimum(t1, win).astype(F32)


def _first_norm(x2, g_attn):
    s_len = x2.shape[0]
    t = 512

    def body(x_ref, g_ref, hn_ref):
        hn_ref[...] = _rms_fwd(x_ref[...], g_ref[...]).astype(BF16)

    row = pl.BlockSpec((t, D_MODEL), lambda i: (i, 0))
    return pl.pallas_call(
        body, name="first_norm", grid=(s_len // t,), in_specs=[row, _full((1, D_MODEL))], out_specs=row,
        out_shape=jax.ShapeDtypeStruct((s_len, D_MODEL), BF16), compiler_params=_params(),
    )(x2, g_attn)


def _attn_in(hn1, gq, gk, wts):
    s_len = hn1.shape[0]
    t = 512

    def body(hn_ref, gq_ref, gk_ref, sl_ref, lo_ref, me_ref, zqk_ref, u_ref, kn_ref, v_ref, qst_ref, w_ref, sems):
        @pl.when(pl.program_id(0) == 0)
        def _():
            _load_rows((sl_ref, lo_ref, me_ref), "inT", w_ref, sems)

        z = _dot(hn_ref[...], w_ref[...], 1, 1)
        zqk_ref[...] = z[:, :640]
        u_ref[...] = z[:, 768:]
        v_ref[...] = z[:, 640:768].astype(BF16)
        lo = lax.broadcasted_iota(jnp.int32, (t, 128), 1) < 64
        kn_ref[...] = _pair_norm(z[:, 512:640], gk_ref[...], lo).astype(BF16)
        pairs = [_pair_norm(z[:, 128 * p:128 * p + 128], gq_ref[...], lo) for p in range(4)]
        for j, entry in enumerate(_pack_heads(pairs, lo)):
            qst_ref[j] = entry.astype(BF16)

    row = lambda w: pl.BlockSpec((t, w), lambda i: (i, 0))
    return pl.pallas_call(
        body, name="attn_in", grid=(s_len // t,),
        in_specs=[row(D_MODEL), _full((1, 128)), _full((1, 128))] + W_SPECS,
        out_specs=[row(640), row(POOL_WIDTH), row(128), row(128), pl.BlockSpec((4, t, 128), lambda i: (0, i, 0))],
        out_shape=[jax.ShapeDtypeStruct((s_len, 640), F32), jax.ShapeDtypeStruct((s_len, POOL_WIDTH), F32),
                   jax.ShapeDtypeStruct((s_len, 128), BF16), jax.ShapeDtypeStruct((s_len, 128), BF16),
                   jax.ShapeDtypeStruct((4, s_len, 128), BF16)],
        scratch_shapes=[pltpu.VMEM((IN_WIDTH, D_MODEL), BF16), pltpu.SemaphoreType.DMA((N_CHIPS,))],
        compiler_params=_params(),
    )(hn1, gq, gk, *wts)


def _bucket_table():
    i_idx = np.arange(BLOCK)[:, None]
    j_idx = np.arange(2 * BLOCK)[None, :]
    d = BLOCK + i_idx - j_idx
    n = np.maximum(d, 0)
    max_exact = N_BUCKETS // 2
    nf = np.maximum(n, 1).astype(np.float64)
    large = max_exact + (np.log(nf / max_exact) / np.log(MAX_DISTANCE / max_exact) * (N_BUCKETS - max_exact)).astype(np.int64)
    large = np.minimum(large, N_BUCKETS - 1)
    bucket = np.where(n < max_exact, n, large)
    return np.where((d >= 0) & (d < BLOCK), bucket, -1).astype(np.int32)


def _bias_build(rel_bias_t, bucket):
    def body(rb_ref, bucket_ref, out_ref):
        bk = bucket_ref[...]
        for h in range(N_Q_HEADS):
            acc = jnp.full((BLOCK, 2 * BLOCK), NEG, F32)
            for b in range(N_BUCKETS):
                acc = jnp.where(bk == b, rb_ref[h, b], acc)
            out_ref[0, pl.ds(h * BLOCK, BLOCK), :] = acc
            out_ref[1, pl.ds(h * BLOCK, BLOCK), :] = acc
            out_ref[1, pl.ds(h * BLOCK, BLOCK), 0:BLOCK] = jnp.full((BLOCK, BLOCK), NEG, F32)

    return pl.pallas_call(
        body, name="bias_build",
        in_specs=[pl.BlockSpec(memory_space=pltpu.SMEM), VMEM_WHOLE], out_specs=VMEM_WHOLE,
        out_shape=jax.ShapeDtypeStruct((2, N_Q_HEADS * BLOCK, 2 * BLOCK), F32),
    )(rel_bias_t, bucket)


def _head_softmax(s_ref, bias_ref, sink_ref, h):
    rows = pl.ds(pl.multiple_of(h * BLOCK, BLOCK), BLOCK)
    s = s_ref[rows, :] * (HEAD_DIM ** -0.5) + bias_ref[rows, :]
    sink = sink_ref[h]
    m = jnp.maximum(jnp.max(s, axis=-1, keepdims=True), sink)
    p = jnp.exp(s - m)
    e_sink = jnp.exp(sink - m)
    inv = 1.0 / (jnp.sum(p, axis=-1, keepdims=True) + e_sink)
    return rows, p * inv, e_sink * inv


ATTN_STEP_BLOCKS = 4
BAND = (N_Q_HEADS * BLOCK, 2 * BLOCK)


def _attn_specs():
    nb = ATTN_STEP_BLOCKS
    stacked = pl.BlockSpec((4, nb * BLOCK, 128), lambda i: (0, i, 0))
    kv = [pl.BlockSpec((BLOCK, 128), lambda i: (jnp.maximum(nb * i - 1, 0), 0)), pl.BlockSpec((nb * BLOCK, 128), lambda i: (i, 0))]
    consts = [_full((2,) + BAND), pl.BlockSpec(memory_space=pltpu.SMEM)]
    return stacked, kv, consts


def _step_blocks(i, kp_ref, kc_ref, vp_ref, vc_ref, bias_ref):
    blocks = []
    for b in range(ATTN_STEP_BLOCKS):
        if b == 0:
            k2 = jnp.concatenate([kp_ref[...], kc_ref[pl.ds(0, BLOCK), :]], axis=0)
            v2 = jnp.concatenate([vp_ref[...], vc_ref[pl.ds(0, BLOCK), :]], axis=0)
            bias = bias_ref.at[jnp.where(i == 0, 1, 0)]
        else:
            k2, v2, bias = kc_ref[pl.ds((b - 1) * BLOCK, 2 * BLOCK), :], vc_ref[pl.ds((b - 1) * BLOCK, 2 * BLOCK), :], bias_ref.at[0]
        blocks.append((pl.ds(b * BLOCK, BLOCK), k2, v2, bias))
    return blocks


def _attn_fwd(qst, kn, vb, bias_st, sinks):
    s_len = kn.shape[0]

    def body(q_ref, kp_ref, kc_ref, vp_ref, vc_ref, bias_ref, sink_ref, o_ref, s_ref, p_ref):
        for b, (rows, k2, v2, bias) in enumerate(_step_blocks(pl.program_id(0), kp_ref, kc_ref, vp_ref, vc_ref, bias_ref)):
            s_b, p_b = s_ref.at[b], p_ref.at[b]
            s_b[...] = _dot(_expand_heads(q_ref[:, rows, :]), k2, 1, 1)

            def head(h, carry):
                head_rows, probs, _ = _head_softmax(s_b, bias, sink_ref, h)
                p_b[head_rows, :] = probs.astype(BF16)
                return carry

            lax.fori_loop(0, N_Q_HEADS, head, 0, unroll=True)
            o_ref[:, rows, :] = _fold_heads(_dot(p_b[...], v2, 1, 0)).astype(BF16)

    stacked, kv, consts = _attn_specs()
    return pl.pallas_call(
        body, name="attn_fwd", grid=(s_len // (ATTN_STEP_BLOCKS * BLOCK),),
        in_specs=[stacked] + kv + kv + consts, out_specs=stacked,
        out_shape=jax.ShapeDtypeStruct((4, s_len, 128), BF16),
        scratch_shapes=[pltpu.VMEM((ATTN_STEP_BLOCKS,) + BAND, F32), pltpu.VMEM((ATTN_STEP_BLOCKS,) + BAND, BF16)],
        compiler_params=_params(),
    )(qst, kn, kn, vb, vb, bias_st, sinks)


def _mix_out(u, ost, x2, wts, wpool, pool_scale, g_ffn):
    s_len = x2.shape[0]
    t = 512
    n = t + 16

    def body(u_ref, o_ref, x_ref, sl_ref, lo_ref, me_ref, wp_ref, sc_ref, g_ref, pooled_ref, mix_ref, h1_ref, hn_ref,
             w_ref, ext_ref, st_ref, sems):
        i = pl.program_id(0)

        @pl.when(i == 0)
        def _():
            _load_rows((sl_ref, lo_ref, me_ref), "out", w_ref, sems)
            ext_ref[...] = jnp.zeros_like(ext_ref)
            st_ref[...] = jnp.zeros_like(st_ref)

        u_tile = u_ref[...]
        ext_ref[pl.ds(POOL_HALO, t), :] = u_tile
        st_ref[pl.ds(8, n), :] = ext_ref[pl.ds(8, n), :] + ext_ref[pl.ds(7, n), :]
        st_ref[pl.ds(8, n), 128:] = st_ref[pl.ds(8, n), 128:] + st_ref[pl.ds(6, n), 128:]
        st_ref[pl.ds(8, n), 256:] = st_ref[pl.ds(8, n), 256:] + st_ref[pl.ds(4, n), 256:]
        st_ref[pl.ds(8, n), 384:] = st_ref[pl.ds(8, n), 384:] + st_ref[pl.ds(0, n), 384:]
        ext_ref[pl.ds(0, POOL_HALO), :] = ext_ref[pl.ds(t, POOL_HALO), :]
        pooled = (st_ref[pl.ds(POOL_HALO, t), :] / _pool_counts(i, t) - u_tile).astype(BF16)
        pooled_ref[...] = pooled
        for g in range(4):
            cols = slice(128 * g, 128 * g + 128)
            y = _dot(pooled[:, cols], wp_ref[g], 1, 0) * sc_ref[:, cols]
            mix_ref[:, ATTN_WIDTH + 128 * g:ATTN_WIDTH + 128 * g + 128] = y.astype(BF16)
        lo = lax.broadcasted_iota(jnp.int32, (t, 128), 1) < 64
        for p, pair in enumerate(_unpack_heads([o_ref[j].astype(F32) for j in range(4)], lo)):
            mix_ref[:, 128 * p:128 * p + 128] = pair.astype(BF16)
        h1 = x_ref[...] + _dot(mix_ref[...], w_ref[...], 1, 0)
        h1_ref[...] = h1
        hn_ref[...] = _rms_fwd(h1, g_ref[...]).astype(BF16)

    row = lambda w: pl.BlockSpec((t, w), lambda i: (i, 0))
    return pl.pallas_call(
        body, name="mix_out", grid=(s_len // t,),
        in_specs=[row(POOL_WIDTH), pl.BlockSpec((4, t, 128), lambda i: (0, i, 0)), row(D_MODEL)] + W_SPECS
        + [_full((4, 128, 128)), _full((1, POOL_WIDTH)), _full((1, D_MODEL))],
        out_specs=[row(POOL_WIDTH), row(D_MODEL), row(D_MODEL), row(D_MODEL)],
        out_shape=[jax.ShapeDtypeStruct((s_len, POOL_WIDTH), BF16), jax.ShapeDtypeStruct((s_len, D_MODEL), BF16),
                   jax.ShapeDtypeStruct((s_len, D_MODEL), F32), jax.ShapeDtypeStruct((s_len, D_MODEL), BF16)],
        scratch_shapes=[pltpu.VMEM((D_MODEL, D_MODEL), BF16), pltpu.VMEM((t + POOL_HALO, POOL_WIDTH), F32),
                        pltpu.VMEM((t + POOL_HALO, POOL_WIDTH), F32), pltpu.SemaphoreType.DMA((N_CHIPS,))],
        compiler_params=_params(),
    )(u, ost, x2, *wts, wpool, pool_scale, g_ffn)


def _ffn_ple(hn2, h1, p2, tgt, wts, g_ffn, g_ple):
    s_len = h1.shape[0]
    t = 256
    n_tiles = s_len // t

    def body(hn_ref, h1_ref, p_ref, tgt_ref, sl_ref, lo_ref, me_ref, gf_ref, gp_ref,
             loss_ref, dgate_ref, dup_ref, act_ref, dh2b_ref, hn3_ref, dgl_ref, dwp_ref, dh1_ref, dgf_ref, dgp_ref,
             wg_ref, wu_ref, wd_ref, wl_ref, wp_ref, packed_ref, gate_s, up_s, loss_acc, dwp_acc, sems):
        i = pl.program_id(0)

        @pl.when(i == 0)
        def _():
            w_refs = (sl_ref, lo_ref, me_ref)
            _load_rows(w_refs, "gateT", wg_ref, sems)
            _load_rows(w_refs, "upT", wu_ref, sems)
            _load_rows(w_refs, "down", wd_ref, sems)
            _load_rows(w_refs, "plg", wl_ref, sems)
            _load_rows(w_refs, "plp", packed_ref, sems)
            for j in range(N_CHIPS):
                for q in range(4):
                    wp_ref[pl.ds(64 * q, 64), 256 * j:256 * j + 256] = packed_ref[pl.ds(64 * j, 64), 256 * q:256 * q + 256]
            loss_acc[...] = jnp.zeros_like(loss_acc)
            dwp_acc[...] = jnp.zeros_like(dwp_acc)
            dgf_ref[...] = jnp.zeros_like(dgf_ref)
            dgp_ref[...] = jnp.zeros_like(dgp_ref)

        hn = hn_ref[...]
        h1v = h1_ref[...]
        h2 = h1v
        for ch in range(N_FF_CHUNKS):
            rows = pl.ds(ch * FF_CHUNK, FF_CHUNK)
            gate = _dot(hn, wg_ref[rows, :], 1, 1)
            up = _dot(hn, wu_ref[rows, :], 1, 1)
            gate_s[ch] = gate
            up_s[ch] = up
            act = (gate * _sigmoid(gate) * up).astype(BF16)
            act_ref[ch] = act
            h2 = h2 + _dot(act, wd_ref[rows, :], 1, 0)
        gp = gp_ref[...]
        hn3 = _rms_fwd(h2, gp).astype(BF16)
        hn3_ref[...] = hn3
        gate2 = _sigmoid(_dot(hn3, wl_ref[...], 1, 0))
        p_tile = p_ref[...].astype(BF16)
        pp = _dot(p_tile, wp_ref[...], 1, 0)
        err = h2 + gate2 * pp - tgt_ref[...]
        loss_acc[...] += jnp.sum(err * err, axis=0, keepdims=True)
        dy = err * (1.0 / D_MODEL)
        dwp_acc[...] += _dot(p_tile, (dy * gate2).astype(BF16), 0, 0)
        dgl = (dy * pp * gate2 * (1.0 - gate2)).astype(BF16)
        dgl_ref[...] = dgl
        dx3, dg3 = _rms_bwd(h2, gp, _dot(dgl, wl_ref[...], 1, 1))
        dh2 = dy + dx3
        dgp_ref[...] += dg3
        dh2b = dh2.astype(BF16)
        dh2b_ref[...] = dh2b
        dhn = jnp.zeros((t, D_MODEL), F32)
        for ch in range(N_FF_CHUNKS):
            rows = pl.ds(ch * FF_CHUNK, FF_CHUNK)
            dact = _dot(dh2b, wd_ref[rows, :], 1, 1)
            gate_v = gate_s[ch]
            up_v = up_s[ch]
            sg = _sigmoid(gate_v)
            dup = (dact * (gate_v * sg)).astype(BF16)
            dgate = (dact * up_v * (sg * (1.0 + gate_v * (1.0 - sg)))).astype(BF16)
            dup_ref[ch] = dup
            dgate_ref[ch] = dgate
            dhn = dhn + _dot(dgate, wg_ref[rows, :], 1, 0) + _dot(dup, wu_ref[rows, :], 1, 0)
        dx, dg = _rms_bwd(h1v, gf_ref[...], dhn)
        dh1_ref[...] = dh2 + dx
        dgf_ref[...] += dg

        @pl.when(i == n_tiles - 1)
        def _():
            total = jnp.sum(loss_acc[...], axis=-1, keepdims=True) * (0.5 / D_MODEL)
            loss_ref[...] = jnp.broadcast_to(total, loss_ref.shape)
            dwp_ref[...] = dwp_acc[...].astype(BF16)

    row = lambda w: pl.BlockSpec((t, w), lambda i: (i, 0))
    chunked = pl.BlockSpec((N_FF_CHUNKS, t, FF_CHUNK), lambda i: (0, i, 0))
    vec = _full((1, D_MODEL))
    act_shape = jax.ShapeDtypeStruct((N_FF_CHUNKS, s_len, FF_CHUNK), BF16)
    tok = lambda dtype: jax.ShapeDtypeStruct((s_len, D_MODEL), dtype)
    return pl.pallas_call(
        body, name="ffn_ple", grid=(n_tiles,),
        in_specs=[row(D_MODEL), row(D_MODEL), row(PLE_DIM), row(D_MODEL)] + W_SPECS + [vec, vec],
        out_specs=[_full((1, 128)), chunked, chunked, chunked] + [row(D_MODEL)] * 3 + [_full((PLE_DIM, D_MODEL)), row(D_MODEL),
                                                                                       vec, vec],
        out_shape=[jax.ShapeDtypeStruct((1, 128), F32), act_shape, act_shape, act_shape, tok(BF16), tok(BF16), tok(BF16),
                   jax.ShapeDtypeStruct((PLE_DIM, D_MODEL), BF16), tok(F32), jax.ShapeDtypeStruct((1, D_MODEL), F32),
                   jax.ShapeDtypeStruct((1, D_MODEL), F32)],
        scratch_shapes=[pltpu.VMEM((D_FF, D_MODEL), BF16)] * 3
        + [pltpu.VMEM((D_MODEL, D_MODEL), BF16), pltpu.VMEM((PLE_DIM, D_MODEL), BF16), pltpu.VMEM((PLE_DIM, D_MODEL), BF16),
           pltpu.VMEM((N_FF_CHUNKS, t, FF_CHUNK), F32), pltpu.VMEM((N_FF_CHUNKS, t, FF_CHUNK), F32), pltpu.VMEM((1, D_MODEL), F32),
           pltpu.VMEM((PLE_DIM, D_MODEL), F32), pltpu.SemaphoreType.DMA((N_CHIPS,))],
        compiler_params=_params(VMEM_LIMIT_BIG),
    )(hn2, h1, p2, tgt, *wts, g_ffn, g_ple)


def _accumulate_tn(acc_ref, a, b, first):
    @pl.when(first)
    def _():
        acc_ref[...] = _dot(a, b, 0, 0)

    @pl.when(jnp.logical_not(first))
    def _():
        acc_ref[...] += _dot(a, b, 0, 0)


def _flush_chunks(acc_ref, stage_ref, slab_ref, name, sems):
    stage_ref[...] = acc_ref[...].astype(BF16)
    off, rows = SLAB[name]
    copies = [pltpu.make_async_copy(stage_ref.at[pl.ds(j * rows, rows), :], slab_ref.at[j, pl.ds(off, rows), :], sems.at[j])
              for j in range(N_CHIPS)]
    for cp in copies:
        cp.start()
    for cp in copies:
        cp.wait()


def _mix_out_bwd(dh1, wts, pooled, wpool, pool_scale, mix, after):
    s_len = dh1.shape[0]
    t = 512
    n = t + 16
    n_tiles = s_len // t
    early_rows = GATHER_PARTS[0][1]

    def body(dh1_ref, sl_ref, lo_ref, me_ref, pooled_ref, wp_ref, sc_ref, mix_ref, after_ref, dost_ref, du_ref, dwp_ref,
             dsc_ref, slab_ref, w_ref, ext_ref, st_ref, acc_ref, stage_ref, sems):
        del after_ref
        i = pl.program_id(0)

        @pl.when(i == 0)
        def _():
            _load_rows((sl_ref, lo_ref, me_ref), "out", w_ref, sems)
            ext_ref[...] = jnp.zeros_like(ext_ref)
            st_ref[...] = jnp.zeros_like(st_ref)
            dsc_ref[...] = jnp.zeros_like(dsc_ref)
            dwp_ref[...] = jnp.zeros_like(dwp_ref)
            acc_ref[...] = jnp.zeros_like(acc_ref)

        dh1b = dh1_ref[...].astype(BF16)
        acc_ref[...] += _dot(mix_ref[...], dh1b, 0, 0)
        dmix = _dot(dh1b, w_ref[...], 1, 1)
        lo = lax.broadcasted_iota(jnp.int32, (t, 128), 1) < 64
        for j, entry in enumerate(_pack_heads([dmix[:, 128 * p:128 * p + 128] for p in range(4)], lo)):
            dost_ref[j] = entry.astype(BF16)
        pooled_v = pooled_ref[...]
        counts = _pool_counts(n_tiles - 1 - i, t)
        for g in range(4):
            cols = slice(128 * g, 128 * g + 128)
            dm = dmix[:, ATTN_WIDTH + 128 * g:ATTN_WIDTH + 128 * g + 128]
            ypre = _dot(pooled_v[:, cols], wp_ref[g], 1, 0)
            dsc_ref[:, cols] += jnp.sum(ypre * dm, axis=0, keepdims=True)
            dyp = (dm * sc_ref[:, cols]).astype(BF16)
            dwp_ref[g] += _dot(pooled_v[:, cols], dyp, 0, 0)
            dpooled = _dot(dyp, wp_ref[g], 1, 1)
            du_ref[:, cols] = -dpooled
            ext_ref[pl.ds(0, t), cols] = dpooled / counts[:, cols]
        st_ref[pl.ds(0, n), :] = ext_ref[pl.ds(0, n), :] + ext_ref[pl.ds(1, n), :]
        st_ref[pl.ds(0, n), 128:] = st_ref[pl.ds(0, n), 128:] + st_ref[pl.ds(2, n), 128:]
        st_ref[pl.ds(0, n), 256:] = st_ref[pl.ds(0, n), 256:] + st_ref[pl.ds(4, n), 256:]
        st_ref[pl.ds(0, n), 384:] = st_ref[pl.ds(0, n), 384:] + st_ref[pl.ds(8, n), 384:]
        ext_ref[pl.ds(t, POOL_HALO), :] = ext_ref[pl.ds(0, POOL_HALO), :]
        du_ref[...] += st_ref[pl.ds(0, t), :]

        @pl.when(i == n_tiles - 1)
        def _():
            _flush_chunks(acc_ref, stage_ref, slab_ref, "out", sems)

    rev = lambda w: pl.BlockSpec((t, w), lambda i: (n_tiles - 1 - i, 0))
    return pl.pallas_call(
        body, name="mix_out_bwd", grid=(n_tiles,),
        in_specs=[rev(D_MODEL)] + W_SPECS + [rev(POOL_WIDTH), _full((4, 128, 128)), _full((1, POOL_WIDTH)), rev(D_MODEL), ANY],
        out_specs=[pl.BlockSpec((4, t, 128), lambda i: (0, n_tiles - 1 - i, 0)), rev(POOL_WIDTH),
                   _full((4, 128, 128)), _full((1, POOL_WIDTH)), ANY],
        out_shape=[jax.ShapeDtypeStruct((4, s_len, 128), BF16), jax.ShapeDtypeStruct((s_len, POOL_WIDTH), F32),
                   jax.ShapeDtypeStruct((4, 128, 128), F32), jax.ShapeDtypeStruct((1, POOL_WIDTH), F32),
                   jax.ShapeDtypeStruct((N_CHIPS, early_rows, D_MODEL), BF16)],
        scratch_shapes=[pltpu.VMEM((D_MODEL, D_MODEL), BF16), pltpu.VMEM((t + POOL_HALO, POOL_WIDTH), F32),
                        pltpu.VMEM((t + POOL_HALO, POOL_WIDTH), F32), pltpu.VMEM((D_MODEL, D_MODEL), F32),
                        pltpu.VMEM((D_MODEL, D_MODEL), BF16), pltpu.SemaphoreType.DMA((N_CHIPS,))],
        compiler_params=_params(),
    )(dh1, *wts, pooled, wpool, pool_scale, mix, after)


def _attn_bwd(qst, kn, vb, dost, bias_st, sinks, after):
    s_len = kn.shape[0]

    def body(q_ref, kp_ref, kc_ref, vp_ref, vc_ref, do_ref, bias_ref, sink_ref, after_ref, dq_ref, dk_ref, dv_ref, dbias_ref,
             dsink_ref, s_ref, dp_ref, p_ref, dl_ref):
        del after_ref
        i = pl.program_id(0)

        @pl.when(i == 0)
        def _():
            dk_ref[...] = jnp.zeros_like(dk_ref)
            dv_ref[...] = jnp.zeros_like(dv_ref)
            dbias_ref[...] = jnp.zeros_like(dbias_ref)
            dsink_ref[...] = jnp.zeros_like(dsink_ref)

        for b, (rows, k2, v2, bias) in enumerate(_step_blocks(i, kp_ref, kc_ref, vp_ref, vc_ref, bias_ref)):
            s_b, dp_b, p_b, dl_b = s_ref.at[b], dp_ref.at[b], p_ref.at[b], dl_ref.at[b]
            q = _expand_heads(q_ref[:, rows, :])
            do = _expand_heads(do_ref[:, rows, :])
            s_b[...] = _dot(q, k2, 1, 1)
            dp_b[...] = _dot(do, v2, 1, 1)

            def head(h, carry):
                head_rows, probs, p_sink = _head_softmax(s_b, bias, sink_ref, h)
                dp = dp_b[head_rows, :]
                dsum = jnp.sum(probs * dp, axis=-1, keepdims=True)
                dlog = probs * (dp - dsum)
                dsink_ref[head_rows, :] -= p_sink * dsum
                dbias_ref[head_rows, :] += dlog
                p_b[head_rows, :] = probs.astype(BF16)
                dl_b[head_rows, :] = (dlog * (HEAD_DIM ** -0.5)).astype(BF16)
                return carry

            lax.fori_loop(0, N_Q_HEADS, head, 0, unroll=True)
            dlog_s = dl_b[...]
            dq_ref[:, rows, :] = _fold_heads(_dot(dlog_s, k2, 1, 0))
            dk2 = _dot(dlog_s, q, 0, 0)
            dv2 = _dot(p_b[...], do, 0, 0)
            block = ATTN_STEP_BLOCKS * i + b
            prev_rows = pl.ds(pl.multiple_of(jnp.maximum(block - 1, 0) * BLOCK, BLOCK), BLOCK)
            cur_rows = pl.ds(pl.multiple_of(block * BLOCK, BLOCK), BLOCK)
            dk_ref[prev_rows, :] += dk2[:BLOCK]
            dk_ref[cur_rows, :] += dk2[BLOCK:]
            dv_ref[prev_rows, :] += dv2[:BLOCK]
            dv_ref[cur_rows, :] += dv2[BLOCK:]

    stacked, kv, consts = _attn_specs()
    per_step = (ATTN_STEP_BLOCKS,) + BAND
    return pl.pallas_call(
        body, name="attn_bwd", grid=(s_len // (ATTN_STEP_BLOCKS * BLOCK),),
        in_specs=[stacked] + kv + kv + [stacked] + consts + [ANY],
        out_specs=[stacked, _full((s_len, 128)), _full((s_len, 128)), _full(BAND), _full((N_Q_HEADS * BLOCK, 1))],
        out_shape=[jax.ShapeDtypeStruct((4, s_len, 128), F32), jax.ShapeDtypeStruct((s_len, 128), F32),
                   jax.ShapeDtypeStruct((s_len, 128), F32), jax.ShapeDtypeStruct(BAND, F32),
                   jax.ShapeDtypeStruct((N_Q_HEADS * BLOCK, 1), F32)],
        scratch_shapes=[pltpu.VMEM(per_step, F32), pltpu.VMEM(per_step, F32), pltpu.VMEM(per_step, BF16),
                        pltpu.VMEM(per_step, BF16)],
        compiler_params=_params(),
    )(qst, kn, kn, vb, vb, dost, bias_st, sinks, after)


def _flip_rows(x):
    n = x.shape[0]
    exchange = (lax.broadcasted_iota(jnp.int32, (n, n), 0) + lax.broadcasted_iota(jnp.int32, (n, n), 1) == n - 1)
    exchange = jnp.where(exchange, 1.0, 0.0).astype(BF16)
    flipped, rest = None, x
    for _ in range(3):
        term = rest.astype(BF16)
        rest = rest - term.astype(F32)
        part = _dot(exchange, term, 1, 0)
        flipped = part if flipped is None else flipped + part
    return flipped


def _small_pack(dg_attn, dg_ffn, dg_ple, dscale, dgq, dgk, dbias, dsink_rows, loss_v):
    def body(ga_ref, gf_ref, gp_ref, sc_ref, gq_ref, gk_ref, db_ref, ds_ref, bucket_ref, loss_ref, out_ref):
        out_ref[...] = jnp.zeros((SMALL_ROWS, 128), F32)
        for name, ref, n in (("g_attn", ga_ref, 8), ("g_ffn", gf_ref, 8), ("g_ple", gp_ref, 8), ("pool_scale", sc_ref, 4)):
            for k in range(n):
                out_ref[pl.ds(SMALL[name] + k, 1), :] = ref[:, 128 * k:128 * k + 128]
        for name, ref in (("g_q", gq_ref), ("g_k", gk_ref)):
            both = ref[...]
            out_ref[pl.ds(SMALL[name], 1), :] = both + pltpu.roll(both, 64, axis=1)
        out_ref[pl.ds(SMALL["loss"], 1), :] = loss_ref[...]
        by_diagonal = lambda flipped: pltpu.roll(flipped, 0, 1, stride=1, stride_axis=0)
        bucket_of = jnp.max(by_diagonal(bucket_ref[...]), axis=0, keepdims=True)
        sums = jnp.concatenate([jnp.sum(by_diagonal(_flip_rows(db_ref[pl.ds(h * BLOCK, BLOCK), :])), axis=0, keepdims=True)
                                for h in range(N_Q_HEADS)], axis=0)
        lanes = lax.broadcasted_iota(jnp.int32, (N_Q_HEADS, 128), 1)
        lane1 = lax.broadcasted_iota(jnp.int32, (1, 128), 1)
        rb = jnp.zeros((N_Q_HEADS, 128), F32)
        for b in range(N_BUCKETS):
            rb = jnp.where(lanes == b, jnp.sum(jnp.where(bucket_of == float(b), sums, 0.0), axis=1, keepdims=True), rb)
        sk = jnp.zeros((1, 128), F32)
        for h in range(N_Q_HEADS):
            sk = jnp.where(lane1 == h, jnp.sum(ds_ref[pl.ds(h * BLOCK, BLOCK), :]), sk)
        out_ref[pl.ds(SMALL["rel_bias"], N_Q_HEADS), :] = rb
        out_ref[pl.ds(SMALL["sinks"], 1), :] = sk

    bucket = jnp.asarray(_bucket_table()[::-1].astype(np.float32))
    return pl.pallas_call(
        body, name="small_pack", in_specs=[VMEM_WHOLE] * 10, out_specs=VMEM_WHOLE,
        out_shape=jax.ShapeDtypeStruct((SMALL_ROWS, 128), F32),
    )(dg_attn, dg_ffn, dg_ple, dscale, dgq, dgk, dbias, dsink_rows, bucket, loss_v)


def _attn_in_bwd(dqst, zqk, dk, dv, du, x2, dh1, hn1, slab, wts, g_attn, gq, gk):
    s_len = x2.shape[0]
    t = 512
    n_tiles = s_len // t

    def body(dq_ref, zqk_ref, dk_ref, dv_ref, du_ref, x_ref, dh1_ref, hn_ref, slab_in_ref, sl_ref, lo_ref, me_ref, g_ref,
             gq_ref, gk_ref, dx_ref, dg_ref, dgq_ref, dgk_ref, slab_ref, w_ref, dz_ref, acc_ref, stage_ref, sems):
        del slab_in_ref
        i = pl.program_id(0)

        @pl.when(i == 0)
        def _():
            _load_rows((sl_ref, lo_ref, me_ref), "inT", w_ref, sems)
            dg_ref[...] = jnp.zeros_like(dg_ref)
            dgq_ref[...] = jnp.zeros_like(dgq_ref)
            dgk_ref[...] = jnp.zeros_like(dgk_ref)
            acc_ref[...] = jnp.zeros_like(acc_ref)

        lo = lax.broadcasted_iota(jnp.int32, (t, 128), 1) < 64
        for p, dqn in enumerate(_unpack_heads([dq_ref[j] for j in range(4)], lo)):
            dq_raw, dgq = _pair_norm_bwd(zqk_ref[:, 128 * p:128 * p + 128], gq_ref[...], dqn)
            dz_ref[:, 128 * p:128 * p + 128] = dq_raw.astype(BF16)
            dgq_ref[...] += dgq
        dk_raw, dgk = _pair_norm_bwd(zqk_ref[:, 512:640], gk_ref[...], dk_ref[...])
        dgk_ref[...] += dgk
        dz_ref[:, 512:640] = dk_raw.astype(BF16)
        dz_ref[:, 640:768] = dv_ref[...].astype(BF16)
        dz_ref[:, 768:] = du_ref[...].astype(BF16)
        dz = dz_ref[...]
        acc_ref[...] += _dot(dz, hn_ref[...], 0, 0)
        dx, dg = _rms_bwd(x_ref[...], g_ref[...], _dot(dz, w_ref[...], 1, 0))
        dx_ref[...] = dh1_ref[...] + dx
        dg_ref[...] += dg

        @pl.when(i == n_tiles - 1)
        def _():
            _flush_chunks(acc_ref, stage_ref, slab_ref, "inT", sems)

    row = lambda w: pl.BlockSpec((t, w), lambda i: (i, 0))
    return pl.pallas_call(
        body, name="attn_in_bwd", grid=(n_tiles,),
        in_specs=[pl.BlockSpec((4, t, 128), lambda i: (0, i, 0)), row(640), row(128), row(128), row(POOL_WIDTH),
                  row(D_MODEL), row(D_MODEL), row(D_MODEL), ANY] + W_SPECS + [_full((1, D_MODEL)), _full((1, 128)),
                                                                              _full((1, 128))],
        out_specs=[row(D_MODEL), _full((1, D_MODEL)), _full((1, 128)), _full((1, 128)), ANY],
        out_shape=[jax.ShapeDtypeStruct((s_len, D_MODEL), F32), jax.ShapeDtypeStruct((1, D_MODEL), F32),
                   jax.ShapeDtypeStruct((1, 128), F32), jax.ShapeDtypeStruct((1, 128), F32),
                   jax.ShapeDtypeStruct(slab.shape, BF16)],
        input_output_aliases={8: 4},
        scratch_shapes=[pltpu.VMEM((IN_WIDTH, D_MODEL), BF16), pltpu.VMEM((t, IN_WIDTH), BF16),
                        pltpu.VMEM((IN_WIDTH, D_MODEL), F32), pltpu.VMEM((IN_WIDTH, D_MODEL), BF16),
                        pltpu.SemaphoreType.DMA((N_CHIPS,))],
        compiler_params=_params(),
    )(dqst, zqk, dk, dv, du, x2, dh1, hn1, slab, *wts, g_attn, gq, gk)


def _dw(lefts, b, name, slab, slab_rows, row_offs):
    a0, n_a = lefts[0], len(lefts)
    assert b.shape[1] == D_MODEL
    if a0.ndim == 3:
        n_chunks, s_len, tm = a0.shape
        m = n_chunks * tm
    else:
        s_len, tm = a0.shape
        m = tm
    tk = 2048 if n_a * tm <= 1408 else 1024
    if a0.ndim == 3:
        a_spec = pl.BlockSpec((None, tk, tm), lambda i, k: (i, k, 0))
    else:
        a_spec = pl.BlockSpec((tk, tm), lambda i, k: (k, i))
    n_steps, n_tiles = s_len // tk, m // tm
    chunk = m // N_CHIPS
    per_tile = tm // chunk

    def body(*refs):
        a_refs, b_ref = refs[:n_a], refs[n_a]
        o_ref, acc_ref, stage_ref, sems = refs[-4:]
        i, k = pl.program_id(0), pl.program_id(1)
        b_tile = b_ref[...].astype(BF16)
        for w, a_ref in enumerate(a_refs):
            _accumulate_tn(acc_ref.at[w], a_ref[...].astype(BF16), b_tile, k == 0)

        def out_copies(tile, slot):
            return [pltpu.make_async_copy(stage_ref.at[slot, w, pl.ds(jj * chunk, chunk), :],
                                          o_ref.at[tile * per_tile + jj, pl.ds(row_offs[w], chunk), :], sems.at[slot, w, jj])
                    for w in range(n_a) for jj in range(per_tile)]

        @pl.when(k == n_steps - 1)
        def _():
            slot = i % 2

            @pl.when(i >= 2)
            def _():
                for cp in out_copies(i - 2, slot):
                    cp.wait()

            stage_ref[slot] = acc_ref[...].astype(BF16)
            for cp in out_copies(i, slot):
                cp.start()

            @pl.when(i == n_tiles - 1)
            def _():
                for cp in out_copies(i, slot):
                    cp.wait()
                if n_tiles > 1:
                    for cp in out_copies(i - 1, 1 - slot):
                        cp.wait()

    in_specs = [a_spec] * n_a + [pl.BlockSpec((tk, D_MODEL), lambda i, k: (k, 0))]
    operands, aliases = [*lefts, b], {}
    if slab is not None:
        in_specs.append(ANY)
        operands.append(slab)
        aliases = {n_a + 1: 0}
    return pl.pallas_call(
        body, name=name, grid=(n_tiles, n_steps), in_specs=in_specs, out_specs=ANY,
        out_shape=jax.ShapeDtypeStruct((N_CHIPS, slab_rows, D_MODEL), BF16), input_output_aliases=aliases,
        scratch_shapes=[pltpu.VMEM((n_a, tm, D_MODEL), F32), pltpu.VMEM((2, n_a, tm, D_MODEL), BF16),
                        pltpu.SemaphoreType.DMA((2, n_a, per_tile))],
        compiler_params=_params(VMEM_LIMIT_BIG, n_axes=2),
    )(*operands)


def _position():
    x, y, c = lax.axis_index("x"), lax.axis_index("y"), lax.axis_index("c")
    other_chips = [(1 - x, y), (x, 1 - y), (1 - x, 1 - y)]
    return x, y, c, other_chips


def _ag_weights(local_slab, row0, n_rows, name, collective_id):
    half = n_rows // 2
    quarter = half // 2
    assert quarter % 16 == 0

    def body(l_ref, g_ref, send, recv):
        x, y, c, chips = _position()
        me, (via_x, via_y, diagonal) = 2 * x + y, [2 * chip[0] + chip[1] for chip in chips]
        here, sibling, x_nbr, y_nbr = (x, y, c), (x, y, 1 - c), (1 - x, y, c), (x, 1 - y, c)
        peers = [sibling, x_nbr, y_nbr]
        barrier = pltpu.get_barrier_semaphore()
        for peer in peers:
            pl.semaphore_signal(barrier, inc=1, device_id=peer, device_id_type=MESH)
        pl.semaphore_wait(barrier, len(peers))

        def rows(core, part):
            start, size = (core * half, half) if part is None else (core * half + part * quarter, quarter)
            return pl.ds(pl.multiple_of(start, 16), size)

        def copy(k, chip_idx, where, to, src=None):
            dst = g_ref.at[chip_idx, where, :]
            return pltpu.make_async_remote_copy(src_ref=dst if src is None else src, dst_ref=dst, send_sem=send.at[k],
                                                recv_sem=recv.at[k], device_id=to, device_id_type=MESH)

        own_rows = l_ref.at[pl.ds(pl.multiple_of(row0 + c * half, 16), half), :]
        started = [copy(0, me, rows(c, None), x_nbr, src=own_rows), copy(1, me, rows(c, None), y_nbr, src=own_rows)]
        for cp in started:
            cp.start()
        after_arrival = [
            (copy(0, via_x, rows(c, None), here), [copy(4, via_x, rows(c, None), sibling), copy(3, via_x, rows(c, 1), y_nbr)]),
            (copy(1, via_y, rows(c, None), here), [copy(5, via_y, rows(c, None), sibling), copy(2, via_y, rows(c, 0), x_nbr)]),
            (copy(2, diagonal, rows(c, 0), here), [copy(6, diagonal, rows(c, 0), sibling)]),
            (copy(3, diagonal, rows(c, 1), here), [copy(7, diagonal, rows(c, 1), sibling)]),
        ]
        for arrival, onward in after_arrival:
            arrival.wait_recv()
            for cp in onward:
                cp.start()
            started += onward
        for cp in (copy(4, via_x, rows(1 - c, None), here), copy(5, via_y, rows(1 - c, None), here),
                   copy(6, diagonal, rows(1 - c, 0), here), copy(7, diagonal, rows(1 - c, 1), here)):
            cp.wait_recv()
        for cp in started:
            cp.wait_send()

    return pl.kernel(
        body, out_type=jax.ShapeDtypeStruct((N_CHIPS, n_rows, D_MODEL), BF16),
        mesh=plsc.ScalarSubcoreMesh(axis_name="sequencer", num_cores=1), name=name,
        scratch_types=[pltpu.SemaphoreType.DMA((8,)), pltpu.SemaphoreType.DMA((8,))],
        compiler_params=pltpu.CompilerParams(collective_id=collective_id),
    )(local_slab)


def _comm_call(body, peers_of, out_shape, n_sems, operand, name, collective_id):
    sems = [pltpu.SemaphoreType.DMA((n_sems,)), pltpu.SemaphoreType.DMA((n_sems,))]

    def with_handshake(in_ref, out_ref, send, recv):
        x, y, c, _ = _position()
        peers = peers_of(x, y, c)
        barrier = pltpu.get_barrier_semaphore()
        for peer in peers:
            pl.semaphore_signal(barrier, inc=1, device_id=peer, device_id_type=MESH)
        pl.semaphore_wait(barrier, len(peers))
        body(in_ref, out_ref, send, recv)

    return pl.kernel(with_handshake, out_type=out_shape, mesh=plsc.ScalarSubcoreMesh(axis_name="sequencer", num_cores=1),
                     name=name, scratch_types=sems, compiler_params=pltpu.CompilerParams(collective_id=collective_id))(operand)


def _rs_swap_halves(partial, name, collective_id):
    half = partial.shape[1] // 2

    def body(p_ref, r_ref, send, recv):
        x, y, c, _ = _position()
        theirs = pl.ds(pl.multiple_of((1 - c) * half, 16), half)
        cp = pltpu.make_async_remote_copy(src_ref=p_ref.at[:, theirs, :], dst_ref=r_ref, send_sem=send.at[0],
                                          recv_sem=recv.at[0], device_id=(x, y, 1 - c), device_id_type=MESH)
        cp.start()
        cp.wait()

    return _comm_call(body, lambda x, y, c: [(x, y, 1 - c)], jax.ShapeDtypeStruct((N_CHIPS, half, D_MODEL), BF16), 1,
                      partial, name, collective_id)


def _gather_chip_sums(s_ref, sib_ref, sum_ref, o_ref, send, recv):
    x, y, c, chips = _position()
    me, sibling, here = 2 * x + y, (x, y, 1 - c), (x, y, c)
    half = s_ref.shape[0] // 2

    def rows(core):
        return pl.ds(pl.multiple_of(core * half, 8), half)

    def copy(k, src, dst, to):
        return pltpu.make_async_remote_copy(src_ref=src, dst_ref=dst, send_sem=send.at[k], recv_sem=recv.at[k], device_id=to,
                                            device_id_type=MESH)

    swap = copy(0, s_ref, sib_ref, sibling)
    keep = pltpu.make_async_copy(sum_ref, o_ref.at[me], send.at[7])
    sends = [copy(1 + k, sum_ref.at[rows(c), :], o_ref.at[me, rows(c), :], (*chip, c)) for k, chip in enumerate(chips)]

    def landed(chip, core):
        return o_ref.at[2 * chip[0] + chip[1], rows(core), :]

    def add_and_send():
        swap.wait_recv()
        sum_ref[...] = s_ref[...] + sib_ref[...]
        keep.start()
        for cp in sends:
            cp.start()

    def finish():
        passed = []
        for k, chip in enumerate(chips):
            copy(1 + k, landed(chip, c), landed(chip, c), here).wait_recv()
            fwd = copy(4 + k, landed(chip, c), landed(chip, c), sibling)
            fwd.start()
            passed.append(fwd)
        for k, chip in enumerate(chips):
            copy(4 + k, landed(chip, 1 - c), landed(chip, 1 - c), here).wait_recv()
        for cp in [swap] + sends + passed:
            cp.wait_send()
        keep.wait()

    return swap.start, add_and_send, finish


def _rs_add_halves(partial, other, core, name, after, small=None):
    half = other.shape[1]
    t = half // 2
    steps = half // t

    def body(core_ref, a_ref, b_ref, after_ref, *rest):
        del after_ref
        o_ref = rest[0] if small is None else rest[1]
        if small is not None:
            small_ref, _, t_ref, sib_ref, sum_ref, t_send, t_recv = rest
            swap, add_and_send, finish_tables = _gather_chip_sums(small_ref, sib_ref, sum_ref, t_ref, t_send, t_recv)
            step = pl.program_id(0) * steps + pl.program_id(1)
            pl.when(step == 0)(swap)
            pl.when(step == 1)(add_and_send)
        o_ref[...] = (a_ref[...].astype(F32) + b_ref[...].astype(F32)).astype(BF16)
        if small is not None:
            pl.when(step == N_CHIPS * steps - 1)(finish_tables)

    t_in, t_out, t_scratch = [], [], []
    if small is not None:
        t_in, t_out = [VMEM_WHOLE], [jax.ShapeDtypeStruct((N_CHIPS, *small.shape), F32)]
        t_scratch = [pltpu.VMEM(small.shape, F32)] * 2 + [pltpu.SemaphoreType.DMA((8,))] * 2
    res = pl.pallas_call(
        body, name=name,
        grid_spec=pltpu.PrefetchScalarGridSpec(
            num_scalar_prefetch=1, grid=(N_CHIPS, steps),
            in_specs=[pl.BlockSpec((1, t, D_MODEL), lambda j, i, core_ref: (j, core_ref[0] * steps + i, 0)),
                      pl.BlockSpec((1, t, D_MODEL), lambda j, i, core_ref: (j, i, 0)), ANY] + t_in,
            out_specs=[pl.BlockSpec((1, t, D_MODEL), lambda j, i, core_ref: (j, i, 0))] + [ANY] * len(t_out),
            scratch_shapes=t_scratch),
        out_shape=[jax.ShapeDtypeStruct((N_CHIPS, half, D_MODEL), BF16)] + t_out,
        compiler_params=_params(n_axes=2),
    )(core, partial, other, after, *([] if small is None else [small]))
    return res[0] if small is None else res


def _rs_exchange_chips(pre, name, collective_id):
    def body(s_ref, r_ref, send, recv):
        x, y, c, chips = _position()

        def copy(k, chunk, to):
            return pltpu.make_async_remote_copy(src_ref=s_ref.at[chunk], dst_ref=r_ref.at[k], send_sem=send.at[k],
                                                recv_sem=recv.at[k], device_id=to, device_id_type=MESH)

        sends = [copy(k, 2 * chip[0] + chip[1], (*chip, c)) for k, chip in enumerate(chips)]
        for cp in sends:
            cp.start()
        for cp in sends:
            cp.wait()

    return _comm_call(body, lambda x, y, c: [(1 - x, y, c), (x, 1 - y, c), (1 - x, 1 - y, c)],
                      jax.ShapeDtypeStruct((3, pre.shape[1], D_MODEL), BF16), 3, pre, name, collective_id)


def _gather_small(s_ref, t_ref, send, recv):
    x, y, c, chips = _position()
    sibling = (x, y, 1 - c)

    def slot(px, py, pc):
        return t_ref.at[4 * px + 2 * py + pc]

    def copy(k, block, to, src=None):
        return pltpu.make_async_remote_copy(src_ref=slot(*block) if src is None else src, dst_ref=slot(*block),
                                            send_sem=send.at[k], recv_sem=recv.at[k], device_id=to, device_id_type=MESH)

    own = pltpu.make_async_copy(s_ref, slot(x, y, c), send.at[7])
    first = [copy(0, (x, y, c), sibling, src=s_ref)]
    first += [copy(1 + k, (x, y, c), (*chip, c), src=s_ref) for k, chip in enumerate(chips)]

    def start():
        own.start()
        for cp in first:
            cp.start()

    def finish():
        passed = []
        for k, chip in enumerate(chips):
            copy(1 + k, (*chip, c), (x, y, c)).wait_recv()
            fwd = copy(4 + k, (*chip, c), sibling)
            fwd.start()
            passed.append(fwd)
        copy(0, sibling, (x, y, c)).wait_recv()
        for k, chip in enumerate(chips):
            copy(4 + k, (*chip, 1 - c), (x, y, c)).wait_recv()
        for cp in first + passed:
            cp.wait_send()
        own.wait()

    return start, finish


def _table_gather_parts(small):
    if small is None:
        return [], [], []
    return [VMEM_WHOLE], [jax.ShapeDtypeStruct((N_DEV, *small.shape), F32)], [pltpu.SemaphoreType.DMA((8,))] * 2


def _rs_sum_chips(pre, received, place, name, after, small=None):
    half = pre.shape[1]
    steps = 4 if half > 512 else 2
    t = half // steps
    assert t % 16 == 0 and t * steps == half

    def body(place_ref, own_ref, r_ref, after_ref, *rest):
        del place_ref, after_ref
        if small is None:
            o_ref, stage, kept_sems, send, recv = rest
        else:
            small_ref, o_ref, t_ref, stage, kept_sems, send, recv, t_send, t_recv = rest
            start_tables, finish_tables = _gather_small(small_ref, t_ref, t_send, t_recv)
            pl.when(pl.program_id(0) == 0)(start_tables)
        i = pl.program_id(0)
        x, y, c, _ = _position()

        def rows(core, step):
            return o_ref.at[pl.ds(pl.multiple_of((core * steps + step) * t, 8), t), :]

        def kept(step):
            return pltpu.make_async_copy(stage.at[step], rows(c, step), kept_sems.at[step])

        def sent(core, step):
            return pltpu.make_async_remote_copy(src_ref=stage.at[step], dst_ref=rows(core, step), send_sem=send.at[step],
                                                recv_sem=recv.at[step], device_id=(x, y, 1 - core), device_id_type=MESH)

        acc = own_ref[0].astype(F32)
        for k in range(3):
            acc = acc + r_ref[k].astype(F32)
        stage[i] = acc
        kept(i).start()
        sent(c, i).start()

        @pl.when(i == steps - 1)
        def _():
            if small is not None:
                finish_tables()
            for step in range(steps):
                kept(step).wait()
                sent(c, step).wait_send()
                sent(1 - c, step).wait_recv()

    t_in, t_out, t_scratch = _table_gather_parts(small)
    res = pl.pallas_call(
        body, name=name,
        grid_spec=pltpu.PrefetchScalarGridSpec(
            num_scalar_prefetch=1, grid=(steps,),
            in_specs=[pl.BlockSpec((1, t, D_MODEL), lambda i, place_ref: (place_ref[0], i, 0)),
                      pl.BlockSpec((3, t, D_MODEL), lambda i, place_ref: (0, i, 0)), ANY] + t_in,
            out_specs=[ANY] * (1 + len(t_out)),
            scratch_shapes=[pltpu.VMEM((steps, t, D_MODEL), F32)] + [pltpu.SemaphoreType.DMA((steps,))] * 3 + t_scratch),
        out_shape=[jax.ShapeDtypeStruct((2 * half, D_MODEL), F32)] + t_out, compiler_params=_params(),
    )(place, pre, received, after, *([] if small is None else [small]))
    return res[0] if small is None else res


def _adam_update(w, g, m, v):
    m_new = ADAM_B1 * m + (1.0 - ADAM_B1) * g
    v_new = ADAM_B2 * v + (1.0 - ADAM_B2) * (g * g)
    m_hat = m_new / (1.0 - ADAM_B1 ** ADAM_STEP)
    v_hat = v_new / (1.0 - ADAM_B2 ** ADAM_STEP)
    return -ADAM_LR * (m_hat / (jnp.sqrt(v_hat) + ADAM_EPS) + ADAM_WD * w), m_new, v_new


def _adamw(w, g_rows, row_off, m, v, name):
    rows, cols = w.shape
    t = rows // 4
    assert t * 4 == rows

    def body(w_ref, g_ref, m_ref, v_ref, go_ref, d_ref, nm_ref, nv_ref):
        g = g_ref[...]
        go_ref[...] = g
        d_ref[...], nm_ref[...], nv_ref[...] = _adam_update(w_ref[...], g, m_ref[...], v_ref[...])

    blk = pl.BlockSpec((t, cols), lambda i: (i, 0))
    assert row_off % 8 == 0 and t % 8 == 0
    g_blk = pl.BlockSpec((pl.Element(t), pl.Element(cols)), lambda i: (pl.multiple_of(row_off + i * t, 8), 0))
    shape = jax.ShapeDtypeStruct((rows, cols), F32)
    return pl.pallas_call(
        body, name=name, grid=(rows // t,), in_specs=[blk, g_blk, blk, blk], out_specs=[blk] * 4, out_shape=[shape] * 4,
        compiler_params=_params(),
    )(w, g_rows, m, v)


SMALL_PARAMS = [("g_attn", (1, D_MODEL), 8), ("g_q", (1, HEAD_DIM), None), ("g_k", (1, HEAD_DIM), None),
                ("sinks", (1, N_Q_HEADS), None), ("rel_bias", (N_Q_HEADS, N_BUCKETS), None), ("w_pool", (512, 128), None),
                ("pool_scale", (1, POOL_WIDTH), 4), ("g_ffn", (1, D_MODEL), 8), ("g_ple", (1, D_MODEL), 8)]


def _adamw_small(tables, pool_tables, wmv):
    n_par = len(SMALL_PARAMS)

    def body(*refs):
        t_ref, p_ref = refs[:2]
        ins = refs[2:2 + 3 * n_par]
        loss_ref = refs[2 + 3 * n_par]
        outs = refs[3 + 3 * n_par:-1]
        tot_ref = refs[-1]

        def in_order(ref):
            total = ref[0]
            for d in range(1, ref.shape[0]):
                total = total + ref[d]
            return total

        tot_ref[...] = in_order(t_ref)
        loss_ref[...] = tot_ref[pl.ds(SMALL["loss"], 1), 0:1]
        for i, (name, shape, split) in enumerate(SMALL_PARAMS):
            g_ref, d_ref, nm_ref, nv_ref = outs[4 * i:4 * i + 4]
            row = SMALL.get(name)
            if name == "w_pool":
                g_ref[...] = in_order(p_ref)
            elif split:
                for k in range(split):
                    g_ref[:, 128 * k:128 * k + 128] = tot_ref[pl.ds(row + k, 1), :]
            else:
                g_ref[...] = tot_ref[pl.ds(row, shape[0]), 0:shape[1]]
            w_ref, m_ref, v_ref = ins[3 * i:3 * i + 3]
            d_ref[...], nm_ref[...], nv_ref[...] = _adam_update(w_ref[...], g_ref[...], m_ref[...], v_ref[...])

    shapes = [jax.ShapeDtypeStruct((1, 1), F32)]
    for _, shape, _ in SMALL_PARAMS:
        shapes += [jax.ShapeDtypeStruct(shape, F32)] * 4
    flat = [a for triple in wmv for a in triple]
    res = pl.pallas_call(
        body, name="adamw_small", in_specs=[VMEM_WHOLE] * (2 + 3 * n_par), out_specs=[VMEM_WHOLE] * len(shapes),
        out_shape=shapes, scratch_shapes=[pltpu.VMEM((SMALL_ROWS, 128), F32)],
    )(tables, pool_tables, *flat)
    return res[0], [res[1 + 4 * i:5 + 4 * i] for i in range(n_par)]


def _pack_ple_proj(shard):
    return shard.reshape(4, 64, 256).transpose(1, 0, 2).reshape(64, D_MODEL)


class _Reduction:
    def __init__(self, tag, place, ids=(None, None)):
        self.tag, self.place, self.ids = tag, place, ids

    def start(self, partial):
        self.partial = partial
        self.other = _rs_swap_halves(partial, "rs_swap_" + self.tag, self.ids[0])
        return partial

    def middle(self, after, small=None):
        res = _rs_add_halves(self.partial, self.other, self.place[1:], "rs_add_" + self.tag, after, small)
        self.pre, self.tables = (res, None) if small is None else res
        self.received = _rs_exchange_chips(self.pre, "rs_exchange_" + self.tag, self.ids[1])
        return self.pre

    def finish(self, after, small=None):
        return _rs_sum_chips(self.pre, self.received, self.place, "rs_sum_" + self.tag, after, small)


def _local_grads(x2, p2, tgt, wts, g_attn_norm, g_q, g_k, attn_sinks, rel_bias, w_pool, pool_scale, g_ffn_norm, g_ple_norm,
                 reduce_a):
    w_early, w_late = wts
    w_in = w_out = w_early
    bucket = jnp.asarray(_bucket_table())
    gq = jnp.tile(g_q, (1, 2))
    gk = jnp.tile(g_k, (1, 2))
    wpool = w_pool[0].astype(BF16)
    sinks = attn_sinks[0]
    bias_st = _bias_build(rel_bias.T, bucket)

    hn1 = _first_norm(x2, g_attn_norm)
    zqk, u, kn, vb, qst = _attn_in(hn1, gq, gk, w_in)
    ost = _attn_fwd(qst, kn, vb, bias_st, sinks)
    pooled, mix, h1, hn2 = _mix_out(u, ost, x2, w_out, wpool, pool_scale, g_ffn_norm)
    loss_v, dgate, dup, act, dh2, hn3, dgl, dw_plp, dh1, dg_ffn, dg_ple = _ffn_ple(hn2, h1, p2, tgt, w_late, g_ffn_norm,
                                                                                      g_ple_norm)

    late0, late_rows = GATHER_PARTS[1][0], SLAB_ROWS - GATHER_PARTS[1][0]
    partial_a = None
    for names, lefts, right in ((("gateT", "upT"), [dgate, dup], hn2), (("down",), [act], dh2), (("plg",), [hn3], dgl)):
        partial_a = _dw(lefts, right, "dw_" + names[0], partial_a, late_rows, [SLAB[name][0] - late0 for name in names])
    dw_plp = dw_plp.reshape(4, 64, N_CHIPS, 256).transpose(2, 1, 0, 3).reshape(N_CHIPS, 64, D_MODEL)
    partial_a = reduce_a.start(lax.dynamic_update_slice(partial_a, dw_plp, (0, SLAB["plp"][0] - late0, 0)))
    dost, du, dw_pool, dscale, partial_b = _mix_out_bwd(dh1, w_out, pooled, wpool, pool_scale, mix, partial_a)
    pre_a = reduce_a.middle(du, dw_pool.reshape(512, 128))
    dqst, dk, dv, dbias, dsink_rows = _attn_bwd(qst, kn, vb, dost, bias_st, sinks, pre_a)
    dx, dg_attn, dgq, dgk, partial_b = _attn_in_bwd(dqst, zqk, dk, dv, du, x2, dh1, hn1, partial_b, w_in, g_attn_norm, gq, gk)

    small = _small_pack(dg_attn, dg_ffn, dg_ple, dscale, dgq, dgk, dbias, dsink_rows, loss_v)
    return dx, partial_b, small


def kernel(x, p, w_in, w_out, g_attn_norm, g_q, g_k, attn_sinks, rel_bias, w_pool, pool_scale, g_ffn_norm, w_gate, w_up, w_down, g_ple_norm, w_ple_gate, w_ple_proj, loss_target, m_w_in, m_w_out, m_g_attn_norm, m_g_q, m_g_k, m_attn_sinks, m_rel_bias, m_w_pool, m_pool_scale, m_g_ffn_norm, m_w_gate, m_w_up, m_w_down, m_g_ple_norm, m_w_ple_gate, m_w_ple_proj, v_w_in, v_w_out, v_g_attn_norm, v_g_q, v_g_k, v_attn_sinks, v_rel_bias, v_w_pool, v_pool_scale, v_g_ffn_norm, v_w_gate, v_w_up, v_w_down, v_g_ple_norm, v_w_ple_gate, v_w_ple_proj):
    core = lax.axis_index("c").astype(jnp.int32).reshape(1)
    me = (2 * lax.axis_index("x") + lax.axis_index("y")).astype(jnp.int32).reshape(1)

    local_parts = [jnp.concatenate(pieces, axis=0).astype(BF16) for pieces in (
        [w_in[0].T, w_out[0]], [w_gate[0].T, w_up[0].T, w_down[0], w_ple_gate[0], _pack_ple_proj(w_ple_proj[0])])]
    wts = [(_ag_weights(local, 0, local.shape[0], name, collective_id), local, me)
           for local, name, collective_id in zip(local_parts, ("ag_early", "ag_late"), (1, 2))]

    place = jnp.concatenate([me, core])
    reduce_a = _Reduction("a", place, ids=(3, 4))
    dx, partial_b, small = _local_grads(x[0], p[0, 0], loss_target[0], wts, g_attn_norm, g_q, g_k, attn_sinks, rel_bias,
                                        w_pool, pool_scale, g_ffn_norm, g_ple_norm, reduce_a)
    reduce_b = _Reduction("b", place, ids=(6, 7))
    reduce_b.start(partial_b)
    grads_a, small_all = reduce_a.finish(partial_b, small)
    reduce_b.middle(grads_a)

    late0 = GATHER_PARTS[1][0]

    def rows(name):
        return grads_a, SLAB[name][0] - late0

    plp_rows = grads_a[SLAB["plp"][0] - late0:]
    big = {
        "w_gate": (w_gate, m_w_gate, v_w_gate, rows("gateT"), True),
        "w_up": (w_up, m_w_up, v_w_up, rows("upT"), True),
        "w_down": (w_down, m_w_down, v_w_down, rows("down"), False),
        "w_ple_gate": (w_ple_gate, m_w_ple_gate, v_w_ple_gate, rows("plg"), False),
        "w_ple_proj": (w_ple_proj, m_w_ple_proj, v_w_ple_proj,
                       (plp_rows.reshape(64, 4, 256).transpose(1, 0, 2).reshape(PLE_DIM, PLE_DIM), 0), False),
        "w_out": (w_out, m_w_out, v_w_out, None, False),
        "w_in": (w_in, m_w_in, v_w_in, None, True),
    }
    small_params = {
        "g_attn_norm": (g_attn_norm, m_g_attn_norm, v_g_attn_norm), "g_q": (g_q, m_g_q, v_g_q), "g_k": (g_k, m_g_k, v_g_k),
        "attn_sinks": (attn_sinks, m_attn_sinks, v_attn_sinks), "rel_bias": (rel_bias.T, m_rel_bias.T, v_rel_bias.T),
        "w_pool": tuple(a.reshape(512, 128) for a in (w_pool, m_w_pool, v_w_pool)),
        "pool_scale": (pool_scale, m_pool_scale, v_pool_scale), "g_ffn_norm": (g_ffn_norm, m_g_ffn_norm, v_g_ffn_norm),
        "g_ple_norm": (g_ple_norm, m_g_ple_norm, v_g_ple_norm),
    }

    grads, deltas, new_ms, new_vs = {}, {}, {}, {}
    out = grads_b = None
    for name, (w, m, v, g_src, transposed) in big.items():
        if g_src is None:
            if grads_b is None:
                grads_b = reduce_b.finish(out[-1])
            g_src = (grads_b, SLAB["out" if name == "w_out" else "inT"][0])
        view = (lambda a: a.T) if transposed else (lambda a: a)
        out = _adamw(view(w[0]), *g_src, view(m[0]), view(v[0]), "adamw_" + name)
        grads[name], deltas[name], new_ms[name], new_vs[name] = (view(a)[None] for a in out)

    loss, small_out = _adamw_small(small_all, reduce_a.tables, list(small_params.values()))
    for name, (g2, d, nm, nv) in zip(small_params, small_out):
        restore = {"w_pool": lambda a: a.reshape(w_pool.shape), "rel_bias": lambda a: a.T}.get(name, lambda a: a)
        grads[name], deltas[name], new_ms[name], new_vs[name] = (restore(a) for a in (g2, d, nm, nv))

    order = ["w_in", "w_out", "g_attn_norm", "g_q", "g_k", "attn_sinks", "rel_bias", "w_pool", "pool_scale", "g_ffn_norm",
             "w_gate", "w_up", "w_down", "g_ple_norm", "w_ple_gate", "w_ple_proj"]
    return (loss.reshape(()), dx[None], *[grads[n] for n in order], *[deltas[n] for n in order],
            *[new_ms[n] for n in order], *[new_vs[n] for n in order])
```

```python
import numpy as np
import jax
import jax.numpy as jnp
from jax import lax
from jax.experimental import pallas as pl
from jax.experimental.pallas import tpu as pltpu
from jax.experimental.pallas import tpu_sc as plsc

F32 = jnp.float32
BF16 = jnp.bfloat16
MESH = pl.DeviceIdType.MESH

D_MODEL = 1024
HEAD_DIM = 64
N_Q_HEADS = 8
ATTN_WIDTH = 512
POOL_WIDTH = 512
IN_WIDTH = 1280
D_FF = 2816
PLE_DIM = 256
FF_CHUNK = 1408
N_FF_CHUNKS = D_FF // FF_CHUNK
BLOCK = 128
N_BUCKETS = 32
MAX_DISTANCE = 128
EPS = 1e-6
NEG = -1e30
N_CHIPS = 4
N_DEV = 8

ADAM_LR = 0.001
ADAM_B1 = 0.9
ADAM_B2 = 0.999
ADAM_EPS = 1e-08
ADAM_WD = 0.01
ADAM_STEP = 10

SLAB = {"inT": (0, 320), "out": (320, 256), "gateT": (576, 704), "upT": (1280, 704), "down": (1984, 704),
        "plg": (2688, 256), "plp": (2944, 64)}
SLAB_ROWS = 3008
GATHER_PARTS = ((0, 576), (576, SLAB_ROWS))
POOL_HALO = 24

SMALL = {"g_attn": 0, "g_ffn": 8, "g_ple": 16, "pool_scale": 24, "g_q": 28, "g_k": 29, "sinks": 30, "loss": 31,
         "rel_bias": 32}
SMALL_ROWS = 64

VMEM_LIMIT_BIG = 60 * 1024 * 1024
VMEM_LIMIT = 48 * 1024 * 1024


def _params(vmem=VMEM_LIMIT, n_axes=1):
    return pltpu.CompilerParams(dimension_semantics=("arbitrary",) * n_axes, vmem_limit_bytes=vmem)


def _dot(a, b, ca, cb):
    return lax.dot_general(a, b, (((ca,), (cb,)), ((), ())), preferred_element_type=F32)


def _full(shape):
    return pl.BlockSpec(shape, lambda i: (0,) * len(shape))


ANY = pl.BlockSpec(memory_space=pl.ANY)
VMEM_WHOLE = pl.BlockSpec(memory_space=pltpu.VMEM)


W_SPECS = [ANY, ANY, pl.BlockSpec(memory_space=pltpu.SMEM)]


def _load_rows(w_refs, name, dst_ref, sems):
    slab_ref, local_ref, me_ref = w_refs
    off, rows = SLAB[name]
    slab_off = off - max(start for start, _ in GATHER_PARTS if start <= off)
    me = me_ref[0]
    for phase in ("start", "wait"):
        for j in range(N_CHIPS):
            dst = dst_ref.at[pl.ds(j * rows, rows), :]
            theirs = pltpu.make_async_copy(slab_ref.at[j, pl.ds(slab_off, rows), :], dst, sems.at[j])
            own = pltpu.make_async_copy(local_ref.at[pl.ds(slab_off, rows), :], dst, sems.at[j])

            @pl.when(me == j)
            def _():
                getattr(own, phase)()

            @pl.when(me != j)
            def _():
                getattr(theirs, phase)()


def _rms_fwd(x, g):
    r = lax.rsqrt(jnp.mean(x * x, axis=-1, keepdims=True) + EPS)
    return x * r * g


def _rms_bwd(x, g, dy):
    r = lax.rsqrt(jnp.mean(x * x, axis=-1, keepdims=True) + EPS)
    xn = x * r
    dyg = dy * g
    dx = r * (dyg - xn * jnp.mean(dyg * xn, axis=-1, keepdims=True))
    return dx, jnp.sum(dy * xn, axis=0, keepdims=True)


def _half_sum(v, lo):
    s_lo = jnp.sum(jnp.where(lo, v, 0.0), axis=-1, keepdims=True)
    s_hi = jnp.sum(jnp.where(lo, 0.0, v), axis=-1, keepdims=True)
    return jnp.where(lo, s_lo, s_hi)


def _half_sum_mxu(v):
    upper = lax.broadcasted_iota(jnp.int32, (128, 128), 0) < 64
    left = lax.broadcasted_iota(jnp.int32, (128, 128), 1) < 64
    ones = jnp.where(upper == left, 1.0, 0.0).astype(BF16)
    high = v.astype(BF16)
    low = (v - high.astype(F32)).astype(BF16)
    return _dot(high, ones, 1, 0) + _dot(low, ones, 1, 0)


def _pair_norm(zp, g, lo):
    r = lax.rsqrt(_half_sum(zp * zp, lo) * (1.0 / HEAD_DIM) + EPS)
    return zp * r * g


def _pair_norm_bwd(zp, g, dy):
    r = lax.rsqrt(_half_sum_mxu(zp * zp) * (1.0 / HEAD_DIM) + EPS)
    xn = zp * r
    dyg = dy * g
    dx = r * (dyg - xn * (_half_sum_mxu(dyg * xn) * (1.0 / HEAD_DIM)))
    return dx, jnp.sum(dy * xn, axis=0, keepdims=True)


def _pack_heads(pairs, lo):
    packed = [None] * 4
    for m in range(2):
        a, b = pairs[m], pairs[m + 2]
        packed[2 * m] = jnp.where(lo, a, pltpu.roll(b, 64, axis=1))
        packed[2 * m + 1] = jnp.where(lo, pltpu.roll(a, 64, axis=1), b)
    return packed


def _unpack_heads(packed, lo):
    pairs = [None] * 4
    for m in range(2):
        a, b = packed[2 * m], packed[2 * m + 1]
        pairs[m] = jnp.where(lo, a, pltpu.roll(b, 64, axis=1))
        pairs[m + 2] = jnp.where(lo, pltpu.roll(a, 64, axis=1), b)
    return pairs


def _expand_heads(packed):
    flat = packed.reshape(4 * BLOCK, 128)
    lo = lax.broadcasted_iota(jnp.int32, flat.shape, 1) < 64
    zero = jnp.zeros_like(flat)
    return jnp.concatenate([jnp.where(lo, flat, zero), jnp.where(lo, zero, flat)], axis=0)


def _fold_heads(stacked):
    half = 4 * BLOCK
    lo = lax.broadcasted_iota(jnp.int32, (half, 128), 1) < 64
    return jnp.where(lo, stacked[:half], stacked[half:]).reshape(4, BLOCK, 128)


def _sigmoid(v):
    return 1.0 / (1.0 + jnp.exp(-v))


def _pool_counts(tile, n_rows):
    t1 = tile * n_rows + lax.broadcasted_iota(jnp.int32, (n_rows, POOL_WIDTH), 0) + 1
    lane = lax.broadcasted_iota(jnp.int32, (n_rows, POOL_WIDTH), 1)
    win = jnp.where(lane < 128, 2, jnp.where(lane < 256, 4, jnp.where(lane < 384, 8, 16)))
    return jnp.minimum(t1, win).astype(F32)


def _first_norm(x2, g_attn):
    s_len = x2.shape[0]
    t = 512

    def body(x_ref, g_ref, hn_ref):
        hn_ref[...] = _rms_fwd(x_ref[...], g_ref[...]).astype(BF16)

    row = pl.BlockSpec((t, D_MODEL), lambda i: (i, 0))
    return pl.pallas_call(
        body, name="first_norm", grid=(s_len // t,), in_specs=[row, _full((1, D_MODEL))], out_specs=row,
        out_shape=jax.ShapeDtypeStruct((s_len, D_MODEL), BF16), compiler_params=_params(),
    )(x2, g_attn)


def _attn_in(hn1, gq, gk, wts):
    s_len = hn1.shape[0]
    t = 512

    def body(hn_ref, gq_ref, gk_ref, sl_ref, lo_ref, me_ref, zqk_ref, u_ref, kn_ref, v_ref, qst_ref, w_ref, sems):
        @pl.when(pl.program_id(0) == 0)
        def _():
            _load_rows((sl_ref, lo_ref, me_ref), "inT", w_ref, sems)

        z = _dot(hn_ref[...], w_ref[...], 1, 1)
        zqk_ref[...] = z[:, :640]
        u_ref[...] = z[:, 768:]
        v_ref[...] = z[:, 640:768].astype(BF16)
        lo = lax.broadcasted_iota(jnp.int32, (t, 128), 1) < 64
        kn_ref[...] = _pair_norm(z[:, 512:640], gk_ref[...], lo).astype(BF16)
        pairs = [_pair_norm(z[:, 128 * p:128 * p + 128], gq_ref[...], lo) for p in range(4)]
        for j, entry in enumerate(_pack_heads(pairs, lo)):
            qst_ref[j] = entry.astype(BF16)

    row = lambda w: pl.BlockSpec((t, w), lambda i: (i, 0))
    return pl.pallas_call(
        body, name="attn_in", grid=(s_len // t,),
        in_specs=[row(D_MODEL), _full((1, 128)), _full((1, 128))] + W_SPECS,
        out_specs=[row(640), row(POOL_WIDTH), row(128), row(128), pl.BlockSpec((4, t, 128), lambda i: (0, i, 0))],
        out_shape=[jax.ShapeDtypeStruct((s_len, 640), F32), jax.ShapeDtypeStruct((s_len, POOL_WIDTH), F32),
                   jax.ShapeDtypeStruct((s_len, 128), BF16), jax.ShapeDtypeStruct((s_len, 128), BF16),
                   jax.ShapeDtypeStruct((4, s_len, 128), BF16)],
        scratch_shapes=[pltpu.VMEM((IN_WIDTH, D_MODEL), BF16), pltpu.SemaphoreType.DMA((N_CHIPS,))],
        compiler_params=_params(),
    )(hn1, gq, gk, *wts)


def _bucket_table():
    i_idx = np.arange(BLOCK)[:, None]
    j_idx = np.arange(2 * BLOCK)[None, :]
    d = BLOCK + i_idx - j_idx
    n = np.maximum(d, 0)
    max_exact = N_BUCKETS // 2
    nf = np.maximum(n, 1).astype(np.float64)
    large = max_exact + (np.log(nf / max_exact) / np.log(MAX_DISTANCE / max_exact) * (N_BUCKETS - max_exact)).astype(np.int64)
    large = np.minimum(large, N_BUCKETS - 1)
    bucket = np.where(n < max_exact, n, large)
    return np.where((d >= 0) & (d < BLOCK), bucket, -1).astype(np.int32)


def _bias_build(rel_bias_t, bucket):
    def body(rb_ref, bucket_ref, out_ref):
        bk = bucket_ref[...]
        for h in range(N_Q_HEADS):
            acc = jnp.full((BLOCK, 2 * BLOCK), NEG, F32)
            for b in range(N_BUCKETS):
                acc = jnp.where(bk == b, rb_ref[h, b], acc)
            out_ref[0, pl.ds(h * BLOCK, BLOCK), :] = acc
            out_ref[1, pl.ds(h * BLOCK, BLOCK), :] = acc
            out_ref[1, pl.ds(h * BLOCK, BLOCK), 0:BLOCK] = jnp.full((BLOCK, BLOCK), NEG, F32)

    return pl.pallas_call(
        body, name="bias_build",
        in_specs=[pl.BlockSpec(memory_space=pltpu.SMEM), VMEM_WHOLE], out_specs=VMEM_WHOLE,
        out_shape=jax.ShapeDtypeStruct((2, N_Q_HEADS * BLOCK, 2 * BLOCK), F32),
    )(rel_bias_t, bucket)


def _head_softmax(s_ref, bias_ref, sink_ref, h):
    rows = pl.ds(pl.multiple_of(h * BLOCK, BLOCK), BLOCK)
    s = s_ref[rows, :] * (HEAD_DIM ** -0.5) + bias_ref[rows, :]
    sink = sink_ref[h]
    m = jnp.maximum(jnp.max(s, axis=-1, keepdims=True), sink)
    p = jnp.exp(s - m)
    e_sink = jnp.exp(sink - m)
    inv = 1.0 / (jnp.sum(p, axis=-1, keepdims=True) + e_sink)
    return rows, p * inv, e_sink * inv


ATTN_STEP_BLOCKS = 4
BAND = (N_Q_HEADS * BLOCK, 2 * BLOCK)


def _attn_specs():
    nb = ATTN_STEP_BLOCKS
    stacked = pl.BlockSpec((4, nb * BLOCK, 128), lambda i: (0, i, 0))
    kv = [pl.BlockSpec((BLOCK, 128), lambda i: (jnp.maximum(nb * i - 1, 0), 0)), pl.BlockSpec((nb * BLOCK, 128), lambda i: (i, 0))]
    consts = [_full((2,) + BAND), pl.BlockSpec(memory_space=pltpu.SMEM)]
    return stacked, kv, consts


def _step_blocks(i, kp_ref, kc_ref, vp_ref, vc_ref, bias_ref):
    blocks = []
    for b in range(ATTN_STEP_BLOCKS):
        if b == 0:
            k2 = jnp.concatenate([kp_ref[...], kc_ref[pl.ds(0, BLOCK), :]], axis=0)
            v2 = jnp.concatenate([vp_ref[...], vc_ref[pl.ds(0, BLOCK), :]], axis=0)
            bias = bias_ref.at[jnp.where(i == 0, 1, 0)]
        else:
            k2, v2, bias = kc_ref[pl.ds((b - 1) * BLOCK, 2 * BLOCK), :], vc_ref[pl.ds((b - 1) * BLOCK, 2 * BLOCK), :], bias_ref.at[0]
        blocks.append((pl.ds(b * BLOCK, BLOCK), k2, v2, bias))
    return blocks


def _attn_fwd(qst, kn, vb, bias_st, sinks):
    s_len = kn.shape[0]

    def body(q_ref, kp_ref, kc_ref, vp_ref, vc_ref, bias_ref, sink_ref, o_ref, s_ref, p_ref):
        for b, (rows, k2, v2, bias) in enumerate(_step_blocks(pl.program_id(0), kp_ref, kc_ref, vp_ref, vc_ref, bias_ref)):
            s_b, p_b = s_ref.at[b], p_ref.at[b]
            s_b[...] = _dot(_expand_heads(q_ref[:, rows, :]), k2, 1, 1)

            def head(h, carry):
                head_rows, probs, _ = _head_softmax(s_b, bias, sink_ref, h)
                p_b[head_rows, :] = probs.astype(BF16)
                return carry

            lax.fori_loop(0, N_Q_HEADS, head, 0, unroll=True)
            o_ref[:, rows, :] = _fold_heads(_dot(p_b[...], v2, 1, 0)).astype(BF16)

    stacked, kv, consts = _attn_specs()
    return pl.pallas_call(
        body, name="attn_fwd", grid=(s_len // (ATTN_STEP_BLOCKS * BLOCK),),
        in_specs=[stacked] + kv + kv + consts, out_specs=stacked,
        out_shape=jax.ShapeDtypeStruct((4, s_len, 128), BF16),
        scratch_shapes=[pltpu.VMEM((ATTN_STEP_BLOCKS,) + BAND, F32), pltpu.VMEM((ATTN_STEP_BLOCKS,) + BAND, BF16)],
        compiler_params=_params(),
    )(qst, kn, kn, vb, vb, bias_st, sinks)


def _mix_out(u, ost, x2, wts, wpool, pool_scale, g_ffn):
    s_len = x2.shape[0]
    t = 512
    n = t + 16

    def body(u_ref, o_ref, x_ref, sl_ref, lo_ref, me_ref, wp_ref, sc_ref, g_ref, pooled_ref, mix_ref, h1_ref, hn_ref,
             w_ref, ext_ref, st_ref, sems):
        i = pl.program_id(0)

        @pl.when(i == 0)
        def _():
            _load_rows((sl_ref, lo_ref, me_ref), "out", w_ref, sems)
            ext_ref[...] = jnp.zeros_like(ext_ref)
            st_ref[...] = jnp.zeros_like(st_ref)

        u_tile = u_ref[...]
        ext_ref[pl.ds(POOL_HALO, t), :] = u_tile
        st_ref[pl.ds(8, n), :] = ext_ref[pl.ds(8, n), :] + ext_ref[pl.ds(7, n), :]
        st_ref[pl.ds(8, n), 128:] = st_ref[pl.ds(8, n), 128:] + st_ref[pl.ds(6, n), 128:]
        st_ref[pl.ds(8, n), 256:] = st_ref[pl.ds(8, n), 256:] + st_ref[pl.ds(4, n), 256:]
        st_ref[pl.ds(8, n), 384:] = st_ref[pl.ds(8, n), 384:] + st_ref[pl.ds(0, n), 384:]
        ext_ref[pl.ds(0, POOL_HALO), :] = ext_ref[pl.ds(t, POOL_HALO), :]
        pooled = (st_ref[pl.ds(POOL_HALO, t), :] / _pool_counts(i, t) - u_tile).astype(BF16)
        pooled_ref[...] = pooled
        for g in range(4):
            cols = slice(128 * g, 128 * g + 128)
            y = _dot(pooled[:, cols], wp_ref[g], 1, 0) * sc_ref[:, cols]
            mix_ref[:, ATTN_WIDTH + 128 * g:ATTN_WIDTH + 128 * g + 128] = y.astype(BF16)
        lo = lax.broadcasted_iota(jnp.int32, (t, 128), 1) < 64
        for p, pair in enumerate(_unpack_heads([o_ref[j].astype(F32) for j in range(4)], lo)):
            mix_ref[:, 128 * p:128 * p + 128] = pair.astype(BF16)
        h1 = x_ref[...] + _dot(mix_ref[...], w_ref[...], 1, 0)
        h1_ref[...] = h1
        hn_ref[...] = _rms_fwd(h1, g_ref[...]).astype(BF16)

    row = lambda w: pl.BlockSpec((t, w), lambda i: (i, 0))
    return pl.pallas_call(
        body, name="mix_out", grid=(s_len // t,),
        in_specs=[row(POOL_WIDTH), pl.BlockSpec((4, t, 128), lambda i: (0, i, 0)), row(D_MODEL)] + W_SPECS
        + [_full((4, 128, 128)), _full((1, POOL_WIDTH)), _full((1, D_MODEL))],
        out_specs=[row(POOL_WIDTH), row(D_MODEL), row(D_MODEL), row(D_MODEL)],
        out_shape=[jax.ShapeDtypeStruct((s_len, POOL_WIDTH), BF16), jax.ShapeDtypeStruct((s_len, D_MODEL), BF16),
                   jax.ShapeDtypeStruct((s_len, D_MODEL), F32), jax.ShapeDtypeStruct((s_len, D_MODEL), BF16)],
        scratch_shapes=[pltpu.VMEM((D_MODEL, D_MODEL), BF16), pltpu.VMEM((t + POOL_HALO, POOL_WIDTH), F32),
                        pltpu.VMEM((t + POOL_HALO, POOL_WIDTH), F32), pltpu.SemaphoreType.DMA((N_CHIPS,))],
        compiler_params=_params(),
    )(u, ost, x2, *wts, wpool, pool_scale, g_ffn)


def _ffn_ple(hn2, h1, p2, tgt, wts, g_ffn, g_ple):
    s_len = h1.shape[0]
    t = 256
    n_tiles = s_len // t

    def body(hn_ref, h1_ref, p_ref, tgt_ref, sl_ref, lo_ref, me_ref, gf_ref, gp_ref,
             loss_ref, dgate_ref, dup_ref, act_ref, dh2b_ref, hn3_ref, dgl_ref, dwp_ref, dh1_ref, dgf_ref, dgp_ref,
             wg_ref, wu_ref, wd_ref, wl_ref, wp_ref, packed_ref, gate_s, up_s, loss_acc, dwp_acc, sems):
        i = pl.program_id(0)

        @pl.when(i == 0)
        def _():
            w_refs = (sl_ref, lo_ref, me_ref)
            _load_rows(w_refs, "gateT", wg_ref, sems)
            _load_rows(w_refs, "upT", wu_ref, sems)
            _load_rows(w_refs, "down", wd_ref, sems)
            _load_rows(w_refs, "plg", wl_ref, sems)
            _load_rows(w_refs, "plp", packed_ref, sems)
            for j in range(N_CHIPS):
                for q in range(4):
                    wp_ref[pl.ds(64 * q, 64), 256 * j:256 * j + 256] = packed_ref[pl.ds(64 * j, 64), 256 * q:256 * q + 256]
            loss_acc[...] = jnp.zeros_like(loss_acc)
            dwp_acc[...] = jnp.zeros_like(dwp_acc)
            dgf_ref[...] = jnp.zeros_like(dgf_ref)
            dgp_ref[...] = jnp.zeros_like(dgp_ref)

        hn = hn_ref[...]
        h1v = h1_ref[...]
        chunks = [slice(ch * FF_CHUNK, (ch + 1) * FF_CHUNK) for ch in range(N_FF_CHUNKS)]
        gate = _dot(hn, wg_ref[...], 1, 1)
        up = _dot(hn, wu_ref[...], 1, 1)
        gate_s[...] = gate
        up_s[...] = up
        act = (gate * _sigmoid(gate) * up).astype(BF16)
        for ch, cols in enumerate(chunks):
            act_ref[ch] = act[:, cols]
        h2 = h1v + _dot(act, wd_ref[...], 1, 0)
        gp = gp_ref[...]
        hn3 = _rms_fwd(h2, gp).astype(BF16)
        hn3_ref[...] = hn3
        gate2 = _sigmoid(_dot(hn3, wl_ref[...], 1, 0))
        p_tile = p_ref[...].astype(BF16)
        pp = _dot(p_tile, wp_ref[...], 1, 0)
        err = h2 + gate2 * pp - tgt_ref[...]
        loss_acc[...] += jnp.sum(err * err, axis=0, keepdims=True)
        dy = err * (1.0 / D_MODEL)
        dwp_acc[...] += _dot(p_tile, (dy * gate2).astype(BF16), 0, 0)
        dgl = (dy * pp * gate2 * (1.0 - gate2)).astype(BF16)
        dgl_ref[...] = dgl
        dx3, dg3 = _rms_bwd(h2, gp, _dot(dgl, wl_ref[...], 1, 1))
        dh2 = dy + dx3
        dgp_ref[...] += dg3
        dh2b = dh2.astype(BF16)
        dh2b_ref[...] = dh2b
        dact = _dot(dh2b, wd_ref[...], 1, 1)
        gate_v = gate_s[...]
        up_v = up_s[...]
        sg = _sigmoid(gate_v)
        dup = (dact * (gate_v * sg)).astype(BF16)
        dgate = (dact * up_v * (sg * (1.0 + gate_v * (1.0 - sg)))).astype(BF16)
        for ch, cols in enumerate(chunks):
            dup_ref[ch] = dup[:, cols]
            dgate_ref[ch] = dgate[:, cols]
        dhn = _dot(dgate, wg_ref[...], 1, 0) + _dot(dup, wu_ref[...], 1, 0)
        dx, dg = _rms_bwd(h1v, gf_ref[...], dhn)
        dh1_ref[...] = dh2 + dx
        dgf_ref[...] += dg

        @pl.when(i == n_tiles - 1)
        def _():
            total = jnp.sum(loss_acc[...], axis=-1, keepdims=True) * (0.5 / D_MODEL)
            loss_ref[...] = jnp.broadcast_to(total, loss_ref.shape)
            dwp_ref[...] = dwp_acc[...].astype(BF16)

    row = lambda w: pl.BlockSpec((t, w), lambda i: (i, 0))
    chunked = pl.BlockSpec((N_FF_CHUNKS, t, FF_CHUNK), lambda i: (0, i, 0))
    vec = _full((1, D_MODEL))
    act_shape = jax.ShapeDtypeStruct((N_FF_CHUNKS, s_len, FF_CHUNK), BF16)
    tok = lambda dtype: jax.ShapeDtypeStruct((s_len, D_MODEL), dtype)
    return pl.pallas_call(
        body, name="ffn_ple", grid=(n_tiles,),
        in_specs=[row(D_MODEL), row(D_MODEL), row(PLE_DIM), row(D_MODEL)] + W_SPECS + [vec, vec],
        out_specs=[_full((1, 128)), chunked, chunked, chunked] + [row(D_MODEL)] * 3 + [_full((PLE_DIM, D_MODEL)), row(D_MODEL),
                                                                                       vec, vec],
        out_shape=[jax.ShapeDtypeStruct((1, 128), F32), act_shape, act_shape, act_shape, tok(BF16), tok(BF16), tok(BF16),
                   jax.ShapeDtypeStruct((PLE_DIM, D_MODEL), BF16), tok(F32), jax.ShapeDtypeStruct((1, D_MODEL), F32),
                   jax.ShapeDtypeStruct((1, D_MODEL), F32)],
        scratch_shapes=[pltpu.VMEM((D_FF, D_MODEL), BF16)] * 3
        + [pltpu.VMEM((D_MODEL, D_MODEL), BF16), pltpu.VMEM((PLE_DIM, D_MODEL), BF16), pltpu.VMEM((PLE_DIM, D_MODEL), BF16),
           pltpu.VMEM((t, D_FF), F32), pltpu.VMEM((t, D_FF), F32), pltpu.VMEM((1, D_MODEL), F32),
           pltpu.VMEM((PLE_DIM, D_MODEL), F32), pltpu.SemaphoreType.DMA((N_CHIPS,))],
        compiler_params=_params(VMEM_LIMIT_BIG),
    )(hn2, h1, p2, tgt, *wts, g_ffn, g_ple)


def _accumulate_tn(acc_ref, a, b, first):
    @pl.when(first)
    def _():
        acc_ref[...] = _dot(a, b, 0, 0)

    @pl.when(jnp.logical_not(first))
    def _():
        acc_ref[...] += _dot(a, b, 0, 0)


def _flush_chunks(acc_ref, stage_ref, slab_ref, name, sems):
    stage_ref[...] = acc_ref[...].astype(BF16)
    off, rows = SLAB[name]
    copies = [pltpu.make_async_copy(stage_ref.at[pl.ds(j * rows, rows), :], slab_ref.at[j, pl.ds(off, rows), :], sems.at[j])
              for j in range(N_CHIPS)]
    for cp in copies:
        cp.start()
    for cp in copies:
        cp.wait()


def _mix_out_bwd(dh1, wts, pooled, wpool, pool_scale, mix, after):
    s_len = dh1.shape[0]
    t = 512
    n = t + 16
    n_tiles = s_len // t
    early_rows = GATHER_PARTS[0][1]

    def body(dh1_ref, sl_ref, lo_ref, me_ref, pooled_ref, wp_ref, sc_ref, mix_ref, after_ref, dost_ref, du_ref, dwp_ref,
             dsc_ref, slab_ref, w_ref, ext_ref, st_ref, acc_ref, stage_ref, sems):
        del after_ref
        i = pl.program_id(0)

        @pl.when(i == 0)
        def _():
            _load_rows((sl_ref, lo_ref, me_ref), "out", w_ref, sems)
            ext_ref[...] = jnp.zeros_like(ext_ref)
            st_ref[...] = jnp.zeros_like(st_ref)
            dsc_ref[...] = jnp.zeros_like(dsc_ref)
            dwp_ref[...] = jnp.zeros_like(dwp_ref)
            acc_ref[...] = jnp.zeros_like(acc_ref)

        dh1b = dh1_ref[...].astype(BF16)
        acc_ref[...] += _dot(mix_ref[...], dh1b, 0, 0)
        dmix = _dot(dh1b, w_ref[...], 1, 1)
        lo = lax.broadcasted_iota(jnp.int32, (t, 128), 1) < 64
        for j, entry in enumerate(_pack_heads([dmix[:, 128 * p:128 * p + 128] for p in range(4)], lo)):
            dost_ref[j] = entry.astype(BF16)
        pooled_v = pooled_ref[...]
        counts = _pool_counts(n_tiles - 1 - i, t)
        for g in range(4):
            cols = slice(128 * g, 128 * g + 128)
            dm = dmix[:, ATTN_WIDTH + 128 * g:ATTN_WIDTH + 128 * g + 128]
            ypre = _dot(pooled_v[:, cols], wp_ref[g], 1, 0)
            dsc_ref[:, cols] += jnp.sum(ypre * dm, axis=0, keepdims=True)
            dyp = (dm * sc_ref[:, cols]).astype(BF16)
            dwp_ref[g] += _dot(pooled_v[:, cols], dyp, 0, 0)
            dpooled = _dot(dyp, wp_ref[g], 1, 1)
            du_ref[:, cols] = -dpooled
            ext_ref[pl.ds(0, t), cols] = dpooled / counts[:, cols]
        st_ref[pl.ds(0, n), :] = ext_ref[pl.ds(0, n), :] + ext_ref[pl.ds(1, n), :]
        st_ref[pl.ds(0, n), 128:] = st_ref[pl.ds(0, n), 128:] + st_ref[pl.ds(2, n), 128:]
        st_ref[pl.ds(0, n), 256:] = st_ref[pl.ds(0, n), 256:] + st_ref[pl.ds(4, n), 256:]
        st_ref[pl.ds(0, n), 384:] = st_ref[pl.ds(0, n), 384:] + st_ref[pl.ds(8, n), 384:]
        ext_ref[pl.ds(t, POOL_HALO), :] = ext_ref[pl.ds(0, POOL_HALO), :]
        du_ref[...] += st_ref[pl.ds(0, t), :]

        @pl.when(i == n_tiles - 1)
        def _():
            _flush_chunks(acc_ref, stage_ref, slab_ref, "out", sems)

    rev = lambda w: pl.BlockSpec((t, w), lambda i: (n_tiles - 1 - i, 0))
    return pl.pallas_call(
        body, name="mix_out_bwd", grid=(n_tiles,),
        in_specs=[rev(D_MODEL)] + W_SPECS + [rev(POOL_WIDTH), _full((4, 128, 128)), _full((1, POOL_WIDTH)), rev(D_MODEL), ANY],
        out_specs=[pl.BlockSpec((4, t, 128), lambda i: (0, n_tiles - 1 - i, 0)), rev(POOL_WIDTH),
                   _full((4, 128, 128)), _full((1, POOL_WIDTH)), ANY],
        out_shape=[jax.ShapeDtypeStruct((4, s_len, 128), BF16), jax.ShapeDtypeStruct((s_len, POOL_WIDTH), F32),
                   jax.ShapeDtypeStruct((4, 128, 128), F32), jax.ShapeDtypeStruct((1, POOL_WIDTH), F32),
                   jax.ShapeDtypeStruct((N_CHIPS, early_rows, D_MODEL), BF16)],
        scratch_shapes=[pltpu.VMEM((D_MODEL, D_MODEL), BF16), pltpu.VMEM((t + POOL_HALO, POOL_WIDTH), F32),
                        pltpu.VMEM((t + POOL_HALO, POOL_WIDTH), F32), pltpu.VMEM((D_MODEL, D_MODEL), F32),
                        pltpu.VMEM((D_MODEL, D_MODEL), BF16), pltpu.SemaphoreType.DMA((N_CHIPS,))],
        compiler_params=_params(),
    )(dh1, *wts, pooled, wpool, pool_scale, mix, after)


def _attn_bwd(qst, kn, vb, dost, bias_st, sinks, after):
    s_len = kn.shape[0]

    def body(q_ref, kp_ref, kc_ref, vp_ref, vc_ref, do_ref, bias_ref, sink_ref, after_ref, dq_ref, dk_ref, dv_ref, dbias_ref,
             dsink_ref, s_ref, dp_ref, p_ref, dl_ref):
        del after_ref
        i = pl.program_id(0)

        @pl.when(i == 0)
        def _():
            dk_ref[...] = jnp.zeros_like(dk_ref)
            dv_ref[...] = jnp.zeros_like(dv_ref)
            dbias_ref[...] = jnp.zeros_like(dbias_ref)
            dsink_ref[...] = jnp.zeros_like(dsink_ref)

        for b, (rows, k2, v2, bias) in enumerate(_step_blocks(i, kp_ref, kc_ref, vp_ref, vc_ref, bias_ref)):
            s_b, dp_b, p_b, dl_b = s_ref.at[b], dp_ref.at[b], p_ref.at[b], dl_ref.at[b]
            q = _expand_heads(q_ref[:, rows, :])
            do = _expand_heads(do_ref[:, rows, :])
            s_b[...] = _dot(q, k2, 1, 1)
            dp_b[...] = _dot(do, v2, 1, 1)

            def head(h, carry):
                head_rows, probs, p_sink = _head_softmax(s_b, bias, sink_ref, h)
                dp = dp_b[head_rows, :]
                dsum = jnp.sum(probs * dp, axis=-1, keepdims=True)
                dlog = probs * (dp - dsum)
                dsink_ref[head_rows, :] -= p_sink * dsum
                dbias_ref[head_rows, :] += dlog
                p_b[head_rows, :] = probs.astype(BF16)
                dl_b[head_rows, :] = (dlog * (HEAD_DIM ** -0.5)).astype(BF16)
                return carry

            lax.fori_loop(0, N_Q_HEADS, head, 0, unroll=True)
            dlog_s = dl_b[...]
            dq_ref[:, rows, :] = _fold_heads(_dot(dlog_s, k2, 1, 0))
            dk2 = _dot(dlog_s, q, 0, 0)
            dv2 = _dot(p_b[...], do, 0, 0)
            block = ATTN_STEP_BLOCKS * i + b
            prev_rows = pl.ds(pl.multiple_of(jnp.maximum(block - 1, 0) * BLOCK, BLOCK), BLOCK)
            cur_rows = pl.ds(pl.multiple_of(block * BLOCK, BLOCK), BLOCK)
            dk_ref[prev_rows, :] += dk2[:BLOCK]
            dk_ref[cur_rows, :] += dk2[BLOCK:]
            dv_ref[prev_rows, :] += dv2[:BLOCK]
            dv_ref[cur_rows, :] += dv2[BLOCK:]

    stacked, kv, consts = _attn_specs()
    per_step = (ATTN_STEP_BLOCKS,) + BAND
    return pl.pallas_call(
        body, name="attn_bwd", grid=(s_len // (ATTN_STEP_BLOCKS * BLOCK),),
        in_specs=[stacked] + kv + kv + [stacked] + consts + [ANY],
        out_specs=[stacked, _full((s_len, 128)), _full((s_len, 128)), _full(BAND), _full((N_Q_HEADS * BLOCK, 1))],
        out_shape=[jax.ShapeDtypeStruct((4, s_len, 128), F32), jax.ShapeDtypeStruct((s_len, 128), F32),
                   jax.ShapeDtypeStruct((s_len, 128), F32), jax.ShapeDtypeStruct(BAND, F32),
                   jax.ShapeDtypeStruct((N_Q_HEADS * BLOCK, 1), F32)],
        scratch_shapes=[pltpu.VMEM(per_step, F32), pltpu.VMEM(per_step, F32), pltpu.VMEM(per_step, BF16),
                        pltpu.VMEM(per_step, BF16)],
        compiler_params=_params(),
    )(qst, kn, kn, vb, vb, dost, bias_st, sinks, after)


def _flip_rows(x):
    n = x.shape[0]
    exchange = (lax.broadcasted_iota(jnp.int32, (n, n), 0) + lax.broadcasted_iota(jnp.int32, (n, n), 1) == n - 1)
    exchange = jnp.where(exchange, 1.0, 0.0).astype(BF16)
    flipped, rest = None, x
    for _ in range(3):
        term = rest.astype(BF16)
        rest = rest - term.astype(F32)
        part = _dot(exchange, term, 1, 0)
        flipped = part if flipped is None else flipped + part
    return flipped


def _small_pack(dg_attn, dg_ffn, dg_ple, dscale, dgq, dgk, dbias, dsink_rows, loss_v):
    def body(ga_ref, gf_ref, gp_ref, sc_ref, gq_ref, gk_ref, db_ref, ds_ref, bucket_ref, loss_ref, out_ref):
        out_ref[...] = jnp.zeros((SMALL_ROWS, 128), F32)
        for name, ref, n in (("g_attn", ga_ref, 8), ("g_ffn", gf_ref, 8), ("g_ple", gp_ref, 8), ("pool_scale", sc_ref, 4)):
            for k in range(n):
                out_ref[pl.ds(SMALL[name] + k, 1), :] = ref[:, 128 * k:128 * k + 128]
        for name, ref in (("g_q", gq_ref), ("g_k", gk_ref)):
            both = ref[...]
            out_ref[pl.ds(SMALL[name], 1), :] = both + pltpu.roll(both, 64, axis=1)
        out_ref[pl.ds(SMALL["loss"], 1), :] = loss_ref[...]
        by_diagonal = lambda flipped: pltpu.roll(flipped, 0, 1, stride=1, stride_axis=0)
        bucket_of = jnp.max(by_diagonal(bucket_ref[...]), axis=0, keepdims=True)
        sums = jnp.concatenate([jnp.sum(by_diagonal(_flip_rows(db_ref[pl.ds(h * BLOCK, BLOCK), :])), axis=0, keepdims=True)
                                for h in range(N_Q_HEADS)], axis=0)
        lanes = lax.broadcasted_iota(jnp.int32, (N_Q_HEADS, 128), 1)
        lane1 = lax.broadcasted_iota(jnp.int32, (1, 128), 1)
        rb = jnp.zeros((N_Q_HEADS, 128), F32)
        for b in range(N_BUCKETS):
            rb = jnp.where(lanes == b, jnp.sum(jnp.where(bucket_of == float(b), sums, 0.0), axis=1, keepdims=True), rb)
        sk = jnp.zeros((1, 128), F32)
        for h in range(N_Q_HEADS):
            sk = jnp.where(lane1 == h, jnp.sum(ds_ref[pl.ds(h * BLOCK, BLOCK), :]), sk)
        out_ref[pl.ds(SMALL["rel_bias"], N_Q_HEADS), :] = rb
        out_ref[pl.ds(SMALL["sinks"], 1), :] = sk

    bucket = jnp.asarray(_bucket_table()[::-1].astype(np.float32))
    return pl.pallas_call(
        body, name="small_pack", in_specs=[VMEM_WHOLE] * 10, out_specs=VMEM_WHOLE,
        out_shape=jax.ShapeDtypeStruct((SMALL_ROWS, 128), F32),
    )(dg_attn, dg_ffn, dg_ple, dscale, dgq, dgk, dbias, dsink_rows, bucket, loss_v)


def _attn_in_bwd(dqst, zqk, dk, dv, du, x2, dh1, hn1, slab, wts, g_attn, gq, gk):
    s_len = x2.shape[0]
    t = 512
    n_tiles = s_len // t

    def body(dq_ref, zqk_ref, dk_ref, dv_ref, du_ref, x_ref, dh1_ref, hn_ref, slab_in_ref, sl_ref, lo_ref, me_ref, g_ref,
             gq_ref, gk_ref, dx_ref, dg_ref, dgq_ref, dgk_ref, slab_ref, w_ref, dz_ref, acc_ref, stage_ref, sems):
        del slab_in_ref
        i = pl.program_id(0)

        @pl.when(i == 0)
        def _():
            _load_rows((sl_ref, lo_ref, me_ref), "inT", w_ref, sems)
            dg_ref[...] = jnp.zeros_like(dg_ref)
            dgq_ref[...] = jnp.zeros_like(dgq_ref)
            dgk_ref[...] = jnp.zeros_like(dgk_ref)
            acc_ref[...] = jnp.zeros_like(acc_ref)

        lo = lax.broadcasted_iota(jnp.int32, (t, 128), 1) < 64
        for p, dqn in enumerate(_unpack_heads([dq_ref[j] for j in range(4)], lo)):
            dq_raw, dgq = _pair_norm_bwd(zqk_ref[:, 128 * p:128 * p + 128], gq_ref[...], dqn)
            dz_ref[:, 128 * p:128 * p + 128] = dq_raw.astype(BF16)
            dgq_ref[...] += dgq
        dk_raw, dgk = _pair_norm_bwd(zqk_ref[:, 512:640], gk_ref[...], dk_ref[...])
        dgk_ref[...] += dgk
        dz_ref[:, 512:640] = dk_raw.astype(BF16)
        dz_ref[:, 640:768] = dv_ref[...].astype(BF16)
        dz_ref[:, 768:] = du_ref[...].astype(BF16)
        dz = dz_ref[...]
        acc_ref[...] += _dot(dz, hn_ref[...], 0, 0)
        dx, dg = _rms_bwd(x_ref[...], g_ref[...], _dot(dz, w_ref[...], 1, 0))
        dx_ref[...] = dh1_ref[...] + dx
        dg_ref[...] += dg

        @pl.when(i == n_tiles - 1)
        def _():
            _flush_chunks(acc_ref, stage_ref, slab_ref, "inT", sems)

    row = lambda w: pl.BlockSpec((t, w), lambda i: (i, 0))
    return pl.pallas_call(
        body, name="attn_in_bwd", grid=(n_tiles,),
        in_specs=[pl.BlockSpec((4, t, 128), lambda i: (0, i, 0)), row(640), row(128), row(128), row(POOL_WIDTH),
                  row(D_MODEL), row(D_MODEL), row(D_MODEL), ANY] + W_SPECS + [_full((1, D_MODEL)), _full((1, 128)),
                                                                              _full((1, 128))],
        out_specs=[row(D_MODEL), _full((1, D_MODEL)), _full((1, 128)), _full((1, 128)), ANY],
        out_shape=[jax.ShapeDtypeStruct((s_len, D_MODEL), F32), jax.ShapeDtypeStruct((1, D_MODEL), F32),
                   jax.ShapeDtypeStruct((1, 128), F32), jax.ShapeDtypeStruct((1, 128), F32),
                   jax.ShapeDtypeStruct(slab.shape, BF16)],
        input_output_aliases={8: 4},
        scratch_shapes=[pltpu.VMEM((IN_WIDTH, D_MODEL), BF16), pltpu.VMEM((t, IN_WIDTH), BF16),
                        pltpu.VMEM((IN_WIDTH, D_MODEL), F32), pltpu.VMEM((IN_WIDTH, D_MODEL), BF16),
                        pltpu.SemaphoreType.DMA((N_CHIPS,))],
        compiler_params=_params(),
    )(dqst, zqk, dk, dv, du, x2, dh1, hn1, slab, *wts, g_attn, gq, gk)


def _dw(lefts, b, name, slab, slab_rows, row_offs):
    a0, n_a = lefts[0], len(lefts)
    assert b.shape[1] == D_MODEL
    if a0.ndim == 3:
        n_chunks, s_len, tm = a0.shape
        m = n_chunks * tm
    else:
        s_len, tm = a0.shape
        m = tm
    tk = 2048 if n_a * tm <= 1408 else 1024
    if a0.ndim == 3:
        a_spec = pl.BlockSpec((None, tk, tm), lambda i, k: (i, k, 0))
    else:
        a_spec = pl.BlockSpec((tk, tm), lambda i, k: (k, i))
    n_steps, n_tiles = s_len // tk, m // tm
    chunk = m // N_CHIPS
    per_tile = tm // chunk

    def body(*refs):
        a_refs, b_ref = refs[:n_a], refs[n_a]
        o_ref, acc_ref, stage_ref, sems = refs[-4:]
        i, k = pl.program_id(0), pl.program_id(1)
        b_tile = b_ref[...].astype(BF16)
        for w, a_ref in enumerate(a_refs):
            _accumulate_tn(acc_ref.at[w], a_ref[...].astype(BF16), b_tile, k == 0)

        def out_copies(tile, slot):
            return [pltpu.make_async_copy(stage_ref.at[slot, w, pl.ds(jj * chunk, chunk), :],
                                          o_ref.at[tile * per_tile + jj, pl.ds(row_offs[w], chunk), :], sems.at[slot, w, jj])
                    for w in range(n_a) for jj in range(per_tile)]

        @pl.when(k == n_steps - 1)
        def _():
            slot = i % 2

            @pl.when(i >= 2)
            def _():
                for cp in out_copies(i - 2, slot):
                    cp.wait()

            stage_ref[slot] = acc_ref[...].astype(BF16)
            for cp in out_copies(i, slot):
                cp.start()

            @pl.when(i == n_tiles - 1)
            def _():
                for cp in out_copies(i, slot):
                    cp.wait()
                if n_tiles > 1:
                    for cp in out_copies(i - 1, 1 - slot):
                        cp.wait()

    in_specs = [a_spec] * n_a + [pl.BlockSpec((tk, D_MODEL), lambda i, k: (k, 0))]
    operands, aliases = [*lefts, b], {}
    if slab is not None:
        in_specs.append(ANY)
        operands.append(slab)
        aliases = {n_a + 1: 0}
    return pl.pallas_call(
        body, name=name, grid=(n_tiles, n_steps), in_specs=in_specs, out_specs=ANY,
        out_shape=jax.ShapeDtypeStruct((N_CHIPS, slab_rows, D_MODEL), BF16), input_output_aliases=aliases,
        scratch_shapes=[pltpu.VMEM((n_a, tm, D_MODEL), F32), pltpu.VMEM((2, n_a, tm, D_MODEL), BF16),
                        pltpu.SemaphoreType.DMA((2, n_a, per_tile))],
        compiler_params=_params(VMEM_LIMIT_BIG, n_axes=2),
    )(*operands)


def _position():
    x, y, c = lax.axis_index("x"), lax.axis_index("y"), lax.axis_index("c")
    other_chips = [(1 - x, y), (x, 1 - y), (1 - x, 1 - y)]
    return x, y, c, other_chips


def _ag_weights(local_slab, row0, n_rows, name, collective_id):
    half = n_rows // 2
    quarter = half // 2
    assert quarter % 16 == 0

    def body(l_ref, g_ref, send, recv):
        x, y, c, chips = _position()
        me, (via_x, via_y, diagonal) = 2 * x + y, [2 * chip[0] + chip[1] for chip in chips]
        here, sibling, x_nbr, y_nbr = (x, y, c), (x, y, 1 - c), (1 - x, y, c), (x, 1 - y, c)
        peers = [sibling, x_nbr, y_nbr]
        barrier = pltpu.get_barrier_semaphore()
        for peer in peers:
            pl.semaphore_signal(barrier, inc=1, device_id=peer, device_id_type=MESH)
        pl.semaphore_wait(barrier, len(peers))

        def rows(core, part):
            start, size = (core * half, half) if part is None else (core * half + part * quarter, quarter)
            return pl.ds(pl.multiple_of(start, 16), size)

        def copy(k, chip_idx, where, to, src=None):
            dst = g_ref.at[chip_idx, where, :]
            return pltpu.make_async_remote_copy(src_ref=dst if src is None else src, dst_ref=dst, send_sem=send.at[k],
                                                recv_sem=recv.at[k], device_id=to, device_id_type=MESH)

        own_rows = l_ref.at[pl.ds(pl.multiple_of(row0 + c * half, 16), half), :]
        started = [copy(0, me, rows(c, None), x_nbr, src=own_rows), copy(1, me, rows(c, None), y_nbr, src=own_rows)]
        for cp in started:
            cp.start()
        after_arrival = [
            (copy(0, via_x, rows(c, None), here), [copy(4, via_x, rows(c, None), sibling), copy(3, via_x, rows(c, 1), y_nbr)]),
            (copy(1, via_y, rows(c, None), here), [copy(5, via_y, rows(c, None), sibling), copy(2, via_y, rows(c, 0), x_nbr)]),
            (copy(2, diagonal, rows(c, 0), here), [copy(6, diagonal, rows(c, 0), sibling)]),
            (copy(3, diagonal, rows(c, 1), here), [copy(7, diagonal, rows(c, 1), sibling)]),
        ]
        for arrival, onward in after_arrival:
            arrival.wait_recv()
            for cp in onward:
                cp.start()
            started += onward
        for cp in (copy(4, via_x, rows(1 - c, None), here), copy(5, via_y, rows(1 - c, None), here),
                   copy(6, diagonal, rows(1 - c, 0), here), copy(7, diagonal, rows(1 - c, 1), here)):
            cp.wait_recv()
        for cp in started:
            cp.wait_send()

    return pl.kernel(
        body, out_type=jax.ShapeDtypeStruct((N_CHIPS, n_rows, D_MODEL), BF16),
        mesh=plsc.ScalarSubcoreMesh(axis_name="sequencer", num_cores=1), name=name,
        scratch_types=[pltpu.SemaphoreType.DMA((8,)), pltpu.SemaphoreType.DMA((8,))],
        compiler_params=pltpu.CompilerParams(collective_id=collective_id),
    )(local_slab)


def _comm_call(body, peers_of, out_shape, n_sems, operand, name, collective_id):
    sems = [pltpu.SemaphoreType.DMA((n_sems,)), pltpu.SemaphoreType.DMA((n_sems,))]

    def with_handshake(in_ref, out_ref, send, recv):
        x, y, c, _ = _position()
        peers = peers_of(x, y, c)
        barrier = pltpu.get_barrier_semaphore()
        for peer in peers:
            pl.semaphore_signal(barrier, inc=1, device_id=peer, device_id_type=MESH)
        pl.semaphore_wait(barrier, len(peers))
        body(in_ref, out_ref, send, recv)

    return pl.kernel(with_handshake, out_type=out_shape, mesh=plsc.ScalarSubcoreMesh(axis_name="sequencer", num_cores=1),
                     name=name, scratch_types=sems, compiler_params=pltpu.CompilerParams(collective_id=collective_id))(operand)


def _rs_swap_halves(partial, name, collective_id):
    half = partial.shape[1] // 2

    def body(p_ref, r_ref, send, recv):
        x, y, c, _ = _position()
        theirs = pl.ds(pl.multiple_of((1 - c) * half, 16), half)
        cp = pltpu.make_async_remote_copy(src_ref=p_ref.at[:, theirs, :], dst_ref=r_ref, send_sem=send.at[0],
                                          recv_sem=recv.at[0], device_id=(x, y, 1 - c), device_id_type=MESH)
        cp.start()
        cp.wait()

    return _comm_call(body, lambda x, y, c: [(x, y, 1 - c)], jax.ShapeDtypeStruct((N_CHIPS, half, D_MODEL), BF16), 1,
                      partial, name, collective_id)


def _gather_chip_sums(s_ref, sib_ref, sum_ref, o_ref, send, recv):
    x, y, c, chips = _position()
    me, sibling, here = 2 * x + y, (x, y, 1 - c), (x, y, c)
    half = s_ref.shape[0] // 2

    def rows(core):
        return pl.ds(pl.multiple_of(core * half, 8), half)

    def copy(k, src, dst, to):
        return pltpu.make_async_remote_copy(src_ref=src, dst_ref=dst, send_sem=send.at[k], recv_sem=recv.at[k], device_id=to,
                                            device_id_type=MESH)

    swap = copy(0, s_ref, sib_ref, sibling)
    keep = pltpu.make_async_copy(sum_ref, o_ref.at[me], send.at[7])
    sends = [copy(1 + k, sum_ref.at[rows(c), :], o_ref.at[me, rows(c), :], (*chip, c)) for k, chip in enumerate(chips)]

    def landed(chip, core):
        return o_ref.at[2 * chip[0] + chip[1], rows(core), :]

    def add_and_send():
        swap.wait_recv()
        sum_ref[...] = s_ref[...] + sib_ref[...]
        keep.start()
        for cp in sends:
            cp.start()

    def finish():
        passed = []
        for k, chip in enumerate(chips):
            copy(1 + k, landed(chip, c), landed(chip, c), here).wait_recv()
            fwd = copy(4 + k, landed(chip, c), landed(chip, c), sibling)
            fwd.start()
            passed.append(fwd)
        for k, chip in enumerate(chips):
            copy(4 + k, landed(chip, 1 - c), landed(chip, 1 - c), here).wait_recv()
        for cp in [swap] + sends + passed:
            cp.wait_send()
        keep.wait()

    return swap.start, add_and_send, finish


def _rs_add_halves(partial, other, core, name, after, small=None):
    half = other.shape[1]
    t = half // 2
    steps = half // t

    def body(core_ref, a_ref, b_ref, after_ref, *rest):
        del after_ref
        o_ref = rest[0] if small is None else rest[1]
        if small is not None:
            small_ref, _, t_ref, sib_ref, sum_ref, t_send, t_recv = rest
            swap, add_and_send, finish_tables = _gather_chip_sums(small_ref, sib_ref, sum_ref, t_ref, t_send, t_recv)
            step = pl.program_id(0) * steps + pl.program_id(1)
            pl.when(step == 0)(swap)
            pl.when(step == 1)(add_and_send)
        o_ref[...] = (a_ref[...].astype(F32) + b_ref[...].astype(F32)).astype(BF16)
        if small is not None:
            pl.when(step == N_CHIPS * steps - 1)(finish_tables)

    t_in, t_out, t_scratch = [], [], []
    if small is not None:
        t_in, t_out = [VMEM_WHOLE], [jax.ShapeDtypeStruct((N_CHIPS, *small.shape), F32)]
        t_scratch = [pltpu.VMEM(small.shape, F32)] * 2 + [pltpu.SemaphoreType.DMA((8,))] * 2
    res = pl.pallas_call(
        body, name=name,
        grid_spec=pltpu.PrefetchScalarGridSpec(
            num_scalar_prefetch=1, grid=(N_CHIPS, steps),
            in_specs=[pl.BlockSpec((1, t, D_MODEL), lambda j, i, core_ref: (j, core_ref[0] * steps + i, 0)),
                      pl.BlockSpec((1, t, D_MODEL), lambda j, i, core_ref: (j, i, 0)), ANY] + t_in,
            out_specs=[pl.BlockSpec((1, t, D_MODEL), lambda j, i, core_ref: (j, i, 0))] + [ANY] * len(t_out),
            scratch_shapes=t_scratch),
        out_shape=[jax.ShapeDtypeStruct((N_CHIPS, half, D_MODEL), BF16)] + t_out,
        compiler_params=_params(n_axes=2),
    )(core, partial, other, after, *([] if small is None else [small]))
    return res[0] if small is None else res


def _rs_exchange_chips(pre, name, collective_id):
    def body(s_ref, r_ref, send, recv):
        x, y, c, chips = _position()

        def copy(k, chunk, to):
            return pltpu.make_async_remote_copy(src_ref=s_ref.at[chunk], dst_ref=r_ref.at[k], send_sem=send.at[k],
                                                recv_sem=recv.at[k], device_id=to, device_id_type=MESH)

        sends = [copy(k, 2 * chip[0] + chip[1], (*chip, c)) for k, chip in enumerate(chips)]
        for cp in sends:
            cp.start()
        for cp in sends:
            cp.wait()

    return _comm_call(body, lambda x, y, c: [(1 - x, y, c), (x, 1 - y, c), (1 - x, 1 - y, c)],
                      jax.ShapeDtypeStruct((3, pre.shape[1], D_MODEL), BF16), 3, pre, name, collective_id)


def _gather_small(s_ref, t_ref, send, recv):
    x, y, c, chips = _position()
    sibling = (x, y, 1 - c)

    def slot(px, py, pc):
        return t_ref.at[4 * px + 2 * py + pc]

    def copy(k, block, to, src=None):
        return pltpu.make_async_remote_copy(src_ref=slot(*block) if src is None else src, dst_ref=slot(*block),
                                            send_sem=send.at[k], recv_sem=recv.at[k], device_id=to, device_id_type=MESH)

    own = pltpu.make_async_copy(s_ref, slot(x, y, c), send.at[7])
    first = [copy(0, (x, y, c), sibling, src=s_ref)]
    first += [copy(1 + k, (x, y, c), (*chip, c), src=s_ref) for k, chip in enumerate(chips)]

    def start():
        own.start()
        for cp in first:
            cp.start()

    def finish():
        passed = []
        for k, chip in enumerate(chips):
            copy(1 + k, (*chip, c), (x, y, c)).wait_recv()
            fwd = copy(4 + k, (*chip, c), sibling)
            fwd.start()
            passed.append(fwd)
        copy(0, sibling, (x, y, c)).wait_recv()
        for k, chip in enumerate(chips):
            copy(4 + k, (*chip, 1 - c), (x, y, c)).wait_recv()
        for cp in first + passed:
            cp.wait_send()
        own.wait()

    return start, finish


def _table_gather_parts(small):
    if small is None:
        return [], [], []
    return [VMEM_WHOLE], [jax.ShapeDtypeStruct((N_DEV, *small.shape), F32)], [pltpu.SemaphoreType.DMA((8,))] * 2


def _rs_sum_chips(pre, received, place, name, after, small=None):
    half = pre.shape[1]
    steps = 4 if half > 512 else 2
    t = half // steps
    assert t % 16 == 0 and t * steps == half

    def body(place_ref, own_ref, r_ref, after_ref, *rest):
        del place_ref, after_ref
        if small is None:
            o_ref, stage, kept_sems, send, recv = rest
        else:
            small_ref, o_ref, t_ref, stage, kept_sems, send, recv, t_send, t_recv = rest
            start_tables, finish_tables = _gather_small(small_ref, t_ref, t_send, t_recv)
            pl.when(pl.program_id(0) == 0)(start_tables)
        i = pl.program_id(0)
        x, y, c, _ = _position()

        def rows(core, step):
            return o_ref.at[pl.ds(pl.multiple_of((core * steps + step) * t, 8), t), :]

        def kept(step):
            return pltpu.make_async_copy(stage.at[step], rows(c, step), kept_sems.at[step])

        def sent(core, step):
            return pltpu.make_async_remote_copy(src_ref=stage.at[step], dst_ref=rows(core, step), send_sem=send.at[step],
                                                recv_sem=recv.at[step], device_id=(x, y, 1 - core), device_id_type=MESH)

        acc = own_ref[0].astype(F32)
        for k in range(3):
            acc = acc + r_ref[k].astype(F32)
        stage[i] = acc
        kept(i).start()
        sent(c, i).start()

        @pl.when(i == steps - 1)
        def _():
            if small is not None:
                finish_tables()
            for step in range(steps):
                kept(step).wait()
                sent(c, step).wait_send()
                sent(1 - c, step).wait_recv()

    t_in, t_out, t_scratch = _table_gather_parts(small)
    res = pl.pallas_call(
        body, name=name,
        grid_spec=pltpu.PrefetchScalarGridSpec(
            num_scalar_prefetch=1, grid=(steps,),
            in_specs=[pl.BlockSpec((1, t, D_MODEL), lambda i, place_ref: (place_ref[0], i, 0)),
                      pl.BlockSpec((3, t, D_MODEL), lambda i, place_ref: (0, i, 0)), ANY] + t_in,
            out_specs=[ANY] * (1 + len(t_out)),
            scratch_shapes=[pltpu.VMEM((steps, t, D_MODEL), F32)] + [pltpu.SemaphoreType.DMA((steps,))] * 3 + t_scratch),
        out_shape=[jax.ShapeDtypeStruct((2 * half, D_MODEL), F32)] + t_out, compiler_params=_params(),
    )(place, pre, received, after, *([] if small is None else [small]))
    return res[0] if small is None else res


def _adam_update(w, g, m, v):
    m_new = ADAM_B1 * m + (1.0 - ADAM_B1) * g
    v_new = ADAM_B2 * v + (1.0 - ADAM_B2) * (g * g)
    m_hat = m_new / (1.0 - ADAM_B1 ** ADAM_STEP)
    v_hat = v_new / (1.0 - ADAM_B2 ** ADAM_STEP)
    return -ADAM_LR * (m_hat / (jnp.sqrt(v_hat) + ADAM_EPS) + ADAM_WD * w), m_new, v_new


def _adamw(w, g_rows, row_off, m, v, name):
    rows, cols = w.shape
    t = rows if rows <= 320 else (rows // 2 if rows % 256 else 256)

    def body(w_ref, g_ref, m_ref, v_ref, go_ref, d_ref, nm_ref, nv_ref):
        g = g_ref[...]
        go_ref[...] = g
        d_ref[...], nm_ref[...], nv_ref[...] = _adam_update(w_ref[...], g, m_ref[...], v_ref[...])

    blk = pl.BlockSpec((t, cols), lambda i: (i, 0))
    assert row_off % 8 == 0 and t % 8 == 0
    g_blk = pl.BlockSpec((pl.Element(t), pl.Element(cols)), lambda i: (pl.multiple_of(row_off + i * t, 8), 0))
    shape = jax.ShapeDtypeStruct((rows, cols), F32)
    return pl.pallas_call(
        body, name=name, grid=(rows // t,), in_specs=[blk, g_blk, blk, blk], out_specs=[blk] * 4, out_shape=[shape] * 4,
        compiler_params=_params(),
    )(w, g_rows, m, v)


SMALL_PARAMS = [("g_attn", (1, D_MODEL), 8), ("g_q", (1, HEAD_DIM), None), ("g_k", (1, HEAD_DIM), None),
                ("sinks", (1, N_Q_HEADS), None), ("rel_bias", (N_Q_HEADS, N_BUCKETS), None), ("w_pool", (512, 128), None),
                ("pool_scale", (1, POOL_WIDTH), 4), ("g_ffn", (1, D_MODEL), 8), ("g_ple", (1, D_MODEL), 8)]


def _adamw_small(tables, pool_tables, wmv):
    n_par = len(SMALL_PARAMS)

    def body(*refs):
        t_ref, p_ref = refs[:2]
        ins = refs[2:2 + 3 * n_par]
        loss_ref = refs[2 + 3 * n_par]
        outs = refs[3 + 3 * n_par:-1]
        tot_ref = refs[-1]

        def in_order(ref):
            total = ref[0]
            for d in range(1, ref.shape[0]):
                total = total + ref[d]
            return total

        tot_ref[...] = in_order(t_ref)
        loss_ref[...] = tot_ref[pl.ds(SMALL["loss"], 1), 0:1]
        for i, (name, shape, split) in enumerate(SMALL_PARAMS):
            g_ref, d_ref, nm_ref, nv_ref = outs[4 * i:4 * i + 4]
            row = SMALL.get(name)
            if name == "w_pool":
                g_ref[...] = in_order(p_ref)
            elif split:
                for k in range(split):
                    g_ref[:, 128 * k:128 * k + 128] = tot_ref[pl.ds(row + k, 1), :]
            else:
                g_ref[...] = tot_ref[pl.ds(row, shape[0]), 0:shape[1]]
            w_ref, m_ref, v_ref = ins[3 * i:3 * i + 3]
            d_ref[...], nm_ref[...], nv_ref[...] = _adam_update(w_ref[...], g_ref[...], m_ref[...], v_ref[...])

    shapes = [jax.ShapeDtypeStruct((1, 1), F32)]
    for _, shape, _ in SMALL_PARAMS:
        shapes += [jax.ShapeDtypeStruct(shape, F32)] * 4
    flat = [a for triple in wmv for a in triple]
    res = pl.pallas_call(
        body, name="adamw_small", in_specs=[VMEM_WHOLE] * (2 + 3 * n_par), out_specs=[VMEM_WHOLE] * len(shapes),
        out_shape=shapes, scratch_shapes=[pltpu.VMEM((SMALL_ROWS, 128), F32)],
    )(tables, pool_tables, *flat)
    return res[0], [res[1 + 4 * i:5 + 4 * i] for i in range(n_par)]


def _pack_ple_proj(shard):
    return shard.reshape(4, 64, 256).transpose(1, 0, 2).reshape(64, D_MODEL)


class _Reduction:
    def __init__(self, tag, place, ids=(None, None)):
        self.tag, self.place, self.ids = tag, place, ids

    def start(self, partial):
        self.partial = partial
        self.other = _rs_swap_halves(partial, "rs_swap_" + self.tag, self.ids[0])
        return partial

    def middle(self, after, small=None):
        res = _rs_add_halves(self.partial, self.other, self.place[1:], "rs_add_" + self.tag, after, small)
        self.pre, self.tables = (res, None) if small is None else res
        self.received = _rs_exchange_chips(self.pre, "rs_exchange_" + self.tag, self.ids[1])
        return self.pre

    def finish(self, after, small=None):
        return _rs_sum_chips(self.pre, self.received, self.place, "rs_sum_" + self.tag, after, small)


def _local_grads(x2, p2, tgt, wts, g_attn_norm, g_q, g_k, attn_sinks, rel_bias, w_pool, pool_scale, g_ffn_norm, g_ple_norm,
                 reduce_a):
    w_early, w_late = wts
    w_in = w_out = w_early
    bucket = jnp.asarray(_bucket_table())
    gq = jnp.tile(g_q, (1, 2))
    gk = jnp.tile(g_k, (1, 2))
    wpool = w_pool[0].astype(BF16)
    sinks = attn_sinks[0]
    bias_st = _bias_build(rel_bias.T, bucket)

    hn1 = _first_norm(x2, g_attn_norm)
    zqk, u, kn, vb, qst = _attn_in(hn1, gq, gk, w_in)
    ost = _attn_fwd(qst, kn, vb, bias_st, sinks)
    pooled, mix, h1, hn2 = _mix_out(u, ost, x2, w_out, wpool, pool_scale, g_ffn_norm)
    loss_v, dgate, dup, act, dh2, hn3, dgl, dw_plp, dh1, dg_ffn, dg_ple = _ffn_ple(hn2, h1, p2, tgt, w_late, g_ffn_norm,
                                                                                      g_ple_norm)

    late0, late_rows = GATHER_PARTS[1][0], SLAB_ROWS - GATHER_PARTS[1][0]
    partial_a = None
    for names, lefts, right in ((("gateT", "upT"), [dgate, dup], hn2), (("down",), [act], dh2), (("plg",), [hn3], dgl)):
        partial_a = _dw(lefts, right, "dw_" + names[0], partial_a, late_rows, [SLAB[name][0] - late0 for name in names])
    dw_plp = dw_plp.reshape(4, 64, N_CHIPS, 256).transpose(2, 1, 0, 3).reshape(N_CHIPS, 64, D_MODEL)
    partial_a = reduce_a.start(lax.dynamic_update_slice(partial_a, dw_plp, (0, SLAB["plp"][0] - late0, 0)))
    dost, du, dw_pool, dscale, partial_b = _mix_out_bwd(dh1, w_out, pooled, wpool, pool_scale, mix, partial_a)
    pre_a = reduce_a.middle(du, dw_pool.reshape(512, 128))
    dqst, dk, dv, dbias, dsink_rows = _attn_bwd(qst, kn, vb, dost, bias_st, sinks, pre_a)
    dx, dg_attn, dgq, dgk, partial_b = _attn_in_bwd(dqst, zqk, dk, dv, du, x2, dh1, hn1, partial_b, w_in, g_attn_norm, gq, gk)

    small = _small_pack(dg_attn, dg_ffn, dg_ple, dscale, dgq, dgk, dbias, dsink_rows, loss_v)
    return dx, partial_b, small


def kernel(x, p, w_in, w_out, g_attn_norm, g_q, g_k, attn_sinks, rel_bias, w_pool, pool_scale, g_ffn_norm, w_gate, w_up, w_down, g_ple_norm, w_ple_gate, w_ple_proj, loss_target, m_w_in, m_w_out, m_g_attn_norm, m_g_q, m_g_k, m_attn_sinks, m_rel_bias, m_w_pool, m_pool_scale, m_g_ffn_norm, m_w_gate, m_w_up, m_w_down, m_g_ple_norm, m_w_ple_gate, m_w_ple_proj, v_w_in, v_w_out, v_g_attn_norm, v_g_q, v_g_k, v_attn_sinks, v_rel_bias, v_w_pool, v_pool_scale, v_g_ffn_norm, v_w_gate, v_w_up, v_w_down, v_g_ple_norm, v_w_ple_gate, v_w_ple_proj):
    core = lax.axis_index("c").astype(jnp.int32).reshape(1)
    me = (2 * lax.axis_index("x") + lax.axis_index("y")).astype(jnp.int32).reshape(1)

    local_parts = [jnp.concatenate(pieces, axis=0).astype(BF16) for pieces in (
        [w_in[0].T, w_out[0]], [w_gate[0].T, w_up[0].T, w_down[0], w_ple_gate[0], _pack_ple_proj(w_ple_proj[0])])]
    wts = [(_ag_weights(local, 0, local.shape[0], name, collective_id), local, me)
           for local, name, collective_id in zip(local_parts, ("ag_early", "ag_late"), (1, 2))]

    place = jnp.concatenate([me, core])
    reduce_a = _Reduction("a", place, ids=(3, 4))
    dx, partial_b, small = _local_grads(x[0], p[0, 0], loss_target[0], wts, g_attn_norm, g_q, g_k, attn_sinks, rel_bias,
                                        w_pool, pool_scale, g_ffn_norm, g_ple_norm, reduce_a)
    reduce_b = _Reduction("b", place, ids=(6, 7))
    reduce_b.start(partial_b)
    grads_a, small_all = reduce_a.finish(partial_b, small)
    reduce_b.middle(grads_a)

    late0 = GATHER_PARTS[1][0]

    def rows(name):
        return grads_a, SLAB[name][0] - late0

    plp_rows = grads_a[SLAB["plp"][0] - late0:]
    big = {
        "w_gate": (w_gate, m_w_gate, v_w_gate, rows("gateT"), True),
        "w_up": (w_up, m_w_up, v_w_up, rows("upT"), True),
        "w_down": (w_down, m_w_down, v_w_down, rows("down"), False),
        "w_ple_gate": (w_ple_gate, m_w_ple_gate, v_w_ple_gate, rows("plg"), False),
        "w_ple_proj": (w_ple_proj, m_w_ple_proj, v_w_ple_proj,
                       (plp_rows.reshape(64, 4, 256).transpose(1, 0, 2).reshape(PLE_DIM, PLE_DIM), 0), False),
        "w_out": (w_out, m_w_out, v_w_out, None, False),
        "w_in": (w_in, m_w_in, v_w_in, None, True),
    }
    small_params = {
        "g_attn_norm": (g_attn_norm, m_g_attn_norm, v_g_attn_norm), "g_q": (g_q, m_g_q, v_g_q), "g_k": (g_k, m_g_k, v_g_k),
        "attn_sinks": (attn_sinks, m_attn_sinks, v_attn_sinks), "rel_bias": (rel_bias.T, m_rel_bias.T, v_rel_bias.T),
        "w_pool": tuple(a.reshape(512, 128) for a in (w_pool, m_w_pool, v_w_pool)),
        "pool_scale": (pool_scale, m_pool_scale, v_pool_scale), "g_ffn_norm": (g_ffn_norm, m_g_ffn_norm, v_g_ffn_norm),
        "g_ple_norm": (g_ple_norm, m_g_ple_norm, v_g_ple_norm),
    }

    grads, deltas, new_ms, new_vs = {}, {}, {}, {}
    out = grads_b = None
    for name, (w, m, v, g_src, transposed) in big.items():
        if g_src is None:
            if grads_b is None:
                grads_b = reduce_b.finish(out[-1])
            g_src = (grads_b, SLAB["out" if name == "w_out" else "inT"][0])
        view = (lambda a: a.T) if transposed else (lambda a: a)
        out = _adamw(view(w[0]), *g_src, view(m[0]), view(v[0]), "adamw_" + name)
        grads[name], deltas[name], new_ms[name], new_vs[name] = (view(a)[None] for a in out)

    loss, small_out = _adamw_small(small_all, reduce_a.tables, list(small_params.values()))
    for name, (g2, d, nm, nv) in zip(small_params, small_out):
        restore = {"w_pool": lambda a: a.reshape(w_pool.shape), "rel_bias": lambda a: a.T}.get(name, lambda a: a)
        grads[name], deltas[name], new_ms[name], new_vs[name] = (restore(a) for a in (g2, d, nm, nv))

    order = ["w_in", "w_out", "g_attn_norm", "g_q", "g_k", "attn_sinks", "rel_bias", "w_pool", "pool_scale", "g_ffn_norm",
             "w_gate", "w_up", "w_down", "g_ple_norm", "w_ple_gate", "w_ple_proj"]
    return (loss.reshape(()), dx[None], *[grads[n] for n in order], *[deltas[n] for n in order],
            *[new_ms[n] for n in order], *[new_vs[n] for n in order])
```

```python
import numpy as np
import jax
import jax.numpy as jnp
from jax import lax
from jax.experimental import pallas as pl
from jax.experimental.pallas import tpu as pltpu
from jax.experimental.pallas import tpu_sc as plsc

F32 = jnp.float32
BF16 = jnp.bfloat16
MESH = pl.DeviceIdType.MESH

D_MODEL = 1024
HEAD_DIM = 64
N_Q_HEADS = 8
ATTN_WIDTH = 512
POOL_WIDTH = 512
IN_WIDTH = 1280
D_FF = 2816
PLE_DIM = 256
FF_CHUNK = 1408
N_FF_CHUNKS = D_FF // FF_CHUNK
BLOCK = 128
N_BUCKETS = 32
MAX_DISTANCE = 128
EPS = 1e-6
NEG = -1e30
N_CHIPS = 4
N_DEV = 8

ADAM_LR = 0.001
ADAM_B1 = 0.9
ADAM_B2 = 0.999
ADAM_EPS = 1e-08
ADAM_WD = 0.01
ADAM_STEP = 10

SLAB = {"inT": (0, 320), "out": (320, 256), "gateT": (576, 704), "upT": (1280, 704), "down": (1984, 704),
        "plg": (2688, 256), "plp": (2944, 64)}
SLAB_ROWS = 3008
GATHER_PARTS = ((0, 576), (576, SLAB_ROWS))
POOL_HALO = 24

SMALL = {"g_attn": 0, "g_ffn": 8, "g_ple": 16, "pool_scale": 24, "g_q": 28, "g_k": 29, "sinks": 30, "loss": 31,
         "rel_bias": 32}
SMALL_ROWS = 64

VMEM_LIMIT_BIG = 60 * 1024 * 1024
VMEM_LIMIT = 48 * 1024 * 1024


def _params(vmem=VMEM_LIMIT, n_axes=1):
    return pltpu.CompilerParams(dimension_semantics=("arbitrary",) * n_axes, vmem_limit_bytes=vmem)


def _dot(a, b, ca, cb):
    return lax.dot_general(a, b, (((ca,), (cb,)), ((), ())), preferred_element_type=F32)


def _full(shape):
    return pl.BlockSpec(shape, lambda i: (0,) * len(shape))


ANY = pl.BlockSpec(memory_space=pl.ANY)
VMEM_WHOLE = pl.BlockSpec(memory_space=pltpu.VMEM)


W_SPECS = [ANY, ANY, pl.BlockSpec(memory_space=pltpu.SMEM)]


def _load_rows(w_refs, name, dst_ref, sems):
    slab_ref, local_ref, me_ref = w_refs
    off, rows = SLAB[name]
    slab_off = off - max(start for start, _ in GATHER_PARTS if start <= off)
    me = me_ref[0]
    for phase in ("start", "wait"):
        for j in range(N_CHIPS):
            dst = dst_ref.at[pl.ds(j * rows, rows), :]
            theirs = pltpu.make_async_copy(slab_ref.at[j, pl.ds(slab_off, rows), :], dst, sems.at[j])
            own = pltpu.make_async_copy(local_ref.at[pl.ds(slab_off, rows), :], dst, sems.at[j])

            @pl.when(me == j)
            def _():
                getattr(own, phase)()

            @pl.when(me != j)
            def _():
                getattr(theirs, phase)()


def _rms_fwd(x, g):
    r = lax.rsqrt(jnp.mean(x * x, axis=-1, keepdims=True) + EPS)
    return x * r * g


def _rms_bwd(x, g, dy):
    r = lax.rsqrt(jnp.mean(x * x, axis=-1, keepdims=True) + EPS)
    xn = x * r
    dyg = dy * g
    dx = r * (dyg - xn * jnp.mean(dyg * xn, axis=-1, keepdims=True))
    return dx, jnp.sum(dy * xn, axis=0, keepdims=True)


def _half_sum(v, lo):
    s_lo = jnp.sum(jnp.where(lo, v, 0.0), axis=-1, keepdims=True)
    s_hi = jnp.sum(jnp.where(lo, 0.0, v), axis=-1, keepdims=True)
    return jnp.where(lo, s_lo, s_hi)


def _half_sum_mxu(v):
    upper = lax.broadcasted_iota(jnp.int32, (128, 128), 0) < 64
    left = lax.broadcasted_iota(jnp.int32, (128, 128), 1) < 64
    ones = jnp.where(upper == left, 1.0, 0.0).astype(BF16)
    high = v.astype(BF16)
    low = (v - high.astype(F32)).astype(BF16)
    return _dot(high, ones, 1, 0) + _dot(low, ones, 1, 0)


def _pair_norm(zp, g, lo):
    r = lax.rsqrt(_half_sum(zp * zp, lo) * (1.0 / HEAD_DIM) + EPS)
    return zp * r * g


def _pair_norm_bwd(zp, g, dy):
    r = lax.rsqrt(_half_sum_mxu(zp * zp) * (1.0 / HEAD_DIM) + EPS)
    xn = zp * r
    dyg = dy * g
    dx = r * (dyg - xn * (_half_sum_mxu(dyg * xn) * (1.0 / HEAD_DIM)))
    return dx, jnp.sum(dy * xn, axis=0, keepdims=True)


def _pack_heads(pairs, lo):
    packed = [None] * 4
    for m in range(2):
        a, b = pairs[m], pairs[m + 2]
        packed[2 * m] = jnp.where(lo, a, pltpu.roll(b, 64, axis=1))
        packed[2 * m + 1] = jnp.where(lo, pltpu.roll(a, 64, axis=1), b)
    return packed


def _unpack_heads(packed, lo):
    pairs = [None] * 4
    for m in range(2):
        a, b = packed[2 * m], packed[2 * m + 1]
        pairs[m] = jnp.where(lo, a, pltpu.roll(b, 64, axis=1))
        pairs[m + 2] = jnp.where(lo, pltpu.roll(a, 64, axis=1), b)
    return pairs


def _expand_heads(packed):
    flat = packed.reshape(4 * BLOCK, 128)
    lo = lax.broadcasted_iota(jnp.int32, flat.shape, 1) < 64
    zero = jnp.zeros_like(flat)
    return jnp.concatenate([jnp.where(lo, flat, zero), jnp.where(lo, zero, flat)], axis=0)


def _fold_heads(stacked):
    half = 4 * BLOCK
    lo = lax.broadcasted_iota(jnp.int32, (half, 128), 1) < 64
    return jnp.where(lo, stacked[:half], stacked[half:]).reshape(4, BLOCK, 128)


def _sigmoid(v):
    return 1.0 / (1.0 + jnp.exp(-v))


def _pool_counts(tile, n_rows):
    t1 = tile * n_rows + lax.broadcasted_iota(jnp.int32, (n_rows, POOL_WIDTH), 0) + 1
    lane = lax.broadcasted_iota(jnp.int32, (n_rows, POOL_WIDTH), 1)
    win = jnp.where(lane < 128, 2, jnp.where(lane < 256, 4, jnp.where(lane < 384, 8, 16)))
    return jnp.minimum(t1, win).astype(F32)


def _first_norm(x2, g_attn):
    s_len = x2.shape[0]
    t = 512

    def body(x_ref, g_ref, hn_ref):
        hn_ref[...] = _rms_fwd(x_ref[...], g_ref[...]).astype(BF16)

    row = pl.BlockSpec((t, D_MODEL), lambda i: (i, 0))
    return pl.pallas_call(
        body, name="first_norm", grid=(s_len // t,), in_specs=[row, _full((1, D_MODEL))], out_specs=row,
        out_shape=jax.ShapeDtypeStruct((s_len, D_MODEL), BF16), compiler_params=_params(),
    )(x2, g_attn)


def _attn_in(hn1, gq, gk, wts):
    s_len = hn1.shape[0]
    t = 512

    def body(hn_ref, gq_ref, gk_ref, sl_ref, lo_ref, me_ref, zqk_ref, u_ref, kn_ref, v_ref, qst_ref, w_ref, sems):
        @pl.when(pl.program_id(0) == 0)
        def _():
            _load_rows((sl_ref, lo_ref, me_ref), "inT", w_ref, sems)

        z = _dot(hn_ref[...], w_ref[...], 1, 1)
        zqk_ref[...] = z[:, :640]
        u_ref[...] = z[:, 768:]
        v_ref[...] = z[:, 640:768].astype(BF16)
        lo = lax.broadcasted_iota(jnp.int32, (t, 128), 1) < 64
        kn_ref[...] = _pair_norm(z[:, 512:640], gk_ref[...], lo).astype(BF16)
        pairs = [_pair_norm(z[:, 128 * p:128 * p + 128], gq_ref[...], lo) for p in range(4)]
        for j, entry in enumerate(_pack_heads(pairs, lo)):
            qst_ref[j] = entry.astype(BF16)

    row = lambda w: pl.BlockSpec((t, w), lambda i: (i, 0))
    return pl.pallas_call(
        body, name="attn_in", grid=(s_len // t,),
        in_specs=[row(D_MODEL), _full((1, 128)), _full((1, 128))] + W_SPECS,
        out_specs=[row(640), row(POOL_WIDTH), row(128), row(128), pl.BlockSpec((4, t, 128), lambda i: (0, i, 0))],
        out_shape=[jax.ShapeDtypeStruct((s_len, 640), F32), jax.ShapeDtypeStruct((s_len, POOL_WIDTH), F32),
                   jax.ShapeDtypeStruct((s_len, 128), BF16), jax.ShapeDtypeStruct((s_len, 128), BF16),
                   jax.ShapeDtypeStruct((4, s_len, 128), BF16)],
        scratch_shapes=[pltpu.VMEM((IN_WIDTH, D_MODEL), BF16), pltpu.SemaphoreType.DMA((N_CHIPS,))],
        compiler_params=_params(),
    )(hn1, gq, gk, *wts)


def _bucket_table():
    i_idx = np.arange(BLOCK)[:, None]
    j_idx = np.arange(2 * BLOCK)[None, :]
    d = BLOCK + i_idx - j_idx
    n = np.maximum(d, 0)
    max_exact = N_BUCKETS // 2
    nf = np.maximum(n, 1).astype(np.float64)
    large = max_exact + (np.log(nf / max_exact) / np.log(MAX_DISTANCE / max_exact) * (N_BUCKETS - max_exact)).astype(np.int64)
    large = np.minimum(large, N_BUCKETS - 1)
    bucket = np.where(n < max_exact, n, large)
    return np.where((d >= 0) & (d < BLOCK), bucket, -1).astype(np.int32)


def _bias_build(rel_bias_t, bucket):
    def body(rb_ref, bucket_ref, out_ref):
        bk = bucket_ref[...]
        for h in range(N_Q_HEADS):
            acc = jnp.full((BLOCK, 2 * BLOCK), NEG, F32)
            for b in range(N_BUCKETS):
                acc = jnp.where(bk == b, rb_ref[h, b], acc)
            out_ref[0, pl.ds(h * BLOCK, BLOCK), :] = acc
            out_ref[1, pl.ds(h * BLOCK, BLOCK), :] = acc
            out_ref[1, pl.ds(h * BLOCK, BLOCK), 0:BLOCK] = jnp.full((BLOCK, BLOCK), NEG, F32)

    return pl.pallas_call(
        body, name="bias_build",
        in_specs=[pl.BlockSpec(memory_space=pltpu.SMEM), VMEM_WHOLE], out_specs=VMEM_WHOLE,
        out_shape=jax.ShapeDtypeStruct((2, N_Q_HEADS * BLOCK, 2 * BLOCK), F32),
    )(rel_bias_t, bucket)


def _head_softmax(s_ref, bias_ref, sink_ref, h):
    rows = pl.ds(pl.multiple_of(h * BLOCK, BLOCK), BLOCK)
    s = s_ref[rows, :] * (HEAD_DIM ** -0.5) + bias_ref[rows, :]
    sink = sink_ref[h]
    m = jnp.maximum(jnp.max(s, axis=-1, keepdims=True), sink)
    p = jnp.exp(s - m)
    e_sink = jnp.exp(sink - m)
    inv = 1.0 / (jnp.sum(p, axis=-1, keepdims=True) + e_sink)
    return rows, p * inv, e_sink * inv


ATTN_STEP_BLOCKS = 4
BAND = (N_Q_HEADS * BLOCK, 2 * BLOCK)


def _attn_specs():
    nb = ATTN_STEP_BLOCKS
    stacked = pl.BlockSpec((4, nb * BLOCK, 128), lambda i: (0, i, 0))
    kv = [pl.BlockSpec((BLOCK, 128), lambda i: (jnp.maximum(nb * i - 1, 0), 0)), pl.BlockSpec((nb * BLOCK, 128), lambda i: (i, 0))]
    consts = [_full((2,) + BAND), pl.BlockSpec(memory_space=pltpu.SMEM)]
    return stacked, kv, consts


def _step_blocks(i, kp_ref, kc_ref, vp_ref, vc_ref, bias_ref):
    blocks = []
    for b in range(ATTN_STEP_BLOCKS):
        if b == 0:
            k2 = jnp.concatenate([kp_ref[...], kc_ref[pl.ds(0, BLOCK), :]], axis=0)
            v2 = jnp.concatenate([vp_ref[...], vc_ref[pl.ds(0, BLOCK), :]], axis=0)
            bias = bias_ref.at[jnp.where(i == 0, 1, 0)]
        else:
            k2, v2, bias = kc_ref[pl.ds((b - 1) * BLOCK, 2 * BLOCK), :], vc_ref[pl.ds((b - 1) * BLOCK, 2 * BLOCK), :], bias_ref.at[0]
        blocks.append((pl.ds(b * BLOCK, BLOCK), k2, v2, bias))
    return blocks


def _attn_fwd(qst, kn, vb, bias_st, sinks):
    s_len = kn.shape[0]

    def body(q_ref, kp_ref, kc_ref, vp_ref, vc_ref, bias_ref, sink_ref, o_ref, s_ref, p_ref):
        for b, (rows, k2, v2, bias) in enumerate(_step_blocks(pl.program_id(0), kp_ref, kc_ref, vp_ref, vc_ref, bias_ref)):
            s_b, p_b = s_ref.at[b], p_ref.at[b]
            s_b[...] = _dot(_expand_heads(q_ref[:, rows, :]), k2, 1, 1)

            def head(h, carry):
                head_rows, probs, _ = _head_softmax(s_b, bias, sink_ref, h)
                p_b[head_rows, :] = probs.astype(BF16)
                return carry

            lax.fori_loop(0, N_Q_HEADS, head, 0, unroll=True)
            o_ref[:, rows, :] = _fold_heads(_dot(p_b[...], v2, 1, 0)).astype(BF16)

    stacked, kv, consts = _attn_specs()
    return pl.pallas_call(
        body, name="attn_fwd", grid=(s_len // (ATTN_STEP_BLOCKS * BLOCK),),
        in_specs=[stacked] + kv + kv + consts, out_specs=stacked,
        out_shape=jax.ShapeDtypeStruct((4, s_len, 128), BF16),
        scratch_shapes=[pltpu.VMEM((ATTN_STEP_BLOCKS,) + BAND, F32), pltpu.VMEM((ATTN_STEP_BLOCKS,) + BAND, BF16)],
        compiler_params=_params(),
    )(qst, kn, kn, vb, vb, bias_st, sinks)


def _mix_out(u, ost, x2, wts, wpool, pool_scale, g_ffn):
    s_len = x2.shape[0]
    t = 512
    n = t + 16

    def body(u_ref, o_ref, x_ref, sl_ref, lo_ref, me_ref, wp_ref, sc_ref, g_ref, pooled_ref, mix_ref, h1_ref, hn_ref,
             w_ref, ext_ref, st_ref, sems):
        i = pl.program_id(0)

        @pl.when(i == 0)
        def _():
            _load_rows((sl_ref, lo_ref, me_ref), "out", w_ref, sems)
            ext_ref[...] = jnp.zeros_like(ext_ref)
            st_ref[...] = jnp.zeros_like(st_ref)

        u_tile = u_ref[...]
        ext_ref[pl.ds(POOL_HALO, t), :] = u_tile
        st_ref[pl.ds(8, n), :] = ext_ref[pl.ds(8, n), :] + ext_ref[pl.ds(7, n), :]
        st_ref[pl.ds(8, n), 128:] = st_ref[pl.ds(8, n), 128:] + st_ref[pl.ds(6, n), 128:]
        st_ref[pl.ds(8, n), 256:] = st_ref[pl.ds(8, n), 256:] + st_ref[pl.ds(4, n), 256:]
        st_ref[pl.ds(8, n), 384:] = st_ref[pl.ds(8, n), 384:] + st_ref[pl.ds(0, n), 384:]
        ext_ref[pl.ds(0, POOL_HALO), :] = ext_ref[pl.ds(t, POOL_HALO), :]
        pooled = (st_ref[pl.ds(POOL_HALO, t), :] / _pool_counts(i, t) - u_tile).astype(BF16)
        pooled_ref[...] = pooled
        for g in range(4):
            cols = slice(128 * g, 128 * g + 128)
            y = _dot(pooled[:, cols], wp_ref[g], 1, 0) * sc_ref[:, cols]
            mix_ref[:, ATTN_WIDTH + 128 * g:ATTN_WIDTH + 128 * g + 128] = y.astype(BF16)
        lo = lax.broadcasted_iota(jnp.int32, (t, 128), 1) < 64
        for p, pair in enumerate(_unpack_heads([o_ref[j].astype(F32) for j in range(4)], lo)):
            mix_ref[:, 128 * p:128 * p + 128] = pair.astype(BF16)
        h1 = x_ref[...] + _dot(mix_ref[...], w_ref[...], 1, 0)
        h1_ref[...] = h1
        hn_ref[...] = _rms_fwd(h1, g_ref[...]).astype(BF16)

    row = lambda w: pl.BlockSpec((t, w), lambda i: (i, 0))
    return pl.pallas_call(
        body, name="mix_out", grid=(s_len // t,),
        in_specs=[row(POOL_WIDTH), pl.BlockSpec((4, t, 128), lambda i: (0, i, 0)), row(D_MODEL)] + W_SPECS
        + [_full((4, 128, 128)), _full((1, POOL_WIDTH)), _full((1, D_MODEL))],
        out_specs=[row(POOL_WIDTH), row(D_MODEL), row(D_MODEL), row(D_MODEL)],
        out_shape=[jax.ShapeDtypeStruct((s_len, POOL_WIDTH), BF16), jax.ShapeDtypeStruct((s_len, D_MODEL), BF16),
                   jax.ShapeDtypeStruct((s_len, D_MODEL), F32), jax.ShapeDtypeStruct((s_len, D_MODEL), BF16)],
        scratch_shapes=[pltpu.VMEM((D_MODEL, D_MODEL), BF16), pltpu.VMEM((t + POOL_HALO, POOL_WIDTH), F32),
                        pltpu.VMEM((t + POOL_HALO, POOL_WIDTH), F32), pltpu.SemaphoreType.DMA((N_CHIPS,))],
        compiler_params=_params(),
    )(u, ost, x2, *wts, wpool, pool_scale, g_ffn)


def _ffn_ple(hn2, h1, p2, tgt, wts, g_ffn, g_ple):
    s_len = h1.shape[0]
    t = 256
    n_tiles = s_len // t

    def body(hn_ref, h1_ref, p_ref, tgt_ref, sl_ref, lo_ref, me_ref, gf_ref, gp_ref,
             loss_ref, dgate_ref, dup_ref, act_ref, dh2b_ref, hn3_ref, dgl_ref, dwp_ref, dh1_ref, dgf_ref, dgp_ref,
             wg_ref, wu_ref, wd_ref, wl_ref, wp_ref, packed_ref, gate_s, up_s, loss_acc, dwp_acc, sems):
        i = pl.program_id(0)

        @pl.when(i == 0)
        def _():
            w_refs = (sl_ref, lo_ref, me_ref)
            _load_rows(w_refs, "gateT", wg_ref, sems)
            _load_rows(w_refs, "upT", wu_ref, sems)
            _load_rows(w_refs, "down", wd_ref, sems)
            _load_rows(w_refs, "plg", wl_ref, sems)
            _load_rows(w_refs, "plp", packed_ref, sems)
            for j in range(N_CHIPS):
                for q in range(4):
                    wp_ref[pl.ds(64 * q, 64), 256 * j:256 * j + 256] = packed_ref[pl.ds(64 * j, 64), 256 * q:256 * q + 256]
            loss_acc[...] = jnp.zeros_like(loss_acc)
            dwp_acc[...] = jnp.zeros_like(dwp_acc)
            dgf_ref[...] = jnp.zeros_like(dgf_ref)
            dgp_ref[...] = jnp.zeros_like(dgp_ref)

        hn = hn_ref[...]
        h1v = h1_ref[...]
        chunks = [slice(ch * FF_CHUNK, (ch + 1) * FF_CHUNK) for ch in range(N_FF_CHUNKS)]
        gate = _dot(hn, wg_ref[...], 1, 1)
        up = _dot(hn, wu_ref[...], 1, 1)
        gate_s[...] = gate
        up_s[...] = up
        act = (gate * _sigmoid(gate) * up).astype(BF16)
        for ch, cols in enumerate(chunks):
            act_ref[ch] = act[:, cols]
        h2 = h1v + _dot(act, wd_ref[...], 1, 0)
        gp = gp_ref[...]
        hn3 = _rms_fwd(h2, gp).astype(BF16)
        hn3_ref[...] = hn3
        gate2 = _sigmoid(_dot(hn3, wl_ref[...], 1, 0))
        p_tile = p_ref[...].astype(BF16)
        pp = _dot(p_tile, wp_ref[...], 1, 0)
        err = h2 + gate2 * pp - tgt_ref[...]
        loss_acc[...] += jnp.sum(err * err, axis=0, keepdims=True)
        dy = err * (1.0 / D_MODEL)
        dwp_acc[...] += _dot(p_tile, (dy * gate2).astype(BF16), 0, 0)
        dgl = (dy * pp * gate2 * (1.0 - gate2)).astype(BF16)
        dgl_ref[...] = dgl
        dx3, dg3 = _rms_bwd(h2, gp, _dot(dgl, wl_ref[...], 1, 1))
        dh2 = dy + dx3
        dgp_ref[...] += dg3
        dh2b = dh2.astype(BF16)
        dh2b_ref[...] = dh2b
        dact = _dot(dh2b, wd_ref[...], 1, 1)
        gate_v = gate_s[...]
        up_v = up_s[...]
        sg = _sigmoid(gate_v)
        dup = (dact * (gate_v * sg)).astype(BF16)
        dgate = (dact * up_v * (sg * (1.0 + gate_v * (1.0 - sg)))).astype(BF16)
        for ch, cols in enumerate(chunks):
            dup_ref[ch] = dup[:, cols]
            dgate_ref[ch] = dgate[:, cols]
        dhn = _dot(dgate, wg_ref[...], 1, 0) + _dot(dup, wu_ref[...], 1, 0)
        dx, dg = _rms_bwd(h1v, gf_ref[...], dhn)
        dh1_ref[...] = dh2 + dx
        dgf_ref[...] += dg

        @pl.when(i == n_tiles - 1)
        def _():
            total = jnp.sum(loss_acc[...], axis=-1, keepdims=True) * (0.5 / D_MODEL)
            loss_ref[...] = jnp.broadcast_to(total, loss_ref.shape)
            dwp_ref[...] = dwp_acc[...].astype(BF16)

    row = lambda w: pl.BlockSpec((t, w), lambda i: (i, 0))
    chunked = pl.BlockSpec((N_FF_CHUNKS, t, FF_CHUNK), lambda i: (0, i, 0))
    vec = _full((1, D_MODEL))
    act_shape = jax.ShapeDtypeStruct((N_FF_CHUNKS, s_len, FF_CHUNK), BF16)
    tok = lambda dtype: jax.ShapeDtypeStruct((s_len, D_MODEL), dtype)
    return pl.pallas_call(
        body, name="ffn_ple", grid=(n_tiles,),
        in_specs=[row(D_MODEL), row(D_MODEL), row(PLE_DIM), row(D_MODEL)] + W_SPECS + [vec, vec],
        out_specs=[_full((1, 128)), chunked, chunked, chunked] + [row(D_MODEL)] * 3 + [_full((PLE_DIM, D_MODEL)), row(D_MODEL),
                                                                                       vec, vec],
        out_shape=[jax.ShapeDtypeStruct((1, 128), F32), act_shape, act_shape, act_shape, tok(BF16), tok(BF16), tok(BF16),
                   jax.ShapeDtypeStruct((PLE_DIM, D_MODEL), BF16), tok(F32), jax.ShapeDtypeStruct((1, D_MODEL), F32),
                   jax.ShapeDtypeStruct((1, D_MODEL), F32)],
        scratch_shapes=[pltpu.VMEM((D_FF, D_MODEL), BF16)] * 3
        + [pltpu.VMEM((D_MODEL, D_MODEL), BF16), pltpu.VMEM((PLE_DIM, D_MODEL), BF16), pltpu.VMEM((PLE_DIM, D_MODEL), BF16),
           pltpu.VMEM((t, D_FF), F32), pltpu.VMEM((t, D_FF), F32), pltpu.VMEM((1, D_MODEL), F32),
           pltpu.VMEM((PLE_DIM, D_MODEL), F32), pltpu.SemaphoreType.DMA((N_CHIPS,))],
        compiler_params=_params(VMEM_LIMIT_BIG),
    )(hn2, h1, p2, tgt, *wts, g_ffn, g_ple)


def _flush_chunks(acc_ref, stage_ref, slab_ref, name, sems):
    stage_ref[...] = acc_ref[...].astype(BF16)
    off, rows = SLAB[name]
    copies = [pltpu.make_async_copy(stage_ref.at[pl.ds(j * rows, rows), :], slab_ref.at[j, pl.ds(off, rows), :], sems.at[j])
              for j in range(N_CHIPS)]
    for cp in copies:
        cp.start()
    for cp in copies:
        cp.wait()


def _mix_out_bwd(dh1, wts, pooled, wpool, pool_scale, mix, after):
    s_len = dh1.shape[0]
    t = 512
    n = t + 16
    n_tiles = s_len // t
    early_rows = GATHER_PARTS[0][1]

    def body(dh1_ref, sl_ref, lo_ref, me_ref, pooled_ref, wp_ref, sc_ref, mix_ref, after_ref, dost_ref, du_ref, dwp_ref,
             dsc_ref, slab_ref, w_ref, ext_ref, st_ref, acc_ref, stage_ref, sems):
        del after_ref
        i = pl.program_id(0)

        @pl.when(i == 0)
        def _():
            _load_rows((sl_ref, lo_ref, me_ref), "out", w_ref, sems)
            ext_ref[...] = jnp.zeros_like(ext_ref)
            st_ref[...] = jnp.zeros_like(st_ref)
            dsc_ref[...] = jnp.zeros_like(dsc_ref)
            dwp_ref[...] = jnp.zeros_like(dwp_ref)
            acc_ref[...] = jnp.zeros_like(acc_ref)

        dh1b = dh1_ref[...].astype(BF16)
        acc_ref[...] += _dot(mix_ref[...], dh1b, 0, 0)
        dmix = _dot(dh1b, w_ref[...], 1, 1)
        lo = lax.broadcasted_iota(jnp.int32, (t, 128), 1) < 64
        for j, entry in enumerate(_pack_heads([dmix[:, 128 * p:128 * p + 128] for p in range(4)], lo)):
            dost_ref[j] = entry.astype(BF16)
        pooled_v = pooled_ref[...]
        counts = _pool_counts(n_tiles - 1 - i, t)
        for g in range(4):
            cols = slice(128 * g, 128 * g + 128)
            dm = dmix[:, ATTN_WIDTH + 128 * g:ATTN_WIDTH + 128 * g + 128]
            ypre = _dot(pooled_v[:, cols], wp_ref[g], 1, 0)
            dsc_ref[:, cols] += jnp.sum(ypre * dm, axis=0, keepdims=True)
            dyp = (dm * sc_ref[:, cols]).astype(BF16)
            dwp_ref[g] += _dot(pooled_v[:, cols], dyp, 0, 0)
            dpooled = _dot(dyp, wp_ref[g], 1, 1)
            du_ref[:, cols] = -dpooled
            ext_ref[pl.ds(0, t), cols] = dpooled / counts[:, cols]
        st_ref[pl.ds(0, n), :] = ext_ref[pl.ds(0, n), :] + ext_ref[pl.ds(1, n), :]
        st_ref[pl.ds(0, n), 128:] = st_ref[pl.ds(0, n), 128:] + st_ref[pl.ds(2, n), 128:]
        st_ref[pl.ds(0, n), 256:] = st_ref[pl.ds(0, n), 256:] + st_ref[pl.ds(4, n), 256:]
        st_ref[pl.ds(0, n), 384:] = st_ref[pl.ds(0, n), 384:] + st_ref[pl.ds(8, n), 384:]
        ext_ref[pl.ds(t, POOL_HALO), :] = ext_ref[pl.ds(0, POOL_HALO), :]
        du_ref[...] += st_ref[pl.ds(0, t), :]

        @pl.when(i == n_tiles - 1)
        def _():
            _flush_chunks(acc_ref, stage_ref, slab_ref, "out", sems)

    rev = lambda w: pl.BlockSpec((t, w), lambda i: (n_tiles - 1 - i, 0))
    return pl.pallas_call(
        body, name="mix_out_bwd", grid=(n_tiles,),
        in_specs=[rev(D_MODEL)] + W_SPECS + [rev(POOL_WIDTH), _full((4, 128, 128)), _full((1, POOL_WIDTH)), rev(D_MODEL), ANY],
        out_specs=[pl.BlockSpec((4, t, 128), lambda i: (0, n_tiles - 1 - i, 0)), rev(POOL_WIDTH),
                   _full((4, 128, 128)), _full((1, POOL_WIDTH)), ANY],
        out_shape=[jax.ShapeDtypeStruct((4, s_len, 128), BF16), jax.ShapeDtypeStruct((s_len, POOL_WIDTH), F32),
                   jax.ShapeDtypeStruct((4, 128, 128), F32), jax.ShapeDtypeStruct((1, POOL_WIDTH), F32),
                   jax.ShapeDtypeStruct((N_CHIPS, early_rows, D_MODEL), BF16)],
        scratch_shapes=[pltpu.VMEM((D_MODEL, D_MODEL), BF16), pltpu.VMEM((t + POOL_HALO, POOL_WIDTH), F32),
                        pltpu.VMEM((t + POOL_HALO, POOL_WIDTH), F32), pltpu.VMEM((D_MODEL, D_MODEL), F32),
                        pltpu.VMEM((D_MODEL, D_MODEL), BF16), pltpu.SemaphoreType.DMA((N_CHIPS,))],
        compiler_params=_params(),
    )(dh1, *wts, pooled, wpool, pool_scale, mix, after)


def _attn_bwd(qst, kn, vb, dost, bias_st, sinks, after):
    s_len = kn.shape[0]

    def body(q_ref, kp_ref, kc_ref, vp_ref, vc_ref, do_ref, bias_ref, sink_ref, after_ref, dq_ref, dk_ref, dv_ref, dbias_ref,
             dsink_ref, s_ref, dp_ref, p_ref, dl_ref):
        del after_ref
        i = pl.program_id(0)

        @pl.when(i == 0)
        def _():
            dk_ref[...] = jnp.zeros_like(dk_ref)
            dv_ref[...] = jnp.zeros_like(dv_ref)
            dbias_ref[...] = jnp.zeros_like(dbias_ref)
            dsink_ref[...] = jnp.zeros_like(dsink_ref)

        for b, (rows, k2, v2, bias) in enumerate(_step_blocks(i, kp_ref, kc_ref, vp_ref, vc_ref, bias_ref)):
            s_b, dp_b, p_b, dl_b = s_ref.at[b], dp_ref.at[b], p_ref.at[b], dl_ref.at[b]
            q = _expand_heads(q_ref[:, rows, :])
            do = _expand_heads(do_ref[:, rows, :])
            s_b[...] = _dot(q, k2, 1, 1)
            dp_b[...] = _dot(do, v2, 1, 1)

            def head(h, carry):
                head_rows, probs, p_sink = _head_softmax(s_b, bias, sink_ref, h)
                dp = dp_b[head_rows, :]
                dsum = jnp.sum(probs * dp, axis=-1, keepdims=True)
                dlog = probs * (dp - dsum)
                dsink_ref[head_rows, :] -= p_sink * dsum
                dbias_ref[head_rows, :] += dlog
                p_b[head_rows, :] = probs.astype(BF16)
                dl_b[head_rows, :] = (dlog * (HEAD_DIM ** -0.5)).astype(BF16)
                return carry

            lax.fori_loop(0, N_Q_HEADS, head, 0, unroll=True)
            dlog_s = dl_b[...]
            dq_ref[:, rows, :] = _fold_heads(_dot(dlog_s, k2, 1, 0))
            dk2 = _dot(dlog_s, q, 0, 0)
            dv2 = _dot(p_b[...], do, 0, 0)
            block = ATTN_STEP_BLOCKS * i + b
            prev_rows = pl.ds(pl.multiple_of(jnp.maximum(block - 1, 0) * BLOCK, BLOCK), BLOCK)
            cur_rows = pl.ds(pl.multiple_of(block * BLOCK, BLOCK), BLOCK)
            dk_ref[prev_rows, :] += dk2[:BLOCK]
            dk_ref[cur_rows, :] += dk2[BLOCK:]
            dv_ref[prev_rows, :] += dv2[:BLOCK]
            dv_ref[cur_rows, :] += dv2[BLOCK:]

    stacked, kv, consts = _attn_specs()
    per_step = (ATTN_STEP_BLOCKS,) + BAND
    return pl.pallas_call(
        body, name="attn_bwd", grid=(s_len // (ATTN_STEP_BLOCKS * BLOCK),),
        in_specs=[stacked] + kv + kv + [stacked] + consts + [ANY],
        out_specs=[stacked, _full((s_len, 128)), _full((s_len, 128)), _full(BAND), _full((N_Q_HEADS * BLOCK, 1))],
        out_shape=[jax.ShapeDtypeStruct((4, s_len, 128), F32), jax.ShapeDtypeStruct((s_len, 128), F32),
                   jax.ShapeDtypeStruct((s_len, 128), F32), jax.ShapeDtypeStruct(BAND, F32),
                   jax.ShapeDtypeStruct((N_Q_HEADS * BLOCK, 1), F32)],
        scratch_shapes=[pltpu.VMEM(per_step, F32), pltpu.VMEM(per_step, F32), pltpu.VMEM(per_step, BF16),
                        pltpu.VMEM(per_step, BF16)],
        compiler_params=_params(),
    )(qst, kn, kn, vb, vb, dost, bias_st, sinks, after)


def _flip_rows(x):
    n = x.shape[0]
    exchange = (lax.broadcasted_iota(jnp.int32, (n, n), 0) + lax.broadcasted_iota(jnp.int32, (n, n), 1) == n - 1)
    exchange = jnp.where(exchange, 1.0, 0.0).astype(BF16)
    flipped, rest = None, x
    for _ in range(3):
        term = rest.astype(BF16)
        rest = rest - term.astype(F32)
        part = _dot(exchange, term, 1, 0)
        flipped = part if flipped is None else flipped + part
    return flipped


def _small_pack(dg_attn, dg_ffn, dg_ple, dscale, dgq, dgk, dbias, dsink_rows, loss_v):
    def body(ga_ref, gf_ref, gp_ref, sc_ref, gq_ref, gk_ref, db_ref, ds_ref, bucket_ref, loss_ref, out_ref):
        out_ref[...] = jnp.zeros((SMALL_ROWS, 128), F32)
        for name, ref, n in (("g_attn", ga_ref, 8), ("g_ffn", gf_ref, 8), ("g_ple", gp_ref, 8), ("pool_scale", sc_ref, 4)):
            for k in range(n):
                out_ref[pl.ds(SMALL[name] + k, 1), :] = ref[:, 128 * k:128 * k + 128]
        for name, ref in (("g_q", gq_ref), ("g_k", gk_ref)):
            both = ref[...]
            out_ref[pl.ds(SMALL[name], 1), :] = both + pltpu.roll(both, 64, axis=1)
        out_ref[pl.ds(SMALL["loss"], 1), :] = loss_ref[...]
        by_diagonal = lambda flipped: pltpu.roll(flipped, 0, 1, stride=1, stride_axis=0)
        bucket_of = jnp.max(by_diagonal(bucket_ref[...]), axis=0, keepdims=True)
        sums = jnp.concatenate([jnp.sum(by_diagonal(_flip_rows(db_ref[pl.ds(h * BLOCK, BLOCK), :])), axis=0, keepdims=True)
                                for h in range(N_Q_HEADS)], axis=0)
        lanes = lax.broadcasted_iota(jnp.int32, (N_Q_HEADS, 128), 1)
        lane1 = lax.broadcasted_iota(jnp.int32, (1, 128), 1)
        rb = jnp.zeros((N_Q_HEADS, 128), F32)
        for b in range(N_BUCKETS):
            rb = jnp.where(lanes == b, jnp.sum(jnp.where(bucket_of == float(b), sums, 0.0), axis=1, keepdims=True), rb)
        sk = jnp.zeros((1, 128), F32)
        for h in range(N_Q_HEADS):
            sk = jnp.where(lane1 == h, jnp.sum(ds_ref[pl.ds(h * BLOCK, BLOCK), :]), sk)
        out_ref[pl.ds(SMALL["rel_bias"], N_Q_HEADS), :] = rb
        out_ref[pl.ds(SMALL["sinks"], 1), :] = sk

    bucket = jnp.asarray(_bucket_table()[::-1].astype(np.float32))
    return pl.pallas_call(
        body, name="small_pack", in_specs=[VMEM_WHOLE] * 10, out_specs=VMEM_WHOLE,
        out_shape=jax.ShapeDtypeStruct((SMALL_ROWS, 128), F32),
    )(dg_attn, dg_ffn, dg_ple, dscale, dgq, dgk, dbias, dsink_rows, bucket, loss_v)


def _attn_in_bwd(dqst, zqk, dk, dv, du, x2, dh1, hn1, slab, wts, g_attn, gq, gk):
    s_len = x2.shape[0]
    t = 512
    n_tiles = s_len // t

    def body(dq_ref, zqk_ref, dk_ref, dv_ref, du_ref, x_ref, dh1_ref, hn_ref, slab_in_ref, sl_ref, lo_ref, me_ref, g_ref,
             gq_ref, gk_ref, dx_ref, dg_ref, dgq_ref, dgk_ref, slab_ref, w_ref, dz_ref, acc_ref, stage_ref, sems):
        del slab_in_ref
        i = pl.program_id(0)

        @pl.when(i == 0)
        def _():
            _load_rows((sl_ref, lo_ref, me_ref), "inT", w_ref, sems)
            dg_ref[...] = jnp.zeros_like(dg_ref)
            dgq_ref[...] = jnp.zeros_like(dgq_ref)
            dgk_ref[...] = jnp.zeros_like(dgk_ref)
            acc_ref[...] = jnp.zeros_like(acc_ref)

        lo = lax.broadcasted_iota(jnp.int32, (t, 128), 1) < 64
        for p, dqn in enumerate(_unpack_heads([dq_ref[j] for j in range(4)], lo)):
            dq_raw, dgq = _pair_norm_bwd(zqk_ref[:, 128 * p:128 * p + 128], gq_ref[...], dqn)
            dz_ref[:, 128 * p:128 * p + 128] = dq_raw.astype(BF16)
            dgq_ref[...] += dgq
        dk_raw, dgk = _pair_norm_bwd(zqk_ref[:, 512:640], gk_ref[...], dk_ref[...])
        dgk_ref[...] += dgk
        dz_ref[:, 512:640] = dk_raw.astype(BF16)
        dz_ref[:, 640:768] = dv_ref[...].astype(BF16)
        dz_ref[:, 768:] = du_ref[...].astype(BF16)
        dz = dz_ref[...]
        acc_ref[...] += _dot(dz, hn_ref[...], 0, 0)
        dx, dg = _rms_bwd(x_ref[...], g_ref[...], _dot(dz, w_ref[...], 1, 0))
        dx_ref[...] = dh1_ref[...] + dx
        dg_ref[...] += dg

        @pl.when(i == n_tiles - 1)
        def _():
            _flush_chunks(acc_ref, stage_ref, slab_ref, "inT", sems)

    row = lambda w: pl.BlockSpec((t, w), lambda i: (i, 0))
    return pl.pallas_call(
        body, name="attn_in_bwd", grid=(n_tiles,),
        in_specs=[pl.BlockSpec((4, t, 128), lambda i: (0, i, 0)), row(640), row(128), row(128), row(POOL_WIDTH),
                  row(D_MODEL), row(D_MODEL), row(D_MODEL), ANY] + W_SPECS + [_full((1, D_MODEL)), _full((1, 128)),
                                                                              _full((1, 128))],
        out_specs=[row(D_MODEL), _full((1, D_MODEL)), _full((1, 128)), _full((1, 128)), ANY],
        out_shape=[jax.ShapeDtypeStruct((s_len, D_MODEL), F32), jax.ShapeDtypeStruct((1, D_MODEL), F32),
                   jax.ShapeDtypeStruct((1, 128), F32), jax.ShapeDtypeStruct((1, 128), F32),
                   jax.ShapeDtypeStruct(slab.shape, BF16)],
        input_output_aliases={8: 4},
        scratch_shapes=[pltpu.VMEM((IN_WIDTH, D_MODEL), BF16), pltpu.VMEM((t, IN_WIDTH), BF16),
                        pltpu.VMEM((IN_WIDTH, D_MODEL), F32), pltpu.VMEM((IN_WIDTH, D_MODEL), BF16),
                        pltpu.SemaphoreType.DMA((N_CHIPS,))],
        compiler_params=_params(),
    )(dqst, zqk, dk, dv, du, x2, dh1, hn1, slab, *wts, g_attn, gq, gk)


def _dw(lefts, b, name, slab, slab_rows, row_offs):
    a0, n_a = lefts[0], len(lefts)
    assert b.shape[1] == D_MODEL
    if a0.ndim == 3:
        n_chunks, s_len, tm = a0.shape
        m = n_chunks * tm
    else:
        s_len, tm = a0.shape
        m = tm
    tk = 2048 if n_a * tm <= 1408 else 1024
    if a0.ndim == 3:
        a_spec = pl.BlockSpec((None, tk, tm), lambda i, k: (i, k, 0))
    else:
        a_spec = pl.BlockSpec((tk, tm), lambda i, k: (k, i))
    n_steps, n_tiles = s_len // tk, m // tm
    chunk = m // N_CHIPS
    per_tile = tm // chunk

    def body(*refs):
        a_refs, b_ref = refs[:n_a], refs[n_a]
        o_ref, acc_ref, stage_ref, sems = refs[-4:]
        i, k = pl.program_id(0), pl.program_id(1)
        b_tile = b_ref[...].astype(BF16)
        products = lambda: [_dot(a_ref[...].astype(BF16), b_tile, 0, 0) for a_ref in a_refs]

        @pl.when(k == 0)
        def _():
            for w, product in enumerate(products()):
                acc_ref[w] = product

        @pl.when(k > 0)
        def _():
            for w, product in enumerate(products()):
                acc_ref[w] += product

        def out_copies(tile, slot):
            return [pltpu.make_async_copy(stage_ref.at[slot, w, pl.ds(jj * chunk, chunk), :],
                                          o_ref.at[tile * per_tile + jj, pl.ds(row_offs[w], chunk), :], sems.at[slot, w, jj])
                    for w in range(n_a) for jj in range(per_tile)]

        @pl.when(k == n_steps - 1)
        def _():
            slot = i % 2

            @pl.when(i >= 2)
            def _():
                for cp in out_copies(i - 2, slot):
                    cp.wait()

            stage_ref[slot] = acc_ref[...].astype(BF16)
            for cp in out_copies(i, slot):
                cp.start()

            @pl.when(i == n_tiles - 1)
            def _():
                for cp in out_copies(i, slot):
                    cp.wait()
                if n_tiles > 1:
                    for cp in out_copies(i - 1, 1 - slot):
                        cp.wait()

    in_specs = [a_spec] * n_a + [pl.BlockSpec((tk, D_MODEL), lambda i, k: (k, 0))]
    operands, aliases = [*lefts, b], {}
    if slab is not None:
        in_specs.append(ANY)
        operands.append(slab)
        aliases = {n_a + 1: 0}
    return pl.pallas_call(
        body, name=name, grid=(n_tiles, n_steps), in_specs=in_specs, out_specs=ANY,
        out_shape=jax.ShapeDtypeStruct((N_CHIPS, slab_rows, D_MODEL), BF16), input_output_aliases=aliases,
        scratch_shapes=[pltpu.VMEM((n_a, tm, D_MODEL), F32), pltpu.VMEM((2, n_a, tm, D_MODEL), BF16),
                        pltpu.SemaphoreType.DMA((2, n_a, per_tile))],
        compiler_params=_params(VMEM_LIMIT_BIG, n_axes=2),
    )(*operands)


def _position():
    x, y, c = lax.axis_index("x"), lax.axis_index("y"), lax.axis_index("c")
    other_chips = [(1 - x, y), (x, 1 - y), (1 - x, 1 - y)]
    return x, y, c, other_chips


def _ag_weights(local_slab, row0, n_rows, name, collective_id):
    half = n_rows // 2
    quarter = half // 2
    assert quarter % 16 == 0

    def body(l_ref, g_ref, send, recv):
        x, y, c, chips = _position()
        me, (via_x, via_y, diagonal) = 2 * x + y, [2 * chip[0] + chip[1] for chip in chips]
        here, sibling, x_nbr, y_nbr = (x, y, c), (x, y, 1 - c), (1 - x, y, c), (x, 1 - y, c)
        peers = [sibling, x_nbr, y_nbr]
        barrier = pltpu.get_barrier_semaphore()
        for peer in peers:
            pl.semaphore_signal(barrier, inc=1, device_id=peer, device_id_type=MESH)
        pl.semaphore_wait(barrier, len(peers))

        def rows(core, part):
            start, size = (core * half, half) if part is None else (core * half + part * quarter, quarter)
            return pl.ds(pl.multiple_of(start, 16), size)

        def copy(k, chip_idx, where, to, src=None):
            dst = g_ref.at[chip_idx, where, :]
            return pltpu.make_async_remote_copy(src_ref=dst if src is None else src, dst_ref=dst, send_sem=send.at[k],
                                                recv_sem=recv.at[k], device_id=to, device_id_type=MESH)

        own_rows = l_ref.at[pl.ds(pl.multiple_of(row0 + c * half, 16), half), :]
        started = [copy(0, me, rows(c, None), x_nbr, src=own_rows), copy(1, me, rows(c, None), y_nbr, src=own_rows)]
        for cp in started:
            cp.start()
        after_arrival = [
            (copy(0, via_x, rows(c, None), here), [copy(4, via_x, rows(c, None), sibling), copy(3, via_x, rows(c, 1), y_nbr)]),
            (copy(1, via_y, rows(c, None), here), [copy(5, via_y, rows(c, None), sibling), copy(2, via_y, rows(c, 0), x_nbr)]),
            (copy(2, diagonal, rows(c, 0), here), [copy(6, diagonal, rows(c, 0), sibling)]),
            (copy(3, diagonal, rows(c, 1), here), [copy(7, diagonal, rows(c, 1), sibling)]),
        ]
        for arrival, onward in after_arrival:
            arrival.wait_recv()
            for cp in onward:
                cp.start()
            started += onward
        for cp in (copy(4, via_x, rows(1 - c, None), here), copy(5, via_y, rows(1 - c, None), here),
                   copy(6, diagonal, rows(1 - c, 0), here), copy(7, diagonal, rows(1 - c, 1), here)):
            cp.wait_recv()
        for cp in started:
            cp.wait_send()

    return pl.kernel(
        body, out_type=jax.ShapeDtypeStruct((N_CHIPS, n_rows, D_MODEL), BF16),
        mesh=plsc.ScalarSubcoreMesh(axis_name="sequencer", num_cores=1), name=name,
        scratch_types=[pltpu.SemaphoreType.DMA((8,)), pltpu.SemaphoreType.DMA((8,))],
        compiler_params=pltpu.CompilerParams(collective_id=collective_id),
    )(local_slab)


def _comm_call(body, peers_of, out_shape, n_sems, operand, name, collective_id):
    sems = [pltpu.SemaphoreType.DMA((n_sems,)), pltpu.SemaphoreType.DMA((n_sems,))]

    def with_handshake(in_ref, out_ref, send, recv):
        x, y, c, _ = _position()
        peers = peers_of(x, y, c)
        barrier = pltpu.get_barrier_semaphore()
        for peer in peers:
            pl.semaphore_signal(barrier, inc=1, device_id=peer, device_id_type=MESH)
        pl.semaphore_wait(barrier, len(peers))
        body(in_ref, out_ref, send, recv)

    return pl.kernel(with_handshake, out_type=out_shape, mesh=plsc.ScalarSubcoreMesh(axis_name="sequencer", num_cores=1),
                     name=name, scratch_types=sems, compiler_params=pltpu.CompilerParams(collective_id=collective_id))(operand)


def _rs_swap_halves(partial, name, collective_id):
    half = partial.shape[1] // 2

    def body(p_ref, r_ref, send, recv):
        x, y, c, _ = _position()
        theirs = pl.ds(pl.multiple_of((1 - c) * half, 16), half)
        cp = pltpu.make_async_remote_copy(src_ref=p_ref.at[:, theirs, :], dst_ref=r_ref, send_sem=send.at[0],
                                          recv_sem=recv.at[0], device_id=(x, y, 1 - c), device_id_type=MESH)
        cp.start()
        cp.wait()

    return _comm_call(body, lambda x, y, c: [(x, y, 1 - c)], jax.ShapeDtypeStruct((N_CHIPS, half, D_MODEL), BF16), 1,
                      partial, name, collective_id)


def _gather_chip_sums(s_ref, sib_ref, sum_ref, o_ref, send, recv):
    x, y, c, chips = _position()
    me, sibling, here = 2 * x + y, (x, y, 1 - c), (x, y, c)
    half = s_ref.shape[0] // 2

    def rows(core):
        return pl.ds(pl.multiple_of(core * half, 8), half)

    def copy(k, src, dst, to):
        return pltpu.make_async_remote_copy(src_ref=src, dst_ref=dst, send_sem=send.at[k], recv_sem=recv.at[k], device_id=to,
                                            device_id_type=MESH)

    swap = copy(0, s_ref, sib_ref, sibling)
    keep = pltpu.make_async_copy(sum_ref, o_ref.at[me], send.at[7])
    sends = [copy(1 + k, sum_ref.at[rows(c), :], o_ref.at[me, rows(c), :], (*chip, c)) for k, chip in enumerate(chips)]

    def landed(chip, core):
        return o_ref.at[2 * chip[0] + chip[1], rows(core), :]

    def add_and_send():
        swap.wait_recv()
        sum_ref[...] = s_ref[...] + sib_ref[...]
        keep.start()
        for cp in sends:
            cp.start()

    def finish():
        passed = []
        for k, chip in enumerate(chips):
            copy(1 + k, landed(chip, c), landed(chip, c), here).wait_recv()
            fwd = copy(4 + k, landed(chip, c), landed(chip, c), sibling)
            fwd.start()
            passed.append(fwd)
        for k, chip in enumerate(chips):
            copy(4 + k, landed(chip, 1 - c), landed(chip, 1 - c), here).wait_recv()
        for cp in [swap] + sends + passed:
            cp.wait_send()
        keep.wait()

    return swap.start, add_and_send, finish


def _rs_add_halves(partial, other, core, name, after, small=None):
    half = other.shape[1]
    t = half // 2
    steps = half // t

    def body(core_ref, a_ref, b_ref, after_ref, *rest):
        del after_ref
        o_ref = rest[0] if small is None else rest[1]
        if small is not None:
            small_ref, _, t_ref, sib_ref, sum_ref, t_send, t_recv = rest
            swap, add_and_send, finish_tables = _gather_chip_sums(small_ref, sib_ref, sum_ref, t_ref, t_send, t_recv)
            step = pl.program_id(0) * steps + pl.program_id(1)
            pl.when(step == 0)(swap)
            pl.when(step == 1)(add_and_send)
        o_ref[...] = (a_ref[...].astype(F32) + b_ref[...].astype(F32)).astype(BF16)
        if small is not None:
            pl.when(step == N_CHIPS * steps - 1)(finish_tables)

    t_in, t_out, t_scratch = [], [], []
    if small is not None:
        t_in, t_out = [VMEM_WHOLE], [jax.ShapeDtypeStruct((N_CHIPS, *small.shape), F32)]
        t_scratch = [pltpu.VMEM(small.shape, F32)] * 2 + [pltpu.SemaphoreType.DMA((8,))] * 2
    res = pl.pallas_call(
        body, name=name,
        grid_spec=pltpu.PrefetchScalarGridSpec(
            num_scalar_prefetch=1, grid=(N_CHIPS, steps),
            in_specs=[pl.BlockSpec((1, t, D_MODEL), lambda j, i, core_ref: (j, core_ref[0] * steps + i, 0)),
                      pl.BlockSpec((1, t, D_MODEL), lambda j, i, core_ref: (j, i, 0)), ANY] + t_in,
            out_specs=[pl.BlockSpec((1, t, D_MODEL), lambda j, i, core_ref: (j, i, 0))] + [ANY] * len(t_out),
            scratch_shapes=t_scratch),
        out_shape=[jax.ShapeDtypeStruct((N_CHIPS, half, D_MODEL), BF16)] + t_out,
        compiler_params=_params(n_axes=2),
    )(core, partial, other, after, *([] if small is None else [small]))
    return res[0] if small is None else res


def _rs_exchange_chips(pre, name, collective_id):
    def body(s_ref, r_ref, send, recv):
        x, y, c, chips = _position()

        def copy(k, chunk, to):
            return pltpu.make_async_remote_copy(src_ref=s_ref.at[chunk], dst_ref=r_ref.at[k], send_sem=send.at[k],
                                                recv_sem=recv.at[k], device_id=to, device_id_type=MESH)

        sends = [copy(k, 2 * chip[0] + chip[1], (*chip, c)) for k, chip in enumerate(chips)]
        for cp in sends:
            cp.start()
        for cp in sends:
            cp.wait()

    return _comm_call(body, lambda x, y, c: [(1 - x, y, c), (x, 1 - y, c), (1 - x, 1 - y, c)],
                      jax.ShapeDtypeStruct((3, pre.shape[1], D_MODEL), BF16), 3, pre, name, collective_id)


def _gather_small(s_ref, t_ref, send, recv):
    x, y, c, chips = _position()
    sibling = (x, y, 1 - c)

    def slot(px, py, pc):
        return t_ref.at[4 * px + 2 * py + pc]

    def copy(k, block, to, src=None):
        return pltpu.make_async_remote_copy(src_ref=slot(*block) if src is None else src, dst_ref=slot(*block),
                                            send_sem=send.at[k], recv_sem=recv.at[k], device_id=to, device_id_type=MESH)

    own = pltpu.make_async_copy(s_ref, slot(x, y, c), send.at[7])
    first = [copy(0, (x, y, c), sibling, src=s_ref)]
    first += [copy(1 + k, (x, y, c), (*chip, c), src=s_ref) for k, chip in enumerate(chips)]

    def start():
        own.start()
        for cp in first:
            cp.start()

    def finish():
        passed = []
        for k, chip in enumerate(chips):
            copy(1 + k, (*chip, c), (x, y, c)).wait_recv()
            fwd = copy(4 + k, (*chip, c), sibling)
            fwd.start()
            passed.append(fwd)
        copy(0, sibling, (x, y, c)).wait_recv()
        for k, chip in enumerate(chips):
            copy(4 + k, (*chip, 1 - c), (x, y, c)).wait_recv()
        for cp in first + passed:
            cp.wait_send()
        own.wait()

    return start, finish


def _rs_sum_chips(pre, received, place, name, after, small=None):
    half = pre.shape[1]
    steps = 4 if half > 512 else 2
    t = half // steps
    assert t % 16 == 0 and t * steps == half

    def body(place_ref, own_ref, r_ref, after_ref, *rest):
        del place_ref, after_ref
        if small is None:
            o_ref, stage, kept_sems, send, recv = rest
        else:
            small_ref, o_ref, t_ref, stage, kept_sems, send, recv, t_send, t_recv = rest
            start_tables, finish_tables = _gather_small(small_ref, t_ref, t_send, t_recv)
            pl.when(pl.program_id(0) == 0)(start_tables)
        i = pl.program_id(0)
        x, y, c, _ = _position()

        def rows(core, step):
            return o_ref.at[pl.ds(pl.multiple_of((core * steps + step) * t, 8), t), :]

        def kept(step):
            return pltpu.make_async_copy(stage.at[step], rows(c, step), kept_sems.at[step])

        def sent(core, step):
            return pltpu.make_async_remote_copy(src_ref=stage.at[step], dst_ref=rows(core, step), send_sem=send.at[step],
                                                recv_sem=recv.at[step], device_id=(x, y, 1 - core), device_id_type=MESH)

        acc = own_ref[0].astype(F32)
        for k in range(3):
            acc = acc + r_ref[k].astype(F32)
        stage[i] = acc
        kept(i).start()
        sent(c, i).start()

        @pl.when(i == steps - 1)
        def _():
            if small is not None:
                finish_tables()
            for step in range(steps):
                kept(step).wait()
                sent(c, step).wait_send()
                sent(1 - c, step).wait_recv()

    t_in, t_out, t_scratch = [], [], []
    if small is not None:
        t_in, t_out = [VMEM_WHOLE], [jax.ShapeDtypeStruct((N_DEV, *small.shape), F32)]
        t_scratch = [pltpu.SemaphoreType.DMA((8,))] * 2
    res = pl.pallas_call(
        body, name=name,
        grid_spec=pltpu.PrefetchScalarGridSpec(
            num_scalar_prefetch=1, grid=(steps,),
            in_specs=[pl.BlockSpec((1, t, D_MODEL), lambda i, place_ref: (place_ref[0], i, 0)),
                      pl.BlockSpec((3, t, D_MODEL), lambda i, place_ref: (0, i, 0)), ANY] + t_in,
            out_specs=[ANY] * (1 + len(t_out)),
            scratch_shapes=[pltpu.VMEM((steps, t, D_MODEL), F32)] + [pltpu.SemaphoreType.DMA((steps,))] * 3 + t_scratch),
        out_shape=[jax.ShapeDtypeStruct((2 * half, D_MODEL), F32)] + t_out, compiler_params=_params(),
    )(place, pre, received, after, *([] if small is None else [small]))
    return res[0] if small is None else res


def _adam_update(w, g, m, v):
    m_new = ADAM_B1 * m + (1.0 - ADAM_B1) * g
    v_new = ADAM_B2 * v + (1.0 - ADAM_B2) * (g * g)
    m_hat = m_new / (1.0 - ADAM_B1 ** ADAM_STEP)
    v_hat = v_new / (1.0 - ADAM_B2 ** ADAM_STEP)
    return -ADAM_LR * (m_hat / (jnp.sqrt(v_hat) + ADAM_EPS) + ADAM_WD * w), m_new, v_new


def _adamw(w, g_rows, row_off, m, v, name):
    rows, cols = w.shape
    t = rows if rows <= 320 else (rows // 2 if rows % 256 else 256)

    def body(w_ref, g_ref, m_ref, v_ref, go_ref, d_ref, nm_ref, nv_ref):
        g = g_ref[...]
        go_ref[...] = g
        d_ref[...], nm_ref[...], nv_ref[...] = _adam_update(w_ref[...], g, m_ref[...], v_ref[...])

    blk = pl.BlockSpec((t, cols), lambda i: (i, 0))
    assert row_off % 8 == 0 and t % 8 == 0
    g_blk = pl.BlockSpec((pl.Element(t), pl.Element(cols)), lambda i: (pl.multiple_of(row_off + i * t, 8), 0))
    shape = jax.ShapeDtypeStruct((rows, cols), F32)
    return pl.pallas_call(
        body, name=name, grid=(rows // t,), in_specs=[blk, g_blk, blk, blk], out_specs=[blk] * 4, out_shape=[shape] * 4,
        compiler_params=_params(),
    )(w, g_rows, m, v)


SMALL_PARAMS = [("g_attn", (1, D_MODEL), 8), ("g_q", (1, HEAD_DIM), None), ("g_k", (1, HEAD_DIM), None),
                ("sinks", (1, N_Q_HEADS), None), ("rel_bias", (N_Q_HEADS, N_BUCKETS), None), ("w_pool", (512, 128), None),
                ("pool_scale", (1, POOL_WIDTH), 4), ("g_ffn", (1, D_MODEL), 8), ("g_ple", (1, D_MODEL), 8)]


def _adamw_small(tables, pool_tables, wmv):
    n_par = len(SMALL_PARAMS)

    def body(*refs):
        t_ref, p_ref = refs[:2]
        ins = refs[2:2 + 3 * n_par]
        loss_ref = refs[2 + 3 * n_par]
        outs = refs[3 + 3 * n_par:-1]
        tot_ref = refs[-1]

        def in_order(ref):
            total = ref[0]
            for d in range(1, ref.shape[0]):
                total = total + ref[d]
            return total

        tot_ref[...] = in_order(t_ref)
        loss_ref[...] = tot_ref[pl.ds(SMALL["loss"], 1), 0:1]
        for i, (name, shape, split) in enumerate(SMALL_PARAMS):
            g_ref, d_ref, nm_ref, nv_ref = outs[4 * i:4 * i + 4]
            row = SMALL.get(name)
            if name == "w_pool":
                g_ref[...] = in_order(p_ref)
            elif split:
                for k in range(split):
                    g_ref[:, 128 * k:128 * k + 128] = tot_ref[pl.ds(row + k, 1), :]
            else:
                g_ref[...] = tot_ref[pl.ds(row, shape[0]), 0:shape[1]]
            w_ref, m_ref, v_ref = ins[3 * i:3 * i + 3]
            d_ref[...], nm_ref[...], nv_ref[...] = _adam_update(w_ref[...], g_ref[...], m_ref[...], v_ref[...])

    shapes = [jax.ShapeDtypeStruct((1, 1), F32)]
    for _, shape, _ in SMALL_PARAMS:
        shapes += [jax.ShapeDtypeStruct(shape, F32)] * 4
    flat = [a for triple in wmv for a in triple]
    res = pl.pallas_call(
        body, name="adamw_small", in_specs=[VMEM_WHOLE] * (2 + 3 * n_par), out_specs=[VMEM_WHOLE] * len(shapes),
        out_shape=shapes, scratch_shapes=[pltpu.VMEM((SMALL_ROWS, 128), F32)],
    )(tables, pool_tables, *flat)
    return res[0], [res[1 + 4 * i:5 + 4 * i] for i in range(n_par)]


def _pack_ple_proj(shard):
    return shard.reshape(4, 64, 256).transpose(1, 0, 2).reshape(64, D_MODEL)


class _Reduction:
    def __init__(self, tag, place, ids=(None, None)):
        self.tag, self.place, self.ids = tag, place, ids

    def start(self, partial):
        self.partial = partial
        self.other = _rs_swap_halves(partial, "rs_swap_" + self.tag, self.ids[0])
        return partial

    def middle(self, after, small=None):
        res = _rs_add_halves(self.partial, self.other, self.place[1:], "rs_add_" + self.tag, after, small)
        self.pre, self.tables = (res, None) if small is None else res
        self.received = _rs_exchange_chips(self.pre, "rs_exchange_" + self.tag, self.ids[1])
        return self.pre

    def finish(self, after, small=None):
        return _rs_sum_chips(self.pre, self.received, self.place, "rs_sum_" + self.tag, after, small)


def _local_grads(x2, p2, tgt, wts, g_attn_norm, g_q, g_k, attn_sinks, rel_bias, w_pool, pool_scale, g_ffn_norm, g_ple_norm,
                 reduce_a):
    w_early, w_late = wts
    w_in = w_out = w_early
    bucket = jnp.asarray(_bucket_table())
    gq = jnp.tile(g_q, (1, 2))
    gk = jnp.tile(g_k, (1, 2))
    wpool = w_pool[0].astype(BF16)
    sinks = attn_sinks[0]
    bias_st = _bias_build(rel_bias.T, bucket)

    hn1 = _first_norm(x2, g_attn_norm)
    zqk, u, kn, vb, qst = _attn_in(hn1, gq, gk, w_in)
    ost = _attn_fwd(qst, kn, vb, bias_st, sinks)
    pooled, mix, h1, hn2 = _mix_out(u, ost, x2, w_out, wpool, pool_scale, g_ffn_norm)
    loss_v, dgate, dup, act, dh2, hn3, dgl, dw_plp, dh1, dg_ffn, dg_ple = _ffn_ple(hn2, h1, p2, tgt, w_late, g_ffn_norm,
                                                                                      g_ple_norm)

    late0, late_rows = GATHER_PARTS[1][0], SLAB_ROWS - GATHER_PARTS[1][0]
    partial_a = None
    for names, lefts, right in ((("gateT", "upT"), [dgate, dup], hn2), (("down",), [act], dh2), (("plg",), [hn3], dgl)):
        partial_a = _dw(lefts, right, "dw_" + names[0], partial_a, late_rows, [SLAB[name][0] - late0 for name in names])
    dw_plp = dw_plp.reshape(4, 64, N_CHIPS, 256).transpose(2, 1, 0, 3).reshape(N_CHIPS, 64, D_MODEL)
    partial_a = reduce_a.start(lax.dynamic_update_slice(partial_a, dw_plp, (0, SLAB["plp"][0] - late0, 0)))
    dost, du, dw_pool, dscale, partial_b = _mix_out_bwd(dh1, w_out, pooled, wpool, pool_scale, mix, partial_a)
    pre_a = reduce_a.middle(du, dw_pool.reshape(512, 128))
    dqst, dk, dv, dbias, dsink_rows = _attn_bwd(qst, kn, vb, dost, bias_st, sinks, pre_a)
    dx, dg_attn, dgq, dgk, partial_b = _attn_in_bwd(dqst, zqk, dk, dv, du, x2, dh1, hn1, partial_b, w_in, g_attn_norm, gq, gk)

    small = _small_pack(dg_attn, dg_ffn, dg_ple, dscale, dgq, dgk, dbias, dsink_rows, loss_v)
    return dx, partial_b, small


def kernel(x, p, w_in, w_out, g_attn_norm, g_q, g_k, attn_sinks, rel_bias, w_pool, pool_scale, g_ffn_norm, w_gate, w_up, w_down, g_ple_norm, w_ple_gate, w_ple_proj, loss_target, m_w_in, m_w_out, m_g_attn_norm, m_g_q, m_g_k, m_attn_sinks, m_rel_bias, m_w_pool, m_pool_scale, m_g_ffn_norm, m_w_gate, m_w_up, m_w_down, m_g_ple_norm, m_w_ple_gate, m_w_ple_proj, v_w_in, v_w_out, v_g_attn_norm, v_g_q, v_g_k, v_attn_sinks, v_rel_bias, v_w_pool, v_pool_scale, v_g_ffn_norm, v_w_gate, v_w_up, v_w_down, v_g_ple_norm, v_w_ple_gate, v_w_ple_proj):
    core = lax.axis_index("c").astype(jnp.int32).reshape(1)
    me = (2 * lax.axis_index("x") + lax.axis_index("y")).astype(jnp.int32).reshape(1)

    local_parts = [jnp.concatenate(pieces, axis=0).astype(BF16) for pieces in (
        [w_in[0].T, w_out[0]], [w_gate[0].T, w_up[0].T, w_down[0], w_ple_gate[0], _pack_ple_proj(w_ple_proj[0])])]
    wts = [(_ag_weights(local, 0, local.shape[0], name, collective_id), local, me)
           for local, name, collective_id in zip(local_parts, ("ag_early", "ag_late"), (1, 2))]

    place = jnp.concatenate([me, core])
    reduce_a = _Reduction("a", place, ids=(3, 4))
    dx, partial_b, small = _local_grads(x[0], p[0, 0], loss_target[0], wts, g_attn_norm, g_q, g_k, attn_sinks, rel_bias,
                                        w_pool, pool_scale, g_ffn_norm, g_ple_norm, reduce_a)
    reduce_b = _Reduction("b", place, ids=(6, 7))
    reduce_b.start(partial_b)
    grads_a, small_all = reduce_a.finish(partial_b, small)
    reduce_b.middle(grads_a)

    late0 = GATHER_PARTS[1][0]

    def rows(name):
        return grads_a, SLAB[name][0] - late0

    plp_rows = grads_a[SLAB["plp"][0] - late0:]
    big = {
        "w_gate": (w_gate, m_w_gate, v_w_gate, rows("gateT"), True),
        "w_up": (w_up, m_w_up, v_w_up, rows("upT"), True),
        "w_down": (w_down, m_w_down, v_w_down, rows("down"), False),
        "w_ple_gate": (w_ple_gate, m_w_ple_gate, v_w_ple_gate, rows("plg"), False),
        "w_ple_proj": (w_ple_proj, m_w_ple_proj, v_w_ple_proj,
                       (plp_rows.reshape(64, 4, 256).transpose(1, 0, 2).reshape(PLE_DIM, PLE_DIM), 0), False),
        "w_out": (w_out, m_w_out, v_w_out, None, False),
        "w_in": (w_in, m_w_in, v_w_in, None, True),
    }
    small_params = {
        "g_attn_norm": (g_attn_norm, m_g_attn_norm, v_g_attn_norm), "g_q": (g_q, m_g_q, v_g_q), "g_k": (g_k, m_g_k, v_g_k),
        "attn_sinks": (attn_sinks, m_attn_sinks, v_attn_sinks), "rel_bias": (rel_bias.T, m_rel_bias.T, v_rel_bias.T),
        "w_pool": tuple(a.reshape(512, 128) for a in (w_pool, m_w_pool, v_w_pool)),
        "pool_scale": (pool_scale, m_pool_scale, v_pool_scale), "g_ffn_norm": (g_ffn_norm, m_g_ffn_norm, v_g_ffn_norm),
        "g_ple_norm": (g_ple_norm, m_g_ple_norm, v_g_ple_norm),
    }

    grads, deltas, new_ms, new_vs = {}, {}, {}, {}
    out = grads_b = None
    for name, (w, m, v, g_src, transposed) in big.items():
        if g_src is None:
            if grads_b is None:
                grads_b = reduce_b.finish(out[-1])
            g_src = (grads_b, SLAB["out" if name == "w_out" else "inT"][0])
        view = (lambda a: a.T) if transposed else (lambda a: a)
        out = _adamw(view(w[0]), *g_src, view(m[0]), view(v[0]), "adamw_" + name)
        grads[name], deltas[name], new_ms[name], new_vs[name] = (view(a)[None] for a in out)

    loss, small_out = _adamw_small(small_all, reduce_a.tables, list(small_params.values()))
    for name, (g2, d, nm, nv) in zip(small_params, small_out):
        restore = {"w_pool": lambda a: a.reshape(w_pool.shape), "rel_bias": lambda a: a.T}.get(name, lambda a: a)
        grads[name], deltas[name], new_ms[name], new_vs[name] = (restore(a) for a in (g2, d, nm, nv))

    order = ["w_in", "w_out", "g_attn_norm", "g_q", "g_k", "attn_sinks", "rel_bias", "w_pool", "pool_scale", "g_ffn_norm",
             "w_gate", "w_up", "w_down", "g_ple_norm", "w_ple_gate", "w_ple_proj"]
    return (loss.reshape(()), dx[None], *[grads[n] for n in order], *[deltas[n] for n in order],
            *[new_ms[n] for n in order], *[new_vs[n] for n in order])
```

```python
import numpy as np
import jax
import jax.numpy as jnp
from jax import lax
from jax.experimental import pallas as pl
from jax.experimental.pallas import tpu as pltpu
from jax.experimental.pallas import tpu_sc as plsc

F32 = jnp.float32
BF16 = jnp.bfloat16
MESH = pl.DeviceIdType.MESH

D_MODEL = 1024
HEAD_DIM = 64
N_Q_HEADS = 8
ATTN_WIDTH = 512
POOL_WIDTH = 512
IN_WIDTH = 1280
D_FF = 2816
PLE_DIM = 256
FF_CHUNK = 1408
N_FF_CHUNKS = D_FF // FF_CHUNK
BLOCK = 128
N_BUCKETS = 32
MAX_DISTANCE = 128
EPS = 1e-6
NEG = -1e30
N_CHIPS = 4
N_DEV = 8

ADAM_LR = 0.001
ADAM_B1 = 0.9
ADAM_B2 = 0.999
ADAM_EPS = 1e-08
ADAM_WD = 0.01
ADAM_STEP = 10

SLAB = {"inT": (0, 320), "out": (320, 256), "gateT": (576, 704), "upT": (1280, 704), "down": (1984, 704),
        "plg": (2688, 256), "plp": (2944, 64)}
SLAB_ROWS = 3008
GATHER_PARTS = ((0, 576), (576, SLAB_ROWS))
POOL_HALO = 24

SMALL = {"g_attn": 0, "g_ffn": 8, "g_ple": 16, "pool_scale": 24, "g_q": 28, "g_k": 29, "sinks": 30, "loss": 31,
         "rel_bias": 32}
SMALL_ROWS = 64

VMEM_LIMIT_BIG = 60 * 1024 * 1024
VMEM_LIMIT = 48 * 1024 * 1024


def _params(vmem=VMEM_LIMIT, n_axes=1):
    return pltpu.CompilerParams(dimension_semantics=("arbitrary",) * n_axes, vmem_limit_bytes=vmem)


def _dot(a, b, ca, cb):
    return lax.dot_general(a, b, (((ca,), (cb,)), ((), ())), preferred_element_type=F32)


def _full(shape):
    return pl.BlockSpec(shape, lambda i: (0,) * len(shape))


ANY = pl.BlockSpec(memory_space=pl.ANY)
VMEM_WHOLE = pl.BlockSpec(memory_space=pltpu.VMEM)


W_SPECS = [ANY, ANY, pl.BlockSpec(memory_space=pltpu.SMEM)]


def _load_rows(w_refs, name, dst_ref, sems):
    slab_ref, local_ref, me_ref = w_refs
    off, rows = SLAB[name]
    slab_off = off - max(start for start, _ in GATHER_PARTS if start <= off)
    me = me_ref[0]
    for phase in ("start", "wait"):
        for j in range(N_CHIPS):
            dst = dst_ref.at[pl.ds(j * rows, rows), :]
            theirs = pltpu.make_async_copy(slab_ref.at[j, pl.ds(slab_off, rows), :], dst, sems.at[j])
            own = pltpu.make_async_copy(local_ref.at[pl.ds(slab_off, rows), :], dst, sems.at[j])

            @pl.when(me == j)
            def _():
                getattr(own, phase)()

            @pl.when(me != j)
            def _():
                getattr(theirs, phase)()


def _rms_fwd(x, g):
    r = lax.rsqrt(jnp.mean(x * x, axis=-1, keepdims=True) + EPS)
    return x * r * g


def _rms_bwd(x, g, dy):
    r = lax.rsqrt(jnp.mean(x * x, axis=-1, keepdims=True) + EPS)
    xn = x * r
    dyg = dy * g
    dx = r * (dyg - xn * jnp.mean(dyg * xn, axis=-1, keepdims=True))
    return dx, jnp.sum(dy * xn, axis=0, keepdims=True)


def _half_sum(v, lo):
    s_lo = jnp.sum(jnp.where(lo, v, 0.0), axis=-1, keepdims=True)
    s_hi = jnp.sum(jnp.where(lo, 0.0, v), axis=-1, keepdims=True)
    return jnp.where(lo, s_lo, s_hi)


def _half_sum_mxu(v):
    upper = lax.broadcasted_iota(jnp.int32, (128, 128), 0) < 64
    left = lax.broadcasted_iota(jnp.int32, (128, 128), 1) < 64
    ones = jnp.where(upper == left, 1.0, 0.0).astype(BF16)
    high = v.astype(BF16)
    low = (v - high.astype(F32)).astype(BF16)
    return _dot(high, ones, 1, 0) + _dot(low, ones, 1, 0)


def _pair_norm(zp, g, lo):
    r = lax.rsqrt(_half_sum(zp * zp, lo) * (1.0 / HEAD_DIM) + EPS)
    return zp * r * g


def _pair_norm_bwd(zp, g, dy):
    r = lax.rsqrt(_half_sum_mxu(zp * zp) * (1.0 / HEAD_DIM) + EPS)
    xn = zp * r
    dyg = dy * g
    dx = r * (dyg - xn * (_half_sum_mxu(dyg * xn) * (1.0 / HEAD_DIM)))
    return dx, jnp.sum(dy * xn, axis=0, keepdims=True)


def _pack_heads(pairs, lo):
    packed = [None] * 4
    for m in range(2):
        a, b = pairs[m], pairs[m + 2]
        packed[2 * m] = jnp.where(lo, a, pltpu.roll(b, 64, axis=1))
        packed[2 * m + 1] = jnp.where(lo, pltpu.roll(a, 64, axis=1), b)
    return packed


def _unpack_heads(packed, lo):
    pairs = [None] * 4
    for m in range(2):
        a, b = packed[2 * m], packed[2 * m + 1]
        pairs[m] = jnp.where(lo, a, pltpu.roll(b, 64, axis=1))
        pairs[m + 2] = jnp.where(lo, pltpu.roll(a, 64, axis=1), b)
    return pairs


def _expand_heads(packed):
    flat = packed.reshape(4 * BLOCK, 128)
    lo = lax.broadcasted_iota(jnp.int32, flat.shape, 1) < 64
    zero = jnp.zeros_like(flat)
    return jnp.concatenate([jnp.where(lo, flat, zero), jnp.where(lo, zero, flat)], axis=0)


def _fold_heads(stacked):
    half = 4 * BLOCK
    lo = lax.broadcasted_iota(jnp.int32, (half, 128), 1) < 64
    return jnp.where(lo, stacked[:half], stacked[half:]).reshape(4, BLOCK, 128)


def _sigmoid(v):
    return 1.0 / (1.0 + jnp.exp(-v))


def _pool_counts(tile, n_rows):
    t1 = tile * n_rows + lax.broadcasted_iota(jnp.int32, (n_rows, POOL_WIDTH), 0) + 1
    lane = lax.broadcasted_iota(jnp.int32, (n_rows, POOL_WIDTH), 1)
    win = jnp.where(lane < 128, 2, jnp.where(lane < 256, 4, jnp.where(lane < 384, 8, 16)))
    return jnp.minimum(t1, win).astype(F32)


def _first_norm(x2, g_attn):
    s_len = x2.shape[0]
    t = 512

    def body(x_ref, g_ref, hn_ref):
        hn_ref[...] = _rms_fwd(x_ref[...], g_ref[...]).astype(BF16)

    row = pl.BlockSpec((t, D_MODEL), lambda i: (i, 0))
    return pl.pallas_call(
        body, name="first_norm", grid=(s_len // t,), in_specs=[row, _full((1, D_MODEL))], out_specs=row,
        out_shape=jax.ShapeDtypeStruct((s_len, D_MODEL), BF16), compiler_params=_params(),
    )(x2, g_attn)


def _attn_in(hn1, gq, gk, wts):
    s_len = hn1.shape[0]
    t = 512

    def body(hn_ref, gq_ref, gk_ref, sl_ref, lo_ref, me_ref, zqk_ref, u_ref, kn_ref, v_ref, qst_ref, w_ref, sems):
        @pl.when(pl.program_id(0) == 0)
        def _():
            _load_rows((sl_ref, lo_ref, me_ref), "inT", w_ref, sems)

        z = _dot(hn_ref[...], w_ref[...], 1, 1)
        zqk_ref[...] = z[:, :640]
        u_ref[...] = z[:, 768:]
        v_ref[...] = z[:, 640:768].astype(BF16)
        lo = lax.broadcasted_iota(jnp.int32, (t, 128), 1) < 64
        kn_ref[...] = _pair_norm(z[:, 512:640], gk_ref[...], lo).astype(BF16)
        pairs = [_pair_norm(z[:, 128 * p:128 * p + 128], gq_ref[...], lo) for p in range(4)]
        for j, entry in enumerate(_pack_heads(pairs, lo)):
            qst_ref[j] = entry.astype(BF16)

    row = lambda w: pl.BlockSpec((t, w), lambda i: (i, 0))
    return pl.pallas_call(
        body, name="attn_in", grid=(s_len // t,),
        in_specs=[row(D_MODEL), _full((1, 128)), _full((1, 128))] + W_SPECS,
        out_specs=[row(640), row(POOL_WIDTH), row(128), row(128), pl.BlockSpec((4, t, 128), lambda i: (0, i, 0))],
        out_shape=[jax.ShapeDtypeStruct((s_len, 640), F32), jax.ShapeDtypeStruct((s_len, POOL_WIDTH), F32),
                   jax.ShapeDtypeStruct((s_len, 128), BF16), jax.ShapeDtypeStruct((s_len, 128), BF16),
                   jax.ShapeDtypeStruct((4, s_len, 128), BF16)],
        scratch_shapes=[pltpu.VMEM((IN_WIDTH, D_MODEL), BF16), pltpu.SemaphoreType.DMA((N_CHIPS,))],
        compiler_params=_params(),
    )(hn1, gq, gk, *wts)


def _bucket_table():
    i_idx = np.arange(BLOCK)[:, None]
    j_idx = np.arange(2 * BLOCK)[None, :]
    d = BLOCK + i_idx - j_idx
    n = np.maximum(d, 0)
    max_exact = N_BUCKETS // 2
    nf = np.maximum(n, 1).astype(np.float64)
    large = max_exact + (np.log(nf / max_exact) / np.log(MAX_DISTANCE / max_exact) * (N_BUCKETS - max_exact)).astype(np.int64)
    large = np.minimum(large, N_BUCKETS - 1)
    bucket = np.where(n < max_exact, n, large)
    return np.where((d >= 0) & (d < BLOCK), bucket, -1).astype(np.int32)


def _bias_build(rel_bias_t, bucket):
    def body(rb_ref, bucket_ref, out_ref):
        bk = bucket_ref[...]
        for h in range(N_Q_HEADS):
            acc = jnp.full((BLOCK, 2 * BLOCK), NEG, F32)
            for b in range(N_BUCKETS):
                acc = jnp.where(bk == b, rb_ref[h, b], acc)
            out_ref[0, pl.ds(h * BLOCK, BLOCK), :] = acc
            out_ref[1, pl.ds(h * BLOCK, BLOCK), :] = acc
            out_ref[1, pl.ds(h * BLOCK, BLOCK), 0:BLOCK] = jnp.full((BLOCK, BLOCK), NEG, F32)

    return pl.pallas_call(
        body, name="bias_build",
        in_specs=[pl.BlockSpec(memory_space=pltpu.SMEM), VMEM_WHOLE], out_specs=VMEM_WHOLE,
        out_shape=jax.ShapeDtypeStruct((2, N_Q_HEADS * BLOCK, 2 * BLOCK), F32),
    )(rel_bias_t, bucket)


def _head_softmax(s_ref, bias_ref, sink_ref, h):
    rows = pl.ds(pl.multiple_of(h * BLOCK, BLOCK), BLOCK)
    s = s_ref[rows, :] * (HEAD_DIM ** -0.5) + bias_ref[rows, :]
    sink = sink_ref[h]
    m = jnp.maximum(jnp.max(s, axis=-1, keepdims=True), sink)
    p = jnp.exp(s - m)
    e_sink = jnp.exp(sink - m)
    inv = 1.0 / (jnp.sum(p, axis=-1, keepdims=True) + e_sink)
    return rows, p * inv, e_sink * inv


ATTN_STEP_BLOCKS = 4
BAND = (N_Q_HEADS * BLOCK, 2 * BLOCK)


def _attn_specs():
    nb = ATTN_STEP_BLOCKS
    stacked = pl.BlockSpec((4, nb * BLOCK, 128), lambda i: (0, i, 0))
    kv = [pl.BlockSpec((BLOCK, 128), lambda i: (jnp.maximum(nb * i - 1, 0), 0)), pl.BlockSpec((nb * BLOCK, 128), lambda i: (i, 0))]
    consts = [_full((2,) + BAND), pl.BlockSpec(memory_space=pltpu.SMEM)]
    return stacked, kv, consts


def _step_blocks(i, kp_ref, kc_ref, vp_ref, vc_ref, bias_ref):
    blocks = []
    for b in range(ATTN_STEP_BLOCKS):
        if b == 0:
            k2 = jnp.concatenate([kp_ref[...], kc_ref[pl.ds(0, BLOCK), :]], axis=0)
            v2 = jnp.concatenate([vp_ref[...], vc_ref[pl.ds(0, BLOCK), :]], axis=0)
            bias = bias_ref.at[jnp.where(i == 0, 1, 0)]
        else:
            k2, v2, bias = kc_ref[pl.ds((b - 1) * BLOCK, 2 * BLOCK), :], vc_ref[pl.ds((b - 1) * BLOCK, 2 * BLOCK), :], bias_ref.at[0]
        blocks.append((pl.ds(b * BLOCK, BLOCK), k2, v2, bias))
    return blocks


def _attn_fwd(qst, kn, vb, bias_st, sinks):
    s_len = kn.shape[0]

    def body(q_ref, kp_ref, kc_ref, vp_ref, vc_ref, bias_ref, sink_ref, o_ref, s_ref, p_ref):
        for b, (rows, k2, v2, bias) in enumerate(_step_blocks(pl.program_id(0), kp_ref, kc_ref, vp_ref, vc_ref, bias_ref)):
            s_b, p_b = s_ref.at[b], p_ref.at[b]
            s_b[...] = _dot(_expand_heads(q_ref[:, rows, :]), k2, 1, 1)

            def head(h, carry):
                head_rows, probs, _ = _head_softmax(s_b, bias, sink_ref, h)
                p_b[head_rows, :] = probs.astype(BF16)
                return carry

            lax.fori_loop(0, N_Q_HEADS, head, 0, unroll=True)
            o_ref[:, rows, :] = _fold_heads(_dot(p_b[...], v2, 1, 0)).astype(BF16)

    stacked, kv, consts = _attn_specs()
    return pl.pallas_call(
        body, name="attn_fwd", grid=(s_len // (ATTN_STEP_BLOCKS * BLOCK),),
        in_specs=[stacked] + kv + kv + consts, out_specs=stacked,
        out_shape=jax.ShapeDtypeStruct((4, s_len, 128), BF16),
        scratch_shapes=[pltpu.VMEM((ATTN_STEP_BLOCKS,) + BAND, F32), pltpu.VMEM((ATTN_STEP_BLOCKS,) + BAND, BF16)],
        compiler_params=_params(),
    )(qst, kn, kn, vb, vb, bias_st, sinks)


def _mix_out(u, ost, x2, wts, wpool, pool_scale, g_ffn):
    s_len = x2.shape[0]
    t = 512
    n = t + 16

    def body(u_ref, o_ref, x_ref, sl_ref, lo_ref, me_ref, wp_ref, sc_ref, g_ref, pooled_ref, mix_ref, h1_ref, hn_ref,
             w_ref, ext_ref, st_ref, sems):
        i = pl.program_id(0)

        @pl.when(i == 0)
        def _():
            _load_rows((sl_ref, lo_ref, me_ref), "out", w_ref, sems)
            ext_ref[...] = jnp.zeros_like(ext_ref)
            st_ref[...] = jnp.zeros_like(st_ref)

        u_tile = u_ref[...]
        ext_ref[pl.ds(POOL_HALO, t), :] = u_tile
        st_ref[pl.ds(8, n), :] = ext_ref[pl.ds(8, n), :] + ext_ref[pl.ds(7, n), :]
        st_ref[pl.ds(8, n), 128:] = st_ref[pl.ds(8, n), 128:] + st_ref[pl.ds(6, n), 128:]
        st_ref[pl.ds(8, n), 256:] = st_ref[pl.ds(8, n), 256:] + st_ref[pl.ds(4, n), 256:]
        st_ref[pl.ds(8, n), 384:] = st_ref[pl.ds(8, n), 384:] + st_ref[pl.ds(0, n), 384:]
        ext_ref[pl.ds(0, POOL_HALO), :] = ext_ref[pl.ds(t, POOL_HALO), :]
        pooled = (st_ref[pl.ds(POOL_HALO, t), :] / _pool_counts(i, t) - u_tile).astype(BF16)
        pooled_ref[...] = pooled
        for g in range(4):
            cols = slice(128 * g, 128 * g + 128)
            y = _dot(pooled[:, cols], wp_ref[g], 1, 0) * sc_ref[:, cols]
            mix_ref[:, ATTN_WIDTH + 128 * g:ATTN_WIDTH + 128 * g + 128] = y.astype(BF16)
        lo = lax.broadcasted_iota(jnp.int32, (t, 128), 1) < 64
        for p, pair in enumerate(_unpack_heads([o_ref[j].astype(F32) for j in range(4)], lo)):
            mix_ref[:, 128 * p:128 * p + 128] = pair.astype(BF16)
        h1 = x_ref[...] + _dot(mix_ref[...], w_ref[...], 1, 0)
        h1_ref[...] = h1
        hn_ref[...] = _rms_fwd(h1, g_ref[...]).astype(BF16)

    row = lambda w: pl.BlockSpec((t, w), lambda i: (i, 0))
    return pl.pallas_call(
        body, name="mix_out", grid=(s_len // t,),
        in_specs=[row(POOL_WIDTH), pl.BlockSpec((4, t, 128), lambda i: (0, i, 0)), row(D_MODEL)] + W_SPECS
        + [_full((4, 128, 128)), _full((1, POOL_WIDTH)), _full((1, D_MODEL))],
        out_specs=[row(POOL_WIDTH), row(D_MODEL), row(D_MODEL), row(D_MODEL)],
        out_shape=[jax.ShapeDtypeStruct((s_len, POOL_WIDTH), BF16), jax.ShapeDtypeStruct((s_len, D_MODEL), BF16),
                   jax.ShapeDtypeStruct((s_len, D_MODEL), F32), jax.ShapeDtypeStruct((s_len, D_MODEL), BF16)],
        scratch_shapes=[pltpu.VMEM((D_MODEL, D_MODEL), BF16), pltpu.VMEM((t + POOL_HALO, POOL_WIDTH), F32),
                        pltpu.VMEM((t + POOL_HALO, POOL_WIDTH), F32), pltpu.SemaphoreType.DMA((N_CHIPS,))],
        compiler_params=_params(),
    )(u, ost, x2, *wts, wpool, pool_scale, g_ffn)


def _ffn_ple(hn2, h1, p2, tgt, wts, g_ffn, g_ple):
    s_len = h1.shape[0]
    t = 256
    n_tiles = s_len // t

    def body(hn_ref, h1_ref, p_ref, tgt_ref, sl_ref, lo_ref, me_ref, gf_ref, gp_ref,
             loss_ref, dgate_ref, dup_ref, act_ref, dh2b_ref, hn3_ref, dgl_ref, dwp_ref, dh1_ref, dgf_ref, dgp_ref,
             wg_ref, wu_ref, wd_ref, wl_ref, wp_ref, packed_ref, gate_s, up_s, loss_acc, dwp_acc, sems):
        i = pl.program_id(0)

        @pl.when(i == 0)
        def _():
            w_refs = (sl_ref, lo_ref, me_ref)
            _load_rows(w_refs, "gateT", wg_ref, sems)
            _load_rows(w_refs, "upT", wu_ref, sems)
            _load_rows(w_refs, "down", wd_ref, sems)
            _load_rows(w_refs, "plg", wl_ref, sems)
            _load_rows(w_refs, "plp", packed_ref, sems)
            for j in range(N_CHIPS):
                for q in range(4):
                    wp_ref[pl.ds(64 * q, 64), 256 * j:256 * j + 256] = packed_ref[pl.ds(64 * j, 64), 256 * q:256 * q + 256]
            loss_acc[...] = jnp.zeros_like(loss_acc)
            dwp_acc[...] = jnp.zeros_like(dwp_acc)
            dgf_ref[...] = jnp.zeros_like(dgf_ref)
            dgp_ref[...] = jnp.zeros_like(dgp_ref)

        hn = hn_ref[...]
        h1v = h1_ref[...]
        chunks = [slice(ch * FF_CHUNK, (ch + 1) * FF_CHUNK) for ch in range(N_FF_CHUNKS)]
        gate = _dot(hn, wg_ref[...], 1, 1)
        up = _dot(hn, wu_ref[...], 1, 1)
        gate_s[...] = gate
        up_s[...] = up
        act = (gate * _sigmoid(gate) * up).astype(BF16)
        for ch, cols in enumerate(chunks):
            act_ref[ch] = act[:, cols]
        h2 = h1v + _dot(act, wd_ref[...], 1, 0)
        gp = gp_ref[...]
        hn3 = _rms_fwd(h2, gp).astype(BF16)
        hn3_ref[...] = hn3
        gate2 = _sigmoid(_dot(hn3, wl_ref[...], 1, 0))
        p_tile = p_ref[...].astype(BF16)
        pp = _dot(p_tile, wp_ref[...], 1, 0)
        err = h2 + gate2 * pp - tgt_ref[...]
        loss_acc[...] += jnp.sum(err * err, axis=0, keepdims=True)
        dy = err * (1.0 / D_MODEL)
        dwp_acc[...] += _dot(p_tile, (dy * gate2).astype(BF16), 0, 0)
        dgl = (dy * pp * gate2 * (1.0 - gate2)).astype(BF16)
        dgl_ref[...] = dgl
        dx3, dg3 = _rms_bwd(h2, gp, _dot(dgl, wl_ref[...], 1, 1))
        dh2 = dy + dx3
        dgp_ref[...] += dg3
        dh2b = dh2.astype(BF16)
        dh2b_ref[...] = dh2b
        dact = _dot(dh2b, wd_ref[...], 1, 1)
        gate_v = gate_s[...]
        up_v = up_s[...]
        sg = _sigmoid(gate_v)
        dup = (dact * (gate_v * sg)).astype(BF16)
        dgate = (dact * up_v * (sg * (1.0 + gate_v * (1.0 - sg)))).astype(BF16)
        for ch, cols in enumerate(chunks):
            dup_ref[ch] = dup[:, cols]
            dgate_ref[ch] = dgate[:, cols]
        dhn = _dot(dgate, wg_ref[...], 1, 0) + _dot(dup, wu_ref[...], 1, 0)
        dx, dg = _rms_bwd(h1v, gf_ref[...], dhn)
        dh1_ref[...] = dh2 + dx
        dgf_ref[...] += dg

        @pl.when(i == n_tiles - 1)
        def _():
            total = jnp.sum(loss_acc[...], axis=-1, keepdims=True) * (0.5 / D_MODEL)
            loss_ref[...] = jnp.broadcast_to(total, loss_ref.shape)
            dwp_ref[...] = dwp_acc[...].astype(BF16)

    row = lambda w: pl.BlockSpec((t, w), lambda i: (i, 0))
    chunked = pl.BlockSpec((N_FF_CHUNKS, t, FF_CHUNK), lambda i: (0, i, 0))
    vec = _full((1, D_MODEL))
    act_shape = jax.ShapeDtypeStruct((N_FF_CHUNKS, s_len, FF_CHUNK), BF16)
    tok = lambda dtype: jax.ShapeDtypeStruct((s_len, D_MODEL), dtype)
    return pl.pallas_call(
        body, name="ffn_ple", grid=(n_tiles,),
        in_specs=[row(D_MODEL), row(D_MODEL), row(PLE_DIM), row(D_MODEL)] + W_SPECS + [vec, vec],
        out_specs=[_full((1, 128)), chunked, chunked, chunked] + [row(D_MODEL)] * 3 + [_full((PLE_DIM, D_MODEL)), row(D_MODEL),
                                                                                       vec, vec],
        out_shape=[jax.ShapeDtypeStruct((1, 128), F32), act_shape, act_shape, act_shape, tok(BF16), tok(BF16), tok(BF16),
                   jax.ShapeDtypeStruct((PLE_DIM, D_MODEL), BF16), tok(F32), jax.ShapeDtypeStruct((1, D_MODEL), F32),
                   jax.ShapeDtypeStruct((1, D_MODEL), F32)],
        scratch_shapes=[pltpu.VMEM((D_FF, D_MODEL), BF16)] * 3
        + [pltpu.VMEM((D_MODEL, D_MODEL), BF16), pltpu.VMEM((PLE_DIM, D_MODEL), BF16), pltpu.VMEM((PLE_DIM, D_MODEL), BF16),
           pltpu.VMEM((t, D_FF), F32), pltpu.VMEM((t, D_FF), F32), pltpu.VMEM((1, D_MODEL), F32),
           pltpu.VMEM((PLE_DIM, D_MODEL), F32), pltpu.SemaphoreType.DMA((N_CHIPS,))],
        compiler_params=_params(VMEM_LIMIT_BIG),
    )(hn2, h1, p2, tgt, *wts, g_ffn, g_ple)


def _flush_chunks(acc_ref, stage_ref, slab_ref, name, sems):
    stage_ref[...] = acc_ref[...].astype(BF16)
    off, rows = SLAB[name]
    copies = [pltpu.make_async_copy(stage_ref.at[pl.ds(j * rows, rows), :], slab_ref.at[j, pl.ds(off, rows), :], sems.at[j])
              for j in range(N_CHIPS)]
    for cp in copies:
        cp.start()
    for cp in copies:
        cp.wait()


def _mix_out_bwd(dh1, wts, pooled, wpool, pool_scale, mix, after):
    s_len = dh1.shape[0]
    t = 512
    n = t + 16
    n_tiles = s_len // t
    early_rows = GATHER_PARTS[0][1]

    def body(dh1_ref, sl_ref, lo_ref, me_ref, pooled_ref, wp_ref, sc_ref, mix_ref, after_ref, dost_ref, du_ref, dwp_ref,
             dsc_ref, slab_ref, w_ref, ext_ref, st_ref, acc_ref, stage_ref, sems):
        del after_ref
        i = pl.program_id(0)

        @pl.when(i == 0)
        def _():
            _load_rows((sl_ref, lo_ref, me_ref), "out", w_ref, sems)
            ext_ref[...] = jnp.zeros_like(ext_ref)
            st_ref[...] = jnp.zeros_like(st_ref)
            dsc_ref[...] = jnp.zeros_like(dsc_ref)
            dwp_ref[...] = jnp.zeros_like(dwp_ref)
            acc_ref[...] = jnp.zeros_like(acc_ref)

        dh1b = dh1_ref[...].astype(BF16)
        acc_ref[...] += _dot(mix_ref[...], dh1b, 0, 0)
        dmix = _dot(dh1b, w_ref[...], 1, 1)
        lo = lax.broadcasted_iota(jnp.int32, (t, 128), 1) < 64
        for j, entry in enumerate(_pack_heads([dmix[:, 128 * p:128 * p + 128] for p in range(4)], lo)):
            dost_ref[j] = entry.astype(BF16)
        pooled_v = pooled_ref[...]
        counts = _pool_counts(n_tiles - 1 - i, t)
        for g in range(4):
            cols = slice(128 * g, 128 * g + 128)
            dm = dmix[:, ATTN_WIDTH + 128 * g:ATTN_WIDTH + 128 * g + 128]
            ypre = _dot(pooled_v[:, cols], wp_ref[g], 1, 0)
            dsc_ref[:, cols] += jnp.sum(ypre * dm, axis=0, keepdims=True)
            dyp = (dm * sc_ref[:, cols]).astype(BF16)
            dwp_ref[g] += _dot(pooled_v[:, cols], dyp, 0, 0)
            dpooled = _dot(dyp, wp_ref[g], 1, 1)
            du_ref[:, cols] = -dpooled
            ext_ref[pl.ds(0, t), cols] = dpooled / counts[:, cols]
        st_ref[pl.ds(0, n), :] = ext_ref[pl.ds(0, n), :] + ext_ref[pl.ds(1, n), :]
        st_ref[pl.ds(0, n), 128:] = st_ref[pl.ds(0, n), 128:] + st_ref[pl.ds(2, n), 128:]
        st_ref[pl.ds(0, n), 256:] = st_ref[pl.ds(0, n), 256:] + st_ref[pl.ds(4, n), 256:]
        st_ref[pl.ds(0, n), 384:] = st_ref[pl.ds(0, n), 384:] + st_ref[pl.ds(8, n), 384:]
        ext_ref[pl.ds(t, POOL_HALO), :] = ext_ref[pl.ds(0, POOL_HALO), :]
        du_ref[...] += st_ref[pl.ds(0, t), :]

        @pl.when(i == n_tiles - 1)
        def _():
            _flush_chunks(acc_ref, stage_ref, slab_ref, "out", sems)

    rev = lambda w: pl.BlockSpec((t, w), lambda i: (n_tiles - 1 - i, 0))
    return pl.pallas_call(
        body, name="mix_out_bwd", grid=(n_tiles,),
        in_specs=[rev(D_MODEL)] + W_SPECS + [rev(POOL_WIDTH), _full((4, 128, 128)), _full((1, POOL_WIDTH)), rev(D_MODEL), ANY],
        out_specs=[pl.BlockSpec((4, t, 128), lambda i: (0, n_tiles - 1 - i, 0)), rev(POOL_WIDTH),
                   _full((4, 128, 128)), _full((1, POOL_WIDTH)), ANY],
        out_shape=[jax.ShapeDtypeStruct((4, s_len, 128), BF16), jax.ShapeDtypeStruct((s_len, POOL_WIDTH), F32),
                   jax.ShapeDtypeStruct((4, 128, 128), F32), jax.ShapeDtypeStruct((1, POOL_WIDTH), F32),
                   jax.ShapeDtypeStruct((N_CHIPS, early_rows, D_MODEL), BF16)],
        scratch_shapes=[pltpu.VMEM((D_MODEL, D_MODEL), BF16), pltpu.VMEM((t + POOL_HALO, POOL_WIDTH), F32),
                        pltpu.VMEM((t + POOL_HALO, POOL_WIDTH), F32), pltpu.VMEM((D_MODEL, D_MODEL), F32),
                        pltpu.VMEM((D_MODEL, D_MODEL), BF16), pltpu.SemaphoreType.DMA((N_CHIPS,))],
        compiler_params=_params(),
    )(dh1, *wts, pooled, wpool, pool_scale, mix, after)


def _attn_bwd(qst, kn, vb, dost, bias_st, sinks, after):
    s_len = kn.shape[0]

    def body(q_ref, kp_ref, kc_ref, vp_ref, vc_ref, do_ref, bias_ref, sink_ref, after_ref, dq_ref, dk_ref, dv_ref, dbias_ref,
             dsink_ref, s_ref, dp_ref, p_ref, dl_ref):
        del after_ref
        i = pl.program_id(0)

        @pl.when(i == 0)
        def _():
            dk_ref[...] = jnp.zeros_like(dk_ref)
            dv_ref[...] = jnp.zeros_like(dv_ref)
            dbias_ref[...] = jnp.zeros_like(dbias_ref)
            dsink_ref[...] = jnp.zeros_like(dsink_ref)

        for b, (rows, k2, v2, bias) in enumerate(_step_blocks(i, kp_ref, kc_ref, vp_ref, vc_ref, bias_ref)):
            s_b, dp_b, p_b, dl_b = s_ref.at[b], dp_ref.at[b], p_ref.at[b], dl_ref.at[b]
            q = _expand_heads(q_ref[:, rows, :])
            do = _expand_heads(do_ref[:, rows, :])
            s_b[...] = _dot(q, k2, 1, 1)
            dp_b[...] = _dot(do, v2, 1, 1)

            def head(h, carry):
                head_rows, probs, p_sink = _head_softmax(s_b, bias, sink_ref, h)
                dp = dp_b[head_rows, :]
                dsum = jnp.sum(probs * dp, axis=-1, keepdims=True)
                dlog = probs * (dp - dsum)
                dsink_ref[head_rows, :] -= p_sink * dsum
                dbias_ref[head_rows, :] += dlog
                p_b[head_rows, :] = probs.astype(BF16)
                dl_b[head_rows, :] = (dlog * (HEAD_DIM ** -0.5)).astype(BF16)
                return carry

            lax.fori_loop(0, N_Q_HEADS, head, 0, unroll=True)
            dlog_s = dl_b[...]
            dq_ref[:, rows, :] = _fold_heads(_dot(dlog_s, k2, 1, 0))
            dk2 = _dot(dlog_s, q, 0, 0)
            dv2 = _dot(p_b[...], do, 0, 0)
            block = ATTN_STEP_BLOCKS * i + b
            prev_rows = pl.ds(pl.multiple_of(jnp.maximum(block - 1, 0) * BLOCK, BLOCK), BLOCK)
            cur_rows = pl.ds(pl.multiple_of(block * BLOCK, BLOCK), BLOCK)
            dk_ref[prev_rows, :] += dk2[:BLOCK]
            dk_ref[cur_rows, :] += dk2[BLOCK:]
            dv_ref[prev_rows, :] += dv2[:BLOCK]
            dv_ref[cur_rows, :] += dv2[BLOCK:]

    stacked, kv, consts = _attn_specs()
    per_step = (ATTN_STEP_BLOCKS,) + BAND
    return pl.pallas_call(
        body, name="attn_bwd", grid=(s_len // (ATTN_STEP_BLOCKS * BLOCK),),
        in_specs=[stacked] + kv + kv + [stacked] + consts + [ANY],
        out_specs=[stacked, _full((s_len, 128)), _full((s_len, 128)), _full(BAND), _full((N_Q_HEADS * BLOCK, 1))],
        out_shape=[jax.ShapeDtypeStruct((4, s_len, 128), F32), jax.ShapeDtypeStruct((s_len, 128), F32),
                   jax.ShapeDtypeStruct((s_len, 128), F32), jax.ShapeDtypeStruct(BAND, F32),
                   jax.ShapeDtypeStruct((N_Q_HEADS * BLOCK, 1), F32)],
        scratch_shapes=[pltpu.VMEM(per_step, F32), pltpu.VMEM(per_step, F32), pltpu.VMEM(per_step, BF16),
                        pltpu.VMEM(per_step, BF16)],
        compiler_params=_params(),
    )(qst, kn, kn, vb, vb, dost, bias_st, sinks, after)


def _flip_rows(x):
    n = x.shape[0]
    exchange = (lax.broadcasted_iota(jnp.int32, (n, n), 0) + lax.broadcasted_iota(jnp.int32, (n, n), 1) == n - 1)
    exchange = jnp.where(exchange, 1.0, 0.0).astype(BF16)
    flipped, rest = None, x
    for _ in range(3):
        term = rest.astype(BF16)
        rest = rest - term.astype(F32)
        part = _dot(exchange, term, 1, 0)
        flipped = part if flipped is None else flipped + part
    return flipped


def _small_pack(dg_attn, dg_ffn, dg_ple, dscale, dgq, dgk, dbias, dsink_rows, loss_v):
    def body(ga_ref, gf_ref, gp_ref, sc_ref, gq_ref, gk_ref, db_ref, ds_ref, bucket_ref, loss_ref, out_ref):
        out_ref[...] = jnp.zeros((SMALL_ROWS, 128), F32)
        for name, ref, n in (("g_attn", ga_ref, 8), ("g_ffn", gf_ref, 8), ("g_ple", gp_ref, 8), ("pool_scale", sc_ref, 4)):
            for k in range(n):
                out_ref[pl.ds(SMALL[name] + k, 1), :] = ref[:, 128 * k:128 * k + 128]
        for name, ref in (("g_q", gq_ref), ("g_k", gk_ref)):
            both = ref[...]
            out_ref[pl.ds(SMALL[name], 1), :] = both + pltpu.roll(both, 64, axis=1)
        out_ref[pl.ds(SMALL["loss"], 1), :] = loss_ref[...]
        by_diagonal = lambda flipped: pltpu.roll(flipped, 0, 1, stride=1, stride_axis=0)
        bucket_of = jnp.max(by_diagonal(bucket_ref[...]), axis=0, keepdims=True)
        sums = jnp.concatenate([jnp.sum(by_diagonal(_flip_rows(db_ref[pl.ds(h * BLOCK, BLOCK), :])), axis=0, keepdims=True)
                                for h in range(N_Q_HEADS)], axis=0)
        lanes = lax.broadcasted_iota(jnp.int32, (N_Q_HEADS, 128), 1)
        lane1 = lax.broadcasted_iota(jnp.int32, (1, 128), 1)
        rb = jnp.zeros((N_Q_HEADS, 128), F32)
        for b in range(N_BUCKETS):
            rb = jnp.where(lanes == b, jnp.sum(jnp.where(bucket_of == float(b), sums, 0.0), axis=1, keepdims=True), rb)
        sk = jnp.zeros((1, 128), F32)
        for h in range(N_Q_HEADS):
            sk = jnp.where(lane1 == h, jnp.sum(ds_ref[pl.ds(h * BLOCK, BLOCK), :]), sk)
        out_ref[pl.ds(SMALL["rel_bias"], N_Q_HEADS), :] = rb
        out_ref[pl.ds(SMALL["sinks"], 1), :] = sk

    bucket = jnp.asarray(_bucket_table()[::-1].astype(np.float32))
    return pl.pallas_call(
        body, name="small_pack", in_specs=[VMEM_WHOLE] * 10, out_specs=VMEM_WHOLE,
        out_shape=jax.ShapeDtypeStruct((SMALL_ROWS, 128), F32),
    )(dg_attn, dg_ffn, dg_ple, dscale, dgq, dgk, dbias, dsink_rows, bucket, loss_v)


def _attn_in_bwd(dqst, zqk, dk, dv, du, x2, dh1, hn1, slab, wts, g_attn, gq, gk):
    s_len = x2.shape[0]
    t = 512
    n_tiles = s_len // t

    def body(dq_ref, zqk_ref, dk_ref, dv_ref, du_ref, x_ref, dh1_ref, hn_ref, slab_in_ref, sl_ref, lo_ref, me_ref, g_ref,
             gq_ref, gk_ref, dx_ref, dg_ref, dgq_ref, dgk_ref, slab_ref, w_ref, dz_ref, acc_ref, stage_ref, sems):
        del slab_in_ref
        i = pl.program_id(0)

        @pl.when(i == 0)
        def _():
            _load_rows((sl_ref, lo_ref, me_ref), "inT", w_ref, sems)
            dg_ref[...] = jnp.zeros_like(dg_ref)
            dgq_ref[...] = jnp.zeros_like(dgq_ref)
            dgk_ref[...] = jnp.zeros_like(dgk_ref)
            acc_ref[...] = jnp.zeros_like(acc_ref)

        lo = lax.broadcasted_iota(jnp.int32, (t, 128), 1) < 64
        for p, dqn in enumerate(_unpack_heads([dq_ref[j] for j in range(4)], lo)):
            dq_raw, dgq = _pair_norm_bwd(zqk_ref[:, 128 * p:128 * p + 128], gq_ref[...], dqn)
            dz_ref[:, 128 * p:128 * p + 128] = dq_raw.astype(BF16)
            dgq_ref[...] += dgq
        dk_raw, dgk = _pair_norm_bwd(zqk_ref[:, 512:640], gk_ref[...], dk_ref[...])
        dgk_ref[...] += dgk
        dz_ref[:, 512:640] = dk_raw.astype(BF16)
        dz_ref[:, 640:768] = dv_ref[...].astype(BF16)
        dz_ref[:, 768:] = du_ref[...].astype(BF16)
        dz = dz_ref[...]
        acc_ref[...] += _dot(dz, hn_ref[...], 0, 0)
        dx, dg = _rms_bwd(x_ref[...], g_ref[...], _dot(dz, w_ref[...], 1, 0))
        dx_ref[...] = dh1_ref[...] + dx
        dg_ref[...] += dg

        @pl.when(i == n_tiles - 1)
        def _():
            _flush_chunks(acc_ref, stage_ref, slab_ref, "inT", sems)

    row = lambda w: pl.BlockSpec((t, w), lambda i: (i, 0))
    return pl.pallas_call(
        body, name="attn_in_bwd", grid=(n_tiles,),
        in_specs=[pl.BlockSpec((4, t, 128), lambda i: (0, i, 0)), row(640), row(128), row(128), row(POOL_WIDTH),
                  row(D_MODEL), row(D_MODEL), row(D_MODEL), ANY] + W_SPECS + [_full((1, D_MODEL)), _full((1, 128)),
                                                                              _full((1, 128))],
        out_specs=[row(D_MODEL), _full((1, D_MODEL)), _full((1, 128)), _full((1, 128)), ANY],
        out_shape=[jax.ShapeDtypeStruct((s_len, D_MODEL), F32), jax.ShapeDtypeStruct((1, D_MODEL), F32),
                   jax.ShapeDtypeStruct((1, 128), F32), jax.ShapeDtypeStruct((1, 128), F32),
                   jax.ShapeDtypeStruct(slab.shape, BF16)],
        input_output_aliases={8: 4},
        scratch_shapes=[pltpu.VMEM((IN_WIDTH, D_MODEL), BF16), pltpu.VMEM((t, IN_WIDTH), BF16),
                        pltpu.VMEM((IN_WIDTH, D_MODEL), F32), pltpu.VMEM((IN_WIDTH, D_MODEL), BF16),
                        pltpu.SemaphoreType.DMA((N_CHIPS,))],
        compiler_params=_params(),
    )(dqst, zqk, dk, dv, du, x2, dh1, hn1, slab, *wts, g_attn, gq, gk)


def _dw(lefts, b, name, slab, slab_rows, row_offs):
    a0, n_a = lefts[0], len(lefts)
    assert b.shape[1] == D_MODEL
    if a0.ndim == 3:
        n_chunks, s_len, tm = a0.shape
        m = n_chunks * tm
    else:
        s_len, tm = a0.shape
        m = tm
    tk = 2048 if n_a * tm <= 1408 else 1024
    if a0.ndim == 3:
        a_spec = pl.BlockSpec((None, tk, tm), lambda i, k: (i, k, 0))
    else:
        a_spec = pl.BlockSpec((tk, tm), lambda i, k: (k, i))
    n_steps, n_tiles = s_len // tk, m // tm
    chunk = m // N_CHIPS
    per_tile = tm // chunk

    def body(*refs):
        a_refs, b_ref = refs[:n_a], refs[n_a]
        o_ref, acc_ref, stage_ref, sems = refs[-4:]
        i, k = pl.program_id(0), pl.program_id(1)
        products = lambda: [_dot(a_ref[...].astype(BF16), b_ref[...].astype(BF16), 0, 0) for a_ref in a_refs]

        @pl.when(k == 0)
        def _():
            for w, product in enumerate(products()):
                acc_ref[w] = product

        @pl.when(k > 0)
        def _():
            for w, product in enumerate(products()):
                acc_ref[w] += product

        def out_copies(tile, slot):
            return [pltpu.make_async_copy(stage_ref.at[slot, w, pl.ds(jj * chunk, chunk), :],
                                          o_ref.at[tile * per_tile + jj, pl.ds(row_offs[w], chunk), :], sems.at[slot, w, jj])
                    for w in range(n_a) for jj in range(per_tile)]

        @pl.when(k == n_steps - 1)
        def _():
            slot = i % 2

            @pl.when(i >= 2)
            def _():
                for cp in out_copies(i - 2, slot):
                    cp.wait()

            stage_ref[slot] = acc_ref[...].astype(BF16)
            for cp in out_copies(i, slot):
                cp.start()

            @pl.when(i == n_tiles - 1)
            def _():
                for cp in out_copies(i, slot):
                    cp.wait()
                if n_tiles > 1:
                    for cp in out_copies(i - 1, 1 - slot):
                        cp.wait()

    in_specs = [a_spec] * n_a + [pl.BlockSpec((tk, D_MODEL), lambda i, k: (k, 0))]
    operands, aliases = [*lefts, b], {}
    if slab is not None:
        in_specs.append(ANY)
        operands.append(slab)
        aliases = {n_a + 1: 0}
    return pl.pallas_call(
        body, name=name, grid=(n_tiles, n_steps), in_specs=in_specs, out_specs=ANY,
        out_shape=jax.ShapeDtypeStruct((N_CHIPS, slab_rows, D_MODEL), BF16), input_output_aliases=aliases,
        scratch_shapes=[pltpu.VMEM((n_a, tm, D_MODEL), F32), pltpu.VMEM((2, n_a, tm, D_MODEL), BF16),
                        pltpu.SemaphoreType.DMA((2, n_a, per_tile))],
        compiler_params=_params(VMEM_LIMIT_BIG, n_axes=2),
    )(*operands)


def _position():
    x, y, c = lax.axis_index("x"), lax.axis_index("y"), lax.axis_index("c")
    other_chips = [(1 - x, y), (x, 1 - y), (1 - x, 1 - y)]
    return x, y, c, other_chips


def _ag_weights(local_slab, row0, n_rows, name, collective_id):
    half = n_rows // 2
    quarter = half // 2
    assert quarter % 16 == 0

    def body(l_ref, g_ref, send, recv):
        x, y, c, chips = _position()
        me, (via_x, via_y, diagonal) = 2 * x + y, [2 * chip[0] + chip[1] for chip in chips]
        here, sibling, x_nbr, y_nbr = (x, y, c), (x, y, 1 - c), (1 - x, y, c), (x, 1 - y, c)
        peers = [sibling, x_nbr, y_nbr]
        barrier = pltpu.get_barrier_semaphore()
        for peer in peers:
            pl.semaphore_signal(barrier, inc=1, device_id=peer, device_id_type=MESH)
        pl.semaphore_wait(barrier, len(peers))

        def rows(core, part):
            start, size = (core * half, half) if part is None else (core * half + part * quarter, quarter)
            return pl.ds(pl.multiple_of(start, 16), size)

        def copy(k, chip_idx, where, to, src=None):
            dst = g_ref.at[chip_idx, where, :]
            return pltpu.make_async_remote_copy(src_ref=dst if src is None else src, dst_ref=dst, send_sem=send.at[k],
                                                recv_sem=recv.at[k], device_id=to, device_id_type=MESH)

        own_rows = l_ref.at[pl.ds(pl.multiple_of(row0 + c * half, 16), half), :]
        started = [copy(0, me, rows(c, None), x_nbr, src=own_rows), copy(1, me, rows(c, None), y_nbr, src=own_rows)]
        for cp in started:
            cp.start()
        after_arrival = [
            (copy(0, via_x, rows(c, None), here), [copy(4, via_x, rows(c, None), sibling), copy(3, via_x, rows(c, 1), y_nbr)]),
            (copy(1, via_y, rows(c, None), here), [copy(5, via_y, rows(c, None), sibling), copy(2, via_y, rows(c, 0), x_nbr)]),
            (copy(2, diagonal, rows(c, 0), here), [copy(6, diagonal, rows(c, 0), sibling)]),
            (copy(3, diagonal, rows(c, 1), here), [copy(7, diagonal, rows(c, 1), sibling)]),
        ]
        for arrival, onward in after_arrival:
            arrival.wait_recv()
            for cp in onward:
                cp.start()
            started += onward
        for cp in (copy(4, via_x, rows(1 - c, None), here), copy(5, via_y, rows(1 - c, None), here),
                   copy(6, diagonal, rows(1 - c, 0), here), copy(7, diagonal, rows(1 - c, 1), here)):
            cp.wait_recv()
        for cp in started:
            cp.wait_send()

    return pl.kernel(
        body, out_type=jax.ShapeDtypeStruct((N_CHIPS, n_rows, D_MODEL), BF16),
        mesh=plsc.ScalarSubcoreMesh(axis_name="sequencer", num_cores=1), name=name,
        scratch_types=[pltpu.SemaphoreType.DMA((8,)), pltpu.SemaphoreType.DMA((8,))],
        compiler_params=pltpu.CompilerParams(collective_id=collective_id),
    )(local_slab)


def _comm_call(body, peers_of, out_shape, n_sems, operand, name, collective_id):
    sems = [pltpu.SemaphoreType.DMA((n_sems,)), pltpu.SemaphoreType.DMA((n_sems,))]

    def with_handshake(in_ref, out_ref, send, recv):
        x, y, c, _ = _position()
        peers = peers_of(x, y, c)
        barrier = pltpu.get_barrier_semaphore()
        for peer in peers:
            pl.semaphore_signal(barrier, inc=1, device_id=peer, device_id_type=MESH)
        pl.semaphore_wait(barrier, len(peers))
        body(in_ref, out_ref, send, recv)

    return pl.kernel(with_handshake, out_type=out_shape, mesh=plsc.ScalarSubcoreMesh(axis_name="sequencer", num_cores=1),
                     name=name, scratch_types=sems, compiler_params=pltpu.CompilerParams(collective_id=collective_id))(operand)


def _rs_swap_halves(partial, name, collective_id):
    half = partial.shape[1] // 2

    def body(p_ref, r_ref, send, recv):
        x, y, c, _ = _position()
        theirs = pl.ds(pl.multiple_of((1 - c) * half, 16), half)
        cp = pltpu.make_async_remote_copy(src_ref=p_ref.at[:, theirs, :], dst_ref=r_ref, send_sem=send.at[0],
                                          recv_sem=recv.at[0], device_id=(x, y, 1 - c), device_id_type=MESH)
        cp.start()
        cp.wait()

    return _comm_call(body, lambda x, y, c: [(x, y, 1 - c)], jax.ShapeDtypeStruct((N_CHIPS, half, D_MODEL), BF16), 1,
                      partial, name, collective_id)


def _gather_chip_sums(s_ref, sib_ref, sum_ref, o_ref, send, recv):
    x, y, c, chips = _position()
    me, sibling, here = 2 * x + y, (x, y, 1 - c), (x, y, c)
    half = s_ref.shape[0] // 2

    def rows(core):
        return pl.ds(pl.multiple_of(core * half, 8), half)

    def copy(k, src, dst, to):
        return pltpu.make_async_remote_copy(src_ref=src, dst_ref=dst, send_sem=send.at[k], recv_sem=recv.at[k], device_id=to,
                                            device_id_type=MESH)

    swap = copy(0, s_ref, sib_ref, sibling)
    keep = pltpu.make_async_copy(sum_ref, o_ref.at[me], send.at[7])
    sends = [copy(1 + k, sum_ref.at[rows(c), :], o_ref.at[me, rows(c), :], (*chip, c)) for k, chip in enumerate(chips)]

    def landed(chip, core):
        return o_ref.at[2 * chip[0] + chip[1], rows(core), :]

    def add_and_send():
        swap.wait_recv()
        sum_ref[...] = s_ref[...] + sib_ref[...]
        keep.start()
        for cp in sends:
            cp.start()

    def finish():
        passed = []
        for k, chip in enumerate(chips):
            copy(1 + k, landed(chip, c), landed(chip, c), here).wait_recv()
            fwd = copy(4 + k, landed(chip, c), landed(chip, c), sibling)
            fwd.start()
            passed.append(fwd)
        for k, chip in enumerate(chips):
            copy(4 + k, landed(chip, 1 - c), landed(chip, 1 - c), here).wait_recv()
        for cp in [swap] + sends + passed:
            cp.wait_send()
        keep.wait()

    return swap.start, add_and_send, finish


def _rs_add_halves(partial, other, core, name, after, small=None):
    half = other.shape[1]
    t = half // 2
    steps = half // t

    def body(core_ref, a_ref, b_ref, after_ref, *rest):
        del after_ref
        o_ref = rest[0] if small is None else rest[1]
        if small is not None:
            small_ref, _, t_ref, sib_ref, sum_ref, t_send, t_recv = rest
            swap, add_and_send, finish_tables = _gather_chip_sums(small_ref, sib_ref, sum_ref, t_ref, t_send, t_recv)
            step = pl.program_id(0) * steps + pl.program_id(1)
            pl.when(step == 0)(swap)
            pl.when(step == 1)(add_and_send)
        o_ref[...] = (a_ref[...].astype(F32) + b_ref[...].astype(F32)).astype(BF16)
        if small is not None:
            pl.when(step == N_CHIPS * steps - 1)(finish_tables)

    t_in, t_out, t_scratch = [], [], []
    if small is not None:
        t_in, t_out = [VMEM_WHOLE], [jax.ShapeDtypeStruct((N_CHIPS, *small.shape), F32)]
        t_scratch = [pltpu.VMEM(small.shape, F32)] * 2 + [pltpu.SemaphoreType.DMA((8,))] * 2
    res = pl.pallas_call(
        body, name=name,
        grid_spec=pltpu.PrefetchScalarGridSpec(
            num_scalar_prefetch=1, grid=(N_CHIPS, steps),
            in_specs=[pl.BlockSpec((1, t, D_MODEL), lambda j, i, core_ref: (j, core_ref[0] * steps + i, 0)),
                      pl.BlockSpec((1, t, D_MODEL), lambda j, i, core_ref: (j, i, 0)), ANY] + t_in,
            out_specs=[pl.BlockSpec((1, t, D_MODEL), lambda j, i, core_ref: (j, i, 0))] + [ANY] * len(t_out),
            scratch_shapes=t_scratch),
        out_shape=[jax.ShapeDtypeStruct((N_CHIPS, half, D_MODEL), BF16)] + t_out,
        compiler_params=_params(n_axes=2),
    )(core, partial, other, after, *([] if small is None else [small]))
    return res[0] if small is None else res


def _rs_exchange_chips(pre, name, collective_id):
    def body(s_ref, r_ref, send, recv):
        x, y, c, chips = _position()

        def copy(k, chunk, to):
            return pltpu.make_async_remote_copy(src_ref=s_ref.at[chunk], dst_ref=r_ref.at[k], send_sem=send.at[k],
                                                recv_sem=recv.at[k], device_id=to, device_id_type=MESH)

        sends = [copy(k, 2 * chip[0] + chip[1], (*chip, c)) for k, chip in enumerate(chips)]
        for cp in sends:
            cp.start()
        for cp in sends:
            cp.wait()

    return _comm_call(body, lambda x, y, c: [(1 - x, y, c), (x, 1 - y, c), (1 - x, 1 - y, c)],
                      jax.ShapeDtypeStruct((3, pre.shape[1], D_MODEL), BF16), 3, pre, name, collective_id)


def _gather_small(s_ref, t_ref, send, recv):
    x, y, c, chips = _position()
    sibling = (x, y, 1 - c)

    def slot(px, py, pc):
        return t_ref.at[4 * px + 2 * py + pc]

    def copy(k, block, to, src=None):
        return pltpu.make_async_remote_copy(src_ref=slot(*block) if src is None else src, dst_ref=slot(*block),
                                            send_sem=send.at[k], recv_sem=recv.at[k], device_id=to, device_id_type=MESH)

    own = pltpu.make_async_copy(s_ref, slot(x, y, c), send.at[7])
    first = [copy(0, (x, y, c), sibling, src=s_ref)]
    first += [copy(1 + k, (x, y, c), (*chip, c), src=s_ref) for k, chip in enumerate(chips)]

    def start():
        own.start()
        for cp in first:
            cp.start()

    def finish():
        passed = []
        for k, chip in enumerate(chips):
            copy(1 + k, (*chip, c), (x, y, c)).wait_recv()
            fwd = copy(4 + k, (*chip, c), sibling)
            fwd.start()
            passed.append(fwd)
        copy(0, sibling, (x, y, c)).wait_recv()
        for k, chip in enumerate(chips):
            copy(4 + k, (*chip, 1 - c), (x, y, c)).wait_recv()
        for cp in first + passed:
            cp.wait_send()
        own.wait()

    return start, finish


def _rs_sum_chips(pre, received, place, name, after, small=None):
    half = pre.shape[1]
    steps = 4 if half > 512 else 2
    t = half // steps
    assert t % 16 == 0 and t * steps == half

    def body(place_ref, own_ref, r_ref, after_ref, *rest):
        del place_ref, after_ref
        if small is None:
            o_ref, stage, kept_sems, send, recv = rest
        else:
            small_ref, o_ref, t_ref, stage, kept_sems, send, recv, t_send, t_recv = rest
            start_tables, finish_tables = _gather_small(small_ref, t_ref, t_send, t_recv)
            pl.when(pl.program_id(0) == 0)(start_tables)
        i = pl.program_id(0)
        x, y, c, _ = _position()

        def rows(core, step):
            return o_ref.at[pl.ds(pl.multiple_of((core * steps + step) * t, 8), t), :]

        def kept(step):
            return pltpu.make_async_copy(stage.at[step], rows(c, step), kept_sems.at[step])

        def sent(core, step):
            return pltpu.make_async_remote_copy(src_ref=stage.at[step], dst_ref=rows(core, step), send_sem=send.at[step],
                                                recv_sem=recv.at[step], device_id=(x, y, 1 - core), device_id_type=MESH)

        acc = own_ref[0].astype(F32)
        for k in range(3):
            acc = acc + r_ref[k].astype(F32)
        stage[i] = acc
        kept(i).start()
        sent(c, i).start()

        @pl.when(i == steps - 1)
        def _():
            if small is not None:
                finish_tables()
            for step in range(steps):
                kept(step).wait()
                sent(c, step).wait_send()
                sent(1 - c, step).wait_recv()

    t_in, t_out, t_scratch = [], [], []
    if small is not None:
        t_in, t_out = [VMEM_WHOLE], [jax.ShapeDtypeStruct((N_DEV, *small.shape), F32)]
        t_scratch = [pltpu.SemaphoreType.DMA((8,))] * 2
    res = pl.pallas_call(
        body, name=name,
        grid_spec=pltpu.PrefetchScalarGridSpec(
            num_scalar_prefetch=1, grid=(steps,),
            in_specs=[pl.BlockSpec((1, t, D_MODEL), lambda i, place_ref: (place_ref[0], i, 0)),
                      pl.BlockSpec((3, t, D_MODEL), lambda i, place_ref: (0, i, 0)), ANY] + t_in,
            out_specs=[ANY] * (1 + len(t_out)),
            scratch_shapes=[pltpu.VMEM((steps, t, D_MODEL), F32)] + [pltpu.SemaphoreType.DMA((steps,))] * 3 + t_scratch),
        out_shape=[jax.ShapeDtypeStruct((2 * half, D_MODEL), F32)] + t_out, compiler_params=_params(),
    )(place, pre, received, after, *([] if small is None else [small]))
    return res[0] if small is None else res


def _adam_update(w, g, m, v):
    m_new = ADAM_B1 * m + (1.0 - ADAM_B1) * g
    v_new = ADAM_B2 * v + (1.0 - ADAM_B2) * (g * g)
    m_hat = m_new / (1.0 - ADAM_B1 ** ADAM_STEP)
    v_hat = v_new / (1.0 - ADAM_B2 ** ADAM_STEP)
    return -ADAM_LR * (m_hat / (jnp.sqrt(v_hat) + ADAM_EPS) + ADAM_WD * w), m_new, v_new


def _adamw(w, g_rows, row_off, m, v, name):
    rows, cols = w.shape
    t = rows if rows <= 320 else (rows // 2 if rows % 256 else 256)

    def body(w_ref, g_ref, m_ref, v_ref, go_ref, d_ref, nm_ref, nv_ref):
        g = g_ref[...]
        go_ref[...] = g
        d_ref[...], nm_ref[...], nv_ref[...] = _adam_update(w_ref[...], g, m_ref[...], v_ref[...])

    blk = pl.BlockSpec((t, cols), lambda i: (i, 0))
    assert row_off % 8 == 0 and t % 8 == 0
    g_blk = pl.BlockSpec((pl.Element(t), pl.Element(cols)), lambda i: (pl.multiple_of(row_off + i * t, 8), 0))
    shape = jax.ShapeDtypeStruct((rows, cols), F32)
    return pl.pallas_call(
        body, name=name, grid=(rows // t,), in_specs=[blk, g_blk, blk, blk], out_specs=[blk] * 4, out_shape=[shape] * 4,
        compiler_params=_params(),
    )(w, g_rows, m, v)


SMALL_PARAMS = [("g_attn", (1, D_MODEL), 8), ("g_q", (1, HEAD_DIM), None), ("g_k", (1, HEAD_DIM), None),
                ("sinks", (1, N_Q_HEADS), None), ("rel_bias", (N_Q_HEADS, N_BUCKETS), None), ("w_pool", (512, 128), None),
                ("pool_scale", (1, POOL_WIDTH), 4), ("g_ffn", (1, D_MODEL), 8), ("g_ple", (1, D_MODEL), 8)]


def _adamw_small(tables, pool_tables, wmv):
    n_par = len(SMALL_PARAMS)

    def body(*refs):
        t_ref, p_ref = refs[:2]
        ins = refs[2:2 + 3 * n_par]
        loss_ref = refs[2 + 3 * n_par]
        outs = refs[3 + 3 * n_par:-1]
        tot_ref = refs[-1]

        def in_order(ref):
            total = ref[0]
            for d in range(1, ref.shape[0]):
                total = total + ref[d]
            return total

        tot_ref[...] = in_order(t_ref)
        loss_ref[...] = tot_ref[pl.ds(SMALL["loss"], 1), 0:1]
        for i, (name, shape, split) in enumerate(SMALL_PARAMS):
            g_ref, d_ref, nm_ref, nv_ref = outs[4 * i:4 * i + 4]
            row = SMALL.get(name)
            if name == "w_pool":
                g_ref[...] = in_order(p_ref)
            elif split:
                for k in range(split):
                    g_ref[:, 128 * k:128 * k + 128] = tot_ref[pl.ds(row + k, 1), :]
            else:
                g_ref[...] = tot_ref[pl.ds(row, shape[0]), 0:shape[1]]
            w_ref, m_ref, v_ref = ins[3 * i:3 * i + 3]
            d_ref[...], nm_ref[...], nv_ref[...] = _adam_update(w_ref[...], g_ref[...], m_ref[...], v_ref[...])

    shapes = [jax.ShapeDtypeStruct((1, 1), F32)]
    for _, shape, _ in SMALL_PARAMS:
        shapes += [jax.ShapeDtypeStruct(shape, F32)] * 4
    flat = [a for triple in wmv for a in triple]
    res = pl.pallas_call(
        body, name="adamw_small", in_specs=[VMEM_WHOLE] * (2 + 3 * n_par), out_specs=[VMEM_WHOLE] * len(shapes),
        out_shape=shapes, scratch_shapes=[pltpu.VMEM((SMALL_ROWS, 128), F32)],
    )(tables, pool_tables, *flat)
    return res[0], [res[1 + 4 * i:5 + 4 * i] for i in range(n_par)]


def _pack_ple_proj(shard):
    return shard.reshape(4, 64, 256).transpose(1, 0, 2).reshape(64, D_MODEL)


class _Reduction:
    def __init__(self, tag, place, ids=(None, None)):
        self.tag, self.place, self.ids = tag, place, ids

    def start(self, partial):
        self.partial = partial
        self.other = _rs_swap_halves(partial, "rs_swap_" + self.tag, self.ids[0])
        return partial

    def middle(self, after, small=None):
        res = _rs_add_halves(self.partial, self.other, self.place[1:], "rs_add_" + self.tag, after, small)
        self.pre, self.tables = (res, None) if small is None else res
        self.received = _rs_exchange_chips(self.pre, "rs_exchange_" + self.tag, self.ids[1])
        return self.pre

    def finish(self, after, small=None):
        return _rs_sum_chips(self.pre, self.received, self.place, "rs_sum_" + self.tag, after, small)


def _local_grads(x2, p2, tgt, wts, g_attn_norm, g_q, g_k, attn_sinks, rel_bias, w_pool, pool_scale, g_ffn_norm, g_ple_norm,
                 reduce_a):
    w_early, w_late = wts
    w_in = w_out = w_early
    bucket = jnp.asarray(_bucket_table())
    gq = jnp.tile(g_q, (1, 2))
    gk = jnp.tile(g_k, (1, 2))
    wpool = w_pool[0].astype(BF16)
    sinks = attn_sinks[0]
    bias_st = _bias_build(rel_bias.T, bucket)

    hn1 = _first_norm(x2, g_attn_norm)
    zqk, u, kn, vb, qst = _attn_in(hn1, gq, gk, w_in)
    ost = _attn_fwd(qst, kn, vb, bias_st, sinks)
    pooled, mix, h1, hn2 = _mix_out(u, ost, x2, w_out, wpool, pool_scale, g_ffn_norm)
    loss_v, dgate, dup, act, dh2, hn3, dgl, dw_plp, dh1, dg_ffn, dg_ple = _ffn_ple(hn2, h1, p2, tgt, w_late, g_ffn_norm,
                                                                                      g_ple_norm)

    late0, late_rows = GATHER_PARTS[1][0], SLAB_ROWS - GATHER_PARTS[1][0]
    partial_a = None
    for names, lefts, right in ((("gateT", "upT"), [dgate, dup], hn2), (("down",), [act], dh2), (("plg",), [hn3], dgl)):
        partial_a = _dw(lefts, right, "dw_" + names[0], partial_a, late_rows, [SLAB[name][0] - late0 for name in names])
    dw_plp = dw_plp.reshape(4, 64, N_CHIPS, 256).transpose(2, 1, 0, 3).reshape(N_CHIPS, 64, D_MODEL)
    partial_a = reduce_a.start(lax.dynamic_update_slice(partial_a, dw_plp, (0, SLAB["plp"][0] - late0, 0)))
    dost, du, dw_pool, dscale, partial_b = _mix_out_bwd(dh1, w_out, pooled, wpool, pool_scale, mix, partial_a)
    pre_a = reduce_a.middle(du, dw_pool.reshape(512, 128))
    dqst, dk, dv, dbias, dsink_rows = _attn_bwd(qst, kn, vb, dost, bias_st, sinks, pre_a)
    dx, dg_attn, dgq, dgk, partial_b = _attn_in_bwd(dqst, zqk, dk, dv, du, x2, dh1, hn1, partial_b, w_in, g_attn_norm, gq, gk)

    small = _small_pack(dg_attn, dg_ffn, dg_ple, dscale, dgq, dgk, dbias, dsink_rows, loss_v)
    return dx, partial_b, small


def kernel(x, p, w_in, w_out, g_attn_norm, g_q, g_k, attn_sinks, rel_bias, w_pool, pool_scale, g_ffn_norm, w_gate, w_up, w_down, g_ple_norm, w_ple_gate, w_ple_proj, loss_target, m_w_in, m_w_out, m_g_attn_norm, m_g_q, m_g_k, m_attn_sinks, m_rel_bias, m_w_pool, m_pool_scale, m_g_ffn_norm, m_w_gate, m_w_up, m_w_down, m_g_ple_norm, m_w_ple_gate, m_w_ple_proj, v_w_in, v_w_out, v_g_attn_norm, v_g_q, v_g_k, v_attn_sinks, v_rel_bias, v_w_pool, v_pool_scale, v_g_ffn_norm, v_w_gate, v_w_up, v_w_down, v_g_ple_norm, v_w_ple_gate, v_w_ple_proj):
    core = lax.axis_index("c").astype(jnp.int32).reshape(1)
    me = (2 * lax.axis_index("x") + lax.axis_index("y")).astype(jnp.int32).reshape(1)

    local_parts = [jnp.concatenate(pieces, axis=0).astype(BF16) for pieces in (
        [w_in[0].T, w_out[0]], [w_gate[0].T, w_up[0].T, w_down[0], w_ple_gate[0], _pack_ple_proj(w_ple_proj[0])])]
    wts = [(_ag_weights(local, 0, local.shape[0], name, collective_id), local, me)
           for local, name, collective_id in zip(local_parts, ("ag_early", "ag_late"), (1, 2))]

    place = jnp.concatenate([me, core])
    reduce_a = _Reduction("a", place, ids=(3, 4))
    dx, partial_b, small = _local_grads(x[0], p[0, 0], loss_target[0], wts, g_attn_norm, g_q, g_k, attn_sinks, rel_bias,
                                        w_pool, pool_scale, g_ffn_norm, g_ple_norm, reduce_a)
    reduce_b = _Reduction("b", place, ids=(6, 7))
    reduce_b.start(partial_b)
    grads_a, small_all = reduce_a.finish(partial_b, small)
    reduce_b.middle(grads_a)

    late0 = GATHER_PARTS[1][0]

    def rows(name):
        return grads_a, SLAB[name][0] - late0

    plp_rows = grads_a[SLAB["plp"][0] - late0:]
    big = {
        "w_gate": (w_gate, m_w_gate, v_w_gate, rows("gateT"), True),
        "w_up": (w_up, m_w_up, v_w_up, rows("upT"), True),
        "w_down": (w_down, m_w_down, v_w_down, rows("down"), False),
        "w_ple_gate": (w_ple_gate, m_w_ple_gate, v_w_ple_gate, rows("plg"), False),
        "w_ple_proj": (w_ple_proj, m_w_ple_proj, v_w_ple_proj,
                       (plp_rows.reshape(64, 4, 256).transpose(1, 0, 2).reshape(PLE_DIM, PLE_DIM), 0), False),
        "w_out": (w_out, m_w_out, v_w_out, None, False),
        "w_in": (w_in, m_w_in, v_w_in, None, True),
    }
    small_params = {
        "g_attn_norm": (g_attn_norm, m_g_attn_norm, v_g_attn_norm), "g_q": (g_q, m_g_q, v_g_q), "g_k": (g_k, m_g_k, v_g_k),
        "attn_sinks": (attn_sinks, m_attn_sinks, v_attn_sinks), "rel_bias": (rel_bias.T, m_rel_bias.T, v_rel_bias.T),
        "w_pool": tuple(a.reshape(512, 128) for a in (w_pool, m_w_pool, v_w_pool)),
        "pool_scale": (pool_scale, m_pool_scale, v_pool_scale), "g_ffn_norm": (g_ffn_norm, m_g_ffn_norm, v_g_ffn_norm),
        "g_ple_norm": (g_ple_norm, m_g_ple_norm, v_g_ple_norm),
    }

    grads, deltas, new_ms, new_vs = {}, {}, {}, {}
    out = grads_b = None
    for name, (w, m, v, g_src, transposed) in big.items():
        if g_src is None:
            if grads_b is None:
                grads_b = reduce_b.finish(out[-1])
            g_src = (grads_b, SLAB["out" if name == "w_out" else "inT"][0])
        view = (lambda a: a.T) if transposed else (lambda a: a)
        out = _adamw(view(w[0]), *g_src, view(m[0]), view(v[0]), "adamw_" + name)
        grads[name], deltas[name], new_ms[name], new_vs[name] = (view(a)[None] for a in out)

    loss, small_out = _adamw_small(small_all, reduce_a.tables, list(small_params.values()))
    for name, (g2, d, nm, nv) in zip(small_params, small_out):
        restore = {"w_pool": lambda a: a.reshape(w_pool.shape), "rel_bias": lambda a: a.T}.get(name, lambda a: a)
        grads[name], deltas[name], new_ms[name], new_vs[name] = (restore(a) for a in (g2, d, nm, nv))

    order = ["w_in", "w_out", "g_attn_norm", "g_q", "g_k", "attn_sinks", "rel_bias", "w_pool", "pool_scale", "g_ffn_norm",
             "w_gate", "w_up", "w_down", "g_ple_norm", "w_ple_gate", "w_ple_proj"]
    return (loss.reshape(()), dx[None], *[grads[n] for n in order], *[deltas[n] for n in order],
            *[new_ms[n] for n in order], *[new_vs[n] for n in order])
```

```python
import numpy as np
import jax
import jax.numpy as jnp
from jax import lax
from jax.experimental import pallas as pl
from jax.experimental.pallas import tpu as pltpu
from jax.experimental.pallas import tpu_sc as plsc

F32 = jnp.float32
BF16 = jnp.bfloat16
MESH = pl.DeviceIdType.MESH

D_MODEL = 1024
HEAD_DIM = 64
N_Q_HEADS = 8
ATTN_WIDTH = 512
POOL_WIDTH = 512
IN_WIDTH = 1280
D_FF = 2816
PLE_DIM = 256
FF_CHUNK = 1408
N_FF_CHUNKS = D_FF // FF_CHUNK
BLOCK = 128
N_BUCKETS = 32
MAX_DISTANCE = 128
EPS = 1e-6
NEG = -1e30
N_CHIPS = 4
N_DEV = 8

ADAM_LR = 0.001
ADAM_B1 = 0.9
ADAM_B2 = 0.999
ADAM_EPS = 1e-08
ADAM_WD = 0.01
ADAM_STEP = 10

SLAB = {"inT": (0, 320), "out": (320, 256), "gateT": (576, 704), "upT": (1280, 704), "down": (1984, 704),
        "plg": (2688, 256), "plp": (2944, 64)}
SLAB_ROWS = 3008
GATHER_PARTS = ((0, 576), (576, SLAB_ROWS))
POOL_HALO = 24

SMALL = {"g_attn": 0, "g_ffn": 8, "g_ple": 16, "pool_scale": 24, "g_q": 28, "g_k": 29, "sinks": 30, "loss": 31,
         "rel_bias": 32}
SMALL_ROWS = 64

VMEM_LIMIT_BIG = 60 * 1024 * 1024
VMEM_LIMIT = 48 * 1024 * 1024


def _params(vmem=VMEM_LIMIT, n_axes=1):
    return pltpu.CompilerParams(dimension_semantics=("arbitrary",) * n_axes, vmem_limit_bytes=vmem)


def _dot(a, b, ca, cb):
    return lax.dot_general(a, b, (((ca,), (cb,)), ((), ())), preferred_element_type=F32)


def _full(shape):
    return pl.BlockSpec(shape, lambda i: (0,) * len(shape))


ANY = pl.BlockSpec(memory_space=pl.ANY)
VMEM_WHOLE = pl.BlockSpec(memory_space=pltpu.VMEM)


W_SPECS = [ANY, ANY, pl.BlockSpec(memory_space=pltpu.SMEM)]


def _load_rows(w_refs, name, dst_ref, sems):
    slab_ref, local_ref, me_ref = w_refs
    off, rows = SLAB[name]
    slab_off = off - max(start for start, _ in GATHER_PARTS if start <= off)
    me = me_ref[0]
    for phase in ("start", "wait"):
        for j in range(N_CHIPS):
            dst = dst_ref.at[pl.ds(j * rows, rows), :]
            theirs = pltpu.make_async_copy(slab_ref.at[j, pl.ds(slab_off, rows), :], dst, sems.at[j])
            own = pltpu.make_async_copy(local_ref.at[pl.ds(slab_off, rows), :], dst, sems.at[j])

            @pl.when(me == j)
            def _():
                getattr(own, phase)()

            @pl.when(me != j)
            def _():
                getattr(theirs, phase)()


def _rms_fwd(x, g):
    r = lax.rsqrt(jnp.mean(x * x, axis=-1, keepdims=True) + EPS)
    return x * r * g


def _rms_bwd(x, g, dy):
    r = lax.rsqrt(jnp.mean(x * x, axis=-1, keepdims=True) + EPS)
    xn = x * r
    dyg = dy * g
    dx = r * (dyg - xn * jnp.mean(dyg * xn, axis=-1, keepdims=True))
    return dx, jnp.sum(dy * xn, axis=0, keepdims=True)


def _half_sum(v, lo):
    s_lo = jnp.sum(jnp.where(lo, v, 0.0), axis=-1, keepdims=True)
    s_hi = jnp.sum(jnp.where(lo, 0.0, v), axis=-1, keepdims=True)
    return jnp.where(lo, s_lo, s_hi)


def _half_sum_mxu(v):
    upper = lax.broadcasted_iota(jnp.int32, (128, 128), 0) < 64
    left = lax.broadcasted_iota(jnp.int32, (128, 128), 1) < 64
    ones = jnp.where(upper == left, 1.0, 0.0).astype(BF16)
    high = v.astype(BF16)
    low = (v - high.astype(F32)).astype(BF16)
    return _dot(high, ones, 1, 0) + _dot(low, ones, 1, 0)


def _pair_norm(zp, g, lo):
    r = lax.rsqrt(_half_sum(zp * zp, lo) * (1.0 / HEAD_DIM) + EPS)
    return zp * r * g


def _pair_norm_bwd(zp, g, dy):
    r = lax.rsqrt(_half_sum_mxu(zp * zp) * (1.0 / HEAD_DIM) + EPS)
    xn = zp * r
    dyg = dy * g
    dx = r * (dyg - xn * (_half_sum_mxu(dyg * xn) * (1.0 / HEAD_DIM)))
    return dx, jnp.sum(dy * xn, axis=0, keepdims=True)


def _pack_heads(pairs, lo):
    packed = [None] * 4
    for m in range(2):
        a, b = pairs[m], pairs[m + 2]
        packed[2 * m] = jnp.where(lo, a, pltpu.roll(b, 64, axis=1))
        packed[2 * m + 1] = jnp.where(lo, pltpu.roll(a, 64, axis=1), b)
    return packed


def _unpack_heads(packed, lo):
    pairs = [None] * 4
    for m in range(2):
        a, b = packed[2 * m], packed[2 * m + 1]
        pairs[m] = jnp.where(lo, a, pltpu.roll(b, 64, axis=1))
        pairs[m + 2] = jnp.where(lo, pltpu.roll(a, 64, axis=1), b)
    return pairs


def _expand_heads(packed):
    flat = packed.reshape(4 * BLOCK, 128)
    lo = lax.broadcasted_iota(jnp.int32, flat.shape, 1) < 64
    zero = jnp.zeros_like(flat)
    return jnp.concatenate([jnp.where(lo, flat, zero), jnp.where(lo, zero, flat)], axis=0)


def _fold_heads(stacked):
    half = 4 * BLOCK
    lo = lax.broadcasted_iota(jnp.int32, (half, 128), 1) < 64
    return jnp.where(lo, stacked[:half], stacked[half:]).reshape(4, BLOCK, 128)


def _sigmoid(v):
    return 1.0 / (1.0 + jnp.exp(-v))


def _pool_counts(tile, n_rows):
    t1 = tile * n_rows + lax.broadcasted_iota(jnp.int32, (n_rows, POOL_WIDTH), 0) + 1
    lane = lax.broadcasted_iota(jnp.int32, (n_rows, POOL_WIDTH), 1)
    win = jnp.where(lane < 128, 2, jnp.where(lane < 256, 4, jnp.where(lane < 384, 8, 16)))
    return jnp.minimum(t1, win).astype(F32)


def _first_norm(x2, g_attn):
    s_len = x2.shape[0]
    t = 512

    def body(x_ref, g_ref, hn_ref):
        hn_ref[...] = _rms_fwd(x_ref[...], g_ref[...]).astype(BF16)

    row = pl.BlockSpec((t, D_MODEL), lambda i: (i, 0))
    return pl.pallas_call(
        body, name="first_norm", grid=(s_len // t,), in_specs=[row, _full((1, D_MODEL))], out_specs=row,
        out_shape=jax.ShapeDtypeStruct((s_len, D_MODEL), BF16), compiler_params=_params(),
    )(x2, g_attn)


def _attn_in(hn1, gq, gk, wts):
    s_len = hn1.shape[0]
    t = 512

    def body(hn_ref, gq_ref, gk_ref, sl_ref, lo_ref, me_ref, zqk_ref, u_ref, kn_ref, v_ref, qst_ref, w_ref, sems):
        @pl.when(pl.program_id(0) == 0)
        def _():
            _load_rows((sl_ref, lo_ref, me_ref), "inT", w_ref, sems)

        z = _dot(hn_ref[...], w_ref[...], 1, 1)
        zqk_ref[...] = z[:, :640]
        u_ref[...] = z[:, 768:]
        v_ref[...] = z[:, 640:768].astype(BF16)
        lo = lax.broadcasted_iota(jnp.int32, (t, 128), 1) < 64
        kn_ref[...] = _pair_norm(z[:, 512:640], gk_ref[...], lo).astype(BF16)
        pairs = [_pair_norm(z[:, 128 * p:128 * p + 128], gq_ref[...], lo) for p in range(4)]
        for j, entry in enumerate(_pack_heads(pairs, lo)):
            qst_ref[j] = entry.astype(BF16)

    row = lambda w: pl.BlockSpec((t, w), lambda i: (i, 0))
    return pl.pallas_call(
        body, name="attn_in", grid=(s_len // t,),
        in_specs=[row(D_MODEL), _full((1, 128)), _full((1, 128))] + W_SPECS,
        out_specs=[row(640), row(POOL_WIDTH), row(128), row(128), pl.BlockSpec((4, t, 128), lambda i: (0, i, 0))],
        out_shape=[jax.ShapeDtypeStruct((s_len, 640), F32), jax.ShapeDtypeStruct((s_len, POOL_WIDTH), F32),
                   jax.ShapeDtypeStruct((s_len, 128), BF16), jax.ShapeDtypeStruct((s_len, 128), BF16),
                   jax.ShapeDtypeStruct((4, s_len, 128), BF16)],
        scratch_shapes=[pltpu.VMEM((IN_WIDTH, D_MODEL), BF16), pltpu.SemaphoreType.DMA((N_CHIPS,))],
        compiler_params=_params(),
    )(hn1, gq, gk, *wts)


def _bucket_table():
    i_idx = np.arange(BLOCK)[:, None]
    j_idx = np.arange(2 * BLOCK)[None, :]
    d = BLOCK + i_idx - j_idx
    n = np.maximum(d, 0)
    max_exact = N_BUCKETS // 2
    nf = np.maximum(n, 1).astype(np.float64)
    large = max_exact + (np.log(nf / max_exact) / np.log(MAX_DISTANCE / max_exact) * (N_BUCKETS - max_exact)).astype(np.int64)
    large = np.minimum(large, N_BUCKETS - 1)
    bucket = np.where(n < max_exact, n, large)
    return np.where((d >= 0) & (d < BLOCK), bucket, -1).astype(np.int32)


def _bias_build(rel_bias_t, bucket):
    def body(rb_ref, bucket_ref, out_ref):
        bk = bucket_ref[...]
        for h in range(N_Q_HEADS):
            acc = jnp.full((BLOCK, 2 * BLOCK), NEG, F32)
            for b in range(N_BUCKETS):
                acc = jnp.where(bk == b, rb_ref[h, b], acc)
            out_ref[0, pl.ds(h * BLOCK, BLOCK), :] = acc
            out_ref[1, pl.ds(h * BLOCK, BLOCK), :] = acc
            out_ref[1, pl.ds(h * BLOCK, BLOCK), 0:BLOCK] = jnp.full((BLOCK, BLOCK), NEG, F32)

    return pl.pallas_call(
        body, name="bias_build",
        in_specs=[pl.BlockSpec(memory_space=pltpu.SMEM), VMEM_WHOLE], out_specs=VMEM_WHOLE,
        out_shape=jax.ShapeDtypeStruct((2, N_Q_HEADS * BLOCK, 2 * BLOCK), F32),
    )(rel_bias_t, bucket)


def _head_softmax(s_ref, bias_ref, sink_ref, h):
    rows = pl.ds(pl.multiple_of(h * BLOCK, BLOCK), BLOCK)
    s = s_ref[rows, :] * (HEAD_DIM ** -0.5) + bias_ref[rows, :]
    sink = sink_ref[h]
    m = jnp.maximum(jnp.max(s, axis=-1, keepdims=True), sink)
    p = jnp.exp(s - m)
    e_sink = jnp.exp(sink - m)
    inv = 1.0 / (jnp.sum(p, axis=-1, keepdims=True) + e_sink)
    return rows, p * inv, e_sink * inv


ATTN_STEP_BLOCKS = 4
BAND = (N_Q_HEADS * BLOCK, 2 * BLOCK)


def _attn_specs():
    nb = ATTN_STEP_BLOCKS
    stacked = pl.BlockSpec((4, nb * BLOCK, 128), lambda i: (0, i, 0))
    kv = [pl.BlockSpec((BLOCK, 128), lambda i: (jnp.maximum(nb * i - 1, 0), 0)), pl.BlockSpec((nb * BLOCK, 128), lambda i: (i, 0))]
    consts = [_full((2,) + BAND), pl.BlockSpec(memory_space=pltpu.SMEM)]
    return stacked, kv, consts


def _step_blocks(i, kp_ref, kc_ref, vp_ref, vc_ref, bias_ref):
    blocks = []
    for b in range(ATTN_STEP_BLOCKS):
        if b == 0:
            k2 = jnp.concatenate([kp_ref[...], kc_ref[pl.ds(0, BLOCK), :]], axis=0)
            v2 = jnp.concatenate([vp_ref[...], vc_ref[pl.ds(0, BLOCK), :]], axis=0)
            bias = bias_ref.at[jnp.where(i == 0, 1, 0)]
        else:
            k2, v2, bias = kc_ref[pl.ds((b - 1) * BLOCK, 2 * BLOCK), :], vc_ref[pl.ds((b - 1) * BLOCK, 2 * BLOCK), :], bias_ref.at[0]
        blocks.append((pl.ds(b * BLOCK, BLOCK), k2, v2, bias))
    return blocks


def _attn_fwd(qst, kn, vb, bias_st, sinks):
    s_len = kn.shape[0]

    def body(q_ref, kp_ref, kc_ref, vp_ref, vc_ref, bias_ref, sink_ref, o_ref, s_ref, p_ref):
        for b, (rows, k2, v2, bias) in enumerate(_step_blocks(pl.program_id(0), kp_ref, kc_ref, vp_ref, vc_ref, bias_ref)):
            s_b, p_b = s_ref.at[b], p_ref.at[b]
            s_b[...] = _dot(_expand_heads(q_ref[:, rows, :]), k2, 1, 1)

            def head(h, carry):
                head_rows, probs, _ = _head_softmax(s_b, bias, sink_ref, h)
                p_b[head_rows, :] = probs.astype(BF16)
                return carry

            lax.fori_loop(0, N_Q_HEADS, head, 0, unroll=True)
            o_ref[:, rows, :] = _fold_heads(_dot(p_b[...], v2, 1, 0)).astype(BF16)

    stacked, kv, consts = _attn_specs()
    return pl.pallas_call(
        body, name="attn_fwd", grid=(s_len // (ATTN_STEP_BLOCKS * BLOCK),),
        in_specs=[stacked] + kv + kv + consts, out_specs=stacked,
        out_shape=jax.ShapeDtypeStruct((4, s_len, 128), BF16),
        scratch_shapes=[pltpu.VMEM((ATTN_STEP_BLOCKS,) + BAND, F32), pltpu.VMEM((ATTN_STEP_BLOCKS,) + BAND, BF16)],
        compiler_params=_params(),
    )(qst, kn, kn, vb, vb, bias_st, sinks)


def _mix_out(u, ost, x2, wts, wpool, pool_scale, g_ffn):
    s_len = x2.shape[0]
    t = 512
    n = t + 16

    def body(u_ref, o_ref, x_ref, sl_ref, lo_ref, me_ref, wp_ref, sc_ref, g_ref, pooled_ref, mix_ref, h1_ref, hn_ref,
             w_ref, ext_ref, st_ref, sems):
        i = pl.program_id(0)

        @pl.when(i == 0)
        def _():
            _load_rows((sl_ref, lo_ref, me_ref), "out", w_ref, sems)
            ext_ref[...] = jnp.zeros_like(ext_ref)
            st_ref[...] = jnp.zeros_like(st_ref)

        u_tile = u_ref[...]
        ext_ref[pl.ds(POOL_HALO, t), :] = u_tile
        st_ref[pl.ds(8, n), :] = ext_ref[pl.ds(8, n), :] + ext_ref[pl.ds(7, n), :]
        st_ref[pl.ds(8, n), 128:] = st_ref[pl.ds(8, n), 128:] + st_ref[pl.ds(6, n), 128:]
        st_ref[pl.ds(8, n), 256:] = st_ref[pl.ds(8, n), 256:] + st_ref[pl.ds(4, n), 256:]
        st_ref[pl.ds(8, n), 384:] = st_ref[pl.ds(8, n), 384:] + st_ref[pl.ds(0, n), 384:]
        ext_ref[pl.ds(0, POOL_HALO), :] = ext_ref[pl.ds(t, POOL_HALO), :]
        pooled = (st_ref[pl.ds(POOL_HALO, t), :] / _pool_counts(i, t) - u_tile).astype(BF16)
        pooled_ref[...] = pooled
        for g in range(4):
            cols = slice(128 * g, 128 * g + 128)
            y = _dot(pooled[:, cols], wp_ref[g], 1, 0) * sc_ref[:, cols]
            mix_ref[:, ATTN_WIDTH + 128 * g:ATTN_WIDTH + 128 * g + 128] = y.astype(BF16)
        lo = lax.broadcasted_iota(jnp.int32, (t, 128), 1) < 64
        for p, pair in enumerate(_unpack_heads([o_ref[j].astype(F32) for j in range(4)], lo)):
            mix_ref[:, 128 * p:128 * p + 128] = pair.astype(BF16)
        h1 = x_ref[...] + _dot(mix_ref[...], w_ref[...], 1, 0)
        h1_ref[...] = h1
        hn_ref[...] = _rms_fwd(h1, g_ref[...]).astype(BF16)

    row = lambda w: pl.BlockSpec((t, w), lambda i: (i, 0))
    return pl.pallas_call(
        body, name="mix_out", grid=(s_len // t,),
        in_specs=[row(POOL_WIDTH), pl.BlockSpec((4, t, 128), lambda i: (0, i, 0)), row(D_MODEL)] + W_SPECS
        + [_full((4, 128, 128)), _full((1, POOL_WIDTH)), _full((1, D_MODEL))],
        out_specs=[row(POOL_WIDTH), row(D_MODEL), row(D_MODEL), row(D_MODEL)],
        out_shape=[jax.ShapeDtypeStruct((s_len, POOL_WIDTH), BF16), jax.ShapeDtypeStruct((s_len, D_MODEL), BF16),
                   jax.ShapeDtypeStruct((s_len, D_MODEL), F32), jax.ShapeDtypeStruct((s_len, D_MODEL), BF16)],
        scratch_shapes=[pltpu.VMEM((D_MODEL, D_MODEL), BF16), pltpu.VMEM((t + POOL_HALO, POOL_WIDTH), F32),
                        pltpu.VMEM((t + POOL_HALO, POOL_WIDTH), F32), pltpu.SemaphoreType.DMA((N_CHIPS,))],
        compiler_params=_params(),
    )(u, ost, x2, *wts, wpool, pool_scale, g_ffn)


def _ffn_ple(hn2, h1, p2, tgt, wts, g_ffn, g_ple):
    s_len = h1.shape[0]
    t = 256
    n_tiles = s_len // t

    def body(hn_ref, h1_ref, p_ref, tgt_ref, sl_ref, lo_ref, me_ref, gf_ref, gp_ref,
             loss_ref, dgate_ref, dup_ref, act_ref, dh2b_ref, hn3_ref, dgl_ref, dwp_ref, dh1_ref, dgf_ref, dgp_ref,
             wg_ref, wu_ref, wd_ref, wl_ref, wp_ref, packed_ref, gate_s, up_s, loss_acc, dwp_acc, sems):
        i = pl.program_id(0)

        @pl.when(i == 0)
        def _():
            w_refs = (sl_ref, lo_ref, me_ref)
            _load_rows(w_refs, "gateT", wg_ref, sems)
            _load_rows(w_refs, "upT", wu_ref, sems)
            _load_rows(w_refs, "down", wd_ref, sems)
            _load_rows(w_refs, "plg", wl_ref, sems)
            _load_rows(w_refs, "plp", packed_ref, sems)
            for j in range(N_CHIPS):
                for q in range(4):
                    wp_ref[pl.ds(64 * q, 64), 256 * j:256 * j + 256] = packed_ref[pl.ds(64 * j, 64), 256 * q:256 * q + 256]
            loss_acc[...] = jnp.zeros_like(loss_acc)
            dwp_acc[...] = jnp.zeros_like(dwp_acc)
            dgf_ref[...] = jnp.zeros_like(dgf_ref)
            dgp_ref[...] = jnp.zeros_like(dgp_ref)

        hn = hn_ref[...]
        h1v = h1_ref[...]
        chunks = [slice(ch * FF_CHUNK, (ch + 1) * FF_CHUNK) for ch in range(N_FF_CHUNKS)]
        gate = _dot(hn, wg_ref[...], 1, 1)
        up = _dot(hn, wu_ref[...], 1, 1)
        gate_s[...] = gate
        up_s[...] = up
        act = (gate * _sigmoid(gate) * up).astype(BF16)
        for ch, cols in enumerate(chunks):
            act_ref[ch] = act[:, cols]
        h2 = h1v + _dot(act, wd_ref[...], 1, 0)
        gp = gp_ref[...]
        hn3 = _rms_fwd(h2, gp).astype(BF16)
        hn3_ref[...] = hn3
        gate2 = _sigmoid(_dot(hn3, wl_ref[...], 1, 0))
        p_tile = p_ref[...].astype(BF16)
        pp = _dot(p_tile, wp_ref[...], 1, 0)
        err = h2 + gate2 * pp - tgt_ref[...]
        loss_acc[...] += jnp.sum(err * err, axis=0, keepdims=True)
        dy = err * (1.0 / D_MODEL)
        dwp_acc[...] += _dot(p_tile, (dy * gate2).astype(BF16), 0, 0)
        dgl = (dy * pp * gate2 * (1.0 - gate2)).astype(BF16)
        dgl_ref[...] = dgl
        dx3, dg3 = _rms_bwd(h2, gp, _dot(dgl, wl_ref[...], 1, 1))
        dh2 = dy + dx3
        dgp_ref[...] += dg3
        dh2b = dh2.astype(BF16)
        dh2b_ref[...] = dh2b
        dact = _dot(dh2b, wd_ref[...], 1, 1)
        gate_v = gate_s[...]
        up_v = up_s[...]
        sg = _sigmoid(gate_v)
        dup = (dact * (gate_v * sg)).astype(BF16)
        dgate = (dact * up_v * (sg * (1.0 + gate_v * (1.0 - sg)))).astype(BF16)
        for ch, cols in enumerate(chunks):
            dup_ref[ch] = dup[:, cols]
            dgate_ref[ch] = dgate[:, cols]
        dhn = _dot(dgate, wg_ref[...], 1, 0) + _dot(dup, wu_ref[...], 1, 0)
        dx, dg = _rms_bwd(h1v, gf_ref[...], dhn)
        dh1_ref[...] = dh2 + dx
        dgf_ref[...] += dg

        @pl.when(i == n_tiles - 1)
        def _():
            total = jnp.sum(loss_acc[...], axis=-1, keepdims=True) * (0.5 / D_MODEL)
            loss_ref[...] = jnp.broadcast_to(total, loss_ref.shape)
            dwp_ref[...] = dwp_acc[...].astype(BF16)

    row = lambda w: pl.BlockSpec((t, w), lambda i: (i, 0))
    chunked = pl.BlockSpec((N_FF_CHUNKS, t, FF_CHUNK), lambda i: (0, i, 0))
    vec = _full((1, D_MODEL))
    act_shape = jax.ShapeDtypeStruct((N_FF_CHUNKS, s_len, FF_CHUNK), BF16)
    tok = lambda dtype: jax.ShapeDtypeStruct((s_len, D_MODEL), dtype)
    return pl.pallas_call(
        body, name="ffn_ple", grid=(n_tiles,),
        in_specs=[row(D_MODEL), row(D_MODEL), row(PLE_DIM), row(D_MODEL)] + W_SPECS + [vec, vec],
        out_specs=[_full((1, 128)), chunked, chunked, chunked] + [row(D_MODEL)] * 3 + [_full((PLE_DIM, D_MODEL)), row(D_MODEL),
                                                                                       vec, vec],
        out_shape=[jax.ShapeDtypeStruct((1, 128), F32), act_shape, act_shape, act_shape, tok(BF16), tok(BF16), tok(BF16),
                   jax.ShapeDtypeStruct((PLE_DIM, D_MODEL), BF16), tok(F32), jax.ShapeDtypeStruct((1, D_MODEL), F32),
                   jax.ShapeDtypeStruct((1, D_MODEL), F32)],
        scratch_shapes=[pltpu.VMEM((D_FF, D_MODEL), BF16)] * 3
        + [pltpu.VMEM((D_MODEL, D_MODEL), BF16), pltpu.VMEM((PLE_DIM, D_MODEL), BF16), pltpu.VMEM((PLE_DIM, D_MODEL), BF16),
           pltpu.VMEM((t, D_FF), F32), pltpu.VMEM((t, D_FF), F32), pltpu.VMEM((1, D_MODEL), F32),
           pltpu.VMEM((PLE_DIM, D_MODEL), F32), pltpu.SemaphoreType.DMA((N_CHIPS,))],
        compiler_params=_params(VMEM_LIMIT_BIG),
    )(hn2, h1, p2, tgt, *wts, g_ffn, g_ple)


def _flush_chunks(acc_ref, stage_ref, slab_ref, name, sems):
    stage_ref[...] = acc_ref[...].astype(BF16)
    off, rows = SLAB[name]
    copies = [pltpu.make_async_copy(stage_ref.at[pl.ds(j * rows, rows), :], slab_ref.at[j, pl.ds(off, rows), :], sems.at[j])
              for j in range(N_CHIPS)]
    for cp in copies:
        cp.start()
    for cp in copies:
        cp.wait()


def _mix_out_bwd(dh1, wts, pooled, wpool, pool_scale, mix, after):
    s_len = dh1.shape[0]
    t = 512
    n = t + 16
    n_tiles = s_len // t
    early_rows = GATHER_PARTS[0][1]

    def body(dh1_ref, sl_ref, lo_ref, me_ref, pooled_ref, wp_ref, sc_ref, mix_ref, after_ref, dost_ref, du_ref, dwp_ref,
             dsc_ref, slab_ref, w_ref, ext_ref, st_ref, acc_ref, stage_ref, sems):
        del after_ref
        i = pl.program_id(0)

        @pl.when(i == 0)
        def _():
            _load_rows((sl_ref, lo_ref, me_ref), "out", w_ref, sems)
            ext_ref[...] = jnp.zeros_like(ext_ref)
            st_ref[...] = jnp.zeros_like(st_ref)
            dsc_ref[...] = jnp.zeros_like(dsc_ref)
            dwp_ref[...] = jnp.zeros_like(dwp_ref)
            acc_ref[...] = jnp.zeros_like(acc_ref)

        dh1b = dh1_ref[...].astype(BF16)
        acc_ref[...] += _dot(mix_ref[...], dh1b, 0, 0)
        dmix = _dot(dh1b, w_ref[...], 1, 1)
        lo = lax.broadcasted_iota(jnp.int32, (t, 128), 1) < 64
        for j, entry in enumerate(_pack_heads([dmix[:, 128 * p:128 * p + 128] for p in range(4)], lo)):
            dost_ref[j] = entry.astype(BF16)
        pooled_v = pooled_ref[...]
        counts = _pool_counts(n_tiles - 1 - i, t)
        for g in range(4):
            cols = slice(128 * g, 128 * g + 128)
            dm = dmix[:, ATTN_WIDTH + 128 * g:ATTN_WIDTH + 128 * g + 128]
            ypre = _dot(pooled_v[:, cols], wp_ref[g], 1, 0)
            dsc_ref[:, cols] += jnp.sum(ypre * dm, axis=0, keepdims=True)
            dyp = (dm * sc_ref[:, cols]).astype(BF16)
            dwp_ref[g] += _dot(pooled_v[:, cols], dyp, 0, 0)
            dpooled = _dot(dyp, wp_ref[g], 1, 1)
            du_ref[:, cols] = -dpooled
            ext_ref[pl.ds(0, t), cols] = dpooled / counts[:, cols]
        st_ref[pl.ds(0, n), :] = ext_ref[pl.ds(0, n), :] + ext_ref[pl.ds(1, n), :]
        st_ref[pl.ds(0, n), 128:] = st_ref[pl.ds(0, n), 128:] + st_ref[pl.ds(2, n), 128:]
        st_ref[pl.ds(0, n), 256:] = st_ref[pl.ds(0, n), 256:] + st_ref[pl.ds(4, n), 256:]
        st_ref[pl.ds(0, n), 384:] = st_ref[pl.ds(0, n), 384:] + st_ref[pl.ds(8, n), 384:]
        ext_ref[pl.ds(t, POOL_HALO), :] = ext_ref[pl.ds(0, POOL_HALO), :]
        du_ref[...] += st_ref[pl.ds(0, t), :]

        @pl.when(i == n_tiles - 1)
        def _():
            _flush_chunks(acc_ref, stage_ref, slab_ref, "out", sems)

    rev = lambda w: pl.BlockSpec((t, w), lambda i: (n_tiles - 1 - i, 0))
    return pl.pallas_call(
        body, name="mix_out_bwd", grid=(n_tiles,),
        in_specs=[rev(D_MODEL)] + W_SPECS + [rev(POOL_WIDTH), _full((4, 128, 128)), _full((1, POOL_WIDTH)), rev(D_MODEL), ANY],
        out_specs=[pl.BlockSpec((4, t, 128), lambda i: (0, n_tiles - 1 - i, 0)), rev(POOL_WIDTH),
                   _full((4, 128, 128)), _full((1, POOL_WIDTH)), ANY],
        out_shape=[jax.ShapeDtypeStruct((4, s_len, 128), BF16), jax.ShapeDtypeStruct((s_len, POOL_WIDTH), F32),
                   jax.ShapeDtypeStruct((4, 128, 128), F32), jax.ShapeDtypeStruct((1, POOL_WIDTH), F32),
                   jax.ShapeDtypeStruct((N_CHIPS, early_rows, D_MODEL), BF16)],
        scratch_shapes=[pltpu.VMEM((D_MODEL, D_MODEL), BF16), pltpu.VMEM((t + POOL_HALO, POOL_WIDTH), F32),
                        pltpu.VMEM((t + POOL_HALO, POOL_WIDTH), F32), pltpu.VMEM((D_MODEL, D_MODEL), F32),
                        pltpu.VMEM((D_MODEL, D_MODEL), BF16), pltpu.SemaphoreType.DMA((N_CHIPS,))],
        compiler_params=_params(),
    )(dh1, *wts, pooled, wpool, pool_scale, mix, after)


def _attn_bwd(qst, kn, vb, dost, bias_st, sinks, after):
    s_len = kn.shape[0]

    def body(q_ref, kp_ref, kc_ref, vp_ref, vc_ref, do_ref, bias_ref, sink_ref, after_ref, dq_ref, dk_ref, dv_ref, dbias_ref,
             dsink_ref, s_ref, dp_ref, p_ref, dl_ref):
        del after_ref
        i = pl.program_id(0)

        @pl.when(i == 0)
        def _():
            dk_ref[...] = jnp.zeros_like(dk_ref)
            dv_ref[...] = jnp.zeros_like(dv_ref)
            dbias_ref[...] = jnp.zeros_like(dbias_ref)
            dsink_ref[...] = jnp.zeros_like(dsink_ref)

        for b, (rows, k2, v2, bias) in enumerate(_step_blocks(i, kp_ref, kc_ref, vp_ref, vc_ref, bias_ref)):
            s_b, dp_b, p_b, dl_b = s_ref.at[b], dp_ref.at[b], p_ref.at[b], dl_ref.at[b]
            q = _expand_heads(q_ref[:, rows, :])
            do = _expand_heads(do_ref[:, rows, :])
            s_b[...] = _dot(q, k2, 1, 1)
            dp_b[...] = _dot(do, v2, 1, 1)

            def head(h, carry):
                head_rows, probs, p_sink = _head_softmax(s_b, bias, sink_ref, h)
                dp = dp_b[head_rows, :]
                dsum = jnp.sum(probs * dp, axis=-1, keepdims=True)
                dlog = probs * (dp - dsum)
                dsink_ref[head_rows, :] -= p_sink * dsum
                dbias_ref[head_rows, :] += dlog
                p_b[head_rows, :] = probs.astype(BF16)
                dl_b[head_rows, :] = (dlog * (HEAD_DIM ** -0.5)).astype(BF16)
                return carry

            lax.fori_loop(0, N_Q_HEADS, head, 0, unroll=True)
            dlog_s = dl_b[...]
            dq_ref[:, rows, :] = _fold_heads(_dot(dlog_s, k2, 1, 0))
            dk2 = _dot(dlog_s, q, 0, 0)
            dv2 = _dot(p_b[...], do, 0, 0)
            block = ATTN_STEP_BLOCKS * i + b
            prev_rows = pl.ds(pl.multiple_of(jnp.maximum(block - 1, 0) * BLOCK, BLOCK), BLOCK)
            cur_rows = pl.ds(pl.multiple_of(block * BLOCK, BLOCK), BLOCK)
            dk_ref[prev_rows, :] += dk2[:BLOCK]
            dk_ref[cur_rows, :] += dk2[BLOCK:]
            dv_ref[prev_rows, :] += dv2[:BLOCK]
            dv_ref[cur_rows, :] += dv2[BLOCK:]

    stacked, kv, consts = _attn_specs()
    per_step = (ATTN_STEP_BLOCKS,) + BAND
    return pl.pallas_call(
        body, name="attn_bwd", grid=(s_len // (ATTN_STEP_BLOCKS * BLOCK),),
        in_specs=[stacked] + kv + kv + [stacked] + consts + [ANY],
        out_specs=[stacked, _full((s_len, 128)), _full((s_len, 128)), _full(BAND), _full((N_Q_HEADS * BLOCK, 1))],
        out_shape=[jax.ShapeDtypeStruct((4, s_len, 128), F32), jax.ShapeDtypeStruct((s_len, 128), F32),
                   jax.ShapeDtypeStruct((s_len, 128), F32), jax.ShapeDtypeStruct(BAND, F32),
                   jax.ShapeDtypeStruct((N_Q_HEADS * BLOCK, 1), F32)],
        scratch_shapes=[pltpu.VMEM(per_step, F32), pltpu.VMEM(per_step, F32), pltpu.VMEM(per_step, BF16),
                        pltpu.VMEM(per_step, BF16)],
        compiler_params=_params(),
    )(qst, kn, kn, vb, vb, dost, bias_st, sinks, after)


def _flip_rows(x):
    n = x.shape[0]
    exchange = (lax.broadcasted_iota(jnp.int32, (n, n), 0) + lax.broadcasted_iota(jnp.int32, (n, n), 1) == n - 1)
    exchange = jnp.where(exchange, 1.0, 0.0).astype(BF16)
    flipped, rest = None, x
    for _ in range(3):
        term = rest.astype(BF16)
        rest = rest - term.astype(F32)
        part = _dot(exchange, term, 1, 0)
        flipped = part if flipped is None else flipped + part
    return flipped


def _small_pack(dg_attn, dg_ffn, dg_ple, dscale, dgq, dgk, dbias, dsink_rows, loss_v):
    def body(ga_ref, gf_ref, gp_ref, sc_ref, gq_ref, gk_ref, db_ref, ds_ref, bucket_ref, loss_ref, out_ref):
        out_ref[...] = jnp.zeros((SMALL_ROWS, 128), F32)
        for name, ref, n in (("g_attn", ga_ref, 8), ("g_ffn", gf_ref, 8), ("g_ple", gp_ref, 8), ("pool_scale", sc_ref, 4)):
            for k in range(n):
                out_ref[pl.ds(SMALL[name] + k, 1), :] = ref[:, 128 * k:128 * k + 128]
        for name, ref in (("g_q", gq_ref), ("g_k", gk_ref)):
            both = ref[...]
            out_ref[pl.ds(SMALL[name], 1), :] = both + pltpu.roll(both, 64, axis=1)
        out_ref[pl.ds(SMALL["loss"], 1), :] = loss_ref[...]
        by_diagonal = lambda flipped: pltpu.roll(flipped, 0, 1, stride=1, stride_axis=0)
        bucket_of = jnp.max(by_diagonal(bucket_ref[...]), axis=0, keepdims=True)
        sums = jnp.concatenate([jnp.sum(by_diagonal(_flip_rows(db_ref[pl.ds(h * BLOCK, BLOCK), :])), axis=0, keepdims=True)
                                for h in range(N_Q_HEADS)], axis=0)
        lanes = lax.broadcasted_iota(jnp.int32, (N_Q_HEADS, 128), 1)
        lane1 = lax.broadcasted_iota(jnp.int32, (1, 128), 1)
        rb = jnp.zeros((N_Q_HEADS, 128), F32)
        for b in range(N_BUCKETS):
            rb = jnp.where(lanes == b, jnp.sum(jnp.where(bucket_of == float(b), sums, 0.0), axis=1, keepdims=True), rb)
        sk = jnp.zeros((1, 128), F32)
        for h in range(N_Q_HEADS):
            sk = jnp.where(lane1 == h, jnp.sum(ds_ref[pl.ds(h * BLOCK, BLOCK), :]), sk)
        out_ref[pl.ds(SMALL["rel_bias"], N_Q_HEADS), :] = rb
        out_ref[pl.ds(SMALL["sinks"], 1), :] = sk

    bucket = jnp.asarray(_bucket_table()[::-1].astype(np.float32))
    return pl.pallas_call(
        body, name="small_pack", in_specs=[VMEM_WHOLE] * 10, out_specs=VMEM_WHOLE,
        out_shape=jax.ShapeDtypeStruct((SMALL_ROWS, 128), F32),
    )(dg_attn, dg_ffn, dg_ple, dscale, dgq, dgk, dbias, dsink_rows, bucket, loss_v)


def _attn_in_bwd(dqst, zqk, dk, dv, du, x2, dh1, hn1, slab, wts, g_attn, gq, gk):
    s_len = x2.shape[0]
    t = 512
    n_tiles = s_len // t

    def body(dq_ref, zqk_ref, dk_ref, dv_ref, du_ref, x_ref, dh1_ref, hn_ref, slab_in_ref, sl_ref, lo_ref, me_ref, g_ref,
             gq_ref, gk_ref, dx_ref, dg_ref, dgq_ref, dgk_ref, slab_ref, w_ref, dz_ref, acc_ref, stage_ref, sems):
        del slab_in_ref
        i = pl.program_id(0)

        @pl.when(i == 0)
        def _():
            _load_rows((sl_ref, lo_ref, me_ref), "inT", w_ref, sems)
            dg_ref[...] = jnp.zeros_like(dg_ref)
            dgq_ref[...] = jnp.zeros_like(dgq_ref)
            dgk_ref[...] = jnp.zeros_like(dgk_ref)
            acc_ref[...] = jnp.zeros_like(acc_ref)

        lo = lax.broadcasted_iota(jnp.int32, (t, 128), 1) < 64
        for p, dqn in enumerate(_unpack_heads([dq_ref[j] for j in range(4)], lo)):
            dq_raw, dgq = _pair_norm_bwd(zqk_ref[:, 128 * p:128 * p + 128], gq_ref[...], dqn)
            dz_ref[:, 128 * p:128 * p + 128] = dq_raw.astype(BF16)
            dgq_ref[...] += dgq
        dk_raw, dgk = _pair_norm_bwd(zqk_ref[:, 512:640], gk_ref[...], dk_ref[...])
        dgk_ref[...] += dgk
        dz_ref[:, 512:640] = dk_raw.astype(BF16)
        dz_ref[:, 640:768] = dv_ref[...].astype(BF16)
        dz_ref[:, 768:] = du_ref[...].astype(BF16)
        dz = dz_ref[...]
        acc_ref[...] += _dot(dz, hn_ref[...], 0, 0)
        dx, dg = _rms_bwd(x_ref[...], g_ref[...], _dot(dz, w_ref[...], 1, 0))
        dx_ref[...] = dh1_ref[...] + dx
        dg_ref[...] += dg

        @pl.when(i == n_tiles - 1)
        def _():
            _flush_chunks(acc_ref, stage_ref, slab_ref, "inT", sems)

    row = lambda w: pl.BlockSpec((t, w), lambda i: (i, 0))
    return pl.pallas_call(
        body, name="attn_in_bwd", grid=(n_tiles,),
        in_specs=[pl.BlockSpec((4, t, 128), lambda i: (0, i, 0)), row(640), row(128), row(128), row(POOL_WIDTH),
                  row(D_MODEL), row(D_MODEL), row(D_MODEL), ANY] + W_SPECS + [_full((1, D_MODEL)), _full((1, 128)),
                                                                              _full((1, 128))],
        out_specs=[row(D_MODEL), _full((1, D_MODEL)), _full((1, 128)), _full((1, 128)), ANY],
        out_shape=[jax.ShapeDtypeStruct((s_len, D_MODEL), F32), jax.ShapeDtypeStruct((1, D_MODEL), F32),
                   jax.ShapeDtypeStruct((1, 128), F32), jax.ShapeDtypeStruct((1, 128), F32),
                   jax.ShapeDtypeStruct(slab.shape, BF16)],
        input_output_aliases={8: 4},
        scratch_shapes=[pltpu.VMEM((IN_WIDTH, D_MODEL), BF16), pltpu.VMEM((t, IN_WIDTH), BF16),
                        pltpu.VMEM((IN_WIDTH, D_MODEL), F32), pltpu.VMEM((IN_WIDTH, D_MODEL), BF16),
                        pltpu.SemaphoreType.DMA((N_CHIPS,))],
        compiler_params=_params(),
    )(dqst, zqk, dk, dv, du, x2, dh1, hn1, slab, *wts, g_attn, gq, gk)


def _dw(lefts, b, name, slab, slab_rows, row_offs):
    a0, n_a = lefts[0], len(lefts)
    assert b.shape[1] == D_MODEL
    if a0.ndim == 3:
        n_chunks, s_len, tm = a0.shape
        m = n_chunks * tm
    else:
        s_len, tm = a0.shape
        m = tm
    tk = 2048 if n_a * tm <= 1408 else 1024
    if a0.ndim == 3:
        a_spec = pl.BlockSpec((None, tk, tm), lambda i, k: (i, k, 0))
    else:
        a_spec = pl.BlockSpec((tk, tm), lambda i, k: (k, i))
    n_steps, n_tiles = s_len // tk, m // tm
    chunk = m // N_CHIPS
    per_tile = tm // chunk

    def body(*refs):
        a_refs, b_ref = refs[:n_a], refs[n_a]
        o_ref, acc_ref, stage_ref, sems = refs[-4:]
        i, k = pl.program_id(0), pl.program_id(1)
        products = lambda: [_dot(a_ref[...].astype(BF16), b_ref[...].astype(BF16), 0, 0) for a_ref in a_refs]

        @pl.when(k == 0)
        def _():
            for w, product in enumerate(products()):
                acc_ref[w] = product

        @pl.when(k > 0)
        def _():
            for w, product in enumerate(products()):
                acc_ref[w] += product

        def out_copies(tile, slot):
            return [pltpu.make_async_copy(stage_ref.at[slot, w, pl.ds(jj * chunk, chunk), :],
                                          o_ref.at[tile * per_tile + jj, pl.ds(row_offs[w], chunk), :], sems.at[slot, w, jj])
                    for w in range(n_a) for jj in range(per_tile)]

        @pl.when(k == n_steps - 1)
        def _():
            slot = i % 2

            @pl.when(i >= 2)
            def _():
                for cp in out_copies(i - 2, slot):
                    cp.wait()

            stage_ref[slot] = acc_ref[...].astype(BF16)
            for cp in out_copies(i, slot):
                cp.start()

            @pl.when(i == n_tiles - 1)
            def _():
                for cp in out_copies(i, slot):
                    cp.wait()
                if n_tiles > 1:
                    for cp in out_copies(i - 1, 1 - slot):
                        cp.wait()

    in_specs = [a_spec] * n_a + [pl.BlockSpec((tk, D_MODEL), lambda i, k: (k, 0))]
    operands, aliases = [*lefts, b], {}
    if slab is not None:
        in_specs.append(ANY)
        operands.append(slab)
        aliases = {n_a + 1: 0}
    return pl.pallas_call(
        body, name=name, grid=(n_tiles, n_steps), in_specs=in_specs, out_specs=ANY,
        out_shape=jax.ShapeDtypeStruct((N_CHIPS, slab_rows, D_MODEL), BF16), input_output_aliases=aliases,
        scratch_shapes=[pltpu.VMEM((n_a, tm, D_MODEL), F32), pltpu.VMEM((2, n_a, tm, D_MODEL), BF16),
                        pltpu.SemaphoreType.DMA((2, n_a, per_tile))],
        compiler_params=_params(VMEM_LIMIT_BIG, n_axes=2),
    )(*operands)


def _position():
    x, y, c = lax.axis_index("x"), lax.axis_index("y"), lax.axis_index("c")
    other_chips = [(1 - x, y), (x, 1 - y), (1 - x, 1 - y)]
    return x, y, c, other_chips


def _ag_weights(local_slab, row0, n_rows, name, collective_id):
    half = n_rows // 2
    quarter = half // 2
    assert quarter % 16 == 0

    def body(l_ref, g_ref, send, recv):
        x, y, c, chips = _position()
        me, (via_x, via_y, diagonal) = 2 * x + y, [2 * chip[0] + chip[1] for chip in chips]
        here, sibling, x_nbr, y_nbr = (x, y, c), (x, y, 1 - c), (1 - x, y, c), (x, 1 - y, c)
        peers = [sibling, x_nbr, y_nbr]
        barrier = pltpu.get_barrier_semaphore()
        for peer in peers:
            pl.semaphore_signal(barrier, inc=1, device_id=peer, device_id_type=MESH)
        pl.semaphore_wait(barrier, len(peers))

        def rows(core, part):
            start, size = (core * half, half) if part is None else (core * half + part * quarter, quarter)
            return pl.ds(pl.multiple_of(start, 16), size)

        def copy(k, chip_idx, where, to, src=None):
            dst = g_ref.at[chip_idx, where, :]
            return pltpu.make_async_remote_copy(src_ref=dst if src is None else src, dst_ref=dst, send_sem=send.at[k],
                                                recv_sem=recv.at[k], device_id=to, device_id_type=MESH)

        own_rows = l_ref.at[pl.ds(pl.multiple_of(row0 + c * half, 16), half), :]
        started = [copy(0, me, rows(c, None), x_nbr, src=own_rows), copy(1, me, rows(c, None), y_nbr, src=own_rows)]
        for cp in started:
            cp.start()
        after_arrival = [
            (copy(0, via_x, rows(c, None), here), [copy(4, via_x, rows(c, None), sibling), copy(3, via_x, rows(c, 1), y_nbr)]),
            (copy(1, via_y, rows(c, None), here), [copy(5, via_y, rows(c, None), sibling), copy(2, via_y, rows(c, 0), x_nbr)]),
            (copy(2, diagonal, rows(c, 0), here), [copy(6, diagonal, rows(c, 0), sibling)]),
            (copy(3, diagonal, rows(c, 1), here), [copy(7, diagonal, rows(c, 1), sibling)]),
        ]
        for arrival, onward in after_arrival:
            arrival.wait_recv()
            for cp in onward:
                cp.start()
            started += onward
        for cp in (copy(4, via_x, rows(1 - c, None), here), copy(5, via_y, rows(1 - c, None), here),
                   copy(6, diagonal, rows(1 - c, 0), here), copy(7, diagonal, rows(1 - c, 1), here)):
            cp.wait_recv()
        for cp in started:
            cp.wait_send()

    return pl.kernel(
        body, out_type=jax.ShapeDtypeStruct((N_CHIPS, n_rows, D_MODEL), BF16),
        mesh=plsc.ScalarSubcoreMesh(axis_name="sequencer", num_cores=1), name=name,
        scratch_types=[pltpu.SemaphoreType.DMA((8,)), pltpu.SemaphoreType.DMA((8,))],
        compiler_params=pltpu.CompilerParams(collective_id=collective_id),
    )(local_slab)


def _comm_call(body, peers_of, out_shape, n_sems, operand, name, collective_id):
    sems = [pltpu.SemaphoreType.DMA((n_sems,)), pltpu.SemaphoreType.DMA((n_sems,))]

    def with_handshake(in_ref, out_ref, send, recv):
        x, y, c, _ = _position()
        peers = peers_of(x, y, c)
        barrier = pltpu.get_barrier_semaphore()
        for peer in peers:
            pl.semaphore_signal(barrier, inc=1, device_id=peer, device_id_type=MESH)
        pl.semaphore_wait(barrier, len(peers))
        body(in_ref, out_ref, send, recv)

    return pl.kernel(with_handshake, out_type=out_shape, mesh=plsc.ScalarSubcoreMesh(axis_name="sequencer", num_cores=1),
                     name=name, scratch_types=sems, compiler_params=pltpu.CompilerParams(collective_id=collective_id))(operand)


def _rs_swap_halves(partial, name, collective_id):
    half = partial.shape[1] // 2

    def body(p_ref, r_ref, send, recv):
        x, y, c, _ = _position()
        theirs = pl.ds(pl.multiple_of((1 - c) * half, 16), half)
        cp = pltpu.make_async_remote_copy(src_ref=p_ref.at[:, theirs, :], dst_ref=r_ref, send_sem=send.at[0],
                                          recv_sem=recv.at[0], device_id=(x, y, 1 - c), device_id_type=MESH)
        cp.start()
        cp.wait()

    return _comm_call(body, lambda x, y, c: [(x, y, 1 - c)], jax.ShapeDtypeStruct((N_CHIPS, half, D_MODEL), BF16), 1,
                      partial, name, collective_id)


def _gather_chip_sums(s_ref, sib_ref, sum_ref, o_ref, send, recv):
    x, y, c, chips = _position()
    me, sibling, here = 2 * x + y, (x, y, 1 - c), (x, y, c)
    half = s_ref.shape[0] // 2

    def rows(core):
        return pl.ds(pl.multiple_of(core * half, 8), half)

    def copy(k, src, dst, to):
        return pltpu.make_async_remote_copy(src_ref=src, dst_ref=dst, send_sem=send.at[k], recv_sem=recv.at[k], device_id=to,
                                            device_id_type=MESH)

    swap = copy(0, s_ref, sib_ref, sibling)
    keep = pltpu.make_async_copy(sum_ref, o_ref.at[me], send.at[7])
    sends = [copy(1 + k, sum_ref.at[rows(c), :], o_ref.at[me, rows(c), :], (*chip, c)) for k, chip in enumerate(chips)]

    def landed(chip, core):
        return o_ref.at[2 * chip[0] + chip[1], rows(core), :]

    def add_and_send():
        swap.wait_recv()
        sum_ref[...] = s_ref[...] + sib_ref[...]
        keep.start()
        for cp in sends:
            cp.start()

    def finish():
        passed = []
        for k, chip in enumerate(chips):
            copy(1 + k, landed(chip, c), landed(chip, c), here).wait_recv()
            fwd = copy(4 + k, landed(chip, c), landed(chip, c), sibling)
            fwd.start()
            passed.append(fwd)
        for k, chip in enumerate(chips):
            copy(4 + k, landed(chip, 1 - c), landed(chip, 1 - c), here).wait_recv()
        for cp in [swap] + sends + passed:
            cp.wait_send()
        keep.wait()

    return swap.start, add_and_send, finish


def _rs_add_halves(partial, other, core, name, after, small=None):
    half = other.shape[1]
    t = half // 2
    steps = half // t

    n_steps, ring = N_CHIPS * steps, 3

    def body(core_ref, a_ref, b_ref, after_ref, *rest):
        del after_ref
        if small is None:
            o_ref, a_buf, b_buf, a_sems, b_sems = rest
        else:
            small_ref, o_ref, t_ref, a_buf, b_buf, a_sems, b_sems, sib_ref, sum_ref, t_send, t_recv = rest
            swap, add_and_send, finish_tables = _gather_chip_sums(small_ref, sib_ref, sum_ref, t_ref, t_send, t_recv)
        step = pl.program_id(0) * steps + pl.program_id(1)

        def loads(s):
            j, i, slot = s // steps, s % steps, s % ring
            mine = a_ref.at[j, pl.ds(pl.multiple_of((core_ref[0] * steps + i) * t, 16), t), :]
            theirs = b_ref.at[j, pl.ds(pl.multiple_of(i * t, 16), t), :]
            return (pltpu.make_async_copy(mine, a_buf.at[slot], a_sems.at[slot]),
                    pltpu.make_async_copy(theirs, b_buf.at[slot], b_sems.at[slot]))

        @pl.when(step == 0)
        def _():
            for s in range(ring - 1):
                for cp in loads(s):
                    cp.start()
            if small is not None:
                swap()

        @pl.when(step + ring - 1 < n_steps)
        def _():
            for cp in loads(step + ring - 1):
                cp.start()

        if small is not None:
            pl.when(step == 1)(add_and_send)
        for cp in loads(step):
            cp.wait()
        slot = step % ring
        o_ref[0] = (a_buf[slot].astype(F32) + b_buf[slot].astype(F32)).astype(BF16)
        if small is not None:
            pl.when(step == n_steps - 1)(finish_tables)

    t_in, t_out, t_scratch = [], [], []
    if small is not None:
        t_in, t_out = [VMEM_WHOLE], [jax.ShapeDtypeStruct((N_CHIPS, *small.shape), F32)]
        t_scratch = [pltpu.VMEM(small.shape, F32)] * 2 + [pltpu.SemaphoreType.DMA((8,))] * 2
    assert t % 16 == 0 and n_steps >= ring
    res = pl.pallas_call(
        body, name=name,
        grid_spec=pltpu.PrefetchScalarGridSpec(
            num_scalar_prefetch=1, grid=(N_CHIPS, steps),
            in_specs=[ANY, ANY, ANY] + t_in,
            out_specs=[pl.BlockSpec((1, t, D_MODEL), lambda j, i, core_ref: (j, i, 0))] + [ANY] * len(t_out),
            scratch_shapes=[pltpu.VMEM((ring, t, D_MODEL), BF16)] * 2 + [pltpu.SemaphoreType.DMA((ring,))] * 2 + t_scratch),
        out_shape=[jax.ShapeDtypeStruct((N_CHIPS, half, D_MODEL), BF16)] + t_out,
        compiler_params=_params(n_axes=2),
    )(core, partial, other, after, *([] if small is None else [small]))
    return res[0] if small is None else res


def _rs_exchange_chips(pre, name, collective_id):
    def body(s_ref, r_ref, send, recv):
        x, y, c, chips = _position()

        def copy(k, chunk, to):
            return pltpu.make_async_remote_copy(src_ref=s_ref.at[chunk], dst_ref=r_ref.at[k], send_sem=send.at[k],
                                                recv_sem=recv.at[k], device_id=to, device_id_type=MESH)

        sends = [copy(k, 2 * chip[0] + chip[1], (*chip, c)) for k, chip in enumerate(chips)]
        for cp in sends:
            cp.start()
        for cp in sends:
            cp.wait()

    return _comm_call(body, lambda x, y, c: [(1 - x, y, c), (x, 1 - y, c), (1 - x, 1 - y, c)],
                      jax.ShapeDtypeStruct((3, pre.shape[1], D_MODEL), BF16), 3, pre, name, collective_id)


def _gather_small(s_ref, t_ref, send, recv):
    x, y, c, chips = _position()
    sibling = (x, y, 1 - c)

    def slot(px, py, pc):
        return t_ref.at[4 * px + 2 * py + pc]

    def copy(k, block, to, src=None):
        return pltpu.make_async_remote_copy(src_ref=slot(*block) if src is None else src, dst_ref=slot(*block),
                                            send_sem=send.at[k], recv_sem=recv.at[k], device_id=to, device_id_type=MESH)

    own = pltpu.make_async_copy(s_ref, slot(x, y, c), send.at[7])
    first = [copy(0, (x, y, c), sibling, src=s_ref)]
    first += [copy(1 + k, (x, y, c), (*chip, c), src=s_ref) for k, chip in enumerate(chips)]

    def start():
        own.start()
        for cp in first:
            cp.start()

    def finish():
        passed = []
        for k, chip in enumerate(chips):
            copy(1 + k, (*chip, c), (x, y, c)).wait_recv()
            fwd = copy(4 + k, (*chip, c), sibling)
            fwd.start()
            passed.append(fwd)
        copy(0, sibling, (x, y, c)).wait_recv()
        for k, chip in enumerate(chips):
            copy(4 + k, (*chip, 1 - c), (x, y, c)).wait_recv()
        for cp in first + passed:
            cp.wait_send()
        own.wait()

    return start, finish


def _rs_sum_chips(pre, received, place, name, after, small=None):
    half = pre.shape[1]
    steps = 4 if half > 512 else 2
    t = half // steps
    assert t % 16 == 0 and t * steps == half

    def body(place_ref, own_ref, r_ref, after_ref, *rest):
        del place_ref, after_ref
        if small is None:
            o_ref, stage, kept_sems, send, recv = rest
        else:
            small_ref, o_ref, t_ref, stage, kept_sems, send, recv, t_send, t_recv = rest
            start_tables, finish_tables = _gather_small(small_ref, t_ref, t_send, t_recv)
            pl.when(pl.program_id(0) == 0)(start_tables)
        i = pl.program_id(0)
        x, y, c, _ = _position()

        def rows(core, step):
            return o_ref.at[pl.ds(pl.multiple_of((core * steps + step) * t, 8), t), :]

        def kept(step):
            return pltpu.make_async_copy(stage.at[step], rows(c, step), kept_sems.at[step])

        def sent(core, step):
            return pltpu.make_async_remote_copy(src_ref=stage.at[step], dst_ref=rows(core, step), send_sem=send.at[step],
                                                recv_sem=recv.at[step], device_id=(x, y, 1 - core), device_id_type=MESH)

        acc = own_ref[0].astype(F32)
        for k in range(3):
            acc = acc + r_ref[k].astype(F32)
        stage[i] = acc
        kept(i).start()
        sent(c, i).start()

        @pl.when(i == steps - 1)
        def _():
            if small is not None:
                finish_tables()
            for step in range(steps):
                kept(step).wait()
                sent(c, step).wait_send()
                sent(1 - c, step).wait_recv()

    t_in, t_out, t_scratch = [], [], []
    if small is not None:
        t_in, t_out = [VMEM_WHOLE], [jax.ShapeDtypeStruct((N_DEV, *small.shape), F32)]
        t_scratch = [pltpu.SemaphoreType.DMA((8,))] * 2
    res = pl.pallas_call(
        body, name=name,
        grid_spec=pltpu.PrefetchScalarGridSpec(
            num_scalar_prefetch=1, grid=(steps,),
            in_specs=[pl.BlockSpec((1, t, D_MODEL), lambda i, place_ref: (place_ref[0], i, 0)),
                      pl.BlockSpec((3, t, D_MODEL), lambda i, place_ref: (0, i, 0)), ANY] + t_in,
            out_specs=[ANY] * (1 + len(t_out)),
            scratch_shapes=[pltpu.VMEM((steps, t, D_MODEL), F32)] + [pltpu.SemaphoreType.DMA((steps,))] * 3 + t_scratch),
        out_shape=[jax.ShapeDtypeStruct((2 * half, D_MODEL), F32)] + t_out, compiler_params=_params(),
    )(place, pre, received, after, *([] if small is None else [small]))
    return res[0] if small is None else res


def _adam_update(w, g, m, v):
    m_new = ADAM_B1 * m + (1.0 - ADAM_B1) * g
    v_new = ADAM_B2 * v + (1.0 - ADAM_B2) * (g * g)
    m_hat = m_new / (1.0 - ADAM_B1 ** ADAM_STEP)
    v_hat = v_new / (1.0 - ADAM_B2 ** ADAM_STEP)
    return -ADAM_LR * (m_hat / (jnp.sqrt(v_hat) + ADAM_EPS) + ADAM_WD * w), m_new, v_new


def _adamw(w, g_rows, row_off, m, v, name):
    rows, cols = w.shape
    t = rows if rows <= 320 else (rows // 2 if rows % 256 else 256)

    def body(w_ref, g_ref, m_ref, v_ref, go_ref, d_ref, nm_ref, nv_ref):
        g = g_ref[...]
        go_ref[...] = g
        d_ref[...], nm_ref[...], nv_ref[...] = _adam_update(w_ref[...], g, m_ref[...], v_ref[...])

    blk = pl.BlockSpec((t, cols), lambda i: (i, 0))
    assert row_off % 8 == 0 and t % 8 == 0
    g_blk = pl.BlockSpec((pl.Element(t), pl.Element(cols)), lambda i: (pl.multiple_of(row_off + i * t, 8), 0))
    shape = jax.ShapeDtypeStruct((rows, cols), F32)
    return pl.pallas_call(
        body, name=name, grid=(rows // t,), in_specs=[blk, g_blk, blk, blk], out_specs=[blk] * 4, out_shape=[shape] * 4,
        compiler_params=_params(),
    )(w, g_rows, m, v)


SMALL_PARAMS = [("g_attn", (1, D_MODEL), 8), ("g_q", (1, HEAD_DIM), None), ("g_k", (1, HEAD_DIM), None),
                ("sinks", (1, N_Q_HEADS), None), ("rel_bias", (N_Q_HEADS, N_BUCKETS), None), ("w_pool", (512, 128), None),
                ("pool_scale", (1, POOL_WIDTH), 4), ("g_ffn", (1, D_MODEL), 8), ("g_ple", (1, D_MODEL), 8)]


def _adamw_small(tables, pool_tables, wmv):
    n_par = len(SMALL_PARAMS)

    def body(*refs):
        t_ref, p_ref = refs[:2]
        ins = refs[2:2 + 3 * n_par]
        loss_ref = refs[2 + 3 * n_par]
        outs = refs[3 + 3 * n_par:-1]
        tot_ref = refs[-1]

        def in_order(ref):
            total = ref[0]
            for d in range(1, ref.shape[0]):
                total = total + ref[d]
            return total

        tot_ref[...] = in_order(t_ref)
        loss_ref[...] = tot_ref[pl.ds(SMALL["loss"], 1), 0:1]
        for i, (name, shape, split) in enumerate(SMALL_PARAMS):
            g_ref, d_ref, nm_ref, nv_ref = outs[4 * i:4 * i + 4]
            row = SMALL.get(name)
            if name == "w_pool":
                g_ref[...] = in_order(p_ref)
            elif split:
                for k in range(split):
                    g_ref[:, 128 * k:128 * k + 128] = tot_ref[pl.ds(row + k, 1), :]
            else:
                g_ref[...] = tot_ref[pl.ds(row, shape[0]), 0:shape[1]]
            w_ref, m_ref, v_ref = ins[3 * i:3 * i + 3]
            d_ref[...], nm_ref[...], nv_ref[...] = _adam_update(w_ref[...], g_ref[...], m_ref[...], v_ref[...])

    shapes = [jax.ShapeDtypeStruct((1, 1), F32)]
    for _, shape, _ in SMALL_PARAMS:
        shapes += [jax.ShapeDtypeStruct(shape, F32)] * 4
    flat = [a for triple in wmv for a in triple]
    res = pl.pallas_call(
        body, name="adamw_small", in_specs=[VMEM_WHOLE] * (2 + 3 * n_par), out_specs=[VMEM_WHOLE] * len(shapes),
        out_shape=shapes, scratch_shapes=[pltpu.VMEM((SMALL_ROWS, 128), F32)],
    )(tables, pool_tables, *flat)
    return res[0], [res[1 + 4 * i:5 + 4 * i] for i in range(n_par)]


def _pack_ple_proj(shard):
    return shard.reshape(4, 64, 256).transpose(1, 0, 2).reshape(64, D_MODEL)


class _Reduction:
    def __init__(self, tag, place, ids=(None, None)):
        self.tag, self.place, self.ids = tag, place, ids

    def start(self, partial):
        self.partial = partial
        self.other = _rs_swap_halves(partial, "rs_swap_" + self.tag, self.ids[0])
        return partial

    def middle(self, after, small=None):
        res = _rs_add_halves(self.partial, self.other, self.place[1:], "rs_add_" + self.tag, after, small)
        self.pre, self.tables = (res, None) if small is None else res
        self.received = _rs_exchange_chips(self.pre, "rs_exchange_" + self.tag, self.ids[1])
        return self.pre

    def finish(self, after, small=None):
        return _rs_sum_chips(self.pre, self.received, self.place, "rs_sum_" + self.tag, after, small)


def _local_grads(x2, p2, tgt, wts, g_attn_norm, g_q, g_k, attn_sinks, rel_bias, w_pool, pool_scale, g_ffn_norm, g_ple_norm,
                 reduce_a):
    w_early, w_late = wts
    w_in = w_out = w_early
    bucket = jnp.asarray(_bucket_table())
    gq = jnp.tile(g_q, (1, 2))
    gk = jnp.tile(g_k, (1, 2))
    wpool = w_pool[0].astype(BF16)
    sinks = attn_sinks[0]
    bias_st = _bias_build(rel_bias.T, bucket)

    hn1 = _first_norm(x2, g_attn_norm)
    zqk, u, kn, vb, qst = _attn_in(hn1, gq, gk, w_in)
    ost = _attn_fwd(qst, kn, vb, bias_st, sinks)
    pooled, mix, h1, hn2 = _mix_out(u, ost, x2, w_out, wpool, pool_scale, g_ffn_norm)
    loss_v, dgate, dup, act, dh2, hn3, dgl, dw_plp, dh1, dg_ffn, dg_ple = _ffn_ple(hn2, h1, p2, tgt, w_late, g_ffn_norm,
                                                                                      g_ple_norm)

    late0, late_rows = GATHER_PARTS[1][0], SLAB_ROWS - GATHER_PARTS[1][0]
    partial_a = None
    for names, lefts, right in ((("gateT", "upT"), [dgate, dup], hn2), (("down",), [act], dh2), (("plg",), [hn3], dgl)):
        partial_a = _dw(lefts, right, "dw_" + names[0], partial_a, late_rows, [SLAB[name][0] - late0 for name in names])
    dw_plp = dw_plp.reshape(4, 64, N_CHIPS, 256).transpose(2, 1, 0, 3).reshape(N_CHIPS, 64, D_MODEL)
    partial_a = reduce_a.start(lax.dynamic_update_slice(partial_a, dw_plp, (0, SLAB["plp"][0] - late0, 0)))
    dost, du, dw_pool, dscale, partial_b = _mix_out_bwd(dh1, w_out, pooled, wpool, pool_scale, mix, partial_a)
    pre_a = reduce_a.middle(du, dw_pool.reshape(512, 128))
    dqst, dk, dv, dbias, dsink_rows = _attn_bwd(qst, kn, vb, dost, bias_st, sinks, pre_a)
    dx, dg_attn, dgq, dgk, partial_b = _attn_in_bwd(dqst, zqk, dk, dv, du, x2, dh1, hn1, partial_b, w_in, g_attn_norm, gq, gk)

    small = _small_pack(dg_attn, dg_ffn, dg_ple, dscale, dgq, dgk, dbias, dsink_rows, loss_v)
    return dx, partial_b, small


def kernel(x, p, w_in, w_out, g_attn_norm, g_q, g_k, attn_sinks, rel_bias, w_pool, pool_scale, g_ffn_norm, w_gate, w_up, w_down, g_ple_norm, w_ple_gate, w_ple_proj, loss_target, m_w_in, m_w_out, m_g_attn_norm, m_g_q, m_g_k, m_attn_sinks, m_rel_bias, m_w_pool, m_pool_scale, m_g_ffn_norm, m_w_gate, m_w_up, m_w_down, m_g_ple_norm, m_w_ple_gate, m_w_ple_proj, v_w_in, v_w_out, v_g_attn_norm, v_g_q, v_g_k, v_attn_sinks, v_rel_bias, v_w_pool, v_pool_scale, v_g_ffn_norm, v_w_gate, v_w_up, v_w_down, v_g_ple_norm, v_w_ple_gate, v_w_ple_proj):
    core = lax.axis_index("c").astype(jnp.int32).reshape(1)
    me = (2 * lax.axis_index("x") + lax.axis_index("y")).astype(jnp.int32).reshape(1)

    local_parts = [jnp.concatenate(pieces, axis=0).astype(BF16) for pieces in (
        [w_in[0].T, w_out[0]], [w_gate[0].T, w_up[0].T, w_down[0], w_ple_gate[0], _pack_ple_proj(w_ple_proj[0])])]
    wts = [(_ag_weights(local, 0, local.shape[0], name, collective_id), local, me)
           for local, name, collective_id in zip(local_parts, ("ag_early", "ag_late"), (1, 2))]

    place = jnp.concatenate([me, core])
    reduce_a = _Reduction("a", place, ids=(3, 4))
    dx, partial_b, small = _local_grads(x[0], p[0, 0], loss_target[0], wts, g_attn_norm, g_q, g_k, attn_sinks, rel_bias,
                                        w_pool, pool_scale, g_ffn_norm, g_ple_norm, reduce_a)
    reduce_b = _Reduction("b", place, ids=(6, 7))
    reduce_b.start(partial_b)
    grads_a, small_all = reduce_a.finish(partial_b, small)
    reduce_b.middle(grads_a)

    late0 = GATHER_PARTS[1][0]

    def rows(name):
        return grads_a, SLAB[name][0] - late0

    plp_rows = grads_a[SLAB["plp"][0] - late0:]
    big = {
        "w_gate": (w_gate, m_w_gate, v_w_gate, rows("gateT"), True),
        "w_up": (w_up, m_w_up, v_w_up, rows("upT"), True),
        "w_down": (w_down, m_w_down, v_w_down, rows("down"), False),
        "w_ple_gate": (w_ple_gate, m_w_ple_gate, v_w_ple_gate, rows("plg"), False),
        "w_ple_proj": (w_ple_proj, m_w_ple_proj, v_w_ple_proj,
                       (plp_rows.reshape(64, 4, 256).transpose(1, 0, 2).reshape(PLE_DIM, PLE_DIM), 0), False),
        "w_out": (w_out, m_w_out, v_w_out, None, False),
        "w_in": (w_in, m_w_in, v_w_in, None, True),
    }
    small_params = {
        "g_attn_norm": (g_attn_norm, m_g_attn_norm, v_g_attn_norm), "g_q": (g_q, m_g_q, v_g_q), "g_k": (g_k, m_g_k, v_g_k),
        "attn_sinks": (attn_sinks, m_attn_sinks, v_attn_sinks), "rel_bias": (rel_bias.T, m_rel_bias.T, v_rel_bias.T),
        "w_pool": tuple(a.reshape(512, 128) for a in (w_pool, m_w_pool, v_w_pool)),
        "pool_scale": (pool_scale, m_pool_scale, v_pool_scale), "g_ffn_norm": (g_ffn_norm, m_g_ffn_norm, v_g_ffn_norm),
        "g_ple_norm": (g_ple_norm, m_g_ple_norm, v_g_ple_norm),
    }

    grads, deltas, new_ms, new_vs = {}, {}, {}, {}
    out = grads_b = None
    for name, (w, m, v, g_src, transposed) in big.items():
        if g_src is None:
            if grads_b is None:
                grads_b = reduce_b.finish(out[-1])
            g_src = (grads_b, SLAB["out" if name == "w_out" else "inT"][0])
        view = (lambda a: a.T) if transposed else (lambda a: a)
        out = _adamw(view(w[0]), *g_src, view(m[0]), view(v[0]), "adamw_" + name)
        grads[name], deltas[name], new_ms[name], new_vs[name] = (view(a)[None] for a in out)

    loss, small_out = _adamw_small(small_all, reduce_a.tables, list(small_params.values()))
    for name, (g2, d, nm, nv) in zip(small_params, small_out):
        restore = {"w_pool": lambda a: a.reshape(w_pool.shape), "rel_bias": lambda a: a.T}.get(name, lambda a: a)
        grads[name], deltas[name], new_ms[name], new_vs[name] = (restore(a) for a in (g2, d, nm, nv))

    order = ["w_in", "w_out", "g_attn_norm", "g_q", "g_k", "attn_sinks", "rel_bias", "w_pool", "pool_scale", "g_ffn_norm",
             "w_gate", "w_up", "w_down", "g_ple_norm", "w_ple_gate", "w_ple_proj"]
    return (loss.reshape(()), dx[None], *[grads[n] for n in order], *[deltas[n] for n in order],
            *[new_ms[n] for n in order], *[new_vs[n] for n in order])
```

```python
import numpy as np
import jax
import jax.numpy as jnp
from jax import lax
from jax.experimental import pallas as pl
from jax.experimental.pallas import tpu as pltpu
from jax.experimental.pallas import tpu_sc as plsc

F32 = jnp.float32
BF16 = jnp.bfloat16
MESH = pl.DeviceIdType.MESH

D_MODEL = 1024
HEAD_DIM = 64
N_Q_HEADS = 8
ATTN_WIDTH = 512
POOL_WIDTH = 512
IN_WIDTH = 1280
D_FF = 2816
PLE_DIM = 256
FF_CHUNK = 1408
N_FF_CHUNKS = D_FF // FF_CHUNK
BLOCK = 128
N_BUCKETS = 32
MAX_DISTANCE = 128
EPS = 1e-6
NEG = -1e30
N_CHIPS = 4
N_DEV = 8

ADAM_LR = 0.001
ADAM_B1 = 0.9
ADAM_B2 = 0.999
ADAM_EPS = 1e-08
ADAM_WD = 0.01
ADAM_STEP = 10

SLAB = {"inT": (0, 320), "out": (320, 256), "gateT": (576, 704), "upT": (1280, 704), "down": (1984, 704),
        "plg": (2688, 256), "plp": (2944, 64)}
SLAB_ROWS = 3008
GATHER_PARTS = ((0, 576), (576, SLAB_ROWS))
POOL_HALO = 24

SMALL = {"g_attn": 0, "g_ffn": 8, "g_ple": 16, "pool_scale": 24, "g_q": 28, "g_k": 29, "sinks": 30, "loss": 31,
         "rel_bias": 32}
SMALL_ROWS = 64

VMEM_LIMIT_BIG = 60 * 1024 * 1024
VMEM_LIMIT = 48 * 1024 * 1024


def _params(vmem=VMEM_LIMIT, n_axes=1):
    return pltpu.CompilerParams(dimension_semantics=("arbitrary",) * n_axes, vmem_limit_bytes=vmem)


def _dot(a, b, ca, cb):
    return lax.dot_general(a, b, (((ca,), (cb,)), ((), ())), preferred_element_type=F32)


def _full(shape):
    return pl.BlockSpec(shape, lambda i: (0,) * len(shape))


ANY = pl.BlockSpec(memory_space=pl.ANY)
VMEM_WHOLE = pl.BlockSpec(memory_space=pltpu.VMEM)


W_SPECS = [ANY, ANY, pl.BlockSpec(memory_space=pltpu.SMEM)]


def _load_rows(w_refs, name, dst_ref, sems):
    slab_ref, local_ref, me_ref = w_refs
    off, rows = SLAB[name]
    slab_off = off - max(start for start, _ in GATHER_PARTS if start <= off)
    me = me_ref[0]
    for phase in ("start", "wait"):
        for j in range(N_CHIPS):
            dst = dst_ref.at[pl.ds(j * rows, rows), :]
            theirs = pltpu.make_async_copy(slab_ref.at[j, pl.ds(slab_off, rows), :], dst, sems.at[j])
            own = pltpu.make_async_copy(local_ref.at[pl.ds(slab_off, rows), :], dst, sems.at[j])

            @pl.when(me == j)
            def _():
                getattr(own, phase)()

            @pl.when(me != j)
            def _():
                getattr(theirs, phase)()


def _rms_fwd(x, g):
    r = lax.rsqrt(jnp.mean(x * x, axis=-1, keepdims=True) + EPS)
    return x * r * g


def _rms_bwd(x, g, dy):
    r = lax.rsqrt(jnp.mean(x * x, axis=-1, keepdims=True) + EPS)
    xn = x * r
    dyg = dy * g
    dx = r * (dyg - xn * jnp.mean(dyg * xn, axis=-1, keepdims=True))
    return dx, jnp.sum(dy * xn, axis=0, keepdims=True)


def _half_sum(v, lo):
    s_lo = jnp.sum(jnp.where(lo, v, 0.0), axis=-1, keepdims=True)
    s_hi = jnp.sum(jnp.where(lo, 0.0, v), axis=-1, keepdims=True)
    return jnp.where(lo, s_lo, s_hi)


def _half_sum_mxu(v):
    upper = lax.broadcasted_iota(jnp.int32, (128, 128), 0) < 64
    left = lax.broadcasted_iota(jnp.int32, (128, 128), 1) < 64
    ones = jnp.where(upper == left, 1.0, 0.0).astype(BF16)
    high = v.astype(BF16)
    low = (v - high.astype(F32)).astype(BF16)
    return _dot(high, ones, 1, 0) + _dot(low, ones, 1, 0)


def _pair_norm(zp, g, lo):
    r = lax.rsqrt(_half_sum(zp * zp, lo) * (1.0 / HEAD_DIM) + EPS)
    return zp * r * g


def _pair_norm_bwd(zp, g, dy):
    r = lax.rsqrt(_half_sum_mxu(zp * zp) * (1.0 / HEAD_DIM) + EPS)
    xn = zp * r
    dyg = dy * g
    dx = r * (dyg - xn * (_half_sum_mxu(dyg * xn) * (1.0 / HEAD_DIM)))
    return dx, jnp.sum(dy * xn, axis=0, keepdims=True)


def _pack_heads(pairs, lo):
    packed = [None] * 4
    for m in range(2):
        a, b = pairs[m], pairs[m + 2]
        packed[2 * m] = jnp.where(lo, a, pltpu.roll(b, 64, axis=1))
        packed[2 * m + 1] = jnp.where(lo, pltpu.roll(a, 64, axis=1), b)
    return packed


def _unpack_heads(packed, lo):
    pairs = [None] * 4
    for m in range(2):
        a, b = packed[2 * m], packed[2 * m + 1]
        pairs[m] = jnp.where(lo, a, pltpu.roll(b, 64, axis=1))
        pairs[m + 2] = jnp.where(lo, pltpu.roll(a, 64, axis=1), b)
    return pairs


def _expand_heads(packed):
    flat = packed.reshape(4 * BLOCK, 128)
    lo = lax.broadcasted_iota(jnp.int32, flat.shape, 1) < 64
    zero = jnp.zeros_like(flat)
    return jnp.concatenate([jnp.where(lo, flat, zero), jnp.where(lo, zero, flat)], axis=0)


def _fold_heads(stacked):
    half = 4 * BLOCK
    lo = lax.broadcasted_iota(jnp.int32, (half, 128), 1) < 64
    return jnp.where(lo, stacked[:half], stacked[half:]).reshape(4, BLOCK, 128)


def _sigmoid(v):
    return 1.0 / (1.0 + jnp.exp(-v))


def _pool_counts(tile, n_rows):
    t1 = tile * n_rows + lax.broadcasted_iota(jnp.int32, (n_rows, POOL_WIDTH), 0) + 1
    lane = lax.broadcasted_iota(jnp.int32, (n_rows, POOL_WIDTH), 1)
    win = jnp.where(lane < 128, 2, jnp.where(lane < 256, 4, jnp.where(lane < 384, 8, 16)))
    return jnp.minimum(t1, win).astype(F32)


def _first_norm(x2, g_attn):
    s_len = x2.shape[0]
    t = 512

    def body(x_ref, g_ref, hn_ref):
        hn_ref[...] = _rms_fwd(x_ref[...], g_ref[...]).astype(BF16)

    row = pl.BlockSpec((t, D_MODEL), lambda i: (i, 0))
    return pl.pallas_call(
        body, name="first_norm", grid=(s_len // t,), in_specs=[row, _full((1, D_MODEL))], out_specs=row,
        out_shape=jax.ShapeDtypeStruct((s_len, D_MODEL), BF16), compiler_params=_params(),
    )(x2, g_attn)


def _attn_in(hn1, gq, gk, wts):
    s_len = hn1.shape[0]
    t = 512

    def body(hn_ref, gq_ref, gk_ref, sl_ref, lo_ref, me_ref, zqk_ref, u_ref, kn_ref, v_ref, qst_ref, w_ref, sems):
        @pl.when(pl.program_id(0) == 0)
        def _():
            _load_rows((sl_ref, lo_ref, me_ref), "inT", w_ref, sems)

        z = _dot(hn_ref[...], w_ref[...], 1, 1)
        zqk_ref[...] = z[:, :640]
        u_ref[...] = z[:, 768:]
        v_ref[...] = z[:, 640:768].astype(BF16)
        lo = lax.broadcasted_iota(jnp.int32, (t, 128), 1) < 64
        kn_ref[...] = _pair_norm(z[:, 512:640], gk_ref[...], lo).astype(BF16)
        pairs = [_pair_norm(z[:, 128 * p:128 * p + 128], gq_ref[...], lo) for p in range(4)]
        for j, entry in enumerate(_pack_heads(pairs, lo)):
            qst_ref[j] = entry.astype(BF16)

    row = lambda w: pl.BlockSpec((t, w), lambda i: (i, 0))
    return pl.pallas_call(
        body, name="attn_in", grid=(s_len // t,),
        in_specs=[row(D_MODEL), _full((1, 128)), _full((1, 128))] + W_SPECS,
        out_specs=[row(640), row(POOL_WIDTH), row(128), row(128), pl.BlockSpec((4, t, 128), lambda i: (0, i, 0))],
        out_shape=[jax.ShapeDtypeStruct((s_len, 640), F32), jax.ShapeDtypeStruct((s_len, POOL_WIDTH), F32),
                   jax.ShapeDtypeStruct((s_len, 128), BF16), jax.ShapeDtypeStruct((s_len, 128), BF16),
                   jax.ShapeDtypeStruct((4, s_len, 128), BF16)],
        scratch_shapes=[pltpu.VMEM((IN_WIDTH, D_MODEL), BF16), pltpu.SemaphoreType.DMA((N_CHIPS,))],
        compiler_params=_params(),
    )(hn1, gq, gk, *wts)


def _bucket_table():
    i_idx = np.arange(BLOCK)[:, None]
    j_idx = np.arange(2 * BLOCK)[None, :]
    d = BLOCK + i_idx - j_idx
    n = np.maximum(d, 0)
    max_exact = N_BUCKETS // 2
    nf = np.maximum(n, 1).astype(np.float64)
    large = max_exact + (np.log(nf / max_exact) / np.log(MAX_DISTANCE / max_exact) * (N_BUCKETS - max_exact)).astype(np.int64)
    large = np.minimum(large, N_BUCKETS - 1)
    bucket = np.where(n < max_exact, n, large)
    return np.where((d >= 0) & (d < BLOCK), bucket, -1).astype(np.int32)


def _bias_build(rel_bias_t, bucket):
    def body(rb_ref, bucket_ref, out_ref):
        bk = bucket_ref[...]
        for h in range(N_Q_HEADS):
            acc = jnp.full((BLOCK, 2 * BLOCK), NEG, F32)
            for b in range(N_BUCKETS):
                acc = jnp.where(bk == b, rb_ref[h, b], acc)
            out_ref[0, pl.ds(h * BLOCK, BLOCK), :] = acc
            out_ref[1, pl.ds(h * BLOCK, BLOCK), :] = acc
            out_ref[1, pl.ds(h * BLOCK, BLOCK), 0:BLOCK] = jnp.full((BLOCK, BLOCK), NEG, F32)

    return pl.pallas_call(
        body, name="bias_build",
        in_specs=[pl.BlockSpec(memory_space=pltpu.SMEM), VMEM_WHOLE], out_specs=VMEM_WHOLE,
        out_shape=jax.ShapeDtypeStruct((2, N_Q_HEADS * BLOCK, 2 * BLOCK), F32),
    )(rel_bias_t, bucket)


def _head_softmax(s_ref, bias_ref, sink_ref, h):
    rows = pl.ds(pl.multiple_of(h * BLOCK, BLOCK), BLOCK)
    s = s_ref[rows, :] * (HEAD_DIM ** -0.5) + bias_ref[rows, :]
    sink = sink_ref[h]
    m = jnp.maximum(jnp.max(s, axis=-1, keepdims=True), sink)
    p = jnp.exp(s - m)
    e_sink = jnp.exp(sink - m)
    inv = 1.0 / (jnp.sum(p, axis=-1, keepdims=True) + e_sink)
    return rows, p * inv, e_sink * inv


ATTN_STEP_BLOCKS = 4
BAND = (N_Q_HEADS * BLOCK, 2 * BLOCK)


def _attn_specs():
    nb = ATTN_STEP_BLOCKS
    stacked = pl.BlockSpec((4, nb * BLOCK, 128), lambda i: (0, i, 0))
    kv = [pl.BlockSpec((BLOCK, 128), lambda i: (jnp.maximum(nb * i - 1, 0), 0)), pl.BlockSpec((nb * BLOCK, 128), lambda i: (i, 0))]
    consts = [_full((2,) + BAND), pl.BlockSpec(memory_space=pltpu.SMEM)]
    return stacked, kv, consts


def _step_blocks(i, kp_ref, kc_ref, vp_ref, vc_ref, bias_ref):
    blocks = []
    for b in range(ATTN_STEP_BLOCKS):
        if b == 0:
            k2 = jnp.concatenate([kp_ref[...], kc_ref[pl.ds(0, BLOCK), :]], axis=0)
            v2 = jnp.concatenate([vp_ref[...], vc_ref[pl.ds(0, BLOCK), :]], axis=0)
            bias = bias_ref.at[jnp.where(i == 0, 1, 0)]
        else:
            k2, v2, bias = kc_ref[pl.ds((b - 1) * BLOCK, 2 * BLOCK), :], vc_ref[pl.ds((b - 1) * BLOCK, 2 * BLOCK), :], bias_ref.at[0]
        blocks.append((pl.ds(b * BLOCK, BLOCK), k2, v2, bias))
    return blocks


def _attn_fwd(qst, kn, vb, bias_st, sinks):
    s_len = kn.shape[0]

    def body(q_ref, kp_ref, kc_ref, vp_ref, vc_ref, bias_ref, sink_ref, o_ref, s_ref, p_ref):
        for b, (rows, k2, v2, bias) in enumerate(_step_blocks(pl.program_id(0), kp_ref, kc_ref, vp_ref, vc_ref, bias_ref)):
            s_b, p_b = s_ref.at[b], p_ref.at[b]
            s_b[...] = _dot(_expand_heads(q_ref[:, rows, :]), k2, 1, 1)

            def head(h, carry):
                head_rows, probs, _ = _head_softmax(s_b, bias, sink_ref, h)
                p_b[head_rows, :] = probs.astype(BF16)
                return carry

            lax.fori_loop(0, N_Q_HEADS, head, 0, unroll=True)
            o_ref[:, rows, :] = _fold_heads(_dot(p_b[...], v2, 1, 0)).astype(BF16)

    stacked, kv, consts = _attn_specs()
    return pl.pallas_call(
        body, name="attn_fwd", grid=(s_len // (ATTN_STEP_BLOCKS * BLOCK),),
        in_specs=[stacked] + kv + kv + consts, out_specs=stacked,
        out_shape=jax.ShapeDtypeStruct((4, s_len, 128), BF16),
        scratch_shapes=[pltpu.VMEM((ATTN_STEP_BLOCKS,) + BAND, F32), pltpu.VMEM((ATTN_STEP_BLOCKS,) + BAND, BF16)],
        compiler_params=_params(),
    )(qst, kn, kn, vb, vb, bias_st, sinks)


def _mix_out(u, ost, x2, wts, wpool, pool_scale, g_ffn):
    s_len = x2.shape[0]
    t = 512
    n = t + 16

    def body(u_ref, o_ref, x_ref, sl_ref, lo_ref, me_ref, wp_ref, sc_ref, g_ref, pooled_ref, mix_ref, h1_ref, hn_ref,
             w_ref, ext_ref, st_ref, sems):
        i = pl.program_id(0)

        @pl.when(i == 0)
        def _():
            _load_rows((sl_ref, lo_ref, me_ref), "out", w_ref, sems)
            ext_ref[...] = jnp.zeros_like(ext_ref)
            st_ref[...] = jnp.zeros_like(st_ref)

        u_tile = u_ref[...]
        ext_ref[pl.ds(POOL_HALO, t), :] = u_tile
        st_ref[pl.ds(8, n), :] = ext_ref[pl.ds(8, n), :] + ext_ref[pl.ds(7, n), :]
        st_ref[pl.ds(8, n), 128:] = st_ref[pl.ds(8, n), 128:] + st_ref[pl.ds(6, n), 128:]
        st_ref[pl.ds(8, n), 256:] = st_ref[pl.ds(8, n), 256:] + st_ref[pl.ds(4, n), 256:]
        st_ref[pl.ds(8, n), 384:] = st_ref[pl.ds(8, n), 384:] + st_ref[pl.ds(0, n), 384:]
        ext_ref[pl.ds(0, POOL_HALO), :] = ext_ref[pl.ds(t, POOL_HALO), :]
        pooled = (st_ref[pl.ds(POOL_HALO, t), :] / _pool_counts(i, t) - u_tile).astype(BF16)
        pooled_ref[...] = pooled
        for g in range(4):
            cols = slice(128 * g, 128 * g + 128)
            y = _dot(pooled[:, cols], wp_ref[g], 1, 0) * sc_ref[:, cols]
            mix_ref[:, ATTN_WIDTH + 128 * g:ATTN_WIDTH + 128 * g + 128] = y.astype(BF16)
        lo = lax.broadcasted_iota(jnp.int32, (t, 128), 1) < 64
        for p, pair in enumerate(_unpack_heads([o_ref[j].astype(F32) for j in range(4)], lo)):
            mix_ref[:, 128 * p:128 * p + 128] = pair.astype(BF16)
        h1 = x_ref[...] + _dot(mix_ref[...], w_ref[...], 1, 0)
        h1_ref[...] = h1
        hn_ref[...] = _rms_fwd(h1, g_ref[...]).astype(BF16)

    row = lambda w: pl.BlockSpec((t, w), lambda i: (i, 0))
    return pl.pallas_call(
        body, name="mix_out", grid=(s_len // t,),
        in_specs=[row(POOL_WIDTH), pl.BlockSpec((4, t, 128), lambda i: (0, i, 0)), row(D_MODEL)] + W_SPECS
        + [_full((4, 128, 128)), _full((1, POOL_WIDTH)), _full((1, D_MODEL))],
        out_specs=[row(POOL_WIDTH), row(D_MODEL), row(D_MODEL), row(D_MODEL)],
        out_shape=[jax.ShapeDtypeStruct((s_len, POOL_WIDTH), BF16), jax.ShapeDtypeStruct((s_len, D_MODEL), BF16),
                   jax.ShapeDtypeStruct((s_len, D_MODEL), F32), jax.ShapeDtypeStruct((s_len, D_MODEL), BF16)],
        scratch_shapes=[pltpu.VMEM((D_MODEL, D_MODEL), BF16), pltpu.VMEM((t + POOL_HALO, POOL_WIDTH), F32),
                        pltpu.VMEM((t + POOL_HALO, POOL_WIDTH), F32), pltpu.SemaphoreType.DMA((N_CHIPS,))],
        compiler_params=_params(),
    )(u, ost, x2, *wts, wpool, pool_scale, g_ffn)


def _ffn_ple(hn2, h1, p2, tgt, wts, g_ffn, g_ple):
    s_len = h1.shape[0]
    t = 256
    n_tiles = s_len // t

    def body(hn_ref, h1_ref, p_ref, tgt_ref, sl_ref, lo_ref, me_ref, gf_ref, gp_ref,
             loss_ref, dgate_ref, dup_ref, act_ref, dh2b_ref, hn3_ref, dgl_ref, dwp_ref, dh1_ref, dgf_ref, dgp_ref,
             wg_ref, wu_ref, wd_ref, wl_ref, wp_ref, packed_ref, gate_s, up_s, loss_acc, dwp_acc, sems):
        i = pl.program_id(0)

        @pl.when(i == 0)
        def _():
            w_refs = (sl_ref, lo_ref, me_ref)
            _load_rows(w_refs, "gateT", wg_ref, sems)
            _load_rows(w_refs, "upT", wu_ref, sems)
            _load_rows(w_refs, "down", wd_ref, sems)
            _load_rows(w_refs, "plg", wl_ref, sems)
            _load_rows(w_refs, "plp", packed_ref, sems)
            for j in range(N_CHIPS):
                for q in range(4):
                    wp_ref[pl.ds(64 * q, 64), 256 * j:256 * j + 256] = packed_ref[pl.ds(64 * j, 64), 256 * q:256 * q + 256]
            loss_acc[...] = jnp.zeros_like(loss_acc)
            dwp_acc[...] = jnp.zeros_like(dwp_acc)
            dgf_ref[...] = jnp.zeros_like(dgf_ref)
            dgp_ref[...] = jnp.zeros_like(dgp_ref)

        hn = hn_ref[...]
        h1v = h1_ref[...]
        chunks = [slice(ch * FF_CHUNK, (ch + 1) * FF_CHUNK) for ch in range(N_FF_CHUNKS)]
        gate = _dot(hn, wg_ref[...], 1, 1)
        up = _dot(hn, wu_ref[...], 1, 1)
        gate_s[...] = gate
        up_s[...] = up
        act = (gate * _sigmoid(gate) * up).astype(BF16)
        for ch, cols in enumerate(chunks):
            act_ref[ch] = act[:, cols]
        h2 = h1v + _dot(act, wd_ref[...], 1, 0)
        gp = gp_ref[...]
        hn3 = _rms_fwd(h2, gp).astype(BF16)
        hn3_ref[...] = hn3
        gate2 = _sigmoid(_dot(hn3, wl_ref[...], 1, 0))
        p_tile = p_ref[...].astype(BF16)
        pp = _dot(p_tile, wp_ref[...], 1, 0)
        err = h2 + gate2 * pp - tgt_ref[...]
        loss_acc[...] += jnp.sum(err * err, axis=0, keepdims=True)
        dy = err * (1.0 / D_MODEL)
        dwp_acc[...] += _dot(p_tile, (dy * gate2).astype(BF16), 0, 0)
        dgl = (dy * pp * gate2 * (1.0 - gate2)).astype(BF16)
        dgl_ref[...] = dgl
        dx3, dg3 = _rms_bwd(h2, gp, _dot(dgl, wl_ref[...], 1, 1))
        dh2 = dy + dx3
        dgp_ref[...] += dg3
        dh2b = dh2.astype(BF16)
        dh2b_ref[...] = dh2b
        dact = _dot(dh2b, wd_ref[...], 1, 1)
        gate_v = gate_s[...]
        up_v = up_s[...]
        sg = _sigmoid(gate_v)
        dup = (dact * (gate_v * sg)).astype(BF16)
        dgate = (dact * up_v * (sg * (1.0 + gate_v * (1.0 - sg)))).astype(BF16)
        for ch, cols in enumerate(chunks):
            dup_ref[ch] = dup[:, cols]
            dgate_ref[ch] = dgate[:, cols]
        dhn = _dot(dgate, wg_ref[...], 1, 0) + _dot(dup, wu_ref[...], 1, 0)
        dx, dg = _rms_bwd(h1v, gf_ref[...], dhn)
        dh1_ref[...] = dh2 + dx
        dgf_ref[...] += dg

        @pl.when(i == n_tiles - 1)
        def _():
            total = jnp.sum(loss_acc[...], axis=-1, keepdims=True) * (0.5 / D_MODEL)
            loss_ref[...] = jnp.broadcast_to(total, loss_ref.shape)
            dwp_ref[...] = dwp_acc[...].astype(BF16)

    row = lambda w: pl.BlockSpec((t, w), lambda i: (i, 0))
    chunked = pl.BlockSpec((N_FF_CHUNKS, t, FF_CHUNK), lambda i: (0, i, 0))
    vec = _full((1, D_MODEL))
    act_shape = jax.ShapeDtypeStruct((N_FF_CHUNKS, s_len, FF_CHUNK), BF16)
    tok = lambda dtype: jax.ShapeDtypeStruct((s_len, D_MODEL), dtype)
    return pl.pallas_call(
        body, name="ffn_ple", grid=(n_tiles,),
        in_specs=[row(D_MODEL), row(D_MODEL), row(PLE_DIM), row(D_MODEL)] + W_SPECS + [vec, vec],
        out_specs=[_full((1, 128)), chunked, chunked, chunked] + [row(D_MODEL)] * 3 + [_full((PLE_DIM, D_MODEL)), row(D_MODEL),
                                                                                       vec, vec],
        out_shape=[jax.ShapeDtypeStruct((1, 128), F32), act_shape, act_shape, act_shape, tok(BF16), tok(BF16), tok(BF16),
                   jax.ShapeDtypeStruct((PLE_DIM, D_MODEL), BF16), tok(F32), jax.ShapeDtypeStruct((1, D_MODEL), F32),
                   jax.ShapeDtypeStruct((1, D_MODEL), F32)],
        scratch_shapes=[pltpu.VMEM((D_FF, D_MODEL), BF16)] * 3
        + [pltpu.VMEM((D_MODEL, D_MODEL), BF16), pltpu.VMEM((PLE_DIM, D_MODEL), BF16), pltpu.VMEM((PLE_DIM, D_MODEL), BF16),
           pltpu.VMEM((t, D_FF), F32), pltpu.VMEM((t, D_FF), F32), pltpu.VMEM((1, D_MODEL), F32),
           pltpu.VMEM((PLE_DIM, D_MODEL), F32), pltpu.SemaphoreType.DMA((N_CHIPS,))],
        compiler_params=_params(VMEM_LIMIT_BIG),
    )(hn2, h1, p2, tgt, *wts, g_ffn, g_ple)


def _flush_chunks(acc_ref, stage_ref, slab_ref, name, sems):
    stage_ref[...] = acc_ref[...].astype(BF16)
    off, rows = SLAB[name]
    copies = [pltpu.make_async_copy(stage_ref.at[pl.ds(j * rows, rows), :], slab_ref.at[j, pl.ds(off, rows), :], sems.at[j])
              for j in range(N_CHIPS)]
    for cp in copies:
        cp.start()
    for cp in copies:
        cp.wait()


def _mix_out_bwd(dh1, wts, pooled, wpool, pool_scale, mix, after):
    s_len = dh1.shape[0]
    t = 512
    n = t + 16
    n_tiles = s_len // t
    early_rows = GATHER_PARTS[0][1]

    def body(dh1_ref, sl_ref, lo_ref, me_ref, pooled_ref, wp_ref, sc_ref, mix_ref, after_ref, dost_ref, du_ref, dwp_ref,
             dsc_ref, slab_ref, w_ref, ext_ref, st_ref, acc_ref, stage_ref, sems):
        del after_ref
        i = pl.program_id(0)

        @pl.when(i == 0)
        def _():
            _load_rows((sl_ref, lo_ref, me_ref), "out", w_ref, sems)
            ext_ref[...] = jnp.zeros_like(ext_ref)
            st_ref[...] = jnp.zeros_like(st_ref)
            dsc_ref[...] = jnp.zeros_like(dsc_ref)
            dwp_ref[...] = jnp.zeros_like(dwp_ref)
            acc_ref[...] = jnp.zeros_like(acc_ref)

        dh1b = dh1_ref[...].astype(BF16)
        acc_ref[...] += _dot(mix_ref[...], dh1b, 0, 0)
        dmix = _dot(dh1b, w_ref[...], 1, 1)
        lo = lax.broadcasted_iota(jnp.int32, (t, 128), 1) < 64
        for j, entry in enumerate(_pack_heads([dmix[:, 128 * p:128 * p + 128] for p in range(4)], lo)):
            dost_ref[j] = entry.astype(BF16)
        pooled_v = pooled_ref[...]
        counts = _pool_counts(n_tiles - 1 - i, t)
        for g in range(4):
            cols = slice(128 * g, 128 * g + 128)
            dm = dmix[:, ATTN_WIDTH + 128 * g:ATTN_WIDTH + 128 * g + 128]
            ypre = _dot(pooled_v[:, cols], wp_ref[g], 1, 0)
            dsc_ref[:, cols] += jnp.sum(ypre * dm, axis=0, keepdims=True)
            dyp = (dm * sc_ref[:, cols]).astype(BF16)
            dwp_ref[g] += _dot(pooled_v[:, cols], dyp, 0, 0)
            dpooled = _dot(dyp, wp_ref[g], 1, 1)
            du_ref[:, cols] = -dpooled
            ext_ref[pl.ds(0, t), cols] = dpooled / counts[:, cols]
        st_ref[pl.ds(0, n), :] = ext_ref[pl.ds(0, n), :] + ext_ref[pl.ds(1, n), :]
        st_ref[pl.ds(0, n), 128:] = st_ref[pl.ds(0, n), 128:] + st_ref[pl.ds(2, n), 128:]
        st_ref[pl.ds(0, n), 256:] = st_ref[pl.ds(0, n), 256:] + st_ref[pl.ds(4, n), 256:]
        st_ref[pl.ds(0, n), 384:] = st_ref[pl.ds(0, n), 384:] + st_ref[pl.ds(8, n), 384:]
        ext_ref[pl.ds(t, POOL_HALO), :] = ext_ref[pl.ds(0, POOL_HALO), :]
        du_ref[...] += st_ref[pl.ds(0, t), :]

        @pl.when(i == n_tiles - 1)
        def _():
            _flush_chunks(acc_ref, stage_ref, slab_ref, "out", sems)

    rev = lambda w: pl.BlockSpec((t, w), lambda i: (n_tiles - 1 - i, 0))
    return pl.pallas_call(
        body, name="mix_out_bwd", grid=(n_tiles,),
        in_specs=[rev(D_MODEL)] + W_SPECS + [rev(POOL_WIDTH), _full((4, 128, 128)), _full((1, POOL_WIDTH)), rev(D_MODEL), ANY],
        out_specs=[pl.BlockSpec((4, t, 128), lambda i: (0, n_tiles - 1 - i, 0)), rev(POOL_WIDTH),
                   _full((4, 128, 128)), _full((1, POOL_WIDTH)), ANY],
        out_shape=[jax.ShapeDtypeStruct((4, s_len, 128), BF16), jax.ShapeDtypeStruct((s_len, POOL_WIDTH), F32),
                   jax.ShapeDtypeStruct((4, 128, 128), F32), jax.ShapeDtypeStruct((1, POOL_WIDTH), F32),
                   jax.ShapeDtypeStruct((N_CHIPS, early_rows, D_MODEL), BF16)],
        scratch_shapes=[pltpu.VMEM((D_MODEL, D_MODEL), BF16), pltpu.VMEM((t + POOL_HALO, POOL_WIDTH), F32),
                        pltpu.VMEM((t + POOL_HALO, POOL_WIDTH), F32), pltpu.VMEM((D_MODEL, D_MODEL), F32),
                        pltpu.VMEM((D_MODEL, D_MODEL), BF16), pltpu.SemaphoreType.DMA((N_CHIPS,))],
        compiler_params=_params(),
    )(dh1, *wts, pooled, wpool, pool_scale, mix, after)


def _attn_bwd(qst, kn, vb, dost, bias_st, sinks, after):
    s_len = kn.shape[0]

    def body(q_ref, kp_ref, kc_ref, vp_ref, vc_ref, do_ref, bias_ref, sink_ref, after_ref, dq_ref, dk_ref, dv_ref, dbias_ref,
             dsink_ref, s_ref, dp_ref, p_ref, dl_ref):
        del after_ref
        i = pl.program_id(0)

        @pl.when(i == 0)
        def _():
            dk_ref[...] = jnp.zeros_like(dk_ref)
            dv_ref[...] = jnp.zeros_like(dv_ref)
            dbias_ref[...] = jnp.zeros_like(dbias_ref)
            dsink_ref[...] = jnp.zeros_like(dsink_ref)

        for b, (rows, k2, v2, bias) in enumerate(_step_blocks(i, kp_ref, kc_ref, vp_ref, vc_ref, bias_ref)):
            s_b, dp_b, p_b, dl_b = s_ref.at[b], dp_ref.at[b], p_ref.at[b], dl_ref.at[b]
            q = _expand_heads(q_ref[:, rows, :])
            do = _expand_heads(do_ref[:, rows, :])
            s_b[...] = _dot(q, k2, 1, 1)
            dp_b[...] = _dot(do, v2, 1, 1)

            def head(h, carry):
                head_rows, probs, p_sink = _head_softmax(s_b, bias, sink_ref, h)
                dp = dp_b[head_rows, :]
                dsum = jnp.sum(probs * dp, axis=-1, keepdims=True)
                dlog = probs * (dp - dsum)
                dsink_ref[head_rows, :] -= p_sink * dsum
                dbias_ref[head_rows, :] += dlog
                p_b[head_rows, :] = probs.astype(BF16)
                dl_b[head_rows, :] = (dlog * (HEAD_DIM ** -0.5)).astype(BF16)
                return carry

            lax.fori_loop(0, N_Q_HEADS, head, 0, unroll=True)
            dlog_s = dl_b[...]
            dq_ref[:, rows, :] = _fold_heads(_dot(dlog_s, k2, 1, 0))
            dk2 = _dot(dlog_s, q, 0, 0)
            dv2 = _dot(p_b[...], do, 0, 0)
            block = ATTN_STEP_BLOCKS * i + b
            prev_rows = pl.ds(pl.multiple_of(jnp.maximum(block - 1, 0) * BLOCK, BLOCK), BLOCK)
            cur_rows = pl.ds(pl.multiple_of(block * BLOCK, BLOCK), BLOCK)
            dk_ref[prev_rows, :] += dk2[:BLOCK]
            dk_ref[cur_rows, :] += dk2[BLOCK:]
            dv_ref[prev_rows, :] += dv2[:BLOCK]
            dv_ref[cur_rows, :] += dv2[BLOCK:]

    stacked, kv, consts = _attn_specs()
    per_step = (ATTN_STEP_BLOCKS,) + BAND
    return pl.pallas_call(
        body, name="attn_bwd", grid=(s_len // (ATTN_STEP_BLOCKS * BLOCK),),
        in_specs=[stacked] + kv + kv + [stacked] + consts + [ANY],
        out_specs=[stacked, _full((s_len, 128)), _full((s_len, 128)), _full(BAND), _full((N_Q_HEADS * BLOCK, 1))],
        out_shape=[jax.ShapeDtypeStruct((4, s_len, 128), F32), jax.ShapeDtypeStruct((s_len, 128), F32),
                   jax.ShapeDtypeStruct((s_len, 128), F32), jax.ShapeDtypeStruct(BAND, F32),
                   jax.ShapeDtypeStruct((N_Q_HEADS * BLOCK, 1), F32)],
        scratch_shapes=[pltpu.VMEM(per_step, F32), pltpu.VMEM(per_step, F32), pltpu.VMEM(per_step, BF16),
                        pltpu.VMEM(per_step, BF16)],
        compiler_params=_params(),
    )(qst, kn, kn, vb, vb, dost, bias_st, sinks, after)


def _flip_rows(x):
    n = x.shape[0]
    exchange = (lax.broadcasted_iota(jnp.int32, (n, n), 0) + lax.broadcasted_iota(jnp.int32, (n, n), 1) == n - 1)
    exchange = jnp.where(exchange, 1.0, 0.0).astype(BF16)
    flipped, rest = None, x
    for _ in range(3):
        term = rest.astype(BF16)
        rest = rest - term.astype(F32)
        part = _dot(exchange, term, 1, 0)
        flipped = part if flipped is None else flipped + part
    return flipped


def _small_pack(dg_attn, dg_ffn, dg_ple, dscale, dgq, dgk, dbias, dsink_rows, loss_v):
    def body(ga_ref, gf_ref, gp_ref, sc_ref, gq_ref, gk_ref, db_ref, ds_ref, bucket_ref, loss_ref, out_ref):
        out_ref[...] = jnp.zeros((SMALL_ROWS, 128), F32)
        for name, ref, n in (("g_attn", ga_ref, 8), ("g_ffn", gf_ref, 8), ("g_ple", gp_ref, 8), ("pool_scale", sc_ref, 4)):
            for k in range(n):
                out_ref[pl.ds(SMALL[name] + k, 1), :] = ref[:, 128 * k:128 * k + 128]
        for name, ref in (("g_q", gq_ref), ("g_k", gk_ref)):
            both = ref[...]
            out_ref[pl.ds(SMALL[name], 1), :] = both + pltpu.roll(both, 64, axis=1)
        out_ref[pl.ds(SMALL["loss"], 1), :] = loss_ref[...]
        by_diagonal = lambda flipped: pltpu.roll(flipped, 0, 1, stride=1, stride_axis=0)
        bucket_of = jnp.max(by_diagonal(bucket_ref[...]), axis=0, keepdims=True)
        sums = jnp.concatenate([jnp.sum(by_diagonal(_flip_rows(db_ref[pl.ds(h * BLOCK, BLOCK), :])), axis=0, keepdims=True)
                                for h in range(N_Q_HEADS)], axis=0)
        lanes = lax.broadcasted_iota(jnp.int32, (N_Q_HEADS, 128), 1)
        lane1 = lax.broadcasted_iota(jnp.int32, (1, 128), 1)
        rb = jnp.zeros((N_Q_HEADS, 128), F32)
        for b in range(N_BUCKETS):
            rb = jnp.where(lanes == b, jnp.sum(jnp.where(bucket_of == float(b), sums, 0.0), axis=1, keepdims=True), rb)
        sk = jnp.zeros((1, 128), F32)
        for h in range(N_Q_HEADS):
            sk = jnp.where(lane1 == h, jnp.sum(ds_ref[pl.ds(h * BLOCK, BLOCK), :]), sk)
        out_ref[pl.ds(SMALL["rel_bias"], N_Q_HEADS), :] = rb
        out_ref[pl.ds(SMALL["sinks"], 1), :] = sk

    bucket = jnp.asarray(_bucket_table()[::-1].astype(np.float32))
    return pl.pallas_call(
        body, name="small_pack", in_specs=[VMEM_WHOLE] * 10, out_specs=VMEM_WHOLE,
        out_shape=jax.ShapeDtypeStruct((SMALL_ROWS, 128), F32),
    )(dg_attn, dg_ffn, dg_ple, dscale, dgq, dgk, dbias, dsink_rows, bucket, loss_v)


def _attn_in_bwd(dqst, zqk, dk, dv, du, x2, dh1, hn1, slab, wts, g_attn, gq, gk):
    s_len = x2.shape[0]
    t = 512
    n_tiles = s_len // t

    def body(dq_ref, zqk_ref, dk_ref, dv_ref, du_ref, x_ref, dh1_ref, hn_ref, slab_in_ref, sl_ref, lo_ref, me_ref, g_ref,
             gq_ref, gk_ref, dx_ref, dg_ref, dgq_ref, dgk_ref, slab_ref, w_ref, dz_ref, acc_ref, stage_ref, sems):
        del slab_in_ref
        i = pl.program_id(0)

        @pl.when(i == 0)
        def _():
            _load_rows((sl_ref, lo_ref, me_ref), "inT", w_ref, sems)
            dg_ref[...] = jnp.zeros_like(dg_ref)
            dgq_ref[...] = jnp.zeros_like(dgq_ref)
            dgk_ref[...] = jnp.zeros_like(dgk_ref)
            acc_ref[...] = jnp.zeros_like(acc_ref)

        lo = lax.broadcasted_iota(jnp.int32, (t, 128), 1) < 64
        for p, dqn in enumerate(_unpack_heads([dq_ref[j] for j in range(4)], lo)):
            dq_raw, dgq = _pair_norm_bwd(zqk_ref[:, 128 * p:128 * p + 128], gq_ref[...], dqn)
            dz_ref[:, 128 * p:128 * p + 128] = dq_raw.astype(BF16)
            dgq_ref[...] += dgq
        dk_raw, dgk = _pair_norm_bwd(zqk_ref[:, 512:640], gk_ref[...], dk_ref[...])
        dgk_ref[...] += dgk
        dz_ref[:, 512:640] = dk_raw.astype(BF16)
        dz_ref[:, 640:768] = dv_ref[...].astype(BF16)
        dz_ref[:, 768:] = du_ref[...].astype(BF16)
        dz = dz_ref[...]
        acc_ref[...] += _dot(dz, hn_ref[...], 0, 0)
        dx, dg = _rms_bwd(x_ref[...], g_ref[...], _dot(dz, w_ref[...], 1, 0))
        dx_ref[...] = dh1_ref[...] + dx
        dg_ref[...] += dg

        @pl.when(i == n_tiles - 1)
        def _():
            _flush_chunks(acc_ref, stage_ref, slab_ref, "inT", sems)

    row = lambda w: pl.BlockSpec((t, w), lambda i: (i, 0))
    return pl.pallas_call(
        body, name="attn_in_bwd", grid=(n_tiles,),
        in_specs=[pl.BlockSpec((4, t, 128), lambda i: (0, i, 0)), row(640), row(128), row(128), row(POOL_WIDTH),
                  row(D_MODEL), row(D_MODEL), row(D_MODEL), ANY] + W_SPECS + [_full((1, D_MODEL)), _full((1, 128)),
                                                                              _full((1, 128))],
        out_specs=[row(D_MODEL), _full((1, D_MODEL)), _full((1, 128)), _full((1, 128)), ANY],
        out_shape=[jax.ShapeDtypeStruct((s_len, D_MODEL), F32), jax.ShapeDtypeStruct((1, D_MODEL), F32),
                   jax.ShapeDtypeStruct((1, 128), F32), jax.ShapeDtypeStruct((1, 128), F32),
                   jax.ShapeDtypeStruct(slab.shape, BF16)],
        input_output_aliases={8: 4},
        scratch_shapes=[pltpu.VMEM((IN_WIDTH, D_MODEL), BF16), pltpu.VMEM((t, IN_WIDTH), BF16),
                        pltpu.VMEM((IN_WIDTH, D_MODEL), F32), pltpu.VMEM((IN_WIDTH, D_MODEL), BF16),
                        pltpu.SemaphoreType.DMA((N_CHIPS,))],
        compiler_params=_params(),
    )(dqst, zqk, dk, dv, du, x2, dh1, hn1, slab, *wts, g_attn, gq, gk)


def _dw(lefts, b, name, slab, slab_rows, row_offs):
    a0, n_a = lefts[0], len(lefts)
    assert b.shape[1] == D_MODEL
    if a0.ndim == 3:
        n_chunks, s_len, tm = a0.shape
        m = n_chunks * tm
    else:
        s_len, tm = a0.shape
        m = tm
    tk = 2048 if n_a * tm <= 1408 else 1024
    if a0.ndim == 3:
        a_spec = pl.BlockSpec((None, tk, tm), lambda i, k: (i, k, 0))
    else:
        a_spec = pl.BlockSpec((tk, tm), lambda i, k: (k, i))
    n_steps, n_tiles = s_len // tk, m // tm
    chunk = m // N_CHIPS
    per_tile = tm // chunk

    def body(*refs):
        a_refs, b_ref = refs[:n_a], refs[n_a]
        o_ref, acc_ref, stage_ref, sems = refs[-4:]
        i, k = pl.program_id(0), pl.program_id(1)
        products = lambda: [_dot(a_ref[...].astype(BF16), b_ref[...].astype(BF16), 0, 0) for a_ref in a_refs]

        @pl.when(k == 0)
        def _():
            for w, product in enumerate(products()):
                acc_ref[w] = product

        @pl.when(k > 0)
        def _():
            for w, product in enumerate(products()):
                acc_ref[w] += product

        def out_copies(tile, slot):
            return [pltpu.make_async_copy(stage_ref.at[slot, w, pl.ds(jj * chunk, chunk), :],
                                          o_ref.at[tile * per_tile + jj, pl.ds(row_offs[w], chunk), :], sems.at[slot, w, jj])
                    for w in range(n_a) for jj in range(per_tile)]

        @pl.when(k == n_steps - 1)
        def _():
            slot = i % 2

            @pl.when(i >= 2)
            def _():
                for cp in out_copies(i - 2, slot):
                    cp.wait()

            stage_ref[slot] = acc_ref[...].astype(BF16)
            for cp in out_copies(i, slot):
                cp.start()

            @pl.when(i == n_tiles - 1)
            def _():
                for cp in out_copies(i, slot):
                    cp.wait()
                if n_tiles > 1:
                    for cp in out_copies(i - 1, 1 - slot):
                        cp.wait()

    in_specs = [a_spec] * n_a + [pl.BlockSpec((tk, D_MODEL), lambda i, k: (k, 0))]
    operands, aliases = [*lefts, b], {}
    if slab is not None:
        in_specs.append(ANY)
        operands.append(slab)
        aliases = {n_a + 1: 0}
    return pl.pallas_call(
        body, name=name, grid=(n_tiles, n_steps), in_specs=in_specs, out_specs=ANY,
        out_shape=jax.ShapeDtypeStruct((N_CHIPS, slab_rows, D_MODEL), BF16), input_output_aliases=aliases,
        scratch_shapes=[pltpu.VMEM((n_a, tm, D_MODEL), F32), pltpu.VMEM((2, n_a, tm, D_MODEL), BF16),
                        pltpu.SemaphoreType.DMA((2, n_a, per_tile))],
        compiler_params=_params(VMEM_LIMIT_BIG, n_axes=2),
    )(*operands)


def _position():
    x, y, c = lax.axis_index("x"), lax.axis_index("y"), lax.axis_index("c")
    other_chips = [(1 - x, y), (x, 1 - y), (1 - x, 1 - y)]
    return x, y, c, other_chips


def _ag_weights(local_slab, row0, n_rows, name, collective_id):
    half = n_rows // 2
    quarter = half // 2
    assert quarter % 16 == 0

    def body(l_ref, g_ref, send, recv):
        x, y, c, chips = _position()
        me, (via_x, via_y, diagonal) = 2 * x + y, [2 * chip[0] + chip[1] for chip in chips]
        here, sibling, x_nbr, y_nbr = (x, y, c), (x, y, 1 - c), (1 - x, y, c), (x, 1 - y, c)
        peers = [sibling, x_nbr, y_nbr]
        barrier = pltpu.get_barrier_semaphore()
        for peer in peers:
            pl.semaphore_signal(barrier, inc=1, device_id=peer, device_id_type=MESH)
        pl.semaphore_wait(barrier, len(peers))

        def rows(core, part):
            start, size = (core * half, half) if part is None else (core * half + part * quarter, quarter)
            return pl.ds(pl.multiple_of(start, 16), size)

        def copy(k, chip_idx, where, to, src=None):
            dst = g_ref.at[chip_idx, where, :]
            return pltpu.make_async_remote_copy(src_ref=dst if src is None else src, dst_ref=dst, send_sem=send.at[k],
                                                recv_sem=recv.at[k], device_id=to, device_id_type=MESH)

        own_rows = l_ref.at[pl.ds(pl.multiple_of(row0 + c * half, 16), half), :]
        started = [copy(0, me, rows(c, None), x_nbr, src=own_rows), copy(1, me, rows(c, None), y_nbr, src=own_rows)]
        for cp in started:
            cp.start()
        after_arrival = [
            (copy(0, via_x, rows(c, None), here), [copy(4, via_x, rows(c, None), sibling), copy(3, via_x, rows(c, 1), y_nbr)]),
            (copy(1, via_y, rows(c, None), here), [copy(5, via_y, rows(c, None), sibling), copy(2, via_y, rows(c, 0), x_nbr)]),
            (copy(2, diagonal, rows(c, 0), here), [copy(6, diagonal, rows(c, 0), sibling)]),
            (copy(3, diagonal, rows(c, 1), here), [copy(7, diagonal, rows(c, 1), sibling)]),
        ]
        for arrival, onward in after_arrival:
            arrival.wait_recv()
            for cp in onward:
                cp.start()
            started += onward
        for cp in (copy(4, via_x, rows(1 - c, None), here), copy(5, via_y, rows(1 - c, None), here),
                   copy(6, diagonal, rows(1 - c, 0), here), copy(7, diagonal, rows(1 - c, 1), here)):
            cp.wait_recv()
        for cp in started:
            cp.wait_send()

    return pl.kernel(
        body, out_type=jax.ShapeDtypeStruct((N_CHIPS, n_rows, D_MODEL), BF16),
        mesh=plsc.ScalarSubcoreMesh(axis_name="sequencer", num_cores=1), name=name,
        scratch_types=[pltpu.SemaphoreType.DMA((8,)), pltpu.SemaphoreType.DMA((8,))],
        compiler_params=pltpu.CompilerParams(collective_id=collective_id),
    )(local_slab)


def _comm_call(body, peers_of, out_shape, n_sems, operand, name, collective_id):
    sems = [pltpu.SemaphoreType.DMA((n_sems,)), pltpu.SemaphoreType.DMA((n_sems,))]

    def with_handshake(in_ref, out_ref, send, recv):
        x, y, c, _ = _position()
        peers = peers_of(x, y, c)
        barrier = pltpu.get_barrier_semaphore()
        for peer in peers:
            pl.semaphore_signal(barrier, inc=1, device_id=peer, device_id_type=MESH)
        pl.semaphore_wait(barrier, len(peers))
        body(in_ref, out_ref, send, recv)

    return pl.kernel(with_handshake, out_type=out_shape, mesh=plsc.ScalarSubcoreMesh(axis_name="sequencer", num_cores=1),
                     name=name, scratch_types=sems, compiler_params=pltpu.CompilerParams(collective_id=collective_id))(operand)


def _rs_swap_halves(partial, name, collective_id):
    half = partial.shape[1] // 2

    def body(p_ref, r_ref, send, recv):
        x, y, c, _ = _position()
        theirs = pl.ds(pl.multiple_of((1 - c) * half, 16), half)
        cp = pltpu.make_async_remote_copy(src_ref=p_ref.at[:, theirs, :], dst_ref=r_ref, send_sem=send.at[0],
                                          recv_sem=recv.at[0], device_id=(x, y, 1 - c), device_id_type=MESH)
        cp.start()
        cp.wait()

    return _comm_call(body, lambda x, y, c: [(x, y, 1 - c)], jax.ShapeDtypeStruct((N_CHIPS, half, D_MODEL), BF16), 1,
                      partial, name, collective_id)


def _gather_chip_sums(s_ref, sib_ref, sum_ref, o_ref, send, recv):
    x, y, c, chips = _position()
    me, sibling, here = 2 * x + y, (x, y, 1 - c), (x, y, c)
    half = s_ref.shape[0] // 2

    def rows(core):
        return pl.ds(pl.multiple_of(core * half, 8), half)

    def copy(k, src, dst, to):
        return pltpu.make_async_remote_copy(src_ref=src, dst_ref=dst, send_sem=send.at[k], recv_sem=recv.at[k], device_id=to,
                                            device_id_type=MESH)

    swap = copy(0, s_ref, sib_ref, sibling)
    keep = pltpu.make_async_copy(sum_ref, o_ref.at[me], send.at[7])
    sends = [copy(1 + k, sum_ref.at[rows(c), :], o_ref.at[me, rows(c), :], (*chip, c)) for k, chip in enumerate(chips)]

    def landed(chip, core):
        return o_ref.at[2 * chip[0] + chip[1], rows(core), :]

    def add_and_send():
        swap.wait_recv()
        sum_ref[...] = s_ref[...] + sib_ref[...]
        keep.start()
        for cp in sends:
            cp.start()

    def finish():
        passed = []
        for k, chip in enumerate(chips):
            copy(1 + k, landed(chip, c), landed(chip, c), here).wait_recv()
            fwd = copy(4 + k, landed(chip, c), landed(chip, c), sibling)
            fwd.start()
            passed.append(fwd)
        for k, chip in enumerate(chips):
            copy(4 + k, landed(chip, 1 - c), landed(chip, 1 - c), here).wait_recv()
        for cp in [swap] + sends + passed:
            cp.wait_send()
        keep.wait()

    return swap.start, add_and_send, finish


def _rs_add_halves(partial, other, core, name, after, small=None):
    half = other.shape[1]
    t = half // 2
    steps = half // t

    n_steps, ring = N_CHIPS * steps, 3

    def body(core_ref, a_ref, b_ref, after_ref, *rest):
        del after_ref
        if small is None:
            o_ref, a_buf, b_buf, a_sems, b_sems = rest
        else:
            small_ref, o_ref, t_ref, a_buf, b_buf, a_sems, b_sems, sib_ref, sum_ref, t_send, t_recv = rest
            swap, add_and_send, finish_tables = _gather_chip_sums(small_ref, sib_ref, sum_ref, t_ref, t_send, t_recv)
        step = pl.program_id(0) * steps + pl.program_id(1)

        def loads(s):
            j, i, slot = s // steps, s % steps, s % ring
            mine = a_ref.at[j, pl.ds(pl.multiple_of((core_ref[0] * steps + i) * t, 16), t), :]
            theirs = b_ref.at[j, pl.ds(pl.multiple_of(i * t, 16), t), :]
            return (pltpu.make_async_copy(mine, a_buf.at[slot], a_sems.at[slot]),
                    pltpu.make_async_copy(theirs, b_buf.at[slot], b_sems.at[slot]))

        @pl.when(step == 0)
        def _():
            for s in range(ring - 1):
                for cp in loads(s):
                    cp.start()
            if small is not None:
                swap()

        @pl.when(step + ring - 1 < n_steps)
        def _():
            for cp in loads(step + ring - 1):
                cp.start()

        if small is not None:
            pl.when(step == 1)(add_and_send)
        for cp in loads(step):
            cp.wait()
        slot = step % ring
        o_ref[0] = (a_buf[slot].astype(F32) + b_buf[slot].astype(F32)).astype(BF16)
        if small is not None:
            pl.when(step == n_steps - 1)(finish_tables)

    t_in, t_out, t_scratch = [], [], []
    if small is not None:
        t_in, t_out = [VMEM_WHOLE], [jax.ShapeDtypeStruct((N_CHIPS, *small.shape), F32)]
        t_scratch = [pltpu.VMEM(small.shape, F32)] * 2 + [pltpu.SemaphoreType.DMA((8,))] * 2
    assert t % 16 == 0 and n_steps >= ring
    res = pl.pallas_call(
        body, name=name,
        grid_spec=pltpu.PrefetchScalarGridSpec(
            num_scalar_prefetch=1, grid=(N_CHIPS, steps),
            in_specs=[ANY, ANY, ANY] + t_in,
            out_specs=[pl.BlockSpec((1, t, D_MODEL), lambda j, i, core_ref: (j, i, 0))] + [ANY] * len(t_out),
            scratch_shapes=[pltpu.VMEM((ring, t, D_MODEL), BF16)] * 2 + [pltpu.SemaphoreType.DMA((ring,))] * 2 + t_scratch),
        out_shape=[jax.ShapeDtypeStruct((N_CHIPS, half, D_MODEL), BF16)] + t_out,
        compiler_params=_params(n_axes=2),
    )(core, partial, other, after, *([] if small is None else [small]))
    return res[0] if small is None else res


def _rs_exchange_chips(pre, name, collective_id):
    def body(s_ref, r_ref, send, recv):
        x, y, c, chips = _position()

        def copy(k, chunk, to):
            return pltpu.make_async_remote_copy(src_ref=s_ref.at[chunk], dst_ref=r_ref.at[k], send_sem=send.at[k],
                                                recv_sem=recv.at[k], device_id=to, device_id_type=MESH)

        sends = [copy(k, 2 * chip[0] + chip[1], (*chip, c)) for k, chip in enumerate(chips)]
        for cp in sends:
            cp.start()
        for cp in sends:
            cp.wait()

    return _comm_call(body, lambda x, y, c: [(1 - x, y, c), (x, 1 - y, c), (1 - x, 1 - y, c)],
                      jax.ShapeDtypeStruct((3, pre.shape[1], D_MODEL), BF16), 3, pre, name, collective_id)


def _gather_small(s_ref, t_ref, send, recv):
    x, y, c, chips = _position()
    sibling = (x, y, 1 - c)

    def slot(px, py, pc):
        return t_ref.at[4 * px + 2 * py + pc]

    def copy(k, block, to, src=None):
        return pltpu.make_async_remote_copy(src_ref=slot(*block) if src is None else src, dst_ref=slot(*block),
                                            send_sem=send.at[k], recv_sem=recv.at[k], device_id=to, device_id_type=MESH)

    own = pltpu.make_async_copy(s_ref, slot(x, y, c), send.at[7])
    first = [copy(0, (x, y, c), sibling, src=s_ref)]
    first += [copy(1 + k, (x, y, c), (*chip, c), src=s_ref) for k, chip in enumerate(chips)]

    def start():
        own.start()
        for cp in first:
            cp.start()

    def finish():
        passed = []
        for k, chip in enumerate(chips):
            copy(1 + k, (*chip, c), (x, y, c)).wait_recv()
            fwd = copy(4 + k, (*chip, c), sibling)
            fwd.start()
            passed.append(fwd)
        copy(0, sibling, (x, y, c)).wait_recv()
        for k, chip in enumerate(chips):
            copy(4 + k, (*chip, 1 - c), (x, y, c)).wait_recv()
        for cp in first + passed:
            cp.wait_send()
        own.wait()

    return start, finish


def _rs_sum_chips(pre, received, place, name, after, small=None):
    half = pre.shape[1]
    steps = 4 if half > 512 else 2
    t = half // steps
    assert t % 16 == 0 and t * steps == half

    def body(place_ref, own_ref, r_ref, after_ref, *rest):
        del after_ref
        if small is None:
            o_ref, stage, kept_sems, send, recv, in_buf, in_sems = rest
        else:
            small_ref, o_ref, t_ref, stage, kept_sems, send, recv, in_buf, in_sems, t_send, t_recv = rest
            start_tables, finish_tables = _gather_small(small_ref, t_ref, t_send, t_recv)
        i = pl.program_id(0)
        x, y, c, _ = _position()

        def loads(step):
            tile = pl.ds(pl.multiple_of(step * t, 16), t)
            sources = [own_ref.at[place_ref[0], tile, :]] + [r_ref.at[k, tile, :] for k in range(3)]
            return [pltpu.make_async_copy(src, in_buf.at[k, step], in_sems.at[k, step]) for k, src in enumerate(sources)]

        @pl.when(i == 0)
        def _():
            for step in range(steps):
                for cp in loads(step):
                    cp.start()
            if small is not None:
                start_tables()

        for cp in loads(i):
            cp.wait()

        def rows(core, step):
            return o_ref.at[pl.ds(pl.multiple_of((core * steps + step) * t, 8), t), :]

        def kept(step):
            return pltpu.make_async_copy(stage.at[step], rows(c, step), kept_sems.at[step])

        def sent(core, step):
            return pltpu.make_async_remote_copy(src_ref=stage.at[step], dst_ref=rows(core, step), send_sem=send.at[step],
                                                recv_sem=recv.at[step], device_id=(x, y, 1 - core), device_id_type=MESH)

        acc = in_buf[0, i].astype(F32)
        for k in range(3):
            acc = acc + in_buf[1 + k, i].astype(F32)
        stage[i] = acc
        kept(i).start()
        sent(c, i).start()

        @pl.when(i == steps - 1)
        def _():
            if small is not None:
                finish_tables()
            for step in range(steps):
                kept(step).wait()
                sent(c, step).wait_send()
                sent(1 - c, step).wait_recv()

    t_in, t_out, t_scratch = [], [], []
    if small is not None:
        t_in, t_out = [VMEM_WHOLE], [jax.ShapeDtypeStruct((N_DEV, *small.shape), F32)]
        t_scratch = [pltpu.SemaphoreType.DMA((8,))] * 2
    res = pl.pallas_call(
        body, name=name,
        grid_spec=pltpu.PrefetchScalarGridSpec(
            num_scalar_prefetch=1, grid=(steps,),
            in_specs=[ANY, ANY, ANY] + t_in,
            out_specs=[ANY] * (1 + len(t_out)),
            scratch_shapes=[pltpu.VMEM((steps, t, D_MODEL), F32)] + [pltpu.SemaphoreType.DMA((steps,))] * 3
            + [pltpu.VMEM((4, steps, t, D_MODEL), BF16), pltpu.SemaphoreType.DMA((4, steps))] + t_scratch),
        out_shape=[jax.ShapeDtypeStruct((2 * half, D_MODEL), F32)] + t_out, compiler_params=_params(),
    )(place, pre, received, after, *([] if small is None else [small]))
    return res[0] if small is None else res


def _adam_update(w, g, m, v):
    m_new = ADAM_B1 * m + (1.0 - ADAM_B1) * g
    v_new = ADAM_B2 * v + (1.0 - ADAM_B2) * (g * g)
    m_hat = m_new / (1.0 - ADAM_B1 ** ADAM_STEP)
    v_hat = v_new / (1.0 - ADAM_B2 ** ADAM_STEP)
    return -ADAM_LR * (m_hat / (jnp.sqrt(v_hat) + ADAM_EPS) + ADAM_WD * w), m_new, v_new


def _adamw(w, g_rows, row_off, m, v, name):
    rows, cols = w.shape
    t = rows if rows <= 320 else (rows // 2 if rows % 256 else 256)

    def body(w_ref, g_ref, m_ref, v_ref, go_ref, d_ref, nm_ref, nv_ref):
        g = g_ref[...]
        go_ref[...] = g
        d_ref[...], nm_ref[...], nv_ref[...] = _adam_update(w_ref[...], g, m_ref[...], v_ref[...])

    blk = pl.BlockSpec((t, cols), lambda i: (i, 0))
    assert row_off % 8 == 0 and t % 8 == 0
    g_blk = pl.BlockSpec((pl.Element(t), pl.Element(cols)), lambda i: (pl.multiple_of(row_off + i * t, 8), 0))
    shape = jax.ShapeDtypeStruct((rows, cols), F32)
    return pl.pallas_call(
        body, name=name, grid=(rows // t,), in_specs=[blk, g_blk, blk, blk], out_specs=[blk] * 4, out_shape=[shape] * 4,
        compiler_params=_params(),
    )(w, g_rows, m, v)


SMALL_PARAMS = [("g_attn", (1, D_MODEL), 8), ("g_q", (1, HEAD_DIM), None), ("g_k", (1, HEAD_DIM), None),
                ("sinks", (1, N_Q_HEADS), None), ("rel_bias", (N_Q_HEADS, N_BUCKETS), None), ("w_pool", (512, 128), None),
                ("pool_scale", (1, POOL_WIDTH), 4), ("g_ffn", (1, D_MODEL), 8), ("g_ple", (1, D_MODEL), 8)]


def _adamw_small(tables, pool_tables, wmv):
    n_par = len(SMALL_PARAMS)

    def body(*refs):
        t_ref, p_ref = refs[:2]
        ins = refs[2:2 + 3 * n_par]
        loss_ref = refs[2 + 3 * n_par]
        outs = refs[3 + 3 * n_par:-1]
        tot_ref = refs[-1]

        def in_order(ref):
            total = ref[0]
            for d in range(1, ref.shape[0]):
                total = total + ref[d]
            return total

        tot_ref[...] = in_order(t_ref)
        loss_ref[...] = tot_ref[pl.ds(SMALL["loss"], 1), 0:1]
        for i, (name, shape, split) in enumerate(SMALL_PARAMS):
            g_ref, d_ref, nm_ref, nv_ref = outs[4 * i:4 * i + 4]
            row = SMALL.get(name)
            if name == "w_pool":
                g_ref[...] = in_order(p_ref)
            elif split:
                for k in range(split):
                    g_ref[:, 128 * k:128 * k + 128] = tot_ref[pl.ds(row + k, 1), :]
            else:
                g_ref[...] = tot_ref[pl.ds(row, shape[0]), 0:shape[1]]
            w_ref, m_ref, v_ref = ins[3 * i:3 * i + 3]
            d_ref[...], nm_ref[...], nv_ref[...] = _adam_update(w_ref[...], g_ref[...], m_ref[...], v_ref[...])

    shapes = [jax.ShapeDtypeStruct((1, 1), F32)]
    for _, shape, _ in SMALL_PARAMS:
        shapes += [jax.ShapeDtypeStruct(shape, F32)] * 4
    flat = [a for triple in wmv for a in triple]
    res = pl.pallas_call(
        body, name="adamw_small", in_specs=[VMEM_WHOLE] * (2 + 3 * n_par), out_specs=[VMEM_WHOLE] * len(shapes),
        out_shape=shapes, scratch_shapes=[pltpu.VMEM((SMALL_ROWS, 128), F32)],
    )(tables, pool_tables, *flat)
    return res[0], [res[1 + 4 * i:5 + 4 * i] for i in range(n_par)]


def _pack_ple_proj(shard):
    return shard.reshape(4, 64, 256).transpose(1, 0, 2).reshape(64, D_MODEL)


class _Reduction:
    def __init__(self, tag, place, ids=(None, None)):
        self.tag, self.place, self.ids = tag, place, ids

    def start(self, partial):
        self.partial = partial
        self.other = _rs_swap_halves(partial, "rs_swap_" + self.tag, self.ids[0])
        return partial

    def middle(self, after, small=None):
        res = _rs_add_halves(self.partial, self.other, self.place[1:], "rs_add_" + self.tag, after, small)
        self.pre, self.tables = (res, None) if small is None else res
        self.received = _rs_exchange_chips(self.pre, "rs_exchange_" + self.tag, self.ids[1])
        return self.pre

    def finish(self, after, small=None):
        return _rs_sum_chips(self.pre, self.received, self.place, "rs_sum_" + self.tag, after, small)


def _local_grads(x2, p2, tgt, wts, g_attn_norm, g_q, g_k, attn_sinks, rel_bias, w_pool, pool_scale, g_ffn_norm, g_ple_norm,
                 reduce_a):
    w_early, w_late = wts
    w_in = w_out = w_early
    bucket = jnp.asarray(_bucket_table())
    gq = jnp.tile(g_q, (1, 2))
    gk = jnp.tile(g_k, (1, 2))
    wpool = w_pool[0].astype(BF16)
    sinks = attn_sinks[0]
    bias_st = _bias_build(rel_bias.T, bucket)

    hn1 = _first_norm(x2, g_attn_norm)
    zqk, u, kn, vb, qst = _attn_in(hn1, gq, gk, w_in)
    ost = _attn_fwd(qst, kn, vb, bias_st, sinks)
    pooled, mix, h1, hn2 = _mix_out(u, ost, x2, w_out, wpool, pool_scale, g_ffn_norm)
    loss_v, dgate, dup, act, dh2, hn3, dgl, dw_plp, dh1, dg_ffn, dg_ple = _ffn_ple(hn2, h1, p2, tgt, w_late, g_ffn_norm,
                                                                                      g_ple_norm)

    late0, late_rows = GATHER_PARTS[1][0], SLAB_ROWS - GATHER_PARTS[1][0]
    partial_a = None
    for names, lefts, right in ((("gateT", "upT"), [dgate, dup], hn2), (("down",), [act], dh2), (("plg",), [hn3], dgl)):
        partial_a = _dw(lefts, right, "dw_" + names[0], partial_a, late_rows, [SLAB[name][0] - late0 for name in names])
    dw_plp = dw_plp.reshape(4, 64, N_CHIPS, 256).transpose(2, 1, 0, 3).reshape(N_CHIPS, 64, D_MODEL)
    partial_a = reduce_a.start(lax.dynamic_update_slice(partial_a, dw_plp, (0, SLAB["plp"][0] - late0, 0)))
    dost, du, dw_pool, dscale, partial_b = _mix_out_bwd(dh1, w_out, pooled, wpool, pool_scale, mix, partial_a)
    pre_a = reduce_a.middle(du, dw_pool.reshape(512, 128))
    dqst, dk, dv, dbias, dsink_rows = _attn_bwd(qst, kn, vb, dost, bias_st, sinks, pre_a)
    dx, dg_attn, dgq, dgk, partial_b = _attn_in_bwd(dqst, zqk, dk, dv, du, x2, dh1, hn1, partial_b, w_in, g_attn_norm, gq, gk)

    small = _small_pack(dg_attn, dg_ffn, dg_ple, dscale, dgq, dgk, dbias, dsink_rows, loss_v)
    return dx, partial_b, small


def kernel(x, p, w_in, w_out, g_attn_norm, g_q, g_k, attn_sinks, rel_bias, w_pool, pool_scale, g_ffn_norm, w_gate, w_up, w_down, g_ple_norm, w_ple_gate, w_ple_proj, loss_target, m_w_in, m_w_out, m_g_attn_norm, m_g_q, m_g_k, m_attn_sinks, m_rel_bias, m_w_pool, m_pool_scale, m_g_ffn_norm, m_w_gate, m_w_up, m_w_down, m_g_ple_norm, m_w_ple_gate, m_w_ple_proj, v_w_in, v_w_out, v_g_attn_norm, v_g_q, v_g_k, v_attn_sinks, v_rel_bias, v_w_pool, v_pool_scale, v_g_ffn_norm, v_w_gate, v_w_up, v_w_down, v_g_ple_norm, v_w_ple_gate, v_w_ple_proj):
    core = lax.axis_index("c").astype(jnp.int32).reshape(1)
    me = (2 * lax.axis_index("x") + lax.axis_index("y")).astype(jnp.int32).reshape(1)

    local_parts = [jnp.concatenate(pieces, axis=0).astype(BF16) for pieces in (
        [w_in[0].T, w_out[0]], [w_gate[0].T, w_up[0].T, w_down[0], w_ple_gate[0], _pack_ple_proj(w_ple_proj[0])])]
    wts = [(_ag_weights(local, 0, local.shape[0], name, collective_id), local, me)
           for local, name, collective_id in zip(local_parts, ("ag_early", "ag_late"), (1, 2))]

    place = jnp.concatenate([me, core])
    reduce_a = _Reduction("a", place, ids=(3, 4))
    dx, partial_b, small = _local_grads(x[0], p[0, 0], loss_target[0], wts, g_attn_norm, g_q, g_k, attn_sinks, rel_bias,
                                        w_pool, pool_scale, g_ffn_norm, g_ple_norm, reduce_a)
    reduce_b = _Reduction("b", place, ids=(6, 7))
    reduce_b.start(partial_b)
    grads_a, small_all = reduce_a.finish(partial_b, small)
    reduce_b.middle(grads_a)

    late0 = GATHER_PARTS[1][0]

    def rows(name):
        return grads_a, SLAB[name][0] - late0

    plp_rows = grads_a[SLAB["plp"][0] - late0:]
    big = {
        "w_gate": (w_gate, m_w_gate, v_w_gate, rows("gateT"), True),
        "w_up": (w_up, m_w_up, v_w_up, rows("upT"), True),
        "w_down": (w_down, m_w_down, v_w_down, rows("down"), False),
        "w_ple_gate": (w_ple_gate, m_w_ple_gate, v_w_ple_gate, rows("plg"), False),
        "w_ple_proj": (w_ple_proj, m_w_ple_proj, v_w_ple_proj,
                       (plp_rows.reshape(64, 4, 256).transpose(1, 0, 2).reshape(PLE_DIM, PLE_DIM), 0), False),
        "w_out": (w_out, m_w_out, v_w_out, None, False),
        "w_in": (w_in, m_w_in, v_w_in, None, True),
    }
    small_params = {
        "g_attn_norm": (g_attn_norm, m_g_attn_norm, v_g_attn_norm), "g_q": (g_q, m_g_q, v_g_q), "g_k": (g_k, m_g_k, v_g_k),
        "attn_sinks": (attn_sinks, m_attn_sinks, v_attn_sinks), "rel_bias": (rel_bias.T, m_rel_bias.T, v_rel_bias.T),
        "w_pool": tuple(a.reshape(512, 128) for a in (w_pool, m_w_pool, v_w_pool)),
        "pool_scale": (pool_scale, m_pool_scale, v_pool_scale), "g_ffn_norm": (g_ffn_norm, m_g_ffn_norm, v_g_ffn_norm),
        "g_ple_norm": (g_ple_norm, m_g_ple_norm, v_g_ple_norm),
    }

    grads, deltas, new_ms, new_vs = {}, {}, {}, {}
    out = grads_b = None
    for name, (w, m, v, g_src, transposed) in big.items():
        if g_src is None:
            if grads_b is None:
                grads_b = reduce_b.finish(out[-1])
            g_src = (grads_b, SLAB["out" if name == "w_out" else "inT"][0])
        view = (lambda a: a.T) if transposed else (lambda a: a)
        out = _adamw(view(w[0]), *g_src, view(m[0]), view(v[0]), "adamw_" + name)
        grads[name], deltas[name], new_ms[name], new_vs[name] = (view(a)[None] for a in out)

    loss, small_out = _adamw_small(small_all, reduce_a.tables, list(small_params.values()))
    for name, (g2, d, nm, nv) in zip(small_params, small_out):
        restore = {"w_pool": lambda a: a.reshape(w_pool.shape), "rel_bias": lambda a: a.T}.get(name, lambda a: a)
        grads[name], deltas[name], new_ms[name], new_vs[name] = (restore(a) for a in (g2, d, nm, nv))

    order = ["w_in", "w_out", "g_attn_norm", "g_q", "g_k", "attn_sinks", "rel_bias", "w_pool", "pool_scale", "g_ffn_norm",
             "w_gate", "w_up", "w_down", "g_ple_norm", "w_ple_gate", "w_ple_proj"]
    return (loss.reshape(()), dx[None], *[grads[n] for n in order], *[deltas[n] for n in order],
            *[new_ms[n] for n in order], *[new_vs[n] for n in order])
```
